```python
import jax, jax.numpy as jnp
from jax import lax
import numpy as np

D_MODEL = 1024
BATCH = 8
SEQ = 2048
DEPTH = 2

HEAD_DIM = 64
ROT_DIM = HEAD_DIM // 4
ROPE_THETA = 500000.0
DIL_GROUPS = ((128, 1), (512, 4), (2048, 16))
N_GROUPS = len(DIL_GROUPS)
HEADS_PER_GROUP = 8
GROUP_WIDTH = HEADS_PER_GROUP * HEAD_DIM
BLOCK = 128
N_MEM = 256
MEM_HEADS = 4
MEM_WIDTH = MEM_HEADS * HEAD_DIM
CONV_WIDTH = D_MODEL
CONV_K = 3
N_MIXERS = 2
N_ATTN_LAYERS = (DEPTH + 1) // 2
N_CONV_LAYERS = DEPTH // 2
BRANCH_A = GROUP_WIDTH + MEM_WIDTH
BRANCH_B = CONV_WIDTH + MEM_WIDTH
IN_A = 3 * N_GROUPS * GROUP_WIDTH + MEM_WIDTH + BRANCH_A
IN_B = 3 * CONV_WIDTH + MEM_WIDTH + BRANCH_B
EPS = 1e-6

kernel_name = "hybrid_dilated_attn_shortconv_memory"


def rms_norm(x, g):
    xf = x.astype(jnp.float32)
    y = xf * lax.rsqrt(jnp.mean(xf * xf, axis=-1, keepdims=True) + EPS)
    return (y * g.astype(jnp.float32)).astype(x.dtype)


def partial_rope(t, positions):
    half = ROT_DIM // 2
    inv_freq = ROPE_THETA ** (-jnp.arange(half, dtype=jnp.float32) * (2.0 / ROT_DIM))
    ang = positions.astype(jnp.float32)[:, :, None] * inv_freq
    cos = jnp.cos(ang)[:, :, None, :]
    sin = jnp.sin(ang)[:, :, None, :]
    tr = t[..., :ROT_DIM].astype(jnp.float32)
    t1, t2 = tr[..., :half], tr[..., half:]
    rot = jnp.concatenate([t1 * cos - t2 * sin, t2 * cos + t1 * sin], axis=-1)
    return jnp.concatenate([rot.astype(t.dtype), t[..., ROT_DIM:]], axis=-1)


def dilated_window_attention(q, k, v, window, dilation):
    b, s, h, dh = q.shape
    n_stream = s // dilation
    span = window // dilation
    nb = -(-n_stream // BLOCK)
    lp = nb * BLOCK

    def to_streams(t):
        t = t.reshape(b, n_stream, dilation, h, dh).transpose(0, 2, 1, 3, 4)
        return jnp.pad(t, ((0, 0), (0, 0), (0, lp - n_stream), (0, 0), (0, 0)))

    def banded(t):
        t = jnp.pad(t, ((0, 0), (0, 0), (BLOCK, 0), (0, 0), (0, 0)))
        t = t.reshape(b, dilation, nb + 1, BLOCK, h, dh)
        return jnp.concatenate([t[:, :, :-1], t[:, :, 1:]], axis=3)

    qb = to_streams(q).reshape(b, dilation, nb, BLOCK, h, dh)
    kb = banded(to_streams(k))
    vb = banded(to_streams(v))

    scores = jnp.einsum('brnqhd,brnkhd->brnhqk', qb, kb).astype(jnp.float32) * (dh ** -0.5)
    qi = jnp.arange(BLOCK)[:, None]
    kj = jnp.arange(2 * BLOCK)[None, :]
    blk = jnp.arange(nb)[:, None, None]
    dist = qi + BLOCK - kj
    kpos = blk * BLOCK + kj - BLOCK
    valid = (dist >= 0) & (dist <= span) & (kpos >= 0)
    scores = jnp.where(valid[None, None, :, None], scores, -jnp.inf)
    lse = jax.nn.logsumexp(scores, axis=-1)
    p = jnp.exp(scores - lse[..., None])
    out = jnp.einsum('brnhqk,brnkhd->brnqhd', p.astype(vb.dtype), vb).astype(jnp.float32)

    out = out.reshape(b, dilation, lp, h, dh)[:, :, :n_stream]
    out = out.transpose(0, 2, 1, 3, 4).reshape(b, s, h, dh)
    lse = lse.transpose(0, 1, 2, 4, 3).reshape(b, dilation, lp, h)[:, :, :n_stream]
    lse = lse.transpose(0, 2, 1, 3).reshape(b, s, h)
    return out, lse


def memory_cross_attention(qm, mem_n, w_mem_kv):
    b, s, _ = qm.shape
    kv = mem_n @ w_mem_kv
    km, vm = jnp.split(kv, 2, axis=-1)
    km = km.reshape(b, N_MEM, MEM_HEADS, HEAD_DIM)
    vm = vm.reshape(b, N_MEM, MEM_HEADS, HEAD_DIM)
    qh = qm.reshape(b, s, MEM_HEADS, HEAD_DIM)
    scores = jnp.einsum('bshd,bmhd->bhsm', qh, km).astype(jnp.float32) * (HEAD_DIM ** -0.5)
    p = jax.nn.softmax(scores, axis=-1)
    out = jnp.einsum('bhsm,bmhd->bshd', p.astype(vm.dtype), vm)
    return out.reshape(b, s, MEM_WIDTH)


def dilated_attention_layer(hn, positions, mem_n, w_in, w_mem_kv, w_out):
    b, s, _ = hn.shape
    gw = N_GROUPS * GROUP_WIDTH
    proj = hn @ w_in
    q, k, v, qm, z = jnp.split(proj, [gw, 2 * gw, 3 * gw, 3 * gw + MEM_WIDTH], axis=-1)
    n_heads = N_GROUPS * HEADS_PER_GROUP
    q = partial_rope(q.reshape(b, s, n_heads, HEAD_DIM), positions)
    k = partial_rope(k.reshape(b, s, n_heads, HEAD_DIM), positions)
    v = v.reshape(b, s, n_heads, HEAD_DIM)
    outs, lses = [], []
    for g, (window, dilation) in enumerate(DIL_GROUPS):
        sl = slice(g * HEADS_PER_GROUP, (g + 1) * HEADS_PER_GROUP)
        o, l = dilated_window_attention(q[:, :, sl], k[:, :, sl], v[:, :, sl], window, dilation)
        outs.append(o)
        lses.append(l)
    wts = jax.nn.softmax(jnp.stack(lses, axis=0), axis=0)
    mix = jnp.sum(wts[..., None] * jnp.stack(outs, axis=0), axis=0)
    mix = mix.reshape(b, s, GROUP_WIDTH).astype(hn.dtype)
    mem_out = memory_cross_attention(qm, mem_n, w_mem_kv)
    y = jnp.concatenate([mix, mem_out], axis=-1) * jax.nn.silu(z)
    return y @ w_out


def short_conv_layer(hn, mem_n, w_in, conv_w, w_mem_kv, w_out):
    c = CONV_WIDTH
    proj = hn @ w_in
    bg, cg, u, qm, z = jnp.split(proj, [c, 2 * c, 3 * c, 3 * c + MEM_WIDTH], axis=-1)
    conv = lax.conv_general_dilated(
        cg * u, conv_w[:, None, :].astype(u.dtype),
        window_strides=(1,), padding=((CONV_K - 1, 0),),
        dimension_numbers=('NWC', 'WIO', 'NWC'), feature_group_count=c)
    mix = bg * conv
    mem_out = memory_cross_attention(qm, mem_n, w_mem_kv)
    y = jnp.concatenate([mix, mem_out], axis=-1) * jax.nn.silu(z)
    return y @ w_out


def _fwd_setup_inputs(seed: int = 0) -> dict:
    key = jax.random.key(seed)
    ks = jax.random.split(key, 14)
    f32 = jnp.float32

    def nrm(k, shape, fan_in):
        return jax.random.normal(k, shape, f32) * (fan_in ** -0.5)

    x = jax.random.normal(ks[0], (BATCH, SEQ, D_MODEL), f32)
    mem = jax.random.normal(ks[1], (BATCH, N_MEM, D_MODEL), f32)
    offset = jax.random.randint(ks[2], (BATCH, 1), 0, 1024, dtype=jnp.int32)
    positions = offset + jnp.arange(SEQ, dtype=jnp.int32)[None, :]
    norm_g = 1.0 + 0.05 * jax.random.normal(ks[3], (DEPTH, D_MODEL), f32)
    mem_norm_g = 1.0 + 0.05 * jax.random.normal(ks[4], (DEPTH, D_MODEL), f32)
    w_mem_kv = nrm(ks[5], (DEPTH, D_MODEL, 2 * MEM_WIDTH), D_MODEL)
    attn_w_in = nrm(ks[6], (N_ATTN_LAYERS, D_MODEL, IN_A), D_MODEL)
    attn_w_out = nrm(ks[7], (N_ATTN_LAYERS, BRANCH_A, D_MODEL), BRANCH_A)
    conv_w_in = nrm(ks[8], (N_CONV_LAYERS, D_MODEL, IN_B), D_MODEL)
    conv_w = nrm(ks[9], (N_CONV_LAYERS, CONV_K, CONV_WIDTH), CONV_K)
    conv_w_out = nrm(ks[10], (N_CONV_LAYERS, BRANCH_B, D_MODEL), BRANCH_B)
    final_g = 1.0 + 0.05 * jax.random.normal(ks[11], (D_MODEL,), f32)
    return {"x": x, "mem": mem, "positions": positions, "norm_g": norm_g,
            "mem_norm_g": mem_norm_g, "w_mem_kv": w_mem_kv,
            "attn_w_in": attn_w_in, "attn_w_out": attn_w_out,
            "conv_w_in": conv_w_in, "conv_w": conv_w, "conv_w_out": conv_w_out,
            "final_g": final_g}


def _fwd_reference(x, mem, positions, norm_g, mem_norm_g, w_mem_kv, attn_w_in, attn_w_out,
              conv_w_in, conv_w, conv_w_out, final_g):
    h = x
    for i in range(DEPTH):
        j = i // N_MIXERS
        hn = rms_norm(h, norm_g[i])
        mem_n = rms_norm(mem, mem_norm_g[i])
        if i % N_MIXERS == 0:
            delta = dilated_attention_layer(hn, positions, mem_n, attn_w_in[j],
                                            w_mem_kv[i], attn_w_out[j])
        else:
            delta = short_conv_layer(hn, mem_n, conv_w_in[j], conv_w[j],
                                     w_mem_kv[i], conv_w_out[j])
        h = h + delta
    return rms_norm(h, final_g)


import jax as _jax
import jax.numpy as _jnp

TWIN_FORMAT = 'train_step'
FWD_PARAMS = ['x', 'mem', 'positions', 'norm_g', 'mem_norm_g', 'w_mem_kv', 'attn_w_in', 'attn_w_out', 'conv_w_in', 'conv_w', 'conv_w_out', 'final_g']
TWIN_WEIGHTS = ['norm_g', 'mem_norm_g', 'w_mem_kv', 'attn_w_in', 'attn_w_out', 'conv_w_in', 'conv_w', 'conv_w_out', 'final_g']
TWIN_DIFF_INPUT = 'x'
TWIN_INPUTS = ['x', 'mem', 'positions', 'norm_g', 'mem_norm_g', 'w_mem_kv', 'attn_w_in', 'attn_w_out', 'conv_w_in', 'conv_w', 'conv_w_out', 'final_g', 'loss_target', 'm_norm_g', 'm_mem_norm_g', 'm_w_mem_kv', 'm_attn_w_in', 'm_attn_w_out', 'm_conv_w_in', 'm_conv_w', 'm_conv_w_out', 'm_final_g', 'v_norm_g', 'v_mem_norm_g', 'v_w_mem_kv', 'v_attn_w_in', 'v_attn_w_out', 'v_conv_w_in', 'v_conv_w', 'v_conv_w_out', 'v_final_g']
TWIN_OUTPUTS = ['loss', 'grad_x', 'grad_norm_g', 'grad_mem_norm_g', 'grad_w_mem_kv', 'grad_attn_w_in', 'grad_attn_w_out', 'grad_conv_w_in', 'grad_conv_w', 'grad_conv_w_out', 'grad_final_g', 'delta_norm_g', 'delta_mem_norm_g', 'delta_w_mem_kv', 'delta_attn_w_in', 'delta_attn_w_out', 'delta_conv_w_in', 'delta_conv_w', 'delta_conv_w_out', 'delta_final_g', 'new_m_norm_g', 'new_m_mem_norm_g', 'new_m_w_mem_kv', 'new_m_attn_w_in', 'new_m_attn_w_out', 'new_m_conv_w_in', 'new_m_conv_w', 'new_m_conv_w_out', 'new_m_final_g', 'new_v_norm_g', 'new_v_mem_norm_g', 'new_v_w_mem_kv', 'new_v_attn_w_in', 'new_v_attn_w_out', 'new_v_conv_w_in', 'new_v_conv_w', 'new_v_conv_w_out', 'new_v_final_g']
TWIN_LEAF_KINDS = {'loss': 'loss', 'grad_x': 'grad_x', 'grad_norm_g': 'grad_w', 'grad_mem_norm_g': 'grad_w', 'grad_w_mem_kv': 'grad_w', 'grad_attn_w_in': 'grad_w', 'grad_attn_w_out': 'grad_w', 'grad_conv_w_in': 'grad_w', 'grad_conv_w': 'grad_w', 'grad_conv_w_out': 'grad_w', 'grad_final_g': 'grad_w', 'delta_norm_g': 'delta_w', 'delta_mem_norm_g': 'delta_w', 'delta_w_mem_kv': 'delta_w', 'delta_attn_w_in': 'delta_w', 'delta_attn_w_out': 'delta_w', 'delta_conv_w_in': 'delta_w', 'delta_conv_w': 'delta_w', 'delta_conv_w_out': 'delta_w', 'delta_final_g': 'delta_w', 'new_m_norm_g': 'new_m', 'new_m_mem_norm_g': 'new_m', 'new_m_w_mem_kv': 'new_m', 'new_m_attn_w_in': 'new_m', 'new_m_attn_w_out': 'new_m', 'new_m_conv_w_in': 'new_m', 'new_m_conv_w': 'new_m', 'new_m_conv_w_out': 'new_m', 'new_m_final_g': 'new_m', 'new_v_norm_g': 'new_v', 'new_v_mem_norm_g': 'new_v', 'new_v_w_mem_kv': 'new_v', 'new_v_attn_w_in': 'new_v', 'new_v_attn_w_out': 'new_v', 'new_v_conv_w_in': 'new_v', 'new_v_conv_w': 'new_v', 'new_v_conv_w_out': 'new_v', 'new_v_final_g': 'new_v'}


def _forward(args):
    return _fwd_reference(*[args[k] for k in FWD_PARAMS])


def _output_shape():
    out = _jax.eval_shape(lambda: _forward(_fwd_setup_inputs(0)))
    return out.shape, out.dtype

N_MICROBATCH = 1
ADAM_LR = 0.001
ADAM_B1 = 0.9
ADAM_B2 = 0.999
ADAM_EPS = 1e-08
ADAM_WD = 0.01
ADAM_STEP = 10
PER_EXAMPLE_BATCH_AXIS = {'x': 0, 'mem': 0, 'positions': 0, 'loss_target': 0}
SHARED_INPUTS = []
_WEIGHT_DTYPES = {'norm_g': _jnp.float32, 'mem_norm_g': _jnp.float32, 'w_mem_kv': _jnp.float32, 'attn_w_in': _jnp.float32, 'attn_w_out': _jnp.float32, 'conv_w_in': _jnp.float32, 'conv_w': _jnp.float32, 'conv_w_out': _jnp.float32, 'final_g': _jnp.float32}
MOMENT_SCALE = {'norm_g': 8.595805e-02, 'mem_norm_g': 6.842644e-03, 'w_mem_kv': 9.420699e-03, 'attn_w_in': 1.255412e-02, 'attn_w_out': 1.597460e-02, 'conv_w_in': 5.564802e-02, 'conv_w': 5.890486e-02, 'conv_w_out': 5.915467e-02, 'final_g': 1.603775e+01}


def _to_microbatches(a, axis):
    t = _jnp.moveaxis(a, axis, 0)
    t = t.reshape((N_MICROBATCH, t.shape[0] // N_MICROBATCH) + t.shape[1:])
    return _jnp.moveaxis(t, 1, axis + 1)


def setup_inputs(seed: int = 0) -> dict:
    inp = _fwd_setup_inputs(seed)
    key = _jax.random.fold_in(_jax.random.key(seed), 7919)
    shape, _ = _output_shape()
    out = dict(inp)
    out["loss_target"] = _jax.random.normal(_jax.random.fold_in(key, 0), shape, _jnp.float32)
    for i, name in enumerate(TWIN_WEIGHTS):
        w = inp[name].astype(_jnp.float32)
        if MOMENT_SCALE is None:
            s = _jnp.sqrt(_jnp.mean(_jnp.square(w)) + 1e-30)
        else:
            s = MOMENT_SCALE[name]
        km, kv = _jax.random.split(_jax.random.fold_in(key, i + 1))
        out[name] = w
        out["m_" + name] = s * _jax.random.normal(km, w.shape, _jnp.float32)
        out["v_" + name] = (s * s) * _jax.random.uniform(kv, w.shape, _jnp.float32, 0.5, 1.5)
    if N_MICROBATCH > 1:
        for name, axis in PER_EXAMPLE_BATCH_AXIS.items():
            out[name] = _to_microbatches(out[name], axis)
    return {'x': out['x'], 'mem': out['mem'], 'positions': out['positions'], 'norm_g': out['norm_g'], 'mem_norm_g': out['mem_norm_g'], 'w_mem_kv': out['w_mem_kv'], 'attn_w_in': out['attn_w_in'], 'attn_w_out': out['attn_w_out'], 'conv_w_in': out['conv_w_in'], 'conv_w': out['conv_w'], 'conv_w_out': out['conv_w_out'], 'final_g': out['final_g'], 'loss_target': out['loss_target'], 'm_norm_g': out['m_norm_g'], 'm_mem_norm_g': out['m_mem_norm_g'], 'm_w_mem_kv': out['m_w_mem_kv'], 'm_attn_w_in': out['m_attn_w_in'], 'm_attn_w_out': out['m_attn_w_out'], 'm_conv_w_in': out['m_conv_w_in'], 'm_conv_w': out['m_conv_w'], 'm_conv_w_out': out['m_conv_w_out'], 'm_final_g': out['m_final_g'], 'v_norm_g': out['v_norm_g'], 'v_mem_norm_g': out['v_mem_norm_g'], 'v_w_mem_kv': out['v_w_mem_kv'], 'v_attn_w_in': out['v_attn_w_in'], 'v_attn_w_out': out['v_attn_w_out'], 'v_conv_w_in': out['v_conv_w_in'], 'v_conv_w': out['v_conv_w'], 'v_conv_w_out': out['v_conv_w_out'], 'v_final_g': out['v_final_g']}


def _loss(weights, diff, rest, loss_target):
    with _jax.named_scope("forward"):
        args = {**rest, TWIN_DIFF_INPUT: diff, **{k: w.astype(_WEIGHT_DTYPES[k]) for k, w in weights.items()}}
        y = _forward(args)
    with _jax.named_scope("loss_head"):
        err = _jnp.square(y.astype(_jnp.float32) - loss_target)
        return 0.5 * _jnp.sum(_jnp.mean(err, axis=-1)) if err.ndim else 0.5 * err


def _adamw(w, g, m, v):
    m = ADAM_B1 * m + (1.0 - ADAM_B1) * g
    v = ADAM_B2 * v + (1.0 - ADAM_B2) * _jnp.square(g)
    m_hat = m / (1.0 - ADAM_B1 ** ADAM_STEP)
    v_hat = v / (1.0 - ADAM_B2 ** ADAM_STEP)
    delta = -ADAM_LR * (m_hat / (_jnp.sqrt(v_hat) + ADAM_EPS) + ADAM_WD * w)
    return delta, m, v


def reference(x, mem, positions, norm_g, mem_norm_g, w_mem_kv, attn_w_in, attn_w_out, conv_w_in, conv_w, conv_w_out, final_g, loss_target, m_norm_g, m_mem_norm_g, m_w_mem_kv, m_attn_w_in, m_attn_w_out, m_conv_w_in, m_conv_w, m_conv_w_out, m_final_g, v_norm_g, v_mem_norm_g, v_w_mem_kv, v_attn_w_in, v_attn_w_out, v_conv_w_in, v_conv_w, v_conv_w_out, v_final_g):
    given = dict(x=x, mem=mem, positions=positions, norm_g=norm_g, mem_norm_g=mem_norm_g, w_mem_kv=w_mem_kv, attn_w_in=attn_w_in, attn_w_out=attn_w_out, conv_w_in=conv_w_in, conv_w=conv_w, conv_w_out=conv_w_out, final_g=final_g, loss_target=loss_target, m_norm_g=m_norm_g, m_mem_norm_g=m_mem_norm_g, m_w_mem_kv=m_w_mem_kv, m_attn_w_in=m_attn_w_in, m_attn_w_out=m_attn_w_out, m_conv_w_in=m_conv_w_in, m_conv_w=m_conv_w, m_conv_w_out=m_conv_w_out, m_final_g=m_final_g, v_norm_g=v_norm_g, v_mem_norm_g=v_mem_norm_g, v_w_mem_kv=v_w_mem_kv, v_attn_w_in=v_attn_w_in, v_attn_w_out=v_attn_w_out, v_conv_w_in=v_conv_w_in, v_conv_w=v_conv_w, v_conv_w_out=v_conv_w_out, v_final_g=v_final_g)
    weights = {n: given[n] for n in TWIN_WEIGHTS}
    shared = {n: given[n] for n in SHARED_INPUTS}
    per_example = {n: given[n] for n in ['x', 'mem', 'positions']}
    grad_fn = _jax.value_and_grad(_loss, argnums=(0, 1))

    def one_microbatch(ex, loss_target):
        ex = dict(ex)
        diff = ex.pop(TWIN_DIFF_INPUT)
        return grad_fn(weights, diff, {**shared, **ex}, loss_target)

    if N_MICROBATCH == 1:
        loss, (grad_w, grad_x) = one_microbatch(per_example, given["loss_target"])
    else:
        def body(carry, xs):
            loss_sum, grad_sum = carry
            l_k, (gw_k, gx_k) = one_microbatch(xs[0], xs[1])
            with _jax.named_scope("update"):
                return (loss_sum + l_k, _jax.tree.map(_jnp.add, grad_sum, gw_k)), gx_k

        init = (_jnp.zeros((), _jnp.float32), _jax.tree.map(_jnp.zeros_like, weights))
        (loss, grad_w), grad_x = _jax.lax.scan(body, init, (per_example, given["loss_target"]))
    with _jax.named_scope("update"):
        delta_w, new_m, new_v = {}, {}, {}
        for n in TWIN_WEIGHTS:
            delta_w[n], new_m[n], new_v[n] = _adamw(weights[n], grad_w[n], given["m_" + n], given["v_" + n])
    return (loss, grad_x, *[grad_w[n] for n in TWIN_WEIGHTS], *[delta_w[n] for n in TWIN_WEIGHTS],
            *[new_m[n] for n in TWIN_WEIGHTS], *[new_v[n] for n in TWIN_WEIGHTS])
```

```python
import functools

import jax
import jax.numpy as jnp
from jax import lax
from jax.experimental import pallas as pl
from jax.experimental.pallas import tpu as pltpu

F32 = jnp.float32
BF16 = jnp.bfloat16

N_DEV = 8
D_MODEL = 1024
HEAD_DIM = 64
ROT_DIM = HEAD_DIM // 4
ROPE_THETA = 500000.0
DILATIONS = (1, 4, 16)
HEADS_PER_GROUP = 8
GROUP_WIDTH = HEADS_PER_GROUP * HEAD_DIM
BLOCK = 128
N_MEM = 256
MEM_HEADS = 4
MEM_WIDTH = MEM_HEADS * HEAD_DIM
CONV_WIDTH = D_MODEL
EPS = 1e-6
SCALE = HEAD_DIM ** -0.5
NEG = -1e30

ADAM_LR = 0.001
ADAM_B1 = 0.9
ADAM_B2 = 0.999
ADAM_EPS = 1e-08
ADAM_WD = 0.01
ADAM_STEP = 10

ROW_TILE = 256
LANES = 128
MESH = pl.DeviceIdType.MESH
ANY = pl.BlockSpec(memory_space=pl.ANY)


def _dot(a, b):
    return lax.dot_general(a, b, (((1,), (0,)), ((), ())), preferred_element_type=F32)


def _dot_nt(a, b):
    return lax.dot_general(a, b, (((1,), (1,)), ((), ())), preferred_element_type=F32)


def _dot_tn(a, b):
    return lax.dot_general(a, b, (((0,), (0,)), ((), ())), preferred_element_type=F32)


def _params(n_grid, vmem_mb=48):
    return pltpu.CompilerParams(dimension_semantics=("arbitrary",) * n_grid, vmem_limit_bytes=vmem_mb << 20)


def _rows(width, tm=ROW_TILE):
    return pl.BlockSpec((tm, width), lambda i: (i, 0))


def _whole(shape):
    return pl.BlockSpec(shape, lambda *_: (0,) * len(shape))


def _sds(shape, dtype):
    return jax.ShapeDtypeStruct(shape, dtype)


def _silu_parts(z):
    sg = jax.nn.sigmoid(z)
    return z * sg, sg * (1.0 + z * (1.0 - sg))


def _rope_tables(pos):
    half = ROT_DIM // 2
    inv_freq = ROPE_THETA ** (-jnp.arange(half, dtype=F32) * (2.0 / ROT_DIM))
    ang = pos.astype(F32)[:, None] * inv_freq
    cos, sin = jnp.cos(ang), jnp.sin(ang)
    s = pos.shape[0]
    z8 = jnp.zeros((s, half), F32)
    rest = HEAD_DIM - ROT_DIM
    cosf = jnp.concatenate([cos, cos, jnp.ones((s, rest), F32)], axis=1)
    sa = jnp.concatenate([-sin, z8, jnp.zeros((s, rest), F32)], axis=1)
    sb = jnp.concatenate([z8, sin, jnp.zeros((s, rest), F32)], axis=1)
    return tuple(jnp.tile(t, (1, LANES // HEAD_DIM)) for t in (cosf, sa, sb))


def _rope_fwd(t, cv, sav, sbv):
    w = t.shape[1]
    return t * cv + pltpu.roll(t, w - ROT_DIM // 2, 1) * sav + pltpu.roll(t, ROT_DIM // 2, 1) * sbv


def _rope_bwd(g, cv, sav, sbv):
    w = g.shape[1]
    return g * cv + pltpu.roll(g * sav, ROT_DIM // 2, 1) + pltpu.roll(g * sbv, w - ROT_DIM // 2, 1)


def _inproj_attn(x, g, w, tabs):
    s, d = x.shape
    gw = GROUP_WIDTH
    reps = gw // LANES

    def body(x_ref, g_ref, w_ref, c_ref, sa_ref, sb_ref, hn_ref, *outs):
        q_refs, k_refs, v_refs, qm_ref, z_ref = outs[0:3], outs[3:6], outs[6:9], outs[9], outs[10]
        xb = x_ref[...]
        r = lax.rsqrt(jnp.mean(xb * xb, axis=-1, keepdims=True) + EPS)
        hn = ((xb * r) * g_ref[...]).astype(BF16)
        hn_ref[...] = hn
        cv = jnp.tile(c_ref[...], (1, reps))
        sav = jnp.tile(sa_ref[...], (1, reps))
        sbv = jnp.tile(sb_ref[...], (1, reps))
        for j in range(3):
            tq = _rope_fwd(_dot(hn, w_ref[:, j * gw:(j + 1) * gw]), cv, sav, sbv)
            q_refs[j][...] = (tq * SCALE).astype(BF16)
            tk = _rope_fwd(_dot(hn, w_ref[:, (3 + j) * gw:(4 + j) * gw]), cv, sav, sbv)
            k_refs[j][...] = tk.astype(BF16)
            v_refs[j][...] = _dot(hn, w_ref[:, (6 + j) * gw:(7 + j) * gw]).astype(BF16)
        qm_ref[...] = _dot(hn, w_ref[:, 9 * gw:9 * gw + MEM_WIDTH]).astype(BF16)
        z_ref[...] = _dot(hn, w_ref[:, 9 * gw + MEM_WIDTH:])

    nz = w.shape[1] - 9 * gw - MEM_WIDTH
    out_shape = [_sds((s, d), BF16)] + [_sds((s, gw), BF16)] * 9 + [_sds((s, MEM_WIDTH), BF16), _sds((s, nz), F32)]
    out_specs = [_rows(d)] + [_rows(gw)] * 9 + [_rows(MEM_WIDTH), _rows(nz)]
    res = pl.pallas_call(
        body, name="inproj_attn", grid=(s // ROW_TILE,), out_shape=out_shape,
        in_specs=[_rows(d), _whole((1, d)), _whole(w.shape), _rows(LANES), _rows(LANES), _rows(LANES)],
        out_specs=out_specs, compiler_params=_params(1, 56),
    )(x, g, w, *tabs)
    return res[0], res[1:4], res[4:7], res[7:10], res[10], res[11]


def _stream_maps(d, nb, nblk):
    def cur(n):
        return (n % nb, n // nb)

    def prev(n):
        return (jnp.maximum(n % nb - 1, 0), n // nb)

    return cur, prev


def _band_mask(first):
    qi = lax.broadcasted_iota(jnp.int32, (BLOCK, 2 * BLOCK), 0)
    kj = lax.broadcasted_iota(jnp.int32, (BLOCK, 2 * BLOCK), 1)
    lo = qi + jnp.where(first, 2 * BLOCK, 0)
    return jnp.logical_or(jnp.logical_and(kj < BLOCK, kj >= lo), jnp.logical_and(kj >= BLOCK, (kj - BLOCK) <= qi))


def _attn_fwd(q, k, v, d):
    s, w = q.shape
    ln = s // d
    nb = ln // BLOCK
    nblk = d * nb
    cur, prev = _stream_maps(d, nb, nblk)

    def body(q_ref, kp_ref, kc_ref, vp_ref, vc_ref, o_ref, lse_ref):
        n = pl.program_id(0)
        valid = _band_mask((n % nb) == 0)
        kk = jnp.concatenate([kp_ref[...], kc_ref[...]], axis=0)
        vv = jnp.concatenate([vp_ref[...], vc_ref[...]], axis=0)
        for h in range(HEADS_PER_GROUP):
            sl = slice(h * HEAD_DIM, (h + 1) * HEAD_DIM)
            sc = jnp.where(valid, _dot_nt(q_ref[:, sl], kk[:, sl]), NEG)
            m = jnp.max(sc, axis=-1, keepdims=True)
            p = jnp.exp(sc - m)
            l = jnp.sum(p, axis=-1, keepdims=True)
            pn = p * (1.0 / l)
            o_ref[:, sl] = _dot(pn.astype(BF16), vv[:, sl])
            lse_ref[:, sl] = jnp.broadcast_to(m + jnp.log(l), (BLOCK, HEAD_DIM))

    blk = lambda f: pl.BlockSpec((BLOCK, w), f)
    view = lambda t: t.reshape(ln, d * w)
    o, lse = pl.pallas_call(
        body, name=f"attn_fwd_d{d}", grid=(nblk,),
        out_shape=[_sds((ln, d * w), F32)] * 2,
        in_specs=[blk(cur), blk(prev), blk(cur), blk(prev), blk(cur)],
        out_specs=[blk(cur), blk(cur)], compiler_params=_params(1, 32),
    )(view(q), view(k), view(k), view(v), view(v))
    return o.reshape(s, w), lse.reshape(s, w)


def _memkv_fwd(mem, g, w):
    n_layers = w.shape[0]

    def body(mem_ref, g_ref, w_ref, kv_ref):
        mb = mem_ref[...]
        r = lax.rsqrt(jnp.mean(mb * mb, axis=-1, keepdims=True) + EPS)
        mn = ((mb * r) * g_ref[...]).astype(BF16)
        kv_ref[...] = _dot(mn, w_ref[...]).astype(BF16)

    return pl.pallas_call(
        body, name="memkv_fwd", grid=(n_layers,),
        out_shape=_sds((n_layers, N_MEM, 2 * MEM_WIDTH), BF16),
        in_specs=[_whole(mem.shape), pl.BlockSpec((None, 1, D_MODEL), lambda l: (l, 0, 0)),
                  pl.BlockSpec((None, D_MODEL, 2 * MEM_WIDTH), lambda l: (l, 0, 0))],
        out_specs=pl.BlockSpec((None, N_MEM, 2 * MEM_WIDTH), lambda l: (l, 0, 0)),
        compiler_params=_params(1, 32),
    )(mem, g.reshape(n_layers, 1, D_MODEL), w)


def _mix_groups(o_refs, l_refs):
    ls = [r[...] for r in l_refs]
    mx = jnp.maximum(jnp.maximum(ls[0], ls[1]), ls[2])
    es = [jnp.exp(t - mx) for t in ls]
    inv = 1.0 / (es[0] + es[1] + es[2])
    ws = [e * inv for e in es]
    mix = ws[0] * o_refs[0][...] + ws[1] * o_refs[1][...] + ws[2] * o_refs[2][...]
    return ws, mix


def _mem_probs(qm, km, h):
    sl = slice(h * HEAD_DIM, (h + 1) * HEAD_DIM)
    sc = _dot_nt(qm[:, sl], km[:, sl]) * SCALE
    e = jnp.exp(sc - jnp.max(sc, axis=-1, keepdims=True))
    return e * (1.0 / jnp.sum(e, axis=-1, keepdims=True))


def _mem_attn_into(qm, kv_ref, mo_ref):
    km, vm = kv_ref[:, :MEM_WIDTH], kv_ref[:, MEM_WIDTH:]
    for h in range(MEM_HEADS):
        sl = slice(h * HEAD_DIM, (h + 1) * HEAD_DIM)
        p = _mem_probs(qm, km, h)
        mo_ref[:, sl] = _dot(p.astype(BF16), vm[:, sl])


def _mem_attn_bwd(qm, kv_ref, dmem, dqm_ref, dkv_ref):
    km, vm = kv_ref[:, :MEM_WIDTH], kv_ref[:, MEM_WIDTH:]
    dmb = dmem.astype(BF16)
    for h in range(MEM_HEADS):
        sl = slice(h * HEAD_DIM, (h + 1) * HEAD_DIM)
        slv = slice(MEM_WIDTH + h * HEAD_DIM, MEM_WIDTH + (h + 1) * HEAD_DIM)
        p = _mem_probs(qm, km, h)
        dp = _dot_nt(dmb[:, sl], vm[:, sl])
        ds = (p * (dp - jnp.sum(dp * p, axis=-1, keepdims=True)) * SCALE).astype(BF16)
        dqm_ref[:, sl] = _dot(ds, km[:, sl]).astype(BF16)
        dkv_ref[:, sl] += _dot_tn(ds, qm[:, sl])
        dkv_ref[:, slv] += _dot_tn(p.astype(BF16), dmb[:, sl])


def _post_attn(os_, ls_, qm, kv, z, x, w_out):
    s, d = x.shape
    gw = GROUP_WIDTH
    nb = gw + MEM_WIDTH

    def body(o0, o1, o2, l0, l1, l2, qm_ref, kv_ref, z_ref, x_ref, w_ref, y_ref, mo_ref, h_ref):
        _, mix = _mix_groups((o0, o1, o2), (l0, l1, l2))
        _mem_attn_into(qm_ref[...], kv_ref, mo_ref)
        sz, _ = _silu_parts(z_ref[...])
        y_ref[:, :gw] = (mix * sz[:, :gw]).astype(BF16)
        y_ref[:, gw:] = (mo_ref[...] * sz[:, gw:]).astype(BF16)
        h_ref[...] = x_ref[...] + _dot(y_ref[...], w_ref[...])

    return pl.pallas_call(
        body, name="post_attn", grid=(s // ROW_TILE,),
        out_shape=[_sds((s, nb), BF16), _sds((s, MEM_WIDTH), F32), _sds((s, d), F32)],
        in_specs=[_rows(gw)] * 6 + [_rows(MEM_WIDTH), _whole(kv.shape), _rows(nb), _rows(d), _whole(w_out.shape)],
        out_specs=[_rows(nb), _rows(MEM_WIDTH), _rows(d)], compiler_params=_params(1, 40),
    )(*os_, *ls_, qm, kv, z, x, w_out)


def _inproj_conv(x, g, w):
    s, d = x.shape
    c = CONV_WIDTH
    nz = w.shape[1] - 3 * c - MEM_WIDTH

    def body(x_ref, g_ref, w_ref, hn_ref, bg_ref, cg_ref, u_ref, qm_ref, z_ref):
        xb = x_ref[...]
        r = lax.rsqrt(jnp.mean(xb * xb, axis=-1, keepdims=True) + EPS)
        hn = ((xb * r) * g_ref[...]).astype(BF16)
        hn_ref[...] = hn
        bg_ref[...] = _dot(hn, w_ref[:, 0:c])
        cg_ref[...] = _dot(hn, w_ref[:, c:2 * c])
        u_ref[...] = _dot(hn, w_ref[:, 2 * c:3 * c])
        qm_ref[...] = _dot(hn, w_ref[:, 3 * c:3 * c + MEM_WIDTH]).astype(BF16)
        z_ref[...] = _dot(hn, w_ref[:, 3 * c + MEM_WIDTH:])

    return pl.pallas_call(
        body, name="inproj_conv", grid=(s // ROW_TILE,),
        out_shape=[_sds((s, d), BF16)] + [_sds((s, c), F32)] * 3 + [_sds((s, MEM_WIDTH), BF16), _sds((s, nz), F32)],
        in_specs=[_rows(d), _whole((1, d)), _whole(w.shape)],
        out_specs=[_rows(d)] + [_rows(c)] * 3 + [_rows(MEM_WIDTH), _rows(nz)],
        compiler_params=_params(1, 56),
    )(x, g, w)


HALO = 8


def _halo_before(width, tm=ROW_TILE):
    return pl.BlockSpec((HALO, width), lambda i: (jnp.maximum(i * (tm // HALO) - 1, 0), 0))


def _halo_after(width, n_rows, tm=ROW_TILE):
    return pl.BlockSpec((HALO, width), lambda i: (jnp.minimum((i + 1) * (tm // HALO), n_rows // HALO - 1), 0))


def _conv_taps(cg_ref, u_ref, cgh_ref, uh_ref, i):
    a = cg_ref[...] * u_ref[...]
    ah = jnp.where(i > 0, cgh_ref[...] * uh_ref[...], 0.0)
    row = lax.broadcasted_iota(jnp.int32, a.shape, 0)
    a1 = jnp.where(row == 0, ah[HALO - 1:HALO], pltpu.roll(a, 1, 0))
    a2 = jnp.where(row == 0, ah[HALO - 2:HALO - 1], jnp.where(row == 1, ah[HALO - 1:HALO], pltpu.roll(a, 2, 0)))
    return a, a1, a2


def _post_conv_loss(bg, cg, u, qm, kv, z, h1, w_out, cw, gf, tgt):
    s, d = h1.shape
    c = CONV_WIDTH
    nb = c + MEM_WIDTH

    def body(bg_ref, cg_ref, u_ref, cgh_ref, uh_ref, qm_ref, kv_ref, z_ref, h_ref, w_ref, cw_ref, gf_ref, t_ref,
             y_ref, mo_ref, dh_ref, dhb_ref, loss_ref, dgf_ref):
        i = pl.program_id(0)
        a, a1, a2 = _conv_taps(cg_ref, u_ref, cgh_ref, uh_ref, i)
        conv = cw_ref[0:1, :] * a2 + cw_ref[1:2, :] * a1 + cw_ref[2:3, :] * a
        mix = bg_ref[...] * conv
        _mem_attn_into(qm_ref[...], kv_ref, mo_ref)
        sz, _ = _silu_parts(z_ref[...])
        y_ref[:, :c] = (mix * sz[:, :c]).astype(BF16)
        y_ref[:, c:] = (mo_ref[...] * sz[:, c:]).astype(BF16)
        h2 = h_ref[...] + _dot(y_ref[...], w_ref[...])
        r = lax.rsqrt(jnp.mean(h2 * h2, axis=-1, keepdims=True) + EPS)
        nh = h2 * r
        gfv = gf_ref[...]
        diff = nh * gfv - t_ref[...]
        dout = diff * (1.0 / d)
        dn = dout * gfv
        dh2 = r * dn - h2 * ((r * r * r) * jnp.mean(dn * h2, axis=-1, keepdims=True))
        dh_ref[...] = dh2
        dhb_ref[...] = dh2.astype(BF16)

        @pl.when(i == 0)
        def _():
            loss_ref[...] = jnp.zeros_like(loss_ref)
            dgf_ref[...] = jnp.zeros_like(dgf_ref)

        loss_ref[...] += 0.5 * jnp.sum(jnp.mean(diff * diff, axis=-1, keepdims=True))
        dgf_ref[...] += jnp.sum(dout * nh, axis=0, keepdims=True)

    return pl.pallas_call(
        body, name="post_conv_loss", grid=(s // ROW_TILE,),
        out_shape=[_sds((s, nb), BF16), _sds((s, MEM_WIDTH), F32), _sds((s, d), F32), _sds((s, d), BF16),
                   _sds((8, LANES), F32), _sds((1, d), F32)],
        in_specs=[_rows(c)] * 3 + [_halo_before(c)] * 2 + [_rows(MEM_WIDTH), _whole(kv.shape), _rows(nb), _rows(d),
                  _whole(w_out.shape), _whole(cw.shape), _whole((1, d)), _rows(d)],
        out_specs=[_rows(nb), _rows(MEM_WIDTH), _rows(d), _rows(d), _whole((8, LANES)), _whole((1, d))],
        compiler_params=_params(1, 48),
    )(bg, cg, u, cg, u, qm, kv, z, h1, w_out, cw, gf, tgt)


def _bwd_post_conv(dhb, w_out_t, bg, cg, u, z, qm, kv, mo, cw):
    s = dhb.shape[0]
    c = CONV_WIDTH
    nb = c + MEM_WIDTH

    def body(dh_ref, w_ref, bg_ref, cg_ref, u_ref, cgh_ref, uh_ref, z_ref, qm_ref, kv_ref, mo_ref, cw_ref,
             dz_ref, dbg_ref, dc_ref, dqm_ref, dkv_ref):
        i = pl.program_id(0)

        @pl.when(i == 0)
        def _():
            dkv_ref[...] = jnp.zeros_like(dkv_ref)

        dy = _dot(dh_ref[...], w_ref[...])
        sz, dsz = _silu_parts(z_ref[...])
        a, a1, a2 = _conv_taps(cg_ref, u_ref, cgh_ref, uh_ref, i)
        conv = cw_ref[0:1, :] * a2 + cw_ref[1:2, :] * a1 + cw_ref[2:3, :] * a
        bgv = bg_ref[...]
        dz_ref[:, :c] = (dy[:, :c] * (bgv * conv) * dsz[:, :c]).astype(BF16)
        dz_ref[:, c:] = (dy[:, c:] * mo_ref[...] * dsz[:, c:]).astype(BF16)
        dbr = dy * sz
        dmix = dbr[:, :c]
        dbg_ref[...] = (dmix * conv).astype(BF16)
        dc_ref[...] = dmix * bgv
        _mem_attn_bwd(qm_ref[...], kv_ref, dbr[:, c:], dqm_ref, dkv_ref)

    return pl.pallas_call(
        body, name="bwd_post_conv", grid=(s // ROW_TILE,),
        out_shape=[_sds((s, nb), BF16), _sds((s, c), BF16), _sds((s, c), F32), _sds((s, MEM_WIDTH), BF16),
                   _sds(kv.shape, F32)],
        in_specs=[_rows(D_MODEL), _whole(w_out_t.shape)] + [_rows(c)] * 3 + [_halo_before(c)] * 2
                 + [_rows(nb), _rows(MEM_WIDTH), _whole(kv.shape), _rows(MEM_WIDTH), _whole(cw.shape)],
        out_specs=[_rows(nb), _rows(c), _rows(c), _rows(MEM_WIDTH), _whole(kv.shape)],
        compiler_params=_params(1, 48),
    )(dhb, w_out_t, bg, cg, u, cg, u, z, qm, kv, mo, cw)


def _bwd_conv(dconv, cg, u, cw):
    s, c = dconv.shape
    tm = ROW_TILE
    last = s // tm - 1

    def body(dc_ref, dcn_ref, cg_ref, u_ref, cgh_ref, uh_ref, cw_ref, dcg_ref, du_ref, dcw_ref):
        i = pl.program_id(0)

        @pl.when(i == 0)
        def _():
            dcw_ref[...] = jnp.zeros_like(dcw_ref)

        dc = dc_ref[...]
        dcn = jnp.where(i < last, dcn_ref[...], 0.0)
        row = lax.broadcasted_iota(jnp.int32, dc.shape, 0)
        d1 = jnp.where(row == tm - 1, dcn[0:1], pltpu.roll(dc, tm - 1, 0))
        d2 = jnp.where(row == tm - 1, dcn[1:2], jnp.where(row == tm - 2, dcn[0:1], pltpu.roll(dc, tm - 2, 0)))
        da = cw_ref[2:3, :] * dc + cw_ref[1:2, :] * d1 + cw_ref[0:1, :] * d2
        a, a1, a2 = _conv_taps(cg_ref, u_ref, cgh_ref, uh_ref, i)
        dcg_ref[...] = (da * u_ref[...]).astype(BF16)
        du_ref[...] = (da * cg_ref[...]).astype(BF16)
        dcw_ref[0:1, :] += jnp.sum(dc * a2, axis=0, keepdims=True)
        dcw_ref[1:2, :] += jnp.sum(dc * a1, axis=0, keepdims=True)
        dcw_ref[2:3, :] += jnp.sum(dc * a, axis=0, keepdims=True)

    return pl.pallas_call(
        body, name="bwd_conv", grid=(s // tm,),
        out_shape=[_sds((s, c), BF16), _sds((s, c), BF16), _sds((8, c), F32)],
        in_specs=[_rows(c), _halo_after(c, s), _rows(c), _rows(c), _halo_before(c), _halo_before(c), _whole(cw.shape)],
        out_specs=[_rows(c), _rows(c), _whole((8, c))], compiler_params=_params(1, 40),
    )(dconv, dconv, cg, u, cg, u, cw)


def _dgrad_norm(dproj, w_t, h, g, dres):
    s, d = h.shape
    n = dproj.shape[1]

    def body(dp_ref, w_ref, h_ref, g_ref, dr_ref, dh_ref, dg_ref):
        @pl.when(pl.program_id(0) == 0)
        def _():
            dg_ref[...] = jnp.zeros_like(dg_ref)

        dhn = _dot(dp_ref[...], w_ref[...])
        hb = h_ref[...]
        r = lax.rsqrt(jnp.mean(hb * hb, axis=-1, keepdims=True) + EPS)
        dg_ref[...] += jnp.sum(dhn * (hb * r), axis=0, keepdims=True)
        dn = dhn * g_ref[...]
        dh_ref[...] = dr_ref[...] + r * dn - hb * ((r * r * r) * jnp.mean(dn * hb, axis=-1, keepdims=True))

    return pl.pallas_call(
        body, name=f"dgrad_norm_{n}", grid=(s // ROW_TILE,),
        out_shape=[_sds((s, d), F32), _sds((1, d), F32)],
        in_specs=[_rows(n), _whole(w_t.shape), _rows(d), _whole((1, d)), _rows(d)],
        out_specs=[_rows(d), _whole((1, d))], compiler_params=_params(1, 56),
    )(dproj, w_t, h, g, dres)


def _matmul(a, b, tn, name):
    m, k = a.shape
    n = b.shape[1]

    def body(a_ref, b_ref, o_ref):
        o_ref[...] = _dot(a_ref[...], b_ref[...])

    return pl.pallas_call(
        body, name=name, grid=(n // tn,), out_shape=_sds((m, n), F32),
        in_specs=[_whole(a.shape), pl.BlockSpec((k, tn), lambda j: (0, j))],
        out_specs=pl.BlockSpec((m, tn), lambda j: (0, j)), compiler_params=_params(1, 40),
    )(a, b)


def _memkv_bwd(dkv, w, mem, g):
    n_layers = w.shape[0]

    def body(dkv_ref, w_ref, mem_ref, g_ref, dw_ref, dg_ref):
        mb = mem_ref[...]
        r = lax.rsqrt(jnp.mean(mb * mb, axis=-1, keepdims=True) + EPS)
        nm = mb * r
        mn = (nm * g_ref[...]).astype(BF16)
        dkvb = dkv_ref[...].astype(BF16)
        dw_ref[...] = _dot_tn(mn, dkvb)
        dmn = _dot_nt(dkvb, w_ref[...])
        dg_ref[...] = jnp.sum(dmn * nm, axis=0, keepdims=True)

    lay = lambda *shape: pl.BlockSpec((None,) + shape, lambda l: (l, 0, 0))
    return pl.pallas_call(
        body, name="memkv_bwd", grid=(n_layers,),
        out_shape=[_sds((n_layers, D_MODEL, 2 * MEM_WIDTH), F32), _sds((n_layers, 1, D_MODEL), F32)],
        in_specs=[lay(N_MEM, 2 * MEM_WIDTH), lay(D_MODEL, 2 * MEM_WIDTH), _whole(mem.shape), lay(1, D_MODEL)],
        out_specs=[lay(D_MODEL, 2 * MEM_WIDTH), lay(1, D_MODEL)], compiler_params=_params(1, 32),
    )(dkv, w, mem, g.reshape(n_layers, 1, D_MODEL))


def _bwd_post_attn(dhb, w_out_t, z, os_, ls_, qm, kv, mo, head_ones):
    s = dhb.shape[0]
    gw = GROUP_WIDTH
    nb = gw + MEM_WIDTH

    def body(dh_ref, w_ref, z_ref, o0, o1, o2, l0, l1, l2, qm_ref, kv_ref, mo_ref, bd_ref,
             dz_ref, do0, do1, do2, dl0, dl1, dl2, dqm_ref, dkv_ref):
        @pl.when(pl.program_id(0) == 0)
        def _():
            dkv_ref[...] = jnp.zeros_like(dkv_ref)

        dy = _dot(dh_ref[...], w_ref[...])
        ws, mix = _mix_groups((o0, o1, o2), (l0, l1, l2))
        sz, dsz = _silu_parts(z_ref[...])
        dz_ref[:, :gw] = (dy[:, :gw] * mix * dsz[:, :gw]).astype(BF16)
        dz_ref[:, gw:] = (dy[:, gw:] * mo_ref[...] * dsz[:, gw:]).astype(BF16)
        dbr = dy * sz
        dmix = dbr[:, :gw]
        t = dmix * mix
        th = t.astype(BF16)
        tl = (t - th.astype(F32)).astype(BF16)
        rs = _dot(th, bd_ref[...]) + _dot(tl, bd_ref[...])
        for wg, do_ref, dl_ref in zip(ws, (do0, do1, do2), (dl0, dl1, dl2)):
            do_ref[...] = (wg * dmix).astype(BF16)
            dl_ref[...] = wg * rs
        _mem_attn_bwd(qm_ref[...], kv_ref, dbr[:, gw:], dqm_ref, dkv_ref)

    return pl.pallas_call(
        body, name="bwd_post_attn", grid=(s // ROW_TILE,),
        out_shape=[_sds((s, nb), BF16)] + [_sds((s, gw), BF16)] * 3 + [_sds((s, gw), F32)] * 3
                  + [_sds((s, MEM_WIDTH), BF16), _sds(kv.shape, F32)],
        in_specs=[_rows(D_MODEL), _whole(w_out_t.shape), _rows(nb)] + [_rows(gw)] * 6
                 + [_rows(MEM_WIDTH), _whole(kv.shape), _rows(MEM_WIDTH), _whole(head_ones.shape)],
        out_specs=[_rows(nb)] + [_rows(gw)] * 6 + [_rows(MEM_WIDTH), _whole(kv.shape)],
        compiler_params=_params(1, 48),
    )(dhb, w_out_t, z, *os_, *ls_, qm, kv, mo, head_ones)


def _attn_bwd(q, k, v, lse, do, dl, tabs, d):
    s, w = q.shape
    ln = s // d
    nb = ln // BLOCK
    nblk = d * nb
    reps = w // LANES
    cur, prev = _stream_maps(d, nb, nblk)
    qmap = lambda n: cur(jnp.minimum(n, nblk - 1))
    pmap = lambda n: prev(jnp.minimum(n, nblk - 1))
    omap = lambda n: cur(jnp.maximum(n - 1, 0))

    def body(q_ref, kp_ref, kc_ref, vp_ref, vc_ref, l_ref, do_ref, dl_ref, cq, saq, sbq, ck, sak, sbk,
             dq_ref, dk_ref, dv_ref, acck, accv, dqs):
        n = pl.program_id(0)

        @pl.when(n == 0)
        def _():
            acck[...] = jnp.zeros_like(acck)
            accv[...] = jnp.zeros_like(accv)

        @pl.when(n < nblk)
        def _():
            valid = _band_mask((n % nb) == 0)
            kk = jnp.concatenate([kp_ref[...], kc_ref[...]], axis=0)
            vv = jnp.concatenate([vp_ref[...], vc_ref[...]], axis=0)
            for h in range(HEADS_PER_GROUP):
                sl = slice(h * HEAD_DIM, (h + 1) * HEAD_DIM)
                col = slice(h * HEAD_DIM, h * HEAD_DIM + 1)
                qh = q_ref[:, sl]
                dob = do_ref[:, sl]
                sc = jnp.where(valid, _dot_nt(qh, kk[:, sl]), NEG)
                p = jnp.exp(sc - l_ref[:, col])
                dp = _dot_nt(dob, vv[:, sl])
                ds = (p * (dp - dl_ref[:, col])).astype(BF16)
                dqs[:, sl] = _dot(ds, kk[:, sl]) * SCALE
                acck[:, sl] += _dot_tn(ds, qh)
                accv[:, sl] += _dot_tn(p.astype(BF16), dob)
            tq = [jnp.tile(r[...], (1, reps)) for r in (cq, saq, sbq)]
            dq_ref[...] = _rope_bwd(dqs[...], *tq).astype(BF16)

        tk = [jnp.tile(r[...], (1, reps)) for r in (ck, sak, sbk)]
        dk_ref[...] = _rope_bwd(acck[0:BLOCK, :], *tk).astype(BF16)
        dv_ref[...] = accv[0:BLOCK, :].astype(BF16)
        acck[0:BLOCK, :] = acck[BLOCK:, :]
        accv[0:BLOCK, :] = accv[BLOCK:, :]
        acck[BLOCK:, :] = jnp.zeros((BLOCK, w), F32)
        accv[BLOCK:, :] = jnp.zeros((BLOCK, w), F32)

    blk = lambda f: pl.BlockSpec((BLOCK, w), f)
    tblk = lambda f: pl.BlockSpec((BLOCK, LANES), f)
    view = lambda t: t.reshape(ln, d * t.shape[1])
    tv = [view(t) for t in tabs]
    dq, dk, dv = pl.pallas_call(
        body, name=f"attn_bwd_d{d}", grid=(nblk + 1,),
        out_shape=[_sds((ln, d * w), BF16)] * 3,
        in_specs=[blk(qmap), blk(pmap), blk(qmap), blk(pmap), blk(qmap), blk(qmap), blk(qmap), blk(qmap)]
                 + [tblk(qmap)] * 3 + [tblk(omap)] * 3,
        out_specs=[blk(qmap), blk(omap), blk(omap)],
        scratch_shapes=[pltpu.VMEM((2 * BLOCK, w), F32), pltpu.VMEM((2 * BLOCK, w), F32), pltpu.VMEM((BLOCK, w), F32)],
        compiler_params=_params(1, 32),
    )(view(q), view(k), view(k), view(v), view(v), view(lse), view(do), view(dl), *tv, *tv)
    return dq.reshape(s, w), dk.reshape(s, w), dv.reshape(s, w)


def _position():
    return lax.axis_index("x"), lax.axis_index("y"), lax.axis_index("c")


def _all_gather(xs, name):
    def body(x_ref, out_ref, send_sems, recv_sems, local_sem):
        x, y, c = _position()
        me, sibling = (x, y, c), (x, y, 1 - c)
        chips = [(1 - x, y), (x, 1 - y), (1 - x, 1 - y)]

        def rows(px, py, pc):
            return out_ref.at[4 * px + 2 * py + pc]

        def copy(k, block, to, src=None):
            return pltpu.make_async_remote_copy(
                src_ref=rows(*block) if src is None else src, dst_ref=rows(*block),
                send_sem=send_sems.at[k], recv_sem=recv_sems.at[k], device_id=to, device_id_type=MESH)

        mine = pltpu.make_async_copy(x_ref, rows(*me), local_sem)
        mine.start()
        first = [copy(0, me, sibling, src=x_ref)]
        first += [copy(1 + j, me, (*chip, c), src=x_ref) for j, chip in enumerate(chips)]
        for cp in first:
            cp.start()
        passed = [copy(4 + j, (*chip, c), sibling) for j, chip in enumerate(chips)]
        for j, chip in enumerate(chips):
            copy(1 + j, (*chip, c), me).wait_recv()
            passed[j].start()
        copy(0, sibling, me).wait_recv()
        for j, chip in enumerate(chips):
            copy(4 + j, (*chip, 1 - c), me).wait_recv()
        for cp in first + passed:
            cp.wait_send()
        mine.wait()

    return pl.pallas_call(
        body, name=name, out_shape=_sds((N_DEV,) + xs.shape, xs.dtype),
        in_specs=[ANY], out_specs=ANY,
        scratch_shapes=[pltpu.SemaphoreType.DMA((7,)), pltpu.SemaphoreType.DMA((7,)), pltpu.SemaphoreType.DMA],
    )(xs)


def _rs_to_sibling(gp):
    def body(g_ref, recv_ref, send_sems, recv_sems):
        x, y, c = _position()
        copies = []
        for k in range(4):
            copies.append(pltpu.make_async_remote_copy(
                src_ref=g_ref.at[2 * k + (1 - c)], dst_ref=recv_ref.at[k],
                send_sem=send_sems.at[k], recv_sem=recv_sems.at[k], device_id=(x, y, 1 - c), device_id_type=MESH))
        for cp in copies:
            cp.start()
        for cp in copies:
            cp.wait()

    return pl.pallas_call(
        body, name="rs_to_sibling", out_shape=_sds((4,) + gp.shape[1:], gp.dtype),
        in_specs=[ANY], out_specs=ANY,
        scratch_shapes=[pltpu.SemaphoreType.DMA((4,)), pltpu.SemaphoreType.DMA((4,))],
    )(gp)


def _rs_to_chips(pb):
    def body(p_ref, recv_ref, send_sems, recv_sems):
        x, y, c = _position()
        chips = [(1 - x, y), (x, 1 - y), (1 - x, 1 - y)]
        copies = []
        for j, (px, py) in enumerate(chips):
            copies.append(pltpu.make_async_remote_copy(
                src_ref=p_ref.at[2 * px + py], dst_ref=recv_ref.at[j],
                send_sem=send_sems.at[j], recv_sem=recv_sems.at[j], device_id=(px, py, c), device_id_type=MESH))
        for cp in copies:
            cp.start()
        for cp in copies:
            cp.wait()

    return pl.pallas_call(
        body, name="rs_to_chips", out_shape=_sds((3,) + pb.shape[1:], pb.dtype),
        in_specs=[ANY], out_specs=ANY,
        scratch_shapes=[pltpu.SemaphoreType.DMA((3,)), pltpu.SemaphoreType.DMA((3,))],
    )(pb)


RS_TILE = 1024


def _rs_add_sibling(gp, recv, c_arr):
    _, r, l = gp.shape

    def body(c_ref, g_ref, r_ref, pf_ref, pb_ref):
        sm = g_ref[...].astype(F32) + r_ref[...].astype(F32)
        pf_ref[...] = sm
        pb_ref[...] = sm.astype(BF16)

    spec = pl.BlockSpec((None, RS_TILE, l), lambda k, i, c: (k, i, 0))
    return pl.pallas_call(
        body, name="rs_add_sibling",
        grid_spec=pltpu.PrefetchScalarGridSpec(
            num_scalar_prefetch=1, grid=(4, r // RS_TILE),
            in_specs=[pl.BlockSpec((None, RS_TILE, l), lambda k, i, c: (2 * k + c[0], i, 0)), spec],
            out_specs=[spec, spec]),
        out_shape=[_sds((4, r, l), F32), _sds((4, r, l), BF16)], compiler_params=_params(2, 32),
    )(c_arr, gp, recv)


def _rs_add_chips(pf, recv, k_arr):
    _, r, l = pf.shape

    def body(k_ref, p_ref, r_ref, o_ref):
        o_ref[...] = ((p_ref[...] + r_ref[0].astype(F32)) + r_ref[1].astype(F32)) + r_ref[2].astype(F32)

    return pl.pallas_call(
        body, name="rs_add_chips",
        grid_spec=pltpu.PrefetchScalarGridSpec(
            num_scalar_prefetch=1, grid=(r // RS_TILE,),
            in_specs=[pl.BlockSpec((None, RS_TILE, l), lambda i, k: (k[0], i, 0)),
                      pl.BlockSpec((3, RS_TILE, l), lambda i, k: (0, i, 0))],
            out_specs=pl.BlockSpec((RS_TILE, l), lambda i, k: (i, 0))),
        out_shape=_sds((r, l), F32), compiler_params=_params(1, 32),
    )(k_arr, pf, recv)


def _sum_devices(g):
    def body(g_ref, o_ref):
        acc = g_ref[0]
        for j in range(1, N_DEV):
            acc = acc + g_ref[j]
        o_ref[...] = acc

    return pl.pallas_call(body, name="sum_devices", out_shape=_sds(g.shape[1:], F32))(g)


def _adamw(w, g, m, v, name):
    shape = w.shape
    w2, g2, m2, v2 = [t.reshape((-1, shape[-1])) for t in (w, g, m, v)]
    r, c = w2.shape
    tr = ROW_TILE if r % ROW_TILE == 0 else r

    def body(w_ref, g_ref, m_ref, v_ref, d_ref, nm_ref, nv_ref):
        gv = g_ref[...]
        nm = ADAM_B1 * m_ref[...] + (1.0 - ADAM_B1) * gv
        nv = ADAM_B2 * v_ref[...] + (1.0 - ADAM_B2) * (gv * gv)
        m_hat = nm / (1.0 - ADAM_B1 ** ADAM_STEP)
        v_hat = nv / (1.0 - ADAM_B2 ** ADAM_STEP)
        d_ref[...] = -ADAM_LR * (m_hat / (jnp.sqrt(v_hat) + ADAM_EPS) + ADAM_WD * w_ref[...])
        nm_ref[...] = nm
        nv_ref[...] = nv

    spec = pl.BlockSpec((tr, c), lambda i: (i, 0))
    outs = pl.pallas_call(
        body, name=name, grid=(r // tr,), out_shape=[_sds((r, c), F32)] * 3,
        in_specs=[spec] * 4, out_specs=[spec] * 3, compiler_params=_params(1, 32),
    )(w2, g2, m2, v2)
    return tuple(t.reshape(shape) for t in outs)


def _local_step(x, mem, pos, tgt, norm_g, mem_norm_g, final_g, cw, w_in0, w_in0_t, w_out0, w_out0_t,
                w_in1, w_in1_t, w_out1, w_out1_t, w_kv):
    tabs = _rope_tables(pos)
    g0, g1 = norm_g[0:1], norm_g[1:2]
    kv = _memkv_fwd(mem, mem_norm_g, w_kv)

    hn0, qs, ks, vs, qm0, z0 = _inproj_attn(x, g0, w_in0, tabs)
    os_, ls_ = [], []
    for j, d in enumerate(DILATIONS):
        o, l = _attn_fwd(qs[j], ks[j], vs[j], d)
        os_.append(o)
        ls_.append(l)
    y0, mo0, h1 = _post_attn(os_, ls_, qm0, kv[0], z0, x, w_out0)

    hn1, bg, cg, u, qm1, z1 = _inproj_conv(h1, g1, w_in1)
    y1, mo1, dh2, dh2b, loss_acc, d_final_g = _post_conv_loss(
        bg, cg, u, qm1, kv[1], z1, h1, w_out1, cw, final_g.reshape(1, -1), tgt)

    d_w_out1 = _matmul(y1.T, dh2b, 512, "wgrad_out1")
    dz1, dbg, dconv, dqm1, dkv1 = _bwd_post_conv(dh2b, w_out1_t, bg, cg, u, z1, qm1, kv[1], mo1, cw)
    dcg, du, dcw = _bwd_conv(dconv, cg, u, cw)
    dproj1 = jnp.concatenate([dbg, dcg, du, dqm1, dz1], axis=1)
    d_w_in1 = _matmul(hn1.T, dproj1, 512, "wgrad_in1")
    dh1, dg1 = _dgrad_norm(dproj1, w_in1_t, h1, g1, dh2)

    dh1b = dh1.astype(BF16)
    d_w_out0 = _matmul(y0.T, dh1b, 512, "wgrad_out0")
    gw = GROUP_WIDTH
    ones = (jnp.arange(gw)[:, None] // HEAD_DIM == jnp.arange(gw)[None, :] // HEAD_DIM).astype(BF16)
    res = _bwd_post_attn(dh1b, w_out0_t, z0, os_, ls_, qm0, kv[0], mo0, ones)
    dz0, dos, dls, dqm0, dkv0 = res[0], res[1:4], res[4:7], res[7], res[8]
    dqs, dks, dvs = [], [], []
    for j, d in enumerate(DILATIONS):
        dq, dk, dv = _attn_bwd(qs[j], ks[j], vs[j], ls_[j], dos[j], dls[j], tabs, d)
        dqs.append(dq)
        dks.append(dk)
        dvs.append(dv)
    dproj0 = jnp.concatenate(dqs + dks + dvs + [dqm0, dz0], axis=1)
    d_w_in0 = _matmul(hn0.T, dproj0, 512, "wgrad_in0")
    dx, dg0 = _dgrad_norm(dproj0, w_in0_t, x, g0, dh1)

    d_w_kv, d_mem_g = _memkv_bwd(jnp.stack([dkv0, dkv1]), w_kv, mem, mem_norm_g)
    small = jnp.concatenate([dg0, dg1, d_mem_g.reshape(2, -1), d_final_g, dcw[0:3]], axis=0)
    return loss_acc[0, 0], dx, dict(w_in0=d_w_in0, w_out0=d_w_out0, w_in1=d_w_in1, w_out1=d_w_out1,
                                    w_kv=d_w_kv, small=small)


def _pack_rows(n_elems):
    return n_elems // LANES


def _unpack_weights(gathered, shapes):
    out = {}
    off = 0
    for name, shp in shapes:
        n = 1
        for t in shp:
            n *= t
        rows = _pack_rows(n)
        blk = gathered[:, off:off + rows].reshape((N_DEV,) + shp)
        off += rows
        if name in ("w_in0", "w_out0", "w_in1"):
            out[name] = blk.transpose(1, 0, 2).reshape(shp[0], N_DEV * shp[1])
        elif name == "w_out1":
            out[name] = blk.reshape(N_DEV * shp[0], shp[1])
        else:
            out[name] = blk.transpose(1, 0, 2, 3).reshape(shp[0], N_DEV * shp[1], shp[2])
    return out


def _pack_grads(grads, shapes):
    parts = []
    for name, shp in shapes:
        g = grads[name]
        if name in ("w_in0", "w_out0", "w_in1"):
            blk = g.reshape(shp[0], N_DEV, shp[1]).transpose(1, 0, 2)
        elif name == "w_out1":
            blk = g.reshape(N_DEV, shp[0], shp[1])
        else:
            blk = g.reshape(shp[0], N_DEV, shp[1], shp[2]).transpose(1, 0, 2, 3)
        parts.append(blk.astype(BF16).reshape(N_DEV, -1, LANES))
    return jnp.concatenate(parts, axis=1)


def kernel(x, mem, positions, norm_g, mem_norm_g, w_mem_kv, attn_w_in, attn_w_out, conv_w_in, conv_w, conv_w_out, final_g, loss_target, m_norm_g, m_mem_norm_g, m_w_mem_kv, m_attn_w_in, m_attn_w_out, m_conv_w_in, m_conv_w, m_conv_w_out, m_final_g, v_norm_g, v_mem_norm_g, v_w_mem_kv, v_attn_w_in, v_attn_w_out, v_conv_w_in, v_conv_w, v_conv_w_out, v_final_g):
    px, py, pc = _position()
    me = 4 * px + 2 * py + pc

    local = dict(w_in0=attn_w_in[0], w_out0=attn_w_out[0], w_in1=conv_w_in[0], w_out1=conv_w_out[0], w_kv=w_mem_kv)
    shapes = [(k, tuple(v.shape)) for k, v in local.items()]
    packed = jnp.concatenate([v.astype(BF16).reshape(-1, LANES) for v in local.values()], axis=0)
    full = _unpack_weights(_all_gather(packed, "gather_weights"), shapes)
    cw_all = _all_gather(jnp.pad(conv_w[0], ((0, 5), (0, 0))), "gather_conv_w")
    cw = cw_all[:, 0:3].transpose(1, 0, 2).reshape(3, -1)

    loss_part, dx, grads = _local_step(
        x[0], mem[0], positions[0], loss_target[0], norm_g, mem_norm_g, final_g, cw,
        full["w_in0"], full["w_in0"].T, full["w_out0"], full["w_out0"].T,
        full["w_in1"], full["w_in1"].T, full["w_out1"], full["w_out1"].T, full["w_kv"])
    loss = lax.psum(loss_part, ("x", "y", "c"))

    gp = _pack_grads(grads, shapes)
    c_arr = jnp.reshape(pc, (1,)).astype(jnp.int32)
    k_arr = jnp.reshape(2 * px + py, (1,)).astype(jnp.int32)
    pf, pb = _rs_add_sibling(gp, _rs_to_sibling(gp), c_arr)
    gsum = _rs_add_chips(pf, _rs_to_chips(pb), k_arr)
    small = _sum_devices(_all_gather(grads["small"], "gather_small_grads"))

    g_local = {}
    off = 0
    for name, shp in shapes:
        n = 1
        for t in shp:
            n *= t
        g_local[name] = gsum[off:off + _pack_rows(n)].reshape(shp)
        off += _pack_rows(n)
    g_conv_w = lax.dynamic_slice(small[5:8], (0, me * LANES), (3, LANES))

    grad = dict(norm_g=small[0:2], mem_norm_g=small[2:4], w_mem_kv=g_local["w_kv"], attn_w_in=g_local["w_in0"][None],
                attn_w_out=g_local["w_out0"][None], conv_w_in=g_local["w_in1"][None], conv_w=g_conv_w[None],
                conv_w_out=g_local["w_out1"][None], final_g=small[4])
    weights = dict(norm_g=(norm_g, m_norm_g, v_norm_g), mem_norm_g=(mem_norm_g, m_mem_norm_g, v_mem_norm_g),
                   w_mem_kv=(w_mem_kv, m_w_mem_kv, v_w_mem_kv), attn_w_in=(attn_w_in, m_attn_w_in, v_attn_w_in),
                   attn_w_out=(attn_w_out, m_attn_w_out, v_attn_w_out), conv_w_in=(conv_w_in, m_conv_w_in, v_conv_w_in),
                   conv_w=(conv_w, m_conv_w, v_conv_w), conv_w_out=(conv_w_out, m_conv_w_out, v_conv_w_out),
                   final_g=(final_g, m_final_g, v_final_g))
    deltas, new_m, new_v = [], [], []
    for name, (w, m, v) in weights.items():
        d, nm, nv = _adamw(w, grad[name], m, v, "adamw_" + name)
        deltas.append(d)
        new_m.append(nm)
        new_v.append(nv)
    return (loss, dx[None], *[grad[n] for n in weights], *deltas, *new_m, *new_v)
```

```python
import functools

import jax
import jax.numpy as jnp
from jax import lax
from jax.experimental import pallas as pl
from jax.experimental.pallas import tpu as pltpu

F32 = jnp.float32
BF16 = jnp.bfloat16

N_DEV = 8
D_MODEL = 1024
HEAD_DIM = 64
ROT_DIM = HEAD_DIM // 4
ROPE_THETA = 500000.0
DILATIONS = (1, 4, 16)
HEADS_PER_GROUP = 8
GROUP_WIDTH = HEADS_PER_GROUP * HEAD_DIM
BLOCK = 128
N_MEM = 256
MEM_HEADS = 4
MEM_WIDTH = MEM_HEADS * HEAD_DIM
CONV_WIDTH = D_MODEL
EPS = 1e-6
SCALE = HEAD_DIM ** -0.5
NEG = -1e30

ADAM_LR = 0.001
ADAM_B1 = 0.9
ADAM_B2 = 0.999
ADAM_EPS = 1e-08
ADAM_WD = 0.01
ADAM_STEP = 10

ROW_TILE = 256
LANES = 128
MESH = pl.DeviceIdType.MESH
ANY = pl.BlockSpec(memory_space=pl.ANY)


def _dot(a, b):
    return lax.dot_general(a, b, (((1,), (0,)), ((), ())), preferred_element_type=F32)


def _dot_nt(a, b):
    return lax.dot_general(a, b, (((1,), (1,)), ((), ())), preferred_element_type=F32)


def _dot_tn(a, b):
    return lax.dot_general(a, b, (((0,), (0,)), ((), ())), preferred_element_type=F32)


def _params(n_grid, vmem_mb=48):
    return pltpu.CompilerParams(dimension_semantics=("arbitrary",) * n_grid, vmem_limit_bytes=vmem_mb << 20)


def _rows(width, tm=ROW_TILE):
    return pl.BlockSpec((tm, width), lambda i: (i, 0))


def _view_rows(width, d, tm=ROW_TILE):
    return pl.BlockSpec((tm // d, d * width), lambda i: (i, 0))


def _whole(shape):
    return pl.BlockSpec(shape, lambda *_: (0,) * len(shape))


def _sds(shape, dtype):
    return jax.ShapeDtypeStruct(shape, dtype)


def _silu_parts(z):
    sg = jax.nn.sigmoid(z)
    return z * sg, sg * (1.0 + z * (1.0 - sg))


def _to_view(scr, val, out_ref, d):
    tm, w = val.shape
    if d == 1:
        out_ref[...] = val.astype(out_ref.dtype)
        return
    for cb in range(w // LANES):
        scr[cb] = val[:, cb * LANES:(cb + 1) * LANES]
    for r in range(d):
        for cb in range(w // LANES):
            lo = r * w + cb * LANES
            out_ref[:, lo:lo + LANES] = scr[cb, pl.ds(r, tm // d, stride=d), :].astype(out_ref.dtype)


def _from_view(scr, in_ref, d):
    if d == 1:
        return in_ref[...].astype(F32)
    nc, tm, _ = scr.shape
    w = nc * LANES
    for r in range(d):
        for cb in range(nc):
            lo = r * w + cb * LANES
            scr[cb, pl.ds(r, tm // d, stride=d), :] = in_ref[:, lo:lo + LANES].astype(F32)
    return jnp.concatenate([scr[cb] for cb in range(nc)], axis=1)


def _view_scratch(width, tm=ROW_TILE):
    return pltpu.VMEM((width // LANES, tm, LANES), F32)


def _rope_tables(pos):
    half = ROT_DIM // 2
    inv_freq = ROPE_THETA ** (-jnp.arange(half, dtype=F32) * (2.0 / ROT_DIM))
    ang = pos.astype(F32)[:, None] * inv_freq
    cos, sin = jnp.cos(ang), jnp.sin(ang)
    s = pos.shape[0]
    z8 = jnp.zeros((s, half), F32)
    rest = HEAD_DIM - ROT_DIM
    cosf = jnp.concatenate([cos, cos, jnp.ones((s, rest), F32)], axis=1)
    sa = jnp.concatenate([-sin, z8, jnp.zeros((s, rest), F32)], axis=1)
    sb = jnp.concatenate([z8, sin, jnp.zeros((s, rest), F32)], axis=1)
    return tuple(jnp.tile(t, (1, LANES // HEAD_DIM)) for t in (cosf, sa, sb))


def _rope_fwd(t, cv, sav, sbv):
    w = t.shape[1]
    return t * cv + pltpu.roll(t, w - ROT_DIM // 2, 1) * sav + pltpu.roll(t, ROT_DIM // 2, 1) * sbv


def _rope_bwd(g, cv, sav, sbv):
    w = g.shape[1]
    return g * cv + pltpu.roll(g * sav, ROT_DIM // 2, 1) + pltpu.roll(g * sbv, w - ROT_DIM // 2, 1)


def _project(hn, wg_ref, proj_scr):
    c = wg_ref.shape[2]
    for j in range(N_DEV):
        proj_scr[:, j * c:(j + 1) * c] = _dot(hn, wg_ref[j])


def _inproj_attn(x, g, wg, tabs):
    s, d_model = x.shape
    gw = GROUP_WIDTH
    n = N_DEV * wg.shape[2]
    nz = n - 9 * gw - MEM_WIDTH
    reps = gw // LANES
    tm = ROW_TILE

    def body(x_ref, g_ref, w_ref, c_ref, sa_ref, sb_ref, hn_ref, hnt_ref, *rest):
        outs, (proj, scr, tscr) = rest[:-3], rest[-3:]
        q_refs, k_refs, v_refs, t_refs, qm_ref, z_ref = outs[0:3], outs[3:6], outs[6:9], outs[9:18], outs[18], outs[19]
        xb = x_ref[...]
        r = lax.rsqrt(jnp.mean(xb * xb, axis=-1, keepdims=True) + EPS)
        hn = ((xb * r) * g_ref[...]).astype(BF16)
        hn_ref[...] = hn
        hnt_ref[...] = hn.T
        _project(hn, w_ref, proj)
        tab = (c_ref[...], sa_ref[...], sb_ref[...])
        cv, sav, sbv = [jnp.tile(t, (1, reps)) for t in tab]
        for j, d in enumerate(DILATIONS):
            tq = _rope_fwd(proj[:, j * gw:(j + 1) * gw], cv, sav, sbv)
            _to_view(scr, tq * SCALE, q_refs[j], d)
            tk = _rope_fwd(proj[:, (3 + j) * gw:(4 + j) * gw], cv, sav, sbv)
            _to_view(scr, tk, k_refs[j], d)
            _to_view(scr, proj[:, (6 + j) * gw:(7 + j) * gw], v_refs[j], d)
            for i in range(3):
                _to_view(tscr, tab[i], t_refs[3 * j + i], d)
        qm_ref[...] = proj[:, 9 * gw:9 * gw + MEM_WIDTH].astype(BF16)
        z_ref[...] = proj[:, 9 * gw + MEM_WIDTH:]

    views = [_sds((s // d, d * gw), BF16) for d in DILATIONS]
    tviews = [_sds((s // d, d * LANES), F32) for d in DILATIONS for _ in range(3)]
    out_shape = ([_sds((s, d_model), BF16), _sds((d_model, s), BF16)] + views * 3 + tviews
                 + [_sds((s, MEM_WIDTH), BF16), _sds((s, nz), F32)])
    vspecs = [_view_rows(gw, d) for d in DILATIONS]
    tspecs = [_view_rows(LANES, d) for d in DILATIONS for _ in range(3)]
    out_specs = ([_rows(d_model), pl.BlockSpec((d_model, tm), lambda i: (0, i))] + vspecs * 3 + tspecs
                 + [_rows(MEM_WIDTH), _rows(nz)])
    res = pl.pallas_call(
        body, name="inproj_attn", grid=(s // tm,), out_shape=out_shape,
        in_specs=[_rows(d_model), _whole((1, d_model)), _whole(wg.shape), _rows(LANES), _rows(LANES), _rows(LANES)],
        out_specs=out_specs,
        scratch_shapes=[pltpu.VMEM((tm, n), F32), _view_scratch(gw), _view_scratch(LANES)],
        compiler_params=_params(1, 60),
    )(x, g, wg, *tabs)
    tabs_v = [res[11 + 3 * j:14 + 3 * j] for j in range(3)]
    return res[0], res[1], res[2:5], res[5:8], res[8:11], tabs_v, res[20], res[21]


def _stream_maps(d, nb):
    def cur(n):
        return (n % nb, n // nb)

    def prev(n):
        return (jnp.maximum(n % nb - 1, 0), n // nb)

    return cur, prev


def _band_mask(first):
    qi = lax.broadcasted_iota(jnp.int32, (BLOCK, 2 * BLOCK), 0)
    kj = lax.broadcasted_iota(jnp.int32, (BLOCK, 2 * BLOCK), 1)
    lo = qi + jnp.where(first, 2 * BLOCK, 0)
    return jnp.logical_or(jnp.logical_and(kj < BLOCK, kj >= lo), jnp.logical_and(kj >= BLOCK, (kj - BLOCK) <= qi))


def _attn_fwd(q, k, v, d):
    ln, dw = q.shape
    w = dw // d
    nb = ln // BLOCK
    nblk = d * nb
    cur, prev = _stream_maps(d, nb)

    def body(q_ref, kp_ref, kc_ref, vp_ref, vc_ref, o_ref, lse_ref):
        n = pl.program_id(0)
        valid = _band_mask((n % nb) == 0)
        kk = jnp.concatenate([kp_ref[...], kc_ref[...]], axis=0)
        vv = jnp.concatenate([vp_ref[...], vc_ref[...]], axis=0)
        for h in range(HEADS_PER_GROUP):
            sl = slice(h * HEAD_DIM, (h + 1) * HEAD_DIM)
            sc = jnp.where(valid, _dot_nt(q_ref[:, sl], kk[:, sl]), NEG)
            m = jnp.max(sc, axis=-1, keepdims=True)
            p = jnp.exp(sc - m)
            l = jnp.sum(p, axis=-1, keepdims=True)
            pn = p * (1.0 / l)
            o_ref[:, sl] = _dot(pn.astype(BF16), vv[:, sl])
            lse_ref[:, sl] = jnp.broadcast_to(m + jnp.log(l), (BLOCK, HEAD_DIM))

    blk = lambda f: pl.BlockSpec((BLOCK, w), f)
    return pl.pallas_call(
        body, name=f"attn_fwd_d{d}", grid=(nblk,),
        out_shape=[_sds((ln, dw), F32)] * 2,
        in_specs=[blk(cur), blk(prev), blk(cur), blk(prev), blk(cur)],
        out_specs=[blk(cur), blk(cur)], compiler_params=_params(1, 32),
    )(q, k, k, v, v)


def _memkv_fwd(mem, g, w):
    n_layers = w.shape[0]

    def body(mem_ref, g_ref, w_ref, kv_ref):
        mb = mem_ref[...]
        r = lax.rsqrt(jnp.mean(mb * mb, axis=-1, keepdims=True) + EPS)
        mn = ((mb * r) * g_ref[...]).astype(BF16)
        kv_ref[...] = _dot(mn, w_ref[...]).astype(BF16)

    return pl.pallas_call(
        body, name="memkv_fwd", grid=(n_layers,),
        out_shape=_sds((n_layers, N_MEM, 2 * MEM_WIDTH), BF16),
        in_specs=[_whole(mem.shape), pl.BlockSpec((None, 1, D_MODEL), lambda l: (l, 0, 0)),
                  pl.BlockSpec((None, D_MODEL, 2 * MEM_WIDTH), lambda l: (l, 0, 0))],
        out_specs=pl.BlockSpec((None, N_MEM, 2 * MEM_WIDTH), lambda l: (l, 0, 0)),
        compiler_params=_params(1, 32),
    )(mem, g.reshape(n_layers, 1, D_MODEL), w)


def _mix_groups(os_, ls_):
    mx = jnp.maximum(jnp.maximum(ls_[0], ls_[1]), ls_[2])
    es = [jnp.exp(t - mx) for t in ls_]
    inv = 1.0 / (es[0] + es[1] + es[2])
    ws = [e * inv for e in es]
    mix = ws[0] * os_[0] + ws[1] * os_[1] + ws[2] * os_[2]
    return ws, mix


def _mem_probs(qm, km, h):
    sl = slice(h * HEAD_DIM, (h + 1) * HEAD_DIM)
    sc = _dot_nt(qm[:, sl], km[:, sl]) * SCALE
    e = jnp.exp(sc - jnp.max(sc, axis=-1, keepdims=True))
    return e * (1.0 / jnp.sum(e, axis=-1, keepdims=True))


def _mem_attn_into(qm, kv_ref, mo_ref):
    km, vm = kv_ref[:, :MEM_WIDTH], kv_ref[:, MEM_WIDTH:]
    for h in range(MEM_HEADS):
        sl = slice(h * HEAD_DIM, (h + 1) * HEAD_DIM)
        p = _mem_probs(qm, km, h)
        mo_ref[:, sl] = _dot(p.astype(BF16), vm[:, sl])


def _mem_attn_bwd(qm, kv_ref, dmem, dqm_ref, dkv_ref):
    km, vm = kv_ref[:, :MEM_WIDTH], kv_ref[:, MEM_WIDTH:]
    dmb = dmem.astype(BF16)
    for h in range(MEM_HEADS):
        sl = slice(h * HEAD_DIM, (h + 1) * HEAD_DIM)
        slv = slice(MEM_WIDTH + h * HEAD_DIM, MEM_WIDTH + (h + 1) * HEAD_DIM)
        p = _mem_probs(qm, km, h)
        dp = _dot_nt(dmb[:, sl], vm[:, sl])
        ds = (p * (dp - jnp.sum(dp * p, axis=-1, keepdims=True)) * SCALE).astype(BF16)
        dqm_ref[:, sl] = _dot(ds, km[:, sl]).astype(BF16)
        dkv_ref[:, sl] += _dot_tn(ds, qm[:, sl])
        dkv_ref[:, slv] += _dot_tn(p.astype(BF16), dmb[:, sl])


def _post_attn(os_, ls_, qm, kv, z, x, wg_out):
    s, d_model = x.shape
    gw = GROUP_WIDTH
    nb = gw + MEM_WIDTH
    c = wg_out.shape[2]
    tm = ROW_TILE

    def body(o0, o1, o2, l0, l1, l2, qm_ref, kv_ref, z_ref, x_ref, w_ref, y_ref, yt_ref, mo_ref, h_ref, s0, s1):
        ov, lv = [], []
        for o_ref, l_ref, d in zip((o0, o1, o2), (l0, l1, l2), DILATIONS):
            ov.append(_from_view(s0, o_ref, d))
            lv.append(_from_view(s1, l_ref, d))
        _, mix = _mix_groups(ov, lv)
        _mem_attn_into(qm_ref[...], kv_ref, mo_ref)
        sz, _ = _silu_parts(z_ref[...])
        y_ref[:, :gw] = (mix * sz[:, :gw]).astype(BF16)
        y_ref[:, gw:] = (mo_ref[...] * sz[:, gw:]).astype(BF16)
        y = y_ref[...]
        yt_ref[...] = y.T
        for j in range(N_DEV):
            h_ref[:, j * c:(j + 1) * c] = x_ref[:, j * c:(j + 1) * c] + _dot(y, w_ref[j])

    vspecs = [_view_rows(gw, d) for d in DILATIONS]
    return pl.pallas_call(
        body, name="post_attn", grid=(s // tm,),
        out_shape=[_sds((s, nb), BF16), _sds((nb, s), BF16), _sds((s, MEM_WIDTH), F32), _sds((s, d_model), F32)],
        in_specs=vspecs * 2 + [_rows(MEM_WIDTH), _whole(kv.shape), _rows(nb), _rows(d_model), _whole(wg_out.shape)],
        out_specs=[_rows(nb), pl.BlockSpec((nb, tm), lambda i: (0, i)), _rows(MEM_WIDTH), _rows(d_model)],
        scratch_shapes=[_view_scratch(gw), _view_scratch(gw)],
        compiler_params=_params(1, 40),
    )(*os_, *ls_, qm, kv, z, x, wg_out)


def _inproj_conv(x, g, wg):
    s, d_model = x.shape
    c = CONV_WIDTH
    n = N_DEV * wg.shape[2]
    nz = n - 3 * c - MEM_WIDTH
    tm = ROW_TILE

    def body(x_ref, g_ref, w_ref, hn_ref, hnt_ref, bg_ref, cg_ref, u_ref, qm_ref, z_ref, proj):
        xb = x_ref[...]
        r = lax.rsqrt(jnp.mean(xb * xb, axis=-1, keepdims=True) + EPS)
        hn = ((xb * r) * g_ref[...]).astype(BF16)
        hn_ref[...] = hn
        hnt_ref[...] = hn.T
        _project(hn, w_ref, proj)
        bg_ref[...] = proj[:, 0:c]
        cg_ref[...] = proj[:, c:2 * c]
        u_ref[...] = proj[:, 2 * c:3 * c]
        qm_ref[...] = proj[:, 3 * c:3 * c + MEM_WIDTH].astype(BF16)
        z_ref[...] = proj[:, 3 * c + MEM_WIDTH:]

    return pl.pallas_call(
        body, name="inproj_conv", grid=(s // tm,),
        out_shape=[_sds((s, d_model), BF16), _sds((d_model, s), BF16)] + [_sds((s, c), F32)] * 3
                  + [_sds((s, MEM_WIDTH), BF16), _sds((s, nz), F32)],
        in_specs=[_rows(d_model), _whole((1, d_model)), _whole(wg.shape)],
        out_specs=[_rows(d_model), pl.BlockSpec((d_model, tm), lambda i: (0, i))] + [_rows(c)] * 3
                  + [_rows(MEM_WIDTH), _rows(nz)],
        scratch_shapes=[pltpu.VMEM((tm, n), F32)],
        compiler_params=_params(1, 60),
    )(x, g, wg)


HALO = 8


def _halo_before(width, tm=ROW_TILE):
    return pl.BlockSpec((HALO, width), lambda i: (jnp.maximum(i * (tm // HALO) - 1, 0), 0))


def _halo_after(width, n_rows, tm=ROW_TILE):
    return pl.BlockSpec((HALO, width), lambda i: (jnp.minimum((i + 1) * (tm // HALO), n_rows // HALO - 1), 0))


def _conv_taps(cg_ref, u_ref, cgh_ref, uh_ref, i):
    a = cg_ref[...] * u_ref[...]
    ah = jnp.where(i > 0, cgh_ref[...] * uh_ref[...], 0.0)
    row = lax.broadcasted_iota(jnp.int32, a.shape, 0)
    a1 = jnp.where(row == 0, ah[HALO - 1:HALO], pltpu.roll(a, 1, 0))
    a2 = jnp.where(row == 0, ah[HALO - 2:HALO - 1], jnp.where(row == 1, ah[HALO - 1:HALO], pltpu.roll(a, 2, 0)))
    return a, a1, a2


def _post_conv_loss(bg, cg, u, qm, kv, z, h1, w_out, cw, gf, tgt):
    s, d = h1.shape
    c = CONV_WIDTH
    nb = c + MEM_WIDTH
    tm = ROW_TILE

    def body(bg_ref, cg_ref, u_ref, cgh_ref, uh_ref, qm_ref, kv_ref, z_ref, h_ref, w_ref, cw_ref, gf_ref, t_ref,
             y_ref, yt_ref, mo_ref, dh_ref, dhb_ref, loss_ref, dgf_ref):
        i = pl.program_id(0)
        a, a1, a2 = _conv_taps(cg_ref, u_ref, cgh_ref, uh_ref, i)
        conv = cw_ref[0:1, :] * a2 + cw_ref[1:2, :] * a1 + cw_ref[2:3, :] * a
        mix = bg_ref[...] * conv
        _mem_attn_into(qm_ref[...], kv_ref, mo_ref)
        sz, _ = _silu_parts(z_ref[...])
        y_ref[:, :c] = (mix * sz[:, :c]).astype(BF16)
        y_ref[:, c:] = (mo_ref[...] * sz[:, c:]).astype(BF16)
        y = y_ref[...]
        yt_ref[...] = y.T
        h2 = h_ref[...] + _dot(y, w_ref[...])
        r = lax.rsqrt(jnp.mean(h2 * h2, axis=-1, keepdims=True) + EPS)
        nh = h2 * r
        gfv = gf_ref[...]
        diff = nh * gfv - t_ref[...]
        dout = diff * (1.0 / d)
        dn = dout * gfv
        dh2 = r * dn - h2 * ((r * r * r) * jnp.mean(dn * h2, axis=-1, keepdims=True))
        dh_ref[...] = dh2
        dhb_ref[...] = dh2.astype(BF16)

        @pl.when(i == 0)
        def _():
            loss_ref[...] = jnp.zeros_like(loss_ref)
            dgf_ref[...] = jnp.zeros_like(dgf_ref)

        loss_ref[...] += 0.5 * jnp.sum(jnp.mean(diff * diff, axis=-1, keepdims=True))
        dgf_ref[...] += jnp.sum(dout * nh, axis=0, keepdims=True)

    return pl.pallas_call(
        body, name="post_conv_loss", grid=(s // tm,),
        out_shape=[_sds((s, nb), BF16), _sds((nb, s), BF16), _sds((s, MEM_WIDTH), F32), _sds((s, d), F32),
                   _sds((s, d), BF16), _sds((8, LANES), F32), _sds((1, d), F32)],
        in_specs=[_rows(c)] * 3 + [_halo_before(c)] * 2 + [_rows(MEM_WIDTH), _whole(kv.shape), _rows(nb), _rows(d),
                  _whole(w_out.shape), _whole(cw.shape), _whole((1, d)), _rows(d)],
        out_specs=[_rows(nb), pl.BlockSpec((nb, tm), lambda i: (0, i)), _rows(MEM_WIDTH), _rows(d), _rows(d),
                   _whole((8, LANES)), _whole((1, d))],
        compiler_params=_params(1, 48),
    )(bg, cg, u, cg, u, qm, kv, z, h1, w_out, cw, gf, tgt)


def _bwd_post_conv(dhb, w_out, bg, cg, u, z, qm, kv, mo, cw):
    s = dhb.shape[0]
    c = CONV_WIDTH
    nb = c + MEM_WIDTH

    def body(dh_ref, w_ref, bg_ref, cg_ref, u_ref, cgh_ref, uh_ref, z_ref, qm_ref, kv_ref, mo_ref, cw_ref,
             dz_ref, dbg_ref, dc_ref, dqm_ref, dkv_ref):
        i = pl.program_id(0)

        @pl.when(i == 0)
        def _():
            dkv_ref[...] = jnp.zeros_like(dkv_ref)

        dy = _dot_nt(dh_ref[...], w_ref[...])
        sz, dsz = _silu_parts(z_ref[...])
        a, a1, a2 = _conv_taps(cg_ref, u_ref, cgh_ref, uh_ref, i)
        conv = cw_ref[0:1, :] * a2 + cw_ref[1:2, :] * a1 + cw_ref[2:3, :] * a
        bgv = bg_ref[...]
        dz_ref[:, :c] = (dy[:, :c] * (bgv * conv) * dsz[:, :c]).astype(BF16)
        dz_ref[:, c:] = (dy[:, c:] * mo_ref[...] * dsz[:, c:]).astype(BF16)
        dbr = dy * sz
        dmix = dbr[:, :c]
        dbg_ref[...] = (dmix * conv).astype(BF16)
        dc_ref[...] = dmix * bgv
        _mem_attn_bwd(qm_ref[...], kv_ref, dbr[:, c:], dqm_ref, dkv_ref)

    return pl.pallas_call(
        body, name="bwd_post_conv", grid=(s // ROW_TILE,),
        out_shape=[_sds((s, nb), BF16), _sds((s, c), BF16), _sds((s, c), F32), _sds((s, MEM_WIDTH), BF16),
                   _sds(kv.shape, F32)],
        in_specs=[_rows(D_MODEL), _whole(w_out.shape)] + [_rows(c)] * 3 + [_halo_before(c)] * 2
                 + [_rows(nb), _rows(MEM_WIDTH), _whole(kv.shape), _rows(MEM_WIDTH), _whole(cw.shape)],
        out_specs=[_rows(nb), _rows(c), _rows(c), _rows(MEM_WIDTH), _whole(kv.shape)],
        compiler_params=_params(1, 48),
    )(dhb, w_out, bg, cg, u, cg, u, z, qm, kv, mo, cw)


def _bwd_conv(dconv, cg, u, cw):
    s, c = dconv.shape
    tm = ROW_TILE
    last = s // tm - 1

    def body(dc_ref, dcn_ref, cg_ref, u_ref, cgh_ref, uh_ref, cw_ref, dcg_ref, du_ref, dcw_ref):
        i = pl.program_id(0)

        @pl.when(i == 0)
        def _():
            dcw_ref[...] = jnp.zeros_like(dcw_ref)

        dc = dc_ref[...]
        dcn = jnp.where(i < last, dcn_ref[...], 0.0)
        row = lax.broadcasted_iota(jnp.int32, dc.shape, 0)
        d1 = jnp.where(row == tm - 1, dcn[0:1], pltpu.roll(dc, tm - 1, 0))
        d2 = jnp.where(row == tm - 1, dcn[1:2], jnp.where(row == tm - 2, dcn[0:1], pltpu.roll(dc, tm - 2, 0)))
        da = cw_ref[2:3, :] * dc + cw_ref[1:2, :] * d1 + cw_ref[0:1, :] * d2
        a, a1, a2 = _conv_taps(cg_ref, u_ref, cgh_ref, uh_ref, i)
        dcg_ref[...] = (da * u_ref[...]).astype(BF16)
        du_ref[...] = (da * cg_ref[...]).astype(BF16)
        dcw_ref[0:1, :] += jnp.sum(dc * a2, axis=0, keepdims=True)
        dcw_ref[1:2, :] += jnp.sum(dc * a1, axis=0, keepdims=True)
        dcw_ref[2:3, :] += jnp.sum(dc * a, axis=0, keepdims=True)

    return pl.pallas_call(
        body, name="bwd_conv", grid=(s // tm,),
        out_shape=[_sds((s, c), BF16), _sds((s, c), BF16), _sds((8, c), F32)],
        in_specs=[_rows(c), _halo_after(c, s), _rows(c), _rows(c), _halo_before(c), _halo_before(c), _whole(cw.shape)],
        out_specs=[_rows(c), _rows(c), _whole((8, c))], compiler_params=_params(1, 40),
    )(dconv, dconv, cg, u, cg, u, cw)


def _dgrad_norm(pieces, wg, h, g, dres, name):
    s, d_model = h.shape
    c = wg.shape[2]
    n = N_DEV * c
    tm = ROW_TILE
    widths = [p.shape[1] // d for p, d in pieces]
    assert sum(widths) == n
    n_p = len(pieces)

    def body(*refs):
        p_refs = refs[:n_p]
        w_ref, h_ref, g_ref, dr_ref, dh_ref, dg_ref, dpd_ref, dp, scr = refs[n_p:]

        @pl.when(pl.program_id(0) == 0)
        def _():
            dg_ref[...] = jnp.zeros_like(dg_ref)

        off = 0
        for p_ref, (_, d), wd in zip(p_refs, pieces, widths):
            if d == 1:
                dp[:, off:off + wd] = p_ref[...]
            else:
                dp[:, off:off + wd] = _from_view(scr, p_ref, d).astype(BF16)
            off += wd
        dhn = jnp.zeros((tm, d_model), F32)
        for j in range(N_DEV):
            dpj = dp[:, j * c:(j + 1) * c]
            dpd_ref[j] = dpj
            dhn += _dot_nt(dpj, w_ref[j])
        hb = h_ref[...]
        r = lax.rsqrt(jnp.mean(hb * hb, axis=-1, keepdims=True) + EPS)
        dg_ref[...] += jnp.sum(dhn * (hb * r), axis=0, keepdims=True)
        dn = dhn * g_ref[...]
        dh_ref[...] = dr_ref[...] + r * dn - hb * ((r * r * r) * jnp.mean(dn * hb, axis=-1, keepdims=True))

    p_specs = [_view_rows(wd, d) for (_, d), wd in zip(pieces, widths)]
    return pl.pallas_call(
        body, name=name, grid=(s // tm,),
        out_shape=[_sds((s, d_model), F32), _sds((1, d_model), F32), _sds((N_DEV, s, c), BF16)],
        in_specs=p_specs + [_whole(wg.shape), _rows(d_model), _whole((1, d_model)), _rows(d_model)],
        out_specs=[_rows(d_model), _whole((1, d_model)), pl.BlockSpec((N_DEV, tm, c), lambda i: (0, i, 0))],
        scratch_shapes=[pltpu.VMEM((tm, n), BF16), _view_scratch(GROUP_WIDTH)],
        compiler_params=_params(1, 60),
    )(*[p for p, _ in pieces], wg, h, g, dres)


def _wgrad_shards(a_t, b_dm, name):
    m, s = a_t.shape
    c = b_dm.shape[2]

    def body(a_ref, b_ref, o_ref):
        o_ref[...] = _dot(a_ref[...], b_ref[...]).astype(BF16)

    return pl.pallas_call(
        body, name=name, grid=(N_DEV,), out_shape=_sds((N_DEV, m, c), BF16),
        in_specs=[_whole(a_t.shape), pl.BlockSpec((None, s, c), lambda j: (j, 0, 0))],
        out_specs=pl.BlockSpec((None, m, c), lambda j: (j, 0, 0)), compiler_params=_params(1, 40),
    )(a_t, b_dm)


def _wgrad_cols(a_t, b, c, name):
    m, s = a_t.shape

    def body(a_ref, b_ref, o_ref):
        o_ref[...] = _dot(a_ref[...], b_ref[...]).astype(BF16)

    return pl.pallas_call(
        body, name=name, grid=(N_DEV,), out_shape=_sds((N_DEV, m, c), BF16),
        in_specs=[_whole(a_t.shape), pl.BlockSpec((s, c), lambda j: (0, j))],
        out_specs=pl.BlockSpec((None, m, c), lambda j: (j, 0, 0)), compiler_params=_params(1, 40),
    )(a_t, b)


def _wgrad_rows(a_t, b, name):
    m, s = a_t.shape
    n = b.shape[1]
    mr = m // N_DEV

    def body(a_ref, b_ref, o_ref):
        o_ref[...] = _dot(a_ref[...], b_ref[...]).astype(BF16)

    return pl.pallas_call(
        body, name=name, grid=(N_DEV,), out_shape=_sds((N_DEV, mr, n), BF16),
        in_specs=[pl.BlockSpec((mr, s), lambda j: (j, 0)), _whole(b.shape)],
        out_specs=pl.BlockSpec((None, mr, n), lambda j: (j, 0, 0)), compiler_params=_params(1, 40),
    )(a_t, b)


def _memkv_bwd(dkv, w, mem, g):
    n_layers = w.shape[0]

    def body(dkv_ref, w_ref, mem_ref, g_ref, dw_ref, dg_ref):
        mb = mem_ref[...]
        r = lax.rsqrt(jnp.mean(mb * mb, axis=-1, keepdims=True) + EPS)
        nm = mb * r
        mn = (nm * g_ref[...]).astype(BF16)
        dkvb = dkv_ref[...].astype(BF16)
        dw_ref[...] = _dot_tn(mn, dkvb).astype(BF16)
        dmn = _dot_nt(dkvb, w_ref[...])
        dg_ref[...] = jnp.sum(dmn * nm, axis=0, keepdims=True)

    lay = lambda *shape: pl.BlockSpec((None,) + shape, lambda l: (l, 0, 0))
    return pl.pallas_call(
        body, name="memkv_bwd", grid=(n_layers,),
        out_shape=[_sds((n_layers, D_MODEL, 2 * MEM_WIDTH), BF16), _sds((n_layers, 1, D_MODEL), F32)],
        in_specs=[lay(N_MEM, 2 * MEM_WIDTH), lay(D_MODEL, 2 * MEM_WIDTH), _whole(mem.shape), lay(1, D_MODEL)],
        out_specs=[lay(D_MODEL, 2 * MEM_WIDTH), lay(1, D_MODEL)], compiler_params=_params(1, 32),
    )(dkv, w, mem, g.reshape(n_layers, 1, D_MODEL))


def _bwd_post_attn(dhb, wg_out, z, os_, ls_, qm, kv, mo, head_ones):
    s = dhb.shape[0]
    gw = GROUP_WIDTH
    nb = gw + MEM_WIDTH
    c = wg_out.shape[2]
    tm = ROW_TILE

    def body(dh_ref, w_ref, z_ref, o0, o1, o2, l0, l1, l2, qm_ref, kv_ref, mo_ref, bd_ref,
             dz_ref, do0, do1, do2, dl0, dl1, dl2, dqm_ref, dkv_ref, s0, s1):
        @pl.when(pl.program_id(0) == 0)
        def _():
            dkv_ref[...] = jnp.zeros_like(dkv_ref)

        dy = jnp.zeros((tm, nb), F32)
        for j in range(N_DEV):
            dy += _dot_nt(dh_ref[:, j * c:(j + 1) * c], w_ref[j])
        ov, lv = [], []
        for o_ref, l_ref, d in zip((o0, o1, o2), (l0, l1, l2), DILATIONS):
            ov.append(_from_view(s0, o_ref, d))
            lv.append(_from_view(s1, l_ref, d))
        ws, mix = _mix_groups(ov, lv)
        sz, dsz = _silu_parts(z_ref[...])
        dz_ref[:, :gw] = (dy[:, :gw] * mix * dsz[:, :gw]).astype(BF16)
        dz_ref[:, gw:] = (dy[:, gw:] * mo_ref[...] * dsz[:, gw:]).astype(BF16)
        dbr = dy * sz
        dmix = dbr[:, :gw]
        t = dmix * mix
        th = t.astype(BF16)
        tl = (t - th.astype(F32)).astype(BF16)
        rs = _dot(th, bd_ref[...]) + _dot(tl, bd_ref[...])
        for wg_, do_ref, dl_ref, d in zip(ws, (do0, do1, do2), (dl0, dl1, dl2), DILATIONS):
            _to_view(s0, wg_ * dmix, do_ref, d)
            _to_view(s1, wg_ * rs, dl_ref, d)
        _mem_attn_bwd(qm_ref[...], kv_ref, dbr[:, gw:], dqm_ref, dkv_ref)

    vspecs = [_view_rows(gw, d) for d in DILATIONS]
    return pl.pallas_call(
        body, name="bwd_post_attn", grid=(s // tm,),
        out_shape=[_sds((s, nb), BF16)] + [_sds((s // d, d * gw), BF16) for d in DILATIONS]
                  + [_sds((s // d, d * gw), F32) for d in DILATIONS] + [_sds((s, MEM_WIDTH), BF16), _sds(kv.shape, F32)],
        in_specs=[_rows(D_MODEL), _whole(wg_out.shape), _rows(nb)] + vspecs * 2
                 + [_rows(MEM_WIDTH), _whole(kv.shape), _rows(MEM_WIDTH), _whole(head_ones.shape)],
        out_specs=[_rows(nb)] + vspecs * 2 + [_rows(MEM_WIDTH), _whole(kv.shape)],
        scratch_shapes=[_view_scratch(gw), _view_scratch(gw)],
        compiler_params=_params(1, 48),
    )(dhb, wg_out, z, *os_, *ls_, qm, kv, mo, head_ones)


def _attn_bwd(q, k, v, lse, do, dl, tabs, d):
    ln, dw = q.shape
    w = dw // d
    nb = ln // BLOCK
    nblk = d * nb
    reps = w // LANES
    cur, prev = _stream_maps(d, nb)
    qmap = lambda n: cur(jnp.minimum(n, nblk - 1))
    pmap = lambda n: prev(jnp.minimum(n, nblk - 1))
    omap = lambda n: cur(jnp.maximum(n - 1, 0))

    def body(q_ref, kp_ref, kc_ref, vp_ref, vc_ref, l_ref, do_ref, dl_ref, cq, saq, sbq, ck, sak, sbk,
             dq_ref, dk_ref, dv_ref, acck, accv, dqs):
        n = pl.program_id(0)

        @pl.when(n == 0)
        def _():
            acck[...] = jnp.zeros_like(acck)
            accv[...] = jnp.zeros_like(accv)

        @pl.when(n < nblk)
        def _():
            valid = _band_mask((n % nb) == 0)
            kk = jnp.concatenate([kp_ref[...], kc_ref[...]], axis=0)
            vv = jnp.concatenate([vp_ref[...], vc_ref[...]], axis=0)
            for h in range(HEADS_PER_GROUP):
                sl = slice(h * HEAD_DIM, (h + 1) * HEAD_DIM)
                col = slice(h * HEAD_DIM, h * HEAD_DIM + 1)
                qh = q_ref[:, sl]
                dob = do_ref[:, sl]
                sc = jnp.where(valid, _dot_nt(qh, kk[:, sl]), NEG)
                p = jnp.exp(sc - l_ref[:, col])
                dp = _dot_nt(dob, vv[:, sl])
                ds = (p * (dp - dl_ref[:, col])).astype(BF16)
                dqs[:, sl] = _dot(ds, kk[:, sl]) * SCALE
                acck[:, sl] += _dot_tn(ds, qh)
                accv[:, sl] += _dot_tn(p.astype(BF16), dob)
            tq = [jnp.tile(r[...], (1, reps)) for r in (cq, saq, sbq)]
            dq_ref[...] = _rope_bwd(dqs[...], *tq).astype(BF16)

        tk = [jnp.tile(r[...], (1, reps)) for r in (ck, sak, sbk)]
        dk_ref[...] = _rope_bwd(acck[0:BLOCK, :], *tk).astype(BF16)
        dv_ref[...] = accv[0:BLOCK, :].astype(BF16)
        acck[0:BLOCK, :] = acck[BLOCK:, :]
        accv[0:BLOCK, :] = accv[BLOCK:, :]
        acck[BLOCK:, :] = jnp.zeros((BLOCK, w), F32)
        accv[BLOCK:, :] = jnp.zeros((BLOCK, w), F32)

    blk = lambda f: pl.BlockSpec((BLOCK, w), f)
    tblk = lambda f: pl.BlockSpec((BLOCK, LANES), f)
    return pl.pallas_call(
        body, name=f"attn_bwd_d{d}", grid=(nblk + 1,),
        out_shape=[_sds((ln, dw), BF16)] * 3,
        in_specs=[blk(qmap), blk(pmap), blk(qmap), blk(pmap), blk(qmap), blk(qmap), blk(qmap), blk(qmap)]
                 + [tblk(qmap)] * 3 + [tblk(omap)] * 3,
        out_specs=[blk(qmap), blk(omap), blk(omap)],
        scratch_shapes=[pltpu.VMEM((2 * BLOCK, w), F32), pltpu.VMEM((2 * BLOCK, w), F32), pltpu.VMEM((BLOCK, w), F32)],
        compiler_params=_params(1, 32),
    )(q, k, k, v, v, lse, do, dl, *tabs, *tabs)


def _position():
    return lax.axis_index("x"), lax.axis_index("y"), lax.axis_index("c")


def _all_gather(shards, name):
    n_a = len(shards)

    def body(*refs):
        x_refs, out_refs = refs[:n_a], refs[n_a:2 * n_a]
        send_sems, recv_sems, local_sems = refs[2 * n_a:]
        x, y, c = _position()
        me, sibling = (x, y, c), (x, y, 1 - c)
        chips = [(1 - x, y), (x, 1 - y), (1 - x, 1 - y)]

        def rows(a, px, py, pc):
            return out_refs[a].at[4 * px + 2 * py + pc]

        def copy(a, k, block, to, own=False):
            return pltpu.make_async_remote_copy(
                src_ref=x_refs[a] if own else rows(a, *block), dst_ref=rows(a, *block),
                send_sem=send_sems.at[a, k], recv_sem=recv_sems.at[a, k], device_id=to, device_id_type=MESH)

        mine = [pltpu.make_async_copy(x_refs[a], rows(a, *me), local_sems.at[a]) for a in range(n_a)]
        for cp in mine:
            cp.start()
        first = []
        for j, chip in enumerate(chips):
            first += [copy(a, 1 + j, me, (*chip, c), own=True) for a in range(n_a)]
        first += [copy(a, 0, me, sibling, own=True) for a in range(n_a)]
        for cp in first:
            cp.start()
        passed = []
        for j, chip in enumerate(chips):
            for a in range(n_a):
                copy(a, 1 + j, (*chip, c), me).wait_recv()
                fwd = copy(a, 4 + j, (*chip, c), sibling)
                fwd.start()
                passed.append(fwd)
        for a in range(n_a):
            copy(a, 0, sibling, me).wait_recv()
        for j, chip in enumerate(chips):
            for a in range(n_a):
                copy(a, 4 + j, (*chip, 1 - c), me).wait_recv()
        for cp in first + passed:
            cp.wait_send()
        for cp in mine:
            cp.wait()

    return pl.pallas_call(
        body, name=name, out_shape=[_sds((N_DEV,) + t.shape, t.dtype) for t in shards],
        in_specs=[ANY] * n_a, out_specs=[ANY] * n_a,
        scratch_shapes=[pltpu.SemaphoreType.DMA((n_a, 7)), pltpu.SemaphoreType.DMA((n_a, 7)),
                        pltpu.SemaphoreType.DMA((n_a,))],
    )(*shards)


def _rs_to_sibling(gs):
    n_a = len(gs)

    def body(*refs):
        g_refs, recv_refs = refs[:n_a], refs[n_a:2 * n_a]
        send_sems, recv_sems = refs[2 * n_a:]
        x, y, c = _position()
        copies = []
        for k in range(4):
            for a in range(n_a):
                copies.append(pltpu.make_async_remote_copy(
                    src_ref=g_refs[a].at[2 * k + (1 - c)], dst_ref=recv_refs[a].at[k],
                    send_sem=send_sems.at[a, k], recv_sem=recv_sems.at[a, k],
                    device_id=(x, y, 1 - c), device_id_type=MESH))
        for cp in copies:
            cp.start()
        for cp in copies:
            cp.wait()

    return pl.pallas_call(
        body, name="rs_to_sibling", out_shape=[_sds((4,) + g.shape[1:], g.dtype) for g in gs],
        in_specs=[ANY] * n_a, out_specs=[ANY] * n_a,
        scratch_shapes=[pltpu.SemaphoreType.DMA((n_a, 4)), pltpu.SemaphoreType.DMA((n_a, 4))],
    )(*gs)


def _rs_to_chips(pbs):
    n_a = len(pbs)

    def body(*refs):
        p_refs, recv_refs = refs[:n_a], refs[n_a:2 * n_a]
        send_sems, recv_sems = refs[2 * n_a:]
        x, y, c = _position()
        chips = [(1 - x, y), (x, 1 - y), (1 - x, 1 - y)]
        copies = []
        for j, (px, py) in enumerate(chips):
            for a in range(n_a):
                copies.append(pltpu.make_async_remote_copy(
                    src_ref=p_refs[a].at[2 * px + py], dst_ref=recv_refs[a].at[j],
                    send_sem=send_sems.at[a, j], recv_sem=recv_sems.at[a, j],
                    device_id=(px, py, c), device_id_type=MESH))
        for cp in copies:
            cp.start()
        for cp in copies:
            cp.wait()

    return pl.pallas_call(
        body, name="rs_to_chips", out_shape=[_sds((3,) + p.shape[1:], p.dtype) for p in pbs],
        in_specs=[ANY] * n_a, out_specs=[ANY] * n_a,
        scratch_shapes=[pltpu.SemaphoreType.DMA((n_a, 3)), pltpu.SemaphoreType.DMA((n_a, 3))],
    )(*pbs)


def _row_tile(r):
    return ROW_TILE if r % ROW_TILE == 0 else r


def _rs_add_sibling(gp, recv, c_arr, name):
    _, r, l = gp.shape
    tr = _row_tile(r)

    def body(c_ref, g_ref, r_ref, pf_ref, pb_ref):
        sm = g_ref[...].astype(F32) + r_ref[...].astype(F32)
        pf_ref[...] = sm
        pb_ref[...] = sm.astype(BF16)

    spec = pl.BlockSpec((None, tr, l), lambda k, i, c: (k, i, 0))
    return pl.pallas_call(
        body, name=name,
        grid_spec=pltpu.PrefetchScalarGridSpec(
            num_scalar_prefetch=1, grid=(4, r // tr),
            in_specs=[pl.BlockSpec((None, tr, l), lambda k, i, c: (2 * k + c[0], i, 0)), spec],
            out_specs=[spec, spec]),
        out_shape=[_sds((4, r, l), F32), _sds((4, r, l), BF16)], compiler_params=_params(2, 32),
    )(c_arr, gp, recv)


def _adam_update(w, gv, m, v):
    nm = ADAM_B1 * m + (1.0 - ADAM_B1) * gv
    nv = ADAM_B2 * v + (1.0 - ADAM_B2) * (gv * gv)
    m_hat = nm / (1.0 - ADAM_B1 ** ADAM_STEP)
    v_hat = nv / (1.0 - ADAM_B2 ** ADAM_STEP)
    return -ADAM_LR * (m_hat / (jnp.sqrt(v_hat) + ADAM_EPS) + ADAM_WD * w), nm, nv


def _rs_finish_adamw(pf, recv, k_arr, w, m, v, name):
    _, r, l = pf.shape
    tr = _row_tile(r)

    def body(k_ref, p_ref, r_ref, w_ref, m_ref, v_ref, g_ref, d_ref, nm_ref, nv_ref):
        gv = ((p_ref[...] + r_ref[0].astype(F32)) + r_ref[1].astype(F32)) + r_ref[2].astype(F32)
        g_ref[...] = gv
        d_ref[...], nm_ref[...], nv_ref[...] = _adam_update(w_ref[...], gv, m_ref[...], v_ref[...])

    spec = pl.BlockSpec((tr, l), lambda i, k: (i, 0))
    return pl.pallas_call(
        body, name=name,
        grid_spec=pltpu.PrefetchScalarGridSpec(
            num_scalar_prefetch=1, grid=(r // tr,),
            in_specs=[pl.BlockSpec((None, tr, l), lambda i, k: (k[0], i, 0)),
                      pl.BlockSpec((3, tr, l), lambda i, k: (0, i, 0)), spec, spec, spec],
            out_specs=[spec] * 4),
        out_shape=[_sds((r, l), F32)] * 4, compiler_params=_params(1, 32),
    )(k_arr, pf, recv, w, m, v)


def _sum_devices(g):
    def body(g_ref, o_ref):
        acc = g_ref[0]
        for j in range(1, N_DEV):
            acc = acc + g_ref[j]
        o_ref[...] = acc

    return pl.pallas_call(body, name="sum_devices", out_shape=_sds(g.shape[1:], F32))(g)


def _adamw(w, g, m, v, name):
    shape = w.shape
    w2, g2, m2, v2 = [t.reshape((-1, shape[-1])) for t in (w, g, m, v)]

    def body(w_ref, g_ref, m_ref, v_ref, d_ref, nm_ref, nv_ref):
        d_ref[...], nm_ref[...], nv_ref[...] = _adam_update(w_ref[...], g_ref[...], m_ref[...], v_ref[...])

    outs = pl.pallas_call(body, name=name, out_shape=[_sds(w2.shape, F32)] * 3)(w2, g2, m2, v2)
    return tuple(t.reshape(shape) for t in outs)


def _local_step(x, mem, pos, tgt, norm_g, mem_norm_g, final_g, cw, wg_in0, wg_out0, wg_in1, w_out1, w_kv):
    tabs = _rope_tables(pos)
    g0, g1 = norm_g[0:1], norm_g[1:2]
    kv = _memkv_fwd(mem, mem_norm_g, w_kv)

    hn0, hn0_t, qs, ks, vs, tabs_v, qm0, z0 = _inproj_attn(x, g0, wg_in0, tabs)
    os_, ls_ = [], []
    for j, d in enumerate(DILATIONS):
        o, l = _attn_fwd(qs[j], ks[j], vs[j], d)
        os_.append(o)
        ls_.append(l)
    y0, y0_t, mo0, h1 = _post_attn(os_, ls_, qm0, kv[0], z0, x, wg_out0)

    hn1, hn1_t, bg, cg, u, qm1, z1 = _inproj_conv(h1, g1, wg_in1)
    y1, y1_t, mo1, dh2, dh2b, loss_acc, d_final_g = _post_conv_loss(
        bg, cg, u, qm1, kv[1], z1, h1, w_out1, cw, final_g.reshape(1, -1), tgt)

    d_w_out1 = _wgrad_rows(y1_t, dh2b, "wgrad_out1")
    dz1, dbg, dconv, dqm1, dkv1 = _bwd_post_conv(dh2b, w_out1, bg, cg, u, z1, qm1, kv[1], mo1, cw)
    dcg, du, dcw = _bwd_conv(dconv, cg, u, cw)
    dh1, dg1, dproj1 = _dgrad_norm([(dbg, 1), (dcg, 1), (du, 1), (dqm1, 1), (dz1, 1)], wg_in1, h1, g1, dh2,
                                   "dgrad_norm_conv")
    d_w_in1 = _wgrad_shards(hn1_t, dproj1, "wgrad_in1")

    dh1b = dh1.astype(BF16)
    d_w_out0 = _wgrad_cols(y0_t, dh1b, wg_out0.shape[2], "wgrad_out0")
    gw = GROUP_WIDTH
    ones = (jnp.arange(gw)[:, None] // HEAD_DIM == jnp.arange(gw)[None, :] // HEAD_DIM).astype(BF16)
    res = _bwd_post_attn(dh1b, wg_out0, z0, os_, ls_, qm0, kv[0], mo0, ones)
    dz0, dos, dls, dqm0, dkv0 = res[0], res[1:4], res[4:7], res[7], res[8]
    dqs, dks, dvs = [], [], []
    for j, d in enumerate(DILATIONS):
        dq, dk, dv = _attn_bwd(qs[j], ks[j], vs[j], ls_[j], dos[j], dls[j], tabs_v[j], d)
        dqs.append((dq, d))
        dks.append((dk, d))
        dvs.append((dv, d))
    dx, dg0, dproj0 = _dgrad_norm(dqs + dks + dvs + [(dqm0, 1), (dz0, 1)], wg_in0, x, g0, dh1, "dgrad_norm_attn")
    d_w_in0 = _wgrad_shards(hn0_t, dproj0, "wgrad_in0")

    d_w_kv, d_mem_g = _memkv_bwd(jnp.stack([dkv0, dkv1]), w_kv, mem, mem_norm_g)
    small = jnp.concatenate([dg0, dg1, d_mem_g.reshape(2, -1), d_final_g, dcw[0:3]], axis=0)
    n_kv = d_w_kv.shape[1] // N_DEV
    grads = [d_w_in0, d_w_out0, d_w_in1, d_w_out1,
             d_w_kv[0].reshape(N_DEV, n_kv, -1), d_w_kv[1].reshape(N_DEV, n_kv, -1)]
    return loss_acc[0, 0], dx, grads, small


def kernel(x, mem, positions, norm_g, mem_norm_g, w_mem_kv, attn_w_in, attn_w_out, conv_w_in, conv_w, conv_w_out, final_g, loss_target, m_norm_g, m_mem_norm_g, m_w_mem_kv, m_attn_w_in, m_attn_w_out, m_conv_w_in, m_conv_w, m_conv_w_out, m_final_g, v_norm_g, v_mem_norm_g, v_w_mem_kv, v_attn_w_in, v_attn_w_out, v_conv_w_in, v_conv_w, v_conv_w_out, v_final_g):
    px, py, pc = _position()
    me = 4 * px + 2 * py + pc

    shards = [attn_w_in[0], attn_w_out[0], conv_w_in[0], conv_w_out[0], w_mem_kv[0], w_mem_kv[1]]
    gathered = _all_gather([t.astype(BF16) for t in shards] + [jnp.pad(conv_w[0], ((0, 5), (0, 0)))], "gather_weights")
    wg_in0, wg_out0, wg_in1, wg_out1, wg_kv0, wg_kv1, cw_all = gathered
    w_out1 = wg_out1.reshape(-1, wg_out1.shape[2])
    w_kv = jnp.stack([wg_kv0.reshape(-1, wg_kv0.shape[2]), wg_kv1.reshape(-1, wg_kv1.shape[2])])
    cw = cw_all[:, 0:3].transpose(1, 0, 2).reshape(3, -1)

    loss_part, dx, grads, small_part = _local_step(
        x[0], mem[0], positions[0], loss_target[0], norm_g, mem_norm_g, final_g, cw,
        wg_in0, wg_out0, wg_in1, w_out1, w_kv)
    loss = lax.psum(loss_part, ("x", "y", "c"))

    c_arr = jnp.reshape(pc, (1,)).astype(jnp.int32)
    k_arr = jnp.reshape(2 * px + py, (1,)).astype(jnp.int32)
    names = ["attn_w_in", "attn_w_out", "conv_w_in", "conv_w_out", "w_mem_kv0", "w_mem_kv1"]
    from_sibling = _rs_to_sibling(grads)
    chip_parts = [_rs_add_sibling(g, r, c_arr, "rs_add_sibling_" + n) for g, r, n in zip(grads, from_sibling, names)]
    from_chips = _rs_to_chips([pb for _, pb in chip_parts])
    moments = [(m_attn_w_in[0], v_attn_w_in[0]), (m_attn_w_out[0], v_attn_w_out[0]), (m_conv_w_in[0], v_conv_w_in[0]),
               (m_conv_w_out[0], v_conv_w_out[0]), (m_w_mem_kv[0], v_w_mem_kv[0]), (m_w_mem_kv[1], v_w_mem_kv[1])]
    big = {}
    for n, w, (pf, _), r, (m, v) in zip(names, shards, chip_parts, from_chips, moments):
        big[n] = _rs_finish_adamw(pf, r, k_arr, w, m, v, "rs_finish_adamw_" + n)
    for n in ("attn_w_in", "attn_w_out", "conv_w_in", "conv_w_out"):
        big[n] = tuple(t[None] for t in big[n])
    big["w_mem_kv"] = tuple(jnp.stack([a, b]) for a, b in zip(big["w_mem_kv0"], big["w_mem_kv1"]))

    small = _sum_devices(_all_gather([small_part], "gather_small_grads")[0])
    g_conv_w = lax.dynamic_slice(small[5:8], (0, me * LANES), (3, LANES))[None]
    small_g = dict(norm_g=small[0:2], mem_norm_g=small[2:4], conv_w=g_conv_w, final_g=small[4])
    small_w = dict(norm_g=(norm_g, m_norm_g, v_norm_g), mem_norm_g=(mem_norm_g, m_mem_norm_g, v_mem_norm_g),
                   conv_w=(conv_w, m_conv_w, v_conv_w), final_g=(final_g, m_final_g, v_final_g))
    for n, (w, m, v) in small_w.items():
        big[n] = (small_g[n],) + _adamw(w, small_g[n], m, v, "adamw_" + n)

    order = ["norm_g", "mem_norm_g", "w_mem_kv", "attn_w_in", "attn_w_out", "conv_w_in", "conv_w", "conv_w_out", "final_g"]
    return (loss, dx[None], *[big[n][0] for n in order], *[big[n][1] for n in order],
            *[big[n][2] for n in order], *[big[n][3] for n in order])
```

```python
import functools

import jax
import jax.numpy as jnp
from jax import lax
from jax.experimental import pallas as pl
from jax.experimental.pallas import tpu as pltpu

F32 = jnp.float32
BF16 = jnp.bfloat16

N_DEV = 8
D_MODEL = 1024
HEAD_DIM = 64
ROT_DIM = HEAD_DIM // 4
ROPE_THETA = 500000.0
DILATIONS = (1, 4, 16)
HEADS_PER_GROUP = 8
GROUP_WIDTH = HEADS_PER_GROUP * HEAD_DIM
BLOCK = 128
N_MEM = 256
MEM_HEADS = 4
MEM_WIDTH = MEM_HEADS * HEAD_DIM
CONV_WIDTH = D_MODEL
EPS = 1e-6
SCALE = HEAD_DIM ** -0.5
NEG = -1e30

ADAM_LR = 0.001
ADAM_B1 = 0.9
ADAM_B2 = 0.999
ADAM_EPS = 1e-08
ADAM_WD = 0.01
ADAM_STEP = 10

ROW_TILE = 256
LANES = 128
MESH = pl.DeviceIdType.MESH
ANY = pl.BlockSpec(memory_space=pl.ANY)


def _pallas_call(body, **kw):
    call = pl.pallas_call(body, **kw)

    def run(*args):
        pinned = [pltpu.with_memory_space_constraint(a, pltpu.HBM) if jnp.issubdtype(a.dtype, jnp.floating) else a
                  for a in args]
        return call(*pinned)

    return run


def _dot(a, b):
    return lax.dot_general(a, b, (((1,), (0,)), ((), ())), preferred_element_type=F32)


def _dot_nt(a, b):
    return lax.dot_general(a, b, (((1,), (1,)), ((), ())), preferred_element_type=F32)


def _dot_tn(a, b):
    return lax.dot_general(a, b, (((0,), (0,)), ((), ())), preferred_element_type=F32)


def _params(n_grid, vmem_mb=48):
    return pltpu.CompilerParams(dimension_semantics=("arbitrary",) * n_grid, vmem_limit_bytes=vmem_mb << 20)


def _rows(width, tm=ROW_TILE):
    return pl.BlockSpec((tm, width), lambda i: (i, 0))


def _view_rows(width, d, tm=ROW_TILE):
    return pl.BlockSpec((tm // d, d * width), lambda i: (i, 0))


def _whole(shape):
    return pl.BlockSpec(shape, lambda *_: (0,) * len(shape))


def _sds(shape, dtype):
    return jax.ShapeDtypeStruct(shape, dtype)


def _silu_parts(z):
    sg = jax.nn.sigmoid(z)
    return z * sg, sg * (1.0 + z * (1.0 - sg))


def _to_view(scr, val, out_ref, d):
    tm, w = val.shape
    if d == 1:
        out_ref[...] = val.astype(out_ref.dtype)
        return
    for cb in range(w // LANES):
        scr[cb] = val[:, cb * LANES:(cb + 1) * LANES]
    for r in range(d):
        for cb in range(w // LANES):
            lo = r * w + cb * LANES
            out_ref[:, lo:lo + LANES] = scr[cb, pl.ds(r, tm // d, stride=d), :].astype(out_ref.dtype)


def _from_view(scr, in_ref, d):
    if d == 1:
        return in_ref[...].astype(F32)
    nc, tm, _ = scr.shape
    w = nc * LANES
    for r in range(d):
        for cb in range(nc):
            lo = r * w + cb * LANES
            scr[cb, pl.ds(r, tm // d, stride=d), :] = in_ref[:, lo:lo + LANES].astype(F32)
    return jnp.concatenate([scr[cb] for cb in range(nc)], axis=1)


def _view_scratch(width, tm=ROW_TILE):
    return pltpu.VMEM((width // LANES, tm, LANES), F32)


def _rope_tables(pos):
    half = ROT_DIM // 2
    inv_freq = ROPE_THETA ** (-jnp.arange(half, dtype=F32) * (2.0 / ROT_DIM))
    ang = pos.astype(F32)[:, None] * inv_freq
    cos, sin = jnp.cos(ang), jnp.sin(ang)
    s = pos.shape[0]
    z8 = jnp.zeros((s, half), F32)
    rest = HEAD_DIM - ROT_DIM
    cosf = jnp.concatenate([cos, cos, jnp.ones((s, rest), F32)], axis=1)
    sa = jnp.concatenate([-sin, z8, jnp.zeros((s, rest), F32)], axis=1)
    sb = jnp.concatenate([z8, sin, jnp.zeros((s, rest), F32)], axis=1)
    return tuple(jnp.tile(t, (1, LANES // HEAD_DIM)) for t in (cosf, sa, sb))


def _rope_fwd(t, cv, sav, sbv):
    w = t.shape[1]
    return t * cv + pltpu.roll(t, w - ROT_DIM // 2, 1) * sav + pltpu.roll(t, ROT_DIM // 2, 1) * sbv


def _rope_bwd(g, cv, sav, sbv):
    w = g.shape[1]
    return g * cv + pltpu.roll(g * sav, ROT_DIM // 2, 1) + pltpu.roll(g * sbv, w - ROT_DIM // 2, 1)


def _project(hn, wg_ref, proj_scr):
    c = wg_ref.shape[2]
    for j in range(N_DEV):
        proj_scr[:, j * c:(j + 1) * c] = _dot(hn, wg_ref[j])


def _inproj_attn(x, g, wg, tabs):
    s, d_model = x.shape
    gw = GROUP_WIDTH
    n = N_DEV * wg.shape[2]
    nz = n - 9 * gw - MEM_WIDTH
    reps = gw // LANES
    tm = ROW_TILE

    def body(x_ref, g_ref, w_ref, c_ref, sa_ref, sb_ref, hn_ref, hnt_ref, *rest):
        outs, (proj, scr, tscr) = rest[:-3], rest[-3:]
        q_refs, k_refs, v_refs, t_refs, qm_ref, z_ref = outs[0:3], outs[3:6], outs[6:9], outs[9:18], outs[18], outs[19]
        xb = x_ref[...]
        r = lax.rsqrt(jnp.mean(xb * xb, axis=-1, keepdims=True) + EPS)
        hn = ((xb * r) * g_ref[...]).astype(BF16)
        hn_ref[...] = hn
        hnt_ref[...] = hn.T
        _project(hn, w_ref, proj)
        tab = (c_ref[...], sa_ref[...], sb_ref[...])
        cv, sav, sbv = [jnp.tile(t, (1, reps)) for t in tab]
        for j, d in enumerate(DILATIONS):
            tq = _rope_fwd(proj[:, j * gw:(j + 1) * gw], cv, sav, sbv)
            _to_view(scr, tq * SCALE, q_refs[j], d)
            tk = _rope_fwd(proj[:, (3 + j) * gw:(4 + j) * gw], cv, sav, sbv)
            _to_view(scr, tk, k_refs[j], d)
            _to_view(scr, proj[:, (6 + j) * gw:(7 + j) * gw], v_refs[j], d)
            for i in range(3):
                _to_view(tscr, tab[i], t_refs[3 * j + i], d)
        qm_ref[...] = proj[:, 9 * gw:9 * gw + MEM_WIDTH].astype(BF16)
        z_ref[...] = proj[:, 9 * gw + MEM_WIDTH:]

    views = [_sds((s // d, d * gw), BF16) for d in DILATIONS]
    tviews = [_sds((s // d, d * LANES), F32) for d in DILATIONS for _ in range(3)]
    out_shape = ([_sds((s, d_model), BF16), _sds((d_model, s), BF16)] + views * 3 + tviews
                 + [_sds((s, MEM_WIDTH), BF16), _sds((s, nz), F32)])
    vspecs = [_view_rows(gw, d) for d in DILATIONS]
    tspecs = [_view_rows(LANES, d) for d in DILATIONS for _ in range(3)]
    out_specs = ([_rows(d_model), pl.BlockSpec((d_model, tm), lambda i: (0, i))] + vspecs * 3 + tspecs
                 + [_rows(MEM_WIDTH), _rows(nz)])
    res = _pallas_call(
        body, name="inproj_attn", grid=(s // tm,), out_shape=out_shape,
        in_specs=[_rows(d_model), _whole((1, d_model)), _whole(wg.shape), _rows(LANES), _rows(LANES), _rows(LANES)],
        out_specs=out_specs,
        scratch_shapes=[pltpu.VMEM((tm, n), F32), _view_scratch(gw), _view_scratch(LANES)],
        compiler_params=_params(1, 60),
    )(x, g, wg, *tabs)
    tabs_v = [res[11 + 3 * j:14 + 3 * j] for j in range(3)]
    return res[0], res[1], res[2:5], res[5:8], res[8:11], tabs_v, res[20], res[21]


def _stream_maps(d, nb):
    def cur(n):
        return (n % nb, n // nb)

    def prev(n):
        return (jnp.maximum(n % nb - 1, 0), n // nb)

    return cur, prev


def _band_mask(first):
    qi = lax.broadcasted_iota(jnp.int32, (BLOCK, 2 * BLOCK), 0)
    kj = lax.broadcasted_iota(jnp.int32, (BLOCK, 2 * BLOCK), 1)
    lo = qi + jnp.where(first, 2 * BLOCK, 0)
    return jnp.logical_or(jnp.logical_and(kj < BLOCK, kj >= lo), jnp.logical_and(kj >= BLOCK, (kj - BLOCK) <= qi))


def _attn_fwd(q, k, v, d):
    ln, dw = q.shape
    w = dw // d
    nb = ln // BLOCK
    nblk = d * nb
    cur, prev = _stream_maps(d, nb)

    def body(q_ref, kp_ref, kc_ref, vp_ref, vc_ref, o_ref, lse_ref):
        n = pl.program_id(0)
        valid = _band_mask((n % nb) == 0)
        kk = jnp.concatenate([kp_ref[...], kc_ref[...]], axis=0)
        vv = jnp.concatenate([vp_ref[...], vc_ref[...]], axis=0)
        for h in range(HEADS_PER_GROUP):
            sl = slice(h * HEAD_DIM, (h + 1) * HEAD_DIM)
            sc = jnp.where(valid, _dot_nt(q_ref[:, sl], kk[:, sl]), NEG)
            m = jnp.max(sc, axis=-1, keepdims=True)
            p = jnp.exp(sc - m)
            l = jnp.sum(p, axis=-1, keepdims=True)
            pn = p * (1.0 / l)
            o_ref[:, sl] = _dot(pn.astype(BF16), vv[:, sl])
            lse_ref[:, sl] = jnp.broadcast_to(m + jnp.log(l), (BLOCK, HEAD_DIM))

    blk = lambda f: pl.BlockSpec((BLOCK, w), f)
    return _pallas_call(
        body, name=f"attn_fwd_d{d}", grid=(nblk,),
        out_shape=[_sds((ln, dw), F32)] * 2,
        in_specs=[blk(cur), blk(prev), blk(cur), blk(prev), blk(cur)],
        out_specs=[blk(cur), blk(cur)], compiler_params=_params(1, 32),
    )(q, k, k, v, v)


def _memkv_fwd(mem, g, w):
    n_layers = w.shape[0]

    def body(mem_ref, g_ref, w_ref, kv_ref):
        mb = mem_ref[...]
        r = lax.rsqrt(jnp.mean(mb * mb, axis=-1, keepdims=True) + EPS)
        mn = ((mb * r) * g_ref[...]).astype(BF16)
        kv_ref[...] = _dot(mn, w_ref[...]).astype(BF16)

    return _pallas_call(
        body, name="memkv_fwd", grid=(n_layers,),
        out_shape=_sds((n_layers, N_MEM, 2 * MEM_WIDTH), BF16),
        in_specs=[_whole(mem.shape), pl.BlockSpec((None, 1, D_MODEL), lambda l: (l, 0, 0)),
                  pl.BlockSpec((None, D_MODEL, 2 * MEM_WIDTH), lambda l: (l, 0, 0))],
        out_specs=pl.BlockSpec((None, N_MEM, 2 * MEM_WIDTH), lambda l: (l, 0, 0)),
        compiler_params=_params(1, 32),
    )(mem, g.reshape(n_layers, 1, D_MODEL), w)


def _mix_groups(os_, ls_):
    mx = jnp.maximum(jnp.maximum(ls_[0], ls_[1]), ls_[2])
    es = [jnp.exp(t - mx) for t in ls_]
    inv = 1.0 / (es[0] + es[1] + es[2])
    ws = [e * inv for e in es]
    mix = ws[0] * os_[0] + ws[1] * os_[1] + ws[2] * os_[2]
    return ws, mix


def _mem_probs(qm, km, h):
    sl = slice(h * HEAD_DIM, (h + 1) * HEAD_DIM)
    sc = _dot_nt(qm[:, sl], km[:, sl]) * SCALE
    e = jnp.exp(sc - jnp.max(sc, axis=-1, keepdims=True))
    return e * (1.0 / jnp.sum(e, axis=-1, keepdims=True))


def _mem_attn_into(qm, kv_ref, mo_ref):
    km, vm = kv_ref[:, :MEM_WIDTH], kv_ref[:, MEM_WIDTH:]
    for h in range(MEM_HEADS):
        sl = slice(h * HEAD_DIM, (h + 1) * HEAD_DIM)
        p = _mem_probs(qm, km, h)
        mo_ref[:, sl] = _dot(p.astype(BF16), vm[:, sl])


def _mem_attn_bwd(qm, kv_ref, dmem, dqm_ref, dkv_ref):
    km, vm = kv_ref[:, :MEM_WIDTH], kv_ref[:, MEM_WIDTH:]
    dmb = dmem.astype(BF16)
    for h in range(MEM_HEADS):
        sl = slice(h * HEAD_DIM, (h + 1) * HEAD_DIM)
        slv = slice(MEM_WIDTH + h * HEAD_DIM, MEM_WIDTH + (h + 1) * HEAD_DIM)
        p = _mem_probs(qm, km, h)
        dp = _dot_nt(dmb[:, sl], vm[:, sl])
        ds = (p * (dp - jnp.sum(dp * p, axis=-1, keepdims=True)) * SCALE).astype(BF16)
        dqm_ref[:, sl] = _dot(ds, km[:, sl]).astype(BF16)
        dkv_ref[:, sl] += _dot_tn(ds, qm[:, sl])
        dkv_ref[:, slv] += _dot_tn(p.astype(BF16), dmb[:, sl])


def _post_attn(os_, ls_, qm, kv, z, x, wg_out):
    s, d_model = x.shape
    gw = GROUP_WIDTH
    nb = gw + MEM_WIDTH
    c = wg_out.shape[2]
    tm = ROW_TILE

    def body(o0, o1, o2, l0, l1, l2, qm_ref, kv_ref, z_ref, x_ref, w_ref, y_ref, yt_ref, mo_ref, h_ref, s0, s1):
        ov, lv = [], []
        for o_ref, l_ref, d in zip((o0, o1, o2), (l0, l1, l2), DILATIONS):
            ov.append(_from_view(s0, o_ref, d))
            lv.append(_from_view(s1, l_ref, d))
        _, mix = _mix_groups(ov, lv)
        _mem_attn_into(qm_ref[...], kv_ref, mo_ref)
        sz, _ = _silu_parts(z_ref[...])
        y_ref[:, :gw] = (mix * sz[:, :gw]).astype(BF16)
        y_ref[:, gw:] = (mo_ref[...] * sz[:, gw:]).astype(BF16)
        y = y_ref[...]
        yt_ref[...] = y.T
        for j in range(N_DEV):
            h_ref[:, j * c:(j + 1) * c] = x_ref[:, j * c:(j + 1) * c] + _dot(y, w_ref[j])

    vspecs = [_view_rows(gw, d) for d in DILATIONS]
    return _pallas_call(
        body, name="post_attn", grid=(s // tm,),
        out_shape=[_sds((s, nb), BF16), _sds((nb, s), BF16), _sds((s, MEM_WIDTH), F32), _sds((s, d_model), F32)],
        in_specs=vspecs * 2 + [_rows(MEM_WIDTH), _whole(kv.shape), _rows(nb), _rows(d_model), _whole(wg_out.shape)],
        out_specs=[_rows(nb), pl.BlockSpec((nb, tm), lambda i: (0, i)), _rows(MEM_WIDTH), _rows(d_model)],
        scratch_shapes=[_view_scratch(gw), _view_scratch(gw)],
        compiler_params=_params(1, 40),
    )(*os_, *ls_, qm, kv, z, x, wg_out)


def _inproj_conv(x, g, wg):
    s, d_model = x.shape
    c = CONV_WIDTH
    n = N_DEV * wg.shape[2]
    nz = n - 3 * c - MEM_WIDTH
    tm = ROW_TILE

    def body(x_ref, g_ref, w_ref, hn_ref, hnt_ref, bg_ref, cg_ref, u_ref, qm_ref, z_ref, proj):
        xb = x_ref[...]
        r = lax.rsqrt(jnp.mean(xb * xb, axis=-1, keepdims=True) + EPS)
        hn = ((xb * r) * g_ref[...]).astype(BF16)
        hn_ref[...] = hn
        hnt_ref[...] = hn.T
        _project(hn, w_ref, proj)
        bg_ref[...] = proj[:, 0:c]
        cg_ref[...] = proj[:, c:2 * c]
        u_ref[...] = proj[:, 2 * c:3 * c]
        qm_ref[...] = proj[:, 3 * c:3 * c + MEM_WIDTH].astype(BF16)
        z_ref[...] = proj[:, 3 * c + MEM_WIDTH:]

    return _pallas_call(
        body, name="inproj_conv", grid=(s // tm,),
        out_shape=[_sds((s, d_model), BF16), _sds((d_model, s), BF16)] + [_sds((s, c), F32)] * 3
                  + [_sds((s, MEM_WIDTH), BF16), _sds((s, nz), F32)],
        in_specs=[_rows(d_model), _whole((1, d_model)), _whole(wg.shape)],
        out_specs=[_rows(d_model), pl.BlockSpec((d_model, tm), lambda i: (0, i))] + [_rows(c)] * 3
                  + [_rows(MEM_WIDTH), _rows(nz)],
        scratch_shapes=[pltpu.VMEM((tm, n), F32)],
        compiler_params=_params(1, 60),
    )(x, g, wg)


HALO = 8


def _halo_before(width, tm=ROW_TILE):
    return pl.BlockSpec((HALO, width), lambda i: (jnp.maximum(i * (tm // HALO) - 1, 0), 0))


def _halo_after(width, n_rows, tm=ROW_TILE):
    return pl.BlockSpec((HALO, width), lambda i: (jnp.minimum((i + 1) * (tm // HALO), n_rows // HALO - 1), 0))


def _conv_taps(cg_ref, u_ref, cgh_ref, uh_ref, i):
    a = cg_ref[...] * u_ref[...]
    ah = jnp.where(i > 0, cgh_ref[...] * uh_ref[...], 0.0)
    row = lax.broadcasted_iota(jnp.int32, a.shape, 0)
    a1 = jnp.where(row == 0, ah[HALO - 1:HALO], pltpu.roll(a, 1, 0))
    a2 = jnp.where(row == 0, ah[HALO - 2:HALO - 1], jnp.where(row == 1, ah[HALO - 1:HALO], pltpu.roll(a, 2, 0)))
    return a, a1, a2


def _post_conv_loss(bg, cg, u, qm, kv, z, h1, w_out, cw, gf, tgt):
    s, d = h1.shape
    c = CONV_WIDTH
    nb = c + MEM_WIDTH
    tm = ROW_TILE

    def body(bg_ref, cg_ref, u_ref, cgh_ref, uh_ref, qm_ref, kv_ref, z_ref, h_ref, w_ref, cw_ref, gf_ref, t_ref,
             y_ref, yt_ref, mo_ref, dh_ref, dhb_ref, loss_ref, dgf_ref):
        i = pl.program_id(0)
        a, a1, a2 = _conv_taps(cg_ref, u_ref, cgh_ref, uh_ref, i)
        conv = cw_ref[0:1, :] * a2 + cw_ref[1:2, :] * a1 + cw_ref[2:3, :] * a
        mix = bg_ref[...] * conv
        _mem_attn_into(qm_ref[...], kv_ref, mo_ref)
        sz, _ = _silu_parts(z_ref[...])
        y_ref[:, :c] = (mix * sz[:, :c]).astype(BF16)
        y_ref[:, c:] = (mo_ref[...] * sz[:, c:]).astype(BF16)
        y = y_ref[...]
        yt_ref[...] = y.T
        h2 = h_ref[...] + _dot(y, w_ref[...])
        r = lax.rsqrt(jnp.mean(h2 * h2, axis=-1, keepdims=True) + EPS)
        nh = h2 * r
        gfv = gf_ref[...]
        diff = nh * gfv - t_ref[...]
        dout = diff * (1.0 / d)
        dn = dout * gfv
        dh2 = r * dn - h2 * ((r * r * r) * jnp.mean(dn * h2, axis=-1, keepdims=True))
        dh_ref[...] = dh2
        dhb_ref[...] = dh2.astype(BF16)

        @pl.when(i == 0)
        def _():
            loss_ref[...] = jnp.zeros_like(loss_ref)
            dgf_ref[...] = jnp.zeros_like(dgf_ref)

        loss_ref[...] += 0.5 * jnp.sum(jnp.mean(diff * diff, axis=-1, keepdims=True))
        dgf_ref[...] += jnp.sum(dout * nh, axis=0, keepdims=True)

    return _pallas_call(
        body, name="post_conv_loss", grid=(s // tm,),
        out_shape=[_sds((s, nb), BF16), _sds((nb, s), BF16), _sds((s, MEM_WIDTH), F32), _sds((s, d), F32),
                   _sds((s, d), BF16), _sds((8, LANES), F32), _sds((1, d), F32)],
        in_specs=[_rows(c)] * 3 + [_halo_before(c)] * 2 + [_rows(MEM_WIDTH), _whole(kv.shape), _rows(nb), _rows(d),
                  _whole(w_out.shape), _whole(cw.shape), _whole((1, d)), _rows(d)],
        out_specs=[_rows(nb), pl.BlockSpec((nb, tm), lambda i: (0, i)), _rows(MEM_WIDTH), _rows(d), _rows(d),
                   _whole((8, LANES)), _whole((1, d))],
        compiler_params=_params(1, 48),
    )(bg, cg, u, cg, u, qm, kv, z, h1, w_out, cw, gf, tgt)


def _bwd_post_conv(dhb, w_out, bg, cg, u, z, qm, kv, mo, cw):
    s = dhb.shape[0]
    c = CONV_WIDTH
    nb = c + MEM_WIDTH

    def body(dh_ref, w_ref, bg_ref, cg_ref, u_ref, cgh_ref, uh_ref, z_ref, qm_ref, kv_ref, mo_ref, cw_ref,
             dz_ref, dbg_ref, dc_ref, dqm_ref, dkv_ref):
        i = pl.program_id(0)

        @pl.when(i == 0)
        def _():
            dkv_ref[...] = jnp.zeros_like(dkv_ref)

        dy = _dot_nt(dh_ref[...], w_ref[...])
        sz, dsz = _silu_parts(z_ref[...])
        a, a1, a2 = _conv_taps(cg_ref, u_ref, cgh_ref, uh_ref, i)
        conv = cw_ref[0:1, :] * a2 + cw_ref[1:2, :] * a1 + cw_ref[2:3, :] * a
        bgv = bg_ref[...]
        dz_ref[:, :c] = (dy[:, :c] * (bgv * conv) * dsz[:, :c]).astype(BF16)
        dz_ref[:, c:] = (dy[:, c:] * mo_ref[...] * dsz[:, c:]).astype(BF16)
        dbr = dy * sz
        dmix = dbr[:, :c]
        dbg_ref[...] = (dmix * conv).astype(BF16)
        dc_ref[...] = dmix * bgv
        _mem_attn_bwd(qm_ref[...], kv_ref, dbr[:, c:], dqm_ref, dkv_ref)

    return _pallas_call(
        body, name="bwd_post_conv", grid=(s // ROW_TILE,),
        out_shape=[_sds((s, nb), BF16), _sds((s, c), BF16), _sds((s, c), F32), _sds((s, MEM_WIDTH), BF16),
                   _sds(kv.shape, F32)],
        in_specs=[_rows(D_MODEL), _whole(w_out.shape)] + [_rows(c)] * 3 + [_halo_before(c)] * 2
                 + [_rows(nb), _rows(MEM_WIDTH), _whole(kv.shape), _rows(MEM_WIDTH), _whole(cw.shape)],
        out_specs=[_rows(nb), _rows(c), _rows(c), _rows(MEM_WIDTH), _whole(kv.shape)],
        compiler_params=_params(1, 48),
    )(dhb, w_out, bg, cg, u, cg, u, z, qm, kv, mo, cw)


def _bwd_conv(dconv, cg, u, cw):
    s, c = dconv.shape
    tm = ROW_TILE
    last = s // tm - 1

    def body(dc_ref, dcn_ref, cg_ref, u_ref, cgh_ref, uh_ref, cw_ref, dcg_ref, du_ref, dcw_ref):
        i = pl.program_id(0)

        @pl.when(i == 0)
        def _():
            dcw_ref[...] = jnp.zeros_like(dcw_ref)

        dc = dc_ref[...]
        dcn = jnp.where(i < last, dcn_ref[...], 0.0)
        row = lax.broadcasted_iota(jnp.int32, dc.shape, 0)
        d1 = jnp.where(row == tm - 1, dcn[0:1], pltpu.roll(dc, tm - 1, 0))
        d2 = jnp.where(row == tm - 1, dcn[1:2], jnp.where(row == tm - 2, dcn[0:1], pltpu.roll(dc, tm - 2, 0)))
        da = cw_ref[2:3, :] * dc + cw_ref[1:2, :] * d1 + cw_ref[0:1, :] * d2
        a, a1, a2 = _conv_taps(cg_ref, u_ref, cgh_ref, uh_ref, i)
        dcg_ref[...] = (da * u_ref[...]).astype(BF16)
        du_ref[...] = (da * cg_ref[...]).astype(BF16)
        dcw_ref[0:1, :] += jnp.sum(dc * a2, axis=0, keepdims=True)
        dcw_ref[1:2, :] += jnp.sum(dc * a1, axis=0, keepdims=True)
        dcw_ref[2:3, :] += jnp.sum(dc * a, axis=0, keepdims=True)

    return _pallas_call(
        body, name="bwd_conv", grid=(s // tm,),
        out_shape=[_sds((s, c), BF16), _sds((s, c), BF16), _sds((8, c), F32)],
        in_specs=[_rows(c), _halo_after(c, s), _rows(c), _rows(c), _halo_before(c), _halo_before(c), _whole(cw.shape)],
        out_specs=[_rows(c), _rows(c), _whole((8, c))], compiler_params=_params(1, 40),
    )(dconv, dconv, cg, u, cg, u, cw)


def _dgrad_norm(pieces, wg, h, g, dres, name):
    s, d_model = h.shape
    c = wg.shape[2]
    n = N_DEV * c
    tm = ROW_TILE
    widths = [p.shape[1] // d for p, d in pieces]
    assert sum(widths) == n
    n_p = len(pieces)

    def body(*refs):
        p_refs = refs[:n_p]
        w_ref, h_ref, g_ref, dr_ref, dh_ref, dg_ref, dpd_ref, dp, scr = refs[n_p:]

        @pl.when(pl.program_id(0) == 0)
        def _():
            dg_ref[...] = jnp.zeros_like(dg_ref)

        off = 0
        for p_ref, (_, d), wd in zip(p_refs, pieces, widths):
            if d == 1:
                dp[:, off:off + wd] = p_ref[...]
            else:
                dp[:, off:off + wd] = _from_view(scr, p_ref, d).astype(BF16)
            off += wd
        dhn = jnp.zeros((tm, d_model), F32)
        for j in range(N_DEV):
            dpj = dp[:, j * c:(j + 1) * c]
            dpd_ref[j] = dpj
            dhn += _dot_nt(dpj, w_ref[j])
        hb = h_ref[...]
        r = lax.rsqrt(jnp.mean(hb * hb, axis=-1, keepdims=True) + EPS)
        dg_ref[...] += jnp.sum(dhn * (hb * r), axis=0, keepdims=True)
        dn = dhn * g_ref[...]
        dh_ref[...] = dr_ref[...] + r * dn - hb * ((r * r * r) * jnp.mean(dn * hb, axis=-1, keepdims=True))

    p_specs = [_view_rows(wd, d) for (_, d), wd in zip(pieces, widths)]
    return _pallas_call(
        body, name=name, grid=(s // tm,),
        out_shape=[_sds((s, d_model), F32), _sds((1, d_model), F32), _sds((N_DEV, s, c), BF16)],
        in_specs=p_specs + [_whole(wg.shape), _rows(d_model), _whole((1, d_model)), _rows(d_model)],
        out_specs=[_rows(d_model), _whole((1, d_model)), pl.BlockSpec((N_DEV, tm, c), lambda i: (0, i, 0))],
        scratch_shapes=[pltpu.VMEM((tm, n), BF16), _view_scratch(GROUP_WIDTH)],
        compiler_params=_params(1, 60),
    )(*[p for p, _ in pieces], wg, h, g, dres)


def _wgrad_shards(a_t, b_dm, name):
    m, s = a_t.shape
    c = b_dm.shape[2]

    def body(a_ref, b_ref, o_ref):
        o_ref[...] = _dot(a_ref[...], b_ref[...]).astype(BF16)

    return _pallas_call(
        body, name=name, grid=(N_DEV,), out_shape=_sds((N_DEV, m, c), BF16),
        in_specs=[_whole(a_t.shape), pl.BlockSpec((None, s, c), lambda j: (j, 0, 0))],
        out_specs=pl.BlockSpec((None, m, c), lambda j: (j, 0, 0)), compiler_params=_params(1, 40),
    )(a_t, b_dm)


def _wgrad_cols(a_t, b, c, name):
    m, s = a_t.shape

    def body(a_ref, b_ref, o_ref):
        o_ref[...] = _dot(a_ref[...], b_ref[...]).astype(BF16)

    return _pallas_call(
        body, name=name, grid=(N_DEV,), out_shape=_sds((N_DEV, m, c), BF16),
        in_specs=[_whole(a_t.shape), pl.BlockSpec((s, c), lambda j: (0, j))],
        out_specs=pl.BlockSpec((None, m, c), lambda j: (j, 0, 0)), compiler_params=_params(1, 40),
    )(a_t, b)


def _wgrad_rows(a_t, b, name):
    m, s = a_t.shape
    n = b.shape[1]
    mr = m // N_DEV

    def body(a_ref, b_ref, o_ref):
        o_ref[...] = _dot(a_ref[...], b_ref[...]).astype(BF16)

    return _pallas_call(
        body, name=name, grid=(N_DEV,), out_shape=_sds((N_DEV, mr, n), BF16),
        in_specs=[pl.BlockSpec((mr, s), lambda j: (j, 0)), _whole(b.shape)],
        out_specs=pl.BlockSpec((None, mr, n), lambda j: (j, 0, 0)), compiler_params=_params(1, 40),
    )(a_t, b)


def _memkv_bwd(dkv, w, mem, g):
    n_layers = w.shape[0]

    def body(dkv_ref, w_ref, mem_ref, g_ref, dw_ref, dg_ref):
        mb = mem_ref[...]
        r = lax.rsqrt(jnp.mean(mb * mb, axis=-1, keepdims=True) + EPS)
        nm = mb * r
        mn = (nm * g_ref[...]).astype(BF16)
        dkvb = dkv_ref[...].astype(BF16)
        dw_ref[...] = _dot_tn(mn, dkvb).astype(BF16)
        dmn = _dot_nt(dkvb, w_ref[...])
        dg_ref[...] = jnp.sum(dmn * nm, axis=0, keepdims=True)

    lay = lambda *shape: pl.BlockSpec((None,) + shape, lambda l: (l, 0, 0))
    return _pallas_call(
        body, name="memkv_bwd", grid=(n_layers,),
        out_shape=[_sds((n_layers, D_MODEL, 2 * MEM_WIDTH), BF16), _sds((n_layers, 1, D_MODEL), F32)],
        in_specs=[lay(N_MEM, 2 * MEM_WIDTH), lay(D_MODEL, 2 * MEM_WIDTH), _whole(mem.shape), lay(1, D_MODEL)],
        out_specs=[lay(D_MODEL, 2 * MEM_WIDTH), lay(1, D_MODEL)], compiler_params=_params(1, 32),
    )(dkv, w, mem, g.reshape(n_layers, 1, D_MODEL))


def _bwd_post_attn(dhb, wg_out, z, os_, ls_, qm, kv, mo, head_ones):
    s = dhb.shape[0]
    gw = GROUP_WIDTH
    nb = gw + MEM_WIDTH
    c = wg_out.shape[2]
    tm = ROW_TILE

    def body(dh_ref, w_ref, z_ref, o0, o1, o2, l0, l1, l2, qm_ref, kv_ref, mo_ref, bd_ref,
             dz_ref, do0, do1, do2, dl0, dl1, dl2, dqm_ref, dkv_ref, s0, s1):
        @pl.when(pl.program_id(0) == 0)
        def _():
            dkv_ref[...] = jnp.zeros_like(dkv_ref)

        dy = jnp.zeros((tm, nb), F32)
        for j in range(N_DEV):
            dy += _dot_nt(dh_ref[:, j * c:(j + 1) * c], w_ref[j])
        ov, lv = [], []
        for o_ref, l_ref, d in zip((o0, o1, o2), (l0, l1, l2), DILATIONS):
            ov.append(_from_view(s0, o_ref, d))
            lv.append(_from_view(s1, l_ref, d))
        ws, mix = _mix_groups(ov, lv)
        sz, dsz = _silu_parts(z_ref[...])
        dz_ref[:, :gw] = (dy[:, :gw] * mix * dsz[:, :gw]).astype(BF16)
        dz_ref[:, gw:] = (dy[:, gw:] * mo_ref[...] * dsz[:, gw:]).astype(BF16)
        dbr = dy * sz
        dmix = dbr[:, :gw]
        t = dmix * mix
        th = t.astype(BF16)
        tl = (t - th.astype(F32)).astype(BF16)
        rs = _dot(th, bd_ref[...]) + _dot(tl, bd_ref[...])
        for wg_, do_ref, dl_ref, d in zip(ws, (do0, do1, do2), (dl0, dl1, dl2), DILATIONS):
            _to_view(s0, wg_ * dmix, do_ref, d)
            _to_view(s1, wg_ * rs, dl_ref, d)
        _mem_attn_bwd(qm_ref[...], kv_ref, dbr[:, gw:], dqm_ref, dkv_ref)

    vspecs = [_view_rows(gw, d) for d in DILATIONS]
    return _pallas_call(
        body, name="bwd_post_attn", grid=(s // tm,),
        out_shape=[_sds((s, nb), BF16)] + [_sds((s // d, d * gw), BF16) for d in DILATIONS]
                  + [_sds((s // d, d * gw), F32) for d in DILATIONS] + [_sds((s, MEM_WIDTH), BF16), _sds(kv.shape, F32)],
        in_specs=[_rows(D_MODEL), _whole(wg_out.shape), _rows(nb)] + vspecs * 2
                 + [_rows(MEM_WIDTH), _whole(kv.shape), _rows(MEM_WIDTH), _whole(head_ones.shape)],
        out_specs=[_rows(nb)] + vspecs * 2 + [_rows(MEM_WIDTH), _whole(kv.shape)],
        scratch_shapes=[_view_scratch(gw), _view_scratch(gw)],
        compiler_params=_params(1, 48),
    )(dhb, wg_out, z, *os_, *ls_, qm, kv, mo, head_ones)


def _attn_bwd(q, k, v, lse, do, dl, tabs, d):
    ln, dw = q.shape
    w = dw // d
    nb = ln // BLOCK
    nblk = d * nb
    reps = w // LANES
    cur, prev = _stream_maps(d, nb)
    qmap = lambda n: cur(jnp.minimum(n, nblk - 1))
    pmap = lambda n: prev(jnp.minimum(n, nblk - 1))
    omap = lambda n: cur(jnp.maximum(n - 1, 0))

    def body(q_ref, kp_ref, kc_ref, vp_ref, vc_ref, l_ref, do_ref, dl_ref, cq, saq, sbq, ck, sak, sbk,
             dq_ref, dk_ref, dv_ref, acck, accv, dqs):
        n = pl.program_id(0)

        @pl.when(n == 0)
        def _():
            acck[...] = jnp.zeros_like(acck)
            accv[...] = jnp.zeros_like(accv)

        @pl.when(n < nblk)
        def _():
            valid = _band_mask((n % nb) == 0)
            kk = jnp.concatenate([kp_ref[...], kc_ref[...]], axis=0)
            vv = jnp.concatenate([vp_ref[...], vc_ref[...]], axis=0)
            for h in range(HEADS_PER_GROUP):
                sl = slice(h * HEAD_DIM, (h + 1) * HEAD_DIM)
                col = slice(h * HEAD_DIM, h * HEAD_DIM + 1)
                qh = q_ref[:, sl]
                dob = do_ref[:, sl]
                sc = jnp.where(valid, _dot_nt(qh, kk[:, sl]), NEG)
                p = jnp.exp(sc - l_ref[:, col])
                dp = _dot_nt(dob, vv[:, sl])
                ds = (p * (dp - dl_ref[:, col])).astype(BF16)
                dqs[:, sl] = _dot(ds, kk[:, sl]) * SCALE
                acck[:, sl] += _dot_tn(ds, qh)
                accv[:, sl] += _dot_tn(p.astype(BF16), dob)
            tq = [jnp.tile(r[...], (1, reps)) for r in (cq, saq, sbq)]
            dq_ref[...] = _rope_bwd(dqs[...], *tq).astype(BF16)

        tk = [jnp.tile(r[...], (1, reps)) for r in (ck, sak, sbk)]
        dk_ref[...] = _rope_bwd(acck[0:BLOCK, :], *tk).astype(BF16)
        dv_ref[...] = accv[0:BLOCK, :].astype(BF16)
        acck[0:BLOCK, :] = acck[BLOCK:, :]
        accv[0:BLOCK, :] = accv[BLOCK:, :]
        acck[BLOCK:, :] = jnp.zeros((BLOCK, w), F32)
        accv[BLOCK:, :] = jnp.zeros((BLOCK, w), F32)

    blk = lambda f: pl.BlockSpec((BLOCK, w), f)
    tblk = lambda f: pl.BlockSpec((BLOCK, LANES), f)
    return _pallas_call(
        body, name=f"attn_bwd_d{d}", grid=(nblk + 1,),
        out_shape=[_sds((ln, dw), BF16)] * 3,
        in_specs=[blk(qmap), blk(pmap), blk(qmap), blk(pmap), blk(qmap), blk(qmap), blk(qmap), blk(qmap)]
                 + [tblk(qmap)] * 3 + [tblk(omap)] * 3,
        out_specs=[blk(qmap), blk(omap), blk(omap)],
        scratch_shapes=[pltpu.VMEM((2 * BLOCK, w), F32), pltpu.VMEM((2 * BLOCK, w), F32), pltpu.VMEM((BLOCK, w), F32)],
        compiler_params=_params(1, 32),
    )(q, k, k, v, v, lse, do, dl, *tabs, *tabs)


def _position():
    return lax.axis_index("x"), lax.axis_index("y"), lax.axis_index("c")


def _all_gather(shards, name):
    n_a = len(shards)

    def body(*refs):
        x_refs, out_refs = refs[:n_a], refs[n_a:2 * n_a]
        send_sems, recv_sems, local_sems = refs[2 * n_a:]
        x, y, c = _position()
        me, sibling = (x, y, c), (x, y, 1 - c)
        chips = [(1 - x, y), (x, 1 - y), (1 - x, 1 - y)]

        def rows(a, px, py, pc):
            return out_refs[a].at[4 * px + 2 * py + pc]

        def copy(a, k, block, to, own=False):
            return pltpu.make_async_remote_copy(
                src_ref=x_refs[a] if own else rows(a, *block), dst_ref=rows(a, *block),
                send_sem=send_sems.at[a, k], recv_sem=recv_sems.at[a, k], device_id=to, device_id_type=MESH)

        mine = [pltpu.make_async_copy(x_refs[a], rows(a, *me), local_sems.at[a]) for a in range(n_a)]
        for cp in mine:
            cp.start()
        first = []
        for j, chip in enumerate(chips):
            first += [copy(a, 1 + j, me, (*chip, c), own=True) for a in range(n_a)]
        first += [copy(a, 0, me, sibling, own=True) for a in range(n_a)]
        for cp in first:
            cp.start()
        passed = []
        for j, chip in enumerate(chips):
            for a in range(n_a):
                copy(a, 1 + j, (*chip, c), me).wait_recv()
                fwd = copy(a, 4 + j, (*chip, c), sibling)
                fwd.start()
                passed.append(fwd)
        for a in range(n_a):
            copy(a, 0, sibling, me).wait_recv()
        for j, chip in enumerate(chips):
            for a in range(n_a):
                copy(a, 4 + j, (*chip, 1 - c), me).wait_recv()
        for cp in first + passed:
            cp.wait_send()
        for cp in mine:
            cp.wait()

    return _pallas_call(
        body, name=name, out_shape=[_sds((N_DEV,) + t.shape, t.dtype) for t in shards],
        in_specs=[ANY] * n_a, out_specs=[ANY] * n_a,
        scratch_shapes=[pltpu.SemaphoreType.DMA((n_a, 7)), pltpu.SemaphoreType.DMA((n_a, 7)),
                        pltpu.SemaphoreType.DMA((n_a,))],
    )(*shards)


def _rs_to_sibling(gs):
    n_a = len(gs)

    def body(*refs):
        g_refs, recv_refs = refs[:n_a], refs[n_a:2 * n_a]
        send_sems, recv_sems = refs[2 * n_a:]
        x, y, c = _position()
        copies = []
        for k in range(4):
            for a in range(n_a):
                copies.append(pltpu.make_async_remote_copy(
                    src_ref=g_refs[a].at[2 * k + (1 - c)], dst_ref=recv_refs[a].at[k],
                    send_sem=send_sems.at[a, k], recv_sem=recv_sems.at[a, k],
                    device_id=(x, y, 1 - c), device_id_type=MESH))
        for cp in copies:
            cp.start()
        for cp in copies:
            cp.wait()

    return _pallas_call(
        body, name="rs_to_sibling", out_shape=[_sds((4,) + g.shape[1:], g.dtype) for g in gs],
        in_specs=[ANY] * n_a, out_specs=[ANY] * n_a,
        scratch_shapes=[pltpu.SemaphoreType.DMA((n_a, 4)), pltpu.SemaphoreType.DMA((n_a, 4))],
    )(*gs)


def _rs_to_chips(pbs):
    n_a = len(pbs)

    def body(*refs):
        p_refs, recv_refs = refs[:n_a], refs[n_a:2 * n_a]
        send_sems, recv_sems = refs[2 * n_a:]
        x, y, c = _position()
        chips = [(1 - x, y), (x, 1 - y), (1 - x, 1 - y)]
        copies = []
        for j, (px, py) in enumerate(chips):
            for a in range(n_a):
                copies.append(pltpu.make_async_remote_copy(
                    src_ref=p_refs[a].at[2 * px + py], dst_ref=recv_refs[a].at[j],
                    send_sem=send_sems.at[a, j], recv_sem=recv_sems.at[a, j],
                    device_id=(px, py, c), device_id_type=MESH))
        for cp in copies:
            cp.start()
        for cp in copies:
            cp.wait()

    return _pallas_call(
        body, name="rs_to_chips", out_shape=[_sds((3,) + p.shape[1:], p.dtype) for p in pbs],
        in_specs=[ANY] * n_a, out_specs=[ANY] * n_a,
        scratch_shapes=[pltpu.SemaphoreType.DMA((n_a, 3)), pltpu.SemaphoreType.DMA((n_a, 3))],
    )(*pbs)


def _row_tile(r):
    return ROW_TILE if r % ROW_TILE == 0 else r


def _rs_add_sibling(gp, recv, c_arr, name):
    _, r, l = gp.shape
    tr = _row_tile(r)

    def body(c_ref, g_ref, r_ref, pf_ref, pb_ref):
        sm = g_ref[...].astype(F32) + r_ref[...].astype(F32)
        pf_ref[...] = sm
        pb_ref[...] = sm.astype(BF16)

    spec = pl.BlockSpec((None, tr, l), lambda k, i, c: (k, i, 0))
    return _pallas_call(
        body, name=name,
        grid_spec=pltpu.PrefetchScalarGridSpec(
            num_scalar_prefetch=1, grid=(4, r // tr),
            in_specs=[pl.BlockSpec((None, tr, l), lambda k, i, c: (2 * k + c[0], i, 0)), spec],
            out_specs=[spec, spec]),
        out_shape=[_sds((4, r, l), F32), _sds((4, r, l), BF16)], compiler_params=_params(2, 32),
    )(c_arr, gp, recv)


def _adam_update(w, gv, m, v):
    nm = ADAM_B1 * m + (1.0 - ADAM_B1) * gv
    nv = ADAM_B2 * v + (1.0 - ADAM_B2) * (gv * gv)
    m_hat = nm / (1.0 - ADAM_B1 ** ADAM_STEP)
    v_hat = nv / (1.0 - ADAM_B2 ** ADAM_STEP)
    return -ADAM_LR * (m_hat / (jnp.sqrt(v_hat) + ADAM_EPS) + ADAM_WD * w), nm, nv


def _rs_finish_adamw(pf, recv, k_arr, w, m, v, name):
    _, r, l = pf.shape
    tr = _row_tile(r)

    def body(k_ref, p_ref, r_ref, w_ref, m_ref, v_ref, g_ref, d_ref, nm_ref, nv_ref):
        gv = ((p_ref[...] + r_ref[0].astype(F32)) + r_ref[1].astype(F32)) + r_ref[2].astype(F32)
        g_ref[...] = gv
        d_ref[...], nm_ref[...], nv_ref[...] = _adam_update(w_ref[...], gv, m_ref[...], v_ref[...])

    spec = pl.BlockSpec((tr, l), lambda i, k: (i, 0))
    return _pallas_call(
        body, name=name,
        grid_spec=pltpu.PrefetchScalarGridSpec(
            num_scalar_prefetch=1, grid=(r // tr,),
            in_specs=[pl.BlockSpec((None, tr, l), lambda i, k: (k[0], i, 0)),
                      pl.BlockSpec((3, tr, l), lambda i, k: (0, i, 0)), spec, spec, spec],
            out_specs=[spec] * 4),
        out_shape=[_sds((r, l), F32)] * 4, compiler_params=_params(1, 32),
    )(k_arr, pf, recv, w, m, v)


def _sum_devices(g):
    def body(g_ref, o_ref):
        acc = g_ref[0]
        for j in range(1, N_DEV):
            acc = acc + g_ref[j]
        o_ref[...] = acc

    return _pallas_call(body, name="sum_devices", out_shape=_sds(g.shape[1:], F32))(g)


def _adamw(w, g, m, v, name):
    shape = w.shape
    w2, g2, m2, v2 = [t.reshape((-1, shape[-1])) for t in (w, g, m, v)]

    def body(w_ref, g_ref, m_ref, v_ref, d_ref, nm_ref, nv_ref):
        d_ref[...], nm_ref[...], nv_ref[...] = _adam_update(w_ref[...], g_ref[...], m_ref[...], v_ref[...])

    outs = _pallas_call(body, name=name, out_shape=[_sds(w2.shape, F32)] * 3)(w2, g2, m2, v2)
    return tuple(t.reshape(shape) for t in outs)


def _local_step(x, mem, pos, tgt, norm_g, mem_norm_g, final_g, cw, wg_in0, wg_out0, wg_in1, w_out1, w_kv):
    tabs = _rope_tables(pos)
    g0, g1 = norm_g[0:1], norm_g[1:2]
    kv = _memkv_fwd(mem, mem_norm_g, w_kv)

    hn0, hn0_t, qs, ks, vs, tabs_v, qm0, z0 = _inproj_attn(x, g0, wg_in0, tabs)
    os_, ls_ = [], []
    for j, d in enumerate(DILATIONS):
        o, l = _attn_fwd(qs[j], ks[j], vs[j], d)
        os_.append(o)
        ls_.append(l)
    y0, y0_t, mo0, h1 = _post_attn(os_, ls_, qm0, kv[0], z0, x, wg_out0)

    hn1, hn1_t, bg, cg, u, qm1, z1 = _inproj_conv(h1, g1, wg_in1)
    y1, y1_t, mo1, dh2, dh2b, loss_acc, d_final_g = _post_conv_loss(
        bg, cg, u, qm1, kv[1], z1, h1, w_out1, cw, final_g.reshape(1, -1), tgt)

    d_w_out1 = _wgrad_rows(y1_t, dh2b, "wgrad_out1")
    dz1, dbg, dconv, dqm1, dkv1 = _bwd_post_conv(dh2b, w_out1, bg, cg, u, z1, qm1, kv[1], mo1, cw)
    dcg, du, dcw = _bwd_conv(dconv, cg, u, cw)
    dh1, dg1, dproj1 = _dgrad_norm([(dbg, 1), (dcg, 1), (du, 1), (dqm1, 1), (dz1, 1)], wg_in1, h1, g1, dh2,
                                   "dgrad_norm_conv")
    d_w_in1 = _wgrad_shards(hn1_t, dproj1, "wgrad_in1")

    dh1b = dh1.astype(BF16)
    d_w_out0 = _wgrad_cols(y0_t, dh1b, wg_out0.shape[2], "wgrad_out0")
    gw = GROUP_WIDTH
    ones = (jnp.arange(gw)[:, None] // HEAD_DIM == jnp.arange(gw)[None, :] // HEAD_DIM).astype(BF16)
    res = _bwd_post_attn(dh1b, wg_out0, z0, os_, ls_, qm0, kv[0], mo0, ones)
    dz0, dos, dls, dqm0, dkv0 = res[0], res[1:4], res[4:7], res[7], res[8]
    dqs, dks, dvs = [], [], []
    for j, d in enumerate(DILATIONS):
        dq, dk, dv = _attn_bwd(qs[j], ks[j], vs[j], ls_[j], dos[j], dls[j], tabs_v[j], d)
        dqs.append((dq, d))
        dks.append((dk, d))
        dvs.append((dv, d))
    dx, dg0, dproj0 = _dgrad_norm(dqs + dks + dvs + [(dqm0, 1), (dz0, 1)], wg_in0, x, g0, dh1, "dgrad_norm_attn")
    d_w_in0 = _wgrad_shards(hn0_t, dproj0, "wgrad_in0")

    d_w_kv, d_mem_g = _memkv_bwd(jnp.stack([dkv0, dkv1]), w_kv, mem, mem_norm_g)
    small = jnp.concatenate([dg0, dg1, d_mem_g.reshape(2, -1), d_final_g, dcw[0:3]], axis=0)
    n_kv = d_w_kv.shape[1] // N_DEV
    grads = [d_w_in0, d_w_out0, d_w_in1, d_w_out1,
             d_w_kv[0].reshape(N_DEV, n_kv, -1), d_w_kv[1].reshape(N_DEV, n_kv, -1)]
    return loss_acc[0, 0], dx, grads, small


def kernel(x, mem, positions, norm_g, mem_norm_g, w_mem_kv, attn_w_in, attn_w_out, conv_w_in, conv_w, conv_w_out, final_g, loss_target, m_norm_g, m_mem_norm_g, m_w_mem_kv, m_attn_w_in, m_attn_w_out, m_conv_w_in, m_conv_w, m_conv_w_out, m_final_g, v_norm_g, v_mem_norm_g, v_w_mem_kv, v_attn_w_in, v_attn_w_out, v_conv_w_in, v_conv_w, v_conv_w_out, v_final_g):
    px, py, pc = _position()
    me = 4 * px + 2 * py + pc

    shards = [attn_w_in[0], attn_w_out[0], conv_w_in[0], conv_w_out[0], w_mem_kv[0], w_mem_kv[1]]
    gathered = _all_gather([t.astype(BF16) for t in shards] + [jnp.pad(conv_w[0], ((0, 5), (0, 0)))], "gather_weights")
    wg_in0, wg_out0, wg_in1, wg_out1, wg_kv0, wg_kv1, cw_all = gathered
    w_out1 = wg_out1.reshape(-1, wg_out1.shape[2])
    w_kv = jnp.stack([wg_kv0.reshape(-1, wg_kv0.shape[2]), wg_kv1.reshape(-1, wg_kv1.shape[2])])
    cw = cw_all[:, 0:3].transpose(1, 0, 2).reshape(3, -1)

    loss_part, dx, grads, small_part = _local_step(
        x[0], mem[0], positions[0], loss_target[0], norm_g, mem_norm_g, final_g, cw,
        wg_in0, wg_out0, wg_in1, w_out1, w_kv)

    c_arr = jnp.reshape(pc, (1,)).astype(jnp.int32)
    k_arr = jnp.reshape(2 * px + py, (1,)).astype(jnp.int32)
    names = ["attn_w_in", "attn_w_out", "conv_w_in", "conv_w_out", "w_mem_kv0", "w_mem_kv1"]
    from_sibling = _rs_to_sibling(grads)
    chip_parts = [_rs_add_sibling(g, r, c_arr, "rs_add_sibling_" + n) for g, r, n in zip(grads, from_sibling, names)]
    from_chips = _rs_to_chips([pb for _, pb in chip_parts])
    moments = [(m_attn_w_in[0], v_attn_w_in[0]), (m_attn_w_out[0], v_attn_w_out[0]), (m_conv_w_in[0], v_conv_w_in[0]),
               (m_conv_w_out[0], v_conv_w_out[0]), (m_w_mem_kv[0], v_w_mem_kv[0]), (m_w_mem_kv[1], v_w_mem_kv[1])]
    big = {}
    for n, w, (pf, _), r, (m, v) in zip(names, shards, chip_parts, from_chips, moments):
        big[n] = _rs_finish_adamw(pf, r, k_arr, w, m, v, "rs_finish_adamw_" + n)
    for n in ("attn_w_in", "attn_w_out", "conv_w_in", "conv_w_out"):
        big[n] = tuple(t[None] for t in big[n])
    big["w_mem_kv"] = tuple(jnp.stack([a, b]) for a, b in zip(big["w_mem_kv0"], big["w_mem_kv1"]))

    small_part = jnp.concatenate([small_part, jnp.broadcast_to(loss_part, small_part.shape)], axis=0)
    small = _sum_devices(_all_gather([small_part], "gather_small_grads")[0])
    loss = small[8, 0]
    g_conv_w = lax.dynamic_slice(small[5:8], (0, me * LANES), (3, LANES))[None]
    small_g = dict(norm_g=small[0:2], mem_norm_g=small[2:4], conv_w=g_conv_w, final_g=small[4])
    small_w = dict(norm_g=(norm_g, m_norm_g, v_norm_g), mem_norm_g=(mem_norm_g, m_mem_norm_g, v_mem_norm_g),
                   conv_w=(conv_w, m_conv_w, v_conv_w), final_g=(final_g, m_final_g, v_final_g))
    for n, (w, m, v) in small_w.items():
        big[n] = (small_g[n],) + _adamw(w, small_g[n], m, v, "adamw_" + n)

    order = ["norm_g", "mem_norm_g", "w_mem_kv", "attn_w_in", "attn_w_out", "conv_w_in", "conv_w", "conv_w_out", "final_g"]
    return (loss, dx[None], *[big[n][0] for n in order], *[big[n][1] for n in order],
            *[big[n][2] for n in order], *[big[n][3] for n in order])
```

```python
import functools

import jax
import jax.numpy as jnp
from jax import lax
from jax.experimental import pallas as pl
from jax.experimental.pallas import tpu as pltpu

F32 = jnp.float32
BF16 = jnp.bfloat16

N_DEV = 8
D_MODEL = 1024
HEAD_DIM = 64
ROT_DIM = HEAD_DIM // 4
ROPE_THETA = 500000.0
DILATIONS = (1, 4, 16)
HEADS_PER_GROUP = 8
GROUP_WIDTH = HEADS_PER_GROUP * HEAD_DIM
BLOCK = 128
N_MEM = 256
MEM_HEADS = 4
MEM_WIDTH = MEM_HEADS * HEAD_DIM
CONV_WIDTH = D_MODEL
EPS = 1e-6
SCALE = HEAD_DIM ** -0.5
NEG = -1e30

ADAM_LR = 0.001
ADAM_B1 = 0.9
ADAM_B2 = 0.999
ADAM_EPS = 1e-08
ADAM_WD = 0.01
ADAM_STEP = 10

ROW_TILE = 256
LANES = 128
MESH = pl.DeviceIdType.MESH
ANY = pl.BlockSpec(memory_space=pl.ANY)


def _pallas_call(body, **kw):
    call = pl.pallas_call(body, **kw)

    def run(*args):
        pinned = [pltpu.with_memory_space_constraint(a, pltpu.HBM) if jnp.issubdtype(a.dtype, jnp.floating) else a
                  for a in args]
        return call(*pinned)

    return run


def _dot(a, b):
    return lax.dot_general(a, b, (((1,), (0,)), ((), ())), preferred_element_type=F32)


def _dot_nt(a, b):
    return lax.dot_general(a, b, (((1,), (1,)), ((), ())), preferred_element_type=F32)


def _dot_tn(a, b):
    return lax.dot_general(a, b, (((0,), (0,)), ((), ())), preferred_element_type=F32)


def _params(n_grid, vmem_mb=48):
    return pltpu.CompilerParams(dimension_semantics=("arbitrary",) * n_grid, vmem_limit_bytes=vmem_mb << 20)


def _rows(width, tm=ROW_TILE):
    return pl.BlockSpec((tm, width), lambda i: (i, 0))


def _view_rows(width, d, tm=ROW_TILE):
    return pl.BlockSpec((tm // d, d * width), lambda i: (i, 0))


def _whole(shape):
    return pl.BlockSpec(shape, lambda *_: (0,) * len(shape))


def _sds(shape, dtype):
    return jax.ShapeDtypeStruct(shape, dtype)


def _silu_parts(z):
    sg = jax.nn.sigmoid(z)
    return z * sg, sg * (1.0 + z * (1.0 - sg))


def _to_view(scr, val, out_ref, d):
    tm, w = val.shape
    if d == 1:
        out_ref[...] = val.astype(out_ref.dtype)
        return
    for cb in range(w // LANES):
        scr[cb] = val[:, cb * LANES:(cb + 1) * LANES]
    for r in range(d):
        for cb in range(w // LANES):
            lo = r * w + cb * LANES
            out_ref[:, lo:lo + LANES] = scr[cb, pl.ds(r, tm // d, stride=d), :].astype(out_ref.dtype)


def _from_view(scr, in_ref, d):
    if d == 1:
        return in_ref[...].astype(F32)
    nc, tm, _ = scr.shape
    w = nc * LANES
    for r in range(d):
        for cb in range(nc):
            lo = r * w + cb * LANES
            scr[cb, pl.ds(r, tm // d, stride=d), :] = in_ref[:, lo:lo + LANES].astype(F32)
    return jnp.concatenate([scr[cb] for cb in range(nc)], axis=1)


def _view_scratch(width, tm=ROW_TILE):
    return pltpu.VMEM((width // LANES, tm, LANES), F32)


def _rope_tables(pos):
    half = ROT_DIM // 2
    inv_freq = ROPE_THETA ** (-jnp.arange(half, dtype=F32) * (2.0 / ROT_DIM))
    ang = pos.astype(F32)[:, None] * inv_freq
    cos, sin = jnp.cos(ang), jnp.sin(ang)
    s = pos.shape[0]
    z8 = jnp.zeros((s, half), F32)
    rest = HEAD_DIM - ROT_DIM
    cosf = jnp.concatenate([cos, cos, jnp.ones((s, rest), F32)], axis=1)
    sa = jnp.concatenate([-sin, z8, jnp.zeros((s, rest), F32)], axis=1)
    sb = jnp.concatenate([z8, sin, jnp.zeros((s, rest), F32)], axis=1)
    return tuple(jnp.tile(t, (1, LANES // HEAD_DIM)) for t in (cosf, sa, sb))


def _rope_fwd(t, cv, sav, sbv):
    w = t.shape[1]
    return t * cv + pltpu.roll(t, w - ROT_DIM // 2, 1) * sav + pltpu.roll(t, ROT_DIM // 2, 1) * sbv


def _rope_bwd(g, cv, sav, sbv):
    w = g.shape[1]
    return g * cv + pltpu.roll(g * sav, ROT_DIM // 2, 1) + pltpu.roll(g * sbv, w - ROT_DIM // 2, 1)


def _project(hn, wg_ref, proj_scr):
    c = wg_ref.shape[2]
    for j in range(N_DEV):
        proj_scr[:, j * c:(j + 1) * c] = _dot(hn, wg_ref[j])


def _inproj_attn(x, g, wg, tabs):
    s, d_model = x.shape
    gw = GROUP_WIDTH
    n = N_DEV * wg.shape[2]
    nz = n - 9 * gw - MEM_WIDTH
    reps = gw // LANES
    tm = ROW_TILE

    def body(x_ref, g_ref, w_ref, c_ref, sa_ref, sb_ref, hn_ref, hnt_ref, *rest):
        outs, (proj, scr, tscr) = rest[:-3], rest[-3:]
        q_refs, k_refs, v_refs, t_refs, qm_ref, z_ref = outs[0:3], outs[3:6], outs[6:9], outs[9:18], outs[18], outs[19]
        xb = x_ref[...]
        r = lax.rsqrt(jnp.mean(xb * xb, axis=-1, keepdims=True) + EPS)
        hn = ((xb * r) * g_ref[...]).astype(BF16)
        hn_ref[...] = hn
        hnt_ref[...] = hn.T
        _project(hn, w_ref, proj)
        tab = (c_ref[...], sa_ref[...], sb_ref[...])
        cv, sav, sbv = [jnp.tile(t, (1, reps)) for t in tab]
        for j, d in enumerate(DILATIONS):
            tq = _rope_fwd(proj[:, j * gw:(j + 1) * gw], cv, sav, sbv)
            _to_view(scr, tq * SCALE, q_refs[j], d)
            tk = _rope_fwd(proj[:, (3 + j) * gw:(4 + j) * gw], cv, sav, sbv)
            _to_view(scr, tk, k_refs[j], d)
            _to_view(scr, proj[:, (6 + j) * gw:(7 + j) * gw], v_refs[j], d)
            for i in range(3):
                _to_view(tscr, tab[i], t_refs[3 * j + i], d)
        qm_ref[...] = proj[:, 9 * gw:9 * gw + MEM_WIDTH].astype(BF16)
        z_ref[...] = proj[:, 9 * gw + MEM_WIDTH:]

    views = [_sds((s // d, d * gw), BF16) for d in DILATIONS]
    tviews = [_sds((s // d, d * LANES), F32) for d in DILATIONS for _ in range(3)]
    out_shape = ([_sds((s, d_model), BF16), _sds((d_model, s), BF16)] + views * 3 + tviews
                 + [_sds((s, MEM_WIDTH), BF16), _sds((s, nz), F32)])
    vspecs = [_view_rows(gw, d) for d in DILATIONS]
    tspecs = [_view_rows(LANES, d) for d in DILATIONS for _ in range(3)]
    out_specs = ([_rows(d_model), pl.BlockSpec((d_model, tm), lambda i: (0, i))] + vspecs * 3 + tspecs
                 + [_rows(MEM_WIDTH), _rows(nz)])
    res = _pallas_call(
        body, name="inproj_attn", grid=(s // tm,), out_shape=out_shape,
        in_specs=[_rows(d_model), _whole((1, d_model)), _whole(wg.shape), _rows(LANES), _rows(LANES), _rows(LANES)],
        out_specs=out_specs,
        scratch_shapes=[pltpu.VMEM((tm, n), F32), _view_scratch(gw), _view_scratch(LANES)],
        compiler_params=_params(1, 60),
    )(x, g, wg, *tabs)
    tabs_v = [res[11 + 3 * j:14 + 3 * j] for j in range(3)]
    return res[0], res[1], res[2:5], res[5:8], res[8:11], tabs_v, res[20], res[21]


def _stream_maps(d, nb):
    def cur(n):
        return (n % nb, n // nb)

    def prev(n):
        return (jnp.maximum(n % nb - 1, 0), n // nb)

    return cur, prev


def _band_mask(first):
    qi = lax.broadcasted_iota(jnp.int32, (BLOCK, 2 * BLOCK), 0)
    kj = lax.broadcasted_iota(jnp.int32, (BLOCK, 2 * BLOCK), 1)
    lo = qi + jnp.where(first, 2 * BLOCK, 0)
    return jnp.logical_or(jnp.logical_and(kj < BLOCK, kj >= lo), jnp.logical_and(kj >= BLOCK, (kj - BLOCK) <= qi))


def _attn_fwd(q, k, v, d):
    ln, dw = q.shape
    w = dw // d
    nb = ln // BLOCK
    nblk = d * nb
    cur, prev = _stream_maps(d, nb)

    def body(q_ref, kp_ref, kc_ref, vp_ref, vc_ref, o_ref, lse_ref):
        n = pl.program_id(0)
        valid = _band_mask((n % nb) == 0)
        kk = jnp.concatenate([kp_ref[...], kc_ref[...]], axis=0)
        vv = jnp.concatenate([vp_ref[...], vc_ref[...]], axis=0)
        for h in range(HEADS_PER_GROUP):
            sl = slice(h * HEAD_DIM, (h + 1) * HEAD_DIM)
            sc = jnp.where(valid, _dot_nt(q_ref[:, sl], kk[:, sl]), NEG)
            m = jnp.max(sc, axis=-1, keepdims=True)
            p = jnp.exp(sc - m)
            l = jnp.sum(p, axis=-1, keepdims=True)
            pn = p * (1.0 / l)
            o_ref[:, sl] = _dot(pn.astype(BF16), vv[:, sl])
            lse_ref[:, sl] = jnp.broadcast_to(m + jnp.log(l), (BLOCK, HEAD_DIM))

    blk = lambda f: pl.BlockSpec((BLOCK, w), f)
    return _pallas_call(
        body, name=f"attn_fwd_d{d}", grid=(nblk,),
        out_shape=[_sds((ln, dw), F32)] * 2,
        in_specs=[blk(cur), blk(prev), blk(cur), blk(prev), blk(cur)],
        out_specs=[blk(cur), blk(cur)], compiler_params=_params(1, 32),
    )(q, k, k, v, v)


def _memkv_fwd(mem, g, w):
    n_layers = w.shape[0]

    def body(mem_ref, g_ref, w_ref, kv_ref):
        mb = mem_ref[...]
        r = lax.rsqrt(jnp.mean(mb * mb, axis=-1, keepdims=True) + EPS)
        mn = ((mb * r) * g_ref[...]).astype(BF16)
        kv_ref[...] = _dot(mn, w_ref[...]).astype(BF16)

    return _pallas_call(
        body, name="memkv_fwd", grid=(n_layers,),
        out_shape=_sds((n_layers, N_MEM, 2 * MEM_WIDTH), BF16),
        in_specs=[_whole(mem.shape), pl.BlockSpec((None, 1, D_MODEL), lambda l: (l, 0, 0)),
                  pl.BlockSpec((None, D_MODEL, 2 * MEM_WIDTH), lambda l: (l, 0, 0))],
        out_specs=pl.BlockSpec((None, N_MEM, 2 * MEM_WIDTH), lambda l: (l, 0, 0)),
        compiler_params=_params(1, 32),
    )(mem, g.reshape(n_layers, 1, D_MODEL), w)


def _mix_groups(os_, ls_):
    mx = jnp.maximum(jnp.maximum(ls_[0], ls_[1]), ls_[2])
    es = [jnp.exp(t - mx) for t in ls_]
    inv = 1.0 / (es[0] + es[1] + es[2])
    ws = [e * inv for e in es]
    mix = ws[0] * os_[0] + ws[1] * os_[1] + ws[2] * os_[2]
    return ws, mix


def _mem_probs(qm, km, h):
    sl = slice(h * HEAD_DIM, (h + 1) * HEAD_DIM)
    sc = _dot_nt(qm[:, sl], km[:, sl]) * SCALE
    e = jnp.exp(sc - jnp.max(sc, axis=-1, keepdims=True))
    return e * (1.0 / jnp.sum(e, axis=-1, keepdims=True))


def _mem_attn_into(qm, kv_ref, mo_ref):
    km, vm = kv_ref[:, :MEM_WIDTH], kv_ref[:, MEM_WIDTH:]
    for h in range(MEM_HEADS):
        sl = slice(h * HEAD_DIM, (h + 1) * HEAD_DIM)
        p = _mem_probs(qm, km, h)
        mo_ref[:, sl] = _dot(p.astype(BF16), vm[:, sl])


def _mem_attn_bwd(qm, kv_ref, dmem, dqm_ref, dkv_ref):
    km, vm = kv_ref[:, :MEM_WIDTH], kv_ref[:, MEM_WIDTH:]
    dmb = dmem.astype(BF16)
    for h in range(MEM_HEADS):
        sl = slice(h * HEAD_DIM, (h + 1) * HEAD_DIM)
        slv = slice(MEM_WIDTH + h * HEAD_DIM, MEM_WIDTH + (h + 1) * HEAD_DIM)
        p = _mem_probs(qm, km, h)
        dp = _dot_nt(dmb[:, sl], vm[:, sl])
        ds = (p * (dp - jnp.sum(dp * p, axis=-1, keepdims=True)) * SCALE).astype(BF16)
        dqm_ref[:, sl] = _dot(ds, km[:, sl]).astype(BF16)
        dkv_ref[:, sl] += _dot_tn(ds, qm[:, sl])
        dkv_ref[:, slv] += _dot_tn(p.astype(BF16), dmb[:, sl])


def _post_attn(os_, ls_, qm, kv, z, x, wg_out):
    s, d_model = x.shape
    gw = GROUP_WIDTH
    nb = gw + MEM_WIDTH
    c = wg_out.shape[2]
    tm = ROW_TILE

    def body(o0, o1, o2, l0, l1, l2, qm_ref, kv_ref, z_ref, x_ref, w_ref, y_ref, yt_ref, mo_ref, h_ref, s0, s1):
        ov, lv = [], []
        for o_ref, l_ref, d in zip((o0, o1, o2), (l0, l1, l2), DILATIONS):
            ov.append(_from_view(s0, o_ref, d))
            lv.append(_from_view(s1, l_ref, d))
        _, mix = _mix_groups(ov, lv)
        _mem_attn_into(qm_ref[...], kv_ref, mo_ref)
        sz, _ = _silu_parts(z_ref[...])
        y_ref[:, :gw] = (mix * sz[:, :gw]).astype(BF16)
        y_ref[:, gw:] = (mo_ref[...] * sz[:, gw:]).astype(BF16)
        y = y_ref[...]
        yt_ref[...] = y.T
        for j in range(N_DEV):
            h_ref[:, j * c:(j + 1) * c] = x_ref[:, j * c:(j + 1) * c] + _dot(y, w_ref[j])

    vspecs = [_view_rows(gw, d) for d in DILATIONS]
    return _pallas_call(
        body, name="post_attn", grid=(s // tm,),
        out_shape=[_sds((s, nb), BF16), _sds((nb, s), BF16), _sds((s, MEM_WIDTH), F32), _sds((s, d_model), F32)],
        in_specs=vspecs * 2 + [_rows(MEM_WIDTH), _whole(kv.shape), _rows(nb), _rows(d_model), _whole(wg_out.shape)],
        out_specs=[_rows(nb), pl.BlockSpec((nb, tm), lambda i: (0, i)), _rows(MEM_WIDTH), _rows(d_model)],
        scratch_shapes=[_view_scratch(gw), _view_scratch(gw)],
        compiler_params=_params(1, 40),
    )(*os_, *ls_, qm, kv, z, x, wg_out)


def _inproj_conv(x, g, wg):
    s, d_model = x.shape
    c = CONV_WIDTH
    n = N_DEV * wg.shape[2]
    nz = n - 3 * c - MEM_WIDTH
    tm = ROW_TILE

    def body(x_ref, g_ref, w_ref, hn_ref, hnt_ref, bg_ref, cg_ref, u_ref, qm_ref, z_ref, proj):
        xb = x_ref[...]
        r = lax.rsqrt(jnp.mean(xb * xb, axis=-1, keepdims=True) + EPS)
        hn = ((xb * r) * g_ref[...]).astype(BF16)
        hn_ref[...] = hn
        hnt_ref[...] = hn.T
        _project(hn, w_ref, proj)
        bg_ref[...] = proj[:, 0:c]
        cg_ref[...] = proj[:, c:2 * c]
        u_ref[...] = proj[:, 2 * c:3 * c]
        qm_ref[...] = proj[:, 3 * c:3 * c + MEM_WIDTH].astype(BF16)
        z_ref[...] = proj[:, 3 * c + MEM_WIDTH:]

    return _pallas_call(
        body, name="inproj_conv", grid=(s // tm,),
        out_shape=[_sds((s, d_model), BF16), _sds((d_model, s), BF16)] + [_sds((s, c), F32)] * 3
                  + [_sds((s, MEM_WIDTH), BF16), _sds((s, nz), F32)],
        in_specs=[_rows(d_model), _whole((1, d_model)), _whole(wg.shape)],
        out_specs=[_rows(d_model), pl.BlockSpec((d_model, tm), lambda i: (0, i))] + [_rows(c)] * 3
                  + [_rows(MEM_WIDTH), _rows(nz)],
        scratch_shapes=[pltpu.VMEM((tm, n), F32)],
        compiler_params=_params(1, 60),
    )(x, g, wg)


HALO = 8


def _halo_before(width, tm=ROW_TILE):
    return pl.BlockSpec((HALO, width), lambda i: (jnp.maximum(i * (tm // HALO) - 1, 0), 0))


def _halo_after(width, n_rows, tm=ROW_TILE):
    return pl.BlockSpec((HALO, width), lambda i: (jnp.minimum((i + 1) * (tm // HALO), n_rows // HALO - 1), 0))


def _conv_taps(cg_ref, u_ref, cgh_ref, uh_ref, i):
    a = cg_ref[...] * u_ref[...]
    ah = jnp.where(i > 0, cgh_ref[...] * uh_ref[...], 0.0)
    row = lax.broadcasted_iota(jnp.int32, a.shape, 0)
    a1 = jnp.where(row == 0, ah[HALO - 1:HALO], pltpu.roll(a, 1, 0))
    a2 = jnp.where(row == 0, ah[HALO - 2:HALO - 1], jnp.where(row == 1, ah[HALO - 1:HALO], pltpu.roll(a, 2, 0)))
    return a, a1, a2


def _post_conv_loss(bg, cg, u, qm, kv, z, h1, w_out, cw, gf, tgt):
    s, d = h1.shape
    c = CONV_WIDTH
    nb = c + MEM_WIDTH
    tm = ROW_TILE

    def body(bg_ref, cg_ref, u_ref, cgh_ref, uh_ref, qm_ref, kv_ref, z_ref, h_ref, w_ref, cw_ref, gf_ref, t_ref,
             y_ref, yt_ref, mo_ref, dh_ref, dhb_ref, loss_ref, dgf_ref):
        i = pl.program_id(0)
        a, a1, a2 = _conv_taps(cg_ref, u_ref, cgh_ref, uh_ref, i)
        conv = cw_ref[0:1, :] * a2 + cw_ref[1:2, :] * a1 + cw_ref[2:3, :] * a
        mix = bg_ref[...] * conv
        _mem_attn_into(qm_ref[...], kv_ref, mo_ref)
        sz, _ = _silu_parts(z_ref[...])
        y_ref[:, :c] = (mix * sz[:, :c]).astype(BF16)
        y_ref[:, c:] = (mo_ref[...] * sz[:, c:]).astype(BF16)
        y = y_ref[...]
        yt_ref[...] = y.T
        h2 = h_ref[...] + _dot(y, w_ref[...])
        r = lax.rsqrt(jnp.mean(h2 * h2, axis=-1, keepdims=True) + EPS)
        nh = h2 * r
        gfv = gf_ref[...]
        diff = nh * gfv - t_ref[...]
        dout = diff * (1.0 / d)
        dn = dout * gfv
        dh2 = r * dn - h2 * ((r * r * r) * jnp.mean(dn * h2, axis=-1, keepdims=True))
        dh_ref[...] = dh2
        dhb_ref[...] = dh2.astype(BF16)

        @pl.when(i == 0)
        def _():
            loss_ref[...] = jnp.zeros_like(loss_ref)
            dgf_ref[...] = jnp.zeros_like(dgf_ref)

        loss_ref[...] += 0.5 * jnp.sum(jnp.mean(diff * diff, axis=-1, keepdims=True))
        dgf_ref[...] += jnp.sum(dout * nh, axis=0, keepdims=True)

    return _pallas_call(
        body, name="post_conv_loss", grid=(s // tm,),
        out_shape=[_sds((s, nb), BF16), _sds((nb, s), BF16), _sds((s, MEM_WIDTH), F32), _sds((s, d), F32),
                   _sds((s, d), BF16), _sds((8, LANES), F32), _sds((1, d), F32)],
        in_specs=[_rows(c)] * 3 + [_halo_before(c)] * 2 + [_rows(MEM_WIDTH), _whole(kv.shape), _rows(nb), _rows(d),
                  _whole(w_out.shape), _whole(cw.shape), _whole((1, d)), _rows(d)],
        out_specs=[_rows(nb), pl.BlockSpec((nb, tm), lambda i: (0, i)), _rows(MEM_WIDTH), _rows(d), _rows(d),
                   _whole((8, LANES)), _whole((1, d))],
        compiler_params=_params(1, 48),
    )(bg, cg, u, cg, u, qm, kv, z, h1, w_out, cw, gf, tgt)


def _bwd_post_conv(dhb, w_out, bg, cg, u, z, qm, kv, mo, cw):
    s = dhb.shape[0]
    c = CONV_WIDTH
    nb = c + MEM_WIDTH

    def body(dh_ref, w_ref, bg_ref, cg_ref, u_ref, cgh_ref, uh_ref, z_ref, qm_ref, kv_ref, mo_ref, cw_ref,
             dz_ref, dbg_ref, dc_ref, dqm_ref, dkv_ref):
        i = pl.program_id(0)

        @pl.when(i == 0)
        def _():
            dkv_ref[...] = jnp.zeros_like(dkv_ref)

        dy = _dot_nt(dh_ref[...], w_ref[...])
        sz, dsz = _silu_parts(z_ref[...])
        a, a1, a2 = _conv_taps(cg_ref, u_ref, cgh_ref, uh_ref, i)
        conv = cw_ref[0:1, :] * a2 + cw_ref[1:2, :] * a1 + cw_ref[2:3, :] * a
        bgv = bg_ref[...]
        dz_ref[:, :c] = (dy[:, :c] * (bgv * conv) * dsz[:, :c]).astype(BF16)
        dz_ref[:, c:] = (dy[:, c:] * mo_ref[...] * dsz[:, c:]).astype(BF16)
        dbr = dy * sz
        dmix = dbr[:, :c]
        dbg_ref[...] = (dmix * conv).astype(BF16)
        dc_ref[...] = dmix * bgv
        _mem_attn_bwd(qm_ref[...], kv_ref, dbr[:, c:], dqm_ref, dkv_ref)

    return _pallas_call(
        body, name="bwd_post_conv", grid=(s // ROW_TILE,),
        out_shape=[_sds((s, nb), BF16), _sds((s, c), BF16), _sds((s, c), F32), _sds((s, MEM_WIDTH), BF16),
                   _sds(kv.shape, F32)],
        in_specs=[_rows(D_MODEL), _whole(w_out.shape)] + [_rows(c)] * 3 + [_halo_before(c)] * 2
                 + [_rows(nb), _rows(MEM_WIDTH), _whole(kv.shape), _rows(MEM_WIDTH), _whole(cw.shape)],
        out_specs=[_rows(nb), _rows(c), _rows(c), _rows(MEM_WIDTH), _whole(kv.shape)],
        compiler_params=_params(1, 48),
    )(dhb, w_out, bg, cg, u, cg, u, z, qm, kv, mo, cw)


def _bwd_conv(dconv, cg, u, cw):
    s, c = dconv.shape
    tm = ROW_TILE
    last = s // tm - 1

    def body(dc_ref, dcn_ref, cg_ref, u_ref, cgh_ref, uh_ref, cw_ref, dcg_ref, du_ref, dcw_ref):
        i = pl.program_id(0)

        @pl.when(i == 0)
        def _():
            dcw_ref[...] = jnp.zeros_like(dcw_ref)

        dc = dc_ref[...]
        dcn = jnp.where(i < last, dcn_ref[...], 0.0)
        row = lax.broadcasted_iota(jnp.int32, dc.shape, 0)
        d1 = jnp.where(row == tm - 1, dcn[0:1], pltpu.roll(dc, tm - 1, 0))
        d2 = jnp.where(row == tm - 1, dcn[1:2], jnp.where(row == tm - 2, dcn[0:1], pltpu.roll(dc, tm - 2, 0)))
        da = cw_ref[2:3, :] * dc + cw_ref[1:2, :] * d1 + cw_ref[0:1, :] * d2
        a, a1, a2 = _conv_taps(cg_ref, u_ref, cgh_ref, uh_ref, i)
        dcg_ref[...] = (da * u_ref[...]).astype(BF16)
        du_ref[...] = (da * cg_ref[...]).astype(BF16)
        dcw_ref[0:1, :] += jnp.sum(dc * a2, axis=0, keepdims=True)
        dcw_ref[1:2, :] += jnp.sum(dc * a1, axis=0, keepdims=True)
        dcw_ref[2:3, :] += jnp.sum(dc * a, axis=0, keepdims=True)

    return _pallas_call(
        body, name="bwd_conv", grid=(s // tm,),
        out_shape=[_sds((s, c), BF16), _sds((s, c), BF16), _sds((8, c), F32)],
        in_specs=[_rows(c), _halo_after(c, s), _rows(c), _rows(c), _halo_before(c), _halo_before(c), _whole(cw.shape)],
        out_specs=[_rows(c), _rows(c), _whole((8, c))], compiler_params=_params(1, 40),
    )(dconv, dconv, cg, u, cg, u, cw)


def _dgrad_norm(pieces, wg, h, g, dres, name):
    s, d_model = h.shape
    c = wg.shape[2]
    n = N_DEV * c
    tm = ROW_TILE
    widths = [p.shape[1] // d for p, d in pieces]
    assert sum(widths) == n
    n_p = len(pieces)

    def body(*refs):
        p_refs = refs[:n_p]
        w_ref, h_ref, g_ref, dr_ref, dh_ref, dg_ref, dpd_ref, dp, scr = refs[n_p:]

        @pl.when(pl.program_id(0) == 0)
        def _():
            dg_ref[...] = jnp.zeros_like(dg_ref)

        off = 0
        for p_ref, (_, d), wd in zip(p_refs, pieces, widths):
            if d == 1:
                dp[:, off:off + wd] = p_ref[...]
            else:
                dp[:, off:off + wd] = _from_view(scr, p_ref, d).astype(BF16)
            off += wd
        dhn = jnp.zeros((tm, d_model), F32)
        for j in range(N_DEV):
            dpj = dp[:, j * c:(j + 1) * c]
            dpd_ref[j] = dpj
            dhn += _dot_nt(dpj, w_ref[j])
        hb = h_ref[...]
        r = lax.rsqrt(jnp.mean(hb * hb, axis=-1, keepdims=True) + EPS)
        dg_ref[...] += jnp.sum(dhn * (hb * r), axis=0, keepdims=True)
        dn = dhn * g_ref[...]
        dh_ref[...] = dr_ref[...] + r * dn - hb * ((r * r * r) * jnp.mean(dn * hb, axis=-1, keepdims=True))

    p_specs = [_view_rows(wd, d) for (_, d), wd in zip(pieces, widths)]
    return _pallas_call(
        body, name=name, grid=(s // tm,),
        out_shape=[_sds((s, d_model), F32), _sds((1, d_model), F32), _sds((N_DEV, s, c), BF16)],
        in_specs=p_specs + [_whole(wg.shape), _rows(d_model), _whole((1, d_model)), _rows(d_model)],
        out_specs=[_rows(d_model), _whole((1, d_model)), pl.BlockSpec((N_DEV, tm, c), lambda i: (0, i, 0))],
        scratch_shapes=[pltpu.VMEM((tm, n), BF16), _view_scratch(GROUP_WIDTH)],
        compiler_params=_params(1, 60),
    )(*[p for p, _ in pieces], wg, h, g, dres)


def _wgrad_shards(a_t, b_dm, name):
    m, s = a_t.shape
    c = b_dm.shape[2]

    def body(a_ref, b_ref, o_ref):
        o_ref[...] = _dot(a_ref[...], b_ref[...]).astype(BF16)

    return _pallas_call(
        body, name=name, grid=(N_DEV,), out_shape=_sds((N_DEV, m, c), BF16),
        in_specs=[_whole(a_t.shape), pl.BlockSpec((None, s, c), lambda j: (j, 0, 0))],
        out_specs=pl.BlockSpec((None, m, c), lambda j: (j, 0, 0)), compiler_params=_params(1, 40),
    )(a_t, b_dm)


def _wgrad_cols(a_t, b, c, name):
    m, s = a_t.shape

    def body(a_ref, b_ref, o_ref):
        o_ref[...] = _dot(a_ref[...], b_ref[...]).astype(BF16)

    return _pallas_call(
        body, name=name, grid=(N_DEV,), out_shape=_sds((N_DEV, m, c), BF16),
        in_specs=[_whole(a_t.shape), pl.BlockSpec((s, c), lambda j: (0, j))],
        out_specs=pl.BlockSpec((None, m, c), lambda j: (j, 0, 0)), compiler_params=_params(1, 40),
    )(a_t, b)


def _wgrad_rows(a_t, b, name):
    m, s = a_t.shape
    n = b.shape[1]
    mr = m // N_DEV

    def body(a_ref, b_ref, o_ref):
        o_ref[...] = _dot(a_ref[...], b_ref[...]).astype(BF16)

    return _pallas_call(
        body, name=name, grid=(N_DEV,), out_shape=_sds((N_DEV, mr, n), BF16),
        in_specs=[pl.BlockSpec((mr, s), lambda j: (j, 0)), _whole(b.shape)],
        out_specs=pl.BlockSpec((None, mr, n), lambda j: (j, 0, 0)), compiler_params=_params(1, 40),
    )(a_t, b)


def _memkv_bwd(dkv, w, mem, g):
    n_layers = w.shape[0]

    def body(dkv_ref, w_ref, mem_ref, g_ref, dw_ref, dg_ref):
        mb = mem_ref[...]
        r = lax.rsqrt(jnp.mean(mb * mb, axis=-1, keepdims=True) + EPS)
        nm = mb * r
        mn = (nm * g_ref[...]).astype(BF16)
        dkvb = dkv_ref[...].astype(BF16)
        dw_ref[...] = _dot_tn(mn, dkvb).astype(BF16)
        dmn = _dot_nt(dkvb, w_ref[...])
        dg_ref[...] = jnp.sum(dmn * nm, axis=0, keepdims=True)

    lay = lambda *shape: pl.BlockSpec((None,) + shape, lambda l: (l, 0, 0))
    return _pallas_call(
        body, name="memkv_bwd", grid=(n_layers,),
        out_shape=[_sds((n_layers, D_MODEL, 2 * MEM_WIDTH), BF16), _sds((n_layers, 1, D_MODEL), F32)],
        in_specs=[lay(N_MEM, 2 * MEM_WIDTH), lay(D_MODEL, 2 * MEM_WIDTH), _whole(mem.shape), lay(1, D_MODEL)],
        out_specs=[lay(D_MODEL, 2 * MEM_WIDTH), lay(1, D_MODEL)], compiler_params=_params(1, 32),
    )(dkv, w, mem, g.reshape(n_layers, 1, D_MODEL))


def _bwd_post_attn(dhb, wg_out, z, os_, ls_, qm, kv, mo, head_ones):
    s = dhb.shape[0]
    gw = GROUP_WIDTH
    nb = gw + MEM_WIDTH
    c = wg_out.shape[2]
    tm = ROW_TILE

    def body(dh_ref, w_ref, z_ref, o0, o1, o2, l0, l1, l2, qm_ref, kv_ref, mo_ref, bd_ref,
             dz_ref, do0, do1, do2, dl0, dl1, dl2, dqm_ref, dkv_ref, s0, s1):
        @pl.when(pl.program_id(0) == 0)
        def _():
            dkv_ref[...] = jnp.zeros_like(dkv_ref)

        dy = jnp.zeros((tm, nb), F32)
        for j in range(N_DEV):
            dy += _dot_nt(dh_ref[:, j * c:(j + 1) * c], w_ref[j])
        ov, lv = [], []
        for o_ref, l_ref, d in zip((o0, o1, o2), (l0, l1, l2), DILATIONS):
            ov.append(_from_view(s0, o_ref, d))
            lv.append(_from_view(s1, l_ref, d))
        ws, mix = _mix_groups(ov, lv)
        sz, dsz = _silu_parts(z_ref[...])
        dz_ref[:, :gw] = (dy[:, :gw] * mix * dsz[:, :gw]).astype(BF16)
        dz_ref[:, gw:] = (dy[:, gw:] * mo_ref[...] * dsz[:, gw:]).astype(BF16)
        dbr = dy * sz
        dmix = dbr[:, :gw]
        t = dmix * mix
        th = t.astype(BF16)
        tl = (t - th.astype(F32)).astype(BF16)
        rs = _dot(th, bd_ref[...]) + _dot(tl, bd_ref[...])
        for wg_, do_ref, dl_ref, d in zip(ws, (do0, do1, do2), (dl0, dl1, dl2), DILATIONS):
            _to_view(s0, wg_ * dmix, do_ref, d)
            _to_view(s1, wg_ * rs, dl_ref, d)
        _mem_attn_bwd(qm_ref[...], kv_ref, dbr[:, gw:], dqm_ref, dkv_ref)

    vspecs = [_view_rows(gw, d) for d in DILATIONS]
    return _pallas_call(
        body, name="bwd_post_attn", grid=(s // tm,),
        out_shape=[_sds((s, nb), BF16)] + [_sds((s // d, d * gw), BF16) for d in DILATIONS]
                  + [_sds((s // d, d * gw), F32) for d in DILATIONS] + [_sds((s, MEM_WIDTH), BF16), _sds(kv.shape, F32)],
        in_specs=[_rows(D_MODEL), _whole(wg_out.shape), _rows(nb)] + vspecs * 2
                 + [_rows(MEM_WIDTH), _whole(kv.shape), _rows(MEM_WIDTH), _whole(head_ones.shape)],
        out_specs=[_rows(nb)] + vspecs * 2 + [_rows(MEM_WIDTH), _whole(kv.shape)],
        scratch_shapes=[_view_scratch(gw), _view_scratch(gw)],
        compiler_params=_params(1, 48),
    )(dhb, wg_out, z, *os_, *ls_, qm, kv, mo, head_ones)


def _attn_bwd(q, k, v, lse, do, dl, tabs, d):
    ln, dw = q.shape
    w = dw // d
    nb = ln // BLOCK
    nblk = d * nb
    reps = w // LANES
    cur, prev = _stream_maps(d, nb)
    qmap = lambda n: cur(jnp.minimum(n, nblk - 1))
    pmap = lambda n: prev(jnp.minimum(n, nblk - 1))
    omap = lambda n: cur(jnp.maximum(n - 1, 0))

    def body(q_ref, kp_ref, kc_ref, vp_ref, vc_ref, l_ref, do_ref, dl_ref, cq, saq, sbq, ck, sak, sbk,
             dq_ref, dk_ref, dv_ref, acck, accv, dqs):
        n = pl.program_id(0)

        @pl.when(n == 0)
        def _():
            acck[...] = jnp.zeros_like(acck)
            accv[...] = jnp.zeros_like(accv)

        @pl.when(n < nblk)
        def _():
            valid = _band_mask((n % nb) == 0)
            kk = jnp.concatenate([kp_ref[...], kc_ref[...]], axis=0)
            vv = jnp.concatenate([vp_ref[...], vc_ref[...]], axis=0)
            for h in range(HEADS_PER_GROUP):
                sl = slice(h * HEAD_DIM, (h + 1) * HEAD_DIM)
                col = slice(h * HEAD_DIM, h * HEAD_DIM + 1)
                qh = q_ref[:, sl]
                dob = do_ref[:, sl]
                sc = jnp.where(valid, _dot_nt(qh, kk[:, sl]), NEG)
                p = jnp.exp(sc - l_ref[:, col])
                dp = _dot_nt(dob, vv[:, sl])
                ds = (p * (dp - dl_ref[:, col])).astype(BF16)
                dqs[:, sl] = _dot(ds, kk[:, sl]) * SCALE
                acck[:, sl] += _dot_tn(ds, qh)
                accv[:, sl] += _dot_tn(p.astype(BF16), dob)
            tq = [jnp.tile(r[...], (1, reps)) for r in (cq, saq, sbq)]
            dq_ref[...] = _rope_bwd(dqs[...], *tq).astype(BF16)

        tk = [jnp.tile(r[...], (1, reps)) for r in (ck, sak, sbk)]
        dk_ref[...] = _rope_bwd(acck[0:BLOCK, :], *tk).astype(BF16)
        dv_ref[...] = accv[0:BLOCK, :].astype(BF16)
        acck[0:BLOCK, :] = acck[BLOCK:, :]
        accv[0:BLOCK, :] = accv[BLOCK:, :]
        acck[BLOCK:, :] = jnp.zeros((BLOCK, w), F32)
        accv[BLOCK:, :] = jnp.zeros((BLOCK, w), F32)

    blk = lambda f: pl.BlockSpec((BLOCK, w), f)
    tblk = lambda f: pl.BlockSpec((BLOCK, LANES), f)
    return _pallas_call(
        body, name=f"attn_bwd_d{d}", grid=(nblk + 1,),
        out_shape=[_sds((ln, dw), BF16)] * 3,
        in_specs=[blk(qmap), blk(pmap), blk(qmap), blk(pmap), blk(qmap), blk(qmap), blk(qmap), blk(qmap)]
                 + [tblk(qmap)] * 3 + [tblk(omap)] * 3,
        out_specs=[blk(qmap), blk(omap), blk(omap)],
        scratch_shapes=[pltpu.VMEM((2 * BLOCK, w), F32), pltpu.VMEM((2 * BLOCK, w), F32), pltpu.VMEM((BLOCK, w), F32)],
        compiler_params=_params(1, 32),
    )(q, k, k, v, v, lse, do, dl, *tabs, *tabs)


def _position():
    return lax.axis_index("x"), lax.axis_index("y"), lax.axis_index("c")


def _all_gather(shards, name):
    n_a = len(shards)

    def body(*refs):
        x_refs, out_refs = refs[:n_a], refs[n_a:2 * n_a]
        send_sems, recv_sems, local_sems = refs[2 * n_a:]
        x, y, c = _position()
        me, sibling = (x, y, c), (x, y, 1 - c)
        chips = [(1 - x, y), (x, 1 - y), (1 - x, 1 - y)]

        def rows(a, px, py, pc):
            return out_refs[a].at[4 * px + 2 * py + pc]

        def copy(a, k, block, to, own=False):
            return pltpu.make_async_remote_copy(
                src_ref=x_refs[a] if own else rows(a, *block), dst_ref=rows(a, *block),
                send_sem=send_sems.at[a, k], recv_sem=recv_sems.at[a, k], device_id=to, device_id_type=MESH)

        mine = [pltpu.make_async_copy(x_refs[a], rows(a, *me), local_sems.at[a]) for a in range(n_a)]
        for cp in mine:
            cp.start()
        first = []
        for j, chip in enumerate(chips):
            first += [copy(a, 1 + j, me, (*chip, c), own=True) for a in range(n_a)]
        first += [copy(a, 0, me, sibling, own=True) for a in range(n_a)]
        for cp in first:
            cp.start()
        passed = []
        for j, chip in enumerate(chips):
            for a in range(n_a):
                copy(a, 1 + j, (*chip, c), me).wait_recv()
                fwd = copy(a, 4 + j, (*chip, c), sibling)
                fwd.start()
                passed.append(fwd)
        for a in range(n_a):
            copy(a, 0, sibling, me).wait_recv()
        for j, chip in enumerate(chips):
            for a in range(n_a):
                copy(a, 4 + j, (*chip, 1 - c), me).wait_recv()
        for cp in first + passed:
            cp.wait_send()
        for cp in mine:
            cp.wait()

    return _pallas_call(
        body, name=name, out_shape=[_sds((N_DEV,) + t.shape, t.dtype) for t in shards],
        in_specs=[ANY] * n_a, out_specs=[ANY] * n_a,
        scratch_shapes=[pltpu.SemaphoreType.DMA((n_a, 7)), pltpu.SemaphoreType.DMA((n_a, 7)),
                        pltpu.SemaphoreType.DMA((n_a,))],
    )(*shards)


def _rs_to_sibling(gs):
    n_a = len(gs)

    def body(*refs):
        g_refs, recv_refs = refs[:n_a], refs[n_a:2 * n_a]
        send_sems, recv_sems = refs[2 * n_a:]
        x, y, c = _position()
        copies = []
        for k in range(4):
            for a in range(n_a):
                copies.append(pltpu.make_async_remote_copy(
                    src_ref=g_refs[a].at[2 * k + (1 - c)], dst_ref=recv_refs[a].at[k],
                    send_sem=send_sems.at[a, k], recv_sem=recv_sems.at[a, k],
                    device_id=(x, y, 1 - c), device_id_type=MESH))
        for cp in copies:
            cp.start()
        for cp in copies:
            cp.wait()

    return _pallas_call(
        body, name="rs_to_sibling", out_shape=[_sds((4,) + g.shape[1:], g.dtype) for g in gs],
        in_specs=[ANY] * n_a, out_specs=[ANY] * n_a,
        scratch_shapes=[pltpu.SemaphoreType.DMA((n_a, 4)), pltpu.SemaphoreType.DMA((n_a, 4))],
    )(*gs)


def _rs_to_chips(pbs):
    n_a = len(pbs)

    def body(*refs):
        p_refs, recv_refs = refs[:n_a], refs[n_a:2 * n_a]
        send_sems, recv_sems = refs[2 * n_a:]
        x, y, c = _position()
        chips = [(1 - x, y), (x, 1 - y), (1 - x, 1 - y)]
        copies = []
        for j, (px, py) in enumerate(chips):
            for a in range(n_a):
                copies.append(pltpu.make_async_remote_copy(
                    src_ref=p_refs[a].at[2 * px + py], dst_ref=recv_refs[a].at[j],
                    send_sem=send_sems.at[a, j], recv_sem=recv_sems.at[a, j],
                    device_id=(px, py, c), device_id_type=MESH))
        for cp in copies:
            cp.start()
        for cp in copies:
            cp.wait()

    return _pallas_call(
        body, name="rs_to_chips", out_shape=[_sds((3,) + p.shape[1:], p.dtype) for p in pbs],
        in_specs=[ANY] * n_a, out_specs=[ANY] * n_a,
        scratch_shapes=[pltpu.SemaphoreType.DMA((n_a, 3)), pltpu.SemaphoreType.DMA((n_a, 3))],
    )(*pbs)


HBM_SPEC = pl.BlockSpec(memory_space=pltpu.HBM)
SEM_SPEC = pl.BlockSpec(memory_space=pltpu.SEMAPHORE)
EFFECT = pltpu.SideEffectType.DATAFLOW_SIDE_EFFECTING
FLIPS = [(0, 0, 1), (1, 0, 0), (0, 1, 0), (1, 1, 0), (1, 0, 1), (0, 1, 1), (1, 1, 1)]


def _plan_gather(src_refs, land_refs):
    x, y, c = _position()
    me = 4 * x + 2 * y + c
    plan = []
    for k, (fx, fy, fc) in enumerate(FLIPS):
        peer = (1 - x if fx else x, 1 - y if fy else y, 1 - c if fc else c)
        plan += [(src_refs[a], land_refs[a].at[me], (a, k), peer) for a in range(len(src_refs))]
    return plan


def _plan_to_sibling(src_refs, land_refs):
    x, y, c = _position()
    return [(src_refs[a].at[2 * k + (1 - c)], land_refs[a].at[k], (a, k), (x, y, 1 - c))
            for k in range(4) for a in range(len(src_refs))]


def _plan_to_chips(src_refs, land_refs):
    x, y, c = _position()
    chips = [(1 - x, y), (x, 1 - y), (1 - x, 1 - y)]
    return [(src_refs[a].at[2 * px + py], land_refs[a].at[j], (a, j), (px, py, c))
            for j, (px, py) in enumerate(chips) for a in range(len(src_refs))]


def _split_start(srcs, lands, plan, n_sem, name):
    n_a = len(srcs)

    def body(*refs):
        src_refs, land_refs = refs[:n_a], refs[n_a:2 * n_a]
        send_sems, recv_sems, token = refs[2 * n_a], refs[2 * n_a + 1], refs[-1]
        for src, dst, (a, k), dev in plan(src_refs, land_refs):
            i = a * n_sem + k
            pltpu.make_async_remote_copy(src_ref=src, dst_ref=dst, send_sem=send_sems.at[i], recv_sem=recv_sems.at[i],
                                         device_id=dev, device_id_type=MESH).start()
        token[...] = jnp.zeros_like(token)

    bufs = list(srcs) + list(lands)
    res = pl.pallas_call(
        body, name=name,
        out_shape=(pltpu.SemaphoreType.DMA((n_a * n_sem,)), pltpu.SemaphoreType.DMA((n_a * n_sem,)),
                   *[pltpu.HBM(t.shape, t.dtype) for t in bufs], _sds((8, LANES), F32)),
        in_specs=[HBM_SPEC] * (2 * n_a),
        out_specs=(SEM_SPEC, SEM_SPEC, *[HBM_SPEC] * (2 * n_a), pl.BlockSpec(memory_space=pltpu.VMEM)),
        input_output_aliases={i: 2 + i for i in range(2 * n_a)},
        compiler_params=pltpu.CompilerParams(has_side_effects=EFFECT),
    )(*[pltpu.with_memory_space_constraint(t, pltpu.HBM) for t in bufs])
    return (res[0], res[1], res[2:2 + n_a], res[2 + n_a:2 + 2 * n_a]), res[-1]


def _split_wait(started, plan, after, name):
    send_sems, recv_sems, srcs, lands = started
    n_a = len(srcs)
    n_sem = send_sems.shape[0] // n_a

    def body(*refs):
        src_refs, land_refs = refs[:n_a], refs[n_a:2 * n_a]
        s_sems, r_sems = refs[2 * n_a], refs[2 * n_a + 1]
        for src, dst, (a, k), dev in plan(src_refs, land_refs):
            i = a * n_sem + k
            cp = pltpu.make_async_remote_copy(src_ref=src, dst_ref=dst, send_sem=s_sems.at[i], recv_sem=r_sems.at[i],
                                              device_id=dev, device_id_type=MESH)
            cp.wait_send()
            cp.wait_recv()

    bufs = list(srcs) + list(lands)
    res = pl.pallas_call(
        body, name=name, out_shape=tuple(pltpu.HBM(t.shape, t.dtype) for t in bufs),
        in_specs=[HBM_SPEC] * (2 * n_a) + [SEM_SPEC, SEM_SPEC, ANY],
        out_specs=tuple([HBM_SPEC] * (2 * n_a)),
        input_output_aliases={i: i for i in range(2 * n_a)},
        compiler_params=pltpu.CompilerParams(has_side_effects=EFFECT),
    )(*bufs, send_sems, recv_sems, after)
    return res[:n_a], res[n_a:]


def _row_tile(r):
    return ROW_TILE if r % ROW_TILE == 0 else r


def _rs_add_sibling(gp, recv, c_arr, name):
    _, r, l = gp.shape
    tr = _row_tile(r)

    def body(c_ref, g_ref, r_ref, pf_ref, pb_ref):
        sm = g_ref[...].astype(F32) + r_ref[...].astype(F32)
        pf_ref[...] = sm
        pb_ref[...] = sm.astype(BF16)

    spec = pl.BlockSpec((None, tr, l), lambda k, i, c: (k, i, 0))
    return _pallas_call(
        body, name=name,
        grid_spec=pltpu.PrefetchScalarGridSpec(
            num_scalar_prefetch=1, grid=(4, r // tr),
            in_specs=[pl.BlockSpec((None, tr, l), lambda k, i, c: (2 * k + c[0], i, 0)), spec],
            out_specs=[spec, spec]),
        out_shape=[_sds((4, r, l), F32), _sds((4, r, l), BF16)], compiler_params=_params(2, 32),
    )(c_arr, gp, recv)


def _adam_update(w, gv, m, v):
    nm = ADAM_B1 * m + (1.0 - ADAM_B1) * gv
    nv = ADAM_B2 * v + (1.0 - ADAM_B2) * (gv * gv)
    m_hat = nm / (1.0 - ADAM_B1 ** ADAM_STEP)
    v_hat = nv / (1.0 - ADAM_B2 ** ADAM_STEP)
    return -ADAM_LR * (m_hat / (jnp.sqrt(v_hat) + ADAM_EPS) + ADAM_WD * w), nm, nv


def _rs_finish_adamw(pf, recv, k_arr, w, m, v, name):
    _, r, l = pf.shape
    tr = _row_tile(r)

    def body(k_ref, p_ref, r_ref, w_ref, m_ref, v_ref, g_ref, d_ref, nm_ref, nv_ref):
        gv = ((p_ref[...] + r_ref[0].astype(F32)) + r_ref[1].astype(F32)) + r_ref[2].astype(F32)
        g_ref[...] = gv
        d_ref[...], nm_ref[...], nv_ref[...] = _adam_update(w_ref[...], gv, m_ref[...], v_ref[...])

    spec = pl.BlockSpec((tr, l), lambda i, k: (i, 0))
    return _pallas_call(
        body, name=name,
        grid_spec=pltpu.PrefetchScalarGridSpec(
            num_scalar_prefetch=1, grid=(r // tr,),
            in_specs=[pl.BlockSpec((None, tr, l), lambda i, k: (k[0], i, 0)),
                      pl.BlockSpec((3, tr, l), lambda i, k: (0, i, 0)), spec, spec, spec],
            out_specs=[spec] * 4),
        out_shape=[_sds((r, l), F32)] * 4, compiler_params=_params(1, 32),
    )(k_arr, pf, recv, w, m, v)


def _sum_devices(g):
    def body(g_ref, o_ref):
        acc = g_ref[0]
        for j in range(1, N_DEV):
            acc = acc + g_ref[j]
        o_ref[...] = acc

    return _pallas_call(body, name="sum_devices", out_shape=_sds(g.shape[1:], F32))(g)


def _adamw(w, g, m, v, name):
    shape = w.shape
    w2, g2, m2, v2 = [t.reshape((-1, shape[-1])) for t in (w, g, m, v)]

    def body(w_ref, g_ref, m_ref, v_ref, d_ref, nm_ref, nv_ref):
        d_ref[...], nm_ref[...], nv_ref[...] = _adam_update(w_ref[...], g_ref[...], m_ref[...], v_ref[...])

    outs = _pallas_call(body, name=name, out_shape=[_sds(w2.shape, F32)] * 3)(w2, g2, m2, v2)
    return tuple(t.reshape(shape) for t in outs)


def _after(t, token):
    return t + token[0:1, 0:1].astype(t.dtype)


def _local_step(x, mem, pos, tgt, norm_g, mem_norm_g, final_g, wg_in0, late_weights, late_token, c_arr):
    tabs = _rope_tables(pos)
    g0, g1 = _after(norm_g[0:1], late_token), norm_g[1:2]

    hn0, hn0_t, qs, ks, vs, tabs_v, qm0, z0 = _inproj_attn(x, g0, wg_in0, tabs)
    os_, ls_ = [], []
    for j, d in enumerate(DILATIONS):
        o, l = _attn_fwd(qs[j], ks[j], vs[j], d)
        os_.append(o)
        ls_.append(l)

    _, gathered = _split_wait(late_weights, _plan_gather, ls_[2], "gather_late_wait")
    wg_out0, wg_in1, wg_out1, wg_kv0, wg_kv1, cw_all = gathered
    w_out1 = wg_out1.reshape(-1, wg_out1.shape[2])
    w_kv = jnp.stack([wg_kv0.reshape(-1, wg_kv0.shape[2]), wg_kv1.reshape(-1, wg_kv1.shape[2])])
    cw = cw_all[:, 0:3].transpose(1, 0, 2).reshape(3, -1)
    kv = _memkv_fwd(mem, mem_norm_g, w_kv)
    y0, y0_t, mo0, h1 = _post_attn(os_, ls_, qm0, kv[0], z0, x, wg_out0)

    hn1, hn1_t, bg, cg, u, qm1, z1 = _inproj_conv(h1, g1, wg_in1)
    y1, y1_t, mo1, dh2, dh2b, loss_acc, d_final_g = _post_conv_loss(
        bg, cg, u, qm1, kv[1], z1, h1, w_out1, cw, final_g.reshape(1, -1), tgt)

    d_w_out1 = _wgrad_rows(y1_t, dh2b, "wgrad_out1")
    dz1, dbg, dconv, dqm1, dkv1 = _bwd_post_conv(dh2b, w_out1, bg, cg, u, z1, qm1, kv[1], mo1, cw)
    dcg, du, dcw = _bwd_conv(dconv, cg, u, cw)
    dh1, dg1, dproj1 = _dgrad_norm([(dbg, 1), (dcg, 1), (du, 1), (dqm1, 1), (dz1, 1)], wg_in1, h1, g1, dh2,
                                   "dgrad_norm_conv")
    d_w_in1 = _wgrad_shards(hn1_t, dproj1, "wgrad_in1")

    grads1 = [d_w_in1, d_w_out1]
    started, token = _split_start(grads1, [lax.empty((4,) + g.shape[1:], g.dtype) for g in grads1],
                                  _plan_to_sibling, 4, "rs1_sibling_start")

    dh1b = _after(dh1, token).astype(BF16)
    d_w_out0 = _wgrad_cols(y0_t, dh1b, wg_out0.shape[2], "wgrad_out0")
    gw = GROUP_WIDTH
    ones = (jnp.arange(gw)[:, None] // HEAD_DIM == jnp.arange(gw)[None, :] // HEAD_DIM).astype(BF16)
    res = _bwd_post_attn(dh1b, wg_out0, z0, os_, ls_, qm0, kv[0], mo0, ones)
    dz0, dos, dls, dqm0, dkv0 = res[0], res[1:4], res[4:7], res[7], res[8]

    grads1, from_sibling = _split_wait(started, _plan_to_sibling, dz0, "rs1_sibling_wait")
    parts1 = [_rs_add_sibling(g, r, c_arr, "rs_add_sibling_" + n)
              for g, r, n in zip(grads1, from_sibling, ("conv_w_in", "conv_w_out"))]
    pbs1 = [pb for _, pb in parts1]
    started, token = _split_start(pbs1, [lax.empty((3,) + p.shape[1:], p.dtype) for p in pbs1],
                                  _plan_to_chips, 3, "rs1_chips_start")

    dqs, dks, dvs = [], [], []
    for j, d in enumerate(DILATIONS):
        do_j = _after(dos[j], token) if j == 0 else dos[j]
        dq, dk, dv = _attn_bwd(qs[j], ks[j], vs[j], ls_[j], do_j, dls[j], tabs_v[j], d)
        dqs.append((dq, d))
        dks.append((dk, d))
        dvs.append((dv, d))
    dx, dg0, dproj0 = _dgrad_norm(dqs + dks + dvs + [(dqm0, 1), (dz0, 1)], wg_in0, x, g0, dh1, "dgrad_norm_attn")
    d_w_in0 = _wgrad_shards(hn0_t, dproj0, "wgrad_in0")
    _, from_chips1 = _split_wait(started, _plan_to_chips, d_w_in0, "rs1_chips_wait")

    d_w_kv, d_mem_g = _memkv_bwd(jnp.stack([dkv0, dkv1]), w_kv, mem, mem_norm_g)
    small = jnp.concatenate([dg0, dg1, d_mem_g.reshape(2, -1), d_final_g, dcw[0:3]], axis=0)
    n_kv = d_w_kv.shape[1] // N_DEV
    grads0 = [d_w_in0, d_w_out0, d_w_kv[0].reshape(N_DEV, n_kv, -1), d_w_kv[1].reshape(N_DEV, n_kv, -1)]
    return loss_acc[0, 0], dx, grads0, small, [pf for pf, _ in parts1], from_chips1


def kernel(x, mem, positions, norm_g, mem_norm_g, w_mem_kv, attn_w_in, attn_w_out, conv_w_in, conv_w, conv_w_out, final_g, loss_target, m_norm_g, m_mem_norm_g, m_w_mem_kv, m_attn_w_in, m_attn_w_out, m_conv_w_in, m_conv_w, m_conv_w_out, m_final_g, v_norm_g, v_mem_norm_g, v_w_mem_kv, v_attn_w_in, v_attn_w_out, v_conv_w_in, v_conv_w, v_conv_w_out, v_final_g):
    px, py, pc = _position()
    me = 4 * px + 2 * py + pc

    c_arr = jnp.reshape(pc, (1,)).astype(jnp.int32)
    k_arr = jnp.reshape(2 * px + py, (1,)).astype(jnp.int32)
    wg_in0 = _all_gather([attn_w_in[0].astype(BF16)], "gather_w_in0")[0]
    late = [attn_w_out[0].astype(BF16), conv_w_in[0].astype(BF16), conv_w_out[0].astype(BF16),
            w_mem_kv[0].astype(BF16), w_mem_kv[1].astype(BF16), jnp.pad(conv_w[0], ((0, 5), (0, 0)))]
    lands = [lax.dynamic_update_slice(lax.empty((N_DEV,) + t.shape, t.dtype), t[None], (me, 0, 0)) for t in late]
    late_weights, late_token = _split_start(late, lands, _plan_gather, len(FLIPS), "gather_late_start")

    loss_part, dx, grads0, small_part, pfs1, from_chips1 = _local_step(
        x[0], mem[0], positions[0], loss_target[0], norm_g, mem_norm_g, final_g, wg_in0, late_weights, late_token, c_arr)

    names0 = ["attn_w_in", "attn_w_out", "w_mem_kv0", "w_mem_kv1"]
    from_sibling = _rs_to_sibling(grads0)
    parts0 = [_rs_add_sibling(g, r, c_arr, "rs_add_sibling_" + n) for g, r, n in zip(grads0, from_sibling, names0)]
    from_chips0 = _rs_to_chips([pb for _, pb in parts0])
    names = names0 + ["conv_w_in", "conv_w_out"]
    pfs = [pf for pf, _ in parts0] + list(pfs1)
    from_chips = list(from_chips0) + list(from_chips1)
    shards = [attn_w_in[0], attn_w_out[0], w_mem_kv[0], w_mem_kv[1], conv_w_in[0], conv_w_out[0]]
    moments = [(m_attn_w_in[0], v_attn_w_in[0]), (m_attn_w_out[0], v_attn_w_out[0]), (m_w_mem_kv[0], v_w_mem_kv[0]),
               (m_w_mem_kv[1], v_w_mem_kv[1]), (m_conv_w_in[0], v_conv_w_in[0]), (m_conv_w_out[0], v_conv_w_out[0])]
    big = {}
    for n, w, pf, r, (m, v) in zip(names, shards, pfs, from_chips, moments):
        big[n] = _rs_finish_adamw(pf, r, k_arr, w, m, v, "rs_finish_adamw_" + n)
    for n in ("attn_w_in", "attn_w_out", "conv_w_in", "conv_w_out"):
        big[n] = tuple(t[None] for t in big[n])
    big["w_mem_kv"] = tuple(jnp.stack([a, b]) for a, b in zip(big["w_mem_kv0"], big["w_mem_kv1"]))

    small_part = jnp.concatenate([small_part, jnp.broadcast_to(loss_part, small_part.shape)], axis=0)
    small = _sum_devices(_all_gather([small_part], "gather_small_grads")[0])
    loss = small[8, 0]
    g_conv_w = lax.dynamic_slice(small[5:8], (0, me * LANES), (3, LANES))[None]
    small_g = dict(norm_g=small[0:2], mem_norm_g=small[2:4], conv_w=g_conv_w, final_g=small[4])
    small_w = dict(norm_g=(norm_g, m_norm_g, v_norm_g), mem_norm_g=(mem_norm_g, m_mem_norm_g, v_mem_norm_g),
                   conv_w=(conv_w, m_conv_w, v_conv_w), final_g=(final_g, m_final_g, v_final_g))
    for n, (w, m, v) in small_w.items():
        big[n] = (small_g[n],) + _adamw(w, small_g[n], m, v, "adamw_" + n)

    order = ["norm_g", "mem_norm_g", "w_mem_kv", "attn_w_in", "attn_w_out", "conv_w_in", "conv_w", "conv_w_out", "final_g"]
    return (loss, dx[None], *[big[n][0] for n in order], *[big[n][1] for n in order],
            *[big[n][2] for n in order], *[big[n][3] for n in order])
```

```python
import functools

import jax
import jax.numpy as jnp
from jax import lax
from jax.experimental import pallas as pl
from jax.experimental.pallas import tpu as pltpu

F32 = jnp.float32
BF16 = jnp.bfloat16

N_DEV = 8
D_MODEL = 1024
HEAD_DIM = 64
ROT_DIM = HEAD_DIM // 4
ROPE_THETA = 500000.0
DILATIONS = (1, 4, 16)
HEADS_PER_GROUP = 8
GROUP_WIDTH = HEADS_PER_GROUP * HEAD_DIM
BLOCK = 128
N_MEM = 256
MEM_HEADS = 4
MEM_WIDTH = MEM_HEADS * HEAD_DIM
CONV_WIDTH = D_MODEL
EPS = 1e-6
SCALE = HEAD_DIM ** -0.5
NEG = -1e30

ADAM_LR = 0.001
ADAM_B1 = 0.9
ADAM_B2 = 0.999
ADAM_EPS = 1e-08
ADAM_WD = 0.01
ADAM_STEP = 10

ROW_TILE = 256
LANES = 128
MESH = pl.DeviceIdType.MESH
ANY = pl.BlockSpec(memory_space=pl.ANY)


def _pallas_call(body, **kw):
    call = pl.pallas_call(body, **kw)

    def run(*args):
        pinned = [pltpu.with_memory_space_constraint(a, pltpu.HBM) if jnp.issubdtype(a.dtype, jnp.floating) else a
                  for a in args]
        return call(*pinned)

    return run


def _dot(a, b):
    return lax.dot_general(a, b, (((1,), (0,)), ((), ())), preferred_element_type=F32)


def _dot_nt(a, b):
    return lax.dot_general(a, b, (((1,), (1,)), ((), ())), preferred_element_type=F32)


def _dot_tn(a, b):
    return lax.dot_general(a, b, (((0,), (0,)), ((), ())), preferred_element_type=F32)


def _params(n_grid, vmem_mb=48):
    return pltpu.CompilerParams(dimension_semantics=("arbitrary",) * n_grid, vmem_limit_bytes=vmem_mb << 20)


def _rows(width, tm=ROW_TILE):
    return pl.BlockSpec((tm, width), lambda i: (i, 0))


def _view_rows(width, d, tm=ROW_TILE):
    return pl.BlockSpec((tm // d, d * width), lambda i: (i, 0))


def _whole(shape):
    return pl.BlockSpec(shape, lambda *_: (0,) * len(shape))


def _sds(shape, dtype):
    return jax.ShapeDtypeStruct(shape, dtype)


def _silu_parts(z):
    sg = jax.nn.sigmoid(z)
    return z * sg, sg * (1.0 + z * (1.0 - sg))


def _to_view(scr, val, out_ref, d):
    tm, w = val.shape
    if d == 1:
        out_ref[...] = val.astype(out_ref.dtype)
        return
    for cb in range(w // LANES):
        scr[cb] = val[:, cb * LANES:(cb + 1) * LANES]
    for r in range(d):
        for cb in range(w // LANES):
            lo = r * w + cb * LANES
            out_ref[:, lo:lo + LANES] = scr[cb, pl.ds(r, tm // d, stride=d), :].astype(out_ref.dtype)


def _from_view(scr, in_ref, d):
    if d == 1:
        return in_ref[...].astype(F32)
    nc, tm, _ = scr.shape
    w = nc * LANES
    for r in range(d):
        for cb in range(nc):
            lo = r * w + cb * LANES
            scr[cb, pl.ds(r, tm // d, stride=d), :] = in_ref[:, lo:lo + LANES].astype(F32)
    return jnp.concatenate([scr[cb] for cb in range(nc)], axis=1)


def _view_scratch(width, tm=ROW_TILE):
    return pltpu.VMEM((width // LANES, tm, LANES), F32)


def _rope_tables(pos):
    half = ROT_DIM // 2
    inv_freq = ROPE_THETA ** (-jnp.arange(half, dtype=F32) * (2.0 / ROT_DIM))
    ang = pos.astype(F32)[:, None] * inv_freq
    cos, sin = jnp.cos(ang), jnp.sin(ang)
    s = pos.shape[0]
    z8 = jnp.zeros((s, half), F32)
    rest = HEAD_DIM - ROT_DIM
    cosf = jnp.concatenate([cos, cos, jnp.ones((s, rest), F32)], axis=1)
    sa = jnp.concatenate([-sin, z8, jnp.zeros((s, rest), F32)], axis=1)
    sb = jnp.concatenate([z8, sin, jnp.zeros((s, rest), F32)], axis=1)
    return tuple(jnp.tile(t, (1, LANES // HEAD_DIM)) for t in (cosf, sa, sb))


def _rope_fwd(t, cv, sav, sbv):
    w = t.shape[1]
    return t * cv + pltpu.roll(t, w - ROT_DIM // 2, 1) * sav + pltpu.roll(t, ROT_DIM // 2, 1) * sbv


def _rope_bwd(g, cv, sav, sbv):
    w = g.shape[1]
    return g * cv + pltpu.roll(g * sav, ROT_DIM // 2, 1) + pltpu.roll(g * sbv, w - ROT_DIM // 2, 1)


def _project(hn, wg_ref, proj_scr):
    c = wg_ref.shape[2]
    for j in range(N_DEV):
        proj_scr[:, j * c:(j + 1) * c] = _dot(hn, wg_ref[j])


def _inproj_attn(x, g, wg, tabs):
    s, d_model = x.shape
    gw = GROUP_WIDTH
    n = N_DEV * wg.shape[2]
    nz = n - 9 * gw - MEM_WIDTH
    reps = gw // LANES
    tm = ROW_TILE

    def body(x_ref, g_ref, w_ref, c_ref, sa_ref, sb_ref, hn_ref, hnt_ref, *rest):
        outs, (proj, scr, tscr) = rest[:-3], rest[-3:]
        q_refs, k_refs, v_refs, t_refs, qm_ref, z_ref = outs[0:3], outs[3:6], outs[6:9], outs[9:18], outs[18], outs[19]
        xb = x_ref[...]
        r = lax.rsqrt(jnp.mean(xb * xb, axis=-1, keepdims=True) + EPS)
        hn = ((xb * r) * g_ref[...]).astype(BF16)
        hn_ref[...] = hn
        hnt_ref[...] = hn.T
        _project(hn, w_ref, proj)
        tab = (c_ref[...], sa_ref[...], sb_ref[...])
        cv, sav, sbv = [jnp.tile(t, (1, reps)) for t in tab]
        for j, d in enumerate(DILATIONS):
            tq = _rope_fwd(proj[:, j * gw:(j + 1) * gw], cv, sav, sbv)
            _to_view(scr, tq * SCALE, q_refs[j], d)
            tk = _rope_fwd(proj[:, (3 + j) * gw:(4 + j) * gw], cv, sav, sbv)
            _to_view(scr, tk, k_refs[j], d)
            _to_view(scr, proj[:, (6 + j) * gw:(7 + j) * gw], v_refs[j], d)
            for i in range(3):
                _to_view(tscr, tab[i], t_refs[3 * j + i], d)
        qm_ref[...] = proj[:, 9 * gw:9 * gw + MEM_WIDTH].astype(BF16)
        z_ref[...] = proj[:, 9 * gw + MEM_WIDTH:]

    views = [_sds((s // d, d * gw), BF16) for d in DILATIONS]
    tviews = [_sds((s // d, d * LANES), F32) for d in DILATIONS for _ in range(3)]
    out_shape = ([_sds((s, d_model), BF16), _sds((d_model, s), BF16)] + views * 3 + tviews
                 + [_sds((s, MEM_WIDTH), BF16), _sds((s, nz), F32)])
    vspecs = [_view_rows(gw, d) for d in DILATIONS]
    tspecs = [_view_rows(LANES, d) for d in DILATIONS for _ in range(3)]
    out_specs = ([_rows(d_model), pl.BlockSpec((d_model, tm), lambda i: (0, i))] + vspecs * 3 + tspecs
                 + [_rows(MEM_WIDTH), _rows(nz)])
    res = _pallas_call(
        body, name="inproj_attn", grid=(s // tm,), out_shape=out_shape,
        in_specs=[_rows(d_model), _whole((1, d_model)), _whole(wg.shape), _rows(LANES), _rows(LANES), _rows(LANES)],
        out_specs=out_specs,
        scratch_shapes=[pltpu.VMEM((tm, n), F32), _view_scratch(gw), _view_scratch(LANES)],
        compiler_params=_params(1, 60),
    )(x, g, wg, *tabs)
    tabs_v = [res[11 + 3 * j:14 + 3 * j] for j in range(3)]
    return res[0], res[1], res[2:5], res[5:8], res[8:11], tabs_v, res[20], res[21]


def _stream_maps(d, nb):
    def cur(n):
        return (n % nb, n // nb)

    def prev(n):
        return (jnp.maximum(n % nb - 1, 0), n // nb)

    return cur, prev


def _band_mask(first):
    qi = lax.broadcasted_iota(jnp.int32, (BLOCK, 2 * BLOCK), 0)
    kj = lax.broadcasted_iota(jnp.int32, (BLOCK, 2 * BLOCK), 1)
    lo = qi + jnp.where(first, 2 * BLOCK, 0)
    return jnp.logical_or(jnp.logical_and(kj < BLOCK, kj >= lo), jnp.logical_and(kj >= BLOCK, (kj - BLOCK) <= qi))


def _attn_fwd(q, k, v, d):
    ln, dw = q.shape
    w = dw // d
    nb = ln // BLOCK
    nblk = d * nb
    cur, prev = _stream_maps(d, nb)

    def body(q_ref, kp_ref, kc_ref, vp_ref, vc_ref, o_ref, lse_ref):
        n = pl.program_id(0)
        valid = _band_mask((n % nb) == 0)
        kk = jnp.concatenate([kp_ref[...], kc_ref[...]], axis=0)
        vv = jnp.concatenate([vp_ref[...], vc_ref[...]], axis=0)
        for h in range(HEADS_PER_GROUP):
            sl = slice(h * HEAD_DIM, (h + 1) * HEAD_DIM)
            sc = jnp.where(valid, _dot_nt(q_ref[:, sl], kk[:, sl]), NEG)
            m = jnp.max(sc, axis=-1, keepdims=True)
            p = jnp.exp(sc - m)
            l = jnp.sum(p, axis=-1, keepdims=True)
            pn = p * (1.0 / l)
            o_ref[:, sl] = _dot(pn.astype(BF16), vv[:, sl])
            lse_ref[:, sl] = jnp.broadcast_to(m + jnp.log(l), (BLOCK, HEAD_DIM))

    blk = lambda f: pl.BlockSpec((BLOCK, w), f)
    return _pallas_call(
        body, name=f"attn_fwd_d{d}", grid=(nblk,),
        out_shape=[_sds((ln, dw), F32)] * 2,
        in_specs=[blk(cur), blk(prev), blk(cur), blk(prev), blk(cur)],
        out_specs=[blk(cur), blk(cur)], compiler_params=_params(1, 32),
    )(q, k, k, v, v)


def _memkv_fwd(mem, g, w):
    n_layers = w.shape[0]

    def body(mem_ref, g_ref, w_ref, kv_ref):
        mb = mem_ref[...]
        r = lax.rsqrt(jnp.mean(mb * mb, axis=-1, keepdims=True) + EPS)
        mn = ((mb * r) * g_ref[...]).astype(BF16)
        kv_ref[...] = _dot(mn, w_ref[...]).astype(BF16)

    return _pallas_call(
        body, name="memkv_fwd", grid=(n_layers,),
        out_shape=_sds((n_layers, N_MEM, 2 * MEM_WIDTH), BF16),
        in_specs=[_whole(mem.shape), pl.BlockSpec((None, 1, D_MODEL), lambda l: (l, 0, 0)),
                  pl.BlockSpec((None, D_MODEL, 2 * MEM_WIDTH), lambda l: (l, 0, 0))],
        out_specs=pl.BlockSpec((None, N_MEM, 2 * MEM_WIDTH), lambda l: (l, 0, 0)),
        compiler_params=_params(1, 32),
    )(mem, g.reshape(n_layers, 1, D_MODEL), w)


def _mix_groups(os_, ls_):
    mx = jnp.maximum(jnp.maximum(ls_[0], ls_[1]), ls_[2])
    es = [jnp.exp(t - mx) for t in ls_]
    inv = 1.0 / (es[0] + es[1] + es[2])
    ws = [e * inv for e in es]
    mix = ws[0] * os_[0] + ws[1] * os_[1] + ws[2] * os_[2]
    return ws, mix


def _mem_probs(qm, km, h):
    sl = slice(h * HEAD_DIM, (h + 1) * HEAD_DIM)
    sc = _dot_nt(qm[:, sl], km[:, sl]) * SCALE
    e = jnp.exp(sc - jnp.max(sc, axis=-1, keepdims=True))
    return e * (1.0 / jnp.sum(e, axis=-1, keepdims=True))


def _mem_attn_into(qm, kv_ref, mo_ref):
    km, vm = kv_ref[:, :MEM_WIDTH], kv_ref[:, MEM_WIDTH:]
    for h in range(MEM_HEADS):
        sl = slice(h * HEAD_DIM, (h + 1) * HEAD_DIM)
        p = _mem_probs(qm, km, h)
        mo_ref[:, sl] = _dot(p.astype(BF16), vm[:, sl])


def _mem_attn_bwd(qm, kv_ref, dmem, dqm_ref, dkv_ref):
    km, vm = kv_ref[:, :MEM_WIDTH], kv_ref[:, MEM_WIDTH:]
    dmb = dmem.astype(BF16)
    for h in range(MEM_HEADS):
        sl = slice(h * HEAD_DIM, (h + 1) * HEAD_DIM)
        slv = slice(MEM_WIDTH + h * HEAD_DIM, MEM_WIDTH + (h + 1) * HEAD_DIM)
        p = _mem_probs(qm, km, h)
        dp = _dot_nt(dmb[:, sl], vm[:, sl])
        ds = (p * (dp - jnp.sum(dp * p, axis=-1, keepdims=True)) * SCALE).astype(BF16)
        dqm_ref[:, sl] = _dot(ds, km[:, sl]).astype(BF16)
        dkv_ref[:, sl] += _dot_tn(ds, qm[:, sl])
        dkv_ref[:, slv] += _dot_tn(p.astype(BF16), dmb[:, sl])


def _post_attn(os_, ls_, qm, kv, z, x, wg_out):
    s, d_model = x.shape
    gw = GROUP_WIDTH
    nb = gw + MEM_WIDTH
    c = wg_out.shape[2]
    tm = ROW_TILE

    def body(o0, o1, o2, l0, l1, l2, qm_ref, kv_ref, z_ref, x_ref, w_ref, y_ref, yt_ref, mo_ref, h_ref, s0, s1):
        ov, lv = [], []
        for o_ref, l_ref, d in zip((o0, o1, o2), (l0, l1, l2), DILATIONS):
            ov.append(_from_view(s0, o_ref, d))
            lv.append(_from_view(s1, l_ref, d))
        _, mix = _mix_groups(ov, lv)
        _mem_attn_into(qm_ref[...], kv_ref, mo_ref)
        sz, _ = _silu_parts(z_ref[...])
        y_ref[:, :gw] = (mix * sz[:, :gw]).astype(BF16)
        y_ref[:, gw:] = (mo_ref[...] * sz[:, gw:]).astype(BF16)
        y = y_ref[...]
        yt_ref[...] = y.T
        for j in range(N_DEV):
            h_ref[:, j * c:(j + 1) * c] = x_ref[:, j * c:(j + 1) * c] + _dot(y, w_ref[j])

    vspecs = [_view_rows(gw, d) for d in DILATIONS]
    return _pallas_call(
        body, name="post_attn", grid=(s // tm,),
        out_shape=[_sds((s, nb), BF16), _sds((nb, s), BF16), _sds((s, MEM_WIDTH), F32), _sds((s, d_model), F32)],
        in_specs=vspecs * 2 + [_rows(MEM_WIDTH), _whole(kv.shape), _rows(nb), _rows(d_model), _whole(wg_out.shape)],
        out_specs=[_rows(nb), pl.BlockSpec((nb, tm), lambda i: (0, i)), _rows(MEM_WIDTH), _rows(d_model)],
        scratch_shapes=[_view_scratch(gw), _view_scratch(gw)],
        compiler_params=_params(1, 40),
    )(*os_, *ls_, qm, kv, z, x, wg_out)


def _inproj_conv(x, g, wg):
    s, d_model = x.shape
    c = CONV_WIDTH
    n = N_DEV * wg.shape[2]
    nz = n - 3 * c - MEM_WIDTH
    tm = ROW_TILE

    def body(x_ref, g_ref, w_ref, hn_ref, hnt_ref, bg_ref, cg_ref, u_ref, qm_ref, z_ref, proj):
        xb = x_ref[...]
        r = lax.rsqrt(jnp.mean(xb * xb, axis=-1, keepdims=True) + EPS)
        hn = ((xb * r) * g_ref[...]).astype(BF16)
        hn_ref[...] = hn
        hnt_ref[...] = hn.T
        _project(hn, w_ref, proj)
        bg_ref[...] = proj[:, 0:c]
        cg_ref[...] = proj[:, c:2 * c]
        u_ref[...] = proj[:, 2 * c:3 * c]
        qm_ref[...] = proj[:, 3 * c:3 * c + MEM_WIDTH].astype(BF16)
        z_ref[...] = proj[:, 3 * c + MEM_WIDTH:]

    return _pallas_call(
        body, name="inproj_conv", grid=(s // tm,),
        out_shape=[_sds((s, d_model), BF16), _sds((d_model, s), BF16)] + [_sds((s, c), F32)] * 3
                  + [_sds((s, MEM_WIDTH), BF16), _sds((s, nz), F32)],
        in_specs=[_rows(d_model), _whole((1, d_model)), _whole(wg.shape)],
        out_specs=[_rows(d_model), pl.BlockSpec((d_model, tm), lambda i: (0, i))] + [_rows(c)] * 3
                  + [_rows(MEM_WIDTH), _rows(nz)],
        scratch_shapes=[pltpu.VMEM((tm, n), F32)],
        compiler_params=_params(1, 60),
    )(x, g, wg)


HALO = 8


def _halo_before(width, tm=ROW_TILE):
    return pl.BlockSpec((HALO, width), lambda i: (jnp.maximum(i * (tm // HALO) - 1, 0), 0))


def _halo_after(width, n_rows, tm=ROW_TILE):
    return pl.BlockSpec((HALO, width), lambda i: (jnp.minimum((i + 1) * (tm // HALO), n_rows // HALO - 1), 0))


def _conv_taps(cg_ref, u_ref, cgh_ref, uh_ref, i):
    a = cg_ref[...] * u_ref[...]
    ah = jnp.where(i > 0, cgh_ref[...] * uh_ref[...], 0.0)
    row = lax.broadcasted_iota(jnp.int32, a.shape, 0)
    a1 = jnp.where(row == 0, ah[HALO - 1:HALO], pltpu.roll(a, 1, 0))
    a2 = jnp.where(row == 0, ah[HALO - 2:HALO - 1], jnp.where(row == 1, ah[HALO - 1:HALO], pltpu.roll(a, 2, 0)))
    return a, a1, a2


def _post_conv_loss(bg, cg, u, qm, kv, z, h1, w_out, cw, gf, tgt):
    s, d = h1.shape
    c = CONV_WIDTH
    nb = c + MEM_WIDTH
    tm = ROW_TILE

    def body(bg_ref, cg_ref, u_ref, cgh_ref, uh_ref, qm_ref, kv_ref, z_ref, h_ref, w_ref, cw_ref, gf_ref, t_ref,
             y_ref, yt_ref, mo_ref, dh_ref, dhb_ref, loss_ref, dgf_ref):
        i = pl.program_id(0)
        a, a1, a2 = _conv_taps(cg_ref, u_ref, cgh_ref, uh_ref, i)
        conv = cw_ref[0:1, :] * a2 + cw_ref[1:2, :] * a1 + cw_ref[2:3, :] * a
        mix = bg_ref[...] * conv
        _mem_attn_into(qm_ref[...], kv_ref, mo_ref)
        sz, _ = _silu_parts(z_ref[...])
        y_ref[:, :c] = (mix * sz[:, :c]).astype(BF16)
        y_ref[:, c:] = (mo_ref[...] * sz[:, c:]).astype(BF16)
        y = y_ref[...]
        yt_ref[...] = y.T
        h2 = h_ref[...] + _dot(y, w_ref[...])
        r = lax.rsqrt(jnp.mean(h2 * h2, axis=-1, keepdims=True) + EPS)
        nh = h2 * r
        gfv = gf_ref[...]
        diff = nh * gfv - t_ref[...]
        dout = diff * (1.0 / d)
        dn = dout * gfv
        dh2 = r * dn - h2 * ((r * r * r) * jnp.mean(dn * h2, axis=-1, keepdims=True))
        dh_ref[...] = dh2
        dhb_ref[...] = dh2.astype(BF16)

        @pl.when(i == 0)
        def _():
            loss_ref[...] = jnp.zeros_like(loss_ref)
            dgf_ref[...] = jnp.zeros_like(dgf_ref)

        loss_ref[...] += 0.5 * jnp.sum(jnp.mean(diff * diff, axis=-1, keepdims=True))
        dgf_ref[...] += jnp.sum(dout * nh, axis=0, keepdims=True)

    return _pallas_call(
        body, name="post_conv_loss", grid=(s // tm,),
        out_shape=[_sds((s, nb), BF16), _sds((nb, s), BF16), _sds((s, MEM_WIDTH), F32), _sds((s, d), F32),
                   _sds((s, d), BF16), _sds((8, LANES), F32), _sds((1, d), F32)],
        in_specs=[_rows(c)] * 3 + [_halo_before(c)] * 2 + [_rows(MEM_WIDTH), _whole(kv.shape), _rows(nb), _rows(d),
                  _whole(w_out.shape), _whole(cw.shape), _whole((1, d)), _rows(d)],
        out_specs=[_rows(nb), pl.BlockSpec((nb, tm), lambda i: (0, i)), _rows(MEM_WIDTH), _rows(d), _rows(d),
                   _whole((8, LANES)), _whole((1, d))],
        compiler_params=_params(1, 48),
    )(bg, cg, u, cg, u, qm, kv, z, h1, w_out, cw, gf, tgt)


def _bwd_post_conv(dhb, w_out, bg, cg, u, z, qm, kv, mo, cw):
    s = dhb.shape[0]
    c = CONV_WIDTH
    nb = c + MEM_WIDTH

    def body(dh_ref, w_ref, bg_ref, cg_ref, u_ref, cgh_ref, uh_ref, z_ref, qm_ref, kv_ref, mo_ref, cw_ref,
             dz_ref, dbg_ref, dc_ref, dqm_ref, dkv_ref):
        i = pl.program_id(0)

        @pl.when(i == 0)
        def _():
            dkv_ref[...] = jnp.zeros_like(dkv_ref)

        dy = _dot_nt(dh_ref[...], w_ref[...])
        sz, dsz = _silu_parts(z_ref[...])
        a, a1, a2 = _conv_taps(cg_ref, u_ref, cgh_ref, uh_ref, i)
        conv = cw_ref[0:1, :] * a2 + cw_ref[1:2, :] * a1 + cw_ref[2:3, :] * a
        bgv = bg_ref[...]
        dz_ref[:, :c] = (dy[:, :c] * (bgv * conv) * dsz[:, :c]).astype(BF16)
        dz_ref[:, c:] = (dy[:, c:] * mo_ref[...] * dsz[:, c:]).astype(BF16)
        dbr = dy * sz
        dmix = dbr[:, :c]
        dbg_ref[...] = (dmix * conv).astype(BF16)
        dc_ref[...] = dmix * bgv
        _mem_attn_bwd(qm_ref[...], kv_ref, dbr[:, c:], dqm_ref, dkv_ref)

    return _pallas_call(
        body, name="bwd_post_conv", grid=(s // ROW_TILE,),
        out_shape=[_sds((s, nb), BF16), _sds((s, c), BF16), _sds((s, c), F32), _sds((s, MEM_WIDTH), BF16),
                   _sds(kv.shape, F32)],
        in_specs=[_rows(D_MODEL), _whole(w_out.shape)] + [_rows(c)] * 3 + [_halo_before(c)] * 2
                 + [_rows(nb), _rows(MEM_WIDTH), _whole(kv.shape), _rows(MEM_WIDTH), _whole(cw.shape)],
        out_specs=[_rows(nb), _rows(c), _rows(c), _rows(MEM_WIDTH), _whole(kv.shape)],
        compiler_params=_params(1, 48),
    )(dhb, w_out, bg, cg, u, cg, u, z, qm, kv, mo, cw)


def _bwd_conv(dconv, cg, u, cw):
    s, c = dconv.shape
    tm = ROW_TILE
    last = s // tm - 1

    def body(dc_ref, dcn_ref, cg_ref, u_ref, cgh_ref, uh_ref, cw_ref, dcg_ref, du_ref, dcw_ref):
        i = pl.program_id(0)

        @pl.when(i == 0)
        def _():
            dcw_ref[...] = jnp.zeros_like(dcw_ref)

        dc = dc_ref[...]
        dcn = jnp.where(i < last, dcn_ref[...], 0.0)
        row = lax.broadcasted_iota(jnp.int32, dc.shape, 0)
        d1 = jnp.where(row == tm - 1, dcn[0:1], pltpu.roll(dc, tm - 1, 0))
        d2 = jnp.where(row == tm - 1, dcn[1:2], jnp.where(row == tm - 2, dcn[0:1], pltpu.roll(dc, tm - 2, 0)))
        da = cw_ref[2:3, :] * dc + cw_ref[1:2, :] * d1 + cw_ref[0:1, :] * d2
        a, a1, a2 = _conv_taps(cg_ref, u_ref, cgh_ref, uh_ref, i)
        dcg_ref[...] = (da * u_ref[...]).astype(BF16)
        du_ref[...] = (da * cg_ref[...]).astype(BF16)
        dcw_ref[0:1, :] += jnp.sum(dc * a2, axis=0, keepdims=True)
        dcw_ref[1:2, :] += jnp.sum(dc * a1, axis=0, keepdims=True)
        dcw_ref[2:3, :] += jnp.sum(dc * a, axis=0, keepdims=True)

    return _pallas_call(
        body, name="bwd_conv", grid=(s // tm,),
        out_shape=[_sds((s, c), BF16), _sds((s, c), BF16), _sds((8, c), F32)],
        in_specs=[_rows(c), _halo_after(c, s), _rows(c), _rows(c), _halo_before(c), _halo_before(c), _whole(cw.shape)],
        out_specs=[_rows(c), _rows(c), _whole((8, c))], compiler_params=_params(1, 40),
    )(dconv, dconv, cg, u, cg, u, cw)


def _dgrad_norm(pieces, wg, h, g, dres, name):
    s, d_model = h.shape
    c = wg.shape[2]
    n = N_DEV * c
    tm = ROW_TILE
    widths = [p.shape[1] // d for p, d in pieces]
    assert sum(widths) == n
    n_p = len(pieces)

    def body(*refs):
        p_refs = refs[:n_p]
        w_ref, h_ref, g_ref, dr_ref, dh_ref, dhb_ref, dg_ref, dpd_ref, dp, scr = refs[n_p:]

        @pl.when(pl.program_id(0) == 0)
        def _():
            dg_ref[...] = jnp.zeros_like(dg_ref)

        off = 0
        for p_ref, (_, d), wd in zip(p_refs, pieces, widths):
            if d == 1:
                dp[:, off:off + wd] = p_ref[...]
            else:
                dp[:, off:off + wd] = _from_view(scr, p_ref, d).astype(BF16)
            off += wd
        dhn = jnp.zeros((tm, d_model), F32)
        for j in range(N_DEV):
            dpj = dp[:, j * c:(j + 1) * c]
            dpd_ref[j] = dpj
            dhn += _dot_nt(dpj, w_ref[j])
        hb = h_ref[...]
        r = lax.rsqrt(jnp.mean(hb * hb, axis=-1, keepdims=True) + EPS)
        dg_ref[...] += jnp.sum(dhn * (hb * r), axis=0, keepdims=True)
        dn = dhn * g_ref[...]
        dh = dr_ref[...] + r * dn - hb * ((r * r * r) * jnp.mean(dn * hb, axis=-1, keepdims=True))
        dh_ref[...] = dh
        dhb_ref[...] = dh.astype(BF16)

    p_specs = [_view_rows(wd, d) for (_, d), wd in zip(pieces, widths)]
    return _pallas_call(
        body, name=name, grid=(s // tm,),
        out_shape=[_sds((s, d_model), F32), _sds((s, d_model), BF16), _sds((1, d_model), F32), _sds((N_DEV, s, c), BF16)],
        in_specs=p_specs + [_whole(wg.shape), _rows(d_model), _whole((1, d_model)), _rows(d_model)],
        out_specs=[_rows(d_model), _rows(d_model), _whole((1, d_model)), pl.BlockSpec((N_DEV, tm, c), lambda i: (0, i, 0))],
        scratch_shapes=[pltpu.VMEM((tm, n), BF16), _view_scratch(GROUP_WIDTH)],
        compiler_params=_params(1, 60),
    )(*[p for p, _ in pieces], wg, h, g, dres)


def _wgrad_shards(a_t, b_dm, name):
    m, s = a_t.shape
    c = b_dm.shape[2]

    def body(a_ref, b_ref, o_ref):
        o_ref[...] = _dot(a_ref[...], b_ref[...]).astype(BF16)

    return _pallas_call(
        body, name=name, grid=(N_DEV,), out_shape=_sds((N_DEV, m, c), BF16),
        in_specs=[_whole(a_t.shape), pl.BlockSpec((None, s, c), lambda j: (j, 0, 0))],
        out_specs=pl.BlockSpec((None, m, c), lambda j: (j, 0, 0)), compiler_params=_params(1, 40),
    )(a_t, b_dm)


def _wgrad_cols(a_t, b, c, name):
    m, s = a_t.shape

    def body(a_ref, b_ref, o_ref):
        o_ref[...] = _dot(a_ref[...], b_ref[...]).astype(BF16)

    return _pallas_call(
        body, name=name, grid=(N_DEV,), out_shape=_sds((N_DEV, m, c), BF16),
        in_specs=[_whole(a_t.shape), pl.BlockSpec((s, c), lambda j: (0, j))],
        out_specs=pl.BlockSpec((None, m, c), lambda j: (j, 0, 0)), compiler_params=_params(1, 40),
    )(a_t, b)


def _wgrad_rows(a_t, b, name):
    m, s = a_t.shape
    n = b.shape[1]
    mr = m // N_DEV

    def body(a_ref, b_ref, o_ref):
        o_ref[...] = _dot(a_ref[...], b_ref[...]).astype(BF16)

    return _pallas_call(
        body, name=name, grid=(N_DEV,), out_shape=_sds((N_DEV, mr, n), BF16),
        in_specs=[pl.BlockSpec((mr, s), lambda j: (j, 0)), _whole(b.shape)],
        out_specs=pl.BlockSpec((None, mr, n), lambda j: (j, 0, 0)), compiler_params=_params(1, 40),
    )(a_t, b)


def _memkv_bwd(dkv, w, mem, g):
    n_layers = w.shape[0]

    def body(dkv_ref, w_ref, mem_ref, g_ref, dw_ref, dg_ref):
        mb = mem_ref[...]
        r = lax.rsqrt(jnp.mean(mb * mb, axis=-1, keepdims=True) + EPS)
        nm = mb * r
        mn = (nm * g_ref[...]).astype(BF16)
        dkvb = dkv_ref[...].astype(BF16)
        dw_ref[...] = _dot_tn(mn, dkvb).astype(BF16)
        dmn = _dot_nt(dkvb, w_ref[...])
        dg_ref[...] = jnp.sum(dmn * nm, axis=0, keepdims=True)

    lay = lambda *shape: pl.BlockSpec((None,) + shape, lambda l: (l, 0, 0))
    return _pallas_call(
        body, name="memkv_bwd", grid=(n_layers,),
        out_shape=[_sds((n_layers, D_MODEL, 2 * MEM_WIDTH), BF16), _sds((n_layers, 1, D_MODEL), F32)],
        in_specs=[lay(N_MEM, 2 * MEM_WIDTH), lay(D_MODEL, 2 * MEM_WIDTH), _whole(mem.shape), lay(1, D_MODEL)],
        out_specs=[lay(D_MODEL, 2 * MEM_WIDTH), lay(1, D_MODEL)], compiler_params=_params(1, 32),
    )(dkv, w, mem, g.reshape(n_layers, 1, D_MODEL))


def _bwd_post_attn(dhb, wg_out, z, os_, ls_, qm, kv, mo, head_ones):
    s = dhb.shape[0]
    gw = GROUP_WIDTH
    nb = gw + MEM_WIDTH
    c = wg_out.shape[2]
    tm = ROW_TILE

    def body(dh_ref, w_ref, z_ref, o0, o1, o2, l0, l1, l2, qm_ref, kv_ref, mo_ref, bd_ref,
             dz_ref, do0, do1, do2, dl0, dl1, dl2, dqm_ref, dkv_ref, s0, s1):
        @pl.when(pl.program_id(0) == 0)
        def _():
            dkv_ref[...] = jnp.zeros_like(dkv_ref)

        dy = jnp.zeros((tm, nb), F32)
        for j in range(N_DEV):
            dy += _dot_nt(dh_ref[:, j * c:(j + 1) * c], w_ref[j])
        ov, lv = [], []
        for o_ref, l_ref, d in zip((o0, o1, o2), (l0, l1, l2), DILATIONS):
            ov.append(_from_view(s0, o_ref, d))
            lv.append(_from_view(s1, l_ref, d))
        ws, mix = _mix_groups(ov, lv)
        sz, dsz = _silu_parts(z_ref[...])
        dz_ref[:, :gw] = (dy[:, :gw] * mix * dsz[:, :gw]).astype(BF16)
        dz_ref[:, gw:] = (dy[:, gw:] * mo_ref[...] * dsz[:, gw:]).astype(BF16)
        dbr = dy * sz
        dmix = dbr[:, :gw]
        t = dmix * mix
        th = t.astype(BF16)
        tl = (t - th.astype(F32)).astype(BF16)
        rs = _dot(th, bd_ref[...]) + _dot(tl, bd_ref[...])
        for wg_, do_ref, dl_ref, d in zip(ws, (do0, do1, do2), (dl0, dl1, dl2), DILATIONS):
            _to_view(s0, wg_ * dmix, do_ref, d)
            _to_view(s1, wg_ * rs, dl_ref, d)
        _mem_attn_bwd(qm_ref[...], kv_ref, dbr[:, gw:], dqm_ref, dkv_ref)

    vspecs = [_view_rows(gw, d) for d in DILATIONS]
    return _pallas_call(
        body, name="bwd_post_attn", grid=(s // tm,),
        out_shape=[_sds((s, nb), BF16)] + [_sds((s // d, d * gw), BF16) for d in DILATIONS]
                  + [_sds((s // d, d * gw), F32) for d in DILATIONS] + [_sds((s, MEM_WIDTH), BF16), _sds(kv.shape, F32)],
        in_specs=[_rows(D_MODEL), _whole(wg_out.shape), _rows(nb)] + vspecs * 2
                 + [_rows(MEM_WIDTH), _whole(kv.shape), _rows(MEM_WIDTH), _whole(head_ones.shape)],
        out_specs=[_rows(nb)] + vspecs * 2 + [_rows(MEM_WIDTH), _whole(kv.shape)],
        scratch_shapes=[_view_scratch(gw), _view_scratch(gw)],
        compiler_params=_params(1, 48),
    )(dhb, wg_out, z, *os_, *ls_, qm, kv, mo, head_ones)


def _attn_bwd(q, k, v, lse, do, dl, tabs, d):
    ln, dw = q.shape
    w = dw // d
    nb = ln // BLOCK
    nblk = d * nb
    reps = w // LANES
    cur, prev = _stream_maps(d, nb)
    qmap = lambda n: cur(jnp.minimum(n, nblk - 1))
    pmap = lambda n: prev(jnp.minimum(n, nblk - 1))
    omap = lambda n: cur(jnp.maximum(n - 1, 0))

    def body(q_ref, kp_ref, kc_ref, vp_ref, vc_ref, l_ref, do_ref, dl_ref, cq, saq, sbq, ck, sak, sbk,
             dq_ref, dk_ref, dv_ref, acck, accv, dqs):
        n = pl.program_id(0)

        @pl.when(n == 0)
        def _():
            acck[...] = jnp.zeros_like(acck)
            accv[...] = jnp.zeros_like(accv)

        @pl.when(n < nblk)
        def _():
            valid = _band_mask((n % nb) == 0)
            kk = jnp.concatenate([kp_ref[...], kc_ref[...]], axis=0)
            vv = jnp.concatenate([vp_ref[...], vc_ref[...]], axis=0)
            for h in range(HEADS_PER_GROUP):
                sl = slice(h * HEAD_DIM, (h + 1) * HEAD_DIM)
                col = slice(h * HEAD_DIM, h * HEAD_DIM + 1)
                qh = q_ref[:, sl]
                dob = do_ref[:, sl]
                sc = jnp.where(valid, _dot_nt(qh, kk[:, sl]), NEG)
                p = jnp.exp(sc - l_ref[:, col])
                dp = _dot_nt(dob, vv[:, sl])
                ds = (p * (dp - dl_ref[:, col])).astype(BF16)
                dqs[:, sl] = _dot(ds, kk[:, sl]) * SCALE
                acck[:, sl] += _dot_tn(ds, qh)
                accv[:, sl] += _dot_tn(p.astype(BF16), dob)
            tq = [jnp.tile(r[...], (1, reps)) for r in (cq, saq, sbq)]
            dq_ref[...] = _rope_bwd(dqs[...], *tq).astype(BF16)

        tk = [jnp.tile(r[...], (1, reps)) for r in (ck, sak, sbk)]
        dk_ref[...] = _rope_bwd(acck[0:BLOCK, :], *tk).astype(BF16)
        dv_ref[...] = accv[0:BLOCK, :].astype(BF16)
        acck[0:BLOCK, :] = acck[BLOCK:, :]
        accv[0:BLOCK, :] = accv[BLOCK:, :]
        acck[BLOCK:, :] = jnp.zeros((BLOCK, w), F32)
        accv[BLOCK:, :] = jnp.zeros((BLOCK, w), F32)

    blk = lambda f: pl.BlockSpec((BLOCK, w), f)
    tblk = lambda f: pl.BlockSpec((BLOCK, LANES), f)
    return _pallas_call(
        body, name=f"attn_bwd_d{d}", grid=(nblk + 1,),
        out_shape=[_sds((ln, dw), BF16)] * 3,
        in_specs=[blk(qmap), blk(pmap), blk(qmap), blk(pmap), blk(qmap), blk(qmap), blk(qmap), blk(qmap)]
                 + [tblk(qmap)] * 3 + [tblk(omap)] * 3,
        out_specs=[blk(qmap), blk(omap), blk(omap)],
        scratch_shapes=[pltpu.VMEM((2 * BLOCK, w), F32), pltpu.VMEM((2 * BLOCK, w), F32), pltpu.VMEM((BLOCK, w), F32)],
        compiler_params=_params(1, 32),
    )(q, k, k, v, v, lse, do, dl, *tabs, *tabs)


def _position():
    return lax.axis_index("x"), lax.axis_index("y"), lax.axis_index("c")


def _all_gather(shards, name):
    n_a = len(shards)

    def body(*refs):
        x_refs, out_refs = refs[:n_a], refs[n_a:2 * n_a]
        send_sems, recv_sems, local_sems = refs[2 * n_a:]
        x, y, c = _position()
        me, sibling = (x, y, c), (x, y, 1 - c)
        chips = [(1 - x, y), (x, 1 - y), (1 - x, 1 - y)]

        def rows(a, px, py, pc):
            return out_refs[a].at[4 * px + 2 * py + pc]

        def copy(a, k, block, to, own=False):
            return pltpu.make_async_remote_copy(
                src_ref=x_refs[a] if own else rows(a, *block), dst_ref=rows(a, *block),
                send_sem=send_sems.at[a, k], recv_sem=recv_sems.at[a, k], device_id=to, device_id_type=MESH)

        mine = [pltpu.make_async_copy(x_refs[a], rows(a, *me), local_sems.at[a]) for a in range(n_a)]
        for cp in mine:
            cp.start()
        first = []
        for j, chip in enumerate(chips):
            first += [copy(a, 1 + j, me, (*chip, c), own=True) for a in range(n_a)]
        first += [copy(a, 0, me, sibling, own=True) for a in range(n_a)]
        for cp in first:
            cp.start()
        passed = []
        for j, chip in enumerate(chips):
            for a in range(n_a):
                copy(a, 1 + j, (*chip, c), me).wait_recv()
                fwd = copy(a, 4 + j, (*chip, c), sibling)
                fwd.start()
                passed.append(fwd)
        for a in range(n_a):
            copy(a, 0, sibling, me).wait_recv()
        for j, chip in enumerate(chips):
            for a in range(n_a):
                copy(a, 4 + j, (*chip, 1 - c), me).wait_recv()
        for cp in first + passed:
            cp.wait_send()
        for cp in mine:
            cp.wait()

    return _pallas_call(
        body, name=name, out_shape=[_sds((N_DEV,) + t.shape, t.dtype) for t in shards],
        in_specs=[ANY] * n_a, out_specs=[ANY] * n_a,
        scratch_shapes=[pltpu.SemaphoreType.DMA((n_a, 7)), pltpu.SemaphoreType.DMA((n_a, 7)),
                        pltpu.SemaphoreType.DMA((n_a,))],
    )(*shards)


def _rs_to_sibling(gs):
    n_a = len(gs)

    def body(*refs):
        g_refs, recv_refs = refs[:n_a], refs[n_a:2 * n_a]
        send_sems, recv_sems = refs[2 * n_a:]
        x, y, c = _position()
        copies = []
        for k in range(4):
            for a in range(n_a):
                copies.append(pltpu.make_async_remote_copy(
                    src_ref=g_refs[a].at[2 * k + (1 - c)], dst_ref=recv_refs[a].at[k],
                    send_sem=send_sems.at[a, k], recv_sem=recv_sems.at[a, k],
                    device_id=(x, y, 1 - c), device_id_type=MESH))
        for cp in copies:
            cp.start()
        for cp in copies:
            cp.wait()

    return _pallas_call(
        body, name="rs_to_sibling", out_shape=[_sds((4,) + g.shape[1:], g.dtype) for g in gs],
        in_specs=[ANY] * n_a, out_specs=[ANY] * n_a,
        scratch_shapes=[pltpu.SemaphoreType.DMA((n_a, 4)), pltpu.SemaphoreType.DMA((n_a, 4))],
    )(*gs)


def _rs_to_chips(pbs):
    n_a = len(pbs)

    def body(*refs):
        p_refs, recv_refs = refs[:n_a], refs[n_a:2 * n_a]
        send_sems, recv_sems = refs[2 * n_a:]
        x, y, c = _position()
        chips = [(1 - x, y), (x, 1 - y), (1 - x, 1 - y)]
        copies = []
        for j, (px, py) in enumerate(chips):
            for a in range(n_a):
                copies.append(pltpu.make_async_remote_copy(
                    src_ref=p_refs[a].at[2 * px + py], dst_ref=recv_refs[a].at[j],
                    send_sem=send_sems.at[a, j], recv_sem=recv_sems.at[a, j],
                    device_id=(px, py, c), device_id_type=MESH))
        for cp in copies:
            cp.start()
        for cp in copies:
            cp.wait()

    return _pallas_call(
        body, name="rs_to_chips", out_shape=[_sds((3,) + p.shape[1:], p.dtype) for p in pbs],
        in_specs=[ANY] * n_a, out_specs=[ANY] * n_a,
        scratch_shapes=[pltpu.SemaphoreType.DMA((n_a, 3)), pltpu.SemaphoreType.DMA((n_a, 3))],
    )(*pbs)


HBM_SPEC = pl.BlockSpec(memory_space=pltpu.HBM)
SEM_SPEC = pl.BlockSpec(memory_space=pltpu.SEMAPHORE)
EFFECT = pltpu.SideEffectType.DATAFLOW_SIDE_EFFECTING
def _plan_gather_own(src_refs, land_refs):
    x, y, c = _position()
    me = 4 * x + 2 * y + c
    peers = [(x, y, 1 - c), (1 - x, y, c), (x, 1 - y, c), (1 - x, 1 - y, c)]
    return [(src_refs[a], land_refs[a].at[me], (a, k), peer) for k, peer in enumerate(peers) for a in range(len(src_refs))]


def _plan_gather_pass(src_refs, land_refs):
    x, y, c = _position()
    chips = [(1 - x, y), (x, 1 - y), (1 - x, 1 - y)]
    return [(land_refs[a].at[4 * px + 2 * py + c], land_refs[a].at[4 * px + 2 * py + c], (a, j), (x, y, 1 - c))
            for j, (px, py) in enumerate(chips) for a in range(len(land_refs))]


def _plan_to_sibling(src_refs, land_refs):
    x, y, c = _position()
    return [(src_refs[a].at[2 * k + (1 - c)], land_refs[a].at[k], (a, k), (x, y, 1 - c))
            for k in range(4) for a in range(len(src_refs))]


def _plan_to_chips(src_refs, land_refs):
    x, y, c = _position()
    chips = [(1 - x, y), (x, 1 - y), (1 - x, 1 - y)]
    return [(src_refs[a].at[2 * px + py], land_refs[a].at[j], (a, j), (px, py, c))
            for j, (px, py) in enumerate(chips) for a in range(len(src_refs))]


def _split_start(srcs, lands, plan, n_sem, after, name):
    n_s, n_a = len(srcs), len(lands)
    n_b = n_s + n_a

    def body(*refs):
        src_refs, land_refs = refs[:n_s], refs[n_s:n_b]
        send_sems, recv_sems, token = refs[n_b + 1], refs[n_b + 2], refs[-1]
        for src, dst, (a, k), dev in plan(src_refs, land_refs):
            i = a * n_sem + k
            pltpu.make_async_remote_copy(src_ref=src, dst_ref=dst, send_sem=send_sems.at[i], recv_sem=recv_sems.at[i],
                                         device_id=dev, device_id_type=MESH).start()
        token[...] = jnp.zeros_like(token)

    bufs = list(srcs) + list(lands)
    res = pl.pallas_call(
        body, name=name,
        out_shape=(pltpu.SemaphoreType.DMA((n_a * n_sem,)), pltpu.SemaphoreType.DMA((n_a * n_sem,)),
                   *[pltpu.HBM(t.shape, t.dtype) for t in bufs], _sds((8, LANES), F32)),
        in_specs=[HBM_SPEC] * n_b + [ANY],
        out_specs=(SEM_SPEC, SEM_SPEC, *[HBM_SPEC] * n_b, pl.BlockSpec(memory_space=pltpu.VMEM)),
        input_output_aliases={i: 2 + i for i in range(n_b)},
        compiler_params=pltpu.CompilerParams(has_side_effects=EFFECT),
    )(*[pltpu.with_memory_space_constraint(t, pltpu.HBM) for t in bufs], after)
    return (res[0], res[1], res[2:2 + n_s], res[2 + n_s:2 + n_b]), res[-1]


def _split_wait(started, plan, after, name):
    send_sems, recv_sems, srcs, lands = started
    n_s, n_a = len(srcs), len(lands)
    n_b = n_s + n_a
    n_sem = send_sems.shape[0] // n_a

    def body(*refs):
        src_refs, land_refs = refs[:n_s], refs[n_s:n_b]
        s_sems, r_sems = refs[n_b], refs[n_b + 1]
        for src, dst, (a, k), dev in plan(src_refs, land_refs):
            i = a * n_sem + k
            cp = pltpu.make_async_remote_copy(src_ref=src, dst_ref=dst, send_sem=s_sems.at[i], recv_sem=r_sems.at[i],
                                              device_id=dev, device_id_type=MESH)
            cp.wait_send()
            cp.wait_recv()

    bufs = list(srcs) + list(lands)
    res = pl.pallas_call(
        body, name=name, out_shape=tuple(pltpu.HBM(t.shape, t.dtype) for t in bufs),
        in_specs=[HBM_SPEC] * n_b + [SEM_SPEC, SEM_SPEC, ANY],
        out_specs=tuple([HBM_SPEC] * n_b),
        input_output_aliases={i: i for i in range(n_b)},
        compiler_params=pltpu.CompilerParams(has_side_effects=EFFECT),
    )(*bufs, send_sems, recv_sems, after)
    return res[:n_s], res[n_s:]


def _row_tile(r):
    return ROW_TILE if r % ROW_TILE == 0 else r


def _rs_add_sibling(gp, recv, c_arr, name):
    _, r, l = gp.shape
    tr = _row_tile(r)

    def body(c_ref, g_ref, r_ref, pf_ref, pb_ref):
        sm = g_ref[...].astype(F32) + r_ref[...].astype(F32)
        pf_ref[...] = sm
        pb_ref[...] = sm.astype(BF16)

    spec = pl.BlockSpec((None, tr, l), lambda k, i, c: (k, i, 0))
    return _pallas_call(
        body, name=name,
        grid_spec=pltpu.PrefetchScalarGridSpec(
            num_scalar_prefetch=1, grid=(4, r // tr),
            in_specs=[pl.BlockSpec((None, tr, l), lambda k, i, c: (2 * k + c[0], i, 0)), spec],
            out_specs=[spec, spec]),
        out_shape=[_sds((4, r, l), F32), _sds((4, r, l), BF16)], compiler_params=_params(2, 32),
    )(c_arr, gp, recv)


def _adam_update(w, gv, m, v):
    nm = ADAM_B1 * m + (1.0 - ADAM_B1) * gv
    nv = ADAM_B2 * v + (1.0 - ADAM_B2) * (gv * gv)
    m_hat = nm / (1.0 - ADAM_B1 ** ADAM_STEP)
    v_hat = nv / (1.0 - ADAM_B2 ** ADAM_STEP)
    return -ADAM_LR * (m_hat / (jnp.sqrt(v_hat) + ADAM_EPS) + ADAM_WD * w), nm, nv


def _rs_finish_adamw(pf, recv, k_arr, w, m, v, name):
    _, r, l = pf.shape
    tr = _row_tile(r)

    def body(k_ref, p_ref, r_ref, w_ref, m_ref, v_ref, g_ref, d_ref, nm_ref, nv_ref):
        gv = ((p_ref[...] + r_ref[0].astype(F32)) + r_ref[1].astype(F32)) + r_ref[2].astype(F32)
        g_ref[...] = gv
        d_ref[...], nm_ref[...], nv_ref[...] = _adam_update(w_ref[...], gv, m_ref[...], v_ref[...])

    spec = pl.BlockSpec((tr, l), lambda i, k: (i, 0))
    return _pallas_call(
        body, name=name,
        grid_spec=pltpu.PrefetchScalarGridSpec(
            num_scalar_prefetch=1, grid=(r // tr,),
            in_specs=[pl.BlockSpec((None, tr, l), lambda i, k: (k[0], i, 0)),
                      pl.BlockSpec((3, tr, l), lambda i, k: (0, i, 0)), spec, spec, spec],
            out_specs=[spec] * 4),
        out_shape=[_sds((r, l), F32)] * 4, compiler_params=_params(1, 32),
    )(k_arr, pf, recv, w, m, v)


def _sum_devices(g):
    def body(g_ref, o_ref):
        acc = g_ref[0]
        for j in range(1, N_DEV):
            acc = acc + g_ref[j]
        o_ref[...] = acc

    return _pallas_call(body, name="sum_devices", out_shape=_sds(g.shape[1:], F32))(g)


def _adamw(w, g, m, v, name):
    shape = w.shape
    w2, g2, m2, v2 = [t.reshape((-1, shape[-1])) for t in (w, g, m, v)]

    def body(w_ref, g_ref, m_ref, v_ref, d_ref, nm_ref, nv_ref):
        d_ref[...], nm_ref[...], nv_ref[...] = _adam_update(w_ref[...], g_ref[...], m_ref[...], v_ref[...])

    outs = _pallas_call(body, name=name, out_shape=[_sds(w2.shape, F32)] * 3)(w2, g2, m2, v2)
    return tuple(t.reshape(shape) for t in outs)


def _after(t, token):
    return t + token[0:1, 0:1].astype(t.dtype)


def _local_step(x, mem, pos, tgt, norm_g, mem_norm_g, final_g, wg_in0, late_weights, late_token, c_arr):
    tabs = _rope_tables(pos)
    g0, g1 = _after(norm_g[0:1], late_token), norm_g[1:2]

    hn0, hn0_t, qs, ks, vs, tabs_v, qm0, z0 = _inproj_attn(x, g0, wg_in0, tabs)
    os_, ls_ = [], []
    for j, d in enumerate(DILATIONS):
        if j == 1:
            _, lands = _split_wait(late_weights, _plan_gather_own, ls_[0], "gather_late_wait")
            late_weights, late_token = _split_start([], lands, _plan_gather_pass, 3, ls_[0], "gather_late_pass_start")
        q_j = _after(qs[j], late_token) if j == 1 else qs[j]
        o, l = _attn_fwd(q_j, ks[j], vs[j], d)
        os_.append(o)
        ls_.append(l)

    _, gathered = _split_wait(late_weights, _plan_gather_pass, ls_[2], "gather_late_pass_wait")
    wg_out0, wg_in1, wg_out1, wg_kv0, wg_kv1, cw_all = gathered
    w_out1 = wg_out1.reshape(-1, wg_out1.shape[2])
    w_kv = jnp.stack([wg_kv0.reshape(-1, wg_kv0.shape[2]), wg_kv1.reshape(-1, wg_kv1.shape[2])])
    cw = cw_all[:, 0:3].transpose(1, 0, 2).reshape(3, -1)
    kv = _memkv_fwd(mem, mem_norm_g, w_kv)
    y0, y0_t, mo0, h1 = _post_attn(os_, ls_, qm0, kv[0], z0, x, wg_out0)

    hn1, hn1_t, bg, cg, u, qm1, z1 = _inproj_conv(h1, g1, wg_in1)
    y1, y1_t, mo1, dh2, dh2b, loss_acc, d_final_g = _post_conv_loss(
        bg, cg, u, qm1, kv[1], z1, h1, w_out1, cw, final_g.reshape(1, -1), tgt)

    d_w_out1 = _wgrad_rows(y1_t, dh2b, "wgrad_out1")
    dz1, dbg, dconv, dqm1, dkv1 = _bwd_post_conv(dh2b, w_out1, bg, cg, u, z1, qm1, kv[1], mo1, cw)
    dcg, du, dcw = _bwd_conv(dconv, cg, u, cw)
    dh1, dh1b, dg1, dproj1 = _dgrad_norm([(dbg, 1), (dcg, 1), (du, 1), (dqm1, 1), (dz1, 1)], wg_in1, h1, g1, dh2,
                                         "dgrad_norm_conv")
    d_w_in1 = _wgrad_shards(hn1_t, dproj1, "wgrad_in1")

    grads1 = [d_w_in1, d_w_out1]
    started, token = _split_start(grads1, [lax.empty((4,) + g.shape[1:], g.dtype) for g in grads1],
                                  _plan_to_sibling, 4, dg1, "rs1_sibling_start")

    d_w_out0 = _wgrad_cols(y0_t, dh1b, wg_out0.shape[2], "wgrad_out0")
    gw = GROUP_WIDTH
    ones = (jnp.arange(gw)[:, None] // HEAD_DIM == jnp.arange(gw)[None, :] // HEAD_DIM).astype(BF16)
    ones = _after(ones, token)
    res = _bwd_post_attn(dh1b, wg_out0, z0, os_, ls_, qm0, kv[0], mo0, ones)
    dz0, dos, dls, dqm0, dkv0 = res[0], res[1:4], res[4:7], res[7], res[8]

    grads1, from_sibling = _split_wait(started, _plan_to_sibling, dz0, "rs1_sibling_wait")
    parts1 = [_rs_add_sibling(g, r, c_arr, "rs_add_sibling_" + n)
              for g, r, n in zip(grads1, from_sibling, ("conv_w_in", "conv_w_out"))]
    pbs1 = [pb for _, pb in parts1]
    started, token = _split_start(pbs1, [lax.empty((3,) + p.shape[1:], p.dtype) for p in pbs1],
                                  _plan_to_chips, 3, dg1, "rs1_chips_start")

    dqs, dks, dvs = [], [], []
    for j, d in enumerate(DILATIONS):
        do_j = _after(dos[j], token) if j == 0 else dos[j]
        dq, dk, dv = _attn_bwd(qs[j], ks[j], vs[j], ls_[j], do_j, dls[j], tabs_v[j], d)
        dqs.append((dq, d))
        dks.append((dk, d))
        dvs.append((dv, d))
    dx, _, dg0, dproj0 = _dgrad_norm(dqs + dks + dvs + [(dqm0, 1), (dz0, 1)], wg_in0, x, g0, dh1, "dgrad_norm_attn")
    d_w_in0 = _wgrad_shards(hn0_t, dproj0, "wgrad_in0")
    _, from_chips1 = _split_wait(started, _plan_to_chips, d_w_in0, "rs1_chips_wait")

    d_w_kv, d_mem_g = _memkv_bwd(jnp.stack([dkv0, dkv1]), w_kv, mem, mem_norm_g)
    small = jnp.concatenate([dg0, dg1, d_mem_g.reshape(2, -1), d_final_g, dcw[0:3]], axis=0)
    n_kv = d_w_kv.shape[1] // N_DEV
    grads0 = [d_w_in0, d_w_out0, d_w_kv[0].reshape(N_DEV, n_kv, -1), d_w_kv[1].reshape(N_DEV, n_kv, -1)]
    return loss_acc[0, 0], dx, grads0, small, [pf for pf, _ in parts1], from_chips1


def kernel(x, mem, positions, norm_g, mem_norm_g, w_mem_kv, attn_w_in, attn_w_out, conv_w_in, conv_w, conv_w_out, final_g, loss_target, m_norm_g, m_mem_norm_g, m_w_mem_kv, m_attn_w_in, m_attn_w_out, m_conv_w_in, m_conv_w, m_conv_w_out, m_final_g, v_norm_g, v_mem_norm_g, v_w_mem_kv, v_attn_w_in, v_attn_w_out, v_conv_w_in, v_conv_w, v_conv_w_out, v_final_g):
    px, py, pc = _position()
    me = 4 * px + 2 * py + pc

    c_arr = jnp.reshape(pc, (1,)).astype(jnp.int32)
    k_arr = jnp.reshape(2 * px + py, (1,)).astype(jnp.int32)
    wg_in0 = _all_gather([attn_w_in[0].astype(BF16)], "gather_w_in0")[0]
    late = [attn_w_out[0].astype(BF16), conv_w_in[0].astype(BF16), conv_w_out[0].astype(BF16),
            w_mem_kv[0].astype(BF16), w_mem_kv[1].astype(BF16), jnp.pad(conv_w[0], ((0, 5), (0, 0)))]
    lands = [lax.dynamic_update_slice(lax.empty((N_DEV,) + t.shape, t.dtype), t[None], (me, 0, 0)) for t in late]
    late_weights, late_token = _split_start(late, lands, _plan_gather_own, 4, wg_in0, "gather_late_start")

    loss_part, dx, grads0, small_part, pfs1, from_chips1 = _local_step(
        x[0], mem[0], positions[0], loss_target[0], norm_g, mem_norm_g, final_g, wg_in0, late_weights, late_token, c_arr)

    names0 = ["attn_w_in", "attn_w_out", "w_mem_kv0", "w_mem_kv1"]
    from_sibling = _rs_to_sibling(grads0)
    parts0 = [_rs_add_sibling(g, r, c_arr, "rs_add_sibling_" + n) for g, r, n in zip(grads0, from_sibling, names0)]
    from_chips0 = _rs_to_chips([pb for _, pb in parts0])
    names = names0 + ["conv_w_in", "conv_w_out"]
    pfs = [pf for pf, _ in parts0] + list(pfs1)
    from_chips = list(from_chips0) + list(from_chips1)
    shards = [attn_w_in[0], attn_w_out[0], w_mem_kv[0], w_mem_kv[1], conv_w_in[0], conv_w_out[0]]
    moments = [(m_attn_w_in[0], v_attn_w_in[0]), (m_attn_w_out[0], v_attn_w_out[0]), (m_w_mem_kv[0], v_w_mem_kv[0]),
               (m_w_mem_kv[1], v_w_mem_kv[1]), (m_conv_w_in[0], v_conv_w_in[0]), (m_conv_w_out[0], v_conv_w_out[0])]
    big = {}
    for n, w, pf, r, (m, v) in zip(names, shards, pfs, from_chips, moments):
        big[n] = _rs_finish_adamw(pf, r, k_arr, w, m, v, "rs_finish_adamw_" + n)
    for n in ("attn_w_in", "attn_w_out", "conv_w_in", "conv_w_out"):
        big[n] = tuple(t[None] for t in big[n])
    big["w_mem_kv"] = tuple(jnp.stack([a, b]) for a, b in zip(big["w_mem_kv0"], big["w_mem_kv1"]))

    small_part = jnp.concatenate([small_part, jnp.broadcast_to(loss_part, small_part.shape)], axis=0)
    small = _sum_devices(_all_gather([small_part], "gather_small_grads")[0])
    loss = small[8, 0]
    g_conv_w = lax.dynamic_slice(small[5:8], (0, me * LANES), (3, LANES))[None]
    small_g = dict(norm_g=small[0:2], mem_norm_g=small[2:4], conv_w=g_conv_w, final_g=small[4])
    small_w = dict(norm_g=(norm_g, m_norm_g, v_norm_g), mem_norm_g=(mem_norm_g, m_mem_norm_g, v_mem_norm_g),
                   conv_w=(conv_w, m_conv_w, v_conv_w), final_g=(final_g, m_final_g, v_final_g))
    for n, (w, m, v) in small_w.items():
        big[n] = (small_g[n],) + _adamw(w, small_g[n], m, v, "adamw_" + n)

    order = ["norm_g", "mem_norm_g", "w_mem_kv", "attn_w_in", "attn_w_out", "conv_w_in", "conv_w", "conv_w_out", "final_g"]
    return (loss, dx[None], *[big[n][0] for n in order], *[big[n][1] for n in order],
            *[big[n][2] for n in order], *[big[n][3] for n in order])
```

```python
import functools

import jax
import jax.numpy as jnp
from jax import lax
from jax.experimental import pallas as pl
from jax.experimental.pallas import tpu as pltpu

F32 = jnp.float32
BF16 = jnp.bfloat16

N_DEV = 8
D_MODEL = 1024
HEAD_DIM = 64
ROT_DIM = HEAD_DIM // 4
ROPE_THETA = 500000.0
DILATIONS = (1, 4, 16)
HEADS_PER_GROUP = 8
GROUP_WIDTH = HEADS_PER_GROUP * HEAD_DIM
BLOCK = 128
N_MEM = 256
MEM_HEADS = 4
MEM_WIDTH = MEM_HEADS * HEAD_DIM
CONV_WIDTH = D_MODEL
EPS = 1e-6
SCALE = HEAD_DIM ** -0.5
NEG = -1e30

ADAM_LR = 0.001
ADAM_B1 = 0.9
ADAM_B2 = 0.999
ADAM_EPS = 1e-08
ADAM_WD = 0.01
ADAM_STEP = 10

ROW_TILE = 256
LANES = 128
MESH = pl.DeviceIdType.MESH
ANY = pl.BlockSpec(memory_space=pl.ANY)


def _pallas_call(body, **kw):
    call = pl.pallas_call(body, **kw)

    def run(*args):
        pinned = [pltpu.with_memory_space_constraint(a, pltpu.HBM) if jnp.issubdtype(a.dtype, jnp.floating) else a
                  for a in args]
        return call(*pinned)

    return run


def _dot(a, b):
    return lax.dot_general(a, b, (((1,), (0,)), ((), ())), preferred_element_type=F32)


def _dot_nt(a, b):
    return lax.dot_general(a, b, (((1,), (1,)), ((), ())), preferred_element_type=F32)


def _dot_tn(a, b):
    return lax.dot_general(a, b, (((0,), (0,)), ((), ())), preferred_element_type=F32)


def _params(n_grid, vmem_mb=48):
    return pltpu.CompilerParams(dimension_semantics=("arbitrary",) * n_grid, vmem_limit_bytes=vmem_mb << 20)


def _rows(width, tm=ROW_TILE):
    return pl.BlockSpec((tm, width), lambda i: (i, 0))


def _view_rows(width, d, tm=ROW_TILE):
    return pl.BlockSpec((tm // d, d * width), lambda i: (i, 0))


def _whole(shape):
    return pl.BlockSpec(shape, lambda *_: (0,) * len(shape))


def _sds(shape, dtype):
    return pltpu.HBM(shape, dtype)


def _plain(shape, dtype):
    return jax.ShapeDtypeStruct(shape, dtype)


def _silu_parts(z):
    sg = jax.nn.sigmoid(z)
    return z * sg, sg * (1.0 + z * (1.0 - sg))


def _to_view(scr, val, out_ref, d):
    tm, w = val.shape
    if d == 1:
        out_ref[...] = val.astype(out_ref.dtype)
        return
    for cb in range(w // LANES):
        scr[cb] = val[:, cb * LANES:(cb + 1) * LANES]
    for r in range(d):
        for cb in range(w // LANES):
            lo = r * w + cb * LANES
            out_ref[:, lo:lo + LANES] = scr[cb, pl.ds(r, tm // d, stride=d), :].astype(out_ref.dtype)


def _from_view(scr, in_ref, d):
    if d == 1:
        return in_ref[...].astype(F32)
    nc, tm, _ = scr.shape
    w = nc * LANES
    for r in range(d):
        for cb in range(nc):
            lo = r * w + cb * LANES
            scr[cb, pl.ds(r, tm // d, stride=d), :] = in_ref[:, lo:lo + LANES].astype(F32)
    return jnp.concatenate([scr[cb] for cb in range(nc)], axis=1)


def _view_scratch(width, tm=ROW_TILE):
    return pltpu.VMEM((width // LANES, tm, LANES), F32)


def _rope_tables(pos):
    half = ROT_DIM // 2
    inv_freq = ROPE_THETA ** (-jnp.arange(half, dtype=F32) * (2.0 / ROT_DIM))
    ang = pos.astype(F32)[:, None] * inv_freq
    cos, sin = jnp.cos(ang), jnp.sin(ang)
    s = pos.shape[0]
    z8 = jnp.zeros((s, half), F32)
    rest = HEAD_DIM - ROT_DIM
    cosf = jnp.concatenate([cos, cos, jnp.ones((s, rest), F32)], axis=1)
    sa = jnp.concatenate([-sin, z8, jnp.zeros((s, rest), F32)], axis=1)
    sb = jnp.concatenate([z8, sin, jnp.zeros((s, rest), F32)], axis=1)
    return tuple(jnp.tile(t, (1, LANES // HEAD_DIM)) for t in (cosf, sa, sb))


def _rope_fwd(t, cv, sav, sbv):
    w = t.shape[1]
    return t * cv + pltpu.roll(t, w - ROT_DIM // 2, 1) * sav + pltpu.roll(t, ROT_DIM // 2, 1) * sbv


def _rope_bwd(g, cv, sav, sbv):
    w = g.shape[1]
    return g * cv + pltpu.roll(g * sav, ROT_DIM // 2, 1) + pltpu.roll(g * sbv, w - ROT_DIM // 2, 1)


def _project(hn, wg_ref, proj_scr):
    c = wg_ref.shape[2]
    for j in range(N_DEV):
        proj_scr[:, j * c:(j + 1) * c] = _dot(hn, wg_ref[j])


def _inproj_attn(x, g, wg, tabs):
    s, d_model = x.shape
    gw = GROUP_WIDTH
    n = N_DEV * wg.shape[2]
    nz = n - 9 * gw - MEM_WIDTH
    reps = gw // LANES
    tm = ROW_TILE

    def body(x_ref, g_ref, w_ref, c_ref, sa_ref, sb_ref, hn_ref, hnt_ref, *rest):
        outs, (proj, scr, tscr) = rest[:-3], rest[-3:]
        q_refs, k_refs, v_refs, t_refs, qm_ref, z_ref = outs[0:3], outs[3:6], outs[6:9], outs[9:18], outs[18], outs[19]
        xb = x_ref[...]
        r = lax.rsqrt(jnp.mean(xb * xb, axis=-1, keepdims=True) + EPS)
        hn = ((xb * r) * g_ref[...]).astype(BF16)
        hn_ref[...] = hn
        hnt_ref[...] = hn.T
        _project(hn, w_ref, proj)
        tab = (c_ref[...], sa_ref[...], sb_ref[...])
        cv, sav, sbv = [jnp.tile(t, (1, reps)) for t in tab]
        for j, d in enumerate(DILATIONS):
            tq = _rope_fwd(proj[:, j * gw:(j + 1) * gw], cv, sav, sbv)
            _to_view(scr, tq * SCALE, q_refs[j], d)
            tk = _rope_fwd(proj[:, (3 + j) * gw:(4 + j) * gw], cv, sav, sbv)
            _to_view(scr, tk, k_refs[j], d)
            _to_view(scr, proj[:, (6 + j) * gw:(7 + j) * gw], v_refs[j], d)
            for i in range(3):
                _to_view(tscr, tab[i], t_refs[3 * j + i], d)
        qm_ref[...] = proj[:, 9 * gw:9 * gw + MEM_WIDTH].astype(BF16)
        z_ref[...] = proj[:, 9 * gw + MEM_WIDTH:]

    views = [_sds((s // d, d * gw), BF16) for d in DILATIONS]
    tviews = [_sds((s // d, d * LANES), F32) for d in DILATIONS for _ in range(3)]
    out_shape = ([_sds((s, d_model), BF16), _sds((d_model, s), BF16)] + views * 3 + tviews
                 + [_sds((s, MEM_WIDTH), BF16), _sds((s, nz), F32)])
    vspecs = [_view_rows(gw, d) for d in DILATIONS]
    tspecs = [_view_rows(LANES, d) for d in DILATIONS for _ in range(3)]
    out_specs = ([_rows(d_model), pl.BlockSpec((d_model, tm), lambda i: (0, i))] + vspecs * 3 + tspecs
                 + [_rows(MEM_WIDTH), _rows(nz)])
    res = _pallas_call(
        body, name="inproj_attn", grid=(s // tm,), out_shape=out_shape,
        in_specs=[_rows(d_model), _whole((1, d_model)), _whole(wg.shape), _rows(LANES), _rows(LANES), _rows(LANES)],
        out_specs=out_specs,
        scratch_shapes=[pltpu.VMEM((tm, n), F32), _view_scratch(gw), _view_scratch(LANES)],
        compiler_params=_params(1, 60),
    )(x, g, wg, *tabs)
    tabs_v = [res[11 + 3 * j:14 + 3 * j] for j in range(3)]
    return res[0], res[1], res[2:5], res[5:8], res[8:11], tabs_v, res[20], res[21]


def _stream_maps(d, nb):
    def cur(n):
        return (n % nb, n // nb)

    def prev(n):
        return (jnp.maximum(n % nb - 1, 0), n // nb)

    return cur, prev


def _band_mask(n_keys):
    qi = lax.broadcasted_iota(jnp.int32, (BLOCK, n_keys), 0)
    kj = lax.broadcasted_iota(jnp.int32, (BLOCK, n_keys), 1)
    if n_keys == BLOCK:
        return kj <= qi
    return jnp.logical_or(jnp.logical_and(kj < BLOCK, kj >= qi), jnp.logical_and(kj >= BLOCK, (kj - BLOCK) <= qi))


def _per_stream_block(n, nb, run):
    if nb == 1:
        run(False)
        return
    first = (n % nb) == 0
    pl.when(first)(lambda: run(False))
    pl.when(jnp.logical_not(first))(lambda: run(True))


def _attn_fwd(q, k, v, d, token):
    ln, dw = q.shape
    w = dw // d
    nb = ln // BLOCK
    nblk = d * nb
    cur, prev = _stream_maps(d, nb)

    def body(_, q_ref, kp_ref, kc_ref, vp_ref, vc_ref, o_ref, lse_ref):
        def run(with_prev):
            if with_prev:
                kk = jnp.concatenate([kp_ref[...], kc_ref[...]], axis=0)
                vv = jnp.concatenate([vp_ref[...], vc_ref[...]], axis=0)
            else:
                kk, vv = kc_ref[...], vc_ref[...]
            valid = _band_mask(kk.shape[0])
            for h in range(HEADS_PER_GROUP):
                sl = slice(h * HEAD_DIM, (h + 1) * HEAD_DIM)
                sc = jnp.where(valid, _dot_nt(q_ref[:, sl], kk[:, sl]), NEG)
                m = jnp.max(sc, axis=-1, keepdims=True)
                p = jnp.exp(sc - m)
                l = jnp.sum(p, axis=-1, keepdims=True)
                pn = p * (1.0 / l)
                o_ref[:, sl] = _dot(pn.astype(BF16), vv[:, sl])
                lse_ref[:, sl] = jnp.broadcast_to(m + jnp.log(l), (BLOCK, HEAD_DIM))

        _per_stream_block(pl.program_id(0), nb, run)

    blk = lambda f: pl.BlockSpec((BLOCK, w), f)
    return _pallas_call(
        body, name=f"attn_fwd_d{d}", grid=(nblk,),
        out_shape=[_sds((ln, dw), F32)] * 2,
        in_specs=[ANY, blk(cur), blk(prev), blk(cur), blk(prev), blk(cur)],
        out_specs=[blk(cur), blk(cur)], compiler_params=_params(1, 32),
    )(token, q, k, k, v, v)


def _memkv_fwd(mem, g, w):
    n_layers = w.shape[0]

    def body(mem_ref, g_ref, w_ref, kv_ref):
        mb = mem_ref[...]
        r = lax.rsqrt(jnp.mean(mb * mb, axis=-1, keepdims=True) + EPS)
        mn = ((mb * r) * g_ref[...]).astype(BF16)
        kv_ref[...] = _dot(mn, w_ref[...]).astype(BF16)

    return _pallas_call(
        body, name="memkv_fwd", grid=(n_layers,),
        out_shape=_plain((n_layers, N_MEM, 2 * MEM_WIDTH), BF16),
        in_specs=[_whole(mem.shape), pl.BlockSpec((None, 1, D_MODEL), lambda l: (l, 0, 0)),
                  pl.BlockSpec((None, D_MODEL, 2 * MEM_WIDTH), lambda l: (l, 0, 0))],
        out_specs=pl.BlockSpec((None, N_MEM, 2 * MEM_WIDTH), lambda l: (l, 0, 0)),
        compiler_params=_params(1, 32),
    )(mem, g.reshape(n_layers, 1, D_MODEL), w)


def _mix_groups(os_, ls_):
    mx = jnp.maximum(jnp.maximum(ls_[0], ls_[1]), ls_[2])
    es = [jnp.exp(t - mx) for t in ls_]
    inv = 1.0 / (es[0] + es[1] + es[2])
    ws = [e * inv for e in es]
    mix = ws[0] * os_[0] + ws[1] * os_[1] + ws[2] * os_[2]
    return ws, mix


def _mem_probs(qm, km, h):
    sl = slice(h * HEAD_DIM, (h + 1) * HEAD_DIM)
    sc = _dot_nt(qm[:, sl], km[:, sl]) * SCALE
    e = jnp.exp(sc - jnp.max(sc, axis=-1, keepdims=True))
    return e * (1.0 / jnp.sum(e, axis=-1, keepdims=True))


def _mem_attn_into(qm, kv_ref, mo_ref):
    km, vm = kv_ref[:, :MEM_WIDTH], kv_ref[:, MEM_WIDTH:]
    for h in range(MEM_HEADS):
        sl = slice(h * HEAD_DIM, (h + 1) * HEAD_DIM)
        p = _mem_probs(qm, km, h)
        mo_ref[:, sl] = _dot(p.astype(BF16), vm[:, sl])


def _mem_attn_bwd(qm, kv_ref, dmem, dqm_ref, dkv_ref):
    km, vm = kv_ref[:, :MEM_WIDTH], kv_ref[:, MEM_WIDTH:]
    dmb = dmem.astype(BF16)
    for h in range(MEM_HEADS):
        sl = slice(h * HEAD_DIM, (h + 1) * HEAD_DIM)
        slv = slice(MEM_WIDTH + h * HEAD_DIM, MEM_WIDTH + (h + 1) * HEAD_DIM)
        p = _mem_probs(qm, km, h)
        dp = _dot_nt(dmb[:, sl], vm[:, sl])
        ds = (p * (dp - jnp.sum(dp * p, axis=-1, keepdims=True)) * SCALE).astype(BF16)
        dqm_ref[:, sl] = _dot(ds, km[:, sl]).astype(BF16)
        dkv_ref[:, sl] += _dot_tn(ds, qm[:, sl])
        dkv_ref[:, slv] += _dot_tn(p.astype(BF16), dmb[:, sl])


def _post_attn(os_, ls_, qm, kv, z, x, wg_out):
    s, d_model = x.shape
    gw = GROUP_WIDTH
    nb = gw + MEM_WIDTH
    c = wg_out.shape[2]
    tm = ROW_TILE

    def body(o0, o1, o2, l0, l1, l2, qm_ref, kv_ref, z_ref, x_ref, w_ref, y_ref, yt_ref, mo_ref, h_ref, s0, s1):
        ov, lv = [], []
        for o_ref, l_ref, d in zip((o0, o1, o2), (l0, l1, l2), DILATIONS):
            ov.append(_from_view(s0, o_ref, d))
            lv.append(_from_view(s1, l_ref, d))
        _, mix = _mix_groups(ov, lv)
        _mem_attn_into(qm_ref[...], kv_ref, mo_ref)
        sz, _ = _silu_parts(z_ref[...])
        y_ref[:, :gw] = (mix * sz[:, :gw]).astype(BF16)
        y_ref[:, gw:] = (mo_ref[...] * sz[:, gw:]).astype(BF16)
        y = y_ref[...]
        yt_ref[...] = y.T
        for j in range(N_DEV):
            h_ref[:, j * c:(j + 1) * c] = x_ref[:, j * c:(j + 1) * c] + _dot(y, w_ref[j])

    vspecs = [_view_rows(gw, d) for d in DILATIONS]
    return _pallas_call(
        body, name="post_attn", grid=(s // tm,),
        out_shape=[_sds((s, nb), BF16), _sds((nb, s), BF16), _sds((s, MEM_WIDTH), F32), _sds((s, d_model), F32)],
        in_specs=vspecs * 2 + [_rows(MEM_WIDTH), _whole(kv.shape), _rows(nb), _rows(d_model), _whole(wg_out.shape)],
        out_specs=[_rows(nb), pl.BlockSpec((nb, tm), lambda i: (0, i)), _rows(MEM_WIDTH), _rows(d_model)],
        scratch_shapes=[_view_scratch(gw), _view_scratch(gw)],
        compiler_params=_params(1, 40),
    )(*os_, *ls_, qm, kv, z, x, wg_out)


def _inproj_conv(x, g, wg):
    s, d_model = x.shape
    c = CONV_WIDTH
    n = N_DEV * wg.shape[2]
    nz = n - 3 * c - MEM_WIDTH
    tm = ROW_TILE

    def body(x_ref, g_ref, w_ref, hn_ref, hnt_ref, bg_ref, cg_ref, u_ref, qm_ref, z_ref, proj):
        xb = x_ref[...]
        r = lax.rsqrt(jnp.mean(xb * xb, axis=-1, keepdims=True) + EPS)
        hn = ((xb * r) * g_ref[...]).astype(BF16)
        hn_ref[...] = hn
        hnt_ref[...] = hn.T
        _project(hn, w_ref, proj)
        bg_ref[...] = proj[:, 0:c]
        cg_ref[...] = proj[:, c:2 * c]
        u_ref[...] = proj[:, 2 * c:3 * c]
        qm_ref[...] = proj[:, 3 * c:3 * c + MEM_WIDTH].astype(BF16)
        z_ref[...] = proj[:, 3 * c + MEM_WIDTH:]

    return _pallas_call(
        body, name="inproj_conv", grid=(s // tm,),
        out_shape=[_sds((s, d_model), BF16), _sds((d_model, s), BF16)] + [_sds((s, c), F32)] * 3
                  + [_sds((s, MEM_WIDTH), BF16), _sds((s, nz), F32)],
        in_specs=[_rows(d_model), _whole((1, d_model)), _whole(wg.shape)],
        out_specs=[_rows(d_model), pl.BlockSpec((d_model, tm), lambda i: (0, i))] + [_rows(c)] * 3
                  + [_rows(MEM_WIDTH), _rows(nz)],
        scratch_shapes=[pltpu.VMEM((tm, n), F32)],
        compiler_params=_params(1, 60),
    )(x, g, wg)


HALO = 8


def _halo_before(width, tm=ROW_TILE):
    return pl.BlockSpec((HALO, width), lambda i: (jnp.maximum(i * (tm // HALO) - 1, 0), 0))


def _halo_after(width, n_rows, tm=ROW_TILE):
    return pl.BlockSpec((HALO, width), lambda i: (jnp.minimum((i + 1) * (tm // HALO), n_rows // HALO - 1), 0))


def _conv_taps(cg_ref, u_ref, cgh_ref, uh_ref, i):
    a = cg_ref[...] * u_ref[...]
    ah = jnp.where(i > 0, cgh_ref[...] * uh_ref[...], 0.0)
    row = lax.broadcasted_iota(jnp.int32, a.shape, 0)
    a1 = jnp.where(row == 0, ah[HALO - 1:HALO], pltpu.roll(a, 1, 0))
    a2 = jnp.where(row == 0, ah[HALO - 2:HALO - 1], jnp.where(row == 1, ah[HALO - 1:HALO], pltpu.roll(a, 2, 0)))
    return a, a1, a2


def _post_conv_loss(bg, cg, u, qm, kv, z, h1, w_out, cw, gf, tgt):
    s, d = h1.shape
    c = CONV_WIDTH
    nb = c + MEM_WIDTH
    tm = ROW_TILE

    def body(bg_ref, cg_ref, u_ref, cgh_ref, uh_ref, qm_ref, kv_ref, z_ref, h_ref, w_ref, cw_ref, gf_ref, t_ref,
             y_ref, yt_ref, mo_ref, dh_ref, dhb_ref, loss_ref, dgf_ref):
        i = pl.program_id(0)
        a, a1, a2 = _conv_taps(cg_ref, u_ref, cgh_ref, uh_ref, i)
        conv = cw_ref[0:1, :] * a2 + cw_ref[1:2, :] * a1 + cw_ref[2:3, :] * a
        mix = bg_ref[...] * conv
        _mem_attn_into(qm_ref[...], kv_ref, mo_ref)
        sz, _ = _silu_parts(z_ref[...])
        y_ref[:, :c] = (mix * sz[:, :c]).astype(BF16)
        y_ref[:, c:] = (mo_ref[...] * sz[:, c:]).astype(BF16)
        y = y_ref[...]
        yt_ref[...] = y.T
        h2 = h_ref[...] + _dot(y, w_ref[...])
        r = lax.rsqrt(jnp.mean(h2 * h2, axis=-1, keepdims=True) + EPS)
        nh = h2 * r
        gfv = gf_ref[...]
        diff = nh * gfv - t_ref[...]
        dout = diff * (1.0 / d)
        dn = dout * gfv
        dh2 = r * dn - h2 * ((r * r * r) * jnp.mean(dn * h2, axis=-1, keepdims=True))
        dh_ref[...] = dh2
        dhb_ref[...] = dh2.astype(BF16)

        @pl.when(i == 0)
        def _():
            loss_ref[...] = jnp.zeros_like(loss_ref)
            dgf_ref[...] = jnp.zeros_like(dgf_ref)

        loss_ref[...] += 0.5 * jnp.sum(jnp.mean(diff * diff, axis=-1, keepdims=True))
        dgf_ref[...] += jnp.sum(dout * nh, axis=0, keepdims=True)

    return _pallas_call(
        body, name="post_conv_loss", grid=(s // tm,),
        out_shape=[_sds((s, nb), BF16), _sds((nb, s), BF16), _sds((s, MEM_WIDTH), F32), _sds((s, d), F32),
                   _sds((s, d), BF16), _plain((8, LANES), F32), _plain((1, d), F32)],
        in_specs=[_rows(c)] * 3 + [_halo_before(c)] * 2 + [_rows(MEM_WIDTH), _whole(kv.shape), _rows(nb), _rows(d),
                  _whole(w_out.shape), _whole(cw.shape), _whole((1, d)), _rows(d)],
        out_specs=[_rows(nb), pl.BlockSpec((nb, tm), lambda i: (0, i)), _rows(MEM_WIDTH), _rows(d), _rows(d),
                   _whole((8, LANES)), _whole((1, d))],
        compiler_params=_params(1, 48),
    )(bg, cg, u, cg, u, qm, kv, z, h1, w_out, cw, gf, tgt)


def _bwd_post_conv(dhb, w_out, bg, cg, u, z, qm, kv, mo, cw):
    s = dhb.shape[0]
    c = CONV_WIDTH
    nb = c + MEM_WIDTH

    def body(dh_ref, w_ref, bg_ref, cg_ref, u_ref, cgh_ref, uh_ref, z_ref, qm_ref, kv_ref, mo_ref, cw_ref,
             dz_ref, dbg_ref, dc_ref, dqm_ref, dkv_ref):
        i = pl.program_id(0)

        @pl.when(i == 0)
        def _():
            dkv_ref[...] = jnp.zeros_like(dkv_ref)

        dy = _dot_nt(dh_ref[...], w_ref[...])
        sz, dsz = _silu_parts(z_ref[...])
        a, a1, a2 = _conv_taps(cg_ref, u_ref, cgh_ref, uh_ref, i)
        conv = cw_ref[0:1, :] * a2 + cw_ref[1:2, :] * a1 + cw_ref[2:3, :] * a
        bgv = bg_ref[...]
        dz_ref[:, :c] = (dy[:, :c] * (bgv * conv) * dsz[:, :c]).astype(BF16)
        dz_ref[:, c:] = (dy[:, c:] * mo_ref[...] * dsz[:, c:]).astype(BF16)
        dbr = dy * sz
        dmix = dbr[:, :c]
        dbg_ref[...] = (dmix * conv).astype(BF16)
        dc_ref[...] = dmix * bgv
        _mem_attn_bwd(qm_ref[...], kv_ref, dbr[:, c:], dqm_ref, dkv_ref)

    return _pallas_call(
        body, name="bwd_post_conv", grid=(s // ROW_TILE,),
        out_shape=[_sds((s, nb), BF16), _sds((s, c), BF16), _sds((s, c), F32), _sds((s, MEM_WIDTH), BF16),
                   _plain(kv.shape, F32)],
        in_specs=[_rows(D_MODEL), _whole(w_out.shape)] + [_rows(c)] * 3 + [_halo_before(c)] * 2
                 + [_rows(nb), _rows(MEM_WIDTH), _whole(kv.shape), _rows(MEM_WIDTH), _whole(cw.shape)],
        out_specs=[_rows(nb), _rows(c), _rows(c), _rows(MEM_WIDTH), _whole(kv.shape)],
        compiler_params=_params(1, 48),
    )(dhb, w_out, bg, cg, u, cg, u, z, qm, kv, mo, cw)


def _bwd_conv(dconv, cg, u, cw):
    s, c = dconv.shape
    tm = ROW_TILE
    last = s // tm - 1

    def body(dc_ref, dcn_ref, cg_ref, u_ref, cgh_ref, uh_ref, cw_ref, dcg_ref, du_ref, dcw_ref):
        i = pl.program_id(0)

        @pl.when(i == 0)
        def _():
            dcw_ref[...] = jnp.zeros_like(dcw_ref)

        dc = dc_ref[...]
        dcn = jnp.where(i < last, dcn_ref[...], 0.0)
        row = lax.broadcasted_iota(jnp.int32, dc.shape, 0)
        d1 = jnp.where(row == tm - 1, dcn[0:1], pltpu.roll(dc, tm - 1, 0))
        d2 = jnp.where(row == tm - 1, dcn[1:2], jnp.where(row == tm - 2, dcn[0:1], pltpu.roll(dc, tm - 2, 0)))
        da = cw_ref[2:3, :] * dc + cw_ref[1:2, :] * d1 + cw_ref[0:1, :] * d2
        a, a1, a2 = _conv_taps(cg_ref, u_ref, cgh_ref, uh_ref, i)
        dcg_ref[...] = (da * u_ref[...]).astype(BF16)
        du_ref[...] = (da * cg_ref[...]).astype(BF16)
        dcw_ref[0:1, :] += jnp.sum(dc * a2, axis=0, keepdims=True)
        dcw_ref[1:2, :] += jnp.sum(dc * a1, axis=0, keepdims=True)
        dcw_ref[2:3, :] += jnp.sum(dc * a, axis=0, keepdims=True)

    return _pallas_call(
        body, name="bwd_conv", grid=(s // tm,),
        out_shape=[_sds((s, c), BF16), _sds((s, c), BF16), _plain((8, c), F32)],
        in_specs=[_rows(c), _halo_after(c, s), _rows(c), _rows(c), _halo_before(c), _halo_before(c), _whole(cw.shape)],
        out_specs=[_rows(c), _rows(c), _whole((8, c))], compiler_params=_params(1, 40),
    )(dconv, dconv, cg, u, cg, u, cw)


def _dgrad_norm(pieces, wg, h, g, dres, name):
    s, d_model = h.shape
    c = wg.shape[2]
    n = N_DEV * c
    tm = ROW_TILE
    widths = [p.shape[1] // d for p, d in pieces]
    assert sum(widths) == n
    n_p = len(pieces)

    def body(*refs):
        p_refs = refs[:n_p]
        w_ref, h_ref, g_ref, dr_ref, dh_ref, dhb_ref, dg_ref, dpd_ref, dp, scr = refs[n_p:]

        @pl.when(pl.program_id(0) == 0)
        def _():
            dg_ref[...] = jnp.zeros_like(dg_ref)

        off = 0
        for p_ref, (_, d), wd in zip(p_refs, pieces, widths):
            if d == 1:
                dp[:, off:off + wd] = p_ref[...]
            else:
                dp[:, off:off + wd] = _from_view(scr, p_ref, d).astype(BF16)
            off += wd
        dhn = jnp.zeros((tm, d_model), F32)
        for j in range(N_DEV):
            dpj = dp[:, j * c:(j + 1) * c]
            dpd_ref[j] = dpj
            dhn += _dot_nt(dpj, w_ref[j])
        hb = h_ref[...]
        r = lax.rsqrt(jnp.mean(hb * hb, axis=-1, keepdims=True) + EPS)
        dg_ref[...] += jnp.sum(dhn * (hb * r), axis=0, keepdims=True)
        dn = dhn * g_ref[...]
        dh = dr_ref[...] + r * dn - hb * ((r * r * r) * jnp.mean(dn * hb, axis=-1, keepdims=True))
        dh_ref[...] = dh
        dhb_ref[...] = dh.astype(BF16)

    p_specs = [_view_rows(wd, d) for (_, d), wd in zip(pieces, widths)]
    return _pallas_call(
        body, name=name, grid=(s // tm,),
        out_shape=[_plain((s, d_model), F32), _sds((s, d_model), BF16), _plain((1, d_model), F32), _sds((N_DEV, s, c), BF16)],
        in_specs=p_specs + [_whole(wg.shape), _rows(d_model), _whole((1, d_model)), _rows(d_model)],
        out_specs=[_rows(d_model), _rows(d_model), _whole((1, d_model)), pl.BlockSpec((N_DEV, tm, c), lambda i: (0, i, 0))],
        scratch_shapes=[pltpu.VMEM((tm, n), BF16), _view_scratch(GROUP_WIDTH)],
        compiler_params=_params(1, 60),
    )(*[p for p, _ in pieces], wg, h, g, dres)


def _wgrad_shards(a_t, b_dm, name):
    m, s = a_t.shape
    c = b_dm.shape[2]

    def body(a_ref, b_ref, o_ref):
        o_ref[...] = _dot(a_ref[...], b_ref[...]).astype(BF16)

    return _pallas_call(
        body, name=name, grid=(N_DEV,), out_shape=_sds((N_DEV, m, c), BF16),
        in_specs=[_whole(a_t.shape), pl.BlockSpec((None, s, c), lambda j: (j, 0, 0))],
        out_specs=pl.BlockSpec((None, m, c), lambda j: (j, 0, 0)), compiler_params=_params(1, 40),
    )(a_t, b_dm)


def _wgrad_cols(a_t, b, c, name):
    m, s = a_t.shape

    def body(a_ref, b_ref, o_ref):
        o_ref[...] = _dot(a_ref[...], b_ref[...]).astype(BF16)

    return _pallas_call(
        body, name=name, grid=(N_DEV,), out_shape=_sds((N_DEV, m, c), BF16),
        in_specs=[_whole(a_t.shape), pl.BlockSpec((s, c), lambda j: (0, j))],
        out_specs=pl.BlockSpec((None, m, c), lambda j: (j, 0, 0)), compiler_params=_params(1, 40),
    )(a_t, b)


def _wgrad_rows(a_t, b, name):
    m, s = a_t.shape
    n = b.shape[1]
    mr = m // N_DEV

    def body(a_ref, b_ref, o_ref):
        o_ref[...] = _dot(a_ref[...], b_ref[...]).astype(BF16)

    return _pallas_call(
        body, name=name, grid=(N_DEV,), out_shape=_sds((N_DEV, mr, n), BF16),
        in_specs=[pl.BlockSpec((mr, s), lambda j: (j, 0)), _whole(b.shape)],
        out_specs=pl.BlockSpec((None, mr, n), lambda j: (j, 0, 0)), compiler_params=_params(1, 40),
    )(a_t, b)


def _memkv_bwd(dkv, w, mem, g):
    n_layers = w.shape[0]

    def body(dkv_ref, w_ref, mem_ref, g_ref, dw_ref, dg_ref):
        mb = mem_ref[...]
        r = lax.rsqrt(jnp.mean(mb * mb, axis=-1, keepdims=True) + EPS)
        nm = mb * r
        mn = (nm * g_ref[...]).astype(BF16)
        dkvb = dkv_ref[...].astype(BF16)
        dw_ref[...] = _dot_tn(mn, dkvb).astype(BF16)
        dmn = _dot_nt(dkvb, w_ref[...])
        dg_ref[...] = jnp.sum(dmn * nm, axis=0, keepdims=True)

    lay = lambda *shape: pl.BlockSpec((None,) + shape, lambda l: (l, 0, 0))
    return _pallas_call(
        body, name="memkv_bwd", grid=(n_layers,),
        out_shape=[_plain((n_layers, D_MODEL, 2 * MEM_WIDTH), BF16), _plain((n_layers, 1, D_MODEL), F32)],
        in_specs=[lay(N_MEM, 2 * MEM_WIDTH), lay(D_MODEL, 2 * MEM_WIDTH), _whole(mem.shape), lay(1, D_MODEL)],
        out_specs=[lay(D_MODEL, 2 * MEM_WIDTH), lay(1, D_MODEL)], compiler_params=_params(1, 32),
    )(dkv, w, mem, g.reshape(n_layers, 1, D_MODEL))


def _bwd_post_attn(dhb, wg_out, z, os_, ls_, qm, kv, mo, head_ones):
    s = dhb.shape[0]
    gw = GROUP_WIDTH
    nb = gw + MEM_WIDTH
    c = wg_out.shape[2]
    tm = ROW_TILE

    def body(dh_ref, w_ref, z_ref, o0, o1, o2, l0, l1, l2, qm_ref, kv_ref, mo_ref, bd_ref,
             dz_ref, do0, do1, do2, dl0, dl1, dl2, dqm_ref, dkv_ref, s0, s1):
        @pl.when(pl.program_id(0) == 0)
        def _():
            dkv_ref[...] = jnp.zeros_like(dkv_ref)

        dy = jnp.zeros((tm, nb), F32)
        for j in range(N_DEV):
            dy += _dot_nt(dh_ref[:, j * c:(j + 1) * c], w_ref[j])
        ov, lv = [], []
        for o_ref, l_ref, d in zip((o0, o1, o2), (l0, l1, l2), DILATIONS):
            ov.append(_from_view(s0, o_ref, d))
            lv.append(_from_view(s1, l_ref, d))
        ws, mix = _mix_groups(ov, lv)
        sz, dsz = _silu_parts(z_ref[...])
        dz_ref[:, :gw] = (dy[:, :gw] * mix * dsz[:, :gw]).astype(BF16)
        dz_ref[:, gw:] = (dy[:, gw:] * mo_ref[...] * dsz[:, gw:]).astype(BF16)
        dbr = dy * sz
        dmix = dbr[:, :gw]
        t = dmix * mix
        th = t.astype(BF16)
        tl = (t - th.astype(F32)).astype(BF16)
        rs = _dot(th, bd_ref[...]) + _dot(tl, bd_ref[...])
        for wg_, do_ref, dl_ref, d in zip(ws, (do0, do1, do2), (dl0, dl1, dl2), DILATIONS):
            _to_view(s0, wg_ * dmix, do_ref, d)
            _to_view(s1, wg_ * rs, dl_ref, d)
        _mem_attn_bwd(qm_ref[...], kv_ref, dbr[:, gw:], dqm_ref, dkv_ref)

    vspecs = [_view_rows(gw, d) for d in DILATIONS]
    return _pallas_call(
        body, name="bwd_post_attn", grid=(s // tm,),
        out_shape=[_sds((s, nb), BF16)] + [_sds((s // d, d * gw), BF16) for d in DILATIONS]
                  + [_sds((s // d, d * gw), F32) for d in DILATIONS] + [_sds((s, MEM_WIDTH), BF16), _plain(kv.shape, F32)],
        in_specs=[_rows(D_MODEL), _whole(wg_out.shape), _rows(nb)] + vspecs * 2
                 + [_rows(MEM_WIDTH), _whole(kv.shape), _rows(MEM_WIDTH), _whole(head_ones.shape)],
        out_specs=[_rows(nb)] + vspecs * 2 + [_rows(MEM_WIDTH), _whole(kv.shape)],
        scratch_shapes=[_view_scratch(gw), _view_scratch(gw)],
        compiler_params=_params(1, 48),
    )(dhb, wg_out, z, *os_, *ls_, qm, kv, mo, head_ones)


def _attn_bwd(q, k, v, lse, do, dl, tabs, d, token):
    ln, dw = q.shape
    w = dw // d
    nb = ln // BLOCK
    nblk = d * nb
    reps = w // LANES
    cur, prev = _stream_maps(d, nb)
    qmap = lambda n: cur(jnp.minimum(n, nblk - 1))
    pmap = lambda n: prev(jnp.minimum(n, nblk - 1))
    omap = lambda n: cur(jnp.maximum(n - 1, 0))

    def body(_, q_ref, kp_ref, kc_ref, vp_ref, vc_ref, l_ref, do_ref, dl_ref, cq, saq, sbq, ck, sak, sbk,
             dq_ref, dk_ref, dv_ref, acck, accv, dqs):
        n = pl.program_id(0)

        @pl.when(n == 0)
        def _():
            acck[...] = jnp.zeros_like(acck)
            accv[...] = jnp.zeros_like(accv)

        def run(with_prev):
            if with_prev:
                kk = jnp.concatenate([kp_ref[...], kc_ref[...]], axis=0)
                vv = jnp.concatenate([vp_ref[...], vc_ref[...]], axis=0)
                rows = slice(0, 2 * BLOCK)
            else:
                kk, vv = kc_ref[...], vc_ref[...]
                rows = slice(BLOCK, 2 * BLOCK)
            valid = _band_mask(kk.shape[0])
            for h in range(HEADS_PER_GROUP):
                sl = slice(h * HEAD_DIM, (h + 1) * HEAD_DIM)
                col = slice(h * HEAD_DIM, h * HEAD_DIM + 1)
                qh = q_ref[:, sl]
                dob = do_ref[:, sl]
                sc = jnp.where(valid, _dot_nt(qh, kk[:, sl]), NEG)
                p = jnp.exp(sc - l_ref[:, col])
                dp = _dot_nt(dob, vv[:, sl])
                ds = (p * (dp - dl_ref[:, col])).astype(BF16)
                dqs[:, sl] = _dot(ds, kk[:, sl]) * SCALE
                acck[rows, sl] += _dot_tn(ds, qh)
                accv[rows, sl] += _dot_tn(p.astype(BF16), dob)
            tq = [jnp.tile(r[...], (1, reps)) for r in (cq, saq, sbq)]
            dq_ref[...] = _rope_bwd(dqs[...], *tq).astype(BF16)

        pl.when(n < nblk)(lambda: _per_stream_block(n, nb, run))

        tk = [jnp.tile(r[...], (1, reps)) for r in (ck, sak, sbk)]
        dk_ref[...] = _rope_bwd(acck[0:BLOCK, :], *tk).astype(BF16)
        dv_ref[...] = accv[0:BLOCK, :].astype(BF16)
        acck[0:BLOCK, :] = acck[BLOCK:, :]
        accv[0:BLOCK, :] = accv[BLOCK:, :]
        acck[BLOCK:, :] = jnp.zeros((BLOCK, w), F32)
        accv[BLOCK:, :] = jnp.zeros((BLOCK, w), F32)

    blk = lambda f: pl.BlockSpec((BLOCK, w), f)
    tblk = lambda f: pl.BlockSpec((BLOCK, LANES), f)
    return _pallas_call(
        body, name=f"attn_bwd_d{d}", grid=(nblk + 1,),
        out_shape=[_sds((ln, dw), BF16)] * 3,
        in_specs=[ANY, blk(qmap), blk(pmap), blk(qmap), blk(pmap), blk(qmap), blk(qmap), blk(qmap), blk(qmap)]
                 + [tblk(qmap)] * 3 + [tblk(omap)] * 3,
        out_specs=[blk(qmap), blk(omap), blk(omap)],
        scratch_shapes=[pltpu.VMEM((2 * BLOCK, w), F32), pltpu.VMEM((2 * BLOCK, w), F32), pltpu.VMEM((BLOCK, w), F32)],
        compiler_params=_params(1, 32),
    )(token, q, k, k, v, v, lse, do, dl, *tabs, *tabs)


def _position():
    return lax.axis_index("x"), lax.axis_index("y"), lax.axis_index("c")


def _all_gather(shards, name):
    n_a = len(shards)

    def body(*refs):
        x_refs, out_refs = refs[:n_a], refs[n_a:2 * n_a]
        send_sems, recv_sems, local_sems = refs[2 * n_a:]
        x, y, c = _position()
        me, sibling = (x, y, c), (x, y, 1 - c)
        chips = [(1 - x, y), (x, 1 - y), (1 - x, 1 - y)]

        def rows(a, px, py, pc):
            return out_refs[a].at[4 * px + 2 * py + pc]

        def copy(a, k, block, to, own=False):
            return pltpu.make_async_remote_copy(
                src_ref=x_refs[a] if own else rows(a, *block), dst_ref=rows(a, *block),
                send_sem=send_sems.at[a, k], recv_sem=recv_sems.at[a, k], device_id=to, device_id_type=MESH)

        mine = [pltpu.make_async_copy(x_refs[a], rows(a, *me), local_sems.at[a]) for a in range(n_a)]
        for cp in mine:
            cp.start()
        first = []
        for j, chip in enumerate(chips):
            first += [copy(a, 1 + j, me, (*chip, c), own=True) for a in range(n_a)]
        first += [copy(a, 0, me, sibling, own=True) for a in range(n_a)]
        for cp in first:
            cp.start()
        passed = []
        for j, chip in enumerate(chips):
            for a in range(n_a):
                copy(a, 1 + j, (*chip, c), me).wait_recv()
                fwd = copy(a, 4 + j, (*chip, c), sibling)
                fwd.start()
                passed.append(fwd)
        for a in range(n_a):
            copy(a, 0, sibling, me).wait_recv()
        for j, chip in enumerate(chips):
            for a in range(n_a):
                copy(a, 4 + j, (*chip, 1 - c), me).wait_recv()
        for cp in first + passed:
            cp.wait_send()
        for cp in mine:
            cp.wait()

    return _pallas_call(
        body, name=name, out_shape=[_sds((N_DEV,) + t.shape, t.dtype) for t in shards],
        in_specs=[ANY] * n_a, out_specs=[ANY] * n_a,
        scratch_shapes=[pltpu.SemaphoreType.DMA((n_a, 7)), pltpu.SemaphoreType.DMA((n_a, 7)),
                        pltpu.SemaphoreType.DMA((n_a,))],
    )(*shards)


def _rs_to_sibling(gs):
    n_a = len(gs)

    def body(*refs):
        g_refs, recv_refs = refs[:n_a], refs[n_a:2 * n_a]
        send_sems, recv_sems = refs[2 * n_a:]
        x, y, c = _position()
        copies = []
        for k in range(4):
            for a in range(n_a):
                copies.append(pltpu.make_async_remote_copy(
                    src_ref=g_refs[a].at[2 * k + (1 - c)], dst_ref=recv_refs[a].at[k],
                    send_sem=send_sems.at[a, k], recv_sem=recv_sems.at[a, k],
                    device_id=(x, y, 1 - c), device_id_type=MESH))
        for cp in copies:
            cp.start()
        for cp in copies:
            cp.wait()

    return _pallas_call(
        body, name="rs_to_sibling", out_shape=[_sds((4,) + g.shape[1:], g.dtype) for g in gs],
        in_specs=[ANY] * n_a, out_specs=[ANY] * n_a,
        scratch_shapes=[pltpu.SemaphoreType.DMA((n_a, 4)), pltpu.SemaphoreType.DMA((n_a, 4))],
    )(*gs)


def _rs_to_chips(pbs):
    n_a = len(pbs)

    def body(*refs):
        p_refs, recv_refs = refs[:n_a], refs[n_a:2 * n_a]
        send_sems, recv_sems = refs[2 * n_a:]
        x, y, c = _position()
        chips = [(1 - x, y), (x, 1 - y), (1 - x, 1 - y)]
        copies = []
        for j, (px, py) in enumerate(chips):
            for a in range(n_a):
                copies.append(pltpu.make_async_remote_copy(
                    src_ref=p_refs[a].at[2 * px + py], dst_ref=recv_refs[a].at[j],
                    send_sem=send_sems.at[a, j], recv_sem=recv_sems.at[a, j],
                    device_id=(px, py, c), device_id_type=MESH))
        for cp in copies:
            cp.start()
        for cp in copies:
            cp.wait()

    return _pallas_call(
        body, name="rs_to_chips", out_shape=[_sds((3,) + p.shape[1:], p.dtype) for p in pbs],
        in_specs=[ANY] * n_a, out_specs=[ANY] * n_a,
        scratch_shapes=[pltpu.SemaphoreType.DMA((n_a, 3)), pltpu.SemaphoreType.DMA((n_a, 3))],
    )(*pbs)


HBM_SPEC = pl.BlockSpec(memory_space=pltpu.HBM)
SEM_SPEC = pl.BlockSpec(memory_space=pltpu.SEMAPHORE)
EFFECT = pltpu.SideEffectType.DATAFLOW_SIDE_EFFECTING
def _plan_gather_own(src_refs, land_refs):
    x, y, c = _position()
    me = 4 * x + 2 * y + c
    peers = [(x, y, 1 - c), (1 - x, y, c), (x, 1 - y, c), (1 - x, 1 - y, c)]
    return [(src_refs[a], land_refs[a].at[me], (a, k), peer) for k, peer in enumerate(peers) for a in range(len(src_refs))]


def _plan_gather_pass(src_refs, land_refs):
    x, y, c = _position()
    chips = [(1 - x, y), (x, 1 - y), (1 - x, 1 - y)]
    return [(land_refs[a].at[4 * px + 2 * py + c], land_refs[a].at[4 * px + 2 * py + c], (a, j), (x, y, 1 - c))
            for j, (px, py) in enumerate(chips) for a in range(len(land_refs))]


def _plan_to_sibling(src_refs, land_refs):
    x, y, c = _position()
    return [(src_refs[a].at[2 * k + (1 - c)], land_refs[a].at[k], (a, k), (x, y, 1 - c))
            for k in range(4) for a in range(len(src_refs))]


def _plan_to_chips(src_refs, land_refs):
    x, y, c = _position()
    chips = [(1 - x, y), (x, 1 - y), (1 - x, 1 - y)]
    return [(src_refs[a].at[2 * px + py], land_refs[a].at[j], (a, j), (px, py, c))
            for j, (px, py) in enumerate(chips) for a in range(len(src_refs))]


def _split_start(srcs, lands, plan, n_sem, after, name):
    n_s, n_a = len(srcs), len(lands)
    n_b = n_s + n_a

    def body(*refs):
        src_refs, land_refs = refs[:n_s], refs[n_s:n_b]
        send_sems, recv_sems, token = refs[n_b + 1], refs[n_b + 2], refs[-1]
        for src, dst, (a, k), dev in plan(src_refs, land_refs):
            i = a * n_sem + k
            pltpu.make_async_remote_copy(src_ref=src, dst_ref=dst, send_sem=send_sems.at[i], recv_sem=recv_sems.at[i],
                                         device_id=dev, device_id_type=MESH).start()
        token[...] = jnp.zeros_like(token)

    bufs = list(srcs) + list(lands)
    res = pl.pallas_call(
        body, name=name,
        out_shape=(pltpu.SemaphoreType.DMA((n_a * n_sem,)), pltpu.SemaphoreType.DMA((n_a * n_sem,)),
                   *[pltpu.HBM(t.shape, t.dtype) for t in bufs], _plain((8, LANES), F32)),
        in_specs=[HBM_SPEC] * n_b + [ANY],
        out_specs=(SEM_SPEC, SEM_SPEC, *[HBM_SPEC] * n_b, pl.BlockSpec(memory_space=pltpu.VMEM)),
        input_output_aliases={i: 2 + i for i in range(n_b)},
        compiler_params=pltpu.CompilerParams(has_side_effects=EFFECT),
    )(*[pltpu.with_memory_space_constraint(t, pltpu.HBM) for t in bufs], after)
    return (res[0], res[1], res[2:2 + n_s], res[2 + n_s:2 + n_b]), res[-1]


def _split_wait(started, plan, after, name):
    send_sems, recv_sems, srcs, lands = started
    n_s, n_a = len(srcs), len(lands)
    n_b = n_s + n_a
    n_sem = send_sems.shape[0] // n_a

    def body(*refs):
        src_refs, land_refs = refs[:n_s], refs[n_s:n_b]
        s_sems, r_sems = refs[n_b], refs[n_b + 1]
        for src, dst, (a, k), dev in plan(src_refs, land_refs):
            i = a * n_sem + k
            cp = pltpu.make_async_remote_copy(src_ref=src, dst_ref=dst, send_sem=s_sems.at[i], recv_sem=r_sems.at[i],
                                              device_id=dev, device_id_type=MESH)
            cp.wait_send()
            cp.wait_recv()

    bufs = list(srcs) + list(lands)
    res = pl.pallas_call(
        body, name=name, out_shape=tuple(pltpu.HBM(t.shape, t.dtype) for t in bufs),
        in_specs=[HBM_SPEC] * n_b + [SEM_SPEC, SEM_SPEC, ANY],
        out_specs=tuple([HBM_SPEC] * n_b),
        input_output_aliases={i: i for i in range(n_b)},
        compiler_params=pltpu.CompilerParams(has_side_effects=EFFECT),
    )(*bufs, send_sems, recv_sems, after)
    return res[:n_s], res[n_s:]


def _row_tile(r):
    return ROW_TILE if r % ROW_TILE == 0 else r


def _rs_add_sibling(gp, recv, c_arr, name):
    _, r, l = gp.shape
    tr = _row_tile(r)

    def body(c_ref, g_ref, r_ref, pf_ref, pb_ref):
        sm = g_ref[...].astype(F32) + r_ref[...].astype(F32)
        pf_ref[...] = sm
        pb_ref[...] = sm.astype(BF16)

    spec = pl.BlockSpec((None, tr, l), lambda k, i, c: (k, i, 0))
    return _pallas_call(
        body, name=name,
        grid_spec=pltpu.PrefetchScalarGridSpec(
            num_scalar_prefetch=1, grid=(4, r // tr),
            in_specs=[pl.BlockSpec((None, tr, l), lambda k, i, c: (2 * k + c[0], i, 0)), spec],
            out_specs=[spec, spec]),
        out_shape=[_sds((4, r, l), F32), _sds((4, r, l), BF16)], compiler_params=_params(2, 32),
    )(c_arr, gp, recv)


def _adam_update(w, gv, m, v):
    nm = ADAM_B1 * m + (1.0 - ADAM_B1) * gv
    nv = ADAM_B2 * v + (1.0 - ADAM_B2) * (gv * gv)
    m_hat = nm / (1.0 - ADAM_B1 ** ADAM_STEP)
    v_hat = nv / (1.0 - ADAM_B2 ** ADAM_STEP)
    return -ADAM_LR * (m_hat / (jnp.sqrt(v_hat) + ADAM_EPS) + ADAM_WD * w), nm, nv


def _rs_finish_adamw(pf, recv, k_arr, w, m, v, name):
    _, r, l = pf.shape
    tr = _row_tile(r)

    def body(k_ref, p_ref, r_ref, w_ref, m_ref, v_ref, g_ref, d_ref, nm_ref, nv_ref):
        gv = ((p_ref[...] + r_ref[0].astype(F32)) + r_ref[1].astype(F32)) + r_ref[2].astype(F32)
        g_ref[...] = gv
        d_ref[...], nm_ref[...], nv_ref[...] = _adam_update(w_ref[...], gv, m_ref[...], v_ref[...])

    spec = pl.BlockSpec((tr, l), lambda i, k: (i, 0))
    return _pallas_call(
        body, name=name,
        grid_spec=pltpu.PrefetchScalarGridSpec(
            num_scalar_prefetch=1, grid=(r // tr,),
            in_specs=[pl.BlockSpec((None, tr, l), lambda i, k: (k[0], i, 0)),
                      pl.BlockSpec((3, tr, l), lambda i, k: (0, i, 0)), spec, spec, spec],
            out_specs=[spec] * 4),
        out_shape=[_plain((r, l), F32)] * 4, compiler_params=_params(1, 32),
    )(k_arr, pf, recv, w, m, v)


def _sum_devices(g):
    def body(g_ref, o_ref):
        acc = g_ref[0]
        for j in range(1, N_DEV):
            acc = acc + g_ref[j]
        o_ref[...] = acc

    return _pallas_call(body, name="sum_devices", out_shape=_plain(g.shape[1:], F32))(g)


def _adamw(w, g, m, v, name):
    shape = w.shape
    w2, g2, m2, v2 = [t.reshape((-1, shape[-1])) for t in (w, g, m, v)]

    def body(w_ref, g_ref, m_ref, v_ref, d_ref, nm_ref, nv_ref):
        d_ref[...], nm_ref[...], nv_ref[...] = _adam_update(w_ref[...], g_ref[...], m_ref[...], v_ref[...])

    outs = _pallas_call(body, name=name, out_shape=[_plain(w2.shape, F32)] * 3)(w2, g2, m2, v2)
    return tuple(t.reshape(shape) for t in outs)


def _after(t, token):
    return t + token[0:1, 0:1].astype(t.dtype)


def _local_step(x, mem, pos, tgt, norm_g, mem_norm_g, final_g, wg_in0, late_weights, late_token, c_arr):
    tabs = _rope_tables(pos)
    g0, g1 = _after(norm_g[0:1], late_token), norm_g[1:2]

    hn0, hn0_t, qs, ks, vs, tabs_v, qm0, z0 = _inproj_attn(x, g0, wg_in0, tabs)
    os_, ls_ = [], []
    for j, d in enumerate(DILATIONS):
        if j == 1:
            _, lands = _split_wait(late_weights, _plan_gather_own, ls_[0], "gather_late_wait")
            late_weights, late_token = _split_start([], lands, _plan_gather_pass, 3, ls_[0], "gather_late_pass_start")
        o, l = _attn_fwd(qs[j], ks[j], vs[j], d, late_token)
        os_.append(o)
        ls_.append(l)

    _, gathered = _split_wait(late_weights, _plan_gather_pass, ls_[2], "gather_late_pass_wait")
    wg_out0, wg_in1, wg_out1, wg_kv0, wg_kv1, cw_all = gathered
    w_out1 = wg_out1.reshape(-1, wg_out1.shape[2])
    w_kv = jnp.stack([wg_kv0.reshape(-1, wg_kv0.shape[2]), wg_kv1.reshape(-1, wg_kv1.shape[2])])
    cw = cw_all[:, 0:3].transpose(1, 0, 2).reshape(3, -1)
    kv = _memkv_fwd(mem, mem_norm_g, w_kv)
    y0, y0_t, mo0, h1 = _post_attn(os_, ls_, qm0, kv[0], z0, x, wg_out0)

    hn1, hn1_t, bg, cg, u, qm1, z1 = _inproj_conv(h1, g1, wg_in1)
    y1, y1_t, mo1, dh2, dh2b, loss_acc, d_final_g = _post_conv_loss(
        bg, cg, u, qm1, kv[1], z1, h1, w_out1, cw, final_g.reshape(1, -1), tgt)

    d_w_out1 = _wgrad_rows(y1_t, dh2b, "wgrad_out1")
    dz1, dbg, dconv, dqm1, dkv1 = _bwd_post_conv(dh2b, w_out1, bg, cg, u, z1, qm1, kv[1], mo1, cw)
    dcg, du, dcw = _bwd_conv(dconv, cg, u, cw)
    dh1, dh1b, dg1, dproj1 = _dgrad_norm([(dbg, 1), (dcg, 1), (du, 1), (dqm1, 1), (dz1, 1)], wg_in1, h1, g1, dh2,
                                         "dgrad_norm_conv")
    d_w_in1 = _wgrad_shards(hn1_t, dproj1, "wgrad_in1")

    grads1 = [d_w_in1, d_w_out1]
    started, token = _split_start(grads1, [lax.empty((4,) + g.shape[1:], g.dtype) for g in grads1],
                                  _plan_to_sibling, 4, dg1, "rs1_sibling_start")

    d_w_out0 = _wgrad_cols(y0_t, dh1b, wg_out0.shape[2], "wgrad_out0")
    gw = GROUP_WIDTH
    ones = (jnp.arange(gw)[:, None] // HEAD_DIM == jnp.arange(gw)[None, :] // HEAD_DIM).astype(BF16)
    ones = _after(ones, token)
    res = _bwd_post_attn(dh1b, wg_out0, z0, os_, ls_, qm0, kv[0], mo0, ones)
    dz0, dos, dls, dqm0, dkv0 = res[0], res[1:4], res[4:7], res[7], res[8]

    grads1, from_sibling = _split_wait(started, _plan_to_sibling, dz0, "rs1_sibling_wait")
    parts1 = [_rs_add_sibling(g, r, c_arr, "rs_add_sibling_" + n)
              for g, r, n in zip(grads1, from_sibling, ("conv_w_in", "conv_w_out"))]
    pbs1 = [pb for _, pb in parts1]
    started, token = _split_start(pbs1, [lax.empty((3,) + p.shape[1:], p.dtype) for p in pbs1],
                                  _plan_to_chips, 3, dg1, "rs1_chips_start")

    dqs, dks, dvs = [], [], []
    for j, d in enumerate(DILATIONS):
        dq, dk, dv = _attn_bwd(qs[j], ks[j], vs[j], ls_[j], dos[j], dls[j], tabs_v[j], d, token)
        dqs.append((dq, d))
        dks.append((dk, d))
        dvs.append((dv, d))
    dx, _, dg0, dproj0 = _dgrad_norm(dqs + dks + dvs + [(dqm0, 1), (dz0, 1)], wg_in0, x, g0, dh1, "dgrad_norm_attn")
    d_w_in0 = _wgrad_shards(hn0_t, dproj0, "wgrad_in0")
    _, from_chips1 = _split_wait(started, _plan_to_chips, d_w_in0, "rs1_chips_wait")

    d_w_kv, d_mem_g = _memkv_bwd(jnp.stack([dkv0, dkv1]), w_kv, mem, mem_norm_g)
    small = jnp.concatenate([dg0, dg1, d_mem_g.reshape(2, -1), d_final_g, dcw[0:3]], axis=0)
    n_kv = d_w_kv.shape[1] // N_DEV
    grads0 = [d_w_in0, d_w_out0, d_w_kv[0].reshape(N_DEV, n_kv, -1), d_w_kv[1].reshape(N_DEV, n_kv, -1)]
    return loss_acc[0, 0], dx, grads0, small, [pf for pf, _ in parts1], from_chips1


def kernel(x, mem, positions, norm_g, mem_norm_g, w_mem_kv, attn_w_in, attn_w_out, conv_w_in, conv_w, conv_w_out, final_g, loss_target, m_norm_g, m_mem_norm_g, m_w_mem_kv, m_attn_w_in, m_attn_w_out, m_conv_w_in, m_conv_w, m_conv_w_out, m_final_g, v_norm_g, v_mem_norm_g, v_w_mem_kv, v_attn_w_in, v_attn_w_out, v_conv_w_in, v_conv_w, v_conv_w_out, v_final_g):
    px, py, pc = _position()
    me = 4 * px + 2 * py + pc

    c_arr = jnp.reshape(pc, (1,)).astype(jnp.int32)
    k_arr = jnp.reshape(2 * px + py, (1,)).astype(jnp.int32)
    wg_in0 = _all_gather([attn_w_in[0].astype(BF16)], "gather_w_in0")[0]
    late = [attn_w_out[0].astype(BF16), conv_w_in[0].astype(BF16), conv_w_out[0].astype(BF16),
            w_mem_kv[0].astype(BF16), w_mem_kv[1].astype(BF16), jnp.pad(conv_w[0], ((0, 5), (0, 0)))]
    lands = [lax.dynamic_update_slice(lax.empty((N_DEV,) + t.shape, t.dtype), t[None], (me, 0, 0)) for t in late]
    late_weights, late_token = _split_start(late, lands, _plan_gather_own, 4, wg_in0, "gather_late_start")

    loss_part, dx, grads0, small_part, pfs1, from_chips1 = _local_step(
        x[0], mem[0], positions[0], loss_target[0], norm_g, mem_norm_g, final_g, wg_in0, late_weights, late_token, c_arr)

    names0 = ["attn_w_in", "attn_w_out", "w_mem_kv0", "w_mem_kv1"]
    from_sibling = _rs_to_sibling(grads0)
    parts0 = [_rs_add_sibling(g, r, c_arr, "rs_add_sibling_" + n) for g, r, n in zip(grads0, from_sibling, names0)]
    from_chips0 = _rs_to_chips([pb for _, pb in parts0])
    names = names0 + ["conv_w_in", "conv_w_out"]
    pfs = [pf for pf, _ in parts0] + list(pfs1)
    from_chips = list(from_chips0) + list(from_chips1)
    shards = [attn_w_in[0], attn_w_out[0], w_mem_kv[0], w_mem_kv[1], conv_w_in[0], conv_w_out[0]]
    moments = [(m_attn_w_in[0], v_attn_w_in[0]), (m_attn_w_out[0], v_attn_w_out[0]), (m_w_mem_kv[0], v_w_mem_kv[0]),
               (m_w_mem_kv[1], v_w_mem_kv[1]), (m_conv_w_in[0], v_conv_w_in[0]), (m_conv_w_out[0], v_conv_w_out[0])]
    big = {}
    for n, w, pf, r, (m, v) in zip(names, shards, pfs, from_chips, moments):
        big[n] = _rs_finish_adamw(pf, r, k_arr, w, m, v, "rs_finish_adamw_" + n)
    for n in ("attn_w_in", "attn_w_out", "conv_w_in", "conv_w_out"):
        big[n] = tuple(t[None] for t in big[n])
    big["w_mem_kv"] = tuple(jnp.stack([a, b]) for a, b in zip(big["w_mem_kv0"], big["w_mem_kv1"]))

    small_part = jnp.concatenate([small_part, jnp.broadcast_to(loss_part, small_part.shape)], axis=0)
    small = _sum_devices(_all_gather([small_part], "gather_small_grads")[0])
    loss = small[8, 0]
    g_conv_w = lax.dynamic_slice(small[5:8], (0, me * LANES), (3, LANES))[None]
    small_g = dict(norm_g=small[0:2], mem_norm_g=small[2:4], conv_w=g_conv_w, final_g=small[4])
    small_w = dict(norm_g=(norm_g, m_norm_g, v_norm_g), mem_norm_g=(mem_norm_g, m_mem_norm_g, v_mem_norm_g),
                   conv_w=(conv_w, m_conv_w, v_conv_w), final_g=(final_g, m_final_g, v_final_g))
    for n, (w, m, v) in small_w.items():
        big[n] = (small_g[n],) + _adamw(w, small_g[n], m, v, "adamw_" + n)

    order = ["norm_g", "mem_norm_g", "w_mem_kv", "attn_w_in", "attn_w_out", "conv_w_in", "conv_w", "conv_w_out", "final_g"]
    return (loss, dx[None], *[big[n][0] for n in order], *[big[n][1] for n in order],
            *[big[n][2] for n in order], *[big[n][3] for n in order])
```

```python
import functools

import jax
import jax.numpy as jnp
from jax import lax
from jax.experimental import pallas as pl
from jax.experimental.pallas import tpu as pltpu

F32 = jnp.float32
BF16 = jnp.bfloat16

N_DEV = 8
D_MODEL = 1024
HEAD_DIM = 64
ROT_DIM = HEAD_DIM // 4
ROPE_THETA = 500000.0
DILATIONS = (1, 4, 16)
HEADS_PER_GROUP = 8
HEAD_BATCH = 8
GROUP_WIDTH = HEADS_PER_GROUP * HEAD_DIM
BLOCK = 128
N_MEM = 256
MEM_HEADS = 4
MEM_WIDTH = MEM_HEADS * HEAD_DIM
CONV_WIDTH = D_MODEL
EPS = 1e-6
SCALE = HEAD_DIM ** -0.5
NEG = -1e30

ADAM_LR = 0.001
ADAM_B1 = 0.9
ADAM_B2 = 0.999
ADAM_EPS = 1e-08
ADAM_WD = 0.01
ADAM_STEP = 10

ROW_TILE = 256
LANES = 128
MESH = pl.DeviceIdType.MESH
ANY = pl.BlockSpec(memory_space=pl.ANY)


def _pallas_call(body, **kw):
    call = pl.pallas_call(body, **kw)

    def run(*args):
        pinned = [pltpu.with_memory_space_constraint(a, pltpu.HBM) if jnp.issubdtype(a.dtype, jnp.floating) else a
                  for a in args]
        return call(*pinned)

    return run


def _dot(a, b):
    return lax.dot_general(a, b, (((1,), (0,)), ((), ())), preferred_element_type=F32)


def _dot_nt(a, b):
    return lax.dot_general(a, b, (((1,), (1,)), ((), ())), preferred_element_type=F32)


def _dot_tn(a, b):
    return lax.dot_general(a, b, (((0,), (0,)), ((), ())), preferred_element_type=F32)


def _params(n_grid, vmem_mb=48):
    return pltpu.CompilerParams(dimension_semantics=("arbitrary",) * n_grid, vmem_limit_bytes=vmem_mb << 20)


def _rows(width, tm=ROW_TILE):
    return pl.BlockSpec((tm, width), lambda i: (i, 0))


def _view_rows(width, d, tm=ROW_TILE):
    return pl.BlockSpec((tm // d, d * width), lambda i: (i, 0))


def _whole(shape):
    return pl.BlockSpec(shape, lambda *_: (0,) * len(shape))


def _sds(shape, dtype):
    return pltpu.HBM(shape, dtype)


def _plain(shape, dtype):
    return jax.ShapeDtypeStruct(shape, dtype)


def _silu_parts(z):
    sg = jax.nn.sigmoid(z)
    return z * sg, sg * (1.0 + z * (1.0 - sg))


def _to_view(scr, val, out_ref, d):
    tm, w = val.shape
    if d == 1:
        out_ref[...] = val.astype(out_ref.dtype)
        return
    for cb in range(w // LANES):
        scr[cb] = val[:, cb * LANES:(cb + 1) * LANES]
    for r in range(d):
        for cb in range(w // LANES):
            lo = r * w + cb * LANES
            out_ref[:, lo:lo + LANES] = scr[cb, pl.ds(r, tm // d, stride=d), :].astype(out_ref.dtype)


def _from_view(scr, in_ref, d):
    if d == 1:
        return in_ref[...].astype(F32)
    nc, tm, _ = scr.shape
    w = nc * LANES
    for r in range(d):
        for cb in range(nc):
            lo = r * w + cb * LANES
            scr[cb, pl.ds(r, tm // d, stride=d), :] = in_ref[:, lo:lo + LANES].astype(F32)
    return jnp.concatenate([scr[cb] for cb in range(nc)], axis=1)


def _view_scratch(width, tm=ROW_TILE):
    return pltpu.VMEM((width // LANES, tm, LANES), F32)


def _rope_tables(pos):
    half = ROT_DIM // 2
    inv_freq = ROPE_THETA ** (-jnp.arange(half, dtype=F32) * (2.0 / ROT_DIM))
    ang = pos.astype(F32)[:, None] * inv_freq
    cos, sin = jnp.cos(ang), jnp.sin(ang)
    s = pos.shape[0]
    z8 = jnp.zeros((s, half), F32)
    rest = HEAD_DIM - ROT_DIM
    cosf = jnp.concatenate([cos, cos, jnp.ones((s, rest), F32)], axis=1)
    sa = jnp.concatenate([-sin, z8, jnp.zeros((s, rest), F32)], axis=1)
    sb = jnp.concatenate([z8, sin, jnp.zeros((s, rest), F32)], axis=1)
    return tuple(jnp.tile(t, (1, LANES // HEAD_DIM)) for t in (cosf, sa, sb))


def _rope_fwd(t, cv, sav, sbv):
    w = t.shape[1]
    return t * cv + pltpu.roll(t, w - ROT_DIM // 2, 1) * sav + pltpu.roll(t, ROT_DIM // 2, 1) * sbv


def _rope_bwd(g, cv, sav, sbv):
    w = g.shape[1]
    return g * cv + pltpu.roll(g * sav, ROT_DIM // 2, 1) + pltpu.roll(g * sbv, w - ROT_DIM // 2, 1)


def _project(hn, wg_ref, proj_scr):
    c = wg_ref.shape[2]
    for j in range(N_DEV):
        proj_scr[:, j * c:(j + 1) * c] = _dot(hn, wg_ref[j])


def _inproj_attn(x, g, wg, tabs):
    s, d_model = x.shape
    gw = GROUP_WIDTH
    n = N_DEV * wg.shape[2]
    nz = n - 9 * gw - MEM_WIDTH
    reps = gw // LANES
    tm = ROW_TILE

    def body(x_ref, g_ref, w_ref, c_ref, sa_ref, sb_ref, hn_ref, hnt_ref, *rest):
        outs, (proj, scr, tscr) = rest[:-3], rest[-3:]
        q_refs, k_refs, v_refs, t_refs, qm_ref, z_ref = outs[0:3], outs[3:6], outs[6:9], outs[9:18], outs[18], outs[19]
        xb = x_ref[...]
        r = lax.rsqrt(jnp.mean(xb * xb, axis=-1, keepdims=True) + EPS)
        hn = ((xb * r) * g_ref[...]).astype(BF16)
        hn_ref[...] = hn
        hnt_ref[...] = hn.T
        _project(hn, w_ref, proj)
        tab = (c_ref[...], sa_ref[...], sb_ref[...])
        cv, sav, sbv = [jnp.tile(t, (1, reps)) for t in tab]
        for j, d in enumerate(DILATIONS):
            tq = _rope_fwd(proj[:, j * gw:(j + 1) * gw], cv, sav, sbv)
            _to_view(scr, tq * SCALE, q_refs[j], d)
            tk = _rope_fwd(proj[:, (3 + j) * gw:(4 + j) * gw], cv, sav, sbv)
            _to_view(scr, tk, k_refs[j], d)
            _to_view(scr, proj[:, (6 + j) * gw:(7 + j) * gw], v_refs[j], d)
            for i in range(3):
                _to_view(tscr, tab[i], t_refs[3 * j + i], d)
        qm_ref[...] = proj[:, 9 * gw:9 * gw + MEM_WIDTH].astype(BF16)
        z_ref[...] = proj[:, 9 * gw + MEM_WIDTH:]

    views = [_sds((s // d, d * gw), BF16) for d in DILATIONS]
    tviews = [_sds((s // d, d * LANES), F32) for d in DILATIONS for _ in range(3)]
    out_shape = ([_sds((s, d_model), BF16), _sds((d_model, s), BF16)] + views * 3 + tviews
                 + [_sds((s, MEM_WIDTH), BF16), _sds((s, nz), F32)])
    vspecs = [_view_rows(gw, d) for d in DILATIONS]
    tspecs = [_view_rows(LANES, d) for d in DILATIONS for _ in range(3)]
    out_specs = ([_rows(d_model), pl.BlockSpec((d_model, tm), lambda i: (0, i))] + vspecs * 3 + tspecs
                 + [_rows(MEM_WIDTH), _rows(nz)])
    res = _pallas_call(
        body, name="inproj_attn", grid=(s // tm,), out_shape=out_shape,
        in_specs=[_rows(d_model), _whole((1, d_model)), _whole(wg.shape), _rows(LANES), _rows(LANES), _rows(LANES)],
        out_specs=out_specs,
        scratch_shapes=[pltpu.VMEM((tm, n), F32), _view_scratch(gw), _view_scratch(LANES)],
        compiler_params=_params(1, 60),
    )(x, g, wg, *tabs)
    tabs_v = [res[11 + 3 * j:14 + 3 * j] for j in range(3)]
    return res[0], res[1], res[2:5], res[5:8], res[8:11], tabs_v, res[20], res[21]


def _stream_maps(d, nb):
    def cur(n):
        return (n % nb, n // nb)

    def prev(n):
        return (jnp.maximum(n % nb - 1, 0), n // nb)

    return cur, prev


def _band_mask(n_keys):
    qi = lax.broadcasted_iota(jnp.int32, (BLOCK, n_keys), 0)
    kj = lax.broadcasted_iota(jnp.int32, (BLOCK, n_keys), 1)
    if n_keys == BLOCK:
        return kj <= qi
    return jnp.logical_or(jnp.logical_and(kj < BLOCK, kj >= qi), jnp.logical_and(kj >= BLOCK, (kj - BLOCK) <= qi))


def _per_stream_block(n, nb, run):
    if nb == 1:
        run(False)
        return
    first = (n % nb) == 0
    pl.when(first)(lambda: run(False))
    pl.when(jnp.logical_not(first))(lambda: run(True))


def _attn_fwd(q, k, v, d, token):
    ln, dw = q.shape
    w = dw // d
    nb = ln // BLOCK
    nblk = d * nb
    cur, prev = _stream_maps(d, nb)

    def body(_, q_ref, kp_ref, kc_ref, vp_ref, vc_ref, o_ref, lse_ref):
        def run(with_prev):
            if with_prev:
                kk = jnp.concatenate([kp_ref[...], kc_ref[...]], axis=0)
                vv = jnp.concatenate([vp_ref[...], vc_ref[...]], axis=0)
            else:
                kk, vv = kc_ref[...], vc_ref[...]
            valid = _band_mask(kk.shape[0])
            hb = HEAD_BATCH
            for h0 in range(0, HEADS_PER_GROUP, hb):
                sls = [slice(h * HEAD_DIM, (h + 1) * HEAD_DIM) for h in range(h0, h0 + hb)]
                scs = [jnp.where(valid, _dot_nt(q_ref[:, sl], kk[:, sl]), NEG) for sl in sls]
                ms = [jnp.max(sc, axis=-1, keepdims=True) for sc in scs]
                ps = [jnp.exp(sc - m) for sc, m in zip(scs, ms)]
                ls = [jnp.sum(p, axis=-1, keepdims=True) for p in ps]
                pns = [(p * (1.0 / l)).astype(BF16) for p, l in zip(ps, ls)]
                for sl, pn, m, l in zip(sls, pns, ms, ls):
                    o_ref[:, sl] = _dot(pn, vv[:, sl])
                    lse_ref[:, sl] = jnp.broadcast_to(m + jnp.log(l), (BLOCK, HEAD_DIM))

        _per_stream_block(pl.program_id(0), nb, run)

    blk = lambda f: pl.BlockSpec((BLOCK, w), f)
    return _pallas_call(
        body, name=f"attn_fwd_d{d}", grid=(nblk,),
        out_shape=[_sds((ln, dw), F32)] * 2,
        in_specs=[ANY, blk(cur), blk(prev), blk(cur), blk(prev), blk(cur)],
        out_specs=[blk(cur), blk(cur)], compiler_params=_params(1, 32),
    )(token, q, k, k, v, v)


def _memkv_fwd(mem, g, w):
    n_layers = w.shape[0]

    def body(mem_ref, g_ref, w_ref, kv_ref):
        mb = mem_ref[...]
        r = lax.rsqrt(jnp.mean(mb * mb, axis=-1, keepdims=True) + EPS)
        mn = ((mb * r) * g_ref[...]).astype(BF16)
        kv_ref[...] = _dot(mn, w_ref[...]).astype(BF16)

    return _pallas_call(
        body, name="memkv_fwd", grid=(n_layers,),
        out_shape=_plain((n_layers, N_MEM, 2 * MEM_WIDTH), BF16),
        in_specs=[_whole(mem.shape), pl.BlockSpec((None, 1, D_MODEL), lambda l: (l, 0, 0)),
                  pl.BlockSpec((None, D_MODEL, 2 * MEM_WIDTH), lambda l: (l, 0, 0))],
        out_specs=pl.BlockSpec((None, N_MEM, 2 * MEM_WIDTH), lambda l: (l, 0, 0)),
        compiler_params=_params(1, 32),
    )(mem, g.reshape(n_layers, 1, D_MODEL), w)


def _mix_groups(os_, ls_):
    mx = jnp.maximum(jnp.maximum(ls_[0], ls_[1]), ls_[2])
    es = [jnp.exp(t - mx) for t in ls_]
    inv = 1.0 / (es[0] + es[1] + es[2])
    ws = [e * inv for e in es]
    mix = ws[0] * os_[0] + ws[1] * os_[1] + ws[2] * os_[2]
    return ws, mix


MEM_SLICES = [slice(h * HEAD_DIM, (h + 1) * HEAD_DIM) for h in range(MEM_HEADS)]


def _mem_probs(qm, km):
    scs = [_dot_nt(qm[:, sl], km[:, sl]) * SCALE for sl in MEM_SLICES]
    es = [jnp.exp(sc - jnp.max(sc, axis=-1, keepdims=True)) for sc in scs]
    return [e * (1.0 / jnp.sum(e, axis=-1, keepdims=True)) for e in es]


def _mem_attn_into(qm, kv_ref, mo_ref):
    km, vm = kv_ref[:, :MEM_WIDTH], kv_ref[:, MEM_WIDTH:]
    ps = [p.astype(BF16) for p in _mem_probs(qm, km)]
    for sl, p in zip(MEM_SLICES, ps):
        mo_ref[:, sl] = _dot(p, vm[:, sl])


def _mem_attn_bwd(qm, kv_ref, dmem, dqm_ref, dkv_ref):
    km, vm = kv_ref[:, :MEM_WIDTH], kv_ref[:, MEM_WIDTH:]
    dmb = dmem.astype(BF16)
    ps = _mem_probs(qm, km)
    dps = [_dot_nt(dmb[:, sl], vm[:, sl]) for sl in MEM_SLICES]
    dss = [(p * (dp - jnp.sum(dp * p, axis=-1, keepdims=True)) * SCALE).astype(BF16) for p, dp in zip(ps, dps)]
    pbs = [p.astype(BF16) for p in ps]
    for h, (sl, ds, pb) in enumerate(zip(MEM_SLICES, dss, pbs)):
        slv = slice(MEM_WIDTH + h * HEAD_DIM, MEM_WIDTH + (h + 1) * HEAD_DIM)
        dqm_ref[:, sl] = _dot(ds, km[:, sl]).astype(BF16)
        dkv_ref[:, sl] += _dot_tn(ds, qm[:, sl])
        dkv_ref[:, slv] += _dot_tn(pb, dmb[:, sl])


def _post_attn(os_, ls_, qm, kv, z, x, wg_out):
    s, d_model = x.shape
    gw = GROUP_WIDTH
    nb = gw + MEM_WIDTH
    c = wg_out.shape[2]
    tm = ROW_TILE

    def body(o0, o1, o2, l0, l1, l2, qm_ref, kv_ref, z_ref, x_ref, w_ref, y_ref, yt_ref, mo_ref, h_ref, s0, s1):
        ov, lv = [], []
        for o_ref, l_ref, d in zip((o0, o1, o2), (l0, l1, l2), DILATIONS):
            ov.append(_from_view(s0, o_ref, d))
            lv.append(_from_view(s1, l_ref, d))
        _, mix = _mix_groups(ov, lv)
        _mem_attn_into(qm_ref[...], kv_ref, mo_ref)
        sz, _ = _silu_parts(z_ref[...])
        y_ref[:, :gw] = (mix * sz[:, :gw]).astype(BF16)
        y_ref[:, gw:] = (mo_ref[...] * sz[:, gw:]).astype(BF16)
        y = y_ref[...]
        yt_ref[...] = y.T
        for j in range(N_DEV):
            h_ref[:, j * c:(j + 1) * c] = x_ref[:, j * c:(j + 1) * c] + _dot(y, w_ref[j])

    vspecs = [_view_rows(gw, d) for d in DILATIONS]
    return _pallas_call(
        body, name="post_attn", grid=(s // tm,),
        out_shape=[_sds((s, nb), BF16), _sds((nb, s), BF16), _sds((s, MEM_WIDTH), F32), _sds((s, d_model), F32)],
        in_specs=vspecs * 2 + [_rows(MEM_WIDTH), _whole(kv.shape), _rows(nb), _rows(d_model), _whole(wg_out.shape)],
        out_specs=[_rows(nb), pl.BlockSpec((nb, tm), lambda i: (0, i)), _rows(MEM_WIDTH), _rows(d_model)],
        scratch_shapes=[_view_scratch(gw), _view_scratch(gw)],
        compiler_params=_params(1, 40),
    )(*os_, *ls_, qm, kv, z, x, wg_out)


def _inproj_conv(x, g, wg):
    s, d_model = x.shape
    c = CONV_WIDTH
    n = N_DEV * wg.shape[2]
    nz = n - 3 * c - MEM_WIDTH
    tm = ROW_TILE

    def body(x_ref, g_ref, w_ref, hn_ref, hnt_ref, bg_ref, cg_ref, u_ref, qm_ref, z_ref, proj):
        xb = x_ref[...]
        r = lax.rsqrt(jnp.mean(xb * xb, axis=-1, keepdims=True) + EPS)
        hn = ((xb * r) * g_ref[...]).astype(BF16)
        hn_ref[...] = hn
        hnt_ref[...] = hn.T
        _project(hn, w_ref, proj)
        bg_ref[...] = proj[:, 0:c]
        cg_ref[...] = proj[:, c:2 * c]
        u_ref[...] = proj[:, 2 * c:3 * c]
        qm_ref[...] = proj[:, 3 * c:3 * c + MEM_WIDTH].astype(BF16)
        z_ref[...] = proj[:, 3 * c + MEM_WIDTH:]

    return _pallas_call(
        body, name="inproj_conv", grid=(s // tm,),
        out_shape=[_sds((s, d_model), BF16), _sds((d_model, s), BF16)] + [_sds((s, c), F32)] * 3
                  + [_sds((s, MEM_WIDTH), BF16), _sds((s, nz), F32)],
        in_specs=[_rows(d_model), _whole((1, d_model)), _whole(wg.shape)],
        out_specs=[_rows(d_model), pl.BlockSpec((d_model, tm), lambda i: (0, i))] + [_rows(c)] * 3
                  + [_rows(MEM_WIDTH), _rows(nz)],
        scratch_shapes=[pltpu.VMEM((tm, n), F32)],
        compiler_params=_params(1, 60),
    )(x, g, wg)


HALO = 8


def _halo_before(width, tm=ROW_TILE):
    return pl.BlockSpec((HALO, width), lambda i: (jnp.maximum(i * (tm // HALO) - 1, 0), 0))


def _halo_after(width, n_rows, tm=ROW_TILE):
    return pl.BlockSpec((HALO, width), lambda i: (jnp.minimum((i + 1) * (tm // HALO), n_rows // HALO - 1), 0))


def _conv_taps(cg_ref, u_ref, cgh_ref, uh_ref, i):
    a = cg_ref[...] * u_ref[...]
    ah = jnp.where(i > 0, cgh_ref[...] * uh_ref[...], 0.0)
    row = lax.broadcasted_iota(jnp.int32, a.shape, 0)
    a1 = jnp.where(row == 0, ah[HALO - 1:HALO], pltpu.roll(a, 1, 0))
    a2 = jnp.where(row == 0, ah[HALO - 2:HALO - 1], jnp.where(row == 1, ah[HALO - 1:HALO], pltpu.roll(a, 2, 0)))
    return a, a1, a2


def _post_conv_loss(bg, cg, u, qm, kv, z, h1, w_out, cw, gf, tgt):
    s, d = h1.shape
    c = CONV_WIDTH
    nb = c + MEM_WIDTH
    tm = ROW_TILE

    def body(bg_ref, cg_ref, u_ref, cgh_ref, uh_ref, qm_ref, kv_ref, z_ref, h_ref, w_ref, cw_ref, gf_ref, t_ref,
             y_ref, yt_ref, mo_ref, dh_ref, dhb_ref, loss_ref, dgf_ref):
        i = pl.program_id(0)
        a, a1, a2 = _conv_taps(cg_ref, u_ref, cgh_ref, uh_ref, i)
        conv = cw_ref[0:1, :] * a2 + cw_ref[1:2, :] * a1 + cw_ref[2:3, :] * a
        mix = bg_ref[...] * conv
        _mem_attn_into(qm_ref[...], kv_ref, mo_ref)
        sz, _ = _silu_parts(z_ref[...])
        y_ref[:, :c] = (mix * sz[:, :c]).astype(BF16)
        y_ref[:, c:] = (mo_ref[...] * sz[:, c:]).astype(BF16)
        y = y_ref[...]
        yt_ref[...] = y.T
        h2 = h_ref[...] + _dot(y, w_ref[...])
        r = lax.rsqrt(jnp.mean(h2 * h2, axis=-1, keepdims=True) + EPS)
        nh = h2 * r
        gfv = gf_ref[...]
        diff = nh * gfv - t_ref[...]
        dout = diff * (1.0 / d)
        dn = dout * gfv
        dh2 = r * dn - h2 * ((r * r * r) * jnp.mean(dn * h2, axis=-1, keepdims=True))
        dh_ref[...] = dh2
        dhb_ref[...] = dh2.astype(BF16)

        @pl.when(i == 0)
        def _():
            loss_ref[...] = jnp.zeros_like(loss_ref)
            dgf_ref[...] = jnp.zeros_like(dgf_ref)

        loss_ref[...] += 0.5 * jnp.sum(jnp.mean(diff * diff, axis=-1, keepdims=True))
        dgf_ref[...] += jnp.sum(dout * nh, axis=0, keepdims=True)

    return _pallas_call(
        body, name="post_conv_loss", grid=(s // tm,),
        out_shape=[_sds((s, nb), BF16), _sds((nb, s), BF16), _sds((s, MEM_WIDTH), F32), _sds((s, d), F32),
                   _sds((s, d), BF16), _plain((8, LANES), F32), _plain((1, d), F32)],
        in_specs=[_rows(c)] * 3 + [_halo_before(c)] * 2 + [_rows(MEM_WIDTH), _whole(kv.shape), _rows(nb), _rows(d),
                  _whole(w_out.shape), _whole(cw.shape), _whole((1, d)), _rows(d)],
        out_specs=[_rows(nb), pl.BlockSpec((nb, tm), lambda i: (0, i)), _rows(MEM_WIDTH), _rows(d), _rows(d),
                   _whole((8, LANES)), _whole((1, d))],
        compiler_params=_params(1, 48),
    )(bg, cg, u, cg, u, qm, kv, z, h1, w_out, cw, gf, tgt)


def _bwd_post_conv(dhb, w_out, bg, cg, u, z, qm, kv, mo, cw):
    s = dhb.shape[0]
    c = CONV_WIDTH
    nb = c + MEM_WIDTH

    def body(dh_ref, w_ref, bg_ref, cg_ref, u_ref, cgh_ref, uh_ref, z_ref, qm_ref, kv_ref, mo_ref, cw_ref,
             dz_ref, dbg_ref, dc_ref, dqm_ref, dkv_ref):
        i = pl.program_id(0)

        @pl.when(i == 0)
        def _():
            dkv_ref[...] = jnp.zeros_like(dkv_ref)

        dy = _dot_nt(dh_ref[...], w_ref[...])
        sz, dsz = _silu_parts(z_ref[...])
        a, a1, a2 = _conv_taps(cg_ref, u_ref, cgh_ref, uh_ref, i)
        conv = cw_ref[0:1, :] * a2 + cw_ref[1:2, :] * a1 + cw_ref[2:3, :] * a
        bgv = bg_ref[...]
        dz_ref[:, :c] = (dy[:, :c] * (bgv * conv) * dsz[:, :c]).astype(BF16)
        dz_ref[:, c:] = (dy[:, c:] * mo_ref[...] * dsz[:, c:]).astype(BF16)
        dbr = dy * sz
        dmix = dbr[:, :c]
        dbg_ref[...] = (dmix * conv).astype(BF16)
        dc_ref[...] = dmix * bgv
        _mem_attn_bwd(qm_ref[...], kv_ref, dbr[:, c:], dqm_ref, dkv_ref)

    return _pallas_call(
        body, name="bwd_post_conv", grid=(s // ROW_TILE,),
        out_shape=[_sds((s, nb), BF16), _sds((s, c), BF16), _sds((s, c), F32), _sds((s, MEM_WIDTH), BF16),
                   _plain(kv.shape, F32)],
        in_specs=[_rows(D_MODEL), _whole(w_out.shape)] + [_rows(c)] * 3 + [_halo_before(c)] * 2
                 + [_rows(nb), _rows(MEM_WIDTH), _whole(kv.shape), _rows(MEM_WIDTH), _whole(cw.shape)],
        out_specs=[_rows(nb), _rows(c), _rows(c), _rows(MEM_WIDTH), _whole(kv.shape)],
        compiler_params=_params(1, 48),
    )(dhb, w_out, bg, cg, u, cg, u, z, qm, kv, mo, cw)


def _bwd_conv(dconv, cg, u, cw):
    s, c = dconv.shape
    tm = ROW_TILE
    last = s // tm - 1

    def body(dc_ref, dcn_ref, cg_ref, u_ref, cgh_ref, uh_ref, cw_ref, dcg_ref, du_ref, dcw_ref):
        i = pl.program_id(0)

        @pl.when(i == 0)
        def _():
            dcw_ref[...] = jnp.zeros_like(dcw_ref)

        dc = dc_ref[...]
        dcn = jnp.where(i < last, dcn_ref[...], 0.0)
        row = lax.broadcasted_iota(jnp.int32, dc.shape, 0)
        d1 = jnp.where(row == tm - 1, dcn[0:1], pltpu.roll(dc, tm - 1, 0))
        d2 = jnp.where(row == tm - 1, dcn[1:2], jnp.where(row == tm - 2, dcn[0:1], pltpu.roll(dc, tm - 2, 0)))
        da = cw_ref[2:3, :] * dc + cw_ref[1:2, :] * d1 + cw_ref[0:1, :] * d2
        a, a1, a2 = _conv_taps(cg_ref, u_ref, cgh_ref, uh_ref, i)
        dcg_ref[...] = (da * u_ref[...]).astype(BF16)
        du_ref[...] = (da * cg_ref[...]).astype(BF16)
        dcw_ref[0:1, :] += jnp.sum(dc * a2, axis=0, keepdims=True)
        dcw_ref[1:2, :] += jnp.sum(dc * a1, axis=0, keepdims=True)
        dcw_ref[2:3, :] += jnp.sum(dc * a, axis=0, keepdims=True)

    return _pallas_call(
        body, name="bwd_conv", grid=(s // tm,),
        out_shape=[_sds((s, c), BF16), _sds((s, c), BF16), _plain((8, c), F32)],
        in_specs=[_rows(c), _halo_after(c, s), _rows(c), _rows(c), _halo_before(c), _halo_before(c), _whole(cw.shape)],
        out_specs=[_rows(c), _rows(c), _whole((8, c))], compiler_params=_params(1, 40),
    )(dconv, dconv, cg, u, cg, u, cw)


def _dgrad_norm(pieces, wg, h, g, dres, name):
    s, d_model = h.shape
    c = wg.shape[2]
    n = N_DEV * c
    tm = ROW_TILE
    widths = [p.shape[1] // d for p, d in pieces]
    assert sum(widths) == n
    n_p = len(pieces)

    def body(*refs):
        p_refs = refs[:n_p]
        w_ref, h_ref, g_ref, dr_ref, dh_ref, dhb_ref, dg_ref, dpd_ref, dp, scr = refs[n_p:]

        @pl.when(pl.program_id(0) == 0)
        def _():
            dg_ref[...] = jnp.zeros_like(dg_ref)

        off = 0
        for p_ref, (_, d), wd in zip(p_refs, pieces, widths):
            if d == 1:
                dp[:, off:off + wd] = p_ref[...]
            else:
                dp[:, off:off + wd] = _from_view(scr, p_ref, d).astype(BF16)
            off += wd
        dhn = jnp.zeros((tm, d_model), F32)
        for j in range(N_DEV):
            dpj = dp[:, j * c:(j + 1) * c]
            dpd_ref[j] = dpj
            dhn += _dot_nt(dpj, w_ref[j])
        hb = h_ref[...]
        r = lax.rsqrt(jnp.mean(hb * hb, axis=-1, keepdims=True) + EPS)
        dg_ref[...] += jnp.sum(dhn * (hb * r), axis=0, keepdims=True)
        dn = dhn * g_ref[...]
        dh = dr_ref[...] + r * dn - hb * ((r * r * r) * jnp.mean(dn * hb, axis=-1, keepdims=True))
        dh_ref[...] = dh
        dhb_ref[...] = dh.astype(BF16)

    p_specs = [_view_rows(wd, d) for (_, d), wd in zip(pieces, widths)]
    return _pallas_call(
        body, name=name, grid=(s // tm,),
        out_shape=[_plain((s, d_model), F32), _sds((s, d_model), BF16), _plain((1, d_model), F32), _sds((N_DEV, s, c), BF16)],
        in_specs=p_specs + [_whole(wg.shape), _rows(d_model), _whole((1, d_model)), _rows(d_model)],
        out_specs=[_rows(d_model), _rows(d_model), _whole((1, d_model)), pl.BlockSpec((N_DEV, tm, c), lambda i: (0, i, 0))],
        scratch_shapes=[pltpu.VMEM((tm, n), BF16), _view_scratch(GROUP_WIDTH)],
        compiler_params=_params(1, 60),
    )(*[p for p, _ in pieces], wg, h, g, dres)


def _wgrad_shards(a_t, b_dm, name):
    m, s = a_t.shape
    c = b_dm.shape[2]

    def body(a_ref, b_ref, o_ref):
        o_ref[...] = _dot(a_ref[...], b_ref[...]).astype(BF16)

    return _pallas_call(
        body, name=name, grid=(N_DEV,), out_shape=_sds((N_DEV, m, c), BF16),
        in_specs=[_whole(a_t.shape), pl.BlockSpec((None, s, c), lambda j: (j, 0, 0))],
        out_specs=pl.BlockSpec((None, m, c), lambda j: (j, 0, 0)), compiler_params=_params(1, 40),
    )(a_t, b_dm)


def _wgrad_cols(a_t, b, c, name):
    m, s = a_t.shape

    def body(a_ref, b_ref, o_ref):
        o_ref[...] = _dot(a_ref[...], b_ref[...]).astype(BF16)

    return _pallas_call(
        body, name=name, grid=(N_DEV,), out_shape=_sds((N_DEV, m, c), BF16),
        in_specs=[_whole(a_t.shape), pl.BlockSpec((s, c), lambda j: (0, j))],
        out_specs=pl.BlockSpec((None, m, c), lambda j: (j, 0, 0)), compiler_params=_params(1, 40),
    )(a_t, b)


def _wgrad_rows(a_t, b, name):
    m, s = a_t.shape
    n = b.shape[1]
    mr = m // N_DEV

    def body(a_ref, b_ref, o_ref):
        o_ref[...] = _dot(a_ref[...], b_ref[...]).astype(BF16)

    return _pallas_call(
        body, name=name, grid=(N_DEV,), out_shape=_sds((N_DEV, mr, n), BF16),
        in_specs=[pl.BlockSpec((mr, s), lambda j: (j, 0)), _whole(b.shape)],
        out_specs=pl.BlockSpec((None, mr, n), lambda j: (j, 0, 0)), compiler_params=_params(1, 40),
    )(a_t, b)


def _memkv_bwd(dkv, w, mem, g):
    n_layers = w.shape[0]

    def body(dkv_ref, w_ref, mem_ref, g_ref, dw_ref, dg_ref):
        mb = mem_ref[...]
        r = lax.rsqrt(jnp.mean(mb * mb, axis=-1, keepdims=True) + EPS)
        nm = mb * r
        mn = (nm * g_ref[...]).astype(BF16)
        dkvb = dkv_ref[...].astype(BF16)
        dw_ref[...] = _dot_tn(mn, dkvb).astype(BF16)
        dmn = _dot_nt(dkvb, w_ref[...])
        dg_ref[...] = jnp.sum(dmn * nm, axis=0, keepdims=True)

    lay = lambda *shape: pl.BlockSpec((None,) + shape, lambda l: (l, 0, 0))
    return _pallas_call(
        body, name="memkv_bwd", grid=(n_layers,),
        out_shape=[_plain((n_layers, D_MODEL, 2 * MEM_WIDTH), BF16), _plain((n_layers, 1, D_MODEL), F32)],
        in_specs=[lay(N_MEM, 2 * MEM_WIDTH), lay(D_MODEL, 2 * MEM_WIDTH), _whole(mem.shape), lay(1, D_MODEL)],
        out_specs=[lay(D_MODEL, 2 * MEM_WIDTH), lay(1, D_MODEL)], compiler_params=_params(1, 32),
    )(dkv, w, mem, g.reshape(n_layers, 1, D_MODEL))


def _bwd_post_attn(dhb, wg_out, z, os_, ls_, qm, kv, mo, head_ones):
    s = dhb.shape[0]
    gw = GROUP_WIDTH
    nb = gw + MEM_WIDTH
    c = wg_out.shape[2]
    tm = ROW_TILE

    def body(dh_ref, w_ref, z_ref, o0, o1, o2, l0, l1, l2, qm_ref, kv_ref, mo_ref, bd_ref,
             dz_ref, do0, do1, do2, dl0, dl1, dl2, dqm_ref, dkv_ref, s0, s1):
        @pl.when(pl.program_id(0) == 0)
        def _():
            dkv_ref[...] = jnp.zeros_like(dkv_ref)

        dy = jnp.zeros((tm, nb), F32)
        for j in range(N_DEV):
            dy += _dot_nt(dh_ref[:, j * c:(j + 1) * c], w_ref[j])
        ov, lv = [], []
        for o_ref, l_ref, d in zip((o0, o1, o2), (l0, l1, l2), DILATIONS):
            ov.append(_from_view(s0, o_ref, d))
            lv.append(_from_view(s1, l_ref, d))
        ws, mix = _mix_groups(ov, lv)
        sz, dsz = _silu_parts(z_ref[...])
        dz_ref[:, :gw] = (dy[:, :gw] * mix * dsz[:, :gw]).astype(BF16)
        dz_ref[:, gw:] = (dy[:, gw:] * mo_ref[...] * dsz[:, gw:]).astype(BF16)
        dbr = dy * sz
        dmix = dbr[:, :gw]
        t = dmix * mix
        th = t.astype(BF16)
        tl = (t - th.astype(F32)).astype(BF16)
        rs = _dot(th, bd_ref[...]) + _dot(tl, bd_ref[...])
        for wg_, do_ref, dl_ref, d in zip(ws, (do0, do1, do2), (dl0, dl1, dl2), DILATIONS):
            _to_view(s0, wg_ * dmix, do_ref, d)
            _to_view(s1, wg_ * rs, dl_ref, d)
        _mem_attn_bwd(qm_ref[...], kv_ref, dbr[:, gw:], dqm_ref, dkv_ref)

    vspecs = [_view_rows(gw, d) for d in DILATIONS]
    return _pallas_call(
        body, name="bwd_post_attn", grid=(s // tm,),
        out_shape=[_sds((s, nb), BF16)] + [_sds((s // d, d * gw), BF16) for d in DILATIONS]
                  + [_sds((s // d, d * gw), F32) for d in DILATIONS] + [_sds((s, MEM_WIDTH), BF16), _plain(kv.shape, F32)],
        in_specs=[_rows(D_MODEL), _whole(wg_out.shape), _rows(nb)] + vspecs * 2
                 + [_rows(MEM_WIDTH), _whole(kv.shape), _rows(MEM_WIDTH), _whole(head_ones.shape)],
        out_specs=[_rows(nb)] + vspecs * 2 + [_rows(MEM_WIDTH), _whole(kv.shape)],
        scratch_shapes=[_view_scratch(gw), _view_scratch(gw)],
        compiler_params=_params(1, 48),
    )(dhb, wg_out, z, *os_, *ls_, qm, kv, mo, head_ones)


def _attn_bwd(q, k, v, lse, do, dl, tabs, d, token):
    ln, dw = q.shape
    w = dw // d
    nb = ln // BLOCK
    nblk = d * nb
    reps = w // LANES
    cur, prev = _stream_maps(d, nb)
    qmap = lambda n: cur(jnp.minimum(n, nblk - 1))
    pmap = lambda n: prev(jnp.minimum(n, nblk - 1))
    omap = lambda n: cur(jnp.maximum(n - 1, 0))

    def body(_, q_ref, kp_ref, kc_ref, vp_ref, vc_ref, l_ref, do_ref, dl_ref, cq, saq, sbq, ck, sak, sbk,
             dq_ref, dk_ref, dv_ref, acck, accv, dqs):
        n = pl.program_id(0)

        @pl.when(n == 0)
        def _():
            acck[...] = jnp.zeros_like(acck)
            accv[...] = jnp.zeros_like(accv)

        def run(with_prev):
            if with_prev:
                kk = jnp.concatenate([kp_ref[...], kc_ref[...]], axis=0)
                vv = jnp.concatenate([vp_ref[...], vc_ref[...]], axis=0)
                rows = slice(0, 2 * BLOCK)
            else:
                kk, vv = kc_ref[...], vc_ref[...]
                rows = slice(BLOCK, 2 * BLOCK)
            valid = _band_mask(kk.shape[0])
            for h0 in range(0, HEADS_PER_GROUP, HEAD_BATCH):
                hs = range(h0, h0 + HEAD_BATCH)
                sls = [slice(h * HEAD_DIM, (h + 1) * HEAD_DIM) for h in hs]
                cols = [slice(h * HEAD_DIM, h * HEAD_DIM + 1) for h in hs]
                qhs = [q_ref[:, sl] for sl in sls]
                dobs = [do_ref[:, sl] for sl in sls]
                scs = [jnp.where(valid, _dot_nt(qh, kk[:, sl]), NEG) for qh, sl in zip(qhs, sls)]
                dps = [_dot_nt(dob, vv[:, sl]) for dob, sl in zip(dobs, sls)]
                ps = [jnp.exp(sc - l_ref[:, col]) for sc, col in zip(scs, cols)]
                dss = [(p * (dp - dl_ref[:, col])).astype(BF16) for p, dp, col in zip(ps, dps, cols)]
                pbs = [p.astype(BF16) for p in ps]
                for sl, qh, dob, ds, pb in zip(sls, qhs, dobs, dss, pbs):
                    dqs[:, sl] = _dot(ds, kk[:, sl]) * SCALE
                    acck[rows, sl] += _dot_tn(ds, qh)
                    accv[rows, sl] += _dot_tn(pb, dob)
            tq = [jnp.tile(r[...], (1, reps)) for r in (cq, saq, sbq)]
            dq_ref[...] = _rope_bwd(dqs[...], *tq).astype(BF16)

        pl.when(n < nblk)(lambda: _per_stream_block(n, nb, run))

        tk = [jnp.tile(r[...], (1, reps)) for r in (ck, sak, sbk)]
        dk_ref[...] = _rope_bwd(acck[0:BLOCK, :], *tk).astype(BF16)
        dv_ref[...] = accv[0:BLOCK, :].astype(BF16)
        acck[0:BLOCK, :] = acck[BLOCK:, :]
        accv[0:BLOCK, :] = accv[BLOCK:, :]
        acck[BLOCK:, :] = jnp.zeros((BLOCK, w), F32)
        accv[BLOCK:, :] = jnp.zeros((BLOCK, w), F32)

    blk = lambda f: pl.BlockSpec((BLOCK, w), f)
    tblk = lambda f: pl.BlockSpec((BLOCK, LANES), f)
    return _pallas_call(
        body, name=f"attn_bwd_d{d}", grid=(nblk + 1,),
        out_shape=[_sds((ln, dw), BF16)] * 3,
        in_specs=[ANY, blk(qmap), blk(pmap), blk(qmap), blk(pmap), blk(qmap), blk(qmap), blk(qmap), blk(qmap)]
                 + [tblk(qmap)] * 3 + [tblk(omap)] * 3,
        out_specs=[blk(qmap), blk(omap), blk(omap)],
        scratch_shapes=[pltpu.VMEM((2 * BLOCK, w), F32), pltpu.VMEM((2 * BLOCK, w), F32), pltpu.VMEM((BLOCK, w), F32)],
        compiler_params=_params(1, 32),
    )(token, q, k, k, v, v, lse, do, dl, *tabs, *tabs)


def _position():
    return lax.axis_index("x"), lax.axis_index("y"), lax.axis_index("c")


def _all_gather(shards, name):
    n_a = len(shards)

    def body(*refs):
        x_refs, out_refs = refs[:n_a], refs[n_a:2 * n_a]
        send_sems, recv_sems, local_sems = refs[2 * n_a:]
        x, y, c = _position()
        me, sibling = (x, y, c), (x, y, 1 - c)
        chips = [(1 - x, y), (x, 1 - y), (1 - x, 1 - y)]

        def rows(a, px, py, pc):
            return out_refs[a].at[4 * px + 2 * py + pc]

        def copy(a, k, block, to, own=False):
            return pltpu.make_async_remote_copy(
                src_ref=x_refs[a] if own else rows(a, *block), dst_ref=rows(a, *block),
                send_sem=send_sems.at[a, k], recv_sem=recv_sems.at[a, k], device_id=to, device_id_type=MESH)

        mine = [pltpu.make_async_copy(x_refs[a], rows(a, *me), local_sems.at[a]) for a in range(n_a)]
        for cp in mine:
            cp.start()
        first = []
        for j, chip in enumerate(chips):
            first += [copy(a, 1 + j, me, (*chip, c), own=True) for a in range(n_a)]
        first += [copy(a, 0, me, sibling, own=True) for a in range(n_a)]
        for cp in first:
            cp.start()
        passed = []
        for j, chip in enumerate(chips):
            for a in range(n_a):
                copy(a, 1 + j, (*chip, c), me).wait_recv()
                fwd = copy(a, 4 + j, (*chip, c), sibling)
                fwd.start()
                passed.append(fwd)
        for a in range(n_a):
            copy(a, 0, sibling, me).wait_recv()
        for j, chip in enumerate(chips):
            for a in range(n_a):
                copy(a, 4 + j, (*chip, 1 - c), me).wait_recv()
        for cp in first + passed:
            cp.wait_send()
        for cp in mine:
            cp.wait()

    return _pallas_call(
        body, name=name, out_shape=[_sds((N_DEV,) + t.shape, t.dtype) for t in shards],
        in_specs=[ANY] * n_a, out_specs=[ANY] * n_a,
        scratch_shapes=[pltpu.SemaphoreType.DMA((n_a, 7)), pltpu.SemaphoreType.DMA((n_a, 7)),
                        pltpu.SemaphoreType.DMA((n_a,))],
    )(*shards)


def _rs_to_sibling(gs):
    n_a = len(gs)

    def body(*refs):
        g_refs, recv_refs = refs[:n_a], refs[n_a:2 * n_a]
        send_sems, recv_sems = refs[2 * n_a:]
        x, y, c = _position()
        copies = []
        for k in range(4):
            for a in range(n_a):
                copies.append(pltpu.make_async_remote_copy(
                    src_ref=g_refs[a].at[2 * k + (1 - c)], dst_ref=recv_refs[a].at[k],
                    send_sem=send_sems.at[a, k], recv_sem=recv_sems.at[a, k],
                    device_id=(x, y, 1 - c), device_id_type=MESH))
        for cp in copies:
            cp.start()
        for cp in copies:
            cp.wait()

    return _pallas_call(
        body, name="rs_to_sibling", out_shape=[_sds((4,) + g.shape[1:], g.dtype) for g in gs],
        in_specs=[ANY] * n_a, out_specs=[ANY] * n_a,
        scratch_shapes=[pltpu.SemaphoreType.DMA((n_a, 4)), pltpu.SemaphoreType.DMA((n_a, 4))],
    )(*gs)


def _rs_to_chips(pbs):
    n_a = len(pbs)

    def body(*refs):
        p_refs, recv_refs = refs[:n_a], refs[n_a:2 * n_a]
        send_sems, recv_sems = refs[2 * n_a:]
        x, y, c = _position()
        chips = [(1 - x, y), (x, 1 - y), (1 - x, 1 - y)]
        copies = []
        for j, (px, py) in enumerate(chips):
            for a in range(n_a):
                copies.append(pltpu.make_async_remote_copy(
                    src_ref=p_refs[a].at[2 * px + py], dst_ref=recv_refs[a].at[j],
                    send_sem=send_sems.at[a, j], recv_sem=recv_sems.at[a, j],
                    device_id=(px, py, c), device_id_type=MESH))
        for cp in copies:
            cp.start()
        for cp in copies:
            cp.wait()

    return _pallas_call(
        body, name="rs_to_chips", out_shape=[_sds((3,) + p.shape[1:], p.dtype) for p in pbs],
        in_specs=[ANY] * n_a, out_specs=[ANY] * n_a,
        scratch_shapes=[pltpu.SemaphoreType.DMA((n_a, 3)), pltpu.SemaphoreType.DMA((n_a, 3))],
    )(*pbs)


HBM_SPEC = pl.BlockSpec(memory_space=pltpu.HBM)
SEM_SPEC = pl.BlockSpec(memory_space=pltpu.SEMAPHORE)
EFFECT = pltpu.SideEffectType.DATAFLOW_SIDE_EFFECTING
def _plan_gather_own(src_refs, land_refs):
    x, y, c = _position()
    me = 4 * x + 2 * y + c
    peers = [(x, y, 1 - c), (1 - x, y, c), (x, 1 - y, c), (1 - x, 1 - y, c)]
    return [(src_refs[a], land_refs[a].at[me], (a, k), peer) for k, peer in enumerate(peers) for a in range(len(src_refs))]


def _plan_gather_pass(src_refs, land_refs):
    x, y, c = _position()
    chips = [(1 - x, y), (x, 1 - y), (1 - x, 1 - y)]
    return [(land_refs[a].at[4 * px + 2 * py + c], land_refs[a].at[4 * px + 2 * py + c], (a, j), (x, y, 1 - c))
            for j, (px, py) in enumerate(chips) for a in range(len(land_refs))]


def _plan_to_sibling(src_refs, land_refs):
    x, y, c = _position()
    return [(src_refs[a].at[2 * k + (1 - c)], land_refs[a].at[k], (a, k), (x, y, 1 - c))
            for k in range(4) for a in range(len(src_refs))]


def _plan_to_chips(src_refs, land_refs):
    x, y, c = _position()
    chips = [(1 - x, y), (x, 1 - y), (1 - x, 1 - y)]
    return [(src_refs[a].at[2 * px + py], land_refs[a].at[j], (a, j), (px, py, c))
            for j, (px, py) in enumerate(chips) for a in range(len(src_refs))]


def _split_start(srcs, lands, plan, n_sem, after, name):
    n_s, n_a = len(srcs), len(lands)
    n_b = n_s + n_a

    def body(*refs):
        src_refs, land_refs = refs[:n_s], refs[n_s:n_b]
        send_sems, recv_sems, token = refs[n_b + 1], refs[n_b + 2], refs[-1]
        for src, dst, (a, k), dev in plan(src_refs, land_refs):
            i = a * n_sem + k
            pltpu.make_async_remote_copy(src_ref=src, dst_ref=dst, send_sem=send_sems.at[i], recv_sem=recv_sems.at[i],
                                         device_id=dev, device_id_type=MESH).start()
        token[...] = jnp.zeros_like(token)

    bufs = list(srcs) + list(lands)
    res = pl.pallas_call(
        body, name=name,
        out_shape=(pltpu.SemaphoreType.DMA((n_a * n_sem,)), pltpu.SemaphoreType.DMA((n_a * n_sem,)),
                   *[pltpu.HBM(t.shape, t.dtype) for t in bufs], _plain((8, LANES), F32)),
        in_specs=[HBM_SPEC] * n_b + [ANY],
        out_specs=(SEM_SPEC, SEM_SPEC, *[HBM_SPEC] * n_b, pl.BlockSpec(memory_space=pltpu.VMEM)),
        input_output_aliases={i: 2 + i for i in range(n_b)},
        compiler_params=pltpu.CompilerParams(has_side_effects=EFFECT),
    )(*[pltpu.with_memory_space_constraint(t, pltpu.HBM) for t in bufs], after)
    return (res[0], res[1], res[2:2 + n_s], res[2 + n_s:2 + n_b]), res[-1]


def _split_wait(started, plan, after, name):
    send_sems, recv_sems, srcs, lands = started
    n_s, n_a = len(srcs), len(lands)
    n_b = n_s + n_a
    n_sem = send_sems.shape[0] // n_a

    def body(*refs):
        src_refs, land_refs = refs[:n_s], refs[n_s:n_b]
        s_sems, r_sems = refs[n_b], refs[n_b + 1]
        for src, dst, (a, k), dev in plan(src_refs, land_refs):
            i = a * n_sem + k
            cp = pltpu.make_async_remote_copy(src_ref=src, dst_ref=dst, send_sem=s_sems.at[i], recv_sem=r_sems.at[i],
                                              device_id=dev, device_id_type=MESH)
            cp.wait_send()
            cp.wait_recv()

    bufs = list(srcs) + list(lands)
    res = pl.pallas_call(
        body, name=name, out_shape=tuple(pltpu.HBM(t.shape, t.dtype) for t in bufs),
        in_specs=[HBM_SPEC] * n_b + [SEM_SPEC, SEM_SPEC, ANY],
        out_specs=tuple([HBM_SPEC] * n_b),
        input_output_aliases={i: i for i in range(n_b)},
        compiler_params=pltpu.CompilerParams(has_side_effects=EFFECT),
    )(*bufs, send_sems, recv_sems, after)
    return res[:n_s], res[n_s:]


def _row_tile(r):
    return ROW_TILE if r % ROW_TILE == 0 else r


def _rs_add_sibling(gp, recv, c_arr, name):
    _, r, l = gp.shape
    tr = _row_tile(r)

    def body(c_ref, g_ref, r_ref, pf_ref, pb_ref):
        sm = g_ref[...].astype(F32) + r_ref[...].astype(F32)
        pf_ref[...] = sm
        pb_ref[...] = sm.astype(BF16)

    spec = pl.BlockSpec((None, tr, l), lambda k, i, c: (k, i, 0))
    return _pallas_call(
        body, name=name,
        grid_spec=pltpu.PrefetchScalarGridSpec(
            num_scalar_prefetch=1, grid=(4, r // tr),
            in_specs=[pl.BlockSpec((None, tr, l), lambda k, i, c: (2 * k + c[0], i, 0)), spec],
            out_specs=[spec, spec]),
        out_shape=[_sds((4, r, l), F32), _sds((4, r, l), BF16)], compiler_params=_params(2, 32),
    )(c_arr, gp, recv)


def _adam_update(w, gv, m, v):
    nm = ADAM_B1 * m + (1.0 - ADAM_B1) * gv
    nv = ADAM_B2 * v + (1.0 - ADAM_B2) * (gv * gv)
    m_hat = nm / (1.0 - ADAM_B1 ** ADAM_STEP)
    v_hat = nv / (1.0 - ADAM_B2 ** ADAM_STEP)
    return -ADAM_LR * (m_hat / (jnp.sqrt(v_hat) + ADAM_EPS) + ADAM_WD * w), nm, nv


def _rs_finish_adamw(pf, recv, k_arr, w, m, v, name):
    _, r, l = pf.shape
    tr = _row_tile(r)

    def body(k_ref, p_ref, r_ref, w_ref, m_ref, v_ref, g_ref, d_ref, nm_ref, nv_ref):
        gv = ((p_ref[...] + r_ref[0].astype(F32)) + r_ref[1].astype(F32)) + r_ref[2].astype(F32)
        g_ref[...] = gv
        d_ref[...], nm_ref[...], nv_ref[...] = _adam_update(w_ref[...], gv, m_ref[...], v_ref[...])

    spec = pl.BlockSpec((tr, l), lambda i, k: (i, 0))
    return _pallas_call(
        body, name=name,
        grid_spec=pltpu.PrefetchScalarGridSpec(
            num_scalar_prefetch=1, grid=(r // tr,),
            in_specs=[pl.BlockSpec((None, tr, l), lambda i, k: (k[0], i, 0)),
                      pl.BlockSpec((3, tr, l), lambda i, k: (0, i, 0)), spec, spec, spec],
            out_specs=[spec] * 4),
        out_shape=[_plain((r, l), F32)] * 4, compiler_params=_params(1, 32),
    )(k_arr, pf, recv, w, m, v)


def _sum_devices(g):
    def body(g_ref, o_ref):
        acc = g_ref[0]
        for j in range(1, N_DEV):
            acc = acc + g_ref[j]
        o_ref[...] = acc

    return _pallas_call(body, name="sum_devices", out_shape=_plain(g.shape[1:], F32))(g)


def _adamw(w, g, m, v, name):
    shape = w.shape
    w2, g2, m2, v2 = [t.reshape((-1, shape[-1])) for t in (w, g, m, v)]

    def body(w_ref, g_ref, m_ref, v_ref, d_ref, nm_ref, nv_ref):
        d_ref[...], nm_ref[...], nv_ref[...] = _adam_update(w_ref[...], g_ref[...], m_ref[...], v_ref[...])

    outs = _pallas_call(body, name=name, out_shape=[_plain(w2.shape, F32)] * 3)(w2, g2, m2, v2)
    return tuple(t.reshape(shape) for t in outs)


def _after(t, token):
    return t + token[0:1, 0:1].astype(t.dtype)


def _local_step(x, mem, pos, tgt, norm_g, mem_norm_g, final_g, wg_in0, late_weights, late_token, c_arr):
    tabs = _rope_tables(pos)
    g0, g1 = _after(norm_g[0:1], late_token), norm_g[1:2]

    hn0, hn0_t, qs, ks, vs, tabs_v, qm0, z0 = _inproj_attn(x, g0, wg_in0, tabs)
    os_, ls_ = [], []
    for j, d in enumerate(DILATIONS):
        if j == 1:
            _, lands = _split_wait(late_weights, _plan_gather_own, ls_[0], "gather_late_wait")
            late_weights, late_token = _split_start([], lands, _plan_gather_pass, 3, ls_[0], "gather_late_pass_start")
        o, l = _attn_fwd(qs[j], ks[j], vs[j], d, late_token)
        os_.append(o)
        ls_.append(l)

    _, gathered = _split_wait(late_weights, _plan_gather_pass, ls_[2], "gather_late_pass_wait")
    wg_out0, wg_in1, wg_out1, wg_kv0, wg_kv1, cw_all = gathered
    w_out1 = wg_out1.reshape(-1, wg_out1.shape[2])
    w_kv = jnp.stack([wg_kv0.reshape(-1, wg_kv0.shape[2]), wg_kv1.reshape(-1, wg_kv1.shape[2])])
    cw = cw_all[:, 0:3].transpose(1, 0, 2).reshape(3, -1)
    kv = _memkv_fwd(mem, mem_norm_g, w_kv)
    y0, y0_t, mo0, h1 = _post_attn(os_, ls_, qm0, kv[0], z0, x, wg_out0)

    hn1, hn1_t, bg, cg, u, qm1, z1 = _inproj_conv(h1, g1, wg_in1)
    y1, y1_t, mo1, dh2, dh2b, loss_acc, d_final_g = _post_conv_loss(
        bg, cg, u, qm1, kv[1], z1, h1, w_out1, cw, final_g.reshape(1, -1), tgt)

    d_w_out1 = _wgrad_rows(y1_t, dh2b, "wgrad_out1")
    dz1, dbg, dconv, dqm1, dkv1 = _bwd_post_conv(dh2b, w_out1, bg, cg, u, z1, qm1, kv[1], mo1, cw)
    dcg, du, dcw = _bwd_conv(dconv, cg, u, cw)
    dh1, dh1b, dg1, dproj1 = _dgrad_norm([(dbg, 1), (dcg, 1), (du, 1), (dqm1, 1), (dz1, 1)], wg_in1, h1, g1, dh2,
                                         "dgrad_norm_conv")
    d_w_in1 = _wgrad_shards(hn1_t, dproj1, "wgrad_in1")

    grads1 = [d_w_in1, d_w_out1]
    started, token = _split_start(grads1, [lax.empty((4,) + g.shape[1:], g.dtype) for g in grads1],
                                  _plan_to_sibling, 4, dg1, "rs1_sibling_start")

    d_w_out0 = _wgrad_cols(y0_t, dh1b, wg_out0.shape[2], "wgrad_out0")
    gw = GROUP_WIDTH
    ones = (jnp.arange(gw)[:, None] // HEAD_DIM == jnp.arange(gw)[None, :] // HEAD_DIM).astype(BF16)
    ones = _after(ones, token)
    res = _bwd_post_attn(dh1b, wg_out0, z0, os_, ls_, qm0, kv[0], mo0, ones)
    dz0, dos, dls, dqm0, dkv0 = res[0], res[1:4], res[4:7], res[7], res[8]

    grads1, from_sibling = _split_wait(started, _plan_to_sibling, dz0, "rs1_sibling_wait")
    parts1 = [_rs_add_sibling(g, r, c_arr, "rs_add_sibling_" + n)
              for g, r, n in zip(grads1, from_sibling, ("conv_w_in", "conv_w_out"))]
    pbs1 = [pb for _, pb in parts1]
    started, token = _split_start(pbs1, [lax.empty((3,) + p.shape[1:], p.dtype) for p in pbs1],
                                  _plan_to_chips, 3, dg1, "rs1_chips_start")

    dqs, dks, dvs = [], [], []
    for j, d in enumerate(DILATIONS):
        dq, dk, dv = _attn_bwd(qs[j], ks[j], vs[j], ls_[j], dos[j], dls[j], tabs_v[j], d, token)
        dqs.append((dq, d))
        dks.append((dk, d))
        dvs.append((dv, d))
    dx, _, dg0, dproj0 = _dgrad_norm(dqs + dks + dvs + [(dqm0, 1), (dz0, 1)], wg_in0, x, g0, dh1, "dgrad_norm_attn")
    d_w_in0 = _wgrad_shards(hn0_t, dproj0, "wgrad_in0")
    _, from_chips1 = _split_wait(started, _plan_to_chips, d_w_in0, "rs1_chips_wait")

    d_w_kv, d_mem_g = _memkv_bwd(jnp.stack([dkv0, dkv1]), w_kv, mem, mem_norm_g)
    small = jnp.concatenate([dg0, dg1, d_mem_g.reshape(2, -1), d_final_g, dcw[0:3]], axis=0)
    n_kv = d_w_kv.shape[1] // N_DEV
    grads0 = [d_w_in0, d_w_out0, d_w_kv[0].reshape(N_DEV, n_kv, -1), d_w_kv[1].reshape(N_DEV, n_kv, -1)]
    return loss_acc[0, 0], dx, grads0, small, [pf for pf, _ in parts1], from_chips1


def kernel(x, mem, positions, norm_g, mem_norm_g, w_mem_kv, attn_w_in, attn_w_out, conv_w_in, conv_w, conv_w_out, final_g, loss_target, m_norm_g, m_mem_norm_g, m_w_mem_kv, m_attn_w_in, m_attn_w_out, m_conv_w_in, m_conv_w, m_conv_w_out, m_final_g, v_norm_g, v_mem_norm_g, v_w_mem_kv, v_attn_w_in, v_attn_w_out, v_conv_w_in, v_conv_w, v_conv_w_out, v_final_g):
    px, py, pc = _position()
    me = 4 * px + 2 * py + pc

    c_arr = jnp.reshape(pc, (1,)).astype(jnp.int32)
    k_arr = jnp.reshape(2 * px + py, (1,)).astype(jnp.int32)
    wg_in0 = _all_gather([attn_w_in[0].astype(BF16)], "gather_w_in0")[0]
    late = [attn_w_out[0].astype(BF16), conv_w_in[0].astype(BF16), conv_w_out[0].astype(BF16),
            w_mem_kv[0].astype(BF16), w_mem_kv[1].astype(BF16), jnp.pad(conv_w[0], ((0, 5), (0, 0)))]
    lands = [lax.dynamic_update_slice(lax.empty((N_DEV,) + t.shape, t.dtype), t[None], (me, 0, 0)) for t in late]
    late_weights, late_token = _split_start(late, lands, _plan_gather_own, 4, wg_in0, "gather_late_start")

    loss_part, dx, grads0, small_part, pfs1, from_chips1 = _local_step(
        x[0], mem[0], positions[0], loss_target[0], norm_g, mem_norm_g, final_g, wg_in0, late_weights, late_token, c_arr)

    names0 = ["attn_w_in", "attn_w_out", "w_mem_kv0", "w_mem_kv1"]
    from_sibling = _rs_to_sibling(grads0)
    parts0 = [_rs_add_sibling(g, r, c_arr, "rs_add_sibling_" + n) for g, r, n in zip(grads0, from_sibling, names0)]
    from_chips0 = _rs_to_chips([pb for _, pb in parts0])
    names = names0 + ["conv_w_in", "conv_w_out"]
    pfs = [pf for pf, _ in parts0] + list(pfs1)
    from_chips = list(from_chips0) + list(from_chips1)
    shards = [attn_w_in[0], attn_w_out[0], w_mem_kv[0], w_mem_kv[1], conv_w_in[0], conv_w_out[0]]
    moments = [(m_attn_w_in[0], v_attn_w_in[0]), (m_attn_w_out[0], v_attn_w_out[0]), (m_w_mem_kv[0], v_w_mem_kv[0]),
               (m_w_mem_kv[1], v_w_mem_kv[1]), (m_conv_w_in[0], v_conv_w_in[0]), (m_conv_w_out[0], v_conv_w_out[0])]
    big = {}
    for n, w, pf, r, (m, v) in zip(names, shards, pfs, from_chips, moments):
        big[n] = _rs_finish_adamw(pf, r, k_arr, w, m, v, "rs_finish_adamw_" + n)
    for n in ("attn_w_in", "attn_w_out", "conv_w_in", "conv_w_out"):
        big[n] = tuple(t[None] for t in big[n])
    big["w_mem_kv"] = tuple(jnp.stack([a, b]) for a, b in zip(big["w_mem_kv0"], big["w_mem_kv1"]))

    small_part = jnp.concatenate([small_part, jnp.broadcast_to(loss_part, small_part.shape)], axis=0)
    small = _sum_devices(_all_gather([small_part], "gather_small_grads")[0])
    loss = small[8, 0]
    g_conv_w = lax.dynamic_slice(small[5:8], (0, me * LANES), (3, LANES))[None]
    small_g = dict(norm_g=small[0:2], mem_norm_g=small[2:4], conv_w=g_conv_w, final_g=small[4])
    small_w = dict(norm_g=(norm_g, m_norm_g, v_norm_g), mem_norm_g=(mem_norm_g, m_mem_norm_g, v_mem_norm_g),
                   conv_w=(conv_w, m_conv_w, v_conv_w), final_g=(final_g, m_final_g, v_final_g))
    for n, (w, m, v) in small_w.items():
        big[n] = (small_g[n],) + _adamw(w, small_g[n], m, v, "adamw_" + n)

    order = ["norm_g", "mem_norm_g", "w_mem_kv", "attn_w_in", "attn_w_out", "conv_w_in", "conv_w", "conv_w_out", "final_g"]
    return (loss, dx[None], *[big[n][0] for n in order], *[big[n][1] for n in order],
            *[big[n][2] for n in order], *[big[n][3] for n in order])
```

```python
import functools

import jax
import jax.numpy as jnp
from jax import lax
from jax.experimental import pallas as pl
from jax.experimental.pallas import tpu as pltpu

F32 = jnp.float32
BF16 = jnp.bfloat16

N_DEV = 8
D_MODEL = 1024
HEAD_DIM = 64
ROT_DIM = HEAD_DIM // 4
ROPE_THETA = 500000.0
DILATIONS = (1, 4, 16)
HEADS_PER_GROUP = 8
HEAD_BATCH = 8
GROUP_WIDTH = HEADS_PER_GROUP * HEAD_DIM
BLOCK = 128
N_MEM = 256
MEM_HEADS = 4
MEM_WIDTH = MEM_HEADS * HEAD_DIM
CONV_WIDTH = D_MODEL
EPS = 1e-6
SCALE = HEAD_DIM ** -0.5
NEG = -1e30

ADAM_LR = 0.001
ADAM_B1 = 0.9
ADAM_B2 = 0.999
ADAM_EPS = 1e-08
ADAM_WD = 0.01
ADAM_STEP = 10

ROW_TILE = 256
LANES = 128
MESH = pl.DeviceIdType.MESH
ANY = pl.BlockSpec(memory_space=pl.ANY)


def _pallas_call(body, **kw):
    call = pl.pallas_call(body, **kw)

    def run(*args):
        pinned = [pltpu.with_memory_space_constraint(a, pltpu.HBM) if jnp.issubdtype(a.dtype, jnp.floating) else a
                  for a in args]
        return call(*pinned)

    return run


def _dot(a, b):
    return lax.dot_general(a, b, (((1,), (0,)), ((), ())), preferred_element_type=F32)


def _dot_nt(a, b):
    return lax.dot_general(a, b, (((1,), (1,)), ((), ())), preferred_element_type=F32)


def _dot_tn(a, b):
    return lax.dot_general(a, b, (((0,), (0,)), ((), ())), preferred_element_type=F32)


def _params(n_grid, vmem_mb=48):
    return pltpu.CompilerParams(dimension_semantics=("arbitrary",) * n_grid, vmem_limit_bytes=vmem_mb << 20)


def _rows(width, tm=ROW_TILE):
    return pl.BlockSpec((tm, width), lambda i: (i, 0))


def _view_rows(width, d, tm=ROW_TILE):
    return pl.BlockSpec((tm // d, d * width), lambda i: (i, 0))


def _whole(shape):
    return pl.BlockSpec(shape, lambda *_: (0,) * len(shape))


def _sds(shape, dtype):
    return pltpu.HBM(shape, dtype)


def _plain(shape, dtype):
    return jax.ShapeDtypeStruct(shape, dtype)


def _silu_parts(z):
    sg = jax.nn.sigmoid(z)
    return z * sg, sg * (1.0 + z * (1.0 - sg))


def _to_view(scr, val, out_ref, d):
    tm, w = val.shape
    if d == 1:
        out_ref[...] = val.astype(out_ref.dtype)
        return
    for cb in range(w // LANES):
        scr[cb] = val[:, cb * LANES:(cb + 1) * LANES]
    for r in range(d):
        for cb in range(w // LANES):
            lo = r * w + cb * LANES
            out_ref[:, lo:lo + LANES] = scr[cb, pl.ds(r, tm // d, stride=d), :].astype(out_ref.dtype)


def _from_view(scr, in_ref, d):
    if d == 1:
        return in_ref[...].astype(F32)
    nc, tm, _ = scr.shape
    w = nc * LANES
    for r in range(d):
        for cb in range(nc):
            lo = r * w + cb * LANES
            scr[cb, pl.ds(r, tm // d, stride=d), :] = in_ref[:, lo:lo + LANES].astype(F32)
    return jnp.concatenate([scr[cb] for cb in range(nc)], axis=1)


def _view_scratch(width, tm=ROW_TILE):
    return pltpu.VMEM((width // LANES, tm, LANES), F32)


def _rope_tables(pos):
    half = ROT_DIM // 2
    inv_freq = ROPE_THETA ** (-jnp.arange(half, dtype=F32) * (2.0 / ROT_DIM))
    ang = pos.astype(F32)[:, None] * inv_freq
    cos, sin = jnp.cos(ang), jnp.sin(ang)
    s = pos.shape[0]
    z8 = jnp.zeros((s, half), F32)
    rest = HEAD_DIM - ROT_DIM
    cosf = jnp.concatenate([cos, cos, jnp.ones((s, rest), F32)], axis=1)
    sa = jnp.concatenate([-sin, z8, jnp.zeros((s, rest), F32)], axis=1)
    sb = jnp.concatenate([z8, sin, jnp.zeros((s, rest), F32)], axis=1)
    return tuple(jnp.tile(t, (1, LANES // HEAD_DIM)) for t in (cosf, sa, sb))


def _rope_fwd(t, cv, sav, sbv):
    w = t.shape[1]
    return t * cv + pltpu.roll(t, w - ROT_DIM // 2, 1) * sav + pltpu.roll(t, ROT_DIM // 2, 1) * sbv


def _rope_bwd(g, cv, sav, sbv):
    w = g.shape[1]
    return g * cv + pltpu.roll(g * sav, ROT_DIM // 2, 1) + pltpu.roll(g * sbv, w - ROT_DIM // 2, 1)


def _project(hn, wg_ref, proj_scr):
    c = wg_ref.shape[2]
    for j in range(N_DEV):
        proj_scr[:, j * c:(j + 1) * c] = _dot(hn, wg_ref[j])


def _inproj_attn(x, g, wg, tabs):
    s, d_model = x.shape
    gw = GROUP_WIDTH
    n = N_DEV * wg.shape[2]
    nz = n - 9 * gw - MEM_WIDTH
    reps = gw // LANES
    tm = ROW_TILE

    def body(x_ref, g_ref, w_ref, c_ref, sa_ref, sb_ref, hn_ref, hnt_ref, *rest):
        outs, (proj, scr, tscr) = rest[:-3], rest[-3:]
        q_refs, k_refs, v_refs, t_refs, qm_ref, z_ref = outs[0:3], outs[3:6], outs[6:9], outs[9:18], outs[18], outs[19]
        xb = x_ref[...]
        r = lax.rsqrt(jnp.mean(xb * xb, axis=-1, keepdims=True) + EPS)
        hn = ((xb * r) * g_ref[...]).astype(BF16)
        hn_ref[...] = hn
        hnt_ref[...] = hn.T
        _project(hn, w_ref, proj)
        tab = (c_ref[...], sa_ref[...], sb_ref[...])
        cv, sav, sbv = [jnp.tile(t, (1, reps)) for t in tab]
        for j, d in enumerate(DILATIONS):
            tq = _rope_fwd(proj[:, j * gw:(j + 1) * gw], cv, sav, sbv)
            _to_view(scr, tq * SCALE, q_refs[j], d)
            tk = _rope_fwd(proj[:, (3 + j) * gw:(4 + j) * gw], cv, sav, sbv)
            _to_view(scr, tk, k_refs[j], d)
            _to_view(scr, proj[:, (6 + j) * gw:(7 + j) * gw], v_refs[j], d)
            for i in range(3):
                _to_view(tscr, tab[i], t_refs[3 * j + i], d)
        qm_ref[...] = proj[:, 9 * gw:9 * gw + MEM_WIDTH].astype(BF16)
        z_ref[...] = proj[:, 9 * gw + MEM_WIDTH:]

    views = [_sds((s // d, d * gw), BF16) for d in DILATIONS]
    tviews = [_sds((s // d, d * LANES), F32) for d in DILATIONS for _ in range(3)]
    out_shape = ([_sds((s, d_model), BF16), _sds((d_model, s), BF16)] + views * 3 + tviews
                 + [_sds((s, MEM_WIDTH), BF16), _sds((s, nz), F32)])
    vspecs = [_view_rows(gw, d) for d in DILATIONS]
    tspecs = [_view_rows(LANES, d) for d in DILATIONS for _ in range(3)]
    out_specs = ([_rows(d_model), pl.BlockSpec((d_model, tm), lambda i: (0, i))] + vspecs * 3 + tspecs
                 + [_rows(MEM_WIDTH), _rows(nz)])
    res = _pallas_call(
        body, name="inproj_attn", grid=(s // tm,), out_shape=out_shape,
        in_specs=[_rows(d_model), _whole((1, d_model)), _whole(wg.shape), _rows(LANES), _rows(LANES), _rows(LANES)],
        out_specs=out_specs,
        scratch_shapes=[pltpu.VMEM((tm, n), F32), _view_scratch(gw), _view_scratch(LANES)],
        compiler_params=_params(1, 60),
    )(x, g, wg, *tabs)
    tabs_v = [res[11 + 3 * j:14 + 3 * j] for j in range(3)]
    return res[0], res[1], res[2:5], res[5:8], res[8:11], tabs_v, res[20], res[21]


def _stream_maps(d, nb):
    def cur(n):
        return (n % nb, n // nb)

    def prev(n):
        return (jnp.maximum(n % nb - 1, 0), n // nb)

    return cur, prev


def _band_mask(n_keys):
    qi = lax.broadcasted_iota(jnp.int32, (BLOCK, n_keys), 0)
    kj = lax.broadcasted_iota(jnp.int32, (BLOCK, n_keys), 1)
    if n_keys == BLOCK:
        return kj <= qi
    return jnp.logical_or(jnp.logical_and(kj < BLOCK, kj >= qi), jnp.logical_and(kj >= BLOCK, (kj - BLOCK) <= qi))


def _per_stream_block(n, nb, run):
    if nb == 1:
        run(False)
        return
    first = (n % nb) == 0
    pl.when(first)(lambda: run(False))
    pl.when(jnp.logical_not(first))(lambda: run(True))


def _attn_fwd(q, k, v, d, token):
    ln, dw = q.shape
    w = dw // d
    nb = ln // BLOCK
    nblk = d * nb
    cur, prev = _stream_maps(d, nb)

    def body(_, q_ref, kp_ref, kc_ref, vp_ref, vc_ref, o_ref, lse_ref):
        def run(with_prev):
            if with_prev:
                kk = jnp.concatenate([kp_ref[...], kc_ref[...]], axis=0)
                vv = jnp.concatenate([vp_ref[...], vc_ref[...]], axis=0)
            else:
                kk, vv = kc_ref[...], vc_ref[...]
            valid = _band_mask(kk.shape[0])
            hb = HEAD_BATCH
            for h0 in range(0, HEADS_PER_GROUP, hb):
                sls = [slice(h * HEAD_DIM, (h + 1) * HEAD_DIM) for h in range(h0, h0 + hb)]
                scs = [jnp.where(valid, _dot_nt(q_ref[:, sl], kk[:, sl]), NEG) for sl in sls]
                ms = [jnp.max(sc, axis=-1, keepdims=True) for sc in scs]
                ps = [jnp.exp(sc - m) for sc, m in zip(scs, ms)]
                ls = [jnp.sum(p, axis=-1, keepdims=True) for p in ps]
                pns = [(p * (1.0 / l)).astype(BF16) for p, l in zip(ps, ls)]
                for sl, pn, m, l in zip(sls, pns, ms, ls):
                    o_ref[:, sl] = _dot(pn, vv[:, sl])
                    lse_ref[:, sl] = jnp.broadcast_to(m + jnp.log(l), (BLOCK, HEAD_DIM))

        _per_stream_block(pl.program_id(0), nb, run)

    blk = lambda f: pl.BlockSpec((BLOCK, w), f)
    return _pallas_call(
        body, name=f"attn_fwd_d{d}", grid=(nblk,),
        out_shape=[_sds((ln, dw), F32)] * 2,
        in_specs=[ANY, blk(cur), blk(prev), blk(cur), blk(prev), blk(cur)],
        out_specs=[blk(cur), blk(cur)], compiler_params=_params(1, 32),
    )(token, q, k, k, v, v)


def _memkv_fwd(mem, g, w):
    n_layers = w.shape[0]

    def body(mem_ref, g_ref, w_ref, kv_ref):
        mb = mem_ref[...]
        r = lax.rsqrt(jnp.mean(mb * mb, axis=-1, keepdims=True) + EPS)
        mn = ((mb * r) * g_ref[...]).astype(BF16)
        kv_ref[...] = _dot(mn, w_ref[...]).astype(BF16)

    return _pallas_call(
        body, name="memkv_fwd", grid=(n_layers,),
        out_shape=_plain((n_layers, N_MEM, 2 * MEM_WIDTH), BF16),
        in_specs=[_whole(mem.shape), pl.BlockSpec((None, 1, D_MODEL), lambda l: (l, 0, 0)),
                  pl.BlockSpec((None, D_MODEL, 2 * MEM_WIDTH), lambda l: (l, 0, 0))],
        out_specs=pl.BlockSpec((None, N_MEM, 2 * MEM_WIDTH), lambda l: (l, 0, 0)),
        compiler_params=_params(1, 32),
    )(mem, g.reshape(n_layers, 1, D_MODEL), w)


def _mix_groups(os_, ls_):
    mx = jnp.maximum(jnp.maximum(ls_[0], ls_[1]), ls_[2])
    es = [jnp.exp(t - mx) for t in ls_]
    inv = 1.0 / (es[0] + es[1] + es[2])
    ws = [e * inv for e in es]
    mix = ws[0] * os_[0] + ws[1] * os_[1] + ws[2] * os_[2]
    return ws, mix


MEM_SLICES = [slice(h * HEAD_DIM, (h + 1) * HEAD_DIM) for h in range(MEM_HEADS)]


def _mem_probs(qm, km):
    scs = [_dot_nt(qm[:, sl], km[:, sl]) * SCALE for sl in MEM_SLICES]
    es = [jnp.exp(sc - jnp.max(sc, axis=-1, keepdims=True)) for sc in scs]
    return [e * (1.0 / jnp.sum(e, axis=-1, keepdims=True)) for e in es]


def _mem_attn_into(qm, kv_ref, mo_ref):
    km, vm = kv_ref[:, :MEM_WIDTH], kv_ref[:, MEM_WIDTH:]
    ps = [p.astype(BF16) for p in _mem_probs(qm, km)]
    for sl, p in zip(MEM_SLICES, ps):
        mo_ref[:, sl] = _dot(p, vm[:, sl])


def _mem_attn_bwd(qm, kv_ref, dmem, dqm_ref, dkv_ref):
    km, vm = kv_ref[:, :MEM_WIDTH], kv_ref[:, MEM_WIDTH:]
    dmb = dmem.astype(BF16)
    ps = _mem_probs(qm, km)
    dps = [_dot_nt(dmb[:, sl], vm[:, sl]) for sl in MEM_SLICES]
    dss = [(p * (dp - jnp.sum(dp * p, axis=-1, keepdims=True)) * SCALE).astype(BF16) for p, dp in zip(ps, dps)]
    pbs = [p.astype(BF16) for p in ps]
    for h, (sl, ds, pb) in enumerate(zip(MEM_SLICES, dss, pbs)):
        slv = slice(MEM_WIDTH + h * HEAD_DIM, MEM_WIDTH + (h + 1) * HEAD_DIM)
        dqm_ref[:, sl] = _dot(ds, km[:, sl]).astype(BF16)
        dkv_ref[:, sl] += _dot_tn(ds, qm[:, sl])
        dkv_ref[:, slv] += _dot_tn(pb, dmb[:, sl])


def _post_attn(os_, ls_, qm, kv, z, x, wg_out):
    s, d_model = x.shape
    gw = GROUP_WIDTH
    nb = gw + MEM_WIDTH
    c = wg_out.shape[2]
    tm = ROW_TILE

    def body(o0, o1, o2, l0, l1, l2, qm_ref, kv_ref, z_ref, x_ref, w_ref, y_ref, yt_ref, mo_ref, h_ref, s0, s1):
        ov, lv = [], []
        for o_ref, l_ref, d in zip((o0, o1, o2), (l0, l1, l2), DILATIONS):
            ov.append(_from_view(s0, o_ref, d))
            lv.append(_from_view(s1, l_ref, d))
        _, mix = _mix_groups(ov, lv)
        _mem_attn_into(qm_ref[...], kv_ref, mo_ref)
        sz, _ = _silu_parts(z_ref[...])
        y_ref[:, :gw] = (mix * sz[:, :gw]).astype(BF16)
        y_ref[:, gw:] = (mo_ref[...] * sz[:, gw:]).astype(BF16)
        y = y_ref[...]
        yt_ref[...] = y.T
        for j in range(N_DEV):
            h_ref[:, j * c:(j + 1) * c] = x_ref[:, j * c:(j + 1) * c] + _dot(y, w_ref[j])

    vspecs = [_view_rows(gw, d) for d in DILATIONS]
    return _pallas_call(
        body, name="post_attn", grid=(s // tm,),
        out_shape=[_sds((s, nb), BF16), _sds((nb, s), BF16), _sds((s, MEM_WIDTH), F32), _sds((s, d_model), F32)],
        in_specs=vspecs * 2 + [_rows(MEM_WIDTH), _whole(kv.shape), _rows(nb), _rows(d_model), _whole(wg_out.shape)],
        out_specs=[_rows(nb), pl.BlockSpec((nb, tm), lambda i: (0, i)), _rows(MEM_WIDTH), _rows(d_model)],
        scratch_shapes=[_view_scratch(gw), _view_scratch(gw)],
        compiler_params=_params(1, 40),
    )(*os_, *ls_, qm, kv, z, x, wg_out)


def _inproj_conv(x, g, wg):
    s, d_model = x.shape
    c = CONV_WIDTH
    n = N_DEV * wg.shape[2]
    nz = n - 3 * c - MEM_WIDTH
    tm = ROW_TILE

    def body(x_ref, g_ref, w_ref, hn_ref, hnt_ref, bg_ref, cg_ref, u_ref, qm_ref, z_ref, proj):
        xb = x_ref[...]
        r = lax.rsqrt(jnp.mean(xb * xb, axis=-1, keepdims=True) + EPS)
        hn = ((xb * r) * g_ref[...]).astype(BF16)
        hn_ref[...] = hn
        hnt_ref[...] = hn.T
        _project(hn, w_ref, proj)
        bg_ref[...] = proj[:, 0:c]
        cg_ref[...] = proj[:, c:2 * c]
        u_ref[...] = proj[:, 2 * c:3 * c]
        qm_ref[...] = proj[:, 3 * c:3 * c + MEM_WIDTH].astype(BF16)
        z_ref[...] = proj[:, 3 * c + MEM_WIDTH:]

    return _pallas_call(
        body, name="inproj_conv", grid=(s // tm,),
        out_shape=[_sds((s, d_model), BF16), _sds((d_model, s), BF16)] + [_sds((s, c), F32)] * 3
                  + [_sds((s, MEM_WIDTH), BF16), _sds((s, nz), F32)],
        in_specs=[_rows(d_model), _whole((1, d_model)), _whole(wg.shape)],
        out_specs=[_rows(d_model), pl.BlockSpec((d_model, tm), lambda i: (0, i))] + [_rows(c)] * 3
                  + [_rows(MEM_WIDTH), _rows(nz)],
        scratch_shapes=[pltpu.VMEM((tm, n), F32)],
        compiler_params=_params(1, 60),
    )(x, g, wg)


HALO = 8


def _halo_before(width, tm=ROW_TILE):
    return pl.BlockSpec((HALO, width), lambda i: (jnp.maximum(i * (tm // HALO) - 1, 0), 0))


def _halo_after(width, n_rows, tm=ROW_TILE):
    return pl.BlockSpec((HALO, width), lambda i: (jnp.minimum((i + 1) * (tm // HALO), n_rows // HALO - 1), 0))


def _conv_taps(cg_ref, u_ref, cgh_ref, uh_ref, i):
    a = cg_ref[...] * u_ref[...]
    ah = jnp.where(i > 0, cgh_ref[...] * uh_ref[...], 0.0)
    row = lax.broadcasted_iota(jnp.int32, a.shape, 0)
    a1 = jnp.where(row == 0, ah[HALO - 1:HALO], pltpu.roll(a, 1, 0))
    a2 = jnp.where(row == 0, ah[HALO - 2:HALO - 1], jnp.where(row == 1, ah[HALO - 1:HALO], pltpu.roll(a, 2, 0)))
    return a, a1, a2


def _post_conv_loss(bg, cg, u, qm, kv, z, h1, w_out, cw, gf, tgt):
    s, d = h1.shape
    c = CONV_WIDTH
    nb = c + MEM_WIDTH
    tm = ROW_TILE

    def body(bg_ref, cg_ref, u_ref, cgh_ref, uh_ref, qm_ref, kv_ref, z_ref, h_ref, w_ref, cw_ref, gf_ref, t_ref,
             y_ref, yt_ref, mo_ref, dh_ref, dhb_ref, loss_ref, dgf_ref):
        i = pl.program_id(0)
        a, a1, a2 = _conv_taps(cg_ref, u_ref, cgh_ref, uh_ref, i)
        conv = cw_ref[0:1, :] * a2 + cw_ref[1:2, :] * a1 + cw_ref[2:3, :] * a
        mix = bg_ref[...] * conv
        _mem_attn_into(qm_ref[...], kv_ref, mo_ref)
        sz, _ = _silu_parts(z_ref[...])
        y_ref[:, :c] = (mix * sz[:, :c]).astype(BF16)
        y_ref[:, c:] = (mo_ref[...] * sz[:, c:]).astype(BF16)
        y = y_ref[...]
        yt_ref[...] = y.T
        h2 = h_ref[...] + _dot(y, w_ref[...])
        r = lax.rsqrt(jnp.mean(h2 * h2, axis=-1, keepdims=True) + EPS)
        nh = h2 * r
        gfv = gf_ref[...]
        diff = nh * gfv - t_ref[...]
        dout = diff * (1.0 / d)
        dn = dout * gfv
        dh2 = r * dn - h2 * ((r * r * r) * jnp.mean(dn * h2, axis=-1, keepdims=True))
        dh_ref[...] = dh2
        dhb_ref[...] = dh2.astype(BF16)

        @pl.when(i == 0)
        def _():
            loss_ref[...] = jnp.zeros_like(loss_ref)
            dgf_ref[...] = jnp.zeros_like(dgf_ref)

        loss_ref[...] += 0.5 * jnp.sum(jnp.mean(diff * diff, axis=-1, keepdims=True))
        dgf_ref[...] += jnp.sum(dout * nh, axis=0, keepdims=True)

    return _pallas_call(
        body, name="post_conv_loss", grid=(s // tm,),
        out_shape=[_sds((s, nb), BF16), _sds((nb, s), BF16), _sds((s, MEM_WIDTH), F32), _sds((s, d), F32),
                   _sds((s, d), BF16), _plain((8, LANES), F32), _plain((1, d), F32)],
        in_specs=[_rows(c)] * 3 + [_halo_before(c)] * 2 + [_rows(MEM_WIDTH), _whole(kv.shape), _rows(nb), _rows(d),
                  _whole(w_out.shape), _whole(cw.shape), _whole((1, d)), _rows(d)],
        out_specs=[_rows(nb), pl.BlockSpec((nb, tm), lambda i: (0, i)), _rows(MEM_WIDTH), _rows(d), _rows(d),
                   _whole((8, LANES)), _whole((1, d))],
        compiler_params=_params(1, 48),
    )(bg, cg, u, cg, u, qm, kv, z, h1, w_out, cw, gf, tgt)


def _bwd_post_conv(dhb, w_out, bg, cg, u, z, qm, kv, mo, cw):
    s = dhb.shape[0]
    c = CONV_WIDTH
    nb = c + MEM_WIDTH

    def body(dh_ref, w_ref, bg_ref, cg_ref, u_ref, cgh_ref, uh_ref, z_ref, qm_ref, kv_ref, mo_ref, cw_ref,
             dz_ref, dbg_ref, dc_ref, dqm_ref, dkv_ref):
        i = pl.program_id(0)

        @pl.when(i == 0)
        def _():
            dkv_ref[...] = jnp.zeros_like(dkv_ref)

        dy = _dot_nt(dh_ref[...], w_ref[...])
        sz, dsz = _silu_parts(z_ref[...])
        a, a1, a2 = _conv_taps(cg_ref, u_ref, cgh_ref, uh_ref, i)
        conv = cw_ref[0:1, :] * a2 + cw_ref[1:2, :] * a1 + cw_ref[2:3, :] * a
        bgv = bg_ref[...]
        dz_ref[:, :c] = (dy[:, :c] * (bgv * conv) * dsz[:, :c]).astype(BF16)
        dz_ref[:, c:] = (dy[:, c:] * mo_ref[...] * dsz[:, c:]).astype(BF16)
        dbr = dy * sz
        dmix = dbr[:, :c]
        dbg_ref[...] = (dmix * conv).astype(BF16)
        dc_ref[...] = dmix * bgv
        _mem_attn_bwd(qm_ref[...], kv_ref, dbr[:, c:], dqm_ref, dkv_ref)

    return _pallas_call(
        body, name="bwd_post_conv", grid=(s // ROW_TILE,),
        out_shape=[_sds((s, nb), BF16), _sds((s, c), BF16), _sds((s, c), F32), _sds((s, MEM_WIDTH), BF16),
                   _plain(kv.shape, F32)],
        in_specs=[_rows(D_MODEL), _whole(w_out.shape)] + [_rows(c)] * 3 + [_halo_before(c)] * 2
                 + [_rows(nb), _rows(MEM_WIDTH), _whole(kv.shape), _rows(MEM_WIDTH), _whole(cw.shape)],
        out_specs=[_rows(nb), _rows(c), _rows(c), _rows(MEM_WIDTH), _whole(kv.shape)],
        compiler_params=_params(1, 48),
    )(dhb, w_out, bg, cg, u, cg, u, z, qm, kv, mo, cw)


def _bwd_conv(dconv, cg, u, cw):
    s, c = dconv.shape
    tm = ROW_TILE
    last = s // tm - 1

    def body(dc_ref, dcn_ref, cg_ref, u_ref, cgh_ref, uh_ref, cw_ref, dcg_ref, du_ref, dcw_ref):
        i = pl.program_id(0)

        @pl.when(i == 0)
        def _():
            dcw_ref[...] = jnp.zeros_like(dcw_ref)

        dc = dc_ref[...]
        dcn = jnp.where(i < last, dcn_ref[...], 0.0)
        row = lax.broadcasted_iota(jnp.int32, dc.shape, 0)
        d1 = jnp.where(row == tm - 1, dcn[0:1], pltpu.roll(dc, tm - 1, 0))
        d2 = jnp.where(row == tm - 1, dcn[1:2], jnp.where(row == tm - 2, dcn[0:1], pltpu.roll(dc, tm - 2, 0)))
        da = cw_ref[2:3, :] * dc + cw_ref[1:2, :] * d1 + cw_ref[0:1, :] * d2
        a, a1, a2 = _conv_taps(cg_ref, u_ref, cgh_ref, uh_ref, i)
        dcg_ref[...] = (da * u_ref[...]).astype(BF16)
        du_ref[...] = (da * cg_ref[...]).astype(BF16)
        dcw_ref[0:1, :] += jnp.sum(dc * a2, axis=0, keepdims=True)
        dcw_ref[1:2, :] += jnp.sum(dc * a1, axis=0, keepdims=True)
        dcw_ref[2:3, :] += jnp.sum(dc * a, axis=0, keepdims=True)

    return _pallas_call(
        body, name="bwd_conv", grid=(s // tm,),
        out_shape=[_sds((s, c), BF16), _sds((s, c), BF16), _plain((8, c), F32)],
        in_specs=[_rows(c), _halo_after(c, s), _rows(c), _rows(c), _halo_before(c), _halo_before(c), _whole(cw.shape)],
        out_specs=[_rows(c), _rows(c), _whole((8, c))], compiler_params=_params(1, 40),
    )(dconv, dconv, cg, u, cg, u, cw)


def _dgrad_norm(pieces, wg, h, g, dres, name):
    s, d_model = h.shape
    c = wg.shape[2]
    n = N_DEV * c
    tm = ROW_TILE
    widths = [p.shape[1] // d for p, d in pieces]
    assert sum(widths) == n
    n_p = len(pieces)

    def body(*refs):
        p_refs = refs[:n_p]
        w_ref, h_ref, g_ref, dr_ref, dh_ref, dhb_ref, dg_ref, dpd_ref, dp, scr = refs[n_p:]

        @pl.when(pl.program_id(0) == 0)
        def _():
            dg_ref[...] = jnp.zeros_like(dg_ref)

        off = 0
        for p_ref, (_, d), wd in zip(p_refs, pieces, widths):
            if d == 1:
                dp[:, off:off + wd] = p_ref[...]
            else:
                dp[:, off:off + wd] = _from_view(scr, p_ref, d).astype(BF16)
            off += wd
        dhn = jnp.zeros((tm, d_model), F32)
        for j in range(N_DEV):
            dpj = dp[:, j * c:(j + 1) * c]
            dpd_ref[j] = dpj
            dhn += _dot_nt(dpj, w_ref[j])
        hb = h_ref[...]
        r = lax.rsqrt(jnp.mean(hb * hb, axis=-1, keepdims=True) + EPS)
        dg_ref[...] += jnp.sum(dhn * (hb * r), axis=0, keepdims=True)
        dn = dhn * g_ref[...]
        dh = dr_ref[...] + r * dn - hb * ((r * r * r) * jnp.mean(dn * hb, axis=-1, keepdims=True))
        dh_ref[...] = dh
        dhb_ref[...] = dh.astype(BF16)

    p_specs = [_view_rows(wd, d) for (_, d), wd in zip(pieces, widths)]
    return _pallas_call(
        body, name=name, grid=(s // tm,),
        out_shape=[_plain((s, d_model), F32), _sds((s, d_model), BF16), _plain((1, d_model), F32), _sds((N_DEV, s, c), BF16)],
        in_specs=p_specs + [_whole(wg.shape), _rows(d_model), _whole((1, d_model)), _rows(d_model)],
        out_specs=[_rows(d_model), _rows(d_model), _whole((1, d_model)), pl.BlockSpec((N_DEV, tm, c), lambda i: (0, i, 0))],
        scratch_shapes=[pltpu.VMEM((tm, n), BF16), _view_scratch(GROUP_WIDTH)],
        compiler_params=_params(1, 60),
    )(*[p for p, _ in pieces], wg, h, g, dres)


def _wgrad_shards(a_t, b_dm, name):
    m, s = a_t.shape
    c = b_dm.shape[2]

    def body(a_ref, b_ref, o_ref):
        o_ref[...] = _dot(a_ref[...], b_ref[...]).astype(BF16)

    return _pallas_call(
        body, name=name, grid=(N_DEV,), out_shape=_sds((N_DEV, m, c), BF16),
        in_specs=[_whole(a_t.shape), pl.BlockSpec((None, s, c), lambda j: (j, 0, 0))],
        out_specs=pl.BlockSpec((None, m, c), lambda j: (j, 0, 0)), compiler_params=_params(1, 40),
    )(a_t, b_dm)


def _wgrad_cols(a_t, b, c, name):
    m, s = a_t.shape

    def body(a_ref, b_ref, o_ref):
        o_ref[...] = _dot(a_ref[...], b_ref[...]).astype(BF16)

    return _pallas_call(
        body, name=name, grid=(N_DEV,), out_shape=_sds((N_DEV, m, c), BF16),
        in_specs=[_whole(a_t.shape), pl.BlockSpec((s, c), lambda j: (0, j))],
        out_specs=pl.BlockSpec((None, m, c), lambda j: (j, 0, 0)), compiler_params=_params(1, 40),
    )(a_t, b)


def _wgrad_rows(a_t, b, name):
    m, s = a_t.shape
    n = b.shape[1]
    mr = m // N_DEV

    def body(a_ref, b_ref, o_ref):
        o_ref[...] = _dot(a_ref[...], b_ref[...]).astype(BF16)

    return _pallas_call(
        body, name=name, grid=(N_DEV,), out_shape=_sds((N_DEV, mr, n), BF16),
        in_specs=[pl.BlockSpec((mr, s), lambda j: (j, 0)), _whole(b.shape)],
        out_specs=pl.BlockSpec((None, mr, n), lambda j: (j, 0, 0)), compiler_params=_params(1, 40),
    )(a_t, b)


def _memkv_bwd(dkv, w, mem, g):
    n_layers = w.shape[0]

    def body(dkv_ref, w_ref, mem_ref, g_ref, dw_ref, dg_ref):
        mb = mem_ref[...]
        r = lax.rsqrt(jnp.mean(mb * mb, axis=-1, keepdims=True) + EPS)
        nm = mb * r
        mn = (nm * g_ref[...]).astype(BF16)
        dkvb = dkv_ref[...].astype(BF16)
        dw_ref[...] = _dot_tn(mn, dkvb).astype(BF16)
        dmn = _dot_nt(dkvb, w_ref[...])
        dg_ref[...] = jnp.sum(dmn * nm, axis=0, keepdims=True)

    lay = lambda *shape: pl.BlockSpec((None,) + shape, lambda l: (l, 0, 0))
    return _pallas_call(
        body, name="memkv_bwd", grid=(n_layers,),
        out_shape=[_plain((n_layers, D_MODEL, 2 * MEM_WIDTH), BF16), _plain((n_layers, 1, D_MODEL), F32)],
        in_specs=[lay(N_MEM, 2 * MEM_WIDTH), lay(D_MODEL, 2 * MEM_WIDTH), _whole(mem.shape), lay(1, D_MODEL)],
        out_specs=[lay(D_MODEL, 2 * MEM_WIDTH), lay(1, D_MODEL)], compiler_params=_params(1, 32),
    )(dkv, w, mem, g.reshape(n_layers, 1, D_MODEL))


def _bwd_post_attn(dhb, wg_out, z, os_, ls_, qm, kv, mo, head_ones):
    s = dhb.shape[0]
    gw = GROUP_WIDTH
    nb = gw + MEM_WIDTH
    c = wg_out.shape[2]
    tm = ROW_TILE

    def body(dh_ref, w_ref, z_ref, o0, o1, o2, l0, l1, l2, qm_ref, kv_ref, mo_ref, bd_ref,
             dz_ref, do0, do1, do2, dl0, dl1, dl2, dqm_ref, dkv_ref, s0, s1):
        @pl.when(pl.program_id(0) == 0)
        def _():
            dkv_ref[...] = jnp.zeros_like(dkv_ref)

        dy = jnp.zeros((tm, nb), F32)
        for j in range(N_DEV):
            dy += _dot_nt(dh_ref[:, j * c:(j + 1) * c], w_ref[j])
        ov, lv = [], []
        for o_ref, l_ref, d in zip((o0, o1, o2), (l0, l1, l2), DILATIONS):
            ov.append(_from_view(s0, o_ref, d))
            lv.append(_from_view(s1, l_ref, d))
        ws, mix = _mix_groups(ov, lv)
        sz, dsz = _silu_parts(z_ref[...])
        dz_ref[:, :gw] = (dy[:, :gw] * mix * dsz[:, :gw]).astype(BF16)
        dz_ref[:, gw:] = (dy[:, gw:] * mo_ref[...] * dsz[:, gw:]).astype(BF16)
        dbr = dy * sz
        dmix = dbr[:, :gw]
        t = dmix * mix
        th = t.astype(BF16)
        tl = (t - th.astype(F32)).astype(BF16)
        rs = _dot(th, bd_ref[...]) + _dot(tl, bd_ref[...])
        for wg_, do_ref, dl_ref, d in zip(ws, (do0, do1, do2), (dl0, dl1, dl2), DILATIONS):
            _to_view(s0, wg_ * dmix, do_ref, d)
            _to_view(s1, wg_ * rs, dl_ref, d)
        _mem_attn_bwd(qm_ref[...], kv_ref, dbr[:, gw:], dqm_ref, dkv_ref)

    vspecs = [_view_rows(gw, d) for d in DILATIONS]
    return _pallas_call(
        body, name="bwd_post_attn", grid=(s // tm,),
        out_shape=[_sds((s, nb), BF16)] + [_sds((s // d, d * gw), BF16) for d in DILATIONS]
                  + [_sds((s // d, d * gw), F32) for d in DILATIONS] + [_sds((s, MEM_WIDTH), BF16), _plain(kv.shape, F32)],
        in_specs=[_rows(D_MODEL), _whole(wg_out.shape), _rows(nb)] + vspecs * 2
                 + [_rows(MEM_WIDTH), _whole(kv.shape), _rows(MEM_WIDTH), _whole(head_ones.shape)],
        out_specs=[_rows(nb)] + vspecs * 2 + [_rows(MEM_WIDTH), _whole(kv.shape)],
        scratch_shapes=[_view_scratch(gw), _view_scratch(gw)],
        compiler_params=_params(1, 48),
    )(dhb, wg_out, z, *os_, *ls_, qm, kv, mo, head_ones)


def _attn_bwd(q, k, v, lse, do, dl, tabs, d, token):
    ln, dw = q.shape
    w = dw // d
    nb = ln // BLOCK
    nblk = d * nb
    reps = w // LANES
    cur, prev = _stream_maps(d, nb)
    qmap = lambda n: cur(jnp.minimum(n, nblk - 1))
    pmap = lambda n: prev(jnp.minimum(n, nblk - 1))
    omap = lambda n: cur(jnp.maximum(n - 1, 0))

    def body(_, q_ref, kp_ref, kc_ref, vp_ref, vc_ref, l_ref, do_ref, dl_ref, cq, saq, sbq, ck, sak, sbk,
             dq_ref, dk_ref, dv_ref, acck, accv, dqs):
        n = pl.program_id(0)

        @pl.when(n == 0)
        def _():
            acck[...] = jnp.zeros_like(acck)
            accv[...] = jnp.zeros_like(accv)

        def run(with_prev):
            if with_prev:
                kk = jnp.concatenate([kp_ref[...], kc_ref[...]], axis=0)
                vv = jnp.concatenate([vp_ref[...], vc_ref[...]], axis=0)
                rows = slice(0, 2 * BLOCK)
            else:
                kk, vv = kc_ref[...], vc_ref[...]
                rows = slice(BLOCK, 2 * BLOCK)
            valid = _band_mask(kk.shape[0])
            for h0 in range(0, HEADS_PER_GROUP, HEAD_BATCH):
                hs = range(h0, h0 + HEAD_BATCH)
                sls = [slice(h * HEAD_DIM, (h + 1) * HEAD_DIM) for h in hs]
                cols = [slice(h * HEAD_DIM, h * HEAD_DIM + 1) for h in hs]
                qhs = [q_ref[:, sl] for sl in sls]
                dobs = [do_ref[:, sl] for sl in sls]
                scs = [jnp.where(valid, _dot_nt(qh, kk[:, sl]), NEG) for qh, sl in zip(qhs, sls)]
                dps = [_dot_nt(dob, vv[:, sl]) for dob, sl in zip(dobs, sls)]
                ps = [jnp.exp(sc - l_ref[:, col]) for sc, col in zip(scs, cols)]
                dss = [(p * (dp - dl_ref[:, col])).astype(BF16) for p, dp, col in zip(ps, dps, cols)]
                pbs = [p.astype(BF16) for p in ps]
                for sl, qh, dob, ds, pb in zip(sls, qhs, dobs, dss, pbs):
                    dqs[:, sl] = _dot(ds, kk[:, sl]) * SCALE
                    acck[rows, sl] += _dot_tn(ds, qh)
                    accv[rows, sl] += _dot_tn(pb, dob)
            tq = [jnp.tile(r[...], (1, reps)) for r in (cq, saq, sbq)]
            dq_ref[...] = _rope_bwd(dqs[...], *tq).astype(BF16)

        pl.when(n < nblk)(lambda: _per_stream_block(n, nb, run))

        tk = [jnp.tile(r[...], (1, reps)) for r in (ck, sak, sbk)]
        dk_ref[...] = _rope_bwd(acck[0:BLOCK, :], *tk).astype(BF16)
        dv_ref[...] = accv[0:BLOCK, :].astype(BF16)
        acck[0:BLOCK, :] = acck[BLOCK:, :]
        accv[0:BLOCK, :] = accv[BLOCK:, :]
        acck[BLOCK:, :] = jnp.zeros((BLOCK, w), F32)
        accv[BLOCK:, :] = jnp.zeros((BLOCK, w), F32)

    blk = lambda f: pl.BlockSpec((BLOCK, w), f)
    tblk = lambda f: pl.BlockSpec((BLOCK, LANES), f)
    return _pallas_call(
        body, name=f"attn_bwd_d{d}", grid=(nblk + 1,),
        out_shape=[_sds((ln, dw), BF16)] * 3,
        in_specs=[ANY, blk(qmap), blk(pmap), blk(qmap), blk(pmap), blk(qmap), blk(qmap), blk(qmap), blk(qmap)]
                 + [tblk(qmap)] * 3 + [tblk(omap)] * 3,
        out_specs=[blk(qmap), blk(omap), blk(omap)],
        scratch_shapes=[pltpu.VMEM((2 * BLOCK, w), F32), pltpu.VMEM((2 * BLOCK, w), F32), pltpu.VMEM((BLOCK, w), F32)],
        compiler_params=_params(1, 32),
    )(token, q, k, k, v, v, lse, do, dl, *tabs, *tabs)


def _position():
    return lax.axis_index("x"), lax.axis_index("y"), lax.axis_index("c")


def _all_gather(shards, name):
    n_a = len(shards)

    def body(*refs):
        x_refs, out_refs = refs[:n_a], refs[n_a:2 * n_a]
        send_sems, recv_sems, local_sems = refs[2 * n_a:]
        x, y, c = _position()
        me, sibling = (x, y, c), (x, y, 1 - c)
        chips = [(1 - x, y), (x, 1 - y), (1 - x, 1 - y)]

        def rows(a, px, py, pc):
            return out_refs[a].at[4 * px + 2 * py + pc]

        def copy(a, k, block, to, own=False):
            return pltpu.make_async_remote_copy(
                src_ref=x_refs[a] if own else rows(a, *block), dst_ref=rows(a, *block),
                send_sem=send_sems.at[a, k], recv_sem=recv_sems.at[a, k], device_id=to, device_id_type=MESH)

        mine = [pltpu.make_async_copy(x_refs[a], rows(a, *me), local_sems.at[a]) for a in range(n_a)]
        for cp in mine:
            cp.start()
        first = []
        for j, chip in enumerate(chips):
            first += [copy(a, 1 + j, me, (*chip, c), own=True) for a in range(n_a)]
        first += [copy(a, 0, me, sibling, own=True) for a in range(n_a)]
        for cp in first:
            cp.start()
        passed = []
        for j, chip in enumerate(chips):
            for a in range(n_a):
                copy(a, 1 + j, (*chip, c), me).wait_recv()
                fwd = copy(a, 4 + j, (*chip, c), sibling)
                fwd.start()
                passed.append(fwd)
        for a in range(n_a):
            copy(a, 0, sibling, me).wait_recv()
        for j, chip in enumerate(chips):
            for a in range(n_a):
                copy(a, 4 + j, (*chip, 1 - c), me).wait_recv()
        for cp in first + passed:
            cp.wait_send()
        for cp in mine:
            cp.wait()

    return _pallas_call(
        body, name=name, out_shape=[_sds((N_DEV,) + t.shape, t.dtype) for t in shards],
        in_specs=[ANY] * n_a, out_specs=[ANY] * n_a,
        scratch_shapes=[pltpu.SemaphoreType.DMA((n_a, 7)), pltpu.SemaphoreType.DMA((n_a, 7)),
                        pltpu.SemaphoreType.DMA((n_a,))],
    )(*shards)


def _rs_to_sibling(gs):
    n_a = len(gs)

    def body(*refs):
        g_refs, recv_refs = refs[:n_a], refs[n_a:2 * n_a]
        send_sems, recv_sems = refs[2 * n_a:]
        x, y, c = _position()
        copies = []
        for k in range(4):
            for a in range(n_a):
                copies.append(pltpu.make_async_remote_copy(
                    src_ref=g_refs[a].at[2 * k + (1 - c)], dst_ref=recv_refs[a].at[k],
                    send_sem=send_sems.at[a, k], recv_sem=recv_sems.at[a, k],
                    device_id=(x, y, 1 - c), device_id_type=MESH))
        for cp in copies:
            cp.start()
        for cp in copies:
            cp.wait()

    return _pallas_call(
        body, name="rs_to_sibling", out_shape=[_sds((4,) + g.shape[1:], g.dtype) for g in gs],
        in_specs=[ANY] * n_a, out_specs=[ANY] * n_a,
        scratch_shapes=[pltpu.SemaphoreType.DMA((n_a, 4)), pltpu.SemaphoreType.DMA((n_a, 4))],
    )(*gs)


def _rs_to_chips(pbs):
    n_a = len(pbs)

    def body(*refs):
        p_refs, recv_refs = refs[:n_a], refs[n_a:2 * n_a]
        send_sems, recv_sems = refs[2 * n_a:]
        x, y, c = _position()
        chips = [(1 - x, y), (x, 1 - y), (1 - x, 1 - y)]
        copies = []
        for j, (px, py) in enumerate(chips):
            for a in range(n_a):
                copies.append(pltpu.make_async_remote_copy(
                    src_ref=p_refs[a].at[2 * px + py], dst_ref=recv_refs[a].at[j],
                    send_sem=send_sems.at[a, j], recv_sem=recv_sems.at[a, j],
                    device_id=(px, py, c), device_id_type=MESH))
        for cp in copies:
            cp.start()
        for cp in copies:
            cp.wait()

    return _pallas_call(
        body, name="rs_to_chips", out_shape=[_sds((3,) + p.shape[1:], p.dtype) for p in pbs],
        in_specs=[ANY] * n_a, out_specs=[ANY] * n_a,
        scratch_shapes=[pltpu.SemaphoreType.DMA((n_a, 3)), pltpu.SemaphoreType.DMA((n_a, 3))],
    )(*pbs)


HBM_SPEC = pl.BlockSpec(memory_space=pltpu.HBM)
SEM_SPEC = pl.BlockSpec(memory_space=pltpu.SEMAPHORE)
EFFECT = pltpu.SideEffectType.DATAFLOW_SIDE_EFFECTING
def _plan_gather_own(src_refs, land_refs):
    x, y, c = _position()
    me = 4 * x + 2 * y + c
    peers = [(x, y, 1 - c), (1 - x, y, c), (x, 1 - y, c), (1 - x, 1 - y, c)]
    return [(src_refs[a], land_refs[a].at[me], (a, k), peer) for k, peer in enumerate(peers) for a in range(len(src_refs))]


def _plan_gather_pass(src_refs, land_refs):
    x, y, c = _position()
    chips = [(1 - x, y), (x, 1 - y), (1 - x, 1 - y)]
    return [(land_refs[a].at[4 * px + 2 * py + c], land_refs[a].at[4 * px + 2 * py + c], (a, j), (x, y, 1 - c))
            for j, (px, py) in enumerate(chips) for a in range(len(land_refs))]


def _plan_to_sibling(src_refs, land_refs):
    x, y, c = _position()
    return [(src_refs[a].at[2 * k + (1 - c)], land_refs[a].at[k], (a, k), (x, y, 1 - c))
            for k in range(4) for a in range(len(src_refs))]


def _plan_to_chips(src_refs, land_refs):
    x, y, c = _position()
    chips = [(1 - x, y), (x, 1 - y), (1 - x, 1 - y)]
    return [(src_refs[a].at[2 * px + py], land_refs[a].at[j], (a, j), (px, py, c))
            for j, (px, py) in enumerate(chips) for a in range(len(src_refs))]


def _split_start(srcs, lands, plan, n_sem, after, name):
    n_s, n_a = len(srcs), len(lands)
    n_b = n_s + n_a

    def body(*refs):
        src_refs, land_refs = refs[:n_s], refs[n_s:n_b]
        send_sems, recv_sems, token = refs[n_b + 1], refs[n_b + 2], refs[-1]
        for src, dst, (a, k), dev in plan(src_refs, land_refs):
            i = a * n_sem + k
            pltpu.make_async_remote_copy(src_ref=src, dst_ref=dst, send_sem=send_sems.at[i], recv_sem=recv_sems.at[i],
                                         device_id=dev, device_id_type=MESH).start()
        token[...] = jnp.zeros_like(token)

    bufs = list(srcs) + list(lands)
    res = pl.pallas_call(
        body, name=name,
        out_shape=(pltpu.SemaphoreType.DMA((n_a * n_sem,)), pltpu.SemaphoreType.DMA((n_a * n_sem,)),
                   *[pltpu.HBM(t.shape, t.dtype) for t in bufs], _plain((8, LANES), F32)),
        in_specs=[HBM_SPEC] * n_b + [ANY],
        out_specs=(SEM_SPEC, SEM_SPEC, *[HBM_SPEC] * n_b, pl.BlockSpec(memory_space=pltpu.VMEM)),
        input_output_aliases={i: 2 + i for i in range(n_b)},
        compiler_params=pltpu.CompilerParams(has_side_effects=EFFECT),
    )(*[pltpu.with_memory_space_constraint(t, pltpu.HBM) for t in bufs], after)
    return (res[0], res[1], res[2:2 + n_s], res[2 + n_s:2 + n_b]), res[-1]


def _split_wait(started, plan, after, name):
    send_sems, recv_sems, srcs, lands = started
    n_s, n_a = len(srcs), len(lands)
    n_b = n_s + n_a
    n_sem = send_sems.shape[0] // n_a

    def body(*refs):
        src_refs, land_refs = refs[:n_s], refs[n_s:n_b]
        s_sems, r_sems = refs[n_b], refs[n_b + 1]
        for src, dst, (a, k), dev in plan(src_refs, land_refs):
            i = a * n_sem + k
            cp = pltpu.make_async_remote_copy(src_ref=src, dst_ref=dst, send_sem=s_sems.at[i], recv_sem=r_sems.at[i],
                                              device_id=dev, device_id_type=MESH)
            cp.wait_send()
            cp.wait_recv()

    bufs = list(srcs) + list(lands)
    res = pl.pallas_call(
        body, name=name, out_shape=tuple(pltpu.HBM(t.shape, t.dtype) for t in bufs),
        in_specs=[HBM_SPEC] * n_b + [SEM_SPEC, SEM_SPEC, ANY],
        out_specs=tuple([HBM_SPEC] * n_b),
        input_output_aliases={i: i for i in range(n_b)},
        compiler_params=pltpu.CompilerParams(has_side_effects=EFFECT),
    )(*bufs, send_sems, recv_sems, after)
    return res[:n_s], res[n_s:]


def _row_tile(r):
    return ROW_TILE if r % ROW_TILE == 0 else r


def _rs_add_sibling(gp, recv, c_arr, name):
    _, r, l = gp.shape
    tr = r if r <= 4 * ROW_TILE else _row_tile(r)

    def body(c_ref, g_ref, r_ref, pf_ref, pb_ref):
        sm = g_ref[...].astype(F32) + r_ref[...].astype(F32)
        pf_ref[...] = sm
        pb_ref[...] = sm.astype(BF16)

    spec = pl.BlockSpec((None, tr, l), lambda k, i, c: (k, i, 0))
    return _pallas_call(
        body, name=name,
        grid_spec=pltpu.PrefetchScalarGridSpec(
            num_scalar_prefetch=1, grid=(4, r // tr),
            in_specs=[pl.BlockSpec((None, tr, l), lambda k, i, c: (2 * k + c[0], i, 0)), spec],
            out_specs=[spec, spec]),
        out_shape=[_sds((4, r, l), F32), _sds((4, r, l), BF16)], compiler_params=_params(2, 32),
    )(c_arr, gp, recv)


def _adam_update(w, gv, m, v):
    nm = ADAM_B1 * m + (1.0 - ADAM_B1) * gv
    nv = ADAM_B2 * v + (1.0 - ADAM_B2) * (gv * gv)
    m_hat = nm / (1.0 - ADAM_B1 ** ADAM_STEP)
    v_hat = nv / (1.0 - ADAM_B2 ** ADAM_STEP)
    return -ADAM_LR * (m_hat / (jnp.sqrt(v_hat) + ADAM_EPS) + ADAM_WD * w), nm, nv


def _rs_finish_adamw(pf, recv, k_arr, w, m, v, name):
    _, r, l = pf.shape
    tr = _row_tile(r)

    def body(k_ref, p_ref, r_ref, w_ref, m_ref, v_ref, g_ref, d_ref, nm_ref, nv_ref):
        gv = ((p_ref[...] + r_ref[0].astype(F32)) + r_ref[1].astype(F32)) + r_ref[2].astype(F32)
        g_ref[...] = gv
        d_ref[...], nm_ref[...], nv_ref[...] = _adam_update(w_ref[...], gv, m_ref[...], v_ref[...])

    spec = pl.BlockSpec((tr, l), lambda i, k: (i, 0))
    return _pallas_call(
        body, name=name,
        grid_spec=pltpu.PrefetchScalarGridSpec(
            num_scalar_prefetch=1, grid=(r // tr,),
            in_specs=[pl.BlockSpec((None, tr, l), lambda i, k: (k[0], i, 0)),
                      pl.BlockSpec((3, tr, l), lambda i, k: (0, i, 0)), spec, spec, spec],
            out_specs=[spec] * 4),
        out_shape=[_plain((r, l), F32)] * 4, compiler_params=_params(1, 32),
    )(k_arr, pf, recv, w, m, v)


def _rs_finish_adamw_t(pf, recv, k_arr, w_t, m_t, v_t, name):
    _, r, c = pf.shape
    tr = _row_tile(r)
    cp = -(-c // LANES) * LANES

    def body(k_ref, p_ref, r_ref, w_ref, m_ref, v_ref, g_ref, d_ref, nm_ref, nv_ref, pad):
        gv = ((p_ref[...] + r_ref[0].astype(F32)) + r_ref[1].astype(F32)) + r_ref[2].astype(F32)
        pad[...] = jnp.zeros_like(pad)
        pad[:, 0:c] = gv
        gt = pad[...].T[0:c, :]
        g_ref[...] = gt
        d_ref[...], nm_ref[...], nv_ref[...] = _adam_update(w_ref[...], gt, m_ref[...], v_ref[...])

    spec = pl.BlockSpec((c, tr), lambda i, k: (0, i))
    return _pallas_call(
        body, name=name,
        grid_spec=pltpu.PrefetchScalarGridSpec(
            num_scalar_prefetch=1, grid=(r // tr,),
            in_specs=[pl.BlockSpec((None, tr, c), lambda i, k: (k[0], i, 0)),
                      pl.BlockSpec((3, tr, c), lambda i, k: (0, i, 0)), spec, spec, spec],
            out_specs=[spec] * 4, scratch_shapes=[pltpu.VMEM((tr, cp), F32)]),
        out_shape=[_plain((c, r), F32)] * 4, compiler_params=_params(1, 32),
    )(k_arr, pf, recv, w_t, m_t, v_t)


def _sum_devices(g):
    def body(g_ref, o_ref):
        acc = g_ref[0]
        for j in range(1, N_DEV):
            acc = acc + g_ref[j]
        o_ref[...] = acc

    return _pallas_call(body, name="sum_devices", out_shape=_plain(g.shape[1:], F32))(g)


def _adamw(w, g, m, v, name):
    shape = w.shape
    w2, g2, m2, v2 = [t.reshape((-1, shape[-1])) for t in (w, g, m, v)]

    def body(w_ref, g_ref, m_ref, v_ref, d_ref, nm_ref, nv_ref):
        d_ref[...], nm_ref[...], nv_ref[...] = _adam_update(w_ref[...], g_ref[...], m_ref[...], v_ref[...])

    outs = _pallas_call(body, name=name, out_shape=[_plain(w2.shape, F32)] * 3)(w2, g2, m2, v2)
    return tuple(t.reshape(shape) for t in outs)


def _after(t, token):
    return t + token[0:1, 0:1].astype(t.dtype)


def _local_step(x, mem, pos, tgt, norm_g, mem_norm_g, final_g, wg_in0, late_weights, late_token, c_arr):
    tabs = _rope_tables(pos)
    g0, g1 = _after(norm_g[0:1], late_token), norm_g[1:2]

    hn0, hn0_t, qs, ks, vs, tabs_v, qm0, z0 = _inproj_attn(x, g0, wg_in0, tabs)
    os_, ls_ = [], []
    for j, d in enumerate(DILATIONS):
        if j == 1:
            _, lands = _split_wait(late_weights, _plan_gather_own, ls_[0], "gather_late_wait")
            late_weights, late_token = _split_start([], lands, _plan_gather_pass, 3, ls_[0], "gather_late_pass_start")
        o, l = _attn_fwd(qs[j], ks[j], vs[j], d, late_token)
        os_.append(o)
        ls_.append(l)

    _, gathered = _split_wait(late_weights, _plan_gather_pass, ls_[2], "gather_late_pass_wait")
    wg_out0, wg_in1, wg_out1, wg_kv0, wg_kv1, cw_all = gathered
    w_out1 = wg_out1.reshape(-1, wg_out1.shape[2])
    w_kv = jnp.stack([wg_kv0.reshape(-1, wg_kv0.shape[2]), wg_kv1.reshape(-1, wg_kv1.shape[2])])
    cw = cw_all[:, 0:3].transpose(1, 0, 2).reshape(3, -1)
    kv = _memkv_fwd(mem, mem_norm_g, w_kv)
    y0, y0_t, mo0, h1 = _post_attn(os_, ls_, qm0, kv[0], z0, x, wg_out0)

    hn1, hn1_t, bg, cg, u, qm1, z1 = _inproj_conv(h1, g1, wg_in1)
    y1, y1_t, mo1, dh2, dh2b, loss_acc, d_final_g = _post_conv_loss(
        bg, cg, u, qm1, kv[1], z1, h1, w_out1, cw, final_g.reshape(1, -1), tgt)

    d_w_out1 = _wgrad_rows(y1_t, dh2b, "wgrad_out1")
    dz1, dbg, dconv, dqm1, dkv1 = _bwd_post_conv(dh2b, w_out1, bg, cg, u, z1, qm1, kv[1], mo1, cw)
    dcg, du, dcw = _bwd_conv(dconv, cg, u, cw)
    dh1, dh1b, dg1, dproj1 = _dgrad_norm([(dbg, 1), (dcg, 1), (du, 1), (dqm1, 1), (dz1, 1)], wg_in1, h1, g1, dh2,
                                         "dgrad_norm_conv")
    d_w_in1 = _wgrad_shards(hn1_t, dproj1, "wgrad_in1")

    grads1 = [d_w_in1, d_w_out1]
    started, token = _split_start(grads1, [lax.empty((4,) + g.shape[1:], g.dtype) for g in grads1],
                                  _plan_to_sibling, 4, dg1, "rs1_sibling_start")

    d_w_out0 = _wgrad_cols(y0_t, dh1b, wg_out0.shape[2], "wgrad_out0")
    gw = GROUP_WIDTH
    ones = (jnp.arange(gw)[:, None] // HEAD_DIM == jnp.arange(gw)[None, :] // HEAD_DIM).astype(BF16)
    ones = _after(ones, token)
    res = _bwd_post_attn(dh1b, wg_out0, z0, os_, ls_, qm0, kv[0], mo0, ones)
    dz0, dos, dls, dqm0, dkv0 = res[0], res[1:4], res[4:7], res[7], res[8]

    grads1, from_sibling = _split_wait(started, _plan_to_sibling, dz0, "rs1_sibling_wait")
    parts1 = [_rs_add_sibling(g, r, c_arr, "rs_add_sibling_" + n)
              for g, r, n in zip(grads1, from_sibling, ("conv_w_in", "conv_w_out"))]
    pbs1 = [pb for _, pb in parts1]
    started, token = _split_start(pbs1, [lax.empty((3,) + p.shape[1:], p.dtype) for p in pbs1],
                                  _plan_to_chips, 3, dg1, "rs1_chips_start")

    dqs, dks, dvs = [], [], []
    for j, d in enumerate(DILATIONS):
        dq, dk, dv = _attn_bwd(qs[j], ks[j], vs[j], ls_[j], dos[j], dls[j], tabs_v[j], d, token)
        dqs.append((dq, d))
        dks.append((dk, d))
        dvs.append((dv, d))
    dx, _, dg0, dproj0 = _dgrad_norm(dqs + dks + dvs + [(dqm0, 1), (dz0, 1)], wg_in0, x, g0, dh1, "dgrad_norm_attn")
    d_w_in0 = _wgrad_shards(hn0_t, dproj0, "wgrad_in0")
    _, from_chips1 = _split_wait(started, _plan_to_chips, d_w_in0, "rs1_chips_wait")

    d_w_kv, d_mem_g = _memkv_bwd(jnp.stack([dkv0, dkv1]), w_kv, mem, mem_norm_g)
    small = jnp.concatenate([dg0, dg1, d_mem_g.reshape(2, -1), d_final_g, dcw[0:3]], axis=0)
    n_kv = d_w_kv.shape[1] // N_DEV
    grads0 = [d_w_in0, d_w_out0, d_w_kv[0].reshape(N_DEV, n_kv, -1), d_w_kv[1].reshape(N_DEV, n_kv, -1)]
    return loss_acc[0, 0], dx, grads0, small, [pf for pf, _ in parts1], from_chips1


def kernel(x, mem, positions, norm_g, mem_norm_g, w_mem_kv, attn_w_in, attn_w_out, conv_w_in, conv_w, conv_w_out, final_g, loss_target, m_norm_g, m_mem_norm_g, m_w_mem_kv, m_attn_w_in, m_attn_w_out, m_conv_w_in, m_conv_w, m_conv_w_out, m_final_g, v_norm_g, v_mem_norm_g, v_w_mem_kv, v_attn_w_in, v_attn_w_out, v_conv_w_in, v_conv_w, v_conv_w_out, v_final_g):
    px, py, pc = _position()
    me = 4 * px + 2 * py + pc

    c_arr = jnp.reshape(pc, (1,)).astype(jnp.int32)
    k_arr = jnp.reshape(2 * px + py, (1,)).astype(jnp.int32)
    wg_in0 = _all_gather([attn_w_in[0].astype(BF16)], "gather_w_in0")[0]
    late = [attn_w_out[0].astype(BF16), conv_w_in[0].astype(BF16), conv_w_out[0].astype(BF16),
            w_mem_kv[0].astype(BF16), w_mem_kv[1].astype(BF16), jnp.pad(conv_w[0], ((0, 5), (0, 0)))]
    lands = [lax.dynamic_update_slice(lax.empty((N_DEV,) + t.shape, t.dtype), t[None], (me, 0, 0)) for t in late]
    late_weights, late_token = _split_start(late, lands, _plan_gather_own, 4, wg_in0, "gather_late_start")

    loss_part, dx, grads0, small_part, pfs1, from_chips1 = _local_step(
        x[0], mem[0], positions[0], loss_target[0], norm_g, mem_norm_g, final_g, wg_in0, late_weights, late_token, c_arr)

    names0 = ["attn_w_in", "attn_w_out", "w_mem_kv0", "w_mem_kv1"]
    from_sibling = _rs_to_sibling(grads0)
    parts0 = [_rs_add_sibling(g, r, c_arr, "rs_add_sibling_" + n) for g, r, n in zip(grads0, from_sibling, names0)]
    from_chips0 = _rs_to_chips([pb for _, pb in parts0])
    names = names0 + ["conv_w_in", "conv_w_out"]
    pfs = [pf for pf, _ in parts0] + list(pfs1)
    from_chips = list(from_chips0) + list(from_chips1)
    shards = [attn_w_in[0], attn_w_out[0], w_mem_kv[0], w_mem_kv[1], conv_w_in[0], conv_w_out[0]]
    moments = [(m_attn_w_in[0], v_attn_w_in[0]), (m_attn_w_out[0], v_attn_w_out[0]), (m_w_mem_kv[0], v_w_mem_kv[0]),
               (m_w_mem_kv[1], v_w_mem_kv[1]), (m_conv_w_in[0], v_conv_w_in[0]), (m_conv_w_out[0], v_conv_w_out[0])]
    big = {}
    for n, w, pf, r, (m, v) in zip(names, shards, pfs, from_chips, moments):
        if n in ("attn_w_in", "conv_w_in"):
            res = _rs_finish_adamw_t(pf, r, k_arr, w.T, m.T, v.T, "rs_finish_adamw_" + n)
            big[n] = tuple(t.T for t in res)
        else:
            big[n] = _rs_finish_adamw(pf, r, k_arr, w, m, v, "rs_finish_adamw_" + n)
    for n in ("attn_w_in", "attn_w_out", "conv_w_in", "conv_w_out"):
        big[n] = tuple(t[None] for t in big[n])
    big["w_mem_kv"] = tuple(jnp.stack([a, b]) for a, b in zip(big["w_mem_kv0"], big["w_mem_kv1"]))

    small_part = jnp.concatenate([small_part, jnp.broadcast_to(loss_part, small_part.shape)], axis=0)
    small = _sum_devices(_all_gather([small_part], "gather_small_grads")[0])
    loss = small[8, 0]
    g_conv_w = lax.dynamic_slice(small[5:8], (0, me * LANES), (3, LANES))[None]
    small_g = dict(norm_g=small[0:2], mem_norm_g=small[2:4], conv_w=g_conv_w, final_g=small[4])
    small_w = dict(norm_g=(norm_g, m_norm_g, v_norm_g), mem_norm_g=(mem_norm_g, m_mem_norm_g, v_mem_norm_g),
                   conv_w=(conv_w, m_conv_w, v_conv_w), final_g=(final_g, m_final_g, v_final_g))
    for n, (w, m, v) in small_w.items():
        big[n] = (small_g[n],) + _adamw(w, small_g[n], m, v, "adamw_" + n)

    order = ["norm_g", "mem_norm_g", "w_mem_kv", "attn_w_in", "attn_w_out", "conv_w_in", "conv_w", "conv_w_out", "final_g"]
    return (loss, dx[None], *[big[n][0] for n in order], *[big[n][1] for n in order],
            *[big[n][2] for n in order], *[big[n][3] for n in order])
```

```python
import functools

import jax
import jax.numpy as jnp
from jax import lax
from jax.experimental import pallas as pl
from jax.experimental.pallas import tpu as pltpu

F32 = jnp.float32
BF16 = jnp.bfloat16

N_DEV = 8
D_MODEL = 1024
HEAD_DIM = 64
ROT_DIM = HEAD_DIM // 4
ROPE_THETA = 500000.0
DILATIONS = (1, 4, 16)
HEADS_PER_GROUP = 8
HEAD_BATCH = 8
GROUP_WIDTH = HEADS_PER_GROUP * HEAD_DIM
BLOCK = 128
N_MEM = 256
MEM_HEADS = 4
MEM_WIDTH = MEM_HEADS * HEAD_DIM
CONV_WIDTH = D_MODEL
EPS = 1e-6
SCALE = HEAD_DIM ** -0.5
NEG = -1e30

ADAM_LR = 0.001
ADAM_B1 = 0.9
ADAM_B2 = 0.999
ADAM_EPS = 1e-08
ADAM_WD = 0.01
ADAM_STEP = 10

ROW_TILE = 256
LANES = 128
MESH = pl.DeviceIdType.MESH
ANY = pl.BlockSpec(memory_space=pl.ANY)


def _pallas_call(body, **kw):
    call = pl.pallas_call(body, **kw)

    def run(*args):
        pinned = [pltpu.with_memory_space_constraint(a, pltpu.HBM) if jnp.issubdtype(a.dtype, jnp.floating) else a
                  for a in args]
        return call(*pinned)

    return run


def _dot(a, b):
    return lax.dot_general(a, b, (((1,), (0,)), ((), ())), preferred_element_type=F32)


def _dot_nt(a, b):
    return lax.dot_general(a, b, (((1,), (1,)), ((), ())), preferred_element_type=F32)


def _dot_tn(a, b):
    return lax.dot_general(a, b, (((0,), (0,)), ((), ())), preferred_element_type=F32)


def _params(n_grid, vmem_mb=48):
    return pltpu.CompilerParams(dimension_semantics=("arbitrary",) * n_grid, vmem_limit_bytes=vmem_mb << 20)


def _rows(width, tm=ROW_TILE):
    return pl.BlockSpec((tm, width), lambda i: (i, 0))


def _view_rows(width, d, tm=ROW_TILE):
    return pl.BlockSpec((tm // d, d * width), lambda i: (i, 0))


def _whole(shape):
    return pl.BlockSpec(shape, lambda *_: (0,) * len(shape))


def _sds(shape, dtype):
    return pltpu.HBM(shape, dtype)


def _plain(shape, dtype):
    return jax.ShapeDtypeStruct(shape, dtype)


def _silu_parts(z):
    sg = jax.nn.sigmoid(z)
    return z * sg, sg * (1.0 + z * (1.0 - sg))


def _to_view(scr, val, out_ref, d):
    tm, w = val.shape
    if d == 1:
        out_ref[...] = val.astype(out_ref.dtype)
        return
    for cb in range(w // LANES):
        scr[cb] = val[:, cb * LANES:(cb + 1) * LANES]
    for r in range(d):
        for cb in range(w // LANES):
            lo = r * w + cb * LANES
            out_ref[:, lo:lo + LANES] = scr[cb, pl.ds(r, tm // d, stride=d), :].astype(out_ref.dtype)


def _from_view(scr, in_ref, d):
    if d == 1:
        return in_ref[...].astype(F32)
    nc, tm, _ = scr.shape
    w = nc * LANES
    for r in range(d):
        for cb in range(nc):
            lo = r * w + cb * LANES
            scr[cb, pl.ds(r, tm // d, stride=d), :] = in_ref[:, lo:lo + LANES].astype(F32)
    return jnp.concatenate([scr[cb] for cb in range(nc)], axis=1)


def _view_scratch(width, tm=ROW_TILE):
    return pltpu.VMEM((width // LANES, tm, LANES), F32)


def _rope_tables(pos):
    half = ROT_DIM // 2
    inv_freq = ROPE_THETA ** (-jnp.arange(half, dtype=F32) * (2.0 / ROT_DIM))
    ang = pos.astype(F32)[:, None] * inv_freq
    cos, sin = jnp.cos(ang), jnp.sin(ang)
    s = pos.shape[0]
    z8 = jnp.zeros((s, half), F32)
    rest = HEAD_DIM - ROT_DIM
    cosf = jnp.concatenate([cos, cos, jnp.ones((s, rest), F32)], axis=1)
    sa = jnp.concatenate([-sin, z8, jnp.zeros((s, rest), F32)], axis=1)
    sb = jnp.concatenate([z8, sin, jnp.zeros((s, rest), F32)], axis=1)
    return tuple(jnp.tile(t, (1, LANES // HEAD_DIM)) for t in (cosf, sa, sb))


def _rope_fwd(t, cv, sav, sbv):
    w = t.shape[1]
    return t * cv + pltpu.roll(t, w - ROT_DIM // 2, 1) * sav + pltpu.roll(t, ROT_DIM // 2, 1) * sbv


def _rope_bwd(g, cv, sav, sbv):
    w = g.shape[1]
    return g * cv + pltpu.roll(g * sav, ROT_DIM // 2, 1) + pltpu.roll(g * sbv, w - ROT_DIM // 2, 1)


def _project(hn, wg_ref, proj_scr):
    c = wg_ref.shape[2]
    for j in range(N_DEV):
        proj_scr[:, j * c:(j + 1) * c] = _dot(hn, wg_ref[j])


def _inproj_attn(x, g, wg, tabs):
    s, d_model = x.shape
    gw = GROUP_WIDTH
    n = N_DEV * wg.shape[2]
    nz = n - 9 * gw - MEM_WIDTH
    reps = gw // LANES
    tm = ROW_TILE

    def body(x_ref, g_ref, w_ref, c_ref, sa_ref, sb_ref, hn_ref, hnt_ref, *rest):
        outs, (proj, scr, tscr) = rest[:-3], rest[-3:]
        q_refs, k_refs, v_refs, t_refs, qm_ref, z_ref = outs[0:3], outs[3:6], outs[6:9], outs[9:18], outs[18], outs[19]
        xb = x_ref[...]
        r = lax.rsqrt(jnp.mean(xb * xb, axis=-1, keepdims=True) + EPS)
        hn = ((xb * r) * g_ref[...]).astype(BF16)
        hn_ref[...] = hn
        hnt_ref[...] = hn.T
        _project(hn, w_ref, proj)
        tab = (c_ref[...], sa_ref[...], sb_ref[...])
        cv, sav, sbv = [jnp.tile(t, (1, reps)) for t in tab]
        for j, d in enumerate(DILATIONS):
            tq = _rope_fwd(proj[:, j * gw:(j + 1) * gw], cv, sav, sbv)
            _to_view(scr, tq * SCALE, q_refs[j], d)
            tk = _rope_fwd(proj[:, (3 + j) * gw:(4 + j) * gw], cv, sav, sbv)
            _to_view(scr, tk, k_refs[j], d)
            _to_view(scr, proj[:, (6 + j) * gw:(7 + j) * gw], v_refs[j], d)
            for i in range(3):
                _to_view(tscr, tab[i], t_refs[3 * j + i], d)
        qm_ref[...] = proj[:, 9 * gw:9 * gw + MEM_WIDTH].astype(BF16)
        z_ref[...] = proj[:, 9 * gw + MEM_WIDTH:]

    views = [_sds((s // d, d * gw), BF16) for d in DILATIONS]
    tviews = [_sds((s // d, d * LANES), F32) for d in DILATIONS for _ in range(3)]
    out_shape = ([_sds((s, d_model), BF16), _sds((d_model, s), BF16)] + views * 3 + tviews
                 + [_sds((s, MEM_WIDTH), BF16), _sds((s, nz), F32)])
    vspecs = [_view_rows(gw, d) for d in DILATIONS]
    tspecs = [_view_rows(LANES, d) for d in DILATIONS for _ in range(3)]
    out_specs = ([_rows(d_model), pl.BlockSpec((d_model, tm), lambda i: (0, i))] + vspecs * 3 + tspecs
                 + [_rows(MEM_WIDTH), _rows(nz)])
    res = _pallas_call(
        body, name="inproj_attn", grid=(s // tm,), out_shape=out_shape,
        in_specs=[_rows(d_model), _whole((1, d_model)), _whole(wg.shape), _rows(LANES), _rows(LANES), _rows(LANES)],
        out_specs=out_specs,
        scratch_shapes=[pltpu.VMEM((tm, n), F32), _view_scratch(gw), _view_scratch(LANES)],
        compiler_params=_params(1, 60),
    )(x, g, wg, *tabs)
    tabs_v = [res[11 + 3 * j:14 + 3 * j] for j in range(3)]
    return res[0], res[1], res[2:5], res[5:8], res[8:11], tabs_v, res[20], res[21]


def _stream_maps(d, nb):
    def cur(n):
        return (n % nb, n // nb)

    def prev(n):
        return (jnp.maximum(n % nb - 1, 0), n // nb)

    return cur, prev


def _band_mask(n_keys):
    qi = lax.broadcasted_iota(jnp.int32, (BLOCK, n_keys), 0)
    kj = lax.broadcasted_iota(jnp.int32, (BLOCK, n_keys), 1)
    if n_keys == BLOCK:
        return kj <= qi
    return jnp.logical_or(jnp.logical_and(kj < BLOCK, kj >= qi), jnp.logical_and(kj >= BLOCK, (kj - BLOCK) <= qi))


def _per_stream_block(n, nb, run):
    if nb == 1:
        run(False)
        return
    first = (n % nb) == 0
    pl.when(first)(lambda: run(False))
    pl.when(jnp.logical_not(first))(lambda: run(True))


def _attn_fwd(q, k, v, d, token):
    ln, dw = q.shape
    w = dw // d
    nb = ln // BLOCK
    nblk = d * nb
    cur, prev = _stream_maps(d, nb)

    def body(_, q_ref, kp_ref, kc_ref, vp_ref, vc_ref, o_ref, lse_ref):
        def run(with_prev):
            if with_prev:
                kk = jnp.concatenate([kp_ref[...], kc_ref[...]], axis=0)
                vv = jnp.concatenate([vp_ref[...], vc_ref[...]], axis=0)
            else:
                kk, vv = kc_ref[...], vc_ref[...]
            valid = _band_mask(kk.shape[0])
            hb = HEAD_BATCH
            for h0 in range(0, HEADS_PER_GROUP, hb):
                sls = [slice(h * HEAD_DIM, (h + 1) * HEAD_DIM) for h in range(h0, h0 + hb)]
                scs = [jnp.where(valid, _dot_nt(q_ref[:, sl], kk[:, sl]), NEG) for sl in sls]
                ms = [jnp.max(sc, axis=-1, keepdims=True) for sc in scs]
                ps = [jnp.exp(sc - m) for sc, m in zip(scs, ms)]
                ls = [jnp.sum(p, axis=-1, keepdims=True) for p in ps]
                pns = [(p * (1.0 / l)).astype(BF16) for p, l in zip(ps, ls)]
                for sl, pn, m, l in zip(sls, pns, ms, ls):
                    o_ref[:, sl] = _dot(pn, vv[:, sl])
                    lse_ref[:, sl] = jnp.broadcast_to(m + jnp.log(l), (BLOCK, HEAD_DIM))

        _per_stream_block(pl.program_id(0), nb, run)

    blk = lambda f: pl.BlockSpec((BLOCK, w), f)
    return _pallas_call(
        body, name=f"attn_fwd_d{d}", grid=(nblk,),
        out_shape=[_sds((ln, dw), F32)] * 2,
        in_specs=[ANY, blk(cur), blk(prev), blk(cur), blk(prev), blk(cur)],
        out_specs=[blk(cur), blk(cur)], compiler_params=_params(1, 32),
    )(token, q, k, k, v, v)


def _memkv_fwd(mem, g, w):
    n_layers = w.shape[0]

    def body(mem_ref, g_ref, w_ref, kv_ref):
        mb = mem_ref[...]
        r = lax.rsqrt(jnp.mean(mb * mb, axis=-1, keepdims=True) + EPS)
        mn = ((mb * r) * g_ref[...]).astype(BF16)
        kv_ref[...] = _dot(mn, w_ref[...]).astype(BF16)

    return _pallas_call(
        body, name="memkv_fwd", grid=(n_layers,),
        out_shape=_plain((n_layers, N_MEM, 2 * MEM_WIDTH), BF16),
        in_specs=[_whole(mem.shape), pl.BlockSpec((None, 1, D_MODEL), lambda l: (l, 0, 0)),
                  pl.BlockSpec((None, D_MODEL, 2 * MEM_WIDTH), lambda l: (l, 0, 0))],
        out_specs=pl.BlockSpec((None, N_MEM, 2 * MEM_WIDTH), lambda l: (l, 0, 0)),
        compiler_params=_params(1, 32),
    )(mem, g.reshape(n_layers, 1, D_MODEL), w)


def _mix_groups(os_, ls_):
    mx = jnp.maximum(jnp.maximum(ls_[0], ls_[1]), ls_[2])
    es = [jnp.exp(t - mx) for t in ls_]
    inv = 1.0 / (es[0] + es[1] + es[2])
    ws = [e * inv for e in es]
    mix = ws[0] * os_[0] + ws[1] * os_[1] + ws[2] * os_[2]
    return ws, mix


MEM_SLICES = [slice(h * HEAD_DIM, (h + 1) * HEAD_DIM) for h in range(MEM_HEADS)]


def _mem_probs(qm, km):
    scs = [_dot_nt(qm[:, sl], km[:, sl]) * SCALE for sl in MEM_SLICES]
    es = [jnp.exp(sc - jnp.max(sc, axis=-1, keepdims=True)) for sc in scs]
    return [e * (1.0 / jnp.sum(e, axis=-1, keepdims=True)) for e in es]


def _mem_attn_into(qm, kv_ref, mo_ref):
    km, vm = kv_ref[:, :MEM_WIDTH], kv_ref[:, MEM_WIDTH:]
    ps = [p.astype(BF16) for p in _mem_probs(qm, km)]
    for sl, p in zip(MEM_SLICES, ps):
        mo_ref[:, sl] = _dot(p, vm[:, sl])


def _mem_attn_bwd(qm, kv_ref, dmem, dqm_ref, dkv_ref):
    km, vm = kv_ref[:, :MEM_WIDTH], kv_ref[:, MEM_WIDTH:]
    dmb = dmem.astype(BF16)
    ps = _mem_probs(qm, km)
    dps = [_dot_nt(dmb[:, sl], vm[:, sl]) for sl in MEM_SLICES]
    dss = [(p * (dp - jnp.sum(dp * p, axis=-1, keepdims=True)) * SCALE).astype(BF16) for p, dp in zip(ps, dps)]
    pbs = [p.astype(BF16) for p in ps]
    for h, (sl, ds, pb) in enumerate(zip(MEM_SLICES, dss, pbs)):
        slv = slice(MEM_WIDTH + h * HEAD_DIM, MEM_WIDTH + (h + 1) * HEAD_DIM)
        dqm_ref[:, sl] = _dot(ds, km[:, sl]).astype(BF16)
        dkv_ref[:, sl] += _dot_tn(ds, qm[:, sl])
        dkv_ref[:, slv] += _dot_tn(pb, dmb[:, sl])


def _post_attn(os_, ls_, qm, kv, z, x, wg_out):
    s, d_model = x.shape
    gw = GROUP_WIDTH
    nb = gw + MEM_WIDTH
    c = wg_out.shape[2]
    tm = ROW_TILE

    def body(o0, o1, o2, l0, l1, l2, qm_ref, kv_ref, z_ref, x_ref, w_ref, y_ref, yt_ref, mo_ref, h_ref, s0, s1):
        ov, lv = [], []
        for o_ref, l_ref, d in zip((o0, o1, o2), (l0, l1, l2), DILATIONS):
            ov.append(_from_view(s0, o_ref, d))
            lv.append(_from_view(s1, l_ref, d))
        _, mix = _mix_groups(ov, lv)
        _mem_attn_into(qm_ref[...], kv_ref, mo_ref)
        sz, _ = _silu_parts(z_ref[...])
        y_ref[:, :gw] = (mix * sz[:, :gw]).astype(BF16)
        y_ref[:, gw:] = (mo_ref[...] * sz[:, gw:]).astype(BF16)
        y = y_ref[...]
        yt_ref[...] = y.T
        for j in range(N_DEV):
            h_ref[:, j * c:(j + 1) * c] = x_ref[:, j * c:(j + 1) * c] + _dot(y, w_ref[j])

    vspecs = [_view_rows(gw, d) for d in DILATIONS]
    return _pallas_call(
        body, name="post_attn", grid=(s // tm,),
        out_shape=[_sds((s, nb), BF16), _sds((nb, s), BF16), _sds((s, MEM_WIDTH), F32), _sds((s, d_model), F32)],
        in_specs=vspecs * 2 + [_rows(MEM_WIDTH), _whole(kv.shape), _rows(nb), _rows(d_model), _whole(wg_out.shape)],
        out_specs=[_rows(nb), pl.BlockSpec((nb, tm), lambda i: (0, i)), _rows(MEM_WIDTH), _rows(d_model)],
        scratch_shapes=[_view_scratch(gw), _view_scratch(gw)],
        compiler_params=_params(1, 40),
    )(*os_, *ls_, qm, kv, z, x, wg_out)


def _inproj_conv(x, g, wg):
    s, d_model = x.shape
    c = CONV_WIDTH
    n = N_DEV * wg.shape[2]
    nz = n - 3 * c - MEM_WIDTH
    tm = ROW_TILE

    def body(x_ref, g_ref, w_ref, hn_ref, hnt_ref, bg_ref, cg_ref, u_ref, qm_ref, z_ref, proj):
        xb = x_ref[...]
        r = lax.rsqrt(jnp.mean(xb * xb, axis=-1, keepdims=True) + EPS)
        hn = ((xb * r) * g_ref[...]).astype(BF16)
        hn_ref[...] = hn
        hnt_ref[...] = hn.T
        _project(hn, w_ref, proj)
        bg_ref[...] = proj[:, 0:c]
        cg_ref[...] = proj[:, c:2 * c]
        u_ref[...] = proj[:, 2 * c:3 * c]
        qm_ref[...] = proj[:, 3 * c:3 * c + MEM_WIDTH].astype(BF16)
        z_ref[...] = proj[:, 3 * c + MEM_WIDTH:]

    return _pallas_call(
        body, name="inproj_conv", grid=(s // tm,),
        out_shape=[_sds((s, d_model), BF16), _sds((d_model, s), BF16)] + [_sds((s, c), F32)] * 3
                  + [_sds((s, MEM_WIDTH), BF16), _sds((s, nz), F32)],
        in_specs=[_rows(d_model), _whole((1, d_model)), _whole(wg.shape)],
        out_specs=[_rows(d_model), pl.BlockSpec((d_model, tm), lambda i: (0, i))] + [_rows(c)] * 3
                  + [_rows(MEM_WIDTH), _rows(nz)],
        scratch_shapes=[pltpu.VMEM((tm, n), F32)],
        compiler_params=_params(1, 60),
    )(x, g, wg)


HALO = 8


def _halo_before(width, tm=ROW_TILE):
    return pl.BlockSpec((HALO, width), lambda i: (jnp.maximum(i * (tm // HALO) - 1, 0), 0))


def _halo_after(width, n_rows, tm=ROW_TILE):
    return pl.BlockSpec((HALO, width), lambda i: (jnp.minimum((i + 1) * (tm // HALO), n_rows // HALO - 1), 0))


def _conv_taps(cg_ref, u_ref, cgh_ref, uh_ref, i):
    a = cg_ref[...] * u_ref[...]
    ah = jnp.where(i > 0, cgh_ref[...] * uh_ref[...], 0.0)
    row = lax.broadcasted_iota(jnp.int32, a.shape, 0)
    a1 = jnp.where(row == 0, ah[HALO - 1:HALO], pltpu.roll(a, 1, 0))
    a2 = jnp.where(row == 0, ah[HALO - 2:HALO - 1], jnp.where(row == 1, ah[HALO - 1:HALO], pltpu.roll(a, 2, 0)))
    return a, a1, a2


def _post_conv_loss(bg, cg, u, qm, kv, z, h1, w_out, cw, gf, tgt):
    s, d = h1.shape
    c = CONV_WIDTH
    nb = c + MEM_WIDTH
    tm = ROW_TILE

    def body(bg_ref, cg_ref, u_ref, cgh_ref, uh_ref, qm_ref, kv_ref, z_ref, h_ref, w_ref, cw_ref, gf_ref, t_ref,
             y_ref, yt_ref, mo_ref, dh_ref, dhb_ref, loss_ref, dgf_ref):
        i = pl.program_id(0)
        a, a1, a2 = _conv_taps(cg_ref, u_ref, cgh_ref, uh_ref, i)
        conv = cw_ref[0:1, :] * a2 + cw_ref[1:2, :] * a1 + cw_ref[2:3, :] * a
        mix = bg_ref[...] * conv
        _mem_attn_into(qm_ref[...], kv_ref, mo_ref)
        sz, _ = _silu_parts(z_ref[...])
        y_ref[:, :c] = (mix * sz[:, :c]).astype(BF16)
        y_ref[:, c:] = (mo_ref[...] * sz[:, c:]).astype(BF16)
        y = y_ref[...]
        yt_ref[...] = y.T
        h2 = h_ref[...] + _dot(y, w_ref[...])
        r = lax.rsqrt(jnp.mean(h2 * h2, axis=-1, keepdims=True) + EPS)
        nh = h2 * r
        gfv = gf_ref[...]
        diff = nh * gfv - t_ref[...]
        dout = diff * (1.0 / d)
        dn = dout * gfv
        dh2 = r * dn - h2 * ((r * r * r) * jnp.mean(dn * h2, axis=-1, keepdims=True))
        dh_ref[...] = dh2
        dhb_ref[...] = dh2.astype(BF16)

        @pl.when(i == 0)
        def _():
            loss_ref[...] = jnp.zeros_like(loss_ref)
            dgf_ref[...] = jnp.zeros_like(dgf_ref)

        loss_ref[...] += 0.5 * jnp.sum(jnp.mean(diff * diff, axis=-1, keepdims=True))
        dgf_ref[...] += jnp.sum(dout * nh, axis=0, keepdims=True)

    return _pallas_call(
        body, name="post_conv_loss", grid=(s // tm,),
        out_shape=[_sds((s, nb), BF16), _sds((nb, s), BF16), _sds((s, MEM_WIDTH), F32), _sds((s, d), F32),
                   _sds((s, d), BF16), _plain((8, LANES), F32), _plain((1, d), F32)],
        in_specs=[_rows(c)] * 3 + [_halo_before(c)] * 2 + [_rows(MEM_WIDTH), _whole(kv.shape), _rows(nb), _rows(d),
                  _whole(w_out.shape), _whole(cw.shape), _whole((1, d)), _rows(d)],
        out_specs=[_rows(nb), pl.BlockSpec((nb, tm), lambda i: (0, i)), _rows(MEM_WIDTH), _rows(d), _rows(d),
                   _whole((8, LANES)), _whole((1, d))],
        compiler_params=_params(1, 48),
    )(bg, cg, u, cg, u, qm, kv, z, h1, w_out, cw, gf, tgt)


def _bwd_post_conv(dhb, w_out, bg, cg, u, z, qm, kv, mo, cw):
    s = dhb.shape[0]
    c = CONV_WIDTH
    nb = c + MEM_WIDTH

    def body(dh_ref, w_ref, bg_ref, cg_ref, u_ref, cgh_ref, uh_ref, z_ref, qm_ref, kv_ref, mo_ref, cw_ref,
             dz_ref, dbg_ref, dc_ref, dqm_ref, dkv_ref):
        i = pl.program_id(0)

        @pl.when(i == 0)
        def _():
            dkv_ref[...] = jnp.zeros_like(dkv_ref)

        dy = _dot_nt(dh_ref[...], w_ref[...])
        sz, dsz = _silu_parts(z_ref[...])
        a, a1, a2 = _conv_taps(cg_ref, u_ref, cgh_ref, uh_ref, i)
        conv = cw_ref[0:1, :] * a2 + cw_ref[1:2, :] * a1 + cw_ref[2:3, :] * a
        bgv = bg_ref[...]
        dz_ref[:, :c] = (dy[:, :c] * (bgv * conv) * dsz[:, :c]).astype(BF16)
        dz_ref[:, c:] = (dy[:, c:] * mo_ref[...] * dsz[:, c:]).astype(BF16)
        dbr = dy * sz
        dmix = dbr[:, :c]
        dbg_ref[...] = (dmix * conv).astype(BF16)
        dc_ref[...] = dmix * bgv
        _mem_attn_bwd(qm_ref[...], kv_ref, dbr[:, c:], dqm_ref, dkv_ref)

    return _pallas_call(
        body, name="bwd_post_conv", grid=(s // ROW_TILE,),
        out_shape=[_sds((s, nb), BF16), _sds((s, c), BF16), _sds((s, c), F32), _sds((s, MEM_WIDTH), BF16),
                   _plain(kv.shape, F32)],
        in_specs=[_rows(D_MODEL), _whole(w_out.shape)] + [_rows(c)] * 3 + [_halo_before(c)] * 2
                 + [_rows(nb), _rows(MEM_WIDTH), _whole(kv.shape), _rows(MEM_WIDTH), _whole(cw.shape)],
        out_specs=[_rows(nb), _rows(c), _rows(c), _rows(MEM_WIDTH), _whole(kv.shape)],
        compiler_params=_params(1, 48),
    )(dhb, w_out, bg, cg, u, cg, u, z, qm, kv, mo, cw)


def _bwd_conv(dconv, cg, u, cw):
    s, c = dconv.shape
    tm = ROW_TILE
    last = s // tm - 1

    def body(dc_ref, dcn_ref, cg_ref, u_ref, cgh_ref, uh_ref, cw_ref, dcg_ref, du_ref, dcw_ref):
        i = pl.program_id(0)

        @pl.when(i == 0)
        def _():
            dcw_ref[...] = jnp.zeros_like(dcw_ref)

        dc = dc_ref[...]
        dcn = jnp.where(i < last, dcn_ref[...], 0.0)
        row = lax.broadcasted_iota(jnp.int32, dc.shape, 0)
        d1 = jnp.where(row == tm - 1, dcn[0:1], pltpu.roll(dc, tm - 1, 0))
        d2 = jnp.where(row == tm - 1, dcn[1:2], jnp.where(row == tm - 2, dcn[0:1], pltpu.roll(dc, tm - 2, 0)))
        da = cw_ref[2:3, :] * dc + cw_ref[1:2, :] * d1 + cw_ref[0:1, :] * d2
        a, a1, a2 = _conv_taps(cg_ref, u_ref, cgh_ref, uh_ref, i)
        dcg_ref[...] = (da * u_ref[...]).astype(BF16)
        du_ref[...] = (da * cg_ref[...]).astype(BF16)
        dcw_ref[0:1, :] += jnp.sum(dc * a2, axis=0, keepdims=True)
        dcw_ref[1:2, :] += jnp.sum(dc * a1, axis=0, keepdims=True)
        dcw_ref[2:3, :] += jnp.sum(dc * a, axis=0, keepdims=True)

    return _pallas_call(
        body, name="bwd_conv", grid=(s // tm,),
        out_shape=[_sds((s, c), BF16), _sds((s, c), BF16), _plain((8, c), F32)],
        in_specs=[_rows(c), _halo_after(c, s), _rows(c), _rows(c), _halo_before(c), _halo_before(c), _whole(cw.shape)],
        out_specs=[_rows(c), _rows(c), _whole((8, c))], compiler_params=_params(1, 40),
    )(dconv, dconv, cg, u, cg, u, cw)


def _dgrad_norm(pieces, wg, h, g, dres, name):
    s, d_model = h.shape
    c = wg.shape[2]
    n = N_DEV * c
    tm = ROW_TILE
    widths = [p.shape[1] // d for p, d in pieces]
    assert sum(widths) == n
    n_p = len(pieces)

    def body(*refs):
        p_refs = refs[:n_p]
        w_ref, h_ref, g_ref, dr_ref, dh_ref, dhb_ref, dg_ref, dpd_ref, dp, scr = refs[n_p:]

        @pl.when(pl.program_id(0) == 0)
        def _():
            dg_ref[...] = jnp.zeros_like(dg_ref)

        off = 0
        for p_ref, (_, d), wd in zip(p_refs, pieces, widths):
            if d == 1:
                dp[:, off:off + wd] = p_ref[...]
            else:
                dp[:, off:off + wd] = _from_view(scr, p_ref, d).astype(BF16)
            off += wd
        dhn = jnp.zeros((tm, d_model), F32)
        for j in range(N_DEV):
            dpj = dp[:, j * c:(j + 1) * c]
            dpd_ref[j] = dpj
            dhn += _dot_nt(dpj, w_ref[j])
        hb = h_ref[...]
        r = lax.rsqrt(jnp.mean(hb * hb, axis=-1, keepdims=True) + EPS)
        dg_ref[...] += jnp.sum(dhn * (hb * r), axis=0, keepdims=True)
        dn = dhn * g_ref[...]
        dh = dr_ref[...] + r * dn - hb * ((r * r * r) * jnp.mean(dn * hb, axis=-1, keepdims=True))
        dh_ref[...] = dh
        dhb_ref[...] = dh.astype(BF16)

    p_specs = [_view_rows(wd, d) for (_, d), wd in zip(pieces, widths)]
    return _pallas_call(
        body, name=name, grid=(s // tm,),
        out_shape=[_plain((s, d_model), F32), _sds((s, d_model), BF16), _plain((1, d_model), F32), _sds((N_DEV, s, c), BF16)],
        in_specs=p_specs + [_whole(wg.shape), _rows(d_model), _whole((1, d_model)), _rows(d_model)],
        out_specs=[_rows(d_model), _rows(d_model), _whole((1, d_model)), pl.BlockSpec((N_DEV, tm, c), lambda i: (0, i, 0))],
        scratch_shapes=[pltpu.VMEM((tm, n), BF16), _view_scratch(GROUP_WIDTH)],
        compiler_params=_params(1, 60),
    )(*[p for p, _ in pieces], wg, h, g, dres)


def _assemble_dproj(pieces, c, name):
    n = N_DEV * c
    tm = ROW_TILE
    widths = [p.shape[1] // d for p, d in pieces]
    assert sum(widths) == n
    s = pieces[0][0].shape[0] * pieces[0][1]
    n_p = len(pieces)

    def body(*refs):
        p_refs, (dpd_ref, dp, scr) = refs[:n_p], refs[n_p:]
        off = 0
        for p_ref, (_, d), wd in zip(p_refs, pieces, widths):
            if d == 1:
                dp[:, off:off + wd] = p_ref[...]
            else:
                dp[:, off:off + wd] = _from_view(scr, p_ref, d).astype(BF16)
            off += wd
        for j in range(N_DEV):
            dpd_ref[j] = dp[:, j * c:(j + 1) * c]

    return _pallas_call(
        body, name=name, grid=(s // tm,), out_shape=_sds((N_DEV, s, c), BF16),
        in_specs=[_view_rows(wd, d) for (_, d), wd in zip(pieces, widths)],
        out_specs=pl.BlockSpec((N_DEV, tm, c), lambda i: (0, i, 0)),
        scratch_shapes=[pltpu.VMEM((tm, n), BF16), _view_scratch(GROUP_WIDTH)],
        compiler_params=_params(1, 40),
    )(*[p for p, _ in pieces])


def _dgrad_norm_dm(dproj_dm, wg, h, g, dres, token, name):
    s, d_model = h.shape
    c = wg.shape[2]
    tm = ROW_TILE

    def body(_, dp_ref, w_ref, h_ref, g_ref, dr_ref, dh_ref, dg_ref):
        @pl.when(pl.program_id(0) == 0)
        def _():
            dg_ref[...] = jnp.zeros_like(dg_ref)

        dhn = jnp.zeros((tm, d_model), F32)
        for j in range(N_DEV):
            dhn += _dot_nt(dp_ref[j], w_ref[j])
        hb = h_ref[...]
        r = lax.rsqrt(jnp.mean(hb * hb, axis=-1, keepdims=True) + EPS)
        dg_ref[...] += jnp.sum(dhn * (hb * r), axis=0, keepdims=True)
        dn = dhn * g_ref[...]
        dh_ref[...] = dr_ref[...] + r * dn - hb * ((r * r * r) * jnp.mean(dn * hb, axis=-1, keepdims=True))

    return _pallas_call(
        body, name=name, grid=(s // tm,),
        out_shape=[_plain((s, d_model), F32), _plain((1, d_model), F32)],
        in_specs=[ANY, pl.BlockSpec((N_DEV, tm, c), lambda i: (0, i, 0)), _whole(wg.shape), _rows(d_model),
                  _whole((1, d_model)), _rows(d_model)],
        out_specs=[_rows(d_model), _whole((1, d_model))],
        compiler_params=_params(1, 60),
    )(token, dproj_dm, wg, h, g, dres)


def _wgrad_shards(a_t, b_dm, name):
    m, s = a_t.shape
    c = b_dm.shape[2]

    def body(a_ref, b_ref, o_ref):
        o_ref[...] = _dot(a_ref[...], b_ref[...]).astype(BF16)

    return _pallas_call(
        body, name=name, grid=(N_DEV,), out_shape=_sds((N_DEV, m, c), BF16),
        in_specs=[_whole(a_t.shape), pl.BlockSpec((None, s, c), lambda j: (j, 0, 0))],
        out_specs=pl.BlockSpec((None, m, c), lambda j: (j, 0, 0)), compiler_params=_params(1, 40),
    )(a_t, b_dm)


def _wgrad_cols(a_t, b, c, name):
    m, s = a_t.shape

    def body(a_ref, b_ref, o_ref):
        o_ref[...] = _dot(a_ref[...], b_ref[...]).astype(BF16)

    return _pallas_call(
        body, name=name, grid=(N_DEV,), out_shape=_sds((N_DEV, m, c), BF16),
        in_specs=[_whole(a_t.shape), pl.BlockSpec((s, c), lambda j: (0, j))],
        out_specs=pl.BlockSpec((None, m, c), lambda j: (j, 0, 0)), compiler_params=_params(1, 40),
    )(a_t, b)


def _wgrad_rows(a_t, b, name):
    m, s = a_t.shape
    n = b.shape[1]
    mr = m // N_DEV

    def body(a_ref, b_ref, o_ref):
        o_ref[...] = _dot(a_ref[...], b_ref[...]).astype(BF16)

    return _pallas_call(
        body, name=name, grid=(N_DEV,), out_shape=_sds((N_DEV, mr, n), BF16),
        in_specs=[pl.BlockSpec((mr, s), lambda j: (j, 0)), _whole(b.shape)],
        out_specs=pl.BlockSpec((None, mr, n), lambda j: (j, 0, 0)), compiler_params=_params(1, 40),
    )(a_t, b)


def _memkv_bwd(dkv, w, mem, g):
    n_layers = w.shape[0]

    def body(dkv_ref, w_ref, mem_ref, g_ref, dw_ref, dg_ref):
        mb = mem_ref[...]
        r = lax.rsqrt(jnp.mean(mb * mb, axis=-1, keepdims=True) + EPS)
        nm = mb * r
        mn = (nm * g_ref[...]).astype(BF16)
        dkvb = dkv_ref[...].astype(BF16)
        dw_ref[...] = _dot_tn(mn, dkvb).astype(BF16)
        dmn = _dot_nt(dkvb, w_ref[...])
        dg_ref[...] = jnp.sum(dmn * nm, axis=0, keepdims=True)

    lay = lambda *shape: pl.BlockSpec((None,) + shape, lambda l: (l, 0, 0))
    return _pallas_call(
        body, name="memkv_bwd", grid=(n_layers,),
        out_shape=[_plain((n_layers, D_MODEL, 2 * MEM_WIDTH), BF16), _plain((n_layers, 1, D_MODEL), F32)],
        in_specs=[lay(N_MEM, 2 * MEM_WIDTH), lay(D_MODEL, 2 * MEM_WIDTH), _whole(mem.shape), lay(1, D_MODEL)],
        out_specs=[lay(D_MODEL, 2 * MEM_WIDTH), lay(1, D_MODEL)], compiler_params=_params(1, 32),
    )(dkv, w, mem, g.reshape(n_layers, 1, D_MODEL))


def _bwd_post_attn(dhb, wg_out, z, os_, ls_, qm, kv, mo, head_ones):
    s = dhb.shape[0]
    gw = GROUP_WIDTH
    nb = gw + MEM_WIDTH
    c = wg_out.shape[2]
    tm = ROW_TILE

    def body(dh_ref, w_ref, z_ref, o0, o1, o2, l0, l1, l2, qm_ref, kv_ref, mo_ref, bd_ref,
             dz_ref, do0, do1, do2, dl0, dl1, dl2, dqm_ref, dkv_ref, s0, s1):
        @pl.when(pl.program_id(0) == 0)
        def _():
            dkv_ref[...] = jnp.zeros_like(dkv_ref)

        dy = jnp.zeros((tm, nb), F32)
        for j in range(N_DEV):
            dy += _dot_nt(dh_ref[:, j * c:(j + 1) * c], w_ref[j])
        ov, lv = [], []
        for o_ref, l_ref, d in zip((o0, o1, o2), (l0, l1, l2), DILATIONS):
            ov.append(_from_view(s0, o_ref, d))
            lv.append(_from_view(s1, l_ref, d))
        ws, mix = _mix_groups(ov, lv)
        sz, dsz = _silu_parts(z_ref[...])
        dz_ref[:, :gw] = (dy[:, :gw] * mix * dsz[:, :gw]).astype(BF16)
        dz_ref[:, gw:] = (dy[:, gw:] * mo_ref[...] * dsz[:, gw:]).astype(BF16)
        dbr = dy * sz
        dmix = dbr[:, :gw]
        t = dmix * mix
        th = t.astype(BF16)
        tl = (t - th.astype(F32)).astype(BF16)
        rs = _dot(th, bd_ref[...]) + _dot(tl, bd_ref[...])
        for wg_, do_ref, dl_ref, d in zip(ws, (do0, do1, do2), (dl0, dl1, dl2), DILATIONS):
            _to_view(s0, wg_ * dmix, do_ref, d)
            _to_view(s1, wg_ * rs, dl_ref, d)
        _mem_attn_bwd(qm_ref[...], kv_ref, dbr[:, gw:], dqm_ref, dkv_ref)

    vspecs = [_view_rows(gw, d) for d in DILATIONS]
    return _pallas_call(
        body, name="bwd_post_attn", grid=(s // tm,),
        out_shape=[_sds((s, nb), BF16)] + [_sds((s // d, d * gw), BF16) for d in DILATIONS]
                  + [_sds((s // d, d * gw), F32) for d in DILATIONS] + [_sds((s, MEM_WIDTH), BF16), _plain(kv.shape, F32)],
        in_specs=[_rows(D_MODEL), _whole(wg_out.shape), _rows(nb)] + vspecs * 2
                 + [_rows(MEM_WIDTH), _whole(kv.shape), _rows(MEM_WIDTH), _whole(head_ones.shape)],
        out_specs=[_rows(nb)] + vspecs * 2 + [_rows(MEM_WIDTH), _whole(kv.shape)],
        scratch_shapes=[_view_scratch(gw), _view_scratch(gw)],
        compiler_params=_params(1, 48),
    )(dhb, wg_out, z, *os_, *ls_, qm, kv, mo, head_ones)


def _attn_bwd(q, k, v, lse, do, dl, tabs, d, token):
    ln, dw = q.shape
    w = dw // d
    nb = ln // BLOCK
    nblk = d * nb
    reps = w // LANES
    cur, prev = _stream_maps(d, nb)
    qmap = lambda n: cur(jnp.minimum(n, nblk - 1))
    pmap = lambda n: prev(jnp.minimum(n, nblk - 1))
    omap = lambda n: cur(jnp.maximum(n - 1, 0))

    def body(_, q_ref, kp_ref, kc_ref, vp_ref, vc_ref, l_ref, do_ref, dl_ref, cq, saq, sbq, ck, sak, sbk,
             dq_ref, dk_ref, dv_ref, acck, accv, dqs):
        n = pl.program_id(0)

        @pl.when(n == 0)
        def _():
            acck[...] = jnp.zeros_like(acck)
            accv[...] = jnp.zeros_like(accv)

        def run(with_prev):
            if with_prev:
                kk = jnp.concatenate([kp_ref[...], kc_ref[...]], axis=0)
                vv = jnp.concatenate([vp_ref[...], vc_ref[...]], axis=0)
                rows = slice(0, 2 * BLOCK)
            else:
                kk, vv = kc_ref[...], vc_ref[...]
                rows = slice(BLOCK, 2 * BLOCK)
            valid = _band_mask(kk.shape[0])
            for h0 in range(0, HEADS_PER_GROUP, HEAD_BATCH):
                hs = range(h0, h0 + HEAD_BATCH)
                sls = [slice(h * HEAD_DIM, (h + 1) * HEAD_DIM) for h in hs]
                cols = [slice(h * HEAD_DIM, h * HEAD_DIM + 1) for h in hs]
                qhs = [q_ref[:, sl] for sl in sls]
                dobs = [do_ref[:, sl] for sl in sls]
                scs = [jnp.where(valid, _dot_nt(qh, kk[:, sl]), NEG) for qh, sl in zip(qhs, sls)]
                dps = [_dot_nt(dob, vv[:, sl]) for dob, sl in zip(dobs, sls)]
                ps = [jnp.exp(sc - l_ref[:, col]) for sc, col in zip(scs, cols)]
                dss = [(p * (dp - dl_ref[:, col])).astype(BF16) for p, dp, col in zip(ps, dps, cols)]
                pbs = [p.astype(BF16) for p in ps]
                for sl, qh, dob, ds, pb in zip(sls, qhs, dobs, dss, pbs):
                    dqs[:, sl] = _dot(ds, kk[:, sl]) * SCALE
                    acck[rows, sl] += _dot_tn(ds, qh)
                    accv[rows, sl] += _dot_tn(pb, dob)
            tq = [jnp.tile(r[...], (1, reps)) for r in (cq, saq, sbq)]
            dq_ref[...] = _rope_bwd(dqs[...], *tq).astype(BF16)

        pl.when(n < nblk)(lambda: _per_stream_block(n, nb, run))

        tk = [jnp.tile(r[...], (1, reps)) for r in (ck, sak, sbk)]
        dk_ref[...] = _rope_bwd(acck[0:BLOCK, :], *tk).astype(BF16)
        dv_ref[...] = accv[0:BLOCK, :].astype(BF16)
        acck[0:BLOCK, :] = acck[BLOCK:, :]
        accv[0:BLOCK, :] = accv[BLOCK:, :]
        acck[BLOCK:, :] = jnp.zeros((BLOCK, w), F32)
        accv[BLOCK:, :] = jnp.zeros((BLOCK, w), F32)

    blk = lambda f: pl.BlockSpec((BLOCK, w), f)
    tblk = lambda f: pl.BlockSpec((BLOCK, LANES), f)
    return _pallas_call(
        body, name=f"attn_bwd_d{d}", grid=(nblk + 1,),
        out_shape=[_sds((ln, dw), BF16)] * 3,
        in_specs=[ANY, blk(qmap), blk(pmap), blk(qmap), blk(pmap), blk(qmap), blk(qmap), blk(qmap), blk(qmap)]
                 + [tblk(qmap)] * 3 + [tblk(omap)] * 3,
        out_specs=[blk(qmap), blk(omap), blk(omap)],
        scratch_shapes=[pltpu.VMEM((2 * BLOCK, w), F32), pltpu.VMEM((2 * BLOCK, w), F32), pltpu.VMEM((BLOCK, w), F32)],
        compiler_params=_params(1, 32),
    )(token, q, k, k, v, v, lse, do, dl, *tabs, *tabs)


def _position():
    return lax.axis_index("x"), lax.axis_index("y"), lax.axis_index("c")


def _all_gather(shards, name):
    n_a = len(shards)

    def body(*refs):
        x_refs, out_refs = refs[:n_a], refs[n_a:2 * n_a]
        send_sems, recv_sems, local_sems = refs[2 * n_a:]
        x, y, c = _position()
        me, sibling = (x, y, c), (x, y, 1 - c)
        chips = [(1 - x, y), (x, 1 - y), (1 - x, 1 - y)]

        def rows(a, px, py, pc):
            return out_refs[a].at[4 * px + 2 * py + pc]

        def copy(a, k, block, to, own=False):
            return pltpu.make_async_remote_copy(
                src_ref=x_refs[a] if own else rows(a, *block), dst_ref=rows(a, *block),
                send_sem=send_sems.at[a, k], recv_sem=recv_sems.at[a, k], device_id=to, device_id_type=MESH)

        mine = [pltpu.make_async_copy(x_refs[a], rows(a, *me), local_sems.at[a]) for a in range(n_a)]
        for cp in mine:
            cp.start()
        first = []
        for j, chip in enumerate(chips):
            first += [copy(a, 1 + j, me, (*chip, c), own=True) for a in range(n_a)]
        first += [copy(a, 0, me, sibling, own=True) for a in range(n_a)]
        for cp in first:
            cp.start()
        passed = []
        for j, chip in enumerate(chips):
            for a in range(n_a):
                copy(a, 1 + j, (*chip, c), me).wait_recv()
                fwd = copy(a, 4 + j, (*chip, c), sibling)
                fwd.start()
                passed.append(fwd)
        for a in range(n_a):
            copy(a, 0, sibling, me).wait_recv()
        for j, chip in enumerate(chips):
            for a in range(n_a):
                copy(a, 4 + j, (*chip, 1 - c), me).wait_recv()
        for cp in first + passed:
            cp.wait_send()
        for cp in mine:
            cp.wait()

    return _pallas_call(
        body, name=name, out_shape=[_sds((N_DEV,) + t.shape, t.dtype) for t in shards],
        in_specs=[ANY] * n_a, out_specs=[ANY] * n_a,
        scratch_shapes=[pltpu.SemaphoreType.DMA((n_a, 7)), pltpu.SemaphoreType.DMA((n_a, 7)),
                        pltpu.SemaphoreType.DMA((n_a,))],
    )(*shards)


def _rs_to_sibling(gs):
    n_a = len(gs)

    def body(*refs):
        g_refs, recv_refs = refs[:n_a], refs[n_a:2 * n_a]
        send_sems, recv_sems = refs[2 * n_a:]
        x, y, c = _position()
        copies = []
        for k in range(4):
            for a in range(n_a):
                copies.append(pltpu.make_async_remote_copy(
                    src_ref=g_refs[a].at[2 * k + (1 - c)], dst_ref=recv_refs[a].at[k],
                    send_sem=send_sems.at[a, k], recv_sem=recv_sems.at[a, k],
                    device_id=(x, y, 1 - c), device_id_type=MESH))
        for cp in copies:
            cp.start()
        for cp in copies:
            cp.wait()

    return _pallas_call(
        body, name="rs_to_sibling", out_shape=[_sds((4,) + g.shape[1:], g.dtype) for g in gs],
        in_specs=[ANY] * n_a, out_specs=[ANY] * n_a,
        scratch_shapes=[pltpu.SemaphoreType.DMA((n_a, 4)), pltpu.SemaphoreType.DMA((n_a, 4))],
    )(*gs)


def _rs_to_chips(pbs):
    n_a = len(pbs)

    def body(*refs):
        p_refs, recv_refs = refs[:n_a], refs[n_a:2 * n_a]
        send_sems, recv_sems = refs[2 * n_a:]
        x, y, c = _position()
        chips = [(1 - x, y), (x, 1 - y), (1 - x, 1 - y)]
        copies = []
        for j, (px, py) in enumerate(chips):
            for a in range(n_a):
                copies.append(pltpu.make_async_remote_copy(
                    src_ref=p_refs[a].at[2 * px + py], dst_ref=recv_refs[a].at[j],
                    send_sem=send_sems.at[a, j], recv_sem=recv_sems.at[a, j],
                    device_id=(px, py, c), device_id_type=MESH))
        for cp in copies:
            cp.start()
        for cp in copies:
            cp.wait()

    return _pallas_call(
        body, name="rs_to_chips", out_shape=[_sds((3,) + p.shape[1:], p.dtype) for p in pbs],
        in_specs=[ANY] * n_a, out_specs=[ANY] * n_a,
        scratch_shapes=[pltpu.SemaphoreType.DMA((n_a, 3)), pltpu.SemaphoreType.DMA((n_a, 3))],
    )(*pbs)


HBM_SPEC = pl.BlockSpec(memory_space=pltpu.HBM)
SEM_SPEC = pl.BlockSpec(memory_space=pltpu.SEMAPHORE)
EFFECT = pltpu.SideEffectType.DATAFLOW_SIDE_EFFECTING
def _plan_gather_own(src_refs, land_refs):
    x, y, c = _position()
    me = 4 * x + 2 * y + c
    peers = [(x, y, 1 - c), (1 - x, y, c), (x, 1 - y, c), (1 - x, 1 - y, c)]
    return [(src_refs[a], land_refs[a].at[me], (a, k), peer) for k, peer in enumerate(peers) for a in range(len(src_refs))]


def _plan_gather_pass(src_refs, land_refs):
    x, y, c = _position()
    chips = [(1 - x, y), (x, 1 - y), (1 - x, 1 - y)]
    return [(land_refs[a].at[4 * px + 2 * py + c], land_refs[a].at[4 * px + 2 * py + c], (a, j), (x, y, 1 - c))
            for j, (px, py) in enumerate(chips) for a in range(len(land_refs))]


def _plan_to_sibling(src_refs, land_refs):
    x, y, c = _position()
    return [(src_refs[a].at[2 * k + (1 - c)], land_refs[a].at[k], (a, k), (x, y, 1 - c))
            for k in range(4) for a in range(len(src_refs))]


def _plan_to_chips(src_refs, land_refs):
    x, y, c = _position()
    chips = [(1 - x, y), (x, 1 - y), (1 - x, 1 - y)]
    return [(src_refs[a].at[2 * px + py], land_refs[a].at[j], (a, j), (px, py, c))
            for j, (px, py) in enumerate(chips) for a in range(len(src_refs))]


def _split_start(srcs, lands, plan, n_sem, after, name):
    n_s, n_a = len(srcs), len(lands)
    n_b = n_s + n_a

    def body(*refs):
        src_refs, land_refs = refs[:n_s], refs[n_s:n_b]
        send_sems, recv_sems, token = refs[n_b + 1], refs[n_b + 2], refs[-1]
        for src, dst, (a, k), dev in plan(src_refs, land_refs):
            i = a * n_sem + k
            pltpu.make_async_remote_copy(src_ref=src, dst_ref=dst, send_sem=send_sems.at[i], recv_sem=recv_sems.at[i],
                                         device_id=dev, device_id_type=MESH).start()
        token[...] = jnp.zeros_like(token)

    bufs = list(srcs) + list(lands)
    res = pl.pallas_call(
        body, name=name,
        out_shape=(pltpu.SemaphoreType.DMA((n_a * n_sem,)), pltpu.SemaphoreType.DMA((n_a * n_sem,)),
                   *[pltpu.HBM(t.shape, t.dtype) for t in bufs], _plain((8, LANES), F32)),
        in_specs=[HBM_SPEC] * n_b + [ANY],
        out_specs=(SEM_SPEC, SEM_SPEC, *[HBM_SPEC] * n_b, pl.BlockSpec(memory_space=pltpu.VMEM)),
        input_output_aliases={i: 2 + i for i in range(n_b)},
        compiler_params=pltpu.CompilerParams(has_side_effects=EFFECT),
    )(*[pltpu.with_memory_space_constraint(t, pltpu.HBM) for t in bufs], after)
    return (res[0], res[1], res[2:2 + n_s], res[2 + n_s:2 + n_b]), res[-1]


def _split_wait(started, plan, after, name):
    send_sems, recv_sems, srcs, lands = started
    n_s, n_a = len(srcs), len(lands)
    n_b = n_s + n_a
    n_sem = send_sems.shape[0] // n_a

    def body(*refs):
        src_refs, land_refs = refs[:n_s], refs[n_s:n_b]
        s_sems, r_sems = refs[n_b], refs[n_b + 1]
        for src, dst, (a, k), dev in plan(src_refs, land_refs):
            i = a * n_sem + k
            cp = pltpu.make_async_remote_copy(src_ref=src, dst_ref=dst, send_sem=s_sems.at[i], recv_sem=r_sems.at[i],
                                              device_id=dev, device_id_type=MESH)
            cp.wait_send()
            cp.wait_recv()

    bufs = list(srcs) + list(lands)
    res = pl.pallas_call(
        body, name=name, out_shape=tuple(pltpu.HBM(t.shape, t.dtype) for t in bufs),
        in_specs=[HBM_SPEC] * n_b + [SEM_SPEC, SEM_SPEC, ANY],
        out_specs=tuple([HBM_SPEC] * n_b),
        input_output_aliases={i: i for i in range(n_b)},
        compiler_params=pltpu.CompilerParams(has_side_effects=EFFECT),
    )(*bufs, send_sems, recv_sems, after)
    return res[:n_s], res[n_s:]


def _row_tile(r):
    return ROW_TILE if r % ROW_TILE == 0 else r


def _rs_add_sibling(gp, recv, c_arr, name):
    _, r, l = gp.shape
    tr = r if r <= 4 * ROW_TILE else _row_tile(r)

    def body(c_ref, g_ref, r_ref, pf_ref, pb_ref):
        sm = g_ref[...].astype(F32) + r_ref[...].astype(F32)
        pf_ref[...] = sm
        pb_ref[...] = sm.astype(BF16)

    spec = pl.BlockSpec((None, tr, l), lambda k, i, c: (k, i, 0))
    return _pallas_call(
        body, name=name,
        grid_spec=pltpu.PrefetchScalarGridSpec(
            num_scalar_prefetch=1, grid=(4, r // tr),
            in_specs=[pl.BlockSpec((None, tr, l), lambda k, i, c: (2 * k + c[0], i, 0)), spec],
            out_specs=[spec, spec]),
        out_shape=[_sds((4, r, l), F32), _sds((4, r, l), BF16)], compiler_params=_params(2, 32),
    )(c_arr, gp, recv)


def _adam_update(w, gv, m, v):
    nm = ADAM_B1 * m + (1.0 - ADAM_B1) * gv
    nv = ADAM_B2 * v + (1.0 - ADAM_B2) * (gv * gv)
    m_hat = nm / (1.0 - ADAM_B1 ** ADAM_STEP)
    v_hat = nv / (1.0 - ADAM_B2 ** ADAM_STEP)
    return -ADAM_LR * (m_hat / (jnp.sqrt(v_hat) + ADAM_EPS) + ADAM_WD * w), nm, nv


def _rs_finish_adamw(pf, recv, k_arr, w, m, v, name):
    _, r, l = pf.shape
    tr = _row_tile(r)

    def body(k_ref, p_ref, r_ref, w_ref, m_ref, v_ref, g_ref, d_ref, nm_ref, nv_ref):
        gv = ((p_ref[...] + r_ref[0].astype(F32)) + r_ref[1].astype(F32)) + r_ref[2].astype(F32)
        g_ref[...] = gv
        d_ref[...], nm_ref[...], nv_ref[...] = _adam_update(w_ref[...], gv, m_ref[...], v_ref[...])

    spec = pl.BlockSpec((tr, l), lambda i, k: (i, 0))
    return _pallas_call(
        body, name=name,
        grid_spec=pltpu.PrefetchScalarGridSpec(
            num_scalar_prefetch=1, grid=(r // tr,),
            in_specs=[pl.BlockSpec((None, tr, l), lambda i, k: (k[0], i, 0)),
                      pl.BlockSpec((3, tr, l), lambda i, k: (0, i, 0)), spec, spec, spec],
            out_specs=[spec] * 4),
        out_shape=[_plain((r, l), F32)] * 4, compiler_params=_params(1, 32),
    )(k_arr, pf, recv, w, m, v)


def _rs_finish_adamw_t(pf, recv, k_arr, w_t, m_t, v_t, name):
    _, r, c = pf.shape
    tr = _row_tile(r)
    cp = -(-c // LANES) * LANES

    def body(k_ref, p_ref, r_ref, w_ref, m_ref, v_ref, g_ref, d_ref, nm_ref, nv_ref, pad):
        gv = ((p_ref[...] + r_ref[0].astype(F32)) + r_ref[1].astype(F32)) + r_ref[2].astype(F32)
        pad[...] = jnp.zeros_like(pad)
        pad[:, 0:c] = gv
        gt = pad[...].T[0:c, :]
        g_ref[...] = gt
        d_ref[...], nm_ref[...], nv_ref[...] = _adam_update(w_ref[...], gt, m_ref[...], v_ref[...])

    spec = pl.BlockSpec((c, tr), lambda i, k: (0, i))
    return _pallas_call(
        body, name=name,
        grid_spec=pltpu.PrefetchScalarGridSpec(
            num_scalar_prefetch=1, grid=(r // tr,),
            in_specs=[pl.BlockSpec((None, tr, c), lambda i, k: (k[0], i, 0)),
                      pl.BlockSpec((3, tr, c), lambda i, k: (0, i, 0)), spec, spec, spec],
            out_specs=[spec] * 4, scratch_shapes=[pltpu.VMEM((tr, cp), F32)]),
        out_shape=[_plain((c, r), F32)] * 4, compiler_params=_params(1, 32),
    )(k_arr, pf, recv, w_t, m_t, v_t)


def _sum_devices(g):
    def body(g_ref, o_ref):
        acc = g_ref[0]
        for j in range(1, N_DEV):
            acc = acc + g_ref[j]
        o_ref[...] = acc

    return _pallas_call(body, name="sum_devices", out_shape=_plain(g.shape[1:], F32))(g)


def _adamw(w, g, m, v, name):
    shape = w.shape
    w2, g2, m2, v2 = [t.reshape((-1, shape[-1])) for t in (w, g, m, v)]

    def body(w_ref, g_ref, m_ref, v_ref, d_ref, nm_ref, nv_ref):
        d_ref[...], nm_ref[...], nv_ref[...] = _adam_update(w_ref[...], g_ref[...], m_ref[...], v_ref[...])

    outs = _pallas_call(body, name=name, out_shape=[_plain(w2.shape, F32)] * 3)(w2, g2, m2, v2)
    return tuple(t.reshape(shape) for t in outs)


def _after(t, token):
    return t + token[0:1, 0:1].astype(t.dtype)


def _finish(name, pf, recv, k_arr, w, m, v):
    if name in ("attn_w_in", "conv_w_in"):
        res = _rs_finish_adamw_t(pf, recv, k_arr, w.T, m.T, v.T, "rs_finish_adamw_" + name)
        return tuple(t.T for t in res)
    return _rs_finish_adamw(pf, recv, k_arr, w, m, v, "rs_finish_adamw_" + name)


def kernel(x, mem, positions, norm_g, mem_norm_g, w_mem_kv, attn_w_in, attn_w_out, conv_w_in, conv_w, conv_w_out, final_g, loss_target, m_norm_g, m_mem_norm_g, m_w_mem_kv, m_attn_w_in, m_attn_w_out, m_conv_w_in, m_conv_w, m_conv_w_out, m_final_g, v_norm_g, v_mem_norm_g, v_w_mem_kv, v_attn_w_in, v_attn_w_out, v_conv_w_in, v_conv_w, v_conv_w_out, v_final_g):
    px, py, pc = _position()
    me = 4 * px + 2 * py + pc
    c_arr = jnp.reshape(pc, (1,)).astype(jnp.int32)
    k_arr = jnp.reshape(2 * px + py, (1,)).astype(jnp.int32)
    x, mem, pos, tgt = x[0], mem[0], positions[0], loss_target[0]

    wg_in0 = _all_gather([attn_w_in[0].astype(BF16)], "gather_w_in0")[0]
    late = [attn_w_out[0].astype(BF16), conv_w_in[0].astype(BF16), conv_w_out[0].astype(BF16),
            w_mem_kv[0].astype(BF16), w_mem_kv[1].astype(BF16), jnp.pad(conv_w[0], ((0, 5), (0, 0)))]
    lands = [lax.dynamic_update_slice(lax.empty((N_DEV,) + t.shape, t.dtype), t[None], (me, 0, 0)) for t in late]
    late_weights, late_token = _split_start(late, lands, _plan_gather_own, 4, wg_in0, "gather_late_start")

    tabs = _rope_tables(pos)
    g0, g1 = _after(norm_g[0:1], late_token), norm_g[1:2]

    hn0, hn0_t, qs, ks, vs, tabs_v, qm0, z0 = _inproj_attn(x, g0, wg_in0, tabs)
    os_, ls_ = [], []
    for j, d in enumerate(DILATIONS):
        if j == 1:
            _, lands = _split_wait(late_weights, _plan_gather_own, ls_[0], "gather_late_wait")
            late_weights, late_token = _split_start([], lands, _plan_gather_pass, 3, ls_[0], "gather_late_pass_start")
        o, l = _attn_fwd(qs[j], ks[j], vs[j], d, late_token)
        os_.append(o)
        ls_.append(l)

    _, gathered = _split_wait(late_weights, _plan_gather_pass, ls_[2], "gather_late_pass_wait")
    wg_out0, wg_in1, wg_out1, wg_kv0, wg_kv1, cw_all = gathered
    w_out1 = wg_out1.reshape(-1, wg_out1.shape[2])
    w_kv = jnp.stack([wg_kv0.reshape(-1, wg_kv0.shape[2]), wg_kv1.reshape(-1, wg_kv1.shape[2])])
    cw = cw_all[:, 0:3].transpose(1, 0, 2).reshape(3, -1)
    kv = _memkv_fwd(mem, mem_norm_g, w_kv)
    y0, y0_t, mo0, h1 = _post_attn(os_, ls_, qm0, kv[0], z0, x, wg_out0)

    hn1, hn1_t, bg, cg, u, qm1, z1 = _inproj_conv(h1, g1, wg_in1)
    y1, y1_t, mo1, dh2, dh2b, loss_acc, d_final_g = _post_conv_loss(
        bg, cg, u, qm1, kv[1], z1, h1, w_out1, cw, final_g.reshape(1, -1), tgt)

    d_w_out1 = _wgrad_rows(y1_t, dh2b, "wgrad_out1")
    dz1, dbg, dconv, dqm1, dkv1 = _bwd_post_conv(dh2b, w_out1, bg, cg, u, z1, qm1, kv[1], mo1, cw)
    dcg, du, dcw = _bwd_conv(dconv, cg, u, cw)
    dh1, dh1b, dg1, dproj1 = _dgrad_norm([(dbg, 1), (dcg, 1), (du, 1), (dqm1, 1), (dz1, 1)], wg_in1, h1, g1, dh2,
                                         "dgrad_norm_conv")
    d_w_in1 = _wgrad_shards(hn1_t, dproj1, "wgrad_in1")

    d_w_out0 = _wgrad_cols(y0_t, dh1b, wg_out0.shape[2], "wgrad_out0")

    names1 = ["conv_w_in", "conv_w_out", "attn_w_out"]
    grads1 = [d_w_in1, d_w_out1, d_w_out0]
    started, token = _split_start(grads1, [lax.empty((4,) + g.shape[1:], g.dtype) for g in grads1],
                                  _plan_to_sibling, 4, dg1, "rs1_sibling_start")

    gw = GROUP_WIDTH
    ones = (jnp.arange(gw)[:, None] // HEAD_DIM == jnp.arange(gw)[None, :] // HEAD_DIM).astype(BF16)
    ones = _after(ones, token)
    res = _bwd_post_attn(dh1b, wg_out0, z0, os_, ls_, qm0, kv[0], mo0, ones)
    dz0, dos, dls, dqm0, dkv0 = res[0], res[1:4], res[4:7], res[7], res[8]

    grads1, from_sibling = _split_wait(started, _plan_to_sibling, dz0, "rs1_sibling_wait")
    parts1 = [_rs_add_sibling(g, r, c_arr, "rs_add_sibling_" + n) for g, r, n in zip(grads1, from_sibling, names1)]
    pbs1 = [pb for _, pb in parts1]
    started, token = _split_start(pbs1, [lax.empty((3,) + p.shape[1:], p.dtype) for p in pbs1],
                                  _plan_to_chips, 3, dg1, "rs1_chips_start")

    dqs, dks, dvs = [], [], []
    for j, d in enumerate(DILATIONS):
        dq, dk, dv = _attn_bwd(qs[j], ks[j], vs[j], ls_[j], dos[j], dls[j], tabs_v[j], d, token)
        dqs.append((dq, d))
        dks.append((dk, d))
        dvs.append((dv, d))
    d_w_kv, d_mem_g = _memkv_bwd(jnp.stack([dkv0, dkv1]), w_kv, mem, mem_norm_g)
    n_kv = d_w_kv.shape[1] // N_DEV
    dproj0 = _assemble_dproj(dqs + dks + dvs + [(dqm0, 1), (dz0, 1)], wg_in0.shape[2], "assemble_dproj_attn")
    d_w_in0 = _wgrad_shards(hn0_t, dproj0, "wgrad_in0")

    names0 = ["attn_w_in", "w_mem_kv0", "w_mem_kv1"]
    grads0 = [d_w_in0, d_w_kv[0].reshape(N_DEV, n_kv, -1), d_w_kv[1].reshape(N_DEV, n_kv, -1)]
    started0, token0 = _split_start(grads0, [lax.empty((4,) + g.shape[1:], g.dtype) for g in grads0],
                                    _plan_to_sibling, 4, dg1, "rs0_sibling_start")
    _, from_chips1 = _split_wait(started, _plan_to_chips, token0, "rs1_chips_wait")
    shard = dict(attn_w_in=(attn_w_in[0], m_attn_w_in[0], v_attn_w_in[0]),
                 attn_w_out=(attn_w_out[0], m_attn_w_out[0], v_attn_w_out[0]),
                 conv_w_in=(conv_w_in[0], m_conv_w_in[0], v_conv_w_in[0]),
                 conv_w_out=(conv_w_out[0], m_conv_w_out[0], v_conv_w_out[0]),
                 w_mem_kv0=(w_mem_kv[0], m_w_mem_kv[0], v_w_mem_kv[0]), w_mem_kv1=(w_mem_kv[1], m_w_mem_kv[1], v_w_mem_kv[1]))
    big = {}
    for n, (pf, _), r in zip(names1, parts1, from_chips1):
        big[n] = _finish(n, pf, r, k_arr, *shard[n])

    grads0, from_sibling = _split_wait(started0, _plan_to_sibling, big["conv_w_out"][1], "rs0_sibling_wait")
    parts0 = [_rs_add_sibling(g, r, c_arr, "rs_add_sibling_" + n) for g, r, n in zip(grads0, from_sibling, names0)]
    pbs0 = [pb for _, pb in parts0]
    started0, token0 = _split_start(pbs0, [lax.empty((3,) + p.shape[1:], p.dtype) for p in pbs0],
                                    _plan_to_chips, 3, dg1, "rs0_chips_start")
    dx, dg0 = _dgrad_norm_dm(dproj0, wg_in0, x, g0, dh1, token0, "dgrad_norm_attn")

    small_part = jnp.concatenate([dg0, dg1, d_mem_g.reshape(2, -1), d_final_g, dcw[0:3]], axis=0)
    small_part = jnp.concatenate([small_part, jnp.broadcast_to(loss_acc[0, 0], small_part.shape)], axis=0)
    small = _sum_devices(_all_gather([small_part], "gather_small_grads")[0])
    loss = small[8, 0]
    g_conv_w = lax.dynamic_slice(small[5:8], (0, me * LANES), (3, LANES))[None]
    small_g = dict(norm_g=small[0:2], mem_norm_g=small[2:4], conv_w=g_conv_w, final_g=small[4])
    small_w = dict(norm_g=(norm_g, m_norm_g, v_norm_g), mem_norm_g=(mem_norm_g, m_mem_norm_g, v_mem_norm_g),
                   conv_w=(conv_w, m_conv_w, v_conv_w), final_g=(final_g, m_final_g, v_final_g))
    for n, (w, m, v) in small_w.items():
        big[n] = (small_g[n],) + _adamw(w, small_g[n], m, v, "adamw_" + n)

    _, from_chips0 = _split_wait(started0, _plan_to_chips, big["final_g"][1], "rs0_chips_wait")
    for n, (pf, _), r in zip(names0, parts0, from_chips0):
        big[n] = _finish(n, pf, r, k_arr, *shard[n])
    for n in ("attn_w_in", "attn_w_out", "conv_w_in", "conv_w_out"):
        big[n] = tuple(t[None] for t in big[n])
    big["w_mem_kv"] = tuple(jnp.stack([a, b]) for a, b in zip(big["w_mem_kv0"], big["w_mem_kv1"]))

    order = ["norm_g", "mem_norm_g", "w_mem_kv", "attn_w_in", "attn_w_out", "conv_w_in", "conv_w", "conv_w_out", "final_g"]
    return (loss, dx[None], *[big[n][0] for n in order], *[big[n][1] for n in order],
            *[big[n][2] for n in order], *[big[n][3] for n in order])
```

```python
import functools

import jax
import jax.numpy as jnp
from jax import lax
from jax.experimental import pallas as pl
from jax.experimental.pallas import tpu as pltpu

F32 = jnp.float32
BF16 = jnp.bfloat16

N_DEV = 8
D_MODEL = 1024
HEAD_DIM = 64
ROT_DIM = HEAD_DIM // 4
ROPE_THETA = 500000.0
DILATIONS = (1, 4, 16)
HEADS_PER_GROUP = 8
HEAD_BATCH = 8
GROUP_WIDTH = HEADS_PER_GROUP * HEAD_DIM
BLOCK = 128
N_MEM = 256
MEM_HEADS = 4
MEM_WIDTH = MEM_HEADS * HEAD_DIM
CONV_WIDTH = D_MODEL
EPS = 1e-6
SCALE = HEAD_DIM ** -0.5
NEG = -1e30

ADAM_LR = 0.001
ADAM_B1 = 0.9
ADAM_B2 = 0.999
ADAM_EPS = 1e-08
ADAM_WD = 0.01
ADAM_STEP = 10

ROW_TILE = 256
LANES = 128
MESH = pl.DeviceIdType.MESH
ANY = pl.BlockSpec(memory_space=pl.ANY)


def _pallas_call(body, **kw):
    call = pl.pallas_call(body, **kw)

    def run(*args):
        pinned = [pltpu.with_memory_space_constraint(a, pltpu.HBM) if jnp.issubdtype(a.dtype, jnp.floating) else a
                  for a in args]
        return call(*pinned)

    return run


def _dot(a, b):
    return lax.dot_general(a, b, (((1,), (0,)), ((), ())), preferred_element_type=F32)


def _dot_nt(a, b):
    return lax.dot_general(a, b, (((1,), (1,)), ((), ())), preferred_element_type=F32)


def _dot_tn(a, b):
    return lax.dot_general(a, b, (((0,), (0,)), ((), ())), preferred_element_type=F32)


def _params(n_grid, vmem_mb=48):
    return pltpu.CompilerParams(dimension_semantics=("arbitrary",) * n_grid, vmem_limit_bytes=vmem_mb << 20)


def _rows(width, tm=ROW_TILE):
    return pl.BlockSpec((tm, width), lambda i: (i, 0))


def _view_rows(width, d, tm=ROW_TILE):
    return pl.BlockSpec((tm // d, d * width), lambda i: (i, 0))


def _whole(shape):
    return pl.BlockSpec(shape, lambda *_: (0,) * len(shape))


def _resident(shape):
    return pl.BlockSpec(shape, lambda *_: (0,) * len(shape), pipeline_mode=pl.Buffered(1))


def _sds(shape, dtype):
    return pltpu.HBM(shape, dtype)


def _plain(shape, dtype):
    return jax.ShapeDtypeStruct(shape, dtype)


def _silu_parts(z):
    sg = jax.nn.sigmoid(z)
    return z * sg, sg * (1.0 + z * (1.0 - sg))


def _to_view(scr, val, out_ref, d):
    tm, w = val.shape
    if d == 1:
        out_ref[...] = val.astype(out_ref.dtype)
        return
    for cb in range(w // LANES):
        scr[cb] = val[:, cb * LANES:(cb + 1) * LANES]
    for r in range(d):
        for cb in range(w // LANES):
            lo = r * w + cb * LANES
            out_ref[:, lo:lo + LANES] = scr[cb, pl.ds(r, tm // d, stride=d), :].astype(out_ref.dtype)


def _from_view(scr, in_ref, d):
    if d == 1:
        return in_ref[...].astype(F32)
    nc, tm, _ = scr.shape
    w = nc * LANES
    for r in range(d):
        for cb in range(nc):
            lo = r * w + cb * LANES
            scr[cb, pl.ds(r, tm // d, stride=d), :] = in_ref[:, lo:lo + LANES].astype(F32)
    return jnp.concatenate([scr[cb] for cb in range(nc)], axis=1)


def _view_scratch(width, tm=ROW_TILE):
    return pltpu.VMEM((width // LANES, tm, LANES), F32)


def _rope_tables(pos):
    half = ROT_DIM // 2
    inv_freq = ROPE_THETA ** (-jnp.arange(half, dtype=F32) * (2.0 / ROT_DIM))
    ang = pos.astype(F32)[:, None] * inv_freq
    cos, sin = jnp.cos(ang), jnp.sin(ang)
    s = pos.shape[0]
    z8 = jnp.zeros((s, half), F32)
    rest = HEAD_DIM - ROT_DIM
    cosf = jnp.concatenate([cos, cos, jnp.ones((s, rest), F32)], axis=1)
    sa = jnp.concatenate([-sin, z8, jnp.zeros((s, rest), F32)], axis=1)
    sb = jnp.concatenate([z8, sin, jnp.zeros((s, rest), F32)], axis=1)
    return tuple(jnp.tile(t, (1, LANES // HEAD_DIM)) for t in (cosf, sa, sb))


def _rope_fwd(t, cv, sav, sbv):
    w = t.shape[1]
    return t * cv + pltpu.roll(t, w - ROT_DIM // 2, 1) * sav + pltpu.roll(t, ROT_DIM // 2, 1) * sbv


def _rope_bwd(g, cv, sav, sbv):
    w = g.shape[1]
    return g * cv + pltpu.roll(g * sav, ROT_DIM // 2, 1) + pltpu.roll(g * sbv, w - ROT_DIM // 2, 1)


def _project(hn, wg_ref, proj_scr):
    c = wg_ref.shape[2]
    for j in range(N_DEV):
        proj_scr[:, j * c:(j + 1) * c] = _dot(hn, wg_ref[j])


def _inproj_attn(x, g, wg, tabs):
    s, d_model = x.shape
    gw = GROUP_WIDTH
    n = N_DEV * wg.shape[2]
    nz = n - 9 * gw - MEM_WIDTH
    reps = gw // LANES
    tm = ROW_TILE

    def body(x_ref, g_ref, w_ref, c_ref, sa_ref, sb_ref, hn_ref, hnt_ref, *rest):
        outs, (proj, scr, tscr) = rest[:-3], rest[-3:]
        q_refs, k_refs, v_refs, t_refs, qm_ref, z_ref = outs[0:3], outs[3:6], outs[6:9], outs[9:18], outs[18], outs[19]
        xb = x_ref[...]
        r = lax.rsqrt(jnp.mean(xb * xb, axis=-1, keepdims=True) + EPS)
        hn = ((xb * r) * g_ref[...]).astype(BF16)
        hn_ref[...] = hn
        hnt_ref[...] = hn.T
        _project(hn, w_ref, proj)
        tab = (c_ref[...], sa_ref[...], sb_ref[...])
        cv, sav, sbv = [jnp.tile(t, (1, reps)) for t in tab]
        for j, d in enumerate(DILATIONS):
            tq = _rope_fwd(proj[:, j * gw:(j + 1) * gw], cv, sav, sbv)
            _to_view(scr, tq * SCALE, q_refs[j], d)
            tk = _rope_fwd(proj[:, (3 + j) * gw:(4 + j) * gw], cv, sav, sbv)
            _to_view(scr, tk, k_refs[j], d)
            _to_view(scr, proj[:, (6 + j) * gw:(7 + j) * gw], v_refs[j], d)
            for i in range(3):
                _to_view(tscr, tab[i], t_refs[3 * j + i], d)
        qm_ref[...] = proj[:, 9 * gw:9 * gw + MEM_WIDTH].astype(BF16)
        z_ref[...] = proj[:, 9 * gw + MEM_WIDTH:]

    views = [_sds((s // d, d * gw), BF16) for d in DILATIONS]
    tviews = [_sds((s // d, d * LANES), F32) for d in DILATIONS for _ in range(3)]
    out_shape = ([_sds((s, d_model), BF16), _sds((d_model, s), BF16)] + views * 3 + tviews
                 + [_sds((s, MEM_WIDTH), BF16), _sds((s, nz), F32)])
    vspecs = [_view_rows(gw, d, tm) for d in DILATIONS]
    tspecs = [_view_rows(LANES, d, tm) for d in DILATIONS for _ in range(3)]
    out_specs = ([_rows(d_model, tm), pl.BlockSpec((d_model, tm), lambda i: (0, i))] + vspecs * 3 + tspecs
                 + [_rows(MEM_WIDTH, tm), _rows(nz, tm)])
    res = _pallas_call(
        body, name="inproj_attn", grid=(s // tm,), out_shape=out_shape,
        in_specs=[_rows(d_model, tm), _whole((1, d_model)), _resident(wg.shape)] + [_rows(LANES, tm)] * 3,
        out_specs=out_specs,
        scratch_shapes=[pltpu.VMEM((tm, n), F32), _view_scratch(gw, tm), _view_scratch(LANES, tm)],
        compiler_params=_params(1, 60),
    )(x, g, wg, *tabs)
    tabs_v = [res[11 + 3 * j:14 + 3 * j] for j in range(3)]
    return res[0], res[1], res[2:5], res[5:8], res[8:11], tabs_v, res[20], res[21]


def _stream_maps(d, nb):
    def cur(n):
        return (n % nb, n // nb)

    def prev(n):
        return (jnp.maximum(n % nb - 1, 0), n // nb)

    return cur, prev


def _band_mask(n_keys):
    qi = lax.broadcasted_iota(jnp.int32, (BLOCK, n_keys), 0)
    kj = lax.broadcasted_iota(jnp.int32, (BLOCK, n_keys), 1)
    if n_keys == BLOCK:
        return kj <= qi
    return jnp.logical_or(jnp.logical_and(kj < BLOCK, kj >= qi), jnp.logical_and(kj >= BLOCK, (kj - BLOCK) <= qi))


def _per_stream_block(n, nb, run):
    if nb == 1:
        run(False)
        return
    first = (n % nb) == 0
    pl.when(first)(lambda: run(False))
    pl.when(jnp.logical_not(first))(lambda: run(True))


def _attn_fwd(q, k, v, d, token):
    ln, dw = q.shape
    w = dw // d
    nb = ln // BLOCK
    nblk = d * nb
    cur, prev = _stream_maps(d, nb)

    def body(_, q_ref, kp_ref, kc_ref, vp_ref, vc_ref, o_ref, lse_ref):
        def run(with_prev):
            if with_prev:
                kk = jnp.concatenate([kp_ref[...], kc_ref[...]], axis=0)
                vv = jnp.concatenate([vp_ref[...], vc_ref[...]], axis=0)
            else:
                kk, vv = kc_ref[...], vc_ref[...]
            valid = _band_mask(kk.shape[0])
            hb = HEAD_BATCH
            for h0 in range(0, HEADS_PER_GROUP, hb):
                sls = [slice(h * HEAD_DIM, (h + 1) * HEAD_DIM) for h in range(h0, h0 + hb)]
                scs = [jnp.where(valid, _dot_nt(q_ref[:, sl], kk[:, sl]), NEG) for sl in sls]
                ms = [jnp.max(sc, axis=-1, keepdims=True) for sc in scs]
                ps = [jnp.exp(sc - m) for sc, m in zip(scs, ms)]
                ls = [jnp.sum(p, axis=-1, keepdims=True) for p in ps]
                pns = [(p * (1.0 / l)).astype(BF16) for p, l in zip(ps, ls)]
                for sl, pn, m, l in zip(sls, pns, ms, ls):
                    o_ref[:, sl] = _dot(pn, vv[:, sl])
                    lse_ref[:, sl] = jnp.broadcast_to(m + jnp.log(l), (BLOCK, HEAD_DIM))

        _per_stream_block(pl.program_id(0), nb, run)

    blk = lambda f: pl.BlockSpec((BLOCK, w), f)
    return _pallas_call(
        body, name=f"attn_fwd_d{d}", grid=(nblk,),
        out_shape=[_sds((ln, dw), F32)] * 2,
        in_specs=[ANY, blk(cur), blk(prev), blk(cur), blk(prev), blk(cur)],
        out_specs=[blk(cur), blk(cur)], compiler_params=_params(1, 32),
    )(token, q, k, k, v, v)


def _memkv_fwd(mem, g, w):
    n_layers = w.shape[0]

    def body(mem_ref, g_ref, w_ref, kv_ref):
        mb = mem_ref[...]
        r = lax.rsqrt(jnp.mean(mb * mb, axis=-1, keepdims=True) + EPS)
        mn = ((mb * r) * g_ref[...]).astype(BF16)
        kv_ref[...] = _dot(mn, w_ref[...]).astype(BF16)

    return _pallas_call(
        body, name="memkv_fwd", grid=(n_layers,),
        out_shape=_plain((n_layers, N_MEM, 2 * MEM_WIDTH), BF16),
        in_specs=[_whole(mem.shape), pl.BlockSpec((None, 1, D_MODEL), lambda l: (l, 0, 0)),
                  pl.BlockSpec((None, D_MODEL, 2 * MEM_WIDTH), lambda l: (l, 0, 0))],
        out_specs=pl.BlockSpec((None, N_MEM, 2 * MEM_WIDTH), lambda l: (l, 0, 0)),
        compiler_params=_params(1, 32),
    )(mem, g.reshape(n_layers, 1, D_MODEL), w)


def _mix_groups(os_, ls_):
    mx = jnp.maximum(jnp.maximum(ls_[0], ls_[1]), ls_[2])
    es = [jnp.exp(t - mx) for t in ls_]
    inv = 1.0 / (es[0] + es[1] + es[2])
    ws = [e * inv for e in es]
    mix = ws[0] * os_[0] + ws[1] * os_[1] + ws[2] * os_[2]
    return ws, mix


MEM_SLICES = [slice(h * HEAD_DIM, (h + 1) * HEAD_DIM) for h in range(MEM_HEADS)]


def _mem_probs(qm, km):
    scs = [_dot_nt(qm[:, sl], km[:, sl]) * SCALE for sl in MEM_SLICES]
    es = [jnp.exp(sc - jnp.max(sc, axis=-1, keepdims=True)) for sc in scs]
    return [e * (1.0 / jnp.sum(e, axis=-1, keepdims=True)) for e in es]


def _mem_attn_into(qm, kv_ref, mo_ref):
    km, vm = kv_ref[:, :MEM_WIDTH], kv_ref[:, MEM_WIDTH:]
    ps = [p.astype(BF16) for p in _mem_probs(qm, km)]
    for sl, p in zip(MEM_SLICES, ps):
        mo_ref[:, sl] = _dot(p, vm[:, sl])


def _mem_attn_bwd(qm, kv_ref, dmem, dqm_ref, dkv_ref):
    km, vm = kv_ref[:, :MEM_WIDTH], kv_ref[:, MEM_WIDTH:]
    dmb = dmem.astype(BF16)
    ps = _mem_probs(qm, km)
    dps = [_dot_nt(dmb[:, sl], vm[:, sl]) for sl in MEM_SLICES]
    dss = [(p * (dp - jnp.sum(dp * p, axis=-1, keepdims=True)) * SCALE).astype(BF16) for p, dp in zip(ps, dps)]
    pbs = [p.astype(BF16) for p in ps]
    for h, (sl, ds, pb) in enumerate(zip(MEM_SLICES, dss, pbs)):
        slv = slice(MEM_WIDTH + h * HEAD_DIM, MEM_WIDTH + (h + 1) * HEAD_DIM)
        dqm_ref[:, sl] = _dot(ds, km[:, sl]).astype(BF16)
        dkv_ref[:, sl] += _dot_tn(ds, qm[:, sl])
        dkv_ref[:, slv] += _dot_tn(pb, dmb[:, sl])


def _post_attn(os_, ls_, qm, kv, z, x, wg_out):
    s, d_model = x.shape
    gw = GROUP_WIDTH
    nb = gw + MEM_WIDTH
    c = wg_out.shape[2]
    tm = ROW_TILE

    def body(o0, o1, o2, l0, l1, l2, qm_ref, kv_ref, z_ref, x_ref, w_ref, y_ref, yt_ref, mo_ref, h_ref, s0, s1):
        ov, lv = [], []
        for o_ref, l_ref, d in zip((o0, o1, o2), (l0, l1, l2), DILATIONS):
            ov.append(_from_view(s0, o_ref, d))
            lv.append(_from_view(s1, l_ref, d))
        _, mix = _mix_groups(ov, lv)
        _mem_attn_into(qm_ref[...], kv_ref, mo_ref)
        sz, _ = _silu_parts(z_ref[...])
        y_ref[:, :gw] = (mix * sz[:, :gw]).astype(BF16)
        y_ref[:, gw:] = (mo_ref[...] * sz[:, gw:]).astype(BF16)
        y = y_ref[...]
        yt_ref[...] = y.T
        for j in range(N_DEV):
            h_ref[:, j * c:(j + 1) * c] = x_ref[:, j * c:(j + 1) * c] + _dot(y, w_ref[j])

    vspecs = [_view_rows(gw, d) for d in DILATIONS]
    return _pallas_call(
        body, name="post_attn", grid=(s // tm,),
        out_shape=[_sds((s, nb), BF16), _sds((nb, s), BF16), _sds((s, MEM_WIDTH), F32), _sds((s, d_model), F32)],
        in_specs=vspecs * 2 + [_rows(MEM_WIDTH), _whole(kv.shape), _rows(nb), _rows(d_model), _whole(wg_out.shape)],
        out_specs=[_rows(nb), pl.BlockSpec((nb, tm), lambda i: (0, i)), _rows(MEM_WIDTH), _rows(d_model)],
        scratch_shapes=[_view_scratch(gw), _view_scratch(gw)],
        compiler_params=_params(1, 40),
    )(*os_, *ls_, qm, kv, z, x, wg_out)


def _inproj_conv(x, g, wg):
    s, d_model = x.shape
    c = CONV_WIDTH
    n = N_DEV * wg.shape[2]
    nz = n - 3 * c - MEM_WIDTH
    tm = ROW_TILE

    def body(x_ref, g_ref, w_ref, hn_ref, hnt_ref, bg_ref, cg_ref, u_ref, qm_ref, z_ref, proj):
        xb = x_ref[...]
        r = lax.rsqrt(jnp.mean(xb * xb, axis=-1, keepdims=True) + EPS)
        hn = ((xb * r) * g_ref[...]).astype(BF16)
        hn_ref[...] = hn
        hnt_ref[...] = hn.T
        _project(hn, w_ref, proj)
        bg_ref[...] = proj[:, 0:c]
        cg_ref[...] = proj[:, c:2 * c]
        u_ref[...] = proj[:, 2 * c:3 * c]
        qm_ref[...] = proj[:, 3 * c:3 * c + MEM_WIDTH].astype(BF16)
        z_ref[...] = proj[:, 3 * c + MEM_WIDTH:]

    return _pallas_call(
        body, name="inproj_conv", grid=(s // tm,),
        out_shape=[_sds((s, d_model), BF16), _sds((d_model, s), BF16)] + [_sds((s, c), F32)] * 3
                  + [_sds((s, MEM_WIDTH), BF16), _sds((s, nz), F32)],
        in_specs=[_rows(d_model), _whole((1, d_model)), _whole(wg.shape)],
        out_specs=[_rows(d_model), pl.BlockSpec((d_model, tm), lambda i: (0, i))] + [_rows(c)] * 3
                  + [_rows(MEM_WIDTH), _rows(nz)],
        scratch_shapes=[pltpu.VMEM((tm, n), F32)],
        compiler_params=_params(1, 60),
    )(x, g, wg)


HALO = 8


def _halo_before(width, tm=ROW_TILE):
    return pl.BlockSpec((HALO, width), lambda i: (jnp.maximum(i * (tm // HALO) - 1, 0), 0))


def _halo_after(width, n_rows, tm=ROW_TILE):
    return pl.BlockSpec((HALO, width), lambda i: (jnp.minimum((i + 1) * (tm // HALO), n_rows // HALO - 1), 0))


def _conv_taps(cg_ref, u_ref, cgh_ref, uh_ref, i):
    a = cg_ref[...] * u_ref[...]
    ah = jnp.where(i > 0, cgh_ref[...] * uh_ref[...], 0.0)
    row = lax.broadcasted_iota(jnp.int32, a.shape, 0)
    a1 = jnp.where(row == 0, ah[HALO - 1:HALO], pltpu.roll(a, 1, 0))
    a2 = jnp.where(row == 0, ah[HALO - 2:HALO - 1], jnp.where(row == 1, ah[HALO - 1:HALO], pltpu.roll(a, 2, 0)))
    return a, a1, a2


def _post_conv_loss(bg, cg, u, qm, kv, z, h1, w_out, cw, gf, tgt):
    s, d = h1.shape
    c = CONV_WIDTH
    nb = c + MEM_WIDTH
    tm = ROW_TILE

    def body(bg_ref, cg_ref, u_ref, cgh_ref, uh_ref, qm_ref, kv_ref, z_ref, h_ref, w_ref, cw_ref, gf_ref, t_ref,
             y_ref, yt_ref, mo_ref, dh_ref, dhb_ref, loss_ref, dgf_ref):
        i = pl.program_id(0)
        a, a1, a2 = _conv_taps(cg_ref, u_ref, cgh_ref, uh_ref, i)
        conv = cw_ref[0:1, :] * a2 + cw_ref[1:2, :] * a1 + cw_ref[2:3, :] * a
        mix = bg_ref[...] * conv
        _mem_attn_into(qm_ref[...], kv_ref, mo_ref)
        sz, _ = _silu_parts(z_ref[...])
        y_ref[:, :c] = (mix * sz[:, :c]).astype(BF16)
        y_ref[:, c:] = (mo_ref[...] * sz[:, c:]).astype(BF16)
        y = y_ref[...]
        yt_ref[...] = y.T
        h2 = h_ref[...] + _dot(y, w_ref[...])
        r = lax.rsqrt(jnp.mean(h2 * h2, axis=-1, keepdims=True) + EPS)
        nh = h2 * r
        gfv = gf_ref[...]
        diff = nh * gfv - t_ref[...]
        dout = diff * (1.0 / d)
        dn = dout * gfv
        dh2 = r * dn - h2 * ((r * r * r) * jnp.mean(dn * h2, axis=-1, keepdims=True))
        dh_ref[...] = dh2
        dhb_ref[...] = dh2.astype(BF16)

        @pl.when(i == 0)
        def _():
            loss_ref[...] = jnp.zeros_like(loss_ref)
            dgf_ref[...] = jnp.zeros_like(dgf_ref)

        loss_ref[...] += 0.5 * jnp.sum(jnp.mean(diff * diff, axis=-1, keepdims=True))
        dgf_ref[...] += jnp.sum(dout * nh, axis=0, keepdims=True)

    return _pallas_call(
        body, name="post_conv_loss", grid=(s // tm,),
        out_shape=[_sds((s, nb), BF16), _sds((nb, s), BF16), _sds((s, MEM_WIDTH), F32), _sds((s, d), F32),
                   _sds((s, d), BF16), _plain((8, LANES), F32), _plain((1, d), F32)],
        in_specs=[_rows(c)] * 3 + [_halo_before(c)] * 2 + [_rows(MEM_WIDTH), _whole(kv.shape), _rows(nb), _rows(d),
                  _whole(w_out.shape), _whole(cw.shape), _whole((1, d)), _rows(d)],
        out_specs=[_rows(nb), pl.BlockSpec((nb, tm), lambda i: (0, i)), _rows(MEM_WIDTH), _rows(d), _rows(d),
                   _whole((8, LANES)), _whole((1, d))],
        compiler_params=_params(1, 48),
    )(bg, cg, u, cg, u, qm, kv, z, h1, w_out, cw, gf, tgt)


def _bwd_post_conv(dhb, w_out, bg, cg, u, z, qm, kv, mo, cw):
    s = dhb.shape[0]
    c = CONV_WIDTH
    nb = c + MEM_WIDTH

    def body(dh_ref, w_ref, bg_ref, cg_ref, u_ref, cgh_ref, uh_ref, z_ref, qm_ref, kv_ref, mo_ref, cw_ref,
             dz_ref, dbg_ref, dc_ref, dqm_ref, dkv_ref):
        i = pl.program_id(0)

        @pl.when(i == 0)
        def _():
            dkv_ref[...] = jnp.zeros_like(dkv_ref)

        dy = _dot_nt(dh_ref[...], w_ref[...])
        sz, dsz = _silu_parts(z_ref[...])
        a, a1, a2 = _conv_taps(cg_ref, u_ref, cgh_ref, uh_ref, i)
        conv = cw_ref[0:1, :] * a2 + cw_ref[1:2, :] * a1 + cw_ref[2:3, :] * a
        bgv = bg_ref[...]
        dz_ref[:, :c] = (dy[:, :c] * (bgv * conv) * dsz[:, :c]).astype(BF16)
        dz_ref[:, c:] = (dy[:, c:] * mo_ref[...] * dsz[:, c:]).astype(BF16)
        dbr = dy * sz
        dmix = dbr[:, :c]
        dbg_ref[...] = (dmix * conv).astype(BF16)
        dc_ref[...] = dmix * bgv
        _mem_attn_bwd(qm_ref[...], kv_ref, dbr[:, c:], dqm_ref, dkv_ref)

    return _pallas_call(
        body, name="bwd_post_conv", grid=(s // ROW_TILE,),
        out_shape=[_sds((s, nb), BF16), _sds((s, c), BF16), _sds((s, c), F32), _sds((s, MEM_WIDTH), BF16),
                   _plain(kv.shape, F32)],
        in_specs=[_rows(D_MODEL), _whole(w_out.shape)] + [_rows(c)] * 3 + [_halo_before(c)] * 2
                 + [_rows(nb), _rows(MEM_WIDTH), _whole(kv.shape), _rows(MEM_WIDTH), _whole(cw.shape)],
        out_specs=[_rows(nb), _rows(c), _rows(c), _rows(MEM_WIDTH), _whole(kv.shape)],
        compiler_params=_params(1, 48),
    )(dhb, w_out, bg, cg, u, cg, u, z, qm, kv, mo, cw)


def _bwd_conv(dconv, cg, u, cw):
    s, c = dconv.shape
    tm = ROW_TILE
    last = s // tm - 1

    def body(dc_ref, dcn_ref, cg_ref, u_ref, cgh_ref, uh_ref, cw_ref, dcg_ref, du_ref, dcw_ref):
        i = pl.program_id(0)

        @pl.when(i == 0)
        def _():
            dcw_ref[...] = jnp.zeros_like(dcw_ref)

        dc = dc_ref[...]
        dcn = jnp.where(i < last, dcn_ref[...], 0.0)
        row = lax.broadcasted_iota(jnp.int32, dc.shape, 0)
        d1 = jnp.where(row == tm - 1, dcn[0:1], pltpu.roll(dc, tm - 1, 0))
        d2 = jnp.where(row == tm - 1, dcn[1:2], jnp.where(row == tm - 2, dcn[0:1], pltpu.roll(dc, tm - 2, 0)))
        da = cw_ref[2:3, :] * dc + cw_ref[1:2, :] * d1 + cw_ref[0:1, :] * d2
        a, a1, a2 = _conv_taps(cg_ref, u_ref, cgh_ref, uh_ref, i)
        dcg_ref[...] = (da * u_ref[...]).astype(BF16)
        du_ref[...] = (da * cg_ref[...]).astype(BF16)
        dcw_ref[0:1, :] += jnp.sum(dc * a2, axis=0, keepdims=True)
        dcw_ref[1:2, :] += jnp.sum(dc * a1, axis=0, keepdims=True)
        dcw_ref[2:3, :] += jnp.sum(dc * a, axis=0, keepdims=True)

    return _pallas_call(
        body, name="bwd_conv", grid=(s // tm,),
        out_shape=[_sds((s, c), BF16), _sds((s, c), BF16), _plain((8, c), F32)],
        in_specs=[_rows(c), _halo_after(c, s), _rows(c), _rows(c), _halo_before(c), _halo_before(c), _whole(cw.shape)],
        out_specs=[_rows(c), _rows(c), _whole((8, c))], compiler_params=_params(1, 40),
    )(dconv, dconv, cg, u, cg, u, cw)


def _dgrad_norm(pieces, wg, h, g, dres, name):
    s, d_model = h.shape
    c = wg.shape[2]
    n = N_DEV * c
    tm = ROW_TILE
    widths = [p.shape[1] // d for p, d in pieces]
    assert sum(widths) == n
    n_p = len(pieces)

    def body(*refs):
        p_refs = refs[:n_p]
        w_ref, h_ref, g_ref, dr_ref, dh_ref, dhb_ref, dg_ref, dpd_ref, dp, scr = refs[n_p:]

        @pl.when(pl.program_id(0) == 0)
        def _():
            dg_ref[...] = jnp.zeros_like(dg_ref)

        off = 0
        for p_ref, (_, d), wd in zip(p_refs, pieces, widths):
            if d == 1:
                dp[:, off:off + wd] = p_ref[...]
            else:
                dp[:, off:off + wd] = _from_view(scr, p_ref, d).astype(BF16)
            off += wd
        dhn = jnp.zeros((tm, d_model), F32)
        for j in range(N_DEV):
            dpj = dp[:, j * c:(j + 1) * c]
            dpd_ref[j] = dpj
            dhn += _dot_nt(dpj, w_ref[j])
        hb = h_ref[...]
        r = lax.rsqrt(jnp.mean(hb * hb, axis=-1, keepdims=True) + EPS)
        dg_ref[...] += jnp.sum(dhn * (hb * r), axis=0, keepdims=True)
        dn = dhn * g_ref[...]
        dh = dr_ref[...] + r * dn - hb * ((r * r * r) * jnp.mean(dn * hb, axis=-1, keepdims=True))
        dh_ref[...] = dh
        dhb_ref[...] = dh.astype(BF16)

    p_specs = [_view_rows(wd, d) for (_, d), wd in zip(pieces, widths)]
    return _pallas_call(
        body, name=name, grid=(s // tm,),
        out_shape=[_plain((s, d_model), F32), _sds((s, d_model), BF16), _plain((1, d_model), F32), _sds((N_DEV, s, c), BF16)],
        in_specs=p_specs + [_whole(wg.shape), _rows(d_model), _whole((1, d_model)), _rows(d_model)],
        out_specs=[_rows(d_model), _rows(d_model), _whole((1, d_model)), pl.BlockSpec((N_DEV, tm, c), lambda i: (0, i, 0))],
        scratch_shapes=[pltpu.VMEM((tm, n), BF16), _view_scratch(GROUP_WIDTH)],
        compiler_params=_params(1, 60),
    )(*[p for p, _ in pieces], wg, h, g, dres)


def _assemble_dproj(pieces, c, name):
    n = N_DEV * c
    tm = ROW_TILE
    widths = [p.shape[1] // d for p, d in pieces]
    assert sum(widths) == n
    s = pieces[0][0].shape[0] * pieces[0][1]
    n_p = len(pieces)

    def body(*refs):
        p_refs, (dpd_ref, dp, scr) = refs[:n_p], refs[n_p:]
        off = 0
        for p_ref, (_, d), wd in zip(p_refs, pieces, widths):
            if d == 1:
                dp[:, off:off + wd] = p_ref[...]
            else:
                dp[:, off:off + wd] = _from_view(scr, p_ref, d).astype(BF16)
            off += wd
        for j in range(N_DEV):
            dpd_ref[j] = dp[:, j * c:(j + 1) * c]

    return _pallas_call(
        body, name=name, grid=(s // tm,), out_shape=_sds((N_DEV, s, c), BF16),
        in_specs=[_view_rows(wd, d) for (_, d), wd in zip(pieces, widths)],
        out_specs=pl.BlockSpec((N_DEV, tm, c), lambda i: (0, i, 0)),
        scratch_shapes=[pltpu.VMEM((tm, n), BF16), _view_scratch(GROUP_WIDTH)],
        compiler_params=_params(1, 40),
    )(*[p for p, _ in pieces])


def _dgrad_norm_dm(dproj_dm, wg, h, g, dres, token, name):
    s, d_model = h.shape
    c = wg.shape[2]
    tm = ROW_TILE

    def body(_, dp_ref, w_ref, h_ref, g_ref, dr_ref, dh_ref, dg_ref):
        @pl.when(pl.program_id(0) == 0)
        def _():
            dg_ref[...] = jnp.zeros_like(dg_ref)

        dhn = jnp.zeros((tm, d_model), F32)
        for j in range(N_DEV):
            dhn += _dot_nt(dp_ref[j], w_ref[j])
        hb = h_ref[...]
        r = lax.rsqrt(jnp.mean(hb * hb, axis=-1, keepdims=True) + EPS)
        dg_ref[...] += jnp.sum(dhn * (hb * r), axis=0, keepdims=True)
        dn = dhn * g_ref[...]
        dh_ref[...] = dr_ref[...] + r * dn - hb * ((r * r * r) * jnp.mean(dn * hb, axis=-1, keepdims=True))

    return _pallas_call(
        body, name=name, grid=(s // tm,),
        out_shape=[_plain((s, d_model), F32), _plain((1, d_model), F32)],
        in_specs=[ANY, pl.BlockSpec((N_DEV, tm, c), lambda i: (0, i, 0)), _whole(wg.shape), _rows(d_model),
                  _whole((1, d_model)), _rows(d_model)],
        out_specs=[_rows(d_model), _whole((1, d_model))],
        compiler_params=_params(1, 60),
    )(token, dproj_dm, wg, h, g, dres)


def _wgrad_shards(a_t, b_dm, name):
    m, s = a_t.shape
    c = b_dm.shape[2]

    def body(a_ref, b_ref, o_ref):
        o_ref[...] = _dot(a_ref[...], b_ref[...]).astype(BF16)

    return _pallas_call(
        body, name=name, grid=(N_DEV,), out_shape=_sds((N_DEV, m, c), BF16),
        in_specs=[_whole(a_t.shape), pl.BlockSpec((None, s, c), lambda j: (j, 0, 0))],
        out_specs=pl.BlockSpec((None, m, c), lambda j: (j, 0, 0)), compiler_params=_params(1, 40),
    )(a_t, b_dm)


def _wgrad_cols(a_t, b, c, name):
    m, s = a_t.shape

    def body(a_ref, b_ref, o_ref):
        o_ref[...] = _dot(a_ref[...], b_ref[...]).astype(BF16)

    return _pallas_call(
        body, name=name, grid=(N_DEV,), out_shape=_sds((N_DEV, m, c), BF16),
        in_specs=[_whole(a_t.shape), pl.BlockSpec((s, c), lambda j: (0, j))],
        out_specs=pl.BlockSpec((None, m, c), lambda j: (j, 0, 0)), compiler_params=_params(1, 40),
    )(a_t, b)


def _wgrad_rows(a_t, b, name):
    m, s = a_t.shape
    n = b.shape[1]
    mr = m // N_DEV

    def body(a_ref, b_ref, o_ref):
        o_ref[...] = _dot(a_ref[...], b_ref[...]).astype(BF16)

    return _pallas_call(
        body, name=name, grid=(N_DEV,), out_shape=_sds((N_DEV, mr, n), BF16),
        in_specs=[pl.BlockSpec((mr, s), lambda j: (j, 0)), _whole(b.shape)],
        out_specs=pl.BlockSpec((None, mr, n), lambda j: (j, 0, 0)), compiler_params=_params(1, 40),
    )(a_t, b)


def _memkv_bwd(dkv, w, mem, g):
    n_layers = w.shape[0]

    def body(dkv_ref, w_ref, mem_ref, g_ref, dw_ref, dg_ref):
        mb = mem_ref[...]
        r = lax.rsqrt(jnp.mean(mb * mb, axis=-1, keepdims=True) + EPS)
        nm = mb * r
        mn = (nm * g_ref[...]).astype(BF16)
        dkvb = dkv_ref[...].astype(BF16)
        dw_ref[...] = _dot_tn(mn, dkvb).astype(BF16)
        dmn = _dot_nt(dkvb, w_ref[...])
        dg_ref[...] = jnp.sum(dmn * nm, axis=0, keepdims=True)

    lay = lambda *shape: pl.BlockSpec((None,) + shape, lambda l: (l, 0, 0))
    return _pallas_call(
        body, name="memkv_bwd", grid=(n_layers,),
        out_shape=[_plain((n_layers, D_MODEL, 2 * MEM_WIDTH), BF16), _plain((n_layers, 1, D_MODEL), F32)],
        in_specs=[lay(N_MEM, 2 * MEM_WIDTH), lay(D_MODEL, 2 * MEM_WIDTH), _whole(mem.shape), lay(1, D_MODEL)],
        out_specs=[lay(D_MODEL, 2 * MEM_WIDTH), lay(1, D_MODEL)], compiler_params=_params(1, 32),
    )(dkv, w, mem, g.reshape(n_layers, 1, D_MODEL))


def _bwd_post_attn(dhb, wg_out, z, os_, ls_, qm, kv, mo, head_ones):
    s = dhb.shape[0]
    gw = GROUP_WIDTH
    nb = gw + MEM_WIDTH
    c = wg_out.shape[2]
    tm = ROW_TILE

    def body(dh_ref, w_ref, z_ref, o0, o1, o2, l0, l1, l2, qm_ref, kv_ref, mo_ref, bd_ref,
             dz_ref, do0, do1, do2, dl0, dl1, dl2, dqm_ref, dkv_ref, s0, s1):
        @pl.when(pl.program_id(0) == 0)
        def _():
            dkv_ref[...] = jnp.zeros_like(dkv_ref)

        dy = jnp.zeros((tm, nb), F32)
        for j in range(N_DEV):
            dy += _dot_nt(dh_ref[:, j * c:(j + 1) * c], w_ref[j])
        ov, lv = [], []
        for o_ref, l_ref, d in zip((o0, o1, o2), (l0, l1, l2), DILATIONS):
            ov.append(_from_view(s0, o_ref, d))
            lv.append(_from_view(s1, l_ref, d))
        ws, mix = _mix_groups(ov, lv)
        sz, dsz = _silu_parts(z_ref[...])
        dz_ref[:, :gw] = (dy[:, :gw] * mix * dsz[:, :gw]).astype(BF16)
        dz_ref[:, gw:] = (dy[:, gw:] * mo_ref[...] * dsz[:, gw:]).astype(BF16)
        dbr = dy * sz
        dmix = dbr[:, :gw]
        t = dmix * mix
        th = t.astype(BF16)
        tl = (t - th.astype(F32)).astype(BF16)
        rs = _dot(th, bd_ref[...]) + _dot(tl, bd_ref[...])
        for wg_, do_ref, dl_ref, d in zip(ws, (do0, do1, do2), (dl0, dl1, dl2), DILATIONS):
            _to_view(s0, wg_ * dmix, do_ref, d)
            _to_view(s1, wg_ * rs, dl_ref, d)
        _mem_attn_bwd(qm_ref[...], kv_ref, dbr[:, gw:], dqm_ref, dkv_ref)

    vspecs = [_view_rows(gw, d) for d in DILATIONS]
    return _pallas_call(
        body, name="bwd_post_attn", grid=(s // tm,),
        out_shape=[_sds((s, nb), BF16)] + [_sds((s // d, d * gw), BF16) for d in DILATIONS]
                  + [_sds((s // d, d * gw), F32) for d in DILATIONS] + [_sds((s, MEM_WIDTH), BF16), _plain(kv.shape, F32)],
        in_specs=[_rows(D_MODEL), _whole(wg_out.shape), _rows(nb)] + vspecs * 2
                 + [_rows(MEM_WIDTH), _whole(kv.shape), _rows(MEM_WIDTH), _whole(head_ones.shape)],
        out_specs=[_rows(nb)] + vspecs * 2 + [_rows(MEM_WIDTH), _whole(kv.shape)],
        scratch_shapes=[_view_scratch(gw), _view_scratch(gw)],
        compiler_params=_params(1, 48),
    )(dhb, wg_out, z, *os_, *ls_, qm, kv, mo, head_ones)


def _attn_bwd(q, k, v, lse, do, dl, tabs, d, token):
    ln, dw = q.shape
    w = dw // d
    nb = ln // BLOCK
    nblk = d * nb
    reps = w // LANES
    cur, prev = _stream_maps(d, nb)
    qmap = lambda n: cur(jnp.minimum(n, nblk - 1))
    pmap = lambda n: prev(jnp.minimum(n, nblk - 1))
    omap = lambda n: cur(jnp.maximum(n - 1, 0))

    def body(_, q_ref, kp_ref, kc_ref, vp_ref, vc_ref, l_ref, do_ref, dl_ref, cq, saq, sbq, ck, sak, sbk,
             dq_ref, dk_ref, dv_ref, acck, accv, dqs):
        n = pl.program_id(0)

        @pl.when(n == 0)
        def _():
            acck[...] = jnp.zeros_like(acck)
            accv[...] = jnp.zeros_like(accv)

        def run(with_prev):
            if with_prev:
                kk = jnp.concatenate([kp_ref[...], kc_ref[...]], axis=0)
                vv = jnp.concatenate([vp_ref[...], vc_ref[...]], axis=0)
                rows = slice(0, 2 * BLOCK)
            else:
                kk, vv = kc_ref[...], vc_ref[...]
                rows = slice(BLOCK, 2 * BLOCK)
            valid = _band_mask(kk.shape[0])
            for h0 in range(0, HEADS_PER_GROUP, HEAD_BATCH):
                hs = range(h0, h0 + HEAD_BATCH)
                sls = [slice(h * HEAD_DIM, (h + 1) * HEAD_DIM) for h in hs]
                cols = [slice(h * HEAD_DIM, h * HEAD_DIM + 1) for h in hs]
                qhs = [q_ref[:, sl] for sl in sls]
                dobs = [do_ref[:, sl] for sl in sls]
                scs = [jnp.where(valid, _dot_nt(qh, kk[:, sl]), NEG) for qh, sl in zip(qhs, sls)]
                dps = [_dot_nt(dob, vv[:, sl]) for dob, sl in zip(dobs, sls)]
                ps = [jnp.exp(sc - l_ref[:, col]) for sc, col in zip(scs, cols)]
                dss = [(p * (dp - dl_ref[:, col])).astype(BF16) for p, dp, col in zip(ps, dps, cols)]
                pbs = [p.astype(BF16) for p in ps]
                for sl, qh, dob, ds, pb in zip(sls, qhs, dobs, dss, pbs):
                    dqs[:, sl] = _dot(ds, kk[:, sl]) * SCALE
                    acck[rows, sl] += _dot_tn(ds, qh)
                    accv[rows, sl] += _dot_tn(pb, dob)
            tq = [jnp.tile(r[...], (1, reps)) for r in (cq, saq, sbq)]
            dq_ref[...] = _rope_bwd(dqs[...], *tq).astype(BF16)

        pl.when(n < nblk)(lambda: _per_stream_block(n, nb, run))

        tk = [jnp.tile(r[...], (1, reps)) for r in (ck, sak, sbk)]
        dk_ref[...] = _rope_bwd(acck[0:BLOCK, :], *tk).astype(BF16)
        dv_ref[...] = accv[0:BLOCK, :].astype(BF16)
        acck[0:BLOCK, :] = acck[BLOCK:, :]
        accv[0:BLOCK, :] = accv[BLOCK:, :]
        acck[BLOCK:, :] = jnp.zeros((BLOCK, w), F32)
        accv[BLOCK:, :] = jnp.zeros((BLOCK, w), F32)

    blk = lambda f: pl.BlockSpec((BLOCK, w), f)
    tblk = lambda f: pl.BlockSpec((BLOCK, LANES), f)
    return _pallas_call(
        body, name=f"attn_bwd_d{d}", grid=(nblk + 1,),
        out_shape=[_sds((ln, dw), BF16)] * 3,
        in_specs=[ANY, blk(qmap), blk(pmap), blk(qmap), blk(pmap), blk(qmap), blk(qmap), blk(qmap), blk(qmap)]
                 + [tblk(qmap)] * 3 + [tblk(omap)] * 3,
        out_specs=[blk(qmap), blk(omap), blk(omap)],
        scratch_shapes=[pltpu.VMEM((2 * BLOCK, w), F32), pltpu.VMEM((2 * BLOCK, w), F32), pltpu.VMEM((BLOCK, w), F32)],
        compiler_params=_params(1, 32),
    )(token, q, k, k, v, v, lse, do, dl, *tabs, *tabs)


def _position():
    return lax.axis_index("x"), lax.axis_index("y"), lax.axis_index("c")


def _all_gather(shards, name):
    n_a = len(shards)

    def body(*refs):
        x_refs, out_refs = refs[:n_a], refs[n_a:2 * n_a]
        send_sems, recv_sems, local_sems = refs[2 * n_a:]
        x, y, c = _position()
        me, sibling = (x, y, c), (x, y, 1 - c)
        chips = [(1 - x, y), (x, 1 - y), (1 - x, 1 - y)]

        def rows(a, px, py, pc):
            return out_refs[a].at[4 * px + 2 * py + pc]

        def copy(a, k, block, to, own=False):
            return pltpu.make_async_remote_copy(
                src_ref=x_refs[a] if own else rows(a, *block), dst_ref=rows(a, *block),
                send_sem=send_sems.at[a, k], recv_sem=recv_sems.at[a, k], device_id=to, device_id_type=MESH)

        mine = [pltpu.make_async_copy(x_refs[a], rows(a, *me), local_sems.at[a]) for a in range(n_a)]
        for cp in mine:
            cp.start()
        first = []
        for j, chip in enumerate(chips):
            first += [copy(a, 1 + j, me, (*chip, c), own=True) for a in range(n_a)]
        first += [copy(a, 0, me, sibling, own=True) for a in range(n_a)]
        for cp in first:
            cp.start()
        passed = []
        for j, chip in enumerate(chips):
            for a in range(n_a):
                copy(a, 1 + j, (*chip, c), me).wait_recv()
                fwd = copy(a, 4 + j, (*chip, c), sibling)
                fwd.start()
                passed.append(fwd)
        for a in range(n_a):
            copy(a, 0, sibling, me).wait_recv()
        for j, chip in enumerate(chips):
            for a in range(n_a):
                copy(a, 4 + j, (*chip, 1 - c), me).wait_recv()
        for cp in first + passed:
            cp.wait_send()
        for cp in mine:
            cp.wait()

    return _pallas_call(
        body, name=name, out_shape=[_sds((N_DEV,) + t.shape, t.dtype) for t in shards],
        in_specs=[ANY] * n_a, out_specs=[ANY] * n_a,
        scratch_shapes=[pltpu.SemaphoreType.DMA((n_a, 7)), pltpu.SemaphoreType.DMA((n_a, 7)),
                        pltpu.SemaphoreType.DMA((n_a,))],
    )(*shards)


def _all_gather_relay(xs, name):
    def body(x_ref, out_ref, send_sems, recv_sems, local_sem):
        x, y, c = _position()
        me, sibling = (x, y, c), (x, y, 1 - c)
        xn, yn, diag = (1 - x, y, c), (x, 1 - y, c), (1 - x, 1 - y, c)
        src_nb = (x + c * (1 - 2 * x), y + (1 - c) * (1 - 2 * y), c)
        dst_nb = (x + (1 - c) * (1 - 2 * x), y + c * (1 - 2 * y), c)

        def rows(dev):
            return out_ref.at[4 * dev[0] + 2 * dev[1] + dev[2]]

        def copy(k, block, to, own=False):
            return pltpu.make_async_remote_copy(
                src_ref=x_ref if own else rows(block), dst_ref=rows(block),
                send_sem=send_sems.at[k], recv_sem=recv_sems.at[k], device_id=to, device_id_type=MESH)

        mine = pltpu.make_async_copy(x_ref, rows(me), local_sem)
        mine.start()
        first = [copy(1, me, xn, own=True), copy(2, me, yn, own=True), copy(0, me, sibling, own=True)]
        for cp in first:
            cp.start()
        copy(1, xn, me).wait_recv()
        copy(2, yn, me).wait_recv()
        relay = copy(3, src_nb, dst_nb)
        relay.start()
        passed = [copy(4, xn, sibling), copy(5, yn, sibling)]
        for cp in passed:
            cp.start()
        copy(3, diag, me).wait_recv()
        last = copy(6, diag, sibling)
        last.start()
        copy(0, sibling, me).wait_recv()
        for k, blk in ((4, (1 - x, y, 1 - c)), (5, (x, 1 - y, 1 - c)), (6, (1 - x, 1 - y, 1 - c))):
            copy(k, blk, me).wait_recv()
        for cp in first + [relay] + passed + [last]:
            cp.wait_send()
        mine.wait()

    return _pallas_call(
        body, name=name, out_shape=_sds((N_DEV,) + xs.shape, xs.dtype),
        in_specs=[ANY], out_specs=ANY,
        scratch_shapes=[pltpu.SemaphoreType.DMA((7,)), pltpu.SemaphoreType.DMA((7,)), pltpu.SemaphoreType.DMA],
    )(xs)


def _rs_to_sibling(gs):
    n_a = len(gs)

    def body(*refs):
        g_refs, recv_refs = refs[:n_a], refs[n_a:2 * n_a]
        send_sems, recv_sems = refs[2 * n_a:]
        x, y, c = _position()
        copies = []
        for k in range(4):
            for a in range(n_a):
                copies.append(pltpu.make_async_remote_copy(
                    src_ref=g_refs[a].at[2 * k + (1 - c)], dst_ref=recv_refs[a].at[k],
                    send_sem=send_sems.at[a, k], recv_sem=recv_sems.at[a, k],
                    device_id=(x, y, 1 - c), device_id_type=MESH))
        for cp in copies:
            cp.start()
        for cp in copies:
            cp.wait()

    return _pallas_call(
        body, name="rs_to_sibling", out_shape=[_sds((4,) + g.shape[1:], g.dtype) for g in gs],
        in_specs=[ANY] * n_a, out_specs=[ANY] * n_a,
        scratch_shapes=[pltpu.SemaphoreType.DMA((n_a, 4)), pltpu.SemaphoreType.DMA((n_a, 4))],
    )(*gs)


def _rs_to_chips(pbs):
    n_a = len(pbs)

    def body(*refs):
        p_refs, recv_refs = refs[:n_a], refs[n_a:2 * n_a]
        send_sems, recv_sems = refs[2 * n_a:]
        x, y, c = _position()
        chips = [(1 - x, y), (x, 1 - y), (1 - x, 1 - y)]
        copies = []
        for j, (px, py) in enumerate(chips):
            for a in range(n_a):
                copies.append(pltpu.make_async_remote_copy(
                    src_ref=p_refs[a].at[2 * px + py], dst_ref=recv_refs[a].at[j],
                    send_sem=send_sems.at[a, j], recv_sem=recv_sems.at[a, j],
                    device_id=(px, py, c), device_id_type=MESH))
        for cp in copies:
            cp.start()
        for cp in copies:
            cp.wait()

    return _pallas_call(
        body, name="rs_to_chips", out_shape=[_sds((3,) + p.shape[1:], p.dtype) for p in pbs],
        in_specs=[ANY] * n_a, out_specs=[ANY] * n_a,
        scratch_shapes=[pltpu.SemaphoreType.DMA((n_a, 3)), pltpu.SemaphoreType.DMA((n_a, 3))],
    )(*pbs)


HBM_SPEC = pl.BlockSpec(memory_space=pltpu.HBM)
SEM_SPEC = pl.BlockSpec(memory_space=pltpu.SEMAPHORE)
EFFECT = pltpu.SideEffectType.DATAFLOW_SIDE_EFFECTING
def _plan_gather_own(src_refs, land_refs):
    x, y, c = _position()
    me = 4 * x + 2 * y + c
    peers = [(x, y, 1 - c), (1 - x, y, c), (x, 1 - y, c), (1 - x, 1 - y, c)]
    return [(src_refs[a], land_refs[a].at[me], (a, k), peer) for k, peer in enumerate(peers) for a in range(len(src_refs))]


def _plan_gather_pass(src_refs, land_refs):
    x, y, c = _position()
    chips = [(1 - x, y), (x, 1 - y), (1 - x, 1 - y)]
    return [(land_refs[a].at[4 * px + 2 * py + c], land_refs[a].at[4 * px + 2 * py + c], (a, j), (x, y, 1 - c))
            for j, (px, py) in enumerate(chips) for a in range(len(land_refs))]


def _plan_to_sibling(src_refs, land_refs):
    x, y, c = _position()
    return [(src_refs[a].at[2 * k + (1 - c)], land_refs[a].at[k], (a, k), (x, y, 1 - c))
            for k in range(4) for a in range(len(src_refs))]


def _plan_to_chips(src_refs, land_refs):
    x, y, c = _position()
    chips = [(1 - x, y), (x, 1 - y), (1 - x, 1 - y)]
    return [(src_refs[a].at[2 * px + py], land_refs[a].at[j], (a, j), (px, py, c))
            for j, (px, py) in enumerate(chips) for a in range(len(src_refs))]


def _split_start(srcs, lands, plan, n_sem, after, name):
    n_s, n_a = len(srcs), len(lands)
    n_b = n_s + n_a

    def body(*refs):
        src_refs, land_refs = refs[:n_s], refs[n_s:n_b]
        send_sems, recv_sems, token = refs[n_b + 1], refs[n_b + 2], refs[-1]
        for src, dst, (a, k), dev in plan(src_refs, land_refs):
            i = a * n_sem + k
            pltpu.make_async_remote_copy(src_ref=src, dst_ref=dst, send_sem=send_sems.at[i], recv_sem=recv_sems.at[i],
                                         device_id=dev, device_id_type=MESH).start()
        token[...] = jnp.zeros_like(token)

    bufs = list(srcs) + list(lands)
    res = pl.pallas_call(
        body, name=name,
        out_shape=(pltpu.SemaphoreType.DMA((n_a * n_sem,)), pltpu.SemaphoreType.DMA((n_a * n_sem,)),
                   *[pltpu.HBM(t.shape, t.dtype) for t in bufs], _plain((8, LANES), F32)),
        in_specs=[HBM_SPEC] * n_b + [ANY],
        out_specs=(SEM_SPEC, SEM_SPEC, *[HBM_SPEC] * n_b, pl.BlockSpec(memory_space=pltpu.VMEM)),
        input_output_aliases={i: 2 + i for i in range(n_b)},
        compiler_params=pltpu.CompilerParams(has_side_effects=EFFECT),
    )(*[pltpu.with_memory_space_constraint(t, pltpu.HBM) for t in bufs], after)
    return (res[0], res[1], res[2:2 + n_s], res[2 + n_s:2 + n_b]), res[-1]


def _split_wait(started, plan, after, name):
    send_sems, recv_sems, srcs, lands = started
    n_s, n_a = len(srcs), len(lands)
    n_b = n_s + n_a
    n_sem = send_sems.shape[0] // n_a

    def body(*refs):
        src_refs, land_refs = refs[:n_s], refs[n_s:n_b]
        s_sems, r_sems = refs[n_b], refs[n_b + 1]
        for src, dst, (a, k), dev in plan(src_refs, land_refs):
            i = a * n_sem + k
            cp = pltpu.make_async_remote_copy(src_ref=src, dst_ref=dst, send_sem=s_sems.at[i], recv_sem=r_sems.at[i],
                                              device_id=dev, device_id_type=MESH)
            cp.wait_send()
            cp.wait_recv()

    bufs = list(srcs) + list(lands)
    res = pl.pallas_call(
        body, name=name, out_shape=tuple(pltpu.HBM(t.shape, t.dtype) for t in bufs),
        in_specs=[HBM_SPEC] * n_b + [SEM_SPEC, SEM_SPEC, ANY],
        out_specs=tuple([HBM_SPEC] * n_b),
        input_output_aliases={i: i for i in range(n_b)},
        compiler_params=pltpu.CompilerParams(has_side_effects=EFFECT),
    )(*bufs, send_sems, recv_sems, after)
    return res[:n_s], res[n_s:]


def _row_tile(r):
    return ROW_TILE if r % ROW_TILE == 0 else r


def _rs_add_sibling(gp, recv, c_arr, name):
    _, r, l = gp.shape
    tr = r if r <= 4 * ROW_TILE else _row_tile(r)

    def body(c_ref, g_ref, r_ref, pf_ref, pb_ref):
        sm = g_ref[...].astype(F32) + r_ref[...].astype(F32)
        pf_ref[...] = sm
        pb_ref[...] = sm.astype(BF16)

    spec = pl.BlockSpec((None, tr, l), lambda k, i, c: (k, i, 0))
    return _pallas_call(
        body, name=name,
        grid_spec=pltpu.PrefetchScalarGridSpec(
            num_scalar_prefetch=1, grid=(4, r // tr),
            in_specs=[pl.BlockSpec((None, tr, l), lambda k, i, c: (2 * k + c[0], i, 0)), spec],
            out_specs=[spec, spec]),
        out_shape=[_sds((4, r, l), F32), _sds((4, r, l), BF16)], compiler_params=_params(2, 32),
    )(c_arr, gp, recv)


def _adam_update(w, gv, m, v):
    nm = ADAM_B1 * m + (1.0 - ADAM_B1) * gv
    nv = ADAM_B2 * v + (1.0 - ADAM_B2) * (gv * gv)
    m_hat = nm / (1.0 - ADAM_B1 ** ADAM_STEP)
    v_hat = nv / (1.0 - ADAM_B2 ** ADAM_STEP)
    return -ADAM_LR * (m_hat / (jnp.sqrt(v_hat) + ADAM_EPS) + ADAM_WD * w), nm, nv


def _rs_finish_adamw(pf, recv, k_arr, w, m, v, name):
    _, r, l = pf.shape
    tr = _row_tile(r)

    def body(k_ref, p_ref, r_ref, w_ref, m_ref, v_ref, g_ref, d_ref, nm_ref, nv_ref):
        gv = ((p_ref[...] + r_ref[0].astype(F32)) + r_ref[1].astype(F32)) + r_ref[2].astype(F32)
        g_ref[...] = gv
        d_ref[...], nm_ref[...], nv_ref[...] = _adam_update(w_ref[...], gv, m_ref[...], v_ref[...])

    spec = pl.BlockSpec((tr, l), lambda i, k: (i, 0))
    return _pallas_call(
        body, name=name,
        grid_spec=pltpu.PrefetchScalarGridSpec(
            num_scalar_prefetch=1, grid=(r // tr,),
            in_specs=[pl.BlockSpec((None, tr, l), lambda i, k: (k[0], i, 0)),
                      pl.BlockSpec((3, tr, l), lambda i, k: (0, i, 0)), spec, spec, spec],
            out_specs=[spec] * 4),
        out_shape=[_plain((r, l), F32)] * 4, compiler_params=_params(1, 32),
    )(k_arr, pf, recv, w, m, v)


def _rs_finish_adamw_t(pf, recv, k_arr, w_t, m_t, v_t, name):
    _, r, c = pf.shape
    tr = _row_tile(r)
    cp = -(-c // LANES) * LANES

    def body(k_ref, p_ref, r_ref, w_ref, m_ref, v_ref, g_ref, d_ref, nm_ref, nv_ref, pad):
        gv = ((p_ref[...] + r_ref[0].astype(F32)) + r_ref[1].astype(F32)) + r_ref[2].astype(F32)
        pad[...] = jnp.zeros_like(pad)
        pad[:, 0:c] = gv
        gt = pad[...].T[0:c, :]
        g_ref[...] = gt
        d_ref[...], nm_ref[...], nv_ref[...] = _adam_update(w_ref[...], gt, m_ref[...], v_ref[...])

    spec = pl.BlockSpec((c, tr), lambda i, k: (0, i))
    return _pallas_call(
        body, name=name,
        grid_spec=pltpu.PrefetchScalarGridSpec(
            num_scalar_prefetch=1, grid=(r // tr,),
            in_specs=[pl.BlockSpec((None, tr, c), lambda i, k: (k[0], i, 0)),
                      pl.BlockSpec((3, tr, c), lambda i, k: (0, i, 0)), spec, spec, spec],
            out_specs=[spec] * 4, scratch_shapes=[pltpu.VMEM((tr, cp), F32)]),
        out_shape=[_plain((c, r), F32)] * 4, compiler_params=_params(1, 32),
    )(k_arr, pf, recv, w_t, m_t, v_t)


def _sum_devices(g):
    def body(g_ref, o_ref):
        acc = g_ref[0]
        for j in range(1, N_DEV):
            acc = acc + g_ref[j]
        o_ref[...] = acc

    return _pallas_call(body, name="sum_devices", out_shape=_plain(g.shape[1:], F32))(g)


def _adamw(w, g, m, v, name):
    shape = w.shape
    w2, g2, m2, v2 = [t.reshape((-1, shape[-1])) for t in (w, g, m, v)]

    def body(w_ref, g_ref, m_ref, v_ref, d_ref, nm_ref, nv_ref):
        d_ref[...], nm_ref[...], nv_ref[...] = _adam_update(w_ref[...], g_ref[...], m_ref[...], v_ref[...])

    outs = _pallas_call(body, name=name, out_shape=[_plain(w2.shape, F32)] * 3)(w2, g2, m2, v2)
    return tuple(t.reshape(shape) for t in outs)


def _after(t, token):
    return t + token[0:1, 0:1].astype(t.dtype)


def _finish(name, pf, recv, k_arr, w, m, v):
    if name in ("attn_w_in", "conv_w_in"):
        res = _rs_finish_adamw_t(pf, recv, k_arr, w.T, m.T, v.T, "rs_finish_adamw_" + name)
        return tuple(t.T for t in res)
    return _rs_finish_adamw(pf, recv, k_arr, w, m, v, "rs_finish_adamw_" + name)


def kernel(x, mem, positions, norm_g, mem_norm_g, w_mem_kv, attn_w_in, attn_w_out, conv_w_in, conv_w, conv_w_out, final_g, loss_target, m_norm_g, m_mem_norm_g, m_w_mem_kv, m_attn_w_in, m_attn_w_out, m_conv_w_in, m_conv_w, m_conv_w_out, m_final_g, v_norm_g, v_mem_norm_g, v_w_mem_kv, v_attn_w_in, v_attn_w_out, v_conv_w_in, v_conv_w, v_conv_w_out, v_final_g):
    px, py, pc = _position()
    me = 4 * px + 2 * py + pc
    c_arr = jnp.reshape(pc, (1,)).astype(jnp.int32)
    k_arr = jnp.reshape(2 * px + py, (1,)).astype(jnp.int32)
    x, mem, pos, tgt = x[0], mem[0], positions[0], loss_target[0]

    wg_in0 = _all_gather_relay(attn_w_in[0].astype(BF16), "gather_w_in0")
    late = [attn_w_out[0].astype(BF16), conv_w_in[0].astype(BF16), conv_w_out[0].astype(BF16),
            w_mem_kv[0].astype(BF16), w_mem_kv[1].astype(BF16), jnp.pad(conv_w[0], ((0, 5), (0, 0)))]
    lands = [lax.dynamic_update_slice(lax.empty((N_DEV,) + t.shape, t.dtype), t[None], (me, 0, 0)) for t in late]
    late_weights, late_token = _split_start(late, lands, _plan_gather_own, 4, wg_in0, "gather_late_start")

    tabs = _rope_tables(pos)
    g0, g1 = _after(norm_g[0:1], late_token), norm_g[1:2]

    hn0, hn0_t, qs, ks, vs, tabs_v, qm0, z0 = _inproj_attn(x, g0, wg_in0, tabs)
    os_, ls_ = [], []
    for j, d in enumerate(DILATIONS):
        if j == 1:
            _, lands = _split_wait(late_weights, _plan_gather_own, ls_[0], "gather_late_wait")
            late_weights, late_token = _split_start([], lands, _plan_gather_pass, 3, ls_[0], "gather_late_pass_start")
        o, l = _attn_fwd(qs[j], ks[j], vs[j], d, late_token)
        os_.append(o)
        ls_.append(l)

    _, gathered = _split_wait(late_weights, _plan_gather_pass, ls_[2], "gather_late_pass_wait")
    wg_out0, wg_in1, wg_out1, wg_kv0, wg_kv1, cw_all = gathered
    w_out1 = wg_out1.reshape(-1, wg_out1.shape[2])
    w_kv = jnp.stack([wg_kv0.reshape(-1, wg_kv0.shape[2]), wg_kv1.reshape(-1, wg_kv1.shape[2])])
    cw = cw_all[:, 0:3].transpose(1, 0, 2).reshape(3, -1)
    kv = _memkv_fwd(mem, mem_norm_g, w_kv)
    y0, y0_t, mo0, h1 = _post_attn(os_, ls_, qm0, kv[0], z0, x, wg_out0)

    hn1, hn1_t, bg, cg, u, qm1, z1 = _inproj_conv(h1, g1, wg_in1)
    y1, y1_t, mo1, dh2, dh2b, loss_acc, d_final_g = _post_conv_loss(
        bg, cg, u, qm1, kv[1], z1, h1, w_out1, cw, final_g.reshape(1, -1), tgt)

    d_w_out1 = _wgrad_rows(y1_t, dh2b, "wgrad_out1")
    dz1, dbg, dconv, dqm1, dkv1 = _bwd_post_conv(dh2b, w_out1, bg, cg, u, z1, qm1, kv[1], mo1, cw)
    dcg, du, dcw = _bwd_conv(dconv, cg, u, cw)
    dh1, dh1b, dg1, dproj1 = _dgrad_norm([(dbg, 1), (dcg, 1), (du, 1), (dqm1, 1), (dz1, 1)], wg_in1, h1, g1, dh2,
                                         "dgrad_norm_conv")
    d_w_in1 = _wgrad_shards(hn1_t, dproj1, "wgrad_in1")

    d_w_out0 = _wgrad_cols(y0_t, dh1b, wg_out0.shape[2], "wgrad_out0")

    names1 = ["conv_w_in", "conv_w_out", "attn_w_out"]
    grads1 = [d_w_in1, d_w_out1, d_w_out0]
    started, token = _split_start(grads1, [lax.empty((4,) + g.shape[1:], g.dtype) for g in grads1],
                                  _plan_to_sibling, 4, dg1, "rs1_sibling_start")

    gw = GROUP_WIDTH
    ones = (jnp.arange(gw)[:, None] // HEAD_DIM == jnp.arange(gw)[None, :] // HEAD_DIM).astype(BF16)
    ones = _after(ones, token)
    res = _bwd_post_attn(dh1b, wg_out0, z0, os_, ls_, qm0, kv[0], mo0, ones)
    dz0, dos, dls, dqm0, dkv0 = res[0], res[1:4], res[4:7], res[7], res[8]

    grads1, from_sibling = _split_wait(started, _plan_to_sibling, dz0, "rs1_sibling_wait")
    parts1 = [_rs_add_sibling(g, r, c_arr, "rs_add_sibling_" + n) for g, r, n in zip(grads1, from_sibling, names1)]
    pbs1 = [pb for _, pb in parts1]
    started, token = _split_start(pbs1, [lax.empty((3,) + p.shape[1:], p.dtype) for p in pbs1],
                                  _plan_to_chips, 3, dg1, "rs1_chips_start")

    dqs, dks, dvs = [], [], []
    for j, d in enumerate(DILATIONS):
        dq, dk, dv = _attn_bwd(qs[j], ks[j], vs[j], ls_[j], dos[j], dls[j], tabs_v[j], d, token)
        dqs.append((dq, d))
        dks.append((dk, d))
        dvs.append((dv, d))
    d_w_kv, d_mem_g = _memkv_bwd(jnp.stack([dkv0, dkv1]), w_kv, mem, mem_norm_g)
    n_kv = d_w_kv.shape[1] // N_DEV
    dproj0 = _assemble_dproj(dqs + dks + dvs + [(dqm0, 1), (dz0, 1)], wg_in0.shape[2], "assemble_dproj_attn")
    d_w_in0 = _wgrad_shards(hn0_t, dproj0, "wgrad_in0")

    names0 = ["attn_w_in", "w_mem_kv0", "w_mem_kv1"]
    grads0 = [d_w_in0, d_w_kv[0].reshape(N_DEV, n_kv, -1), d_w_kv[1].reshape(N_DEV, n_kv, -1)]
    started0, token0 = _split_start(grads0, [lax.empty((4,) + g.shape[1:], g.dtype) for g in grads0],
                                    _plan_to_sibling, 4, dg1, "rs0_sibling_start")
    _, from_chips1 = _split_wait(started, _plan_to_chips, token0, "rs1_chips_wait")
    shard = dict(attn_w_in=(attn_w_in[0], m_attn_w_in[0], v_attn_w_in[0]),
                 attn_w_out=(attn_w_out[0], m_attn_w_out[0], v_attn_w_out[0]),
                 conv_w_in=(conv_w_in[0], m_conv_w_in[0], v_conv_w_in[0]),
                 conv_w_out=(conv_w_out[0], m_conv_w_out[0], v_conv_w_out[0]),
                 w_mem_kv0=(w_mem_kv[0], m_w_mem_kv[0], v_w_mem_kv[0]), w_mem_kv1=(w_mem_kv[1], m_w_mem_kv[1], v_w_mem_kv[1]))
    big = {}
    for n, (pf, _), r in zip(names1, parts1, from_chips1):
        big[n] = _finish(n, pf, r, k_arr, *shard[n])

    grads0, from_sibling = _split_wait(started0, _plan_to_sibling, big["conv_w_out"][1], "rs0_sibling_wait")
    parts0 = [_rs_add_sibling(g, r, c_arr, "rs_add_sibling_" + n) for g, r, n in zip(grads0, from_sibling, names0)]
    pbs0 = [pb for _, pb in parts0]
    started0, token0 = _split_start(pbs0, [lax.empty((3,) + p.shape[1:], p.dtype) for p in pbs0],
                                    _plan_to_chips, 3, dg1, "rs0_chips_start")
    dx, dg0 = _dgrad_norm_dm(dproj0, wg_in0, x, g0, dh1, token0, "dgrad_norm_attn")

    small_part = jnp.concatenate([dg0, dg1, d_mem_g.reshape(2, -1), d_final_g, dcw[0:3]], axis=0)
    small_part = jnp.concatenate([small_part, jnp.broadcast_to(loss_acc[0, 0], small_part.shape)], axis=0)
    small = _sum_devices(_all_gather([small_part], "gather_small_grads")[0])
    loss = small[8, 0]
    g_conv_w = lax.dynamic_slice(small[5:8], (0, me * LANES), (3, LANES))[None]
    small_g = dict(norm_g=small[0:2], mem_norm_g=small[2:4], conv_w=g_conv_w, final_g=small[4])
    small_w = dict(norm_g=(norm_g, m_norm_g, v_norm_g), mem_norm_g=(mem_norm_g, m_mem_norm_g, v_mem_norm_g),
                   conv_w=(conv_w, m_conv_w, v_conv_w), final_g=(final_g, m_final_g, v_final_g))
    for n, (w, m, v) in small_w.items():
        big[n] = (small_g[n],) + _adamw(w, small_g[n], m, v, "adamw_" + n)

    _, from_chips0 = _split_wait(started0, _plan_to_chips, big["final_g"][1], "rs0_chips_wait")
    for n, (pf, _), r in zip(names0, parts0, from_chips0):
        big[n] = _finish(n, pf, r, k_arr, *shard[n])
    for n in ("attn_w_in", "attn_w_out", "conv_w_in", "conv_w_out"):
        big[n] = tuple(t[None] for t in big[n])
    big["w_mem_kv"] = tuple(jnp.stack([a, b]) for a, b in zip(big["w_mem_kv0"], big["w_mem_kv1"]))

    order = ["norm_g", "mem_norm_g", "w_mem_kv", "attn_w_in", "attn_w_out", "conv_w_in", "conv_w", "conv_w_out", "final_g"]
    return (loss, dx[None], *[big[n][0] for n in order], *[big[n][1] for n in order],
            *[big[n][2] for n in order], *[big[n][3] for n in order])
```

```python
import functools

import jax
import jax.numpy as jnp
from jax import lax
from jax.experimental import pallas as pl
from jax.experimental.pallas import tpu as pltpu

F32 = jnp.float32
BF16 = jnp.bfloat16

N_DEV = 8
D_MODEL = 1024
HEAD_DIM = 64
ROT_DIM = HEAD_DIM // 4
ROPE_THETA = 500000.0
DILATIONS = (1, 4, 16)
HEADS_PER_GROUP = 8
GROUP_WIDTH = HEADS_PER_GROUP * HEAD_DIM
BLOCK = 128
N_MEM = 256
MEM_HEADS = 4
MEM_WIDTH = MEM_HEADS * HEAD_DIM
CONV_WIDTH = D_MODEL
EPS = 1e-6
SCALE = HEAD_DIM ** -0.5
NEG = -1e30

ADAM_LR = 0.001
ADAM_B1 = 0.9
ADAM_B2 = 0.999
ADAM_EPS = 1e-08
ADAM_WD = 0.01
ADAM_STEP = 10

ROW_TILE = 256
LANES = 128
MESH = pl.DeviceIdType.MESH
ANY = pl.BlockSpec(memory_space=pl.ANY)


def _pallas_call(body, **kw):
    call = pl.pallas_call(body, **kw)

    def run(*args):
        pinned = [pltpu.with_memory_space_constraint(a, pltpu.HBM) if jnp.issubdtype(a.dtype, jnp.floating) else a
                  for a in args]
        return call(*pinned)

    return run


def _dot(a, b):
    return lax.dot_general(a, b, (((1,), (0,)), ((), ())), preferred_element_type=F32)


def _dot_nt(a, b):
    return lax.dot_general(a, b, (((1,), (1,)), ((), ())), preferred_element_type=F32)


def _dot_tn(a, b):
    return lax.dot_general(a, b, (((0,), (0,)), ((), ())), preferred_element_type=F32)


def _params(n_grid, vmem_mb=48):
    return pltpu.CompilerParams(dimension_semantics=("arbitrary",) * n_grid, vmem_limit_bytes=vmem_mb << 20)


def _rows(width, tm=ROW_TILE):
    return pl.BlockSpec((tm, width), lambda i: (i, 0))


def _view_rows(width, d, tm=ROW_TILE):
    return pl.BlockSpec((tm // d, d * width), lambda i: (i, 0))


def _whole(shape):
    return pl.BlockSpec(shape, lambda *_: (0,) * len(shape))


def _resident(shape):
    return pl.BlockSpec(shape, lambda *_: (0,) * len(shape), pipeline_mode=pl.Buffered(1))


def _sds(shape, dtype):
    return pltpu.HBM(shape, dtype)


def _plain(shape, dtype):
    return jax.ShapeDtypeStruct(shape, dtype)


def _silu_parts(z):
    sg = jax.nn.sigmoid(z)
    return z * sg, sg * (1.0 + z * (1.0 - sg))


def _to_view(scr, val, out_ref, d):
    tm, w = val.shape
    if d == 1:
        out_ref[...] = val.astype(out_ref.dtype)
        return
    for cb in range(w // LANES):
        scr[cb] = val[:, cb * LANES:(cb + 1) * LANES]
    for r in range(d):
        for cb in range(w // LANES):
            lo = r * w + cb * LANES
            out_ref[:, lo:lo + LANES] = scr[cb, pl.ds(r, tm // d, stride=d), :].astype(out_ref.dtype)


def _from_view(scr, in_ref, d):
    if d == 1:
        return in_ref[...].astype(F32)
    nc, tm, _ = scr.shape
    w = nc * LANES
    for r in range(d):
        for cb in range(nc):
            lo = r * w + cb * LANES
            scr[cb, pl.ds(r, tm // d, stride=d), :] = in_ref[:, lo:lo + LANES].astype(F32)
    return jnp.concatenate([scr[cb] for cb in range(nc)], axis=1)


def _view_scratch(width, tm=ROW_TILE):
    return pltpu.VMEM((width // LANES, tm, LANES), F32)


def _rope_tables(pos):
    half = ROT_DIM // 2
    inv_freq = ROPE_THETA ** (-jnp.arange(half, dtype=F32) * (2.0 / ROT_DIM))
    ang = pos.astype(F32)[:, None] * inv_freq
    cos, sin = jnp.cos(ang), jnp.sin(ang)
    s = pos.shape[0]
    z8 = jnp.zeros((s, half), F32)
    rest = HEAD_DIM - ROT_DIM
    cosf = jnp.concatenate([cos, cos, jnp.ones((s, rest), F32)], axis=1)
    sa = jnp.concatenate([-sin, z8, jnp.zeros((s, rest), F32)], axis=1)
    sb = jnp.concatenate([z8, sin, jnp.zeros((s, rest), F32)], axis=1)
    return tuple(jnp.tile(t, (1, LANES // HEAD_DIM)) for t in (cosf, sa, sb))


def _rope_fwd(t, cv, sav, sbv):
    w = t.shape[1]
    return t * cv + pltpu.roll(t, w - ROT_DIM // 2, 1) * sav + pltpu.roll(t, ROT_DIM // 2, 1) * sbv


def _rope_bwd(g, cv, sav, sbv):
    w = g.shape[1]
    return g * cv + pltpu.roll(g * sav, ROT_DIM // 2, 1) + pltpu.roll(g * sbv, w - ROT_DIM // 2, 1)


def _project(hn, wg_ref, proj_scr):
    c = wg_ref.shape[2]
    for j in range(N_DEV):
        proj_scr[:, j * c:(j + 1) * c] = _dot(hn, wg_ref[j])


def _inproj_attn(x, g, wg, tabs):
    s, d_model = x.shape
    gw = GROUP_WIDTH
    n = N_DEV * wg.shape[2]
    nz = n - 9 * gw - MEM_WIDTH
    reps = gw // LANES
    tm = ROW_TILE

    def body(x_ref, g_ref, w_ref, c_ref, sa_ref, sb_ref, hn_ref, hnt_ref, *rest):
        outs, (proj, scr, tscr) = rest[:-3], rest[-3:]
        q_refs, k_refs, v_refs, t_refs, qm_ref, z_ref = outs[0:3], outs[3:6], outs[6:9], outs[9:18], outs[18], outs[19]
        xb = x_ref[...]
        r = lax.rsqrt(jnp.mean(xb * xb, axis=-1, keepdims=True) + EPS)
        hn = ((xb * r) * g_ref[...]).astype(BF16)
        hn_ref[...] = hn
        hnt_ref[...] = hn.T
        _project(hn, w_ref, proj)
        tab = (c_ref[...], sa_ref[...], sb_ref[...])
        cv, sav, sbv = [jnp.tile(t, (1, reps)) for t in tab]
        for j, d in enumerate(DILATIONS):
            tq = _rope_fwd(proj[:, j * gw:(j + 1) * gw], cv, sav, sbv)
            _to_view(scr, tq * SCALE, q_refs[j], d)
            tk = _rope_fwd(proj[:, (3 + j) * gw:(4 + j) * gw], cv, sav, sbv)
            _to_view(scr, tk, k_refs[j], d)
            _to_view(scr, proj[:, (6 + j) * gw:(7 + j) * gw], v_refs[j], d)
            for i in range(3):
                _to_view(tscr, tab[i], t_refs[3 * j + i], d)
        qm_ref[...] = proj[:, 9 * gw:9 * gw + MEM_WIDTH].astype(BF16)
        z_ref[...] = proj[:, 9 * gw + MEM_WIDTH:]

    views = [_sds((s // d, d * gw), BF16) for d in DILATIONS]
    tviews = [_sds((s // d, d * LANES), F32) for d in DILATIONS for _ in range(3)]
    out_shape = ([_sds((s, d_model), BF16), _sds((d_model, s), BF16)] + views * 3 + tviews
                 + [_sds((s, MEM_WIDTH), BF16), _sds((s, nz), F32)])
    vspecs = [_view_rows(gw, d, tm) for d in DILATIONS]
    tspecs = [_view_rows(LANES, d, tm) for d in DILATIONS for _ in range(3)]
    out_specs = ([_rows(d_model, tm), pl.BlockSpec((d_model, tm), lambda i: (0, i))] + vspecs * 3 + tspecs
                 + [_rows(MEM_WIDTH, tm), _rows(nz, tm)])
    res = _pallas_call(
        body, name="inproj_attn", grid=(s // tm,), out_shape=out_shape,
        in_specs=[_rows(d_model, tm), _whole((1, d_model)), _resident(wg.shape)] + [_rows(LANES, tm)] * 3,
        out_specs=out_specs,
        scratch_shapes=[pltpu.VMEM((tm, n), F32), _view_scratch(gw, tm), _view_scratch(LANES, tm)],
        compiler_params=_params(1, 60),
    )(x, g, wg, *tabs)
    tabs_v = [res[11 + 3 * j:14 + 3 * j] for j in range(3)]
    return res[0], res[1], res[2:5], res[5:8], res[8:11], tabs_v, res[20], res[21]


def _stream_maps(d, nb):
    def cur(n):
        return (n % nb, n // nb)

    def prev(n):
        return (jnp.maximum(n % nb - 1, 0), n // nb)

    return cur, prev


def _band_mask(n_keys):
    qi = lax.broadcasted_iota(jnp.int32, (BLOCK, n_keys), 0)
    kj = lax.broadcasted_iota(jnp.int32, (BLOCK, n_keys), 1)
    if n_keys == BLOCK:
        return kj <= qi
    return jnp.logical_or(jnp.logical_and(kj < BLOCK, kj >= qi), jnp.logical_and(kj >= BLOCK, (kj - BLOCK) <= qi))


def _per_stream_block(n, nb, run):
    if nb == 1:
        run(False)
        return
    first = (n % nb) == 0
    pl.when(first)(lambda: run(False))
    pl.when(jnp.logical_not(first))(lambda: run(True))


def _low_head_lanes():
    return lax.broadcasted_iota(jnp.int32, (1, LANES), 1) < HEAD_DIM


def _split_pair(t, low):
    zero = jnp.zeros_like(t)
    return jnp.where(low, t, zero), jnp.where(low, zero, t)


def _attn_fwd(q, k, v, d, token):
    ln, dw = q.shape
    w = dw // d
    nb = ln // BLOCK
    nblk = d * nb
    cur, prev = _stream_maps(d, nb)

    def body(_, q_ref, kp_ref, kc_ref, vp_ref, vc_ref, o_ref, lse_ref):
        def run(with_prev):
            if with_prev:
                kk = jnp.concatenate([kp_ref[...], kc_ref[...]], axis=0)
                vv = jnp.concatenate([vp_ref[...], vc_ref[...]], axis=0)
            else:
                kk, vv = kc_ref[...], vc_ref[...]
            valid = _band_mask(kk.shape[0])
            low = _low_head_lanes()
            pairs = [slice(p * LANES, (p + 1) * LANES) for p in range(w // LANES)]
            qs_ = [h for pr in pairs for h in _split_pair(q_ref[:, pr], low)]
            k2s = [kk[:, pr] for pr in pairs for _ in range(2)]
            scs = [jnp.where(valid, _dot_nt(qh, k2), NEG) for qh, k2 in zip(qs_, k2s)]
            ms = [jnp.max(sc, axis=-1, keepdims=True) for sc in scs]
            ps = [jnp.exp(sc - m) for sc, m in zip(scs, ms)]
            ls = [jnp.sum(p, axis=-1, keepdims=True) for p in ps]
            pns = [(p * (1.0 / l)).astype(BF16) for p, l in zip(ps, ls)]
            for i, pr in enumerate(pairs):
                v2 = vv[:, pr]
                a, b = 2 * i, 2 * i + 1
                o_ref[:, pr] = jnp.where(low, _dot(pns[a], v2), _dot(pns[b], v2))
                lse_ref[:, pr] = jnp.where(low, ms[a] + jnp.log(ls[a]), ms[b] + jnp.log(ls[b]))

        _per_stream_block(pl.program_id(0), nb, run)

    blk = lambda f: pl.BlockSpec((BLOCK, w), f)
    return _pallas_call(
        body, name=f"attn_fwd_d{d}", grid=(nblk,),
        out_shape=[_sds((ln, dw), F32)] * 2,
        in_specs=[ANY, blk(cur), blk(prev), blk(cur), blk(prev), blk(cur)],
        out_specs=[blk(cur), blk(cur)], compiler_params=_params(1, 32),
    )(token, q, k, k, v, v)


def _memkv_fwd(mem, g, w):
    n_layers = w.shape[0]

    def body(mem_ref, g_ref, w_ref, kv_ref):
        mb = mem_ref[...]
        r = lax.rsqrt(jnp.mean(mb * mb, axis=-1, keepdims=True) + EPS)
        mn = ((mb * r) * g_ref[...]).astype(BF16)
        kv_ref[...] = _dot(mn, w_ref[...]).astype(BF16)

    return _pallas_call(
        body, name="memkv_fwd", grid=(n_layers,),
        out_shape=_plain((n_layers, N_MEM, 2 * MEM_WIDTH), BF16),
        in_specs=[_whole(mem.shape), pl.BlockSpec((None, 1, D_MODEL), lambda l: (l, 0, 0)),
                  pl.BlockSpec((None, D_MODEL, 2 * MEM_WIDTH), lambda l: (l, 0, 0))],
        out_specs=pl.BlockSpec((None, N_MEM, 2 * MEM_WIDTH), lambda l: (l, 0, 0)),
        compiler_params=_params(1, 32),
    )(mem, g.reshape(n_layers, 1, D_MODEL), w)


def _mix_groups(os_, ls_):
    mx = jnp.maximum(jnp.maximum(ls_[0], ls_[1]), ls_[2])
    es = [jnp.exp(t - mx) for t in ls_]
    inv = 1.0 / (es[0] + es[1] + es[2])
    ws = [e * inv for e in es]
    mix = ws[0] * os_[0] + ws[1] * os_[1] + ws[2] * os_[2]
    return ws, mix


MEM_SLICES = [slice(h * HEAD_DIM, (h + 1) * HEAD_DIM) for h in range(MEM_HEADS)]


def _mem_probs(qm, km):
    scs = [_dot_nt(qm[:, sl], km[:, sl]) * SCALE for sl in MEM_SLICES]
    es = [jnp.exp(sc - jnp.max(sc, axis=-1, keepdims=True)) for sc in scs]
    return [e * (1.0 / jnp.sum(e, axis=-1, keepdims=True)) for e in es]


def _mem_attn_into(qm, kv_ref, mo_ref):
    km, vm = kv_ref[:, :MEM_WIDTH], kv_ref[:, MEM_WIDTH:]
    ps = [p.astype(BF16) for p in _mem_probs(qm, km)]
    for sl, p in zip(MEM_SLICES, ps):
        mo_ref[:, sl] = _dot(p, vm[:, sl])


def _mem_attn_bwd(qm, kv_ref, dmem, dqm_ref, dkv_ref):
    km, vm = kv_ref[:, :MEM_WIDTH], kv_ref[:, MEM_WIDTH:]
    dmb = dmem.astype(BF16)
    ps = _mem_probs(qm, km)
    dps = [_dot_nt(dmb[:, sl], vm[:, sl]) for sl in MEM_SLICES]
    dss = [(p * (dp - jnp.sum(dp * p, axis=-1, keepdims=True)) * SCALE).astype(BF16) for p, dp in zip(ps, dps)]
    pbs = [p.astype(BF16) for p in ps]
    for h, (sl, ds, pb) in enumerate(zip(MEM_SLICES, dss, pbs)):
        slv = slice(MEM_WIDTH + h * HEAD_DIM, MEM_WIDTH + (h + 1) * HEAD_DIM)
        dqm_ref[:, sl] = _dot(ds, km[:, sl]).astype(BF16)
        dkv_ref[:, sl] += _dot_tn(ds, qm[:, sl])
        dkv_ref[:, slv] += _dot_tn(pb, dmb[:, sl])


def _post_attn(os_, ls_, qm, kv, z, x, wg_out):
    s, d_model = x.shape
    gw = GROUP_WIDTH
    nb = gw + MEM_WIDTH
    c = wg_out.shape[2]
    tm = ROW_TILE

    def body(o0, o1, o2, l0, l1, l2, qm_ref, kv_ref, z_ref, x_ref, w_ref, y_ref, yt_ref, mo_ref, h_ref, s0, s1):
        ov, lv = [], []
        for o_ref, l_ref, d in zip((o0, o1, o2), (l0, l1, l2), DILATIONS):
            ov.append(_from_view(s0, o_ref, d))
            lv.append(_from_view(s1, l_ref, d))
        _, mix = _mix_groups(ov, lv)
        _mem_attn_into(qm_ref[...], kv_ref, mo_ref)
        sz, _ = _silu_parts(z_ref[...])
        y_ref[:, :gw] = (mix * sz[:, :gw]).astype(BF16)
        y_ref[:, gw:] = (mo_ref[...] * sz[:, gw:]).astype(BF16)
        y = y_ref[...]
        yt_ref[...] = y.T
        for j in range(N_DEV):
            h_ref[:, j * c:(j + 1) * c] = x_ref[:, j * c:(j + 1) * c] + _dot(y, w_ref[j])

    vspecs = [_view_rows(gw, d) for d in DILATIONS]
    return _pallas_call(
        body, name="post_attn", grid=(s // tm,),
        out_shape=[_sds((s, nb), BF16), _sds((nb, s), BF16), _sds((s, MEM_WIDTH), F32), _sds((s, d_model), F32)],
        in_specs=vspecs * 2 + [_rows(MEM_WIDTH), _whole(kv.shape), _rows(nb), _rows(d_model), _whole(wg_out.shape)],
        out_specs=[_rows(nb), pl.BlockSpec((nb, tm), lambda i: (0, i)), _rows(MEM_WIDTH), _rows(d_model)],
        scratch_shapes=[_view_scratch(gw), _view_scratch(gw)],
        compiler_params=_params(1, 40),
    )(*os_, *ls_, qm, kv, z, x, wg_out)


def _inproj_conv(x, g, wg):
    s, d_model = x.shape
    c = CONV_WIDTH
    n = N_DEV * wg.shape[2]
    nz = n - 3 * c - MEM_WIDTH
    tm = ROW_TILE

    def body(x_ref, g_ref, w_ref, hn_ref, hnt_ref, bg_ref, cg_ref, u_ref, qm_ref, z_ref, proj):
        xb = x_ref[...]
        r = lax.rsqrt(jnp.mean(xb * xb, axis=-1, keepdims=True) + EPS)
        hn = ((xb * r) * g_ref[...]).astype(BF16)
        hn_ref[...] = hn
        hnt_ref[...] = hn.T
        _project(hn, w_ref, proj)
        bg_ref[...] = proj[:, 0:c]
        cg_ref[...] = proj[:, c:2 * c]
        u_ref[...] = proj[:, 2 * c:3 * c]
        qm_ref[...] = proj[:, 3 * c:3 * c + MEM_WIDTH].astype(BF16)
        z_ref[...] = proj[:, 3 * c + MEM_WIDTH:]

    return _pallas_call(
        body, name="inproj_conv", grid=(s // tm,),
        out_shape=[_sds((s, d_model), BF16), _sds((d_model, s), BF16)] + [_sds((s, c), F32)] * 3
                  + [_sds((s, MEM_WIDTH), BF16), _sds((s, nz), F32)],
        in_specs=[_rows(d_model), _whole((1, d_model)), _whole(wg.shape)],
        out_specs=[_rows(d_model), pl.BlockSpec((d_model, tm), lambda i: (0, i))] + [_rows(c)] * 3
                  + [_rows(MEM_WIDTH), _rows(nz)],
        scratch_shapes=[pltpu.VMEM((tm, n), F32)],
        compiler_params=_params(1, 60),
    )(x, g, wg)


HALO = 8


def _halo_before(width, tm=ROW_TILE):
    return pl.BlockSpec((HALO, width), lambda i: (jnp.maximum(i * (tm // HALO) - 1, 0), 0))


def _halo_after(width, n_rows, tm=ROW_TILE):
    return pl.BlockSpec((HALO, width), lambda i: (jnp.minimum((i + 1) * (tm // HALO), n_rows // HALO - 1), 0))


def _conv_taps(cg_ref, u_ref, cgh_ref, uh_ref, i):
    a = cg_ref[...] * u_ref[...]
    ah = jnp.where(i > 0, cgh_ref[...] * uh_ref[...], 0.0)
    row = lax.broadcasted_iota(jnp.int32, a.shape, 0)
    a1 = jnp.where(row == 0, ah[HALO - 1:HALO], pltpu.roll(a, 1, 0))
    a2 = jnp.where(row == 0, ah[HALO - 2:HALO - 1], jnp.where(row == 1, ah[HALO - 1:HALO], pltpu.roll(a, 2, 0)))
    return a, a1, a2


def _post_conv_loss(bg, cg, u, qm, kv, z, h1, w_out, cw, gf, tgt):
    s, d = h1.shape
    c = CONV_WIDTH
    nb = c + MEM_WIDTH
    tm = ROW_TILE

    def body(bg_ref, cg_ref, u_ref, cgh_ref, uh_ref, qm_ref, kv_ref, z_ref, h_ref, w_ref, cw_ref, gf_ref, t_ref,
             y_ref, yt_ref, mo_ref, dh_ref, dhb_ref, loss_ref, dgf_ref):
        i = pl.program_id(0)
        a, a1, a2 = _conv_taps(cg_ref, u_ref, cgh_ref, uh_ref, i)
        conv = cw_ref[0:1, :] * a2 + cw_ref[1:2, :] * a1 + cw_ref[2:3, :] * a
        mix = bg_ref[...] * conv
        _mem_attn_into(qm_ref[...], kv_ref, mo_ref)
        sz, _ = _silu_parts(z_ref[...])
        y_ref[:, :c] = (mix * sz[:, :c]).astype(BF16)
        y_ref[:, c:] = (mo_ref[...] * sz[:, c:]).astype(BF16)
        y = y_ref[...]
        yt_ref[...] = y.T
        h2 = h_ref[...] + _dot(y, w_ref[...])
        r = lax.rsqrt(jnp.mean(h2 * h2, axis=-1, keepdims=True) + EPS)
        nh = h2 * r
        gfv = gf_ref[...]
        diff = nh * gfv - t_ref[...]
        dout = diff * (1.0 / d)
        dn = dout * gfv
        dh2 = r * dn - h2 * ((r * r * r) * jnp.mean(dn * h2, axis=-1, keepdims=True))
        dh_ref[...] = dh2
        dhb_ref[...] = dh2.astype(BF16)

        @pl.when(i == 0)
        def _():
            loss_ref[...] = jnp.zeros_like(loss_ref)
            dgf_ref[...] = jnp.zeros_like(dgf_ref)

        loss_ref[...] += 0.5 * jnp.sum(jnp.mean(diff * diff, axis=-1, keepdims=True))
        dgf_ref[...] += jnp.sum(dout * nh, axis=0, keepdims=True)

    return _pallas_call(
        body, name="post_conv_loss", grid=(s // tm,),
        out_shape=[_sds((s, nb), BF16), _sds((nb, s), BF16), _sds((s, MEM_WIDTH), F32), _sds((s, d), F32),
                   _sds((s, d), BF16), _plain((8, LANES), F32), _plain((1, d), F32)],
        in_specs=[_rows(c)] * 3 + [_halo_before(c)] * 2 + [_rows(MEM_WIDTH), _whole(kv.shape), _rows(nb), _rows(d),
                  _whole(w_out.shape), _whole(cw.shape), _whole((1, d)), _rows(d)],
        out_specs=[_rows(nb), pl.BlockSpec((nb, tm), lambda i: (0, i)), _rows(MEM_WIDTH), _rows(d), _rows(d),
                   _whole((8, LANES)), _whole((1, d))],
        compiler_params=_params(1, 48),
    )(bg, cg, u, cg, u, qm, kv, z, h1, w_out, cw, gf, tgt)


def _bwd_post_conv(dhb, w_out, bg, cg, u, z, qm, kv, mo, cw):
    s = dhb.shape[0]
    c = CONV_WIDTH
    nb = c + MEM_WIDTH

    def body(dh_ref, w_ref, bg_ref, cg_ref, u_ref, cgh_ref, uh_ref, z_ref, qm_ref, kv_ref, mo_ref, cw_ref,
             dz_ref, dbg_ref, dc_ref, dqm_ref, dkv_ref):
        i = pl.program_id(0)

        @pl.when(i == 0)
        def _():
            dkv_ref[...] = jnp.zeros_like(dkv_ref)

        dy = _dot_nt(dh_ref[...], w_ref[...])
        sz, dsz = _silu_parts(z_ref[...])
        a, a1, a2 = _conv_taps(cg_ref, u_ref, cgh_ref, uh_ref, i)
        conv = cw_ref[0:1, :] * a2 + cw_ref[1:2, :] * a1 + cw_ref[2:3, :] * a
        bgv = bg_ref[...]
        dz_ref[:, :c] = (dy[:, :c] * (bgv * conv) * dsz[:, :c]).astype(BF16)
        dz_ref[:, c:] = (dy[:, c:] * mo_ref[...] * dsz[:, c:]).astype(BF16)
        dbr = dy * sz
        dmix = dbr[:, :c]
        dbg_ref[...] = (dmix * conv).astype(BF16)
        dc_ref[...] = dmix * bgv
        _mem_attn_bwd(qm_ref[...], kv_ref, dbr[:, c:], dqm_ref, dkv_ref)

    return _pallas_call(
        body, name="bwd_post_conv", grid=(s // ROW_TILE,),
        out_shape=[_sds((s, nb), BF16), _sds((s, c), BF16), _sds((s, c), F32), _sds((s, MEM_WIDTH), BF16),
                   _plain(kv.shape, F32)],
        in_specs=[_rows(D_MODEL), _whole(w_out.shape)] + [_rows(c)] * 3 + [_halo_before(c)] * 2
                 + [_rows(nb), _rows(MEM_WIDTH), _whole(kv.shape), _rows(MEM_WIDTH), _whole(cw.shape)],
        out_specs=[_rows(nb), _rows(c), _rows(c), _rows(MEM_WIDTH), _whole(kv.shape)],
        compiler_params=_params(1, 48),
    )(dhb, w_out, bg, cg, u, cg, u, z, qm, kv, mo, cw)


def _bwd_conv(dconv, cg, u, cw):
    s, c = dconv.shape
    tm = ROW_TILE
    last = s // tm - 1

    def body(dc_ref, dcn_ref, cg_ref, u_ref, cgh_ref, uh_ref, cw_ref, dcg_ref, du_ref, dcw_ref):
        i = pl.program_id(0)

        @pl.when(i == 0)
        def _():
            dcw_ref[...] = jnp.zeros_like(dcw_ref)

        dc = dc_ref[...]
        dcn = jnp.where(i < last, dcn_ref[...], 0.0)
        row = lax.broadcasted_iota(jnp.int32, dc.shape, 0)
        d1 = jnp.where(row == tm - 1, dcn[0:1], pltpu.roll(dc, tm - 1, 0))
        d2 = jnp.where(row == tm - 1, dcn[1:2], jnp.where(row == tm - 2, dcn[0:1], pltpu.roll(dc, tm - 2, 0)))
        da = cw_ref[2:3, :] * dc + cw_ref[1:2, :] * d1 + cw_ref[0:1, :] * d2
        a, a1, a2 = _conv_taps(cg_ref, u_ref, cgh_ref, uh_ref, i)
        dcg_ref[...] = (da * u_ref[...]).astype(BF16)
        du_ref[...] = (da * cg_ref[...]).astype(BF16)
        dcw_ref[0:1, :] += jnp.sum(dc * a2, axis=0, keepdims=True)
        dcw_ref[1:2, :] += jnp.sum(dc * a1, axis=0, keepdims=True)
        dcw_ref[2:3, :] += jnp.sum(dc * a, axis=0, keepdims=True)

    return _pallas_call(
        body, name="bwd_conv", grid=(s // tm,),
        out_shape=[_sds((s, c), BF16), _sds((s, c), BF16), _plain((8, c), F32)],
        in_specs=[_rows(c), _halo_after(c, s), _rows(c), _rows(c), _halo_before(c), _halo_before(c), _whole(cw.shape)],
        out_specs=[_rows(c), _rows(c), _whole((8, c))], compiler_params=_params(1, 40),
    )(dconv, dconv, cg, u, cg, u, cw)


def _dgrad_norm(pieces, wg, h, g, dres, name):
    s, d_model = h.shape
    c = wg.shape[2]
    n = N_DEV * c
    tm = ROW_TILE
    widths = [p.shape[1] // d for p, d in pieces]
    assert sum(widths) == n
    n_p = len(pieces)

    def body(*refs):
        p_refs = refs[:n_p]
        w_ref, h_ref, g_ref, dr_ref, dh_ref, dhb_ref, dg_ref, dpd_ref, dp, scr = refs[n_p:]

        @pl.when(pl.program_id(0) == 0)
        def _():
            dg_ref[...] = jnp.zeros_like(dg_ref)

        off = 0
        for p_ref, (_, d), wd in zip(p_refs, pieces, widths):
            if d == 1:
                dp[:, off:off + wd] = p_ref[...]
            else:
                dp[:, off:off + wd] = _from_view(scr, p_ref, d).astype(BF16)
            off += wd
        dhn = jnp.zeros((tm, d_model), F32)
        for j in range(N_DEV):
            dpj = dp[:, j * c:(j + 1) * c]
            dpd_ref[j] = dpj
            dhn += _dot_nt(dpj, w_ref[j])
        hb = h_ref[...]
        r = lax.rsqrt(jnp.mean(hb * hb, axis=-1, keepdims=True) + EPS)
        dg_ref[...] += jnp.sum(dhn * (hb * r), axis=0, keepdims=True)
        dn = dhn * g_ref[...]
        dh = dr_ref[...] + r * dn - hb * ((r * r * r) * jnp.mean(dn * hb, axis=-1, keepdims=True))
        dh_ref[...] = dh
        dhb_ref[...] = dh.astype(BF16)

    p_specs = [_view_rows(wd, d) for (_, d), wd in zip(pieces, widths)]
    return _pallas_call(
        body, name=name, grid=(s // tm,),
        out_shape=[_plain((s, d_model), F32), _sds((s, d_model), BF16), _plain((1, d_model), F32), _sds((N_DEV, s, c), BF16)],
        in_specs=p_specs + [_whole(wg.shape), _rows(d_model), _whole((1, d_model)), _rows(d_model)],
        out_specs=[_rows(d_model), _rows(d_model), _whole((1, d_model)), pl.BlockSpec((N_DEV, tm, c), lambda i: (0, i, 0))],
        scratch_shapes=[pltpu.VMEM((tm, n), BF16), _view_scratch(GROUP_WIDTH)],
        compiler_params=_params(1, 60),
    )(*[p for p, _ in pieces], wg, h, g, dres)


def _assemble_dproj(pieces, c, name):
    n = N_DEV * c
    tm = ROW_TILE
    widths = [p.shape[1] // d for p, d in pieces]
    assert sum(widths) == n
    s = pieces[0][0].shape[0] * pieces[0][1]
    n_p = len(pieces)

    def body(*refs):
        p_refs, (dpd_ref, dp, scr) = refs[:n_p], refs[n_p:]
        off = 0
        for p_ref, (_, d), wd in zip(p_refs, pieces, widths):
            if d == 1:
                dp[:, off:off + wd] = p_ref[...]
            else:
                dp[:, off:off + wd] = _from_view(scr, p_ref, d).astype(BF16)
            off += wd
        for j in range(N_DEV):
            dpd_ref[j] = dp[:, j * c:(j + 1) * c]

    return _pallas_call(
        body, name=name, grid=(s // tm,), out_shape=_sds((N_DEV, s, c), BF16),
        in_specs=[_view_rows(wd, d) for (_, d), wd in zip(pieces, widths)],
        out_specs=pl.BlockSpec((N_DEV, tm, c), lambda i: (0, i, 0)),
        scratch_shapes=[pltpu.VMEM((tm, n), BF16), _view_scratch(GROUP_WIDTH)],
        compiler_params=_params(1, 40),
    )(*[p for p, _ in pieces])


def _dgrad_norm_dm(dproj_dm, wg, h, g, dres, token, name):
    s, d_model = h.shape
    c = wg.shape[2]
    tm = ROW_TILE

    def body(_, dp_ref, w_ref, h_ref, g_ref, dr_ref, dh_ref, dg_ref):
        @pl.when(pl.program_id(0) == 0)
        def _():
            dg_ref[...] = jnp.zeros_like(dg_ref)

        dhn = jnp.zeros((tm, d_model), F32)
        for j in range(N_DEV):
            dhn += _dot_nt(dp_ref[j], w_ref[j])
        hb = h_ref[...]
        r = lax.rsqrt(jnp.mean(hb * hb, axis=-1, keepdims=True) + EPS)
        dg_ref[...] += jnp.sum(dhn * (hb * r), axis=0, keepdims=True)
        dn = dhn * g_ref[...]
        dh_ref[...] = dr_ref[...] + r * dn - hb * ((r * r * r) * jnp.mean(dn * hb, axis=-1, keepdims=True))

    return _pallas_call(
        body, name=name, grid=(s // tm,),
        out_shape=[_plain((s, d_model), F32), _plain((1, d_model), F32)],
        in_specs=[ANY, pl.BlockSpec((N_DEV, tm, c), lambda i: (0, i, 0)), _whole(wg.shape), _rows(d_model),
                  _whole((1, d_model)), _rows(d_model)],
        out_specs=[_rows(d_model), _whole((1, d_model))],
        compiler_params=_params(1, 60),
    )(token, dproj_dm, wg, h, g, dres)


def _wgrad_shards(a_t, b_dm, name):
    m, s = a_t.shape
    c = b_dm.shape[2]

    def body(a_ref, b_ref, o_ref):
        o_ref[...] = _dot(a_ref[...], b_ref[...]).astype(BF16)

    return _pallas_call(
        body, name=name, grid=(N_DEV,), out_shape=_sds((N_DEV, m, c), BF16),
        in_specs=[_whole(a_t.shape), pl.BlockSpec((None, s, c), lambda j: (j, 0, 0))],
        out_specs=pl.BlockSpec((None, m, c), lambda j: (j, 0, 0)), compiler_params=_params(1, 40),
    )(a_t, b_dm)


def _wgrad_cols(a_t, b, c, name):
    m, s = a_t.shape

    def body(a_ref, b_ref, o_ref):
        o_ref[...] = _dot(a_ref[...], b_ref[...]).astype(BF16)

    return _pallas_call(
        body, name=name, grid=(N_DEV,), out_shape=_sds((N_DEV, m, c), BF16),
        in_specs=[_whole(a_t.shape), pl.BlockSpec((s, c), lambda j: (0, j))],
        out_specs=pl.BlockSpec((None, m, c), lambda j: (j, 0, 0)), compiler_params=_params(1, 40),
    )(a_t, b)


def _wgrad_rows(a_t, b, name):
    m, s = a_t.shape
    n = b.shape[1]
    mr = m // N_DEV

    def body(a_ref, b_ref, o_ref):
        o_ref[...] = _dot(a_ref[...], b_ref[...]).astype(BF16)

    return _pallas_call(
        body, name=name, grid=(N_DEV,), out_shape=_sds((N_DEV, mr, n), BF16),
        in_specs=[pl.BlockSpec((mr, s), lambda j: (j, 0)), _whole(b.shape)],
        out_specs=pl.BlockSpec((None, mr, n), lambda j: (j, 0, 0)), compiler_params=_params(1, 40),
    )(a_t, b)


def _memkv_bwd(dkv, w, mem, g):
    n_layers = w.shape[0]

    def body(dkv_ref, w_ref, mem_ref, g_ref, dw_ref, dg_ref):
        mb = mem_ref[...]
        r = lax.rsqrt(jnp.mean(mb * mb, axis=-1, keepdims=True) + EPS)
        nm = mb * r
        mn = (nm * g_ref[...]).astype(BF16)
        dkvb = dkv_ref[...].astype(BF16)
        dw_ref[...] = _dot_tn(mn, dkvb).astype(BF16)
        dmn = _dot_nt(dkvb, w_ref[...])
        dg_ref[...] = jnp.sum(dmn * nm, axis=0, keepdims=True)

    lay = lambda *shape: pl.BlockSpec((None,) + shape, lambda l: (l, 0, 0))
    return _pallas_call(
        body, name="memkv_bwd", grid=(n_layers,),
        out_shape=[_plain((n_layers, D_MODEL, 2 * MEM_WIDTH), BF16), _plain((n_layers, 1, D_MODEL), F32)],
        in_specs=[lay(N_MEM, 2 * MEM_WIDTH), lay(D_MODEL, 2 * MEM_WIDTH), _whole(mem.shape), lay(1, D_MODEL)],
        out_specs=[lay(D_MODEL, 2 * MEM_WIDTH), lay(1, D_MODEL)], compiler_params=_params(1, 32),
    )(dkv, w, mem, g.reshape(n_layers, 1, D_MODEL))


def _bwd_post_attn(dhb, wg_out, z, os_, ls_, qm, kv, mo, head_ones):
    s = dhb.shape[0]
    gw = GROUP_WIDTH
    nb = gw + MEM_WIDTH
    c = wg_out.shape[2]
    tm = ROW_TILE

    def body(dh_ref, w_ref, z_ref, o0, o1, o2, l0, l1, l2, qm_ref, kv_ref, mo_ref, bd_ref,
             dz_ref, do0, do1, do2, dl0, dl1, dl2, dqm_ref, dkv_ref, s0, s1):
        @pl.when(pl.program_id(0) == 0)
        def _():
            dkv_ref[...] = jnp.zeros_like(dkv_ref)

        dy = jnp.zeros((tm, nb), F32)
        for j in range(N_DEV):
            dy += _dot_nt(dh_ref[:, j * c:(j + 1) * c], w_ref[j])
        ov, lv = [], []
        for o_ref, l_ref, d in zip((o0, o1, o2), (l0, l1, l2), DILATIONS):
            ov.append(_from_view(s0, o_ref, d))
            lv.append(_from_view(s1, l_ref, d))
        ws, mix = _mix_groups(ov, lv)
        sz, dsz = _silu_parts(z_ref[...])
        dz_ref[:, :gw] = (dy[:, :gw] * mix * dsz[:, :gw]).astype(BF16)
        dz_ref[:, gw:] = (dy[:, gw:] * mo_ref[...] * dsz[:, gw:]).astype(BF16)
        dbr = dy * sz
        dmix = dbr[:, :gw]
        t = dmix * mix
        th = t.astype(BF16)
        tl = (t - th.astype(F32)).astype(BF16)
        rs = _dot(th, bd_ref[...]) + _dot(tl, bd_ref[...])
        for wg_, do_ref, dl_ref, d in zip(ws, (do0, do1, do2), (dl0, dl1, dl2), DILATIONS):
            _to_view(s0, wg_ * dmix, do_ref, d)
            _to_view(s1, wg_ * rs, dl_ref, d)
        _mem_attn_bwd(qm_ref[...], kv_ref, dbr[:, gw:], dqm_ref, dkv_ref)

    vspecs = [_view_rows(gw, d) for d in DILATIONS]
    return _pallas_call(
        body, name="bwd_post_attn", grid=(s // tm,),
        out_shape=[_sds((s, nb), BF16)] + [_sds((s // d, d * gw), BF16) for d in DILATIONS]
                  + [_sds((s // d, d * gw), F32) for d in DILATIONS] + [_sds((s, MEM_WIDTH), BF16), _plain(kv.shape, F32)],
        in_specs=[_rows(D_MODEL), _whole(wg_out.shape), _rows(nb)] + vspecs * 2
                 + [_rows(MEM_WIDTH), _whole(kv.shape), _rows(MEM_WIDTH), _whole(head_ones.shape)],
        out_specs=[_rows(nb)] + vspecs * 2 + [_rows(MEM_WIDTH), _whole(kv.shape)],
        scratch_shapes=[_view_scratch(gw), _view_scratch(gw)],
        compiler_params=_params(1, 48),
    )(dhb, wg_out, z, *os_, *ls_, qm, kv, mo, head_ones)


def _attn_bwd(q, k, v, lse, do, dl, tabs, d, token):
    ln, dw = q.shape
    w = dw // d
    nb = ln // BLOCK
    nblk = d * nb
    reps = w // LANES
    cur, prev = _stream_maps(d, nb)
    qmap = lambda n: cur(jnp.minimum(n, nblk - 1))
    pmap = lambda n: prev(jnp.minimum(n, nblk - 1))
    omap = lambda n: cur(jnp.maximum(n - 1, 0))

    def body(_, q_ref, kp_ref, kc_ref, vp_ref, vc_ref, l_ref, do_ref, dl_ref, cq, saq, sbq, ck, sak, sbk,
             dq_ref, dk_ref, dv_ref, acck, accv, dqs):
        n = pl.program_id(0)

        @pl.when(n == 0)
        def _():
            acck[...] = jnp.zeros_like(acck)
            accv[...] = jnp.zeros_like(accv)

        def run(with_prev):
            if with_prev:
                kk = jnp.concatenate([kp_ref[...], kc_ref[...]], axis=0)
                vv = jnp.concatenate([vp_ref[...], vc_ref[...]], axis=0)
                rows = slice(0, 2 * BLOCK)
            else:
                kk, vv = kc_ref[...], vc_ref[...]
                rows = slice(BLOCK, 2 * BLOCK)
            valid = _band_mask(kk.shape[0])
            low = _low_head_lanes()
            pairs = [slice(p * LANES, (p + 1) * LANES) for p in range(w // LANES)]
            cols = [slice(h * HEAD_DIM, h * HEAD_DIM + 1) for h in range(HEADS_PER_GROUP)]
            qhs = [h for pr in pairs for h in _split_pair(q_ref[:, pr], low)]
            dobs = [h for pr in pairs for h in _split_pair(do_ref[:, pr], low)]
            k2s = [kk[:, pr] for pr in pairs for _ in range(2)]
            v2s = [vv[:, pr] for pr in pairs for _ in range(2)]
            scs = [jnp.where(valid, _dot_nt(qh, k2), NEG) for qh, k2 in zip(qhs, k2s)]
            dps = [_dot_nt(dob, v2) for dob, v2 in zip(dobs, v2s)]
            ps = [jnp.exp(sc - l_ref[:, col]) for sc, col in zip(scs, cols)]
            dss = [(p * (dp - dl_ref[:, col])).astype(BF16) for p, dp, col in zip(ps, dps, cols)]
            pbs = [p.astype(BF16) for p in ps]
            for i, pr in enumerate(pairs):
                a, b = 2 * i, 2 * i + 1
                dqs[:, pr] = jnp.where(low, _dot(dss[a], k2s[a]), _dot(dss[b], k2s[b])) * SCALE
                acck[rows, pr] += _dot_tn(dss[a], qhs[a]) + _dot_tn(dss[b], qhs[b])
                accv[rows, pr] += _dot_tn(pbs[a], dobs[a]) + _dot_tn(pbs[b], dobs[b])
            tq = [jnp.tile(r[...], (1, reps)) for r in (cq, saq, sbq)]
            dq_ref[...] = _rope_bwd(dqs[...], *tq).astype(BF16)

        pl.when(n < nblk)(lambda: _per_stream_block(n, nb, run))

        tk = [jnp.tile(r[...], (1, reps)) for r in (ck, sak, sbk)]
        dk_ref[...] = _rope_bwd(acck[0:BLOCK, :], *tk).astype(BF16)
        dv_ref[...] = accv[0:BLOCK, :].astype(BF16)
        acck[0:BLOCK, :] = acck[BLOCK:, :]
        accv[0:BLOCK, :] = accv[BLOCK:, :]
        acck[BLOCK:, :] = jnp.zeros((BLOCK, w), F32)
        accv[BLOCK:, :] = jnp.zeros((BLOCK, w), F32)

    blk = lambda f: pl.BlockSpec((BLOCK, w), f)
    tblk = lambda f: pl.BlockSpec((BLOCK, LANES), f)
    return _pallas_call(
        body, name=f"attn_bwd_d{d}", grid=(nblk + 1,),
        out_shape=[_sds((ln, dw), BF16)] * 3,
        in_specs=[ANY, blk(qmap), blk(pmap), blk(qmap), blk(pmap), blk(qmap), blk(qmap), blk(qmap), blk(qmap)]
                 + [tblk(qmap)] * 3 + [tblk(omap)] * 3,
        out_specs=[blk(qmap), blk(omap), blk(omap)],
        scratch_shapes=[pltpu.VMEM((2 * BLOCK, w), F32), pltpu.VMEM((2 * BLOCK, w), F32), pltpu.VMEM((BLOCK, w), F32)],
        compiler_params=_params(1, 32),
    )(token, q, k, k, v, v, lse, do, dl, *tabs, *tabs)


def _position():
    return lax.axis_index("x"), lax.axis_index("y"), lax.axis_index("c")


def _all_gather(shards, name):
    n_a = len(shards)

    def body(*refs):
        x_refs, out_refs = refs[:n_a], refs[n_a:2 * n_a]
        send_sems, recv_sems, local_sems = refs[2 * n_a:]
        x, y, c = _position()
        me, sibling = (x, y, c), (x, y, 1 - c)
        chips = [(1 - x, y), (x, 1 - y), (1 - x, 1 - y)]

        def rows(a, px, py, pc):
            return out_refs[a].at[4 * px + 2 * py + pc]

        def copy(a, k, block, to, own=False):
            return pltpu.make_async_remote_copy(
                src_ref=x_refs[a] if own else rows(a, *block), dst_ref=rows(a, *block),
                send_sem=send_sems.at[a, k], recv_sem=recv_sems.at[a, k], device_id=to, device_id_type=MESH)

        mine = [pltpu.make_async_copy(x_refs[a], rows(a, *me), local_sems.at[a]) for a in range(n_a)]
        for cp in mine:
            cp.start()
        first = []
        for j, chip in enumerate(chips):
            first += [copy(a, 1 + j, me, (*chip, c), own=True) for a in range(n_a)]
        first += [copy(a, 0, me, sibling, own=True) for a in range(n_a)]
        for cp in first:
            cp.start()
        passed = []
        for j, chip in enumerate(chips):
            for a in range(n_a):
                copy(a, 1 + j, (*chip, c), me).wait_recv()
                fwd = copy(a, 4 + j, (*chip, c), sibling)
                fwd.start()
                passed.append(fwd)
        for a in range(n_a):
            copy(a, 0, sibling, me).wait_recv()
        for j, chip in enumerate(chips):
            for a in range(n_a):
                copy(a, 4 + j, (*chip, 1 - c), me).wait_recv()
        for cp in first + passed:
            cp.wait_send()
        for cp in mine:
            cp.wait()

    return _pallas_call(
        body, name=name, out_shape=[_sds((N_DEV,) + t.shape, t.dtype) for t in shards],
        in_specs=[ANY] * n_a, out_specs=[ANY] * n_a,
        scratch_shapes=[pltpu.SemaphoreType.DMA((n_a, 7)), pltpu.SemaphoreType.DMA((n_a, 7)),
                        pltpu.SemaphoreType.DMA((n_a,))],
    )(*shards)


def _all_gather_relay(xs, name):
    def body(x_ref, out_ref, send_sems, recv_sems, local_sem):
        x, y, c = _position()
        me, sibling = (x, y, c), (x, y, 1 - c)
        xn, yn, diag = (1 - x, y, c), (x, 1 - y, c), (1 - x, 1 - y, c)
        src_nb = (x + c * (1 - 2 * x), y + (1 - c) * (1 - 2 * y), c)
        dst_nb = (x + (1 - c) * (1 - 2 * x), y + c * (1 - 2 * y), c)

        def rows(dev):
            return out_ref.at[4 * dev[0] + 2 * dev[1] + dev[2]]

        def copy(k, block, to, own=False):
            return pltpu.make_async_remote_copy(
                src_ref=x_ref if own else rows(block), dst_ref=rows(block),
                send_sem=send_sems.at[k], recv_sem=recv_sems.at[k], device_id=to, device_id_type=MESH)

        mine = pltpu.make_async_copy(x_ref, rows(me), local_sem)
        mine.start()
        first = [copy(1, me, xn, own=True), copy(2, me, yn, own=True), copy(0, me, sibling, own=True)]
        for cp in first:
            cp.start()
        copy(1, xn, me).wait_recv()
        copy(2, yn, me).wait_recv()
        relay = copy(3, src_nb, dst_nb)
        relay.start()
        passed = [copy(4, xn, sibling), copy(5, yn, sibling)]
        for cp in passed:
            cp.start()
        copy(3, diag, me).wait_recv()
        last = copy(6, diag, sibling)
        last.start()
        copy(0, sibling, me).wait_recv()
        for k, blk in ((4, (1 - x, y, 1 - c)), (5, (x, 1 - y, 1 - c)), (6, (1 - x, 1 - y, 1 - c))):
            copy(k, blk, me).wait_recv()
        for cp in first + [relay] + passed + [last]:
            cp.wait_send()
        mine.wait()

    return _pallas_call(
        body, name=name, out_shape=_sds((N_DEV,) + xs.shape, xs.dtype),
        in_specs=[ANY], out_specs=ANY,
        scratch_shapes=[pltpu.SemaphoreType.DMA((7,)), pltpu.SemaphoreType.DMA((7,)), pltpu.SemaphoreType.DMA],
    )(xs)


def _rs_to_sibling(gs):
    n_a = len(gs)

    def body(*refs):
        g_refs, recv_refs = refs[:n_a], refs[n_a:2 * n_a]
        send_sems, recv_sems = refs[2 * n_a:]
        x, y, c = _position()
        copies = []
        for k in range(4):
            for a in range(n_a):
                copies.append(pltpu.make_async_remote_copy(
                    src_ref=g_refs[a].at[2 * k + (1 - c)], dst_ref=recv_refs[a].at[k],
                    send_sem=send_sems.at[a, k], recv_sem=recv_sems.at[a, k],
                    device_id=(x, y, 1 - c), device_id_type=MESH))
        for cp in copies:
            cp.start()
        for cp in copies:
            cp.wait()

    return _pallas_call(
        body, name="rs_to_sibling", out_shape=[_sds((4,) + g.shape[1:], g.dtype) for g in gs],
        in_specs=[ANY] * n_a, out_specs=[ANY] * n_a,
        scratch_shapes=[pltpu.SemaphoreType.DMA((n_a, 4)), pltpu.SemaphoreType.DMA((n_a, 4))],
    )(*gs)


def _rs_to_chips(pbs):
    n_a = len(pbs)

    def body(*refs):
        p_refs, recv_refs = refs[:n_a], refs[n_a:2 * n_a]
        send_sems, recv_sems = refs[2 * n_a:]
        x, y, c = _position()
        chips = [(1 - x, y), (x, 1 - y), (1 - x, 1 - y)]
        copies = []
        for j, (px, py) in enumerate(chips):
            for a in range(n_a):
                copies.append(pltpu.make_async_remote_copy(
                    src_ref=p_refs[a].at[2 * px + py], dst_ref=recv_refs[a].at[j],
                    send_sem=send_sems.at[a, j], recv_sem=recv_sems.at[a, j],
                    device_id=(px, py, c), device_id_type=MESH))
        for cp in copies:
            cp.start()
        for cp in copies:
            cp.wait()

    return _pallas_call(
        body, name="rs_to_chips", out_shape=[_sds((3,) + p.shape[1:], p.dtype) for p in pbs],
        in_specs=[ANY] * n_a, out_specs=[ANY] * n_a,
        scratch_shapes=[pltpu.SemaphoreType.DMA((n_a, 3)), pltpu.SemaphoreType.DMA((n_a, 3))],
    )(*pbs)


HBM_SPEC = pl.BlockSpec(memory_space=pltpu.HBM)
SEM_SPEC = pl.BlockSpec(memory_space=pltpu.SEMAPHORE)
EFFECT = pltpu.SideEffectType.DATAFLOW_SIDE_EFFECTING
def _plan_gather_own(src_refs, land_refs):
    x, y, c = _position()
    me = 4 * x + 2 * y + c
    peers = [(x, y, 1 - c), (1 - x, y, c), (x, 1 - y, c), (1 - x, 1 - y, c)]
    return [(src_refs[a], land_refs[a].at[me], (a, k), peer) for k, peer in enumerate(peers) for a in range(len(src_refs))]


def _plan_gather_pass(src_refs, land_refs):
    x, y, c = _position()
    chips = [(1 - x, y), (x, 1 - y), (1 - x, 1 - y)]
    return [(land_refs[a].at[4 * px + 2 * py + c], land_refs[a].at[4 * px + 2 * py + c], (a, j), (x, y, 1 - c))
            for j, (px, py) in enumerate(chips) for a in range(len(land_refs))]


def _plan_to_sibling(src_refs, land_refs):
    x, y, c = _position()
    return [(src_refs[a].at[2 * k + (1 - c)], land_refs[a].at[k], (a, k), (x, y, 1 - c))
            for k in range(4) for a in range(len(src_refs))]


def _plan_to_chips(src_refs, land_refs):
    x, y, c = _position()
    chips = [(1 - x, y), (x, 1 - y), (1 - x, 1 - y)]
    return [(src_refs[a].at[2 * px + py], land_refs[a].at[j], (a, j), (px, py, c))
            for j, (px, py) in enumerate(chips) for a in range(len(src_refs))]


def _split_start(srcs, lands, plan, n_sem, after, name):
    n_s, n_a = len(srcs), len(lands)
    n_b = n_s + n_a

    def body(*refs):
        src_refs, land_refs = refs[:n_s], refs[n_s:n_b]
        send_sems, recv_sems, token = refs[n_b + 1], refs[n_b + 2], refs[-1]
        for src, dst, (a, k), dev in plan(src_refs, land_refs):
            i = a * n_sem + k
            pltpu.make_async_remote_copy(src_ref=src, dst_ref=dst, send_sem=send_sems.at[i], recv_sem=recv_sems.at[i],
                                         device_id=dev, device_id_type=MESH).start()
        token[...] = jnp.zeros_like(token)

    bufs = list(srcs) + list(lands)
    res = pl.pallas_call(
        body, name=name,
        out_shape=(pltpu.SemaphoreType.DMA((n_a * n_sem,)), pltpu.SemaphoreType.DMA((n_a * n_sem,)),
                   *[pltpu.HBM(t.shape, t.dtype) for t in bufs], _plain((8, LANES), F32)),
        in_specs=[HBM_SPEC] * n_b + [ANY],
        out_specs=(SEM_SPEC, SEM_SPEC, *[HBM_SPEC] * n_b, pl.BlockSpec(memory_space=pltpu.VMEM)),
        input_output_aliases={i: 2 + i for i in range(n_b)},
        compiler_params=pltpu.CompilerParams(has_side_effects=EFFECT),
    )(*[pltpu.with_memory_space_constraint(t, pltpu.HBM) for t in bufs], after)
    return (res[0], res[1], res[2:2 + n_s], res[2 + n_s:2 + n_b]), res[-1]


def _split_wait(started, plan, after, name):
    send_sems, recv_sems, srcs, lands = started
    n_s, n_a = len(srcs), len(lands)
    n_b = n_s + n_a
    n_sem = send_sems.shape[0] // n_a

    def body(*refs):
        src_refs, land_refs = refs[:n_s], refs[n_s:n_b]
        s_sems, r_sems = refs[n_b], refs[n_b + 1]
        for src, dst, (a, k), dev in plan(src_refs, land_refs):
            i = a * n_sem + k
            cp = pltpu.make_async_remote_copy(src_ref=src, dst_ref=dst, send_sem=s_sems.at[i], recv_sem=r_sems.at[i],
                                              device_id=dev, device_id_type=MESH)
            cp.wait_send()
            cp.wait_recv()

    bufs = list(srcs) + list(lands)
    res = pl.pallas_call(
        body, name=name, out_shape=tuple(pltpu.HBM(t.shape, t.dtype) for t in bufs),
        in_specs=[HBM_SPEC] * n_b + [SEM_SPEC, SEM_SPEC, ANY],
        out_specs=tuple([HBM_SPEC] * n_b),
        input_output_aliases={i: i for i in range(n_b)},
        compiler_params=pltpu.CompilerParams(has_side_effects=EFFECT),
    )(*bufs, send_sems, recv_sems, after)
    return res[:n_s], res[n_s:]


def _row_tile(r):
    return ROW_TILE if r % ROW_TILE == 0 else r


def _rs_add_sibling(gp, recv, c_arr, name):
    _, r, l = gp.shape
    tr = r if r <= 4 * ROW_TILE else _row_tile(r)

    def body(c_ref, g_ref, r_ref, pf_ref, pb_ref):
        sm = g_ref[...].astype(F32) + r_ref[...].astype(F32)
        pf_ref[...] = sm
        pb_ref[...] = sm.astype(BF16)

    spec = pl.BlockSpec((None, tr, l), lambda k, i, c: (k, i, 0))
    return _pallas_call(
        body, name=name,
        grid_spec=pltpu.PrefetchScalarGridSpec(
            num_scalar_prefetch=1, grid=(4, r // tr),
            in_specs=[pl.BlockSpec((None, tr, l), lambda k, i, c: (2 * k + c[0], i, 0)), spec],
            out_specs=[spec, spec]),
        out_shape=[_sds((4, r, l), F32), _sds((4, r, l), BF16)], compiler_params=_params(2, 32),
    )(c_arr, gp, recv)


def _adam_update(w, gv, m, v):
    nm = ADAM_B1 * m + (1.0 - ADAM_B1) * gv
    nv = ADAM_B2 * v + (1.0 - ADAM_B2) * (gv * gv)
    m_hat = nm / (1.0 - ADAM_B1 ** ADAM_STEP)
    v_hat = nv / (1.0 - ADAM_B2 ** ADAM_STEP)
    return -ADAM_LR * (m_hat / (jnp.sqrt(v_hat) + ADAM_EPS) + ADAM_WD * w), nm, nv


def _rs_finish_adamw(pf, recv, k_arr, w, m, v, name):
    _, r, l = pf.shape
    tr = _row_tile(r)

    def body(k_ref, p_ref, r_ref, w_ref, m_ref, v_ref, g_ref, d_ref, nm_ref, nv_ref):
        gv = ((p_ref[...] + r_ref[0].astype(F32)) + r_ref[1].astype(F32)) + r_ref[2].astype(F32)
        g_ref[...] = gv
        d_ref[...], nm_ref[...], nv_ref[...] = _adam_update(w_ref[...], gv, m_ref[...], v_ref[...])

    spec = pl.BlockSpec((tr, l), lambda i, k: (i, 0))
    return _pallas_call(
        body, name=name,
        grid_spec=pltpu.PrefetchScalarGridSpec(
            num_scalar_prefetch=1, grid=(r // tr,),
            in_specs=[pl.BlockSpec((None, tr, l), lambda i, k: (k[0], i, 0)),
                      pl.BlockSpec((3, tr, l), lambda i, k: (0, i, 0)), spec, spec, spec],
            out_specs=[spec] * 4),
        out_shape=[_plain((r, l), F32)] * 4, compiler_params=_params(1, 32),
    )(k_arr, pf, recv, w, m, v)


def _rs_finish_adamw_t(pf, recv, k_arr, w_t, m_t, v_t, name):
    _, r, c = pf.shape
    tr = _row_tile(r)
    cp = -(-c // LANES) * LANES

    def body(k_ref, p_ref, r_ref, w_ref, m_ref, v_ref, g_ref, d_ref, nm_ref, nv_ref, pad):
        gv = ((p_ref[...] + r_ref[0].astype(F32)) + r_ref[1].astype(F32)) + r_ref[2].astype(F32)
        pad[...] = jnp.zeros_like(pad)
        pad[:, 0:c] = gv
        gt = pad[...].T[0:c, :]
        g_ref[...] = gt
        d_ref[...], nm_ref[...], nv_ref[...] = _adam_update(w_ref[...], gt, m_ref[...], v_ref[...])

    spec = pl.BlockSpec((c, tr), lambda i, k: (0, i))
    return _pallas_call(
        body, name=name,
        grid_spec=pltpu.PrefetchScalarGridSpec(
            num_scalar_prefetch=1, grid=(r // tr,),
            in_specs=[pl.BlockSpec((None, tr, c), lambda i, k: (k[0], i, 0)),
                      pl.BlockSpec((3, tr, c), lambda i, k: (0, i, 0)), spec, spec, spec],
            out_specs=[spec] * 4, scratch_shapes=[pltpu.VMEM((tr, cp), F32)]),
        out_shape=[_plain((c, r), F32)] * 4, compiler_params=_params(1, 32),
    )(k_arr, pf, recv, w_t, m_t, v_t)


def _sum_devices(g):
    def body(g_ref, o_ref):
        acc = g_ref[0]
        for j in range(1, N_DEV):
            acc = acc + g_ref[j]
        o_ref[...] = acc

    return _pallas_call(body, name="sum_devices", out_shape=_plain(g.shape[1:], F32))(g)


def _adamw(w, g, m, v, name):
    shape = w.shape
    w2, g2, m2, v2 = [t.reshape((-1, shape[-1])) for t in (w, g, m, v)]

    def body(w_ref, g_ref, m_ref, v_ref, d_ref, nm_ref, nv_ref):
        d_ref[...], nm_ref[...], nv_ref[...] = _adam_update(w_ref[...], g_ref[...], m_ref[...], v_ref[...])

    outs = _pallas_call(body, name=name, out_shape=[_plain(w2.shape, F32)] * 3)(w2, g2, m2, v2)
    return tuple(t.reshape(shape) for t in outs)


def _after(t, token):
    return t + token[0:1, 0:1].astype(t.dtype)


def _finish(name, pf, recv, k_arr, w, m, v):
    if name in ("attn_w_in", "conv_w_in"):
        res = _rs_finish_adamw_t(pf, recv, k_arr, w.T, m.T, v.T, "rs_finish_adamw_" + name)
        return tuple(t.T for t in res)
    return _rs_finish_adamw(pf, recv, k_arr, w, m, v, "rs_finish_adamw_" + name)


def kernel(x, mem, positions, norm_g, mem_norm_g, w_mem_kv, attn_w_in, attn_w_out, conv_w_in, conv_w, conv_w_out, final_g, loss_target, m_norm_g, m_mem_norm_g, m_w_mem_kv, m_attn_w_in, m_attn_w_out, m_conv_w_in, m_conv_w, m_conv_w_out, m_final_g, v_norm_g, v_mem_norm_g, v_w_mem_kv, v_attn_w_in, v_attn_w_out, v_conv_w_in, v_conv_w, v_conv_w_out, v_final_g):
    px, py, pc = _position()
    me = 4 * px + 2 * py + pc
    c_arr = jnp.reshape(pc, (1,)).astype(jnp.int32)
    k_arr = jnp.reshape(2 * px + py, (1,)).astype(jnp.int32)
    x, mem, pos, tgt = x[0], mem[0], positions[0], loss_target[0]

    wg_in0 = _all_gather_relay(attn_w_in[0].astype(BF16), "gather_w_in0")
    late = [attn_w_out[0].astype(BF16), conv_w_in[0].astype(BF16), conv_w_out[0].astype(BF16),
            w_mem_kv[0].astype(BF16), w_mem_kv[1].astype(BF16), jnp.pad(conv_w[0], ((0, 5), (0, 0)))]
    lands = [lax.dynamic_update_slice(lax.empty((N_DEV,) + t.shape, t.dtype), t[None], (me, 0, 0)) for t in late]
    late_weights, late_token = _split_start(late, lands, _plan_gather_own, 4, wg_in0, "gather_late_start")

    tabs = _rope_tables(pos)
    g0, g1 = _after(norm_g[0:1], late_token), norm_g[1:2]

    hn0, hn0_t, qs, ks, vs, tabs_v, qm0, z0 = _inproj_attn(x, g0, wg_in0, tabs)
    os_, ls_ = [], []
    for j, d in enumerate(DILATIONS):
        if j == 2:
            _, lands = _split_wait(late_weights, _plan_gather_own, ls_[1], "gather_late_wait")
            late_weights, late_token = _split_start([], lands, _plan_gather_pass, 3, ls_[1], "gather_late_pass_start")
        o, l = _attn_fwd(qs[j], ks[j], vs[j], d, late_token)
        os_.append(o)
        ls_.append(l)

    _, gathered = _split_wait(late_weights, _plan_gather_pass, ls_[2], "gather_late_pass_wait")
    wg_out0, wg_in1, wg_out1, wg_kv0, wg_kv1, cw_all = gathered
    w_out1 = wg_out1.reshape(-1, wg_out1.shape[2])
    w_kv = jnp.stack([wg_kv0.reshape(-1, wg_kv0.shape[2]), wg_kv1.reshape(-1, wg_kv1.shape[2])])
    cw = cw_all[:, 0:3].transpose(1, 0, 2).reshape(3, -1)
    kv = _memkv_fwd(mem, mem_norm_g, w_kv)
    y0, y0_t, mo0, h1 = _post_attn(os_, ls_, qm0, kv[0], z0, x, wg_out0)

    hn1, hn1_t, bg, cg, u, qm1, z1 = _inproj_conv(h1, g1, wg_in1)
    y1, y1_t, mo1, dh2, dh2b, loss_acc, d_final_g = _post_conv_loss(
        bg, cg, u, qm1, kv[1], z1, h1, w_out1, cw, final_g.reshape(1, -1), tgt)

    d_w_out1 = _wgrad_rows(y1_t, dh2b, "wgrad_out1")
    dz1, dbg, dconv, dqm1, dkv1 = _bwd_post_conv(dh2b, w_out1, bg, cg, u, z1, qm1, kv[1], mo1, cw)
    dcg, du, dcw = _bwd_conv(dconv, cg, u, cw)
    dh1, dh1b, dg1, dproj1 = _dgrad_norm([(dbg, 1), (dcg, 1), (du, 1), (dqm1, 1), (dz1, 1)], wg_in1, h1, g1, dh2,
                                         "dgrad_norm_conv")
    d_w_in1 = _wgrad_shards(hn1_t, dproj1, "wgrad_in1")

    d_w_out0 = _wgrad_cols(y0_t, dh1b, wg_out0.shape[2], "wgrad_out0")

    names1 = ["conv_w_in", "conv_w_out", "attn_w_out"]
    grads1 = [d_w_in1, d_w_out1, d_w_out0]
    started, token = _split_start(grads1, [lax.empty((4,) + g.shape[1:], g.dtype) for g in grads1],
                                  _plan_to_sibling, 4, dg1, "rs1_sibling_start")

    gw = GROUP_WIDTH
    ones = (jnp.arange(gw)[:, None] // HEAD_DIM == jnp.arange(gw)[None, :] // HEAD_DIM).astype(BF16)
    ones = _after(ones, token)
    res = _bwd_post_attn(dh1b, wg_out0, z0, os_, ls_, qm0, kv[0], mo0, ones)
    dz0, dos, dls, dqm0, dkv0 = res[0], res[1:4], res[4:7], res[7], res[8]

    grads1, from_sibling = _split_wait(started, _plan_to_sibling, dz0, "rs1_sibling_wait")
    parts1 = [_rs_add_sibling(g, r, c_arr, "rs_add_sibling_" + n) for g, r, n in zip(grads1, from_sibling, names1)]
    pbs1 = [pb for _, pb in parts1]
    started, token = _split_start(pbs1, [lax.empty((3,) + p.shape[1:], p.dtype) for p in pbs1],
                                  _plan_to_chips, 3, dg1, "rs1_chips_start")

    dqs, dks, dvs = [], [], []
    for j, d in enumerate(DILATIONS):
        dq, dk, dv = _attn_bwd(qs[j], ks[j], vs[j], ls_[j], dos[j], dls[j], tabs_v[j], d, token)
        dqs.append((dq, d))
        dks.append((dk, d))
        dvs.append((dv, d))
    d_w_kv, d_mem_g = _memkv_bwd(jnp.stack([dkv0, dkv1]), w_kv, mem, mem_norm_g)
    n_kv = d_w_kv.shape[1] // N_DEV
    dproj0 = _assemble_dproj(dqs + dks + dvs + [(dqm0, 1), (dz0, 1)], wg_in0.shape[2], "assemble_dproj_attn")
    d_w_in0 = _wgrad_shards(hn0_t, dproj0, "wgrad_in0")

    names0 = ["attn_w_in", "w_mem_kv0", "w_mem_kv1"]
    grads0 = [d_w_in0, d_w_kv[0].reshape(N_DEV, n_kv, -1), d_w_kv[1].reshape(N_DEV, n_kv, -1)]
    started0, token0 = _split_start(grads0, [lax.empty((4,) + g.shape[1:], g.dtype) for g in grads0],
                                    _plan_to_sibling, 4, dg1, "rs0_sibling_start")
    _, from_chips1 = _split_wait(started, _plan_to_chips, token0, "rs1_chips_wait")
    shard = dict(attn_w_in=(attn_w_in[0], m_attn_w_in[0], v_attn_w_in[0]),
                 attn_w_out=(attn_w_out[0], m_attn_w_out[0], v_attn_w_out[0]),
                 conv_w_in=(conv_w_in[0], m_conv_w_in[0], v_conv_w_in[0]),
                 conv_w_out=(conv_w_out[0], m_conv_w_out[0], v_conv_w_out[0]),
                 w_mem_kv0=(w_mem_kv[0], m_w_mem_kv[0], v_w_mem_kv[0]), w_mem_kv1=(w_mem_kv[1], m_w_mem_kv[1], v_w_mem_kv[1]))
    big = {}
    for n, (pf, _), r in zip(names1, parts1, from_chips1):
        big[n] = _finish(n, pf, r, k_arr, *shard[n])

    grads0, from_sibling = _split_wait(started0, _plan_to_sibling, big["conv_w_out"][1], "rs0_sibling_wait")
    parts0 = [_rs_add_sibling(g, r, c_arr, "rs_add_sibling_" + n) for g, r, n in zip(grads0, from_sibling, names0)]
    pbs0 = [pb for _, pb in parts0]
    started0, token0 = _split_start(pbs0, [lax.empty((3,) + p.shape[1:], p.dtype) for p in pbs0],
                                    _plan_to_chips, 3, dg1, "rs0_chips_start")
    dx, dg0 = _dgrad_norm_dm(dproj0, wg_in0, x, g0, dh1, token0, "dgrad_norm_attn")

    small_part = jnp.concatenate([dg0, dg1, d_mem_g.reshape(2, -1), d_final_g, dcw[0:3]], axis=0)
    small_part = jnp.concatenate([small_part, jnp.broadcast_to(loss_acc[0, 0], small_part.shape)], axis=0)
    small = _sum_devices(_all_gather([small_part], "gather_small_grads")[0])
    loss = small[8, 0]
    g_conv_w = lax.dynamic_slice(small[5:8], (0, me * LANES), (3, LANES))[None]
    small_g = dict(norm_g=small[0:2], mem_norm_g=small[2:4], conv_w=g_conv_w, final_g=small[4])
    small_w = dict(norm_g=(norm_g, m_norm_g, v_norm_g), mem_norm_g=(mem_norm_g, m_mem_norm_g, v_mem_norm_g),
                   conv_w=(conv_w, m_conv_w, v_conv_w), final_g=(final_g, m_final_g, v_final_g))
    for n, (w, m, v) in small_w.items():
        big[n] = (small_g[n],) + _adamw(w, small_g[n], m, v, "adamw_" + n)

    _, from_chips0 = _split_wait(started0, _plan_to_chips, big["final_g"][1], "rs0_chips_wait")
    for n, (pf, _), r in zip(names0, parts0, from_chips0):
        big[n] = _finish(n, pf, r, k_arr, *shard[n])
    for n in ("attn_w_in", "attn_w_out", "conv_w_in", "conv_w_out"):
        big[n] = tuple(t[None] for t in big[n])
    big["w_mem_kv"] = tuple(jnp.stack([a, b]) for a, b in zip(big["w_mem_kv0"], big["w_mem_kv1"]))

    order = ["norm_g", "mem_norm_g", "w_mem_kv", "attn_w_in", "attn_w_out", "conv_w_in", "conv_w", "conv_w_out", "final_g"]
    return (loss, dx[None], *[big[n][0] for n in order], *[big[n][1] for n in order],
            *[big[n][2] for n in order], *[big[n][3] for n in order])
```

```python
import functools

import jax
import jax.numpy as jnp
from jax import lax
from jax.experimental import pallas as pl
from jax.experimental.pallas import tpu as pltpu

F32 = jnp.float32
BF16 = jnp.bfloat16

N_DEV = 8
D_MODEL = 1024
HEAD_DIM = 64
ROT_DIM = HEAD_DIM // 4
ROPE_THETA = 500000.0
DILATIONS = (1, 4, 16)
HEADS_PER_GROUP = 8
GROUP_WIDTH = HEADS_PER_GROUP * HEAD_DIM
BLOCK = 128
N_MEM = 256
MEM_HEADS = 4
MEM_WIDTH = MEM_HEADS * HEAD_DIM
CONV_WIDTH = D_MODEL
EPS = 1e-6
SCALE = HEAD_DIM ** -0.5
NEG = -1e30

ADAM_LR = 0.001
ADAM_B1 = 0.9
ADAM_B2 = 0.999
ADAM_EPS = 1e-08
ADAM_WD = 0.01
ADAM_STEP = 10

ROW_TILE = 256
LANES = 128
MESH = pl.DeviceIdType.MESH
ANY = pl.BlockSpec(memory_space=pl.ANY)


def _pallas_call(body, **kw):
    call = pl.pallas_call(body, **kw)

    def run(*args):
        pinned = [pltpu.with_memory_space_constraint(a, pltpu.HBM) if jnp.issubdtype(a.dtype, jnp.floating) else a
                  for a in args]
        return call(*pinned)

    return run


def _dot(a, b):
    return lax.dot_general(a, b, (((1,), (0,)), ((), ())), preferred_element_type=F32)


def _dot_nt(a, b):
    return lax.dot_general(a, b, (((1,), (1,)), ((), ())), preferred_element_type=F32)


def _dot_tn(a, b):
    return lax.dot_general(a, b, (((0,), (0,)), ((), ())), preferred_element_type=F32)


def _params(n_grid, vmem_mb=48):
    return pltpu.CompilerParams(dimension_semantics=("arbitrary",) * n_grid, vmem_limit_bytes=vmem_mb << 20)


def _rows(width, tm=ROW_TILE):
    return pl.BlockSpec((tm, width), lambda i: (i, 0))


def _view_rows(width, d, tm=ROW_TILE):
    return pl.BlockSpec((tm // d, d * width), lambda i: (i, 0))


def _whole(shape):
    return pl.BlockSpec(shape, lambda *_: (0,) * len(shape))


def _resident(shape):
    return pl.BlockSpec(shape, lambda *_: (0,) * len(shape), pipeline_mode=pl.Buffered(1))


def _sds(shape, dtype):
    return pltpu.HBM(shape, dtype)


def _plain(shape, dtype):
    return jax.ShapeDtypeStruct(shape, dtype)


def _silu_parts(z):
    sg = jax.nn.sigmoid(z)
    return z * sg, sg * (1.0 + z * (1.0 - sg))


def _to_view(scr, val, out_ref, d):
    tm, w = val.shape
    if d == 1:
        out_ref[...] = val.astype(out_ref.dtype)
        return
    for cb in range(w // LANES):
        scr[cb] = val[:, cb * LANES:(cb + 1) * LANES]
    for r in range(d):
        for cb in range(w // LANES):
            lo = r * w + cb * LANES
            out_ref[:, lo:lo + LANES] = scr[cb, pl.ds(r, tm // d, stride=d), :].astype(out_ref.dtype)


def _from_view(scr, in_ref, d):
    if d == 1:
        return in_ref[...].astype(F32)
    nc, tm, _ = scr.shape
    w = nc * LANES
    for r in range(d):
        for cb in range(nc):
            lo = r * w + cb * LANES
            scr[cb, pl.ds(r, tm // d, stride=d), :] = in_ref[:, lo:lo + LANES].astype(F32)
    return jnp.concatenate([scr[cb] for cb in range(nc)], axis=1)


def _view_scratch(width, tm=ROW_TILE):
    return pltpu.VMEM((width // LANES, tm, LANES), F32)


def _rope_tables(pos):
    half = ROT_DIM // 2
    inv_freq = ROPE_THETA ** (-jnp.arange(half, dtype=F32) * (2.0 / ROT_DIM))
    ang = pos.astype(F32)[:, None] * inv_freq
    cos, sin = jnp.cos(ang), jnp.sin(ang)
    s = pos.shape[0]
    z8 = jnp.zeros((s, half), F32)
    rest = HEAD_DIM - ROT_DIM
    cosf = jnp.concatenate([cos, cos, jnp.ones((s, rest), F32)], axis=1)
    sa = jnp.concatenate([-sin, z8, jnp.zeros((s, rest), F32)], axis=1)
    sb = jnp.concatenate([z8, sin, jnp.zeros((s, rest), F32)], axis=1)
    return tuple(jnp.tile(t, (1, LANES // HEAD_DIM)) for t in (cosf, sa, sb))


def _rope_fwd(t, cv, sav, sbv):
    w = t.shape[1]
    return t * cv + pltpu.roll(t, w - ROT_DIM // 2, 1) * sav + pltpu.roll(t, ROT_DIM // 2, 1) * sbv


def _rope_bwd(g, cv, sav, sbv):
    w = g.shape[1]
    return g * cv + pltpu.roll(g * sav, ROT_DIM // 2, 1) + pltpu.roll(g * sbv, w - ROT_DIM // 2, 1)


def _project(hn, wg_ref, proj_scr):
    c = wg_ref.shape[2]
    for j in range(N_DEV):
        proj_scr[:, j * c:(j + 1) * c] = _dot(hn, wg_ref[j])


def _inproj_attn(x, g, wg, tabs):
    s, d_model = x.shape
    gw = GROUP_WIDTH
    n = N_DEV * wg.shape[2]
    nz = n - 9 * gw - MEM_WIDTH
    reps = gw // LANES
    tm = ROW_TILE

    def body(x_ref, g_ref, w_ref, c_ref, sa_ref, sb_ref, hn_ref, hnt_ref, *rest):
        outs, (proj, scr, tscr) = rest[:-3], rest[-3:]
        q_refs, k_refs, v_refs, t_refs, qm_ref, z_ref = outs[0:3], outs[3:6], outs[6:9], outs[9:18], outs[18], outs[19]
        xb = x_ref[...]
        r = lax.rsqrt(jnp.mean(xb * xb, axis=-1, keepdims=True) + EPS)
        hn = ((xb * r) * g_ref[...]).astype(BF16)
        hn_ref[...] = hn
        hnt_ref[...] = hn.T
        _project(hn, w_ref, proj)
        tab = (c_ref[...], sa_ref[...], sb_ref[...])
        cv, sav, sbv = [jnp.tile(t, (1, reps)) for t in tab]
        for j, d in enumerate(DILATIONS):
            tq = _rope_fwd(proj[:, j * gw:(j + 1) * gw], cv, sav, sbv)
            _to_view(scr, tq * SCALE, q_refs[j], d)
            tk = _rope_fwd(proj[:, (3 + j) * gw:(4 + j) * gw], cv, sav, sbv)
            _to_view(scr, tk, k_refs[j], d)
            _to_view(scr, proj[:, (6 + j) * gw:(7 + j) * gw], v_refs[j], d)
            for i in range(3):
                _to_view(tscr, tab[i], t_refs[3 * j + i], d)
        qm_ref[...] = proj[:, 9 * gw:9 * gw + MEM_WIDTH].astype(BF16)
        z_ref[...] = proj[:, 9 * gw + MEM_WIDTH:]

    views = [_sds((s // d, d * gw), BF16) for d in DILATIONS]
    tviews = [_sds((s // d, d * LANES), F32) for d in DILATIONS for _ in range(3)]
    out_shape = ([_sds((s, d_model), BF16), _sds((d_model, s), BF16)] + views * 3 + tviews
                 + [_sds((s, MEM_WIDTH), BF16), _sds((s, nz), F32)])
    vspecs = [_view_rows(gw, d, tm) for d in DILATIONS]
    tspecs = [_view_rows(LANES, d, tm) for d in DILATIONS for _ in range(3)]
    out_specs = ([_rows(d_model, tm), pl.BlockSpec((d_model, tm), lambda i: (0, i))] + vspecs * 3 + tspecs
                 + [_rows(MEM_WIDTH, tm), _rows(nz, tm)])
    res = _pallas_call(
        body, name="inproj_attn", grid=(s // tm,), out_shape=out_shape,
        in_specs=[_rows(d_model, tm), _whole((1, d_model)), _resident(wg.shape)] + [_rows(LANES, tm)] * 3,
        out_specs=out_specs,
        scratch_shapes=[pltpu.VMEM((tm, n), F32), _view_scratch(gw, tm), _view_scratch(LANES, tm)],
        compiler_params=_params(1, 60),
    )(x, g, wg, *tabs)
    tabs_v = [res[11 + 3 * j:14 + 3 * j] for j in range(3)]
    return res[0], res[1], res[2:5], res[5:8], res[8:11], tabs_v, res[20], res[21]


def _stream_maps(d, nb):
    def cur(n):
        return (n % nb, n // nb)

    def prev(n):
        return (jnp.maximum(n % nb - 1, 0), n // nb)

    return cur, prev


def _band_mask(n_keys):
    qi = lax.broadcasted_iota(jnp.int32, (BLOCK, n_keys), 0)
    kj = lax.broadcasted_iota(jnp.int32, (BLOCK, n_keys), 1)
    if n_keys == BLOCK:
        return kj <= qi
    return jnp.logical_or(jnp.logical_and(kj < BLOCK, kj >= qi), jnp.logical_and(kj >= BLOCK, (kj - BLOCK) <= qi))


def _per_stream_block(n, nb, run):
    if nb == 1:
        run(False)
        return
    first = (n % nb) == 0
    pl.when(first)(lambda: run(False))
    pl.when(jnp.logical_not(first))(lambda: run(True))


def _low_head_lanes():
    return lax.broadcasted_iota(jnp.int32, (1, LANES), 1) < HEAD_DIM


def _split_pair(t, low):
    zero = jnp.zeros_like(t)
    return jnp.where(low, t, zero), jnp.where(low, zero, t)


def _attn_fwd(q, k, v, d, token):
    ln, dw = q.shape
    w = dw // d
    nb = ln // BLOCK
    nblk = d * nb
    cur, prev = _stream_maps(d, nb)

    def body(_, q_ref, kp_ref, kc_ref, vp_ref, vc_ref, o_ref, lse_ref):
        def run(with_prev):
            if with_prev:
                kk = jnp.concatenate([kp_ref[...], kc_ref[...]], axis=0)
                vv = jnp.concatenate([vp_ref[...], vc_ref[...]], axis=0)
            else:
                kk, vv = kc_ref[...], vc_ref[...]
            valid = _band_mask(kk.shape[0])
            low = _low_head_lanes()
            pairs = [slice(p * LANES, (p + 1) * LANES) for p in range(w // LANES)]
            qs_ = [h for pr in pairs for h in _split_pair(q_ref[:, pr], low)]
            k2s = [kk[:, pr] for pr in pairs for _ in range(2)]
            scs = [jnp.where(valid, _dot_nt(qh, k2), NEG) for qh, k2 in zip(qs_, k2s)]
            ms = [jnp.max(sc, axis=-1, keepdims=True) for sc in scs]
            ps = [jnp.exp(sc - m) for sc, m in zip(scs, ms)]
            ls = [jnp.sum(p, axis=-1, keepdims=True) for p in ps]
            pns = [(p * (1.0 / l)).astype(BF16) for p, l in zip(ps, ls)]
            for i, pr in enumerate(pairs):
                v2 = vv[:, pr]
                a, b = 2 * i, 2 * i + 1
                o_ref[:, pr] = jnp.where(low, _dot(pns[a], v2), _dot(pns[b], v2))
                lse_ref[:, pr] = jnp.where(low, ms[a] + jnp.log(ls[a]), ms[b] + jnp.log(ls[b]))

        _per_stream_block(pl.program_id(0), nb, run)

    blk = lambda f: pl.BlockSpec((BLOCK, w), f)
    return _pallas_call(
        body, name=f"attn_fwd_d{d}", grid=(nblk,),
        out_shape=[_sds((ln, dw), F32)] * 2,
        in_specs=[ANY, blk(cur), blk(prev), blk(cur), blk(prev), blk(cur)],
        out_specs=[blk(cur), blk(cur)], compiler_params=_params(1, 32),
    )(token, q, k, k, v, v)


def _memkv_fwd(mem, g, w):
    n_layers = w.shape[0]

    def body(mem_ref, g_ref, w_ref, kv_ref):
        mb = mem_ref[...]
        r = lax.rsqrt(jnp.mean(mb * mb, axis=-1, keepdims=True) + EPS)
        mn = ((mb * r) * g_ref[...]).astype(BF16)
        kv_ref[...] = _dot(mn, w_ref[...]).astype(BF16)

    return _pallas_call(
        body, name="memkv_fwd", grid=(n_layers,),
        out_shape=_plain((n_layers, N_MEM, 2 * MEM_WIDTH), BF16),
        in_specs=[_whole(mem.shape), pl.BlockSpec((None, 1, D_MODEL), lambda l: (l, 0, 0)),
                  pl.BlockSpec((None, D_MODEL, 2 * MEM_WIDTH), lambda l: (l, 0, 0))],
        out_specs=pl.BlockSpec((None, N_MEM, 2 * MEM_WIDTH), lambda l: (l, 0, 0)),
        compiler_params=_params(1, 32),
    )(mem, g.reshape(n_layers, 1, D_MODEL), w)


def _mix_groups(os_, ls_):
    mx = jnp.maximum(jnp.maximum(ls_[0], ls_[1]), ls_[2])
    es = [jnp.exp(t - mx) for t in ls_]
    inv = 1.0 / (es[0] + es[1] + es[2])
    ws = [e * inv for e in es]
    mix = ws[0] * os_[0] + ws[1] * os_[1] + ws[2] * os_[2]
    return ws, mix


MEM_PAIRS = [slice(p * LANES, (p + 1) * LANES) for p in range(MEM_WIDTH // LANES)]


def _mem_probs(qhs, k2s):
    scs = [_dot_nt(qh, k2) * SCALE for qh, k2 in zip(qhs, k2s)]
    es = [jnp.exp(sc - jnp.max(sc, axis=-1, keepdims=True)) for sc in scs]
    return [e * (1.0 / jnp.sum(e, axis=-1, keepdims=True)) for e in es]


def _mem_attn_into(qm, kv_ref, mo_ref):
    low = _low_head_lanes()
    qhs = [h for pr in MEM_PAIRS for h in _split_pair(qm[:, pr], low)]
    k2s = [kv_ref[:, pr] for pr in MEM_PAIRS for _ in range(2)]
    ps = [p.astype(BF16) for p in _mem_probs(qhs, k2s)]
    for i, pr in enumerate(MEM_PAIRS):
        v2 = kv_ref[:, MEM_WIDTH + i * LANES:MEM_WIDTH + (i + 1) * LANES]
        mo_ref[:, pr] = jnp.where(low, _dot(ps[2 * i], v2), _dot(ps[2 * i + 1], v2))


def _mem_attn_bwd(qm, kv_ref, dmem, dqm_ref, dkv_ref):
    low = _low_head_lanes()
    dmb = dmem.astype(BF16)
    vps = [slice(MEM_WIDTH + i * LANES, MEM_WIDTH + (i + 1) * LANES) for i in range(len(MEM_PAIRS))]
    qhs = [h for pr in MEM_PAIRS for h in _split_pair(qm[:, pr], low)]
    dhs = [h for pr in MEM_PAIRS for h in _split_pair(dmb[:, pr], low)]
    k2s = [kv_ref[:, pr] for pr in MEM_PAIRS for _ in range(2)]
    v2s = [kv_ref[:, vp] for vp in vps for _ in range(2)]
    ps = _mem_probs(qhs, k2s)
    dps = [_dot_nt(dh, v2) for dh, v2 in zip(dhs, v2s)]
    dss = [(p * (dp - jnp.sum(dp * p, axis=-1, keepdims=True)) * SCALE).astype(BF16) for p, dp in zip(ps, dps)]
    pbs = [p.astype(BF16) for p in ps]
    for i, (pr, vp) in enumerate(zip(MEM_PAIRS, vps)):
        a, b = 2 * i, 2 * i + 1
        dqm_ref[:, pr] = jnp.where(low, _dot(dss[a], k2s[a]), _dot(dss[b], k2s[b])).astype(BF16)
        dkv_ref[:, pr] += _dot_tn(dss[a], qhs[a]) + _dot_tn(dss[b], qhs[b])
        dkv_ref[:, vp] += _dot_tn(pbs[a], dhs[a]) + _dot_tn(pbs[b], dhs[b])


def _post_attn(os_, ls_, qm, kv, z, x, wg_out):
    s, d_model = x.shape
    gw = GROUP_WIDTH
    nb = gw + MEM_WIDTH
    c = wg_out.shape[2]
    tm = ROW_TILE

    def body(o0, o1, o2, l0, l1, l2, qm_ref, kv_ref, z_ref, x_ref, w_ref, y_ref, yt_ref, mo_ref, h_ref, s0, s1):
        ov, lv = [], []
        for o_ref, l_ref, d in zip((o0, o1, o2), (l0, l1, l2), DILATIONS):
            ov.append(_from_view(s0, o_ref, d))
            lv.append(_from_view(s1, l_ref, d))
        _, mix = _mix_groups(ov, lv)
        _mem_attn_into(qm_ref[...], kv_ref, mo_ref)
        sz, _ = _silu_parts(z_ref[...])
        y_ref[:, :gw] = (mix * sz[:, :gw]).astype(BF16)
        y_ref[:, gw:] = (mo_ref[...] * sz[:, gw:]).astype(BF16)
        y = y_ref[...]
        yt_ref[...] = y.T
        for j in range(N_DEV):
            h_ref[:, j * c:(j + 1) * c] = x_ref[:, j * c:(j + 1) * c] + _dot(y, w_ref[j])

    vspecs = [_view_rows(gw, d) for d in DILATIONS]
    return _pallas_call(
        body, name="post_attn", grid=(s // tm,),
        out_shape=[_sds((s, nb), BF16), _sds((nb, s), BF16), _sds((s, MEM_WIDTH), F32), _sds((s, d_model), F32)],
        in_specs=vspecs * 2 + [_rows(MEM_WIDTH), _whole(kv.shape), _rows(nb), _rows(d_model), _whole(wg_out.shape)],
        out_specs=[_rows(nb), pl.BlockSpec((nb, tm), lambda i: (0, i)), _rows(MEM_WIDTH), _rows(d_model)],
        scratch_shapes=[_view_scratch(gw), _view_scratch(gw)],
        compiler_params=_params(1, 40),
    )(*os_, *ls_, qm, kv, z, x, wg_out)


def _inproj_conv(x, g, wg):
    s, d_model = x.shape
    c = CONV_WIDTH
    n = N_DEV * wg.shape[2]
    nz = n - 3 * c - MEM_WIDTH
    tm = ROW_TILE

    def body(x_ref, g_ref, w_ref, hn_ref, hnt_ref, bg_ref, cg_ref, u_ref, qm_ref, z_ref, proj):
        xb = x_ref[...]
        r = lax.rsqrt(jnp.mean(xb * xb, axis=-1, keepdims=True) + EPS)
        hn = ((xb * r) * g_ref[...]).astype(BF16)
        hn_ref[...] = hn
        hnt_ref[...] = hn.T
        _project(hn, w_ref, proj)
        bg_ref[...] = proj[:, 0:c]
        cg_ref[...] = proj[:, c:2 * c]
        u_ref[...] = proj[:, 2 * c:3 * c]
        qm_ref[...] = proj[:, 3 * c:3 * c + MEM_WIDTH].astype(BF16)
        z_ref[...] = proj[:, 3 * c + MEM_WIDTH:]

    return _pallas_call(
        body, name="inproj_conv", grid=(s // tm,),
        out_shape=[_sds((s, d_model), BF16), _sds((d_model, s), BF16)] + [_sds((s, c), F32)] * 3
                  + [_sds((s, MEM_WIDTH), BF16), _sds((s, nz), F32)],
        in_specs=[_rows(d_model), _whole((1, d_model)), _whole(wg.shape)],
        out_specs=[_rows(d_model), pl.BlockSpec((d_model, tm), lambda i: (0, i))] + [_rows(c)] * 3
                  + [_rows(MEM_WIDTH), _rows(nz)],
        scratch_shapes=[pltpu.VMEM((tm, n), F32)],
        compiler_params=_params(1, 60),
    )(x, g, wg)


HALO = 8


def _halo_before(width, tm=ROW_TILE):
    return pl.BlockSpec((HALO, width), lambda i: (jnp.maximum(i * (tm // HALO) - 1, 0), 0))


def _halo_after(width, n_rows, tm=ROW_TILE):
    return pl.BlockSpec((HALO, width), lambda i: (jnp.minimum((i + 1) * (tm // HALO), n_rows // HALO - 1), 0))


def _conv_taps(cg_ref, u_ref, cgh_ref, uh_ref, i):
    a = cg_ref[...] * u_ref[...]
    ah = jnp.where(i > 0, cgh_ref[...] * uh_ref[...], 0.0)
    row = lax.broadcasted_iota(jnp.int32, a.shape, 0)
    a1 = jnp.where(row == 0, ah[HALO - 1:HALO], pltpu.roll(a, 1, 0))
    a2 = jnp.where(row == 0, ah[HALO - 2:HALO - 1], jnp.where(row == 1, ah[HALO - 1:HALO], pltpu.roll(a, 2, 0)))
    return a, a1, a2


def _post_conv_loss(bg, cg, u, qm, kv, z, h1, w_out, cw, gf, tgt):
    s, d = h1.shape
    c = CONV_WIDTH
    nb = c + MEM_WIDTH
    tm = ROW_TILE

    def body(bg_ref, cg_ref, u_ref, cgh_ref, uh_ref, qm_ref, kv_ref, z_ref, h_ref, w_ref, cw_ref, gf_ref, t_ref,
             y_ref, yt_ref, mo_ref, dh_ref, dhb_ref, loss_ref, dgf_ref):
        i = pl.program_id(0)
        a, a1, a2 = _conv_taps(cg_ref, u_ref, cgh_ref, uh_ref, i)
        conv = cw_ref[0:1, :] * a2 + cw_ref[1:2, :] * a1 + cw_ref[2:3, :] * a
        mix = bg_ref[...] * conv
        _mem_attn_into(qm_ref[...], kv_ref, mo_ref)
        sz, _ = _silu_parts(z_ref[...])
        y_ref[:, :c] = (mix * sz[:, :c]).astype(BF16)
        y_ref[:, c:] = (mo_ref[...] * sz[:, c:]).astype(BF16)
        y = y_ref[...]
        yt_ref[...] = y.T
        h2 = h_ref[...] + _dot(y, w_ref[...])
        r = lax.rsqrt(jnp.mean(h2 * h2, axis=-1, keepdims=True) + EPS)
        nh = h2 * r
        gfv = gf_ref[...]
        diff = nh * gfv - t_ref[...]
        dout = diff * (1.0 / d)
        dn = dout * gfv
        dh2 = r * dn - h2 * ((r * r * r) * jnp.mean(dn * h2, axis=-1, keepdims=True))
        dh_ref[...] = dh2
        dhb_ref[...] = dh2.astype(BF16)

        @pl.when(i == 0)
        def _():
            loss_ref[...] = jnp.zeros_like(loss_ref)
            dgf_ref[...] = jnp.zeros_like(dgf_ref)

        loss_ref[...] += 0.5 * jnp.sum(jnp.mean(diff * diff, axis=-1, keepdims=True))
        dgf_ref[...] += jnp.sum(dout * nh, axis=0, keepdims=True)

    return _pallas_call(
        body, name="post_conv_loss", grid=(s // tm,),
        out_shape=[_sds((s, nb), BF16), _sds((nb, s), BF16), _sds((s, MEM_WIDTH), F32), _sds((s, d), F32),
                   _sds((s, d), BF16), _plain((8, LANES), F32), _plain((1, d), F32)],
        in_specs=[_rows(c)] * 3 + [_halo_before(c)] * 2 + [_rows(MEM_WIDTH), _whole(kv.shape), _rows(nb), _rows(d),
                  _whole(w_out.shape), _whole(cw.shape), _whole((1, d)), _rows(d)],
        out_specs=[_rows(nb), pl.BlockSpec((nb, tm), lambda i: (0, i)), _rows(MEM_WIDTH), _rows(d), _rows(d),
                   _whole((8, LANES)), _whole((1, d))],
        compiler_params=_params(1, 48),
    )(bg, cg, u, cg, u, qm, kv, z, h1, w_out, cw, gf, tgt)


def _bwd_post_conv(dhb, w_out, bg, cg, u, z, qm, kv, mo, cw):
    s = dhb.shape[0]
    c = CONV_WIDTH
    nb = c + MEM_WIDTH

    def body(dh_ref, w_ref, bg_ref, cg_ref, u_ref, cgh_ref, uh_ref, z_ref, qm_ref, kv_ref, mo_ref, cw_ref,
             dz_ref, dbg_ref, dc_ref, dqm_ref, dkv_ref):
        i = pl.program_id(0)

        @pl.when(i == 0)
        def _():
            dkv_ref[...] = jnp.zeros_like(dkv_ref)

        dy = _dot_nt(dh_ref[...], w_ref[...])
        sz, dsz = _silu_parts(z_ref[...])
        a, a1, a2 = _conv_taps(cg_ref, u_ref, cgh_ref, uh_ref, i)
        conv = cw_ref[0:1, :] * a2 + cw_ref[1:2, :] * a1 + cw_ref[2:3, :] * a
        bgv = bg_ref[...]
        dz_ref[:, :c] = (dy[:, :c] * (bgv * conv) * dsz[:, :c]).astype(BF16)
        dz_ref[:, c:] = (dy[:, c:] * mo_ref[...] * dsz[:, c:]).astype(BF16)
        dbr = dy * sz
        dmix = dbr[:, :c]
        dbg_ref[...] = (dmix * conv).astype(BF16)
        dc_ref[...] = dmix * bgv
        _mem_attn_bwd(qm_ref[...], kv_ref, dbr[:, c:], dqm_ref, dkv_ref)

    return _pallas_call(
        body, name="bwd_post_conv", grid=(s // ROW_TILE,),
        out_shape=[_sds((s, nb), BF16), _sds((s, c), BF16), _sds((s, c), F32), _sds((s, MEM_WIDTH), BF16),
                   _plain(kv.shape, F32)],
        in_specs=[_rows(D_MODEL), _whole(w_out.shape)] + [_rows(c)] * 3 + [_halo_before(c)] * 2
                 + [_rows(nb), _rows(MEM_WIDTH), _whole(kv.shape), _rows(MEM_WIDTH), _whole(cw.shape)],
        out_specs=[_rows(nb), _rows(c), _rows(c), _rows(MEM_WIDTH), _whole(kv.shape)],
        compiler_params=_params(1, 48),
    )(dhb, w_out, bg, cg, u, cg, u, z, qm, kv, mo, cw)


def _bwd_conv(dconv, cg, u, cw):
    s, c = dconv.shape
    tm = ROW_TILE
    last = s // tm - 1

    def body(dc_ref, dcn_ref, cg_ref, u_ref, cgh_ref, uh_ref, cw_ref, dcg_ref, du_ref, dcw_ref):
        i = pl.program_id(0)

        @pl.when(i == 0)
        def _():
            dcw_ref[...] = jnp.zeros_like(dcw_ref)

        dc = dc_ref[...]
        dcn = jnp.where(i < last, dcn_ref[...], 0.0)
        row = lax.broadcasted_iota(jnp.int32, dc.shape, 0)
        d1 = jnp.where(row == tm - 1, dcn[0:1], pltpu.roll(dc, tm - 1, 0))
        d2 = jnp.where(row == tm - 1, dcn[1:2], jnp.where(row == tm - 2, dcn[0:1], pltpu.roll(dc, tm - 2, 0)))
        da = cw_ref[2:3, :] * dc + cw_ref[1:2, :] * d1 + cw_ref[0:1, :] * d2
        a, a1, a2 = _conv_taps(cg_ref, u_ref, cgh_ref, uh_ref, i)
        dcg_ref[...] = (da * u_ref[...]).astype(BF16)
        du_ref[...] = (da * cg_ref[...]).astype(BF16)
        dcw_ref[0:1, :] += jnp.sum(dc * a2, axis=0, keepdims=True)
        dcw_ref[1:2, :] += jnp.sum(dc * a1, axis=0, keepdims=True)
        dcw_ref[2:3, :] += jnp.sum(dc * a, axis=0, keepdims=True)

    return _pallas_call(
        body, name="bwd_conv", grid=(s // tm,),
        out_shape=[_sds((s, c), BF16), _sds((s, c), BF16), _plain((8, c), F32)],
        in_specs=[_rows(c), _halo_after(c, s), _rows(c), _rows(c), _halo_before(c), _halo_before(c), _whole(cw.shape)],
        out_specs=[_rows(c), _rows(c), _whole((8, c))], compiler_params=_params(1, 40),
    )(dconv, dconv, cg, u, cg, u, cw)


def _dgrad_norm(pieces, wg, h, g, dres, name):
    s, d_model = h.shape
    c = wg.shape[2]
    n = N_DEV * c
    tm = ROW_TILE
    widths = [p.shape[1] // d for p, d in pieces]
    assert sum(widths) == n
    n_p = len(pieces)

    def body(*refs):
        p_refs = refs[:n_p]
        w_ref, h_ref, g_ref, dr_ref, dh_ref, dhb_ref, dg_ref, dpd_ref, dp, scr = refs[n_p:]

        @pl.when(pl.program_id(0) == 0)
        def _():
            dg_ref[...] = jnp.zeros_like(dg_ref)

        off = 0
        for p_ref, (_, d), wd in zip(p_refs, pieces, widths):
            if d == 1:
                dp[:, off:off + wd] = p_ref[...]
            else:
                dp[:, off:off + wd] = _from_view(scr, p_ref, d).astype(BF16)
            off += wd
        dhn = jnp.zeros((tm, d_model), F32)
        for j in range(N_DEV):
            dpj = dp[:, j * c:(j + 1) * c]
            dpd_ref[j] = dpj
            dhn += _dot_nt(dpj, w_ref[j])
        hb = h_ref[...]
        r = lax.rsqrt(jnp.mean(hb * hb, axis=-1, keepdims=True) + EPS)
        dg_ref[...] += jnp.sum(dhn * (hb * r), axis=0, keepdims=True)
        dn = dhn * g_ref[...]
        dh = dr_ref[...] + r * dn - hb * ((r * r * r) * jnp.mean(dn * hb, axis=-1, keepdims=True))
        dh_ref[...] = dh
        dhb_ref[...] = dh.astype(BF16)

    p_specs = [_view_rows(wd, d) for (_, d), wd in zip(pieces, widths)]
    return _pallas_call(
        body, name=name, grid=(s // tm,),
        out_shape=[_plain((s, d_model), F32), _sds((s, d_model), BF16), _plain((1, d_model), F32), _sds((N_DEV, s, c), BF16)],
        in_specs=p_specs + [_whole(wg.shape), _rows(d_model), _whole((1, d_model)), _rows(d_model)],
        out_specs=[_rows(d_model), _rows(d_model), _whole((1, d_model)), pl.BlockSpec((N_DEV, tm, c), lambda i: (0, i, 0))],
        scratch_shapes=[pltpu.VMEM((tm, n), BF16), _view_scratch(GROUP_WIDTH)],
        compiler_params=_params(1, 60),
    )(*[p for p, _ in pieces], wg, h, g, dres)


def _assemble_dproj(pieces, c, name):
    n = N_DEV * c
    tm = ROW_TILE
    widths = [p.shape[1] // d for p, d in pieces]
    assert sum(widths) == n
    s = pieces[0][0].shape[0] * pieces[0][1]
    n_p = len(pieces)

    def body(*refs):
        p_refs, (dpd_ref, dp, scr) = refs[:n_p], refs[n_p:]
        off = 0
        for p_ref, (_, d), wd in zip(p_refs, pieces, widths):
            if d == 1:
                dp[:, off:off + wd] = p_ref[...]
            else:
                dp[:, off:off + wd] = _from_view(scr, p_ref, d).astype(BF16)
            off += wd
        for j in range(N_DEV):
            dpd_ref[j] = dp[:, j * c:(j + 1) * c]

    return _pallas_call(
        body, name=name, grid=(s // tm,), out_shape=_sds((N_DEV, s, c), BF16),
        in_specs=[_view_rows(wd, d) for (_, d), wd in zip(pieces, widths)],
        out_specs=pl.BlockSpec((N_DEV, tm, c), lambda i: (0, i, 0)),
        scratch_shapes=[pltpu.VMEM((tm, n), BF16), _view_scratch(GROUP_WIDTH)],
        compiler_params=_params(1, 40),
    )(*[p for p, _ in pieces])


def _dgrad_norm_dm(dproj_dm, wg, h, g, dres, token, name):
    s, d_model = h.shape
    c = wg.shape[2]
    tm = ROW_TILE

    def body(_, dp_ref, w_ref, h_ref, g_ref, dr_ref, dh_ref, dg_ref):
        @pl.when(pl.program_id(0) == 0)
        def _():
            dg_ref[...] = jnp.zeros_like(dg_ref)

        dhn = jnp.zeros((tm, d_model), F32)
        for j in range(N_DEV):
            dhn += _dot_nt(dp_ref[j], w_ref[j])
        hb = h_ref[...]
        r = lax.rsqrt(jnp.mean(hb * hb, axis=-1, keepdims=True) + EPS)
        dg_ref[...] += jnp.sum(dhn * (hb * r), axis=0, keepdims=True)
        dn = dhn * g_ref[...]
        dh_ref[...] = dr_ref[...] + r * dn - hb * ((r * r * r) * jnp.mean(dn * hb, axis=-1, keepdims=True))

    return _pallas_call(
        body, name=name, grid=(s // tm,),
        out_shape=[_plain((s, d_model), F32), _plain((1, d_model), F32)],
        in_specs=[ANY, pl.BlockSpec((N_DEV, tm, c), lambda i: (0, i, 0)), _whole(wg.shape), _rows(d_model),
                  _whole((1, d_model)), _rows(d_model)],
        out_specs=[_rows(d_model), _whole((1, d_model))],
        compiler_params=_params(1, 60),
    )(token, dproj_dm, wg, h, g, dres)


def _wgrad_shards(a_t, b_dm, name):
    m, s = a_t.shape
    c = b_dm.shape[2]

    def body(a_ref, b_ref, o_ref):
        o_ref[...] = _dot(a_ref[...], b_ref[...]).astype(BF16)

    return _pallas_call(
        body, name=name, grid=(N_DEV,), out_shape=_sds((N_DEV, m, c), BF16),
        in_specs=[_whole(a_t.shape), pl.BlockSpec((None, s, c), lambda j: (j, 0, 0))],
        out_specs=pl.BlockSpec((None, m, c), lambda j: (j, 0, 0)), compiler_params=_params(1, 40),
    )(a_t, b_dm)


def _wgrad_cols(a_t, b, c, name):
    m, s = a_t.shape

    def body(a_ref, b_ref, o_ref):
        o_ref[...] = _dot(a_ref[...], b_ref[...]).astype(BF16)

    return _pallas_call(
        body, name=name, grid=(N_DEV,), out_shape=_sds((N_DEV, m, c), BF16),
        in_specs=[_whole(a_t.shape), pl.BlockSpec((s, c), lambda j: (0, j))],
        out_specs=pl.BlockSpec((None, m, c), lambda j: (j, 0, 0)), compiler_params=_params(1, 40),
    )(a_t, b)


def _wgrad_rows(a_t, b, name):
    m, s = a_t.shape
    n = b.shape[1]
    mr = m // N_DEV

    def body(a_ref, b_ref, o_ref):
        o_ref[...] = _dot(a_ref[...], b_ref[...]).astype(BF16)

    return _pallas_call(
        body, name=name, grid=(N_DEV,), out_shape=_sds((N_DEV, mr, n), BF16),
        in_specs=[pl.BlockSpec((mr, s), lambda j: (j, 0)), _whole(b.shape)],
        out_specs=pl.BlockSpec((None, mr, n), lambda j: (j, 0, 0)), compiler_params=_params(1, 40),
    )(a_t, b)


def _memkv_bwd(dkv, w, mem, g):
    n_layers = w.shape[0]

    def body(dkv_ref, w_ref, mem_ref, g_ref, dw_ref, dg_ref):
        mb = mem_ref[...]
        r = lax.rsqrt(jnp.mean(mb * mb, axis=-1, keepdims=True) + EPS)
        nm = mb * r
        mn = (nm * g_ref[...]).astype(BF16)
        dkvb = dkv_ref[...].astype(BF16)
        dw_ref[...] = _dot_tn(mn, dkvb).astype(BF16)
        dmn = _dot_nt(dkvb, w_ref[...])
        dg_ref[...] = jnp.sum(dmn * nm, axis=0, keepdims=True)

    lay = lambda *shape: pl.BlockSpec((None,) + shape, lambda l: (l, 0, 0))
    return _pallas_call(
        body, name="memkv_bwd", grid=(n_layers,),
        out_shape=[_plain((n_layers, D_MODEL, 2 * MEM_WIDTH), BF16), _plain((n_layers, 1, D_MODEL), F32)],
        in_specs=[lay(N_MEM, 2 * MEM_WIDTH), lay(D_MODEL, 2 * MEM_WIDTH), _whole(mem.shape), lay(1, D_MODEL)],
        out_specs=[lay(D_MODEL, 2 * MEM_WIDTH), lay(1, D_MODEL)], compiler_params=_params(1, 32),
    )(dkv, w, mem, g.reshape(n_layers, 1, D_MODEL))


def _bwd_post_attn(dhb, wg_out, z, os_, ls_, qm, kv, mo, head_ones):
    s = dhb.shape[0]
    gw = GROUP_WIDTH
    nb = gw + MEM_WIDTH
    c = wg_out.shape[2]
    tm = ROW_TILE

    def body(dh_ref, w_ref, z_ref, o0, o1, o2, l0, l1, l2, qm_ref, kv_ref, mo_ref, bd_ref,
             dz_ref, do0, do1, do2, dl0, dl1, dl2, dqm_ref, dkv_ref, s0, s1):
        @pl.when(pl.program_id(0) == 0)
        def _():
            dkv_ref[...] = jnp.zeros_like(dkv_ref)

        dy = jnp.zeros((tm, nb), F32)
        for j in range(N_DEV):
            dy += _dot_nt(dh_ref[:, j * c:(j + 1) * c], w_ref[j])
        ov, lv = [], []
        for o_ref, l_ref, d in zip((o0, o1, o2), (l0, l1, l2), DILATIONS):
            ov.append(_from_view(s0, o_ref, d))
            lv.append(_from_view(s1, l_ref, d))
        ws, mix = _mix_groups(ov, lv)
        sz, dsz = _silu_parts(z_ref[...])
        dz_ref[:, :gw] = (dy[:, :gw] * mix * dsz[:, :gw]).astype(BF16)
        dz_ref[:, gw:] = (dy[:, gw:] * mo_ref[...] * dsz[:, gw:]).astype(BF16)
        dbr = dy * sz
        dmix = dbr[:, :gw]
        t = dmix * mix
        th = t.astype(BF16)
        tl = (t - th.astype(F32)).astype(BF16)
        rs = _dot(th, bd_ref[...]) + _dot(tl, bd_ref[...])
        for wg_, do_ref, dl_ref, d in zip(ws, (do0, do1, do2), (dl0, dl1, dl2), DILATIONS):
            _to_view(s0, wg_ * dmix, do_ref, d)
            _to_view(s1, wg_ * rs, dl_ref, d)
        _mem_attn_bwd(qm_ref[...], kv_ref, dbr[:, gw:], dqm_ref, dkv_ref)

    vspecs = [_view_rows(gw, d) for d in DILATIONS]
    return _pallas_call(
        body, name="bwd_post_attn", grid=(s // tm,),
        out_shape=[_sds((s, nb), BF16)] + [_sds((s // d, d * gw), BF16) for d in DILATIONS]
                  + [_sds((s // d, d * gw), F32) for d in DILATIONS] + [_sds((s, MEM_WIDTH), BF16), _plain(kv.shape, F32)],
        in_specs=[_rows(D_MODEL), _whole(wg_out.shape), _rows(nb)] + vspecs * 2
                 + [_rows(MEM_WIDTH), _whole(kv.shape), _rows(MEM_WIDTH), _whole(head_ones.shape)],
        out_specs=[_rows(nb)] + vspecs * 2 + [_rows(MEM_WIDTH), _whole(kv.shape)],
        scratch_shapes=[_view_scratch(gw), _view_scratch(gw)],
        compiler_params=_params(1, 48),
    )(dhb, wg_out, z, *os_, *ls_, qm, kv, mo, head_ones)


def _attn_bwd(q, k, v, lse, do, dl, tabs, d, token):
    ln, dw = q.shape
    w = dw // d
    nb = ln // BLOCK
    nblk = d * nb
    reps = w // LANES
    cur, prev = _stream_maps(d, nb)
    qmap = lambda n: cur(jnp.minimum(n, nblk - 1))
    pmap = lambda n: prev(jnp.minimum(n, nblk - 1))
    omap = lambda n: cur(jnp.maximum(n - 1, 0))

    def body(_, q_ref, kp_ref, kc_ref, vp_ref, vc_ref, l_ref, do_ref, dl_ref, cq, saq, sbq, ck, sak, sbk,
             dq_ref, dk_ref, dv_ref, acck, accv, dqs):
        n = pl.program_id(0)

        @pl.when(n == 0)
        def _():
            acck[...] = jnp.zeros_like(acck)
            accv[...] = jnp.zeros_like(accv)

        def run(with_prev):
            if with_prev:
                kk = jnp.concatenate([kp_ref[...], kc_ref[...]], axis=0)
                vv = jnp.concatenate([vp_ref[...], vc_ref[...]], axis=0)
                rows = slice(0, 2 * BLOCK)
            else:
                kk, vv = kc_ref[...], vc_ref[...]
                rows = slice(BLOCK, 2 * BLOCK)
            valid = _band_mask(kk.shape[0])
            low = _low_head_lanes()
            pairs = [slice(p * LANES, (p + 1) * LANES) for p in range(w // LANES)]
            cols = [slice(h * HEAD_DIM, h * HEAD_DIM + 1) for h in range(HEADS_PER_GROUP)]
            qhs = [h for pr in pairs for h in _split_pair(q_ref[:, pr], low)]
            dobs = [h for pr in pairs for h in _split_pair(do_ref[:, pr], low)]
            k2s = [kk[:, pr] for pr in pairs for _ in range(2)]
            v2s = [vv[:, pr] for pr in pairs for _ in range(2)]
            scs = [jnp.where(valid, _dot_nt(qh, k2), NEG) for qh, k2 in zip(qhs, k2s)]
            dps = [_dot_nt(dob, v2) for dob, v2 in zip(dobs, v2s)]
            ps = [jnp.exp(sc - l_ref[:, col]) for sc, col in zip(scs, cols)]
            dss = [(p * (dp - dl_ref[:, col])).astype(BF16) for p, dp, col in zip(ps, dps, cols)]
            pbs = [p.astype(BF16) for p in ps]
            for i, pr in enumerate(pairs):
                a, b = 2 * i, 2 * i + 1
                dqs[:, pr] = jnp.where(low, _dot(dss[a], k2s[a]), _dot(dss[b], k2s[b])) * SCALE
                acck[rows, pr] += _dot_tn(dss[a], qhs[a]) + _dot_tn(dss[b], qhs[b])
                accv[rows, pr] += _dot_tn(pbs[a], dobs[a]) + _dot_tn(pbs[b], dobs[b])
            tq = [jnp.tile(r[...], (1, reps)) for r in (cq, saq, sbq)]
            dq_ref[...] = _rope_bwd(dqs[...], *tq).astype(BF16)

        pl.when(n < nblk)(lambda: _per_stream_block(n, nb, run))

        tk = [jnp.tile(r[...], (1, reps)) for r in (ck, sak, sbk)]
        dk_ref[...] = _rope_bwd(acck[0:BLOCK, :], *tk).astype(BF16)
        dv_ref[...] = accv[0:BLOCK, :].astype(BF16)
        acck[0:BLOCK, :] = acck[BLOCK:, :]
        accv[0:BLOCK, :] = accv[BLOCK:, :]
        acck[BLOCK:, :] = jnp.zeros((BLOCK, w), F32)
        accv[BLOCK:, :] = jnp.zeros((BLOCK, w), F32)

    blk = lambda f: pl.BlockSpec((BLOCK, w), f)
    tblk = lambda f: pl.BlockSpec((BLOCK, LANES), f)
    return _pallas_call(
        body, name=f"attn_bwd_d{d}", grid=(nblk + 1,),
        out_shape=[_sds((ln, dw), BF16)] * 3,
        in_specs=[ANY, blk(qmap), blk(pmap), blk(qmap), blk(pmap), blk(qmap), blk(qmap), blk(qmap), blk(qmap)]
                 + [tblk(qmap)] * 3 + [tblk(omap)] * 3,
        out_specs=[blk(qmap), blk(omap), blk(omap)],
        scratch_shapes=[pltpu.VMEM((2 * BLOCK, w), F32), pltpu.VMEM((2 * BLOCK, w), F32), pltpu.VMEM((BLOCK, w), F32)],
        compiler_params=_params(1, 32),
    )(token, q, k, k, v, v, lse, do, dl, *tabs, *tabs)


def _position():
    return lax.axis_index("x"), lax.axis_index("y"), lax.axis_index("c")


def _all_gather(shards, name):
    n_a = len(shards)

    def body(*refs):
        x_refs, out_refs = refs[:n_a], refs[n_a:2 * n_a]
        send_sems, recv_sems, local_sems = refs[2 * n_a:]
        x, y, c = _position()
        me, sibling = (x, y, c), (x, y, 1 - c)
        chips = [(1 - x, y), (x, 1 - y), (1 - x, 1 - y)]

        def rows(a, px, py, pc):
            return out_refs[a].at[4 * px + 2 * py + pc]

        def copy(a, k, block, to, own=False):
            return pltpu.make_async_remote_copy(
                src_ref=x_refs[a] if own else rows(a, *block), dst_ref=rows(a, *block),
                send_sem=send_sems.at[a, k], recv_sem=recv_sems.at[a, k], device_id=to, device_id_type=MESH)

        mine = [pltpu.make_async_copy(x_refs[a], rows(a, *me), local_sems.at[a]) for a in range(n_a)]
        for cp in mine:
            cp.start()
        first = []
        for j, chip in enumerate(chips):
            first += [copy(a, 1 + j, me, (*chip, c), own=True) for a in range(n_a)]
        first += [copy(a, 0, me, sibling, own=True) for a in range(n_a)]
        for cp in first:
            cp.start()
        passed = []
        for j, chip in enumerate(chips):
            for a in range(n_a):
                copy(a, 1 + j, (*chip, c), me).wait_recv()
                fwd = copy(a, 4 + j, (*chip, c), sibling)
                fwd.start()
                passed.append(fwd)
        for a in range(n_a):
            copy(a, 0, sibling, me).wait_recv()
        for j, chip in enumerate(chips):
            for a in range(n_a):
                copy(a, 4 + j, (*chip, 1 - c), me).wait_recv()
        for cp in first + passed:
            cp.wait_send()
        for cp in mine:
            cp.wait()

    return _pallas_call(
        body, name=name, out_shape=[_sds((N_DEV,) + t.shape, t.dtype) for t in shards],
        in_specs=[ANY] * n_a, out_specs=[ANY] * n_a,
        scratch_shapes=[pltpu.SemaphoreType.DMA((n_a, 7)), pltpu.SemaphoreType.DMA((n_a, 7)),
                        pltpu.SemaphoreType.DMA((n_a,))],
    )(*shards)


def _all_gather_relay(xs, name):
    def body(x_ref, out_ref, send_sems, recv_sems, local_sem):
        x, y, c = _position()
        me, sibling = (x, y, c), (x, y, 1 - c)
        xn, yn, diag = (1 - x, y, c), (x, 1 - y, c), (1 - x, 1 - y, c)
        src_nb = (x + c * (1 - 2 * x), y + (1 - c) * (1 - 2 * y), c)
        dst_nb = (x + (1 - c) * (1 - 2 * x), y + c * (1 - 2 * y), c)

        def rows(dev):
            return out_ref.at[4 * dev[0] + 2 * dev[1] + dev[2]]

        def copy(k, block, to, own=False):
            return pltpu.make_async_remote_copy(
                src_ref=x_ref if own else rows(block), dst_ref=rows(block),
                send_sem=send_sems.at[k], recv_sem=recv_sems.at[k], device_id=to, device_id_type=MESH)

        mine = pltpu.make_async_copy(x_ref, rows(me), local_sem)
        mine.start()
        first = [copy(1, me, xn, own=True), copy(2, me, yn, own=True), copy(0, me, sibling, own=True)]
        for cp in first:
            cp.start()
        copy(1, xn, me).wait_recv()
        copy(2, yn, me).wait_recv()
        relay = copy(3, src_nb, dst_nb)
        relay.start()
        passed = [copy(4, xn, sibling), copy(5, yn, sibling)]
        for cp in passed:
            cp.start()
        copy(3, diag, me).wait_recv()
        last = copy(6, diag, sibling)
        last.start()
        copy(0, sibling, me).wait_recv()
        for k, blk in ((4, (1 - x, y, 1 - c)), (5, (x, 1 - y, 1 - c)), (6, (1 - x, 1 - y, 1 - c))):
            copy(k, blk, me).wait_recv()
        for cp in first + [relay] + passed + [last]:
            cp.wait_send()
        mine.wait()

    return _pallas_call(
        body, name=name, out_shape=_sds((N_DEV,) + xs.shape, xs.dtype),
        in_specs=[ANY], out_specs=ANY,
        scratch_shapes=[pltpu.SemaphoreType.DMA((7,)), pltpu.SemaphoreType.DMA((7,)), pltpu.SemaphoreType.DMA],
    )(xs)


def _rs_to_sibling(gs):
    n_a = len(gs)

    def body(*refs):
        g_refs, recv_refs = refs[:n_a], refs[n_a:2 * n_a]
        send_sems, recv_sems = refs[2 * n_a:]
        x, y, c = _position()
        copies = []
        for k in range(4):
            for a in range(n_a):
                copies.append(pltpu.make_async_remote_copy(
                    src_ref=g_refs[a].at[2 * k + (1 - c)], dst_ref=recv_refs[a].at[k],
                    send_sem=send_sems.at[a, k], recv_sem=recv_sems.at[a, k],
                    device_id=(x, y, 1 - c), device_id_type=MESH))
        for cp in copies:
            cp.start()
        for cp in copies:
            cp.wait()

    return _pallas_call(
        body, name="rs_to_sibling", out_shape=[_sds((4,) + g.shape[1:], g.dtype) for g in gs],
        in_specs=[ANY] * n_a, out_specs=[ANY] * n_a,
        scratch_shapes=[pltpu.SemaphoreType.DMA((n_a, 4)), pltpu.SemaphoreType.DMA((n_a, 4))],
    )(*gs)


def _rs_to_chips(pbs):
    n_a = len(pbs)

    def body(*refs):
        p_refs, recv_refs = refs[:n_a], refs[n_a:2 * n_a]
        send_sems, recv_sems = refs[2 * n_a:]
        x, y, c = _position()
        chips = [(1 - x, y), (x, 1 - y), (1 - x, 1 - y)]
        copies = []
        for j, (px, py) in enumerate(chips):
            for a in range(n_a):
                copies.append(pltpu.make_async_remote_copy(
                    src_ref=p_refs[a].at[2 * px + py], dst_ref=recv_refs[a].at[j],
                    send_sem=send_sems.at[a, j], recv_sem=recv_sems.at[a, j],
                    device_id=(px, py, c), device_id_type=MESH))
        for cp in copies:
            cp.start()
        for cp in copies:
            cp.wait()

    return _pallas_call(
        body, name="rs_to_chips", out_shape=[_sds((3,) + p.shape[1:], p.dtype) for p in pbs],
        in_specs=[ANY] * n_a, out_specs=[ANY] * n_a,
        scratch_shapes=[pltpu.SemaphoreType.DMA((n_a, 3)), pltpu.SemaphoreType.DMA((n_a, 3))],
    )(*pbs)


HBM_SPEC = pl.BlockSpec(memory_space=pltpu.HBM)
SEM_SPEC = pl.BlockSpec(memory_space=pltpu.SEMAPHORE)
EFFECT = pltpu.SideEffectType.DATAFLOW_SIDE_EFFECTING
def _plan_gather_own(src_refs, land_refs):
    x, y, c = _position()
    me = 4 * x + 2 * y + c
    peers = [(x, y, 1 - c), (1 - x, y, c), (x, 1 - y, c), (1 - x, 1 - y, c)]
    return [(src_refs[a], land_refs[a].at[me], (a, k), peer) for k, peer in enumerate(peers) for a in range(len(src_refs))]


def _plan_gather_pass(src_refs, land_refs):
    x, y, c = _position()
    chips = [(1 - x, y), (x, 1 - y), (1 - x, 1 - y)]
    return [(land_refs[a].at[4 * px + 2 * py + c], land_refs[a].at[4 * px + 2 * py + c], (a, j), (x, y, 1 - c))
            for j, (px, py) in enumerate(chips) for a in range(len(land_refs))]


def _plan_to_sibling(src_refs, land_refs):
    x, y, c = _position()
    return [(src_refs[a].at[2 * k + (1 - c)], land_refs[a].at[k], (a, k), (x, y, 1 - c))
            for k in range(4) for a in range(len(src_refs))]


def _plan_to_chips(src_refs, land_refs):
    x, y, c = _position()
    chips = [(1 - x, y), (x, 1 - y), (1 - x, 1 - y)]
    return [(src_refs[a].at[2 * px + py], land_refs[a].at[j], (a, j), (px, py, c))
            for j, (px, py) in enumerate(chips) for a in range(len(src_refs))]


def _split_start(srcs, lands, plan, n_sem, after, name):
    n_s, n_a = len(srcs), len(lands)
    n_b = n_s + n_a

    def body(*refs):
        src_refs, land_refs = refs[:n_s], refs[n_s:n_b]
        send_sems, recv_sems, token = refs[n_b + 1], refs[n_b + 2], refs[-1]
        for src, dst, (a, k), dev in plan(src_refs, land_refs):
            i = a * n_sem + k
            pltpu.make_async_remote_copy(src_ref=src, dst_ref=dst, send_sem=send_sems.at[i], recv_sem=recv_sems.at[i],
                                         device_id=dev, device_id_type=MESH).start()
        token[...] = jnp.zeros_like(token)

    bufs = list(srcs) + list(lands)
    res = pl.pallas_call(
        body, name=name,
        out_shape=(pltpu.SemaphoreType.DMA((n_a * n_sem,)), pltpu.SemaphoreType.DMA((n_a * n_sem,)),
                   *[pltpu.HBM(t.shape, t.dtype) for t in bufs], _plain((8, LANES), F32)),
        in_specs=[HBM_SPEC] * n_b + [ANY],
        out_specs=(SEM_SPEC, SEM_SPEC, *[HBM_SPEC] * n_b, pl.BlockSpec(memory_space=pltpu.VMEM)),
        input_output_aliases={i: 2 + i for i in range(n_b)},
        compiler_params=pltpu.CompilerParams(has_side_effects=EFFECT),
    )(*[pltpu.with_memory_space_constraint(t, pltpu.HBM) for t in bufs], after)
    return (res[0], res[1], res[2:2 + n_s], res[2 + n_s:2 + n_b]), res[-1]


def _split_wait(started, plan, after, name):
    send_sems, recv_sems, srcs, lands = started
    n_s, n_a = len(srcs), len(lands)
    n_b = n_s + n_a
    n_sem = send_sems.shape[0] // n_a

    def body(*refs):
        src_refs, land_refs = refs[:n_s], refs[n_s:n_b]
        s_sems, r_sems = refs[n_b], refs[n_b + 1]
        for src, dst, (a, k), dev in plan(src_refs, land_refs):
            i = a * n_sem + k
            cp = pltpu.make_async_remote_copy(src_ref=src, dst_ref=dst, send_sem=s_sems.at[i], recv_sem=r_sems.at[i],
                                              device_id=dev, device_id_type=MESH)
            cp.wait_send()
            cp.wait_recv()

    bufs = list(srcs) + list(lands)
    res = pl.pallas_call(
        body, name=name, out_shape=tuple(pltpu.HBM(t.shape, t.dtype) for t in bufs),
        in_specs=[HBM_SPEC] * n_b + [SEM_SPEC, SEM_SPEC, ANY],
        out_specs=tuple([HBM_SPEC] * n_b),
        input_output_aliases={i: i for i in range(n_b)},
        compiler_params=pltpu.CompilerParams(has_side_effects=EFFECT),
    )(*bufs, send_sems, recv_sems, after)
    return res[:n_s], res[n_s:]


def _row_tile(r):
    return ROW_TILE if r % ROW_TILE == 0 else r


def _rs_add_sibling(gp, recv, c_arr, name):
    _, r, l = gp.shape
    tr = r if r <= 4 * ROW_TILE else _row_tile(r)

    def body(c_ref, g_ref, r_ref, pf_ref, pb_ref):
        sm = g_ref[...].astype(F32) + r_ref[...].astype(F32)
        pf_ref[...] = sm
        pb_ref[...] = sm.astype(BF16)

    spec = pl.BlockSpec((None, tr, l), lambda k, i, c: (k, i, 0))
    return _pallas_call(
        body, name=name,
        grid_spec=pltpu.PrefetchScalarGridSpec(
            num_scalar_prefetch=1, grid=(4, r // tr),
            in_specs=[pl.BlockSpec((None, tr, l), lambda k, i, c: (2 * k + c[0], i, 0)), spec],
            out_specs=[spec, spec]),
        out_shape=[_sds((4, r, l), F32), _sds((4, r, l), BF16)], compiler_params=_params(2, 32),
    )(c_arr, gp, recv)


def _adam_update(w, gv, m, v):
    nm = ADAM_B1 * m + (1.0 - ADAM_B1) * gv
    nv = ADAM_B2 * v + (1.0 - ADAM_B2) * (gv * gv)
    m_hat = nm / (1.0 - ADAM_B1 ** ADAM_STEP)
    v_hat = nv / (1.0 - ADAM_B2 ** ADAM_STEP)
    return -ADAM_LR * (m_hat / (jnp.sqrt(v_hat) + ADAM_EPS) + ADAM_WD * w), nm, nv


def _rs_finish_adamw(pf, recv, k_arr, w, m, v, name):
    _, r, l = pf.shape
    tr = _row_tile(r)

    def body(k_ref, p_ref, r_ref, w_ref, m_ref, v_ref, g_ref, d_ref, nm_ref, nv_ref):
        gv = ((p_ref[...] + r_ref[0].astype(F32)) + r_ref[1].astype(F32)) + r_ref[2].astype(F32)
        g_ref[...] = gv
        d_ref[...], nm_ref[...], nv_ref[...] = _adam_update(w_ref[...], gv, m_ref[...], v_ref[...])

    spec = pl.BlockSpec((tr, l), lambda i, k: (i, 0))
    return _pallas_call(
        body, name=name,
        grid_spec=pltpu.PrefetchScalarGridSpec(
            num_scalar_prefetch=1, grid=(r // tr,),
            in_specs=[pl.BlockSpec((None, tr, l), lambda i, k: (k[0], i, 0)),
                      pl.BlockSpec((3, tr, l), lambda i, k: (0, i, 0)), spec, spec, spec],
            out_specs=[spec] * 4),
        out_shape=[_plain((r, l), F32)] * 4, compiler_params=_params(1, 32),
    )(k_arr, pf, recv, w, m, v)


def _rs_finish_adamw_t(pf, recv, k_arr, w_t, m_t, v_t, name):
    _, r, c = pf.shape
    tr = _row_tile(r)
    cp = -(-c // LANES) * LANES

    def body(k_ref, p_ref, r_ref, w_ref, m_ref, v_ref, g_ref, d_ref, nm_ref, nv_ref, pad):
        gv = ((p_ref[...] + r_ref[0].astype(F32)) + r_ref[1].astype(F32)) + r_ref[2].astype(F32)
        pad[...] = jnp.zeros_like(pad)
        pad[:, 0:c] = gv
        gt = pad[...].T[0:c, :]
        g_ref[...] = gt
        d_ref[...], nm_ref[...], nv_ref[...] = _adam_update(w_ref[...], gt, m_ref[...], v_ref[...])

    spec = pl.BlockSpec((c, tr), lambda i, k: (0, i))
    return _pallas_call(
        body, name=name,
        grid_spec=pltpu.PrefetchScalarGridSpec(
            num_scalar_prefetch=1, grid=(r // tr,),
            in_specs=[pl.BlockSpec((None, tr, c), lambda i, k: (k[0], i, 0)),
                      pl.BlockSpec((3, tr, c), lambda i, k: (0, i, 0)), spec, spec, spec],
            out_specs=[spec] * 4, scratch_shapes=[pltpu.VMEM((tr, cp), F32)]),
        out_shape=[_plain((c, r), F32)] * 4, compiler_params=_params(1, 32),
    )(k_arr, pf, recv, w_t, m_t, v_t)


def _sum_devices(g):
    def body(g_ref, o_ref):
        acc = g_ref[0]
        for j in range(1, N_DEV):
            acc = acc + g_ref[j]
        o_ref[...] = acc

    return _pallas_call(body, name="sum_devices", out_shape=_plain(g.shape[1:], F32))(g)


def _adamw(w, g, m, v, name):
    shape = w.shape
    w2, g2, m2, v2 = [t.reshape((-1, shape[-1])) for t in (w, g, m, v)]

    def body(w_ref, g_ref, m_ref, v_ref, d_ref, nm_ref, nv_ref):
        d_ref[...], nm_ref[...], nv_ref[...] = _adam_update(w_ref[...], g_ref[...], m_ref[...], v_ref[...])

    outs = _pallas_call(body, name=name, out_shape=[_plain(w2.shape, F32)] * 3)(w2, g2, m2, v2)
    return tuple(t.reshape(shape) for t in outs)


def _after(t, token):
    return t + token[0:1, 0:1].astype(t.dtype)


def _finish(name, pf, recv, k_arr, w, m, v):
    if name in ("attn_w_in", "conv_w_in"):
        res = _rs_finish_adamw_t(pf, recv, k_arr, w.T, m.T, v.T, "rs_finish_adamw_" + name)
        return tuple(t.T for t in res)
    return _rs_finish_adamw(pf, recv, k_arr, w, m, v, "rs_finish_adamw_" + name)


def kernel(x, mem, positions, norm_g, mem_norm_g, w_mem_kv, attn_w_in, attn_w_out, conv_w_in, conv_w, conv_w_out, final_g, loss_target, m_norm_g, m_mem_norm_g, m_w_mem_kv, m_attn_w_in, m_attn_w_out, m_conv_w_in, m_conv_w, m_conv_w_out, m_final_g, v_norm_g, v_mem_norm_g, v_w_mem_kv, v_attn_w_in, v_attn_w_out, v_conv_w_in, v_conv_w, v_conv_w_out, v_final_g):
    px, py, pc = _position()
    me = 4 * px + 2 * py + pc
    c_arr = jnp.reshape(pc, (1,)).astype(jnp.int32)
    k_arr = jnp.reshape(2 * px + py, (1,)).astype(jnp.int32)
    x, mem, pos, tgt = x[0], mem[0], positions[0], loss_target[0]

    wg_in0 = _all_gather_relay(attn_w_in[0].astype(BF16), "gather_w_in0")
    late = [attn_w_out[0].astype(BF16), conv_w_in[0].astype(BF16), conv_w_out[0].astype(BF16),
            w_mem_kv[0].astype(BF16), w_mem_kv[1].astype(BF16), jnp.pad(conv_w[0], ((0, 5), (0, 0)))]
    lands = [lax.dynamic_update_slice(lax.empty((N_DEV,) + t.shape, t.dtype), t[None], (me, 0, 0)) for t in late]
    late_weights, late_token = _split_start(late, lands, _plan_gather_own, 4, wg_in0, "gather_late_start")

    tabs = _rope_tables(pos)
    g0, g1 = _after(norm_g[0:1], late_token), norm_g[1:2]

    hn0, hn0_t, qs, ks, vs, tabs_v, qm0, z0 = _inproj_attn(x, g0, wg_in0, tabs)
    os_, ls_ = [], []
    for j, d in enumerate(DILATIONS):
        if j == 2:
            _, lands = _split_wait(late_weights, _plan_gather_own, ls_[1], "gather_late_wait")
            late_weights, late_token = _split_start([], lands, _plan_gather_pass, 3, ls_[1], "gather_late_pass_start")
        o, l = _attn_fwd(qs[j], ks[j], vs[j], d, late_token)
        os_.append(o)
        ls_.append(l)

    _, gathered = _split_wait(late_weights, _plan_gather_pass, ls_[2], "gather_late_pass_wait")
    wg_out0, wg_in1, wg_out1, wg_kv0, wg_kv1, cw_all = gathered
    w_out1 = wg_out1.reshape(-1, wg_out1.shape[2])
    w_kv = jnp.stack([wg_kv0.reshape(-1, wg_kv0.shape[2]), wg_kv1.reshape(-1, wg_kv1.shape[2])])
    cw = cw_all[:, 0:3].transpose(1, 0, 2).reshape(3, -1)
    kv = _memkv_fwd(mem, mem_norm_g, w_kv)
    y0, y0_t, mo0, h1 = _post_attn(os_, ls_, qm0, kv[0], z0, x, wg_out0)

    hn1, hn1_t, bg, cg, u, qm1, z1 = _inproj_conv(h1, g1, wg_in1)
    y1, y1_t, mo1, dh2, dh2b, loss_acc, d_final_g = _post_conv_loss(
        bg, cg, u, qm1, kv[1], z1, h1, w_out1, cw, final_g.reshape(1, -1), tgt)

    d_w_out1 = _wgrad_rows(y1_t, dh2b, "wgrad_out1")
    dz1, dbg, dconv, dqm1, dkv1 = _bwd_post_conv(dh2b, w_out1, bg, cg, u, z1, qm1, kv[1], mo1, cw)
    dcg, du, dcw = _bwd_conv(dconv, cg, u, cw)
    dh1, dh1b, dg1, dproj1 = _dgrad_norm([(dbg, 1), (dcg, 1), (du, 1), (dqm1, 1), (dz1, 1)], wg_in1, h1, g1, dh2,
                                         "dgrad_norm_conv")
    d_w_in1 = _wgrad_shards(hn1_t, dproj1, "wgrad_in1")

    d_w_out0 = _wgrad_cols(y0_t, dh1b, wg_out0.shape[2], "wgrad_out0")

    names1 = ["conv_w_in", "conv_w_out", "attn_w_out"]
    grads1 = [d_w_in1, d_w_out1, d_w_out0]
    started, token = _split_start(grads1, [lax.empty((4,) + g.shape[1:], g.dtype) for g in grads1],
                                  _plan_to_sibling, 4, dg1, "rs1_sibling_start")

    gw = GROUP_WIDTH
    ones = (jnp.arange(gw)[:, None] // HEAD_DIM == jnp.arange(gw)[None, :] // HEAD_DIM).astype(BF16)
    ones = _after(ones, token)
    res = _bwd_post_attn(dh1b, wg_out0, z0, os_, ls_, qm0, kv[0], mo0, ones)
    dz0, dos, dls, dqm0, dkv0 = res[0], res[1:4], res[4:7], res[7], res[8]

    grads1, from_sibling = _split_wait(started, _plan_to_sibling, dz0, "rs1_sibling_wait")
    parts1 = [_rs_add_sibling(g, r, c_arr, "rs_add_sibling_" + n) for g, r, n in zip(grads1, from_sibling, names1)]
    pbs1 = [pb for _, pb in parts1]
    started, token = _split_start(pbs1, [lax.empty((3,) + p.shape[1:], p.dtype) for p in pbs1],
                                  _plan_to_chips, 3, dg1, "rs1_chips_start")

    dqs, dks, dvs = [], [], []
    for j, d in enumerate(DILATIONS):
        dq, dk, dv = _attn_bwd(qs[j], ks[j], vs[j], ls_[j], dos[j], dls[j], tabs_v[j], d, token)
        dqs.append((dq, d))
        dks.append((dk, d))
        dvs.append((dv, d))
    d_w_kv, d_mem_g = _memkv_bwd(jnp.stack([dkv0, dkv1]), w_kv, mem, mem_norm_g)
    n_kv = d_w_kv.shape[1] // N_DEV
    dproj0 = _assemble_dproj(dqs + dks + dvs + [(dqm0, 1), (dz0, 1)], wg_in0.shape[2], "assemble_dproj_attn")
    d_w_in0 = _wgrad_shards(hn0_t, dproj0, "wgrad_in0")

    names0 = ["attn_w_in", "w_mem_kv0", "w_mem_kv1"]
    grads0 = [d_w_in0, d_w_kv[0].reshape(N_DEV, n_kv, -1), d_w_kv[1].reshape(N_DEV, n_kv, -1)]
    started0, token0 = _split_start(grads0, [lax.empty((4,) + g.shape[1:], g.dtype) for g in grads0],
                                    _plan_to_sibling, 4, dg1, "rs0_sibling_start")
    _, from_chips1 = _split_wait(started, _plan_to_chips, token0, "rs1_chips_wait")
    shard = dict(attn_w_in=(attn_w_in[0], m_attn_w_in[0], v_attn_w_in[0]),
                 attn_w_out=(attn_w_out[0], m_attn_w_out[0], v_attn_w_out[0]),
                 conv_w_in=(conv_w_in[0], m_conv_w_in[0], v_conv_w_in[0]),
                 conv_w_out=(conv_w_out[0], m_conv_w_out[0], v_conv_w_out[0]),
                 w_mem_kv0=(w_mem_kv[0], m_w_mem_kv[0], v_w_mem_kv[0]), w_mem_kv1=(w_mem_kv[1], m_w_mem_kv[1], v_w_mem_kv[1]))
    big = {}
    for n, (pf, _), r in zip(names1, parts1, from_chips1):
        big[n] = _finish(n, pf, r, k_arr, *shard[n])

    grads0, from_sibling = _split_wait(started0, _plan_to_sibling, big["conv_w_out"][1], "rs0_sibling_wait")
    parts0 = [_rs_add_sibling(g, r, c_arr, "rs_add_sibling_" + n) for g, r, n in zip(grads0, from_sibling, names0)]
    pbs0 = [pb for _, pb in parts0]
    started0, token0 = _split_start(pbs0, [lax.empty((3,) + p.shape[1:], p.dtype) for p in pbs0],
                                    _plan_to_chips, 3, dg1, "rs0_chips_start")
    dx, dg0 = _dgrad_norm_dm(dproj0, wg_in0, x, g0, dh1, token0, "dgrad_norm_attn")

    small_part = jnp.concatenate([dg0, dg1, d_mem_g.reshape(2, -1), d_final_g, dcw[0:3]], axis=0)
    small_part = jnp.concatenate([small_part, jnp.broadcast_to(loss_acc[0, 0], small_part.shape)], axis=0)
    small = _sum_devices(_all_gather([small_part], "gather_small_grads")[0])
    loss = small[8, 0]
    g_conv_w = lax.dynamic_slice(small[5:8], (0, me * LANES), (3, LANES))[None]
    small_g = dict(norm_g=small[0:2], mem_norm_g=small[2:4], conv_w=g_conv_w, final_g=small[4])
    small_w = dict(norm_g=(norm_g, m_norm_g, v_norm_g), mem_norm_g=(mem_norm_g, m_mem_norm_g, v_mem_norm_g),
                   conv_w=(conv_w, m_conv_w, v_conv_w), final_g=(final_g, m_final_g, v_final_g))
    for n, (w, m, v) in small_w.items():
        big[n] = (small_g[n],) + _adamw(w, small_g[n], m, v, "adamw_" + n)

    _, from_chips0 = _split_wait(started0, _plan_to_chips, big["final_g"][1], "rs0_chips_wait")
    for n, (pf, _), r in zip(names0, parts0, from_chips0):
        big[n] = _finish(n, pf, r, k_arr, *shard[n])
    for n in ("attn_w_in", "attn_w_out", "conv_w_in", "conv_w_out"):
        big[n] = tuple(t[None] for t in big[n])
    big["w_mem_kv"] = tuple(jnp.stack([a, b]) for a, b in zip(big["w_mem_kv0"], big["w_mem_kv1"]))

    order = ["norm_g", "mem_norm_g", "w_mem_kv", "attn_w_in", "attn_w_out", "conv_w_in", "conv_w", "conv_w_out", "final_g"]
    return (loss, dx[None], *[big[n][0] for n in order], *[big[n][1] for n in order],
            *[big[n][2] for n in order], *[big[n][3] for n in order])
```

```python
import functools

import jax
import jax.numpy as jnp
from jax import lax
from jax.experimental import pallas as pl
from jax.experimental.pallas import tpu as pltpu

F32 = jnp.float32
BF16 = jnp.bfloat16

N_DEV = 8
D_MODEL = 1024
HEAD_DIM = 64
ROT_DIM = HEAD_DIM // 4
ROPE_THETA = 500000.0
DILATIONS = (1, 4, 16)
HEADS_PER_GROUP = 8
GROUP_WIDTH = HEADS_PER_GROUP * HEAD_DIM
BLOCK = 128
N_MEM = 256
MEM_HEADS = 4
MEM_WIDTH = MEM_HEADS * HEAD_DIM
CONV_WIDTH = D_MODEL
EPS = 1e-6
SCALE = HEAD_DIM ** -0.5
NEG = -1e30

ADAM_LR = 0.001
ADAM_B1 = 0.9
ADAM_B2 = 0.999
ADAM_EPS = 1e-08
ADAM_WD = 0.01
ADAM_STEP = 10

ROW_TILE = 256
LANES = 128
MESH = pl.DeviceIdType.MESH
ANY = pl.BlockSpec(memory_space=pl.ANY)


def _pallas_call(body, **kw):
    call = pl.pallas_call(body, **kw)

    def run(*args):
        pinned = [pltpu.with_memory_space_constraint(a, pltpu.HBM) if jnp.issubdtype(a.dtype, jnp.floating) else a
                  for a in args]
        return call(*pinned)

    return run


def _dot(a, b):
    return lax.dot_general(a, b, (((1,), (0,)), ((), ())), preferred_element_type=F32)


def _dot_nt(a, b):
    return lax.dot_general(a, b, (((1,), (1,)), ((), ())), preferred_element_type=F32)


def _dot_tn(a, b):
    return lax.dot_general(a, b, (((0,), (0,)), ((), ())), preferred_element_type=F32)


def _params(n_grid, vmem_mb=48):
    return pltpu.CompilerParams(dimension_semantics=("arbitrary",) * n_grid, vmem_limit_bytes=vmem_mb << 20)


def _rows(width, tm=ROW_TILE):
    return pl.BlockSpec((tm, width), lambda i: (i, 0))


def _view_rows(width, d, tm=ROW_TILE):
    return pl.BlockSpec((tm // d, d * width), lambda i: (i, 0))


def _whole(shape):
    return pl.BlockSpec(shape, lambda *_: (0,) * len(shape))


def _resident(shape):
    return pl.BlockSpec(shape, lambda *_: (0,) * len(shape), pipeline_mode=pl.Buffered(1))


def _sds(shape, dtype):
    return pltpu.HBM(shape, dtype)


def _plain(shape, dtype):
    return jax.ShapeDtypeStruct(shape, dtype)


def _silu_parts(z):
    sg = jax.nn.sigmoid(z)
    return z * sg, sg * (1.0 + z * (1.0 - sg))


def _to_view(scr, val, out_ref, d):
    tm, w = val.shape
    if d == 1:
        out_ref[...] = val.astype(out_ref.dtype)
        return
    for cb in range(w // LANES):
        scr[cb] = val[:, cb * LANES:(cb + 1) * LANES]
    for r in range(d):
        for cb in range(w // LANES):
            lo = r * w + cb * LANES
            out_ref[:, lo:lo + LANES] = scr[cb, pl.ds(r, tm // d, stride=d), :].astype(out_ref.dtype)


def _from_view(scr, in_ref, d):
    if d == 1:
        return in_ref[...].astype(F32)
    nc, tm, _ = scr.shape
    w = nc * LANES
    for r in range(d):
        for cb in range(nc):
            lo = r * w + cb * LANES
            scr[cb, pl.ds(r, tm // d, stride=d), :] = in_ref[:, lo:lo + LANES].astype(F32)
    return jnp.concatenate([scr[cb] for cb in range(nc)], axis=1)


def _view_scratch(width, tm=ROW_TILE):
    return pltpu.VMEM((width // LANES, tm, LANES), F32)


def _rope_tables(pos):
    half = ROT_DIM // 2
    inv_freq = ROPE_THETA ** (-jnp.arange(half, dtype=F32) * (2.0 / ROT_DIM))
    ang = pos.astype(F32)[:, None] * inv_freq
    cos, sin = jnp.cos(ang), jnp.sin(ang)
    s = pos.shape[0]
    z8 = jnp.zeros((s, half), F32)
    rest = HEAD_DIM - ROT_DIM
    cosf = jnp.concatenate([cos, cos, jnp.ones((s, rest), F32)], axis=1)
    sa = jnp.concatenate([-sin, z8, jnp.zeros((s, rest), F32)], axis=1)
    sb = jnp.concatenate([z8, sin, jnp.zeros((s, rest), F32)], axis=1)
    return tuple(jnp.tile(t, (1, LANES // HEAD_DIM)) for t in (cosf, sa, sb))


def _rope_fwd(t, cv, sav, sbv):
    w = t.shape[1]
    return t * cv + pltpu.roll(t, w - ROT_DIM // 2, 1) * sav + pltpu.roll(t, ROT_DIM // 2, 1) * sbv


def _rope_bwd(g, cv, sav, sbv):
    w = g.shape[1]
    return g * cv + pltpu.roll(g * sav, ROT_DIM // 2, 1) + pltpu.roll(g * sbv, w - ROT_DIM // 2, 1)


def _project(hn, wg_ref, proj_scr):
    c = wg_ref.shape[2]
    for j in range(N_DEV):
        proj_scr[:, j * c:(j + 1) * c] = _dot(hn, wg_ref[j])


def _inproj_attn(x, g, wg, tabs):
    s, d_model = x.shape
    gw = GROUP_WIDTH
    n = N_DEV * wg.shape[2]
    nz = n - 9 * gw - MEM_WIDTH
    reps = gw // LANES
    tm = ROW_TILE

    def body(x_ref, g_ref, w_ref, c_ref, sa_ref, sb_ref, hn_ref, hnt_ref, *rest):
        outs, (proj, scr, tscr) = rest[:-3], rest[-3:]
        q_refs, k_refs, v_refs, t_refs, qm_ref, z_ref = outs[0:3], outs[3:6], outs[6:9], outs[9:18], outs[18], outs[19]
        xb = x_ref[...]
        r = lax.rsqrt(jnp.mean(xb * xb, axis=-1, keepdims=True) + EPS)
        hn = ((xb * r) * g_ref[...]).astype(BF16)
        hn_ref[...] = hn
        hnt_ref[...] = hn.T
        _project(hn, w_ref, proj)
        tab = (c_ref[...], sa_ref[...], sb_ref[...])
        cv, sav, sbv = [jnp.tile(t, (1, reps)) for t in tab]
        for j, d in enumerate(DILATIONS):
            tq = _rope_fwd(proj[:, j * gw:(j + 1) * gw], cv, sav, sbv)
            _to_view(scr, tq * SCALE, q_refs[j], d)
            tk = _rope_fwd(proj[:, (3 + j) * gw:(4 + j) * gw], cv, sav, sbv)
            _to_view(scr, tk, k_refs[j], d)
            _to_view(scr, proj[:, (6 + j) * gw:(7 + j) * gw], v_refs[j], d)
            for i in range(3):
                _to_view(tscr, tab[i], t_refs[3 * j + i], d)
        qm_ref[...] = proj[:, 9 * gw:9 * gw + MEM_WIDTH].astype(BF16)
        z_ref[...] = proj[:, 9 * gw + MEM_WIDTH:]

    views = [_sds((s // d, d * gw), BF16) for d in DILATIONS]
    tviews = [_sds((s // d, d * LANES), F32) for d in DILATIONS for _ in range(3)]
    out_shape = ([_sds((s, d_model), BF16), _sds((d_model, s), BF16)] + views * 3 + tviews
                 + [_sds((s, MEM_WIDTH), BF16), _sds((s, nz), F32)])
    vspecs = [_view_rows(gw, d, tm) for d in DILATIONS]
    tspecs = [_view_rows(LANES, d, tm) for d in DILATIONS for _ in range(3)]
    out_specs = ([_rows(d_model, tm), pl.BlockSpec((d_model, tm), lambda i: (0, i))] + vspecs * 3 + tspecs
                 + [_rows(MEM_WIDTH, tm), _rows(nz, tm)])
    res = _pallas_call(
        body, name="inproj_attn", grid=(s // tm,), out_shape=out_shape,
        in_specs=[_rows(d_model, tm), _whole((1, d_model)), _resident(wg.shape)] + [_rows(LANES, tm)] * 3,
        out_specs=out_specs,
        scratch_shapes=[pltpu.VMEM((tm, n), F32), _view_scratch(gw, tm), _view_scratch(LANES, tm)],
        compiler_params=_params(1, 60),
    )(x, g, wg, *tabs)
    tabs_v = [res[11 + 3 * j:14 + 3 * j] for j in range(3)]
    return res[0], res[1], res[2:5], res[5:8], res[8:11], tabs_v, res[20], res[21]


def _band_mask(n_keys):
    qi = lax.broadcasted_iota(jnp.int32, (BLOCK, n_keys), 0)
    kj = lax.broadcasted_iota(jnp.int32, (BLOCK, n_keys), 1)
    if n_keys == BLOCK:
        return kj <= qi
    return jnp.logical_or(jnp.logical_and(kj < BLOCK, kj >= qi), jnp.logical_and(kj >= BLOCK, (kj - BLOCK) <= qi))


def _low_head_lanes():
    return lax.broadcasted_iota(jnp.int32, (1, LANES), 1) < HEAD_DIM


def _split_pair(t, low):
    zero = jnp.zeros_like(t)
    return jnp.where(low, t, zero), jnp.where(low, zero, t)


def _pair_specs(d, nb, w):
    if nb == 1:
        return pl.BlockSpec((BLOCK, 2 * w), lambda n: (0, n)), None
    half = nb // 2
    two = pl.BlockSpec((2 * BLOCK, w), lambda n: (n % half, n // half))
    before = pl.BlockSpec((BLOCK, w), lambda n: (jnp.maximum(2 * (n % half) - 1, 0), n // half))
    return two, before


def _head_tiles(w, col0):
    return ([slice(p * LANES, (p + 1) * LANES) for p in range(w // LANES)],
            [slice(col0 + p * LANES, col0 + (p + 1) * LANES) for p in range(w // LANES)])


def _attend_fwd(q_ref, o_ref, lse_ref, rows, col0, kk, vv):
    w = kk.shape[1]
    valid = _band_mask(kk.shape[0])
    low = _low_head_lanes()
    pairs, qcols = _head_tiles(w, col0)
    qs_ = [h for qc in qcols for h in _split_pair(q_ref[rows, qc], low)]
    k2s = [kk[:, pr] for pr in pairs for _ in range(2)]
    scs = [jnp.where(valid, _dot_nt(qh, k2), NEG) for qh, k2 in zip(qs_, k2s)]
    ms = [jnp.max(sc, axis=-1, keepdims=True) for sc in scs]
    ps = [jnp.exp(sc - m) for sc, m in zip(scs, ms)]
    ls = [jnp.sum(p, axis=-1, keepdims=True) for p in ps]
    pns = [(p * (1.0 / l)).astype(BF16) for p, l in zip(ps, ls)]
    for i, (pr, qc) in enumerate(zip(pairs, qcols)):
        v2 = vv[:, pr]
        a, b = 2 * i, 2 * i + 1
        o_ref[rows, qc] = jnp.where(low, _dot(pns[a], v2), _dot(pns[b], v2))
        lse_ref[rows, qc] = jnp.where(low, ms[a] + jnp.log(ls[a]), ms[b] + jnp.log(ls[b]))


TOP, BOTTOM = slice(0, BLOCK), slice(BLOCK, 2 * BLOCK)


def _attn_fwd(q, k, v, d, token):
    ln, dw = q.shape
    w = dw // d
    nb = ln // BLOCK
    two, before = _pair_specs(d, nb, w)

    def body_streams(_, q_ref, kc_ref, vc_ref, o_ref, lse_ref):
        for sb in range(2):
            cols = slice(sb * w, (sb + 1) * w)
            _attend_fwd(q_ref, o_ref, lse_ref, TOP, sb * w, kc_ref[:, cols], vc_ref[:, cols])

    def body_blocks(_, q_ref, kp_ref, kc_ref, vp_ref, vc_ref, o_ref, lse_ref):
        first = pl.program_id(0) % (nb // 2) == 0
        pl.when(first)(lambda: _attend_fwd(q_ref, o_ref, lse_ref, TOP, 0, kc_ref[TOP, :], vc_ref[TOP, :]))
        pl.when(jnp.logical_not(first))(lambda: _attend_fwd(
            q_ref, o_ref, lse_ref, TOP, 0, jnp.concatenate([kp_ref[...], kc_ref[TOP, :]], axis=0),
            jnp.concatenate([vp_ref[...], vc_ref[TOP, :]], axis=0)))
        _attend_fwd(q_ref, o_ref, lse_ref, BOTTOM, 0, kc_ref[...], vc_ref[...])

    if nb == 1:
        body, in_specs, args = body_streams, [ANY, two, two, two], (token, q, k, v)
    else:
        body, in_specs, args = body_blocks, [ANY, two, before, two, before, two], (token, q, k, k, v, v)
    return _pallas_call(
        body, name=f"attn_fwd_d{d}", grid=(d * nb // 2,), out_shape=[_sds((ln, dw), F32)] * 2,
        in_specs=in_specs, out_specs=[two, two], compiler_params=_params(1, 32),
    )(*args)


def _memkv_fwd(mem, g, w):
    n_layers = w.shape[0]

    def body(mem_ref, g_ref, w_ref, kv_ref):
        mb = mem_ref[...]
        r = lax.rsqrt(jnp.mean(mb * mb, axis=-1, keepdims=True) + EPS)
        mn = ((mb * r) * g_ref[...]).astype(BF16)
        kv_ref[...] = _dot(mn, w_ref[...]).astype(BF16)

    return _pallas_call(
        body, name="memkv_fwd", grid=(n_layers,),
        out_shape=_plain((n_layers, N_MEM, 2 * MEM_WIDTH), BF16),
        in_specs=[_whole(mem.shape), pl.BlockSpec((None, 1, D_MODEL), lambda l: (l, 0, 0)),
                  pl.BlockSpec((None, D_MODEL, 2 * MEM_WIDTH), lambda l: (l, 0, 0))],
        out_specs=pl.BlockSpec((None, N_MEM, 2 * MEM_WIDTH), lambda l: (l, 0, 0)),
        compiler_params=_params(1, 32),
    )(mem, g.reshape(n_layers, 1, D_MODEL), w)


def _mix_groups(os_, ls_):
    mx = jnp.maximum(jnp.maximum(ls_[0], ls_[1]), ls_[2])
    es = [jnp.exp(t - mx) for t in ls_]
    inv = 1.0 / (es[0] + es[1] + es[2])
    ws = [e * inv for e in es]
    mix = ws[0] * os_[0] + ws[1] * os_[1] + ws[2] * os_[2]
    return ws, mix


MEM_PAIRS = [slice(p * LANES, (p + 1) * LANES) for p in range(MEM_WIDTH // LANES)]


def _mem_probs(qhs, k2s):
    scs = [_dot_nt(qh, k2) * SCALE for qh, k2 in zip(qhs, k2s)]
    es = [jnp.exp(sc - jnp.max(sc, axis=-1, keepdims=True)) for sc in scs]
    return [e * (1.0 / jnp.sum(e, axis=-1, keepdims=True)) for e in es]


def _mem_attn_into(qm, kv_ref, mo_ref):
    low = _low_head_lanes()
    qhs = [h for pr in MEM_PAIRS for h in _split_pair(qm[:, pr], low)]
    k2s = [kv_ref[:, pr] for pr in MEM_PAIRS for _ in range(2)]
    ps = [p.astype(BF16) for p in _mem_probs(qhs, k2s)]
    for i, pr in enumerate(MEM_PAIRS):
        v2 = kv_ref[:, MEM_WIDTH + i * LANES:MEM_WIDTH + (i + 1) * LANES]
        mo_ref[:, pr] = jnp.where(low, _dot(ps[2 * i], v2), _dot(ps[2 * i + 1], v2))


def _mem_attn_bwd(qm, kv_ref, dmem, dqm_ref, dkv_ref):
    low = _low_head_lanes()
    dmb = dmem.astype(BF16)
    vps = [slice(MEM_WIDTH + i * LANES, MEM_WIDTH + (i + 1) * LANES) for i in range(len(MEM_PAIRS))]
    qhs = [h for pr in MEM_PAIRS for h in _split_pair(qm[:, pr], low)]
    dhs = [h for pr in MEM_PAIRS for h in _split_pair(dmb[:, pr], low)]
    k2s = [kv_ref[:, pr] for pr in MEM_PAIRS for _ in range(2)]
    v2s = [kv_ref[:, vp] for vp in vps for _ in range(2)]
    ps = _mem_probs(qhs, k2s)
    dps = [_dot_nt(dh, v2) for dh, v2 in zip(dhs, v2s)]
    dss = [(p * (dp - jnp.sum(dp * p, axis=-1, keepdims=True)) * SCALE).astype(BF16) for p, dp in zip(ps, dps)]
    pbs = [p.astype(BF16) for p in ps]
    for i, (pr, vp) in enumerate(zip(MEM_PAIRS, vps)):
        a, b = 2 * i, 2 * i + 1
        dqm_ref[:, pr] = jnp.where(low, _dot(dss[a], k2s[a]), _dot(dss[b], k2s[b])).astype(BF16)
        dkv_ref[:, pr] += _dot_tn(dss[a], qhs[a]) + _dot_tn(dss[b], qhs[b])
        dkv_ref[:, vp] += _dot_tn(pbs[a], dhs[a]) + _dot_tn(pbs[b], dhs[b])


def _post_attn(os_, ls_, qm, kv, z, x, wg_out):
    s, d_model = x.shape
    gw = GROUP_WIDTH
    nb = gw + MEM_WIDTH
    c = wg_out.shape[2]
    tm = ROW_TILE

    def body(o0, o1, o2, l0, l1, l2, qm_ref, kv_ref, z_ref, x_ref, w_ref, y_ref, yt_ref, mo_ref, h_ref, s0, s1):
        ov, lv = [], []
        for o_ref, l_ref, d in zip((o0, o1, o2), (l0, l1, l2), DILATIONS):
            ov.append(_from_view(s0, o_ref, d))
            lv.append(_from_view(s1, l_ref, d))
        _, mix = _mix_groups(ov, lv)
        _mem_attn_into(qm_ref[...], kv_ref, mo_ref)
        sz, _ = _silu_parts(z_ref[...])
        y_ref[:, :gw] = (mix * sz[:, :gw]).astype(BF16)
        y_ref[:, gw:] = (mo_ref[...] * sz[:, gw:]).astype(BF16)
        y = y_ref[...]
        yt_ref[...] = y.T
        for j in range(N_DEV):
            h_ref[:, j * c:(j + 1) * c] = x_ref[:, j * c:(j + 1) * c] + _dot(y, w_ref[j])

    vspecs = [_view_rows(gw, d) for d in DILATIONS]
    return _pallas_call(
        body, name="post_attn", grid=(s // tm,),
        out_shape=[_sds((s, nb), BF16), _sds((nb, s), BF16), _sds((s, MEM_WIDTH), F32), _sds((s, d_model), F32)],
        in_specs=vspecs * 2 + [_rows(MEM_WIDTH), _whole(kv.shape), _rows(nb), _rows(d_model), _whole(wg_out.shape)],
        out_specs=[_rows(nb), pl.BlockSpec((nb, tm), lambda i: (0, i)), _rows(MEM_WIDTH), _rows(d_model)],
        scratch_shapes=[_view_scratch(gw), _view_scratch(gw)],
        compiler_params=_params(1, 40),
    )(*os_, *ls_, qm, kv, z, x, wg_out)


def _inproj_conv(x, g, wg):
    s, d_model = x.shape
    c = CONV_WIDTH
    n = N_DEV * wg.shape[2]
    nz = n - 3 * c - MEM_WIDTH
    tm = ROW_TILE

    def body(x_ref, g_ref, w_ref, hn_ref, hnt_ref, bg_ref, cg_ref, u_ref, qm_ref, z_ref, proj):
        xb = x_ref[...]
        r = lax.rsqrt(jnp.mean(xb * xb, axis=-1, keepdims=True) + EPS)
        hn = ((xb * r) * g_ref[...]).astype(BF16)
        hn_ref[...] = hn
        hnt_ref[...] = hn.T
        _project(hn, w_ref, proj)
        bg_ref[...] = proj[:, 0:c]
        cg_ref[...] = proj[:, c:2 * c]
        u_ref[...] = proj[:, 2 * c:3 * c]
        qm_ref[...] = proj[:, 3 * c:3 * c + MEM_WIDTH].astype(BF16)
        z_ref[...] = proj[:, 3 * c + MEM_WIDTH:]

    return _pallas_call(
        body, name="inproj_conv", grid=(s // tm,),
        out_shape=[_sds((s, d_model), BF16), _sds((d_model, s), BF16)] + [_sds((s, c), F32)] * 3
                  + [_sds((s, MEM_WIDTH), BF16), _sds((s, nz), F32)],
        in_specs=[_rows(d_model), _whole((1, d_model)), _whole(wg.shape)],
        out_specs=[_rows(d_model), pl.BlockSpec((d_model, tm), lambda i: (0, i))] + [_rows(c)] * 3
                  + [_rows(MEM_WIDTH), _rows(nz)],
        scratch_shapes=[pltpu.VMEM((tm, n), F32)],
        compiler_params=_params(1, 60),
    )(x, g, wg)


HALO = 8


def _halo_before(width, tm=ROW_TILE):
    return pl.BlockSpec((HALO, width), lambda i: (jnp.maximum(i * (tm // HALO) - 1, 0), 0))


def _halo_after(width, n_rows, tm=ROW_TILE):
    return pl.BlockSpec((HALO, width), lambda i: (jnp.minimum((i + 1) * (tm // HALO), n_rows // HALO - 1), 0))


def _conv_taps(cg_ref, u_ref, cgh_ref, uh_ref, i):
    a = cg_ref[...] * u_ref[...]
    ah = jnp.where(i > 0, cgh_ref[...] * uh_ref[...], 0.0)
    row = lax.broadcasted_iota(jnp.int32, a.shape, 0)
    a1 = jnp.where(row == 0, ah[HALO - 1:HALO], pltpu.roll(a, 1, 0))
    a2 = jnp.where(row == 0, ah[HALO - 2:HALO - 1], jnp.where(row == 1, ah[HALO - 1:HALO], pltpu.roll(a, 2, 0)))
    return a, a1, a2


def _post_conv_loss(bg, cg, u, qm, kv, z, h1, w_out, cw, gf, tgt):
    s, d = h1.shape
    c = CONV_WIDTH
    nb = c + MEM_WIDTH
    tm = ROW_TILE

    def body(bg_ref, cg_ref, u_ref, cgh_ref, uh_ref, qm_ref, kv_ref, z_ref, h_ref, w_ref, cw_ref, gf_ref, t_ref,
             y_ref, yt_ref, mo_ref, dh_ref, dhb_ref, loss_ref, dgf_ref):
        i = pl.program_id(0)
        a, a1, a2 = _conv_taps(cg_ref, u_ref, cgh_ref, uh_ref, i)
        conv = cw_ref[0:1, :] * a2 + cw_ref[1:2, :] * a1 + cw_ref[2:3, :] * a
        mix = bg_ref[...] * conv
        _mem_attn_into(qm_ref[...], kv_ref, mo_ref)
        sz, _ = _silu_parts(z_ref[...])
        y_ref[:, :c] = (mix * sz[:, :c]).astype(BF16)
        y_ref[:, c:] = (mo_ref[...] * sz[:, c:]).astype(BF16)
        y = y_ref[...]
        yt_ref[...] = y.T
        h2 = h_ref[...] + _dot(y, w_ref[...])
        r = lax.rsqrt(jnp.mean(h2 * h2, axis=-1, keepdims=True) + EPS)
        nh = h2 * r
        gfv = gf_ref[...]
        diff = nh * gfv - t_ref[...]
        dout = diff * (1.0 / d)
        dn = dout * gfv
        dh2 = r * dn - h2 * ((r * r * r) * jnp.mean(dn * h2, axis=-1, keepdims=True))
        dh_ref[...] = dh2
        dhb_ref[...] = dh2.astype(BF16)

        @pl.when(i == 0)
        def _():
            loss_ref[...] = jnp.zeros_like(loss_ref)
            dgf_ref[...] = jnp.zeros_like(dgf_ref)

        loss_ref[...] += 0.5 * jnp.sum(jnp.mean(diff * diff, axis=-1, keepdims=True))
        dgf_ref[...] += jnp.sum(dout * nh, axis=0, keepdims=True)

    return _pallas_call(
        body, name="post_conv_loss", grid=(s // tm,),
        out_shape=[_sds((s, nb), BF16), _sds((nb, s), BF16), _sds((s, MEM_WIDTH), F32), _sds((s, d), F32),
                   _sds((s, d), BF16), _plain((8, LANES), F32), _plain((1, d), F32)],
        in_specs=[_rows(c)] * 3 + [_halo_before(c)] * 2 + [_rows(MEM_WIDTH), _whole(kv.shape), _rows(nb), _rows(d),
                  _whole(w_out.shape), _whole(cw.shape), _whole((1, d)), _rows(d)],
        out_specs=[_rows(nb), pl.BlockSpec((nb, tm), lambda i: (0, i)), _rows(MEM_WIDTH), _rows(d), _rows(d),
                   _whole((8, LANES)), _whole((1, d))],
        compiler_params=_params(1, 48),
    )(bg, cg, u, cg, u, qm, kv, z, h1, w_out, cw, gf, tgt)


def _bwd_post_conv(dhb, w_out, bg, cg, u, z, qm, kv, mo, cw):
    s = dhb.shape[0]
    c = CONV_WIDTH
    nb = c + MEM_WIDTH

    def body(dh_ref, w_ref, bg_ref, cg_ref, u_ref, cgh_ref, uh_ref, z_ref, qm_ref, kv_ref, mo_ref, cw_ref,
             dz_ref, dbg_ref, dc_ref, dqm_ref, dkv_ref):
        i = pl.program_id(0)

        @pl.when(i == 0)
        def _():
            dkv_ref[...] = jnp.zeros_like(dkv_ref)

        dy = _dot_nt(dh_ref[...], w_ref[...])
        sz, dsz = _silu_parts(z_ref[...])
        a, a1, a2 = _conv_taps(cg_ref, u_ref, cgh_ref, uh_ref, i)
        conv = cw_ref[0:1, :] * a2 + cw_ref[1:2, :] * a1 + cw_ref[2:3, :] * a
        bgv = bg_ref[...]
        dz_ref[:, :c] = (dy[:, :c] * (bgv * conv) * dsz[:, :c]).astype(BF16)
        dz_ref[:, c:] = (dy[:, c:] * mo_ref[...] * dsz[:, c:]).astype(BF16)
        dbr = dy * sz
        dmix = dbr[:, :c]
        dbg_ref[...] = (dmix * conv).astype(BF16)
        dc_ref[...] = dmix * bgv
        _mem_attn_bwd(qm_ref[...], kv_ref, dbr[:, c:], dqm_ref, dkv_ref)

    return _pallas_call(
        body, name="bwd_post_conv", grid=(s // ROW_TILE,),
        out_shape=[_sds((s, nb), BF16), _sds((s, c), BF16), _sds((s, c), F32), _sds((s, MEM_WIDTH), BF16),
                   _plain(kv.shape, F32)],
        in_specs=[_rows(D_MODEL), _whole(w_out.shape)] + [_rows(c)] * 3 + [_halo_before(c)] * 2
                 + [_rows(nb), _rows(MEM_WIDTH), _whole(kv.shape), _rows(MEM_WIDTH), _whole(cw.shape)],
        out_specs=[_rows(nb), _rows(c), _rows(c), _rows(MEM_WIDTH), _whole(kv.shape)],
        compiler_params=_params(1, 48),
    )(dhb, w_out, bg, cg, u, cg, u, z, qm, kv, mo, cw)


def _bwd_conv(dconv, cg, u, cw):
    s, c = dconv.shape
    tm = ROW_TILE
    last = s // tm - 1

    def body(dc_ref, dcn_ref, cg_ref, u_ref, cgh_ref, uh_ref, cw_ref, dcg_ref, du_ref, dcw_ref):
        i = pl.program_id(0)

        @pl.when(i == 0)
        def _():
            dcw_ref[...] = jnp.zeros_like(dcw_ref)

        dc = dc_ref[...]
        dcn = jnp.where(i < last, dcn_ref[...], 0.0)
        row = lax.broadcasted_iota(jnp.int32, dc.shape, 0)
        d1 = jnp.where(row == tm - 1, dcn[0:1], pltpu.roll(dc, tm - 1, 0))
        d2 = jnp.where(row == tm - 1, dcn[1:2], jnp.where(row == tm - 2, dcn[0:1], pltpu.roll(dc, tm - 2, 0)))
        da = cw_ref[2:3, :] * dc + cw_ref[1:2, :] * d1 + cw_ref[0:1, :] * d2
        a, a1, a2 = _conv_taps(cg_ref, u_ref, cgh_ref, uh_ref, i)
        dcg_ref[...] = (da * u_ref[...]).astype(BF16)
        du_ref[...] = (da * cg_ref[...]).astype(BF16)
        dcw_ref[0:1, :] += jnp.sum(dc * a2, axis=0, keepdims=True)
        dcw_ref[1:2, :] += jnp.sum(dc * a1, axis=0, keepdims=True)
        dcw_ref[2:3, :] += jnp.sum(dc * a, axis=0, keepdims=True)

    return _pallas_call(
        body, name="bwd_conv", grid=(s // tm,),
        out_shape=[_sds((s, c), BF16), _sds((s, c), BF16), _plain((8, c), F32)],
        in_specs=[_rows(c), _halo_after(c, s), _rows(c), _rows(c), _halo_before(c), _halo_before(c), _whole(cw.shape)],
        out_specs=[_rows(c), _rows(c), _whole((8, c))], compiler_params=_params(1, 40),
    )(dconv, dconv, cg, u, cg, u, cw)


def _dgrad_norm(pieces, wg, h, g, dres, name):
    s, d_model = h.shape
    c = wg.shape[2]
    n = N_DEV * c
    tm = ROW_TILE
    widths = [p.shape[1] // d for p, d in pieces]
    assert sum(widths) == n
    n_p = len(pieces)

    def body(*refs):
        p_refs = refs[:n_p]
        w_ref, h_ref, g_ref, dr_ref, dh_ref, dhb_ref, dg_ref, dpd_ref, dp, scr = refs[n_p:]

        @pl.when(pl.program_id(0) == 0)
        def _():
            dg_ref[...] = jnp.zeros_like(dg_ref)

        off = 0
        for p_ref, (_, d), wd in zip(p_refs, pieces, widths):
            if d == 1:
                dp[:, off:off + wd] = p_ref[...]
            else:
                dp[:, off:off + wd] = _from_view(scr, p_ref, d).astype(BF16)
            off += wd
        dhn = jnp.zeros((tm, d_model), F32)
        for j in range(N_DEV):
            dpj = dp[:, j * c:(j + 1) * c]
            dpd_ref[j] = dpj
            dhn += _dot_nt(dpj, w_ref[j])
        hb = h_ref[...]
        r = lax.rsqrt(jnp.mean(hb * hb, axis=-1, keepdims=True) + EPS)
        dg_ref[...] += jnp.sum(dhn * (hb * r), axis=0, keepdims=True)
        dn = dhn * g_ref[...]
        dh = dr_ref[...] + r * dn - hb * ((r * r * r) * jnp.mean(dn * hb, axis=-1, keepdims=True))
        dh_ref[...] = dh
        dhb_ref[...] = dh.astype(BF16)

    p_specs = [_view_rows(wd, d) for (_, d), wd in zip(pieces, widths)]
    return _pallas_call(
        body, name=name, grid=(s // tm,),
        out_shape=[_plain((s, d_model), F32), _sds((s, d_model), BF16), _plain((1, d_model), F32), _sds((N_DEV, s, c), BF16)],
        in_specs=p_specs + [_whole(wg.shape), _rows(d_model), _whole((1, d_model)), _rows(d_model)],
        out_specs=[_rows(d_model), _rows(d_model), _whole((1, d_model)), pl.BlockSpec((N_DEV, tm, c), lambda i: (0, i, 0))],
        scratch_shapes=[pltpu.VMEM((tm, n), BF16), _view_scratch(GROUP_WIDTH)],
        compiler_params=_params(1, 60),
    )(*[p for p, _ in pieces], wg, h, g, dres)


def _assemble_dproj(pieces, c, name):
    n = N_DEV * c
    tm = ROW_TILE
    widths = [p.shape[1] // d for p, d in pieces]
    assert sum(widths) == n
    s = pieces[0][0].shape[0] * pieces[0][1]
    n_p = len(pieces)

    def body(*refs):
        p_refs, (dpd_ref, dp, scr) = refs[:n_p], refs[n_p:]
        off = 0
        for p_ref, (_, d), wd in zip(p_refs, pieces, widths):
            if d == 1:
                dp[:, off:off + wd] = p_ref[...]
            else:
                dp[:, off:off + wd] = _from_view(scr, p_ref, d).astype(BF16)
            off += wd
        for j in range(N_DEV):
            dpd_ref[j] = dp[:, j * c:(j + 1) * c]

    return _pallas_call(
        body, name=name, grid=(s // tm,), out_shape=_sds((N_DEV, s, c), BF16),
        in_specs=[_view_rows(wd, d) for (_, d), wd in zip(pieces, widths)],
        out_specs=pl.BlockSpec((N_DEV, tm, c), lambda i: (0, i, 0)),
        scratch_shapes=[pltpu.VMEM((tm, n), BF16), _view_scratch(GROUP_WIDTH)],
        compiler_params=_params(1, 40),
    )(*[p for p, _ in pieces])


def _dgrad_norm_dm(dproj_dm, wg, h, g, dres, token, name):
    s, d_model = h.shape
    c = wg.shape[2]
    tm = ROW_TILE

    def body(_, dp_ref, w_ref, h_ref, g_ref, dr_ref, dh_ref, dg_ref):
        @pl.when(pl.program_id(0) == 0)
        def _():
            dg_ref[...] = jnp.zeros_like(dg_ref)

        dhn = jnp.zeros((tm, d_model), F32)
        for j in range(N_DEV):
            dhn += _dot_nt(dp_ref[j], w_ref[j])
        hb = h_ref[...]
        r = lax.rsqrt(jnp.mean(hb * hb, axis=-1, keepdims=True) + EPS)
        dg_ref[...] += jnp.sum(dhn * (hb * r), axis=0, keepdims=True)
        dn = dhn * g_ref[...]
        dh_ref[...] = dr_ref[...] + r * dn - hb * ((r * r * r) * jnp.mean(dn * hb, axis=-1, keepdims=True))

    return _pallas_call(
        body, name=name, grid=(s // tm,),
        out_shape=[_plain((s, d_model), F32), _plain((1, d_model), F32)],
        in_specs=[ANY, pl.BlockSpec((N_DEV, tm, c), lambda i: (0, i, 0)), _whole(wg.shape), _rows(d_model),
                  _whole((1, d_model)), _rows(d_model)],
        out_specs=[_rows(d_model), _whole((1, d_model))],
        compiler_params=_params(1, 60),
    )(token, dproj_dm, wg, h, g, dres)


def _wgrad_shards(a_t, b_dm, name):
    m, s = a_t.shape
    c = b_dm.shape[2]

    def body(a_ref, b_ref, o_ref):
        o_ref[...] = _dot(a_ref[...], b_ref[...]).astype(BF16)

    return _pallas_call(
        body, name=name, grid=(N_DEV,), out_shape=_sds((N_DEV, m, c), BF16),
        in_specs=[_whole(a_t.shape), pl.BlockSpec((None, s, c), lambda j: (j, 0, 0))],
        out_specs=pl.BlockSpec((None, m, c), lambda j: (j, 0, 0)), compiler_params=_params(1, 40),
    )(a_t, b_dm)


def _wgrad_cols(a_t, b, c, name):
    m, s = a_t.shape

    def body(a_ref, b_ref, o_ref):
        o_ref[...] = _dot(a_ref[...], b_ref[...]).astype(BF16)

    return _pallas_call(
        body, name=name, grid=(N_DEV,), out_shape=_sds((N_DEV, m, c), BF16),
        in_specs=[_whole(a_t.shape), pl.BlockSpec((s, c), lambda j: (0, j))],
        out_specs=pl.BlockSpec((None, m, c), lambda j: (j, 0, 0)), compiler_params=_params(1, 40),
    )(a_t, b)


def _wgrad_rows(a_t, b, name):
    m, s = a_t.shape
    n = b.shape[1]
    mr = m // N_DEV

    def body(a_ref, b_ref, o_ref):
        o_ref[...] = _dot(a_ref[...], b_ref[...]).astype(BF16)

    return _pallas_call(
        body, name=name, grid=(N_DEV,), out_shape=_sds((N_DEV, mr, n), BF16),
        in_specs=[pl.BlockSpec((mr, s), lambda j: (j, 0)), _whole(b.shape)],
        out_specs=pl.BlockSpec((None, mr, n), lambda j: (j, 0, 0)), compiler_params=_params(1, 40),
    )(a_t, b)


def _memkv_bwd(dkv, w, mem, g):
    n_layers = w.shape[0]

    def body(dkv_ref, w_ref, mem_ref, g_ref, dw_ref, dg_ref):
        mb = mem_ref[...]
        r = lax.rsqrt(jnp.mean(mb * mb, axis=-1, keepdims=True) + EPS)
        nm = mb * r
        mn = (nm * g_ref[...]).astype(BF16)
        dkvb = dkv_ref[...].astype(BF16)
        dw_ref[...] = _dot_tn(mn, dkvb).astype(BF16)
        dmn = _dot_nt(dkvb, w_ref[...])
        dg_ref[...] = jnp.sum(dmn * nm, axis=0, keepdims=True)

    lay = lambda *shape: pl.BlockSpec((None,) + shape, lambda l: (l, 0, 0))
    return _pallas_call(
        body, name="memkv_bwd", grid=(n_layers,),
        out_shape=[_plain((n_layers, D_MODEL, 2 * MEM_WIDTH), BF16), _plain((n_layers, 1, D_MODEL), F32)],
        in_specs=[lay(N_MEM, 2 * MEM_WIDTH), lay(D_MODEL, 2 * MEM_WIDTH), _whole(mem.shape), lay(1, D_MODEL)],
        out_specs=[lay(D_MODEL, 2 * MEM_WIDTH), lay(1, D_MODEL)], compiler_params=_params(1, 32),
    )(dkv, w, mem, g.reshape(n_layers, 1, D_MODEL))


def _bwd_post_attn(dhb, wg_out, z, os_, ls_, qm, kv, mo, head_ones):
    s = dhb.shape[0]
    gw = GROUP_WIDTH
    nb = gw + MEM_WIDTH
    c = wg_out.shape[2]
    tm = ROW_TILE

    def body(dh_ref, w_ref, z_ref, o0, o1, o2, l0, l1, l2, qm_ref, kv_ref, mo_ref, bd_ref,
             dz_ref, do0, do1, do2, dl0, dl1, dl2, dqm_ref, dkv_ref, s0, s1):
        @pl.when(pl.program_id(0) == 0)
        def _():
            dkv_ref[...] = jnp.zeros_like(dkv_ref)

        dy = jnp.zeros((tm, nb), F32)
        for j in range(N_DEV):
            dy += _dot_nt(dh_ref[:, j * c:(j + 1) * c], w_ref[j])
        ov, lv = [], []
        for o_ref, l_ref, d in zip((o0, o1, o2), (l0, l1, l2), DILATIONS):
            ov.append(_from_view(s0, o_ref, d))
            lv.append(_from_view(s1, l_ref, d))
        ws, mix = _mix_groups(ov, lv)
        sz, dsz = _silu_parts(z_ref[...])
        dz_ref[:, :gw] = (dy[:, :gw] * mix * dsz[:, :gw]).astype(BF16)
        dz_ref[:, gw:] = (dy[:, gw:] * mo_ref[...] * dsz[:, gw:]).astype(BF16)
        dbr = dy * sz
        dmix = dbr[:, :gw]
        t = dmix * mix
        th = t.astype(BF16)
        tl = (t - th.astype(F32)).astype(BF16)
        rs = _dot(th, bd_ref[...]) + _dot(tl, bd_ref[...])
        for wg_, do_ref, dl_ref, d in zip(ws, (do0, do1, do2), (dl0, dl1, dl2), DILATIONS):
            _to_view(s0, wg_ * dmix, do_ref, d)
            _to_view(s1, wg_ * rs, dl_ref, d)
        _mem_attn_bwd(qm_ref[...], kv_ref, dbr[:, gw:], dqm_ref, dkv_ref)

    vspecs = [_view_rows(gw, d) for d in DILATIONS]
    return _pallas_call(
        body, name="bwd_post_attn", grid=(s // tm,),
        out_shape=[_sds((s, nb), BF16)] + [_sds((s // d, d * gw), BF16) for d in DILATIONS]
                  + [_sds((s // d, d * gw), F32) for d in DILATIONS] + [_sds((s, MEM_WIDTH), BF16), _plain(kv.shape, F32)],
        in_specs=[_rows(D_MODEL), _whole(wg_out.shape), _rows(nb)] + vspecs * 2
                 + [_rows(MEM_WIDTH), _whole(kv.shape), _rows(MEM_WIDTH), _whole(head_ones.shape)],
        out_specs=[_rows(nb)] + vspecs * 2 + [_rows(MEM_WIDTH), _whole(kv.shape)],
        scratch_shapes=[_view_scratch(gw), _view_scratch(gw)],
        compiler_params=_params(1, 48),
    )(dhb, wg_out, z, *os_, *ls_, qm, kv, mo, head_ones)


def _attn_bwd(q, k, v, lse, do, dl, tabs, d, token):
    ln, dw = q.shape
    w = dw // d
    nb = ln // BLOCK
    reps = w // LANES
    two, before = _pair_specs(d, nb, w)
    two_t, _ = _pair_specs(d, nb, LANES)

    def attend(q_ref, l_ref, do_ref, dl_ref, dqs, acck, accv, rows, col0, kk, vv, acc_rows):
        valid = _band_mask(kk.shape[0])
        low = _low_head_lanes()
        pairs, qcols = _head_tiles(w, col0)
        cols = [slice(col0 + h * HEAD_DIM, col0 + h * HEAD_DIM + 1) for h in range(HEADS_PER_GROUP)]
        qhs = [h for qc in qcols for h in _split_pair(q_ref[rows, qc], low)]
        dobs = [h for qc in qcols for h in _split_pair(do_ref[rows, qc], low)]
        k2s = [kk[:, pr] for pr in pairs for _ in range(2)]
        v2s = [vv[:, pr] for pr in pairs for _ in range(2)]
        scs = [jnp.where(valid, _dot_nt(qh, k2), NEG) for qh, k2 in zip(qhs, k2s)]
        dps = [_dot_nt(dob, v2) for dob, v2 in zip(dobs, v2s)]
        ps = [jnp.exp(sc - l_ref[rows, col]) for sc, col in zip(scs, cols)]
        dss = [(p * (dp - dl_ref[rows, col])).astype(BF16) for p, dp, col in zip(ps, dps, cols)]
        pbs = [p.astype(BF16) for p in ps]
        for i, qc in enumerate(qcols):
            a, b = 2 * i, 2 * i + 1
            dqs[rows, qc] = jnp.where(low, _dot(dss[a], k2s[a]), _dot(dss[b], k2s[b])) * SCALE
            acck[acc_rows, qc] += _dot_tn(dss[a], qhs[a]) + _dot_tn(dss[b], qhs[b])
            accv[acc_rows, qc] += _dot_tn(pbs[a], dobs[a]) + _dot_tn(pbs[b], dobs[b])

    def body_streams(_, q_ref, kc_ref, vc_ref, l_ref, do_ref, dl_ref, c_ref, sa_ref, sb_ref,
                     dq_ref, dk_ref, dv_ref, acck, accv, dqs):
        acck[...] = jnp.zeros_like(acck)
        accv[...] = jnp.zeros_like(accv)
        for sb in range(2):
            cols = slice(sb * w, (sb + 1) * w)
            attend(q_ref, l_ref, do_ref, dl_ref, dqs, acck, accv, TOP, sb * w, kc_ref[:, cols], vc_ref[:, cols], TOP)
        tabs2 = [jnp.concatenate([jnp.tile(r[:, sb * LANES:(sb + 1) * LANES], (1, reps)) for sb in range(2)], axis=1)
                 for r in (c_ref, sa_ref, sb_ref)]
        dq_ref[...] = _rope_bwd(dqs[...], *tabs2).astype(BF16)
        dk_ref[...] = _rope_bwd(acck[...], *tabs2).astype(BF16)
        dv_ref[...] = accv[...].astype(BF16)

    def body_blocks(_, q_ref, kp_ref, kc_ref, vp_ref, vc_ref, l_ref, do_ref, dl_ref, cq, saq, sbq, ck, sak, sbk,
                    dq_ref, dk_ref, dv_ref, acck, accv, dqs):
        i = pl.program_id(0) % (nb // 2)

        @pl.when(i == 0)
        def _():
            acck[...] = jnp.zeros_like(acck)
            accv[...] = jnp.zeros_like(accv)

        refs = (q_ref, l_ref, do_ref, dl_ref, dqs, acck, accv)
        pl.when(i == 0)(lambda: attend(*refs, TOP, 0, kc_ref[TOP, :], vc_ref[TOP, :], TOP))
        pl.when(i != 0)(lambda: attend(
            *refs, TOP, 0, jnp.concatenate([kp_ref[...], kc_ref[TOP, :]], axis=0),
            jnp.concatenate([vp_ref[...], vc_ref[TOP, :]], axis=0),
            pl.ds(pl.multiple_of((2 * i - 1) * BLOCK, BLOCK), 2 * BLOCK)))
        attend(*refs, BOTTOM, 0, kc_ref[...], vc_ref[...], pl.ds(pl.multiple_of(2 * i * BLOCK, BLOCK), 2 * BLOCK))
        tq = [jnp.tile(r[...], (1, reps)) for r in (cq, saq, sbq)]
        dq_ref[...] = _rope_bwd(dqs[...], *tq).astype(BF16)

        @pl.when(i == nb // 2 - 1)
        def _():
            for r0 in range(0, nb * BLOCK, 2 * BLOCK):
                rows = slice(r0, r0 + 2 * BLOCK)
                tk = [jnp.tile(r[rows, :], (1, reps)) for r in (ck, sak, sbk)]
                dk_ref[rows, :] = _rope_bwd(acck[rows, :], *tk).astype(BF16)
                dv_ref[rows, :] = accv[rows, :].astype(BF16)

    if nb == 1:
        body = body_streams
        in_specs = [ANY] + [two] * 6 + [two_t] * 3
        args = (token, q, k, v, lse, do, dl, *tabs)
        out_specs = [two, two, two]
        acc_shape = (BLOCK, 2 * w)
    else:
        body = body_blocks
        stream = pl.BlockSpec((nb * BLOCK, w), lambda n: (0, n // (nb // 2)))
        stream_t = pl.BlockSpec((nb * BLOCK, LANES), lambda n: (0, n // (nb // 2)))
        in_specs = [ANY, two, before, two, before, two, two, two, two] + [two_t] * 3 + [stream_t] * 3
        args = (token, q, k, k, v, v, lse, do, dl, *tabs, *tabs)
        out_specs = [two, stream, stream]
        acc_shape = (nb * BLOCK, w)
    return _pallas_call(
        body, name=f"attn_bwd_d{d}", grid=(d * nb // 2,), out_shape=[_sds((ln, dw), BF16)] * 3,
        in_specs=in_specs, out_specs=out_specs,
        scratch_shapes=[pltpu.VMEM(acc_shape, F32), pltpu.VMEM(acc_shape, F32), pltpu.VMEM(two.block_shape, F32)],
        compiler_params=_params(1, 48),
    )(*args)


def _position():
    return lax.axis_index("x"), lax.axis_index("y"), lax.axis_index("c")


def _all_gather(shards, name):
    n_a = len(shards)

    def body(*refs):
        x_refs, out_refs = refs[:n_a], refs[n_a:2 * n_a]
        send_sems, recv_sems, local_sems = refs[2 * n_a:]
        x, y, c = _position()
        me, sibling = (x, y, c), (x, y, 1 - c)
        chips = [(1 - x, y), (x, 1 - y), (1 - x, 1 - y)]

        def rows(a, px, py, pc):
            return out_refs[a].at[4 * px + 2 * py + pc]

        def copy(a, k, block, to, own=False):
            return pltpu.make_async_remote_copy(
                src_ref=x_refs[a] if own else rows(a, *block), dst_ref=rows(a, *block),
                send_sem=send_sems.at[a, k], recv_sem=recv_sems.at[a, k], device_id=to, device_id_type=MESH)

        mine = [pltpu.make_async_copy(x_refs[a], rows(a, *me), local_sems.at[a]) for a in range(n_a)]
        for cp in mine:
            cp.start()
        first = []
        for j, chip in enumerate(chips):
            first += [copy(a, 1 + j, me, (*chip, c), own=True) for a in range(n_a)]
        first += [copy(a, 0, me, sibling, own=True) for a in range(n_a)]
        for cp in first:
            cp.start()
        passed = []
        for j, chip in enumerate(chips):
            for a in range(n_a):
                copy(a, 1 + j, (*chip, c), me).wait_recv()
                fwd = copy(a, 4 + j, (*chip, c), sibling)
                fwd.start()
                passed.append(fwd)
        for a in range(n_a):
            copy(a, 0, sibling, me).wait_recv()
        for j, chip in enumerate(chips):
            for a in range(n_a):
                copy(a, 4 + j, (*chip, 1 - c), me).wait_recv()
        for cp in first + passed:
            cp.wait_send()
        for cp in mine:
            cp.wait()

    return _pallas_call(
        body, name=name, out_shape=[_sds((N_DEV,) + t.shape, t.dtype) for t in shards],
        in_specs=[ANY] * n_a, out_specs=[ANY] * n_a,
        scratch_shapes=[pltpu.SemaphoreType.DMA((n_a, 7)), pltpu.SemaphoreType.DMA((n_a, 7)),
                        pltpu.SemaphoreType.DMA((n_a,))],
    )(*shards)


def _all_gather_relay(xs, name):
    def body(x_ref, out_ref, send_sems, recv_sems, local_sem):
        x, y, c = _position()
        me, sibling = (x, y, c), (x, y, 1 - c)
        xn, yn, diag = (1 - x, y, c), (x, 1 - y, c), (1 - x, 1 - y, c)
        src_nb = (x + c * (1 - 2 * x), y + (1 - c) * (1 - 2 * y), c)
        dst_nb = (x + (1 - c) * (1 - 2 * x), y + c * (1 - 2 * y), c)

        def rows(dev):
            return out_ref.at[4 * dev[0] + 2 * dev[1] + dev[2]]

        def copy(k, block, to, own=False):
            return pltpu.make_async_remote_copy(
                src_ref=x_ref if own else rows(block), dst_ref=rows(block),
                send_sem=send_sems.at[k], recv_sem=recv_sems.at[k], device_id=to, device_id_type=MESH)

        mine = pltpu.make_async_copy(x_ref, rows(me), local_sem)
        mine.start()
        first = [copy(1, me, xn, own=True), copy(2, me, yn, own=True), copy(0, me, sibling, own=True)]
        for cp in first:
            cp.start()
        copy(1, xn, me).wait_recv()
        copy(2, yn, me).wait_recv()
        relay = copy(3, src_nb, dst_nb)
        relay.start()
        passed = [copy(4, xn, sibling), copy(5, yn, sibling)]
        for cp in passed:
            cp.start()
        copy(3, diag, me).wait_recv()
        last = copy(6, diag, sibling)
        last.start()
        copy(0, sibling, me).wait_recv()
        for k, blk in ((4, (1 - x, y, 1 - c)), (5, (x, 1 - y, 1 - c)), (6, (1 - x, 1 - y, 1 - c))):
            copy(k, blk, me).wait_recv()
        for cp in first + [relay] + passed + [last]:
            cp.wait_send()
        mine.wait()

    return _pallas_call(
        body, name=name, out_shape=_sds((N_DEV,) + xs.shape, xs.dtype),
        in_specs=[ANY], out_specs=ANY,
        scratch_shapes=[pltpu.SemaphoreType.DMA((7,)), pltpu.SemaphoreType.DMA((7,)), pltpu.SemaphoreType.DMA],
    )(xs)


def _rs_to_sibling(gs):
    n_a = len(gs)

    def body(*refs):
        g_refs, recv_refs = refs[:n_a], refs[n_a:2 * n_a]
        send_sems, recv_sems = refs[2 * n_a:]
        x, y, c = _position()
        copies = []
        for k in range(4):
            for a in range(n_a):
                copies.append(pltpu.make_async_remote_copy(
                    src_ref=g_refs[a].at[2 * k + (1 - c)], dst_ref=recv_refs[a].at[k],
                    send_sem=send_sems.at[a, k], recv_sem=recv_sems.at[a, k],
                    device_id=(x, y, 1 - c), device_id_type=MESH))
        for cp in copies:
            cp.start()
        for cp in copies:
            cp.wait()

    return _pallas_call(
        body, name="rs_to_sibling", out_shape=[_sds((4,) + g.shape[1:], g.dtype) for g in gs],
        in_specs=[ANY] * n_a, out_specs=[ANY] * n_a,
        scratch_shapes=[pltpu.SemaphoreType.DMA((n_a, 4)), pltpu.SemaphoreType.DMA((n_a, 4))],
    )(*gs)


def _rs_to_chips(pbs):
    n_a = len(pbs)

    def body(*refs):
        p_refs, recv_refs = refs[:n_a], refs[n_a:2 * n_a]
        send_sems, recv_sems = refs[2 * n_a:]
        x, y, c = _position()
        chips = [(1 - x, y), (x, 1 - y), (1 - x, 1 - y)]
        copies = []
        for j, (px, py) in enumerate(chips):
            for a in range(n_a):
                copies.append(pltpu.make_async_remote_copy(
                    src_ref=p_refs[a].at[2 * px + py], dst_ref=recv_refs[a].at[j],
                    send_sem=send_sems.at[a, j], recv_sem=recv_sems.at[a, j],
                    device_id=(px, py, c), device_id_type=MESH))
        for cp in copies:
            cp.start()
        for cp in copies:
            cp.wait()

    return _pallas_call(
        body, name="rs_to_chips", out_shape=[_sds((3,) + p.shape[1:], p.dtype) for p in pbs],
        in_specs=[ANY] * n_a, out_specs=[ANY] * n_a,
        scratch_shapes=[pltpu.SemaphoreType.DMA((n_a, 3)), pltpu.SemaphoreType.DMA((n_a, 3))],
    )(*pbs)


HBM_SPEC = pl.BlockSpec(memory_space=pltpu.HBM)
SEM_SPEC = pl.BlockSpec(memory_space=pltpu.SEMAPHORE)
EFFECT = pltpu.SideEffectType.DATAFLOW_SIDE_EFFECTING
def _plan_gather_own(src_refs, land_refs):
    x, y, c = _position()
    me = 4 * x + 2 * y + c
    peers = [(x, y, 1 - c), (1 - x, y, c), (x, 1 - y, c), (1 - x, 1 - y, c)]
    return [(src_refs[a], land_refs[a].at[me], (a, k), peer) for k, peer in enumerate(peers) for a in range(len(src_refs))]


def _plan_gather_pass(src_refs, land_refs):
    x, y, c = _position()
    chips = [(1 - x, y), (x, 1 - y), (1 - x, 1 - y)]
    return [(land_refs[a].at[4 * px + 2 * py + c], land_refs[a].at[4 * px + 2 * py + c], (a, j), (x, y, 1 - c))
            for j, (px, py) in enumerate(chips) for a in range(len(land_refs))]


def _plan_to_sibling(src_refs, land_refs):
    x, y, c = _position()
    return [(src_refs[a].at[2 * k + (1 - c)], land_refs[a].at[k], (a, k), (x, y, 1 - c))
            for k in range(4) for a in range(len(src_refs))]


def _plan_to_chips(src_refs, land_refs):
    x, y, c = _position()
    chips = [(1 - x, y), (x, 1 - y), (1 - x, 1 - y)]
    return [(src_refs[a].at[2 * px + py], land_refs[a].at[j], (a, j), (px, py, c))
            for j, (px, py) in enumerate(chips) for a in range(len(src_refs))]


def _split_start(srcs, lands, plan, n_sem, after, name):
    n_s, n_a = len(srcs), len(lands)
    n_b = n_s + n_a

    def body(*refs):
        src_refs, land_refs = refs[:n_s], refs[n_s:n_b]
        send_sems, recv_sems, token = refs[n_b + 1], refs[n_b + 2], refs[-1]
        for src, dst, (a, k), dev in plan(src_refs, land_refs):
            i = a * n_sem + k
            pltpu.make_async_remote_copy(src_ref=src, dst_ref=dst, send_sem=send_sems.at[i], recv_sem=recv_sems.at[i],
                                         device_id=dev, device_id_type=MESH).start()
        token[...] = jnp.zeros_like(token)

    bufs = list(srcs) + list(lands)
    res = pl.pallas_call(
        body, name=name,
        out_shape=(pltpu.SemaphoreType.DMA((n_a * n_sem,)), pltpu.SemaphoreType.DMA((n_a * n_sem,)),
                   *[pltpu.HBM(t.shape, t.dtype) for t in bufs], _plain((8, LANES), F32)),
        in_specs=[HBM_SPEC] * n_b + [ANY],
        out_specs=(SEM_SPEC, SEM_SPEC, *[HBM_SPEC] * n_b, pl.BlockSpec(memory_space=pltpu.VMEM)),
        input_output_aliases={i: 2 + i for i in range(n_b)},
        compiler_params=pltpu.CompilerParams(has_side_effects=EFFECT),
    )(*[pltpu.with_memory_space_constraint(t, pltpu.HBM) for t in bufs], after)
    return (res[0], res[1], res[2:2 + n_s], res[2 + n_s:2 + n_b]), res[-1]


def _split_wait(started, plan, after, name):
    send_sems, recv_sems, srcs, lands = started
    n_s, n_a = len(srcs), len(lands)
    n_b = n_s + n_a
    n_sem = send_sems.shape[0] // n_a

    def body(*refs):
        src_refs, land_refs = refs[:n_s], refs[n_s:n_b]
        s_sems, r_sems = refs[n_b], refs[n_b + 1]
        for src, dst, (a, k), dev in plan(src_refs, land_refs):
            i = a * n_sem + k
            cp = pltpu.make_async_remote_copy(src_ref=src, dst_ref=dst, send_sem=s_sems.at[i], recv_sem=r_sems.at[i],
                                              device_id=dev, device_id_type=MESH)
            cp.wait_send()
            cp.wait_recv()

    bufs = list(srcs) + list(lands)
    res = pl.pallas_call(
        body, name=name, out_shape=tuple(pltpu.HBM(t.shape, t.dtype) for t in bufs),
        in_specs=[HBM_SPEC] * n_b + [SEM_SPEC, SEM_SPEC, ANY],
        out_specs=tuple([HBM_SPEC] * n_b),
        input_output_aliases={i: i for i in range(n_b)},
        compiler_params=pltpu.CompilerParams(has_side_effects=EFFECT),
    )(*bufs, send_sems, recv_sems, after)
    return res[:n_s], res[n_s:]


def _row_tile(r):
    return ROW_TILE if r % ROW_TILE == 0 else r


def _rs_add_sibling(gp, recv, c_arr, name):
    _, r, l = gp.shape
    tr = r if r <= 4 * ROW_TILE else _row_tile(r)

    def body(c_ref, g_ref, r_ref, pf_ref, pb_ref):
        sm = g_ref[...].astype(F32) + r_ref[...].astype(F32)
        pf_ref[...] = sm
        pb_ref[...] = sm.astype(BF16)

    spec = pl.BlockSpec((None, tr, l), lambda k, i, c: (k, i, 0))
    return _pallas_call(
        body, name=name,
        grid_spec=pltpu.PrefetchScalarGridSpec(
            num_scalar_prefetch=1, grid=(4, r // tr),
            in_specs=[pl.BlockSpec((None, tr, l), lambda k, i, c: (2 * k + c[0], i, 0)), spec],
            out_specs=[spec, spec]),
        out_shape=[_sds((4, r, l), F32), _sds((4, r, l), BF16)], compiler_params=_params(2, 32),
    )(c_arr, gp, recv)


def _adam_update(w, gv, m, v):
    nm = ADAM_B1 * m + (1.0 - ADAM_B1) * gv
    nv = ADAM_B2 * v + (1.0 - ADAM_B2) * (gv * gv)
    m_hat = nm / (1.0 - ADAM_B1 ** ADAM_STEP)
    v_hat = nv / (1.0 - ADAM_B2 ** ADAM_STEP)
    return -ADAM_LR * (m_hat / (jnp.sqrt(v_hat) + ADAM_EPS) + ADAM_WD * w), nm, nv


def _rs_finish_adamw(pf, recv, k_arr, w, m, v, name):
    _, r, l = pf.shape
    tr = _row_tile(r)

    def body(k_ref, p_ref, r_ref, w_ref, m_ref, v_ref, g_ref, d_ref, nm_ref, nv_ref):
        gv = ((p_ref[...] + r_ref[0].astype(F32)) + r_ref[1].astype(F32)) + r_ref[2].astype(F32)
        g_ref[...] = gv
        d_ref[...], nm_ref[...], nv_ref[...] = _adam_update(w_ref[...], gv, m_ref[...], v_ref[...])

    spec = pl.BlockSpec((tr, l), lambda i, k: (i, 0))
    return _pallas_call(
        body, name=name,
        grid_spec=pltpu.PrefetchScalarGridSpec(
            num_scalar_prefetch=1, grid=(r // tr,),
            in_specs=[pl.BlockSpec((None, tr, l), lambda i, k: (k[0], i, 0)),
                      pl.BlockSpec((3, tr, l), lambda i, k: (0, i, 0)), spec, spec, spec],
            out_specs=[spec] * 4),
        out_shape=[_plain((r, l), F32)] * 4, compiler_params=_params(1, 32),
    )(k_arr, pf, recv, w, m, v)


def _rs_finish_adamw_t(pf, recv, k_arr, w_t, m_t, v_t, name):
    _, r, c = pf.shape
    tr = _row_tile(r)
    cp = -(-c // LANES) * LANES

    def body(k_ref, p_ref, r_ref, w_ref, m_ref, v_ref, g_ref, d_ref, nm_ref, nv_ref, pad):
        gv = ((p_ref[...] + r_ref[0].astype(F32)) + r_ref[1].astype(F32)) + r_ref[2].astype(F32)
        pad[...] = jnp.zeros_like(pad)
        pad[:, 0:c] = gv
        gt = pad[...].T[0:c, :]
        g_ref[...] = gt
        d_ref[...], nm_ref[...], nv_ref[...] = _adam_update(w_ref[...], gt, m_ref[...], v_ref[...])

    spec = pl.BlockSpec((c, tr), lambda i, k: (0, i))
    return _pallas_call(
        body, name=name,
        grid_spec=pltpu.PrefetchScalarGridSpec(
            num_scalar_prefetch=1, grid=(r // tr,),
            in_specs=[pl.BlockSpec((None, tr, c), lambda i, k: (k[0], i, 0)),
                      pl.BlockSpec((3, tr, c), lambda i, k: (0, i, 0)), spec, spec, spec],
            out_specs=[spec] * 4, scratch_shapes=[pltpu.VMEM((tr, cp), F32)]),
        out_shape=[_plain((c, r), F32)] * 4, compiler_params=_params(1, 32),
    )(k_arr, pf, recv, w_t, m_t, v_t)


def _sum_devices(g):
    def body(g_ref, o_ref):
        acc = g_ref[0]
        for j in range(1, N_DEV):
            acc = acc + g_ref[j]
        o_ref[...] = acc

    return _pallas_call(body, name="sum_devices", out_shape=_plain(g.shape[1:], F32))(g)


def _adamw(w, g, m, v, name):
    shape = w.shape
    w2, g2, m2, v2 = [t.reshape((-1, shape[-1])) for t in (w, g, m, v)]

    def body(w_ref, g_ref, m_ref, v_ref, d_ref, nm_ref, nv_ref):
        d_ref[...], nm_ref[...], nv_ref[...] = _adam_update(w_ref[...], g_ref[...], m_ref[...], v_ref[...])

    outs = _pallas_call(body, name=name, out_shape=[_plain(w2.shape, F32)] * 3)(w2, g2, m2, v2)
    return tuple(t.reshape(shape) for t in outs)


def _after(t, token):
    return t + token[0:1, 0:1].astype(t.dtype)


def _finish(name, pf, recv, k_arr, w, m, v):
    if name in ("attn_w_in", "conv_w_in"):
        res = _rs_finish_adamw_t(pf, recv, k_arr, w.T, m.T, v.T, "rs_finish_adamw_" + name)
        return tuple(t.T for t in res)
    return _rs_finish_adamw(pf, recv, k_arr, w, m, v, "rs_finish_adamw_" + name)


def kernel(x, mem, positions, norm_g, mem_norm_g, w_mem_kv, attn_w_in, attn_w_out, conv_w_in, conv_w, conv_w_out, final_g, loss_target, m_norm_g, m_mem_norm_g, m_w_mem_kv, m_attn_w_in, m_attn_w_out, m_conv_w_in, m_conv_w, m_conv_w_out, m_final_g, v_norm_g, v_mem_norm_g, v_w_mem_kv, v_attn_w_in, v_attn_w_out, v_conv_w_in, v_conv_w, v_conv_w_out, v_final_g):
    px, py, pc = _position()
    me = 4 * px + 2 * py + pc
    c_arr = jnp.reshape(pc, (1,)).astype(jnp.int32)
    k_arr = jnp.reshape(2 * px + py, (1,)).astype(jnp.int32)
    x, mem, pos, tgt = x[0], mem[0], positions[0], loss_target[0]

    wg_in0 = _all_gather_relay(attn_w_in[0].astype(BF16), "gather_w_in0")
    late = [attn_w_out[0].astype(BF16), conv_w_in[0].astype(BF16), conv_w_out[0].astype(BF16),
            w_mem_kv[0].astype(BF16), w_mem_kv[1].astype(BF16), jnp.pad(conv_w[0], ((0, 5), (0, 0)))]
    lands = [lax.dynamic_update_slice(lax.empty((N_DEV,) + t.shape, t.dtype), t[None], (me, 0, 0)) for t in late]
    late_weights, late_token = _split_start(late, lands, _plan_gather_own, 4, wg_in0, "gather_late_start")

    tabs = _rope_tables(pos)
    g0, g1 = _after(norm_g[0:1], late_token), norm_g[1:2]

    hn0, hn0_t, qs, ks, vs, tabs_v, qm0, z0 = _inproj_attn(x, g0, wg_in0, tabs)
    os_, ls_ = [], []
    for j, d in enumerate(DILATIONS):
        if j == 2:
            _, lands = _split_wait(late_weights, _plan_gather_own, ls_[1], "gather_late_wait")
            late_weights, late_token = _split_start([], lands, _plan_gather_pass, 3, ls_[1], "gather_late_pass_start")
        o, l = _attn_fwd(qs[j], ks[j], vs[j], d, late_token)
        os_.append(o)
        ls_.append(l)

    _, gathered = _split_wait(late_weights, _plan_gather_pass, ls_[2], "gather_late_pass_wait")
    wg_out0, wg_in1, wg_out1, wg_kv0, wg_kv1, cw_all = gathered
    w_out1 = wg_out1.reshape(-1, wg_out1.shape[2])
    w_kv = jnp.stack([wg_kv0.reshape(-1, wg_kv0.shape[2]), wg_kv1.reshape(-1, wg_kv1.shape[2])])
    cw = cw_all[:, 0:3].transpose(1, 0, 2).reshape(3, -1)
    kv = _memkv_fwd(mem, mem_norm_g, w_kv)
    y0, y0_t, mo0, h1 = _post_attn(os_, ls_, qm0, kv[0], z0, x, wg_out0)

    hn1, hn1_t, bg, cg, u, qm1, z1 = _inproj_conv(h1, g1, wg_in1)
    y1, y1_t, mo1, dh2, dh2b, loss_acc, d_final_g = _post_conv_loss(
        bg, cg, u, qm1, kv[1], z1, h1, w_out1, cw, final_g.reshape(1, -1), tgt)

    d_w_out1 = _wgrad_rows(y1_t, dh2b, "wgrad_out1")
    dz1, dbg, dconv, dqm1, dkv1 = _bwd_post_conv(dh2b, w_out1, bg, cg, u, z1, qm1, kv[1], mo1, cw)
    dcg, du, dcw = _bwd_conv(dconv, cg, u, cw)
    dh1, dh1b, dg1, dproj1 = _dgrad_norm([(dbg, 1), (dcg, 1), (du, 1), (dqm1, 1), (dz1, 1)], wg_in1, h1, g1, dh2,
                                         "dgrad_norm_conv")
    d_w_in1 = _wgrad_shards(hn1_t, dproj1, "wgrad_in1")

    d_w_out0 = _wgrad_cols(y0_t, dh1b, wg_out0.shape[2], "wgrad_out0")

    names1 = ["conv_w_in", "conv_w_out", "attn_w_out"]
    grads1 = [d_w_in1, d_w_out1, d_w_out0]
    started, token = _split_start(grads1, [lax.empty((4,) + g.shape[1:], g.dtype) for g in grads1],
                                  _plan_to_sibling, 4, dg1, "rs1_sibling_start")

    gw = GROUP_WIDTH
    ones = (jnp.arange(gw)[:, None] // HEAD_DIM == jnp.arange(gw)[None, :] // HEAD_DIM).astype(BF16)
    ones = _after(ones, token)
    res = _bwd_post_attn(dh1b, wg_out0, z0, os_, ls_, qm0, kv[0], mo0, ones)
    dz0, dos, dls, dqm0, dkv0 = res[0], res[1:4], res[4:7], res[7], res[8]

    grads1, from_sibling = _split_wait(started, _plan_to_sibling, dz0, "rs1_sibling_wait")
    parts1 = [_rs_add_sibling(g, r, c_arr, "rs_add_sibling_" + n) for g, r, n in zip(grads1, from_sibling, names1)]
    pbs1 = [pb for _, pb in parts1]
    started, token = _split_start(pbs1, [lax.empty((3,) + p.shape[1:], p.dtype) for p in pbs1],
                                  _plan_to_chips, 3, dg1, "rs1_chips_start")

    dqs, dks, dvs = [], [], []
    for j, d in enumerate(DILATIONS):
        dq, dk, dv = _attn_bwd(qs[j], ks[j], vs[j], ls_[j], dos[j], dls[j], tabs_v[j], d, token)
        dqs.append((dq, d))
        dks.append((dk, d))
        dvs.append((dv, d))
    d_w_kv, d_mem_g = _memkv_bwd(jnp.stack([dkv0, dkv1]), w_kv, mem, mem_norm_g)
    n_kv = d_w_kv.shape[1] // N_DEV
    dproj0 = _assemble_dproj(dqs + dks + dvs + [(dqm0, 1), (dz0, 1)], wg_in0.shape[2], "assemble_dproj_attn")
    d_w_in0 = _wgrad_shards(hn0_t, dproj0, "wgrad_in0")

    names0 = ["attn_w_in", "w_mem_kv0", "w_mem_kv1"]
    grads0 = [d_w_in0, d_w_kv[0].reshape(N_DEV, n_kv, -1), d_w_kv[1].reshape(N_DEV, n_kv, -1)]
    started0, token0 = _split_start(grads0, [lax.empty((4,) + g.shape[1:], g.dtype) for g in grads0],
                                    _plan_to_sibling, 4, dg1, "rs0_sibling_start")
    _, from_chips1 = _split_wait(started, _plan_to_chips, token0, "rs1_chips_wait")
    shard = dict(attn_w_in=(attn_w_in[0], m_attn_w_in[0], v_attn_w_in[0]),
                 attn_w_out=(attn_w_out[0], m_attn_w_out[0], v_attn_w_out[0]),
                 conv_w_in=(conv_w_in[0], m_conv_w_in[0], v_conv_w_in[0]),
                 conv_w_out=(conv_w_out[0], m_conv_w_out[0], v_conv_w_out[0]),
                 w_mem_kv0=(w_mem_kv[0], m_w_mem_kv[0], v_w_mem_kv[0]), w_mem_kv1=(w_mem_kv[1], m_w_mem_kv[1], v_w_mem_kv[1]))
    big = {}
    for n, (pf, _), r in zip(names1, parts1, from_chips1):
        big[n] = _finish(n, pf, r, k_arr, *shard[n])

    grads0, from_sibling = _split_wait(started0, _plan_to_sibling, big["conv_w_out"][1], "rs0_sibling_wait")
    parts0 = [_rs_add_sibling(g, r, c_arr, "rs_add_sibling_" + n) for g, r, n in zip(grads0, from_sibling, names0)]
    pbs0 = [pb for _, pb in parts0]
    started0, token0 = _split_start(pbs0, [lax.empty((3,) + p.shape[1:], p.dtype) for p in pbs0],
                                    _plan_to_chips, 3, dg1, "rs0_chips_start")
    dx, dg0 = _dgrad_norm_dm(dproj0, wg_in0, x, g0, dh1, token0, "dgrad_norm_attn")

    small_part = jnp.concatenate([dg0, dg1, d_mem_g.reshape(2, -1), d_final_g, dcw[0:3]], axis=0)
    small_part = jnp.concatenate([small_part, jnp.broadcast_to(loss_acc[0, 0], small_part.shape)], axis=0)
    small = _sum_devices(_all_gather([small_part], "gather_small_grads")[0])
    loss = small[8, 0]
    g_conv_w = lax.dynamic_slice(small[5:8], (0, me * LANES), (3, LANES))[None]
    small_g = dict(norm_g=small[0:2], mem_norm_g=small[2:4], conv_w=g_conv_w, final_g=small[4])
    small_w = dict(norm_g=(norm_g, m_norm_g, v_norm_g), mem_norm_g=(mem_norm_g, m_mem_norm_g, v_mem_norm_g),
                   conv_w=(conv_w, m_conv_w, v_conv_w), final_g=(final_g, m_final_g, v_final_g))
    for n, (w, m, v) in small_w.items():
        big[n] = (small_g[n],) + _adamw(w, small_g[n], m, v, "adamw_" + n)

    _, from_chips0 = _split_wait(started0, _plan_to_chips, big["final_g"][1], "rs0_chips_wait")
    for n, (pf, _), r in zip(names0, parts0, from_chips0):
        big[n] = _finish(n, pf, r, k_arr, *shard[n])
    for n in ("attn_w_in", "attn_w_out", "conv_w_in", "conv_w_out"):
        big[n] = tuple(t[None] for t in big[n])
    big["w_mem_kv"] = tuple(jnp.stack([a, b]) for a, b in zip(big["w_mem_kv0"], big["w_mem_kv1"]))

    order = ["norm_g", "mem_norm_g", "w_mem_kv", "attn_w_in", "attn_w_out", "conv_w_in", "conv_w", "conv_w_out", "final_g"]
    return (loss, dx[None], *[big[n][0] for n in order], *[big[n][1] for n in order],
            *[big[n][2] for n in order], *[big[n][3] for n in order])
```

```python
import functools

import jax
import jax.numpy as jnp
from jax import lax
from jax.experimental import pallas as pl
from jax.experimental.pallas import tpu as pltpu

F32 = jnp.float32
BF16 = jnp.bfloat16

N_DEV = 8
D_MODEL = 1024
HEAD_DIM = 64
ROT_DIM = HEAD_DIM // 4
ROPE_THETA = 500000.0
DILATIONS = (1, 4, 16)
HEADS_PER_GROUP = 8
GROUP_WIDTH = HEADS_PER_GROUP * HEAD_DIM
BLOCK = 128
N_MEM = 256
MEM_HEADS = 4
MEM_WIDTH = MEM_HEADS * HEAD_DIM
CONV_WIDTH = D_MODEL
EPS = 1e-6
SCALE = HEAD_DIM ** -0.5
NEG = -1e30

ADAM_LR = 0.001
ADAM_B1 = 0.9
ADAM_B2 = 0.999
ADAM_EPS = 1e-08
ADAM_WD = 0.01
ADAM_STEP = 10

ROW_TILE = 256
LANES = 128
MESH = pl.DeviceIdType.MESH
ANY = pl.BlockSpec(memory_space=pl.ANY)


def _pallas_call(body, **kw):
    call = pl.pallas_call(body, **kw)

    def run(*args):
        pinned = [pltpu.with_memory_space_constraint(a, pltpu.HBM) if jnp.issubdtype(a.dtype, jnp.floating) else a
                  for a in args]
        return call(*pinned)

    return run


def _dot(a, b):
    return lax.dot_general(a, b, (((1,), (0,)), ((), ())), preferred_element_type=F32)


def _dot_nt(a, b):
    return lax.dot_general(a, b, (((1,), (1,)), ((), ())), preferred_element_type=F32)


def _dot_tn(a, b):
    return lax.dot_general(a, b, (((0,), (0,)), ((), ())), preferred_element_type=F32)


def _params(n_grid, vmem_mb=48):
    return pltpu.CompilerParams(dimension_semantics=("arbitrary",) * n_grid, vmem_limit_bytes=vmem_mb << 20)


def _rows(width, tm=ROW_TILE):
    return pl.BlockSpec((tm, width), lambda i: (i, 0))


def _view_rows(width, d, tm=ROW_TILE):
    return pl.BlockSpec((tm // d, d * width), lambda i: (i, 0))


def _whole(shape):
    return pl.BlockSpec(shape, lambda *_: (0,) * len(shape))


def _resident(shape):
    return pl.BlockSpec(shape, lambda *_: (0,) * len(shape), pipeline_mode=pl.Buffered(1))


def _sds(shape, dtype):
    return pltpu.HBM(shape, dtype)


def _plain(shape, dtype):
    return jax.ShapeDtypeStruct(shape, dtype)


def _silu_parts(z):
    sg = jax.nn.sigmoid(z)
    return z * sg, sg * (1.0 + z * (1.0 - sg))


def _to_view(scr, val, out_ref, d):
    tm, w = val.shape
    if d == 1:
        out_ref[...] = val.astype(out_ref.dtype)
        return
    for cb in range(w // LANES):
        scr[cb] = val[:, cb * LANES:(cb + 1) * LANES]
    for r in range(d):
        for cb in range(w // LANES):
            lo = r * w + cb * LANES
            out_ref[:, lo:lo + LANES] = scr[cb, pl.ds(r, tm // d, stride=d), :].astype(out_ref.dtype)


def _from_view(scr, in_ref, d):
    if d == 1:
        return in_ref[...].astype(F32)
    nc, tm, _ = scr.shape
    w = nc * LANES
    for r in range(d):
        for cb in range(nc):
            lo = r * w + cb * LANES
            scr[cb, pl.ds(r, tm // d, stride=d), :] = in_ref[:, lo:lo + LANES].astype(F32)
    return jnp.concatenate([scr[cb] for cb in range(nc)], axis=1)


def _view_scratch(width, tm=ROW_TILE):
    return pltpu.VMEM((width // LANES, tm, LANES), F32)


def _rope_tables(pos):
    half = ROT_DIM // 2
    inv_freq = ROPE_THETA ** (-jnp.arange(half, dtype=F32) * (2.0 / ROT_DIM))
    ang = pos.astype(F32)[:, None] * inv_freq
    cos, sin = jnp.cos(ang), jnp.sin(ang)
    s = pos.shape[0]
    z8 = jnp.zeros((s, half), F32)
    rest = HEAD_DIM - ROT_DIM
    cosf = jnp.concatenate([cos, cos, jnp.ones((s, rest), F32)], axis=1)
    sa = jnp.concatenate([-sin, z8, jnp.zeros((s, rest), F32)], axis=1)
    sb = jnp.concatenate([z8, sin, jnp.zeros((s, rest), F32)], axis=1)
    return tuple(jnp.tile(t, (1, LANES // HEAD_DIM)) for t in (cosf, sa, sb))


def _rope_fwd(t, cv, sav, sbv):
    w = t.shape[1]
    return t * cv + pltpu.roll(t, w - ROT_DIM // 2, 1) * sav + pltpu.roll(t, ROT_DIM // 2, 1) * sbv


def _rope_bwd(g, cv, sav, sbv):
    w = g.shape[1]
    return g * cv + pltpu.roll(g * sav, ROT_DIM // 2, 1) + pltpu.roll(g * sbv, w - ROT_DIM // 2, 1)


def _project(hn, wg_ref, proj_scr):
    c = wg_ref.shape[2]
    for j in range(N_DEV):
        proj_scr[:, j * c:(j + 1) * c] = _dot(hn, wg_ref[j])


def _inproj_attn(x, g, wg, tabs):
    s, d_model = x.shape
    gw = GROUP_WIDTH
    n = N_DEV * wg.shape[2]
    nz = n - 9 * gw - MEM_WIDTH
    reps = gw // LANES
    tm = ROW_TILE

    def body(x_ref, g_ref, w_ref, c_ref, sa_ref, sb_ref, hn_ref, hnt_ref, *rest):
        outs, (proj, scr, tscr) = rest[:-3], rest[-3:]
        q_refs, k_refs, v_refs, t_refs, qm_ref, z_ref = outs[0:3], outs[3:6], outs[6:9], outs[9:18], outs[18], outs[19]
        xb = x_ref[...]
        r = lax.rsqrt(jnp.mean(xb * xb, axis=-1, keepdims=True) + EPS)
        hn = ((xb * r) * g_ref[...]).astype(BF16)
        hn_ref[...] = hn
        hnt_ref[...] = hn.T
        _project(hn, w_ref, proj)
        tab = (c_ref[...], sa_ref[...], sb_ref[...])
        cv, sav, sbv = [jnp.tile(t, (1, reps)) for t in tab]
        for j, d in enumerate(DILATIONS):
            tq = _rope_fwd(proj[:, j * gw:(j + 1) * gw], cv, sav, sbv)
            _to_view(scr, tq * SCALE, q_refs[j], d)
            tk = _rope_fwd(proj[:, (3 + j) * gw:(4 + j) * gw], cv, sav, sbv)
            _to_view(scr, tk, k_refs[j], d)
            _to_view(scr, proj[:, (6 + j) * gw:(7 + j) * gw], v_refs[j], d)
            for i in range(3):
                _to_view(tscr, tab[i], t_refs[3 * j + i], d)
        qm_ref[...] = proj[:, 9 * gw:9 * gw + MEM_WIDTH].astype(BF16)
        z_ref[...] = proj[:, 9 * gw + MEM_WIDTH:]

    views = [_sds((s // d, d * gw), BF16) for d in DILATIONS]
    tviews = [_sds((s // d, d * LANES), F32) for d in DILATIONS for _ in range(3)]
    out_shape = ([_sds((s, d_model), BF16), _sds((d_model, s), BF16)] + views * 3 + tviews
                 + [_sds((s, MEM_WIDTH), BF16), _sds((s, nz), F32)])
    vspecs = [_view_rows(gw, d, tm) for d in DILATIONS]
    tspecs = [_view_rows(LANES, d, tm) for d in DILATIONS for _ in range(3)]
    out_specs = ([_rows(d_model, tm), pl.BlockSpec((d_model, tm), lambda i: (0, i))] + vspecs * 3 + tspecs
                 + [_rows(MEM_WIDTH, tm), _rows(nz, tm)])
    res = _pallas_call(
        body, name="inproj_attn", grid=(s // tm,), out_shape=out_shape,
        in_specs=[_rows(d_model, tm), _whole((1, d_model)), _resident(wg.shape)] + [_rows(LANES, tm)] * 3,
        out_specs=out_specs,
        scratch_shapes=[pltpu.VMEM((tm, n), F32), _view_scratch(gw, tm), _view_scratch(LANES, tm)],
        compiler_params=_params(1, 60),
    )(x, g, wg, *tabs)
    tabs_v = [res[11 + 3 * j:14 + 3 * j] for j in range(3)]
    return res[0], res[1], res[2:5], res[5:8], res[8:11], tabs_v, res[20], res[21]


def _band_mask(n_keys):
    qi = lax.broadcasted_iota(jnp.int32, (BLOCK, n_keys), 0)
    kj = lax.broadcasted_iota(jnp.int32, (BLOCK, n_keys), 1)
    if n_keys == BLOCK:
        return kj <= qi
    return jnp.logical_or(jnp.logical_and(kj < BLOCK, kj >= qi), jnp.logical_and(kj >= BLOCK, (kj - BLOCK) <= qi))


def _low_head_lanes():
    return lax.broadcasted_iota(jnp.int32, (1, LANES), 1) < HEAD_DIM


def _split_pair(t, low):
    zero = jnp.zeros_like(t)
    return jnp.where(low, t, zero), jnp.where(low, zero, t)


def _pair_specs(d, nb, w):
    if nb == 1:
        return pl.BlockSpec((BLOCK, 2 * w), lambda n: (0, n)), None
    half = nb // 2
    two = pl.BlockSpec((2 * BLOCK, w), lambda n: (n % half, n // half))
    before = pl.BlockSpec((BLOCK, w), lambda n: (jnp.maximum(2 * (n % half) - 1, 0), n // half))
    return two, before


def _head_tiles(w, col0):
    return ([slice(p * LANES, (p + 1) * LANES) for p in range(w // LANES)],
            [slice(col0 + p * LANES, col0 + (p + 1) * LANES) for p in range(w // LANES)])


def _attend_fwd(q_ref, o_ref, lse_ref, rows, col0, kk, vv):
    w = kk.shape[1]
    valid = _band_mask(kk.shape[0])
    low = _low_head_lanes()
    pairs, qcols = _head_tiles(w, col0)
    qs_ = [h for qc in qcols for h in _split_pair(q_ref[rows, qc], low)]
    k2s = [kk[:, pr] for pr in pairs for _ in range(2)]
    scs = [jnp.where(valid, _dot_nt(qh, k2), NEG) for qh, k2 in zip(qs_, k2s)]
    ms = [jnp.max(sc, axis=-1, keepdims=True) for sc in scs]
    ps = [jnp.exp(sc - m) for sc, m in zip(scs, ms)]
    ls = [jnp.sum(p, axis=-1, keepdims=True) for p in ps]
    pns = [(p * (1.0 / l)).astype(BF16) for p, l in zip(ps, ls)]
    for i, (pr, qc) in enumerate(zip(pairs, qcols)):
        v2 = vv[:, pr]
        a, b = 2 * i, 2 * i + 1
        o_ref[rows, qc] = jnp.where(low, _dot(pns[a], v2), _dot(pns[b], v2))
        lse_ref[rows, qc] = jnp.where(low, ms[a] + jnp.log(ls[a]), ms[b] + jnp.log(ls[b]))


TOP, BOTTOM = slice(0, BLOCK), slice(BLOCK, 2 * BLOCK)


def _attn_fwd(q, k, v, d, token):
    ln, dw = q.shape
    w = dw // d
    nb = ln // BLOCK
    two, before = _pair_specs(d, nb, w)

    def body_streams(_, q_ref, kc_ref, vc_ref, o_ref, lse_ref):
        for sb in range(2):
            cols = slice(sb * w, (sb + 1) * w)
            _attend_fwd(q_ref, o_ref, lse_ref, TOP, sb * w, kc_ref[:, cols], vc_ref[:, cols])

    def body_blocks(_, q_ref, kp_ref, kc_ref, vp_ref, vc_ref, o_ref, lse_ref):
        first = pl.program_id(0) % (nb // 2) == 0
        pl.when(first)(lambda: _attend_fwd(q_ref, o_ref, lse_ref, TOP, 0, kc_ref[TOP, :], vc_ref[TOP, :]))
        pl.when(jnp.logical_not(first))(lambda: _attend_fwd(
            q_ref, o_ref, lse_ref, TOP, 0, jnp.concatenate([kp_ref[...], kc_ref[TOP, :]], axis=0),
            jnp.concatenate([vp_ref[...], vc_ref[TOP, :]], axis=0)))
        _attend_fwd(q_ref, o_ref, lse_ref, BOTTOM, 0, kc_ref[...], vc_ref[...])

    if nb == 1:
        body, in_specs, args = body_streams, [ANY, two, two, two], (token, q, k, v)
    else:
        body, in_specs, args = body_blocks, [ANY, two, before, two, before, two], (token, q, k, k, v, v)
    return _pallas_call(
        body, name=f"attn_fwd_d{d}", grid=(d * nb // 2,), out_shape=[_sds((ln, dw), F32)] * 2,
        in_specs=in_specs, out_specs=[two, two], compiler_params=_params(1, 32),
    )(*args)


def _memkv_fwd(mem, g, w):
    n_layers = w.shape[0]

    def body(mem_ref, g_ref, w_ref, kv_ref):
        mb = mem_ref[...]
        r = lax.rsqrt(jnp.mean(mb * mb, axis=-1, keepdims=True) + EPS)
        mn = ((mb * r) * g_ref[...]).astype(BF16)
        kv_ref[...] = _dot(mn, w_ref[...]).astype(BF16)

    return _pallas_call(
        body, name="memkv_fwd", grid=(n_layers,),
        out_shape=_plain((n_layers, N_MEM, 2 * MEM_WIDTH), BF16),
        in_specs=[_whole(mem.shape), pl.BlockSpec((None, 1, D_MODEL), lambda l: (l, 0, 0)),
                  pl.BlockSpec((None, D_MODEL, 2 * MEM_WIDTH), lambda l: (l, 0, 0))],
        out_specs=pl.BlockSpec((None, N_MEM, 2 * MEM_WIDTH), lambda l: (l, 0, 0)),
        compiler_params=_params(1, 32),
    )(mem, g.reshape(n_layers, 1, D_MODEL), w)


def _mix_groups(os_, ls_):
    mx = jnp.maximum(jnp.maximum(ls_[0], ls_[1]), ls_[2])
    es = [jnp.exp(t - mx) for t in ls_]
    inv = 1.0 / (es[0] + es[1] + es[2])
    ws = [e * inv for e in es]
    mix = ws[0] * os_[0] + ws[1] * os_[1] + ws[2] * os_[2]
    return ws, mix


MEM_PAIRS = [slice(p * LANES, (p + 1) * LANES) for p in range(MEM_WIDTH // LANES)]


def _mem_probs(qhs, k2s):
    scs = [_dot_nt(qh, k2) * SCALE for qh, k2 in zip(qhs, k2s)]
    es = [jnp.exp(sc - jnp.max(sc, axis=-1, keepdims=True)) for sc in scs]
    return [e * (1.0 / jnp.sum(e, axis=-1, keepdims=True)) for e in es]


def _mem_attn_into(qm, kv_ref, mo_ref):
    low = _low_head_lanes()
    qhs = [h for pr in MEM_PAIRS for h in _split_pair(qm[:, pr], low)]
    k2s = [kv_ref[:, pr] for pr in MEM_PAIRS for _ in range(2)]
    ps = [p.astype(BF16) for p in _mem_probs(qhs, k2s)]
    for i, pr in enumerate(MEM_PAIRS):
        v2 = kv_ref[:, MEM_WIDTH + i * LANES:MEM_WIDTH + (i + 1) * LANES]
        mo_ref[:, pr] = jnp.where(low, _dot(ps[2 * i], v2), _dot(ps[2 * i + 1], v2))


def _mem_attn_bwd(qm, kv_ref, dmem, dqm_ref, dkv_ref):
    low = _low_head_lanes()
    dmb = dmem.astype(BF16)
    vps = [slice(MEM_WIDTH + i * LANES, MEM_WIDTH + (i + 1) * LANES) for i in range(len(MEM_PAIRS))]
    qhs = [h for pr in MEM_PAIRS for h in _split_pair(qm[:, pr], low)]
    dhs = [h for pr in MEM_PAIRS for h in _split_pair(dmb[:, pr], low)]
    k2s = [kv_ref[:, pr] for pr in MEM_PAIRS for _ in range(2)]
    v2s = [kv_ref[:, vp] for vp in vps for _ in range(2)]
    ps = _mem_probs(qhs, k2s)
    dps = [_dot_nt(dh, v2) for dh, v2 in zip(dhs, v2s)]
    dss = [(p * (dp - jnp.sum(dp * p, axis=-1, keepdims=True)) * SCALE).astype(BF16) for p, dp in zip(ps, dps)]
    pbs = [p.astype(BF16) for p in ps]
    for i, (pr, vp) in enumerate(zip(MEM_PAIRS, vps)):
        a, b = 2 * i, 2 * i + 1
        dqm_ref[:, pr] = jnp.where(low, _dot(dss[a], k2s[a]), _dot(dss[b], k2s[b])).astype(BF16)
        dkv_ref[:, pr] += _dot_tn(dss[a], qhs[a]) + _dot_tn(dss[b], qhs[b])
        dkv_ref[:, vp] += _dot_tn(pbs[a], dhs[a]) + _dot_tn(pbs[b], dhs[b])


def _post_attn(os_, ls_, qm, kv, z, x, wg_out):
    s, d_model = x.shape
    gw = GROUP_WIDTH
    nb = gw + MEM_WIDTH
    c = wg_out.shape[2]
    tm = ROW_TILE

    def body(o0, o1, o2, l0, l1, l2, qm_ref, kv_ref, z_ref, x_ref, w_ref, y_ref, yt_ref, mo_ref, h_ref, s0, s1):
        ov, lv = [], []
        for o_ref, l_ref, d in zip((o0, o1, o2), (l0, l1, l2), DILATIONS):
            ov.append(_from_view(s0, o_ref, d))
            lv.append(_from_view(s1, l_ref, d))
        _, mix = _mix_groups(ov, lv)
        _mem_attn_into(qm_ref[...], kv_ref, mo_ref)
        sz, _ = _silu_parts(z_ref[...])
        y_ref[:, :gw] = (mix * sz[:, :gw]).astype(BF16)
        y_ref[:, gw:] = (mo_ref[...] * sz[:, gw:]).astype(BF16)
        y = y_ref[...]
        yt_ref[...] = y.T
        for j in range(N_DEV):
            h_ref[:, j * c:(j + 1) * c] = x_ref[:, j * c:(j + 1) * c] + _dot(y, w_ref[j])

    vspecs = [_view_rows(gw, d) for d in DILATIONS]
    return _pallas_call(
        body, name="post_attn", grid=(s // tm,),
        out_shape=[_sds((s, nb), BF16), _sds((nb, s), BF16), _sds((s, MEM_WIDTH), F32), _sds((s, d_model), F32)],
        in_specs=vspecs * 2 + [_rows(MEM_WIDTH), _whole(kv.shape), _rows(nb), _rows(d_model), _whole(wg_out.shape)],
        out_specs=[_rows(nb), pl.BlockSpec((nb, tm), lambda i: (0, i)), _rows(MEM_WIDTH), _rows(d_model)],
        scratch_shapes=[_view_scratch(gw), _view_scratch(gw)],
        compiler_params=_params(1, 40),
    )(*os_, *ls_, qm, kv, z, x, wg_out)


def _inproj_conv(x, g, wg):
    s, d_model = x.shape
    c = CONV_WIDTH
    n = N_DEV * wg.shape[2]
    nz = n - 3 * c - MEM_WIDTH
    tm = ROW_TILE

    def body(x_ref, g_ref, w_ref, hn_ref, hnt_ref, bg_ref, cg_ref, u_ref, qm_ref, z_ref, proj):
        xb = x_ref[...]
        r = lax.rsqrt(jnp.mean(xb * xb, axis=-1, keepdims=True) + EPS)
        hn = ((xb * r) * g_ref[...]).astype(BF16)
        hn_ref[...] = hn
        hnt_ref[...] = hn.T
        _project(hn, w_ref, proj)
        bg_ref[...] = proj[:, 0:c]
        cg_ref[...] = proj[:, c:2 * c]
        u_ref[...] = proj[:, 2 * c:3 * c]
        qm_ref[...] = proj[:, 3 * c:3 * c + MEM_WIDTH].astype(BF16)
        z_ref[...] = proj[:, 3 * c + MEM_WIDTH:]

    return _pallas_call(
        body, name="inproj_conv", grid=(s // tm,),
        out_shape=[_sds((s, d_model), BF16), _sds((d_model, s), BF16)] + [_sds((s, c), F32)] * 3
                  + [_sds((s, MEM_WIDTH), BF16), _sds((s, nz), F32)],
        in_specs=[_rows(d_model), _whole((1, d_model)), _whole(wg.shape)],
        out_specs=[_rows(d_model), pl.BlockSpec((d_model, tm), lambda i: (0, i))] + [_rows(c)] * 3
                  + [_rows(MEM_WIDTH), _rows(nz)],
        scratch_shapes=[pltpu.VMEM((tm, n), F32)],
        compiler_params=_params(1, 60),
    )(x, g, wg)


HALO = 8


def _halo_before(width, tm=ROW_TILE):
    return pl.BlockSpec((HALO, width), lambda i: (jnp.maximum(i * (tm // HALO) - 1, 0), 0))


def _halo_after(width, n_rows, tm=ROW_TILE):
    return pl.BlockSpec((HALO, width), lambda i: (jnp.minimum((i + 1) * (tm // HALO), n_rows // HALO - 1), 0))


def _conv_taps(cg_ref, u_ref, cgh_ref, uh_ref, i):
    a = cg_ref[...] * u_ref[...]
    ah = jnp.where(i > 0, cgh_ref[...] * uh_ref[...], 0.0)
    row = lax.broadcasted_iota(jnp.int32, a.shape, 0)
    a1 = jnp.where(row == 0, ah[HALO - 1:HALO], pltpu.roll(a, 1, 0))
    a2 = jnp.where(row == 0, ah[HALO - 2:HALO - 1], jnp.where(row == 1, ah[HALO - 1:HALO], pltpu.roll(a, 2, 0)))
    return a, a1, a2


def _post_conv_loss(bg, cg, u, qm, kv, z, h1, w_out, cw, gf, tgt):
    s, d = h1.shape
    c = CONV_WIDTH
    nb = c + MEM_WIDTH
    tm = ROW_TILE

    def body(bg_ref, cg_ref, u_ref, cgh_ref, uh_ref, qm_ref, kv_ref, z_ref, h_ref, w_ref, cw_ref, gf_ref, t_ref,
             y_ref, yt_ref, mo_ref, dh_ref, dhb_ref, loss_ref, dgf_ref):
        i = pl.program_id(0)
        a, a1, a2 = _conv_taps(cg_ref, u_ref, cgh_ref, uh_ref, i)
        conv = cw_ref[0:1, :] * a2 + cw_ref[1:2, :] * a1 + cw_ref[2:3, :] * a
        mix = bg_ref[...] * conv
        _mem_attn_into(qm_ref[...], kv_ref, mo_ref)
        sz, _ = _silu_parts(z_ref[...])
        y_ref[:, :c] = (mix * sz[:, :c]).astype(BF16)
        y_ref[:, c:] = (mo_ref[...] * sz[:, c:]).astype(BF16)
        y = y_ref[...]
        yt_ref[...] = y.T
        h2 = h_ref[...] + _dot(y, w_ref[...])
        r = lax.rsqrt(jnp.mean(h2 * h2, axis=-1, keepdims=True) + EPS)
        nh = h2 * r
        gfv = gf_ref[...]
        diff = nh * gfv - t_ref[...]
        dout = diff * (1.0 / d)
        dn = dout * gfv
        dh2 = r * dn - h2 * ((r * r * r) * jnp.mean(dn * h2, axis=-1, keepdims=True))
        dh_ref[...] = dh2
        dhb_ref[...] = dh2.astype(BF16)

        @pl.when(i == 0)
        def _():
            loss_ref[...] = jnp.zeros_like(loss_ref)
            dgf_ref[...] = jnp.zeros_like(dgf_ref)

        loss_ref[...] += 0.5 * jnp.sum(jnp.mean(diff * diff, axis=-1, keepdims=True))
        dgf_ref[...] += jnp.sum(dout * nh, axis=0, keepdims=True)

    return _pallas_call(
        body, name="post_conv_loss", grid=(s // tm,),
        out_shape=[_sds((s, nb), BF16), _sds((nb, s), BF16), _sds((s, MEM_WIDTH), F32), _sds((s, d), F32),
                   _sds((s, d), BF16), _plain((8, LANES), F32), _plain((1, d), F32)],
        in_specs=[_rows(c)] * 3 + [_halo_before(c)] * 2 + [_rows(MEM_WIDTH), _whole(kv.shape), _rows(nb), _rows(d),
                  _whole(w_out.shape), _whole(cw.shape), _whole((1, d)), _rows(d)],
        out_specs=[_rows(nb), pl.BlockSpec((nb, tm), lambda i: (0, i)), _rows(MEM_WIDTH), _rows(d), _rows(d),
                   _whole((8, LANES)), _whole((1, d))],
        compiler_params=_params(1, 48),
    )(bg, cg, u, cg, u, qm, kv, z, h1, w_out, cw, gf, tgt)


def _bwd_post_conv(dhb, w_out, bg, cg, u, z, qm, kv, mo, cw):
    s = dhb.shape[0]
    c = CONV_WIDTH
    nb = c + MEM_WIDTH

    def body(dh_ref, w_ref, bg_ref, cg_ref, u_ref, cgh_ref, uh_ref, z_ref, qm_ref, kv_ref, mo_ref, cw_ref,
             dz_ref, dbg_ref, dc_ref, dqm_ref, dkv_ref):
        i = pl.program_id(0)

        @pl.when(i == 0)
        def _():
            dkv_ref[...] = jnp.zeros_like(dkv_ref)

        dy = _dot_nt(dh_ref[...], w_ref[...])
        sz, dsz = _silu_parts(z_ref[...])
        a, a1, a2 = _conv_taps(cg_ref, u_ref, cgh_ref, uh_ref, i)
        conv = cw_ref[0:1, :] * a2 + cw_ref[1:2, :] * a1 + cw_ref[2:3, :] * a
        bgv = bg_ref[...]
        dz_ref[:, :c] = (dy[:, :c] * (bgv * conv) * dsz[:, :c]).astype(BF16)
        dz_ref[:, c:] = (dy[:, c:] * mo_ref[...] * dsz[:, c:]).astype(BF16)
        dbr = dy * sz
        dmix = dbr[:, :c]
        dbg_ref[...] = (dmix * conv).astype(BF16)
        dc_ref[...] = dmix * bgv
        _mem_attn_bwd(qm_ref[...], kv_ref, dbr[:, c:], dqm_ref, dkv_ref)

    return _pallas_call(
        body, name="bwd_post_conv", grid=(s // ROW_TILE,),
        out_shape=[_sds((s, nb), BF16), _sds((s, c), BF16), _sds((s, c), F32), _sds((s, MEM_WIDTH), BF16),
                   _plain(kv.shape, F32)],
        in_specs=[_rows(D_MODEL), _whole(w_out.shape)] + [_rows(c)] * 3 + [_halo_before(c)] * 2
                 + [_rows(nb), _rows(MEM_WIDTH), _whole(kv.shape), _rows(MEM_WIDTH), _whole(cw.shape)],
        out_specs=[_rows(nb), _rows(c), _rows(c), _rows(MEM_WIDTH), _whole(kv.shape)],
        compiler_params=_params(1, 48),
    )(dhb, w_out, bg, cg, u, cg, u, z, qm, kv, mo, cw)


def _bwd_conv(dconv, cg, u, cw):
    s, c = dconv.shape
    tm = ROW_TILE
    last = s // tm - 1

    def body(dc_ref, dcn_ref, cg_ref, u_ref, cgh_ref, uh_ref, cw_ref, dcg_ref, du_ref, dcw_ref):
        i = pl.program_id(0)

        @pl.when(i == 0)
        def _():
            dcw_ref[...] = jnp.zeros_like(dcw_ref)

        dc = dc_ref[...]
        dcn = jnp.where(i < last, dcn_ref[...], 0.0)
        row = lax.broadcasted_iota(jnp.int32, dc.shape, 0)
        d1 = jnp.where(row == tm - 1, dcn[0:1], pltpu.roll(dc, tm - 1, 0))
        d2 = jnp.where(row == tm - 1, dcn[1:2], jnp.where(row == tm - 2, dcn[0:1], pltpu.roll(dc, tm - 2, 0)))
        da = cw_ref[2:3, :] * dc + cw_ref[1:2, :] * d1 + cw_ref[0:1, :] * d2
        a, a1, a2 = _conv_taps(cg_ref, u_ref, cgh_ref, uh_ref, i)
        dcg_ref[...] = (da * u_ref[...]).astype(BF16)
        du_ref[...] = (da * cg_ref[...]).astype(BF16)
        dcw_ref[0:1, :] += jnp.sum(dc * a2, axis=0, keepdims=True)
        dcw_ref[1:2, :] += jnp.sum(dc * a1, axis=0, keepdims=True)
        dcw_ref[2:3, :] += jnp.sum(dc * a, axis=0, keepdims=True)

    return _pallas_call(
        body, name="bwd_conv", grid=(s // tm,),
        out_shape=[_sds((s, c), BF16), _sds((s, c), BF16), _plain((8, c), F32)],
        in_specs=[_rows(c), _halo_after(c, s), _rows(c), _rows(c), _halo_before(c), _halo_before(c), _whole(cw.shape)],
        out_specs=[_rows(c), _rows(c), _whole((8, c))], compiler_params=_params(1, 40),
    )(dconv, dconv, cg, u, cg, u, cw)


def _dgrad_norm(pieces, wg, h, g, dres, name):
    s, d_model = h.shape
    c = wg.shape[2]
    n = N_DEV * c
    tm = ROW_TILE
    widths = [p.shape[1] // d for p, d in pieces]
    assert sum(widths) == n
    n_p = len(pieces)

    def body(*refs):
        p_refs = refs[:n_p]
        w_ref, h_ref, g_ref, dr_ref, dh_ref, dhb_ref, dg_ref, dpd_ref, dp, scr = refs[n_p:]

        @pl.when(pl.program_id(0) == 0)
        def _():
            dg_ref[...] = jnp.zeros_like(dg_ref)

        off = 0
        for p_ref, (_, d), wd in zip(p_refs, pieces, widths):
            if d == 1:
                dp[:, off:off + wd] = p_ref[...]
            else:
                dp[:, off:off + wd] = _from_view(scr, p_ref, d).astype(BF16)
            off += wd
        dhn = jnp.zeros((tm, d_model), F32)
        for j in range(N_DEV):
            dpj = dp[:, j * c:(j + 1) * c]
            dpd_ref[j] = dpj
            dhn += _dot_nt(dpj, w_ref[j])
        hb = h_ref[...]
        r = lax.rsqrt(jnp.mean(hb * hb, axis=-1, keepdims=True) + EPS)
        dg_ref[...] += jnp.sum(dhn * (hb * r), axis=0, keepdims=True)
        dn = dhn * g_ref[...]
        dh = dr_ref[...] + r * dn - hb * ((r * r * r) * jnp.mean(dn * hb, axis=-1, keepdims=True))
        dh_ref[...] = dh
        dhb_ref[...] = dh.astype(BF16)

    p_specs = [_view_rows(wd, d) for (_, d), wd in zip(pieces, widths)]
    return _pallas_call(
        body, name=name, grid=(s // tm,),
        out_shape=[_plain((s, d_model), F32), _sds((s, d_model), BF16), _plain((1, d_model), F32), _sds((N_DEV, s, c), BF16)],
        in_specs=p_specs + [_whole(wg.shape), _rows(d_model), _whole((1, d_model)), _rows(d_model)],
        out_specs=[_rows(d_model), _rows(d_model), _whole((1, d_model)), pl.BlockSpec((N_DEV, tm, c), lambda i: (0, i, 0))],
        scratch_shapes=[pltpu.VMEM((tm, n), BF16), _view_scratch(GROUP_WIDTH)],
        compiler_params=_params(1, 60),
    )(*[p for p, _ in pieces], wg, h, g, dres)


def _assemble_dproj(pieces, c, name):
    n = N_DEV * c
    tm = ROW_TILE
    widths = [p.shape[1] // d for p, d in pieces]
    assert sum(widths) == n
    s = pieces[0][0].shape[0] * pieces[0][1]
    n_p = len(pieces)

    def body(*refs):
        p_refs, (dpd_ref, dp, scr) = refs[:n_p], refs[n_p:]
        off = 0
        for p_ref, (_, d), wd in zip(p_refs, pieces, widths):
            if d == 1:
                dp[:, off:off + wd] = p_ref[...]
            else:
                dp[:, off:off + wd] = _from_view(scr, p_ref, d).astype(BF16)
            off += wd
        for j in range(N_DEV):
            dpd_ref[j] = dp[:, j * c:(j + 1) * c]

    return _pallas_call(
        body, name=name, grid=(s // tm,), out_shape=_sds((N_DEV, s, c), BF16),
        in_specs=[_view_rows(wd, d) for (_, d), wd in zip(pieces, widths)],
        out_specs=pl.BlockSpec((N_DEV, tm, c), lambda i: (0, i, 0)),
        scratch_shapes=[pltpu.VMEM((tm, n), BF16), _view_scratch(GROUP_WIDTH)],
        compiler_params=_params(1, 40),
    )(*[p for p, _ in pieces])


def _dgrad_norm_dm(dproj_dm, wg, h, g, dres, token, name):
    s, d_model = h.shape
    c = wg.shape[2]
    tm = ROW_TILE

    def body(_, dp_ref, w_ref, h_ref, g_ref, dr_ref, dh_ref, dg_ref):
        @pl.when(pl.program_id(0) == 0)
        def _():
            dg_ref[...] = jnp.zeros_like(dg_ref)

        dhn = jnp.zeros((tm, d_model), F32)
        for j in range(N_DEV):
            dhn += _dot_nt(dp_ref[j], w_ref[j])
        hb = h_ref[...]
        r = lax.rsqrt(jnp.mean(hb * hb, axis=-1, keepdims=True) + EPS)
        dg_ref[...] += jnp.sum(dhn * (hb * r), axis=0, keepdims=True)
        dn = dhn * g_ref[...]
        dh_ref[...] = dr_ref[...] + r * dn - hb * ((r * r * r) * jnp.mean(dn * hb, axis=-1, keepdims=True))

    return _pallas_call(
        body, name=name, grid=(s // tm,),
        out_shape=[_plain((s, d_model), F32), _plain((1, d_model), F32)],
        in_specs=[ANY, pl.BlockSpec((N_DEV, tm, c), lambda i: (0, i, 0)), _whole(wg.shape), _rows(d_model),
                  _whole((1, d_model)), _rows(d_model)],
        out_specs=[_rows(d_model), _whole((1, d_model))],
        compiler_params=_params(1, 60),
    )(token, dproj_dm, wg, h, g, dres)


def _wgrad_shards(a_t, b_dm, name):
    m, s = a_t.shape
    c = b_dm.shape[2]

    def body(a_ref, b_ref, o_ref):
        o_ref[...] = _dot(a_ref[...], b_ref[...]).astype(BF16)

    return _pallas_call(
        body, name=name, grid=(N_DEV,), out_shape=_sds((N_DEV, m, c), BF16),
        in_specs=[_whole(a_t.shape), pl.BlockSpec((None, s, c), lambda j: (j, 0, 0))],
        out_specs=pl.BlockSpec((None, m, c), lambda j: (j, 0, 0)), compiler_params=_params(1, 40),
    )(a_t, b_dm)


def _wgrad_shards_of_core(a_t, b_dm, core_arr, token, name):
    m, s = a_t.shape
    c = b_dm.shape[2]

    def body(core_ref, _, a_ref, b_ref, o_ref):
        o_ref[...] = _dot(a_ref[...], b_ref[...]).astype(BF16)

    return _pallas_call(
        body, name=name,
        grid_spec=pltpu.PrefetchScalarGridSpec(
            num_scalar_prefetch=1, grid=(N_DEV // 2,),
            in_specs=[ANY, _whole(a_t.shape), pl.BlockSpec((None, s, c), lambda k, core: (2 * k + core[0], 0, 0))],
            out_specs=pl.BlockSpec((None, m, c), lambda k, core: (k, 0, 0))),
        out_shape=_sds((N_DEV // 2, m, c), BF16), compiler_params=_params(1, 40),
    )(core_arr, token, a_t, b_dm)


def _wgrad_cols(a_t, b, c, name):
    m, s = a_t.shape

    def body(a_ref, b_ref, o_ref):
        o_ref[...] = _dot(a_ref[...], b_ref[...]).astype(BF16)

    return _pallas_call(
        body, name=name, grid=(N_DEV,), out_shape=_sds((N_DEV, m, c), BF16),
        in_specs=[_whole(a_t.shape), pl.BlockSpec((s, c), lambda j: (0, j))],
        out_specs=pl.BlockSpec((None, m, c), lambda j: (j, 0, 0)), compiler_params=_params(1, 40),
    )(a_t, b)


def _wgrad_rows(a_t, b, name):
    m, s = a_t.shape
    n = b.shape[1]
    mr = m // N_DEV

    def body(a_ref, b_ref, o_ref):
        o_ref[...] = _dot(a_ref[...], b_ref[...]).astype(BF16)

    return _pallas_call(
        body, name=name, grid=(N_DEV,), out_shape=_sds((N_DEV, mr, n), BF16),
        in_specs=[pl.BlockSpec((mr, s), lambda j: (j, 0)), _whole(b.shape)],
        out_specs=pl.BlockSpec((None, mr, n), lambda j: (j, 0, 0)), compiler_params=_params(1, 40),
    )(a_t, b)


def _memkv_bwd(dkv, w, mem, g):
    n_layers = w.shape[0]

    def body(dkv_ref, w_ref, mem_ref, g_ref, dw_ref, dg_ref):
        mb = mem_ref[...]
        r = lax.rsqrt(jnp.mean(mb * mb, axis=-1, keepdims=True) + EPS)
        nm = mb * r
        mn = (nm * g_ref[...]).astype(BF16)
        dkvb = dkv_ref[...].astype(BF16)
        dw_ref[...] = _dot_tn(mn, dkvb).astype(BF16)
        dmn = _dot_nt(dkvb, w_ref[...])
        dg_ref[...] = jnp.sum(dmn * nm, axis=0, keepdims=True)

    lay = lambda *shape: pl.BlockSpec((None,) + shape, lambda l: (l, 0, 0))
    return _pallas_call(
        body, name="memkv_bwd", grid=(n_layers,),
        out_shape=[_plain((n_layers, D_MODEL, 2 * MEM_WIDTH), BF16), _plain((n_layers, 1, D_MODEL), F32)],
        in_specs=[lay(N_MEM, 2 * MEM_WIDTH), lay(D_MODEL, 2 * MEM_WIDTH), _whole(mem.shape), lay(1, D_MODEL)],
        out_specs=[lay(D_MODEL, 2 * MEM_WIDTH), lay(1, D_MODEL)], compiler_params=_params(1, 32),
    )(dkv, w, mem, g.reshape(n_layers, 1, D_MODEL))


def _bwd_post_attn(dhb, wg_out, z, os_, ls_, qm, kv, mo, head_ones):
    s = dhb.shape[0]
    gw = GROUP_WIDTH
    nb = gw + MEM_WIDTH
    c = wg_out.shape[2]
    tm = ROW_TILE

    def body(dh_ref, w_ref, z_ref, o0, o1, o2, l0, l1, l2, qm_ref, kv_ref, mo_ref, bd_ref,
             dz_ref, do0, do1, do2, dl0, dl1, dl2, dqm_ref, dkv_ref, s0, s1):
        @pl.when(pl.program_id(0) == 0)
        def _():
            dkv_ref[...] = jnp.zeros_like(dkv_ref)

        dy = jnp.zeros((tm, nb), F32)
        for j in range(N_DEV):
            dy += _dot_nt(dh_ref[:, j * c:(j + 1) * c], w_ref[j])
        ov, lv = [], []
        for o_ref, l_ref, d in zip((o0, o1, o2), (l0, l1, l2), DILATIONS):
            ov.append(_from_view(s0, o_ref, d))
            lv.append(_from_view(s1, l_ref, d))
        ws, mix = _mix_groups(ov, lv)
        sz, dsz = _silu_parts(z_ref[...])
        dz_ref[:, :gw] = (dy[:, :gw] * mix * dsz[:, :gw]).astype(BF16)
        dz_ref[:, gw:] = (dy[:, gw:] * mo_ref[...] * dsz[:, gw:]).astype(BF16)
        dbr = dy * sz
        dmix = dbr[:, :gw]
        t = dmix * mix
        th = t.astype(BF16)
        tl = (t - th.astype(F32)).astype(BF16)
        rs = _dot(th, bd_ref[...]) + _dot(tl, bd_ref[...])
        for wg_, do_ref, dl_ref, d in zip(ws, (do0, do1, do2), (dl0, dl1, dl2), DILATIONS):
            _to_view(s0, wg_ * dmix, do_ref, d)
            _to_view(s1, wg_ * rs, dl_ref, d)
        _mem_attn_bwd(qm_ref[...], kv_ref, dbr[:, gw:], dqm_ref, dkv_ref)

    vspecs = [_view_rows(gw, d) for d in DILATIONS]
    return _pallas_call(
        body, name="bwd_post_attn", grid=(s // tm,),
        out_shape=[_sds((s, nb), BF16)] + [_sds((s // d, d * gw), BF16) for d in DILATIONS]
                  + [_sds((s // d, d * gw), F32) for d in DILATIONS] + [_sds((s, MEM_WIDTH), BF16), _plain(kv.shape, F32)],
        in_specs=[_rows(D_MODEL), _whole(wg_out.shape), _rows(nb)] + vspecs * 2
                 + [_rows(MEM_WIDTH), _whole(kv.shape), _rows(MEM_WIDTH), _whole(head_ones.shape)],
        out_specs=[_rows(nb)] + vspecs * 2 + [_rows(MEM_WIDTH), _whole(kv.shape)],
        scratch_shapes=[_view_scratch(gw), _view_scratch(gw)],
        compiler_params=_params(1, 48),
    )(dhb, wg_out, z, *os_, *ls_, qm, kv, mo, head_ones)


def _attn_bwd(q, k, v, lse, do, dl, tabs, d, token):
    ln, dw = q.shape
    w = dw // d
    nb = ln // BLOCK
    reps = w // LANES
    two, before = _pair_specs(d, nb, w)
    two_t, _ = _pair_specs(d, nb, LANES)

    def attend(q_ref, l_ref, do_ref, dl_ref, dqs, acck, accv, rows, col0, kk, vv, acc_rows):
        valid = _band_mask(kk.shape[0])
        low = _low_head_lanes()
        pairs, qcols = _head_tiles(w, col0)
        cols = [slice(col0 + h * HEAD_DIM, col0 + h * HEAD_DIM + 1) for h in range(HEADS_PER_GROUP)]
        qhs = [h for qc in qcols for h in _split_pair(q_ref[rows, qc], low)]
        dobs = [h for qc in qcols for h in _split_pair(do_ref[rows, qc], low)]
        k2s = [kk[:, pr] for pr in pairs for _ in range(2)]
        v2s = [vv[:, pr] for pr in pairs for _ in range(2)]
        scs = [jnp.where(valid, _dot_nt(qh, k2), NEG) for qh, k2 in zip(qhs, k2s)]
        dps = [_dot_nt(dob, v2) for dob, v2 in zip(dobs, v2s)]
        ps = [jnp.exp(sc - l_ref[rows, col]) for sc, col in zip(scs, cols)]
        dss = [(p * (dp - dl_ref[rows, col])).astype(BF16) for p, dp, col in zip(ps, dps, cols)]
        pbs = [p.astype(BF16) for p in ps]
        for i, qc in enumerate(qcols):
            a, b = 2 * i, 2 * i + 1
            dqs[rows, qc] = jnp.where(low, _dot(dss[a], k2s[a]), _dot(dss[b], k2s[b])) * SCALE
            acck[acc_rows, qc] += _dot_tn(dss[a], qhs[a]) + _dot_tn(dss[b], qhs[b])
            accv[acc_rows, qc] += _dot_tn(pbs[a], dobs[a]) + _dot_tn(pbs[b], dobs[b])

    def body_streams(_, q_ref, kc_ref, vc_ref, l_ref, do_ref, dl_ref, c_ref, sa_ref, sb_ref,
                     dq_ref, dk_ref, dv_ref, acck, accv, dqs):
        acck[...] = jnp.zeros_like(acck)
        accv[...] = jnp.zeros_like(accv)
        for sb in range(2):
            cols = slice(sb * w, (sb + 1) * w)
            attend(q_ref, l_ref, do_ref, dl_ref, dqs, acck, accv, TOP, sb * w, kc_ref[:, cols], vc_ref[:, cols], TOP)
        tabs2 = [jnp.concatenate([jnp.tile(r[:, sb * LANES:(sb + 1) * LANES], (1, reps)) for sb in range(2)], axis=1)
                 for r in (c_ref, sa_ref, sb_ref)]
        dq_ref[...] = _rope_bwd(dqs[...], *tabs2).astype(BF16)
        dk_ref[...] = _rope_bwd(acck[...], *tabs2).astype(BF16)
        dv_ref[...] = accv[...].astype(BF16)

    def body_blocks(_, q_ref, kp_ref, kc_ref, vp_ref, vc_ref, l_ref, do_ref, dl_ref, cq, saq, sbq, ck, sak, sbk,
                    dq_ref, dk_ref, dv_ref, acck, accv, dqs):
        i = pl.program_id(0) % (nb // 2)

        @pl.when(i == 0)
        def _():
            acck[...] = jnp.zeros_like(acck)
            accv[...] = jnp.zeros_like(accv)

        refs = (q_ref, l_ref, do_ref, dl_ref, dqs, acck, accv)
        pl.when(i == 0)(lambda: attend(*refs, TOP, 0, kc_ref[TOP, :], vc_ref[TOP, :], TOP))
        pl.when(i != 0)(lambda: attend(
            *refs, TOP, 0, jnp.concatenate([kp_ref[...], kc_ref[TOP, :]], axis=0),
            jnp.concatenate([vp_ref[...], vc_ref[TOP, :]], axis=0),
            pl.ds(pl.multiple_of((2 * i - 1) * BLOCK, BLOCK), 2 * BLOCK)))
        attend(*refs, BOTTOM, 0, kc_ref[...], vc_ref[...], pl.ds(pl.multiple_of(2 * i * BLOCK, BLOCK), 2 * BLOCK))
        tq = [jnp.tile(r[...], (1, reps)) for r in (cq, saq, sbq)]
        dq_ref[...] = _rope_bwd(dqs[...], *tq).astype(BF16)

        @pl.when(i == nb // 2 - 1)
        def _():
            for r0 in range(0, nb * BLOCK, 2 * BLOCK):
                rows = slice(r0, r0 + 2 * BLOCK)
                tk = [jnp.tile(r[rows, :], (1, reps)) for r in (ck, sak, sbk)]
                dk_ref[rows, :] = _rope_bwd(acck[rows, :], *tk).astype(BF16)
                dv_ref[rows, :] = accv[rows, :].astype(BF16)

    if nb == 1:
        body = body_streams
        in_specs = [ANY] + [two] * 6 + [two_t] * 3
        args = (token, q, k, v, lse, do, dl, *tabs)
        out_specs = [two, two, two]
        acc_shape = (BLOCK, 2 * w)
    else:
        body = body_blocks
        stream = pl.BlockSpec((nb * BLOCK, w), lambda n: (0, n // (nb // 2)))
        stream_t = pl.BlockSpec((nb * BLOCK, LANES), lambda n: (0, n // (nb // 2)))
        in_specs = [ANY, two, before, two, before, two, two, two, two] + [two_t] * 3 + [stream_t] * 3
        args = (token, q, k, k, v, v, lse, do, dl, *tabs, *tabs)
        out_specs = [two, stream, stream]
        acc_shape = (nb * BLOCK, w)
    return _pallas_call(
        body, name=f"attn_bwd_d{d}", grid=(d * nb // 2,), out_shape=[_sds((ln, dw), BF16)] * 3,
        in_specs=in_specs, out_specs=out_specs,
        scratch_shapes=[pltpu.VMEM(acc_shape, F32), pltpu.VMEM(acc_shape, F32), pltpu.VMEM(two.block_shape, F32)],
        compiler_params=_params(1, 48),
    )(*args)


def _position():
    return lax.axis_index("x"), lax.axis_index("y"), lax.axis_index("c")


def _all_gather(shards, after, name):
    n_a = len(shards)

    def body(*refs):
        x_refs, out_refs = refs[:n_a], refs[n_a + 1:2 * n_a + 1]
        send_sems, recv_sems, local_sems = refs[2 * n_a + 1:]
        x, y, c = _position()
        me, sibling = (x, y, c), (x, y, 1 - c)
        chips = [(1 - x, y), (x, 1 - y), (1 - x, 1 - y)]

        def rows(a, px, py, pc):
            return out_refs[a].at[4 * px + 2 * py + pc]

        def copy(a, k, block, to, own=False):
            return pltpu.make_async_remote_copy(
                src_ref=x_refs[a] if own else rows(a, *block), dst_ref=rows(a, *block),
                send_sem=send_sems.at[a, k], recv_sem=recv_sems.at[a, k], device_id=to, device_id_type=MESH)

        mine = [pltpu.make_async_copy(x_refs[a], rows(a, *me), local_sems.at[a]) for a in range(n_a)]
        for cp in mine:
            cp.start()
        first = []
        for j, chip in enumerate(chips):
            first += [copy(a, 1 + j, me, (*chip, c), own=True) for a in range(n_a)]
        first += [copy(a, 0, me, sibling, own=True) for a in range(n_a)]
        for cp in first:
            cp.start()
        passed = []
        for j, chip in enumerate(chips):
            for a in range(n_a):
                copy(a, 1 + j, (*chip, c), me).wait_recv()
                fwd = copy(a, 4 + j, (*chip, c), sibling)
                fwd.start()
                passed.append(fwd)
        for a in range(n_a):
            copy(a, 0, sibling, me).wait_recv()
        for j, chip in enumerate(chips):
            for a in range(n_a):
                copy(a, 4 + j, (*chip, 1 - c), me).wait_recv()
        for cp in first + passed:
            cp.wait_send()
        for cp in mine:
            cp.wait()

    return _pallas_call(
        body, name=name, out_shape=[_sds((N_DEV,) + t.shape, t.dtype) for t in shards],
        in_specs=[ANY] * (n_a + 1), out_specs=[ANY] * n_a,
        scratch_shapes=[pltpu.SemaphoreType.DMA((n_a, 7)), pltpu.SemaphoreType.DMA((n_a, 7)),
                        pltpu.SemaphoreType.DMA((n_a,))],
    )(*shards, after)


def _all_gather_relay(xs, name):
    def body(x_ref, out_ref, send_sems, recv_sems, local_sem):
        x, y, c = _position()
        me, sibling = (x, y, c), (x, y, 1 - c)
        xn, yn, diag = (1 - x, y, c), (x, 1 - y, c), (1 - x, 1 - y, c)
        src_nb = (x + c * (1 - 2 * x), y + (1 - c) * (1 - 2 * y), c)
        dst_nb = (x + (1 - c) * (1 - 2 * x), y + c * (1 - 2 * y), c)

        def rows(dev):
            return out_ref.at[4 * dev[0] + 2 * dev[1] + dev[2]]

        def copy(k, block, to, own=False):
            return pltpu.make_async_remote_copy(
                src_ref=x_ref if own else rows(block), dst_ref=rows(block),
                send_sem=send_sems.at[k], recv_sem=recv_sems.at[k], device_id=to, device_id_type=MESH)

        mine = pltpu.make_async_copy(x_ref, rows(me), local_sem)
        mine.start()
        first = [copy(1, me, xn, own=True), copy(2, me, yn, own=True), copy(0, me, sibling, own=True)]
        for cp in first:
            cp.start()
        copy(1, xn, me).wait_recv()
        copy(2, yn, me).wait_recv()
        relay = copy(3, src_nb, dst_nb)
        relay.start()
        passed = [copy(4, xn, sibling), copy(5, yn, sibling)]
        for cp in passed:
            cp.start()
        copy(3, diag, me).wait_recv()
        last = copy(6, diag, sibling)
        last.start()
        copy(0, sibling, me).wait_recv()
        for k, blk in ((4, (1 - x, y, 1 - c)), (5, (x, 1 - y, 1 - c)), (6, (1 - x, 1 - y, 1 - c))):
            copy(k, blk, me).wait_recv()
        for cp in first + [relay] + passed + [last]:
            cp.wait_send()
        mine.wait()

    return _pallas_call(
        body, name=name, out_shape=_sds((N_DEV,) + xs.shape, xs.dtype),
        in_specs=[ANY], out_specs=ANY,
        scratch_shapes=[pltpu.SemaphoreType.DMA((7,)), pltpu.SemaphoreType.DMA((7,)), pltpu.SemaphoreType.DMA],
    )(xs)


def _rs_to_sibling(gs):
    n_a = len(gs)

    def body(*refs):
        g_refs, recv_refs = refs[:n_a], refs[n_a:2 * n_a]
        send_sems, recv_sems = refs[2 * n_a:]
        x, y, c = _position()
        copies = []
        for k in range(4):
            for a in range(n_a):
                copies.append(pltpu.make_async_remote_copy(
                    src_ref=g_refs[a].at[2 * k + (1 - c)], dst_ref=recv_refs[a].at[k],
                    send_sem=send_sems.at[a, k], recv_sem=recv_sems.at[a, k],
                    device_id=(x, y, 1 - c), device_id_type=MESH))
        for cp in copies:
            cp.start()
        for cp in copies:
            cp.wait()

    return _pallas_call(
        body, name="rs_to_sibling", out_shape=[_sds((4,) + g.shape[1:], g.dtype) for g in gs],
        in_specs=[ANY] * n_a, out_specs=[ANY] * n_a,
        scratch_shapes=[pltpu.SemaphoreType.DMA((n_a, 4)), pltpu.SemaphoreType.DMA((n_a, 4))],
    )(*gs)


def _rs_to_chips(pbs):
    n_a = len(pbs)

    def body(*refs):
        p_refs, recv_refs = refs[:n_a], refs[n_a:2 * n_a]
        send_sems, recv_sems = refs[2 * n_a:]
        x, y, c = _position()
        chips = [(1 - x, y), (x, 1 - y), (1 - x, 1 - y)]
        copies = []
        for j, (px, py) in enumerate(chips):
            for a in range(n_a):
                copies.append(pltpu.make_async_remote_copy(
                    src_ref=p_refs[a].at[2 * px + py], dst_ref=recv_refs[a].at[j],
                    send_sem=send_sems.at[a, j], recv_sem=recv_sems.at[a, j],
                    device_id=(px, py, c), device_id_type=MESH))
        for cp in copies:
            cp.start()
        for cp in copies:
            cp.wait()

    return _pallas_call(
        body, name="rs_to_chips", out_shape=[_sds((3,) + p.shape[1:], p.dtype) for p in pbs],
        in_specs=[ANY] * n_a, out_specs=[ANY] * n_a,
        scratch_shapes=[pltpu.SemaphoreType.DMA((n_a, 3)), pltpu.SemaphoreType.DMA((n_a, 3))],
    )(*pbs)


HBM_SPEC = pl.BlockSpec(memory_space=pltpu.HBM)
SEM_SPEC = pl.BlockSpec(memory_space=pltpu.SEMAPHORE)
EFFECT = pltpu.SideEffectType.DATAFLOW_SIDE_EFFECTING
def _plan_gather_own(src_refs, land_refs):
    x, y, c = _position()
    me = 4 * x + 2 * y + c
    peers = [(x, y, 1 - c), (1 - x, y, c), (x, 1 - y, c), (1 - x, 1 - y, c)]
    return [(src_refs[a], land_refs[a].at[me], (a, k), peer) for k, peer in enumerate(peers) for a in range(len(src_refs))]


def _plan_gather_pass(src_refs, land_refs):
    x, y, c = _position()
    chips = [(1 - x, y), (x, 1 - y), (1 - x, 1 - y)]
    return [(land_refs[a].at[4 * px + 2 * py + c], land_refs[a].at[4 * px + 2 * py + c], (a, j), (x, y, 1 - c))
            for j, (px, py) in enumerate(chips) for a in range(len(land_refs))]


def _plan_to_sibling(src_refs, land_refs):
    x, y, c = _position()
    return [(src_refs[a].at[2 * k + (1 - c)], land_refs[a].at[k], (a, k), (x, y, 1 - c))
            for k in range(4) for a in range(len(src_refs))]


def _plan_four_to_sibling(src_refs, land_refs):
    x, y, c = _position()
    return [(src_refs[a].at[k], land_refs[a].at[k], (a, k), (x, y, 1 - c)) for k in range(4) for a in range(len(src_refs))]


def _plan_to_chips(src_refs, land_refs):
    x, y, c = _position()
    chips = [(1 - x, y), (x, 1 - y), (1 - x, 1 - y)]
    return [(src_refs[a].at[2 * px + py], land_refs[a].at[j], (a, j), (px, py, c))
            for j, (px, py) in enumerate(chips) for a in range(len(src_refs))]


def _split_start(srcs, lands, plan, n_sem, after, name):
    n_s, n_a = len(srcs), len(lands)
    n_b = n_s + n_a

    def body(*refs):
        src_refs, land_refs = refs[:n_s], refs[n_s:n_b]
        send_sems, recv_sems, token = refs[n_b + 1], refs[n_b + 2], refs[-1]
        for src, dst, (a, k), dev in plan(src_refs, land_refs):
            i = a * n_sem + k
            pltpu.make_async_remote_copy(src_ref=src, dst_ref=dst, send_sem=send_sems.at[i], recv_sem=recv_sems.at[i],
                                         device_id=dev, device_id_type=MESH).start()
        token[...] = jnp.zeros_like(token)

    bufs = list(srcs) + list(lands)
    res = pl.pallas_call(
        body, name=name,
        out_shape=(pltpu.SemaphoreType.DMA((n_a * n_sem,)), pltpu.SemaphoreType.DMA((n_a * n_sem,)),
                   *[pltpu.HBM(t.shape, t.dtype) for t in bufs], _plain((8, LANES), F32)),
        in_specs=[HBM_SPEC] * n_b + [ANY],
        out_specs=(SEM_SPEC, SEM_SPEC, *[HBM_SPEC] * n_b, pl.BlockSpec(memory_space=pltpu.VMEM)),
        input_output_aliases={i: 2 + i for i in range(n_b)},
        compiler_params=pltpu.CompilerParams(has_side_effects=EFFECT),
    )(*[pltpu.with_memory_space_constraint(t, pltpu.HBM) for t in bufs], after)
    return (res[0], res[1], res[2:2 + n_s], res[2 + n_s:2 + n_b]), res[-1]


def _split_wait(started, plan, after, name):
    send_sems, recv_sems, srcs, lands = started
    n_s, n_a = len(srcs), len(lands)
    n_b = n_s + n_a
    n_sem = send_sems.shape[0] // n_a

    def body(*refs):
        src_refs, land_refs = refs[:n_s], refs[n_s:n_b]
        s_sems, r_sems = refs[n_b], refs[n_b + 1]
        for src, dst, (a, k), dev in plan(src_refs, land_refs):
            i = a * n_sem + k
            cp = pltpu.make_async_remote_copy(src_ref=src, dst_ref=dst, send_sem=s_sems.at[i], recv_sem=r_sems.at[i],
                                              device_id=dev, device_id_type=MESH)
            cp.wait_send()
            cp.wait_recv()

    bufs = list(srcs) + list(lands)
    res = pl.pallas_call(
        body, name=name, out_shape=tuple(pltpu.HBM(t.shape, t.dtype) for t in bufs),
        in_specs=[HBM_SPEC] * n_b + [SEM_SPEC, SEM_SPEC, ANY],
        out_specs=tuple([HBM_SPEC] * n_b),
        input_output_aliases={i: i for i in range(n_b)},
        compiler_params=pltpu.CompilerParams(has_side_effects=EFFECT),
    )(*bufs, send_sems, recv_sems, after)
    return res[:n_s], res[n_s:]


def _row_tile(r):
    return ROW_TILE if r % ROW_TILE == 0 else r


def _rs_add_sibling(gp, recv, c_arr, name):
    _, r, l = gp.shape
    tr = r if r <= 4 * ROW_TILE else _row_tile(r)

    def body(c_ref, g_ref, r_ref, pf_ref, pb_ref):
        sm = g_ref[...].astype(F32) + r_ref[...].astype(F32)
        pf_ref[...] = sm
        pb_ref[...] = sm.astype(BF16)

    spec = pl.BlockSpec((None, tr, l), lambda k, i, c: (k, i, 0))
    return _pallas_call(
        body, name=name,
        grid_spec=pltpu.PrefetchScalarGridSpec(
            num_scalar_prefetch=1, grid=(4, r // tr),
            in_specs=[pl.BlockSpec((None, tr, l), lambda k, i, c: (2 * k + c[0], i, 0)), spec],
            out_specs=[spec, spec]),
        out_shape=[_sds((4, r, l), F32), _sds((4, r, l), BF16)], compiler_params=_params(2, 32),
    )(c_arr, gp, recv)


def _rs_add_four(mine, recv, name):
    _, r, l = mine.shape
    tr = r if r <= 4 * ROW_TILE else _row_tile(r)

    def body(g_ref, r_ref, pf_ref, pb_ref):
        sm = g_ref[...].astype(F32) + r_ref[...].astype(F32)
        pf_ref[...] = sm
        pb_ref[...] = sm.astype(BF16)

    spec = pl.BlockSpec((None, tr, l), lambda k, i: (k, i, 0))
    return _pallas_call(
        body, name=name, grid=(4, r // tr), in_specs=[spec, spec], out_specs=[spec, spec],
        out_shape=[_sds((4, r, l), F32), _sds((4, r, l), BF16)], compiler_params=_params(2, 32),
    )(mine, recv)


def _adam_update(w, gv, m, v):
    nm = ADAM_B1 * m + (1.0 - ADAM_B1) * gv
    nv = ADAM_B2 * v + (1.0 - ADAM_B2) * (gv * gv)
    m_hat = nm / (1.0 - ADAM_B1 ** ADAM_STEP)
    v_hat = nv / (1.0 - ADAM_B2 ** ADAM_STEP)
    return -ADAM_LR * (m_hat / (jnp.sqrt(v_hat) + ADAM_EPS) + ADAM_WD * w), nm, nv


def _rs_finish_adamw(pf, recv, k_arr, w, m, v, name):
    _, r, l = pf.shape
    tr = _row_tile(r)

    def body(k_ref, p_ref, r_ref, w_ref, m_ref, v_ref, g_ref, d_ref, nm_ref, nv_ref):
        gv = ((p_ref[...] + r_ref[0].astype(F32)) + r_ref[1].astype(F32)) + r_ref[2].astype(F32)
        g_ref[...] = gv
        d_ref[...], nm_ref[...], nv_ref[...] = _adam_update(w_ref[...], gv, m_ref[...], v_ref[...])

    spec = pl.BlockSpec((tr, l), lambda i, k: (i, 0))
    return _pallas_call(
        body, name=name,
        grid_spec=pltpu.PrefetchScalarGridSpec(
            num_scalar_prefetch=1, grid=(r // tr,),
            in_specs=[pl.BlockSpec((None, tr, l), lambda i, k: (k[0], i, 0)),
                      pl.BlockSpec((3, tr, l), lambda i, k: (0, i, 0)), spec, spec, spec],
            out_specs=[spec] * 4),
        out_shape=[_plain((r, l), F32)] * 4, compiler_params=_params(1, 32),
    )(k_arr, pf, recv, w, m, v)


def _rs_finish_adamw_t(pf, recv, k_arr, w_t, m_t, v_t, name):
    _, r, c = pf.shape
    tr = _row_tile(r)
    cp = -(-c // LANES) * LANES

    def body(k_ref, p_ref, r_ref, w_ref, m_ref, v_ref, g_ref, d_ref, nm_ref, nv_ref, pad):
        gv = ((p_ref[...] + r_ref[0].astype(F32)) + r_ref[1].astype(F32)) + r_ref[2].astype(F32)
        pad[...] = jnp.zeros_like(pad)
        pad[:, 0:c] = gv
        gt = pad[...].T[0:c, :]
        g_ref[...] = gt
        d_ref[...], nm_ref[...], nv_ref[...] = _adam_update(w_ref[...], gt, m_ref[...], v_ref[...])

    spec = pl.BlockSpec((c, tr), lambda i, k: (0, i))
    return _pallas_call(
        body, name=name,
        grid_spec=pltpu.PrefetchScalarGridSpec(
            num_scalar_prefetch=1, grid=(r // tr,),
            in_specs=[pl.BlockSpec((None, tr, c), lambda i, k: (k[0], i, 0)),
                      pl.BlockSpec((3, tr, c), lambda i, k: (0, i, 0)), spec, spec, spec],
            out_specs=[spec] * 4, scratch_shapes=[pltpu.VMEM((tr, cp), F32)]),
        out_shape=[_plain((c, r), F32)] * 4, compiler_params=_params(1, 32),
    )(k_arr, pf, recv, w_t, m_t, v_t)


def _sum_devices(g):
    def body(g_ref, o_ref):
        acc = g_ref[0]
        for j in range(1, N_DEV):
            acc = acc + g_ref[j]
        o_ref[...] = acc

    return _pallas_call(body, name="sum_devices", out_shape=_plain(g.shape[1:], F32))(g)


def _adamw(w, g, m, v, name):
    shape = w.shape
    w2, g2, m2, v2 = [t.reshape((-1, shape[-1])) for t in (w, g, m, v)]

    def body(w_ref, g_ref, m_ref, v_ref, d_ref, nm_ref, nv_ref):
        d_ref[...], nm_ref[...], nv_ref[...] = _adam_update(w_ref[...], g_ref[...], m_ref[...], v_ref[...])

    outs = _pallas_call(body, name=name, out_shape=[_plain(w2.shape, F32)] * 3)(w2, g2, m2, v2)
    return tuple(t.reshape(shape) for t in outs)


def _after(t, token):
    return t + token[0:1, 0:1].astype(t.dtype)


def _finish(name, pf, recv, k_arr, w, m, v):
    if name in ("attn_w_in", "conv_w_in"):
        res = _rs_finish_adamw_t(pf, recv, k_arr, w.T, m.T, v.T, "rs_finish_adamw_" + name)
        return tuple(t.T for t in res)
    return _rs_finish_adamw(pf, recv, k_arr, w, m, v, "rs_finish_adamw_" + name)


def kernel(x, mem, positions, norm_g, mem_norm_g, w_mem_kv, attn_w_in, attn_w_out, conv_w_in, conv_w, conv_w_out, final_g, loss_target, m_norm_g, m_mem_norm_g, m_w_mem_kv, m_attn_w_in, m_attn_w_out, m_conv_w_in, m_conv_w, m_conv_w_out, m_final_g, v_norm_g, v_mem_norm_g, v_w_mem_kv, v_attn_w_in, v_attn_w_out, v_conv_w_in, v_conv_w, v_conv_w_out, v_final_g):
    px, py, pc = _position()
    me = 4 * px + 2 * py + pc
    c_arr = jnp.reshape(pc, (1,)).astype(jnp.int32)
    k_arr = jnp.reshape(2 * px + py, (1,)).astype(jnp.int32)
    x, mem, pos, tgt = x[0], mem[0], positions[0], loss_target[0]

    wg_in0 = _all_gather_relay(attn_w_in[0].astype(BF16), "gather_w_in0")
    late = [attn_w_out[0].astype(BF16), conv_w_in[0].astype(BF16), conv_w_out[0].astype(BF16),
            w_mem_kv[0].astype(BF16), w_mem_kv[1].astype(BF16), jnp.pad(conv_w[0], ((0, 5), (0, 0)))]
    lands = [lax.dynamic_update_slice(lax.empty((N_DEV,) + t.shape, t.dtype), t[None], (me, 0, 0)) for t in late]
    late_weights, late_token = _split_start(late, lands, _plan_gather_own, 4, wg_in0, "gather_late_start")

    tabs = _rope_tables(pos)
    g0, g1 = _after(norm_g[0:1], late_token), norm_g[1:2]

    hn0, hn0_t, qs, ks, vs, tabs_v, qm0, z0 = _inproj_attn(x, g0, wg_in0, tabs)
    os_, ls_ = [], []
    for j, d in enumerate(DILATIONS):
        if j == 2:
            _, lands = _split_wait(late_weights, _plan_gather_own, ls_[1], "gather_late_wait")
            late_weights, late_token = _split_start([], lands, _plan_gather_pass, 3, ls_[1], "gather_late_pass_start")
        o, l = _attn_fwd(qs[j], ks[j], vs[j], d, late_token)
        os_.append(o)
        ls_.append(l)

    _, gathered = _split_wait(late_weights, _plan_gather_pass, ls_[2], "gather_late_pass_wait")
    wg_out0, wg_in1, wg_out1, wg_kv0, wg_kv1, cw_all = gathered
    w_out1 = wg_out1.reshape(-1, wg_out1.shape[2])
    w_kv = jnp.stack([wg_kv0.reshape(-1, wg_kv0.shape[2]), wg_kv1.reshape(-1, wg_kv1.shape[2])])
    cw = cw_all[:, 0:3].transpose(1, 0, 2).reshape(3, -1)
    kv = _memkv_fwd(mem, mem_norm_g, w_kv)
    y0, y0_t, mo0, h1 = _post_attn(os_, ls_, qm0, kv[0], z0, x, wg_out0)

    hn1, hn1_t, bg, cg, u, qm1, z1 = _inproj_conv(h1, g1, wg_in1)
    y1, y1_t, mo1, dh2, dh2b, loss_acc, d_final_g = _post_conv_loss(
        bg, cg, u, qm1, kv[1], z1, h1, w_out1, cw, final_g.reshape(1, -1), tgt)

    d_w_out1 = _wgrad_rows(y1_t, dh2b, "wgrad_out1")
    dz1, dbg, dconv, dqm1, dkv1 = _bwd_post_conv(dh2b, w_out1, bg, cg, u, z1, qm1, kv[1], mo1, cw)
    dcg, du, dcw = _bwd_conv(dconv, cg, u, cw)
    dh1, dh1b, dg1, dproj1 = _dgrad_norm([(dbg, 1), (dcg, 1), (du, 1), (dqm1, 1), (dz1, 1)], wg_in1, h1, g1, dh2,
                                         "dgrad_norm_conv")
    d_w_in1 = _wgrad_shards(hn1_t, dproj1, "wgrad_in1")

    d_w_out0 = _wgrad_cols(y0_t, dh1b, wg_out0.shape[2], "wgrad_out0")

    names1 = ["conv_w_in", "conv_w_out", "attn_w_out"]
    grads1 = [d_w_in1, d_w_out1, d_w_out0]
    started, token = _split_start(grads1, [lax.empty((4,) + g.shape[1:], g.dtype) for g in grads1],
                                  _plan_to_sibling, 4, dg1, "rs1_sibling_start")

    gw = GROUP_WIDTH
    ones = (jnp.arange(gw)[:, None] // HEAD_DIM == jnp.arange(gw)[None, :] // HEAD_DIM).astype(BF16)
    ones = _after(ones, token)
    res = _bwd_post_attn(dh1b, wg_out0, z0, os_, ls_, qm0, kv[0], mo0, ones)
    dz0, dos, dls, dqm0, dkv0 = res[0], res[1:4], res[4:7], res[7], res[8]

    grads1, from_sibling = _split_wait(started, _plan_to_sibling, dz0, "rs1_sibling_wait")
    parts1 = [_rs_add_sibling(g, r, c_arr, "rs_add_sibling_" + n) for g, r, n in zip(grads1, from_sibling, names1)]
    pbs1 = [pb for _, pb in parts1]
    started, token = _split_start(pbs1, [lax.empty((3,) + p.shape[1:], p.dtype) for p in pbs1],
                                  _plan_to_chips, 3, dg1, "rs1_chips_start")

    dqs, dks, dvs = [], [], []
    for j, d in enumerate(DILATIONS):
        dq, dk, dv = _attn_bwd(qs[j], ks[j], vs[j], ls_[j], dos[j], dls[j], tabs_v[j], d, token)
        dqs.append((dq, d))
        dks.append((dk, d))
        dvs.append((dv, d))
    d_w_kv, d_mem_g = _memkv_bwd(jnp.stack([dkv0, dkv1]), w_kv, mem, mem_norm_g)
    n_kv = d_w_kv.shape[1] // N_DEV
    dproj0 = _assemble_dproj(dqs + dks + dvs + [(dqm0, 1), (dz0, 1)], wg_in0.shape[2], "assemble_dproj_attn")

    names0 = ["attn_w_in", "w_mem_kv0", "w_mem_kv1"]
    kv_dm = [d_w_kv[l].reshape(N_DEV // 2, 2, n_kv, -1) for l in range(2)]
    to_sibling = [_wgrad_shards_of_core(hn0_t, dproj0, 1 - c_arr, token, "wgrad_in0_sibling")]
    to_sibling += [lax.dynamic_index_in_dim(t, 1 - pc, 1, keepdims=False) for t in kv_dm]
    started0, token0 = _split_start(to_sibling, [lax.empty(t.shape, t.dtype) for t in to_sibling],
                                    _plan_four_to_sibling, 4, dg1, "rs0_sibling_start")
    mine = [_wgrad_shards_of_core(hn0_t, dproj0, c_arr, token0, "wgrad_in0_own")]
    mine += [lax.dynamic_index_in_dim(t, pc, 1, keepdims=False) for t in kv_dm]
    _, from_sibling = _split_wait(started0, _plan_four_to_sibling, mine[0], "rs0_sibling_wait")
    parts0 = [_rs_add_four(g, r, "rs_add_sibling_" + n) for g, r, n in zip(mine, from_sibling, names0)]

    pbs0 = [pb for _, pb in parts0]
    started0, token0 = _split_start(pbs0, [lax.empty((3,) + p.shape[1:], p.dtype) for p in pbs0],
                                    _plan_to_chips, 3, dg1, "rs0_chips_start")
    dx, dg0 = _dgrad_norm_dm(dproj0, wg_in0, x, g0, dh1, token0, "dgrad_norm_attn")
    _, from_chips1 = _split_wait(started, _plan_to_chips, dg0, "rs1_chips_wait")
    shard = dict(attn_w_in=(attn_w_in[0], m_attn_w_in[0], v_attn_w_in[0]),
                 attn_w_out=(attn_w_out[0], m_attn_w_out[0], v_attn_w_out[0]),
                 conv_w_in=(conv_w_in[0], m_conv_w_in[0], v_conv_w_in[0]),
                 conv_w_out=(conv_w_out[0], m_conv_w_out[0], v_conv_w_out[0]),
                 w_mem_kv0=(w_mem_kv[0], m_w_mem_kv[0], v_w_mem_kv[0]), w_mem_kv1=(w_mem_kv[1], m_w_mem_kv[1], v_w_mem_kv[1]))
    big = {}
    for n, (pf, _), r in zip(names1, parts1, from_chips1):
        big[n] = _finish(n, pf, r, k_arr, *shard[n])
    _, from_chips0 = _split_wait(started0, _plan_to_chips, big["conv_w_out"][1], "rs0_chips_wait")

    small_part = jnp.concatenate([dg0, dg1, d_mem_g.reshape(2, -1), d_final_g, dcw[0:3]], axis=0)
    small_part = jnp.concatenate([small_part, jnp.broadcast_to(loss_acc[0, 0], small_part.shape)], axis=0)
    small = _sum_devices(_all_gather([small_part], from_chips0[0], "gather_small_grads")[0])
    loss = small[8, 0]
    g_conv_w = lax.dynamic_slice(small[5:8], (0, me * LANES), (3, LANES))[None]
    small_g = dict(norm_g=small[0:2], mem_norm_g=small[2:4], conv_w=g_conv_w, final_g=small[4])
    small_w = dict(norm_g=(norm_g, m_norm_g, v_norm_g), mem_norm_g=(mem_norm_g, m_mem_norm_g, v_mem_norm_g),
                   conv_w=(conv_w, m_conv_w, v_conv_w), final_g=(final_g, m_final_g, v_final_g))
    for n, (w, m, v) in small_w.items():
        big[n] = (small_g[n],) + _adamw(w, small_g[n], m, v, "adamw_" + n)

    for n, (pf, _), r in zip(names0, parts0, from_chips0):
        big[n] = _finish(n, pf, r, k_arr, *shard[n])
    for n in ("attn_w_in", "attn_w_out", "conv_w_in", "conv_w_out"):
        big[n] = tuple(t[None] for t in big[n])
    big["w_mem_kv"] = tuple(jnp.stack([a, b]) for a, b in zip(big["w_mem_kv0"], big["w_mem_kv1"]))

    order = ["norm_g", "mem_norm_g", "w_mem_kv", "attn_w_in", "attn_w_out", "conv_w_in", "conv_w", "conv_w_out", "final_g"]
    return (loss, dx[None], *[big[n][0] for n in order], *[big[n][1] for n in order],
            *[big[n][2] for n in order], *[big[n][3] for n in order])
```

```python
import functools

import jax
import jax.numpy as jnp
from jax import lax
from jax.experimental import pallas as pl
from jax.experimental.pallas import tpu as pltpu

F32 = jnp.float32
BF16 = jnp.bfloat16

N_DEV = 8
D_MODEL = 1024
HEAD_DIM = 64
ROT_DIM = HEAD_DIM // 4
ROPE_THETA = 500000.0
DILATIONS = (1, 4, 16)
HEADS_PER_GROUP = 8
GROUP_WIDTH = HEADS_PER_GROUP * HEAD_DIM
BLOCK = 128
N_MEM = 256
MEM_HEADS = 4
MEM_WIDTH = MEM_HEADS * HEAD_DIM
CONV_WIDTH = D_MODEL
EPS = 1e-6
SCALE = HEAD_DIM ** -0.5
NEG = -1e30

ADAM_LR = 0.001
ADAM_B1 = 0.9
ADAM_B2 = 0.999
ADAM_EPS = 1e-08
ADAM_WD = 0.01
ADAM_STEP = 10

ROW_TILE = 256
LANES = 128
MESH = pl.DeviceIdType.MESH
ANY = pl.BlockSpec(memory_space=pl.ANY)


def _pallas_call(body, **kw):
    call = pl.pallas_call(body, **kw)

    def run(*args):
        pinned = [pltpu.with_memory_space_constraint(a, pltpu.HBM) if jnp.issubdtype(a.dtype, jnp.floating) else a
                  for a in args]
        return call(*pinned)

    return run


def _dot(a, b):
    return lax.dot_general(a, b, (((1,), (0,)), ((), ())), preferred_element_type=F32)


def _dot_nt(a, b):
    return lax.dot_general(a, b, (((1,), (1,)), ((), ())), preferred_element_type=F32)


def _dot_tn(a, b):
    return lax.dot_general(a, b, (((0,), (0,)), ((), ())), preferred_element_type=F32)


def _params(n_grid, vmem_mb=48):
    return pltpu.CompilerParams(dimension_semantics=("arbitrary",) * n_grid, vmem_limit_bytes=vmem_mb << 20)


def _rows(width, tm=ROW_TILE):
    return pl.BlockSpec((tm, width), lambda i: (i, 0))


def _view_rows(width, d, tm=ROW_TILE):
    return pl.BlockSpec((tm // d, d * width), lambda i: (i, 0))


def _whole(shape):
    return pl.BlockSpec(shape, lambda *_: (0,) * len(shape))


def _resident(shape):
    return pl.BlockSpec(shape, lambda *_: (0,) * len(shape), pipeline_mode=pl.Buffered(1))


def _sds(shape, dtype):
    return pltpu.HBM(shape, dtype)


def _plain(shape, dtype):
    return jax.ShapeDtypeStruct(shape, dtype)


def _silu_parts(z):
    sg = jax.nn.sigmoid(z)
    return z * sg, sg * (1.0 + z * (1.0 - sg))


def _to_view(scr, val, out_ref, d):
    tm, w = val.shape
    if d == 1:
        out_ref[...] = val.astype(out_ref.dtype)
        return
    for cb in range(w // LANES):
        scr[cb] = val[:, cb * LANES:(cb + 1) * LANES]
    for r in range(d):
        for cb in range(w // LANES):
            lo = r * w + cb * LANES
            out_ref[:, lo:lo + LANES] = scr[cb, pl.ds(r, tm // d, stride=d), :].astype(out_ref.dtype)


def _from_view(scr, in_ref, d):
    if d == 1:
        return in_ref[...].astype(F32)
    nc, tm, _ = scr.shape
    w = nc * LANES
    for r in range(d):
        for cb in range(nc):
            lo = r * w + cb * LANES
            scr[cb, pl.ds(r, tm // d, stride=d), :] = in_ref[:, lo:lo + LANES].astype(F32)
    return jnp.concatenate([scr[cb] for cb in range(nc)], axis=1)


def _view_scratch(width, tm=ROW_TILE):
    return pltpu.VMEM((width // LANES, tm, LANES), F32)


def _rope_tables(pos):
    half = ROT_DIM // 2
    inv_freq = ROPE_THETA ** (-jnp.arange(half, dtype=F32) * (2.0 / ROT_DIM))
    ang = pos.astype(F32)[:, None] * inv_freq
    cos, sin = jnp.cos(ang), jnp.sin(ang)
    s = pos.shape[0]
    z8 = jnp.zeros((s, half), F32)
    rest = HEAD_DIM - ROT_DIM
    cosf = jnp.concatenate([cos, cos, jnp.ones((s, rest), F32)], axis=1)
    sa = jnp.concatenate([-sin, z8, jnp.zeros((s, rest), F32)], axis=1)
    sb = jnp.concatenate([z8, sin, jnp.zeros((s, rest), F32)], axis=1)
    return tuple(jnp.tile(t, (1, LANES // HEAD_DIM)) for t in (cosf, sa, sb))


def _rope_fwd(t, cv, sav, sbv):
    w = t.shape[1]
    return t * cv + pltpu.roll(t, w - ROT_DIM // 2, 1) * sav + pltpu.roll(t, ROT_DIM // 2, 1) * sbv


def _rope_bwd(g, cv, sav, sbv):
    w = g.shape[1]
    return g * cv + pltpu.roll(g * sav, ROT_DIM // 2, 1) + pltpu.roll(g * sbv, w - ROT_DIM // 2, 1)


def _project(hn, wg_ref, proj_scr):
    c = wg_ref.shape[2]
    for j in range(N_DEV):
        proj_scr[:, j * c:(j + 1) * c] = _dot(hn, wg_ref[j])


def _inproj_attn(x, g, wg, tabs):
    s, d_model = x.shape
    gw = GROUP_WIDTH
    n = N_DEV * wg.shape[2]
    nz = n - 9 * gw - MEM_WIDTH
    reps = gw // LANES
    tm = ROW_TILE

    def body(x_ref, g_ref, w_ref, c_ref, sa_ref, sb_ref, hn_ref, hnt_ref, *rest):
        outs, (proj, scr, tscr) = rest[:-3], rest[-3:]
        q_refs, k_refs, v_refs, t_refs, qm_ref, z_ref = outs[0:3], outs[3:6], outs[6:9], outs[9:18], outs[18], outs[19]
        xb = x_ref[...]
        r = lax.rsqrt(jnp.mean(xb * xb, axis=-1, keepdims=True) + EPS)
        hn = ((xb * r) * g_ref[...]).astype(BF16)
        hn_ref[...] = hn
        hnt_ref[...] = hn.T
        _project(hn, w_ref, proj)
        tab = (c_ref[...], sa_ref[...], sb_ref[...])
        cv, sav, sbv = [jnp.tile(t, (1, reps)) for t in tab]
        for j, d in enumerate(DILATIONS):
            tq = _rope_fwd(proj[:, j * gw:(j + 1) * gw], cv, sav, sbv)
            _to_view(scr, tq * SCALE, q_refs[j], d)
            tk = _rope_fwd(proj[:, (3 + j) * gw:(4 + j) * gw], cv, sav, sbv)
            _to_view(scr, tk, k_refs[j], d)
            _to_view(scr, proj[:, (6 + j) * gw:(7 + j) * gw], v_refs[j], d)
            for i in range(3):
                _to_view(tscr, tab[i], t_refs[3 * j + i], d)
        qm_ref[...] = proj[:, 9 * gw:9 * gw + MEM_WIDTH].astype(BF16)
        z_ref[...] = proj[:, 9 * gw + MEM_WIDTH:]

    views = [_sds((s // d, d * gw), BF16) for d in DILATIONS]
    tviews = [_sds((s // d, d * LANES), F32) for d in DILATIONS for _ in range(3)]
    out_shape = ([_sds((s, d_model), BF16), _sds((d_model, s), BF16)] + views * 3 + tviews
                 + [_sds((s, MEM_WIDTH), BF16), _sds((s, nz), F32)])
    vspecs = [_view_rows(gw, d, tm) for d in DILATIONS]
    tspecs = [_view_rows(LANES, d, tm) for d in DILATIONS for _ in range(3)]
    out_specs = ([_rows(d_model, tm), pl.BlockSpec((d_model, tm), lambda i: (0, i))] + vspecs * 3 + tspecs
                 + [_rows(MEM_WIDTH, tm), _rows(nz, tm)])
    res = _pallas_call(
        body, name="inproj_attn", grid=(s // tm,), out_shape=out_shape,
        in_specs=[_rows(d_model, tm), _whole((1, d_model)), _resident(wg.shape)] + [_rows(LANES, tm)] * 3,
        out_specs=out_specs,
        scratch_shapes=[pltpu.VMEM((tm, n), F32), _view_scratch(gw, tm), _view_scratch(LANES, tm)],
        compiler_params=_params(1, 60),
    )(x, g, wg, *tabs)
    tabs_v = [res[11 + 3 * j:14 + 3 * j] for j in range(3)]
    return res[0], res[1], res[2:5], res[5:8], res[8:11], tabs_v, res[20], res[21]


def _band_mask(n_keys):
    qi = lax.broadcasted_iota(jnp.int32, (BLOCK, n_keys), 0)
    kj = lax.broadcasted_iota(jnp.int32, (BLOCK, n_keys), 1)
    if n_keys == BLOCK:
        return kj <= qi
    return jnp.logical_or(jnp.logical_and(kj < BLOCK, kj >= qi), jnp.logical_and(kj >= BLOCK, (kj - BLOCK) <= qi))


def _low_head_lanes():
    return lax.broadcasted_iota(jnp.int32, (1, LANES), 1) < HEAD_DIM


def _split_pair(t, low):
    zero = jnp.zeros_like(t)
    return jnp.where(low, t, zero), jnp.where(low, zero, t)


def _pair_specs(d, nb, w):
    if nb == 1:
        return pl.BlockSpec((BLOCK, 2 * w), lambda n: (0, n)), None
    half = nb // 2
    two = pl.BlockSpec((2 * BLOCK, w), lambda n: (n % half, n // half))
    before = pl.BlockSpec((BLOCK, w), lambda n: (jnp.maximum(2 * (n % half) - 1, 0), n // half))
    return two, before


def _head_tiles(w, col0):
    return ([slice(p * LANES, (p + 1) * LANES) for p in range(w // LANES)],
            [slice(col0 + p * LANES, col0 + (p + 1) * LANES) for p in range(w // LANES)])


def _attend_fwd(q_ref, o_ref, lse_ref, rows, col0, kk, vv):
    w = kk.shape[1]
    valid = _band_mask(kk.shape[0])
    low = _low_head_lanes()
    pairs, qcols = _head_tiles(w, col0)
    qs_ = [h for qc in qcols for h in _split_pair(q_ref[rows, qc], low)]
    k2s = [kk[:, pr] for pr in pairs for _ in range(2)]
    scs = [jnp.where(valid, _dot_nt(qh, k2), NEG) for qh, k2 in zip(qs_, k2s)]
    ms = [jnp.max(sc, axis=-1, keepdims=True) for sc in scs]
    ps = [jnp.exp(sc - m) for sc, m in zip(scs, ms)]
    ls = [jnp.sum(p, axis=-1, keepdims=True) for p in ps]
    pns = [(p * (1.0 / l)).astype(BF16) for p, l in zip(ps, ls)]
    for i, (pr, qc) in enumerate(zip(pairs, qcols)):
        v2 = vv[:, pr]
        a, b = 2 * i, 2 * i + 1
        o_ref[rows, qc] = jnp.where(low, _dot(pns[a], v2), _dot(pns[b], v2))
        lse_ref[rows, qc] = jnp.where(low, ms[a] + jnp.log(ls[a]), ms[b] + jnp.log(ls[b]))


TOP, BOTTOM = slice(0, BLOCK), slice(BLOCK, 2 * BLOCK)


def _attn_fwd(q, k, v, d, token):
    ln, dw = q.shape
    w = dw // d
    nb = ln // BLOCK
    two, before = _pair_specs(d, nb, w)

    def body_streams(_, q_ref, kc_ref, vc_ref, o_ref, lse_ref):
        for sb in range(2):
            cols = slice(sb * w, (sb + 1) * w)
            _attend_fwd(q_ref, o_ref, lse_ref, TOP, sb * w, kc_ref[:, cols], vc_ref[:, cols])

    def body_blocks(_, q_ref, kp_ref, kc_ref, vp_ref, vc_ref, o_ref, lse_ref):
        first = pl.program_id(0) % (nb // 2) == 0
        pl.when(first)(lambda: _attend_fwd(q_ref, o_ref, lse_ref, TOP, 0, kc_ref[TOP, :], vc_ref[TOP, :]))
        pl.when(jnp.logical_not(first))(lambda: _attend_fwd(
            q_ref, o_ref, lse_ref, TOP, 0, jnp.concatenate([kp_ref[...], kc_ref[TOP, :]], axis=0),
            jnp.concatenate([vp_ref[...], vc_ref[TOP, :]], axis=0)))
        _attend_fwd(q_ref, o_ref, lse_ref, BOTTOM, 0, kc_ref[...], vc_ref[...])

    if nb == 1:
        body, in_specs, args = body_streams, [ANY, two, two, two], (token, q, k, v)
    else:
        body, in_specs, args = body_blocks, [ANY, two, before, two, before, two], (token, q, k, k, v, v)
    return _pallas_call(
        body, name=f"attn_fwd_d{d}", grid=(d * nb // 2,), out_shape=[_sds((ln, dw), F32)] * 2,
        in_specs=in_specs, out_specs=[two, two], compiler_params=_params(1, 32),
    )(*args)


def _memkv_fwd(mem, g, w):
    n_layers = w.shape[0]

    def body(mem_ref, g_ref, w_ref, kv_ref):
        mb = mem_ref[...]
        r = lax.rsqrt(jnp.mean(mb * mb, axis=-1, keepdims=True) + EPS)
        mn = ((mb * r) * g_ref[...]).astype(BF16)
        kv_ref[...] = _dot(mn, w_ref[...]).astype(BF16)

    return _pallas_call(
        body, name="memkv_fwd", grid=(n_layers,),
        out_shape=_plain((n_layers, N_MEM, 2 * MEM_WIDTH), BF16),
        in_specs=[_whole(mem.shape), pl.BlockSpec((None, 1, D_MODEL), lambda l: (l, 0, 0)),
                  pl.BlockSpec((None, D_MODEL, 2 * MEM_WIDTH), lambda l: (l, 0, 0))],
        out_specs=pl.BlockSpec((None, N_MEM, 2 * MEM_WIDTH), lambda l: (l, 0, 0)),
        compiler_params=_params(1, 32),
    )(mem, g.reshape(n_layers, 1, D_MODEL), w)


def _mix_groups(os_, ls_):
    mx = jnp.maximum(jnp.maximum(ls_[0], ls_[1]), ls_[2])
    es = [jnp.exp(t - mx) for t in ls_]
    inv = 1.0 / (es[0] + es[1] + es[2])
    ws = [e * inv for e in es]
    mix = ws[0] * os_[0] + ws[1] * os_[1] + ws[2] * os_[2]
    return ws, mix


MEM_PAIRS = [slice(p * LANES, (p + 1) * LANES) for p in range(MEM_WIDTH // LANES)]


def _mem_probs(qhs, k2s):
    scs = [_dot_nt(qh, k2) * SCALE for qh, k2 in zip(qhs, k2s)]
    es = [jnp.exp(sc - jnp.max(sc, axis=-1, keepdims=True)) for sc in scs]
    return [e * (1.0 / jnp.sum(e, axis=-1, keepdims=True)) for e in es]


def _mem_attn_into(qm, kv_ref, mo_ref):
    low = _low_head_lanes()
    qhs = [h for pr in MEM_PAIRS for h in _split_pair(qm[:, pr], low)]
    k2s = [kv_ref[:, pr] for pr in MEM_PAIRS for _ in range(2)]
    ps = [p.astype(BF16) for p in _mem_probs(qhs, k2s)]
    for i, pr in enumerate(MEM_PAIRS):
        v2 = kv_ref[:, MEM_WIDTH + i * LANES:MEM_WIDTH + (i + 1) * LANES]
        mo_ref[:, pr] = jnp.where(low, _dot(ps[2 * i], v2), _dot(ps[2 * i + 1], v2))


def _mem_attn_bwd(qm, kv_ref, dmem, dqm_ref, dkv_ref):
    low = _low_head_lanes()
    dmb = dmem.astype(BF16)
    vps = [slice(MEM_WIDTH + i * LANES, MEM_WIDTH + (i + 1) * LANES) for i in range(len(MEM_PAIRS))]
    qhs = [h for pr in MEM_PAIRS for h in _split_pair(qm[:, pr], low)]
    dhs = [h for pr in MEM_PAIRS for h in _split_pair(dmb[:, pr], low)]
    k2s = [kv_ref[:, pr] for pr in MEM_PAIRS for _ in range(2)]
    v2s = [kv_ref[:, vp] for vp in vps for _ in range(2)]
    ps = _mem_probs(qhs, k2s)
    dps = [_dot_nt(dh, v2) for dh, v2 in zip(dhs, v2s)]
    dss = [(p * (dp - jnp.sum(dp * p, axis=-1, keepdims=True)) * SCALE).astype(BF16) for p, dp in zip(ps, dps)]
    pbs = [p.astype(BF16) for p in ps]
    for i, (pr, vp) in enumerate(zip(MEM_PAIRS, vps)):
        a, b = 2 * i, 2 * i + 1
        dqm_ref[:, pr] = jnp.where(low, _dot(dss[a], k2s[a]), _dot(dss[b], k2s[b])).astype(BF16)
        dkv_ref[:, pr] += _dot_tn(dss[a], qhs[a]) + _dot_tn(dss[b], qhs[b])
        dkv_ref[:, vp] += _dot_tn(pbs[a], dhs[a]) + _dot_tn(pbs[b], dhs[b])


def _post_attn(os_, ls_, qm, kv, z, x, wg_out):
    s, d_model = x.shape
    gw = GROUP_WIDTH
    nb = gw + MEM_WIDTH
    c = wg_out.shape[2]
    tm = ROW_TILE

    def body(o0, o1, o2, l0, l1, l2, qm_ref, kv_ref, z_ref, x_ref, w_ref, y_ref, yt_ref, mo_ref, h_ref, s0, s1):
        ov, lv = [], []
        for o_ref, l_ref, d in zip((o0, o1, o2), (l0, l1, l2), DILATIONS):
            ov.append(_from_view(s0, o_ref, d))
            lv.append(_from_view(s1, l_ref, d))
        _, mix = _mix_groups(ov, lv)
        _mem_attn_into(qm_ref[...], kv_ref, mo_ref)
        sz, _ = _silu_parts(z_ref[...])
        y_ref[:, :gw] = (mix * sz[:, :gw]).astype(BF16)
        y_ref[:, gw:] = (mo_ref[...] * sz[:, gw:]).astype(BF16)
        y = y_ref[...]
        yt_ref[...] = y.T
        for j in range(N_DEV):
            h_ref[:, j * c:(j + 1) * c] = x_ref[:, j * c:(j + 1) * c] + _dot(y, w_ref[j])

    vspecs = [_view_rows(gw, d) for d in DILATIONS]
    return _pallas_call(
        body, name="post_attn", grid=(s // tm,),
        out_shape=[_sds((s, nb), BF16), _sds((nb, s), BF16), _sds((s, MEM_WIDTH), F32), _sds((s, d_model), F32)],
        in_specs=vspecs * 2 + [_rows(MEM_WIDTH), _whole(kv.shape), _rows(nb), _rows(d_model), _whole(wg_out.shape)],
        out_specs=[_rows(nb), pl.BlockSpec((nb, tm), lambda i: (0, i)), _rows(MEM_WIDTH), _rows(d_model)],
        scratch_shapes=[_view_scratch(gw), _view_scratch(gw)],
        compiler_params=_params(1, 40),
    )(*os_, *ls_, qm, kv, z, x, wg_out)


def _inproj_conv(x, g, wg):
    s, d_model = x.shape
    c = CONV_WIDTH
    n = N_DEV * wg.shape[2]
    nz = n - 3 * c - MEM_WIDTH
    tm = ROW_TILE

    def body(x_ref, g_ref, w_ref, hn_ref, hnt_ref, bg_ref, cg_ref, u_ref, qm_ref, z_ref, proj):
        xb = x_ref[...]
        r = lax.rsqrt(jnp.mean(xb * xb, axis=-1, keepdims=True) + EPS)
        hn = ((xb * r) * g_ref[...]).astype(BF16)
        hn_ref[...] = hn
        hnt_ref[...] = hn.T
        _project(hn, w_ref, proj)
        bg_ref[...] = proj[:, 0:c]
        cg_ref[...] = proj[:, c:2 * c]
        u_ref[...] = proj[:, 2 * c:3 * c]
        qm_ref[...] = proj[:, 3 * c:3 * c + MEM_WIDTH].astype(BF16)
        z_ref[...] = proj[:, 3 * c + MEM_WIDTH:]

    return _pallas_call(
        body, name="inproj_conv", grid=(s // tm,),
        out_shape=[_sds((s, d_model), BF16), _sds((d_model, s), BF16)] + [_sds((s, c), F32)] * 3
                  + [_sds((s, MEM_WIDTH), BF16), _sds((s, nz), F32)],
        in_specs=[_rows(d_model), _whole((1, d_model)), _whole(wg.shape)],
        out_specs=[_rows(d_model), pl.BlockSpec((d_model, tm), lambda i: (0, i))] + [_rows(c)] * 3
                  + [_rows(MEM_WIDTH), _rows(nz)],
        scratch_shapes=[pltpu.VMEM((tm, n), F32)],
        compiler_params=_params(1, 60),
    )(x, g, wg)


HALO = 8


def _halo_before(width, tm=ROW_TILE):
    return pl.BlockSpec((HALO, width), lambda i: (jnp.maximum(i * (tm // HALO) - 1, 0), 0))


def _halo_after(width, n_rows, tm=ROW_TILE):
    return pl.BlockSpec((HALO, width), lambda i: (jnp.minimum((i + 1) * (tm // HALO), n_rows // HALO - 1), 0))


def _conv_taps(cg_ref, u_ref, cgh_ref, uh_ref, i):
    a = cg_ref[...] * u_ref[...]
    ah = jnp.where(i > 0, cgh_ref[...] * uh_ref[...], 0.0)
    row = lax.broadcasted_iota(jnp.int32, a.shape, 0)
    a1 = jnp.where(row == 0, ah[HALO - 1:HALO], pltpu.roll(a, 1, 0))
    a2 = jnp.where(row == 0, ah[HALO - 2:HALO - 1], jnp.where(row == 1, ah[HALO - 1:HALO], pltpu.roll(a, 2, 0)))
    return a, a1, a2


def _post_conv_loss(bg, cg, u, qm, kv, z, h1, w_out, cw, gf, tgt):
    s, d = h1.shape
    c = CONV_WIDTH
    nb = c + MEM_WIDTH
    tm = ROW_TILE

    def body(bg_ref, cg_ref, u_ref, cgh_ref, uh_ref, qm_ref, kv_ref, z_ref, h_ref, w_ref, cw_ref, gf_ref, t_ref,
             y_ref, yt_ref, mo_ref, dh_ref, dhb_ref, loss_ref, dgf_ref):
        i = pl.program_id(0)
        a, a1, a2 = _conv_taps(cg_ref, u_ref, cgh_ref, uh_ref, i)
        conv = cw_ref[0:1, :] * a2 + cw_ref[1:2, :] * a1 + cw_ref[2:3, :] * a
        mix = bg_ref[...] * conv
        _mem_attn_into(qm_ref[...], kv_ref, mo_ref)
        sz, _ = _silu_parts(z_ref[...])
        y_ref[:, :c] = (mix * sz[:, :c]).astype(BF16)
        y_ref[:, c:] = (mo_ref[...] * sz[:, c:]).astype(BF16)
        y = y_ref[...]
        yt_ref[...] = y.T
        h2 = h_ref[...] + _dot(y, w_ref[...])
        r = lax.rsqrt(jnp.mean(h2 * h2, axis=-1, keepdims=True) + EPS)
        nh = h2 * r
        gfv = gf_ref[...]
        diff = nh * gfv - t_ref[...]
        dout = diff * (1.0 / d)
        dn = dout * gfv
        dh2 = r * dn - h2 * ((r * r * r) * jnp.mean(dn * h2, axis=-1, keepdims=True))
        dh_ref[...] = dh2
        dhb_ref[...] = dh2.astype(BF16)

        @pl.when(i == 0)
        def _():
            loss_ref[...] = jnp.zeros_like(loss_ref)
            dgf_ref[...] = jnp.zeros_like(dgf_ref)

        loss_ref[...] += 0.5 * jnp.sum(jnp.mean(diff * diff, axis=-1, keepdims=True))
        dgf_ref[...] += jnp.sum(dout * nh, axis=0, keepdims=True)

    return _pallas_call(
        body, name="post_conv_loss", grid=(s // tm,),
        out_shape=[_sds((s, nb), BF16), _sds((nb, s), BF16), _sds((s, MEM_WIDTH), F32), _sds((s, d), F32),
                   _sds((s, d), BF16), _plain((8, LANES), F32), _plain((1, d), F32)],
        in_specs=[_rows(c)] * 3 + [_halo_before(c)] * 2 + [_rows(MEM_WIDTH), _whole(kv.shape), _rows(nb), _rows(d),
                  _whole(w_out.shape), _whole(cw.shape), _whole((1, d)), _rows(d)],
        out_specs=[_rows(nb), pl.BlockSpec((nb, tm), lambda i: (0, i)), _rows(MEM_WIDTH), _rows(d), _rows(d),
                   _whole((8, LANES)), _whole((1, d))],
        compiler_params=_params(1, 48),
    )(bg, cg, u, cg, u, qm, kv, z, h1, w_out, cw, gf, tgt)


def _bwd_post_conv(dhb, w_out, bg, cg, u, z, qm, kv, mo, cw):
    s = dhb.shape[0]
    c = CONV_WIDTH
    nb = c + MEM_WIDTH

    def body(dh_ref, w_ref, bg_ref, cg_ref, u_ref, cgh_ref, uh_ref, z_ref, qm_ref, kv_ref, mo_ref, cw_ref,
             dz_ref, dbg_ref, dc_ref, dqm_ref, dkv_ref):
        i = pl.program_id(0)

        @pl.when(i == 0)
        def _():
            dkv_ref[...] = jnp.zeros_like(dkv_ref)

        dy = _dot_nt(dh_ref[...], w_ref[...])
        sz, dsz = _silu_parts(z_ref[...])
        a, a1, a2 = _conv_taps(cg_ref, u_ref, cgh_ref, uh_ref, i)
        conv = cw_ref[0:1, :] * a2 + cw_ref[1:2, :] * a1 + cw_ref[2:3, :] * a
        bgv = bg_ref[...]
        dz_ref[:, :c] = (dy[:, :c] * (bgv * conv) * dsz[:, :c]).astype(BF16)
        dz_ref[:, c:] = (dy[:, c:] * mo_ref[...] * dsz[:, c:]).astype(BF16)
        dbr = dy * sz
        dmix = dbr[:, :c]
        dbg_ref[...] = (dmix * conv).astype(BF16)
        dc_ref[...] = dmix * bgv
        _mem_attn_bwd(qm_ref[...], kv_ref, dbr[:, c:], dqm_ref, dkv_ref)

    return _pallas_call(
        body, name="bwd_post_conv", grid=(s // ROW_TILE,),
        out_shape=[_sds((s, nb), BF16), _sds((s, c), BF16), _sds((s, c), F32), _sds((s, MEM_WIDTH), BF16),
                   _plain(kv.shape, F32)],
        in_specs=[_rows(D_MODEL), _whole(w_out.shape)] + [_rows(c)] * 3 + [_halo_before(c)] * 2
                 + [_rows(nb), _rows(MEM_WIDTH), _whole(kv.shape), _rows(MEM_WIDTH), _whole(cw.shape)],
        out_specs=[_rows(nb), _rows(c), _rows(c), _rows(MEM_WIDTH), _whole(kv.shape)],
        compiler_params=_params(1, 48),
    )(dhb, w_out, bg, cg, u, cg, u, z, qm, kv, mo, cw)


def _bwd_conv(dconv, cg, u, cw):
    s, c = dconv.shape
    tm = ROW_TILE
    last = s // tm - 1

    def body(dc_ref, dcn_ref, cg_ref, u_ref, cgh_ref, uh_ref, cw_ref, dcg_ref, du_ref, dcw_ref):
        i = pl.program_id(0)

        @pl.when(i == 0)
        def _():
            dcw_ref[...] = jnp.zeros_like(dcw_ref)

        dc = dc_ref[...]
        dcn = jnp.where(i < last, dcn_ref[...], 0.0)
        row = lax.broadcasted_iota(jnp.int32, dc.shape, 0)
        d1 = jnp.where(row == tm - 1, dcn[0:1], pltpu.roll(dc, tm - 1, 0))
        d2 = jnp.where(row == tm - 1, dcn[1:2], jnp.where(row == tm - 2, dcn[0:1], pltpu.roll(dc, tm - 2, 0)))
        da = cw_ref[2:3, :] * dc + cw_ref[1:2, :] * d1 + cw_ref[0:1, :] * d2
        a, a1, a2 = _conv_taps(cg_ref, u_ref, cgh_ref, uh_ref, i)
        dcg_ref[...] = (da * u_ref[...]).astype(BF16)
        du_ref[...] = (da * cg_ref[...]).astype(BF16)
        dcw_ref[0:1, :] += jnp.sum(dc * a2, axis=0, keepdims=True)
        dcw_ref[1:2, :] += jnp.sum(dc * a1, axis=0, keepdims=True)
        dcw_ref[2:3, :] += jnp.sum(dc * a, axis=0, keepdims=True)

    return _pallas_call(
        body, name="bwd_conv", grid=(s // tm,),
        out_shape=[_sds((s, c), BF16), _sds((s, c), BF16), _plain((8, c), F32)],
        in_specs=[_rows(c), _halo_after(c, s), _rows(c), _rows(c), _halo_before(c), _halo_before(c), _whole(cw.shape)],
        out_specs=[_rows(c), _rows(c), _whole((8, c))], compiler_params=_params(1, 40),
    )(dconv, dconv, cg, u, cg, u, cw)


def _dgrad_norm(pieces, wg, h, g, dres, name):
    s, d_model = h.shape
    c = wg.shape[2]
    n = N_DEV * c
    tm = ROW_TILE
    widths = [p.shape[1] // d for p, d in pieces]
    assert sum(widths) == n
    n_p = len(pieces)

    def body(*refs):
        p_refs = refs[:n_p]
        w_ref, h_ref, g_ref, dr_ref, dh_ref, dhb_ref, dg_ref, dpd_ref, dp, scr = refs[n_p:]

        @pl.when(pl.program_id(0) == 0)
        def _():
            dg_ref[...] = jnp.zeros_like(dg_ref)

        off = 0
        for p_ref, (_, d), wd in zip(p_refs, pieces, widths):
            if d == 1:
                dp[:, off:off + wd] = p_ref[...]
            else:
                dp[:, off:off + wd] = _from_view(scr, p_ref, d).astype(BF16)
            off += wd
        dhn = jnp.zeros((tm, d_model), F32)
        for j in range(N_DEV):
            dpj = dp[:, j * c:(j + 1) * c]
            dpd_ref[j] = dpj
            dhn += _dot_nt(dpj, w_ref[j])
        hb = h_ref[...]
        r = lax.rsqrt(jnp.mean(hb * hb, axis=-1, keepdims=True) + EPS)
        dg_ref[...] += jnp.sum(dhn * (hb * r), axis=0, keepdims=True)
        dn = dhn * g_ref[...]
        dh = dr_ref[...] + r * dn - hb * ((r * r * r) * jnp.mean(dn * hb, axis=-1, keepdims=True))
        dh_ref[...] = dh
        dhb_ref[...] = dh.astype(BF16)

    p_specs = [_view_rows(wd, d) for (_, d), wd in zip(pieces, widths)]
    return _pallas_call(
        body, name=name, grid=(s // tm,),
        out_shape=[_plain((s, d_model), F32), _sds((s, d_model), BF16), _plain((1, d_model), F32), _sds((N_DEV, s, c), BF16)],
        in_specs=p_specs + [_whole(wg.shape), _rows(d_model), _whole((1, d_model)), _rows(d_model)],
        out_specs=[_rows(d_model), _rows(d_model), _whole((1, d_model)), pl.BlockSpec((N_DEV, tm, c), lambda i: (0, i, 0))],
        scratch_shapes=[pltpu.VMEM((tm, n), BF16), _view_scratch(GROUP_WIDTH)],
        compiler_params=_params(1, 60),
    )(*[p for p, _ in pieces], wg, h, g, dres)


def _assemble_dproj(pieces, c, name):
    n = N_DEV * c
    tm = ROW_TILE
    widths = [p.shape[1] // d for p, d in pieces]
    assert sum(widths) == n
    s = pieces[0][0].shape[0] * pieces[0][1]
    n_p = len(pieces)

    def body(*refs):
        p_refs, (dpd_ref, dp, scr) = refs[:n_p], refs[n_p:]
        off = 0
        for p_ref, (_, d), wd in zip(p_refs, pieces, widths):
            if d == 1:
                dp[:, off:off + wd] = p_ref[...]
            else:
                dp[:, off:off + wd] = _from_view(scr, p_ref, d).astype(BF16)
            off += wd
        for j in range(N_DEV):
            dpd_ref[j] = dp[:, j * c:(j + 1) * c]

    return _pallas_call(
        body, name=name, grid=(s // tm,), out_shape=_sds((N_DEV, s, c), BF16),
        in_specs=[_view_rows(wd, d) for (_, d), wd in zip(pieces, widths)],
        out_specs=pl.BlockSpec((N_DEV, tm, c), lambda i: (0, i, 0)),
        scratch_shapes=[pltpu.VMEM((tm, n), BF16), _view_scratch(GROUP_WIDTH)],
        compiler_params=_params(1, 40),
    )(*[p for p, _ in pieces])


def _dgrad_norm_dm(dproj_dm, wg, h, g, dres, token, name):
    s, d_model = h.shape
    c = wg.shape[2]
    tm = ROW_TILE

    def body(_, dp_ref, w_ref, h_ref, g_ref, dr_ref, dh_ref, dg_ref):
        @pl.when(pl.program_id(0) == 0)
        def _():
            dg_ref[...] = jnp.zeros_like(dg_ref)

        dhn = jnp.zeros((tm, d_model), F32)
        for j in range(N_DEV):
            dhn += _dot_nt(dp_ref[j], w_ref[j])
        hb = h_ref[...]
        r = lax.rsqrt(jnp.mean(hb * hb, axis=-1, keepdims=True) + EPS)
        dg_ref[...] += jnp.sum(dhn * (hb * r), axis=0, keepdims=True)
        dn = dhn * g_ref[...]
        dh_ref[...] = dr_ref[...] + r * dn - hb * ((r * r * r) * jnp.mean(dn * hb, axis=-1, keepdims=True))

    return _pallas_call(
        body, name=name, grid=(s // tm,),
        out_shape=[_plain((s, d_model), F32), _plain((1, d_model), F32)],
        in_specs=[ANY, pl.BlockSpec((N_DEV, tm, c), lambda i: (0, i, 0)), _whole(wg.shape), _rows(d_model),
                  _whole((1, d_model)), _rows(d_model)],
        out_specs=[_rows(d_model), _whole((1, d_model))],
        compiler_params=_params(1, 60),
    )(token, dproj_dm, wg, h, g, dres)


def _wgrad_shards(a_t, b_dm, name):
    m, s = a_t.shape
    c = b_dm.shape[2]

    def body(a_ref, b_ref, o_ref):
        o_ref[...] = _dot(a_ref[...], b_ref[...]).astype(BF16)

    return _pallas_call(
        body, name=name, grid=(N_DEV,), out_shape=_sds((N_DEV, m, c), BF16),
        in_specs=[_whole(a_t.shape), pl.BlockSpec((None, s, c), lambda j: (j, 0, 0))],
        out_specs=pl.BlockSpec((None, m, c), lambda j: (j, 0, 0)), compiler_params=_params(1, 40),
    )(a_t, b_dm)


def _wgrad_cols(a_t, b, c, name):
    m, s = a_t.shape

    def body(a_ref, b_ref, o_ref):
        o_ref[...] = _dot(a_ref[...], b_ref[...]).astype(BF16)

    return _pallas_call(
        body, name=name, grid=(N_DEV,), out_shape=_sds((N_DEV, m, c), BF16),
        in_specs=[_whole(a_t.shape), pl.BlockSpec((s, c), lambda j: (0, j))],
        out_specs=pl.BlockSpec((None, m, c), lambda j: (j, 0, 0)), compiler_params=_params(1, 40),
    )(a_t, b)


def _wgrad_rows(a_t, b, name):
    m, s = a_t.shape
    n = b.shape[1]
    mr = m // N_DEV

    def body(a_ref, b_ref, o_ref):
        o_ref[...] = _dot(a_ref[...], b_ref[...]).astype(BF16)

    return _pallas_call(
        body, name=name, grid=(N_DEV,), out_shape=_sds((N_DEV, mr, n), BF16),
        in_specs=[pl.BlockSpec((mr, s), lambda j: (j, 0)), _whole(b.shape)],
        out_specs=pl.BlockSpec((None, mr, n), lambda j: (j, 0, 0)), compiler_params=_params(1, 40),
    )(a_t, b)


def _memkv_bwd(dkv, w, mem, g):
    n_layers = w.shape[0]

    def body(dkv_ref, w_ref, mem_ref, g_ref, dw_ref, dg_ref):
        mb = mem_ref[...]
        r = lax.rsqrt(jnp.mean(mb * mb, axis=-1, keepdims=True) + EPS)
        nm = mb * r
        mn = (nm * g_ref[...]).astype(BF16)
        dkvb = dkv_ref[...].astype(BF16)
        dw_ref[...] = _dot_tn(mn, dkvb).astype(BF16)
        dmn = _dot_nt(dkvb, w_ref[...])
        dg_ref[...] = jnp.sum(dmn * nm, axis=0, keepdims=True)

    lay = lambda *shape: pl.BlockSpec((None,) + shape, lambda l: (l, 0, 0))
    return _pallas_call(
        body, name="memkv_bwd", grid=(n_layers,),
        out_shape=[_plain((n_layers, D_MODEL, 2 * MEM_WIDTH), BF16), _plain((n_layers, 1, D_MODEL), F32)],
        in_specs=[lay(N_MEM, 2 * MEM_WIDTH), lay(D_MODEL, 2 * MEM_WIDTH), _whole(mem.shape), lay(1, D_MODEL)],
        out_specs=[lay(D_MODEL, 2 * MEM_WIDTH), lay(1, D_MODEL)], compiler_params=_params(1, 32),
    )(dkv, w, mem, g.reshape(n_layers, 1, D_MODEL))


def _bwd_post_attn(dhb, wg_out, z, os_, ls_, qm, kv, mo, head_ones):
    s = dhb.shape[0]
    gw = GROUP_WIDTH
    nb = gw + MEM_WIDTH
    c = wg_out.shape[2]
    tm = ROW_TILE

    def body(dh_ref, w_ref, z_ref, o0, o1, o2, l0, l1, l2, qm_ref, kv_ref, mo_ref, bd_ref,
             dz_ref, do0, do1, do2, dl0, dl1, dl2, dqm_ref, dkv_ref, s0, s1):
        @pl.when(pl.program_id(0) == 0)
        def _():
            dkv_ref[...] = jnp.zeros_like(dkv_ref)

        dy = jnp.zeros((tm, nb), F32)
        for j in range(N_DEV):
            dy += _dot_nt(dh_ref[:, j * c:(j + 1) * c], w_ref[j])
        ov, lv = [], []
        for o_ref, l_ref, d in zip((o0, o1, o2), (l0, l1, l2), DILATIONS):
            ov.append(_from_view(s0, o_ref, d))
            lv.append(_from_view(s1, l_ref, d))
        ws, mix = _mix_groups(ov, lv)
        sz, dsz = _silu_parts(z_ref[...])
        dz_ref[:, :gw] = (dy[:, :gw] * mix * dsz[:, :gw]).astype(BF16)
        dz_ref[:, gw:] = (dy[:, gw:] * mo_ref[...] * dsz[:, gw:]).astype(BF16)
        dbr = dy * sz
        dmix = dbr[:, :gw]
        t = dmix * mix
        th = t.astype(BF16)
        tl = (t - th.astype(F32)).astype(BF16)
        rs = _dot(th, bd_ref[...]) + _dot(tl, bd_ref[...])
        for wg_, do_ref, dl_ref, d in zip(ws, (do0, do1, do2), (dl0, dl1, dl2), DILATIONS):
            _to_view(s0, wg_ * dmix, do_ref, d)
            _to_view(s1, wg_ * rs, dl_ref, d)
        _mem_attn_bwd(qm_ref[...], kv_ref, dbr[:, gw:], dqm_ref, dkv_ref)

    vspecs = [_view_rows(gw, d) for d in DILATIONS]
    return _pallas_call(
        body, name="bwd_post_attn", grid=(s // tm,),
        out_shape=[_sds((s, nb), BF16)] + [_sds((s // d, d * gw), BF16) for d in DILATIONS]
                  + [_sds((s // d, d * gw), F32) for d in DILATIONS] + [_sds((s, MEM_WIDTH), BF16), _plain(kv.shape, F32)],
        in_specs=[_rows(D_MODEL), _whole(wg_out.shape), _rows(nb)] + vspecs * 2
                 + [_rows(MEM_WIDTH), _whole(kv.shape), _rows(MEM_WIDTH), _whole(head_ones.shape)],
        out_specs=[_rows(nb)] + vspecs * 2 + [_rows(MEM_WIDTH), _whole(kv.shape)],
        scratch_shapes=[_view_scratch(gw), _view_scratch(gw)],
        compiler_params=_params(1, 48),
    )(dhb, wg_out, z, *os_, *ls_, qm, kv, mo, head_ones)


def _attn_bwd(q, k, v, lse, do, dl, tabs, d, token):
    ln, dw = q.shape
    w = dw // d
    nb = ln // BLOCK
    reps = w // LANES
    two, before = _pair_specs(d, nb, w)
    two_t, _ = _pair_specs(d, nb, LANES)

    def attend(q_ref, l_ref, do_ref, dl_ref, dqs, acck, accv, rows, col0, kk, vv, acc_rows):
        valid = _band_mask(kk.shape[0])
        low = _low_head_lanes()
        pairs, qcols = _head_tiles(w, col0)
        cols = [slice(col0 + h * HEAD_DIM, col0 + h * HEAD_DIM + 1) for h in range(HEADS_PER_GROUP)]
        qhs = [h for qc in qcols for h in _split_pair(q_ref[rows, qc], low)]
        dobs = [h for qc in qcols for h in _split_pair(do_ref[rows, qc], low)]
        k2s = [kk[:, pr] for pr in pairs for _ in range(2)]
        v2s = [vv[:, pr] for pr in pairs for _ in range(2)]
        scs = [jnp.where(valid, _dot_nt(qh, k2), NEG) for qh, k2 in zip(qhs, k2s)]
        dps = [_dot_nt(dob, v2) for dob, v2 in zip(dobs, v2s)]
        ps = [jnp.exp(sc - l_ref[rows, col]) for sc, col in zip(scs, cols)]
        dss = [(p * (dp - dl_ref[rows, col])).astype(BF16) for p, dp, col in zip(ps, dps, cols)]
        pbs = [p.astype(BF16) for p in ps]
        for i, qc in enumerate(qcols):
            a, b = 2 * i, 2 * i + 1
            dqs[rows, qc] = jnp.where(low, _dot(dss[a], k2s[a]), _dot(dss[b], k2s[b])) * SCALE
            acck[acc_rows, qc] += _dot_tn(dss[a], qhs[a]) + _dot_tn(dss[b], qhs[b])
            accv[acc_rows, qc] += _dot_tn(pbs[a], dobs[a]) + _dot_tn(pbs[b], dobs[b])

    def body_streams(_, q_ref, kc_ref, vc_ref, l_ref, do_ref, dl_ref, c_ref, sa_ref, sb_ref,
                     dq_ref, dk_ref, dv_ref, acck, accv, dqs):
        acck[...] = jnp.zeros_like(acck)
        accv[...] = jnp.zeros_like(accv)
        for sb in range(2):
            cols = slice(sb * w, (sb + 1) * w)
            attend(q_ref, l_ref, do_ref, dl_ref, dqs, acck, accv, TOP, sb * w, kc_ref[:, cols], vc_ref[:, cols], TOP)
        tabs2 = [jnp.concatenate([jnp.tile(r[:, sb * LANES:(sb + 1) * LANES], (1, reps)) for sb in range(2)], axis=1)
                 for r in (c_ref, sa_ref, sb_ref)]
        dq_ref[...] = _rope_bwd(dqs[...], *tabs2).astype(BF16)
        dk_ref[...] = _rope_bwd(acck[...], *tabs2).astype(BF16)
        dv_ref[...] = accv[...].astype(BF16)

    def body_blocks(_, q_ref, kp_ref, kc_ref, vp_ref, vc_ref, l_ref, do_ref, dl_ref, cq, saq, sbq, ck, sak, sbk,
                    dq_ref, dk_ref, dv_ref, acck, accv, dqs):
        i = pl.program_id(0) % (nb // 2)

        @pl.when(i == 0)
        def _():
            acck[...] = jnp.zeros_like(acck)
            accv[...] = jnp.zeros_like(accv)

        refs = (q_ref, l_ref, do_ref, dl_ref, dqs, acck, accv)
        pl.when(i == 0)(lambda: attend(*refs, TOP, 0, kc_ref[TOP, :], vc_ref[TOP, :], TOP))
        pl.when(i != 0)(lambda: attend(
            *refs, TOP, 0, jnp.concatenate([kp_ref[...], kc_ref[TOP, :]], axis=0),
            jnp.concatenate([vp_ref[...], vc_ref[TOP, :]], axis=0),
            pl.ds(pl.multiple_of((2 * i - 1) * BLOCK, BLOCK), 2 * BLOCK)))
        attend(*refs, BOTTOM, 0, kc_ref[...], vc_ref[...], pl.ds(pl.multiple_of(2 * i * BLOCK, BLOCK), 2 * BLOCK))
        tq = [jnp.tile(r[...], (1, reps)) for r in (cq, saq, sbq)]
        dq_ref[...] = _rope_bwd(dqs[...], *tq).astype(BF16)

        @pl.when(i == nb // 2 - 1)
        def _():
            for r0 in range(0, nb * BLOCK, 2 * BLOCK):
                rows = slice(r0, r0 + 2 * BLOCK)
                tk = [jnp.tile(r[rows, :], (1, reps)) for r in (ck, sak, sbk)]
                dk_ref[rows, :] = _rope_bwd(acck[rows, :], *tk).astype(BF16)
                dv_ref[rows, :] = accv[rows, :].astype(BF16)

    if nb == 1:
        body = body_streams
        in_specs = [ANY] + [two] * 6 + [two_t] * 3
        args = (token, q, k, v, lse, do, dl, *tabs)
        out_specs = [two, two, two]
        acc_shape = (BLOCK, 2 * w)
    else:
        body = body_blocks
        stream = pl.BlockSpec((nb * BLOCK, w), lambda n: (0, n // (nb // 2)))
        stream_t = pl.BlockSpec((nb * BLOCK, LANES), lambda n: (0, n // (nb // 2)))
        in_specs = [ANY, two, before, two, before, two, two, two, two] + [two_t] * 3 + [stream_t] * 3
        args = (token, q, k, k, v, v, lse, do, dl, *tabs, *tabs)
        out_specs = [two, stream, stream]
        acc_shape = (nb * BLOCK, w)
    return _pallas_call(
        body, name=f"attn_bwd_d{d}", grid=(d * nb // 2,), out_shape=[_sds((ln, dw), BF16)] * 3,
        in_specs=in_specs, out_specs=out_specs,
        scratch_shapes=[pltpu.VMEM(acc_shape, F32), pltpu.VMEM(acc_shape, F32), pltpu.VMEM(two.block_shape, F32)],
        compiler_params=_params(1, 48),
    )(*args)


def _position():
    return lax.axis_index("x"), lax.axis_index("y"), lax.axis_index("c")


def _all_gather(shards, after, name):
    n_a = len(shards)

    def body(*refs):
        x_refs, out_refs = refs[:n_a], refs[n_a + 1:2 * n_a + 1]
        send_sems, recv_sems, local_sems = refs[2 * n_a + 1:]
        x, y, c = _position()
        me, sibling = (x, y, c), (x, y, 1 - c)
        chips = [(1 - x, y), (x, 1 - y), (1 - x, 1 - y)]

        def rows(a, px, py, pc):
            return out_refs[a].at[4 * px + 2 * py + pc]

        def copy(a, k, block, to, own=False):
            return pltpu.make_async_remote_copy(
                src_ref=x_refs[a] if own else rows(a, *block), dst_ref=rows(a, *block),
                send_sem=send_sems.at[a, k], recv_sem=recv_sems.at[a, k], device_id=to, device_id_type=MESH)

        mine = [pltpu.make_async_copy(x_refs[a], rows(a, *me), local_sems.at[a]) for a in range(n_a)]
        for cp in mine:
            cp.start()
        first = []
        for j, chip in enumerate(chips):
            first += [copy(a, 1 + j, me, (*chip, c), own=True) for a in range(n_a)]
        first += [copy(a, 0, me, sibling, own=True) for a in range(n_a)]
        for cp in first:
            cp.start()
        passed = []
        for j, chip in enumerate(chips):
            for a in range(n_a):
                copy(a, 1 + j, (*chip, c), me).wait_recv()
                fwd = copy(a, 4 + j, (*chip, c), sibling)
                fwd.start()
                passed.append(fwd)
        for a in range(n_a):
            copy(a, 0, sibling, me).wait_recv()
        for j, chip in enumerate(chips):
            for a in range(n_a):
                copy(a, 4 + j, (*chip, 1 - c), me).wait_recv()
        for cp in first + passed:
            cp.wait_send()
        for cp in mine:
            cp.wait()

    return _pallas_call(
        body, name=name, out_shape=[_sds((N_DEV,) + t.shape, t.dtype) for t in shards],
        in_specs=[ANY] * (n_a + 1), out_specs=[ANY] * n_a,
        scratch_shapes=[pltpu.SemaphoreType.DMA((n_a, 7)), pltpu.SemaphoreType.DMA((n_a, 7)),
                        pltpu.SemaphoreType.DMA((n_a,))],
    )(*shards, after)


def _all_gather_relay(xs, name):
    def body(x_ref, out_ref, send_sems, recv_sems, local_sem):
        x, y, c = _position()
        me, sibling = (x, y, c), (x, y, 1 - c)
        xn, yn, diag = (1 - x, y, c), (x, 1 - y, c), (1 - x, 1 - y, c)
        src_nb = (x + c * (1 - 2 * x), y + (1 - c) * (1 - 2 * y), c)
        dst_nb = (x + (1 - c) * (1 - 2 * x), y + c * (1 - 2 * y), c)

        def rows(dev):
            return out_ref.at[4 * dev[0] + 2 * dev[1] + dev[2]]

        def copy(k, block, to, own=False):
            return pltpu.make_async_remote_copy(
                src_ref=x_ref if own else rows(block), dst_ref=rows(block),
                send_sem=send_sems.at[k], recv_sem=recv_sems.at[k], device_id=to, device_id_type=MESH)

        mine = pltpu.make_async_copy(x_ref, rows(me), local_sem)
        mine.start()
        first = [copy(1, me, xn, own=True), copy(2, me, yn, own=True), copy(0, me, sibling, own=True)]
        for cp in first:
            cp.start()
        copy(1, xn, me).wait_recv()
        copy(2, yn, me).wait_recv()
        relay = copy(3, src_nb, dst_nb)
        relay.start()
        passed = [copy(4, xn, sibling), copy(5, yn, sibling)]
        for cp in passed:
            cp.start()
        copy(3, diag, me).wait_recv()
        last = copy(6, diag, sibling)
        last.start()
        copy(0, sibling, me).wait_recv()
        for k, blk in ((4, (1 - x, y, 1 - c)), (5, (x, 1 - y, 1 - c)), (6, (1 - x, 1 - y, 1 - c))):
            copy(k, blk, me).wait_recv()
        for cp in first + [relay] + passed + [last]:
            cp.wait_send()
        mine.wait()

    return _pallas_call(
        body, name=name, out_shape=_sds((N_DEV,) + xs.shape, xs.dtype),
        in_specs=[ANY], out_specs=ANY,
        scratch_shapes=[pltpu.SemaphoreType.DMA((7,)), pltpu.SemaphoreType.DMA((7,)), pltpu.SemaphoreType.DMA],
    )(xs)


def _rs_to_sibling(gs):
    n_a = len(gs)

    def body(*refs):
        g_refs, recv_refs = refs[:n_a], refs[n_a:2 * n_a]
        send_sems, recv_sems = refs[2 * n_a:]
        x, y, c = _position()
        copies = []
        for k in range(4):
            for a in range(n_a):
                copies.append(pltpu.make_async_remote_copy(
                    src_ref=g_refs[a].at[2 * k + (1 - c)], dst_ref=recv_refs[a].at[k],
                    send_sem=send_sems.at[a, k], recv_sem=recv_sems.at[a, k],
                    device_id=(x, y, 1 - c), device_id_type=MESH))
        for cp in copies:
            cp.start()
        for cp in copies:
            cp.wait()

    return _pallas_call(
        body, name="rs_to_sibling", out_shape=[_sds((4,) + g.shape[1:], g.dtype) for g in gs],
        in_specs=[ANY] * n_a, out_specs=[ANY] * n_a,
        scratch_shapes=[pltpu.SemaphoreType.DMA((n_a, 4)), pltpu.SemaphoreType.DMA((n_a, 4))],
    )(*gs)


def _rs_to_chips(pbs):
    n_a = len(pbs)

    def body(*refs):
        p_refs, recv_refs = refs[:n_a], refs[n_a:2 * n_a]
        send_sems, recv_sems = refs[2 * n_a:]
        x, y, c = _position()
        chips = [(1 - x, y), (x, 1 - y), (1 - x, 1 - y)]
        copies = []
        for j, (px, py) in enumerate(chips):
            for a in range(n_a):
                copies.append(pltpu.make_async_remote_copy(
                    src_ref=p_refs[a].at[2 * px + py], dst_ref=recv_refs[a].at[j],
                    send_sem=send_sems.at[a, j], recv_sem=recv_sems.at[a, j],
                    device_id=(px, py, c), device_id_type=MESH))
        for cp in copies:
            cp.start()
        for cp in copies:
            cp.wait()

    return _pallas_call(
        body, name="rs_to_chips", out_shape=[_sds((3,) + p.shape[1:], p.dtype) for p in pbs],
        in_specs=[ANY] * n_a, out_specs=[ANY] * n_a,
        scratch_shapes=[pltpu.SemaphoreType.DMA((n_a, 3)), pltpu.SemaphoreType.DMA((n_a, 3))],
    )(*pbs)


HBM_SPEC = pl.BlockSpec(memory_space=pltpu.HBM)
SEM_SPEC = pl.BlockSpec(memory_space=pltpu.SEMAPHORE)
EFFECT = pltpu.SideEffectType.DATAFLOW_SIDE_EFFECTING
def _plan_gather_own(src_refs, land_refs):
    x, y, c = _position()
    me = 4 * x + 2 * y + c
    peers = [(x, y, 1 - c), (1 - x, y, c), (x, 1 - y, c), (1 - x, 1 - y, c)]
    return [(src_refs[a], land_refs[a].at[me], (a, k), peer) for k, peer in enumerate(peers) for a in range(len(src_refs))]


def _plan_gather_pass(src_refs, land_refs):
    x, y, c = _position()
    chips = [(1 - x, y), (x, 1 - y), (1 - x, 1 - y)]
    return [(land_refs[a].at[4 * px + 2 * py + c], land_refs[a].at[4 * px + 2 * py + c], (a, j), (x, y, 1 - c))
            for j, (px, py) in enumerate(chips) for a in range(len(land_refs))]


def _plan_to_sibling(src_refs, land_refs):
    x, y, c = _position()
    return [(src_refs[a].at[2 * k + (1 - c)], land_refs[a].at[k], (a, k), (x, y, 1 - c))
            for k in range(4) for a in range(len(src_refs))]


def _plan_to_chips(src_refs, land_refs):
    x, y, c = _position()
    chips = [(1 - x, y), (x, 1 - y), (1 - x, 1 - y)]
    return [(src_refs[a].at[2 * px + py], land_refs[a].at[j], (a, j), (px, py, c))
            for j, (px, py) in enumerate(chips) for a in range(len(src_refs))]


def _split_start(srcs, lands, plan, n_sem, after, name):
    n_s, n_a = len(srcs), len(lands)
    n_b = n_s + n_a

    def body(*refs):
        src_refs, land_refs = refs[:n_s], refs[n_s:n_b]
        send_sems, recv_sems, token = refs[n_b + 1], refs[n_b + 2], refs[-1]
        for src, dst, (a, k), dev in plan(src_refs, land_refs):
            i = a * n_sem + k
            pltpu.make_async_remote_copy(src_ref=src, dst_ref=dst, send_sem=send_sems.at[i], recv_sem=recv_sems.at[i],
                                         device_id=dev, device_id_type=MESH).start()
        token[...] = jnp.zeros_like(token)

    bufs = list(srcs) + list(lands)
    res = pl.pallas_call(
        body, name=name,
        out_shape=(pltpu.SemaphoreType.DMA((n_a * n_sem,)), pltpu.SemaphoreType.DMA((n_a * n_sem,)),
                   *[pltpu.HBM(t.shape, t.dtype) for t in bufs], _plain((8, LANES), F32)),
        in_specs=[HBM_SPEC] * n_b + [ANY],
        out_specs=(SEM_SPEC, SEM_SPEC, *[HBM_SPEC] * n_b, pl.BlockSpec(memory_space=pltpu.VMEM)),
        input_output_aliases={i: 2 + i for i in range(n_b)},
        compiler_params=pltpu.CompilerParams(has_side_effects=EFFECT),
    )(*[pltpu.with_memory_space_constraint(t, pltpu.HBM) for t in bufs], after)
    return (res[0], res[1], res[2:2 + n_s], res[2 + n_s:2 + n_b]), res[-1]


def _split_wait(started, plan, after, name):
    send_sems, recv_sems, srcs, lands = started
    n_s, n_a = len(srcs), len(lands)
    n_b = n_s + n_a
    n_sem = send_sems.shape[0] // n_a

    def body(*refs):
        src_refs, land_refs = refs[:n_s], refs[n_s:n_b]
        s_sems, r_sems = refs[n_b], refs[n_b + 1]
        for src, dst, (a, k), dev in plan(src_refs, land_refs):
            i = a * n_sem + k
            cp = pltpu.make_async_remote_copy(src_ref=src, dst_ref=dst, send_sem=s_sems.at[i], recv_sem=r_sems.at[i],
                                              device_id=dev, device_id_type=MESH)
            cp.wait_send()
            cp.wait_recv()

    bufs = list(srcs) + list(lands)
    res = pl.pallas_call(
        body, name=name, out_shape=tuple(pltpu.HBM(t.shape, t.dtype) for t in bufs),
        in_specs=[HBM_SPEC] * n_b + [SEM_SPEC, SEM_SPEC, ANY],
        out_specs=tuple([HBM_SPEC] * n_b),
        input_output_aliases={i: i for i in range(n_b)},
        compiler_params=pltpu.CompilerParams(has_side_effects=EFFECT),
    )(*bufs, send_sems, recv_sems, after)
    return res[:n_s], res[n_s:]


def _row_tile(r):
    return ROW_TILE if r % ROW_TILE == 0 else r


def _rs_add_sibling(gp, recv, c_arr, name):
    _, r, l = gp.shape
    tr = r if r <= 4 * ROW_TILE else _row_tile(r)

    def body(c_ref, g_ref, r_ref, pf_ref, pb_ref):
        sm = g_ref[...].astype(F32) + r_ref[...].astype(F32)
        pf_ref[...] = sm
        pb_ref[...] = sm.astype(BF16)

    spec = pl.BlockSpec((None, tr, l), lambda k, i, c: (k, i, 0))
    return _pallas_call(
        body, name=name,
        grid_spec=pltpu.PrefetchScalarGridSpec(
            num_scalar_prefetch=1, grid=(4, r // tr),
            in_specs=[pl.BlockSpec((None, tr, l), lambda k, i, c: (2 * k + c[0], i, 0)), spec],
            out_specs=[spec, spec]),
        out_shape=[_sds((4, r, l), F32), _sds((4, r, l), BF16)], compiler_params=_params(2, 32),
    )(c_arr, gp, recv)


def _adam_update(w, gv, m, v):
    nm = ADAM_B1 * m + (1.0 - ADAM_B1) * gv
    nv = ADAM_B2 * v + (1.0 - ADAM_B2) * (gv * gv)
    m_hat = nm / (1.0 - ADAM_B1 ** ADAM_STEP)
    v_hat = nv / (1.0 - ADAM_B2 ** ADAM_STEP)
    return -ADAM_LR * (m_hat / (jnp.sqrt(v_hat) + ADAM_EPS) + ADAM_WD * w), nm, nv


def _rs_finish_adamw(pf, recv, k_arr, w, m, v, name):
    _, r, l = pf.shape
    tr = _row_tile(r)

    def body(k_ref, p_ref, r_ref, w_ref, m_ref, v_ref, g_ref, d_ref, nm_ref, nv_ref):
        gv = ((p_ref[...] + r_ref[0].astype(F32)) + r_ref[1].astype(F32)) + r_ref[2].astype(F32)
        g_ref[...] = gv
        d_ref[...], nm_ref[...], nv_ref[...] = _adam_update(w_ref[...], gv, m_ref[...], v_ref[...])

    spec = pl.BlockSpec((tr, l), lambda i, k: (i, 0))
    return _pallas_call(
        body, name=name,
        grid_spec=pltpu.PrefetchScalarGridSpec(
            num_scalar_prefetch=1, grid=(r // tr,),
            in_specs=[pl.BlockSpec((None, tr, l), lambda i, k: (k[0], i, 0)),
                      pl.BlockSpec((3, tr, l), lambda i, k: (0, i, 0)), spec, spec, spec],
            out_specs=[spec] * 4),
        out_shape=[_plain((r, l), F32)] * 4, compiler_params=_params(1, 32),
    )(k_arr, pf, recv, w, m, v)


def _rs_finish_adamw_t(pf, recv, k_arr, w_t, m_t, v_t, name):
    _, r, c = pf.shape
    tr = _row_tile(r)
    cp = -(-c // LANES) * LANES

    def body(k_ref, p_ref, r_ref, w_ref, m_ref, v_ref, g_ref, d_ref, nm_ref, nv_ref, pad):
        gv = ((p_ref[...] + r_ref[0].astype(F32)) + r_ref[1].astype(F32)) + r_ref[2].astype(F32)
        pad[...] = jnp.zeros_like(pad)
        pad[:, 0:c] = gv
        gt = pad[...].T[0:c, :]
        g_ref[...] = gt
        d_ref[...], nm_ref[...], nv_ref[...] = _adam_update(w_ref[...], gt, m_ref[...], v_ref[...])

    spec = pl.BlockSpec((c, tr), lambda i, k: (0, i))
    return _pallas_call(
        body, name=name,
        grid_spec=pltpu.PrefetchScalarGridSpec(
            num_scalar_prefetch=1, grid=(r // tr,),
            in_specs=[pl.BlockSpec((None, tr, c), lambda i, k: (k[0], i, 0)),
                      pl.BlockSpec((3, tr, c), lambda i, k: (0, i, 0)), spec, spec, spec],
            out_specs=[spec] * 4, scratch_shapes=[pltpu.VMEM((tr, cp), F32)]),
        out_shape=[_plain((c, r), F32)] * 4, compiler_params=_params(1, 32),
    )(k_arr, pf, recv, w_t, m_t, v_t)


def _sum_devices(g):
    def body(g_ref, o_ref):
        acc = g_ref[0]
        for j in range(1, N_DEV):
            acc = acc + g_ref[j]
        o_ref[...] = acc

    return _pallas_call(body, name="sum_devices", out_shape=_plain(g.shape[1:], F32))(g)


def _adamw(w, g, m, v, name):
    shape = w.shape
    w2, g2, m2, v2 = [t.reshape((-1, shape[-1])) for t in (w, g, m, v)]

    def body(w_ref, g_ref, m_ref, v_ref, d_ref, nm_ref, nv_ref):
        d_ref[...], nm_ref[...], nv_ref[...] = _adam_update(w_ref[...], g_ref[...], m_ref[...], v_ref[...])

    outs = _pallas_call(body, name=name, out_shape=[_plain(w2.shape, F32)] * 3)(w2, g2, m2, v2)
    return tuple(t.reshape(shape) for t in outs)


def _after(t, token):
    return t + token[0:1, 0:1].astype(t.dtype)


def _finish(name, pf, recv, k_arr, w, m, v):
    if name in ("attn_w_in", "conv_w_in"):
        res = _rs_finish_adamw_t(pf, recv, k_arr, w.T, m.T, v.T, "rs_finish_adamw_" + name)
        return tuple(t.T for t in res)
    return _rs_finish_adamw(pf, recv, k_arr, w, m, v, "rs_finish_adamw_" + name)


def kernel(x, mem, positions, norm_g, mem_norm_g, w_mem_kv, attn_w_in, attn_w_out, conv_w_in, conv_w, conv_w_out, final_g, loss_target, m_norm_g, m_mem_norm_g, m_w_mem_kv, m_attn_w_in, m_attn_w_out, m_conv_w_in, m_conv_w, m_conv_w_out, m_final_g, v_norm_g, v_mem_norm_g, v_w_mem_kv, v_attn_w_in, v_attn_w_out, v_conv_w_in, v_conv_w, v_conv_w_out, v_final_g):
    px, py, pc = _position()
    me = 4 * px + 2 * py + pc
    c_arr = jnp.reshape(pc, (1,)).astype(jnp.int32)
    k_arr = jnp.reshape(2 * px + py, (1,)).astype(jnp.int32)
    x, mem, pos, tgt = x[0], mem[0], positions[0], loss_target[0]

    wg_in0 = _all_gather_relay(attn_w_in[0].astype(BF16), "gather_w_in0")
    late = [attn_w_out[0].astype(BF16), conv_w_in[0].astype(BF16), conv_w_out[0].astype(BF16),
            w_mem_kv[0].astype(BF16), w_mem_kv[1].astype(BF16), jnp.pad(conv_w[0], ((0, 5), (0, 0)))]
    lands = [lax.dynamic_update_slice(lax.empty((N_DEV,) + t.shape, t.dtype), t[None], (me, 0, 0)) for t in late]
    late_weights, late_token = _split_start(late, lands, _plan_gather_own, 4, wg_in0, "gather_late_start")

    tabs = _rope_tables(pos)
    g0, g1 = _after(norm_g[0:1], late_token), norm_g[1:2]

    hn0, hn0_t, qs, ks, vs, tabs_v, qm0, z0 = _inproj_attn(x, g0, wg_in0, tabs)
    os_, ls_ = [], []
    for j, d in enumerate(DILATIONS):
        if j == 2:
            _, lands = _split_wait(late_weights, _plan_gather_own, ls_[1], "gather_late_wait")
            late_weights, late_token = _split_start([], lands, _plan_gather_pass, 3, ls_[1], "gather_late_pass_start")
        o, l = _attn_fwd(qs[j], ks[j], vs[j], d, late_token)
        os_.append(o)
        ls_.append(l)

    _, gathered = _split_wait(late_weights, _plan_gather_pass, ls_[2], "gather_late_pass_wait")
    wg_out0, wg_in1, wg_out1, wg_kv0, wg_kv1, cw_all = gathered
    w_out1 = wg_out1.reshape(-1, wg_out1.shape[2])
    w_kv = jnp.stack([wg_kv0.reshape(-1, wg_kv0.shape[2]), wg_kv1.reshape(-1, wg_kv1.shape[2])])
    cw = cw_all[:, 0:3].transpose(1, 0, 2).reshape(3, -1)
    kv = _memkv_fwd(mem, mem_norm_g, w_kv)
    y0, y0_t, mo0, h1 = _post_attn(os_, ls_, qm0, kv[0], z0, x, wg_out0)

    hn1, hn1_t, bg, cg, u, qm1, z1 = _inproj_conv(h1, g1, wg_in1)
    y1, y1_t, mo1, dh2, dh2b, loss_acc, d_final_g = _post_conv_loss(
        bg, cg, u, qm1, kv[1], z1, h1, w_out1, cw, final_g.reshape(1, -1), tgt)

    d_w_out1 = _wgrad_rows(y1_t, dh2b, "wgrad_out1")
    dz1, dbg, dconv, dqm1, dkv1 = _bwd_post_conv(dh2b, w_out1, bg, cg, u, z1, qm1, kv[1], mo1, cw)
    dcg, du, dcw = _bwd_conv(dconv, cg, u, cw)
    dh1, dh1b, dg1, dproj1 = _dgrad_norm([(dbg, 1), (dcg, 1), (du, 1), (dqm1, 1), (dz1, 1)], wg_in1, h1, g1, dh2,
                                         "dgrad_norm_conv")
    d_w_in1 = _wgrad_shards(hn1_t, dproj1, "wgrad_in1")

    d_w_out0 = _wgrad_cols(y0_t, dh1b, wg_out0.shape[2], "wgrad_out0")

    names1 = ["conv_w_in", "conv_w_out", "attn_w_out"]
    grads1 = [d_w_in1, d_w_out1, d_w_out0]
    started, token = _split_start(grads1, [lax.empty((4,) + g.shape[1:], g.dtype) for g in grads1],
                                  _plan_to_sibling, 4, dg1, "rs1_sibling_start")

    gw = GROUP_WIDTH
    ones = (jnp.arange(gw)[:, None] // HEAD_DIM == jnp.arange(gw)[None, :] // HEAD_DIM).astype(BF16)
    ones = _after(ones, token)
    res = _bwd_post_attn(dh1b, wg_out0, z0, os_, ls_, qm0, kv[0], mo0, ones)
    dz0, dos, dls, dqm0, dkv0 = res[0], res[1:4], res[4:7], res[7], res[8]

    grads1, from_sibling = _split_wait(started, _plan_to_sibling, dz0, "rs1_sibling_wait")
    parts1 = [_rs_add_sibling(g, r, c_arr, "rs_add_sibling_" + n) for g, r, n in zip(grads1, from_sibling, names1)]
    pbs1 = [pb for _, pb in parts1]
    started, token = _split_start(pbs1, [lax.empty((3,) + p.shape[1:], p.dtype) for p in pbs1],
                                  _plan_to_chips, 3, dg1, "rs1_chips_start")

    dqs, dks, dvs = [], [], []
    for j, d in enumerate(DILATIONS):
        dq, dk, dv = _attn_bwd(qs[j], ks[j], vs[j], ls_[j], dos[j], dls[j], tabs_v[j], d, token)
        dqs.append((dq, d))
        dks.append((dk, d))
        dvs.append((dv, d))
    d_w_kv, d_mem_g = _memkv_bwd(jnp.stack([dkv0, dkv1]), w_kv, mem, mem_norm_g)
    n_kv = d_w_kv.shape[1] // N_DEV
    dproj0 = _assemble_dproj(dqs + dks + dvs + [(dqm0, 1), (dz0, 1)], wg_in0.shape[2], "assemble_dproj_attn")
    d_w_in0 = _wgrad_shards(hn0_t, dproj0, "wgrad_in0")

    names0 = ["attn_w_in", "w_mem_kv0", "w_mem_kv1"]
    grads0 = [d_w_in0, d_w_kv[0].reshape(N_DEV, n_kv, -1), d_w_kv[1].reshape(N_DEV, n_kv, -1)]
    started0, token0 = _split_start(grads0, [lax.empty((4,) + g.shape[1:], g.dtype) for g in grads0],
                                    _plan_to_sibling, 4, dg1, "rs0_sibling_start")
    _, from_chips1 = _split_wait(started, _plan_to_chips, token0, "rs1_chips_wait")
    shard = dict(attn_w_in=(attn_w_in[0], m_attn_w_in[0], v_attn_w_in[0]),
                 attn_w_out=(attn_w_out[0], m_attn_w_out[0], v_attn_w_out[0]),
                 conv_w_in=(conv_w_in[0], m_conv_w_in[0], v_conv_w_in[0]),
                 conv_w_out=(conv_w_out[0], m_conv_w_out[0], v_conv_w_out[0]),
                 w_mem_kv0=(w_mem_kv[0], m_w_mem_kv[0], v_w_mem_kv[0]), w_mem_kv1=(w_mem_kv[1], m_w_mem_kv[1], v_w_mem_kv[1]))
    big = {}
    for n, (pf, _), r in zip(names1, parts1, from_chips1):
        big[n] = _finish(n, pf, r, k_arr, *shard[n])

    grads0, from_sibling = _split_wait(started0, _plan_to_sibling, big["conv_w_out"][1], "rs0_sibling_wait")
    parts0 = [_rs_add_sibling(g, r, c_arr, "rs_add_sibling_" + n) for g, r, n in zip(grads0, from_sibling, names0)]
    pbs0 = [pb for _, pb in parts0]
    started0, token0 = _split_start(pbs0, [lax.empty((3,) + p.shape[1:], p.dtype) for p in pbs0],
                                    _plan_to_chips, 3, dg1, "rs0_chips_start")
    dx, dg0 = _dgrad_norm_dm(dproj0, wg_in0, x, g0, dh1, token0, "dgrad_norm_attn")

    small_part = jnp.concatenate([dg0, dg1, d_mem_g.reshape(2, -1), d_final_g, dcw[0:3]], axis=0)
    small_part = jnp.concatenate([small_part, jnp.broadcast_to(loss_acc[0, 0], small_part.shape)], axis=0)
    small = _sum_devices(_all_gather([small_part], dg0, "gather_small_grads")[0])
    loss = small[8, 0]
    g_conv_w = lax.dynamic_slice(small[5:8], (0, me * LANES), (3, LANES))[None]
    small_g = dict(norm_g=small[0:2], mem_norm_g=small[2:4], conv_w=g_conv_w, final_g=small[4])
    small_w = dict(norm_g=(norm_g, m_norm_g, v_norm_g), mem_norm_g=(mem_norm_g, m_mem_norm_g, v_mem_norm_g),
                   conv_w=(conv_w, m_conv_w, v_conv_w), final_g=(final_g, m_final_g, v_final_g))
    for n, (w, m, v) in small_w.items():
        big[n] = (small_g[n],) + _adamw(w, small_g[n], m, v, "adamw_" + n)

    _, from_chips0 = _split_wait(started0, _plan_to_chips, big["final_g"][1], "rs0_chips_wait")
    for n, (pf, _), r in zip(names0, parts0, from_chips0):
        big[n] = _finish(n, pf, r, k_arr, *shard[n])
    for n in ("attn_w_in", "attn_w_out", "conv_w_in", "conv_w_out"):
        big[n] = tuple(t[None] for t in big[n])
    big["w_mem_kv"] = tuple(jnp.stack([a, b]) for a, b in zip(big["w_mem_kv0"], big["w_mem_kv1"]))

    order = ["norm_g", "mem_norm_g", "w_mem_kv", "attn_w_in", "attn_w_out", "conv_w_in", "conv_w", "conv_w_out", "final_g"]
    return (loss, dx[None], *[big[n][0] for n in order], *[big[n][1] for n in order],
            *[big[n][2] for n in order], *[big[n][3] for n in order])
```

```python
import functools

import jax
import jax.numpy as jnp
from jax import lax
from jax.experimental import pallas as pl
from jax.experimental.pallas import tpu as pltpu

F32 = jnp.float32
BF16 = jnp.bfloat16

N_DEV = 8
D_MODEL = 1024
HEAD_DIM = 64
ROT_DIM = HEAD_DIM // 4
ROPE_THETA = 500000.0
DILATIONS = (1, 4, 16)
HEADS_PER_GROUP = 8
GROUP_WIDTH = HEADS_PER_GROUP * HEAD_DIM
BLOCK = 128
N_MEM = 256
MEM_HEADS = 4
MEM_WIDTH = MEM_HEADS * HEAD_DIM
CONV_WIDTH = D_MODEL
EPS = 1e-6
SCALE = HEAD_DIM ** -0.5
NEG = -1e30

ADAM_LR = 0.001
ADAM_B1 = 0.9
ADAM_B2 = 0.999
ADAM_EPS = 1e-08
ADAM_WD = 0.01
ADAM_STEP = 10

ROW_TILE = 256
LANES = 128
MESH = pl.DeviceIdType.MESH
ANY = pl.BlockSpec(memory_space=pl.ANY)


def _pallas_call(body, **kw):
    call = pl.pallas_call(body, **kw)

    def run(*args):
        pinned = [pltpu.with_memory_space_constraint(a, pltpu.HBM) if jnp.issubdtype(a.dtype, jnp.floating) else a
                  for a in args]
        return call(*pinned)

    return run


def _dot(a, b):
    return lax.dot_general(a, b, (((1,), (0,)), ((), ())), preferred_element_type=F32)


def _dot_nt(a, b):
    return lax.dot_general(a, b, (((1,), (1,)), ((), ())), preferred_element_type=F32)


def _dot_tn(a, b):
    return lax.dot_general(a, b, (((0,), (0,)), ((), ())), preferred_element_type=F32)


def _params(n_grid, vmem_mb=48):
    return pltpu.CompilerParams(dimension_semantics=("arbitrary",) * n_grid, vmem_limit_bytes=vmem_mb << 20)


def _rows(width, tm=ROW_TILE):
    return pl.BlockSpec((tm, width), lambda i: (i, 0))


def _view_rows(width, d, tm=ROW_TILE):
    return pl.BlockSpec((tm // d, d * width), lambda i: (i, 0))


def _whole(shape):
    return pl.BlockSpec(shape, lambda *_: (0,) * len(shape))


def _resident(shape):
    return pl.BlockSpec(shape, lambda *_: (0,) * len(shape), pipeline_mode=pl.Buffered(1))


def _sds(shape, dtype):
    return pltpu.HBM(shape, dtype)


def _plain(shape, dtype):
    return jax.ShapeDtypeStruct(shape, dtype)


def _silu_parts(z):
    sg = jax.nn.sigmoid(z)
    return z * sg, sg * (1.0 + z * (1.0 - sg))


def _to_view(scr, val, out_ref, d):
    tm, w = val.shape
    if d == 1:
        out_ref[...] = val.astype(out_ref.dtype)
        return
    for cb in range(w // LANES):
        scr[cb] = val[:, cb * LANES:(cb + 1) * LANES]
    for r in range(d):
        for cb in range(w // LANES):
            lo = r * w + cb * LANES
            out_ref[:, lo:lo + LANES] = scr[cb, pl.ds(r, tm // d, stride=d), :].astype(out_ref.dtype)


def _from_view(scr, in_ref, d):
    if d == 1:
        return in_ref[...].astype(F32)
    nc, tm, _ = scr.shape
    w = nc * LANES
    for r in range(d):
        for cb in range(nc):
            lo = r * w + cb * LANES
            scr[cb, pl.ds(r, tm // d, stride=d), :] = in_ref[:, lo:lo + LANES].astype(F32)
    return jnp.concatenate([scr[cb] for cb in range(nc)], axis=1)


def _view_scratch(width, tm=ROW_TILE):
    return pltpu.VMEM((width // LANES, tm, LANES), F32)


def _rope_tables(pos):
    half = ROT_DIM // 2
    inv_freq = ROPE_THETA ** (-jnp.arange(half, dtype=F32) * (2.0 / ROT_DIM))
    ang = pos.astype(F32)[:, None] * inv_freq
    cos, sin = jnp.cos(ang), jnp.sin(ang)
    s = pos.shape[0]
    z8 = jnp.zeros((s, half), F32)
    rest = HEAD_DIM - ROT_DIM
    cosf = jnp.concatenate([cos, cos, jnp.ones((s, rest), F32)], axis=1)
    sa = jnp.concatenate([-sin, z8, jnp.zeros((s, rest), F32)], axis=1)
    sb = jnp.concatenate([z8, sin, jnp.zeros((s, rest), F32)], axis=1)
    return tuple(jnp.tile(t, (1, LANES // HEAD_DIM)) for t in (cosf, sa, sb))


def _rope_fwd(t, cv, sav, sbv):
    w = t.shape[1]
    return t * cv + pltpu.roll(t, w - ROT_DIM // 2, 1) * sav + pltpu.roll(t, ROT_DIM // 2, 1) * sbv


def _rope_bwd(g, cv, sav, sbv):
    w = g.shape[1]
    return g * cv + pltpu.roll(g * sav, ROT_DIM // 2, 1) + pltpu.roll(g * sbv, w - ROT_DIM // 2, 1)


def _project(hn, wg_ref, proj_scr):
    c = wg_ref.shape[2]
    for j in range(N_DEV):
        proj_scr[:, j * c:(j + 1) * c] = _dot(hn, wg_ref[j])


def _inproj_attn(x, g, wg, tabs):
    s, d_model = x.shape
    gw = GROUP_WIDTH
    n = N_DEV * wg.shape[2]
    nz = n - 9 * gw - MEM_WIDTH
    reps = gw // LANES
    tm = ROW_TILE

    def body(x_ref, g_ref, w_ref, c_ref, sa_ref, sb_ref, hn_ref, hnt_ref, *rest):
        outs, (proj, scr, tscr) = rest[:-3], rest[-3:]
        q_refs, k_refs, v_refs, t_refs, qm_ref, z_ref = outs[0:3], outs[3:6], outs[6:9], outs[9:18], outs[18], outs[19]
        xb = x_ref[...]
        r = lax.rsqrt(jnp.mean(xb * xb, axis=-1, keepdims=True) + EPS)
        hn = ((xb * r) * g_ref[...]).astype(BF16)
        hn_ref[...] = hn
        hnt_ref[...] = hn.T
        _project(hn, w_ref, proj)
        tab = (c_ref[...], sa_ref[...], sb_ref[...])
        cv, sav, sbv = [jnp.tile(t, (1, reps)) for t in tab]
        for j, d in enumerate(DILATIONS):
            tq = _rope_fwd(proj[:, j * gw:(j + 1) * gw], cv, sav, sbv)
            _to_view(scr, tq * SCALE, q_refs[j], d)
            tk = _rope_fwd(proj[:, (3 + j) * gw:(4 + j) * gw], cv, sav, sbv)
            _to_view(scr, tk, k_refs[j], d)
            _to_view(scr, proj[:, (6 + j) * gw:(7 + j) * gw], v_refs[j], d)
            for i in range(3):
                _to_view(tscr, tab[i], t_refs[3 * j + i], d)
        qm_ref[...] = proj[:, 9 * gw:9 * gw + MEM_WIDTH].astype(BF16)
        z_ref[...] = proj[:, 9 * gw + MEM_WIDTH:]

    views = [_sds((s // d, d * gw), BF16) for d in DILATIONS]
    tviews = [_sds((s // d, d * LANES), F32) for d in DILATIONS for _ in range(3)]
    out_shape = ([_sds((s, d_model), BF16), _sds((d_model, s), BF16)] + views * 3 + tviews
                 + [_sds((s, MEM_WIDTH), BF16), _sds((s, nz), F32)])
    vspecs = [_view_rows(gw, d, tm) for d in DILATIONS]
    tspecs = [_view_rows(LANES, d, tm) for d in DILATIONS for _ in range(3)]
    out_specs = ([_rows(d_model, tm), pl.BlockSpec((d_model, tm), lambda i: (0, i))] + vspecs * 3 + tspecs
                 + [_rows(MEM_WIDTH, tm), _rows(nz, tm)])
    res = _pallas_call(
        body, name="inproj_attn", grid=(s // tm,), out_shape=out_shape,
        in_specs=[_rows(d_model, tm), _whole((1, d_model)), _resident(wg.shape)] + [_rows(LANES, tm)] * 3,
        out_specs=out_specs,
        scratch_shapes=[pltpu.VMEM((tm, n), F32), _view_scratch(gw, tm), _view_scratch(LANES, tm)],
        compiler_params=_params(1, 60),
    )(x, g, wg, *tabs)
    tabs_v = [res[11 + 3 * j:14 + 3 * j] for j in range(3)]
    return res[0], res[1], res[2:5], res[5:8], res[8:11], tabs_v, res[20], res[21]


def _band_mask(n_keys):
    qi = lax.broadcasted_iota(jnp.int32, (BLOCK, n_keys), 0)
    kj = lax.broadcasted_iota(jnp.int32, (BLOCK, n_keys), 1)
    if n_keys == BLOCK:
        return kj <= qi
    return jnp.logical_or(jnp.logical_and(kj < BLOCK, kj >= qi), jnp.logical_and(kj >= BLOCK, (kj - BLOCK) <= qi))


def _low_head_lanes():
    return lax.broadcasted_iota(jnp.int32, (1, LANES), 1) < HEAD_DIM


def _split_pair(t, low):
    zero = jnp.zeros_like(t)
    return jnp.where(low, t, zero), jnp.where(low, zero, t)


def _pair_specs(d, nb, w):
    if nb == 1:
        return pl.BlockSpec((BLOCK, 2 * w), lambda n: (0, n)), None
    half = nb // 2
    two = pl.BlockSpec((2 * BLOCK, w), lambda n: (n % half, n // half))
    before = pl.BlockSpec((BLOCK, w), lambda n: (jnp.maximum(2 * (n % half) - 1, 0), n // half))
    return two, before


def _head_tiles(w, col0):
    return ([slice(p * LANES, (p + 1) * LANES) for p in range(w // LANES)],
            [slice(col0 + p * LANES, col0 + (p + 1) * LANES) for p in range(w // LANES)])


def _attend_fwd(q_ref, o_ref, lse_ref, rows, col0, kk, vv):
    w = kk.shape[1]
    valid = _band_mask(kk.shape[0])
    low = _low_head_lanes()
    pairs, qcols = _head_tiles(w, col0)
    qs_ = [h for qc in qcols for h in _split_pair(q_ref[rows, qc], low)]
    k2s = [kk[:, pr] for pr in pairs for _ in range(2)]
    scs = [jnp.where(valid, _dot_nt(qh, k2), NEG) for qh, k2 in zip(qs_, k2s)]
    ms = [jnp.max(sc, axis=-1, keepdims=True) for sc in scs]
    ps = [jnp.exp(sc - m) for sc, m in zip(scs, ms)]
    ls = [jnp.sum(p, axis=-1, keepdims=True) for p in ps]
    pns = [(p * (1.0 / l)).astype(BF16) for p, l in zip(ps, ls)]
    for i, (pr, qc) in enumerate(zip(pairs, qcols)):
        v2 = vv[:, pr]
        a, b = 2 * i, 2 * i + 1
        o_ref[rows, qc] = jnp.where(low, _dot(pns[a], v2), _dot(pns[b], v2))
        lse_ref[rows, qc] = jnp.where(low, ms[a] + jnp.log(ls[a]), ms[b] + jnp.log(ls[b]))


TOP, BOTTOM = slice(0, BLOCK), slice(BLOCK, 2 * BLOCK)


def _attn_fwd(q, k, v, d, token):
    ln, dw = q.shape
    w = dw // d
    nb = ln // BLOCK
    two, before = _pair_specs(d, nb, w)

    def body_streams(_, q_ref, kc_ref, vc_ref, o_ref, lse_ref):
        for sb in range(2):
            cols = slice(sb * w, (sb + 1) * w)
            _attend_fwd(q_ref, o_ref, lse_ref, TOP, sb * w, kc_ref[:, cols], vc_ref[:, cols])

    def body_blocks(_, q_ref, kp_ref, kc_ref, vp_ref, vc_ref, o_ref, lse_ref):
        first = pl.program_id(0) % (nb // 2) == 0
        pl.when(first)(lambda: _attend_fwd(q_ref, o_ref, lse_ref, TOP, 0, kc_ref[TOP, :], vc_ref[TOP, :]))
        pl.when(jnp.logical_not(first))(lambda: _attend_fwd(
            q_ref, o_ref, lse_ref, TOP, 0, jnp.concatenate([kp_ref[...], kc_ref[TOP, :]], axis=0),
            jnp.concatenate([vp_ref[...], vc_ref[TOP, :]], axis=0)))
        _attend_fwd(q_ref, o_ref, lse_ref, BOTTOM, 0, kc_ref[...], vc_ref[...])

    if nb == 1:
        body, in_specs, args = body_streams, [ANY, two, two, two], (token, q, k, v)
    else:
        body, in_specs, args = body_blocks, [ANY, two, before, two, before, two], (token, q, k, k, v, v)
    return _pallas_call(
        body, name=f"attn_fwd_d{d}", grid=(d * nb // 2,), out_shape=[_sds((ln, dw), F32)] * 2,
        in_specs=in_specs, out_specs=[two, two], compiler_params=_params(1, 32),
    )(*args)


def _memkv_fwd(mem, g, w):
    n_layers = w.shape[0]

    def body(mem_ref, g_ref, w_ref, kv_ref):
        mb = mem_ref[...]
        r = lax.rsqrt(jnp.mean(mb * mb, axis=-1, keepdims=True) + EPS)
        mn = ((mb * r) * g_ref[...]).astype(BF16)
        kv_ref[...] = _dot(mn, w_ref[...]).astype(BF16)

    return _pallas_call(
        body, name="memkv_fwd", grid=(n_layers,),
        out_shape=_plain((n_layers, N_MEM, 2 * MEM_WIDTH), BF16),
        in_specs=[_whole(mem.shape), pl.BlockSpec((None, 1, D_MODEL), lambda l: (l, 0, 0)),
                  pl.BlockSpec((None, D_MODEL, 2 * MEM_WIDTH), lambda l: (l, 0, 0))],
        out_specs=pl.BlockSpec((None, N_MEM, 2 * MEM_WIDTH), lambda l: (l, 0, 0)),
        compiler_params=_params(1, 32),
    )(mem, g.reshape(n_layers, 1, D_MODEL), w)


def _mix_groups(os_, ls_):
    mx = jnp.maximum(jnp.maximum(ls_[0], ls_[1]), ls_[2])
    es = [jnp.exp(t - mx) for t in ls_]
    inv = 1.0 / (es[0] + es[1] + es[2])
    ws = [e * inv for e in es]
    mix = ws[0] * os_[0] + ws[1] * os_[1] + ws[2] * os_[2]
    return ws, mix


MEM_PAIRS = [slice(p * LANES, (p + 1) * LANES) for p in range(MEM_WIDTH // LANES)]


def _mem_probs(qhs, k2s):
    scs = [_dot_nt(qh, k2) * SCALE for qh, k2 in zip(qhs, k2s)]
    es = [jnp.exp(sc - jnp.max(sc, axis=-1, keepdims=True)) for sc in scs]
    return [e * (1.0 / jnp.sum(e, axis=-1, keepdims=True)) for e in es]


def _mem_attn_into(qm, kv_ref, mo_ref):
    low = _low_head_lanes()
    qhs = [h for pr in MEM_PAIRS for h in _split_pair(qm[:, pr], low)]
    k2s = [kv_ref[:, pr] for pr in MEM_PAIRS for _ in range(2)]
    ps = [p.astype(BF16) for p in _mem_probs(qhs, k2s)]
    for i, pr in enumerate(MEM_PAIRS):
        v2 = kv_ref[:, MEM_WIDTH + i * LANES:MEM_WIDTH + (i + 1) * LANES]
        mo_ref[:, pr] = jnp.where(low, _dot(ps[2 * i], v2), _dot(ps[2 * i + 1], v2))


def _mem_attn_bwd(qm, kv_ref, dmem, dqm_ref, dkv_ref):
    low = _low_head_lanes()
    dmb = dmem.astype(BF16)
    vps = [slice(MEM_WIDTH + i * LANES, MEM_WIDTH + (i + 1) * LANES) for i in range(len(MEM_PAIRS))]
    qhs = [h for pr in MEM_PAIRS for h in _split_pair(qm[:, pr], low)]
    dhs = [h for pr in MEM_PAIRS for h in _split_pair(dmb[:, pr], low)]
    k2s = [kv_ref[:, pr] for pr in MEM_PAIRS for _ in range(2)]
    v2s = [kv_ref[:, vp] for vp in vps for _ in range(2)]
    ps = _mem_probs(qhs, k2s)
    dps = [_dot_nt(dh, v2) for dh, v2 in zip(dhs, v2s)]
    dss = [(p * (dp - jnp.sum(dp * p, axis=-1, keepdims=True)) * SCALE).astype(BF16) for p, dp in zip(ps, dps)]
    pbs = [p.astype(BF16) for p in ps]
    for i, (pr, vp) in enumerate(zip(MEM_PAIRS, vps)):
        a, b = 2 * i, 2 * i + 1
        dqm_ref[:, pr] = jnp.where(low, _dot(dss[a], k2s[a]), _dot(dss[b], k2s[b])).astype(BF16)
        dkv_ref[:, pr] += _dot_tn(dss[a], qhs[a]) + _dot_tn(dss[b], qhs[b])
        dkv_ref[:, vp] += _dot_tn(pbs[a], dhs[a]) + _dot_tn(pbs[b], dhs[b])


def _post_attn(os_, ls_, qm, kv, z, x, wg_out):
    s, d_model = x.shape
    gw = GROUP_WIDTH
    nb = gw + MEM_WIDTH
    c = wg_out.shape[2]
    tm = ROW_TILE

    def body(o0, o1, o2, l0, l1, l2, qm_ref, kv_ref, z_ref, x_ref, w_ref, y_ref, yt_ref, mo_ref, h_ref, s0, s1):
        ov, lv = [], []
        for o_ref, l_ref, d in zip((o0, o1, o2), (l0, l1, l2), DILATIONS):
            ov.append(_from_view(s0, o_ref, d))
            lv.append(_from_view(s1, l_ref, d))
        _, mix = _mix_groups(ov, lv)
        _mem_attn_into(qm_ref[...], kv_ref, mo_ref)
        sz, _ = _silu_parts(z_ref[...])
        y_ref[:, :gw] = (mix * sz[:, :gw]).astype(BF16)
        y_ref[:, gw:] = (mo_ref[...] * sz[:, gw:]).astype(BF16)
        y = y_ref[...]
        yt_ref[...] = y.T
        for j in range(N_DEV):
            h_ref[:, j * c:(j + 1) * c] = x_ref[:, j * c:(j + 1) * c] + _dot(y, w_ref[j])

    vspecs = [_view_rows(gw, d) for d in DILATIONS]
    return _pallas_call(
        body, name="post_attn", grid=(s // tm,),
        out_shape=[_sds((s, nb), BF16), _sds((nb, s), BF16), _sds((s, MEM_WIDTH), F32), _sds((s, d_model), F32)],
        in_specs=vspecs * 2 + [_rows(MEM_WIDTH), _whole(kv.shape), _rows(nb), _rows(d_model), _whole(wg_out.shape)],
        out_specs=[_rows(nb), pl.BlockSpec((nb, tm), lambda i: (0, i)), _rows(MEM_WIDTH), _rows(d_model)],
        scratch_shapes=[_view_scratch(gw), _view_scratch(gw)],
        compiler_params=_params(1, 40),
    )(*os_, *ls_, qm, kv, z, x, wg_out)


def _inproj_conv(x, g, wg):
    s, d_model = x.shape
    c = CONV_WIDTH
    n = N_DEV * wg.shape[2]
    nz = n - 3 * c - MEM_WIDTH
    tm = ROW_TILE

    def body(x_ref, g_ref, w_ref, hn_ref, hnt_ref, bg_ref, cg_ref, u_ref, qm_ref, z_ref, proj):
        xb = x_ref[...]
        r = lax.rsqrt(jnp.mean(xb * xb, axis=-1, keepdims=True) + EPS)
        hn = ((xb * r) * g_ref[...]).astype(BF16)
        hn_ref[...] = hn
        hnt_ref[...] = hn.T
        _project(hn, w_ref, proj)
        bg_ref[...] = proj[:, 0:c]
        cg_ref[...] = proj[:, c:2 * c]
        u_ref[...] = proj[:, 2 * c:3 * c]
        qm_ref[...] = proj[:, 3 * c:3 * c + MEM_WIDTH].astype(BF16)
        z_ref[...] = proj[:, 3 * c + MEM_WIDTH:]

    return _pallas_call(
        body, name="inproj_conv", grid=(s // tm,),
        out_shape=[_sds((s, d_model), BF16), _sds((d_model, s), BF16)] + [_sds((s, c), F32)] * 3
                  + [_sds((s, MEM_WIDTH), BF16), _sds((s, nz), F32)],
        in_specs=[_rows(d_model), _whole((1, d_model)), _whole(wg.shape)],
        out_specs=[_rows(d_model), pl.BlockSpec((d_model, tm), lambda i: (0, i))] + [_rows(c)] * 3
                  + [_rows(MEM_WIDTH), _rows(nz)],
        scratch_shapes=[pltpu.VMEM((tm, n), F32)],
        compiler_params=_params(1, 60),
    )(x, g, wg)


HALO = 8


def _halo_before(width, tm=ROW_TILE):
    return pl.BlockSpec((HALO, width), lambda i: (jnp.maximum(i * (tm // HALO) - 1, 0), 0))


def _halo_after(width, n_rows, tm=ROW_TILE):
    return pl.BlockSpec((HALO, width), lambda i: (jnp.minimum((i + 1) * (tm // HALO), n_rows // HALO - 1), 0))


def _conv_taps(cg_ref, u_ref, cgh_ref, uh_ref, i):
    a = cg_ref[...] * u_ref[...]
    ah = jnp.where(i > 0, cgh_ref[...] * uh_ref[...], 0.0)
    row = lax.broadcasted_iota(jnp.int32, a.shape, 0)
    a1 = jnp.where(row == 0, ah[HALO - 1:HALO], pltpu.roll(a, 1, 0))
    a2 = jnp.where(row == 0, ah[HALO - 2:HALO - 1], jnp.where(row == 1, ah[HALO - 1:HALO], pltpu.roll(a, 2, 0)))
    return a, a1, a2


def _post_conv_loss(bg, cg, u, qm, kv, z, h1, w_out, cw, gf, tgt):
    s, d = h1.shape
    c = CONV_WIDTH
    nb = c + MEM_WIDTH
    tm = ROW_TILE

    def body(bg_ref, cg_ref, u_ref, cgh_ref, uh_ref, qm_ref, kv_ref, z_ref, h_ref, w_ref, cw_ref, gf_ref, t_ref,
             y_ref, yt_ref, mo_ref, dh_ref, dhb_ref, loss_ref, dgf_ref):
        i = pl.program_id(0)
        a, a1, a2 = _conv_taps(cg_ref, u_ref, cgh_ref, uh_ref, i)
        conv = cw_ref[0:1, :] * a2 + cw_ref[1:2, :] * a1 + cw_ref[2:3, :] * a
        mix = bg_ref[...] * conv
        _mem_attn_into(qm_ref[...], kv_ref, mo_ref)
        sz, _ = _silu_parts(z_ref[...])
        y_ref[:, :c] = (mix * sz[:, :c]).astype(BF16)
        y_ref[:, c:] = (mo_ref[...] * sz[:, c:]).astype(BF16)
        y = y_ref[...]
        yt_ref[...] = y.T
        h2 = h_ref[...] + _dot(y, w_ref[...])
        r = lax.rsqrt(jnp.mean(h2 * h2, axis=-1, keepdims=True) + EPS)
        nh = h2 * r
        gfv = gf_ref[...]
        diff = nh * gfv - t_ref[...]
        dout = diff * (1.0 / d)
        dn = dout * gfv
        dh2 = r * dn - h2 * ((r * r * r) * jnp.mean(dn * h2, axis=-1, keepdims=True))
        dh_ref[...] = dh2
        dhb_ref[...] = dh2.astype(BF16)

        @pl.when(i == 0)
        def _():
            loss_ref[...] = jnp.zeros_like(loss_ref)
            dgf_ref[...] = jnp.zeros_like(dgf_ref)

        loss_ref[...] += 0.5 * jnp.sum(jnp.mean(diff * diff, axis=-1, keepdims=True))
        dgf_ref[...] += jnp.sum(dout * nh, axis=0, keepdims=True)

    return _pallas_call(
        body, name="post_conv_loss", grid=(s // tm,),
        out_shape=[_sds((s, nb), BF16), _sds((nb, s), BF16), _sds((s, MEM_WIDTH), F32), _sds((s, d), F32),
                   _sds((s, d), BF16), _plain((8, LANES), F32), _plain((1, d), F32)],
        in_specs=[_rows(c)] * 3 + [_halo_before(c)] * 2 + [_rows(MEM_WIDTH), _whole(kv.shape), _rows(nb), _rows(d),
                  _whole(w_out.shape), _whole(cw.shape), _whole((1, d)), _rows(d)],
        out_specs=[_rows(nb), pl.BlockSpec((nb, tm), lambda i: (0, i)), _rows(MEM_WIDTH), _rows(d), _rows(d),
                   _whole((8, LANES)), _whole((1, d))],
        compiler_params=_params(1, 48),
    )(bg, cg, u, cg, u, qm, kv, z, h1, w_out, cw, gf, tgt)


def _bwd_post_conv(dhb, w_out, bg, cg, u, z, qm, kv, mo, cw):
    s = dhb.shape[0]
    c = CONV_WIDTH
    nb = c + MEM_WIDTH

    def body(dh_ref, w_ref, bg_ref, cg_ref, u_ref, cgh_ref, uh_ref, z_ref, qm_ref, kv_ref, mo_ref, cw_ref,
             dz_ref, dbg_ref, dc_ref, dqm_ref, dkv_ref):
        i = pl.program_id(0)

        @pl.when(i == 0)
        def _():
            dkv_ref[...] = jnp.zeros_like(dkv_ref)

        dy = _dot_nt(dh_ref[...], w_ref[...])
        sz, dsz = _silu_parts(z_ref[...])
        a, a1, a2 = _conv_taps(cg_ref, u_ref, cgh_ref, uh_ref, i)
        conv = cw_ref[0:1, :] * a2 + cw_ref[1:2, :] * a1 + cw_ref[2:3, :] * a
        bgv = bg_ref[...]
        dz_ref[:, :c] = (dy[:, :c] * (bgv * conv) * dsz[:, :c]).astype(BF16)
        dz_ref[:, c:] = (dy[:, c:] * mo_ref[...] * dsz[:, c:]).astype(BF16)
        dbr = dy * sz
        dmix = dbr[:, :c]
        dbg_ref[...] = (dmix * conv).astype(BF16)
        dc_ref[...] = dmix * bgv
        _mem_attn_bwd(qm_ref[...], kv_ref, dbr[:, c:], dqm_ref, dkv_ref)

    return _pallas_call(
        body, name="bwd_post_conv", grid=(s // ROW_TILE,),
        out_shape=[_sds((s, nb), BF16), _sds((s, c), BF16), _sds((s, c), F32), _sds((s, MEM_WIDTH), BF16),
                   _plain(kv.shape, F32)],
        in_specs=[_rows(D_MODEL), _whole(w_out.shape)] + [_rows(c)] * 3 + [_halo_before(c)] * 2
                 + [_rows(nb), _rows(MEM_WIDTH), _whole(kv.shape), _rows(MEM_WIDTH), _whole(cw.shape)],
        out_specs=[_rows(nb), _rows(c), _rows(c), _rows(MEM_WIDTH), _whole(kv.shape)],
        compiler_params=_params(1, 48),
    )(dhb, w_out, bg, cg, u, cg, u, z, qm, kv, mo, cw)


def _bwd_conv(dconv, cg, u, cw):
    s, c = dconv.shape
    tm = ROW_TILE
    last = s // tm - 1

    def body(dc_ref, dcn_ref, cg_ref, u_ref, cgh_ref, uh_ref, cw_ref, dcg_ref, du_ref, dcw_ref):
        i = pl.program_id(0)

        @pl.when(i == 0)
        def _():
            dcw_ref[...] = jnp.zeros_like(dcw_ref)

        dc = dc_ref[...]
        dcn = jnp.where(i < last, dcn_ref[...], 0.0)
        row = lax.broadcasted_iota(jnp.int32, dc.shape, 0)
        d1 = jnp.where(row == tm - 1, dcn[0:1], pltpu.roll(dc, tm - 1, 0))
        d2 = jnp.where(row == tm - 1, dcn[1:2], jnp.where(row == tm - 2, dcn[0:1], pltpu.roll(dc, tm - 2, 0)))
        da = cw_ref[2:3, :] * dc + cw_ref[1:2, :] * d1 + cw_ref[0:1, :] * d2
        a, a1, a2 = _conv_taps(cg_ref, u_ref, cgh_ref, uh_ref, i)
        dcg_ref[...] = (da * u_ref[...]).astype(BF16)
        du_ref[...] = (da * cg_ref[...]).astype(BF16)
        dcw_ref[0:1, :] += jnp.sum(dc * a2, axis=0, keepdims=True)
        dcw_ref[1:2, :] += jnp.sum(dc * a1, axis=0, keepdims=True)
        dcw_ref[2:3, :] += jnp.sum(dc * a, axis=0, keepdims=True)

    return _pallas_call(
        body, name="bwd_conv", grid=(s // tm,),
        out_shape=[_sds((s, c), BF16), _sds((s, c), BF16), _plain((8, c), F32)],
        in_specs=[_rows(c), _halo_after(c, s), _rows(c), _rows(c), _halo_before(c), _halo_before(c), _whole(cw.shape)],
        out_specs=[_rows(c), _rows(c), _whole((8, c))], compiler_params=_params(1, 40),
    )(dconv, dconv, cg, u, cg, u, cw)


def _dgrad_norm(pieces, wg, h, g, dres, name):
    s, d_model = h.shape
    c = wg.shape[2]
    n = N_DEV * c
    tm = ROW_TILE
    widths = [p.shape[1] // d for p, d in pieces]
    assert sum(widths) == n
    n_p = len(pieces)

    def body(*refs):
        p_refs = refs[:n_p]
        w_ref, h_ref, g_ref, dr_ref, dh_ref, dhb_ref, dg_ref, dpd_ref, dp, scr = refs[n_p:]

        @pl.when(pl.program_id(0) == 0)
        def _():
            dg_ref[...] = jnp.zeros_like(dg_ref)

        off = 0
        for p_ref, (_, d), wd in zip(p_refs, pieces, widths):
            if d == 1:
                dp[:, off:off + wd] = p_ref[...]
            else:
                dp[:, off:off + wd] = _from_view(scr, p_ref, d).astype(BF16)
            off += wd
        dhn = jnp.zeros((tm, d_model), F32)
        for j in range(N_DEV):
            dpj = dp[:, j * c:(j + 1) * c]
            dpd_ref[j] = dpj
            dhn += _dot_nt(dpj, w_ref[j])
        hb = h_ref[...]
        r = lax.rsqrt(jnp.mean(hb * hb, axis=-1, keepdims=True) + EPS)
        dg_ref[...] += jnp.sum(dhn * (hb * r), axis=0, keepdims=True)
        dn = dhn * g_ref[...]
        dh = dr_ref[...] + r * dn - hb * ((r * r * r) * jnp.mean(dn * hb, axis=-1, keepdims=True))
        dh_ref[...] = dh
        dhb_ref[...] = dh.astype(BF16)

    p_specs = [_view_rows(wd, d) for (_, d), wd in zip(pieces, widths)]
    return _pallas_call(
        body, name=name, grid=(s // tm,),
        out_shape=[_plain((s, d_model), F32), _sds((s, d_model), BF16), _plain((1, d_model), F32), _sds((N_DEV, s, c), BF16)],
        in_specs=p_specs + [_whole(wg.shape), _rows(d_model), _whole((1, d_model)), _rows(d_model)],
        out_specs=[_rows(d_model), _rows(d_model), _whole((1, d_model)), pl.BlockSpec((N_DEV, tm, c), lambda i: (0, i, 0))],
        scratch_shapes=[pltpu.VMEM((tm, n), BF16), _view_scratch(GROUP_WIDTH)],
        compiler_params=_params(1, 60),
    )(*[p for p, _ in pieces], wg, h, g, dres)


def _assemble_dproj(pieces, c, name):
    n = N_DEV * c
    tm = ROW_TILE
    widths = [p.shape[1] // d for p, d in pieces]
    assert sum(widths) == n
    s = pieces[0][0].shape[0] * pieces[0][1]
    n_p = len(pieces)

    def body(*refs):
        p_refs, (dpd_ref, dp, scr) = refs[:n_p], refs[n_p:]
        off = 0
        for p_ref, (_, d), wd in zip(p_refs, pieces, widths):
            if d == 1:
                dp[:, off:off + wd] = p_ref[...]
            else:
                dp[:, off:off + wd] = _from_view(scr, p_ref, d).astype(BF16)
            off += wd
        for j in range(N_DEV):
            dpd_ref[j] = dp[:, j * c:(j + 1) * c]

    return _pallas_call(
        body, name=name, grid=(s // tm,), out_shape=_sds((N_DEV, s, c), BF16),
        in_specs=[_view_rows(wd, d) for (_, d), wd in zip(pieces, widths)],
        out_specs=pl.BlockSpec((N_DEV, tm, c), lambda i: (0, i, 0)),
        scratch_shapes=[pltpu.VMEM((tm, n), BF16), _view_scratch(GROUP_WIDTH)],
        compiler_params=_params(1, 40),
    )(*[p for p, _ in pieces])


def _dgrad_norm_dm(dproj_dm, wg, h, g, dres, token, name):
    s, d_model = h.shape
    c = wg.shape[2]
    tm = ROW_TILE

    def body(_, dp_ref, w_ref, h_ref, g_ref, dr_ref, dh_ref, dg_ref):
        @pl.when(pl.program_id(0) == 0)
        def _():
            dg_ref[...] = jnp.zeros_like(dg_ref)

        dhn = jnp.zeros((tm, d_model), F32)
        for j in range(N_DEV):
            dhn += _dot_nt(dp_ref[j], w_ref[j])
        hb = h_ref[...]
        r = lax.rsqrt(jnp.mean(hb * hb, axis=-1, keepdims=True) + EPS)
        dg_ref[...] += jnp.sum(dhn * (hb * r), axis=0, keepdims=True)
        dn = dhn * g_ref[...]
        dh_ref[...] = dr_ref[...] + r * dn - hb * ((r * r * r) * jnp.mean(dn * hb, axis=-1, keepdims=True))

    return _pallas_call(
        body, name=name, grid=(s // tm,),
        out_shape=[_plain((s, d_model), F32), _plain((1, d_model), F32)],
        in_specs=[ANY, pl.BlockSpec((N_DEV, tm, c), lambda i: (0, i, 0)), _whole(wg.shape), _rows(d_model),
                  _whole((1, d_model)), _rows(d_model)],
        out_specs=[_rows(d_model), _whole((1, d_model))],
        compiler_params=_params(1, 60),
    )(token, dproj_dm, wg, h, g, dres)


def _wgrad_shards(a_t, b_dm, name):
    m, s = a_t.shape
    c = b_dm.shape[2]

    def body(a_ref, b_ref, o_ref):
        o_ref[...] = _dot(a_ref[...], b_ref[...]).astype(BF16)

    return _pallas_call(
        body, name=name, grid=(N_DEV,), out_shape=_sds((N_DEV, m, c), BF16),
        in_specs=[_whole(a_t.shape), pl.BlockSpec((None, s, c), lambda j: (j, 0, 0))],
        out_specs=pl.BlockSpec((None, m, c), lambda j: (j, 0, 0)), compiler_params=_params(1, 40),
    )(a_t, b_dm)


def _wgrad_cols(a_t, b, c, name):
    m, s = a_t.shape

    def body(a_ref, b_ref, o_ref):
        o_ref[...] = _dot(a_ref[...], b_ref[...]).astype(BF16)

    return _pallas_call(
        body, name=name, grid=(N_DEV,), out_shape=_sds((N_DEV, m, c), BF16),
        in_specs=[_whole(a_t.shape), pl.BlockSpec((s, c), lambda j: (0, j))],
        out_specs=pl.BlockSpec((None, m, c), lambda j: (j, 0, 0)), compiler_params=_params(1, 40),
    )(a_t, b)


def _wgrad_rows(a_t, b, name):
    m, s = a_t.shape
    n = b.shape[1]
    mr = m // N_DEV

    def body(a_ref, b_ref, o_ref):
        o_ref[...] = _dot(a_ref[...], b_ref[...]).astype(BF16)

    return _pallas_call(
        body, name=name, grid=(N_DEV,), out_shape=_sds((N_DEV, mr, n), BF16),
        in_specs=[pl.BlockSpec((mr, s), lambda j: (j, 0)), _whole(b.shape)],
        out_specs=pl.BlockSpec((None, mr, n), lambda j: (j, 0, 0)), compiler_params=_params(1, 40),
    )(a_t, b)


def _memkv_bwd(dkv, w, mem, g):
    n_layers = w.shape[0]
    rows = D_MODEL // N_DEV

    def body(dkv_ref, w_ref, mem_ref, g_ref, dw_ref, dg_ref):
        mb = mem_ref[...]
        r = lax.rsqrt(jnp.mean(mb * mb, axis=-1, keepdims=True) + EPS)
        nm = mb * r
        mn = (nm * g_ref[...]).astype(BF16)
        dkvb = dkv_ref[...].astype(BF16)
        dw_ref[...] = _dot_tn(mn, dkvb).astype(BF16).reshape(N_DEV, rows, 2 * MEM_WIDTH)
        dmn = _dot_nt(dkvb, w_ref[...])
        dg_ref[...] = jnp.sum(dmn * nm, axis=0, keepdims=True)

    lay = lambda *shape: pl.BlockSpec((None,) + shape, lambda l: (l, 0, 0))
    return _pallas_call(
        body, name="memkv_bwd", grid=(n_layers,),
        out_shape=[_sds((N_DEV, n_layers * rows, 2 * MEM_WIDTH), BF16), _plain((n_layers, 1, D_MODEL), F32)],
        in_specs=[lay(N_MEM, 2 * MEM_WIDTH), lay(D_MODEL, 2 * MEM_WIDTH), _whole(mem.shape), lay(1, D_MODEL)],
        out_specs=[pl.BlockSpec((N_DEV, rows, 2 * MEM_WIDTH), lambda l: (0, l, 0)), lay(1, D_MODEL)],
        compiler_params=_params(1, 32),
    )(dkv, w, mem, g.reshape(n_layers, 1, D_MODEL))


def _bwd_post_attn(dhb, wg_out, z, os_, ls_, qm, kv, mo, head_ones):
    s = dhb.shape[0]
    gw = GROUP_WIDTH
    nb = gw + MEM_WIDTH
    c = wg_out.shape[2]
    tm = ROW_TILE

    def body(dh_ref, w_ref, z_ref, o0, o1, o2, l0, l1, l2, qm_ref, kv_ref, mo_ref, bd_ref,
             dz_ref, do0, do1, do2, dl0, dl1, dl2, dqm_ref, dkv_ref, s0, s1):
        @pl.when(pl.program_id(0) == 0)
        def _():
            dkv_ref[...] = jnp.zeros_like(dkv_ref)

        dy = jnp.zeros((tm, nb), F32)
        for j in range(N_DEV):
            dy += _dot_nt(dh_ref[:, j * c:(j + 1) * c], w_ref[j])
        ov, lv = [], []
        for o_ref, l_ref, d in zip((o0, o1, o2), (l0, l1, l2), DILATIONS):
            ov.append(_from_view(s0, o_ref, d))
            lv.append(_from_view(s1, l_ref, d))
        ws, mix = _mix_groups(ov, lv)
        sz, dsz = _silu_parts(z_ref[...])
        dz_ref[:, :gw] = (dy[:, :gw] * mix * dsz[:, :gw]).astype(BF16)
        dz_ref[:, gw:] = (dy[:, gw:] * mo_ref[...] * dsz[:, gw:]).astype(BF16)
        dbr = dy * sz
        dmix = dbr[:, :gw]
        t = dmix * mix
        th = t.astype(BF16)
        tl = (t - th.astype(F32)).astype(BF16)
        rs = _dot(th, bd_ref[...]) + _dot(tl, bd_ref[...])
        for wg_, do_ref, dl_ref, d in zip(ws, (do0, do1, do2), (dl0, dl1, dl2), DILATIONS):
            _to_view(s0, wg_ * dmix, do_ref, d)
            _to_view(s1, wg_ * rs, dl_ref, d)
        _mem_attn_bwd(qm_ref[...], kv_ref, dbr[:, gw:], dqm_ref, dkv_ref)

    vspecs = [_view_rows(gw, d) for d in DILATIONS]
    return _pallas_call(
        body, name="bwd_post_attn", grid=(s // tm,),
        out_shape=[_sds((s, nb), BF16)] + [_sds((s // d, d * gw), BF16) for d in DILATIONS]
                  + [_sds((s // d, d * gw), F32) for d in DILATIONS] + [_sds((s, MEM_WIDTH), BF16), _plain(kv.shape, F32)],
        in_specs=[_rows(D_MODEL), _whole(wg_out.shape), _rows(nb)] + vspecs * 2
                 + [_rows(MEM_WIDTH), _whole(kv.shape), _rows(MEM_WIDTH), _whole(head_ones.shape)],
        out_specs=[_rows(nb)] + vspecs * 2 + [_rows(MEM_WIDTH), _whole(kv.shape)],
        scratch_shapes=[_view_scratch(gw), _view_scratch(gw)],
        compiler_params=_params(1, 48),
    )(dhb, wg_out, z, *os_, *ls_, qm, kv, mo, head_ones)


def _attn_bwd(q, k, v, lse, do, dl, tabs, d, token):
    ln, dw = q.shape
    w = dw // d
    nb = ln // BLOCK
    reps = w // LANES
    two, before = _pair_specs(d, nb, w)
    two_t, _ = _pair_specs(d, nb, LANES)

    def attend(q_ref, l_ref, do_ref, dl_ref, dqs, acck, accv, rows, col0, kk, vv, acc_rows):
        valid = _band_mask(kk.shape[0])
        low = _low_head_lanes()
        pairs, qcols = _head_tiles(w, col0)
        cols = [slice(col0 + h * HEAD_DIM, col0 + h * HEAD_DIM + 1) for h in range(HEADS_PER_GROUP)]
        qhs = [h for qc in qcols for h in _split_pair(q_ref[rows, qc], low)]
        dobs = [h for qc in qcols for h in _split_pair(do_ref[rows, qc], low)]
        k2s = [kk[:, pr] for pr in pairs for _ in range(2)]
        v2s = [vv[:, pr] for pr in pairs for _ in range(2)]
        scs = [jnp.where(valid, _dot_nt(qh, k2), NEG) for qh, k2 in zip(qhs, k2s)]
        dps = [_dot_nt(dob, v2) for dob, v2 in zip(dobs, v2s)]
        ps = [jnp.exp(sc - l_ref[rows, col]) for sc, col in zip(scs, cols)]
        dss = [(p * (dp - dl_ref[rows, col])).astype(BF16) for p, dp, col in zip(ps, dps, cols)]
        pbs = [p.astype(BF16) for p in ps]
        for i, qc in enumerate(qcols):
            a, b = 2 * i, 2 * i + 1
            dqs[rows, qc] = jnp.where(low, _dot(dss[a], k2s[a]), _dot(dss[b], k2s[b])) * SCALE
            acck[acc_rows, qc] += _dot_tn(dss[a], qhs[a]) + _dot_tn(dss[b], qhs[b])
            accv[acc_rows, qc] += _dot_tn(pbs[a], dobs[a]) + _dot_tn(pbs[b], dobs[b])

    def body_streams(_, q_ref, kc_ref, vc_ref, l_ref, do_ref, dl_ref, c_ref, sa_ref, sb_ref,
                     dq_ref, dk_ref, dv_ref, acck, accv, dqs):
        acck[...] = jnp.zeros_like(acck)
        accv[...] = jnp.zeros_like(accv)
        for sb in range(2):
            cols = slice(sb * w, (sb + 1) * w)
            attend(q_ref, l_ref, do_ref, dl_ref, dqs, acck, accv, TOP, sb * w, kc_ref[:, cols], vc_ref[:, cols], TOP)
        tabs2 = [jnp.concatenate([jnp.tile(r[:, sb * LANES:(sb + 1) * LANES], (1, reps)) for sb in range(2)], axis=1)
                 for r in (c_ref, sa_ref, sb_ref)]
        dq_ref[...] = _rope_bwd(dqs[...], *tabs2).astype(BF16)
        dk_ref[...] = _rope_bwd(acck[...], *tabs2).astype(BF16)
        dv_ref[...] = accv[...].astype(BF16)

    def body_blocks(_, q_ref, kp_ref, kc_ref, vp_ref, vc_ref, l_ref, do_ref, dl_ref, cq, saq, sbq, ck, sak, sbk,
                    dq_ref, dk_ref, dv_ref, acck, accv, dqs):
        i = pl.program_id(0) % (nb // 2)

        @pl.when(i == 0)
        def _():
            acck[...] = jnp.zeros_like(acck)
            accv[...] = jnp.zeros_like(accv)

        refs = (q_ref, l_ref, do_ref, dl_ref, dqs, acck, accv)
        pl.when(i == 0)(lambda: attend(*refs, TOP, 0, kc_ref[TOP, :], vc_ref[TOP, :], TOP))
        pl.when(i != 0)(lambda: attend(
            *refs, TOP, 0, jnp.concatenate([kp_ref[...], kc_ref[TOP, :]], axis=0),
            jnp.concatenate([vp_ref[...], vc_ref[TOP, :]], axis=0),
            pl.ds(pl.multiple_of((2 * i - 1) * BLOCK, BLOCK), 2 * BLOCK)))
        attend(*refs, BOTTOM, 0, kc_ref[...], vc_ref[...], pl.ds(pl.multiple_of(2 * i * BLOCK, BLOCK), 2 * BLOCK))
        tq = [jnp.tile(r[...], (1, reps)) for r in (cq, saq, sbq)]
        dq_ref[...] = _rope_bwd(dqs[...], *tq).astype(BF16)

        @pl.when(i == nb // 2 - 1)
        def _():
            for r0 in range(0, nb * BLOCK, 2 * BLOCK):
                rows = slice(r0, r0 + 2 * BLOCK)
                tk = [jnp.tile(r[rows, :], (1, reps)) for r in (ck, sak, sbk)]
                dk_ref[rows, :] = _rope_bwd(acck[rows, :], *tk).astype(BF16)
                dv_ref[rows, :] = accv[rows, :].astype(BF16)

    if nb == 1:
        body = body_streams
        in_specs = [ANY] + [two] * 6 + [two_t] * 3
        args = (token, q, k, v, lse, do, dl, *tabs)
        out_specs = [two, two, two]
        acc_shape = (BLOCK, 2 * w)
    else:
        body = body_blocks
        stream = pl.BlockSpec((nb * BLOCK, w), lambda n: (0, n // (nb // 2)))
        stream_t = pl.BlockSpec((nb * BLOCK, LANES), lambda n: (0, n // (nb // 2)))
        in_specs = [ANY, two, before, two, before, two, two, two, two] + [two_t] * 3 + [stream_t] * 3
        args = (token, q, k, k, v, v, lse, do, dl, *tabs, *tabs)
        out_specs = [two, stream, stream]
        acc_shape = (nb * BLOCK, w)
    return _pallas_call(
        body, name=f"attn_bwd_d{d}", grid=(d * nb // 2,), out_shape=[_sds((ln, dw), BF16)] * 3,
        in_specs=in_specs, out_specs=out_specs,
        scratch_shapes=[pltpu.VMEM(acc_shape, F32), pltpu.VMEM(acc_shape, F32), pltpu.VMEM(two.block_shape, F32)],
        compiler_params=_params(1, 48),
    )(*args)


def _position():
    return lax.axis_index("x"), lax.axis_index("y"), lax.axis_index("c")


def _all_gather(shards, after, name):
    n_a = len(shards)

    def body(*refs):
        x_refs, out_refs = refs[:n_a], refs[n_a + 1:2 * n_a + 1]
        send_sems, recv_sems, local_sems = refs[2 * n_a + 1:]
        x, y, c = _position()
        me, sibling = (x, y, c), (x, y, 1 - c)
        chips = [(1 - x, y), (x, 1 - y), (1 - x, 1 - y)]

        def rows(a, px, py, pc):
            return out_refs[a].at[4 * px + 2 * py + pc]

        def copy(a, k, block, to, own=False):
            return pltpu.make_async_remote_copy(
                src_ref=x_refs[a] if own else rows(a, *block), dst_ref=rows(a, *block),
                send_sem=send_sems.at[a, k], recv_sem=recv_sems.at[a, k], device_id=to, device_id_type=MESH)

        mine = [pltpu.make_async_copy(x_refs[a], rows(a, *me), local_sems.at[a]) for a in range(n_a)]
        for cp in mine:
            cp.start()
        first = []
        for j, chip in enumerate(chips):
            first += [copy(a, 1 + j, me, (*chip, c), own=True) for a in range(n_a)]
        first += [copy(a, 0, me, sibling, own=True) for a in range(n_a)]
        for cp in first:
            cp.start()
        passed = []
        for j, chip in enumerate(chips):
            for a in range(n_a):
                copy(a, 1 + j, (*chip, c), me).wait_recv()
                fwd = copy(a, 4 + j, (*chip, c), sibling)
                fwd.start()
                passed.append(fwd)
        for a in range(n_a):
            copy(a, 0, sibling, me).wait_recv()
        for j, chip in enumerate(chips):
            for a in range(n_a):
                copy(a, 4 + j, (*chip, 1 - c), me).wait_recv()
        for cp in first + passed:
            cp.wait_send()
        for cp in mine:
            cp.wait()

    return _pallas_call(
        body, name=name, out_shape=[_sds((N_DEV,) + t.shape, t.dtype) for t in shards],
        in_specs=[ANY] * (n_a + 1), out_specs=[ANY] * n_a,
        scratch_shapes=[pltpu.SemaphoreType.DMA((n_a, 7)), pltpu.SemaphoreType.DMA((n_a, 7)),
                        pltpu.SemaphoreType.DMA((n_a,))],
    )(*shards, after)


def _all_gather_relay(xs, name):
    def body(x_ref, out_ref, send_sems, recv_sems, local_sem):
        x, y, c = _position()
        me, sibling = (x, y, c), (x, y, 1 - c)
        xn, yn, diag = (1 - x, y, c), (x, 1 - y, c), (1 - x, 1 - y, c)
        src_nb = (x + c * (1 - 2 * x), y + (1 - c) * (1 - 2 * y), c)
        dst_nb = (x + (1 - c) * (1 - 2 * x), y + c * (1 - 2 * y), c)

        def rows(dev):
            return out_ref.at[4 * dev[0] + 2 * dev[1] + dev[2]]

        def copy(k, block, to, own=False):
            return pltpu.make_async_remote_copy(
                src_ref=x_ref if own else rows(block), dst_ref=rows(block),
                send_sem=send_sems.at[k], recv_sem=recv_sems.at[k], device_id=to, device_id_type=MESH)

        mine = pltpu.make_async_copy(x_ref, rows(me), local_sem)
        mine.start()
        first = [copy(1, me, xn, own=True), copy(2, me, yn, own=True), copy(0, me, sibling, own=True)]
        for cp in first:
            cp.start()
        copy(1, xn, me).wait_recv()
        copy(2, yn, me).wait_recv()
        relay = copy(3, src_nb, dst_nb)
        relay.start()
        passed = [copy(4, xn, sibling), copy(5, yn, sibling)]
        for cp in passed:
            cp.start()
        copy(3, diag, me).wait_recv()
        last = copy(6, diag, sibling)
        last.start()
        copy(0, sibling, me).wait_recv()
        for k, blk in ((4, (1 - x, y, 1 - c)), (5, (x, 1 - y, 1 - c)), (6, (1 - x, 1 - y, 1 - c))):
            copy(k, blk, me).wait_recv()
        for cp in first + [relay] + passed + [last]:
            cp.wait_send()
        mine.wait()

    return _pallas_call(
        body, name=name, out_shape=_sds((N_DEV,) + xs.shape, xs.dtype),
        in_specs=[ANY], out_specs=ANY,
        scratch_shapes=[pltpu.SemaphoreType.DMA((7,)), pltpu.SemaphoreType.DMA((7,)), pltpu.SemaphoreType.DMA],
    )(xs)


def _rs_to_sibling(gs):
    n_a = len(gs)

    def body(*refs):
        g_refs, recv_refs = refs[:n_a], refs[n_a:2 * n_a]
        send_sems, recv_sems = refs[2 * n_a:]
        x, y, c = _position()
        copies = []
        for k in range(4):
            for a in range(n_a):
                copies.append(pltpu.make_async_remote_copy(
                    src_ref=g_refs[a].at[2 * k + (1 - c)], dst_ref=recv_refs[a].at[k],
                    send_sem=send_sems.at[a, k], recv_sem=recv_sems.at[a, k],
                    device_id=(x, y, 1 - c), device_id_type=MESH))
        for cp in copies:
            cp.start()
        for cp in copies:
            cp.wait()

    return _pallas_call(
        body, name="rs_to_sibling", out_shape=[_sds((4,) + g.shape[1:], g.dtype) for g in gs],
        in_specs=[ANY] * n_a, out_specs=[ANY] * n_a,
        scratch_shapes=[pltpu.SemaphoreType.DMA((n_a, 4)), pltpu.SemaphoreType.DMA((n_a, 4))],
    )(*gs)


def _rs_to_chips(pbs):
    n_a = len(pbs)

    def body(*refs):
        p_refs, recv_refs = refs[:n_a], refs[n_a:2 * n_a]
        send_sems, recv_sems = refs[2 * n_a:]
        x, y, c = _position()
        chips = [(1 - x, y), (x, 1 - y), (1 - x, 1 - y)]
        copies = []
        for j, (px, py) in enumerate(chips):
            for a in range(n_a):
                copies.append(pltpu.make_async_remote_copy(
                    src_ref=p_refs[a].at[2 * px + py], dst_ref=recv_refs[a].at[j],
                    send_sem=send_sems.at[a, j], recv_sem=recv_sems.at[a, j],
                    device_id=(px, py, c), device_id_type=MESH))
        for cp in copies:
            cp.start()
        for cp in copies:
            cp.wait()

    return _pallas_call(
        body, name="rs_to_chips", out_shape=[_sds((3,) + p.shape[1:], p.dtype) for p in pbs],
        in_specs=[ANY] * n_a, out_specs=[ANY] * n_a,
        scratch_shapes=[pltpu.SemaphoreType.DMA((n_a, 3)), pltpu.SemaphoreType.DMA((n_a, 3))],
    )(*pbs)


HBM_SPEC = pl.BlockSpec(memory_space=pltpu.HBM)
SEM_SPEC = pl.BlockSpec(memory_space=pltpu.SEMAPHORE)
EFFECT = pltpu.SideEffectType.DATAFLOW_SIDE_EFFECTING
def _plan_gather_own(src_refs, land_refs):
    x, y, c = _position()
    me = 4 * x + 2 * y + c
    peers = [(x, y, 1 - c), (1 - x, y, c), (x, 1 - y, c), (1 - x, 1 - y, c)]
    return [(src_refs[a], land_refs[a].at[me], (a, k), peer) for k, peer in enumerate(peers) for a in range(len(src_refs))]


def _plan_gather_pass(src_refs, land_refs):
    x, y, c = _position()
    chips = [(1 - x, y), (x, 1 - y), (1 - x, 1 - y)]
    return [(land_refs[a].at[4 * px + 2 * py + c], land_refs[a].at[4 * px + 2 * py + c], (a, j), (x, y, 1 - c))
            for j, (px, py) in enumerate(chips) for a in range(len(land_refs))]


def _plan_to_sibling(src_refs, land_refs):
    x, y, c = _position()
    return [(src_refs[a].at[2 * k + (1 - c)], land_refs[a].at[k], (a, k), (x, y, 1 - c))
            for k in range(4) for a in range(len(src_refs))]


def _plan_to_chips(src_refs, land_refs):
    x, y, c = _position()
    chips = [(1 - x, y), (x, 1 - y), (1 - x, 1 - y)]
    return [(src_refs[a].at[2 * px + py], land_refs[a].at[j], (a, j), (px, py, c))
            for j, (px, py) in enumerate(chips) for a in range(len(src_refs))]


def _split_start(srcs, lands, plan, n_sem, after, name):
    n_s, n_a = len(srcs), len(lands)
    n_b = n_s + n_a

    def body(*refs):
        src_refs, land_refs = refs[:n_s], refs[n_s:n_b]
        send_sems, recv_sems, token = refs[n_b + 1], refs[n_b + 2], refs[-1]
        for src, dst, (a, k), dev in plan(src_refs, land_refs):
            i = a * n_sem + k
            pltpu.make_async_remote_copy(src_ref=src, dst_ref=dst, send_sem=send_sems.at[i], recv_sem=recv_sems.at[i],
                                         device_id=dev, device_id_type=MESH).start()
        token[...] = jnp.zeros_like(token)

    bufs = list(srcs) + list(lands)
    res = pl.pallas_call(
        body, name=name,
        out_shape=(pltpu.SemaphoreType.DMA((n_a * n_sem,)), pltpu.SemaphoreType.DMA((n_a * n_sem,)),
                   *[pltpu.HBM(t.shape, t.dtype) for t in bufs], _plain((8, LANES), F32)),
        in_specs=[HBM_SPEC] * n_b + [ANY],
        out_specs=(SEM_SPEC, SEM_SPEC, *[HBM_SPEC] * n_b, pl.BlockSpec(memory_space=pltpu.VMEM)),
        input_output_aliases={i: 2 + i for i in range(n_b)},
        compiler_params=pltpu.CompilerParams(has_side_effects=EFFECT),
    )(*[pltpu.with_memory_space_constraint(t, pltpu.HBM) for t in bufs], after)
    return (res[0], res[1], res[2:2 + n_s], res[2 + n_s:2 + n_b]), res[-1]


def _split_wait(started, plan, after, name):
    send_sems, recv_sems, srcs, lands = started
    n_s, n_a = len(srcs), len(lands)
    n_b = n_s + n_a
    n_sem = send_sems.shape[0] // n_a

    def body(*refs):
        src_refs, land_refs = refs[:n_s], refs[n_s:n_b]
        s_sems, r_sems = refs[n_b], refs[n_b + 1]
        for src, dst, (a, k), dev in plan(src_refs, land_refs):
            i = a * n_sem + k
            cp = pltpu.make_async_remote_copy(src_ref=src, dst_ref=dst, send_sem=s_sems.at[i], recv_sem=r_sems.at[i],
                                              device_id=dev, device_id_type=MESH)
            cp.wait_send()
            cp.wait_recv()

    bufs = list(srcs) + list(lands)
    res = pl.pallas_call(
        body, name=name, out_shape=tuple(pltpu.HBM(t.shape, t.dtype) for t in bufs),
        in_specs=[HBM_SPEC] * n_b + [SEM_SPEC, SEM_SPEC, ANY],
        out_specs=tuple([HBM_SPEC] * n_b),
        input_output_aliases={i: i for i in range(n_b)},
        compiler_params=pltpu.CompilerParams(has_side_effects=EFFECT),
    )(*bufs, send_sems, recv_sems, after)
    return res[:n_s], res[n_s:]


def _row_tile(r):
    return ROW_TILE if r % ROW_TILE == 0 else r


def _rs_add_sibling(gp, recv, c_arr, name):
    _, r, l = gp.shape
    tr = r if r <= 4 * ROW_TILE else _row_tile(r)

    def body(c_ref, g_ref, r_ref, pf_ref, pb_ref):
        sm = g_ref[...].astype(F32) + r_ref[...].astype(F32)
        pf_ref[...] = sm
        pb_ref[...] = sm.astype(BF16)

    spec = pl.BlockSpec((None, tr, l), lambda k, i, c: (k, i, 0))
    return _pallas_call(
        body, name=name,
        grid_spec=pltpu.PrefetchScalarGridSpec(
            num_scalar_prefetch=1, grid=(4, r // tr),
            in_specs=[pl.BlockSpec((None, tr, l), lambda k, i, c: (2 * k + c[0], i, 0)), spec],
            out_specs=[spec, spec]),
        out_shape=[_sds((4, r, l), F32), _sds((4, r, l), BF16)], compiler_params=_params(2, 32),
    )(c_arr, gp, recv)


def _adam_update(w, gv, m, v):
    nm = ADAM_B1 * m + (1.0 - ADAM_B1) * gv
    nv = ADAM_B2 * v + (1.0 - ADAM_B2) * (gv * gv)
    m_hat = nm / (1.0 - ADAM_B1 ** ADAM_STEP)
    v_hat = nv / (1.0 - ADAM_B2 ** ADAM_STEP)
    return -ADAM_LR * (m_hat / (jnp.sqrt(v_hat) + ADAM_EPS) + ADAM_WD * w), nm, nv


def _rs_finish_adamw(pf, recv, k_arr, w, m, v, name):
    _, r, l = pf.shape
    tr = _row_tile(r)

    def body(k_ref, p_ref, r_ref, w_ref, m_ref, v_ref, g_ref, d_ref, nm_ref, nv_ref):
        gv = ((p_ref[...] + r_ref[0].astype(F32)) + r_ref[1].astype(F32)) + r_ref[2].astype(F32)
        g_ref[...] = gv
        d_ref[...], nm_ref[...], nv_ref[...] = _adam_update(w_ref[...], gv, m_ref[...], v_ref[...])

    spec = pl.BlockSpec((tr, l), lambda i, k: (i, 0))
    return _pallas_call(
        body, name=name,
        grid_spec=pltpu.PrefetchScalarGridSpec(
            num_scalar_prefetch=1, grid=(r // tr,),
            in_specs=[pl.BlockSpec((None, tr, l), lambda i, k: (k[0], i, 0)),
                      pl.BlockSpec((3, tr, l), lambda i, k: (0, i, 0)), spec, spec, spec],
            out_specs=[spec] * 4),
        out_shape=[_plain((r, l), F32)] * 4, compiler_params=_params(1, 32),
    )(k_arr, pf, recv, w, m, v)


def _rs_finish_adamw_t(pf, recv, k_arr, w_t, m_t, v_t, name):
    _, r, c = pf.shape
    tr = _row_tile(r)
    cp = -(-c // LANES) * LANES

    def body(k_ref, p_ref, r_ref, w_ref, m_ref, v_ref, g_ref, d_ref, nm_ref, nv_ref, pad):
        gv = ((p_ref[...] + r_ref[0].astype(F32)) + r_ref[1].astype(F32)) + r_ref[2].astype(F32)
        pad[...] = jnp.zeros_like(pad)
        pad[:, 0:c] = gv
        gt = pad[...].T[0:c, :]
        g_ref[...] = gt
        d_ref[...], nm_ref[...], nv_ref[...] = _adam_update(w_ref[...], gt, m_ref[...], v_ref[...])

    spec = pl.BlockSpec((c, tr), lambda i, k: (0, i))
    return _pallas_call(
        body, name=name,
        grid_spec=pltpu.PrefetchScalarGridSpec(
            num_scalar_prefetch=1, grid=(r // tr,),
            in_specs=[pl.BlockSpec((None, tr, c), lambda i, k: (k[0], i, 0)),
                      pl.BlockSpec((3, tr, c), lambda i, k: (0, i, 0)), spec, spec, spec],
            out_specs=[spec] * 4, scratch_shapes=[pltpu.VMEM((tr, cp), F32)]),
        out_shape=[_plain((c, r), F32)] * 4, compiler_params=_params(1, 32),
    )(k_arr, pf, recv, w_t, m_t, v_t)


def _sum_devices(g):
    def body(g_ref, o_ref):
        acc = g_ref[0]
        for j in range(1, N_DEV):
            acc = acc + g_ref[j]
        o_ref[...] = acc

    return _pallas_call(body, name="sum_devices", out_shape=_plain(g.shape[1:], F32))(g)


def _adamw(w, g, m, v, name):
    shape = w.shape
    w2, g2, m2, v2 = [t.reshape((-1, shape[-1])) for t in (w, g, m, v)]

    def body(w_ref, g_ref, m_ref, v_ref, d_ref, nm_ref, nv_ref):
        d_ref[...], nm_ref[...], nv_ref[...] = _adam_update(w_ref[...], g_ref[...], m_ref[...], v_ref[...])

    outs = _pallas_call(body, name=name, out_shape=[_plain(w2.shape, F32)] * 3)(w2, g2, m2, v2)
    return tuple(t.reshape(shape) for t in outs)


def _after(t, token):
    return t + token[0:1, 0:1].astype(t.dtype)


def _finish(name, pf, recv, k_arr, w, m, v):
    if name in ("attn_w_in", "conv_w_in"):
        res = _rs_finish_adamw_t(pf, recv, k_arr, w.T, m.T, v.T, "rs_finish_adamw_" + name)
        return tuple(t.T for t in res)
    return _rs_finish_adamw(pf, recv, k_arr, w, m, v, "rs_finish_adamw_" + name)


def kernel(x, mem, positions, norm_g, mem_norm_g, w_mem_kv, attn_w_in, attn_w_out, conv_w_in, conv_w, conv_w_out, final_g, loss_target, m_norm_g, m_mem_norm_g, m_w_mem_kv, m_attn_w_in, m_attn_w_out, m_conv_w_in, m_conv_w, m_conv_w_out, m_final_g, v_norm_g, v_mem_norm_g, v_w_mem_kv, v_attn_w_in, v_attn_w_out, v_conv_w_in, v_conv_w, v_conv_w_out, v_final_g):
    px, py, pc = _position()
    me = 4 * px + 2 * py + pc
    c_arr = jnp.reshape(pc, (1,)).astype(jnp.int32)
    k_arr = jnp.reshape(2 * px + py, (1,)).astype(jnp.int32)
    x, mem, pos, tgt = x[0], mem[0], positions[0], loss_target[0]

    wg_in0 = _all_gather_relay(attn_w_in[0].astype(BF16), "gather_w_in0")
    late = [attn_w_out[0].astype(BF16), conv_w_in[0].astype(BF16), conv_w_out[0].astype(BF16),
            w_mem_kv.astype(BF16).reshape(-1, w_mem_kv.shape[2]), jnp.pad(conv_w[0], ((0, 5), (0, 0)))]
    lands = [lax.dynamic_update_slice(lax.empty((N_DEV,) + t.shape, t.dtype), t[None], (me, 0, 0)) for t in late]
    late_weights, late_token = _split_start(late, lands, _plan_gather_own, 4, wg_in0, "gather_late_start")

    tabs = _rope_tables(pos)
    g0, g1 = _after(norm_g[0:1], late_token), norm_g[1:2]

    hn0, hn0_t, qs, ks, vs, tabs_v, qm0, z0 = _inproj_attn(x, g0, wg_in0, tabs)
    os_, ls_ = [], []
    for j, d in enumerate(DILATIONS):
        if j == 2:
            _, lands = _split_wait(late_weights, _plan_gather_own, ls_[1], "gather_late_wait")
            late_weights, late_token = _split_start([], lands, _plan_gather_pass, 3, ls_[1], "gather_late_pass_start")
        o, l = _attn_fwd(qs[j], ks[j], vs[j], d, late_token)
        os_.append(o)
        ls_.append(l)

    _, gathered = _split_wait(late_weights, _plan_gather_pass, ls_[2], "gather_late_pass_wait")
    wg_out0, wg_in1, wg_out1, wg_kv, cw_all = gathered
    w_out1 = wg_out1.reshape(-1, wg_out1.shape[2])
    n_kv = w_mem_kv.shape[1]
    w_kv = wg_kv.reshape(N_DEV, 2, n_kv, -1).transpose(1, 0, 2, 3).reshape(2, N_DEV * n_kv, -1)
    cw = cw_all[:, 0:3].transpose(1, 0, 2).reshape(3, -1)
    kv = _memkv_fwd(mem, mem_norm_g, w_kv)
    y0, y0_t, mo0, h1 = _post_attn(os_, ls_, qm0, kv[0], z0, x, wg_out0)

    hn1, hn1_t, bg, cg, u, qm1, z1 = _inproj_conv(h1, g1, wg_in1)
    y1, y1_t, mo1, dh2, dh2b, loss_acc, d_final_g = _post_conv_loss(
        bg, cg, u, qm1, kv[1], z1, h1, w_out1, cw, final_g.reshape(1, -1), tgt)

    d_w_out1 = _wgrad_rows(y1_t, dh2b, "wgrad_out1")
    dz1, dbg, dconv, dqm1, dkv1 = _bwd_post_conv(dh2b, w_out1, bg, cg, u, z1, qm1, kv[1], mo1, cw)
    dcg, du, dcw = _bwd_conv(dconv, cg, u, cw)
    dh1, dh1b, dg1, dproj1 = _dgrad_norm([(dbg, 1), (dcg, 1), (du, 1), (dqm1, 1), (dz1, 1)], wg_in1, h1, g1, dh2,
                                         "dgrad_norm_conv")
    d_w_in1 = _wgrad_shards(hn1_t, dproj1, "wgrad_in1")

    d_w_out0 = _wgrad_cols(y0_t, dh1b, wg_out0.shape[2], "wgrad_out0")

    names1 = ["conv_w_in", "conv_w_out", "attn_w_out"]
    grads1 = [d_w_in1, d_w_out1, d_w_out0]
    started, token = _split_start(grads1, [lax.empty((4,) + g.shape[1:], g.dtype) for g in grads1],
                                  _plan_to_sibling, 4, dg1, "rs1_sibling_start")

    gw = GROUP_WIDTH
    ones = (jnp.arange(gw)[:, None] // HEAD_DIM == jnp.arange(gw)[None, :] // HEAD_DIM).astype(BF16)
    ones = _after(ones, token)
    res = _bwd_post_attn(dh1b, wg_out0, z0, os_, ls_, qm0, kv[0], mo0, ones)
    dz0, dos, dls, dqm0, dkv0 = res[0], res[1:4], res[4:7], res[7], res[8]

    grads1, from_sibling = _split_wait(started, _plan_to_sibling, dz0, "rs1_sibling_wait")
    parts1 = [_rs_add_sibling(g, r, c_arr, "rs_add_sibling_" + n) for g, r, n in zip(grads1, from_sibling, names1)]
    pbs1 = [pb for _, pb in parts1]
    started, token = _split_start(pbs1, [lax.empty((3,) + p.shape[1:], p.dtype) for p in pbs1],
                                  _plan_to_chips, 3, dg1, "rs1_chips_start")

    dqs, dks, dvs = [], [], []
    for j, d in enumerate(DILATIONS):
        dq, dk, dv = _attn_bwd(qs[j], ks[j], vs[j], ls_[j], dos[j], dls[j], tabs_v[j], d, token)
        dqs.append((dq, d))
        dks.append((dk, d))
        dvs.append((dv, d))
    d_w_kv, d_mem_g = _memkv_bwd(jnp.stack([dkv0, dkv1]), w_kv, mem, mem_norm_g)
    dproj0 = _assemble_dproj(dqs + dks + dvs + [(dqm0, 1), (dz0, 1)], wg_in0.shape[2], "assemble_dproj_attn")
    d_w_in0 = _wgrad_shards(hn0_t, dproj0, "wgrad_in0")

    names0 = ["attn_w_in", "w_mem_kv"]
    grads0 = [d_w_in0, d_w_kv]
    started0, token0 = _split_start(grads0, [lax.empty((4,) + g.shape[1:], g.dtype) for g in grads0],
                                    _plan_to_sibling, 4, dg1, "rs0_sibling_start")
    _, from_chips1 = _split_wait(started, _plan_to_chips, token0, "rs1_chips_wait")
    shard = dict(attn_w_in=(attn_w_in[0], m_attn_w_in[0], v_attn_w_in[0]),
                 attn_w_out=(attn_w_out[0], m_attn_w_out[0], v_attn_w_out[0]),
                 conv_w_in=(conv_w_in[0], m_conv_w_in[0], v_conv_w_in[0]),
                 conv_w_out=(conv_w_out[0], m_conv_w_out[0], v_conv_w_out[0]),
                 w_mem_kv=tuple(t.reshape(-1, t.shape[2]) for t in (w_mem_kv, m_w_mem_kv, v_w_mem_kv)))
    big = {}
    for n, (pf, _), r in zip(names1, parts1, from_chips1):
        big[n] = _finish(n, pf, r, k_arr, *shard[n])

    grads0, from_sibling = _split_wait(started0, _plan_to_sibling, big["conv_w_out"][1], "rs0_sibling_wait")
    parts0 = [_rs_add_sibling(g, r, c_arr, "rs_add_sibling_" + n) for g, r, n in zip(grads0, from_sibling, names0)]
    pbs0 = [pb for _, pb in parts0]
    started0, token0 = _split_start(pbs0, [lax.empty((3,) + p.shape[1:], p.dtype) for p in pbs0],
                                    _plan_to_chips, 3, dg1, "rs0_chips_start")
    dx, dg0 = _dgrad_norm_dm(dproj0, wg_in0, x, g0, dh1, token0, "dgrad_norm_attn")

    small_part = jnp.concatenate([dg0, dg1, d_mem_g.reshape(2, -1), d_final_g, dcw[0:3]], axis=0)
    small_part = jnp.concatenate([small_part, jnp.broadcast_to(loss_acc[0, 0], small_part.shape)], axis=0)
    small = _sum_devices(_all_gather([small_part], dg0, "gather_small_grads")[0])
    loss = small[8, 0]
    g_conv_w = lax.dynamic_slice(small[5:8], (0, me * LANES), (3, LANES))[None]
    small_g = dict(norm_g=small[0:2], mem_norm_g=small[2:4], conv_w=g_conv_w, final_g=small[4])
    small_w = dict(norm_g=(norm_g, m_norm_g, v_norm_g), mem_norm_g=(mem_norm_g, m_mem_norm_g, v_mem_norm_g),
                   conv_w=(conv_w, m_conv_w, v_conv_w), final_g=(final_g, m_final_g, v_final_g))
    for n, (w, m, v) in small_w.items():
        big[n] = (small_g[n],) + _adamw(w, small_g[n], m, v, "adamw_" + n)

    _, from_chips0 = _split_wait(started0, _plan_to_chips, big["final_g"][1], "rs0_chips_wait")
    for n, (pf, _), r in zip(names0, parts0, from_chips0):
        big[n] = _finish(n, pf, r, k_arr, *shard[n])
    for n in ("attn_w_in", "attn_w_out", "conv_w_in", "conv_w_out"):
        big[n] = tuple(t[None] for t in big[n])
    big["w_mem_kv"] = tuple(t.reshape(w_mem_kv.shape) for t in big["w_mem_kv"])

    order = ["norm_g", "mem_norm_g", "w_mem_kv", "attn_w_in", "attn_w_out", "conv_w_in", "conv_w", "conv_w_out", "final_g"]
    return (loss, dx[None], *[big[n][0] for n in order], *[big[n][1] for n in order],
            *[big[n][2] for n in order], *[big[n][3] for n in order])
```

```python
import jax
import jax.numpy as jnp
from jax import lax
from jax.experimental import pallas as pl
from jax.experimental.pallas import tpu as pltpu

F32 = jnp.float32
BF16 = jnp.bfloat16

N_DEV = 8
D_MODEL = 1024
HEAD_DIM = 64
ROT_DIM = HEAD_DIM // 4
ROPE_THETA = 500000.0
DILATIONS = (1, 4, 16)
HEADS_PER_GROUP = 8
GROUP_WIDTH = HEADS_PER_GROUP * HEAD_DIM
BLOCK = 128
N_MEM = 256
MEM_HEADS = 4
MEM_WIDTH = MEM_HEADS * HEAD_DIM
CONV_WIDTH = D_MODEL
EPS = 1e-6
SCALE = HEAD_DIM ** -0.5
NEG = -1e30

ADAM_LR = 0.001
ADAM_B1 = 0.9
ADAM_B2 = 0.999
ADAM_EPS = 1e-08
ADAM_WD = 0.01
ADAM_STEP = 10

ROW_TILE = 256
LANES = 128
MESH = pl.DeviceIdType.MESH
ANY = pl.BlockSpec(memory_space=pl.ANY)


def _pallas_call(body, **kw):
    call = pl.pallas_call(body, **kw)

    def run(*args):
        pinned = [pltpu.with_memory_space_constraint(a, pltpu.HBM) if jnp.issubdtype(a.dtype, jnp.floating) else a
                  for a in args]
        return call(*pinned)

    return run


def _dot(a, b):
    return lax.dot_general(a, b, (((1,), (0,)), ((), ())), preferred_element_type=F32)


def _dot_nt(a, b):
    return lax.dot_general(a, b, (((1,), (1,)), ((), ())), preferred_element_type=F32)


def _dot_tn(a, b):
    return lax.dot_general(a, b, (((0,), (0,)), ((), ())), preferred_element_type=F32)


def _params(n_grid, vmem_mb=48):
    return pltpu.CompilerParams(dimension_semantics=("arbitrary",) * n_grid, vmem_limit_bytes=vmem_mb << 20)


def _rows(width, tm=ROW_TILE):
    return pl.BlockSpec((tm, width), lambda i: (i, 0))


def _view_rows(width, d, tm=ROW_TILE):
    return pl.BlockSpec((tm // d, d * width), lambda i: (i, 0))


def _whole(shape):
    return pl.BlockSpec(shape, lambda *_: (0,) * len(shape))


def _resident(shape):
    return pl.BlockSpec(shape, lambda *_: (0,) * len(shape), pipeline_mode=pl.Buffered(1))


def _sds(shape, dtype):
    return pltpu.HBM(shape, dtype)


def _plain(shape, dtype):
    return jax.ShapeDtypeStruct(shape, dtype)


def _silu_parts(z):
    sg = jax.nn.sigmoid(z)
    return z * sg, sg * (1.0 + z * (1.0 - sg))


def _to_view(scr, val, out_ref, d):
    tm, w = val.shape
    if d == 1:
        out_ref[...] = val.astype(out_ref.dtype)
        return
    for cb in range(w // LANES):
        scr[cb] = val[:, cb * LANES:(cb + 1) * LANES]
    for r in range(d):
        for cb in range(w // LANES):
            lo = r * w + cb * LANES
            out_ref[:, lo:lo + LANES] = scr[cb, pl.ds(r, tm // d, stride=d), :].astype(out_ref.dtype)


def _from_view(scr, in_ref, d):
    if d == 1:
        return in_ref[...].astype(F32)
    nc, tm, _ = scr.shape
    w = nc * LANES
    for r in range(d):
        for cb in range(nc):
            lo = r * w + cb * LANES
            scr[cb, pl.ds(r, tm // d, stride=d), :] = in_ref[:, lo:lo + LANES].astype(F32)
    return jnp.concatenate([scr[cb] for cb in range(nc)], axis=1)


def _view_scratch(width, tm=ROW_TILE):
    return pltpu.VMEM((width // LANES, tm, LANES), F32)


def _rope_tables(pos):
    half = ROT_DIM // 2
    inv_freq = ROPE_THETA ** (-jnp.arange(half, dtype=F32) * (2.0 / ROT_DIM))
    ang = pos.astype(F32)[:, None] * inv_freq
    cos, sin = jnp.cos(ang), jnp.sin(ang)
    s = pos.shape[0]
    z8 = jnp.zeros((s, half), F32)
    rest = HEAD_DIM - ROT_DIM
    cosf = jnp.concatenate([cos, cos, jnp.ones((s, rest), F32)], axis=1)
    sa = jnp.concatenate([-sin, z8, jnp.zeros((s, rest), F32)], axis=1)
    sb = jnp.concatenate([z8, sin, jnp.zeros((s, rest), F32)], axis=1)
    return tuple(jnp.tile(t, (1, LANES // HEAD_DIM)) for t in (cosf, sa, sb))


def _rope_fwd(t, cv, sav, sbv):
    w = t.shape[1]
    return t * cv + pltpu.roll(t, w - ROT_DIM // 2, 1) * sav + pltpu.roll(t, ROT_DIM // 2, 1) * sbv


def _rope_bwd(g, cv, sav, sbv):
    w = g.shape[1]
    return g * cv + pltpu.roll(g * sav, ROT_DIM // 2, 1) + pltpu.roll(g * sbv, w - ROT_DIM // 2, 1)


def _joined_columns(wg_ref):
    assert wg_ref.shape[2] % LANES == 0
    return jnp.concatenate([wg_ref[j] for j in range(N_DEV)], axis=1)


def _project(hn, wg_ref, proj_scr):
    c = wg_ref.shape[2]
    for j in range(N_DEV):
        proj_scr[:, j * c:(j + 1) * c] = _dot(hn, wg_ref[j])


def _inproj_attn(x, g, wg, tabs):
    s, d_model = x.shape
    gw = GROUP_WIDTH
    n = N_DEV * wg.shape[2]
    nz = n - 9 * gw - MEM_WIDTH
    reps = gw // LANES
    tm = ROW_TILE

    def body(x_ref, g_ref, w_ref, c_ref, sa_ref, sb_ref, hn_ref, hnt_ref, *rest):
        outs, (proj, scr, tscr) = rest[:-3], rest[-3:]
        q_refs, k_refs, v_refs, t_refs, qm_ref, z_ref = outs[0:3], outs[3:6], outs[6:9], outs[9:18], outs[18], outs[19]
        xb = x_ref[...]
        r = lax.rsqrt(jnp.mean(xb * xb, axis=-1, keepdims=True) + EPS)
        hn = ((xb * r) * g_ref[...]).astype(BF16)
        hn_ref[...] = hn
        hnt_ref[...] = hn.T
        _project(hn, w_ref, proj)
        tab = (c_ref[...], sa_ref[...], sb_ref[...])
        cv, sav, sbv = [jnp.tile(t, (1, reps)) for t in tab]
        for j, d in enumerate(DILATIONS):
            tq = _rope_fwd(proj[:, j * gw:(j + 1) * gw], cv, sav, sbv)
            _to_view(scr, tq * SCALE, q_refs[j], d)
            tk = _rope_fwd(proj[:, (3 + j) * gw:(4 + j) * gw], cv, sav, sbv)
            _to_view(scr, tk, k_refs[j], d)
            _to_view(scr, proj[:, (6 + j) * gw:(7 + j) * gw], v_refs[j], d)
            for i in range(3):
                _to_view(tscr, tab[i], t_refs[3 * j + i], d)
        qm_ref[...] = proj[:, 9 * gw:9 * gw + MEM_WIDTH].astype(BF16)
        z_ref[...] = proj[:, 9 * gw + MEM_WIDTH:]

    views = [_sds((s // d, d * gw), BF16) for d in DILATIONS]
    tviews = [_sds((s // d, d * LANES), F32) for d in DILATIONS for _ in range(3)]
    out_shape = ([_sds((s, d_model), BF16), _sds((d_model, s), BF16)] + views * 3 + tviews
                 + [_sds((s, MEM_WIDTH), BF16), _sds((s, nz), F32)])
    vspecs = [_view_rows(gw, d, tm) for d in DILATIONS]
    tspecs = [_view_rows(LANES, d, tm) for d in DILATIONS for _ in range(3)]
    out_specs = ([_rows(d_model, tm), pl.BlockSpec((d_model, tm), lambda i: (0, i))] + vspecs * 3 + tspecs
                 + [_rows(MEM_WIDTH, tm), _rows(nz, tm)])
    res = _pallas_call(
        body, name="inproj_attn", grid=(s // tm,), out_shape=out_shape,
        in_specs=[_rows(d_model, tm), _whole((1, d_model)), _resident(wg.shape)] + [_rows(LANES, tm)] * 3,
        out_specs=out_specs,
        scratch_shapes=[pltpu.VMEM((tm, n), F32), _view_scratch(gw, tm), _view_scratch(LANES, tm)],
        compiler_params=_params(1, 60),
    )(x, g, wg, *tabs)
    tabs_v = [res[11 + 3 * j:14 + 3 * j] for j in range(3)]
    return res[0], res[1], res[2:5], res[5:8], res[8:11], tabs_v, res[20], res[21]


def _band_mask(n_keys):
    qi = lax.broadcasted_iota(jnp.int32, (BLOCK, n_keys), 0)
    kj = lax.broadcasted_iota(jnp.int32, (BLOCK, n_keys), 1)
    if n_keys == BLOCK:
        return kj <= qi
    return jnp.logical_or(jnp.logical_and(kj < BLOCK, kj >= qi), jnp.logical_and(kj >= BLOCK, (kj - BLOCK) <= qi))


def _low_head_lanes():
    return lax.broadcasted_iota(jnp.int32, (1, LANES), 1) < HEAD_DIM


def _split_pair(t, low):
    zero = jnp.zeros_like(t)
    return jnp.where(low, t, zero), jnp.where(low, zero, t)


def _pair_specs(d, nb, w):
    if nb == 1:
        return pl.BlockSpec((BLOCK, 2 * w), lambda n: (0, n)), None
    half = nb // 2
    two = pl.BlockSpec((2 * BLOCK, w), lambda n: (n % half, n // half))
    before = pl.BlockSpec((BLOCK, w), lambda n: (jnp.maximum(2 * (n % half) - 1, 0), n // half))
    return two, before


def _head_tiles(w, col0):
    return ([slice(p * LANES, (p + 1) * LANES) for p in range(w // LANES)],
            [slice(col0 + p * LANES, col0 + (p + 1) * LANES) for p in range(w // LANES)])


def _attend_fwd(q_ref, o_ref, lse_ref, rows, col0, kk, vv):
    w = kk.shape[1]
    valid = _band_mask(kk.shape[0])
    low = _low_head_lanes()
    pairs, qcols = _head_tiles(w, col0)
    qs_ = [h for qc in qcols for h in _split_pair(q_ref[rows, qc], low)]
    k2s = [kk[:, pr] for pr in pairs for _ in range(2)]
    scs = [jnp.where(valid, _dot_nt(qh, k2), NEG) for qh, k2 in zip(qs_, k2s)]
    ms = [jnp.max(sc, axis=-1, keepdims=True) for sc in scs]
    ps = [jnp.exp(sc - m) for sc, m in zip(scs, ms)]
    ls = [jnp.sum(p, axis=-1, keepdims=True) for p in ps]
    pns = [(p * (1.0 / l)).astype(BF16) for p, l in zip(ps, ls)]
    for i, (pr, qc) in enumerate(zip(pairs, qcols)):
        v2 = vv[:, pr]
        a, b = 2 * i, 2 * i + 1
        o_ref[rows, qc] = jnp.where(low, _dot(pns[a], v2), _dot(pns[b], v2))
        lse_ref[rows, qc] = jnp.where(low, ms[a] + jnp.log(ls[a]), ms[b] + jnp.log(ls[b]))


TOP, BOTTOM = slice(0, BLOCK), slice(BLOCK, 2 * BLOCK)


def _attn_fwd(q, k, v, d, token):
    ln, dw = q.shape
    w = dw // d
    nb = ln // BLOCK
    two, before = _pair_specs(d, nb, w)

    def body_streams(_, q_ref, kc_ref, vc_ref, o_ref, lse_ref):
        for sb in range(2):
            cols = slice(sb * w, (sb + 1) * w)
            _attend_fwd(q_ref, o_ref, lse_ref, TOP, sb * w, kc_ref[:, cols], vc_ref[:, cols])

    def body_blocks(_, q_ref, kp_ref, kc_ref, vp_ref, vc_ref, o_ref, lse_ref):
        first = pl.program_id(0) % (nb // 2) == 0
        pl.when(first)(lambda: _attend_fwd(q_ref, o_ref, lse_ref, TOP, 0, kc_ref[TOP, :], vc_ref[TOP, :]))
        pl.when(jnp.logical_not(first))(lambda: _attend_fwd(
            q_ref, o_ref, lse_ref, TOP, 0, jnp.concatenate([kp_ref[...], kc_ref[TOP, :]], axis=0),
            jnp.concatenate([vp_ref[...], vc_ref[TOP, :]], axis=0)))
        _attend_fwd(q_ref, o_ref, lse_ref, BOTTOM, 0, kc_ref[...], vc_ref[...])

    if nb == 1:
        body, in_specs, args = body_streams, [ANY, two, two, two], (token, q, k, v)
    else:
        body, in_specs, args = body_blocks, [ANY, two, before, two, before, two], (token, q, k, k, v, v)
    return _pallas_call(
        body, name=f"attn_fwd_d{d}", grid=(d * nb // 2,), out_shape=[_sds((ln, dw), F32)] * 2,
        in_specs=in_specs, out_specs=[two, two], compiler_params=_params(1, 32),
    )(*args)


def _memkv_fwd(mem, g, w):
    n_layers = w.shape[0]

    def body(mem_ref, g_ref, w_ref, kv_ref):
        mb = mem_ref[...]
        r = lax.rsqrt(jnp.mean(mb * mb, axis=-1, keepdims=True) + EPS)
        mn = ((mb * r) * g_ref[...]).astype(BF16)
        kv_ref[...] = _dot(mn, w_ref[...]).astype(BF16)

    return _pallas_call(
        body, name="memkv_fwd", grid=(n_layers,),
        out_shape=_plain((n_layers, N_MEM, 2 * MEM_WIDTH), BF16),
        in_specs=[_whole(mem.shape), pl.BlockSpec((None, 1, D_MODEL), lambda l: (l, 0, 0)),
                  pl.BlockSpec((None, D_MODEL, 2 * MEM_WIDTH), lambda l: (l, 0, 0))],
        out_specs=pl.BlockSpec((None, N_MEM, 2 * MEM_WIDTH), lambda l: (l, 0, 0)),
        compiler_params=_params(1, 32),
    )(mem, g.reshape(n_layers, 1, D_MODEL), w)


def _mix_groups(os_, ls_):
    mx = jnp.maximum(jnp.maximum(ls_[0], ls_[1]), ls_[2])
    es = [jnp.exp(t - mx) for t in ls_]
    inv = 1.0 / (es[0] + es[1] + es[2])
    ws = [e * inv for e in es]
    mix = ws[0] * os_[0] + ws[1] * os_[1] + ws[2] * os_[2]
    return ws, mix


MEM_PAIRS = [slice(p * LANES, (p + 1) * LANES) for p in range(MEM_WIDTH // LANES)]


def _mem_probs(qhs, k2s):
    scs = [_dot_nt(qh, k2) * SCALE for qh, k2 in zip(qhs, k2s)]
    es = [jnp.exp(sc - jnp.max(sc, axis=-1, keepdims=True)) for sc in scs]
    return [e * (1.0 / jnp.sum(e, axis=-1, keepdims=True)) for e in es]


def _mem_attn_into(qm, kv_ref, mo_ref):
    low = _low_head_lanes()
    qhs = [h for pr in MEM_PAIRS for h in _split_pair(qm[:, pr], low)]
    k2s = [kv_ref[:, pr] for pr in MEM_PAIRS for _ in range(2)]
    ps = [p.astype(BF16) for p in _mem_probs(qhs, k2s)]
    for i, pr in enumerate(MEM_PAIRS):
        v2 = kv_ref[:, MEM_WIDTH + i * LANES:MEM_WIDTH + (i + 1) * LANES]
        mo_ref[:, pr] = jnp.where(low, _dot(ps[2 * i], v2), _dot(ps[2 * i + 1], v2))


def _mem_attn_bwd(qm, kv_ref, dmem, dqm_ref, dkv_ref):
    low = _low_head_lanes()
    dmb = dmem.astype(BF16)
    vps = [slice(MEM_WIDTH + i * LANES, MEM_WIDTH + (i + 1) * LANES) for i in range(len(MEM_PAIRS))]
    qhs = [h for pr in MEM_PAIRS for h in _split_pair(qm[:, pr], low)]
    dhs = [h for pr in MEM_PAIRS for h in _split_pair(dmb[:, pr], low)]
    k2s = [kv_ref[:, pr] for pr in MEM_PAIRS for _ in range(2)]
    v2s = [kv_ref[:, vp] for vp in vps for _ in range(2)]
    ps = _mem_probs(qhs, k2s)
    dps = [_dot_nt(dh, v2) for dh, v2 in zip(dhs, v2s)]
    dss = [(p * (dp - jnp.sum(dp * p, axis=-1, keepdims=True)) * SCALE).astype(BF16) for p, dp in zip(ps, dps)]
    pbs = [p.astype(BF16) for p in ps]
    for i, (pr, vp) in enumerate(zip(MEM_PAIRS, vps)):
        a, b = 2 * i, 2 * i + 1
        dqm_ref[:, pr] = jnp.where(low, _dot(dss[a], k2s[a]), _dot(dss[b], k2s[b])).astype(BF16)
        dkv_ref[:, pr] += _dot_tn(dss[a], qhs[a]) + _dot_tn(dss[b], qhs[b])
        dkv_ref[:, vp] += _dot_tn(pbs[a], dhs[a]) + _dot_tn(pbs[b], dhs[b])


def _post_attn(os_, ls_, qm, kv, z, x, wg_out):
    s, d_model = x.shape
    gw = GROUP_WIDTH
    nb = gw + MEM_WIDTH
    tm = ROW_TILE

    def body(o0, o1, o2, l0, l1, l2, qm_ref, kv_ref, z_ref, x_ref, w_ref, y_ref, yt_ref, mo_ref, h_ref, s0, s1):
        ov, lv = [], []
        for o_ref, l_ref, d in zip((o0, o1, o2), (l0, l1, l2), DILATIONS):
            ov.append(_from_view(s0, o_ref, d))
            lv.append(_from_view(s1, l_ref, d))
        _, mix = _mix_groups(ov, lv)
        _mem_attn_into(qm_ref[...], kv_ref, mo_ref)
        sz, _ = _silu_parts(z_ref[...])
        y_ref[:, :gw] = (mix * sz[:, :gw]).astype(BF16)
        y_ref[:, gw:] = (mo_ref[...] * sz[:, gw:]).astype(BF16)
        y = y_ref[...]
        yt_ref[...] = y.T
        h_ref[...] = x_ref[...] + _dot(y, _joined_columns(w_ref))

    vspecs = [_view_rows(gw, d) for d in DILATIONS]
    return _pallas_call(
        body, name="post_attn", grid=(s // tm,),
        out_shape=[_sds((s, nb), BF16), _sds((nb, s), BF16), _sds((s, MEM_WIDTH), F32), _sds((s, d_model), F32)],
        in_specs=vspecs * 2 + [_rows(MEM_WIDTH), _whole(kv.shape), _rows(nb), _rows(d_model), _whole(wg_out.shape)],
        out_specs=[_rows(nb), pl.BlockSpec((nb, tm), lambda i: (0, i)), _rows(MEM_WIDTH), _rows(d_model)],
        scratch_shapes=[_view_scratch(gw), _view_scratch(gw)],
        compiler_params=_params(1, 40),
    )(*os_, *ls_, qm, kv, z, x, wg_out)


def _inproj_conv(x, g, wg):
    s, d_model = x.shape
    c = CONV_WIDTH
    n = N_DEV * wg.shape[2]
    nz = n - 3 * c - MEM_WIDTH
    tm = ROW_TILE

    def body(x_ref, g_ref, w_ref, hn_ref, hnt_ref, bg_ref, cg_ref, u_ref, qm_ref, z_ref, proj):
        xb = x_ref[...]
        r = lax.rsqrt(jnp.mean(xb * xb, axis=-1, keepdims=True) + EPS)
        hn = ((xb * r) * g_ref[...]).astype(BF16)
        hn_ref[...] = hn
        hnt_ref[...] = hn.T
        _project(hn, w_ref, proj)
        bg_ref[...] = proj[:, 0:c]
        cg_ref[...] = proj[:, c:2 * c]
        u_ref[...] = proj[:, 2 * c:3 * c]
        qm_ref[...] = proj[:, 3 * c:3 * c + MEM_WIDTH].astype(BF16)
        z_ref[...] = proj[:, 3 * c + MEM_WIDTH:]

    return _pallas_call(
        body, name="inproj_conv", grid=(s // tm,),
        out_shape=[_sds((s, d_model), BF16), _sds((d_model, s), BF16)] + [_sds((s, c), F32)] * 3
                  + [_sds((s, MEM_WIDTH), BF16), _sds((s, nz), F32)],
        in_specs=[_rows(d_model), _whole((1, d_model)), _whole(wg.shape)],
        out_specs=[_rows(d_model), pl.BlockSpec((d_model, tm), lambda i: (0, i))] + [_rows(c)] * 3
                  + [_rows(MEM_WIDTH), _rows(nz)],
        scratch_shapes=[pltpu.VMEM((tm, n), F32)],
        compiler_params=_params(1, 60),
    )(x, g, wg)


HALO = 8


def _halo_before(width, tm=ROW_TILE):
    return pl.BlockSpec((HALO, width), lambda i: (jnp.maximum(i * (tm // HALO) - 1, 0), 0))


def _halo_after(width, n_rows, tm=ROW_TILE):
    return pl.BlockSpec((HALO, width), lambda i: (jnp.minimum((i + 1) * (tm // HALO), n_rows // HALO - 1), 0))


def _conv_taps(cg_ref, u_ref, cgh_ref, uh_ref, i):
    a = cg_ref[...] * u_ref[...]
    ah = jnp.where(i > 0, cgh_ref[...] * uh_ref[...], 0.0)
    row = lax.broadcasted_iota(jnp.int32, a.shape, 0)
    a1 = jnp.where(row == 0, ah[HALO - 1:HALO], pltpu.roll(a, 1, 0))
    a2 = jnp.where(row == 0, ah[HALO - 2:HALO - 1], jnp.where(row == 1, ah[HALO - 1:HALO], pltpu.roll(a, 2, 0)))
    return a, a1, a2


def _post_conv_loss(bg, cg, u, qm, kv, z, h1, w_out, cw, gf, tgt):
    s, d = h1.shape
    c = CONV_WIDTH
    nb = c + MEM_WIDTH
    tm = ROW_TILE

    def body(bg_ref, cg_ref, u_ref, cgh_ref, uh_ref, qm_ref, kv_ref, z_ref, h_ref, w_ref, cw_ref, gf_ref, t_ref,
             y_ref, yt_ref, mo_ref, dh_ref, dhb_ref, loss_ref, dgf_ref):
        i = pl.program_id(0)
        a, a1, a2 = _conv_taps(cg_ref, u_ref, cgh_ref, uh_ref, i)
        conv = cw_ref[0:1, :] * a2 + cw_ref[1:2, :] * a1 + cw_ref[2:3, :] * a
        mix = bg_ref[...] * conv
        _mem_attn_into(qm_ref[...], kv_ref, mo_ref)
        sz, _ = _silu_parts(z_ref[...])
        y_ref[:, :c] = (mix * sz[:, :c]).astype(BF16)
        y_ref[:, c:] = (mo_ref[...] * sz[:, c:]).astype(BF16)
        y = y_ref[...]
        yt_ref[...] = y.T
        h2 = h_ref[...] + _dot(y, w_ref[...])
        r = lax.rsqrt(jnp.mean(h2 * h2, axis=-1, keepdims=True) + EPS)
        nh = h2 * r
        gfv = gf_ref[...]
        diff = nh * gfv - t_ref[...]
        dout = diff * (1.0 / d)
        dn = dout * gfv
        dh2 = r * dn - h2 * ((r * r * r) * jnp.mean(dn * h2, axis=-1, keepdims=True))
        dh_ref[...] = dh2
        dhb_ref[...] = dh2.astype(BF16)

        @pl.when(i == 0)
        def _():
            loss_ref[...] = jnp.zeros_like(loss_ref)
            dgf_ref[...] = jnp.zeros_like(dgf_ref)

        loss_ref[...] += 0.5 * jnp.sum(jnp.mean(diff * diff, axis=-1, keepdims=True))
        dgf_ref[...] += jnp.sum(dout * nh, axis=0, keepdims=True)

    return _pallas_call(
        body, name="post_conv_loss", grid=(s // tm,),
        out_shape=[_sds((s, nb), BF16), _sds((nb, s), BF16), _sds((s, MEM_WIDTH), F32), _sds((s, d), F32),
                   _sds((s, d), BF16), _plain((8, LANES), F32), _plain((1, d), F32)],
        in_specs=[_rows(c)] * 3 + [_halo_before(c)] * 2 + [_rows(MEM_WIDTH), _whole(kv.shape), _rows(nb), _rows(d),
                  _whole(w_out.shape), _whole(cw.shape), _whole((1, d)), _rows(d)],
        out_specs=[_rows(nb), pl.BlockSpec((nb, tm), lambda i: (0, i)), _rows(MEM_WIDTH), _rows(d), _rows(d),
                   _whole((8, LANES)), _whole((1, d))],
        compiler_params=_params(1, 48),
    )(bg, cg, u, cg, u, qm, kv, z, h1, w_out, cw, gf, tgt)


def _bwd_post_conv(dhb, w_out, bg, cg, u, z, qm, kv, mo, cw):
    s = dhb.shape[0]
    c = CONV_WIDTH
    nb = c + MEM_WIDTH

    def body(dh_ref, w_ref, bg_ref, cg_ref, u_ref, cgh_ref, uh_ref, z_ref, qm_ref, kv_ref, mo_ref, cw_ref,
             dz_ref, dbg_ref, dc_ref, dqm_ref, dkv_ref):
        i = pl.program_id(0)

        @pl.when(i == 0)
        def _():
            dkv_ref[...] = jnp.zeros_like(dkv_ref)

        dy = _dot_nt(dh_ref[...], w_ref[...])
        sz, dsz = _silu_parts(z_ref[...])
        a, a1, a2 = _conv_taps(cg_ref, u_ref, cgh_ref, uh_ref, i)
        conv = cw_ref[0:1, :] * a2 + cw_ref[1:2, :] * a1 + cw_ref[2:3, :] * a
        bgv = bg_ref[...]
        dz_ref[:, :c] = (dy[:, :c] * (bgv * conv) * dsz[:, :c]).astype(BF16)
        dz_ref[:, c:] = (dy[:, c:] * mo_ref[...] * dsz[:, c:]).astype(BF16)
        dbr = dy * sz
        dmix = dbr[:, :c]
        dbg_ref[...] = (dmix * conv).astype(BF16)
        dc_ref[...] = dmix * bgv
        _mem_attn_bwd(qm_ref[...], kv_ref, dbr[:, c:], dqm_ref, dkv_ref)

    return _pallas_call(
        body, name="bwd_post_conv", grid=(s // ROW_TILE,),
        out_shape=[_sds((s, nb), BF16), _sds((s, c), BF16), _sds((s, c), F32), _sds((s, MEM_WIDTH), BF16),
                   _plain(kv.shape, F32)],
        in_specs=[_rows(D_MODEL), _whole(w_out.shape)] + [_rows(c)] * 3 + [_halo_before(c)] * 2
                 + [_rows(nb), _rows(MEM_WIDTH), _whole(kv.shape), _rows(MEM_WIDTH), _whole(cw.shape)],
        out_specs=[_rows(nb), _rows(c), _rows(c), _rows(MEM_WIDTH), _whole(kv.shape)],
        compiler_params=_params(1, 48),
    )(dhb, w_out, bg, cg, u, cg, u, z, qm, kv, mo, cw)


def _bwd_conv(dconv, cg, u, cw):
    s, c = dconv.shape
    tm = ROW_TILE
    last = s // tm - 1

    def body(dc_ref, dcn_ref, cg_ref, u_ref, cgh_ref, uh_ref, cw_ref, dcg_ref, du_ref, dcw_ref):
        i = pl.program_id(0)

        @pl.when(i == 0)
        def _():
            dcw_ref[...] = jnp.zeros_like(dcw_ref)

        dc = dc_ref[...]
        dcn = jnp.where(i < last, dcn_ref[...], 0.0)
        row = lax.broadcasted_iota(jnp.int32, dc.shape, 0)
        d1 = jnp.where(row == tm - 1, dcn[0:1], pltpu.roll(dc, tm - 1, 0))
        d2 = jnp.where(row == tm - 1, dcn[1:2], jnp.where(row == tm - 2, dcn[0:1], pltpu.roll(dc, tm - 2, 0)))
        da = cw_ref[2:3, :] * dc + cw_ref[1:2, :] * d1 + cw_ref[0:1, :] * d2
        a, a1, a2 = _conv_taps(cg_ref, u_ref, cgh_ref, uh_ref, i)
        dcg_ref[...] = (da * u_ref[...]).astype(BF16)
        du_ref[...] = (da * cg_ref[...]).astype(BF16)
        dcw_ref[0:1, :] += jnp.sum(dc * a2, axis=0, keepdims=True)
        dcw_ref[1:2, :] += jnp.sum(dc * a1, axis=0, keepdims=True)
        dcw_ref[2:3, :] += jnp.sum(dc * a, axis=0, keepdims=True)

    return _pallas_call(
        body, name="bwd_conv", grid=(s // tm,),
        out_shape=[_sds((s, c), BF16), _sds((s, c), BF16), _plain((8, c), F32)],
        in_specs=[_rows(c), _halo_after(c, s), _rows(c), _rows(c), _halo_before(c), _halo_before(c), _whole(cw.shape)],
        out_specs=[_rows(c), _rows(c), _whole((8, c))], compiler_params=_params(1, 40),
    )(dconv, dconv, cg, u, cg, u, cw)


def _dgrad_norm(pieces, wg, h, g, dres, name):
    s, d_model = h.shape
    c = wg.shape[2]
    n = N_DEV * c
    tm = ROW_TILE
    widths = [p.shape[1] // d for p, d in pieces]
    assert sum(widths) == n
    n_p = len(pieces)

    def body(*refs):
        p_refs = refs[:n_p]
        w_ref, h_ref, g_ref, dr_ref, dh_ref, dhb_ref, dg_ref, dpd_ref, dp, scr = refs[n_p:]

        @pl.when(pl.program_id(0) == 0)
        def _():
            dg_ref[...] = jnp.zeros_like(dg_ref)

        off = 0
        for p_ref, (_, d), wd in zip(p_refs, pieces, widths):
            if d == 1:
                dp[:, off:off + wd] = p_ref[...]
            else:
                dp[:, off:off + wd] = _from_view(scr, p_ref, d).astype(BF16)
            off += wd
        dhn = jnp.zeros((tm, d_model), F32)
        for j in range(N_DEV):
            dpj = dp[:, j * c:(j + 1) * c]
            dpd_ref[j] = dpj
            dhn += _dot_nt(dpj, w_ref[j])
        hb = h_ref[...]
        r = lax.rsqrt(jnp.mean(hb * hb, axis=-1, keepdims=True) + EPS)
        dg_ref[...] += jnp.sum(dhn * (hb * r), axis=0, keepdims=True)
        dn = dhn * g_ref[...]
        dh = dr_ref[...] + r * dn - hb * ((r * r * r) * jnp.mean(dn * hb, axis=-1, keepdims=True))
        dh_ref[...] = dh
        dhb_ref[...] = dh.astype(BF16)

    p_specs = [_view_rows(wd, d) for (_, d), wd in zip(pieces, widths)]
    return _pallas_call(
        body, name=name, grid=(s // tm,),
        out_shape=[_plain((s, d_model), F32), _sds((s, d_model), BF16), _plain((1, d_model), F32), _sds((N_DEV, s, c), BF16)],
        in_specs=p_specs + [_whole(wg.shape), _rows(d_model), _whole((1, d_model)), _rows(d_model)],
        out_specs=[_rows(d_model), _rows(d_model), _whole((1, d_model)), pl.BlockSpec((N_DEV, tm, c), lambda i: (0, i, 0))],
        scratch_shapes=[pltpu.VMEM((tm, n), BF16), _view_scratch(GROUP_WIDTH)],
        compiler_params=_params(1, 60),
    )(*[p for p, _ in pieces], wg, h, g, dres)


def _assemble_dproj(pieces, c, name):
    n = N_DEV * c
    tm = ROW_TILE
    widths = [p.shape[1] // d for p, d in pieces]
    assert sum(widths) == n
    s = pieces[0][0].shape[0] * pieces[0][1]
    n_p = len(pieces)

    def body(*refs):
        p_refs, (dpd_ref, dp, scr) = refs[:n_p], refs[n_p:]
        off = 0
        for p_ref, (_, d), wd in zip(p_refs, pieces, widths):
            if d == 1:
                dp[:, off:off + wd] = p_ref[...]
            else:
                dp[:, off:off + wd] = _from_view(scr, p_ref, d).astype(BF16)
            off += wd
        for j in range(N_DEV):
            dpd_ref[j] = dp[:, j * c:(j + 1) * c]

    return _pallas_call(
        body, name=name, grid=(s // tm,), out_shape=_sds((N_DEV, s, c), BF16),
        in_specs=[_view_rows(wd, d) for (_, d), wd in zip(pieces, widths)],
        out_specs=pl.BlockSpec((N_DEV, tm, c), lambda i: (0, i, 0)),
        scratch_shapes=[pltpu.VMEM((tm, n), BF16), _view_scratch(GROUP_WIDTH)],
        compiler_params=_params(1, 40),
    )(*[p for p, _ in pieces])


def _dgrad_norm_dm(dproj_dm, wg, h, g, dres, token, name):
    s, d_model = h.shape
    c = wg.shape[2]
    tm = ROW_TILE

    def body(_, dp_ref, w_ref, h_ref, g_ref, dr_ref, dh_ref, dg_ref):
        @pl.when(pl.program_id(0) == 0)
        def _():
            dg_ref[...] = jnp.zeros_like(dg_ref)

        dhn = jnp.zeros((tm, d_model), F32)
        for j in range(N_DEV):
            dhn += _dot_nt(dp_ref[j], w_ref[j])
        hb = h_ref[...]
        r = lax.rsqrt(jnp.mean(hb * hb, axis=-1, keepdims=True) + EPS)
        dg_ref[...] += jnp.sum(dhn * (hb * r), axis=0, keepdims=True)
        dn = dhn * g_ref[...]
        dh_ref[...] = dr_ref[...] + r * dn - hb * ((r * r * r) * jnp.mean(dn * hb, axis=-1, keepdims=True))

    return _pallas_call(
        body, name=name, grid=(s // tm,),
        out_shape=[_plain((s, d_model), F32), _plain((1, d_model), F32)],
        in_specs=[ANY, pl.BlockSpec((N_DEV, tm, c), lambda i: (0, i, 0)), _whole(wg.shape), _rows(d_model),
                  _whole((1, d_model)), _rows(d_model)],
        out_specs=[_rows(d_model), _whole((1, d_model))],
        compiler_params=_params(1, 60),
    )(token, dproj_dm, wg, h, g, dres)


def _wgrad_shards(a_t, b_dm, name):
    m, s = a_t.shape
    c = b_dm.shape[2]

    def body(a_ref, b_ref, o_ref):
        o_ref[...] = _dot(a_ref[...], b_ref[...]).astype(BF16)

    return _pallas_call(
        body, name=name, grid=(N_DEV,), out_shape=_sds((N_DEV, m, c), BF16),
        in_specs=[_whole(a_t.shape), pl.BlockSpec((None, s, c), lambda j: (j, 0, 0))],
        out_specs=pl.BlockSpec((None, m, c), lambda j: (j, 0, 0)), compiler_params=_params(1, 40),
    )(a_t, b_dm)


def _wgrad_cols(a_t, b, c, name):
    m, s = a_t.shape

    def body(a_ref, b_ref, o_ref):
        o_ref[...] = _dot(a_ref[...], b_ref[...]).astype(BF16)

    return _pallas_call(
        body, name=name, grid=(N_DEV,), out_shape=_sds((N_DEV, m, c), BF16),
        in_specs=[_whole(a_t.shape), pl.BlockSpec((s, c), lambda j: (0, j))],
        out_specs=pl.BlockSpec((None, m, c), lambda j: (j, 0, 0)), compiler_params=_params(1, 40),
    )(a_t, b)


def _wgrad_rows(a_t, b, name):
    m, s = a_t.shape
    n = b.shape[1]
    mr = m // N_DEV

    def body(a_ref, b_ref, o_ref):
        o_ref[...] = _dot(a_ref[...], b_ref[...]).astype(BF16)

    return _pallas_call(
        body, name=name, grid=(N_DEV,), out_shape=_sds((N_DEV, mr, n), BF16),
        in_specs=[pl.BlockSpec((mr, s), lambda j: (j, 0)), _whole(b.shape)],
        out_specs=pl.BlockSpec((None, mr, n), lambda j: (j, 0, 0)), compiler_params=_params(1, 40),
    )(a_t, b)


def _memkv_bwd(dkv, w, mem, g):
    n_layers = w.shape[0]
    rows = D_MODEL // N_DEV

    def body(dkv_ref, w_ref, mem_ref, g_ref, dw_ref, dg_ref):
        mb = mem_ref[...]
        r = lax.rsqrt(jnp.mean(mb * mb, axis=-1, keepdims=True) + EPS)
        nm = mb * r
        mn = (nm * g_ref[...]).astype(BF16)
        dkvb = dkv_ref[...].astype(BF16)
        dw_ref[...] = _dot_tn(mn, dkvb).astype(BF16).reshape(N_DEV, rows, 2 * MEM_WIDTH)
        dmn = _dot_nt(dkvb, w_ref[...])
        dg_ref[...] = jnp.sum(dmn * nm, axis=0, keepdims=True)

    lay = lambda *shape: pl.BlockSpec((None,) + shape, lambda l: (l, 0, 0))
    return _pallas_call(
        body, name="memkv_bwd", grid=(n_layers,),
        out_shape=[_sds((N_DEV, n_layers * rows, 2 * MEM_WIDTH), BF16), _plain((n_layers, 1, D_MODEL), F32)],
        in_specs=[lay(N_MEM, 2 * MEM_WIDTH), lay(D_MODEL, 2 * MEM_WIDTH), _whole(mem.shape), lay(1, D_MODEL)],
        out_specs=[pl.BlockSpec((N_DEV, rows, 2 * MEM_WIDTH), lambda l: (0, l, 0)), lay(1, D_MODEL)],
        compiler_params=_params(1, 32),
    )(dkv, w, mem, g.reshape(n_layers, 1, D_MODEL))


def _bwd_post_attn(dhb, wg_out, z, os_, ls_, qm, kv, mo, head_ones):
    s = dhb.shape[0]
    gw = GROUP_WIDTH
    nb = gw + MEM_WIDTH
    tm = ROW_TILE

    def body(dh_ref, w_ref, z_ref, o0, o1, o2, l0, l1, l2, qm_ref, kv_ref, mo_ref, bd_ref,
             dz_ref, do0, do1, do2, dl0, dl1, dl2, dqm_ref, dkv_ref, s0, s1):
        @pl.when(pl.program_id(0) == 0)
        def _():
            dkv_ref[...] = jnp.zeros_like(dkv_ref)

        dy = _dot_nt(dh_ref[...], _joined_columns(w_ref))
        ov, lv = [], []
        for o_ref, l_ref, d in zip((o0, o1, o2), (l0, l1, l2), DILATIONS):
            ov.append(_from_view(s0, o_ref, d))
            lv.append(_from_view(s1, l_ref, d))
        ws, mix = _mix_groups(ov, lv)
        sz, dsz = _silu_parts(z_ref[...])
        dz_ref[:, :gw] = (dy[:, :gw] * mix * dsz[:, :gw]).astype(BF16)
        dz_ref[:, gw:] = (dy[:, gw:] * mo_ref[...] * dsz[:, gw:]).astype(BF16)
        dbr = dy * sz
        dmix = dbr[:, :gw]
        t = dmix * mix
        th = t.astype(BF16)
        tl = (t - th.astype(F32)).astype(BF16)
        rs = _dot(th, bd_ref[...]) + _dot(tl, bd_ref[...])
        for wg_, do_ref, dl_ref, d in zip(ws, (do0, do1, do2), (dl0, dl1, dl2), DILATIONS):
            _to_view(s0, wg_ * dmix, do_ref, d)
            _to_view(s1, wg_ * rs, dl_ref, d)
        _mem_attn_bwd(qm_ref[...], kv_ref, dbr[:, gw:], dqm_ref, dkv_ref)

    vspecs = [_view_rows(gw, d) for d in DILATIONS]
    return _pallas_call(
        body, name="bwd_post_attn", grid=(s // tm,),
        out_shape=[_sds((s, nb), BF16)] + [_sds((s // d, d * gw), BF16) for d in DILATIONS]
                  + [_sds((s // d, d * gw), F32) for d in DILATIONS] + [_sds((s, MEM_WIDTH), BF16), _plain(kv.shape, F32)],
        in_specs=[_rows(D_MODEL), _whole(wg_out.shape), _rows(nb)] + vspecs * 2
                 + [_rows(MEM_WIDTH), _whole(kv.shape), _rows(MEM_WIDTH), _whole(head_ones.shape)],
        out_specs=[_rows(nb)] + vspecs * 2 + [_rows(MEM_WIDTH), _whole(kv.shape)],
        scratch_shapes=[_view_scratch(gw), _view_scratch(gw)],
        compiler_params=_params(1, 48),
    )(dhb, wg_out, z, *os_, *ls_, qm, kv, mo, head_ones)


def _attn_bwd(q, k, v, lse, do, dl, tabs, d, token):
    ln, dw = q.shape
    w = dw // d
    nb = ln // BLOCK
    reps = w // LANES
    two, before = _pair_specs(d, nb, w)
    two_t, _ = _pair_specs(d, nb, LANES)

    def attend(q_ref, l_ref, do_ref, dl_ref, dqs, acck, accv, rows, col0, kk, vv, acc_rows):
        valid = _band_mask(kk.shape[0])
        low = _low_head_lanes()
        pairs, qcols = _head_tiles(w, col0)
        cols = [slice(col0 + h * HEAD_DIM, col0 + h * HEAD_DIM + 1) for h in range(HEADS_PER_GROUP)]
        qhs = [h for qc in qcols for h in _split_pair(q_ref[rows, qc], low)]
        dobs = [h for qc in qcols for h in _split_pair(do_ref[rows, qc], low)]
        k2s = [kk[:, pr] for pr in pairs for _ in range(2)]
        v2s = [vv[:, pr] for pr in pairs for _ in range(2)]
        scs = [jnp.where(valid, _dot_nt(qh, k2), NEG) for qh, k2 in zip(qhs, k2s)]
        dps = [_dot_nt(dob, v2) for dob, v2 in zip(dobs, v2s)]
        ps = [jnp.exp(sc - l_ref[rows, col]) for sc, col in zip(scs, cols)]
        dss = [(p * (dp - dl_ref[rows, col])).astype(BF16) for p, dp, col in zip(ps, dps, cols)]
        pbs = [p.astype(BF16) for p in ps]
        for i, qc in enumerate(qcols):
            a, b = 2 * i, 2 * i + 1
            dqs[rows, qc] = jnp.where(low, _dot(dss[a], k2s[a]), _dot(dss[b], k2s[b])) * SCALE
            acck[acc_rows, qc] += _dot_tn(dss[a], qhs[a]) + _dot_tn(dss[b], qhs[b])
            accv[acc_rows, qc] += _dot_tn(pbs[a], dobs[a]) + _dot_tn(pbs[b], dobs[b])

    def body_streams(_, q_ref, kc_ref, vc_ref, l_ref, do_ref, dl_ref, c_ref, sa_ref, sb_ref,
                     dq_ref, dk_ref, dv_ref, acck, accv, dqs):
        acck[...] = jnp.zeros_like(acck)
        accv[...] = jnp.zeros_like(accv)
        for sb in range(2):
            cols = slice(sb * w, (sb + 1) * w)
            attend(q_ref, l_ref, do_ref, dl_ref, dqs, acck, accv, TOP, sb * w, kc_ref[:, cols], vc_ref[:, cols], TOP)
        tabs2 = [jnp.concatenate([jnp.tile(r[:, sb * LANES:(sb + 1) * LANES], (1, reps)) for sb in range(2)], axis=1)
                 for r in (c_ref, sa_ref, sb_ref)]
        dq_ref[...] = _rope_bwd(dqs[...], *tabs2).astype(BF16)
        dk_ref[...] = _rope_bwd(acck[...], *tabs2).astype(BF16)
        dv_ref[...] = accv[...].astype(BF16)

    def body_blocks(_, q_ref, kp_ref, kc_ref, vp_ref, vc_ref, l_ref, do_ref, dl_ref, cq, saq, sbq, ck, sak, sbk,
                    dq_ref, dk_ref, dv_ref, acck, accv, dqs):
        i = pl.program_id(0) % (nb // 2)

        @pl.when(i == 0)
        def _():
            acck[...] = jnp.zeros_like(acck)
            accv[...] = jnp.zeros_like(accv)

        refs = (q_ref, l_ref, do_ref, dl_ref, dqs, acck, accv)
        pl.when(i == 0)(lambda: attend(*refs, TOP, 0, kc_ref[TOP, :], vc_ref[TOP, :], TOP))
        pl.when(i != 0)(lambda: attend(
            *refs, TOP, 0, jnp.concatenate([kp_ref[...], kc_ref[TOP, :]], axis=0),
            jnp.concatenate([vp_ref[...], vc_ref[TOP, :]], axis=0),
            pl.ds(pl.multiple_of((2 * i - 1) * BLOCK, BLOCK), 2 * BLOCK)))
        attend(*refs, BOTTOM, 0, kc_ref[...], vc_ref[...], pl.ds(pl.multiple_of(2 * i * BLOCK, BLOCK), 2 * BLOCK))
        tq = [jnp.tile(r[...], (1, reps)) for r in (cq, saq, sbq)]
        dq_ref[...] = _rope_bwd(dqs[...], *tq).astype(BF16)

        @pl.when(i == nb // 2 - 1)
        def _():
            for r0 in range(0, nb * BLOCK, 2 * BLOCK):
                rows = slice(r0, r0 + 2 * BLOCK)
                tk = [jnp.tile(r[rows, :], (1, reps)) for r in (ck, sak, sbk)]
                dk_ref[rows, :] = _rope_bwd(acck[rows, :], *tk).astype(BF16)
                dv_ref[rows, :] = accv[rows, :].astype(BF16)

    if nb == 1:
        body = body_streams
        in_specs = [ANY] + [two] * 6 + [two_t] * 3
        args = (token, q, k, v, lse, do, dl, *tabs)
        out_specs = [two, two, two]
        acc_shape = (BLOCK, 2 * w)
    else:
        body = body_blocks
        stream = pl.BlockSpec((nb * BLOCK, w), lambda n: (0, n // (nb // 2)))
        stream_t = pl.BlockSpec((nb * BLOCK, LANES), lambda n: (0, n // (nb // 2)))
        in_specs = [ANY, two, before, two, before, two, two, two, two] + [two_t] * 3 + [stream_t] * 3
        args = (token, q, k, k, v, v, lse, do, dl, *tabs, *tabs)
        out_specs = [two, stream, stream]
        acc_shape = (nb * BLOCK, w)
    return _pallas_call(
        body, name=f"attn_bwd_d{d}", grid=(d * nb // 2,), out_shape=[_sds((ln, dw), BF16)] * 3,
        in_specs=in_specs, out_specs=out_specs,
        scratch_shapes=[pltpu.VMEM(acc_shape, F32), pltpu.VMEM(acc_shape, F32), pltpu.VMEM(two.block_shape, F32)],
        compiler_params=_params(1, 48),
    )(*args)


def _position():
    return lax.axis_index("x"), lax.axis_index("y"), lax.axis_index("c")


def _all_gather(shards, after, name):
    n_a = len(shards)

    def body(*refs):
        x_refs, out_refs = refs[:n_a], refs[n_a + 1:2 * n_a + 1]
        send_sems, recv_sems, local_sems = refs[2 * n_a + 1:]
        x, y, c = _position()
        me, sibling = (x, y, c), (x, y, 1 - c)
        chips = [(1 - x, y), (x, 1 - y), (1 - x, 1 - y)]

        def rows(a, px, py, pc):
            return out_refs[a].at[4 * px + 2 * py + pc]

        def copy(a, k, block, to, own=False):
            return pltpu.make_async_remote_copy(
                src_ref=x_refs[a] if own else rows(a, *block), dst_ref=rows(a, *block),
                send_sem=send_sems.at[a, k], recv_sem=recv_sems.at[a, k], device_id=to, device_id_type=MESH)

        mine = [pltpu.make_async_copy(x_refs[a], rows(a, *me), local_sems.at[a]) for a in range(n_a)]
        for cp in mine:
            cp.start()
        first = []
        for j, chip in enumerate(chips):
            first += [copy(a, 1 + j, me, (*chip, c), own=True) for a in range(n_a)]
        first += [copy(a, 0, me, sibling, own=True) for a in range(n_a)]
        for cp in first:
            cp.start()
        passed = []
        for j, chip in enumerate(chips):
            for a in range(n_a):
                copy(a, 1 + j, (*chip, c), me).wait_recv()
                fwd = copy(a, 4 + j, (*chip, c), sibling)
                fwd.start()
                passed.append(fwd)
        for a in range(n_a):
            copy(a, 0, sibling, me).wait_recv()
        for j, chip in enumerate(chips):
            for a in range(n_a):
                copy(a, 4 + j, (*chip, 1 - c), me).wait_recv()
        for cp in first + passed:
            cp.wait_send()
        for cp in mine:
            cp.wait()

    return _pallas_call(
        body, name=name, out_shape=[_sds((N_DEV,) + t.shape, t.dtype) for t in shards],
        in_specs=[ANY] * (n_a + 1), out_specs=[ANY] * n_a,
        scratch_shapes=[pltpu.SemaphoreType.DMA((n_a, 7)), pltpu.SemaphoreType.DMA((n_a, 7)),
                        pltpu.SemaphoreType.DMA((n_a,))],
    )(*shards, after)


def _all_gather_relay(xs, name):
    def body(x_ref, out_ref, send_sems, recv_sems, local_sem):
        x, y, c = _position()
        me, sibling = (x, y, c), (x, y, 1 - c)
        xn, yn, diag = (1 - x, y, c), (x, 1 - y, c), (1 - x, 1 - y, c)
        src_nb = (x + c * (1 - 2 * x), y + (1 - c) * (1 - 2 * y), c)
        dst_nb = (x + (1 - c) * (1 - 2 * x), y + c * (1 - 2 * y), c)

        def rows(dev):
            return out_ref.at[4 * dev[0] + 2 * dev[1] + dev[2]]

        def copy(k, block, to, own=False):
            return pltpu.make_async_remote_copy(
                src_ref=x_ref if own else rows(block), dst_ref=rows(block),
                send_sem=send_sems.at[k], recv_sem=recv_sems.at[k], device_id=to, device_id_type=MESH)

        mine = pltpu.make_async_copy(x_ref, rows(me), local_sem)
        mine.start()
        first = [copy(1, me, xn, own=True), copy(2, me, yn, own=True), copy(0, me, sibling, own=True)]
        for cp in first:
            cp.start()
        copy(1, xn, me).wait_recv()
        copy(2, yn, me).wait_recv()
        relay = copy(3, src_nb, dst_nb)
        relay.start()
        passed = [copy(4, xn, sibling), copy(5, yn, sibling)]
        for cp in passed:
            cp.start()
        copy(3, diag, me).wait_recv()
        last = copy(6, diag, sibling)
        last.start()
        copy(0, sibling, me).wait_recv()
        for k, blk in ((4, (1 - x, y, 1 - c)), (5, (x, 1 - y, 1 - c)), (6, (1 - x, 1 - y, 1 - c))):
            copy(k, blk, me).wait_recv()
        for cp in first + [relay] + passed + [last]:
            cp.wait_send()
        mine.wait()

    return _pallas_call(
        body, name=name, out_shape=_sds((N_DEV,) + xs.shape, xs.dtype),
        in_specs=[ANY], out_specs=ANY,
        scratch_shapes=[pltpu.SemaphoreType.DMA((7,)), pltpu.SemaphoreType.DMA((7,)), pltpu.SemaphoreType.DMA],
    )(xs)


HBM_SPEC = pl.BlockSpec(memory_space=pltpu.HBM)
SEM_SPEC = pl.BlockSpec(memory_space=pltpu.SEMAPHORE)
EFFECT = pltpu.SideEffectType.DATAFLOW_SIDE_EFFECTING
def _plan_gather_own(src_refs, land_refs):
    x, y, c = _position()
    me = 4 * x + 2 * y + c
    peers = [(x, y, 1 - c), (1 - x, y, c), (x, 1 - y, c), (1 - x, 1 - y, c)]
    return [(src_refs[a], land_refs[a].at[me], (a, k), peer) for k, peer in enumerate(peers) for a in range(len(src_refs))]


def _plan_gather_pass(src_refs, land_refs):
    x, y, c = _position()
    chips = [(1 - x, y), (x, 1 - y), (1 - x, 1 - y)]
    return [(land_refs[a].at[4 * px + 2 * py + c], land_refs[a].at[4 * px + 2 * py + c], (a, j), (x, y, 1 - c))
            for j, (px, py) in enumerate(chips) for a in range(len(land_refs))]


def _plan_to_sibling(src_refs, land_refs):
    x, y, c = _position()
    return [(src_refs[a].at[2 * k + (1 - c)], land_refs[a].at[k], (a, k), (x, y, 1 - c))
            for k in range(4) for a in range(len(src_refs))]


def _plan_to_chips(src_refs, land_refs):
    x, y, c = _position()
    chips = [(1 - x, y), (x, 1 - y), (1 - x, 1 - y)]
    return [(src_refs[a].at[2 * px + py], land_refs[a].at[j], (a, j), (px, py, c))
            for j, (px, py) in enumerate(chips) for a in range(len(src_refs))]


def _split_start(srcs, lands, plan, n_sem, after, name):
    n_s, n_a = len(srcs), len(lands)
    n_b = n_s + n_a

    def body(*refs):
        src_refs, land_refs = refs[:n_s], refs[n_s:n_b]
        send_sems, recv_sems, token = refs[n_b + 1], refs[n_b + 2], refs[-1]
        for src, dst, (a, k), dev in plan(src_refs, land_refs):
            i = a * n_sem + k
            pltpu.make_async_remote_copy(src_ref=src, dst_ref=dst, send_sem=send_sems.at[i], recv_sem=recv_sems.at[i],
                                         device_id=dev, device_id_type=MESH).start()
        token[...] = jnp.zeros_like(token)

    bufs = list(srcs) + list(lands)
    res = pl.pallas_call(
        body, name=name,
        out_shape=(pltpu.SemaphoreType.DMA((n_a * n_sem,)), pltpu.SemaphoreType.DMA((n_a * n_sem,)),
                   *[pltpu.HBM(t.shape, t.dtype) for t in bufs], _plain((8, LANES), F32)),
        in_specs=[HBM_SPEC] * n_b + [ANY],
        out_specs=(SEM_SPEC, SEM_SPEC, *[HBM_SPEC] * n_b, pl.BlockSpec(memory_space=pltpu.VMEM)),
        input_output_aliases={i: 2 + i for i in range(n_b)},
        compiler_params=pltpu.CompilerParams(has_side_effects=EFFECT),
    )(*[pltpu.with_memory_space_constraint(t, pltpu.HBM) for t in bufs], after)
    return (res[0], res[1], res[2:2 + n_s], res[2 + n_s:2 + n_b]), res[-1]


def _split_wait(started, plan, after, name):
    send_sems, recv_sems, srcs, lands = started
    n_s, n_a = len(srcs), len(lands)
    n_b = n_s + n_a
    n_sem = send_sems.shape[0] // n_a

    def body(*refs):
        src_refs, land_refs = refs[:n_s], refs[n_s:n_b]
        s_sems, r_sems = refs[n_b], refs[n_b + 1]
        for src, dst, (a, k), dev in plan(src_refs, land_refs):
            i = a * n_sem + k
            cp = pltpu.make_async_remote_copy(src_ref=src, dst_ref=dst, send_sem=s_sems.at[i], recv_sem=r_sems.at[i],
                                              device_id=dev, device_id_type=MESH)
            cp.wait_send()
            cp.wait_recv()

    bufs = list(srcs) + list(lands)
    res = pl.pallas_call(
        body, name=name, out_shape=tuple(pltpu.HBM(t.shape, t.dtype) for t in bufs),
        in_specs=[HBM_SPEC] * n_b + [SEM_SPEC, SEM_SPEC, ANY],
        out_specs=tuple([HBM_SPEC] * n_b),
        input_output_aliases={i: i for i in range(n_b)},
        compiler_params=pltpu.CompilerParams(has_side_effects=EFFECT),
    )(*bufs, send_sems, recv_sems, after)
    return res[:n_s], res[n_s:]


def _row_tile(r):
    return ROW_TILE if r % ROW_TILE == 0 else r


def _rs_add_sibling(gp, recv, c_arr, name):
    _, r, l = gp.shape
    tr = r if r <= 4 * ROW_TILE else _row_tile(r)

    def body(c_ref, g_ref, r_ref, pf_ref, pb_ref):
        sm = g_ref[...].astype(F32) + r_ref[...].astype(F32)
        pf_ref[...] = sm
        pb_ref[...] = sm.astype(BF16)

    spec = pl.BlockSpec((None, tr, l), lambda k, i, c: (k, i, 0))
    return _pallas_call(
        body, name=name,
        grid_spec=pltpu.PrefetchScalarGridSpec(
            num_scalar_prefetch=1, grid=(4, r // tr),
            in_specs=[pl.BlockSpec((None, tr, l), lambda k, i, c: (2 * k + c[0], i, 0)), spec],
            out_specs=[spec, spec]),
        out_shape=[_sds((4, r, l), F32), _sds((4, r, l), BF16)], compiler_params=_params(2, 32),
    )(c_arr, gp, recv)


def _adam_update(w, gv, m, v):
    nm = ADAM_B1 * m + (1.0 - ADAM_B1) * gv
    nv = ADAM_B2 * v + (1.0 - ADAM_B2) * (gv * gv)
    m_hat = nm / (1.0 - ADAM_B1 ** ADAM_STEP)
    v_hat = nv / (1.0 - ADAM_B2 ** ADAM_STEP)
    return -ADAM_LR * (m_hat / (jnp.sqrt(v_hat) + ADAM_EPS) + ADAM_WD * w), nm, nv


def _rs_finish_adamw(pf, recv, k_arr, w, m, v, name):
    _, r, l = pf.shape
    tr = _row_tile(r)

    def body(k_ref, p_ref, r_ref, w_ref, m_ref, v_ref, g_ref, d_ref, nm_ref, nv_ref):
        gv = ((p_ref[...] + r_ref[0].astype(F32)) + r_ref[1].astype(F32)) + r_ref[2].astype(F32)
        g_ref[...] = gv
        d_ref[...], nm_ref[...], nv_ref[...] = _adam_update(w_ref[...], gv, m_ref[...], v_ref[...])

    spec = pl.BlockSpec((tr, l), lambda i, k: (i, 0))
    return _pallas_call(
        body, name=name,
        grid_spec=pltpu.PrefetchScalarGridSpec(
            num_scalar_prefetch=1, grid=(r // tr,),
            in_specs=[pl.BlockSpec((None, tr, l), lambda i, k: (k[0], i, 0)),
                      pl.BlockSpec((3, tr, l), lambda i, k: (0, i, 0)), spec, spec, spec],
            out_specs=[spec] * 4),
        out_shape=[_plain((r, l), F32)] * 4, compiler_params=_params(1, 32),
    )(k_arr, pf, recv, w, m, v)


def _rs_finish_adamw_t(pf, recv, k_arr, w_t, m_t, v_t, name):
    _, r, c = pf.shape
    tr = _row_tile(r)
    cp = -(-c // LANES) * LANES

    def body(k_ref, p_ref, r_ref, w_ref, m_ref, v_ref, g_ref, d_ref, nm_ref, nv_ref, pad):
        gv = ((p_ref[...] + r_ref[0].astype(F32)) + r_ref[1].astype(F32)) + r_ref[2].astype(F32)
        pad[...] = jnp.zeros_like(pad)
        pad[:, 0:c] = gv
        gt = pad[...].T[0:c, :]
        g_ref[...] = gt
        d_ref[...], nm_ref[...], nv_ref[...] = _adam_update(w_ref[...], gt, m_ref[...], v_ref[...])

    spec = pl.BlockSpec((c, tr), lambda i, k: (0, i))
    return _pallas_call(
        body, name=name,
        grid_spec=pltpu.PrefetchScalarGridSpec(
            num_scalar_prefetch=1, grid=(r // tr,),
            in_specs=[pl.BlockSpec((None, tr, c), lambda i, k: (k[0], i, 0)),
                      pl.BlockSpec((3, tr, c), lambda i, k: (0, i, 0)), spec, spec, spec],
            out_specs=[spec] * 4, scratch_shapes=[pltpu.VMEM((tr, cp), F32)]),
        out_shape=[_plain((c, r), F32)] * 4, compiler_params=_params(1, 32),
    )(k_arr, pf, recv, w_t, m_t, v_t)


def _sum_devices(g):
    def body(g_ref, o_ref):
        acc = g_ref[0]
        for j in range(1, N_DEV):
            acc = acc + g_ref[j]
        o_ref[...] = acc

    return _pallas_call(body, name="sum_devices", out_shape=_plain(g.shape[1:], F32))(g)


def _adamw(w, g, m, v, name):
    shape = w.shape
    w2, g2, m2, v2 = [t.reshape((-1, shape[-1])) for t in (w, g, m, v)]

    def body(w_ref, g_ref, m_ref, v_ref, d_ref, nm_ref, nv_ref):
        d_ref[...], nm_ref[...], nv_ref[...] = _adam_update(w_ref[...], g_ref[...], m_ref[...], v_ref[...])

    outs = _pallas_call(body, name=name, out_shape=[_plain(w2.shape, F32)] * 3)(w2, g2, m2, v2)
    return tuple(t.reshape(shape) for t in outs)


def _after(t, token):
    return t + token[0:1, 0:1].astype(t.dtype)


def _finish(name, pf, recv, k_arr, w, m, v):
    if name in ("attn_w_in", "conv_w_in"):
        res = _rs_finish_adamw_t(pf, recv, k_arr, w.T, m.T, v.T, "rs_finish_adamw_" + name)
        return tuple(t.T for t in res)
    return _rs_finish_adamw(pf, recv, k_arr, w, m, v, "rs_finish_adamw_" + name)


def kernel(x, mem, positions, norm_g, mem_norm_g, w_mem_kv, attn_w_in, attn_w_out, conv_w_in, conv_w, conv_w_out, final_g, loss_target, m_norm_g, m_mem_norm_g, m_w_mem_kv, m_attn_w_in, m_attn_w_out, m_conv_w_in, m_conv_w, m_conv_w_out, m_final_g, v_norm_g, v_mem_norm_g, v_w_mem_kv, v_attn_w_in, v_attn_w_out, v_conv_w_in, v_conv_w, v_conv_w_out, v_final_g):
    px, py, pc = _position()
    me = 4 * px + 2 * py + pc
    c_arr = jnp.reshape(pc, (1,)).astype(jnp.int32)
    k_arr = jnp.reshape(2 * px + py, (1,)).astype(jnp.int32)
    x, mem, pos, tgt = x[0], mem[0], positions[0], loss_target[0]

    wg_in0 = _all_gather_relay(attn_w_in[0].astype(BF16), "gather_w_in0")
    late = [attn_w_out[0].astype(BF16), conv_w_in[0].astype(BF16), conv_w_out[0].astype(BF16),
            w_mem_kv.astype(BF16).reshape(-1, w_mem_kv.shape[2]), jnp.pad(conv_w[0], ((0, 5), (0, 0)))]
    lands = [lax.dynamic_update_slice(lax.empty((N_DEV,) + t.shape, t.dtype), t[None], (me, 0, 0)) for t in late]
    late_weights, late_token = _split_start(late, lands, _plan_gather_own, 4, wg_in0, "gather_late_start")

    tabs = _rope_tables(pos)
    g0, g1 = _after(norm_g[0:1], late_token), norm_g[1:2]

    hn0, hn0_t, qs, ks, vs, tabs_v, qm0, z0 = _inproj_attn(x, g0, wg_in0, tabs)
    os_, ls_ = [], []
    for j, d in enumerate(DILATIONS):
        if j == 2:
            _, lands = _split_wait(late_weights, _plan_gather_own, ls_[1], "gather_late_wait")
            late_weights, late_token = _split_start([], lands, _plan_gather_pass, 3, ls_[1], "gather_late_pass_start")
        o, l = _attn_fwd(qs[j], ks[j], vs[j], d, late_token)
        os_.append(o)
        ls_.append(l)

    _, gathered = _split_wait(late_weights, _plan_gather_pass, ls_[2], "gather_late_pass_wait")
    wg_out0, wg_in1, wg_out1, wg_kv, cw_all = gathered
    w_out1 = wg_out1.reshape(-1, wg_out1.shape[2])
    n_kv = w_mem_kv.shape[1]
    w_kv = wg_kv.reshape(N_DEV, 2, n_kv, -1).transpose(1, 0, 2, 3).reshape(2, N_DEV * n_kv, -1)
    cw = cw_all[:, 0:3].transpose(1, 0, 2).reshape(3, -1)
    kv = _memkv_fwd(mem, mem_norm_g, w_kv)
    y0, y0_t, mo0, h1 = _post_attn(os_, ls_, qm0, kv[0], z0, x, wg_out0)

    hn1, hn1_t, bg, cg, u, qm1, z1 = _inproj_conv(h1, g1, wg_in1)
    y1, y1_t, mo1, dh2, dh2b, loss_acc, d_final_g = _post_conv_loss(
        bg, cg, u, qm1, kv[1], z1, h1, w_out1, cw, final_g.reshape(1, -1), tgt)

    d_w_out1 = _wgrad_rows(y1_t, dh2b, "wgrad_out1")
    dz1, dbg, dconv, dqm1, dkv1 = _bwd_post_conv(dh2b, w_out1, bg, cg, u, z1, qm1, kv[1], mo1, cw)
    dcg, du, dcw = _bwd_conv(dconv, cg, u, cw)
    dh1, dh1b, dg1, dproj1 = _dgrad_norm([(dbg, 1), (dcg, 1), (du, 1), (dqm1, 1), (dz1, 1)], wg_in1, h1, g1, dh2,
                                         "dgrad_norm_conv")
    d_w_in1 = _wgrad_shards(hn1_t, dproj1, "wgrad_in1")

    d_w_out0 = _wgrad_cols(y0_t, dh1b, wg_out0.shape[2], "wgrad_out0")

    names1 = ["conv_w_in", "conv_w_out", "attn_w_out"]
    grads1 = [d_w_in1, d_w_out1, d_w_out0]
    started, token = _split_start(grads1, [lax.empty((4,) + g.shape[1:], g.dtype) for g in grads1],
                                  _plan_to_sibling, 4, dg1, "rs1_sibling_start")

    gw = GROUP_WIDTH
    ones = (jnp.arange(gw)[:, None] // HEAD_DIM == jnp.arange(gw)[None, :] // HEAD_DIM).astype(BF16)
    ones = _after(ones, token)
    res = _bwd_post_attn(dh1b, wg_out0, z0, os_, ls_, qm0, kv[0], mo0, ones)
    dz0, dos, dls, dqm0, dkv0 = res[0], res[1:4], res[4:7], res[7], res[8]

    grads1, from_sibling = _split_wait(started, _plan_to_sibling, dz0, "rs1_sibling_wait")
    parts1 = [_rs_add_sibling(g, r, c_arr, "rs_add_sibling_" + n) for g, r, n in zip(grads1, from_sibling, names1)]
    pbs1 = [pb for _, pb in parts1]
    started, token = _split_start(pbs1, [lax.empty((3,) + p.shape[1:], p.dtype) for p in pbs1],
                                  _plan_to_chips, 3, dg1, "rs1_chips_start")

    dqs, dks, dvs = [], [], []
    for j, d in enumerate(DILATIONS):
        dq, dk, dv = _attn_bwd(qs[j], ks[j], vs[j], ls_[j], dos[j], dls[j], tabs_v[j], d, token)
        dqs.append((dq, d))
        dks.append((dk, d))
        dvs.append((dv, d))
    d_w_kv, d_mem_g = _memkv_bwd(jnp.stack([dkv0, dkv1]), w_kv, mem, mem_norm_g)
    dproj0 = _assemble_dproj(dqs + dks + dvs + [(dqm0, 1), (dz0, 1)], wg_in0.shape[2], "assemble_dproj_attn")
    d_w_in0 = _wgrad_shards(hn0_t, dproj0, "wgrad_in0")

    names0 = ["attn_w_in", "w_mem_kv"]
    grads0 = [d_w_in0, d_w_kv]
    started0, token0 = _split_start(grads0, [lax.empty((4,) + g.shape[1:], g.dtype) for g in grads0],
                                    _plan_to_sibling, 4, dg1, "rs0_sibling_start")
    _, from_chips1 = _split_wait(started, _plan_to_chips, token0, "rs1_chips_wait")
    shard = dict(attn_w_in=(attn_w_in[0], m_attn_w_in[0], v_attn_w_in[0]),
                 attn_w_out=(attn_w_out[0], m_attn_w_out[0], v_attn_w_out[0]),
                 conv_w_in=(conv_w_in[0], m_conv_w_in[0], v_conv_w_in[0]),
                 conv_w_out=(conv_w_out[0], m_conv_w_out[0], v_conv_w_out[0]),
                 w_mem_kv=tuple(t.reshape(-1, t.shape[2]) for t in (w_mem_kv, m_w_mem_kv, v_w_mem_kv)))
    big = {}
    for n, (pf, _), r in zip(names1, parts1, from_chips1):
        big[n] = _finish(n, pf, r, k_arr, *shard[n])

    grads0, from_sibling = _split_wait(started0, _plan_to_sibling, big["conv_w_out"][1], "rs0_sibling_wait")
    parts0 = [_rs_add_sibling(g, r, c_arr, "rs_add_sibling_" + n) for g, r, n in zip(grads0, from_sibling, names0)]
    pbs0 = [pb for _, pb in parts0]
    started0, token0 = _split_start(pbs0, [lax.empty((3,) + p.shape[1:], p.dtype) for p in pbs0],
                                    _plan_to_chips, 3, dg1, "rs0_chips_start")
    dx, dg0 = _dgrad_norm_dm(dproj0, wg_in0, x, g0, dh1, token0, "dgrad_norm_attn")

    small_part = jnp.concatenate([dg0, dg1, d_mem_g.reshape(2, -1), d_final_g, dcw[0:3]], axis=0)
    small_part = jnp.concatenate([small_part, jnp.broadcast_to(loss_acc[0, 0], small_part.shape)], axis=0)
    small = _sum_devices(_all_gather([small_part], dg0, "gather_small_grads")[0])
    loss = small[8, 0]
    g_conv_w = lax.dynamic_slice(small[5:8], (0, me * LANES), (3, LANES))[None]
    small_g = dict(norm_g=small[0:2], mem_norm_g=small[2:4], conv_w=g_conv_w, final_g=small[4])
    small_w = dict(norm_g=(norm_g, m_norm_g, v_norm_g), mem_norm_g=(mem_norm_g, m_mem_norm_g, v_mem_norm_g),
                   conv_w=(conv_w, m_conv_w, v_conv_w), final_g=(final_g, m_final_g, v_final_g))
    for n, (w, m, v) in small_w.items():
        big[n] = (small_g[n],) + _adamw(w, small_g[n], m, v, "adamw_" + n)

    _, from_chips0 = _split_wait(started0, _plan_to_chips, big["final_g"][1], "rs0_chips_wait")
    for n, (pf, _), r in zip(names0, parts0, from_chips0):
        big[n] = _finish(n, pf, r, k_arr, *shard[n])
    for n in ("attn_w_in", "attn_w_out", "conv_w_in", "conv_w_out"):
        big[n] = tuple(t[None] for t in big[n])
    big["w_mem_kv"] = tuple(t.reshape(w_mem_kv.shape) for t in big["w_mem_kv"])

    order = ["norm_g", "mem_norm_g", "w_mem_kv", "attn_w_in", "attn_w_out", "conv_w_in", "conv_w", "conv_w_out", "final_g"]
    return (loss, dx[None], *[big[n][0] for n in order], *[big[n][1] for n in order],
            *[big[n][2] for n in order], *[big[n][3] for n in order])
```

```python
import jax
import jax.numpy as jnp
from jax import lax
from jax.experimental import pallas as pl
from jax.experimental.pallas import tpu as pltpu

F32 = jnp.float32
BF16 = jnp.bfloat16

N_DEV = 8
D_MODEL = 1024
HEAD_DIM = 64
ROT_DIM = HEAD_DIM // 4
ROPE_THETA = 500000.0
DILATIONS = (1, 4, 16)
HEADS_PER_GROUP = 8
GROUP_WIDTH = HEADS_PER_GROUP * HEAD_DIM
BLOCK = 128
N_MEM = 256
MEM_HEADS = 4
MEM_WIDTH = MEM_HEADS * HEAD_DIM
CONV_WIDTH = D_MODEL
EPS = 1e-6
SCALE = HEAD_DIM ** -0.5
NEG = -1e30

ADAM_LR = 0.001
ADAM_B1 = 0.9
ADAM_B2 = 0.999
ADAM_EPS = 1e-08
ADAM_WD = 0.01
ADAM_STEP = 10

ROW_TILE = 256
WGRAD_SHARDS = 4
LANES = 128
MESH = pl.DeviceIdType.MESH
ANY = pl.BlockSpec(memory_space=pl.ANY)


def _pallas_call(body, **kw):
    call = pl.pallas_call(body, **kw)

    def run(*args):
        pinned = [pltpu.with_memory_space_constraint(a, pltpu.HBM) if jnp.issubdtype(a.dtype, jnp.floating) else a
                  for a in args]
        return call(*pinned)

    return run


def _dot(a, b):
    return lax.dot_general(a, b, (((1,), (0,)), ((), ())), preferred_element_type=F32)


def _dot_nt(a, b):
    return lax.dot_general(a, b, (((1,), (1,)), ((), ())), preferred_element_type=F32)


def _dot_tn(a, b):
    return lax.dot_general(a, b, (((0,), (0,)), ((), ())), preferred_element_type=F32)


def _params(n_grid, vmem_mb=48):
    return pltpu.CompilerParams(dimension_semantics=("arbitrary",) * n_grid, vmem_limit_bytes=vmem_mb << 20)


def _rows(width, tm=ROW_TILE):
    return pl.BlockSpec((tm, width), lambda i: (i, 0))


def _view_rows(width, d, tm=ROW_TILE):
    return pl.BlockSpec((tm // d, d * width), lambda i: (i, 0))


def _whole(shape):
    return pl.BlockSpec(shape, lambda *_: (0,) * len(shape))


def _resident(shape):
    return pl.BlockSpec(shape, lambda *_: (0,) * len(shape), pipeline_mode=pl.Buffered(1))


def _sds(shape, dtype):
    return pltpu.HBM(shape, dtype)


def _plain(shape, dtype):
    return jax.ShapeDtypeStruct(shape, dtype)


def _silu_parts(z):
    sg = jax.nn.sigmoid(z)
    return z * sg, sg * (1.0 + z * (1.0 - sg))


def _to_view(scr, val, out_ref, d):
    tm, w = val.shape
    if d == 1:
        out_ref[...] = val.astype(out_ref.dtype)
        return
    for cb in range(w // LANES):
        scr[cb] = val[:, cb * LANES:(cb + 1) * LANES]
    for r in range(d):
        for cb in range(w // LANES):
            lo = r * w + cb * LANES
            out_ref[:, lo:lo + LANES] = scr[cb, pl.ds(r, tm // d, stride=d), :].astype(out_ref.dtype)


def _from_view(scr, in_ref, d):
    if d == 1:
        return in_ref[...].astype(F32)
    nc, tm, _ = scr.shape
    w = nc * LANES
    for r in range(d):
        for cb in range(nc):
            lo = r * w + cb * LANES
            scr[cb, pl.ds(r, tm // d, stride=d), :] = in_ref[:, lo:lo + LANES].astype(F32)
    return jnp.concatenate([scr[cb] for cb in range(nc)], axis=1)


def _view_scratch(width, tm=ROW_TILE):
    return pltpu.VMEM((width // LANES, tm, LANES), F32)


def _rope_tables(pos):
    half = ROT_DIM // 2
    inv_freq = ROPE_THETA ** (-jnp.arange(half, dtype=F32) * (2.0 / ROT_DIM))
    ang = pos.astype(F32)[:, None] * inv_freq
    cos, sin = jnp.cos(ang), jnp.sin(ang)
    s = pos.shape[0]
    z8 = jnp.zeros((s, half), F32)
    rest = HEAD_DIM - ROT_DIM
    cosf = jnp.concatenate([cos, cos, jnp.ones((s, rest), F32)], axis=1)
    sa = jnp.concatenate([-sin, z8, jnp.zeros((s, rest), F32)], axis=1)
    sb = jnp.concatenate([z8, sin, jnp.zeros((s, rest), F32)], axis=1)
    return tuple(jnp.tile(t, (1, LANES // HEAD_DIM)) for t in (cosf, sa, sb))


def _rope_fwd(t, cv, sav, sbv):
    w = t.shape[1]
    return t * cv + pltpu.roll(t, w - ROT_DIM // 2, 1) * sav + pltpu.roll(t, ROT_DIM // 2, 1) * sbv


def _rope_bwd(g, cv, sav, sbv):
    w = g.shape[1]
    return g * cv + pltpu.roll(g * sav, ROT_DIM // 2, 1) + pltpu.roll(g * sbv, w - ROT_DIM // 2, 1)


def _joined_columns(wg_ref):
    assert wg_ref.shape[2] % LANES == 0
    return jnp.concatenate([wg_ref[j] for j in range(N_DEV)], axis=1)


def _project(hn, wg_ref, proj_scr):
    c = wg_ref.shape[2]
    for j in range(N_DEV):
        proj_scr[:, j * c:(j + 1) * c] = _dot(hn, wg_ref[j])


def _inproj_attn(x, g, wg, tabs):
    s, d_model = x.shape
    gw = GROUP_WIDTH
    n = N_DEV * wg.shape[2]
    nz = n - 9 * gw - MEM_WIDTH
    reps = gw // LANES
    tm = ROW_TILE

    def body(x_ref, g_ref, w_ref, c_ref, sa_ref, sb_ref, hn_ref, hnt_ref, *rest):
        outs, (proj, scr, tscr) = rest[:-3], rest[-3:]
        q_refs, k_refs, v_refs, t_refs, qm_ref, z_ref = outs[0:3], outs[3:6], outs[6:9], outs[9:18], outs[18], outs[19]
        xb = x_ref[...]
        r = lax.rsqrt(jnp.mean(xb * xb, axis=-1, keepdims=True) + EPS)
        hn = ((xb * r) * g_ref[...]).astype(BF16)
        hn_ref[...] = hn
        hnt_ref[...] = hn.T
        _project(hn, w_ref, proj)
        tab = (c_ref[...], sa_ref[...], sb_ref[...])
        cv, sav, sbv = [jnp.tile(t, (1, reps)) for t in tab]
        for j, d in enumerate(DILATIONS):
            tq = _rope_fwd(proj[:, j * gw:(j + 1) * gw], cv, sav, sbv)
            _to_view(scr, tq * SCALE, q_refs[j], d)
            tk = _rope_fwd(proj[:, (3 + j) * gw:(4 + j) * gw], cv, sav, sbv)
            _to_view(scr, tk, k_refs[j], d)
            _to_view(scr, proj[:, (6 + j) * gw:(7 + j) * gw], v_refs[j], d)
            for i in range(3):
                _to_view(tscr, tab[i], t_refs[3 * j + i], d)
        qm_ref[...] = proj[:, 9 * gw:9 * gw + MEM_WIDTH].astype(BF16)
        z_ref[...] = proj[:, 9 * gw + MEM_WIDTH:]

    views = [_sds((s // d, d * gw), BF16) for d in DILATIONS]
    tviews = [_sds((s // d, d * LANES), F32) for d in DILATIONS for _ in range(3)]
    out_shape = ([_sds((s, d_model), BF16), _sds((d_model, s), BF16)] + views * 3 + tviews
                 + [_sds((s, MEM_WIDTH), BF16), _sds((s, nz), F32)])
    vspecs = [_view_rows(gw, d, tm) for d in DILATIONS]
    tspecs = [_view_rows(LANES, d, tm) for d in DILATIONS for _ in range(3)]
    out_specs = ([_rows(d_model, tm), pl.BlockSpec((d_model, tm), lambda i: (0, i))] + vspecs * 3 + tspecs
                 + [_rows(MEM_WIDTH, tm), _rows(nz, tm)])
    res = _pallas_call(
        body, name="inproj_attn", grid=(s // tm,), out_shape=out_shape,
        in_specs=[_rows(d_model, tm), _whole((1, d_model)), _resident(wg.shape)] + [_rows(LANES, tm)] * 3,
        out_specs=out_specs,
        scratch_shapes=[pltpu.VMEM((tm, n), F32), _view_scratch(gw, tm), _view_scratch(LANES, tm)],
        compiler_params=_params(1, 60),
    )(x, g, wg, *tabs)
    tabs_v = [res[11 + 3 * j:14 + 3 * j] for j in range(3)]
    return res[0], res[1], res[2:5], res[5:8], res[8:11], tabs_v, res[20], res[21]


def _band_mask(n_keys):
    qi = lax.broadcasted_iota(jnp.int32, (BLOCK, n_keys), 0)
    kj = lax.broadcasted_iota(jnp.int32, (BLOCK, n_keys), 1)
    if n_keys == BLOCK:
        return kj <= qi
    return jnp.logical_or(jnp.logical_and(kj < BLOCK, kj >= qi), jnp.logical_and(kj >= BLOCK, (kj - BLOCK) <= qi))


def _low_head_lanes():
    return lax.broadcasted_iota(jnp.int32, (1, LANES), 1) < HEAD_DIM


def _split_pair(t, low):
    zero = jnp.zeros_like(t)
    return jnp.where(low, t, zero), jnp.where(low, zero, t)


def _pair_specs(d, nb, w):
    if nb == 1:
        return pl.BlockSpec((BLOCK, 2 * w), lambda n: (0, n)), None
    half = nb // 2
    two = pl.BlockSpec((2 * BLOCK, w), lambda n: (n % half, n // half))
    before = pl.BlockSpec((BLOCK, w), lambda n: (jnp.maximum(2 * (n % half) - 1, 0), n // half))
    return two, before


def _head_tiles(w, col0):
    return ([slice(p * LANES, (p + 1) * LANES) for p in range(w // LANES)],
            [slice(col0 + p * LANES, col0 + (p + 1) * LANES) for p in range(w // LANES)])


def _attend_fwd(q_ref, o_ref, lse_ref, rows, col0, kk, vv):
    w = kk.shape[1]
    valid = _band_mask(kk.shape[0])
    low = _low_head_lanes()
    pairs, qcols = _head_tiles(w, col0)
    qs_ = [h for qc in qcols for h in _split_pair(q_ref[rows, qc], low)]
    k2s = [kk[:, pr] for pr in pairs for _ in range(2)]
    scs = [jnp.where(valid, _dot_nt(qh, k2), NEG) for qh, k2 in zip(qs_, k2s)]
    ms = [jnp.max(sc, axis=-1, keepdims=True) for sc in scs]
    ps = [jnp.exp(sc - m) for sc, m in zip(scs, ms)]
    ls = [jnp.sum(p, axis=-1, keepdims=True) for p in ps]
    pns = [(p * (1.0 / l)).astype(BF16) for p, l in zip(ps, ls)]
    for i, (pr, qc) in enumerate(zip(pairs, qcols)):
        v2 = vv[:, pr]
        a, b = 2 * i, 2 * i + 1
        o_ref[rows, qc] = jnp.where(low, _dot(pns[a], v2), _dot(pns[b], v2))
        lse_ref[rows, qc] = jnp.where(low, ms[a] + jnp.log(ls[a]), ms[b] + jnp.log(ls[b]))


TOP, BOTTOM = slice(0, BLOCK), slice(BLOCK, 2 * BLOCK)


def _attn_fwd(q, k, v, d, token):
    ln, dw = q.shape
    w = dw // d
    nb = ln // BLOCK
    two, before = _pair_specs(d, nb, w)

    def body_streams(_, q_ref, kc_ref, vc_ref, o_ref, lse_ref):
        for sb in range(2):
            cols = slice(sb * w, (sb + 1) * w)
            _attend_fwd(q_ref, o_ref, lse_ref, TOP, sb * w, kc_ref[:, cols], vc_ref[:, cols])

    def body_blocks(_, q_ref, kp_ref, kc_ref, vp_ref, vc_ref, o_ref, lse_ref):
        first = pl.program_id(0) % (nb // 2) == 0
        pl.when(first)(lambda: _attend_fwd(q_ref, o_ref, lse_ref, TOP, 0, kc_ref[TOP, :], vc_ref[TOP, :]))
        pl.when(jnp.logical_not(first))(lambda: _attend_fwd(
            q_ref, o_ref, lse_ref, TOP, 0, jnp.concatenate([kp_ref[...], kc_ref[TOP, :]], axis=0),
            jnp.concatenate([vp_ref[...], vc_ref[TOP, :]], axis=0)))
        _attend_fwd(q_ref, o_ref, lse_ref, BOTTOM, 0, kc_ref[...], vc_ref[...])

    if nb == 1:
        body, in_specs, args = body_streams, [ANY, two, two, two], (token, q, k, v)
    else:
        body, in_specs, args = body_blocks, [ANY, two, before, two, before, two], (token, q, k, k, v, v)
    return _pallas_call(
        body, name=f"attn_fwd_d{d}", grid=(d * nb // 2,), out_shape=[_sds((ln, dw), F32)] * 2,
        in_specs=in_specs, out_specs=[two, two], compiler_params=_params(1, 32),
    )(*args)


def _memkv_fwd(mem, g, w):
    n_layers = w.shape[0]

    def body(mem_ref, g_ref, w_ref, kv_ref):
        mb = mem_ref[...]
        r = lax.rsqrt(jnp.mean(mb * mb, axis=-1, keepdims=True) + EPS)
        mn = ((mb * r) * g_ref[...]).astype(BF16)
        kv_ref[...] = _dot(mn, w_ref[...]).astype(BF16)

    return _pallas_call(
        body, name="memkv_fwd", grid=(n_layers,),
        out_shape=_plain((n_layers, N_MEM, 2 * MEM_WIDTH), BF16),
        in_specs=[_whole(mem.shape), pl.BlockSpec((None, 1, D_MODEL), lambda l: (l, 0, 0)),
                  pl.BlockSpec((None, D_MODEL, 2 * MEM_WIDTH), lambda l: (l, 0, 0))],
        out_specs=pl.BlockSpec((None, N_MEM, 2 * MEM_WIDTH), lambda l: (l, 0, 0)),
        compiler_params=_params(1, 32),
    )(mem, g.reshape(n_layers, 1, D_MODEL), w)


def _mix_groups(os_, ls_):
    mx = jnp.maximum(jnp.maximum(ls_[0], ls_[1]), ls_[2])
    es = [jnp.exp(t - mx) for t in ls_]
    inv = 1.0 / (es[0] + es[1] + es[2])
    ws = [e * inv for e in es]
    mix = ws[0] * os_[0] + ws[1] * os_[1] + ws[2] * os_[2]
    return ws, mix


MEM_PAIRS = [slice(p * LANES, (p + 1) * LANES) for p in range(MEM_WIDTH // LANES)]


def _mem_probs(qhs, k2s):
    scs = [_dot_nt(qh, k2) * SCALE for qh, k2 in zip(qhs, k2s)]
    es = [jnp.exp(sc - jnp.max(sc, axis=-1, keepdims=True)) for sc in scs]
    return [e * (1.0 / jnp.sum(e, axis=-1, keepdims=True)) for e in es]


def _mem_attn_into(qm, kv_ref, mo_ref):
    low = _low_head_lanes()
    qhs = [h for pr in MEM_PAIRS for h in _split_pair(qm[:, pr], low)]
    k2s = [kv_ref[:, pr] for pr in MEM_PAIRS for _ in range(2)]
    ps = [p.astype(BF16) for p in _mem_probs(qhs, k2s)]
    for i, pr in enumerate(MEM_PAIRS):
        v2 = kv_ref[:, MEM_WIDTH + i * LANES:MEM_WIDTH + (i + 1) * LANES]
        mo_ref[:, pr] = jnp.where(low, _dot(ps[2 * i], v2), _dot(ps[2 * i + 1], v2))


def _mem_attn_bwd(qm, kv_ref, dmem, dqm_ref, dkv_ref):
    low = _low_head_lanes()
    dmb = dmem.astype(BF16)
    vps = [slice(MEM_WIDTH + i * LANES, MEM_WIDTH + (i + 1) * LANES) for i in range(len(MEM_PAIRS))]
    qhs = [h for pr in MEM_PAIRS for h in _split_pair(qm[:, pr], low)]
    dhs = [h for pr in MEM_PAIRS for h in _split_pair(dmb[:, pr], low)]
    k2s = [kv_ref[:, pr] for pr in MEM_PAIRS for _ in range(2)]
    v2s = [kv_ref[:, vp] for vp in vps for _ in range(2)]
    ps = _mem_probs(qhs, k2s)
    dps = [_dot_nt(dh, v2) for dh, v2 in zip(dhs, v2s)]
    dss = [(p * (dp - jnp.sum(dp * p, axis=-1, keepdims=True)) * SCALE).astype(BF16) for p, dp in zip(ps, dps)]
    pbs = [p.astype(BF16) for p in ps]
    for i, (pr, vp) in enumerate(zip(MEM_PAIRS, vps)):
        a, b = 2 * i, 2 * i + 1
        dqm_ref[:, pr] = jnp.where(low, _dot(dss[a], k2s[a]), _dot(dss[b], k2s[b])).astype(BF16)
        dkv_ref[:, pr] += _dot_tn(dss[a], qhs[a]) + _dot_tn(dss[b], qhs[b])
        dkv_ref[:, vp] += _dot_tn(pbs[a], dhs[a]) + _dot_tn(pbs[b], dhs[b])


def _post_attn(os_, ls_, qm, kv, z, x, wg_out):
    s, d_model = x.shape
    gw = GROUP_WIDTH
    nb = gw + MEM_WIDTH
    tm = ROW_TILE

    def body(o0, o1, o2, l0, l1, l2, qm_ref, kv_ref, z_ref, x_ref, w_ref, y_ref, yt_ref, mo_ref, h_ref, s0, s1):
        ov, lv = [], []
        for o_ref, l_ref, d in zip((o0, o1, o2), (l0, l1, l2), DILATIONS):
            ov.append(_from_view(s0, o_ref, d))
            lv.append(_from_view(s1, l_ref, d))
        _, mix = _mix_groups(ov, lv)
        _mem_attn_into(qm_ref[...], kv_ref, mo_ref)
        sz, _ = _silu_parts(z_ref[...])
        y_ref[:, :gw] = (mix * sz[:, :gw]).astype(BF16)
        y_ref[:, gw:] = (mo_ref[...] * sz[:, gw:]).astype(BF16)
        y = y_ref[...]
        yt_ref[...] = y.T
        h_ref[...] = x_ref[...] + _dot(y, _joined_columns(w_ref))

    vspecs = [_view_rows(gw, d) for d in DILATIONS]
    return _pallas_call(
        body, name="post_attn", grid=(s // tm,),
        out_shape=[_sds((s, nb), BF16), _sds((nb, s), BF16), _sds((s, MEM_WIDTH), F32), _sds((s, d_model), F32)],
        in_specs=vspecs * 2 + [_rows(MEM_WIDTH), _whole(kv.shape), _rows(nb), _rows(d_model), _whole(wg_out.shape)],
        out_specs=[_rows(nb), pl.BlockSpec((nb, tm), lambda i: (0, i)), _rows(MEM_WIDTH), _rows(d_model)],
        scratch_shapes=[_view_scratch(gw), _view_scratch(gw)],
        compiler_params=_params(1, 40),
    )(*os_, *ls_, qm, kv, z, x, wg_out)


def _inproj_conv(x, g, wg):
    s, d_model = x.shape
    c = CONV_WIDTH
    n = N_DEV * wg.shape[2]
    nz = n - 3 * c - MEM_WIDTH
    tm = ROW_TILE

    def body(x_ref, g_ref, w_ref, hn_ref, hnt_ref, bg_ref, cg_ref, u_ref, qm_ref, z_ref, proj):
        xb = x_ref[...]
        r = lax.rsqrt(jnp.mean(xb * xb, axis=-1, keepdims=True) + EPS)
        hn = ((xb * r) * g_ref[...]).astype(BF16)
        hn_ref[...] = hn
        hnt_ref[...] = hn.T
        _project(hn, w_ref, proj)
        bg_ref[...] = proj[:, 0:c]
        cg_ref[...] = proj[:, c:2 * c]
        u_ref[...] = proj[:, 2 * c:3 * c]
        qm_ref[...] = proj[:, 3 * c:3 * c + MEM_WIDTH].astype(BF16)
        z_ref[...] = proj[:, 3 * c + MEM_WIDTH:]

    return _pallas_call(
        body, name="inproj_conv", grid=(s // tm,),
        out_shape=[_sds((s, d_model), BF16), _sds((d_model, s), BF16)] + [_sds((s, c), F32)] * 3
                  + [_sds((s, MEM_WIDTH), BF16), _sds((s, nz), F32)],
        in_specs=[_rows(d_model), _whole((1, d_model)), _whole(wg.shape)],
        out_specs=[_rows(d_model), pl.BlockSpec((d_model, tm), lambda i: (0, i))] + [_rows(c)] * 3
                  + [_rows(MEM_WIDTH), _rows(nz)],
        scratch_shapes=[pltpu.VMEM((tm, n), F32)],
        compiler_params=_params(1, 60),
    )(x, g, wg)


HALO = 8


def _halo_before(width, tm=ROW_TILE):
    return pl.BlockSpec((HALO, width), lambda i: (jnp.maximum(i * (tm // HALO) - 1, 0), 0))


def _halo_after(width, n_rows, tm=ROW_TILE):
    return pl.BlockSpec((HALO, width), lambda i: (jnp.minimum((i + 1) * (tm // HALO), n_rows // HALO - 1), 0))


def _conv_taps(cg_ref, u_ref, cgh_ref, uh_ref, i):
    a = cg_ref[...] * u_ref[...]
    ah = jnp.where(i > 0, cgh_ref[...] * uh_ref[...], 0.0)
    row = lax.broadcasted_iota(jnp.int32, a.shape, 0)
    a1 = jnp.where(row == 0, ah[HALO - 1:HALO], pltpu.roll(a, 1, 0))
    a2 = jnp.where(row == 0, ah[HALO - 2:HALO - 1], jnp.where(row == 1, ah[HALO - 1:HALO], pltpu.roll(a, 2, 0)))
    return a, a1, a2


def _post_conv_loss(bg, cg, u, qm, kv, z, h1, w_out, cw, gf, tgt):
    s, d = h1.shape
    c = CONV_WIDTH
    nb = c + MEM_WIDTH
    tm = ROW_TILE

    def body(bg_ref, cg_ref, u_ref, cgh_ref, uh_ref, qm_ref, kv_ref, z_ref, h_ref, w_ref, cw_ref, gf_ref, t_ref,
             y_ref, yt_ref, mo_ref, dh_ref, dhb_ref, loss_ref, dgf_ref):
        i = pl.program_id(0)
        a, a1, a2 = _conv_taps(cg_ref, u_ref, cgh_ref, uh_ref, i)
        conv = cw_ref[0:1, :] * a2 + cw_ref[1:2, :] * a1 + cw_ref[2:3, :] * a
        mix = bg_ref[...] * conv
        _mem_attn_into(qm_ref[...], kv_ref, mo_ref)
        sz, _ = _silu_parts(z_ref[...])
        y_ref[:, :c] = (mix * sz[:, :c]).astype(BF16)
        y_ref[:, c:] = (mo_ref[...] * sz[:, c:]).astype(BF16)
        y = y_ref[...]
        yt_ref[...] = y.T
        h2 = h_ref[...] + _dot(y, w_ref[...])
        r = lax.rsqrt(jnp.mean(h2 * h2, axis=-1, keepdims=True) + EPS)
        nh = h2 * r
        gfv = gf_ref[...]
        diff = nh * gfv - t_ref[...]
        dout = diff * (1.0 / d)
        dn = dout * gfv
        dh2 = r * dn - h2 * ((r * r * r) * jnp.mean(dn * h2, axis=-1, keepdims=True))
        dh_ref[...] = dh2
        dhb_ref[...] = dh2.astype(BF16)

        @pl.when(i == 0)
        def _():
            loss_ref[...] = jnp.zeros_like(loss_ref)
            dgf_ref[...] = jnp.zeros_like(dgf_ref)

        loss_ref[...] += 0.5 * jnp.sum(jnp.mean(diff * diff, axis=-1, keepdims=True))
        dgf_ref[...] += jnp.sum(dout * nh, axis=0, keepdims=True)

    return _pallas_call(
        body, name="post_conv_loss", grid=(s // tm,),
        out_shape=[_sds((s, nb), BF16), _sds((nb, s), BF16), _sds((s, MEM_WIDTH), F32), _sds((s, d), F32),
                   _sds((s, d), BF16), _plain((8, LANES), F32), _plain((1, d), F32)],
        in_specs=[_rows(c)] * 3 + [_halo_before(c)] * 2 + [_rows(MEM_WIDTH), _whole(kv.shape), _rows(nb), _rows(d),
                  _whole(w_out.shape), _whole(cw.shape), _whole((1, d)), _rows(d)],
        out_specs=[_rows(nb), pl.BlockSpec((nb, tm), lambda i: (0, i)), _rows(MEM_WIDTH), _rows(d), _rows(d),
                   _whole((8, LANES)), _whole((1, d))],
        compiler_params=_params(1, 48),
    )(bg, cg, u, cg, u, qm, kv, z, h1, w_out, cw, gf, tgt)


def _bwd_post_conv(dhb, w_out, bg, cg, u, z, qm, kv, mo, cw):
    s = dhb.shape[0]
    c = CONV_WIDTH
    nb = c + MEM_WIDTH

    def body(dh_ref, w_ref, bg_ref, cg_ref, u_ref, cgh_ref, uh_ref, z_ref, qm_ref, kv_ref, mo_ref, cw_ref,
             dz_ref, dbg_ref, dc_ref, dqm_ref, dkv_ref):
        i = pl.program_id(0)

        @pl.when(i == 0)
        def _():
            dkv_ref[...] = jnp.zeros_like(dkv_ref)

        dy = _dot_nt(dh_ref[...], w_ref[...])
        sz, dsz = _silu_parts(z_ref[...])
        a, a1, a2 = _conv_taps(cg_ref, u_ref, cgh_ref, uh_ref, i)
        conv = cw_ref[0:1, :] * a2 + cw_ref[1:2, :] * a1 + cw_ref[2:3, :] * a
        bgv = bg_ref[...]
        dz_ref[:, :c] = (dy[:, :c] * (bgv * conv) * dsz[:, :c]).astype(BF16)
        dz_ref[:, c:] = (dy[:, c:] * mo_ref[...] * dsz[:, c:]).astype(BF16)
        dbr = dy * sz
        dmix = dbr[:, :c]
        dbg_ref[...] = (dmix * conv).astype(BF16)
        dc_ref[...] = dmix * bgv
        _mem_attn_bwd(qm_ref[...], kv_ref, dbr[:, c:], dqm_ref, dkv_ref)

    return _pallas_call(
        body, name="bwd_post_conv", grid=(s // ROW_TILE,),
        out_shape=[_sds((s, nb), BF16), _sds((s, c), BF16), _sds((s, c), F32), _sds((s, MEM_WIDTH), BF16),
                   _plain(kv.shape, F32)],
        in_specs=[_rows(D_MODEL), _whole(w_out.shape)] + [_rows(c)] * 3 + [_halo_before(c)] * 2
                 + [_rows(nb), _rows(MEM_WIDTH), _whole(kv.shape), _rows(MEM_WIDTH), _whole(cw.shape)],
        out_specs=[_rows(nb), _rows(c), _rows(c), _rows(MEM_WIDTH), _whole(kv.shape)],
        compiler_params=_params(1, 48),
    )(dhb, w_out, bg, cg, u, cg, u, z, qm, kv, mo, cw)


def _bwd_conv(dconv, cg, u, cw):
    s, c = dconv.shape
    tm = ROW_TILE
    last = s // tm - 1

    def body(dc_ref, dcn_ref, cg_ref, u_ref, cgh_ref, uh_ref, cw_ref, dcg_ref, du_ref, dcw_ref):
        i = pl.program_id(0)

        @pl.when(i == 0)
        def _():
            dcw_ref[...] = jnp.zeros_like(dcw_ref)

        dc = dc_ref[...]
        dcn = jnp.where(i < last, dcn_ref[...], 0.0)
        row = lax.broadcasted_iota(jnp.int32, dc.shape, 0)
        d1 = jnp.where(row == tm - 1, dcn[0:1], pltpu.roll(dc, tm - 1, 0))
        d2 = jnp.where(row == tm - 1, dcn[1:2], jnp.where(row == tm - 2, dcn[0:1], pltpu.roll(dc, tm - 2, 0)))
        da = cw_ref[2:3, :] * dc + cw_ref[1:2, :] * d1 + cw_ref[0:1, :] * d2
        a, a1, a2 = _conv_taps(cg_ref, u_ref, cgh_ref, uh_ref, i)
        dcg_ref[...] = (da * u_ref[...]).astype(BF16)
        du_ref[...] = (da * cg_ref[...]).astype(BF16)
        dcw_ref[0:1, :] += jnp.sum(dc * a2, axis=0, keepdims=True)
        dcw_ref[1:2, :] += jnp.sum(dc * a1, axis=0, keepdims=True)
        dcw_ref[2:3, :] += jnp.sum(dc * a, axis=0, keepdims=True)

    return _pallas_call(
        body, name="bwd_conv", grid=(s // tm,),
        out_shape=[_sds((s, c), BF16), _sds((s, c), BF16), _plain((8, c), F32)],
        in_specs=[_rows(c), _halo_after(c, s), _rows(c), _rows(c), _halo_before(c), _halo_before(c), _whole(cw.shape)],
        out_specs=[_rows(c), _rows(c), _whole((8, c))], compiler_params=_params(1, 40),
    )(dconv, dconv, cg, u, cg, u, cw)


def _dgrad_norm(pieces, wg, h, g, dres, name):
    s, d_model = h.shape
    c = wg.shape[2]
    n = N_DEV * c
    tm = ROW_TILE
    widths = [p.shape[1] // d for p, d in pieces]
    assert sum(widths) == n
    n_p = len(pieces)

    def body(*refs):
        p_refs = refs[:n_p]
        w_ref, h_ref, g_ref, dr_ref, dh_ref, dhb_ref, dg_ref, dpd_ref, dp, scr = refs[n_p:]

        @pl.when(pl.program_id(0) == 0)
        def _():
            dg_ref[...] = jnp.zeros_like(dg_ref)

        off = 0
        for p_ref, (_, d), wd in zip(p_refs, pieces, widths):
            if d == 1:
                dp[:, off:off + wd] = p_ref[...]
            else:
                dp[:, off:off + wd] = _from_view(scr, p_ref, d).astype(BF16)
            off += wd
        dhn = jnp.zeros((tm, d_model), F32)
        for j in range(N_DEV):
            dpj = dp[:, j * c:(j + 1) * c]
            dpd_ref[j] = dpj
            dhn += _dot_nt(dpj, w_ref[j])
        hb = h_ref[...]
        r = lax.rsqrt(jnp.mean(hb * hb, axis=-1, keepdims=True) + EPS)
        dg_ref[...] += jnp.sum(dhn * (hb * r), axis=0, keepdims=True)
        dn = dhn * g_ref[...]
        dh = dr_ref[...] + r * dn - hb * ((r * r * r) * jnp.mean(dn * hb, axis=-1, keepdims=True))
        dh_ref[...] = dh
        dhb_ref[...] = dh.astype(BF16)

    p_specs = [_view_rows(wd, d) for (_, d), wd in zip(pieces, widths)]
    return _pallas_call(
        body, name=name, grid=(s // tm,),
        out_shape=[_plain((s, d_model), F32), _sds((s, d_model), BF16), _plain((1, d_model), F32), _sds((N_DEV, s, c), BF16)],
        in_specs=p_specs + [_whole(wg.shape), _rows(d_model), _whole((1, d_model)), _rows(d_model)],
        out_specs=[_rows(d_model), _rows(d_model), _whole((1, d_model)), pl.BlockSpec((N_DEV, tm, c), lambda i: (0, i, 0))],
        scratch_shapes=[pltpu.VMEM((tm, n), BF16), _view_scratch(GROUP_WIDTH)],
        compiler_params=_params(1, 60),
    )(*[p for p, _ in pieces], wg, h, g, dres)


def _assemble_dproj(pieces, c, name):
    n = N_DEV * c
    tm = ROW_TILE
    widths = [p.shape[1] // d for p, d in pieces]
    assert sum(widths) == n
    s = pieces[0][0].shape[0] * pieces[0][1]
    n_p = len(pieces)

    def body(*refs):
        p_refs, (dpd_ref, dp, scr) = refs[:n_p], refs[n_p:]
        off = 0
        for p_ref, (_, d), wd in zip(p_refs, pieces, widths):
            if d == 1:
                dp[:, off:off + wd] = p_ref[...]
            else:
                dp[:, off:off + wd] = _from_view(scr, p_ref, d).astype(BF16)
            off += wd
        for j in range(N_DEV):
            dpd_ref[j] = dp[:, j * c:(j + 1) * c]

    return _pallas_call(
        body, name=name, grid=(s // tm,), out_shape=_sds((N_DEV, s, c), BF16),
        in_specs=[_view_rows(wd, d) for (_, d), wd in zip(pieces, widths)],
        out_specs=pl.BlockSpec((N_DEV, tm, c), lambda i: (0, i, 0)),
        scratch_shapes=[pltpu.VMEM((tm, n), BF16), _view_scratch(GROUP_WIDTH)],
        compiler_params=_params(1, 40),
    )(*[p for p, _ in pieces])


def _dgrad_norm_dm(dproj_dm, wg, h, g, dres, token, name):
    s, d_model = h.shape
    c = wg.shape[2]
    tm = ROW_TILE

    def body(_, dp_ref, w_ref, h_ref, g_ref, dr_ref, dh_ref, dg_ref):
        @pl.when(pl.program_id(0) == 0)
        def _():
            dg_ref[...] = jnp.zeros_like(dg_ref)

        dhn = jnp.zeros((tm, d_model), F32)
        for j in range(N_DEV):
            dhn += _dot_nt(dp_ref[j], w_ref[j])
        hb = h_ref[...]
        r = lax.rsqrt(jnp.mean(hb * hb, axis=-1, keepdims=True) + EPS)
        dg_ref[...] += jnp.sum(dhn * (hb * r), axis=0, keepdims=True)
        dn = dhn * g_ref[...]
        dh_ref[...] = dr_ref[...] + r * dn - hb * ((r * r * r) * jnp.mean(dn * hb, axis=-1, keepdims=True))

    return _pallas_call(
        body, name=name, grid=(s // tm,),
        out_shape=[_plain((s, d_model), F32), _plain((1, d_model), F32)],
        in_specs=[ANY, pl.BlockSpec((N_DEV, tm, c), lambda i: (0, i, 0)), _whole(wg.shape), _rows(d_model),
                  _whole((1, d_model)), _rows(d_model)],
        out_specs=[_rows(d_model), _whole((1, d_model))],
        compiler_params=_params(1, 60),
    )(token, dproj_dm, wg, h, g, dres)


def _wgrad_shards(a_t, b_dm, name):
    m, s = a_t.shape
    c = b_dm.shape[2]

    def body(a_ref, b_ref, o_ref):
        o_ref[...] = _dot(a_ref[...], b_ref[...]).astype(BF16)

    return _pallas_call(
        body, name=name, grid=(N_DEV,), out_shape=_sds((N_DEV, m, c), BF16),
        in_specs=[_whole(a_t.shape), pl.BlockSpec((None, s, c), lambda j: (j, 0, 0))],
        out_specs=pl.BlockSpec((None, m, c), lambda j: (j, 0, 0)), compiler_params=_params(1, 40),
    )(a_t, b_dm)


def _wgrad_cols(a_t, b, c, name):
    m, s = a_t.shape
    assert c % LANES == 0

    def body(a_ref, b_ref, o_ref):
        wide = _dot(a_ref[...], b_ref[...]).astype(BF16)
        for j in range(WGRAD_SHARDS):
            o_ref[j] = wide[:, j * c:(j + 1) * c]

    return _pallas_call(
        body, name=name, grid=(N_DEV // WGRAD_SHARDS,), out_shape=_sds((N_DEV, m, c), BF16),
        in_specs=[_whole(a_t.shape), pl.BlockSpec((s, WGRAD_SHARDS * c), lambda j: (0, j))],
        out_specs=pl.BlockSpec((WGRAD_SHARDS, m, c), lambda j: (j, 0, 0)), compiler_params=_params(1, 40),
    )(a_t, b)


def _wgrad_rows(a_t, b, name):
    m, s = a_t.shape
    n = b.shape[1]
    mr = m // N_DEV

    def body(a_ref, b_ref, o_ref):
        o_ref[...] = _dot(a_ref[...], b_ref[...]).astype(BF16).reshape(WGRAD_SHARDS, mr, n)

    return _pallas_call(
        body, name=name, grid=(N_DEV // WGRAD_SHARDS,), out_shape=_sds((N_DEV, mr, n), BF16),
        in_specs=[pl.BlockSpec((WGRAD_SHARDS * mr, s), lambda j: (j, 0)), _whole(b.shape)],
        out_specs=pl.BlockSpec((WGRAD_SHARDS, mr, n), lambda j: (j, 0, 0)), compiler_params=_params(1, 40),
    )(a_t, b)


def _memkv_bwd(dkv, w, mem, g):
    n_layers = w.shape[0]
    rows = D_MODEL // N_DEV

    def body(dkv_ref, w_ref, mem_ref, g_ref, dw_ref, dg_ref):
        mb = mem_ref[...]
        r = lax.rsqrt(jnp.mean(mb * mb, axis=-1, keepdims=True) + EPS)
        nm = mb * r
        mn = (nm * g_ref[...]).astype(BF16)
        dkvb = dkv_ref[...].astype(BF16)
        dw_ref[...] = _dot_tn(mn, dkvb).astype(BF16).reshape(N_DEV, rows, 2 * MEM_WIDTH)
        dmn = _dot_nt(dkvb, w_ref[...])
        dg_ref[...] = jnp.sum(dmn * nm, axis=0, keepdims=True)

    lay = lambda *shape: pl.BlockSpec((None,) + shape, lambda l: (l, 0, 0))
    return _pallas_call(
        body, name="memkv_bwd", grid=(n_layers,),
        out_shape=[_sds((N_DEV, n_layers * rows, 2 * MEM_WIDTH), BF16), _plain((n_layers, 1, D_MODEL), F32)],
        in_specs=[lay(N_MEM, 2 * MEM_WIDTH), lay(D_MODEL, 2 * MEM_WIDTH), _whole(mem.shape), lay(1, D_MODEL)],
        out_specs=[pl.BlockSpec((N_DEV, rows, 2 * MEM_WIDTH), lambda l: (0, l, 0)), lay(1, D_MODEL)],
        compiler_params=_params(1, 32),
    )(dkv, w, mem, g.reshape(n_layers, 1, D_MODEL))


def _bwd_post_attn(dhb, wg_out, z, os_, ls_, qm, kv, mo, head_ones):
    s = dhb.shape[0]
    gw = GROUP_WIDTH
    nb = gw + MEM_WIDTH
    tm = ROW_TILE

    def body(dh_ref, w_ref, z_ref, o0, o1, o2, l0, l1, l2, qm_ref, kv_ref, mo_ref, bd_ref,
             dz_ref, do0, do1, do2, dl0, dl1, dl2, dqm_ref, dkv_ref, s0, s1):
        @pl.when(pl.program_id(0) == 0)
        def _():
            dkv_ref[...] = jnp.zeros_like(dkv_ref)

        dy = _dot_nt(dh_ref[...], _joined_columns(w_ref))
        ov, lv = [], []
        for o_ref, l_ref, d in zip((o0, o1, o2), (l0, l1, l2), DILATIONS):
            ov.append(_from_view(s0, o_ref, d))
            lv.append(_from_view(s1, l_ref, d))
        ws, mix = _mix_groups(ov, lv)
        sz, dsz = _silu_parts(z_ref[...])
        dz_ref[:, :gw] = (dy[:, :gw] * mix * dsz[:, :gw]).astype(BF16)
        dz_ref[:, gw:] = (dy[:, gw:] * mo_ref[...] * dsz[:, gw:]).astype(BF16)
        dbr = dy * sz
        dmix = dbr[:, :gw]
        t = dmix * mix
        th = t.astype(BF16)
        tl = (t - th.astype(F32)).astype(BF16)
        rs = _dot(th, bd_ref[...]) + _dot(tl, bd_ref[...])
        for wg_, do_ref, dl_ref, d in zip(ws, (do0, do1, do2), (dl0, dl1, dl2), DILATIONS):
            _to_view(s0, wg_ * dmix, do_ref, d)
            _to_view(s1, wg_ * rs, dl_ref, d)
        _mem_attn_bwd(qm_ref[...], kv_ref, dbr[:, gw:], dqm_ref, dkv_ref)

    vspecs = [_view_rows(gw, d) for d in DILATIONS]
    return _pallas_call(
        body, name="bwd_post_attn", grid=(s // tm,),
        out_shape=[_sds((s, nb), BF16)] + [_sds((s // d, d * gw), BF16) for d in DILATIONS]
                  + [_sds((s // d, d * gw), F32) for d in DILATIONS] + [_sds((s, MEM_WIDTH), BF16), _plain(kv.shape, F32)],
        in_specs=[_rows(D_MODEL), _whole(wg_out.shape), _rows(nb)] + vspecs * 2
                 + [_rows(MEM_WIDTH), _whole(kv.shape), _rows(MEM_WIDTH), _whole(head_ones.shape)],
        out_specs=[_rows(nb)] + vspecs * 2 + [_rows(MEM_WIDTH), _whole(kv.shape)],
        scratch_shapes=[_view_scratch(gw), _view_scratch(gw)],
        compiler_params=_params(1, 48),
    )(dhb, wg_out, z, *os_, *ls_, qm, kv, mo, head_ones)


def _attn_bwd(q, k, v, lse, do, dl, tabs, d, token):
    ln, dw = q.shape
    w = dw // d
    nb = ln // BLOCK
    reps = w // LANES
    two, before = _pair_specs(d, nb, w)
    two_t, _ = _pair_specs(d, nb, LANES)

    def attend(q_ref, l_ref, do_ref, dl_ref, dqs, acck, accv, rows, col0, kk, vv, acc_rows):
        valid = _band_mask(kk.shape[0])
        low = _low_head_lanes()
        pairs, qcols = _head_tiles(w, col0)
        cols = [slice(col0 + h * HEAD_DIM, col0 + h * HEAD_DIM + 1) for h in range(HEADS_PER_GROUP)]
        qhs = [h for qc in qcols for h in _split_pair(q_ref[rows, qc], low)]
        dobs = [h for qc in qcols for h in _split_pair(do_ref[rows, qc], low)]
        k2s = [kk[:, pr] for pr in pairs for _ in range(2)]
        v2s = [vv[:, pr] for pr in pairs for _ in range(2)]
        scs = [jnp.where(valid, _dot_nt(qh, k2), NEG) for qh, k2 in zip(qhs, k2s)]
        dps = [_dot_nt(dob, v2) for dob, v2 in zip(dobs, v2s)]
        ps = [jnp.exp(sc - l_ref[rows, col]) for sc, col in zip(scs, cols)]
        dss = [(p * (dp - dl_ref[rows, col])).astype(BF16) for p, dp, col in zip(ps, dps, cols)]
        pbs = [p.astype(BF16) for p in ps]
        for i, qc in enumerate(qcols):
            a, b = 2 * i, 2 * i + 1
            dqs[rows, qc] = jnp.where(low, _dot(dss[a], k2s[a]), _dot(dss[b], k2s[b])) * SCALE
            acck[acc_rows, qc] += _dot_tn(dss[a], qhs[a]) + _dot_tn(dss[b], qhs[b])
            accv[acc_rows, qc] += _dot_tn(pbs[a], dobs[a]) + _dot_tn(pbs[b], dobs[b])

    def body_streams(_, q_ref, kc_ref, vc_ref, l_ref, do_ref, dl_ref, c_ref, sa_ref, sb_ref,
                     dq_ref, dk_ref, dv_ref, acck, accv, dqs):
        acck[...] = jnp.zeros_like(acck)
        accv[...] = jnp.zeros_like(accv)
        for sb in range(2):
            cols = slice(sb * w, (sb + 1) * w)
            attend(q_ref, l_ref, do_ref, dl_ref, dqs, acck, accv, TOP, sb * w, kc_ref[:, cols], vc_ref[:, cols], TOP)
        tabs2 = [jnp.concatenate([jnp.tile(r[:, sb * LANES:(sb + 1) * LANES], (1, reps)) for sb in range(2)], axis=1)
                 for r in (c_ref, sa_ref, sb_ref)]
        dq_ref[...] = _rope_bwd(dqs[...], *tabs2).astype(BF16)
        dk_ref[...] = _rope_bwd(acck[...], *tabs2).astype(BF16)
        dv_ref[...] = accv[...].astype(BF16)

    def body_blocks(_, q_ref, kp_ref, kc_ref, vp_ref, vc_ref, l_ref, do_ref, dl_ref, cq, saq, sbq, ck, sak, sbk,
                    dq_ref, dk_ref, dv_ref, acck, accv, dqs):
        i = pl.program_id(0) % (nb // 2)

        @pl.when(i == 0)
        def _():
            acck[...] = jnp.zeros_like(acck)
            accv[...] = jnp.zeros_like(accv)

        refs = (q_ref, l_ref, do_ref, dl_ref, dqs, acck, accv)
        pl.when(i == 0)(lambda: attend(*refs, TOP, 0, kc_ref[TOP, :], vc_ref[TOP, :], TOP))
        pl.when(i != 0)(lambda: attend(
            *refs, TOP, 0, jnp.concatenate([kp_ref[...], kc_ref[TOP, :]], axis=0),
            jnp.concatenate([vp_ref[...], vc_ref[TOP, :]], axis=0),
            pl.ds(pl.multiple_of((2 * i - 1) * BLOCK, BLOCK), 2 * BLOCK)))
        attend(*refs, BOTTOM, 0, kc_ref[...], vc_ref[...], pl.ds(pl.multiple_of(2 * i * BLOCK, BLOCK), 2 * BLOCK))
        tq = [jnp.tile(r[...], (1, reps)) for r in (cq, saq, sbq)]
        dq_ref[...] = _rope_bwd(dqs[...], *tq).astype(BF16)

        @pl.when(i == nb // 2 - 1)
        def _():
            for r0 in range(0, nb * BLOCK, 2 * BLOCK):
                rows = slice(r0, r0 + 2 * BLOCK)
                tk = [jnp.tile(r[rows, :], (1, reps)) for r in (ck, sak, sbk)]
                dk_ref[rows, :] = _rope_bwd(acck[rows, :], *tk).astype(BF16)
                dv_ref[rows, :] = accv[rows, :].astype(BF16)

    if nb == 1:
        body = body_streams
        in_specs = [ANY] + [two] * 6 + [two_t] * 3
        args = (token, q, k, v, lse, do, dl, *tabs)
        out_specs = [two, two, two]
        acc_shape = (BLOCK, 2 * w)
    else:
        body = body_blocks
        stream = pl.BlockSpec((nb * BLOCK, w), lambda n: (0, n // (nb // 2)))
        stream_t = pl.BlockSpec((nb * BLOCK, LANES), lambda n: (0, n // (nb // 2)))
        in_specs = [ANY, two, before, two, before, two, two, two, two] + [two_t] * 3 + [stream_t] * 3
        args = (token, q, k, k, v, v, lse, do, dl, *tabs, *tabs)
        out_specs = [two, stream, stream]
        acc_shape = (nb * BLOCK, w)
    return _pallas_call(
        body, name=f"attn_bwd_d{d}", grid=(d * nb // 2,), out_shape=[_sds((ln, dw), BF16)] * 3,
        in_specs=in_specs, out_specs=out_specs,
        scratch_shapes=[pltpu.VMEM(acc_shape, F32), pltpu.VMEM(acc_shape, F32), pltpu.VMEM(two.block_shape, F32)],
        compiler_params=_params(1, 48),
    )(*args)


def _position():
    return lax.axis_index("x"), lax.axis_index("y"), lax.axis_index("c")


def _all_gather(shards, after, name):
    n_a = len(shards)

    def body(*refs):
        x_refs, out_refs = refs[:n_a], refs[n_a + 1:2 * n_a + 1]
        send_sems, recv_sems, local_sems = refs[2 * n_a + 1:]
        x, y, c = _position()
        me, sibling = (x, y, c), (x, y, 1 - c)
        chips = [(1 - x, y), (x, 1 - y), (1 - x, 1 - y)]

        def rows(a, px, py, pc):
            return out_refs[a].at[4 * px + 2 * py + pc]

        def copy(a, k, block, to, own=False):
            return pltpu.make_async_remote_copy(
                src_ref=x_refs[a] if own else rows(a, *block), dst_ref=rows(a, *block),
                send_sem=send_sems.at[a, k], recv_sem=recv_sems.at[a, k], device_id=to, device_id_type=MESH)

        mine = [pltpu.make_async_copy(x_refs[a], rows(a, *me), local_sems.at[a]) for a in range(n_a)]
        for cp in mine:
            cp.start()
        first = []
        for j, chip in enumerate(chips):
            first += [copy(a, 1 + j, me, (*chip, c), own=True) for a in range(n_a)]
        first += [copy(a, 0, me, sibling, own=True) for a in range(n_a)]
        for cp in first:
            cp.start()
        passed = []
        for j, chip in enumerate(chips):
            for a in range(n_a):
                copy(a, 1 + j, (*chip, c), me).wait_recv()
                fwd = copy(a, 4 + j, (*chip, c), sibling)
                fwd.start()
                passed.append(fwd)
        for a in range(n_a):
            copy(a, 0, sibling, me).wait_recv()
        for j, chip in enumerate(chips):
            for a in range(n_a):
                copy(a, 4 + j, (*chip, 1 - c), me).wait_recv()
        for cp in first + passed:
            cp.wait_send()
        for cp in mine:
            cp.wait()

    return _pallas_call(
        body, name=name, out_shape=[_sds((N_DEV,) + t.shape, t.dtype) for t in shards],
        in_specs=[ANY] * (n_a + 1), out_specs=[ANY] * n_a,
        scratch_shapes=[pltpu.SemaphoreType.DMA((n_a, 7)), pltpu.SemaphoreType.DMA((n_a, 7)),
                        pltpu.SemaphoreType.DMA((n_a,))],
    )(*shards, after)


def _all_gather_relay(xs, name):
    def body(x_ref, out_ref, send_sems, recv_sems, local_sem):
        x, y, c = _position()
        me, sibling = (x, y, c), (x, y, 1 - c)
        xn, yn, diag = (1 - x, y, c), (x, 1 - y, c), (1 - x, 1 - y, c)
        src_nb = (x + c * (1 - 2 * x), y + (1 - c) * (1 - 2 * y), c)
        dst_nb = (x + (1 - c) * (1 - 2 * x), y + c * (1 - 2 * y), c)

        def rows(dev):
            return out_ref.at[4 * dev[0] + 2 * dev[1] + dev[2]]

        def copy(k, block, to, own=False):
            return pltpu.make_async_remote_copy(
                src_ref=x_ref if own else rows(block), dst_ref=rows(block),
                send_sem=send_sems.at[k], recv_sem=recv_sems.at[k], device_id=to, device_id_type=MESH)

        mine = pltpu.make_async_copy(x_ref, rows(me), local_sem)
        mine.start()
        first = [copy(1, me, xn, own=True), copy(2, me, yn, own=True), copy(0, me, sibling, own=True)]
        for cp in first:
            cp.start()
        copy(1, xn, me).wait_recv()
        copy(2, yn, me).wait_recv()
        relay = copy(3, src_nb, dst_nb)
        relay.start()
        passed = [copy(4, xn, sibling), copy(5, yn, sibling)]
        for cp in passed:
            cp.start()
        copy(3, diag, me).wait_recv()
        last = copy(6, diag, sibling)
        last.start()
        copy(0, sibling, me).wait_recv()
        for k, blk in ((4, (1 - x, y, 1 - c)), (5, (x, 1 - y, 1 - c)), (6, (1 - x, 1 - y, 1 - c))):
            copy(k, blk, me).wait_recv()
        for cp in first + [relay] + passed + [last]:
            cp.wait_send()
        mine.wait()

    return _pallas_call(
        body, name=name, out_shape=_sds((N_DEV,) + xs.shape, xs.dtype),
        in_specs=[ANY], out_specs=ANY,
        scratch_shapes=[pltpu.SemaphoreType.DMA((7,)), pltpu.SemaphoreType.DMA((7,)), pltpu.SemaphoreType.DMA],
    )(xs)


HBM_SPEC = pl.BlockSpec(memory_space=pltpu.HBM)
SEM_SPEC = pl.BlockSpec(memory_space=pltpu.SEMAPHORE)
EFFECT = pltpu.SideEffectType.DATAFLOW_SIDE_EFFECTING
def _plan_gather_own(src_refs, land_refs):
    x, y, c = _position()
    me = 4 * x + 2 * y + c
    peers = [(x, y, 1 - c), (1 - x, y, c), (x, 1 - y, c), (1 - x, 1 - y, c)]
    return [(src_refs[a], land_refs[a].at[me], (a, k), peer) for k, peer in enumerate(peers) for a in range(len(src_refs))]


def _plan_gather_pass(src_refs, land_refs):
    x, y, c = _position()
    chips = [(1 - x, y), (x, 1 - y), (1 - x, 1 - y)]
    return [(land_refs[a].at[4 * px + 2 * py + c], land_refs[a].at[4 * px + 2 * py + c], (a, j), (x, y, 1 - c))
            for j, (px, py) in enumerate(chips) for a in range(len(land_refs))]


def _plan_to_sibling(src_refs, land_refs):
    x, y, c = _position()
    return [(src_refs[a].at[2 * k + (1 - c)], land_refs[a].at[k], (a, k), (x, y, 1 - c))
            for k in range(4) for a in range(len(src_refs))]


def _plan_to_chips(src_refs, land_refs):
    x, y, c = _position()
    chips = [(1 - x, y), (x, 1 - y), (1 - x, 1 - y)]
    return [(src_refs[a].at[2 * px + py], land_refs[a].at[j], (a, j), (px, py, c))
            for j, (px, py) in enumerate(chips) for a in range(len(src_refs))]


def _split_start(srcs, lands, plan, n_sem, after, name):
    n_s, n_a = len(srcs), len(lands)
    n_b = n_s + n_a

    def body(*refs):
        src_refs, land_refs = refs[:n_s], refs[n_s:n_b]
        send_sems, recv_sems, token = refs[n_b + 1], refs[n_b + 2], refs[-1]
        for src, dst, (a, k), dev in plan(src_refs, land_refs):
            i = a * n_sem + k
            pltpu.make_async_remote_copy(src_ref=src, dst_ref=dst, send_sem=send_sems.at[i], recv_sem=recv_sems.at[i],
                                         device_id=dev, device_id_type=MESH).start()
        token[...] = jnp.zeros_like(token)

    bufs = list(srcs) + list(lands)
    res = pl.pallas_call(
        body, name=name,
        out_shape=(pltpu.SemaphoreType.DMA((n_a * n_sem,)), pltpu.SemaphoreType.DMA((n_a * n_sem,)),
                   *[pltpu.HBM(t.shape, t.dtype) for t in bufs], _plain((8, LANES), F32)),
        in_specs=[HBM_SPEC] * n_b + [ANY],
        out_specs=(SEM_SPEC, SEM_SPEC, *[HBM_SPEC] * n_b, pl.BlockSpec(memory_space=pltpu.VMEM)),
        input_output_aliases={i: 2 + i for i in range(n_b)},
        compiler_params=pltpu.CompilerParams(has_side_effects=EFFECT),
    )(*[pltpu.with_memory_space_constraint(t, pltpu.HBM) for t in bufs], after)
    return (res[0], res[1], res[2:2 + n_s], res[2 + n_s:2 + n_b]), res[-1]


def _split_wait(started, plan, after, name):
    send_sems, recv_sems, srcs, lands = started
    n_s, n_a = len(srcs), len(lands)
    n_b = n_s + n_a
    n_sem = send_sems.shape[0] // n_a

    def body(*refs):
        src_refs, land_refs = refs[:n_s], refs[n_s:n_b]
        s_sems, r_sems = refs[n_b], refs[n_b + 1]
        for src, dst, (a, k), dev in plan(src_refs, land_refs):
            i = a * n_sem + k
            cp = pltpu.make_async_remote_copy(src_ref=src, dst_ref=dst, send_sem=s_sems.at[i], recv_sem=r_sems.at[i],
                                              device_id=dev, device_id_type=MESH)
            cp.wait_send()
            cp.wait_recv()

    bufs = list(srcs) + list(lands)
    res = pl.pallas_call(
        body, name=name, out_shape=tuple(pltpu.HBM(t.shape, t.dtype) for t in bufs),
        in_specs=[HBM_SPEC] * n_b + [SEM_SPEC, SEM_SPEC, ANY],
        out_specs=tuple([HBM_SPEC] * n_b),
        input_output_aliases={i: i for i in range(n_b)},
        compiler_params=pltpu.CompilerParams(has_side_effects=EFFECT),
    )(*bufs, send_sems, recv_sems, after)
    return res[:n_s], res[n_s:]


def _row_tile(r):
    return ROW_TILE if r % ROW_TILE == 0 else r


def _rs_add_sibling(gp, recv, c_arr, name):
    _, r, l = gp.shape
    tr = r if r <= 4 * ROW_TILE else _row_tile(r)

    def body(c_ref, g_ref, r_ref, pf_ref, pb_ref):
        sm = g_ref[...].astype(F32) + r_ref[...].astype(F32)
        pf_ref[...] = sm
        pb_ref[...] = sm.astype(BF16)

    spec = pl.BlockSpec((None, tr, l), lambda k, i, c: (k, i, 0))
    return _pallas_call(
        body, name=name,
        grid_spec=pltpu.PrefetchScalarGridSpec(
            num_scalar_prefetch=1, grid=(4, r // tr),
            in_specs=[pl.BlockSpec((None, tr, l), lambda k, i, c: (2 * k + c[0], i, 0)), spec],
            out_specs=[spec, spec]),
        out_shape=[_sds((4, r, l), F32), _sds((4, r, l), BF16)], compiler_params=_params(2, 32),
    )(c_arr, gp, recv)


def _adam_update(w, gv, m, v):
    nm = ADAM_B1 * m + (1.0 - ADAM_B1) * gv
    nv = ADAM_B2 * v + (1.0 - ADAM_B2) * (gv * gv)
    m_hat = nm / (1.0 - ADAM_B1 ** ADAM_STEP)
    v_hat = nv / (1.0 - ADAM_B2 ** ADAM_STEP)
    return -ADAM_LR * (m_hat / (jnp.sqrt(v_hat) + ADAM_EPS) + ADAM_WD * w), nm, nv


def _rs_finish_adamw(pf, recv, k_arr, w, m, v, name):
    _, r, l = pf.shape
    tr = _row_tile(r)

    def body(k_ref, p_ref, r_ref, w_ref, m_ref, v_ref, g_ref, d_ref, nm_ref, nv_ref):
        gv = ((p_ref[...] + r_ref[0].astype(F32)) + r_ref[1].astype(F32)) + r_ref[2].astype(F32)
        g_ref[...] = gv
        d_ref[...], nm_ref[...], nv_ref[...] = _adam_update(w_ref[...], gv, m_ref[...], v_ref[...])

    spec = pl.BlockSpec((tr, l), lambda i, k: (i, 0))
    return _pallas_call(
        body, name=name,
        grid_spec=pltpu.PrefetchScalarGridSpec(
            num_scalar_prefetch=1, grid=(r // tr,),
            in_specs=[pl.BlockSpec((None, tr, l), lambda i, k: (k[0], i, 0)),
                      pl.BlockSpec((3, tr, l), lambda i, k: (0, i, 0)), spec, spec, spec],
            out_specs=[spec] * 4),
        out_shape=[_plain((r, l), F32)] * 4, compiler_params=_params(1, 32),
    )(k_arr, pf, recv, w, m, v)


def _rs_finish_adamw_t(pf, recv, k_arr, w_t, m_t, v_t, name):
    _, r, c = pf.shape
    tr = _row_tile(r)
    cp = -(-c // LANES) * LANES

    def body(k_ref, p_ref, r_ref, w_ref, m_ref, v_ref, g_ref, d_ref, nm_ref, nv_ref, pad):
        gv = ((p_ref[...] + r_ref[0].astype(F32)) + r_ref[1].astype(F32)) + r_ref[2].astype(F32)
        pad[...] = jnp.zeros_like(pad)
        pad[:, 0:c] = gv
        gt = pad[...].T[0:c, :]
        g_ref[...] = gt
        d_ref[...], nm_ref[...], nv_ref[...] = _adam_update(w_ref[...], gt, m_ref[...], v_ref[...])

    spec = pl.BlockSpec((c, tr), lambda i, k: (0, i))
    return _pallas_call(
        body, name=name,
        grid_spec=pltpu.PrefetchScalarGridSpec(
            num_scalar_prefetch=1, grid=(r // tr,),
            in_specs=[pl.BlockSpec((None, tr, c), lambda i, k: (k[0], i, 0)),
                      pl.BlockSpec((3, tr, c), lambda i, k: (0, i, 0)), spec, spec, spec],
            out_specs=[spec] * 4, scratch_shapes=[pltpu.VMEM((tr, cp), F32)]),
        out_shape=[_plain((c, r), F32)] * 4, compiler_params=_params(1, 32),
    )(k_arr, pf, recv, w_t, m_t, v_t)


def _sum_devices(g):
    def body(g_ref, o_ref):
        acc = g_ref[0]
        for j in range(1, N_DEV):
            acc = acc + g_ref[j]
        o_ref[...] = acc

    return _pallas_call(body, name="sum_devices", out_shape=_plain(g.shape[1:], F32))(g)


def _adamw(w, g, m, v, name):
    shape = w.shape
    w2, g2, m2, v2 = [t.reshape((-1, shape[-1])) for t in (w, g, m, v)]

    def body(w_ref, g_ref, m_ref, v_ref, d_ref, nm_ref, nv_ref):
        d_ref[...], nm_ref[...], nv_ref[...] = _adam_update(w_ref[...], g_ref[...], m_ref[...], v_ref[...])

    outs = _pallas_call(body, name=name, out_shape=[_plain(w2.shape, F32)] * 3)(w2, g2, m2, v2)
    return tuple(t.reshape(shape) for t in outs)


def _after(t, token):
    return t + token[0:1, 0:1].astype(t.dtype)


def _finish(name, pf, recv, k_arr, w, m, v):
    if name in ("attn_w_in", "conv_w_in"):
        res = _rs_finish_adamw_t(pf, recv, k_arr, w.T, m.T, v.T, "rs_finish_adamw_" + name)
        return tuple(t.T for t in res)
    return _rs_finish_adamw(pf, recv, k_arr, w, m, v, "rs_finish_adamw_" + name)


def kernel(x, mem, positions, norm_g, mem_norm_g, w_mem_kv, attn_w_in, attn_w_out, conv_w_in, conv_w, conv_w_out, final_g, loss_target, m_norm_g, m_mem_norm_g, m_w_mem_kv, m_attn_w_in, m_attn_w_out, m_conv_w_in, m_conv_w, m_conv_w_out, m_final_g, v_norm_g, v_mem_norm_g, v_w_mem_kv, v_attn_w_in, v_attn_w_out, v_conv_w_in, v_conv_w, v_conv_w_out, v_final_g):
    px, py, pc = _position()
    me = 4 * px + 2 * py + pc
    c_arr = jnp.reshape(pc, (1,)).astype(jnp.int32)
    k_arr = jnp.reshape(2 * px + py, (1,)).astype(jnp.int32)
    x, mem, pos, tgt = x[0], mem[0], positions[0], loss_target[0]

    wg_in0 = _all_gather_relay(attn_w_in[0].astype(BF16), "gather_w_in0")
    late = [attn_w_out[0].astype(BF16), conv_w_in[0].astype(BF16), conv_w_out[0].astype(BF16),
            w_mem_kv.astype(BF16).reshape(-1, w_mem_kv.shape[2]), jnp.pad(conv_w[0], ((0, 5), (0, 0)))]
    lands = [lax.dynamic_update_slice(lax.empty((N_DEV,) + t.shape, t.dtype), t[None], (me, 0, 0)) for t in late]
    late_weights, late_token = _split_start(late, lands, _plan_gather_own, 4, wg_in0, "gather_late_start")

    tabs = _rope_tables(pos)
    g0, g1 = _after(norm_g[0:1], late_token), norm_g[1:2]

    hn0, hn0_t, qs, ks, vs, tabs_v, qm0, z0 = _inproj_attn(x, g0, wg_in0, tabs)
    os_, ls_ = [], []
    for j, d in enumerate(DILATIONS):
        if j == 2:
            _, lands = _split_wait(late_weights, _plan_gather_own, ls_[1], "gather_late_wait")
            late_weights, late_token = _split_start([], lands, _plan_gather_pass, 3, ls_[1], "gather_late_pass_start")
        o, l = _attn_fwd(qs[j], ks[j], vs[j], d, late_token)
        os_.append(o)
        ls_.append(l)

    _, gathered = _split_wait(late_weights, _plan_gather_pass, ls_[2], "gather_late_pass_wait")
    wg_out0, wg_in1, wg_out1, wg_kv, cw_all = gathered
    w_out1 = wg_out1.reshape(-1, wg_out1.shape[2])
    n_kv = w_mem_kv.shape[1]
    w_kv = wg_kv.reshape(N_DEV, 2, n_kv, -1).transpose(1, 0, 2, 3).reshape(2, N_DEV * n_kv, -1)
    cw = cw_all[:, 0:3].transpose(1, 0, 2).reshape(3, -1)
    kv = _memkv_fwd(mem, mem_norm_g, w_kv)
    y0, y0_t, mo0, h1 = _post_attn(os_, ls_, qm0, kv[0], z0, x, wg_out0)

    hn1, hn1_t, bg, cg, u, qm1, z1 = _inproj_conv(h1, g1, wg_in1)
    y1, y1_t, mo1, dh2, dh2b, loss_acc, d_final_g = _post_conv_loss(
        bg, cg, u, qm1, kv[1], z1, h1, w_out1, cw, final_g.reshape(1, -1), tgt)

    d_w_out1 = _wgrad_rows(y1_t, dh2b, "wgrad_out1")
    dz1, dbg, dconv, dqm1, dkv1 = _bwd_post_conv(dh2b, w_out1, bg, cg, u, z1, qm1, kv[1], mo1, cw)
    dcg, du, dcw = _bwd_conv(dconv, cg, u, cw)
    dh1, dh1b, dg1, dproj1 = _dgrad_norm([(dbg, 1), (dcg, 1), (du, 1), (dqm1, 1), (dz1, 1)], wg_in1, h1, g1, dh2,
                                         "dgrad_norm_conv")
    d_w_in1 = _wgrad_shards(hn1_t, dproj1, "wgrad_in1")

    d_w_out0 = _wgrad_cols(y0_t, dh1b, wg_out0.shape[2], "wgrad_out0")

    names1 = ["conv_w_in", "conv_w_out", "attn_w_out"]
    grads1 = [d_w_in1, d_w_out1, d_w_out0]
    started, token = _split_start(grads1, [lax.empty((4,) + g.shape[1:], g.dtype) for g in grads1],
                                  _plan_to_sibling, 4, dg1, "rs1_sibling_start")

    gw = GROUP_WIDTH
    ones = (jnp.arange(gw)[:, None] // HEAD_DIM == jnp.arange(gw)[None, :] // HEAD_DIM).astype(BF16)
    ones = _after(ones, token)
    res = _bwd_post_attn(dh1b, wg_out0, z0, os_, ls_, qm0, kv[0], mo0, ones)
    dz0, dos, dls, dqm0, dkv0 = res[0], res[1:4], res[4:7], res[7], res[8]

    grads1, from_sibling = _split_wait(started, _plan_to_sibling, dz0, "rs1_sibling_wait")
    parts1 = [_rs_add_sibling(g, r, c_arr, "rs_add_sibling_" + n) for g, r, n in zip(grads1, from_sibling, names1)]
    pbs1 = [pb for _, pb in parts1]
    started, token = _split_start(pbs1, [lax.empty((3,) + p.shape[1:], p.dtype) for p in pbs1],
                                  _plan_to_chips, 3, dg1, "rs1_chips_start")

    dqs, dks, dvs = [], [], []
    for j, d in enumerate(DILATIONS):
        dq, dk, dv = _attn_bwd(qs[j], ks[j], vs[j], ls_[j], dos[j], dls[j], tabs_v[j], d, token)
        dqs.append((dq, d))
        dks.append((dk, d))
        dvs.append((dv, d))
    d_w_kv, d_mem_g = _memkv_bwd(jnp.stack([dkv0, dkv1]), w_kv, mem, mem_norm_g)
    dproj0 = _assemble_dproj(dqs + dks + dvs + [(dqm0, 1), (dz0, 1)], wg_in0.shape[2], "assemble_dproj_attn")
    d_w_in0 = _wgrad_shards(hn0_t, dproj0, "wgrad_in0")

    names0 = ["attn_w_in", "w_mem_kv"]
    grads0 = [d_w_in0, d_w_kv]
    started0, token0 = _split_start(grads0, [lax.empty((4,) + g.shape[1:], g.dtype) for g in grads0],
                                    _plan_to_sibling, 4, dg1, "rs0_sibling_start")
    _, from_chips1 = _split_wait(started, _plan_to_chips, token0, "rs1_chips_wait")
    shard = dict(attn_w_in=(attn_w_in[0], m_attn_w_in[0], v_attn_w_in[0]),
                 attn_w_out=(attn_w_out[0], m_attn_w_out[0], v_attn_w_out[0]),
                 conv_w_in=(conv_w_in[0], m_conv_w_in[0], v_conv_w_in[0]),
                 conv_w_out=(conv_w_out[0], m_conv_w_out[0], v_conv_w_out[0]),
                 w_mem_kv=tuple(t.reshape(-1, t.shape[2]) for t in (w_mem_kv, m_w_mem_kv, v_w_mem_kv)))
    big = {}
    for n, (pf, _), r in zip(names1, parts1, from_chips1):
        big[n] = _finish(n, pf, r, k_arr, *shard[n])

    grads0, from_sibling = _split_wait(started0, _plan_to_sibling, big["conv_w_out"][1], "rs0_sibling_wait")
    parts0 = [_rs_add_sibling(g, r, c_arr, "rs_add_sibling_" + n) for g, r, n in zip(grads0, from_sibling, names0)]
    pbs0 = [pb for _, pb in parts0]
    started0, token0 = _split_start(pbs0, [lax.empty((3,) + p.shape[1:], p.dtype) for p in pbs0],
                                    _plan_to_chips, 3, dg1, "rs0_chips_start")
    dx, dg0 = _dgrad_norm_dm(dproj0, wg_in0, x, g0, dh1, token0, "dgrad_norm_attn")

    small_part = jnp.concatenate([dg0, dg1, d_mem_g.reshape(2, -1), d_final_g, dcw[0:3]], axis=0)
    small_part = jnp.concatenate([small_part, jnp.broadcast_to(loss_acc[0, 0], small_part.shape)], axis=0)
    small = _sum_devices(_all_gather([small_part], dg0, "gather_small_grads")[0])
    loss = small[8, 0]
    g_conv_w = lax.dynamic_slice(small[5:8], (0, me * LANES), (3, LANES))[None]
    small_g = dict(norm_g=small[0:2], mem_norm_g=small[2:4], conv_w=g_conv_w, final_g=small[4])
    small_w = dict(norm_g=(norm_g, m_norm_g, v_norm_g), mem_norm_g=(mem_norm_g, m_mem_norm_g, v_mem_norm_g),
                   conv_w=(conv_w, m_conv_w, v_conv_w), final_g=(final_g, m_final_g, v_final_g))
    for n, (w, m, v) in small_w.items():
        big[n] = (small_g[n],) + _adamw(w, small_g[n], m, v, "adamw_" + n)

    _, from_chips0 = _split_wait(started0, _plan_to_chips, big["final_g"][1], "rs0_chips_wait")
    for n, (pf, _), r in zip(names0, parts0, from_chips0):
        big[n] = _finish(n, pf, r, k_arr, *shard[n])
    for n in ("attn_w_in", "attn_w_out", "conv_w_in", "conv_w_out"):
        big[n] = tuple(t[None] for t in big[n])
    big["w_mem_kv"] = tuple(t.reshape(w_mem_kv.shape) for t in big["w_mem_kv"])

    order = ["norm_g", "mem_norm_g", "w_mem_kv", "attn_w_in", "attn_w_out", "conv_w_in", "conv_w", "conv_w_out", "final_g"]
    return (loss, dx[None], *[big[n][0] for n in order], *[big[n][1] for n in order],
            *[big[n][2] for n in order], *[big[n][3] for n in order])
```

```python
import jax
import jax.numpy as jnp
from jax import lax
from jax.experimental import pallas as pl
from jax.experimental.pallas import tpu as pltpu

F32 = jnp.float32
BF16 = jnp.bfloat16

N_DEV = 8
D_MODEL = 1024
HEAD_DIM = 64
ROT_DIM = HEAD_DIM // 4
ROPE_THETA = 500000.0
DILATIONS = (1, 4, 16)
HEADS_PER_GROUP = 8
GROUP_WIDTH = HEADS_PER_GROUP * HEAD_DIM
BLOCK = 128
N_MEM = 256
MEM_HEADS = 4
MEM_WIDTH = MEM_HEADS * HEAD_DIM
CONV_WIDTH = D_MODEL
EPS = 1e-6
SCALE = HEAD_DIM ** -0.5
NEG = -1e30

ADAM_LR = 0.001
ADAM_B1 = 0.9
ADAM_B2 = 0.999
ADAM_EPS = 1e-08
ADAM_WD = 0.01
ADAM_STEP = 10

ROW_TILE = 256
WGRAD_SHARDS = 4
LANES = 128
MESH = pl.DeviceIdType.MESH
ANY = pl.BlockSpec(memory_space=pl.ANY)


def _pallas_call(body, **kw):
    call = pl.pallas_call(body, **kw)

    def run(*args):
        pinned = [pltpu.with_memory_space_constraint(a, pltpu.HBM) if jnp.issubdtype(a.dtype, jnp.floating) else a
                  for a in args]
        return call(*pinned)

    return run


def _dot(a, b):
    return lax.dot_general(a, b, (((1,), (0,)), ((), ())), preferred_element_type=F32)


def _dot_nt(a, b):
    return lax.dot_general(a, b, (((1,), (1,)), ((), ())), preferred_element_type=F32)


def _dot_tn(a, b):
    return lax.dot_general(a, b, (((0,), (0,)), ((), ())), preferred_element_type=F32)


def _params(n_grid, vmem_mb=48):
    return pltpu.CompilerParams(dimension_semantics=("arbitrary",) * n_grid, vmem_limit_bytes=vmem_mb << 20)


def _rows(width, tm=ROW_TILE):
    return pl.BlockSpec((tm, width), lambda i: (i, 0))


def _view_rows(width, d, tm=ROW_TILE):
    return pl.BlockSpec((tm // d, d * width), lambda i: (i, 0))


def _whole(shape):
    return pl.BlockSpec(shape, lambda *_: (0,) * len(shape))


def _resident(shape):
    return pl.BlockSpec(shape, lambda *_: (0,) * len(shape), pipeline_mode=pl.Buffered(1))


def _sds(shape, dtype):
    return pltpu.HBM(shape, dtype)


def _plain(shape, dtype):
    return jax.ShapeDtypeStruct(shape, dtype)


def _silu_parts(z):
    sg = jax.nn.sigmoid(z)
    return z * sg, sg * (1.0 + z * (1.0 - sg))


def _to_view(scr, val, out_ref, d):
    tm, w = val.shape
    if d == 1:
        out_ref[...] = val.astype(out_ref.dtype)
        return
    for cb in range(w // LANES):
        scr[cb] = val[:, cb * LANES:(cb + 1) * LANES]
    for r in range(d):
        for cb in range(w // LANES):
            lo = r * w + cb * LANES
            out_ref[:, lo:lo + LANES] = scr[cb, pl.ds(r, tm // d, stride=d), :].astype(out_ref.dtype)


def _from_view(scr, in_ref, d):
    if d == 1:
        return in_ref[...].astype(F32)
    nc, tm, _ = scr.shape
    w = nc * LANES
    for r in range(d):
        for cb in range(nc):
            lo = r * w + cb * LANES
            scr[cb, pl.ds(r, tm // d, stride=d), :] = in_ref[:, lo:lo + LANES].astype(F32)
    return jnp.concatenate([scr[cb] for cb in range(nc)], axis=1)


def _view_scratch(width, tm=ROW_TILE):
    return pltpu.VMEM((width // LANES, tm, LANES), F32)


def _rope_tables(pos):
    half = ROT_DIM // 2
    inv_freq = ROPE_THETA ** (-jnp.arange(half, dtype=F32) * (2.0 / ROT_DIM))
    ang = pos.astype(F32)[:, None] * inv_freq
    cos, sin = jnp.cos(ang), jnp.sin(ang)
    s = pos.shape[0]
    z8 = jnp.zeros((s, half), F32)
    rest = HEAD_DIM - ROT_DIM
    cosf = jnp.concatenate([cos, cos, jnp.ones((s, rest), F32)], axis=1)
    sa = jnp.concatenate([-sin, z8, jnp.zeros((s, rest), F32)], axis=1)
    sb = jnp.concatenate([z8, sin, jnp.zeros((s, rest), F32)], axis=1)
    return tuple(jnp.tile(t, (1, LANES // HEAD_DIM)) for t in (cosf, sa, sb))


def _rope_fwd(t, cv, sav, sbv):
    w = t.shape[1]
    return t * cv + pltpu.roll(t, w - ROT_DIM // 2, 1) * sav + pltpu.roll(t, ROT_DIM // 2, 1) * sbv


def _rope_bwd(g, cv, sav, sbv):
    w = g.shape[1]
    return g * cv + pltpu.roll(g * sav, ROT_DIM // 2, 1) + pltpu.roll(g * sbv, w - ROT_DIM // 2, 1)


def _joined_columns(wg_ref):
    assert wg_ref.shape[2] % LANES == 0
    return jnp.concatenate([wg_ref[j] for j in range(N_DEV)], axis=1)


def _join_once(wg_ref, w_scr):
    c = wg_ref.shape[2]

    @pl.when(pl.program_id(0) == 0)
    def _():
        for j in range(N_DEV):
            w_scr[:, j * c:(j + 1) * c] = wg_ref[j]


def _inproj_attn(x, g, wg, tabs):
    s, d_model = x.shape
    gw = GROUP_WIDTH
    n = N_DEV * wg.shape[2]
    nz = n - 9 * gw - MEM_WIDTH
    reps = gw // LANES
    tm = ROW_TILE

    def body(x_ref, g_ref, w_ref, c_ref, sa_ref, sb_ref, hn_ref, hnt_ref, *rest):
        outs, (wj, scr, tscr) = rest[:-3], rest[-3:]
        q_refs, k_refs, v_refs, t_refs, qm_ref, z_ref = outs[0:3], outs[3:6], outs[6:9], outs[9:18], outs[18], outs[19]
        _join_once(w_ref, wj)
        xb = x_ref[...]
        r = lax.rsqrt(jnp.mean(xb * xb, axis=-1, keepdims=True) + EPS)
        hn = ((xb * r) * g_ref[...]).astype(BF16)
        hn_ref[...] = hn
        hnt_ref[...] = hn.T
        proj = lambda lo, hi: _dot(hn, wj[:, lo:hi])
        tab = (c_ref[...], sa_ref[...], sb_ref[...])
        cv, sav, sbv = [jnp.tile(t, (1, reps)) for t in tab]
        for j, d in enumerate(DILATIONS):
            tq = _rope_fwd(proj(j * gw, (j + 1) * gw), cv, sav, sbv)
            _to_view(scr, tq * SCALE, q_refs[j], d)
            tk = _rope_fwd(proj((3 + j) * gw, (4 + j) * gw), cv, sav, sbv)
            _to_view(scr, tk, k_refs[j], d)
            _to_view(scr, proj((6 + j) * gw, (7 + j) * gw), v_refs[j], d)
            for i in range(3):
                _to_view(tscr, tab[i], t_refs[3 * j + i], d)
        qm_ref[...] = proj(9 * gw, 9 * gw + MEM_WIDTH).astype(BF16)
        z_ref[...] = proj(9 * gw + MEM_WIDTH, n)

    views = [_sds((s // d, d * gw), BF16) for d in DILATIONS]
    tviews = [_sds((s // d, d * LANES), F32) for d in DILATIONS for _ in range(3)]
    out_shape = ([_sds((s, d_model), BF16), _sds((d_model, s), BF16)] + views * 3 + tviews
                 + [_sds((s, MEM_WIDTH), BF16), _sds((s, nz), F32)])
    vspecs = [_view_rows(gw, d, tm) for d in DILATIONS]
    tspecs = [_view_rows(LANES, d, tm) for d in DILATIONS for _ in range(3)]
    out_specs = ([_rows(d_model, tm), pl.BlockSpec((d_model, tm), lambda i: (0, i))] + vspecs * 3 + tspecs
                 + [_rows(MEM_WIDTH, tm), _rows(nz, tm)])
    res = _pallas_call(
        body, name="inproj_attn", grid=(s // tm,), out_shape=out_shape,
        in_specs=[_rows(d_model, tm), _whole((1, d_model)), _resident(wg.shape)] + [_rows(LANES, tm)] * 3,
        out_specs=out_specs,
        scratch_shapes=[pltpu.VMEM((d_model, n), BF16), _view_scratch(gw, tm), _view_scratch(LANES, tm)],
        compiler_params=_params(1, 60),
    )(x, g, wg, *tabs)
    tabs_v = [res[11 + 3 * j:14 + 3 * j] for j in range(3)]
    return res[0], res[1], res[2:5], res[5:8], res[8:11], tabs_v, res[20], res[21]


def _band_mask(n_keys):
    qi = lax.broadcasted_iota(jnp.int32, (BLOCK, n_keys), 0)
    kj = lax.broadcasted_iota(jnp.int32, (BLOCK, n_keys), 1)
    if n_keys == BLOCK:
        return kj <= qi
    return jnp.logical_or(jnp.logical_and(kj < BLOCK, kj >= qi), jnp.logical_and(kj >= BLOCK, (kj - BLOCK) <= qi))


def _low_head_lanes():
    return lax.broadcasted_iota(jnp.int32, (1, LANES), 1) < HEAD_DIM


def _split_pair(t, low):
    zero = jnp.zeros_like(t)
    return jnp.where(low, t, zero), jnp.where(low, zero, t)


def _pair_specs(d, nb, w):
    if nb == 1:
        return pl.BlockSpec((BLOCK, 2 * w), lambda n: (0, n)), None
    half = nb // 2
    two = pl.BlockSpec((2 * BLOCK, w), lambda n: (n % half, n // half))
    before = pl.BlockSpec((BLOCK, w), lambda n: (jnp.maximum(2 * (n % half) - 1, 0), n // half))
    return two, before


def _head_tiles(w, col0):
    return ([slice(p * LANES, (p + 1) * LANES) for p in range(w // LANES)],
            [slice(col0 + p * LANES, col0 + (p + 1) * LANES) for p in range(w // LANES)])


def _attend_fwd(q_ref, o_ref, lse_ref, rows, col0, kk, vv):
    w = kk.shape[1]
    valid = _band_mask(kk.shape[0])
    low = _low_head_lanes()
    pairs, qcols = _head_tiles(w, col0)
    qs_ = [h for qc in qcols for h in _split_pair(q_ref[rows, qc], low)]
    k2s = [kk[:, pr] for pr in pairs for _ in range(2)]
    scs = [jnp.where(valid, _dot_nt(qh, k2), NEG) for qh, k2 in zip(qs_, k2s)]
    ms = [jnp.max(sc, axis=-1, keepdims=True) for sc in scs]
    ps = [jnp.exp(sc - m) for sc, m in zip(scs, ms)]
    ls = [jnp.sum(p, axis=-1, keepdims=True) for p in ps]
    pns = [(p * (1.0 / l)).astype(BF16) for p, l in zip(ps, ls)]
    for i, (pr, qc) in enumerate(zip(pairs, qcols)):
        v2 = vv[:, pr]
        a, b = 2 * i, 2 * i + 1
        o_ref[rows, qc] = jnp.where(low, _dot(pns[a], v2), _dot(pns[b], v2))
        lse_ref[rows, qc] = jnp.where(low, ms[a] + jnp.log(ls[a]), ms[b] + jnp.log(ls[b]))


TOP, BOTTOM = slice(0, BLOCK), slice(BLOCK, 2 * BLOCK)


def _attn_fwd(q, k, v, d, token):
    ln, dw = q.shape
    w = dw // d
    nb = ln // BLOCK
    two, before = _pair_specs(d, nb, w)

    def body_streams(_, q_ref, kc_ref, vc_ref, o_ref, lse_ref):
        for sb in range(2):
            cols = slice(sb * w, (sb + 1) * w)
            _attend_fwd(q_ref, o_ref, lse_ref, TOP, sb * w, kc_ref[:, cols], vc_ref[:, cols])

    def body_blocks(_, q_ref, kp_ref, kc_ref, vp_ref, vc_ref, o_ref, lse_ref):
        first = pl.program_id(0) % (nb // 2) == 0
        pl.when(first)(lambda: _attend_fwd(q_ref, o_ref, lse_ref, TOP, 0, kc_ref[TOP, :], vc_ref[TOP, :]))
        pl.when(jnp.logical_not(first))(lambda: _attend_fwd(
            q_ref, o_ref, lse_ref, TOP, 0, jnp.concatenate([kp_ref[...], kc_ref[TOP, :]], axis=0),
            jnp.concatenate([vp_ref[...], vc_ref[TOP, :]], axis=0)))
        _attend_fwd(q_ref, o_ref, lse_ref, BOTTOM, 0, kc_ref[...], vc_ref[...])

    if nb == 1:
        body, in_specs, args = body_streams, [ANY, two, two, two], (token, q, k, v)
    else:
        body, in_specs, args = body_blocks, [ANY, two, before, two, before, two], (token, q, k, k, v, v)
    return _pallas_call(
        body, name=f"attn_fwd_d{d}", grid=(d * nb // 2,), out_shape=[_sds((ln, dw), F32)] * 2,
        in_specs=in_specs, out_specs=[two, two], compiler_params=_params(1, 32),
    )(*args)


def _memkv_fwd(mem, g, w):
    n_layers = w.shape[0]

    def body(mem_ref, g_ref, w_ref, kv_ref):
        mb = mem_ref[...]
        r = lax.rsqrt(jnp.mean(mb * mb, axis=-1, keepdims=True) + EPS)
        mn = ((mb * r) * g_ref[...]).astype(BF16)
        kv_ref[...] = _dot(mn, w_ref[...]).astype(BF16)

    return _pallas_call(
        body, name="memkv_fwd", grid=(n_layers,),
        out_shape=_plain((n_layers, N_MEM, 2 * MEM_WIDTH), BF16),
        in_specs=[_whole(mem.shape), pl.BlockSpec((None, 1, D_MODEL), lambda l: (l, 0, 0)),
                  pl.BlockSpec((None, D_MODEL, 2 * MEM_WIDTH), lambda l: (l, 0, 0))],
        out_specs=pl.BlockSpec((None, N_MEM, 2 * MEM_WIDTH), lambda l: (l, 0, 0)),
        compiler_params=_params(1, 32),
    )(mem, g.reshape(n_layers, 1, D_MODEL), w)


def _mix_groups(os_, ls_):
    mx = jnp.maximum(jnp.maximum(ls_[0], ls_[1]), ls_[2])
    es = [jnp.exp(t - mx) for t in ls_]
    inv = 1.0 / (es[0] + es[1] + es[2])
    ws = [e * inv for e in es]
    mix = ws[0] * os_[0] + ws[1] * os_[1] + ws[2] * os_[2]
    return ws, mix


MEM_PAIRS = [slice(p * LANES, (p + 1) * LANES) for p in range(MEM_WIDTH // LANES)]


def _mem_probs(qhs, k2s):
    scs = [_dot_nt(qh, k2) * SCALE for qh, k2 in zip(qhs, k2s)]
    es = [jnp.exp(sc - jnp.max(sc, axis=-1, keepdims=True)) for sc in scs]
    return [e * (1.0 / jnp.sum(e, axis=-1, keepdims=True)) for e in es]


def _mem_attn_into(qm, kv_ref, mo_ref):
    low = _low_head_lanes()
    qhs = [h for pr in MEM_PAIRS for h in _split_pair(qm[:, pr], low)]
    k2s = [kv_ref[:, pr] for pr in MEM_PAIRS for _ in range(2)]
    ps = [p.astype(BF16) for p in _mem_probs(qhs, k2s)]
    for i, pr in enumerate(MEM_PAIRS):
        v2 = kv_ref[:, MEM_WIDTH + i * LANES:MEM_WIDTH + (i + 1) * LANES]
        mo_ref[:, pr] = jnp.where(low, _dot(ps[2 * i], v2), _dot(ps[2 * i + 1], v2))


def _mem_attn_bwd(qm, kv_ref, dmem, dqm_ref, dkv_ref):
    low = _low_head_lanes()
    dmb = dmem.astype(BF16)
    vps = [slice(MEM_WIDTH + i * LANES, MEM_WIDTH + (i + 1) * LANES) for i in range(len(MEM_PAIRS))]
    qhs = [h for pr in MEM_PAIRS for h in _split_pair(qm[:, pr], low)]
    dhs = [h for pr in MEM_PAIRS for h in _split_pair(dmb[:, pr], low)]
    k2s = [kv_ref[:, pr] for pr in MEM_PAIRS for _ in range(2)]
    v2s = [kv_ref[:, vp] for vp in vps for _ in range(2)]
    ps = _mem_probs(qhs, k2s)
    dps = [_dot_nt(dh, v2) for dh, v2 in zip(dhs, v2s)]
    dss = [(p * (dp - jnp.sum(dp * p, axis=-1, keepdims=True)) * SCALE).astype(BF16) for p, dp in zip(ps, dps)]
    pbs = [p.astype(BF16) for p in ps]
    for i, (pr, vp) in enumerate(zip(MEM_PAIRS, vps)):
        a, b = 2 * i, 2 * i + 1
        dqm_ref[:, pr] = jnp.where(low, _dot(dss[a], k2s[a]), _dot(dss[b], k2s[b])).astype(BF16)
        dkv_ref[:, pr] += _dot_tn(dss[a], qhs[a]) + _dot_tn(dss[b], qhs[b])
        dkv_ref[:, vp] += _dot_tn(pbs[a], dhs[a]) + _dot_tn(pbs[b], dhs[b])


def _post_attn(os_, ls_, qm, kv, z, x, wg_out):
    s, d_model = x.shape
    gw = GROUP_WIDTH
    nb = gw + MEM_WIDTH
    tm = ROW_TILE

    def body(o0, o1, o2, l0, l1, l2, qm_ref, kv_ref, z_ref, x_ref, w_ref, y_ref, yt_ref, mo_ref, h_ref, s0, s1):
        ov, lv = [], []
        for o_ref, l_ref, d in zip((o0, o1, o2), (l0, l1, l2), DILATIONS):
            ov.append(_from_view(s0, o_ref, d))
            lv.append(_from_view(s1, l_ref, d))
        _, mix = _mix_groups(ov, lv)
        _mem_attn_into(qm_ref[...], kv_ref, mo_ref)
        sz, _ = _silu_parts(z_ref[...])
        y_ref[:, :gw] = (mix * sz[:, :gw]).astype(BF16)
        y_ref[:, gw:] = (mo_ref[...] * sz[:, gw:]).astype(BF16)
        y = y_ref[...]
        yt_ref[...] = y.T
        h_ref[...] = x_ref[...] + _dot(y, _joined_columns(w_ref))

    vspecs = [_view_rows(gw, d) for d in DILATIONS]
    return _pallas_call(
        body, name="post_attn", grid=(s // tm,),
        out_shape=[_sds((s, nb), BF16), _sds((nb, s), BF16), _sds((s, MEM_WIDTH), F32), _sds((s, d_model), F32)],
        in_specs=vspecs * 2 + [_rows(MEM_WIDTH), _whole(kv.shape), _rows(nb), _rows(d_model), _whole(wg_out.shape)],
        out_specs=[_rows(nb), pl.BlockSpec((nb, tm), lambda i: (0, i)), _rows(MEM_WIDTH), _rows(d_model)],
        scratch_shapes=[_view_scratch(gw), _view_scratch(gw)],
        compiler_params=_params(1, 40),
    )(*os_, *ls_, qm, kv, z, x, wg_out)


def _inproj_conv(x, g, wg):
    s, d_model = x.shape
    c = CONV_WIDTH
    n = N_DEV * wg.shape[2]
    nz = n - 3 * c - MEM_WIDTH
    tm = ROW_TILE

    def body(x_ref, g_ref, w_ref, hn_ref, hnt_ref, bg_ref, cg_ref, u_ref, qm_ref, z_ref, wj):
        _join_once(w_ref, wj)
        xb = x_ref[...]
        r = lax.rsqrt(jnp.mean(xb * xb, axis=-1, keepdims=True) + EPS)
        hn = ((xb * r) * g_ref[...]).astype(BF16)
        hn_ref[...] = hn
        hnt_ref[...] = hn.T
        bg_ref[...] = _dot(hn, wj[:, 0:c])
        cg_ref[...] = _dot(hn, wj[:, c:2 * c])
        u_ref[...] = _dot(hn, wj[:, 2 * c:3 * c])
        qm_ref[...] = _dot(hn, wj[:, 3 * c:3 * c + MEM_WIDTH]).astype(BF16)
        z_ref[...] = _dot(hn, wj[:, 3 * c + MEM_WIDTH:])

    return _pallas_call(
        body, name="inproj_conv", grid=(s // tm,),
        out_shape=[_sds((s, d_model), BF16), _sds((d_model, s), BF16)] + [_sds((s, c), F32)] * 3
                  + [_sds((s, MEM_WIDTH), BF16), _sds((s, nz), F32)],
        in_specs=[_rows(d_model), _whole((1, d_model)), _resident(wg.shape)],
        out_specs=[_rows(d_model), pl.BlockSpec((d_model, tm), lambda i: (0, i))] + [_rows(c)] * 3
                  + [_rows(MEM_WIDTH), _rows(nz)],
        scratch_shapes=[pltpu.VMEM((d_model, n), BF16)],
        compiler_params=_params(1, 60),
    )(x, g, wg)


HALO = 8


def _halo_before(width, tm=ROW_TILE):
    return pl.BlockSpec((HALO, width), lambda i: (jnp.maximum(i * (tm // HALO) - 1, 0), 0))


def _halo_after(width, n_rows, tm=ROW_TILE):
    return pl.BlockSpec((HALO, width), lambda i: (jnp.minimum((i + 1) * (tm // HALO), n_rows // HALO - 1), 0))


def _conv_taps(cg_ref, u_ref, cgh_ref, uh_ref, i):
    a = cg_ref[...] * u_ref[...]
    ah = jnp.where(i > 0, cgh_ref[...] * uh_ref[...], 0.0)
    row = lax.broadcasted_iota(jnp.int32, a.shape, 0)
    a1 = jnp.where(row == 0, ah[HALO - 1:HALO], pltpu.roll(a, 1, 0))
    a2 = jnp.where(row == 0, ah[HALO - 2:HALO - 1], jnp.where(row == 1, ah[HALO - 1:HALO], pltpu.roll(a, 2, 0)))
    return a, a1, a2


def _post_conv_loss(bg, cg, u, qm, kv, z, h1, w_out, cw, gf, tgt):
    s, d = h1.shape
    c = CONV_WIDTH
    nb = c + MEM_WIDTH
    tm = ROW_TILE

    def body(bg_ref, cg_ref, u_ref, cgh_ref, uh_ref, qm_ref, kv_ref, z_ref, h_ref, w_ref, cw_ref, gf_ref, t_ref,
             y_ref, yt_ref, mo_ref, dh_ref, dhb_ref, loss_ref, dgf_ref):
        i = pl.program_id(0)
        a, a1, a2 = _conv_taps(cg_ref, u_ref, cgh_ref, uh_ref, i)
        conv = cw_ref[0:1, :] * a2 + cw_ref[1:2, :] * a1 + cw_ref[2:3, :] * a
        mix = bg_ref[...] * conv
        _mem_attn_into(qm_ref[...], kv_ref, mo_ref)
        sz, _ = _silu_parts(z_ref[...])
        y_ref[:, :c] = (mix * sz[:, :c]).astype(BF16)
        y_ref[:, c:] = (mo_ref[...] * sz[:, c:]).astype(BF16)
        y = y_ref[...]
        yt_ref[...] = y.T
        h2 = h_ref[...] + _dot(y, w_ref[...])
        r = lax.rsqrt(jnp.mean(h2 * h2, axis=-1, keepdims=True) + EPS)
        nh = h2 * r
        gfv = gf_ref[...]
        diff = nh * gfv - t_ref[...]
        dout = diff * (1.0 / d)
        dn = dout * gfv
        dh2 = r * dn - h2 * ((r * r * r) * jnp.mean(dn * h2, axis=-1, keepdims=True))
        dh_ref[...] = dh2
        dhb_ref[...] = dh2.astype(BF16)

        @pl.when(i == 0)
        def _():
            loss_ref[...] = jnp.zeros_like(loss_ref)
            dgf_ref[...] = jnp.zeros_like(dgf_ref)

        loss_ref[...] += 0.5 * jnp.sum(jnp.mean(diff * diff, axis=-1, keepdims=True))
        dgf_ref[...] += jnp.sum(dout * nh, axis=0, keepdims=True)

    return _pallas_call(
        body, name="post_conv_loss", grid=(s // tm,),
        out_shape=[_sds((s, nb), BF16), _sds((nb, s), BF16), _sds((s, MEM_WIDTH), F32), _sds((s, d), F32),
                   _sds((s, d), BF16), _plain((8, LANES), F32), _plain((1, d), F32)],
        in_specs=[_rows(c)] * 3 + [_halo_before(c)] * 2 + [_rows(MEM_WIDTH), _whole(kv.shape), _rows(nb), _rows(d),
                  _whole(w_out.shape), _whole(cw.shape), _whole((1, d)), _rows(d)],
        out_specs=[_rows(nb), pl.BlockSpec((nb, tm), lambda i: (0, i)), _rows(MEM_WIDTH), _rows(d), _rows(d),
                   _whole((8, LANES)), _whole((1, d))],
        compiler_params=_params(1, 48),
    )(bg, cg, u, cg, u, qm, kv, z, h1, w_out, cw, gf, tgt)


def _bwd_post_conv(dhb, w_out, bg, cg, u, z, qm, kv, mo, cw):
    s = dhb.shape[0]
    c = CONV_WIDTH
    nb = c + MEM_WIDTH

    def body(dh_ref, w_ref, bg_ref, cg_ref, u_ref, cgh_ref, uh_ref, z_ref, qm_ref, kv_ref, mo_ref, cw_ref,
             dz_ref, dbg_ref, dc_ref, dqm_ref, dkv_ref):
        i = pl.program_id(0)

        @pl.when(i == 0)
        def _():
            dkv_ref[...] = jnp.zeros_like(dkv_ref)

        dy = _dot_nt(dh_ref[...], w_ref[...])
        sz, dsz = _silu_parts(z_ref[...])
        a, a1, a2 = _conv_taps(cg_ref, u_ref, cgh_ref, uh_ref, i)
        conv = cw_ref[0:1, :] * a2 + cw_ref[1:2, :] * a1 + cw_ref[2:3, :] * a
        bgv = bg_ref[...]
        dz_ref[:, :c] = (dy[:, :c] * (bgv * conv) * dsz[:, :c]).astype(BF16)
        dz_ref[:, c:] = (dy[:, c:] * mo_ref[...] * dsz[:, c:]).astype(BF16)
        dbr = dy * sz
        dmix = dbr[:, :c]
        dbg_ref[...] = (dmix * conv).astype(BF16)
        dc_ref[...] = dmix * bgv
        _mem_attn_bwd(qm_ref[...], kv_ref, dbr[:, c:], dqm_ref, dkv_ref)

    return _pallas_call(
        body, name="bwd_post_conv", grid=(s // ROW_TILE,),
        out_shape=[_sds((s, nb), BF16), _sds((s, c), BF16), _sds((s, c), F32), _sds((s, MEM_WIDTH), BF16),
                   _plain(kv.shape, F32)],
        in_specs=[_rows(D_MODEL), _whole(w_out.shape)] + [_rows(c)] * 3 + [_halo_before(c)] * 2
                 + [_rows(nb), _rows(MEM_WIDTH), _whole(kv.shape), _rows(MEM_WIDTH), _whole(cw.shape)],
        out_specs=[_rows(nb), _rows(c), _rows(c), _rows(MEM_WIDTH), _whole(kv.shape)],
        compiler_params=_params(1, 48),
    )(dhb, w_out, bg, cg, u, cg, u, z, qm, kv, mo, cw)


def _bwd_conv(dconv, cg, u, cw):
    s, c = dconv.shape
    tm = ROW_TILE
    last = s // tm - 1

    def body(dc_ref, dcn_ref, cg_ref, u_ref, cgh_ref, uh_ref, cw_ref, dcg_ref, du_ref, dcw_ref):
        i = pl.program_id(0)

        @pl.when(i == 0)
        def _():
            dcw_ref[...] = jnp.zeros_like(dcw_ref)

        dc = dc_ref[...]
        dcn = jnp.where(i < last, dcn_ref[...], 0.0)
        row = lax.broadcasted_iota(jnp.int32, dc.shape, 0)
        d1 = jnp.where(row == tm - 1, dcn[0:1], pltpu.roll(dc, tm - 1, 0))
        d2 = jnp.where(row == tm - 1, dcn[1:2], jnp.where(row == tm - 2, dcn[0:1], pltpu.roll(dc, tm - 2, 0)))
        da = cw_ref[2:3, :] * dc + cw_ref[1:2, :] * d1 + cw_ref[0:1, :] * d2
        a, a1, a2 = _conv_taps(cg_ref, u_ref, cgh_ref, uh_ref, i)
        dcg_ref[...] = (da * u_ref[...]).astype(BF16)
        du_ref[...] = (da * cg_ref[...]).astype(BF16)
        dcw_ref[0:1, :] += jnp.sum(dc * a2, axis=0, keepdims=True)
        dcw_ref[1:2, :] += jnp.sum(dc * a1, axis=0, keepdims=True)
        dcw_ref[2:3, :] += jnp.sum(dc * a, axis=0, keepdims=True)

    return _pallas_call(
        body, name="bwd_conv", grid=(s // tm,),
        out_shape=[_sds((s, c), BF16), _sds((s, c), BF16), _plain((8, c), F32)],
        in_specs=[_rows(c), _halo_after(c, s), _rows(c), _rows(c), _halo_before(c), _halo_before(c), _whole(cw.shape)],
        out_specs=[_rows(c), _rows(c), _whole((8, c))], compiler_params=_params(1, 40),
    )(dconv, dconv, cg, u, cg, u, cw)


def _dgrad_norm(pieces, wg, h, g, dres, name):
    s, d_model = h.shape
    c = wg.shape[2]
    n = N_DEV * c
    tm = ROW_TILE
    widths = [p.shape[1] // d for p, d in pieces]
    assert sum(widths) == n
    n_p = len(pieces)

    def body(*refs):
        p_refs = refs[:n_p]
        w_ref, h_ref, g_ref, dr_ref, dh_ref, dhb_ref, dg_ref, dpd_ref, dp, scr, wj = refs[n_p:]
        _join_once(w_ref, wj)

        @pl.when(pl.program_id(0) == 0)
        def _():
            dg_ref[...] = jnp.zeros_like(dg_ref)

        off = 0
        for p_ref, (_, d), wd in zip(p_refs, pieces, widths):
            if d == 1:
                dp[:, off:off + wd] = p_ref[...]
            else:
                dp[:, off:off + wd] = _from_view(scr, p_ref, d).astype(BF16)
            off += wd
        for j in range(N_DEV):
            dpd_ref[j] = dp[:, j * c:(j + 1) * c]
        dhn = _dot_nt(dp[...], wj[...])
        hb = h_ref[...]
        r = lax.rsqrt(jnp.mean(hb * hb, axis=-1, keepdims=True) + EPS)
        dg_ref[...] += jnp.sum(dhn * (hb * r), axis=0, keepdims=True)
        dn = dhn * g_ref[...]
        dh = dr_ref[...] + r * dn - hb * ((r * r * r) * jnp.mean(dn * hb, axis=-1, keepdims=True))
        dh_ref[...] = dh
        dhb_ref[...] = dh.astype(BF16)

    p_specs = [_view_rows(wd, d) for (_, d), wd in zip(pieces, widths)]
    return _pallas_call(
        body, name=name, grid=(s // tm,),
        out_shape=[_plain((s, d_model), F32), _sds((s, d_model), BF16), _plain((1, d_model), F32), _sds((N_DEV, s, c), BF16)],
        in_specs=p_specs + [_resident(wg.shape), _rows(d_model), _whole((1, d_model)), _rows(d_model)],
        out_specs=[_rows(d_model), _rows(d_model), _whole((1, d_model)), pl.BlockSpec((N_DEV, tm, c), lambda i: (0, i, 0))],
        scratch_shapes=[pltpu.VMEM((tm, n), BF16), _view_scratch(GROUP_WIDTH), pltpu.VMEM((d_model, n), BF16)],
        compiler_params=_params(1, 60),
    )(*[p for p, _ in pieces], wg, h, g, dres)


def _assemble_dproj(pieces, c, name):
    n = N_DEV * c
    tm = ROW_TILE
    widths = [p.shape[1] // d for p, d in pieces]
    assert sum(widths) == n
    s = pieces[0][0].shape[0] * pieces[0][1]
    n_p = len(pieces)

    def body(*refs):
        p_refs, (dpd_ref, dp, scr) = refs[:n_p], refs[n_p:]
        off = 0
        for p_ref, (_, d), wd in zip(p_refs, pieces, widths):
            if d == 1:
                dp[:, off:off + wd] = p_ref[...]
            else:
                dp[:, off:off + wd] = _from_view(scr, p_ref, d).astype(BF16)
            off += wd
        for j in range(N_DEV):
            dpd_ref[j] = dp[:, j * c:(j + 1) * c]

    return _pallas_call(
        body, name=name, grid=(s // tm,), out_shape=_sds((N_DEV, s, c), BF16),
        in_specs=[_view_rows(wd, d) for (_, d), wd in zip(pieces, widths)],
        out_specs=pl.BlockSpec((N_DEV, tm, c), lambda i: (0, i, 0)),
        scratch_shapes=[pltpu.VMEM((tm, n), BF16), _view_scratch(GROUP_WIDTH)],
        compiler_params=_params(1, 40),
    )(*[p for p, _ in pieces])


def _dgrad_norm_dm(dproj_dm, wg, h, g, dres, token, name):
    s, d_model = h.shape
    c = wg.shape[2]
    tm = ROW_TILE

    def body(_, dp_ref, w_ref, h_ref, g_ref, dr_ref, dh_ref, dg_ref):
        @pl.when(pl.program_id(0) == 0)
        def _():
            dg_ref[...] = jnp.zeros_like(dg_ref)

        dhn = jnp.zeros((tm, d_model), F32)
        for j in range(N_DEV):
            dhn += _dot_nt(dp_ref[j], w_ref[j])
        hb = h_ref[...]
        r = lax.rsqrt(jnp.mean(hb * hb, axis=-1, keepdims=True) + EPS)
        dg_ref[...] += jnp.sum(dhn * (hb * r), axis=0, keepdims=True)
        dn = dhn * g_ref[...]
        dh_ref[...] = dr_ref[...] + r * dn - hb * ((r * r * r) * jnp.mean(dn * hb, axis=-1, keepdims=True))

    return _pallas_call(
        body, name=name, grid=(s // tm,),
        out_shape=[_plain((s, d_model), F32), _plain((1, d_model), F32)],
        in_specs=[ANY, pl.BlockSpec((N_DEV, tm, c), lambda i: (0, i, 0)), _whole(wg.shape), _rows(d_model),
                  _whole((1, d_model)), _rows(d_model)],
        out_specs=[_rows(d_model), _whole((1, d_model))],
        compiler_params=_params(1, 60),
    )(token, dproj_dm, wg, h, g, dres)


def _wgrad_shards(a_t, b_dm, name):
    m, s = a_t.shape
    c = b_dm.shape[2]

    def body(a_ref, b_ref, o_ref):
        o_ref[...] = _dot(a_ref[...], b_ref[...]).astype(BF16)

    return _pallas_call(
        body, name=name, grid=(N_DEV,), out_shape=_sds((N_DEV, m, c), BF16),
        in_specs=[_whole(a_t.shape), pl.BlockSpec((None, s, c), lambda j: (j, 0, 0))],
        out_specs=pl.BlockSpec((None, m, c), lambda j: (j, 0, 0)), compiler_params=_params(1, 40),
    )(a_t, b_dm)


def _wgrad_cols(a_t, b, c, name):
    m, s = a_t.shape
    assert c % LANES == 0

    def body(a_ref, b_ref, o_ref):
        wide = _dot(a_ref[...], b_ref[...]).astype(BF16)
        for j in range(WGRAD_SHARDS):
            o_ref[j] = wide[:, j * c:(j + 1) * c]

    return _pallas_call(
        body, name=name, grid=(N_DEV // WGRAD_SHARDS,), out_shape=_sds((N_DEV, m, c), BF16),
        in_specs=[_whole(a_t.shape), pl.BlockSpec((s, WGRAD_SHARDS * c), lambda j: (0, j))],
        out_specs=pl.BlockSpec((WGRAD_SHARDS, m, c), lambda j: (j, 0, 0)), compiler_params=_params(1, 40),
    )(a_t, b)


def _wgrad_rows(a_t, b, name):
    m, s = a_t.shape
    n = b.shape[1]
    mr = m // N_DEV

    def body(a_ref, b_ref, o_ref):
        o_ref[...] = _dot(a_ref[...], b_ref[...]).astype(BF16).reshape(WGRAD_SHARDS, mr, n)

    return _pallas_call(
        body, name=name, grid=(N_DEV // WGRAD_SHARDS,), out_shape=_sds((N_DEV, mr, n), BF16),
        in_specs=[pl.BlockSpec((WGRAD_SHARDS * mr, s), lambda j: (j, 0)), _whole(b.shape)],
        out_specs=pl.BlockSpec((WGRAD_SHARDS, mr, n), lambda j: (j, 0, 0)), compiler_params=_params(1, 40),
    )(a_t, b)


def _memkv_bwd(dkv, w, mem, g):
    n_layers = w.shape[0]
    rows = D_MODEL // N_DEV

    def body(dkv_ref, w_ref, mem_ref, g_ref, dw_ref, dg_ref):
        mb = mem_ref[...]
        r = lax.rsqrt(jnp.mean(mb * mb, axis=-1, keepdims=True) + EPS)
        nm = mb * r
        mn = (nm * g_ref[...]).astype(BF16)
        dkvb = dkv_ref[...].astype(BF16)
        dw_ref[...] = _dot_tn(mn, dkvb).astype(BF16).reshape(N_DEV, rows, 2 * MEM_WIDTH)
        dmn = _dot_nt(dkvb, w_ref[...])
        dg_ref[...] = jnp.sum(dmn * nm, axis=0, keepdims=True)

    lay = lambda *shape: pl.BlockSpec((None,) + shape, lambda l: (l, 0, 0))
    return _pallas_call(
        body, name="memkv_bwd", grid=(n_layers,),
        out_shape=[_sds((N_DEV, n_layers * rows, 2 * MEM_WIDTH), BF16), _plain((n_layers, 1, D_MODEL), F32)],
        in_specs=[lay(N_MEM, 2 * MEM_WIDTH), lay(D_MODEL, 2 * MEM_WIDTH), _whole(mem.shape), lay(1, D_MODEL)],
        out_specs=[pl.BlockSpec((N_DEV, rows, 2 * MEM_WIDTH), lambda l: (0, l, 0)), lay(1, D_MODEL)],
        compiler_params=_params(1, 32),
    )(dkv, w, mem, g.reshape(n_layers, 1, D_MODEL))


def _bwd_post_attn(dhb, wg_out, z, os_, ls_, qm, kv, mo, head_ones):
    s = dhb.shape[0]
    gw = GROUP_WIDTH
    nb = gw + MEM_WIDTH
    tm = ROW_TILE

    def body(dh_ref, w_ref, z_ref, o0, o1, o2, l0, l1, l2, qm_ref, kv_ref, mo_ref, bd_ref,
             dz_ref, do0, do1, do2, dl0, dl1, dl2, dqm_ref, dkv_ref, s0, s1):
        @pl.when(pl.program_id(0) == 0)
        def _():
            dkv_ref[...] = jnp.zeros_like(dkv_ref)

        dy = _dot_nt(dh_ref[...], _joined_columns(w_ref))
        ov, lv = [], []
        for o_ref, l_ref, d in zip((o0, o1, o2), (l0, l1, l2), DILATIONS):
            ov.append(_from_view(s0, o_ref, d))
            lv.append(_from_view(s1, l_ref, d))
        ws, mix = _mix_groups(ov, lv)
        sz, dsz = _silu_parts(z_ref[...])
        dz_ref[:, :gw] = (dy[:, :gw] * mix * dsz[:, :gw]).astype(BF16)
        dz_ref[:, gw:] = (dy[:, gw:] * mo_ref[...] * dsz[:, gw:]).astype(BF16)
        dbr = dy * sz
        dmix = dbr[:, :gw]
        t = dmix * mix
        th = t.astype(BF16)
        tl = (t - th.astype(F32)).astype(BF16)
        rs = _dot(th, bd_ref[...]) + _dot(tl, bd_ref[...])
        for wg_, do_ref, dl_ref, d in zip(ws, (do0, do1, do2), (dl0, dl1, dl2), DILATIONS):
            _to_view(s0, wg_ * dmix, do_ref, d)
            _to_view(s1, wg_ * rs, dl_ref, d)
        _mem_attn_bwd(qm_ref[...], kv_ref, dbr[:, gw:], dqm_ref, dkv_ref)

    vspecs = [_view_rows(gw, d) for d in DILATIONS]
    return _pallas_call(
        body, name="bwd_post_attn", grid=(s // tm,),
        out_shape=[_sds((s, nb), BF16)] + [_sds((s // d, d * gw), BF16) for d in DILATIONS]
                  + [_sds((s // d, d * gw), F32) for d in DILATIONS] + [_sds((s, MEM_WIDTH), BF16), _plain(kv.shape, F32)],
        in_specs=[_rows(D_MODEL), _whole(wg_out.shape), _rows(nb)] + vspecs * 2
                 + [_rows(MEM_WIDTH), _whole(kv.shape), _rows(MEM_WIDTH), _whole(head_ones.shape)],
        out_specs=[_rows(nb)] + vspecs * 2 + [_rows(MEM_WIDTH), _whole(kv.shape)],
        scratch_shapes=[_view_scratch(gw), _view_scratch(gw)],
        compiler_params=_params(1, 48),
    )(dhb, wg_out, z, *os_, *ls_, qm, kv, mo, head_ones)


def _attn_bwd(q, k, v, lse, do, dl, tabs, d, token):
    ln, dw = q.shape
    w = dw // d
    nb = ln // BLOCK
    reps = w // LANES
    two, before = _pair_specs(d, nb, w)
    two_t, _ = _pair_specs(d, nb, LANES)

    def attend(q_ref, l_ref, do_ref, dl_ref, dqs, acck, accv, rows, col0, kk, vv, acc_rows):
        valid = _band_mask(kk.shape[0])
        low = _low_head_lanes()
        pairs, qcols = _head_tiles(w, col0)
        cols = [slice(col0 + h * HEAD_DIM, col0 + h * HEAD_DIM + 1) for h in range(HEADS_PER_GROUP)]
        qhs = [h for qc in qcols for h in _split_pair(q_ref[rows, qc], low)]
        dobs = [h for qc in qcols for h in _split_pair(do_ref[rows, qc], low)]
        k2s = [kk[:, pr] for pr in pairs for _ in range(2)]
        v2s = [vv[:, pr] for pr in pairs for _ in range(2)]
        scs = [jnp.where(valid, _dot_nt(qh, k2), NEG) for qh, k2 in zip(qhs, k2s)]
        dps = [_dot_nt(dob, v2) for dob, v2 in zip(dobs, v2s)]
        ps = [jnp.exp(sc - l_ref[rows, col]) for sc, col in zip(scs, cols)]
        dss = [(p * (dp - dl_ref[rows, col])).astype(BF16) for p, dp, col in zip(ps, dps, cols)]
        pbs = [p.astype(BF16) for p in ps]
        for i, qc in enumerate(qcols):
            a, b = 2 * i, 2 * i + 1
            dqs[rows, qc] = jnp.where(low, _dot(dss[a], k2s[a]), _dot(dss[b], k2s[b])) * SCALE
            acck[acc_rows, qc] += _dot_tn(dss[a], qhs[a]) + _dot_tn(dss[b], qhs[b])
            accv[acc_rows, qc] += _dot_tn(pbs[a], dobs[a]) + _dot_tn(pbs[b], dobs[b])

    def body_streams(_, q_ref, kc_ref, vc_ref, l_ref, do_ref, dl_ref, c_ref, sa_ref, sb_ref,
                     dq_ref, dk_ref, dv_ref, acck, accv, dqs):
        acck[...] = jnp.zeros_like(acck)
        accv[...] = jnp.zeros_like(accv)
        for sb in range(2):
            cols = slice(sb * w, (sb + 1) * w)
            attend(q_ref, l_ref, do_ref, dl_ref, dqs, acck, accv, TOP, sb * w, kc_ref[:, cols], vc_ref[:, cols], TOP)
        tabs2 = [jnp.concatenate([jnp.tile(r[:, sb * LANES:(sb + 1) * LANES], (1, reps)) for sb in range(2)], axis=1)
                 for r in (c_ref, sa_ref, sb_ref)]
        dq_ref[...] = _rope_bwd(dqs[...], *tabs2).astype(BF16)
        dk_ref[...] = _rope_bwd(acck[...], *tabs2).astype(BF16)
        dv_ref[...] = accv[...].astype(BF16)

    def body_blocks(_, q_ref, kp_ref, kc_ref, vp_ref, vc_ref, l_ref, do_ref, dl_ref, cq, saq, sbq, ck, sak, sbk,
                    dq_ref, dk_ref, dv_ref, acck, accv, dqs):
        i = pl.program_id(0) % (nb // 2)

        @pl.when(i == 0)
        def _():
            acck[...] = jnp.zeros_like(acck)
            accv[...] = jnp.zeros_like(accv)

        refs = (q_ref, l_ref, do_ref, dl_ref, dqs, acck, accv)
        pl.when(i == 0)(lambda: attend(*refs, TOP, 0, kc_ref[TOP, :], vc_ref[TOP, :], TOP))
        pl.when(i != 0)(lambda: attend(
            *refs, TOP, 0, jnp.concatenate([kp_ref[...], kc_ref[TOP, :]], axis=0),
            jnp.concatenate([vp_ref[...], vc_ref[TOP, :]], axis=0),
            pl.ds(pl.multiple_of((2 * i - 1) * BLOCK, BLOCK), 2 * BLOCK)))
        attend(*refs, BOTTOM, 0, kc_ref[...], vc_ref[...], pl.ds(pl.multiple_of(2 * i * BLOCK, BLOCK), 2 * BLOCK))
        tq = [jnp.tile(r[...], (1, reps)) for r in (cq, saq, sbq)]
        dq_ref[...] = _rope_bwd(dqs[...], *tq).astype(BF16)

        @pl.when(i == nb // 2 - 1)
        def _():
            for r0 in range(0, nb * BLOCK, 2 * BLOCK):
                rows = slice(r0, r0 + 2 * BLOCK)
                tk = [jnp.tile(r[rows, :], (1, reps)) for r in (ck, sak, sbk)]
                dk_ref[rows, :] = _rope_bwd(acck[rows, :], *tk).astype(BF16)
                dv_ref[rows, :] = accv[rows, :].astype(BF16)

    if nb == 1:
        body = body_streams
        in_specs = [ANY] + [two] * 6 + [two_t] * 3
        args = (token, q, k, v, lse, do, dl, *tabs)
        out_specs = [two, two, two]
        acc_shape = (BLOCK, 2 * w)
    else:
        body = body_blocks
        stream = pl.BlockSpec((nb * BLOCK, w), lambda n: (0, n // (nb // 2)))
        stream_t = pl.BlockSpec((nb * BLOCK, LANES), lambda n: (0, n // (nb // 2)))
        in_specs = [ANY, two, before, two, before, two, two, two, two] + [two_t] * 3 + [stream_t] * 3
        args = (token, q, k, k, v, v, lse, do, dl, *tabs, *tabs)
        out_specs = [two, stream, stream]
        acc_shape = (nb * BLOCK, w)
    return _pallas_call(
        body, name=f"attn_bwd_d{d}", grid=(d * nb // 2,), out_shape=[_sds((ln, dw), BF16)] * 3,
        in_specs=in_specs, out_specs=out_specs,
        scratch_shapes=[pltpu.VMEM(acc_shape, F32), pltpu.VMEM(acc_shape, F32), pltpu.VMEM(two.block_shape, F32)],
        compiler_params=_params(1, 48),
    )(*args)


def _position():
    return lax.axis_index("x"), lax.axis_index("y"), lax.axis_index("c")


def _all_gather(shards, after, name):
    n_a = len(shards)

    def body(*refs):
        x_refs, out_refs = refs[:n_a], refs[n_a + 1:2 * n_a + 1]
        send_sems, recv_sems, local_sems = refs[2 * n_a + 1:]
        x, y, c = _position()
        me, sibling = (x, y, c), (x, y, 1 - c)
        chips = [(1 - x, y), (x, 1 - y), (1 - x, 1 - y)]

        def rows(a, px, py, pc):
            return out_refs[a].at[4 * px + 2 * py + pc]

        def copy(a, k, block, to, own=False):
            return pltpu.make_async_remote_copy(
                src_ref=x_refs[a] if own else rows(a, *block), dst_ref=rows(a, *block),
                send_sem=send_sems.at[a, k], recv_sem=recv_sems.at[a, k], device_id=to, device_id_type=MESH)

        mine = [pltpu.make_async_copy(x_refs[a], rows(a, *me), local_sems.at[a]) for a in range(n_a)]
        for cp in mine:
            cp.start()
        first = []
        for j, chip in enumerate(chips):
            first += [copy(a, 1 + j, me, (*chip, c), own=True) for a in range(n_a)]
        first += [copy(a, 0, me, sibling, own=True) for a in range(n_a)]
        for cp in first:
            cp.start()
        passed = []
        for j, chip in enumerate(chips):
            for a in range(n_a):
                copy(a, 1 + j, (*chip, c), me).wait_recv()
                fwd = copy(a, 4 + j, (*chip, c), sibling)
                fwd.start()
                passed.append(fwd)
        for a in range(n_a):
            copy(a, 0, sibling, me).wait_recv()
        for j, chip in enumerate(chips):
            for a in range(n_a):
                copy(a, 4 + j, (*chip, 1 - c), me).wait_recv()
        for cp in first + passed:
            cp.wait_send()
        for cp in mine:
            cp.wait()

    return _pallas_call(
        body, name=name, out_shape=[_sds((N_DEV,) + t.shape, t.dtype) for t in shards],
        in_specs=[ANY] * (n_a + 1), out_specs=[ANY] * n_a,
        scratch_shapes=[pltpu.SemaphoreType.DMA((n_a, 7)), pltpu.SemaphoreType.DMA((n_a, 7)),
                        pltpu.SemaphoreType.DMA((n_a,))],
    )(*shards, after)


def _all_gather_relay(xs, name):
    def body(x_ref, out_ref, send_sems, recv_sems, local_sem):
        x, y, c = _position()
        me, sibling = (x, y, c), (x, y, 1 - c)
        xn, yn, diag = (1 - x, y, c), (x, 1 - y, c), (1 - x, 1 - y, c)
        src_nb = (x + c * (1 - 2 * x), y + (1 - c) * (1 - 2 * y), c)
        dst_nb = (x + (1 - c) * (1 - 2 * x), y + c * (1 - 2 * y), c)

        def rows(dev):
            return out_ref.at[4 * dev[0] + 2 * dev[1] + dev[2]]

        def copy(k, block, to, own=False):
            return pltpu.make_async_remote_copy(
                src_ref=x_ref if own else rows(block), dst_ref=rows(block),
                send_sem=send_sems.at[k], recv_sem=recv_sems.at[k], device_id=to, device_id_type=MESH)

        mine = pltpu.make_async_copy(x_ref, rows(me), local_sem)
        mine.start()
        first = [copy(1, me, xn, own=True), copy(2, me, yn, own=True), copy(0, me, sibling, own=True)]
        for cp in first:
            cp.start()
        copy(1, xn, me).wait_recv()
        copy(2, yn, me).wait_recv()
        relay = copy(3, src_nb, dst_nb)
        relay.start()
        passed = [copy(4, xn, sibling), copy(5, yn, sibling)]
        for cp in passed:
            cp.start()
        copy(3, diag, me).wait_recv()
        last = copy(6, diag, sibling)
        last.start()
        copy(0, sibling, me).wait_recv()
        for k, blk in ((4, (1 - x, y, 1 - c)), (5, (x, 1 - y, 1 - c)), (6, (1 - x, 1 - y, 1 - c))):
            copy(k, blk, me).wait_recv()
        for cp in first + [relay] + passed + [last]:
            cp.wait_send()
        mine.wait()

    return _pallas_call(
        body, name=name, out_shape=_sds((N_DEV,) + xs.shape, xs.dtype),
        in_specs=[ANY], out_specs=ANY,
        scratch_shapes=[pltpu.SemaphoreType.DMA((7,)), pltpu.SemaphoreType.DMA((7,)), pltpu.SemaphoreType.DMA],
    )(xs)


HBM_SPEC = pl.BlockSpec(memory_space=pltpu.HBM)
SEM_SPEC = pl.BlockSpec(memory_space=pltpu.SEMAPHORE)
EFFECT = pltpu.SideEffectType.DATAFLOW_SIDE_EFFECTING
def _plan_gather_own(src_refs, land_refs):
    x, y, c = _position()
    me = 4 * x + 2 * y + c
    peers = [(x, y, 1 - c), (1 - x, y, c), (x, 1 - y, c), (1 - x, 1 - y, c)]
    return [(src_refs[a], land_refs[a].at[me], (a, k), peer) for k, peer in enumerate(peers) for a in range(len(src_refs))]


def _plan_gather_pass(src_refs, land_refs):
    x, y, c = _position()
    chips = [(1 - x, y), (x, 1 - y), (1 - x, 1 - y)]
    return [(land_refs[a].at[4 * px + 2 * py + c], land_refs[a].at[4 * px + 2 * py + c], (a, j), (x, y, 1 - c))
            for j, (px, py) in enumerate(chips) for a in range(len(land_refs))]


def _plan_to_sibling(src_refs, land_refs):
    x, y, c = _position()
    return [(src_refs[a].at[2 * k + (1 - c)], land_refs[a].at[k], (a, k), (x, y, 1 - c))
            for k in range(4) for a in range(len(src_refs))]


def _plan_to_chips(src_refs, land_refs):
    x, y, c = _position()
    chips = [(1 - x, y), (x, 1 - y), (1 - x, 1 - y)]
    return [(src_refs[a].at[2 * px + py], land_refs[a].at[j], (a, j), (px, py, c))
            for j, (px, py) in enumerate(chips) for a in range(len(src_refs))]


def _split_start(srcs, lands, plan, n_sem, after, name):
    n_s, n_a = len(srcs), len(lands)
    n_b = n_s + n_a

    def body(*refs):
        src_refs, land_refs = refs[:n_s], refs[n_s:n_b]
        send_sems, recv_sems, token = refs[n_b + 1], refs[n_b + 2], refs[-1]
        for src, dst, (a, k), dev in plan(src_refs, land_refs):
            i = a * n_sem + k
            pltpu.make_async_remote_copy(src_ref=src, dst_ref=dst, send_sem=send_sems.at[i], recv_sem=recv_sems.at[i],
                                         device_id=dev, device_id_type=MESH).start()
        token[...] = jnp.zeros_like(token)

    bufs = list(srcs) + list(lands)
    res = pl.pallas_call(
        body, name=name,
        out_shape=(pltpu.SemaphoreType.DMA((n_a * n_sem,)), pltpu.SemaphoreType.DMA((n_a * n_sem,)),
                   *[pltpu.HBM(t.shape, t.dtype) for t in bufs], _plain((8, LANES), F32)),
        in_specs=[HBM_SPEC] * n_b + [ANY],
        out_specs=(SEM_SPEC, SEM_SPEC, *[HBM_SPEC] * n_b, pl.BlockSpec(memory_space=pltpu.VMEM)),
        input_output_aliases={i: 2 + i for i in range(n_b)},
        compiler_params=pltpu.CompilerParams(has_side_effects=EFFECT),
    )(*[pltpu.with_memory_space_constraint(t, pltpu.HBM) for t in bufs], after)
    return (res[0], res[1], res[2:2 + n_s], res[2 + n_s:2 + n_b]), res[-1]


def _split_wait(started, plan, after, name):
    send_sems, recv_sems, srcs, lands = started
    n_s, n_a = len(srcs), len(lands)
    n_b = n_s + n_a
    n_sem = send_sems.shape[0] // n_a

    def body(*refs):
        src_refs, land_refs = refs[:n_s], refs[n_s:n_b]
        s_sems, r_sems = refs[n_b], refs[n_b + 1]
        for src, dst, (a, k), dev in plan(src_refs, land_refs):
            i = a * n_sem + k
            cp = pltpu.make_async_remote_copy(src_ref=src, dst_ref=dst, send_sem=s_sems.at[i], recv_sem=r_sems.at[i],
                                              device_id=dev, device_id_type=MESH)
            cp.wait_send()
            cp.wait_recv()

    bufs = list(srcs) + list(lands)
    res = pl.pallas_call(
        body, name=name, out_shape=tuple(pltpu.HBM(t.shape, t.dtype) for t in bufs),
        in_specs=[HBM_SPEC] * n_b + [SEM_SPEC, SEM_SPEC, ANY],
        out_specs=tuple([HBM_SPEC] * n_b),
        input_output_aliases={i: i for i in range(n_b)},
        compiler_params=pltpu.CompilerParams(has_side_effects=EFFECT),
    )(*bufs, send_sems, recv_sems, after)
    return res[:n_s], res[n_s:]


def _row_tile(r):
    return ROW_TILE if r % ROW_TILE == 0 else r


def _rs_add_sibling(gp, recv, c_arr, name):
    _, r, l = gp.shape
    tr = r if r <= 4 * ROW_TILE else _row_tile(r)

    def body(c_ref, g_ref, r_ref, pf_ref, pb_ref):
        sm = g_ref[...].astype(F32) + r_ref[...].astype(F32)
        pf_ref[...] = sm
        pb_ref[...] = sm.astype(BF16)

    spec = pl.BlockSpec((None, tr, l), lambda k, i, c: (k, i, 0))
    return _pallas_call(
        body, name=name,
        grid_spec=pltpu.PrefetchScalarGridSpec(
            num_scalar_prefetch=1, grid=(4, r // tr),
            in_specs=[pl.BlockSpec((None, tr, l), lambda k, i, c: (2 * k + c[0], i, 0)), spec],
            out_specs=[spec, spec]),
        out_shape=[_sds((4, r, l), F32), _sds((4, r, l), BF16)], compiler_params=_params(2, 32),
    )(c_arr, gp, recv)


def _adam_update(w, gv, m, v):
    nm = ADAM_B1 * m + (1.0 - ADAM_B1) * gv
    nv = ADAM_B2 * v + (1.0 - ADAM_B2) * (gv * gv)
    m_hat = nm / (1.0 - ADAM_B1 ** ADAM_STEP)
    v_hat = nv / (1.0 - ADAM_B2 ** ADAM_STEP)
    return -ADAM_LR * (m_hat / (jnp.sqrt(v_hat) + ADAM_EPS) + ADAM_WD * w), nm, nv


def _rs_finish_adamw(pf, recv, k_arr, w, m, v, name):
    _, r, l = pf.shape
    tr = _row_tile(r)

    def body(k_ref, p_ref, r_ref, w_ref, m_ref, v_ref, g_ref, d_ref, nm_ref, nv_ref):
        gv = ((p_ref[...] + r_ref[0].astype(F32)) + r_ref[1].astype(F32)) + r_ref[2].astype(F32)
        g_ref[...] = gv
        d_ref[...], nm_ref[...], nv_ref[...] = _adam_update(w_ref[...], gv, m_ref[...], v_ref[...])

    spec = pl.BlockSpec((tr, l), lambda i, k: (i, 0))
    return _pallas_call(
        body, name=name,
        grid_spec=pltpu.PrefetchScalarGridSpec(
            num_scalar_prefetch=1, grid=(r // tr,),
            in_specs=[pl.BlockSpec((None, tr, l), lambda i, k: (k[0], i, 0)),
                      pl.BlockSpec((3, tr, l), lambda i, k: (0, i, 0)), spec, spec, spec],
            out_specs=[spec] * 4),
        out_shape=[_plain((r, l), F32)] * 4, compiler_params=_params(1, 32),
    )(k_arr, pf, recv, w, m, v)


def _rs_finish_adamw_t(pf, recv, k_arr, w_t, m_t, v_t, name):
    _, r, c = pf.shape
    tr = _row_tile(r)
    cp = -(-c // LANES) * LANES

    def body(k_ref, p_ref, r_ref, w_ref, m_ref, v_ref, g_ref, d_ref, nm_ref, nv_ref, pad):
        gv = ((p_ref[...] + r_ref[0].astype(F32)) + r_ref[1].astype(F32)) + r_ref[2].astype(F32)
        pad[...] = jnp.zeros_like(pad)
        pad[:, 0:c] = gv
        gt = pad[...].T[0:c, :]
        g_ref[...] = gt
        d_ref[...], nm_ref[...], nv_ref[...] = _adam_update(w_ref[...], gt, m_ref[...], v_ref[...])

    spec = pl.BlockSpec((c, tr), lambda i, k: (0, i))
    return _pallas_call(
        body, name=name,
        grid_spec=pltpu.PrefetchScalarGridSpec(
            num_scalar_prefetch=1, grid=(r // tr,),
            in_specs=[pl.BlockSpec((None, tr, c), lambda i, k: (k[0], i, 0)),
                      pl.BlockSpec((3, tr, c), lambda i, k: (0, i, 0)), spec, spec, spec],
            out_specs=[spec] * 4, scratch_shapes=[pltpu.VMEM((tr, cp), F32)]),
        out_shape=[_plain((c, r), F32)] * 4, compiler_params=_params(1, 32),
    )(k_arr, pf, recv, w_t, m_t, v_t)


def _sum_devices(g):
    def body(g_ref, o_ref):
        acc = g_ref[0]
        for j in range(1, N_DEV):
            acc = acc + g_ref[j]
        o_ref[...] = acc

    return _pallas_call(body, name="sum_devices", out_shape=_plain(g.shape[1:], F32))(g)


def _adamw(w, g, m, v, name):
    shape = w.shape
    w2, g2, m2, v2 = [t.reshape((-1, shape[-1])) for t in (w, g, m, v)]

    def body(w_ref, g_ref, m_ref, v_ref, d_ref, nm_ref, nv_ref):
        d_ref[...], nm_ref[...], nv_ref[...] = _adam_update(w_ref[...], g_ref[...], m_ref[...], v_ref[...])

    outs = _pallas_call(body, name=name, out_shape=[_plain(w2.shape, F32)] * 3)(w2, g2, m2, v2)
    return tuple(t.reshape(shape) for t in outs)


def _after(t, token):
    return t + token[0:1, 0:1].astype(t.dtype)


def _finish(name, pf, recv, k_arr, w, m, v):
    if name in ("attn_w_in", "conv_w_in"):
        res = _rs_finish_adamw_t(pf, recv, k_arr, w.T, m.T, v.T, "rs_finish_adamw_" + name)
        return tuple(t.T for t in res)
    return _rs_finish_adamw(pf, recv, k_arr, w, m, v, "rs_finish_adamw_" + name)


def kernel(x, mem, positions, norm_g, mem_norm_g, w_mem_kv, attn_w_in, attn_w_out, conv_w_in, conv_w, conv_w_out, final_g, loss_target, m_norm_g, m_mem_norm_g, m_w_mem_kv, m_attn_w_in, m_attn_w_out, m_conv_w_in, m_conv_w, m_conv_w_out, m_final_g, v_norm_g, v_mem_norm_g, v_w_mem_kv, v_attn_w_in, v_attn_w_out, v_conv_w_in, v_conv_w, v_conv_w_out, v_final_g):
    px, py, pc = _position()
    me = 4 * px + 2 * py + pc
    c_arr = jnp.reshape(pc, (1,)).astype(jnp.int32)
    k_arr = jnp.reshape(2 * px + py, (1,)).astype(jnp.int32)
    x, mem, pos, tgt = x[0], mem[0], positions[0], loss_target[0]

    wg_in0 = _all_gather_relay(attn_w_in[0].astype(BF16), "gather_w_in0")
    late = [attn_w_out[0].astype(BF16), conv_w_in[0].astype(BF16), conv_w_out[0].astype(BF16),
            w_mem_kv.astype(BF16).reshape(-1, w_mem_kv.shape[2]), jnp.pad(conv_w[0], ((0, 5), (0, 0)))]
    lands = [lax.dynamic_update_slice(lax.empty((N_DEV,) + t.shape, t.dtype), t[None], (me, 0, 0)) for t in late]
    late_weights, late_token = _split_start(late, lands, _plan_gather_own, 4, wg_in0, "gather_late_start")

    tabs = _rope_tables(pos)
    g0, g1 = _after(norm_g[0:1], late_token), norm_g[1:2]

    hn0, hn0_t, qs, ks, vs, tabs_v, qm0, z0 = _inproj_attn(x, g0, wg_in0, tabs)
    os_, ls_ = [], []
    for j, d in enumerate(DILATIONS):
        if j == 2:
            _, lands = _split_wait(late_weights, _plan_gather_own, ls_[1], "gather_late_wait")
            late_weights, late_token = _split_start([], lands, _plan_gather_pass, 3, ls_[1], "gather_late_pass_start")
        o, l = _attn_fwd(qs[j], ks[j], vs[j], d, late_token)
        os_.append(o)
        ls_.append(l)

    _, gathered = _split_wait(late_weights, _plan_gather_pass, ls_[2], "gather_late_pass_wait")
    wg_out0, wg_in1, wg_out1, wg_kv, cw_all = gathered
    w_out1 = wg_out1.reshape(-1, wg_out1.shape[2])
    n_kv = w_mem_kv.shape[1]
    w_kv = wg_kv.reshape(N_DEV, 2, n_kv, -1).transpose(1, 0, 2, 3).reshape(2, N_DEV * n_kv, -1)
    cw = cw_all[:, 0:3].transpose(1, 0, 2).reshape(3, -1)
    kv = _memkv_fwd(mem, mem_norm_g, w_kv)
    y0, y0_t, mo0, h1 = _post_attn(os_, ls_, qm0, kv[0], z0, x, wg_out0)

    hn1, hn1_t, bg, cg, u, qm1, z1 = _inproj_conv(h1, g1, wg_in1)
    y1, y1_t, mo1, dh2, dh2b, loss_acc, d_final_g = _post_conv_loss(
        bg, cg, u, qm1, kv[1], z1, h1, w_out1, cw, final_g.reshape(1, -1), tgt)

    d_w_out1 = _wgrad_rows(y1_t, dh2b, "wgrad_out1")
    dz1, dbg, dconv, dqm1, dkv1 = _bwd_post_conv(dh2b, w_out1, bg, cg, u, z1, qm1, kv[1], mo1, cw)
    dcg, du, dcw = _bwd_conv(dconv, cg, u, cw)
    dh1, dh1b, dg1, dproj1 = _dgrad_norm([(dbg, 1), (dcg, 1), (du, 1), (dqm1, 1), (dz1, 1)], wg_in1, h1, g1, dh2,
                                         "dgrad_norm_conv")
    d_w_in1 = _wgrad_shards(hn1_t, dproj1, "wgrad_in1")

    d_w_out0 = _wgrad_cols(y0_t, dh1b, wg_out0.shape[2], "wgrad_out0")

    names1 = ["conv_w_in", "conv_w_out", "attn_w_out"]
    grads1 = [d_w_in1, d_w_out1, d_w_out0]
    started, token = _split_start(grads1, [lax.empty((4,) + g.shape[1:], g.dtype) for g in grads1],
                                  _plan_to_sibling, 4, dg1, "rs1_sibling_start")

    gw = GROUP_WIDTH
    ones = (jnp.arange(gw)[:, None] // HEAD_DIM == jnp.arange(gw)[None, :] // HEAD_DIM).astype(BF16)
    ones = _after(ones, token)
    res = _bwd_post_attn(dh1b, wg_out0, z0, os_, ls_, qm0, kv[0], mo0, ones)
    dz0, dos, dls, dqm0, dkv0 = res[0], res[1:4], res[4:7], res[7], res[8]

    grads1, from_sibling = _split_wait(started, _plan_to_sibling, dz0, "rs1_sibling_wait")
    parts1 = [_rs_add_sibling(g, r, c_arr, "rs_add_sibling_" + n) for g, r, n in zip(grads1, from_sibling, names1)]
    pbs1 = [pb for _, pb in parts1]
    started, token = _split_start(pbs1, [lax.empty((3,) + p.shape[1:], p.dtype) for p in pbs1],
                                  _plan_to_chips, 3, dg1, "rs1_chips_start")

    dqs, dks, dvs = [], [], []
    for j, d in enumerate(DILATIONS):
        dq, dk, dv = _attn_bwd(qs[j], ks[j], vs[j], ls_[j], dos[j], dls[j], tabs_v[j], d, token)
        dqs.append((dq, d))
        dks.append((dk, d))
        dvs.append((dv, d))
    d_w_kv, d_mem_g = _memkv_bwd(jnp.stack([dkv0, dkv1]), w_kv, mem, mem_norm_g)
    dproj0 = _assemble_dproj(dqs + dks + dvs + [(dqm0, 1), (dz0, 1)], wg_in0.shape[2], "assemble_dproj_attn")
    d_w_in0 = _wgrad_shards(hn0_t, dproj0, "wgrad_in0")

    names0 = ["attn_w_in", "w_mem_kv"]
    grads0 = [d_w_in0, d_w_kv]
    started0, token0 = _split_start(grads0, [lax.empty((4,) + g.shape[1:], g.dtype) for g in grads0],
                                    _plan_to_sibling, 4, dg1, "rs0_sibling_start")
    _, from_chips1 = _split_wait(started, _plan_to_chips, token0, "rs1_chips_wait")
    shard = dict(attn_w_in=(attn_w_in[0], m_attn_w_in[0], v_attn_w_in[0]),
                 attn_w_out=(attn_w_out[0], m_attn_w_out[0], v_attn_w_out[0]),
                 conv_w_in=(conv_w_in[0], m_conv_w_in[0], v_conv_w_in[0]),
                 conv_w_out=(conv_w_out[0], m_conv_w_out[0], v_conv_w_out[0]),
                 w_mem_kv=tuple(t.reshape(-1, t.shape[2]) for t in (w_mem_kv, m_w_mem_kv, v_w_mem_kv)))
    big = {}
    for n, (pf, _), r in zip(names1, parts1, from_chips1):
        big[n] = _finish(n, pf, r, k_arr, *shard[n])

    grads0, from_sibling = _split_wait(started0, _plan_to_sibling, big["conv_w_out"][1], "rs0_sibling_wait")
    parts0 = [_rs_add_sibling(g, r, c_arr, "rs_add_sibling_" + n) for g, r, n in zip(grads0, from_sibling, names0)]
    pbs0 = [pb for _, pb in parts0]
    started0, token0 = _split_start(pbs0, [lax.empty((3,) + p.shape[1:], p.dtype) for p in pbs0],
                                    _plan_to_chips, 3, dg1, "rs0_chips_start")
    dx, dg0 = _dgrad_norm_dm(dproj0, wg_in0, x, g0, dh1, token0, "dgrad_norm_attn")

    small_part = jnp.concatenate([dg0, dg1, d_mem_g.reshape(2, -1), d_final_g, dcw[0:3]], axis=0)
    small_part = jnp.concatenate([small_part, jnp.broadcast_to(loss_acc[0, 0], small_part.shape)], axis=0)
    small = _sum_devices(_all_gather([small_part], dg0, "gather_small_grads")[0])
    loss = small[8, 0]
    g_conv_w = lax.dynamic_slice(small[5:8], (0, me * LANES), (3, LANES))[None]
    small_g = dict(norm_g=small[0:2], mem_norm_g=small[2:4], conv_w=g_conv_w, final_g=small[4])
    small_w = dict(norm_g=(norm_g, m_norm_g, v_norm_g), mem_norm_g=(mem_norm_g, m_mem_norm_g, v_mem_norm_g),
                   conv_w=(conv_w, m_conv_w, v_conv_w), final_g=(final_g, m_final_g, v_final_g))
    for n, (w, m, v) in small_w.items():
        big[n] = (small_g[n],) + _adamw(w, small_g[n], m, v, "adamw_" + n)

    _, from_chips0 = _split_wait(started0, _plan_to_chips, big["final_g"][1], "rs0_chips_wait")
    for n, (pf, _), r in zip(names0, parts0, from_chips0):
        big[n] = _finish(n, pf, r, k_arr, *shard[n])
    for n in ("attn_w_in", "attn_w_out", "conv_w_in", "conv_w_out"):
        big[n] = tuple(t[None] for t in big[n])
    big["w_mem_kv"] = tuple(t.reshape(w_mem_kv.shape) for t in big["w_mem_kv"])

    order = ["norm_g", "mem_norm_g", "w_mem_kv", "attn_w_in", "attn_w_out", "conv_w_in", "conv_w", "conv_w_out", "final_g"]
    return (loss, dx[None], *[big[n][0] for n in order], *[big[n][1] for n in order],
            *[big[n][2] for n in order], *[big[n][3] for n in order])
```

```python
import jax
import jax.numpy as jnp
from jax import lax
from jax.experimental import pallas as pl
from jax.experimental.pallas import tpu as pltpu

F32 = jnp.float32
BF16 = jnp.bfloat16

N_DEV = 8
D_MODEL = 1024
HEAD_DIM = 64
ROT_DIM = HEAD_DIM // 4
ROPE_THETA = 500000.0
DILATIONS = (1, 4, 16)
HEADS_PER_GROUP = 8
GROUP_WIDTH = HEADS_PER_GROUP * HEAD_DIM
BLOCK = 128
N_MEM = 256
MEM_HEADS = 4
MEM_WIDTH = MEM_HEADS * HEAD_DIM
CONV_WIDTH = D_MODEL
EPS = 1e-6
SCALE = HEAD_DIM ** -0.5
NEG = -1e30

ADAM_LR = 0.001
ADAM_B1 = 0.9
ADAM_B2 = 0.999
ADAM_EPS = 1e-08
ADAM_WD = 0.01
ADAM_STEP = 10

ROW_TILE = 256
WGRAD_SHARDS = 4
LANES = 128
MESH = pl.DeviceIdType.MESH
ANY = pl.BlockSpec(memory_space=pl.ANY)


def _pallas_call(body, **kw):
    call = pl.pallas_call(body, **kw)

    def run(*args):
        pinned = [pltpu.with_memory_space_constraint(a, pltpu.HBM) if jnp.issubdtype(a.dtype, jnp.floating) else a
                  for a in args]
        return call(*pinned)

    return run


def _dot(a, b):
    return lax.dot_general(a, b, (((1,), (0,)), ((), ())), preferred_element_type=F32)


def _dot_nt(a, b):
    return lax.dot_general(a, b, (((1,), (1,)), ((), ())), preferred_element_type=F32)


def _dot_tn(a, b):
    return lax.dot_general(a, b, (((0,), (0,)), ((), ())), preferred_element_type=F32)


def _params(n_grid, vmem_mb=48):
    return pltpu.CompilerParams(dimension_semantics=("arbitrary",) * n_grid, vmem_limit_bytes=vmem_mb << 20)


def _rows(width, tm=ROW_TILE):
    return pl.BlockSpec((tm, width), lambda i: (i, 0))


def _view_rows(width, d, tm=ROW_TILE):
    return pl.BlockSpec((tm // d, d * width), lambda i: (i, 0))


def _whole(shape):
    return pl.BlockSpec(shape, lambda *_: (0,) * len(shape))


def _resident(shape):
    return pl.BlockSpec(shape, lambda *_: (0,) * len(shape), pipeline_mode=pl.Buffered(1))


def _sds(shape, dtype):
    return pltpu.HBM(shape, dtype)


def _plain(shape, dtype):
    return jax.ShapeDtypeStruct(shape, dtype)


def _silu_parts(z):
    sg = jax.nn.sigmoid(z)
    return z * sg, sg * (1.0 + z * (1.0 - sg))


def _to_view(scr, val, out_ref, d):
    tm, w = val.shape
    if d == 1:
        out_ref[...] = val.astype(out_ref.dtype)
        return
    for cb in range(w // LANES):
        scr[cb] = val[:, cb * LANES:(cb + 1) * LANES]
    for r in range(d):
        for cb in range(w // LANES):
            lo = r * w + cb * LANES
            out_ref[:, lo:lo + LANES] = scr[cb, pl.ds(r, tm // d, stride=d), :].astype(out_ref.dtype)


def _from_view(scr, in_ref, d):
    if d == 1:
        return in_ref[...].astype(F32)
    nc, tm, _ = scr.shape
    w = nc * LANES
    for r in range(d):
        for cb in range(nc):
            lo = r * w + cb * LANES
            scr[cb, pl.ds(r, tm // d, stride=d), :] = in_ref[:, lo:lo + LANES].astype(F32)
    return jnp.concatenate([scr[cb] for cb in range(nc)], axis=1)


def _view_scratch(width, tm=ROW_TILE):
    return pltpu.VMEM((width // LANES, tm, LANES), F32)


def _rope_tables(pos):
    half = ROT_DIM // 2
    inv_freq = ROPE_THETA ** (-jnp.arange(half, dtype=F32) * (2.0 / ROT_DIM))
    ang = pos.astype(F32)[:, None] * inv_freq
    cos, sin = jnp.cos(ang), jnp.sin(ang)
    s = pos.shape[0]
    z8 = jnp.zeros((s, half), F32)
    rest = HEAD_DIM - ROT_DIM
    cosf = jnp.concatenate([cos, cos, jnp.ones((s, rest), F32)], axis=1)
    sa = jnp.concatenate([-sin, z8, jnp.zeros((s, rest), F32)], axis=1)
    sb = jnp.concatenate([z8, sin, jnp.zeros((s, rest), F32)], axis=1)
    return tuple(jnp.tile(t, (1, LANES // HEAD_DIM)) for t in (cosf, sa, sb))


def _rope_fwd(t, cv, sav, sbv):
    w = t.shape[1]
    return t * cv + pltpu.roll(t, w - ROT_DIM // 2, 1) * sav + pltpu.roll(t, ROT_DIM // 2, 1) * sbv


def _rope_bwd(g, cv, sav, sbv):
    w = g.shape[1]
    return g * cv + pltpu.roll(g * sav, ROT_DIM // 2, 1) + pltpu.roll(g * sbv, w - ROT_DIM // 2, 1)


def _joined_columns(wg_ref):
    assert wg_ref.shape[2] % LANES == 0
    return jnp.concatenate([wg_ref[j] for j in range(N_DEV)], axis=1)


def _join_once(wg_ref, w_scr):
    c = wg_ref.shape[2]

    @pl.when(pl.program_id(0) == 0)
    def _():
        for j in range(N_DEV):
            w_scr[:, j * c:(j + 1) * c] = wg_ref[j]


def _inproj_attn(x, g, wg, tabs):
    s, d_model = x.shape
    gw = GROUP_WIDTH
    n = N_DEV * wg.shape[2]
    nz = n - 9 * gw - MEM_WIDTH
    reps = gw // LANES
    tm = ROW_TILE

    def body(x_ref, g_ref, w_ref, c_ref, sa_ref, sb_ref, hn_ref, hnt_ref, *rest):
        outs, (wj, scr, tscr) = rest[:-3], rest[-3:]
        q_refs, k_refs, v_refs, t_refs, qm_ref, z_ref = outs[0:3], outs[3:6], outs[6:9], outs[9:18], outs[18], outs[19]
        _join_once(w_ref, wj)
        xb = x_ref[...]
        r = lax.rsqrt(jnp.mean(xb * xb, axis=-1, keepdims=True) + EPS)
        hn = ((xb * r) * g_ref[...]).astype(BF16)
        hn_ref[...] = hn
        hnt_ref[...] = hn.T
        proj = lambda lo, hi: _dot(hn, wj[:, lo:hi])
        tab = (c_ref[...], sa_ref[...], sb_ref[...])
        cv, sav, sbv = [jnp.tile(t, (1, reps)) for t in tab]
        for j, d in enumerate(DILATIONS):
            tq = _rope_fwd(proj(j * gw, (j + 1) * gw), cv, sav, sbv)
            _to_view(scr, tq * SCALE, q_refs[j], d)
            tk = _rope_fwd(proj((3 + j) * gw, (4 + j) * gw), cv, sav, sbv)
            _to_view(scr, tk, k_refs[j], d)
            _to_view(scr, proj((6 + j) * gw, (7 + j) * gw), v_refs[j], d)
            for i in range(3):
                _to_view(tscr, tab[i], t_refs[3 * j + i], d)
        qm_ref[...] = proj(9 * gw, 9 * gw + MEM_WIDTH).astype(BF16)
        z_ref[...] = proj(9 * gw + MEM_WIDTH, n)

    views = [_sds((s // d, d * gw), BF16) for d in DILATIONS]
    tviews = [_sds((s // d, d * LANES), F32) for d in DILATIONS for _ in range(3)]
    out_shape = ([_sds((s, d_model), BF16), _sds((d_model, s), BF16)] + views * 3 + tviews
                 + [_sds((s, MEM_WIDTH), BF16), _sds((s, nz), F32)])
    vspecs = [_view_rows(gw, d, tm) for d in DILATIONS]
    tspecs = [_view_rows(LANES, d, tm) for d in DILATIONS for _ in range(3)]
    out_specs = ([_rows(d_model, tm), pl.BlockSpec((d_model, tm), lambda i: (0, i))] + vspecs * 3 + tspecs
                 + [_rows(MEM_WIDTH, tm), _rows(nz, tm)])
    res = _pallas_call(
        body, name="inproj_attn", grid=(s // tm,), out_shape=out_shape,
        in_specs=[_rows(d_model, tm), _whole((1, d_model)), _resident(wg.shape)] + [_rows(LANES, tm)] * 3,
        out_specs=out_specs,
        scratch_shapes=[pltpu.VMEM((d_model, n), BF16), _view_scratch(gw, tm), _view_scratch(LANES, tm)],
        compiler_params=_params(1, 60),
    )(x, g, wg, *tabs)
    tabs_v = [res[11 + 3 * j:14 + 3 * j] for j in range(3)]
    return res[0], res[1], res[2:5], res[5:8], res[8:11], tabs_v, res[20], res[21]


def _band_mask(n_keys):
    qi = lax.broadcasted_iota(jnp.int32, (BLOCK, n_keys), 0)
    kj = lax.broadcasted_iota(jnp.int32, (BLOCK, n_keys), 1)
    if n_keys == BLOCK:
        return kj <= qi
    return jnp.logical_or(jnp.logical_and(kj < BLOCK, kj >= qi), jnp.logical_and(kj >= BLOCK, (kj - BLOCK) <= qi))


def _low_head_lanes():
    return lax.broadcasted_iota(jnp.int32, (1, LANES), 1) < HEAD_DIM


def _split_pair(t, low):
    zero = jnp.zeros_like(t)
    return jnp.where(low, t, zero), jnp.where(low, zero, t)


def _pair_specs(d, nb, w):
    if nb == 1:
        return pl.BlockSpec((BLOCK, 2 * w), lambda n: (0, n)), None
    half = nb // 2
    two = pl.BlockSpec((2 * BLOCK, w), lambda n: (n % half, n // half))
    before = pl.BlockSpec((BLOCK, w), lambda n: (jnp.maximum(2 * (n % half) - 1, 0), n // half))
    return two, before


def _head_tiles(w, col0):
    return ([slice(p * LANES, (p + 1) * LANES) for p in range(w // LANES)],
            [slice(col0 + p * LANES, col0 + (p + 1) * LANES) for p in range(w // LANES)])


def _attend_fwd(q_ref, o_ref, lse_ref, rows, col0, kk, vv):
    w = kk.shape[1]
    valid = _band_mask(kk.shape[0])
    low = _low_head_lanes()
    pairs, qcols = _head_tiles(w, col0)
    qs_ = [h for qc in qcols for h in _split_pair(q_ref[rows, qc], low)]
    k2s = [kk[:, pr] for pr in pairs for _ in range(2)]
    scs = [jnp.where(valid, _dot_nt(qh, k2), NEG) for qh, k2 in zip(qs_, k2s)]
    ms = [jnp.max(sc, axis=-1, keepdims=True) for sc in scs]
    ps = [jnp.exp(sc - m) for sc, m in zip(scs, ms)]
    ls = [jnp.sum(p, axis=-1, keepdims=True) for p in ps]
    pns = [(p * (1.0 / l)).astype(BF16) for p, l in zip(ps, ls)]
    for i, (pr, qc) in enumerate(zip(pairs, qcols)):
        v2 = vv[:, pr]
        a, b = 2 * i, 2 * i + 1
        o_ref[rows, qc] = jnp.where(low, _dot(pns[a], v2), _dot(pns[b], v2))
        lse_ref[rows, qc] = jnp.where(low, ms[a] + jnp.log(ls[a]), ms[b] + jnp.log(ls[b]))


TOP, BOTTOM = slice(0, BLOCK), slice(BLOCK, 2 * BLOCK)


def _attn_fwd(q, k, v, d, token):
    ln, dw = q.shape
    w = dw // d
    nb = ln // BLOCK
    two, before = _pair_specs(d, nb, w)

    def body_streams(_, q_ref, kc_ref, vc_ref, o_ref, lse_ref):
        for sb in range(2):
            cols = slice(sb * w, (sb + 1) * w)
            _attend_fwd(q_ref, o_ref, lse_ref, TOP, sb * w, kc_ref[:, cols], vc_ref[:, cols])

    def body_blocks(_, q_ref, kp_ref, kc_ref, vp_ref, vc_ref, o_ref, lse_ref):
        first = pl.program_id(0) % (nb // 2) == 0
        pl.when(first)(lambda: _attend_fwd(q_ref, o_ref, lse_ref, TOP, 0, kc_ref[TOP, :], vc_ref[TOP, :]))
        pl.when(jnp.logical_not(first))(lambda: _attend_fwd(
            q_ref, o_ref, lse_ref, TOP, 0, jnp.concatenate([kp_ref[...], kc_ref[TOP, :]], axis=0),
            jnp.concatenate([vp_ref[...], vc_ref[TOP, :]], axis=0)))
        _attend_fwd(q_ref, o_ref, lse_ref, BOTTOM, 0, kc_ref[...], vc_ref[...])

    if nb == 1:
        body, in_specs, args = body_streams, [ANY, two, two, two], (token, q, k, v)
    else:
        body, in_specs, args = body_blocks, [ANY, two, before, two, before, two], (token, q, k, k, v, v)
    return _pallas_call(
        body, name=f"attn_fwd_d{d}", grid=(d * nb // 2,), out_shape=[_sds((ln, dw), F32)] * 2,
        in_specs=in_specs, out_specs=[two, two], compiler_params=_params(1, 32),
    )(*args)


def _memkv_fwd(mem, g, w):
    n_layers = w.shape[0]

    def body(mem_ref, g_ref, w_ref, kv_ref):
        mb = mem_ref[...]
        r = lax.rsqrt(jnp.mean(mb * mb, axis=-1, keepdims=True) + EPS)
        mn = ((mb * r) * g_ref[...]).astype(BF16)
        kv_ref[...] = _dot(mn, w_ref[...]).astype(BF16)

    return _pallas_call(
        body, name="memkv_fwd", grid=(n_layers,),
        out_shape=_plain((n_layers, N_MEM, 2 * MEM_WIDTH), BF16),
        in_specs=[_whole(mem.shape), pl.BlockSpec((None, 1, D_MODEL), lambda l: (l, 0, 0)),
                  pl.BlockSpec((None, D_MODEL, 2 * MEM_WIDTH), lambda l: (l, 0, 0))],
        out_specs=pl.BlockSpec((None, N_MEM, 2 * MEM_WIDTH), lambda l: (l, 0, 0)),
        compiler_params=_params(1, 32),
    )(mem, g.reshape(n_layers, 1, D_MODEL), w)


def _mix_groups(os_, ls_):
    mx = jnp.maximum(jnp.maximum(ls_[0], ls_[1]), ls_[2])
    es = [jnp.exp(t - mx) for t in ls_]
    inv = 1.0 / (es[0] + es[1] + es[2])
    ws = [e * inv for e in es]
    mix = ws[0] * os_[0] + ws[1] * os_[1] + ws[2] * os_[2]
    return ws, mix


MEM_PAIRS = [slice(p * LANES, (p + 1) * LANES) for p in range(MEM_WIDTH // LANES)]


def _mem_probs(qhs, k2s):
    scs = [_dot_nt(qh, k2) * SCALE for qh, k2 in zip(qhs, k2s)]
    es = [jnp.exp(sc - jnp.max(sc, axis=-1, keepdims=True)) for sc in scs]
    return [e * (1.0 / jnp.sum(e, axis=-1, keepdims=True)) for e in es]


def _mem_attn_into(qm, kv_ref, mo_ref):
    low = _low_head_lanes()
    qhs = [h for pr in MEM_PAIRS for h in _split_pair(qm[:, pr], low)]
    k2s = [kv_ref[:, pr] for pr in MEM_PAIRS for _ in range(2)]
    ps = [p.astype(BF16) for p in _mem_probs(qhs, k2s)]
    for i, pr in enumerate(MEM_PAIRS):
        v2 = kv_ref[:, MEM_WIDTH + i * LANES:MEM_WIDTH + (i + 1) * LANES]
        mo_ref[:, pr] = jnp.where(low, _dot(ps[2 * i], v2), _dot(ps[2 * i + 1], v2))


def _mem_attn_bwd(qm, kv_ref, dmem, dqm_ref, dkv_ref):
    low = _low_head_lanes()
    dmb = dmem.astype(BF16)
    vps = [slice(MEM_WIDTH + i * LANES, MEM_WIDTH + (i + 1) * LANES) for i in range(len(MEM_PAIRS))]
    qhs = [h for pr in MEM_PAIRS for h in _split_pair(qm[:, pr], low)]
    dhs = [h for pr in MEM_PAIRS for h in _split_pair(dmb[:, pr], low)]
    k2s = [kv_ref[:, pr] for pr in MEM_PAIRS for _ in range(2)]
    v2s = [kv_ref[:, vp] for vp in vps for _ in range(2)]
    ps = _mem_probs(qhs, k2s)
    dps = [_dot_nt(dh, v2) for dh, v2 in zip(dhs, v2s)]
    dss = [(p * (dp - jnp.sum(dp * p, axis=-1, keepdims=True)) * SCALE).astype(BF16) for p, dp in zip(ps, dps)]
    pbs = [p.astype(BF16) for p in ps]
    for i, (pr, vp) in enumerate(zip(MEM_PAIRS, vps)):
        a, b = 2 * i, 2 * i + 1
        dqm_ref[:, pr] = jnp.where(low, _dot(dss[a], k2s[a]), _dot(dss[b], k2s[b])).astype(BF16)
        dkv_ref[:, pr] += _dot_tn(dss[a], qhs[a]) + _dot_tn(dss[b], qhs[b])
        dkv_ref[:, vp] += _dot_tn(pbs[a], dhs[a]) + _dot_tn(pbs[b], dhs[b])


def _post_attn(os_, ls_, qm, kv, z, x, wg_out):
    s, d_model = x.shape
    gw = GROUP_WIDTH
    nb = gw + MEM_WIDTH
    tm = ROW_TILE

    def body(o0, o1, o2, l0, l1, l2, qm_ref, kv_ref, z_ref, x_ref, w_ref, y_ref, yt_ref, mo_ref, h_ref, s0, s1):
        ov, lv = [], []
        for o_ref, l_ref, d in zip((o0, o1, o2), (l0, l1, l2), DILATIONS):
            ov.append(_from_view(s0, o_ref, d))
            lv.append(_from_view(s1, l_ref, d))
        _, mix = _mix_groups(ov, lv)
        _mem_attn_into(qm_ref[...], kv_ref, mo_ref)
        sz, _ = _silu_parts(z_ref[...])
        y_ref[:, :gw] = (mix * sz[:, :gw]).astype(BF16)
        y_ref[:, gw:] = (mo_ref[...] * sz[:, gw:]).astype(BF16)
        y = y_ref[...]
        yt_ref[...] = y.T
        h_ref[...] = x_ref[...] + _dot(y, _joined_columns(w_ref))

    vspecs = [_view_rows(gw, d) for d in DILATIONS]
    return _pallas_call(
        body, name="post_attn", grid=(s // tm,),
        out_shape=[_sds((s, nb), BF16), _sds((nb, s), BF16), _sds((s, MEM_WIDTH), F32), _sds((s, d_model), F32)],
        in_specs=vspecs * 2 + [_rows(MEM_WIDTH), _whole(kv.shape), _rows(nb), _rows(d_model), _whole(wg_out.shape)],
        out_specs=[_rows(nb), pl.BlockSpec((nb, tm), lambda i: (0, i)), _rows(MEM_WIDTH), _rows(d_model)],
        scratch_shapes=[_view_scratch(gw), _view_scratch(gw)],
        compiler_params=_params(1, 40),
    )(*os_, *ls_, qm, kv, z, x, wg_out)


def _inproj_conv(x, g, wg):
    s, d_model = x.shape
    c = CONV_WIDTH
    n = N_DEV * wg.shape[2]
    nz = n - 3 * c - MEM_WIDTH
    tm = ROW_TILE

    def body(x_ref, g_ref, w_ref, hn_ref, hnt_ref, bg_ref, cg_ref, u_ref, qm_ref, z_ref, wj):
        _join_once(w_ref, wj)
        xb = x_ref[...]
        r = lax.rsqrt(jnp.mean(xb * xb, axis=-1, keepdims=True) + EPS)
        hn = ((xb * r) * g_ref[...]).astype(BF16)
        hn_ref[...] = hn
        hnt_ref[...] = hn.T
        bg_ref[...] = _dot(hn, wj[:, 0:c])
        cg_ref[...] = _dot(hn, wj[:, c:2 * c])
        u_ref[...] = _dot(hn, wj[:, 2 * c:3 * c])
        qm_ref[...] = _dot(hn, wj[:, 3 * c:3 * c + MEM_WIDTH]).astype(BF16)
        z_ref[...] = _dot(hn, wj[:, 3 * c + MEM_WIDTH:])

    return _pallas_call(
        body, name="inproj_conv", grid=(s // tm,),
        out_shape=[_sds((s, d_model), BF16), _sds((d_model, s), BF16)] + [_sds((s, c), F32)] * 3
                  + [_sds((s, MEM_WIDTH), BF16), _sds((s, nz), F32)],
        in_specs=[_rows(d_model), _whole((1, d_model)), _resident(wg.shape)],
        out_specs=[_rows(d_model), pl.BlockSpec((d_model, tm), lambda i: (0, i))] + [_rows(c)] * 3
                  + [_rows(MEM_WIDTH), _rows(nz)],
        scratch_shapes=[pltpu.VMEM((d_model, n), BF16)],
        compiler_params=_params(1, 60),
    )(x, g, wg)


HALO = 8


def _halo_before(width, tm=ROW_TILE):
    return pl.BlockSpec((HALO, width), lambda i: (jnp.maximum(i * (tm // HALO) - 1, 0), 0))


def _halo_after(width, n_rows, tm=ROW_TILE):
    return pl.BlockSpec((HALO, width), lambda i: (jnp.minimum((i + 1) * (tm // HALO), n_rows // HALO - 1), 0))


def _conv_taps(cg_ref, u_ref, cgh_ref, uh_ref, i):
    a = cg_ref[...] * u_ref[...]
    ah = jnp.where(i > 0, cgh_ref[...] * uh_ref[...], 0.0)
    row = lax.broadcasted_iota(jnp.int32, a.shape, 0)
    a1 = jnp.where(row == 0, ah[HALO - 1:HALO], pltpu.roll(a, 1, 0))
    a2 = jnp.where(row == 0, ah[HALO - 2:HALO - 1], jnp.where(row == 1, ah[HALO - 1:HALO], pltpu.roll(a, 2, 0)))
    return a, a1, a2


def _post_conv_loss(bg, cg, u, qm, kv, z, h1, w_out, cw, gf, tgt):
    s, d = h1.shape
    c = CONV_WIDTH
    nb = c + MEM_WIDTH
    tm = ROW_TILE

    def body(bg_ref, cg_ref, u_ref, cgh_ref, uh_ref, qm_ref, kv_ref, z_ref, h_ref, w_ref, cw_ref, gf_ref, t_ref,
             y_ref, yt_ref, mo_ref, dh_ref, dhb_ref, loss_ref, dgf_ref):
        i = pl.program_id(0)
        a, a1, a2 = _conv_taps(cg_ref, u_ref, cgh_ref, uh_ref, i)
        conv = cw_ref[0:1, :] * a2 + cw_ref[1:2, :] * a1 + cw_ref[2:3, :] * a
        mix = bg_ref[...] * conv
        _mem_attn_into(qm_ref[...], kv_ref, mo_ref)
        sz, _ = _silu_parts(z_ref[...])
        y_ref[:, :c] = (mix * sz[:, :c]).astype(BF16)
        y_ref[:, c:] = (mo_ref[...] * sz[:, c:]).astype(BF16)
        y = y_ref[...]
        yt_ref[...] = y.T
        h2 = h_ref[...] + _dot(y, w_ref[...])
        r = lax.rsqrt(jnp.mean(h2 * h2, axis=-1, keepdims=True) + EPS)
        nh = h2 * r
        gfv = gf_ref[...]
        diff = nh * gfv - t_ref[...]
        dout = diff * (1.0 / d)
        dn = dout * gfv
        dh2 = r * dn - h2 * ((r * r * r) * jnp.mean(dn * h2, axis=-1, keepdims=True))
        dh_ref[...] = dh2
        dhb_ref[...] = dh2.astype(BF16)

        @pl.when(i == 0)
        def _():
            loss_ref[...] = jnp.zeros_like(loss_ref)
            dgf_ref[...] = jnp.zeros_like(dgf_ref)

        loss_ref[...] += 0.5 * jnp.sum(jnp.mean(diff * diff, axis=-1, keepdims=True))
        dgf_ref[...] += jnp.sum(dout * nh, axis=0, keepdims=True)

    return _pallas_call(
        body, name="post_conv_loss", grid=(s // tm,),
        out_shape=[_sds((s, nb), BF16), _sds((nb, s), BF16), _sds((s, MEM_WIDTH), F32), _sds((s, d), F32),
                   _sds((s, d), BF16), _plain((8, LANES), F32), _plain((1, d), F32)],
        in_specs=[_rows(c)] * 3 + [_halo_before(c)] * 2 + [_rows(MEM_WIDTH), _whole(kv.shape), _rows(nb), _rows(d),
                  _whole(w_out.shape), _whole(cw.shape), _whole((1, d)), _rows(d)],
        out_specs=[_rows(nb), pl.BlockSpec((nb, tm), lambda i: (0, i)), _rows(MEM_WIDTH), _rows(d), _rows(d),
                   _whole((8, LANES)), _whole((1, d))],
        compiler_params=_params(1, 48),
    )(bg, cg, u, cg, u, qm, kv, z, h1, w_out, cw, gf, tgt)


def _bwd_post_conv(dhb, w_out, bg, cg, u, z, qm, kv, mo, cw):
    s = dhb.shape[0]
    c = CONV_WIDTH
    nb = c + MEM_WIDTH

    def body(dh_ref, w_ref, bg_ref, cg_ref, u_ref, cgh_ref, uh_ref, z_ref, qm_ref, kv_ref, mo_ref, cw_ref,
             dz_ref, dbg_ref, dc_ref, dqm_ref, dkv_ref):
        i = pl.program_id(0)

        @pl.when(i == 0)
        def _():
            dkv_ref[...] = jnp.zeros_like(dkv_ref)

        dy = _dot_nt(dh_ref[...], w_ref[...])
        sz, dsz = _silu_parts(z_ref[...])
        a, a1, a2 = _conv_taps(cg_ref, u_ref, cgh_ref, uh_ref, i)
        conv = cw_ref[0:1, :] * a2 + cw_ref[1:2, :] * a1 + cw_ref[2:3, :] * a
        bgv = bg_ref[...]
        dz_ref[:, :c] = (dy[:, :c] * (bgv * conv) * dsz[:, :c]).astype(BF16)
        dz_ref[:, c:] = (dy[:, c:] * mo_ref[...] * dsz[:, c:]).astype(BF16)
        dbr = dy * sz
        dmix = dbr[:, :c]
        dbg_ref[...] = (dmix * conv).astype(BF16)
        dc_ref[...] = dmix * bgv
        _mem_attn_bwd(qm_ref[...], kv_ref, dbr[:, c:], dqm_ref, dkv_ref)

    return _pallas_call(
        body, name="bwd_post_conv", grid=(s // ROW_TILE,),
        out_shape=[_sds((s, nb), BF16), _sds((s, c), BF16), _sds((s, c), F32), _sds((s, MEM_WIDTH), BF16),
                   _plain(kv.shape, F32)],
        in_specs=[_rows(D_MODEL), _whole(w_out.shape)] + [_rows(c)] * 3 + [_halo_before(c)] * 2
                 + [_rows(nb), _rows(MEM_WIDTH), _whole(kv.shape), _rows(MEM_WIDTH), _whole(cw.shape)],
        out_specs=[_rows(nb), _rows(c), _rows(c), _rows(MEM_WIDTH), _whole(kv.shape)],
        compiler_params=_params(1, 48),
    )(dhb, w_out, bg, cg, u, cg, u, z, qm, kv, mo, cw)


def _bwd_conv(dconv, cg, u, cw):
    s, c = dconv.shape
    tm = ROW_TILE
    last = s // tm - 1

    def body(dc_ref, dcn_ref, cg_ref, u_ref, cgh_ref, uh_ref, cw_ref, dcg_ref, du_ref, dcw_ref):
        i = pl.program_id(0)

        @pl.when(i == 0)
        def _():
            dcw_ref[...] = jnp.zeros_like(dcw_ref)

        dc = dc_ref[...]
        dcn = jnp.where(i < last, dcn_ref[...], 0.0)
        row = lax.broadcasted_iota(jnp.int32, dc.shape, 0)
        d1 = jnp.where(row == tm - 1, dcn[0:1], pltpu.roll(dc, tm - 1, 0))
        d2 = jnp.where(row == tm - 1, dcn[1:2], jnp.where(row == tm - 2, dcn[0:1], pltpu.roll(dc, tm - 2, 0)))
        da = cw_ref[2:3, :] * dc + cw_ref[1:2, :] * d1 + cw_ref[0:1, :] * d2
        a, a1, a2 = _conv_taps(cg_ref, u_ref, cgh_ref, uh_ref, i)
        dcg_ref[...] = (da * u_ref[...]).astype(BF16)
        du_ref[...] = (da * cg_ref[...]).astype(BF16)
        dcw_ref[0:1, :] += jnp.sum(dc * a2, axis=0, keepdims=True)
        dcw_ref[1:2, :] += jnp.sum(dc * a1, axis=0, keepdims=True)
        dcw_ref[2:3, :] += jnp.sum(dc * a, axis=0, keepdims=True)

    return _pallas_call(
        body, name="bwd_conv", grid=(s // tm,),
        out_shape=[_sds((s, c), BF16), _sds((s, c), BF16), _plain((8, c), F32)],
        in_specs=[_rows(c), _halo_after(c, s), _rows(c), _rows(c), _halo_before(c), _halo_before(c), _whole(cw.shape)],
        out_specs=[_rows(c), _rows(c), _whole((8, c))], compiler_params=_params(1, 40),
    )(dconv, dconv, cg, u, cg, u, cw)


def _dgrad_norm(pieces, wg, h, g, dres, name):
    s, d_model = h.shape
    c = wg.shape[2]
    n = N_DEV * c
    tm = ROW_TILE
    widths = [p.shape[1] // d for p, d in pieces]
    assert sum(widths) == n
    n_p = len(pieces)

    def body(*refs):
        p_refs = refs[:n_p]
        w_ref, h_ref, g_ref, dr_ref, dh_ref, dhb_ref, dg_ref, dpd_ref, dp, scr, wj = refs[n_p:]
        _join_once(w_ref, wj)

        @pl.when(pl.program_id(0) == 0)
        def _():
            dg_ref[...] = jnp.zeros_like(dg_ref)

        off = 0
        for p_ref, (_, d), wd in zip(p_refs, pieces, widths):
            if d == 1:
                dp[:, off:off + wd] = p_ref[...]
            else:
                dp[:, off:off + wd] = _from_view(scr, p_ref, d).astype(BF16)
            off += wd
        for j in range(N_DEV):
            dpd_ref[j] = dp[:, j * c:(j + 1) * c]
        dhn = _dot_nt(dp[...], wj[...])
        hb = h_ref[...]
        r = lax.rsqrt(jnp.mean(hb * hb, axis=-1, keepdims=True) + EPS)
        dg_ref[...] += jnp.sum(dhn * (hb * r), axis=0, keepdims=True)
        dn = dhn * g_ref[...]
        dh = dr_ref[...] + r * dn - hb * ((r * r * r) * jnp.mean(dn * hb, axis=-1, keepdims=True))
        dh_ref[...] = dh
        dhb_ref[...] = dh.astype(BF16)

    p_specs = [_view_rows(wd, d) for (_, d), wd in zip(pieces, widths)]
    return _pallas_call(
        body, name=name, grid=(s // tm,),
        out_shape=[_plain((s, d_model), F32), _sds((s, d_model), BF16), _plain((1, d_model), F32), _sds((N_DEV, s, c), BF16)],
        in_specs=p_specs + [_resident(wg.shape), _rows(d_model), _whole((1, d_model)), _rows(d_model)],
        out_specs=[_rows(d_model), _rows(d_model), _whole((1, d_model)), pl.BlockSpec((N_DEV, tm, c), lambda i: (0, i, 0))],
        scratch_shapes=[pltpu.VMEM((tm, n), BF16), _view_scratch(GROUP_WIDTH), pltpu.VMEM((d_model, n), BF16)],
        compiler_params=_params(1, 60),
    )(*[p for p, _ in pieces], wg, h, g, dres)


def _assemble_dproj(pieces, c, name):
    n = N_DEV * c
    tm = ROW_TILE
    widths = [p.shape[1] // d for p, d in pieces]
    assert sum(widths) == n
    s = pieces[0][0].shape[0] * pieces[0][1]
    n_p = len(pieces)

    def body(*refs):
        p_refs, (dpd_ref, dp, scr) = refs[:n_p], refs[n_p:]
        off = 0
        for p_ref, (_, d), wd in zip(p_refs, pieces, widths):
            if d == 1:
                dp[:, off:off + wd] = p_ref[...]
            else:
                dp[:, off:off + wd] = _from_view(scr, p_ref, d).astype(BF16)
            off += wd
        for j in range(N_DEV):
            dpd_ref[j] = dp[:, j * c:(j + 1) * c]

    return _pallas_call(
        body, name=name, grid=(s // tm,), out_shape=_sds((N_DEV, s, c), BF16),
        in_specs=[_view_rows(wd, d) for (_, d), wd in zip(pieces, widths)],
        out_specs=pl.BlockSpec((N_DEV, tm, c), lambda i: (0, i, 0)),
        scratch_shapes=[pltpu.VMEM((tm, n), BF16), _view_scratch(GROUP_WIDTH)],
        compiler_params=_params(1, 40),
    )(*[p for p, _ in pieces])


def _dgrad_norm_dm(dproj_dm, wg, h, g, dres, token, name):
    s, d_model = h.shape
    c = wg.shape[2]
    tm = ROW_TILE

    def body(_, dp_ref, w_ref, h_ref, g_ref, dr_ref, dh_ref, dg_ref):
        @pl.when(pl.program_id(0) == 0)
        def _():
            dg_ref[...] = jnp.zeros_like(dg_ref)

        dhn = jnp.zeros((tm, d_model), F32)
        for j in range(N_DEV):
            dhn += _dot_nt(dp_ref[j], w_ref[j])
        hb = h_ref[...]
        r = lax.rsqrt(jnp.mean(hb * hb, axis=-1, keepdims=True) + EPS)
        dg_ref[...] += jnp.sum(dhn * (hb * r), axis=0, keepdims=True)
        dn = dhn * g_ref[...]
        dh_ref[...] = dr_ref[...] + r * dn - hb * ((r * r * r) * jnp.mean(dn * hb, axis=-1, keepdims=True))

    return _pallas_call(
        body, name=name, grid=(s // tm,),
        out_shape=[_plain((s, d_model), F32), _plain((1, d_model), F32)],
        in_specs=[ANY, pl.BlockSpec((N_DEV, tm, c), lambda i: (0, i, 0)), _whole(wg.shape), _rows(d_model),
                  _whole((1, d_model)), _rows(d_model)],
        out_specs=[_rows(d_model), _whole((1, d_model))],
        compiler_params=_params(1, 60),
    )(token, dproj_dm, wg, h, g, dres)


def _wgrad_shards(a_t, b_dm, name):
    m, s = a_t.shape
    c = b_dm.shape[2]

    def body(a_ref, b_ref, o_ref):
        o_ref[...] = _dot(a_ref[...], b_ref[...]).astype(BF16)

    return _pallas_call(
        body, name=name, grid=(N_DEV,), out_shape=_sds((N_DEV, m, c), BF16),
        in_specs=[_whole(a_t.shape), pl.BlockSpec((None, s, c), lambda j: (j, 0, 0))],
        out_specs=pl.BlockSpec((None, m, c), lambda j: (j, 0, 0)), compiler_params=_params(1, 40),
    )(a_t, b_dm)


def _wgrad_cols(a_t, b, c, name):
    m, s = a_t.shape
    assert c % LANES == 0

    def body(a_ref, b_ref, o_ref):
        wide = _dot(a_ref[...], b_ref[...]).astype(BF16)
        for j in range(WGRAD_SHARDS):
            o_ref[j] = wide[:, j * c:(j + 1) * c]

    return _pallas_call(
        body, name=name, grid=(N_DEV // WGRAD_SHARDS,), out_shape=_sds((N_DEV, m, c), BF16),
        in_specs=[_whole(a_t.shape), pl.BlockSpec((s, WGRAD_SHARDS * c), lambda j: (0, j))],
        out_specs=pl.BlockSpec((WGRAD_SHARDS, m, c), lambda j: (j, 0, 0)), compiler_params=_params(1, 40),
    )(a_t, b)


def _wgrad_rows(a_t, b, name):
    m, s = a_t.shape
    n = b.shape[1]
    mr = m // N_DEV

    def body(a_ref, b_ref, o_ref):
        o_ref[...] = _dot(a_ref[...], b_ref[...]).astype(BF16).reshape(WGRAD_SHARDS, mr, n)

    return _pallas_call(
        body, name=name, grid=(N_DEV // WGRAD_SHARDS,), out_shape=_sds((N_DEV, mr, n), BF16),
        in_specs=[pl.BlockSpec((WGRAD_SHARDS * mr, s), lambda j: (j, 0)), _whole(b.shape)],
        out_specs=pl.BlockSpec((WGRAD_SHARDS, mr, n), lambda j: (j, 0, 0)), compiler_params=_params(1, 40),
    )(a_t, b)


def _memkv_bwd(dkv, w, mem, g):
    n_layers = w.shape[0]
    rows = D_MODEL // N_DEV

    def body(dkv_ref, w_ref, mem_ref, g_ref, dw_ref, dg_ref):
        mb = mem_ref[...]
        r = lax.rsqrt(jnp.mean(mb * mb, axis=-1, keepdims=True) + EPS)
        nm = mb * r
        mn = (nm * g_ref[...]).astype(BF16)
        dkvb = dkv_ref[...].astype(BF16)
        dw_ref[...] = _dot_tn(mn, dkvb).astype(BF16).reshape(N_DEV, rows, 2 * MEM_WIDTH)
        dmn = _dot_nt(dkvb, w_ref[...])
        dg_ref[...] = jnp.sum(dmn * nm, axis=0, keepdims=True)

    lay = lambda *shape: pl.BlockSpec((None,) + shape, lambda l: (l, 0, 0))
    return _pallas_call(
        body, name="memkv_bwd", grid=(n_layers,),
        out_shape=[_sds((N_DEV, n_layers * rows, 2 * MEM_WIDTH), BF16), _plain((n_layers, 1, D_MODEL), F32)],
        in_specs=[lay(N_MEM, 2 * MEM_WIDTH), lay(D_MODEL, 2 * MEM_WIDTH), _whole(mem.shape), lay(1, D_MODEL)],
        out_specs=[pl.BlockSpec((N_DEV, rows, 2 * MEM_WIDTH), lambda l: (0, l, 0)), lay(1, D_MODEL)],
        compiler_params=_params(1, 32),
    )(dkv, w, mem, g.reshape(n_layers, 1, D_MODEL))


def _bwd_post_attn(dhb, wg_out, z, os_, ls_, qm, kv, mo, head_ones):
    s = dhb.shape[0]
    gw = GROUP_WIDTH
    nb = gw + MEM_WIDTH
    tm = ROW_TILE

    def body(dh_ref, w_ref, z_ref, o0, o1, o2, l0, l1, l2, qm_ref, kv_ref, mo_ref, bd_ref,
             dz_ref, do0, do1, do2, dl0, dl1, dl2, dqm_ref, dkv_ref, s0, s1):
        @pl.when(pl.program_id(0) == 0)
        def _():
            dkv_ref[...] = jnp.zeros_like(dkv_ref)

        dy = _dot_nt(dh_ref[...], _joined_columns(w_ref))
        ov, lv = [], []
        for o_ref, l_ref, d in zip((o0, o1, o2), (l0, l1, l2), DILATIONS):
            ov.append(_from_view(s0, o_ref, d))
            lv.append(_from_view(s1, l_ref, d))
        ws, mix = _mix_groups(ov, lv)
        sz, dsz = _silu_parts(z_ref[...])
        dz_ref[:, :gw] = (dy[:, :gw] * mix * dsz[:, :gw]).astype(BF16)
        dz_ref[:, gw:] = (dy[:, gw:] * mo_ref[...] * dsz[:, gw:]).astype(BF16)
        dbr = dy * sz
        dmix = dbr[:, :gw]
        t = dmix * mix
        th = t.astype(BF16)
        tl = (t - th.astype(F32)).astype(BF16)
        rs = _dot(th, bd_ref[...]) + _dot(tl, bd_ref[...])
        for wg_, do_ref, dl_ref, d in zip(ws, (do0, do1, do2), (dl0, dl1, dl2), DILATIONS):
            _to_view(s0, wg_ * dmix, do_ref, d)
            _to_view(s1, wg_ * rs, dl_ref, d)
        _mem_attn_bwd(qm_ref[...], kv_ref, dbr[:, gw:], dqm_ref, dkv_ref)

    vspecs = [_view_rows(gw, d) for d in DILATIONS]
    return _pallas_call(
        body, name="bwd_post_attn", grid=(s // tm,),
        out_shape=[_sds((s, nb), BF16)] + [_sds((s // d, d * gw), BF16) for d in DILATIONS]
                  + [_sds((s // d, d * gw), F32) for d in DILATIONS] + [_sds((s, MEM_WIDTH), BF16), _plain(kv.shape, F32)],
        in_specs=[_rows(D_MODEL), _whole(wg_out.shape), _rows(nb)] + vspecs * 2
                 + [_rows(MEM_WIDTH), _whole(kv.shape), _rows(MEM_WIDTH), _whole(head_ones.shape)],
        out_specs=[_rows(nb)] + vspecs * 2 + [_rows(MEM_WIDTH), _whole(kv.shape)],
        scratch_shapes=[_view_scratch(gw), _view_scratch(gw)],
        compiler_params=_params(1, 48),
    )(dhb, wg_out, z, *os_, *ls_, qm, kv, mo, head_ones)


def _attn_bwd(q, k, v, lse, do, dl, tabs, d, token):
    ln, dw = q.shape
    w = dw // d
    nb = ln // BLOCK
    reps = w // LANES
    two, before = _pair_specs(d, nb, w)
    two_t, _ = _pair_specs(d, nb, LANES)

    def attend(q_ref, l_ref, do_ref, dl_ref, dqs, acck, accv, rows, col0, kk, vv, acc_rows):
        valid = _band_mask(kk.shape[0])
        low = _low_head_lanes()
        pairs, qcols = _head_tiles(w, col0)
        cols = [slice(col0 + h * HEAD_DIM, col0 + h * HEAD_DIM + 1) for h in range(HEADS_PER_GROUP)]
        qhs = [h for qc in qcols for h in _split_pair(q_ref[rows, qc], low)]
        dobs = [h for qc in qcols for h in _split_pair(do_ref[rows, qc], low)]
        k2s = [kk[:, pr] for pr in pairs for _ in range(2)]
        v2s = [vv[:, pr] for pr in pairs for _ in range(2)]
        scs = [jnp.where(valid, _dot_nt(qh, k2), NEG) for qh, k2 in zip(qhs, k2s)]
        dps = [_dot_nt(dob, v2) for dob, v2 in zip(dobs, v2s)]
        ps = [jnp.exp(sc - l_ref[rows, col]) for sc, col in zip(scs, cols)]
        dss = [(p * (dp - dl_ref[rows, col])).astype(BF16) for p, dp, col in zip(ps, dps, cols)]
        pbs = [p.astype(BF16) for p in ps]
        for i, qc in enumerate(qcols):
            a, b = 2 * i, 2 * i + 1
            dqs[rows, qc] = jnp.where(low, _dot(dss[a], k2s[a]), _dot(dss[b], k2s[b])) * SCALE
            acck[acc_rows, qc] += _dot_tn(dss[a], qhs[a]) + _dot_tn(dss[b], qhs[b])
            accv[acc_rows, qc] += _dot_tn(pbs[a], dobs[a]) + _dot_tn(pbs[b], dobs[b])

    def body_streams(_, q_ref, kc_ref, vc_ref, l_ref, do_ref, dl_ref, c_ref, sa_ref, sb_ref,
                     dq_ref, dk_ref, dv_ref, acck, accv, dqs):
        acck[...] = jnp.zeros_like(acck)
        accv[...] = jnp.zeros_like(accv)
        for sb in range(2):
            cols = slice(sb * w, (sb + 1) * w)
            attend(q_ref, l_ref, do_ref, dl_ref, dqs, acck, accv, TOP, sb * w, kc_ref[:, cols], vc_ref[:, cols], TOP)
        tabs2 = [jnp.concatenate([jnp.tile(r[:, sb * LANES:(sb + 1) * LANES], (1, reps)) for sb in range(2)], axis=1)
                 for r in (c_ref, sa_ref, sb_ref)]
        dq_ref[...] = _rope_bwd(dqs[...], *tabs2).astype(BF16)
        dk_ref[...] = _rope_bwd(acck[...], *tabs2).astype(BF16)
        dv_ref[...] = accv[...].astype(BF16)

    def body_blocks(_, q_ref, kp_ref, kc_ref, vp_ref, vc_ref, l_ref, do_ref, dl_ref, cq, saq, sbq, ck, sak, sbk,
                    dq_ref, dk_ref, dv_ref, acck, accv, dqs):
        i = pl.program_id(0) % (nb // 2)

        @pl.when(i == 0)
        def _():
            acck[...] = jnp.zeros_like(acck)
            accv[...] = jnp.zeros_like(accv)

        refs = (q_ref, l_ref, do_ref, dl_ref, dqs, acck, accv)
        pl.when(i == 0)(lambda: attend(*refs, TOP, 0, kc_ref[TOP, :], vc_ref[TOP, :], TOP))
        pl.when(i != 0)(lambda: attend(
            *refs, TOP, 0, jnp.concatenate([kp_ref[...], kc_ref[TOP, :]], axis=0),
            jnp.concatenate([vp_ref[...], vc_ref[TOP, :]], axis=0),
            pl.ds(pl.multiple_of((2 * i - 1) * BLOCK, BLOCK), 2 * BLOCK)))
        attend(*refs, BOTTOM, 0, kc_ref[...], vc_ref[...], pl.ds(pl.multiple_of(2 * i * BLOCK, BLOCK), 2 * BLOCK))
        tq = [jnp.tile(r[...], (1, reps)) for r in (cq, saq, sbq)]
        dq_ref[...] = _rope_bwd(dqs[...], *tq).astype(BF16)

        @pl.when(i == nb // 2 - 1)
        def _():
            for r0 in range(0, nb * BLOCK, 2 * BLOCK):
                rows = slice(r0, r0 + 2 * BLOCK)
                tk = [jnp.tile(r[rows, :], (1, reps)) for r in (ck, sak, sbk)]
                dk_ref[rows, :] = _rope_bwd(acck[rows, :], *tk).astype(BF16)
                dv_ref[rows, :] = accv[rows, :].astype(BF16)

    if nb == 1:
        body = body_streams
        in_specs = [ANY] + [two] * 6 + [two_t] * 3
        args = (token, q, k, v, lse, do, dl, *tabs)
        out_specs = [two, two, two]
        acc_shape = (BLOCK, 2 * w)
    else:
        body = body_blocks
        stream = pl.BlockSpec((nb * BLOCK, w), lambda n: (0, n // (nb // 2)))
        stream_t = pl.BlockSpec((nb * BLOCK, LANES), lambda n: (0, n // (nb // 2)))
        in_specs = [ANY, two, before, two, before, two, two, two, two] + [two_t] * 3 + [stream_t] * 3
        args = (token, q, k, k, v, v, lse, do, dl, *tabs, *tabs)
        out_specs = [two, stream, stream]
        acc_shape = (nb * BLOCK, w)
    return _pallas_call(
        body, name=f"attn_bwd_d{d}", grid=(d * nb // 2,), out_shape=[_sds((ln, dw), BF16)] * 3,
        in_specs=in_specs, out_specs=out_specs,
        scratch_shapes=[pltpu.VMEM(acc_shape, F32), pltpu.VMEM(acc_shape, F32), pltpu.VMEM(two.block_shape, F32)],
        compiler_params=_params(1, 48),
    )(*args)


def _position():
    return lax.axis_index("x"), lax.axis_index("y"), lax.axis_index("c")


def _all_gather(shards, after, name):
    n_a = len(shards)

    def body(*refs):
        x_refs, out_refs = refs[:n_a], refs[n_a + 1:2 * n_a + 1]
        send_sems, recv_sems, local_sems = refs[2 * n_a + 1:]
        x, y, c = _position()
        me, sibling = (x, y, c), (x, y, 1 - c)
        chips = [(1 - x, y), (x, 1 - y), (1 - x, 1 - y)]

        def rows(a, px, py, pc):
            return out_refs[a].at[4 * px + 2 * py + pc]

        def copy(a, k, block, to, own=False):
            return pltpu.make_async_remote_copy(
                src_ref=x_refs[a] if own else rows(a, *block), dst_ref=rows(a, *block),
                send_sem=send_sems.at[a, k], recv_sem=recv_sems.at[a, k], device_id=to, device_id_type=MESH)

        mine = [pltpu.make_async_copy(x_refs[a], rows(a, *me), local_sems.at[a]) for a in range(n_a)]
        for cp in mine:
            cp.start()
        first = []
        for j, chip in enumerate(chips):
            first += [copy(a, 1 + j, me, (*chip, c), own=True) for a in range(n_a)]
        first += [copy(a, 0, me, sibling, own=True) for a in range(n_a)]
        for cp in first:
            cp.start()
        passed = []
        for j, chip in enumerate(chips):
            for a in range(n_a):
                copy(a, 1 + j, (*chip, c), me).wait_recv()
                fwd = copy(a, 4 + j, (*chip, c), sibling)
                fwd.start()
                passed.append(fwd)
        for a in range(n_a):
            copy(a, 0, sibling, me).wait_recv()
        for j, chip in enumerate(chips):
            for a in range(n_a):
                copy(a, 4 + j, (*chip, 1 - c), me).wait_recv()
        for cp in first + passed:
            cp.wait_send()
        for cp in mine:
            cp.wait()

    return _pallas_call(
        body, name=name, out_shape=[_sds((N_DEV,) + t.shape, t.dtype) for t in shards],
        in_specs=[ANY] * (n_a + 1), out_specs=[ANY] * n_a,
        scratch_shapes=[pltpu.SemaphoreType.DMA((n_a, 7)), pltpu.SemaphoreType.DMA((n_a, 7)),
                        pltpu.SemaphoreType.DMA((n_a,))],
    )(*shards, after)


def _all_gather_relay(xs, name):
    def body(x_ref, out_ref, send_sems, recv_sems, local_sem):
        x, y, c = _position()
        me, sibling = (x, y, c), (x, y, 1 - c)
        xn, yn, diag = (1 - x, y, c), (x, 1 - y, c), (1 - x, 1 - y, c)
        src_nb = (x + c * (1 - 2 * x), y + (1 - c) * (1 - 2 * y), c)
        dst_nb = (x + (1 - c) * (1 - 2 * x), y + c * (1 - 2 * y), c)

        def rows(dev):
            return out_ref.at[4 * dev[0] + 2 * dev[1] + dev[2]]

        def copy(k, block, to, own=False):
            return pltpu.make_async_remote_copy(
                src_ref=x_ref if own else rows(block), dst_ref=rows(block),
                send_sem=send_sems.at[k], recv_sem=recv_sems.at[k], device_id=to, device_id_type=MESH)

        mine = pltpu.make_async_copy(x_ref, rows(me), local_sem)
        mine.start()
        first = [copy(1, me, xn, own=True), copy(2, me, yn, own=True), copy(0, me, sibling, own=True)]
        for cp in first:
            cp.start()
        copy(1, xn, me).wait_recv()
        copy(2, yn, me).wait_recv()
        relay = copy(3, src_nb, dst_nb)
        relay.start()
        passed = [copy(4, xn, sibling), copy(5, yn, sibling)]
        for cp in passed:
            cp.start()
        copy(3, diag, me).wait_recv()
        last = copy(6, diag, sibling)
        last.start()
        copy(0, sibling, me).wait_recv()
        for k, blk in ((4, (1 - x, y, 1 - c)), (5, (x, 1 - y, 1 - c)), (6, (1 - x, 1 - y, 1 - c))):
            copy(k, blk, me).wait_recv()
        for cp in first + [relay] + passed + [last]:
            cp.wait_send()
        mine.wait()

    return _pallas_call(
        body, name=name, out_shape=_sds((N_DEV,) + xs.shape, xs.dtype),
        in_specs=[ANY], out_specs=ANY,
        scratch_shapes=[pltpu.SemaphoreType.DMA((7,)), pltpu.SemaphoreType.DMA((7,)), pltpu.SemaphoreType.DMA],
    )(xs)


HBM_SPEC = pl.BlockSpec(memory_space=pltpu.HBM)
SEM_SPEC = pl.BlockSpec(memory_space=pltpu.SEMAPHORE)
EFFECT = pltpu.SideEffectType.DATAFLOW_SIDE_EFFECTING
def _plan_gather_own(src_refs, land_refs):
    x, y, c = _position()
    me = 4 * x + 2 * y + c
    peers = [(x, y, 1 - c), (1 - x, y, c), (x, 1 - y, c), (1 - x, 1 - y, c)]
    return [(src_refs[a], land_refs[a].at[me], (a, k), peer) for k, peer in enumerate(peers) for a in range(len(src_refs))]


def _plan_gather_pass(src_refs, land_refs):
    x, y, c = _position()
    chips = [(1 - x, y), (x, 1 - y), (1 - x, 1 - y)]
    return [(land_refs[a].at[4 * px + 2 * py + c], land_refs[a].at[4 * px + 2 * py + c], (a, j), (x, y, 1 - c))
            for j, (px, py) in enumerate(chips) for a in range(len(land_refs))]


def _plan_to_sibling(src_refs, land_refs):
    x, y, c = _position()
    return [(src_refs[a].at[2 * k + (1 - c)], land_refs[a].at[k], (a, k), (x, y, 1 - c))
            for k in range(4) for a in range(len(src_refs))]


def _plan_to_chips(src_refs, land_refs):
    x, y, c = _position()
    chips = [(1 - x, y), (x, 1 - y), (1 - x, 1 - y)]
    return [(src_refs[a].at[2 * px + py], land_refs[a].at[j], (a, j), (px, py, c))
            for j, (px, py) in enumerate(chips) for a in range(len(src_refs))]


def _split_start(srcs, lands, plan, n_sem, after, name):
    n_s, n_a = len(srcs), len(lands)
    n_b = n_s + n_a

    def body(*refs):
        src_refs, land_refs = refs[:n_s], refs[n_s:n_b]
        send_sems, recv_sems, token = refs[n_b + 1], refs[n_b + 2], refs[-1]
        for src, dst, (a, k), dev in plan(src_refs, land_refs):
            i = a * n_sem + k
            pltpu.make_async_remote_copy(src_ref=src, dst_ref=dst, send_sem=send_sems.at[i], recv_sem=recv_sems.at[i],
                                         device_id=dev, device_id_type=MESH).start()
        token[...] = jnp.zeros_like(token)

    bufs = list(srcs) + list(lands)
    res = pl.pallas_call(
        body, name=name,
        out_shape=(pltpu.SemaphoreType.DMA((n_a * n_sem,)), pltpu.SemaphoreType.DMA((n_a * n_sem,)),
                   *[pltpu.HBM(t.shape, t.dtype) for t in bufs], _plain((8, LANES), F32)),
        in_specs=[HBM_SPEC] * n_b + [ANY],
        out_specs=(SEM_SPEC, SEM_SPEC, *[HBM_SPEC] * n_b, pl.BlockSpec(memory_space=pltpu.VMEM)),
        input_output_aliases={i: 2 + i for i in range(n_b)},
        compiler_params=pltpu.CompilerParams(has_side_effects=EFFECT),
    )(*[pltpu.with_memory_space_constraint(t, pltpu.HBM) for t in bufs], after)
    return (res[0], res[1], res[2:2 + n_s], res[2 + n_s:2 + n_b]), res[-1]


def _split_wait(started, plan, after, name):
    send_sems, recv_sems, srcs, lands = started
    n_s, n_a = len(srcs), len(lands)
    n_b = n_s + n_a
    n_sem = send_sems.shape[0] // n_a

    def body(*refs):
        src_refs, land_refs = refs[:n_s], refs[n_s:n_b]
        s_sems, r_sems = refs[n_b], refs[n_b + 1]
        for src, dst, (a, k), dev in plan(src_refs, land_refs):
            i = a * n_sem + k
            cp = pltpu.make_async_remote_copy(src_ref=src, dst_ref=dst, send_sem=s_sems.at[i], recv_sem=r_sems.at[i],
                                              device_id=dev, device_id_type=MESH)
            cp.wait_send()
            cp.wait_recv()

    bufs = list(srcs) + list(lands)
    res = pl.pallas_call(
        body, name=name, out_shape=tuple(pltpu.HBM(t.shape, t.dtype) for t in bufs),
        in_specs=[HBM_SPEC] * n_b + [SEM_SPEC, SEM_SPEC, ANY],
        out_specs=tuple([HBM_SPEC] * n_b),
        input_output_aliases={i: i for i in range(n_b)},
        compiler_params=pltpu.CompilerParams(has_side_effects=EFFECT),
    )(*bufs, send_sems, recv_sems, after)
    return res[:n_s], res[n_s:]


def _row_tile(r):
    return ROW_TILE if r % ROW_TILE == 0 else r


def _rs_add_sibling(gp, recv, ck_arr, name):
    _, r, l = gp.shape
    tr = r if r <= 4 * ROW_TILE else _row_tile(r)
    block = lambda k, ck: (k + ck[1] + 1) % 4

    def body(ck_ref, g_ref, r_ref, pf_ref, pb_ref):
        sm = g_ref[...].astype(F32) + r_ref[...].astype(F32)
        pf_ref[...] = sm
        pb_ref[...] = sm.astype(BF16)

    spec = pl.BlockSpec((None, tr, l), lambda i, k, ck: (block(k, ck), i, 0))
    return _pallas_call(
        body, name=name,
        grid_spec=pltpu.PrefetchScalarGridSpec(
            num_scalar_prefetch=1, grid=(r // tr, 4),
            in_specs=[pl.BlockSpec((None, tr, l), lambda i, k, ck: (2 * block(k, ck) + ck[0], i, 0)), spec],
            out_specs=[pl.BlockSpec((tr, l), lambda i, k, ck: (i, 0)), spec]),
        out_shape=[_sds((r, l), F32), _sds((4, r, l), BF16)], compiler_params=_params(2, 32),
    )(ck_arr, gp, recv)


def _adam_update(w, gv, m, v):
    nm = ADAM_B1 * m + (1.0 - ADAM_B1) * gv
    nv = ADAM_B2 * v + (1.0 - ADAM_B2) * (gv * gv)
    m_hat = nm / (1.0 - ADAM_B1 ** ADAM_STEP)
    v_hat = nv / (1.0 - ADAM_B2 ** ADAM_STEP)
    return -ADAM_LR * (m_hat / (jnp.sqrt(v_hat) + ADAM_EPS) + ADAM_WD * w), nm, nv


def _rs_finish_adamw(pf, recv, w, m, v, name):
    r, l = pf.shape
    tr = _row_tile(r)

    def body(p_ref, r_ref, w_ref, m_ref, v_ref, g_ref, d_ref, nm_ref, nv_ref):
        gv = ((p_ref[...] + r_ref[0].astype(F32)) + r_ref[1].astype(F32)) + r_ref[2].astype(F32)
        g_ref[...] = gv
        d_ref[...], nm_ref[...], nv_ref[...] = _adam_update(w_ref[...], gv, m_ref[...], v_ref[...])

    spec = pl.BlockSpec((tr, l), lambda i: (i, 0))
    return _pallas_call(
        body, name=name, grid=(r // tr,),
        in_specs=[spec, pl.BlockSpec((3, tr, l), lambda i: (0, i, 0)), spec, spec, spec], out_specs=[spec] * 4,
        out_shape=[_plain((r, l), F32)] * 4, compiler_params=_params(1, 32),
    )(pf, recv, w, m, v)


def _rs_finish_adamw_t(pf, recv, w_t, m_t, v_t, name):
    r, c = pf.shape
    tr = _row_tile(r)
    cp = -(-c // LANES) * LANES

    def body(p_ref, r_ref, w_ref, m_ref, v_ref, g_ref, d_ref, nm_ref, nv_ref, pad):
        gv = ((p_ref[...] + r_ref[0].astype(F32)) + r_ref[1].astype(F32)) + r_ref[2].astype(F32)
        pad[...] = jnp.zeros_like(pad)
        pad[:, 0:c] = gv
        gt = pad[...].T[0:c, :]
        g_ref[...] = gt
        d_ref[...], nm_ref[...], nv_ref[...] = _adam_update(w_ref[...], gt, m_ref[...], v_ref[...])

    spec = pl.BlockSpec((c, tr), lambda i: (0, i))
    return _pallas_call(
        body, name=name, grid=(r // tr,),
        in_specs=[pl.BlockSpec((tr, c), lambda i: (i, 0)), pl.BlockSpec((3, tr, c), lambda i: (0, i, 0)),
                  spec, spec, spec],
        out_specs=[spec] * 4, scratch_shapes=[pltpu.VMEM((tr, cp), F32)],
        out_shape=[_plain((c, r), F32)] * 4, compiler_params=_params(1, 32),
    )(pf, recv, w_t, m_t, v_t)


def _sum_devices(g):
    def body(g_ref, o_ref):
        acc = g_ref[0]
        for j in range(1, N_DEV):
            acc = acc + g_ref[j]
        o_ref[...] = acc

    return _pallas_call(body, name="sum_devices", out_shape=_plain(g.shape[1:], F32))(g)


def _adamw(w, g, m, v, name):
    shape = w.shape
    w2, g2, m2, v2 = [t.reshape((-1, shape[-1])) for t in (w, g, m, v)]

    def body(w_ref, g_ref, m_ref, v_ref, d_ref, nm_ref, nv_ref):
        d_ref[...], nm_ref[...], nv_ref[...] = _adam_update(w_ref[...], g_ref[...], m_ref[...], v_ref[...])

    outs = _pallas_call(body, name=name, out_shape=[_plain(w2.shape, F32)] * 3)(w2, g2, m2, v2)
    return tuple(t.reshape(shape) for t in outs)


def _after(t, token):
    return t + token[0:1, 0:1].astype(t.dtype)


def _finish(name, pf, recv, w, m, v):
    if name in ("attn_w_in", "conv_w_in"):
        res = _rs_finish_adamw_t(pf, recv, w.T, m.T, v.T, "rs_finish_adamw_" + name)
        return tuple(t.T for t in res)
    return _rs_finish_adamw(pf, recv, w, m, v, "rs_finish_adamw_" + name)


def kernel(x, mem, positions, norm_g, mem_norm_g, w_mem_kv, attn_w_in, attn_w_out, conv_w_in, conv_w, conv_w_out, final_g, loss_target, m_norm_g, m_mem_norm_g, m_w_mem_kv, m_attn_w_in, m_attn_w_out, m_conv_w_in, m_conv_w, m_conv_w_out, m_final_g, v_norm_g, v_mem_norm_g, v_w_mem_kv, v_attn_w_in, v_attn_w_out, v_conv_w_in, v_conv_w, v_conv_w_out, v_final_g):
    px, py, pc = _position()
    me = 4 * px + 2 * py + pc
    ck_arr = jnp.stack([pc, 2 * px + py]).astype(jnp.int32)
    x, mem, pos, tgt = x[0], mem[0], positions[0], loss_target[0]

    wg_in0 = _all_gather_relay(attn_w_in[0].astype(BF16), "gather_w_in0")
    late = [attn_w_out[0].astype(BF16), conv_w_in[0].astype(BF16), conv_w_out[0].astype(BF16),
            w_mem_kv.astype(BF16).reshape(-1, w_mem_kv.shape[2]), jnp.pad(conv_w[0], ((0, 5), (0, 0)))]
    lands = [lax.dynamic_update_slice(lax.empty((N_DEV,) + t.shape, t.dtype), t[None], (me, 0, 0)) for t in late]
    late_weights, late_token = _split_start(late, lands, _plan_gather_own, 4, wg_in0, "gather_late_start")

    tabs = _rope_tables(pos)
    g0, g1 = _after(norm_g[0:1], late_token), norm_g[1:2]

    hn0, hn0_t, qs, ks, vs, tabs_v, qm0, z0 = _inproj_attn(x, g0, wg_in0, tabs)
    os_, ls_ = [], []
    for j, d in enumerate(DILATIONS):
        if j == 2:
            _, lands = _split_wait(late_weights, _plan_gather_own, ls_[1], "gather_late_wait")
            late_weights, late_token = _split_start([], lands, _plan_gather_pass, 3, ls_[1], "gather_late_pass_start")
        o, l = _attn_fwd(qs[j], ks[j], vs[j], d, late_token)
        os_.append(o)
        ls_.append(l)

    _, gathered = _split_wait(late_weights, _plan_gather_pass, ls_[2], "gather_late_pass_wait")
    wg_out0, wg_in1, wg_out1, wg_kv, cw_all = gathered
    w_out1 = wg_out1.reshape(-1, wg_out1.shape[2])
    n_kv = w_mem_kv.shape[1]
    w_kv = wg_kv.reshape(N_DEV, 2, n_kv, -1).transpose(1, 0, 2, 3).reshape(2, N_DEV * n_kv, -1)
    cw = cw_all[:, 0:3].transpose(1, 0, 2).reshape(3, -1)
    kv = _memkv_fwd(mem, mem_norm_g, w_kv)
    y0, y0_t, mo0, h1 = _post_attn(os_, ls_, qm0, kv[0], z0, x, wg_out0)

    hn1, hn1_t, bg, cg, u, qm1, z1 = _inproj_conv(h1, g1, wg_in1)
    y1, y1_t, mo1, dh2, dh2b, loss_acc, d_final_g = _post_conv_loss(
        bg, cg, u, qm1, kv[1], z1, h1, w_out1, cw, final_g.reshape(1, -1), tgt)

    d_w_out1 = _wgrad_rows(y1_t, dh2b, "wgrad_out1")
    dz1, dbg, dconv, dqm1, dkv1 = _bwd_post_conv(dh2b, w_out1, bg, cg, u, z1, qm1, kv[1], mo1, cw)
    dcg, du, dcw = _bwd_conv(dconv, cg, u, cw)
    dh1, dh1b, dg1, dproj1 = _dgrad_norm([(dbg, 1), (dcg, 1), (du, 1), (dqm1, 1), (dz1, 1)], wg_in1, h1, g1, dh2,
                                         "dgrad_norm_conv")
    d_w_in1 = _wgrad_shards(hn1_t, dproj1, "wgrad_in1")

    d_w_out0 = _wgrad_cols(y0_t, dh1b, wg_out0.shape[2], "wgrad_out0")

    names1 = ["conv_w_in", "conv_w_out", "attn_w_out"]
    grads1 = [d_w_in1, d_w_out1, d_w_out0]
    started, token = _split_start(grads1, [lax.empty((4,) + g.shape[1:], g.dtype) for g in grads1],
                                  _plan_to_sibling, 4, dg1, "rs1_sibling_start")

    gw = GROUP_WIDTH
    ones = (jnp.arange(gw)[:, None] // HEAD_DIM == jnp.arange(gw)[None, :] // HEAD_DIM).astype(BF16)
    ones = _after(ones, token)
    res = _bwd_post_attn(dh1b, wg_out0, z0, os_, ls_, qm0, kv[0], mo0, ones)
    dz0, dos, dls, dqm0, dkv0 = res[0], res[1:4], res[4:7], res[7], res[8]

    grads1, from_sibling = _split_wait(started, _plan_to_sibling, dz0, "rs1_sibling_wait")
    parts1 = [_rs_add_sibling(g, r, ck_arr, "rs_add_sibling_" + n) for g, r, n in zip(grads1, from_sibling, names1)]
    pbs1 = [pb for _, pb in parts1]
    started, token = _split_start(pbs1, [lax.empty((3,) + p.shape[1:], p.dtype) for p in pbs1],
                                  _plan_to_chips, 3, dg1, "rs1_chips_start")

    dqs, dks, dvs = [], [], []
    for j, d in enumerate(DILATIONS):
        dq, dk, dv = _attn_bwd(qs[j], ks[j], vs[j], ls_[j], dos[j], dls[j], tabs_v[j], d, token)
        dqs.append((dq, d))
        dks.append((dk, d))
        dvs.append((dv, d))
    d_w_kv, d_mem_g = _memkv_bwd(jnp.stack([dkv0, dkv1]), w_kv, mem, mem_norm_g)
    dproj0 = _assemble_dproj(dqs + dks + dvs + [(dqm0, 1), (dz0, 1)], wg_in0.shape[2], "assemble_dproj_attn")
    d_w_in0 = _wgrad_shards(hn0_t, dproj0, "wgrad_in0")

    names0 = ["attn_w_in", "w_mem_kv"]
    grads0 = [d_w_in0, d_w_kv]
    started0, token0 = _split_start(grads0, [lax.empty((4,) + g.shape[1:], g.dtype) for g in grads0],
                                    _plan_to_sibling, 4, dg1, "rs0_sibling_start")
    _, from_chips1 = _split_wait(started, _plan_to_chips, token0, "rs1_chips_wait")
    shard = dict(attn_w_in=(attn_w_in[0], m_attn_w_in[0], v_attn_w_in[0]),
                 attn_w_out=(attn_w_out[0], m_attn_w_out[0], v_attn_w_out[0]),
                 conv_w_in=(conv_w_in[0], m_conv_w_in[0], v_conv_w_in[0]),
                 conv_w_out=(conv_w_out[0], m_conv_w_out[0], v_conv_w_out[0]),
                 w_mem_kv=tuple(t.reshape(-1, t.shape[2]) for t in (w_mem_kv, m_w_mem_kv, v_w_mem_kv)))
    big = {}
    for n, (pf, _), r in zip(names1, parts1, from_chips1):
        big[n] = _finish(n, pf, r, *shard[n])

    grads0, from_sibling = _split_wait(started0, _plan_to_sibling, big["conv_w_out"][1], "rs0_sibling_wait")
    parts0 = [_rs_add_sibling(g, r, ck_arr, "rs_add_sibling_" + n) for g, r, n in zip(grads0, from_sibling, names0)]
    pbs0 = [pb for _, pb in parts0]
    started0, token0 = _split_start(pbs0, [lax.empty((3,) + p.shape[1:], p.dtype) for p in pbs0],
                                    _plan_to_chips, 3, dg1, "rs0_chips_start")
    dx, dg0 = _dgrad_norm_dm(dproj0, wg_in0, x, g0, dh1, token0, "dgrad_norm_attn")

    small_part = jnp.concatenate([dg0, dg1, d_mem_g.reshape(2, -1), d_final_g, dcw[0:3]], axis=0)
    small_part = jnp.concatenate([small_part, jnp.broadcast_to(loss_acc[0, 0], small_part.shape)], axis=0)
    small = _sum_devices(_all_gather([small_part], dg0, "gather_small_grads")[0])
    loss = small[8, 0]
    g_conv_w = lax.dynamic_slice(small[5:8], (0, me * LANES), (3, LANES))[None]
    small_g = dict(norm_g=small[0:2], mem_norm_g=small[2:4], conv_w=g_conv_w, final_g=small[4])
    small_w = dict(norm_g=(norm_g, m_norm_g, v_norm_g), mem_norm_g=(mem_norm_g, m_mem_norm_g, v_mem_norm_g),
                   conv_w=(conv_w, m_conv_w, v_conv_w), final_g=(final_g, m_final_g, v_final_g))
    for n, (w, m, v) in small_w.items():
        big[n] = (small_g[n],) + _adamw(w, small_g[n], m, v, "adamw_" + n)

    _, from_chips0 = _split_wait(started0, _plan_to_chips, big["final_g"][1], "rs0_chips_wait")
    for n, (pf, _), r in zip(names0, parts0, from_chips0):
        big[n] = _finish(n, pf, r, *shard[n])
    for n in ("attn_w_in", "attn_w_out", "conv_w_in", "conv_w_out"):
        big[n] = tuple(t[None] for t in big[n])
    big["w_mem_kv"] = tuple(t.reshape(w_mem_kv.shape) for t in big["w_mem_kv"])

    order = ["norm_g", "mem_norm_g", "w_mem_kv", "attn_w_in", "attn_w_out", "conv_w_in", "conv_w", "conv_w_out", "final_g"]
    return (loss, dx[None], *[big[n][0] for n in order], *[big[n][1] for n in order],
            *[big[n][2] for n in order], *[big[n][3] for n in order])
```

```python
import jax
import jax.numpy as jnp
from jax import lax
from jax.experimental import pallas as pl
from jax.experimental.pallas import tpu as pltpu

F32 = jnp.float32
BF16 = jnp.bfloat16

N_DEV = 8
D_MODEL = 1024
HEAD_DIM = 64
ROT_DIM = HEAD_DIM // 4
ROPE_THETA = 500000.0
DILATIONS = (1, 4, 16)
HEADS_PER_GROUP = 8
GROUP_WIDTH = HEADS_PER_GROUP * HEAD_DIM
BLOCK = 128
N_MEM = 256
MEM_HEADS = 4
MEM_WIDTH = MEM_HEADS * HEAD_DIM
CONV_WIDTH = D_MODEL
EPS = 1e-6
SCALE = HEAD_DIM ** -0.5
NEG = -1e30

ADAM_LR = 0.001
ADAM_B1 = 0.9
ADAM_B2 = 0.999
ADAM_EPS = 1e-08
ADAM_WD = 0.01
ADAM_STEP = 10

ROW_TILE = 256
WGRAD_SHARDS = 4
LANES = 128
MESH = pl.DeviceIdType.MESH
ANY = pl.BlockSpec(memory_space=pl.ANY)


def _pallas_call(body, **kw):
    call = pl.pallas_call(body, **kw)

    def run(*args):
        pinned = [pltpu.with_memory_space_constraint(a, pltpu.HBM) if jnp.issubdtype(a.dtype, jnp.floating) else a
                  for a in args]
        return call(*pinned)

    return run


def _dot(a, b):
    return lax.dot_general(a, b, (((1,), (0,)), ((), ())), preferred_element_type=F32)


def _dot_nt(a, b):
    return lax.dot_general(a, b, (((1,), (1,)), ((), ())), preferred_element_type=F32)


def _dot_tn(a, b):
    return lax.dot_general(a, b, (((0,), (0,)), ((), ())), preferred_element_type=F32)


def _params(n_grid, vmem_mb=48):
    return pltpu.CompilerParams(dimension_semantics=("arbitrary",) * n_grid, vmem_limit_bytes=vmem_mb << 20)


def _rows(width, tm=ROW_TILE):
    return pl.BlockSpec((tm, width), lambda i: (i, 0))


def _view_rows(width, d, tm=ROW_TILE):
    return pl.BlockSpec((tm // d, d * width), lambda i: (i, 0))


def _whole(shape):
    return pl.BlockSpec(shape, lambda *_: (0,) * len(shape))


def _resident(shape):
    return pl.BlockSpec(shape, lambda *_: (0,) * len(shape), pipeline_mode=pl.Buffered(1))


def _sds(shape, dtype):
    return pltpu.HBM(shape, dtype)


def _plain(shape, dtype):
    return jax.ShapeDtypeStruct(shape, dtype)


def _silu_parts(z):
    sg = jax.nn.sigmoid(z)
    return z * sg, sg * (1.0 + z * (1.0 - sg))


def _to_view(scr, val, out_ref, d):
    tm, w = val.shape
    if d == 1:
        out_ref[...] = val.astype(out_ref.dtype)
        return
    for cb in range(w // LANES):
        scr[cb] = val[:, cb * LANES:(cb + 1) * LANES]
    for r in range(d):
        for cb in range(w // LANES):
            lo = r * w + cb * LANES
            out_ref[:, lo:lo + LANES] = scr[cb, pl.ds(r, tm // d, stride=d), :].astype(out_ref.dtype)


def _from_view(scr, in_ref, d):
    if d == 1:
        return in_ref[...].astype(F32)
    nc, tm, _ = scr.shape
    w = nc * LANES
    for r in range(d):
        for cb in range(nc):
            lo = r * w + cb * LANES
            scr[cb, pl.ds(r, tm // d, stride=d), :] = in_ref[:, lo:lo + LANES].astype(F32)
    return jnp.concatenate([scr[cb] for cb in range(nc)], axis=1)


def _view_scratch(width, tm=ROW_TILE):
    return pltpu.VMEM((width // LANES, tm, LANES), F32)


def _rope_tables(pos):
    half = ROT_DIM // 2
    inv_freq = ROPE_THETA ** (-jnp.arange(half, dtype=F32) * (2.0 / ROT_DIM))
    ang = pos.astype(F32)[:, None] * inv_freq
    cos, sin = jnp.cos(ang), jnp.sin(ang)
    s = pos.shape[0]
    z8 = jnp.zeros((s, half), F32)
    rest = HEAD_DIM - ROT_DIM
    cosf = jnp.concatenate([cos, cos, jnp.ones((s, rest), F32)], axis=1)
    sa = jnp.concatenate([-sin, z8, jnp.zeros((s, rest), F32)], axis=1)
    sb = jnp.concatenate([z8, sin, jnp.zeros((s, rest), F32)], axis=1)
    return tuple(jnp.tile(t, (1, LANES // HEAD_DIM)) for t in (cosf, sa, sb))


def _rope_fwd(t, cv, sav, sbv):
    w = t.shape[1]
    return t * cv + pltpu.roll(t, w - ROT_DIM // 2, 1) * sav + pltpu.roll(t, ROT_DIM // 2, 1) * sbv


def _rope_bwd(g, cv, sav, sbv):
    w = g.shape[1]
    return g * cv + pltpu.roll(g * sav, ROT_DIM // 2, 1) + pltpu.roll(g * sbv, w - ROT_DIM // 2, 1)


def _joined_columns(wg_ref):
    assert wg_ref.shape[2] % LANES == 0
    return jnp.concatenate([wg_ref[j] for j in range(N_DEV)], axis=1)


def _join_once(wg_ref, w_scr):
    c = wg_ref.shape[2]

    @pl.when(pl.program_id(0) == 0)
    def _():
        for j in range(N_DEV):
            w_scr[:, j * c:(j + 1) * c] = wg_ref[j]


def _inproj_attn(x, g, wg, tabs):
    s, d_model = x.shape
    gw = GROUP_WIDTH
    n = N_DEV * wg.shape[2]
    nz = n - 9 * gw - MEM_WIDTH
    reps = gw // LANES
    tm = ROW_TILE

    def body(x_ref, g_ref, w_ref, c_ref, sa_ref, sb_ref, hn_ref, hnt_ref, *rest):
        outs, (wj, scr, tscr) = rest[:-3], rest[-3:]
        q_refs, k_refs, v_refs, t_refs, qm_ref, z_ref = outs[0:3], outs[3:6], outs[6:9], outs[9:18], outs[18], outs[19]
        _join_once(w_ref, wj)
        xb = x_ref[...]
        r = lax.rsqrt(jnp.mean(xb * xb, axis=-1, keepdims=True) + EPS)
        hn = ((xb * r) * g_ref[...]).astype(BF16)
        hn_ref[...] = hn
        hnt_ref[...] = hn.T
        proj = lambda lo, hi: _dot(hn, wj[:, lo:hi])
        tab = (c_ref[...], sa_ref[...], sb_ref[...])
        cv, sav, sbv = [jnp.tile(t, (1, reps)) for t in tab]
        for j, d in enumerate(DILATIONS):
            tq = _rope_fwd(proj(j * gw, (j + 1) * gw), cv, sav, sbv)
            _to_view(scr, tq * SCALE, q_refs[j], d)
            tk = _rope_fwd(proj((3 + j) * gw, (4 + j) * gw), cv, sav, sbv)
            _to_view(scr, tk, k_refs[j], d)
            _to_view(scr, proj((6 + j) * gw, (7 + j) * gw), v_refs[j], d)
            for i in range(3):
                _to_view(tscr, tab[i], t_refs[3 * j + i], d)
        qm_ref[...] = proj(9 * gw, 9 * gw + MEM_WIDTH).astype(BF16)
        z_ref[...] = proj(9 * gw + MEM_WIDTH, n)

    views = [_sds((s // d, d * gw), BF16) for d in DILATIONS]
    tviews = [_sds((s // d, d * LANES), F32) for d in DILATIONS for _ in range(3)]
    out_shape = ([_sds((s, d_model), BF16), _sds((d_model, s), BF16)] + views * 3 + tviews
                 + [_sds((s, MEM_WIDTH), BF16), _sds((s, nz), F32)])
    vspecs = [_view_rows(gw, d, tm) for d in DILATIONS]
    tspecs = [_view_rows(LANES, d, tm) for d in DILATIONS for _ in range(3)]
    out_specs = ([_rows(d_model, tm), pl.BlockSpec((d_model, tm), lambda i: (0, i))] + vspecs * 3 + tspecs
                 + [_rows(MEM_WIDTH, tm), _rows(nz, tm)])
    res = _pallas_call(
        body, name="inproj_attn", grid=(s // tm,), out_shape=out_shape,
        in_specs=[_rows(d_model, tm), _whole((1, d_model)), _resident(wg.shape)] + [_rows(LANES, tm)] * 3,
        out_specs=out_specs,
        scratch_shapes=[pltpu.VMEM((d_model, n), BF16), _view_scratch(gw, tm), _view_scratch(LANES, tm)],
        compiler_params=_params(1, 60),
    )(x, g, wg, *tabs)
    tabs_v = [res[11 + 3 * j:14 + 3 * j] for j in range(3)]
    return res[0], res[1], res[2:5], res[5:8], res[8:11], tabs_v, res[20], res[21]


def _band_mask(n_keys):
    qi = lax.broadcasted_iota(jnp.int32, (BLOCK, n_keys), 0)
    kj = lax.broadcasted_iota(jnp.int32, (BLOCK, n_keys), 1)
    if n_keys == BLOCK:
        return kj <= qi
    return jnp.logical_or(jnp.logical_and(kj < BLOCK, kj >= qi), jnp.logical_and(kj >= BLOCK, (kj - BLOCK) <= qi))


def _low_head_lanes():
    return lax.broadcasted_iota(jnp.int32, (1, LANES), 1) < HEAD_DIM


def _split_pair(t, low):
    zero = jnp.zeros_like(t)
    return jnp.where(low, t, zero), jnp.where(low, zero, t)


def _pair_specs(d, nb, w):
    if nb == 1:
        return pl.BlockSpec((BLOCK, 2 * w), lambda n: (0, n)), None
    half = nb // 2
    two = pl.BlockSpec((2 * BLOCK, w), lambda n: (n % half, n // half))
    before = pl.BlockSpec((BLOCK, w), lambda n: (jnp.maximum(2 * (n % half) - 1, 0), n // half))
    return two, before


def _head_tiles(w, col0):
    return ([slice(p * LANES, (p + 1) * LANES) for p in range(w // LANES)],
            [slice(col0 + p * LANES, col0 + (p + 1) * LANES) for p in range(w // LANES)])


def _attend_fwd(q_ref, o_ref, lse_ref, rows, col0, kk, vv):
    w = kk.shape[1]
    valid = _band_mask(kk.shape[0])
    low = _low_head_lanes()
    pairs, qcols = _head_tiles(w, col0)
    qs_ = [h for qc in qcols for h in _split_pair(q_ref[rows, qc], low)]
    k2s = [kk[:, pr] for pr in pairs for _ in range(2)]
    scs = [jnp.where(valid, _dot_nt(qh, k2), NEG) for qh, k2 in zip(qs_, k2s)]
    ms = [jnp.max(sc, axis=-1, keepdims=True) for sc in scs]
    ps = [jnp.exp(sc - m) for sc, m in zip(scs, ms)]
    ls = [jnp.sum(p, axis=-1, keepdims=True) for p in ps]
    pns = [(p * (1.0 / l)).astype(BF16) for p, l in zip(ps, ls)]
    for i, (pr, qc) in enumerate(zip(pairs, qcols)):
        v2 = vv[:, pr]
        a, b = 2 * i, 2 * i + 1
        o_ref[rows, qc] = jnp.where(low, _dot(pns[a], v2), _dot(pns[b], v2))
        lse_ref[rows, qc] = jnp.where(low, ms[a] + jnp.log(ls[a]), ms[b] + jnp.log(ls[b]))


TOP, BOTTOM = slice(0, BLOCK), slice(BLOCK, 2 * BLOCK)


def _attn_fwd(q, k, v, d, token):
    ln, dw = q.shape
    w = dw // d
    nb = ln // BLOCK
    two, before = _pair_specs(d, nb, w)

    def body_streams(_, q_ref, kc_ref, vc_ref, o_ref, lse_ref):
        for sb in range(2):
            cols = slice(sb * w, (sb + 1) * w)
            _attend_fwd(q_ref, o_ref, lse_ref, TOP, sb * w, kc_ref[:, cols], vc_ref[:, cols])

    def body_blocks(_, q_ref, kp_ref, kc_ref, vp_ref, vc_ref, o_ref, lse_ref):
        first = pl.program_id(0) % (nb // 2) == 0
        pl.when(first)(lambda: _attend_fwd(q_ref, o_ref, lse_ref, TOP, 0, kc_ref[TOP, :], vc_ref[TOP, :]))
        pl.when(jnp.logical_not(first))(lambda: _attend_fwd(
            q_ref, o_ref, lse_ref, TOP, 0, jnp.concatenate([kp_ref[...], kc_ref[TOP, :]], axis=0),
            jnp.concatenate([vp_ref[...], vc_ref[TOP, :]], axis=0)))
        _attend_fwd(q_ref, o_ref, lse_ref, BOTTOM, 0, kc_ref[...], vc_ref[...])

    if nb == 1:
        body, in_specs, args = body_streams, [ANY, two, two, two], (token, q, k, v)
    else:
        body, in_specs, args = body_blocks, [ANY, two, before, two, before, two], (token, q, k, k, v, v)
    return _pallas_call(
        body, name=f"attn_fwd_d{d}", grid=(d * nb // 2,), out_shape=[_sds((ln, dw), F32)] * 2,
        in_specs=in_specs, out_specs=[two, two], compiler_params=_params(1, 32),
    )(*args)


def _memkv_fwd(mem, g, w):
    n_layers = w.shape[0]

    def body(mem_ref, g_ref, w_ref, kv_ref):
        mb = mem_ref[...]
        r = lax.rsqrt(jnp.mean(mb * mb, axis=-1, keepdims=True) + EPS)
        mn = ((mb * r) * g_ref[...]).astype(BF16)
        kv_ref[...] = _dot(mn, w_ref[...]).astype(BF16)

    return _pallas_call(
        body, name="memkv_fwd", grid=(n_layers,),
        out_shape=_plain((n_layers, N_MEM, 2 * MEM_WIDTH), BF16),
        in_specs=[_whole(mem.shape), pl.BlockSpec((None, 1, D_MODEL), lambda l: (l, 0, 0)),
                  pl.BlockSpec((None, D_MODEL, 2 * MEM_WIDTH), lambda l: (l, 0, 0))],
        out_specs=pl.BlockSpec((None, N_MEM, 2 * MEM_WIDTH), lambda l: (l, 0, 0)),
        compiler_params=_params(1, 32),
    )(mem, g.reshape(n_layers, 1, D_MODEL), w)


def _mix_groups(os_, ls_):
    mx = jnp.maximum(jnp.maximum(ls_[0], ls_[1]), ls_[2])
    es = [jnp.exp(t - mx) for t in ls_]
    inv = 1.0 / (es[0] + es[1] + es[2])
    ws = [e * inv for e in es]
    mix = ws[0] * os_[0] + ws[1] * os_[1] + ws[2] * os_[2]
    return ws, mix


MEM_PAIRS = [slice(p * LANES, (p + 1) * LANES) for p in range(MEM_WIDTH // LANES)]


def _mem_probs(qhs, k2s):
    scs = [_dot_nt(qh, k2) * SCALE for qh, k2 in zip(qhs, k2s)]
    es = [jnp.exp(sc - jnp.max(sc, axis=-1, keepdims=True)) for sc in scs]
    return [e * (1.0 / jnp.sum(e, axis=-1, keepdims=True)) for e in es]


def _mem_attn_into(qm, kv_ref, mo_ref):
    low = _low_head_lanes()
    qhs = [h for pr in MEM_PAIRS for h in _split_pair(qm[:, pr], low)]
    k2s = [kv_ref[:, pr] for pr in MEM_PAIRS for _ in range(2)]
    ps = [p.astype(BF16) for p in _mem_probs(qhs, k2s)]
    for i, pr in enumerate(MEM_PAIRS):
        v2 = kv_ref[:, MEM_WIDTH + i * LANES:MEM_WIDTH + (i + 1) * LANES]
        mo_ref[:, pr] = jnp.where(low, _dot(ps[2 * i], v2), _dot(ps[2 * i + 1], v2))


def _mem_attn_bwd(qm, kv_ref, dmem, dqm_ref, dkv_ref):
    low = _low_head_lanes()
    dmb = dmem.astype(BF16)
    vps = [slice(MEM_WIDTH + i * LANES, MEM_WIDTH + (i + 1) * LANES) for i in range(len(MEM_PAIRS))]
    qhs = [h for pr in MEM_PAIRS for h in _split_pair(qm[:, pr], low)]
    dhs = [h for pr in MEM_PAIRS for h in _split_pair(dmb[:, pr], low)]
    k2s = [kv_ref[:, pr] for pr in MEM_PAIRS for _ in range(2)]
    v2s = [kv_ref[:, vp] for vp in vps for _ in range(2)]
    ps = _mem_probs(qhs, k2s)
    dps = [_dot_nt(dh, v2) for dh, v2 in zip(dhs, v2s)]
    dss = [(p * (dp - jnp.sum(dp * p, axis=-1, keepdims=True)) * SCALE).astype(BF16) for p, dp in zip(ps, dps)]
    pbs = [p.astype(BF16) for p in ps]
    for i, (pr, vp) in enumerate(zip(MEM_PAIRS, vps)):
        a, b = 2 * i, 2 * i + 1
        dqm_ref[:, pr] = jnp.where(low, _dot(dss[a], k2s[a]), _dot(dss[b], k2s[b])).astype(BF16)
        dkv_ref[:, pr] += _dot_tn(dss[a], qhs[a]) + _dot_tn(dss[b], qhs[b])
        dkv_ref[:, vp] += _dot_tn(pbs[a], dhs[a]) + _dot_tn(pbs[b], dhs[b])


def _post_attn(os_, ls_, qm, kv, z, x, wg_out):
    s, d_model = x.shape
    gw = GROUP_WIDTH
    nb = gw + MEM_WIDTH
    tm = ROW_TILE

    def body(o0, o1, o2, l0, l1, l2, qm_ref, kv_ref, z_ref, x_ref, w_ref, y_ref, yt_ref, mo_ref, h_ref, s0, s1):
        ov, lv = [], []
        for o_ref, l_ref, d in zip((o0, o1, o2), (l0, l1, l2), DILATIONS):
            ov.append(_from_view(s0, o_ref, d))
            lv.append(_from_view(s1, l_ref, d))
        _, mix = _mix_groups(ov, lv)
        _mem_attn_into(qm_ref[...], kv_ref, mo_ref)
        sz, _ = _silu_parts(z_ref[...])
        y_ref[:, :gw] = (mix * sz[:, :gw]).astype(BF16)
        y_ref[:, gw:] = (mo_ref[...] * sz[:, gw:]).astype(BF16)
        y = y_ref[...]
        yt_ref[...] = y.T
        h_ref[...] = x_ref[...] + _dot(y, _joined_columns(w_ref))

    vspecs = [_view_rows(gw, d) for d in DILATIONS]
    return _pallas_call(
        body, name="post_attn", grid=(s // tm,),
        out_shape=[_sds((s, nb), BF16), _sds((nb, s), BF16), _sds((s, MEM_WIDTH), F32), _sds((s, d_model), F32)],
        in_specs=vspecs * 2 + [_rows(MEM_WIDTH), _whole(kv.shape), _rows(nb), _rows(d_model), _whole(wg_out.shape)],
        out_specs=[_rows(nb), pl.BlockSpec((nb, tm), lambda i: (0, i)), _rows(MEM_WIDTH), _rows(d_model)],
        scratch_shapes=[_view_scratch(gw), _view_scratch(gw)],
        compiler_params=_params(1, 40),
    )(*os_, *ls_, qm, kv, z, x, wg_out)


def _inproj_conv(x, g, wg):
    s, d_model = x.shape
    c = CONV_WIDTH
    n = N_DEV * wg.shape[2]
    nz = n - 3 * c - MEM_WIDTH
    tm = ROW_TILE

    def body(x_ref, g_ref, w_ref, hn_ref, bg_ref, cg_ref, u_ref, qm_ref, z_ref, wj):
        _join_once(w_ref, wj)
        xb = x_ref[...]
        r = lax.rsqrt(jnp.mean(xb * xb, axis=-1, keepdims=True) + EPS)
        hn = ((xb * r) * g_ref[...]).astype(BF16)
        hn_ref[...] = hn
        bg_ref[...] = _dot(hn, wj[:, 0:c])
        cg_ref[...] = _dot(hn, wj[:, c:2 * c])
        u_ref[...] = _dot(hn, wj[:, 2 * c:3 * c])
        qm_ref[...] = _dot(hn, wj[:, 3 * c:3 * c + MEM_WIDTH]).astype(BF16)
        z_ref[...] = _dot(hn, wj[:, 3 * c + MEM_WIDTH:])

    return _pallas_call(
        body, name="inproj_conv", grid=(s // tm,),
        out_shape=[_sds((s, d_model), BF16)] + [_sds((s, c), F32)] * 3 + [_sds((s, MEM_WIDTH), BF16), _sds((s, nz), F32)],
        in_specs=[_rows(d_model), _whole((1, d_model)), _resident(wg.shape)],
        out_specs=[_rows(d_model)] + [_rows(c)] * 3 + [_rows(MEM_WIDTH), _rows(nz)],
        scratch_shapes=[pltpu.VMEM((d_model, n), BF16)],
        compiler_params=_params(1, 60),
    )(x, g, wg)


HALO = 8


def _halo_before(width, tm=ROW_TILE):
    return pl.BlockSpec((HALO, width), lambda i: (jnp.maximum(i * (tm // HALO) - 1, 0), 0))


def _halo_after(width, n_rows, tm=ROW_TILE):
    return pl.BlockSpec((HALO, width), lambda i: (jnp.minimum((i + 1) * (tm // HALO), n_rows // HALO - 1), 0))


def _conv_taps(cg_ref, u_ref, cgh_ref, uh_ref, i):
    a = cg_ref[...] * u_ref[...]
    ah = jnp.where(i > 0, cgh_ref[...] * uh_ref[...], 0.0)
    row = lax.broadcasted_iota(jnp.int32, a.shape, 0)
    a1 = jnp.where(row == 0, ah[HALO - 1:HALO], pltpu.roll(a, 1, 0))
    a2 = jnp.where(row == 0, ah[HALO - 2:HALO - 1], jnp.where(row == 1, ah[HALO - 1:HALO], pltpu.roll(a, 2, 0)))
    return a, a1, a2


def _post_conv_loss(bg, cg, u, qm, kv, z, h1, w_out, cw, gf, tgt):
    s, d = h1.shape
    c = CONV_WIDTH
    nb = c + MEM_WIDTH
    tm = ROW_TILE

    def body(bg_ref, cg_ref, u_ref, cgh_ref, uh_ref, qm_ref, kv_ref, z_ref, h_ref, w_ref, cw_ref, gf_ref, t_ref,
             y_ref, yt_ref, mo_ref, dh_ref, dhb_ref, loss_ref, dgf_ref):
        i = pl.program_id(0)
        a, a1, a2 = _conv_taps(cg_ref, u_ref, cgh_ref, uh_ref, i)
        conv = cw_ref[0:1, :] * a2 + cw_ref[1:2, :] * a1 + cw_ref[2:3, :] * a
        mix = bg_ref[...] * conv
        _mem_attn_into(qm_ref[...], kv_ref, mo_ref)
        sz, _ = _silu_parts(z_ref[...])
        y_ref[:, :c] = (mix * sz[:, :c]).astype(BF16)
        y_ref[:, c:] = (mo_ref[...] * sz[:, c:]).astype(BF16)
        y = y_ref[...]
        yt_ref[...] = y.T
        h2 = h_ref[...] + _dot(y, w_ref[...])
        r = lax.rsqrt(jnp.mean(h2 * h2, axis=-1, keepdims=True) + EPS)
        nh = h2 * r
        gfv = gf_ref[...]
        diff = nh * gfv - t_ref[...]
        dout = diff * (1.0 / d)
        dn = dout * gfv
        dh2 = r * dn - h2 * ((r * r * r) * jnp.mean(dn * h2, axis=-1, keepdims=True))
        dh_ref[...] = dh2
        dhb_ref[...] = dh2.astype(BF16)

        @pl.when(i == 0)
        def _():
            loss_ref[...] = jnp.zeros_like(loss_ref)
            dgf_ref[...] = jnp.zeros_like(dgf_ref)

        loss_ref[...] += 0.5 * jnp.sum(jnp.mean(diff * diff, axis=-1, keepdims=True))
        dgf_ref[...] += jnp.sum(dout * nh, axis=0, keepdims=True)

    return _pallas_call(
        body, name="post_conv_loss", grid=(s // tm,),
        out_shape=[_sds((s, nb), BF16), _sds((nb, s), BF16), _sds((s, MEM_WIDTH), F32), _sds((s, d), F32),
                   _sds((s, d), BF16), _plain((8, LANES), F32), _plain((1, d), F32)],
        in_specs=[_rows(c)] * 3 + [_halo_before(c)] * 2 + [_rows(MEM_WIDTH), _whole(kv.shape), _rows(nb), _rows(d),
                  _whole(w_out.shape), _whole(cw.shape), _whole((1, d)), _rows(d)],
        out_specs=[_rows(nb), pl.BlockSpec((nb, tm), lambda i: (0, i)), _rows(MEM_WIDTH), _rows(d), _rows(d),
                   _whole((8, LANES)), _whole((1, d))],
        compiler_params=_params(1, 48),
    )(bg, cg, u, cg, u, qm, kv, z, h1, w_out, cw, gf, tgt)


def _bwd_post_conv(dhb, w_out, bg, cg, u, z, qm, kv, mo, cw):
    s = dhb.shape[0]
    c = CONV_WIDTH
    nb = c + MEM_WIDTH

    def body(dh_ref, w_ref, bg_ref, cg_ref, u_ref, cgh_ref, uh_ref, z_ref, qm_ref, kv_ref, mo_ref, cw_ref,
             dz_ref, dbg_ref, dc_ref, dqm_ref, dkv_ref):
        i = pl.program_id(0)

        @pl.when(i == 0)
        def _():
            dkv_ref[...] = jnp.zeros_like(dkv_ref)

        dy = _dot_nt(dh_ref[...], w_ref[...])
        sz, dsz = _silu_parts(z_ref[...])
        a, a1, a2 = _conv_taps(cg_ref, u_ref, cgh_ref, uh_ref, i)
        conv = cw_ref[0:1, :] * a2 + cw_ref[1:2, :] * a1 + cw_ref[2:3, :] * a
        bgv = bg_ref[...]
        dz_ref[:, :c] = (dy[:, :c] * (bgv * conv) * dsz[:, :c]).astype(BF16)
        dz_ref[:, c:] = (dy[:, c:] * mo_ref[...] * dsz[:, c:]).astype(BF16)
        dbr = dy * sz
        dmix = dbr[:, :c]
        dbg_ref[...] = (dmix * conv).astype(BF16)
        dc_ref[...] = dmix * bgv
        _mem_attn_bwd(qm_ref[...], kv_ref, dbr[:, c:], dqm_ref, dkv_ref)

    return _pallas_call(
        body, name="bwd_post_conv", grid=(s // ROW_TILE,),
        out_shape=[_sds((s, nb), BF16), _sds((s, c), BF16), _sds((s, c), F32), _sds((s, MEM_WIDTH), BF16),
                   _plain(kv.shape, F32)],
        in_specs=[_rows(D_MODEL), _whole(w_out.shape)] + [_rows(c)] * 3 + [_halo_before(c)] * 2
                 + [_rows(nb), _rows(MEM_WIDTH), _whole(kv.shape), _rows(MEM_WIDTH), _whole(cw.shape)],
        out_specs=[_rows(nb), _rows(c), _rows(c), _rows(MEM_WIDTH), _whole(kv.shape)],
        compiler_params=_params(1, 48),
    )(dhb, w_out, bg, cg, u, cg, u, z, qm, kv, mo, cw)


def _bwd_conv(dconv, cg, u, cw):
    s, c = dconv.shape
    tm = ROW_TILE
    last = s // tm - 1

    def body(dc_ref, dcn_ref, cg_ref, u_ref, cgh_ref, uh_ref, cw_ref, dcg_ref, du_ref, dcw_ref):
        i = pl.program_id(0)

        @pl.when(i == 0)
        def _():
            dcw_ref[...] = jnp.zeros_like(dcw_ref)

        dc = dc_ref[...]
        dcn = jnp.where(i < last, dcn_ref[...], 0.0)
        row = lax.broadcasted_iota(jnp.int32, dc.shape, 0)
        d1 = jnp.where(row == tm - 1, dcn[0:1], pltpu.roll(dc, tm - 1, 0))
        d2 = jnp.where(row == tm - 1, dcn[1:2], jnp.where(row == tm - 2, dcn[0:1], pltpu.roll(dc, tm - 2, 0)))
        da = cw_ref[2:3, :] * dc + cw_ref[1:2, :] * d1 + cw_ref[0:1, :] * d2
        a, a1, a2 = _conv_taps(cg_ref, u_ref, cgh_ref, uh_ref, i)
        dcg_ref[...] = (da * u_ref[...]).astype(BF16)
        du_ref[...] = (da * cg_ref[...]).astype(BF16)
        dcw_ref[0:1, :] += jnp.sum(dc * a2, axis=0, keepdims=True)
        dcw_ref[1:2, :] += jnp.sum(dc * a1, axis=0, keepdims=True)
        dcw_ref[2:3, :] += jnp.sum(dc * a, axis=0, keepdims=True)

    return _pallas_call(
        body, name="bwd_conv", grid=(s // tm,),
        out_shape=[_sds((s, c), BF16), _sds((s, c), BF16), _plain((8, c), F32)],
        in_specs=[_rows(c), _halo_after(c, s), _rows(c), _rows(c), _halo_before(c), _halo_before(c), _whole(cw.shape)],
        out_specs=[_rows(c), _rows(c), _whole((8, c))], compiler_params=_params(1, 40),
    )(dconv, dconv, cg, u, cg, u, cw)


def _dgrad_norm(pieces, wg, h, g, dres, name):
    s, d_model = h.shape
    c = wg.shape[2]
    n = N_DEV * c
    tm = ROW_TILE
    widths = [p.shape[1] // d for p, d in pieces]
    assert sum(widths) == n
    n_p = len(pieces)

    def body(*refs):
        p_refs = refs[:n_p]
        w_ref, h_ref, g_ref, dr_ref, dh_ref, dhb_ref, dg_ref, dpt_ref, dp, scr, wj = refs[n_p:]
        _join_once(w_ref, wj)

        @pl.when(pl.program_id(0) == 0)
        def _():
            dg_ref[...] = jnp.zeros_like(dg_ref)

        off = 0
        for p_ref, (_, d), wd in zip(p_refs, pieces, widths):
            if d == 1:
                dp[:, off:off + wd] = p_ref[...]
            else:
                dp[:, off:off + wd] = _from_view(scr, p_ref, d).astype(BF16)
            off += wd
        dpt_ref[...] = dp[...].T
        dhn = _dot_nt(dp[...], wj[...])
        hb = h_ref[...]
        r = lax.rsqrt(jnp.mean(hb * hb, axis=-1, keepdims=True) + EPS)
        dg_ref[...] += jnp.sum(dhn * (hb * r), axis=0, keepdims=True)
        dn = dhn * g_ref[...]
        dh = dr_ref[...] + r * dn - hb * ((r * r * r) * jnp.mean(dn * hb, axis=-1, keepdims=True))
        dh_ref[...] = dh
        dhb_ref[...] = dh.astype(BF16)

    p_specs = [_view_rows(wd, d) for (_, d), wd in zip(pieces, widths)]
    return _pallas_call(
        body, name=name, grid=(s // tm,),
        out_shape=[_plain((s, d_model), F32), _sds((s, d_model), BF16), _plain((1, d_model), F32), _sds((n, s), BF16)],
        in_specs=p_specs + [_resident(wg.shape), _rows(d_model), _whole((1, d_model)), _rows(d_model)],
        out_specs=[_rows(d_model), _rows(d_model), _whole((1, d_model)), pl.BlockSpec((n, tm), lambda i: (0, i))],
        scratch_shapes=[pltpu.VMEM((tm, n), BF16), _view_scratch(GROUP_WIDTH), pltpu.VMEM((d_model, n), BF16)],
        compiler_params=_params(1, 60),
    )(*[p for p, _ in pieces], wg, h, g, dres)


def _assemble_dproj(pieces, c, name):
    n = N_DEV * c
    tm = ROW_TILE
    widths = [p.shape[1] // d for p, d in pieces]
    assert sum(widths) == n
    s = pieces[0][0].shape[0] * pieces[0][1]
    n_p = len(pieces)

    def body(*refs):
        p_refs, (dpd_ref, dp, scr) = refs[:n_p], refs[n_p:]
        off = 0
        for p_ref, (_, d), wd in zip(p_refs, pieces, widths):
            if d == 1:
                dp[:, off:off + wd] = p_ref[...]
            else:
                dp[:, off:off + wd] = _from_view(scr, p_ref, d).astype(BF16)
            off += wd
        for j in range(N_DEV):
            dpd_ref[j] = dp[:, j * c:(j + 1) * c]

    return _pallas_call(
        body, name=name, grid=(s // tm,), out_shape=_sds((N_DEV, s, c), BF16),
        in_specs=[_view_rows(wd, d) for (_, d), wd in zip(pieces, widths)],
        out_specs=pl.BlockSpec((N_DEV, tm, c), lambda i: (0, i, 0)),
        scratch_shapes=[pltpu.VMEM((tm, n), BF16), _view_scratch(GROUP_WIDTH)],
        compiler_params=_params(1, 40),
    )(*[p for p, _ in pieces])


def _dgrad_norm_dm(dproj_dm, wg, h, g, dres, token, name):
    s, d_model = h.shape
    c = wg.shape[2]
    tm = ROW_TILE

    def body(_, dp_ref, w_ref, h_ref, g_ref, dr_ref, dh_ref, dg_ref):
        @pl.when(pl.program_id(0) == 0)
        def _():
            dg_ref[...] = jnp.zeros_like(dg_ref)

        dhn = jnp.zeros((tm, d_model), F32)
        for j in range(N_DEV):
            dhn += _dot_nt(dp_ref[j], w_ref[j])
        hb = h_ref[...]
        r = lax.rsqrt(jnp.mean(hb * hb, axis=-1, keepdims=True) + EPS)
        dg_ref[...] += jnp.sum(dhn * (hb * r), axis=0, keepdims=True)
        dn = dhn * g_ref[...]
        dh_ref[...] = dr_ref[...] + r * dn - hb * ((r * r * r) * jnp.mean(dn * hb, axis=-1, keepdims=True))

    return _pallas_call(
        body, name=name, grid=(s // tm,),
        out_shape=[_plain((s, d_model), F32), _plain((1, d_model), F32)],
        in_specs=[ANY, pl.BlockSpec((N_DEV, tm, c), lambda i: (0, i, 0)), _whole(wg.shape), _rows(d_model),
                  _whole((1, d_model)), _rows(d_model)],
        out_specs=[_rows(d_model), _whole((1, d_model))],
        compiler_params=_params(1, 60),
    )(token, dproj_dm, wg, h, g, dres)


def _wgrad_shards(a_t, b_dm, name):
    m, s = a_t.shape
    c = b_dm.shape[2]

    def body(a_ref, b_ref, o_ref):
        o_ref[...] = _dot(a_ref[...], b_ref[...]).astype(BF16)

    return _pallas_call(
        body, name=name, grid=(N_DEV,), out_shape=_sds((N_DEV, m, c), BF16),
        in_specs=[_whole(a_t.shape), pl.BlockSpec((None, s, c), lambda j: (j, 0, 0))],
        out_specs=pl.BlockSpec((None, m, c), lambda j: (j, 0, 0)), compiler_params=_params(1, 40),
    )(a_t, b_dm)


def _wgrad_shards_t(dp_t, h, c, name):
    n, s = dp_t.shape
    d_model = h.shape[1]
    per_step = 2

    def body(a_ref, b_ref, o_ref):
        o_ref[...] = _dot(a_ref[...], b_ref[...]).astype(BF16).reshape(per_step, c, d_model)

    return _pallas_call(
        body, name=name, grid=(N_DEV // per_step,), out_shape=_sds((N_DEV, c, d_model), BF16),
        in_specs=[pl.BlockSpec((per_step * c, s), lambda j: (j, 0)), _resident(h.shape)],
        out_specs=pl.BlockSpec((per_step, c, d_model), lambda j: (j, 0, 0)), compiler_params=_params(1, 40),
    )(dp_t, h)


def _wgrad_cols(a_t, b, c, name):
    m, s = a_t.shape
    assert c % LANES == 0

    def body(a_ref, b_ref, o_ref):
        wide = _dot(a_ref[...], b_ref[...]).astype(BF16)
        for j in range(WGRAD_SHARDS):
            o_ref[j] = wide[:, j * c:(j + 1) * c]

    return _pallas_call(
        body, name=name, grid=(N_DEV // WGRAD_SHARDS,), out_shape=_sds((N_DEV, m, c), BF16),
        in_specs=[_whole(a_t.shape), pl.BlockSpec((s, WGRAD_SHARDS * c), lambda j: (0, j))],
        out_specs=pl.BlockSpec((WGRAD_SHARDS, m, c), lambda j: (j, 0, 0)), compiler_params=_params(1, 40),
    )(a_t, b)


def _wgrad_rows(a_t, b, name):
    m, s = a_t.shape
    n = b.shape[1]
    mr = m // N_DEV

    def body(a_ref, b_ref, o_ref):
        o_ref[...] = _dot(a_ref[...], b_ref[...]).astype(BF16).reshape(WGRAD_SHARDS, mr, n)

    return _pallas_call(
        body, name=name, grid=(N_DEV // WGRAD_SHARDS,), out_shape=_sds((N_DEV, mr, n), BF16),
        in_specs=[pl.BlockSpec((WGRAD_SHARDS * mr, s), lambda j: (j, 0)), _whole(b.shape)],
        out_specs=pl.BlockSpec((WGRAD_SHARDS, mr, n), lambda j: (j, 0, 0)), compiler_params=_params(1, 40),
    )(a_t, b)


def _memkv_bwd(dkv, w, mem, g):
    n_layers = w.shape[0]
    rows = D_MODEL // N_DEV

    def body(dkv_ref, w_ref, mem_ref, g_ref, dw_ref, dg_ref):
        mb = mem_ref[...]
        r = lax.rsqrt(jnp.mean(mb * mb, axis=-1, keepdims=True) + EPS)
        nm = mb * r
        mn = (nm * g_ref[...]).astype(BF16)
        dkvb = dkv_ref[...].astype(BF16)
        dw_ref[...] = _dot_tn(mn, dkvb).astype(BF16).reshape(N_DEV, rows, 2 * MEM_WIDTH)
        dmn = _dot_nt(dkvb, w_ref[...])
        dg_ref[...] = jnp.sum(dmn * nm, axis=0, keepdims=True)

    lay = lambda *shape: pl.BlockSpec((None,) + shape, lambda l: (l, 0, 0))
    return _pallas_call(
        body, name="memkv_bwd", grid=(n_layers,),
        out_shape=[_sds((N_DEV, n_layers * rows, 2 * MEM_WIDTH), BF16), _plain((n_layers, 1, D_MODEL), F32)],
        in_specs=[lay(N_MEM, 2 * MEM_WIDTH), lay(D_MODEL, 2 * MEM_WIDTH), _whole(mem.shape), lay(1, D_MODEL)],
        out_specs=[pl.BlockSpec((N_DEV, rows, 2 * MEM_WIDTH), lambda l: (0, l, 0)), lay(1, D_MODEL)],
        compiler_params=_params(1, 32),
    )(dkv, w, mem, g.reshape(n_layers, 1, D_MODEL))


def _bwd_post_attn(dhb, wg_out, z, os_, ls_, qm, kv, mo, head_ones):
    s = dhb.shape[0]
    gw = GROUP_WIDTH
    nb = gw + MEM_WIDTH
    tm = ROW_TILE

    def body(dh_ref, w_ref, z_ref, o0, o1, o2, l0, l1, l2, qm_ref, kv_ref, mo_ref, bd_ref,
             dz_ref, do0, do1, do2, dl0, dl1, dl2, dqm_ref, dkv_ref, s0, s1):
        @pl.when(pl.program_id(0) == 0)
        def _():
            dkv_ref[...] = jnp.zeros_like(dkv_ref)

        dy = _dot_nt(dh_ref[...], _joined_columns(w_ref))
        ov, lv = [], []
        for o_ref, l_ref, d in zip((o0, o1, o2), (l0, l1, l2), DILATIONS):
            ov.append(_from_view(s0, o_ref, d))
            lv.append(_from_view(s1, l_ref, d))
        ws, mix = _mix_groups(ov, lv)
        sz, dsz = _silu_parts(z_ref[...])
        dz_ref[:, :gw] = (dy[:, :gw] * mix * dsz[:, :gw]).astype(BF16)
        dz_ref[:, gw:] = (dy[:, gw:] * mo_ref[...] * dsz[:, gw:]).astype(BF16)
        dbr = dy * sz
        dmix = dbr[:, :gw]
        t = dmix * mix
        th = t.astype(BF16)
        tl = (t - th.astype(F32)).astype(BF16)
        rs = _dot(th, bd_ref[...]) + _dot(tl, bd_ref[...])
        for wg_, do_ref, dl_ref, d in zip(ws, (do0, do1, do2), (dl0, dl1, dl2), DILATIONS):
            _to_view(s0, wg_ * dmix, do_ref, d)
            _to_view(s1, wg_ * rs, dl_ref, d)
        _mem_attn_bwd(qm_ref[...], kv_ref, dbr[:, gw:], dqm_ref, dkv_ref)

    vspecs = [_view_rows(gw, d) for d in DILATIONS]
    return _pallas_call(
        body, name="bwd_post_attn", grid=(s // tm,),
        out_shape=[_sds((s, nb), BF16)] + [_sds((s // d, d * gw), BF16) for d in DILATIONS]
                  + [_sds((s // d, d * gw), F32) for d in DILATIONS] + [_sds((s, MEM_WIDTH), BF16), _plain(kv.shape, F32)],
        in_specs=[_rows(D_MODEL), _whole(wg_out.shape), _rows(nb)] + vspecs * 2
                 + [_rows(MEM_WIDTH), _whole(kv.shape), _rows(MEM_WIDTH), _whole(head_ones.shape)],
        out_specs=[_rows(nb)] + vspecs * 2 + [_rows(MEM_WIDTH), _whole(kv.shape)],
        scratch_shapes=[_view_scratch(gw), _view_scratch(gw)],
        compiler_params=_params(1, 48),
    )(dhb, wg_out, z, *os_, *ls_, qm, kv, mo, head_ones)


def _attn_bwd(q, k, v, lse, do, dl, tabs, d, token):
    ln, dw = q.shape
    w = dw // d
    nb = ln // BLOCK
    reps = w // LANES
    two, before = _pair_specs(d, nb, w)
    two_t, _ = _pair_specs(d, nb, LANES)

    def attend(q_ref, l_ref, do_ref, dl_ref, dqs, acck, accv, rows, col0, kk, vv, acc_rows):
        valid = _band_mask(kk.shape[0])
        low = _low_head_lanes()
        pairs, qcols = _head_tiles(w, col0)
        cols = [slice(col0 + h * HEAD_DIM, col0 + h * HEAD_DIM + 1) for h in range(HEADS_PER_GROUP)]
        qhs = [h for qc in qcols for h in _split_pair(q_ref[rows, qc], low)]
        dobs = [h for qc in qcols for h in _split_pair(do_ref[rows, qc], low)]
        k2s = [kk[:, pr] for pr in pairs for _ in range(2)]
        v2s = [vv[:, pr] for pr in pairs for _ in range(2)]
        scs = [jnp.where(valid, _dot_nt(qh, k2), NEG) for qh, k2 in zip(qhs, k2s)]
        dps = [_dot_nt(dob, v2) for dob, v2 in zip(dobs, v2s)]
        ps = [jnp.exp(sc - l_ref[rows, col]) for sc, col in zip(scs, cols)]
        dss = [(p * (dp - dl_ref[rows, col])).astype(BF16) for p, dp, col in zip(ps, dps, cols)]
        pbs = [p.astype(BF16) for p in ps]
        for i, qc in enumerate(qcols):
            a, b = 2 * i, 2 * i + 1
            dqs[rows, qc] = jnp.where(low, _dot(dss[a], k2s[a]), _dot(dss[b], k2s[b])) * SCALE
            acck[acc_rows, qc] += _dot_tn(dss[a], qhs[a]) + _dot_tn(dss[b], qhs[b])
            accv[acc_rows, qc] += _dot_tn(pbs[a], dobs[a]) + _dot_tn(pbs[b], dobs[b])

    def body_streams(_, q_ref, kc_ref, vc_ref, l_ref, do_ref, dl_ref, c_ref, sa_ref, sb_ref,
                     dq_ref, dk_ref, dv_ref, acck, accv, dqs):
        acck[...] = jnp.zeros_like(acck)
        accv[...] = jnp.zeros_like(accv)
        for sb in range(2):
            cols = slice(sb * w, (sb + 1) * w)
            attend(q_ref, l_ref, do_ref, dl_ref, dqs, acck, accv, TOP, sb * w, kc_ref[:, cols], vc_ref[:, cols], TOP)
        tabs2 = [jnp.concatenate([jnp.tile(r[:, sb * LANES:(sb + 1) * LANES], (1, reps)) for sb in range(2)], axis=1)
                 for r in (c_ref, sa_ref, sb_ref)]
        dq_ref[...] = _rope_bwd(dqs[...], *tabs2).astype(BF16)
        dk_ref[...] = _rope_bwd(acck[...], *tabs2).astype(BF16)
        dv_ref[...] = accv[...].astype(BF16)

    def body_blocks(_, q_ref, kp_ref, kc_ref, vp_ref, vc_ref, l_ref, do_ref, dl_ref, cq, saq, sbq, ck, sak, sbk,
                    dq_ref, dk_ref, dv_ref, acck, accv, dqs):
        i = pl.program_id(0) % (nb // 2)

        @pl.when(i == 0)
        def _():
            acck[...] = jnp.zeros_like(acck)
            accv[...] = jnp.zeros_like(accv)

        refs = (q_ref, l_ref, do_ref, dl_ref, dqs, acck, accv)
        pl.when(i == 0)(lambda: attend(*refs, TOP, 0, kc_ref[TOP, :], vc_ref[TOP, :], TOP))
        pl.when(i != 0)(lambda: attend(
            *refs, TOP, 0, jnp.concatenate([kp_ref[...], kc_ref[TOP, :]], axis=0),
            jnp.concatenate([vp_ref[...], vc_ref[TOP, :]], axis=0),
            pl.ds(pl.multiple_of((2 * i - 1) * BLOCK, BLOCK), 2 * BLOCK)))
        attend(*refs, BOTTOM, 0, kc_ref[...], vc_ref[...], pl.ds(pl.multiple_of(2 * i * BLOCK, BLOCK), 2 * BLOCK))
        tq = [jnp.tile(r[...], (1, reps)) for r in (cq, saq, sbq)]
        dq_ref[...] = _rope_bwd(dqs[...], *tq).astype(BF16)

        @pl.when(i == nb // 2 - 1)
        def _():
            for r0 in range(0, nb * BLOCK, 2 * BLOCK):
                rows = slice(r0, r0 + 2 * BLOCK)
                tk = [jnp.tile(r[rows, :], (1, reps)) for r in (ck, sak, sbk)]
                dk_ref[rows, :] = _rope_bwd(acck[rows, :], *tk).astype(BF16)
                dv_ref[rows, :] = accv[rows, :].astype(BF16)

    if nb == 1:
        body = body_streams
        in_specs = [ANY] + [two] * 6 + [two_t] * 3
        args = (token, q, k, v, lse, do, dl, *tabs)
        out_specs = [two, two, two]
        acc_shape = (BLOCK, 2 * w)
    else:
        body = body_blocks
        stream = pl.BlockSpec((nb * BLOCK, w), lambda n: (0, n // (nb // 2)))
        stream_t = pl.BlockSpec((nb * BLOCK, LANES), lambda n: (0, n // (nb // 2)))
        in_specs = [ANY, two, before, two, before, two, two, two, two] + [two_t] * 3 + [stream_t] * 3
        args = (token, q, k, k, v, v, lse, do, dl, *tabs, *tabs)
        out_specs = [two, stream, stream]
        acc_shape = (nb * BLOCK, w)
    return _pallas_call(
        body, name=f"attn_bwd_d{d}", grid=(d * nb // 2,), out_shape=[_sds((ln, dw), BF16)] * 3,
        in_specs=in_specs, out_specs=out_specs,
        scratch_shapes=[pltpu.VMEM(acc_shape, F32), pltpu.VMEM(acc_shape, F32), pltpu.VMEM(two.block_shape, F32)],
        compiler_params=_params(1, 48),
    )(*args)


def _position():
    return lax.axis_index("x"), lax.axis_index("y"), lax.axis_index("c")


def _all_gather(shards, after, name):
    n_a = len(shards)

    def body(*refs):
        x_refs, out_refs = refs[:n_a], refs[n_a + 1:2 * n_a + 1]
        send_sems, recv_sems, local_sems = refs[2 * n_a + 1:]
        x, y, c = _position()
        me, sibling = (x, y, c), (x, y, 1 - c)
        chips = [(1 - x, y), (x, 1 - y), (1 - x, 1 - y)]

        def rows(a, px, py, pc):
            return out_refs[a].at[4 * px + 2 * py + pc]

        def copy(a, k, block, to, own=False):
            return pltpu.make_async_remote_copy(
                src_ref=x_refs[a] if own else rows(a, *block), dst_ref=rows(a, *block),
                send_sem=send_sems.at[a, k], recv_sem=recv_sems.at[a, k], device_id=to, device_id_type=MESH)

        mine = [pltpu.make_async_copy(x_refs[a], rows(a, *me), local_sems.at[a]) for a in range(n_a)]
        for cp in mine:
            cp.start()
        first = []
        for j, chip in enumerate(chips):
            first += [copy(a, 1 + j, me, (*chip, c), own=True) for a in range(n_a)]
        first += [copy(a, 0, me, sibling, own=True) for a in range(n_a)]
        for cp in first:
            cp.start()
        passed = []
        for j, chip in enumerate(chips):
            for a in range(n_a):
                copy(a, 1 + j, (*chip, c), me).wait_recv()
                fwd = copy(a, 4 + j, (*chip, c), sibling)
                fwd.start()
                passed.append(fwd)
        for a in range(n_a):
            copy(a, 0, sibling, me).wait_recv()
        for j, chip in enumerate(chips):
            for a in range(n_a):
                copy(a, 4 + j, (*chip, 1 - c), me).wait_recv()
        for cp in first + passed:
            cp.wait_send()
        for cp in mine:
            cp.wait()

    return _pallas_call(
        body, name=name, out_shape=[_sds((N_DEV,) + t.shape, t.dtype) for t in shards],
        in_specs=[ANY] * (n_a + 1), out_specs=[ANY] * n_a,
        scratch_shapes=[pltpu.SemaphoreType.DMA((n_a, 7)), pltpu.SemaphoreType.DMA((n_a, 7)),
                        pltpu.SemaphoreType.DMA((n_a,))],
    )(*shards, after)


def _all_gather_relay(xs, name):
    def body(x_ref, out_ref, send_sems, recv_sems, local_sem):
        x, y, c = _position()
        me, sibling = (x, y, c), (x, y, 1 - c)
        xn, yn, diag = (1 - x, y, c), (x, 1 - y, c), (1 - x, 1 - y, c)
        src_nb = (x + c * (1 - 2 * x), y + (1 - c) * (1 - 2 * y), c)
        dst_nb = (x + (1 - c) * (1 - 2 * x), y + c * (1 - 2 * y), c)

        def rows(dev):
            return out_ref.at[4 * dev[0] + 2 * dev[1] + dev[2]]

        def copy(k, block, to, own=False):
            return pltpu.make_async_remote_copy(
                src_ref=x_ref if own else rows(block), dst_ref=rows(block),
                send_sem=send_sems.at[k], recv_sem=recv_sems.at[k], device_id=to, device_id_type=MESH)

        mine = pltpu.make_async_copy(x_ref, rows(me), local_sem)
        mine.start()
        first = [copy(1, me, xn, own=True), copy(2, me, yn, own=True), copy(0, me, sibling, own=True)]
        for cp in first:
            cp.start()
        copy(1, xn, me).wait_recv()
        copy(2, yn, me).wait_recv()
        relay = copy(3, src_nb, dst_nb)
        relay.start()
        passed = [copy(4, xn, sibling), copy(5, yn, sibling)]
        for cp in passed:
            cp.start()
        copy(3, diag, me).wait_recv()
        last = copy(6, diag, sibling)
        last.start()
        copy(0, sibling, me).wait_recv()
        for k, blk in ((4, (1 - x, y, 1 - c)), (5, (x, 1 - y, 1 - c)), (6, (1 - x, 1 - y, 1 - c))):
            copy(k, blk, me).wait_recv()
        for cp in first + [relay] + passed + [last]:
            cp.wait_send()
        mine.wait()

    return _pallas_call(
        body, name=name, out_shape=_sds((N_DEV,) + xs.shape, xs.dtype),
        in_specs=[ANY], out_specs=ANY,
        scratch_shapes=[pltpu.SemaphoreType.DMA((7,)), pltpu.SemaphoreType.DMA((7,)), pltpu.SemaphoreType.DMA],
    )(xs)


HBM_SPEC = pl.BlockSpec(memory_space=pltpu.HBM)
SEM_SPEC = pl.BlockSpec(memory_space=pltpu.SEMAPHORE)
EFFECT = pltpu.SideEffectType.DATAFLOW_SIDE_EFFECTING
def _plan_gather_own(src_refs, land_refs):
    x, y, c = _position()
    me = 4 * x + 2 * y + c
    peers = [(x, y, 1 - c), (1 - x, y, c), (x, 1 - y, c), (1 - x, 1 - y, c)]
    return [(src_refs[a], land_refs[a].at[me], (a, k), peer) for k, peer in enumerate(peers) for a in range(len(src_refs))]


def _plan_gather_pass(src_refs, land_refs):
    x, y, c = _position()
    chips = [(1 - x, y), (x, 1 - y), (1 - x, 1 - y)]
    return [(land_refs[a].at[4 * px + 2 * py + c], land_refs[a].at[4 * px + 2 * py + c], (a, j), (x, y, 1 - c))
            for j, (px, py) in enumerate(chips) for a in range(len(land_refs))]


def _plan_to_sibling(src_refs, land_refs):
    x, y, c = _position()
    return [(src_refs[a].at[2 * k + (1 - c)], land_refs[a].at[k], (a, k), (x, y, 1 - c))
            for k in range(4) for a in range(len(src_refs))]


def _plan_to_chips(src_refs, land_refs):
    x, y, c = _position()
    chips = [(1 - x, y), (x, 1 - y), (1 - x, 1 - y)]
    return [(src_refs[a].at[2 * px + py], land_refs[a].at[j], (a, j), (px, py, c))
            for j, (px, py) in enumerate(chips) for a in range(len(src_refs))]


def _split_start(srcs, lands, plan, n_sem, after, name):
    n_s, n_a = len(srcs), len(lands)
    n_b = n_s + n_a

    def body(*refs):
        src_refs, land_refs = refs[:n_s], refs[n_s:n_b]
        send_sems, recv_sems, token = refs[n_b + 1], refs[n_b + 2], refs[-1]
        for src, dst, (a, k), dev in plan(src_refs, land_refs):
            i = a * n_sem + k
            pltpu.make_async_remote_copy(src_ref=src, dst_ref=dst, send_sem=send_sems.at[i], recv_sem=recv_sems.at[i],
                                         device_id=dev, device_id_type=MESH).start()
        token[...] = jnp.zeros_like(token)

    bufs = list(srcs) + list(lands)
    res = pl.pallas_call(
        body, name=name,
        out_shape=(pltpu.SemaphoreType.DMA((n_a * n_sem,)), pltpu.SemaphoreType.DMA((n_a * n_sem,)),
                   *[pltpu.HBM(t.shape, t.dtype) for t in bufs], _plain((8, LANES), F32)),
        in_specs=[HBM_SPEC] * n_b + [ANY],
        out_specs=(SEM_SPEC, SEM_SPEC, *[HBM_SPEC] * n_b, pl.BlockSpec(memory_space=pltpu.VMEM)),
        input_output_aliases={i: 2 + i for i in range(n_b)},
        compiler_params=pltpu.CompilerParams(has_side_effects=EFFECT),
    )(*[pltpu.with_memory_space_constraint(t, pltpu.HBM) for t in bufs], after)
    return (res[0], res[1], res[2:2 + n_s], res[2 + n_s:2 + n_b]), res[-1]


def _split_wait(started, plan, after, name):
    send_sems, recv_sems, srcs, lands = started
    n_s, n_a = len(srcs), len(lands)
    n_b = n_s + n_a
    n_sem = send_sems.shape[0] // n_a

    def body(*refs):
        src_refs, land_refs = refs[:n_s], refs[n_s:n_b]
        s_sems, r_sems = refs[n_b], refs[n_b + 1]
        for src, dst, (a, k), dev in plan(src_refs, land_refs):
            i = a * n_sem + k
            cp = pltpu.make_async_remote_copy(src_ref=src, dst_ref=dst, send_sem=s_sems.at[i], recv_sem=r_sems.at[i],
                                              device_id=dev, device_id_type=MESH)
            cp.wait_send()
            cp.wait_recv()

    bufs = list(srcs) + list(lands)
    res = pl.pallas_call(
        body, name=name, out_shape=tuple(pltpu.HBM(t.shape, t.dtype) for t in bufs),
        in_specs=[HBM_SPEC] * n_b + [SEM_SPEC, SEM_SPEC, ANY],
        out_specs=tuple([HBM_SPEC] * n_b),
        input_output_aliases={i: i for i in range(n_b)},
        compiler_params=pltpu.CompilerParams(has_side_effects=EFFECT),
    )(*bufs, send_sems, recv_sems, after)
    return res[:n_s], res[n_s:]


SUBLANES = 8


def _row_tile(r):
    return max(t for t in range(SUBLANES, ROW_TILE + 1, SUBLANES) if r % t == 0)


def _rs_add_sibling(gp, recv, ck_arr, name):
    _, r, l = gp.shape
    tr = r if r <= 4 * ROW_TILE else _row_tile(r)
    block = lambda k, ck: (k + ck[1] + 1) % 4

    def body(ck_ref, g_ref, r_ref, pf_ref, pb_ref):
        sm = g_ref[...].astype(F32) + r_ref[...].astype(F32)
        pf_ref[...] = sm
        pb_ref[...] = sm.astype(BF16)

    spec = pl.BlockSpec((None, tr, l), lambda i, k, ck: (block(k, ck), i, 0))
    return _pallas_call(
        body, name=name,
        grid_spec=pltpu.PrefetchScalarGridSpec(
            num_scalar_prefetch=1, grid=(r // tr, 4),
            in_specs=[pl.BlockSpec((None, tr, l), lambda i, k, ck: (2 * block(k, ck) + ck[0], i, 0)), spec],
            out_specs=[pl.BlockSpec((tr, l), lambda i, k, ck: (i, 0)), spec]),
        out_shape=[_sds((r, l), F32), _sds((4, r, l), BF16)], compiler_params=_params(2, 32),
    )(ck_arr, gp, recv)


def _adam_update(w, gv, m, v):
    nm = ADAM_B1 * m + (1.0 - ADAM_B1) * gv
    nv = ADAM_B2 * v + (1.0 - ADAM_B2) * (gv * gv)
    m_hat = nm / (1.0 - ADAM_B1 ** ADAM_STEP)
    v_hat = nv / (1.0 - ADAM_B2 ** ADAM_STEP)
    return -ADAM_LR * (m_hat / (jnp.sqrt(v_hat) + ADAM_EPS) + ADAM_WD * w), nm, nv


def _rs_finish_adamw(pf, recv, w, m, v, name):
    r, l = pf.shape
    tr = _row_tile(r)

    def body(p_ref, r_ref, w_ref, m_ref, v_ref, g_ref, d_ref, nm_ref, nv_ref):
        gv = ((p_ref[...] + r_ref[0].astype(F32)) + r_ref[1].astype(F32)) + r_ref[2].astype(F32)
        g_ref[...] = gv
        d_ref[...], nm_ref[...], nv_ref[...] = _adam_update(w_ref[...], gv, m_ref[...], v_ref[...])

    spec = pl.BlockSpec((tr, l), lambda i: (i, 0))
    return _pallas_call(
        body, name=name, grid=(r // tr,),
        in_specs=[spec, pl.BlockSpec((3, tr, l), lambda i: (0, i, 0)), spec, spec, spec], out_specs=[spec] * 4,
        out_shape=[_plain((r, l), F32)] * 4, compiler_params=_params(1, 32),
    )(pf, recv, w, m, v)


def _rs_finish_adamw_t(pf, recv, w_t, m_t, v_t, name):
    r, c = pf.shape
    tr = _row_tile(r)
    cp = -(-c // LANES) * LANES

    def body(p_ref, r_ref, w_ref, m_ref, v_ref, g_ref, d_ref, nm_ref, nv_ref, pad):
        gv = ((p_ref[...] + r_ref[0].astype(F32)) + r_ref[1].astype(F32)) + r_ref[2].astype(F32)
        pad[...] = jnp.zeros_like(pad)
        pad[:, 0:c] = gv
        gt = pad[...].T[0:c, :]
        g_ref[...] = gt
        d_ref[...], nm_ref[...], nv_ref[...] = _adam_update(w_ref[...], gt, m_ref[...], v_ref[...])

    spec = pl.BlockSpec((c, tr), lambda i: (0, i))
    return _pallas_call(
        body, name=name, grid=(r // tr,),
        in_specs=[pl.BlockSpec((tr, c), lambda i: (i, 0)), pl.BlockSpec((3, tr, c), lambda i: (0, i, 0)),
                  spec, spec, spec],
        out_specs=[spec] * 4, scratch_shapes=[pltpu.VMEM((tr, cp), F32)],
        out_shape=[_plain((c, r), F32)] * 4, compiler_params=_params(1, 32),
    )(pf, recv, w_t, m_t, v_t)


def _sum_devices(g):
    def body(g_ref, o_ref):
        acc = g_ref[0]
        for j in range(1, N_DEV):
            acc = acc + g_ref[j]
        o_ref[...] = acc

    return _pallas_call(body, name="sum_devices", out_shape=_plain(g.shape[1:], F32))(g)


def _adamw(w, g, m, v, name):
    shape = w.shape
    w2, g2, m2, v2 = [t.reshape((-1, shape[-1])) for t in (w, g, m, v)]

    def body(w_ref, g_ref, m_ref, v_ref, d_ref, nm_ref, nv_ref):
        d_ref[...], nm_ref[...], nv_ref[...] = _adam_update(w_ref[...], g_ref[...], m_ref[...], v_ref[...])

    outs = _pallas_call(body, name=name, out_shape=[_plain(w2.shape, F32)] * 3)(w2, g2, m2, v2)
    return tuple(t.reshape(shape) for t in outs)


def _after(t, token):
    return t + token[0:1, 0:1].astype(t.dtype)


def _finish(name, pf, recv, w, m, v):
    if name == "attn_w_in":
        res = _rs_finish_adamw_t(pf, recv, w.T, m.T, v.T, "rs_finish_adamw_" + name)
        return tuple(t.T for t in res)
    if name == "conv_w_in":
        res = _rs_finish_adamw(pf, recv, w.T, m.T, v.T, "rs_finish_adamw_" + name)
        return tuple(t.T for t in res)
    return _rs_finish_adamw(pf, recv, w, m, v, "rs_finish_adamw_" + name)


def kernel(x, mem, positions, norm_g, mem_norm_g, w_mem_kv, attn_w_in, attn_w_out, conv_w_in, conv_w, conv_w_out, final_g, loss_target, m_norm_g, m_mem_norm_g, m_w_mem_kv, m_attn_w_in, m_attn_w_out, m_conv_w_in, m_conv_w, m_conv_w_out, m_final_g, v_norm_g, v_mem_norm_g, v_w_mem_kv, v_attn_w_in, v_attn_w_out, v_conv_w_in, v_conv_w, v_conv_w_out, v_final_g):
    px, py, pc = _position()
    me = 4 * px + 2 * py + pc
    ck_arr = jnp.stack([pc, 2 * px + py]).astype(jnp.int32)
    x, mem, pos, tgt = x[0], mem[0], positions[0], loss_target[0]

    wg_in0 = _all_gather_relay(attn_w_in[0].astype(BF16), "gather_w_in0")
    late = [attn_w_out[0].astype(BF16), conv_w_in[0].astype(BF16), conv_w_out[0].astype(BF16),
            w_mem_kv.astype(BF16).reshape(-1, w_mem_kv.shape[2]), jnp.pad(conv_w[0], ((0, 5), (0, 0)))]
    lands = [lax.dynamic_update_slice(lax.empty((N_DEV,) + t.shape, t.dtype), t[None], (me, 0, 0)) for t in late]
    late_weights, late_token = _split_start(late, lands, _plan_gather_own, 4, wg_in0, "gather_late_start")

    tabs = _rope_tables(pos)
    g0, g1 = _after(norm_g[0:1], late_token), norm_g[1:2]

    hn0, hn0_t, qs, ks, vs, tabs_v, qm0, z0 = _inproj_attn(x, g0, wg_in0, tabs)
    os_, ls_ = [], []
    for j, d in enumerate(DILATIONS):
        if j == 2:
            _, lands = _split_wait(late_weights, _plan_gather_own, ls_[1], "gather_late_wait")
            late_weights, late_token = _split_start([], lands, _plan_gather_pass, 3, ls_[1], "gather_late_pass_start")
        o, l = _attn_fwd(qs[j], ks[j], vs[j], d, late_token)
        os_.append(o)
        ls_.append(l)

    _, gathered = _split_wait(late_weights, _plan_gather_pass, ls_[2], "gather_late_pass_wait")
    wg_out0, wg_in1, wg_out1, wg_kv, cw_all = gathered
    w_out1 = wg_out1.reshape(-1, wg_out1.shape[2])
    n_kv = w_mem_kv.shape[1]
    w_kv = wg_kv.reshape(N_DEV, 2, n_kv, -1).transpose(1, 0, 2, 3).reshape(2, N_DEV * n_kv, -1)
    cw = cw_all[:, 0:3].transpose(1, 0, 2).reshape(3, -1)
    kv = _memkv_fwd(mem, mem_norm_g, w_kv)
    y0, y0_t, mo0, h1 = _post_attn(os_, ls_, qm0, kv[0], z0, x, wg_out0)

    hn1, bg, cg, u, qm1, z1 = _inproj_conv(h1, g1, wg_in1)
    y1, y1_t, mo1, dh2, dh2b, loss_acc, d_final_g = _post_conv_loss(
        bg, cg, u, qm1, kv[1], z1, h1, w_out1, cw, final_g.reshape(1, -1), tgt)

    d_w_out1 = _wgrad_rows(y1_t, dh2b, "wgrad_out1")
    dz1, dbg, dconv, dqm1, dkv1 = _bwd_post_conv(dh2b, w_out1, bg, cg, u, z1, qm1, kv[1], mo1, cw)
    dcg, du, dcw = _bwd_conv(dconv, cg, u, cw)
    dh1, dh1b, dg1, dproj1_t = _dgrad_norm([(dbg, 1), (dcg, 1), (du, 1), (dqm1, 1), (dz1, 1)], wg_in1, h1, g1, dh2,
                                           "dgrad_norm_conv")
    d_w_in1 = _wgrad_shards_t(dproj1_t, hn1, wg_in1.shape[2], "wgrad_in1")

    d_w_out0 = _wgrad_cols(y0_t, dh1b, wg_out0.shape[2], "wgrad_out0")

    names1 = ["conv_w_in", "conv_w_out", "attn_w_out"]
    grads1 = [d_w_in1, d_w_out1, d_w_out0]
    started, token = _split_start(grads1, [lax.empty((4,) + g.shape[1:], g.dtype) for g in grads1],
                                  _plan_to_sibling, 4, dg1, "rs1_sibling_start")

    gw = GROUP_WIDTH
    ones = (jnp.arange(gw)[:, None] // HEAD_DIM == jnp.arange(gw)[None, :] // HEAD_DIM).astype(BF16)
    ones = _after(ones, token)
    res = _bwd_post_attn(dh1b, wg_out0, z0, os_, ls_, qm0, kv[0], mo0, ones)
    dz0, dos, dls, dqm0, dkv0 = res[0], res[1:4], res[4:7], res[7], res[8]

    grads1, from_sibling = _split_wait(started, _plan_to_sibling, dz0, "rs1_sibling_wait")
    parts1 = [_rs_add_sibling(g, r, ck_arr, "rs_add_sibling_" + n) for g, r, n in zip(grads1, from_sibling, names1)]
    pbs1 = [pb for _, pb in parts1]
    started, token = _split_start(pbs1, [lax.empty((3,) + p.shape[1:], p.dtype) for p in pbs1],
                                  _plan_to_chips, 3, dg1, "rs1_chips_start")

    dqs, dks, dvs = [], [], []
    for j, d in enumerate(DILATIONS):
        dq, dk, dv = _attn_bwd(qs[j], ks[j], vs[j], ls_[j], dos[j], dls[j], tabs_v[j], d, token)
        dqs.append((dq, d))
        dks.append((dk, d))
        dvs.append((dv, d))
    d_w_kv, d_mem_g = _memkv_bwd(jnp.stack([dkv0, dkv1]), w_kv, mem, mem_norm_g)
    dproj0 = _assemble_dproj(dqs + dks + dvs + [(dqm0, 1), (dz0, 1)], wg_in0.shape[2], "assemble_dproj_attn")
    d_w_in0 = _wgrad_shards(hn0_t, dproj0, "wgrad_in0")

    names0 = ["attn_w_in", "w_mem_kv"]
    grads0 = [d_w_in0, d_w_kv]
    started0, token0 = _split_start(grads0, [lax.empty((4,) + g.shape[1:], g.dtype) for g in grads0],
                                    _plan_to_sibling, 4, dg1, "rs0_sibling_start")
    _, from_chips1 = _split_wait(started, _plan_to_chips, token0, "rs1_chips_wait")
    shard = dict(attn_w_in=(attn_w_in[0], m_attn_w_in[0], v_attn_w_in[0]),
                 attn_w_out=(attn_w_out[0], m_attn_w_out[0], v_attn_w_out[0]),
                 conv_w_in=(conv_w_in[0], m_conv_w_in[0], v_conv_w_in[0]),
                 conv_w_out=(conv_w_out[0], m_conv_w_out[0], v_conv_w_out[0]),
                 w_mem_kv=tuple(t.reshape(-1, t.shape[2]) for t in (w_mem_kv, m_w_mem_kv, v_w_mem_kv)))
    big = {}
    for n, (pf, _), r in zip(names1, parts1, from_chips1):
        big[n] = _finish(n, pf, r, *shard[n])

    grads0, from_sibling = _split_wait(started0, _plan_to_sibling, big["conv_w_out"][1], "rs0_sibling_wait")
    parts0 = [_rs_add_sibling(g, r, ck_arr, "rs_add_sibling_" + n) for g, r, n in zip(grads0, from_sibling, names0)]
    pbs0 = [pb for _, pb in parts0]
    started0, token0 = _split_start(pbs0, [lax.empty((3,) + p.shape[1:], p.dtype) for p in pbs0],
                                    _plan_to_chips, 3, dg1, "rs0_chips_start")
    dx, dg0 = _dgrad_norm_dm(dproj0, wg_in0, x, g0, dh1, token0, "dgrad_norm_attn")

    small_part = jnp.concatenate([dg0, dg1, d_mem_g.reshape(2, -1), d_final_g, dcw[0:3]], axis=0)
    small_part = jnp.concatenate([small_part, jnp.broadcast_to(loss_acc[0, 0], small_part.shape)], axis=0)
    small = _sum_devices(_all_gather([small_part], dg0, "gather_small_grads")[0])
    loss = small[8, 0]
    g_conv_w = lax.dynamic_slice(small[5:8], (0, me * LANES), (3, LANES))[None]
    small_g = dict(norm_g=small[0:2], mem_norm_g=small[2:4], conv_w=g_conv_w, final_g=small[4])
    small_w = dict(norm_g=(norm_g, m_norm_g, v_norm_g), mem_norm_g=(mem_norm_g, m_mem_norm_g, v_mem_norm_g),
                   conv_w=(conv_w, m_conv_w, v_conv_w), final_g=(final_g, m_final_g, v_final_g))
    for n, (w, m, v) in small_w.items():
        big[n] = (small_g[n],) + _adamw(w, small_g[n], m, v, "adamw_" + n)

    _, from_chips0 = _split_wait(started0, _plan_to_chips, big["final_g"][1], "rs0_chips_wait")
    for n, (pf, _), r in zip(names0, parts0, from_chips0):
        big[n] = _finish(n, pf, r, *shard[n])
    for n in ("attn_w_in", "attn_w_out", "conv_w_in", "conv_w_out"):
        big[n] = tuple(t[None] for t in big[n])
    big["w_mem_kv"] = tuple(t.reshape(w_mem_kv.shape) for t in big["w_mem_kv"])

    order = ["norm_g", "mem_norm_g", "w_mem_kv", "attn_w_in", "attn_w_out", "conv_w_in", "conv_w", "conv_w_out", "final_g"]
    return (loss, dx[None], *[big[n][0] for n in order], *[big[n][1] for n in order],
            *[big[n][2] for n in order], *[big[n][3] for n in order])
```

```python
import jax
import jax.numpy as jnp
from jax import lax
from jax.experimental import pallas as pl
from jax.experimental.pallas import tpu as pltpu

F32 = jnp.float32
BF16 = jnp.bfloat16

N_DEV = 8
D_MODEL = 1024
HEAD_DIM = 64
ROT_DIM = HEAD_DIM // 4
ROPE_THETA = 500000.0
DILATIONS = (1, 4, 16)
HEADS_PER_GROUP = 8
GROUP_WIDTH = HEADS_PER_GROUP * HEAD_DIM
BLOCK = 128
N_MEM = 256
MEM_HEADS = 4
MEM_WIDTH = MEM_HEADS * HEAD_DIM
CONV_WIDTH = D_MODEL
EPS = 1e-6
SCALE = HEAD_DIM ** -0.5
NEG = -1e30

ADAM_LR = 0.001
ADAM_B1 = 0.9
ADAM_B2 = 0.999
ADAM_EPS = 1e-08
ADAM_WD = 0.01
ADAM_STEP = 10

ROW_TILE = 256
WGRAD_SHARDS = 4
LANES = 128
MESH = pl.DeviceIdType.MESH
ANY = pl.BlockSpec(memory_space=pl.ANY)


def _pallas_call(body, **kw):
    call = pl.pallas_call(body, **kw)

    def run(*args):
        pinned = [pltpu.with_memory_space_constraint(a, pltpu.HBM) if jnp.issubdtype(a.dtype, jnp.floating) else a
                  for a in args]
        return call(*pinned)

    return run


def _dot(a, b):
    return lax.dot_general(a, b, (((1,), (0,)), ((), ())), preferred_element_type=F32)


def _dot_nt(a, b):
    return lax.dot_general(a, b, (((1,), (1,)), ((), ())), preferred_element_type=F32)


def _dot_tn(a, b):
    return lax.dot_general(a, b, (((0,), (0,)), ((), ())), preferred_element_type=F32)


def _params(n_grid, vmem_mb=48):
    return pltpu.CompilerParams(dimension_semantics=("arbitrary",) * n_grid, vmem_limit_bytes=vmem_mb << 20)


def _rows(width, tm=ROW_TILE):
    return pl.BlockSpec((tm, width), lambda i: (i, 0))


def _view_rows(width, d, tm=ROW_TILE):
    return pl.BlockSpec((tm // d, d * width), lambda i: (i, 0))


def _whole(shape):
    return pl.BlockSpec(shape, lambda *_: (0,) * len(shape))


def _resident(shape):
    return pl.BlockSpec(shape, lambda *_: (0,) * len(shape), pipeline_mode=pl.Buffered(1))


def _sds(shape, dtype):
    return pltpu.HBM(shape, dtype)


def _plain(shape, dtype):
    return jax.ShapeDtypeStruct(shape, dtype)


def _silu_parts(z):
    sg = jax.nn.sigmoid(z)
    return z * sg, sg * (1.0 + z * (1.0 - sg))


def _to_view(scr, val, out_ref, d):
    tm, w = val.shape
    if d == 1:
        out_ref[...] = val.astype(out_ref.dtype)
        return
    for cb in range(w // LANES):
        scr[cb] = val[:, cb * LANES:(cb + 1) * LANES]
    for r in range(d):
        for cb in range(w // LANES):
            lo = r * w + cb * LANES
            out_ref[:, lo:lo + LANES] = scr[cb, pl.ds(r, tm // d, stride=d), :].astype(out_ref.dtype)


def _from_view(scr, in_ref, d):
    if d == 1:
        return in_ref[...].astype(F32)
    nc, tm, _ = scr.shape
    w = nc * LANES
    for r in range(d):
        for cb in range(nc):
            lo = r * w + cb * LANES
            scr[cb, pl.ds(r, tm // d, stride=d), :] = in_ref[:, lo:lo + LANES].astype(F32)
    return jnp.concatenate([scr[cb] for cb in range(nc)], axis=1)


def _view_scratch(width, tm=ROW_TILE):
    return pltpu.VMEM((width // LANES, tm, LANES), F32)


def _rope_tables(pos):
    half = ROT_DIM // 2
    inv_freq = ROPE_THETA ** (-jnp.arange(half, dtype=F32) * (2.0 / ROT_DIM))
    ang = pos.astype(F32)[:, None] * inv_freq
    cos, sin = jnp.cos(ang), jnp.sin(ang)
    s = pos.shape[0]
    z8 = jnp.zeros((s, half), F32)
    rest = HEAD_DIM - ROT_DIM
    cosf = jnp.concatenate([cos, cos, jnp.ones((s, rest), F32)], axis=1)
    sa = jnp.concatenate([-sin, z8, jnp.zeros((s, rest), F32)], axis=1)
    sb = jnp.concatenate([z8, sin, jnp.zeros((s, rest), F32)], axis=1)
    return tuple(jnp.tile(t, (1, LANES // HEAD_DIM)) for t in (cosf, sa, sb))


def _rope_fwd(t, cv, sav, sbv):
    w = t.shape[1]
    return t * cv + pltpu.roll(t, w - ROT_DIM // 2, 1) * sav + pltpu.roll(t, ROT_DIM // 2, 1) * sbv


def _rope_bwd(g, cv, sav, sbv):
    w = g.shape[1]
    return g * cv + pltpu.roll(g * sav, ROT_DIM // 2, 1) + pltpu.roll(g * sbv, w - ROT_DIM // 2, 1)


def _joined_columns(wg_ref):
    assert wg_ref.shape[2] % LANES == 0
    return jnp.concatenate([wg_ref[j] for j in range(N_DEV)], axis=1)


def _join_once(wg_ref, w_scr):
    c = wg_ref.shape[2]

    @pl.when(pl.program_id(0) == 0)
    def _():
        for j in range(N_DEV):
            w_scr[:, j * c:(j + 1) * c] = wg_ref[j]


def _inproj_attn(x, g, wg, tabs):
    s, d_model = x.shape
    gw = GROUP_WIDTH
    n = N_DEV * wg.shape[2]
    nz = n - 9 * gw - MEM_WIDTH
    reps = gw // LANES
    tm = ROW_TILE

    def body(x_ref, g_ref, w_ref, c_ref, sa_ref, sb_ref, hn_ref, *rest):
        outs, (wj, scr, tscr) = rest[:-3], rest[-3:]
        q_refs, k_refs, v_refs, t_refs, qm_ref, z_ref = outs[0:3], outs[3:6], outs[6:9], outs[9:18], outs[18], outs[19]
        _join_once(w_ref, wj)
        xb = x_ref[...]
        r = lax.rsqrt(jnp.mean(xb * xb, axis=-1, keepdims=True) + EPS)
        hn = ((xb * r) * g_ref[...]).astype(BF16)
        hn_ref[...] = hn
        proj = lambda lo, hi: _dot(hn, wj[:, lo:hi])
        tab = (c_ref[...], sa_ref[...], sb_ref[...])
        cv, sav, sbv = [jnp.tile(t, (1, reps)) for t in tab]
        for j, d in enumerate(DILATIONS):
            tq = _rope_fwd(proj(j * gw, (j + 1) * gw), cv, sav, sbv)
            _to_view(scr, tq * SCALE, q_refs[j], d)
            tk = _rope_fwd(proj((3 + j) * gw, (4 + j) * gw), cv, sav, sbv)
            _to_view(scr, tk, k_refs[j], d)
            _to_view(scr, proj((6 + j) * gw, (7 + j) * gw), v_refs[j], d)
            for i in range(3):
                _to_view(tscr, tab[i], t_refs[3 * j + i], d)
        qm_ref[...] = proj(9 * gw, 9 * gw + MEM_WIDTH).astype(BF16)
        z_ref[...] = proj(9 * gw + MEM_WIDTH, n)

    views = [_sds((s // d, d * gw), BF16) for d in DILATIONS]
    tviews = [_sds((s // d, d * LANES), F32) for d in DILATIONS for _ in range(3)]
    out_shape = [_sds((s, d_model), BF16)] + views * 3 + tviews + [_sds((s, MEM_WIDTH), BF16), _sds((s, nz), F32)]
    vspecs = [_view_rows(gw, d, tm) for d in DILATIONS]
    tspecs = [_view_rows(LANES, d, tm) for d in DILATIONS for _ in range(3)]
    out_specs = [_rows(d_model, tm)] + vspecs * 3 + tspecs + [_rows(MEM_WIDTH, tm), _rows(nz, tm)]
    res = _pallas_call(
        body, name="inproj_attn", grid=(s // tm,), out_shape=out_shape,
        in_specs=[_rows(d_model, tm), _whole((1, d_model)), _resident(wg.shape)] + [_rows(LANES, tm)] * 3,
        out_specs=out_specs,
        scratch_shapes=[pltpu.VMEM((d_model, n), BF16), _view_scratch(gw, tm), _view_scratch(LANES, tm)],
        compiler_params=_params(1, 60),
    )(x, g, wg, *tabs)
    tabs_v = [res[10 + 3 * j:13 + 3 * j] for j in range(3)]
    return res[0], res[1:4], res[4:7], res[7:10], tabs_v, res[19], res[20]


def _band_mask(n_keys):
    qi = lax.broadcasted_iota(jnp.int32, (BLOCK, n_keys), 0)
    kj = lax.broadcasted_iota(jnp.int32, (BLOCK, n_keys), 1)
    if n_keys == BLOCK:
        return kj <= qi
    return jnp.logical_or(jnp.logical_and(kj < BLOCK, kj >= qi), jnp.logical_and(kj >= BLOCK, (kj - BLOCK) <= qi))


def _low_head_lanes():
    return lax.broadcasted_iota(jnp.int32, (1, LANES), 1) < HEAD_DIM


def _split_pair(t, low):
    zero = jnp.zeros_like(t)
    return jnp.where(low, t, zero), jnp.where(low, zero, t)


def _pair_specs(d, nb, w):
    if nb == 1:
        return pl.BlockSpec((BLOCK, 2 * w), lambda n: (0, n)), None
    half = nb // 2
    two = pl.BlockSpec((2 * BLOCK, w), lambda n: (n % half, n // half))
    before = pl.BlockSpec((BLOCK, w), lambda n: (jnp.maximum(2 * (n % half) - 1, 0), n // half))
    return two, before


def _head_tiles(w, col0):
    return ([slice(p * LANES, (p + 1) * LANES) for p in range(w // LANES)],
            [slice(col0 + p * LANES, col0 + (p + 1) * LANES) for p in range(w // LANES)])


def _attend_fwd(q_ref, o_ref, lse_ref, rows, col0, kk, vv):
    w = kk.shape[1]
    valid = _band_mask(kk.shape[0])
    low = _low_head_lanes()
    pairs, qcols = _head_tiles(w, col0)
    qs_ = [h for qc in qcols for h in _split_pair(q_ref[rows, qc], low)]
    k2s = [kk[:, pr] for pr in pairs for _ in range(2)]
    scs = [jnp.where(valid, _dot_nt(qh, k2), NEG) for qh, k2 in zip(qs_, k2s)]
    ms = [jnp.max(sc, axis=-1, keepdims=True) for sc in scs]
    ps = [jnp.exp(sc - m) for sc, m in zip(scs, ms)]
    ls = [jnp.sum(p, axis=-1, keepdims=True) for p in ps]
    pns = [(p * (1.0 / l)).astype(BF16) for p, l in zip(ps, ls)]
    for i, (pr, qc) in enumerate(zip(pairs, qcols)):
        v2 = vv[:, pr]
        a, b = 2 * i, 2 * i + 1
        o_ref[rows, qc] = jnp.where(low, _dot(pns[a], v2), _dot(pns[b], v2))
        lse_ref[rows, qc] = jnp.where(low, ms[a] + jnp.log(ls[a]), ms[b] + jnp.log(ls[b]))


TOP, BOTTOM = slice(0, BLOCK), slice(BLOCK, 2 * BLOCK)


def _attn_fwd(q, k, v, d, token):
    ln, dw = q.shape
    w = dw // d
    nb = ln // BLOCK
    two, before = _pair_specs(d, nb, w)

    def body_streams(_, q_ref, kc_ref, vc_ref, o_ref, lse_ref):
        for sb in range(2):
            cols = slice(sb * w, (sb + 1) * w)
            _attend_fwd(q_ref, o_ref, lse_ref, TOP, sb * w, kc_ref[:, cols], vc_ref[:, cols])

    def body_blocks(_, q_ref, kp_ref, kc_ref, vp_ref, vc_ref, o_ref, lse_ref):
        first = pl.program_id(0) % (nb // 2) == 0
        pl.when(first)(lambda: _attend_fwd(q_ref, o_ref, lse_ref, TOP, 0, kc_ref[TOP, :], vc_ref[TOP, :]))
        pl.when(jnp.logical_not(first))(lambda: _attend_fwd(
            q_ref, o_ref, lse_ref, TOP, 0, jnp.concatenate([kp_ref[...], kc_ref[TOP, :]], axis=0),
            jnp.concatenate([vp_ref[...], vc_ref[TOP, :]], axis=0)))
        _attend_fwd(q_ref, o_ref, lse_ref, BOTTOM, 0, kc_ref[...], vc_ref[...])

    if nb == 1:
        body, in_specs, args = body_streams, [ANY, two, two, two], (token, q, k, v)
    else:
        body, in_specs, args = body_blocks, [ANY, two, before, two, before, two], (token, q, k, k, v, v)
    return _pallas_call(
        body, name=f"attn_fwd_d{d}", grid=(d * nb // 2,), out_shape=[_sds((ln, dw), F32)] * 2,
        in_specs=in_specs, out_specs=[two, two], compiler_params=_params(1, 32),
    )(*args)


def _memkv_fwd(mem, g, w):
    n_layers = w.shape[0]

    def body(mem_ref, g_ref, w_ref, kv_ref):
        mb = mem_ref[...]
        r = lax.rsqrt(jnp.mean(mb * mb, axis=-1, keepdims=True) + EPS)
        mn = ((mb * r) * g_ref[...]).astype(BF16)
        kv_ref[...] = _dot(mn, w_ref[...]).astype(BF16)

    return _pallas_call(
        body, name="memkv_fwd", grid=(n_layers,),
        out_shape=_plain((n_layers, N_MEM, 2 * MEM_WIDTH), BF16),
        in_specs=[_whole(mem.shape), pl.BlockSpec((None, 1, D_MODEL), lambda l: (l, 0, 0)),
                  pl.BlockSpec((None, D_MODEL, 2 * MEM_WIDTH), lambda l: (l, 0, 0))],
        out_specs=pl.BlockSpec((None, N_MEM, 2 * MEM_WIDTH), lambda l: (l, 0, 0)),
        compiler_params=_params(1, 32),
    )(mem, g.reshape(n_layers, 1, D_MODEL), w)


def _mix_groups(os_, ls_):
    mx = jnp.maximum(jnp.maximum(ls_[0], ls_[1]), ls_[2])
    es = [jnp.exp(t - mx) for t in ls_]
    inv = 1.0 / (es[0] + es[1] + es[2])
    ws = [e * inv for e in es]
    mix = ws[0] * os_[0] + ws[1] * os_[1] + ws[2] * os_[2]
    return ws, mix


MEM_PAIRS = [slice(p * LANES, (p + 1) * LANES) for p in range(MEM_WIDTH // LANES)]


def _mem_probs(qhs, k2s):
    scs = [_dot_nt(qh, k2) * SCALE for qh, k2 in zip(qhs, k2s)]
    es = [jnp.exp(sc - jnp.max(sc, axis=-1, keepdims=True)) for sc in scs]
    return [e * (1.0 / jnp.sum(e, axis=-1, keepdims=True)) for e in es]


def _mem_attn_into(qm, kv_ref, mo_ref):
    low = _low_head_lanes()
    qhs = [h for pr in MEM_PAIRS for h in _split_pair(qm[:, pr], low)]
    k2s = [kv_ref[:, pr] for pr in MEM_PAIRS for _ in range(2)]
    ps = [p.astype(BF16) for p in _mem_probs(qhs, k2s)]
    for i, pr in enumerate(MEM_PAIRS):
        v2 = kv_ref[:, MEM_WIDTH + i * LANES:MEM_WIDTH + (i + 1) * LANES]
        mo_ref[:, pr] = jnp.where(low, _dot(ps[2 * i], v2), _dot(ps[2 * i + 1], v2))


def _mem_attn_bwd(qm, kv_ref, dmem, dqm_ref, dkv_ref):
    low = _low_head_lanes()
    dmb = dmem.astype(BF16)
    vps = [slice(MEM_WIDTH + i * LANES, MEM_WIDTH + (i + 1) * LANES) for i in range(len(MEM_PAIRS))]
    qhs = [h for pr in MEM_PAIRS for h in _split_pair(qm[:, pr], low)]
    dhs = [h for pr in MEM_PAIRS for h in _split_pair(dmb[:, pr], low)]
    k2s = [kv_ref[:, pr] for pr in MEM_PAIRS for _ in range(2)]
    v2s = [kv_ref[:, vp] for vp in vps for _ in range(2)]
    ps = _mem_probs(qhs, k2s)
    dps = [_dot_nt(dh, v2) for dh, v2 in zip(dhs, v2s)]
    dss = [(p * (dp - jnp.sum(dp * p, axis=-1, keepdims=True)) * SCALE).astype(BF16) for p, dp in zip(ps, dps)]
    pbs = [p.astype(BF16) for p in ps]
    for i, (pr, vp) in enumerate(zip(MEM_PAIRS, vps)):
        a, b = 2 * i, 2 * i + 1
        dqm_ref[:, pr] = jnp.where(low, _dot(dss[a], k2s[a]), _dot(dss[b], k2s[b])).astype(BF16)
        dkv_ref[:, pr] += _dot_tn(dss[a], qhs[a]) + _dot_tn(dss[b], qhs[b])
        dkv_ref[:, vp] += _dot_tn(pbs[a], dhs[a]) + _dot_tn(pbs[b], dhs[b])


def _post_attn(os_, ls_, qm, kv, z, x, wg_out):
    s, d_model = x.shape
    gw = GROUP_WIDTH
    nb = gw + MEM_WIDTH
    tm = ROW_TILE

    def body(o0, o1, o2, l0, l1, l2, qm_ref, kv_ref, z_ref, x_ref, w_ref, y_ref, yt_ref, mo_ref, h_ref, s0, s1):
        ov, lv = [], []
        for o_ref, l_ref, d in zip((o0, o1, o2), (l0, l1, l2), DILATIONS):
            ov.append(_from_view(s0, o_ref, d))
            lv.append(_from_view(s1, l_ref, d))
        _, mix = _mix_groups(ov, lv)
        _mem_attn_into(qm_ref[...], kv_ref, mo_ref)
        sz, _ = _silu_parts(z_ref[...])
        y_ref[:, :gw] = (mix * sz[:, :gw]).astype(BF16)
        y_ref[:, gw:] = (mo_ref[...] * sz[:, gw:]).astype(BF16)
        y = y_ref[...]
        yt_ref[...] = y.T
        h_ref[...] = x_ref[...] + _dot(y, _joined_columns(w_ref))

    vspecs = [_view_rows(gw, d) for d in DILATIONS]
    return _pallas_call(
        body, name="post_attn", grid=(s // tm,),
        out_shape=[_sds((s, nb), BF16), _sds((nb, s), BF16), _sds((s, MEM_WIDTH), F32), _sds((s, d_model), F32)],
        in_specs=vspecs * 2 + [_rows(MEM_WIDTH), _whole(kv.shape), _rows(nb), _rows(d_model), _whole(wg_out.shape)],
        out_specs=[_rows(nb), pl.BlockSpec((nb, tm), lambda i: (0, i)), _rows(MEM_WIDTH), _rows(d_model)],
        scratch_shapes=[_view_scratch(gw), _view_scratch(gw)],
        compiler_params=_params(1, 40),
    )(*os_, *ls_, qm, kv, z, x, wg_out)


def _inproj_conv(x, g, wg):
    s, d_model = x.shape
    c = CONV_WIDTH
    n = N_DEV * wg.shape[2]
    nz = n - 3 * c - MEM_WIDTH
    tm = ROW_TILE

    def body(x_ref, g_ref, w_ref, hn_ref, bg_ref, cg_ref, u_ref, qm_ref, z_ref, wj):
        _join_once(w_ref, wj)
        xb = x_ref[...]
        r = lax.rsqrt(jnp.mean(xb * xb, axis=-1, keepdims=True) + EPS)
        hn = ((xb * r) * g_ref[...]).astype(BF16)
        hn_ref[...] = hn
        bg_ref[...] = _dot(hn, wj[:, 0:c])
        cg_ref[...] = _dot(hn, wj[:, c:2 * c])
        u_ref[...] = _dot(hn, wj[:, 2 * c:3 * c])
        qm_ref[...] = _dot(hn, wj[:, 3 * c:3 * c + MEM_WIDTH]).astype(BF16)
        z_ref[...] = _dot(hn, wj[:, 3 * c + MEM_WIDTH:])

    return _pallas_call(
        body, name="inproj_conv", grid=(s // tm,),
        out_shape=[_sds((s, d_model), BF16)] + [_sds((s, c), F32)] * 3 + [_sds((s, MEM_WIDTH), BF16), _sds((s, nz), F32)],
        in_specs=[_rows(d_model), _whole((1, d_model)), _resident(wg.shape)],
        out_specs=[_rows(d_model)] + [_rows(c)] * 3 + [_rows(MEM_WIDTH), _rows(nz)],
        scratch_shapes=[pltpu.VMEM((d_model, n), BF16)],
        compiler_params=_params(1, 60),
    )(x, g, wg)


HALO = 8


def _halo_before(width, tm=ROW_TILE):
    return pl.BlockSpec((HALO, width), lambda i: (jnp.maximum(i * (tm // HALO) - 1, 0), 0))


def _halo_after(width, n_rows, tm=ROW_TILE):
    return pl.BlockSpec((HALO, width), lambda i: (jnp.minimum((i + 1) * (tm // HALO), n_rows // HALO - 1), 0))


def _conv_taps(cg_ref, u_ref, cgh_ref, uh_ref, i):
    a = cg_ref[...] * u_ref[...]
    ah = jnp.where(i > 0, cgh_ref[...] * uh_ref[...], 0.0)
    row = lax.broadcasted_iota(jnp.int32, a.shape, 0)
    a1 = jnp.where(row == 0, ah[HALO - 1:HALO], pltpu.roll(a, 1, 0))
    a2 = jnp.where(row == 0, ah[HALO - 2:HALO - 1], jnp.where(row == 1, ah[HALO - 1:HALO], pltpu.roll(a, 2, 0)))
    return a, a1, a2


def _post_conv_loss(bg, cg, u, qm, kv, z, h1, w_out, cw, gf, tgt):
    s, d = h1.shape
    c = CONV_WIDTH
    nb = c + MEM_WIDTH
    tm = ROW_TILE

    def body(bg_ref, cg_ref, u_ref, cgh_ref, uh_ref, qm_ref, kv_ref, z_ref, h_ref, w_ref, cw_ref, gf_ref, t_ref,
             y_ref, yt_ref, mo_ref, dh_ref, dhb_ref, loss_ref, dgf_ref):
        i = pl.program_id(0)
        a, a1, a2 = _conv_taps(cg_ref, u_ref, cgh_ref, uh_ref, i)
        conv = cw_ref[0:1, :] * a2 + cw_ref[1:2, :] * a1 + cw_ref[2:3, :] * a
        mix = bg_ref[...] * conv
        _mem_attn_into(qm_ref[...], kv_ref, mo_ref)
        sz, _ = _silu_parts(z_ref[...])
        y_ref[:, :c] = (mix * sz[:, :c]).astype(BF16)
        y_ref[:, c:] = (mo_ref[...] * sz[:, c:]).astype(BF16)
        y = y_ref[...]
        yt_ref[...] = y.T
        h2 = h_ref[...] + _dot(y, w_ref[...])
        r = lax.rsqrt(jnp.mean(h2 * h2, axis=-1, keepdims=True) + EPS)
        nh = h2 * r
        gfv = gf_ref[...]
        diff = nh * gfv - t_ref[...]
        dout = diff * (1.0 / d)
        dn = dout * gfv
        dh2 = r * dn - h2 * ((r * r * r) * jnp.mean(dn * h2, axis=-1, keepdims=True))
        dh_ref[...] = dh2
        dhb_ref[...] = dh2.astype(BF16)

        @pl.when(i == 0)
        def _():
            loss_ref[...] = jnp.zeros_like(loss_ref)
            dgf_ref[...] = jnp.zeros_like(dgf_ref)

        loss_ref[...] += 0.5 * jnp.sum(jnp.mean(diff * diff, axis=-1, keepdims=True))
        dgf_ref[...] += jnp.sum(dout * nh, axis=0, keepdims=True)

    return _pallas_call(
        body, name="post_conv_loss", grid=(s // tm,),
        out_shape=[_sds((s, nb), BF16), _sds((nb, s), BF16), _sds((s, MEM_WIDTH), F32), _sds((s, d), F32),
                   _sds((s, d), BF16), _plain((8, LANES), F32), _plain((1, d), F32)],
        in_specs=[_rows(c)] * 3 + [_halo_before(c)] * 2 + [_rows(MEM_WIDTH), _whole(kv.shape), _rows(nb), _rows(d),
                  _whole(w_out.shape), _whole(cw.shape), _whole((1, d)), _rows(d)],
        out_specs=[_rows(nb), pl.BlockSpec((nb, tm), lambda i: (0, i)), _rows(MEM_WIDTH), _rows(d), _rows(d),
                   _whole((8, LANES)), _whole((1, d))],
        compiler_params=_params(1, 48),
    )(bg, cg, u, cg, u, qm, kv, z, h1, w_out, cw, gf, tgt)


def _bwd_post_conv(dhb, w_out, bg, cg, u, z, qm, kv, mo, cw):
    s = dhb.shape[0]
    c = CONV_WIDTH
    nb = c + MEM_WIDTH

    def body(dh_ref, w_ref, bg_ref, cg_ref, u_ref, cgh_ref, uh_ref, z_ref, qm_ref, kv_ref, mo_ref, cw_ref,
             dz_ref, dbg_ref, dc_ref, dqm_ref, dkv_ref):
        i = pl.program_id(0)

        @pl.when(i == 0)
        def _():
            dkv_ref[...] = jnp.zeros_like(dkv_ref)

        dy = _dot_nt(dh_ref[...], w_ref[...])
        sz, dsz = _silu_parts(z_ref[...])
        a, a1, a2 = _conv_taps(cg_ref, u_ref, cgh_ref, uh_ref, i)
        conv = cw_ref[0:1, :] * a2 + cw_ref[1:2, :] * a1 + cw_ref[2:3, :] * a
        bgv = bg_ref[...]
        dz_ref[:, :c] = (dy[:, :c] * (bgv * conv) * dsz[:, :c]).astype(BF16)
        dz_ref[:, c:] = (dy[:, c:] * mo_ref[...] * dsz[:, c:]).astype(BF16)
        dbr = dy * sz
        dmix = dbr[:, :c]
        dbg_ref[...] = (dmix * conv).astype(BF16)
        dc_ref[...] = dmix * bgv
        _mem_attn_bwd(qm_ref[...], kv_ref, dbr[:, c:], dqm_ref, dkv_ref)

    return _pallas_call(
        body, name="bwd_post_conv", grid=(s // ROW_TILE,),
        out_shape=[_sds((s, nb), BF16), _sds((s, c), BF16), _sds((s, c), F32), _sds((s, MEM_WIDTH), BF16),
                   _plain(kv.shape, F32)],
        in_specs=[_rows(D_MODEL), _whole(w_out.shape)] + [_rows(c)] * 3 + [_halo_before(c)] * 2
                 + [_rows(nb), _rows(MEM_WIDTH), _whole(kv.shape), _rows(MEM_WIDTH), _whole(cw.shape)],
        out_specs=[_rows(nb), _rows(c), _rows(c), _rows(MEM_WIDTH), _whole(kv.shape)],
        compiler_params=_params(1, 48),
    )(dhb, w_out, bg, cg, u, cg, u, z, qm, kv, mo, cw)


def _bwd_conv(dconv, cg, u, cw):
    s, c = dconv.shape
    tm = ROW_TILE
    last = s // tm - 1

    def body(dc_ref, dcn_ref, cg_ref, u_ref, cgh_ref, uh_ref, cw_ref, dcg_ref, du_ref, dcw_ref):
        i = pl.program_id(0)

        @pl.when(i == 0)
        def _():
            dcw_ref[...] = jnp.zeros_like(dcw_ref)

        dc = dc_ref[...]
        dcn = jnp.where(i < last, dcn_ref[...], 0.0)
        row = lax.broadcasted_iota(jnp.int32, dc.shape, 0)
        d1 = jnp.where(row == tm - 1, dcn[0:1], pltpu.roll(dc, tm - 1, 0))
        d2 = jnp.where(row == tm - 1, dcn[1:2], jnp.where(row == tm - 2, dcn[0:1], pltpu.roll(dc, tm - 2, 0)))
        da = cw_ref[2:3, :] * dc + cw_ref[1:2, :] * d1 + cw_ref[0:1, :] * d2
        a, a1, a2 = _conv_taps(cg_ref, u_ref, cgh_ref, uh_ref, i)
        dcg_ref[...] = (da * u_ref[...]).astype(BF16)
        du_ref[...] = (da * cg_ref[...]).astype(BF16)
        dcw_ref[0:1, :] += jnp.sum(dc * a2, axis=0, keepdims=True)
        dcw_ref[1:2, :] += jnp.sum(dc * a1, axis=0, keepdims=True)
        dcw_ref[2:3, :] += jnp.sum(dc * a, axis=0, keepdims=True)

    return _pallas_call(
        body, name="bwd_conv", grid=(s // tm,),
        out_shape=[_sds((s, c), BF16), _sds((s, c), BF16), _plain((8, c), F32)],
        in_specs=[_rows(c), _halo_after(c, s), _rows(c), _rows(c), _halo_before(c), _halo_before(c), _whole(cw.shape)],
        out_specs=[_rows(c), _rows(c), _whole((8, c))], compiler_params=_params(1, 40),
    )(dconv, dconv, cg, u, cg, u, cw)


def _dgrad_norm(pieces, wg, h, g, dres, name):
    s, d_model = h.shape
    c = wg.shape[2]
    n = N_DEV * c
    tm = ROW_TILE
    widths = [p.shape[1] // d for p, d in pieces]
    assert sum(widths) == n
    n_p = len(pieces)

    def body(*refs):
        p_refs = refs[:n_p]
        w_ref, h_ref, g_ref, dr_ref, dh_ref, dhb_ref, dg_ref, dpt_ref, dp, scr, wj = refs[n_p:]
        _join_once(w_ref, wj)

        @pl.when(pl.program_id(0) == 0)
        def _():
            dg_ref[...] = jnp.zeros_like(dg_ref)

        off = 0
        for p_ref, (_, d), wd in zip(p_refs, pieces, widths):
            if d == 1:
                dp[:, off:off + wd] = p_ref[...]
            else:
                dp[:, off:off + wd] = _from_view(scr, p_ref, d).astype(BF16)
            off += wd
        dpt_ref[...] = dp[...].T
        dhn = _dot_nt(dp[...], wj[...])
        hb = h_ref[...]
        r = lax.rsqrt(jnp.mean(hb * hb, axis=-1, keepdims=True) + EPS)
        dg_ref[...] += jnp.sum(dhn * (hb * r), axis=0, keepdims=True)
        dn = dhn * g_ref[...]
        dh = dr_ref[...] + r * dn - hb * ((r * r * r) * jnp.mean(dn * hb, axis=-1, keepdims=True))
        dh_ref[...] = dh
        dhb_ref[...] = dh.astype(BF16)

    p_specs = [_view_rows(wd, d) for (_, d), wd in zip(pieces, widths)]
    return _pallas_call(
        body, name=name, grid=(s // tm,),
        out_shape=[_plain((s, d_model), F32), _sds((s, d_model), BF16), _plain((1, d_model), F32), _sds((n, s), BF16)],
        in_specs=p_specs + [_resident(wg.shape), _rows(d_model), _whole((1, d_model)), _rows(d_model)],
        out_specs=[_rows(d_model), _rows(d_model), _whole((1, d_model)), pl.BlockSpec((n, tm), lambda i: (0, i))],
        scratch_shapes=[pltpu.VMEM((tm, n), BF16), _view_scratch(GROUP_WIDTH), pltpu.VMEM((d_model, n), BF16)],
        compiler_params=_params(1, 60),
    )(*[p for p, _ in pieces], wg, h, g, dres)


def _assemble_dproj(pieces, n, name):
    tm = ROW_TILE
    widths = [p.shape[1] // d for p, d in pieces]
    assert sum(widths) == n
    s = pieces[0][0].shape[0] * pieces[0][1]
    n_p = len(pieces)

    def body(*refs):
        p_refs, (dp_ref, dpt_ref, scr) = refs[:n_p], refs[n_p:]
        off = 0
        for p_ref, (_, d), wd in zip(p_refs, pieces, widths):
            if d == 1:
                dp_ref[:, off:off + wd] = p_ref[...]
            else:
                dp_ref[:, off:off + wd] = _from_view(scr, p_ref, d).astype(BF16)
            off += wd
        dpt_ref[...] = dp_ref[...].T

    return _pallas_call(
        body, name=name, grid=(s // tm,), out_shape=[_sds((s, n), BF16), _sds((n, s), BF16)],
        in_specs=[_view_rows(wd, d) for (_, d), wd in zip(pieces, widths)],
        out_specs=[_rows(n), pl.BlockSpec((n, tm), lambda i: (0, i))],
        scratch_shapes=[_view_scratch(GROUP_WIDTH)],
        compiler_params=_params(1, 40),
    )(*[p for p, _ in pieces])


def _dgrad_norm_joined(dproj, wg, h, g, dres, token, name):
    s, d_model = h.shape
    n = N_DEV * wg.shape[2]
    tm = ROW_TILE

    def body(_, dp_ref, w_ref, h_ref, g_ref, dr_ref, dh_ref, dg_ref, wj):
        _join_once(w_ref, wj)

        @pl.when(pl.program_id(0) == 0)
        def _():
            dg_ref[...] = jnp.zeros_like(dg_ref)

        dhn = _dot_nt(dp_ref[...], wj[...])
        hb = h_ref[...]
        r = lax.rsqrt(jnp.mean(hb * hb, axis=-1, keepdims=True) + EPS)
        dg_ref[...] += jnp.sum(dhn * (hb * r), axis=0, keepdims=True)
        dn = dhn * g_ref[...]
        dh_ref[...] = dr_ref[...] + r * dn - hb * ((r * r * r) * jnp.mean(dn * hb, axis=-1, keepdims=True))

    return _pallas_call(
        body, name=name, grid=(s // tm,),
        out_shape=[_plain((s, d_model), F32), _plain((1, d_model), F32)],
        in_specs=[ANY, _rows(n), _resident(wg.shape), _rows(d_model), _whole((1, d_model)), _rows(d_model)],
        out_specs=[_rows(d_model), _whole((1, d_model))],
        scratch_shapes=[pltpu.VMEM((d_model, n), BF16)],
        compiler_params=_params(1, 60),
    )(token, dproj, wg, h, g, dres)


def _wgrad_shards_t(dp_t, h, c, name):
    n, s = dp_t.shape
    d_model = h.shape[1]
    per_step = 2

    def body(a_ref, b_ref, o_ref):
        o_ref[...] = _dot(a_ref[...], b_ref[...]).astype(BF16).reshape(per_step, c, d_model)

    return _pallas_call(
        body, name=name, grid=(N_DEV // per_step,), out_shape=_sds((N_DEV, c, d_model), BF16),
        in_specs=[pl.BlockSpec((per_step * c, s), lambda j: (j, 0)), _resident(h.shape)],
        out_specs=pl.BlockSpec((per_step, c, d_model), lambda j: (j, 0, 0)), compiler_params=_params(1, 40),
    )(dp_t, h)


def _wgrad_cols(a_t, b, c, name):
    m, s = a_t.shape
    assert c % LANES == 0

    def body(a_ref, b_ref, o_ref):
        wide = _dot(a_ref[...], b_ref[...]).astype(BF16)
        for j in range(WGRAD_SHARDS):
            o_ref[j] = wide[:, j * c:(j + 1) * c]

    return _pallas_call(
        body, name=name, grid=(N_DEV // WGRAD_SHARDS,), out_shape=_sds((N_DEV, m, c), BF16),
        in_specs=[_whole(a_t.shape), pl.BlockSpec((s, WGRAD_SHARDS * c), lambda j: (0, j))],
        out_specs=pl.BlockSpec((WGRAD_SHARDS, m, c), lambda j: (j, 0, 0)), compiler_params=_params(1, 40),
    )(a_t, b)


def _wgrad_rows(a_t, b, name):
    m, s = a_t.shape
    n = b.shape[1]
    mr = m // N_DEV

    def body(a_ref, b_ref, o_ref):
        o_ref[...] = _dot(a_ref[...], b_ref[...]).astype(BF16).reshape(WGRAD_SHARDS, mr, n)

    return _pallas_call(
        body, name=name, grid=(N_DEV // WGRAD_SHARDS,), out_shape=_sds((N_DEV, mr, n), BF16),
        in_specs=[pl.BlockSpec((WGRAD_SHARDS * mr, s), lambda j: (j, 0)), _whole(b.shape)],
        out_specs=pl.BlockSpec((WGRAD_SHARDS, mr, n), lambda j: (j, 0, 0)), compiler_params=_params(1, 40),
    )(a_t, b)


def _memkv_bwd(dkv, w, mem, g):
    n_layers = w.shape[0]
    rows = D_MODEL // N_DEV

    def body(dkv_ref, w_ref, mem_ref, g_ref, dw_ref, dg_ref):
        mb = mem_ref[...]
        r = lax.rsqrt(jnp.mean(mb * mb, axis=-1, keepdims=True) + EPS)
        nm = mb * r
        mn = (nm * g_ref[...]).astype(BF16)
        dkvb = dkv_ref[...].astype(BF16)
        dw_ref[...] = _dot_tn(mn, dkvb).astype(BF16).reshape(N_DEV, rows, 2 * MEM_WIDTH)
        dmn = _dot_nt(dkvb, w_ref[...])
        dg_ref[...] = jnp.sum(dmn * nm, axis=0, keepdims=True)

    lay = lambda *shape: pl.BlockSpec((None,) + shape, lambda l: (l, 0, 0))
    return _pallas_call(
        body, name="memkv_bwd", grid=(n_layers,),
        out_shape=[_sds((N_DEV, n_layers * rows, 2 * MEM_WIDTH), BF16), _plain((n_layers, 1, D_MODEL), F32)],
        in_specs=[lay(N_MEM, 2 * MEM_WIDTH), lay(D_MODEL, 2 * MEM_WIDTH), _whole(mem.shape), lay(1, D_MODEL)],
        out_specs=[pl.BlockSpec((N_DEV, rows, 2 * MEM_WIDTH), lambda l: (0, l, 0)), lay(1, D_MODEL)],
        compiler_params=_params(1, 32),
    )(dkv, w, mem, g.reshape(n_layers, 1, D_MODEL))


def _bwd_post_attn(dhb, wg_out, z, os_, ls_, qm, kv, mo, head_ones):
    s = dhb.shape[0]
    gw = GROUP_WIDTH
    nb = gw + MEM_WIDTH
    tm = ROW_TILE

    def body(dh_ref, w_ref, z_ref, o0, o1, o2, l0, l1, l2, qm_ref, kv_ref, mo_ref, bd_ref,
             dz_ref, do0, do1, do2, dl0, dl1, dl2, dqm_ref, dkv_ref, s0, s1):
        @pl.when(pl.program_id(0) == 0)
        def _():
            dkv_ref[...] = jnp.zeros_like(dkv_ref)

        dy = _dot_nt(dh_ref[...], _joined_columns(w_ref))
        ov, lv = [], []
        for o_ref, l_ref, d in zip((o0, o1, o2), (l0, l1, l2), DILATIONS):
            ov.append(_from_view(s0, o_ref, d))
            lv.append(_from_view(s1, l_ref, d))
        ws, mix = _mix_groups(ov, lv)
        sz, dsz = _silu_parts(z_ref[...])
        dz_ref[:, :gw] = (dy[:, :gw] * mix * dsz[:, :gw]).astype(BF16)
        dz_ref[:, gw:] = (dy[:, gw:] * mo_ref[...] * dsz[:, gw:]).astype(BF16)
        dbr = dy * sz
        dmix = dbr[:, :gw]
        t = dmix * mix
        th = t.astype(BF16)
        tl = (t - th.astype(F32)).astype(BF16)
        rs = _dot(th, bd_ref[...]) + _dot(tl, bd_ref[...])
        for wg_, do_ref, dl_ref, d in zip(ws, (do0, do1, do2), (dl0, dl1, dl2), DILATIONS):
            _to_view(s0, wg_ * dmix, do_ref, d)
            _to_view(s1, wg_ * rs, dl_ref, d)
        _mem_attn_bwd(qm_ref[...], kv_ref, dbr[:, gw:], dqm_ref, dkv_ref)

    vspecs = [_view_rows(gw, d) for d in DILATIONS]
    return _pallas_call(
        body, name="bwd_post_attn", grid=(s // tm,),
        out_shape=[_sds((s, nb), BF16)] + [_sds((s // d, d * gw), BF16) for d in DILATIONS]
                  + [_sds((s // d, d * gw), F32) for d in DILATIONS] + [_sds((s, MEM_WIDTH), BF16), _plain(kv.shape, F32)],
        in_specs=[_rows(D_MODEL), _whole(wg_out.shape), _rows(nb)] + vspecs * 2
                 + [_rows(MEM_WIDTH), _whole(kv.shape), _rows(MEM_WIDTH), _whole(head_ones.shape)],
        out_specs=[_rows(nb)] + vspecs * 2 + [_rows(MEM_WIDTH), _whole(kv.shape)],
        scratch_shapes=[_view_scratch(gw), _view_scratch(gw)],
        compiler_params=_params(1, 48),
    )(dhb, wg_out, z, *os_, *ls_, qm, kv, mo, head_ones)


def _attn_bwd(q, k, v, lse, do, dl, tabs, d, token):
    ln, dw = q.shape
    w = dw // d
    nb = ln // BLOCK
    reps = w // LANES
    two, before = _pair_specs(d, nb, w)
    two_t, _ = _pair_specs(d, nb, LANES)

    def attend(q_ref, l_ref, do_ref, dl_ref, dqs, acck, accv, rows, col0, kk, vv, acc_rows):
        valid = _band_mask(kk.shape[0])
        low = _low_head_lanes()
        pairs, qcols = _head_tiles(w, col0)
        cols = [slice(col0 + h * HEAD_DIM, col0 + h * HEAD_DIM + 1) for h in range(HEADS_PER_GROUP)]
        qhs = [h for qc in qcols for h in _split_pair(q_ref[rows, qc], low)]
        dobs = [h for qc in qcols for h in _split_pair(do_ref[rows, qc], low)]
        k2s = [kk[:, pr] for pr in pairs for _ in range(2)]
        v2s = [vv[:, pr] for pr in pairs for _ in range(2)]
        scs = [jnp.where(valid, _dot_nt(qh, k2), NEG) for qh, k2 in zip(qhs, k2s)]
        dps = [_dot_nt(dob, v2) for dob, v2 in zip(dobs, v2s)]
        ps = [jnp.exp(sc - l_ref[rows, col]) for sc, col in zip(scs, cols)]
        dss = [(p * (dp - dl_ref[rows, col])).astype(BF16) for p, dp, col in zip(ps, dps, cols)]
        pbs = [p.astype(BF16) for p in ps]
        for i, qc in enumerate(qcols):
            a, b = 2 * i, 2 * i + 1
            dqs[rows, qc] = jnp.where(low, _dot(dss[a], k2s[a]), _dot(dss[b], k2s[b])) * SCALE
            acck[acc_rows, qc] += _dot_tn(dss[a], qhs[a]) + _dot_tn(dss[b], qhs[b])
            accv[acc_rows, qc] += _dot_tn(pbs[a], dobs[a]) + _dot_tn(pbs[b], dobs[b])

    def body_streams(_, q_ref, kc_ref, vc_ref, l_ref, do_ref, dl_ref, c_ref, sa_ref, sb_ref,
                     dq_ref, dk_ref, dv_ref, acck, accv, dqs):
        acck[...] = jnp.zeros_like(acck)
        accv[...] = jnp.zeros_like(accv)
        for sb in range(2):
            cols = slice(sb * w, (sb + 1) * w)
            attend(q_ref, l_ref, do_ref, dl_ref, dqs, acck, accv, TOP, sb * w, kc_ref[:, cols], vc_ref[:, cols], TOP)
        tabs2 = [jnp.concatenate([jnp.tile(r[:, sb * LANES:(sb + 1) * LANES], (1, reps)) for sb in range(2)], axis=1)
                 for r in (c_ref, sa_ref, sb_ref)]
        dq_ref[...] = _rope_bwd(dqs[...], *tabs2).astype(BF16)
        dk_ref[...] = _rope_bwd(acck[...], *tabs2).astype(BF16)
        dv_ref[...] = accv[...].astype(BF16)

    def body_blocks(_, q_ref, kp_ref, kc_ref, vp_ref, vc_ref, l_ref, do_ref, dl_ref, cq, saq, sbq, ck, sak, sbk,
                    dq_ref, dk_ref, dv_ref, acck, accv, dqs):
        i = pl.program_id(0) % (nb // 2)

        @pl.when(i == 0)
        def _():
            acck[...] = jnp.zeros_like(acck)
            accv[...] = jnp.zeros_like(accv)

        refs = (q_ref, l_ref, do_ref, dl_ref, dqs, acck, accv)
        pl.when(i == 0)(lambda: attend(*refs, TOP, 0, kc_ref[TOP, :], vc_ref[TOP, :], TOP))
        pl.when(i != 0)(lambda: attend(
            *refs, TOP, 0, jnp.concatenate([kp_ref[...], kc_ref[TOP, :]], axis=0),
            jnp.concatenate([vp_ref[...], vc_ref[TOP, :]], axis=0),
            pl.ds(pl.multiple_of((2 * i - 1) * BLOCK, BLOCK), 2 * BLOCK)))
        attend(*refs, BOTTOM, 0, kc_ref[...], vc_ref[...], pl.ds(pl.multiple_of(2 * i * BLOCK, BLOCK), 2 * BLOCK))
        tq = [jnp.tile(r[...], (1, reps)) for r in (cq, saq, sbq)]
        dq_ref[...] = _rope_bwd(dqs[...], *tq).astype(BF16)

        @pl.when(i == nb // 2 - 1)
        def _():
            for r0 in range(0, nb * BLOCK, 2 * BLOCK):
                rows = slice(r0, r0 + 2 * BLOCK)
                tk = [jnp.tile(r[rows, :], (1, reps)) for r in (ck, sak, sbk)]
                dk_ref[rows, :] = _rope_bwd(acck[rows, :], *tk).astype(BF16)
                dv_ref[rows, :] = accv[rows, :].astype(BF16)

    if nb == 1:
        body = body_streams
        in_specs = [ANY] + [two] * 6 + [two_t] * 3
        args = (token, q, k, v, lse, do, dl, *tabs)
        out_specs = [two, two, two]
        acc_shape = (BLOCK, 2 * w)
    else:
        body = body_blocks
        stream = pl.BlockSpec((nb * BLOCK, w), lambda n: (0, n // (nb // 2)))
        stream_t = pl.BlockSpec((nb * BLOCK, LANES), lambda n: (0, n // (nb // 2)))
        in_specs = [ANY, two, before, two, before, two, two, two, two] + [two_t] * 3 + [stream_t] * 3
        args = (token, q, k, k, v, v, lse, do, dl, *tabs, *tabs)
        out_specs = [two, stream, stream]
        acc_shape = (nb * BLOCK, w)
    return _pallas_call(
        body, name=f"attn_bwd_d{d}", grid=(d * nb // 2,), out_shape=[_sds((ln, dw), BF16)] * 3,
        in_specs=in_specs, out_specs=out_specs,
        scratch_shapes=[pltpu.VMEM(acc_shape, F32), pltpu.VMEM(acc_shape, F32), pltpu.VMEM(two.block_shape, F32)],
        compiler_params=_params(1, 48),
    )(*args)


def _position():
    return lax.axis_index("x"), lax.axis_index("y"), lax.axis_index("c")


def _all_gather(shards, after, name):
    n_a = len(shards)

    def body(*refs):
        x_refs, out_refs = refs[:n_a], refs[n_a + 1:2 * n_a + 1]
        send_sems, recv_sems, local_sems = refs[2 * n_a + 1:]
        x, y, c = _position()
        me, sibling = (x, y, c), (x, y, 1 - c)
        chips = [(1 - x, y), (x, 1 - y), (1 - x, 1 - y)]

        def rows(a, px, py, pc):
            return out_refs[a].at[4 * px + 2 * py + pc]

        def copy(a, k, block, to, own=False):
            return pltpu.make_async_remote_copy(
                src_ref=x_refs[a] if own else rows(a, *block), dst_ref=rows(a, *block),
                send_sem=send_sems.at[a, k], recv_sem=recv_sems.at[a, k], device_id=to, device_id_type=MESH)

        mine = [pltpu.make_async_copy(x_refs[a], rows(a, *me), local_sems.at[a]) for a in range(n_a)]
        for cp in mine:
            cp.start()
        first = []
        for j, chip in enumerate(chips):
            first += [copy(a, 1 + j, me, (*chip, c), own=True) for a in range(n_a)]
        first += [copy(a, 0, me, sibling, own=True) for a in range(n_a)]
        for cp in first:
            cp.start()
        passed = []
        for j, chip in enumerate(chips):
            for a in range(n_a):
                copy(a, 1 + j, (*chip, c), me).wait_recv()
                fwd = copy(a, 4 + j, (*chip, c), sibling)
                fwd.start()
                passed.append(fwd)
        for a in range(n_a):
            copy(a, 0, sibling, me).wait_recv()
        for j, chip in enumerate(chips):
            for a in range(n_a):
                copy(a, 4 + j, (*chip, 1 - c), me).wait_recv()
        for cp in first + passed:
            cp.wait_send()
        for cp in mine:
            cp.wait()

    return _pallas_call(
        body, name=name, out_shape=[_sds((N_DEV,) + t.shape, t.dtype) for t in shards],
        in_specs=[ANY] * (n_a + 1), out_specs=[ANY] * n_a,
        scratch_shapes=[pltpu.SemaphoreType.DMA((n_a, 7)), pltpu.SemaphoreType.DMA((n_a, 7)),
                        pltpu.SemaphoreType.DMA((n_a,))],
    )(*shards, after)


def _all_gather_relay(xs, name):
    def body(x_ref, out_ref, send_sems, recv_sems, local_sem):
        x, y, c = _position()
        me, sibling = (x, y, c), (x, y, 1 - c)
        xn, yn, diag = (1 - x, y, c), (x, 1 - y, c), (1 - x, 1 - y, c)
        src_nb = (x + c * (1 - 2 * x), y + (1 - c) * (1 - 2 * y), c)
        dst_nb = (x + (1 - c) * (1 - 2 * x), y + c * (1 - 2 * y), c)

        def rows(dev):
            return out_ref.at[4 * dev[0] + 2 * dev[1] + dev[2]]

        def copy(k, block, to, own=False):
            return pltpu.make_async_remote_copy(
                src_ref=x_ref if own else rows(block), dst_ref=rows(block),
                send_sem=send_sems.at[k], recv_sem=recv_sems.at[k], device_id=to, device_id_type=MESH)

        mine = pltpu.make_async_copy(x_ref, rows(me), local_sem)
        mine.start()
        first = [copy(1, me, xn, own=True), copy(2, me, yn, own=True), copy(0, me, sibling, own=True)]
        for cp in first:
            cp.start()
        copy(1, xn, me).wait_recv()
        copy(2, yn, me).wait_recv()
        relay = copy(3, src_nb, dst_nb)
        relay.start()
        passed = [copy(4, xn, sibling), copy(5, yn, sibling)]
        for cp in passed:
            cp.start()
        copy(3, diag, me).wait_recv()
        last = copy(6, diag, sibling)
        last.start()
        copy(0, sibling, me).wait_recv()
        for k, blk in ((4, (1 - x, y, 1 - c)), (5, (x, 1 - y, 1 - c)), (6, (1 - x, 1 - y, 1 - c))):
            copy(k, blk, me).wait_recv()
        for cp in first + [relay] + passed + [last]:
            cp.wait_send()
        mine.wait()

    return _pallas_call(
        body, name=name, out_shape=_sds((N_DEV,) + xs.shape, xs.dtype),
        in_specs=[ANY], out_specs=ANY,
        scratch_shapes=[pltpu.SemaphoreType.DMA((7,)), pltpu.SemaphoreType.DMA((7,)), pltpu.SemaphoreType.DMA],
    )(xs)


HBM_SPEC = pl.BlockSpec(memory_space=pltpu.HBM)
SEM_SPEC = pl.BlockSpec(memory_space=pltpu.SEMAPHORE)
EFFECT = pltpu.SideEffectType.DATAFLOW_SIDE_EFFECTING
def _plan_gather_own(src_refs, land_refs):
    x, y, c = _position()
    me = 4 * x + 2 * y + c
    peers = [(x, y, 1 - c), (1 - x, y, c), (x, 1 - y, c), (1 - x, 1 - y, c)]
    return [(src_refs[a], land_refs[a].at[me], (a, k), peer) for k, peer in enumerate(peers) for a in range(len(src_refs))]


def _plan_gather_pass(src_refs, land_refs):
    x, y, c = _position()
    chips = [(1 - x, y), (x, 1 - y), (1 - x, 1 - y)]
    return [(land_refs[a].at[4 * px + 2 * py + c], land_refs[a].at[4 * px + 2 * py + c], (a, j), (x, y, 1 - c))
            for j, (px, py) in enumerate(chips) for a in range(len(land_refs))]


def _plan_to_sibling(src_refs, land_refs):
    x, y, c = _position()
    return [(src_refs[a].at[2 * k + (1 - c)], land_refs[a].at[k], (a, k), (x, y, 1 - c))
            for k in range(4) for a in range(len(src_refs))]


def _plan_to_chips(src_refs, land_refs):
    x, y, c = _position()
    chips = [(1 - x, y), (x, 1 - y), (1 - x, 1 - y)]
    return [(src_refs[a].at[2 * px + py], land_refs[a].at[j], (a, j), (px, py, c))
            for j, (px, py) in enumerate(chips) for a in range(len(src_refs))]


def _split_start(srcs, lands, plan, n_sem, after, name):
    n_s, n_a = len(srcs), len(lands)
    n_b = n_s + n_a

    def body(*refs):
        src_refs, land_refs = refs[:n_s], refs[n_s:n_b]
        send_sems, recv_sems, token = refs[n_b + 1], refs[n_b + 2], refs[-1]
        for src, dst, (a, k), dev in plan(src_refs, land_refs):
            i = a * n_sem + k
            pltpu.make_async_remote_copy(src_ref=src, dst_ref=dst, send_sem=send_sems.at[i], recv_sem=recv_sems.at[i],
                                         device_id=dev, device_id_type=MESH).start()
        token[...] = jnp.zeros_like(token)

    bufs = list(srcs) + list(lands)
    res = pl.pallas_call(
        body, name=name,
        out_shape=(pltpu.SemaphoreType.DMA((n_a * n_sem,)), pltpu.SemaphoreType.DMA((n_a * n_sem,)),
                   *[pltpu.HBM(t.shape, t.dtype) for t in bufs], _plain((8, LANES), F32)),
        in_specs=[HBM_SPEC] * n_b + [ANY],
        out_specs=(SEM_SPEC, SEM_SPEC, *[HBM_SPEC] * n_b, pl.BlockSpec(memory_space=pltpu.VMEM)),
        input_output_aliases={i: 2 + i for i in range(n_b)},
        compiler_params=pltpu.CompilerParams(has_side_effects=EFFECT),
    )(*[pltpu.with_memory_space_constraint(t, pltpu.HBM) for t in bufs], after)
    return (res[0], res[1], res[2:2 + n_s], res[2 + n_s:2 + n_b]), res[-1]


def _split_wait(started, plan, after, name):
    send_sems, recv_sems, srcs, lands = started
    n_s, n_a = len(srcs), len(lands)
    n_b = n_s + n_a
    n_sem = send_sems.shape[0] // n_a

    def body(*refs):
        src_refs, land_refs = refs[:n_s], refs[n_s:n_b]
        s_sems, r_sems = refs[n_b], refs[n_b + 1]
        for src, dst, (a, k), dev in plan(src_refs, land_refs):
            i = a * n_sem + k
            cp = pltpu.make_async_remote_copy(src_ref=src, dst_ref=dst, send_sem=s_sems.at[i], recv_sem=r_sems.at[i],
                                              device_id=dev, device_id_type=MESH)
            cp.wait_send()
            cp.wait_recv()

    bufs = list(srcs) + list(lands)
    res = pl.pallas_call(
        body, name=name, out_shape=tuple(pltpu.HBM(t.shape, t.dtype) for t in bufs),
        in_specs=[HBM_SPEC] * n_b + [SEM_SPEC, SEM_SPEC, ANY],
        out_specs=tuple([HBM_SPEC] * n_b),
        input_output_aliases={i: i for i in range(n_b)},
        compiler_params=pltpu.CompilerParams(has_side_effects=EFFECT),
    )(*bufs, send_sems, recv_sems, after)
    return res[:n_s], res[n_s:]


SUBLANES = 8


def _row_tile(r):
    return max(t for t in range(SUBLANES, ROW_TILE + 1, SUBLANES) if r % t == 0)


def _rs_add_sibling(gp, recv, ck_arr, name):
    _, r, l = gp.shape
    tr = r if r <= 4 * ROW_TILE else _row_tile(r)
    block = lambda k, ck: (k + ck[1] + 1) % 4

    def body(ck_ref, g_ref, r_ref, pf_ref, pb_ref):
        sm = g_ref[...].astype(F32) + r_ref[...].astype(F32)
        pf_ref[...] = sm
        pb_ref[...] = sm.astype(BF16)

    spec = pl.BlockSpec((None, tr, l), lambda i, k, ck: (block(k, ck), i, 0))
    return _pallas_call(
        body, name=name,
        grid_spec=pltpu.PrefetchScalarGridSpec(
            num_scalar_prefetch=1, grid=(r // tr, 4),
            in_specs=[pl.BlockSpec((None, tr, l), lambda i, k, ck: (2 * block(k, ck) + ck[0], i, 0)), spec],
            out_specs=[pl.BlockSpec((tr, l), lambda i, k, ck: (i, 0)), spec]),
        out_shape=[_sds((r, l), F32), _sds((4, r, l), BF16)], compiler_params=_params(2, 32),
    )(ck_arr, gp, recv)


def _adam_update(w, gv, m, v):
    nm = ADAM_B1 * m + (1.0 - ADAM_B1) * gv
    nv = ADAM_B2 * v + (1.0 - ADAM_B2) * (gv * gv)
    m_hat = nm / (1.0 - ADAM_B1 ** ADAM_STEP)
    v_hat = nv / (1.0 - ADAM_B2 ** ADAM_STEP)
    return -ADAM_LR * (m_hat / (jnp.sqrt(v_hat) + ADAM_EPS) + ADAM_WD * w), nm, nv


def _rs_finish_adamw(pf, recv, w, m, v, name):
    r, l = pf.shape
    tr = _row_tile(r)

    def body(p_ref, r_ref, w_ref, m_ref, v_ref, g_ref, d_ref, nm_ref, nv_ref):
        gv = ((p_ref[...] + r_ref[0].astype(F32)) + r_ref[1].astype(F32)) + r_ref[2].astype(F32)
        g_ref[...] = gv
        d_ref[...], nm_ref[...], nv_ref[...] = _adam_update(w_ref[...], gv, m_ref[...], v_ref[...])

    spec = pl.BlockSpec((tr, l), lambda i: (i, 0))
    return _pallas_call(
        body, name=name, grid=(r // tr,),
        in_specs=[spec, pl.BlockSpec((3, tr, l), lambda i: (0, i, 0)), spec, spec, spec], out_specs=[spec] * 4,
        out_shape=[_plain((r, l), F32)] * 4, compiler_params=_params(1, 32),
    )(pf, recv, w, m, v)


def _sum_devices(g):
    def body(g_ref, o_ref):
        acc = g_ref[0]
        for j in range(1, N_DEV):
            acc = acc + g_ref[j]
        o_ref[...] = acc

    return _pallas_call(body, name="sum_devices", out_shape=_plain(g.shape[1:], F32))(g)


def _adamw(w, g, m, v, name):
    shape = w.shape
    w2, g2, m2, v2 = [t.reshape((-1, shape[-1])) for t in (w, g, m, v)]

    def body(w_ref, g_ref, m_ref, v_ref, d_ref, nm_ref, nv_ref):
        d_ref[...], nm_ref[...], nv_ref[...] = _adam_update(w_ref[...], g_ref[...], m_ref[...], v_ref[...])

    outs = _pallas_call(body, name=name, out_shape=[_plain(w2.shape, F32)] * 3)(w2, g2, m2, v2)
    return tuple(t.reshape(shape) for t in outs)


def _after(t, token):
    return t + token[0:1, 0:1].astype(t.dtype)


def _finish(name, pf, recv, w, m, v):
    if name in ("attn_w_in", "conv_w_in"):
        res = _rs_finish_adamw(pf, recv, w.T, m.T, v.T, "rs_finish_adamw_" + name)
        return tuple(t.T for t in res)
    return _rs_finish_adamw(pf, recv, w, m, v, "rs_finish_adamw_" + name)


def kernel(x, mem, positions, norm_g, mem_norm_g, w_mem_kv, attn_w_in, attn_w_out, conv_w_in, conv_w, conv_w_out, final_g, loss_target, m_norm_g, m_mem_norm_g, m_w_mem_kv, m_attn_w_in, m_attn_w_out, m_conv_w_in, m_conv_w, m_conv_w_out, m_final_g, v_norm_g, v_mem_norm_g, v_w_mem_kv, v_attn_w_in, v_attn_w_out, v_conv_w_in, v_conv_w, v_conv_w_out, v_final_g):
    px, py, pc = _position()
    me = 4 * px + 2 * py + pc
    ck_arr = jnp.stack([pc, 2 * px + py]).astype(jnp.int32)
    x, mem, pos, tgt = x[0], mem[0], positions[0], loss_target[0]

    wg_in0 = _all_gather_relay(attn_w_in[0].astype(BF16), "gather_w_in0")
    late = [attn_w_out[0].astype(BF16), conv_w_in[0].astype(BF16), conv_w_out[0].astype(BF16),
            w_mem_kv.astype(BF16).reshape(-1, w_mem_kv.shape[2]), jnp.pad(conv_w[0], ((0, 5), (0, 0)))]
    lands = [lax.dynamic_update_slice(lax.empty((N_DEV,) + t.shape, t.dtype), t[None], (me, 0, 0)) for t in late]
    late_weights, late_token = _split_start(late, lands, _plan_gather_own, 4, wg_in0, "gather_late_start")

    tabs = _rope_tables(pos)
    g0, g1 = _after(norm_g[0:1], late_token), norm_g[1:2]

    hn0, qs, ks, vs, tabs_v, qm0, z0 = _inproj_attn(x, g0, wg_in0, tabs)
    os_, ls_ = [], []
    for j, d in enumerate(DILATIONS):
        if j == 2:
            _, lands = _split_wait(late_weights, _plan_gather_own, ls_[1], "gather_late_wait")
            late_weights, late_token = _split_start([], lands, _plan_gather_pass, 3, ls_[1], "gather_late_pass_start")
        o, l = _attn_fwd(qs[j], ks[j], vs[j], d, late_token)
        os_.append(o)
        ls_.append(l)

    _, gathered = _split_wait(late_weights, _plan_gather_pass, ls_[2], "gather_late_pass_wait")
    wg_out0, wg_in1, wg_out1, wg_kv, cw_all = gathered
    w_out1 = wg_out1.reshape(-1, wg_out1.shape[2])
    n_kv = w_mem_kv.shape[1]
    w_kv = wg_kv.reshape(N_DEV, 2, n_kv, -1).transpose(1, 0, 2, 3).reshape(2, N_DEV * n_kv, -1)
    cw = cw_all[:, 0:3].transpose(1, 0, 2).reshape(3, -1)
    kv = _memkv_fwd(mem, mem_norm_g, w_kv)
    y0, y0_t, mo0, h1 = _post_attn(os_, ls_, qm0, kv[0], z0, x, wg_out0)

    hn1, bg, cg, u, qm1, z1 = _inproj_conv(h1, g1, wg_in1)
    y1, y1_t, mo1, dh2, dh2b, loss_acc, d_final_g = _post_conv_loss(
        bg, cg, u, qm1, kv[1], z1, h1, w_out1, cw, final_g.reshape(1, -1), tgt)

    d_w_out1 = _wgrad_rows(y1_t, dh2b, "wgrad_out1")
    dz1, dbg, dconv, dqm1, dkv1 = _bwd_post_conv(dh2b, w_out1, bg, cg, u, z1, qm1, kv[1], mo1, cw)
    dcg, du, dcw = _bwd_conv(dconv, cg, u, cw)
    dh1, dh1b, dg1, dproj1_t = _dgrad_norm([(dbg, 1), (dcg, 1), (du, 1), (dqm1, 1), (dz1, 1)], wg_in1, h1, g1, dh2,
                                           "dgrad_norm_conv")
    d_w_in1 = _wgrad_shards_t(dproj1_t, hn1, wg_in1.shape[2], "wgrad_in1")

    d_w_out0 = _wgrad_cols(y0_t, dh1b, wg_out0.shape[2], "wgrad_out0")

    names1 = ["conv_w_in", "conv_w_out", "attn_w_out"]
    grads1 = [d_w_in1, d_w_out1, d_w_out0]
    started, token = _split_start(grads1, [lax.empty((4,) + g.shape[1:], g.dtype) for g in grads1],
                                  _plan_to_sibling, 4, dg1, "rs1_sibling_start")

    gw = GROUP_WIDTH
    ones = (jnp.arange(gw)[:, None] // HEAD_DIM == jnp.arange(gw)[None, :] // HEAD_DIM).astype(BF16)
    ones = _after(ones, token)
    res = _bwd_post_attn(dh1b, wg_out0, z0, os_, ls_, qm0, kv[0], mo0, ones)
    dz0, dos, dls, dqm0, dkv0 = res[0], res[1:4], res[4:7], res[7], res[8]

    grads1, from_sibling = _split_wait(started, _plan_to_sibling, dz0, "rs1_sibling_wait")
    parts1 = [_rs_add_sibling(g, r, ck_arr, "rs_add_sibling_" + n) for g, r, n in zip(grads1, from_sibling, names1)]
    pbs1 = [pb for _, pb in parts1]
    started, token = _split_start(pbs1, [lax.empty((3,) + p.shape[1:], p.dtype) for p in pbs1],
                                  _plan_to_chips, 3, dg1, "rs1_chips_start")

    dqs, dks, dvs = [], [], []
    for j, d in enumerate(DILATIONS):
        dq, dk, dv = _attn_bwd(qs[j], ks[j], vs[j], ls_[j], dos[j], dls[j], tabs_v[j], d, token)
        dqs.append((dq, d))
        dks.append((dk, d))
        dvs.append((dv, d))
    d_w_kv, d_mem_g = _memkv_bwd(jnp.stack([dkv0, dkv1]), w_kv, mem, mem_norm_g)
    dproj0, dproj0_t = _assemble_dproj(dqs + dks + dvs + [(dqm0, 1), (dz0, 1)], N_DEV * wg_in0.shape[2],
                                       "assemble_dproj_attn")
    d_w_in0 = _wgrad_shards_t(dproj0_t, hn0, wg_in0.shape[2], "wgrad_in0")

    names0 = ["attn_w_in", "w_mem_kv"]
    grads0 = [d_w_in0, d_w_kv]
    started0, token0 = _split_start(grads0, [lax.empty((4,) + g.shape[1:], g.dtype) for g in grads0],
                                    _plan_to_sibling, 4, dg1, "rs0_sibling_start")
    _, from_chips1 = _split_wait(started, _plan_to_chips, token0, "rs1_chips_wait")
    shard = dict(attn_w_in=(attn_w_in[0], m_attn_w_in[0], v_attn_w_in[0]),
                 attn_w_out=(attn_w_out[0], m_attn_w_out[0], v_attn_w_out[0]),
                 conv_w_in=(conv_w_in[0], m_conv_w_in[0], v_conv_w_in[0]),
                 conv_w_out=(conv_w_out[0], m_conv_w_out[0], v_conv_w_out[0]),
                 w_mem_kv=tuple(t.reshape(-1, t.shape[2]) for t in (w_mem_kv, m_w_mem_kv, v_w_mem_kv)))
    big = {}
    for n, (pf, _), r in zip(names1, parts1, from_chips1):
        big[n] = _finish(n, pf, r, *shard[n])

    grads0, from_sibling = _split_wait(started0, _plan_to_sibling, big["conv_w_out"][1], "rs0_sibling_wait")
    parts0 = [_rs_add_sibling(g, r, ck_arr, "rs_add_sibling_" + n) for g, r, n in zip(grads0, from_sibling, names0)]
    pbs0 = [pb for _, pb in parts0]
    started0, token0 = _split_start(pbs0, [lax.empty((3,) + p.shape[1:], p.dtype) for p in pbs0],
                                    _plan_to_chips, 3, dg1, "rs0_chips_start")
    dx, dg0 = _dgrad_norm_joined(dproj0, wg_in0, x, g0, dh1, token0, "dgrad_norm_attn")

    small_part = jnp.concatenate([dg0, dg1, d_mem_g.reshape(2, -1), d_final_g, dcw[0:3]], axis=0)
    small_part = jnp.concatenate([small_part, jnp.broadcast_to(loss_acc[0, 0], small_part.shape)], axis=0)
    small = _sum_devices(_all_gather([small_part], dg0, "gather_small_grads")[0])
    loss = small[8, 0]
    g_conv_w = lax.dynamic_slice(small[5:8], (0, me * LANES), (3, LANES))[None]
    small_g = dict(norm_g=small[0:2], mem_norm_g=small[2:4], conv_w=g_conv_w, final_g=small[4])
    small_w = dict(norm_g=(norm_g, m_norm_g, v_norm_g), mem_norm_g=(mem_norm_g, m_mem_norm_g, v_mem_norm_g),
                   conv_w=(conv_w, m_conv_w, v_conv_w), final_g=(final_g, m_final_g, v_final_g))
    for n, (w, m, v) in small_w.items():
        big[n] = (small_g[n],) + _adamw(w, small_g[n], m, v, "adamw_" + n)

    _, from_chips0 = _split_wait(started0, _plan_to_chips, big["final_g"][1], "rs0_chips_wait")
    for n, (pf, _), r in zip(names0, parts0, from_chips0):
        big[n] = _finish(n, pf, r, *shard[n])
    for n in ("attn_w_in", "attn_w_out", "conv_w_in", "conv_w_out"):
        big[n] = tuple(t[None] for t in big[n])
    big["w_mem_kv"] = tuple(t.reshape(w_mem_kv.shape) for t in big["w_mem_kv"])

    order = ["norm_g", "mem_norm_g", "w_mem_kv", "attn_w_in", "attn_w_out", "conv_w_in", "conv_w", "conv_w_out", "final_g"]
    return (loss, dx[None], *[big[n][0] for n in order], *[big[n][1] for n in order],
            *[big[n][2] for n in order], *[big[n][3] for n in order])
```

```python
import jax
import jax.numpy as jnp
from jax import lax
from jax.experimental import pallas as pl
from jax.experimental.pallas import tpu as pltpu

F32 = jnp.float32
BF16 = jnp.bfloat16

N_DEV = 8
D_MODEL = 1024
HEAD_DIM = 64
ROT_DIM = HEAD_DIM // 4
ROPE_THETA = 500000.0
DILATIONS = (1, 4, 16)
HEADS_PER_GROUP = 8
GROUP_WIDTH = HEADS_PER_GROUP * HEAD_DIM
BLOCK = 128
N_MEM = 256
MEM_HEADS = 4
MEM_WIDTH = MEM_HEADS * HEAD_DIM
CONV_WIDTH = D_MODEL
EPS = 1e-6
SCALE = HEAD_DIM ** -0.5
NEG = -1e30

ADAM_LR = 0.001
ADAM_B1 = 0.9
ADAM_B2 = 0.999
ADAM_EPS = 1e-08
ADAM_WD = 0.01
ADAM_STEP = 10

ROW_TILE = 256
WGRAD_SHARDS = 4
LANES = 128
MESH = pl.DeviceIdType.MESH
ANY = pl.BlockSpec(memory_space=pl.ANY)


def _pallas_call(body, **kw):
    call = pl.pallas_call(body, **kw)

    def run(*args):
        pinned = [pltpu.with_memory_space_constraint(a, pltpu.HBM) if jnp.issubdtype(a.dtype, jnp.floating) else a
                  for a in args]
        return call(*pinned)

    return run


def _dot(a, b):
    return lax.dot_general(a, b, (((1,), (0,)), ((), ())), preferred_element_type=F32)


def _dot_nt(a, b):
    return lax.dot_general(a, b, (((1,), (1,)), ((), ())), preferred_element_type=F32)


def _dot_tn(a, b):
    return lax.dot_general(a, b, (((0,), (0,)), ((), ())), preferred_element_type=F32)


def _params(n_grid, vmem_mb=48):
    return pltpu.CompilerParams(dimension_semantics=("arbitrary",) * n_grid, vmem_limit_bytes=vmem_mb << 20)


def _rows(width, tm=ROW_TILE):
    return pl.BlockSpec((tm, width), lambda i: (i, 0))


def _view_rows(width, d, tm=ROW_TILE):
    return pl.BlockSpec((tm // d, d * width), lambda i: (i, 0))


def _whole(shape):
    return pl.BlockSpec(shape, lambda *_: (0,) * len(shape))


def _resident(shape):
    return pl.BlockSpec(shape, lambda *_: (0,) * len(shape), pipeline_mode=pl.Buffered(1))


def _sds(shape, dtype):
    return pltpu.HBM(shape, dtype)


def _plain(shape, dtype):
    return jax.ShapeDtypeStruct(shape, dtype)


def _silu_parts(z):
    sg = jax.nn.sigmoid(z)
    return z * sg, sg * (1.0 + z * (1.0 - sg))


def _to_view(scr, val, out_ref, d):
    tm, w = val.shape
    if d == 1:
        out_ref[...] = val.astype(out_ref.dtype)
        return
    for cb in range(w // LANES):
        scr[cb] = val[:, cb * LANES:(cb + 1) * LANES]
    for r in range(d):
        for cb in range(w // LANES):
            lo = r * w + cb * LANES
            out_ref[:, lo:lo + LANES] = scr[cb, pl.ds(r, tm // d, stride=d), :].astype(out_ref.dtype)


def _from_view(scr, in_ref, d):
    if d == 1:
        return in_ref[...].astype(F32)
    nc, tm, _ = scr.shape
    w = nc * LANES
    for r in range(d):
        for cb in range(nc):
            lo = r * w + cb * LANES
            scr[cb, pl.ds(r, tm // d, stride=d), :] = in_ref[:, lo:lo + LANES].astype(F32)
    return jnp.concatenate([scr[cb] for cb in range(nc)], axis=1)


def _view_scratch(width, tm=ROW_TILE):
    return pltpu.VMEM((width // LANES, tm, LANES), F32)


def _rope_tables(pos):
    half = ROT_DIM // 2
    inv_freq = ROPE_THETA ** (-jnp.arange(half, dtype=F32) * (2.0 / ROT_DIM))
    ang = pos.astype(F32)[:, None] * inv_freq
    cos, sin = jnp.cos(ang), jnp.sin(ang)
    s = pos.shape[0]
    z8 = jnp.zeros((s, half), F32)
    rest = HEAD_DIM - ROT_DIM
    cosf = jnp.concatenate([cos, cos, jnp.ones((s, rest), F32)], axis=1)
    sa = jnp.concatenate([-sin, z8, jnp.zeros((s, rest), F32)], axis=1)
    sb = jnp.concatenate([z8, sin, jnp.zeros((s, rest), F32)], axis=1)
    return tuple(jnp.tile(t, (1, LANES // HEAD_DIM)) for t in (cosf, sa, sb))


def _rope_fwd(t, cv, sav, sbv):
    w = t.shape[1]
    return t * cv + pltpu.roll(t, w - ROT_DIM // 2, 1) * sav + pltpu.roll(t, ROT_DIM // 2, 1) * sbv


def _rope_bwd(g, cv, sav, sbv):
    w = g.shape[1]
    return g * cv + pltpu.roll(g * sav, ROT_DIM // 2, 1) + pltpu.roll(g * sbv, w - ROT_DIM // 2, 1)


def _joined_columns(wg_ref):
    assert wg_ref.shape[2] % LANES == 0
    return jnp.concatenate([wg_ref[j] for j in range(N_DEV)], axis=1)


def _join_once(wg_ref, w_scr):
    c = wg_ref.shape[2]

    @pl.when(pl.program_id(0) == 0)
    def _():
        for j in range(N_DEV):
            w_scr[:, j * c:(j + 1) * c] = wg_ref[j]


def _inproj_attn(x, g, wg, tabs):
    s, d_model = x.shape
    gw = GROUP_WIDTH
    n = N_DEV * wg.shape[2]
    nz = n - 9 * gw - MEM_WIDTH
    reps = gw // LANES
    tm = ROW_TILE

    def body(x_ref, g_ref, w_ref, c_ref, sa_ref, sb_ref, hn_ref, *rest):
        outs, (wj, scr, tscr) = rest[:-3], rest[-3:]
        q_refs, k_refs, v_refs, t_refs, qm_ref, z_ref = outs[0:3], outs[3:6], outs[6:9], outs[9:18], outs[18], outs[19]
        _join_once(w_ref, wj)
        xb = x_ref[...]
        r = lax.rsqrt(jnp.mean(xb * xb, axis=-1, keepdims=True) + EPS)
        hn = ((xb * r) * g_ref[...]).astype(BF16)
        hn_ref[...] = hn
        proj = lambda lo, hi: _dot(hn, wj[:, lo:hi])
        tab = (c_ref[...], sa_ref[...], sb_ref[...])
        cv, sav, sbv = [jnp.tile(t, (1, reps)) for t in tab]
        for j, d in enumerate(DILATIONS):
            tq = _rope_fwd(proj(j * gw, (j + 1) * gw), cv, sav, sbv)
            _to_view(scr, tq * SCALE, q_refs[j], d)
            tk = _rope_fwd(proj((3 + j) * gw, (4 + j) * gw), cv, sav, sbv)
            _to_view(scr, tk, k_refs[j], d)
            _to_view(scr, proj((6 + j) * gw, (7 + j) * gw), v_refs[j], d)
            for i in range(3):
                _to_view(tscr, tab[i], t_refs[3 * j + i], d)
        qm_ref[...] = proj(9 * gw, 9 * gw + MEM_WIDTH).astype(BF16)
        z_ref[...] = proj(9 * gw + MEM_WIDTH, n)

    views = [_sds((s // d, d * gw), BF16) for d in DILATIONS]
    tviews = [_sds((s // d, d * LANES), F32) for d in DILATIONS for _ in range(3)]
    out_shape = [_sds((s, d_model), BF16)] + views * 3 + tviews + [_sds((s, MEM_WIDTH), BF16), _sds((s, nz), F32)]
    vspecs = [_view_rows(gw, d, tm) for d in DILATIONS]
    tspecs = [_view_rows(LANES, d, tm) for d in DILATIONS for _ in range(3)]
    out_specs = [_rows(d_model, tm)] + vspecs * 3 + tspecs + [_rows(MEM_WIDTH, tm), _rows(nz, tm)]
    res = _pallas_call(
        body, name="inproj_attn", grid=(s // tm,), out_shape=out_shape,
        in_specs=[_rows(d_model, tm), _whole((1, d_model)), _resident(wg.shape)] + [_rows(LANES, tm)] * 3,
        out_specs=out_specs,
        scratch_shapes=[pltpu.VMEM((d_model, n), BF16), _view_scratch(gw, tm), _view_scratch(LANES, tm)],
        compiler_params=_params(1, 60),
    )(x, g, wg, *tabs)
    tabs_v = [res[10 + 3 * j:13 + 3 * j] for j in range(3)]
    return res[0], res[1:4], res[4:7], res[7:10], tabs_v, res[19], res[20]


def _band_mask(n_keys):
    qi = lax.broadcasted_iota(jnp.int32, (BLOCK, n_keys), 0)
    kj = lax.broadcasted_iota(jnp.int32, (BLOCK, n_keys), 1)
    if n_keys == BLOCK:
        return kj <= qi
    return jnp.logical_or(jnp.logical_and(kj < BLOCK, kj >= qi), jnp.logical_and(kj >= BLOCK, (kj - BLOCK) <= qi))


def _low_head_lanes():
    return lax.broadcasted_iota(jnp.int32, (1, LANES), 1) < HEAD_DIM


def _split_pair(t, low):
    zero = jnp.zeros_like(t)
    return jnp.where(low, t, zero), jnp.where(low, zero, t)


def _pair_specs(d, nb, w):
    if nb == 1:
        return pl.BlockSpec((BLOCK, 2 * w), lambda n: (0, n)), None
    half = nb // 2
    two = pl.BlockSpec((2 * BLOCK, w), lambda n: (n % half, n // half))
    before = pl.BlockSpec((BLOCK, w), lambda n: (jnp.maximum(2 * (n % half) - 1, 0), n // half))
    return two, before


def _head_tiles(w, col0):
    return ([slice(p * LANES, (p + 1) * LANES) for p in range(w // LANES)],
            [slice(col0 + p * LANES, col0 + (p + 1) * LANES) for p in range(w // LANES)])


def _attend_fwd(q_ref, o_ref, lse_ref, rows, col0, kk, vv):
    w = kk.shape[1]
    valid = _band_mask(kk.shape[0])
    low = _low_head_lanes()
    pairs, qcols = _head_tiles(w, col0)
    qs_ = [h for qc in qcols for h in _split_pair(q_ref[rows, qc], low)]
    k2s = [kk[:, pr] for pr in pairs for _ in range(2)]
    scs = [jnp.where(valid, _dot_nt(qh, k2), NEG) for qh, k2 in zip(qs_, k2s)]
    ms = [jnp.max(sc, axis=-1, keepdims=True) for sc in scs]
    ps = [jnp.exp(sc - m) for sc, m in zip(scs, ms)]
    ls = [jnp.sum(p, axis=-1, keepdims=True) for p in ps]
    pns = [(p * (1.0 / l)).astype(BF16) for p, l in zip(ps, ls)]
    for i, (pr, qc) in enumerate(zip(pairs, qcols)):
        v2 = vv[:, pr]
        a, b = 2 * i, 2 * i + 1
        o_ref[rows, qc] = jnp.where(low, _dot(pns[a], v2), _dot(pns[b], v2))
        lse_ref[rows, qc] = jnp.where(low, ms[a] + jnp.log(ls[a]), ms[b] + jnp.log(ls[b]))


TOP, BOTTOM = slice(0, BLOCK), slice(BLOCK, 2 * BLOCK)


def _attn_fwd(q, k, v, d, token):
    ln, dw = q.shape
    w = dw // d
    nb = ln // BLOCK
    two, before = _pair_specs(d, nb, w)

    def body_streams(_, q_ref, kc_ref, vc_ref, o_ref, lse_ref):
        for sb in range(2):
            cols = slice(sb * w, (sb + 1) * w)
            _attend_fwd(q_ref, o_ref, lse_ref, TOP, sb * w, kc_ref[:, cols], vc_ref[:, cols])

    def body_blocks(_, q_ref, kp_ref, kc_ref, vp_ref, vc_ref, o_ref, lse_ref):
        first = pl.program_id(0) % (nb // 2) == 0
        pl.when(first)(lambda: _attend_fwd(q_ref, o_ref, lse_ref, TOP, 0, kc_ref[TOP, :], vc_ref[TOP, :]))
        pl.when(jnp.logical_not(first))(lambda: _attend_fwd(
            q_ref, o_ref, lse_ref, TOP, 0, jnp.concatenate([kp_ref[...], kc_ref[TOP, :]], axis=0),
            jnp.concatenate([vp_ref[...], vc_ref[TOP, :]], axis=0)))
        _attend_fwd(q_ref, o_ref, lse_ref, BOTTOM, 0, kc_ref[...], vc_ref[...])

    if nb == 1:
        body, in_specs, args = body_streams, [ANY, two, two, two], (token, q, k, v)
    else:
        body, in_specs, args = body_blocks, [ANY, two, before, two, before, two], (token, q, k, k, v, v)
    return _pallas_call(
        body, name=f"attn_fwd_d{d}", grid=(d * nb // 2,), out_shape=[_sds((ln, dw), F32)] * 2,
        in_specs=in_specs, out_specs=[two, two], compiler_params=_params(1, 32),
    )(*args)


def _memkv_fwd(mem, g, w):
    n_layers = w.shape[0]

    def body(mem_ref, g_ref, w_ref, kv_ref):
        mb = mem_ref[...]
        r = lax.rsqrt(jnp.mean(mb * mb, axis=-1, keepdims=True) + EPS)
        mn = ((mb * r) * g_ref[...]).astype(BF16)
        kv_ref[...] = _dot(mn, w_ref[...]).astype(BF16)

    return _pallas_call(
        body, name="memkv_fwd", grid=(n_layers,),
        out_shape=_plain((n_layers, N_MEM, 2 * MEM_WIDTH), BF16),
        in_specs=[_whole(mem.shape), pl.BlockSpec((None, 1, D_MODEL), lambda l: (l, 0, 0)),
                  pl.BlockSpec((None, D_MODEL, 2 * MEM_WIDTH), lambda l: (l, 0, 0))],
        out_specs=pl.BlockSpec((None, N_MEM, 2 * MEM_WIDTH), lambda l: (l, 0, 0)),
        compiler_params=_params(1, 32),
    )(mem, g.reshape(n_layers, 1, D_MODEL), w)


def _mix_groups(os_, ls_):
    mx = jnp.maximum(jnp.maximum(ls_[0], ls_[1]), ls_[2])
    es = [jnp.exp(t - mx) for t in ls_]
    inv = 1.0 / (es[0] + es[1] + es[2])
    ws = [e * inv for e in es]
    mix = ws[0] * os_[0] + ws[1] * os_[1] + ws[2] * os_[2]
    return ws, mix


MEM_PAIRS = [slice(p * LANES, (p + 1) * LANES) for p in range(MEM_WIDTH // LANES)]


def _mem_probs(qhs, k2s):
    scs = [_dot_nt(qh, k2) * SCALE for qh, k2 in zip(qhs, k2s)]
    es = [jnp.exp(sc - jnp.max(sc, axis=-1, keepdims=True)) for sc in scs]
    return [e * (1.0 / jnp.sum(e, axis=-1, keepdims=True)) for e in es]


def _mem_attn_into(qm, kv_ref, mo_ref):
    low = _low_head_lanes()
    qhs = [h for pr in MEM_PAIRS for h in _split_pair(qm[:, pr], low)]
    k2s = [kv_ref[:, pr] for pr in MEM_PAIRS for _ in range(2)]
    ps = [p.astype(BF16) for p in _mem_probs(qhs, k2s)]
    for i, pr in enumerate(MEM_PAIRS):
        v2 = kv_ref[:, MEM_WIDTH + i * LANES:MEM_WIDTH + (i + 1) * LANES]
        mo_ref[:, pr] = jnp.where(low, _dot(ps[2 * i], v2), _dot(ps[2 * i + 1], v2))


def _mem_attn_bwd(qm, kv_ref, dmem, dqm_ref, dkv_ref):
    low = _low_head_lanes()
    dmb = dmem.astype(BF16)
    vps = [slice(MEM_WIDTH + i * LANES, MEM_WIDTH + (i + 1) * LANES) for i in range(len(MEM_PAIRS))]
    qhs = [h for pr in MEM_PAIRS for h in _split_pair(qm[:, pr], low)]
    dhs = [h for pr in MEM_PAIRS for h in _split_pair(dmb[:, pr], low)]
    k2s = [kv_ref[:, pr] for pr in MEM_PAIRS for _ in range(2)]
    v2s = [kv_ref[:, vp] for vp in vps for _ in range(2)]
    ps = _mem_probs(qhs, k2s)
    dps = [_dot_nt(dh, v2) for dh, v2 in zip(dhs, v2s)]
    dss = [(p * (dp - jnp.sum(dp * p, axis=-1, keepdims=True)) * SCALE).astype(BF16) for p, dp in zip(ps, dps)]
    pbs = [p.astype(BF16) for p in ps]
    for i, (pr, vp) in enumerate(zip(MEM_PAIRS, vps)):
        a, b = 2 * i, 2 * i + 1
        dqm_ref[:, pr] = jnp.where(low, _dot(dss[a], k2s[a]), _dot(dss[b], k2s[b])).astype(BF16)
        dkv_ref[:, pr] += _dot_tn(dss[a], qhs[a]) + _dot_tn(dss[b], qhs[b])
        dkv_ref[:, vp] += _dot_tn(pbs[a], dhs[a]) + _dot_tn(pbs[b], dhs[b])


def _post_attn(os_, ls_, qm, kv, z, x, wg_out):
    s, d_model = x.shape
    gw = GROUP_WIDTH
    nb = gw + MEM_WIDTH
    tm = ROW_TILE

    def body(o0, o1, o2, l0, l1, l2, qm_ref, kv_ref, z_ref, x_ref, w_ref, y_ref, yt_ref, mo_ref, h_ref, s0, s1):
        ov, lv = [], []
        for o_ref, l_ref, d in zip((o0, o1, o2), (l0, l1, l2), DILATIONS):
            ov.append(_from_view(s0, o_ref, d))
            lv.append(_from_view(s1, l_ref, d))
        _, mix = _mix_groups(ov, lv)
        _mem_attn_into(qm_ref[...], kv_ref, mo_ref)
        sz, _ = _silu_parts(z_ref[...])
        y_ref[:, :gw] = (mix * sz[:, :gw]).astype(BF16)
        y_ref[:, gw:] = (mo_ref[...] * sz[:, gw:]).astype(BF16)
        y = y_ref[...]
        yt_ref[...] = y.T
        h_ref[...] = x_ref[...] + _dot(y, _joined_columns(w_ref))

    vspecs = [_view_rows(gw, d) for d in DILATIONS]
    return _pallas_call(
        body, name="post_attn", grid=(s // tm,),
        out_shape=[_sds((s, nb), BF16), _sds((nb, s), BF16), _sds((s, MEM_WIDTH), F32), _sds((s, d_model), F32)],
        in_specs=vspecs * 2 + [_rows(MEM_WIDTH), _whole(kv.shape), _rows(nb), _rows(d_model), _whole(wg_out.shape)],
        out_specs=[_rows(nb), pl.BlockSpec((nb, tm), lambda i: (0, i)), _rows(MEM_WIDTH), _rows(d_model)],
        scratch_shapes=[_view_scratch(gw), _view_scratch(gw)],
        compiler_params=_params(1, 40),
    )(*os_, *ls_, qm, kv, z, x, wg_out)


def _inproj_conv(x, g, wg):
    s, d_model = x.shape
    c = CONV_WIDTH
    n = N_DEV * wg.shape[2]
    nz = n - 3 * c - MEM_WIDTH
    tm = ROW_TILE

    def body(x_ref, g_ref, w_ref, hn_ref, bg_ref, cg_ref, u_ref, qm_ref, z_ref, wj):
        _join_once(w_ref, wj)
        xb = x_ref[...]
        r = lax.rsqrt(jnp.mean(xb * xb, axis=-1, keepdims=True) + EPS)
        hn = ((xb * r) * g_ref[...]).astype(BF16)
        hn_ref[...] = hn
        bg_ref[...] = _dot(hn, wj[:, 0:c])
        cg_ref[...] = _dot(hn, wj[:, c:2 * c])
        u_ref[...] = _dot(hn, wj[:, 2 * c:3 * c])
        qm_ref[...] = _dot(hn, wj[:, 3 * c:3 * c + MEM_WIDTH]).astype(BF16)
        z_ref[...] = _dot(hn, wj[:, 3 * c + MEM_WIDTH:])

    return _pallas_call(
        body, name="inproj_conv", grid=(s // tm,),
        out_shape=[_sds((s, d_model), BF16)] + [_sds((s, c), F32)] * 3 + [_sds((s, MEM_WIDTH), BF16), _sds((s, nz), F32)],
        in_specs=[_rows(d_model), _whole((1, d_model)), _resident(wg.shape)],
        out_specs=[_rows(d_model)] + [_rows(c)] * 3 + [_rows(MEM_WIDTH), _rows(nz)],
        scratch_shapes=[pltpu.VMEM((d_model, n), BF16)],
        compiler_params=_params(1, 60),
    )(x, g, wg)


HALO = 8


def _halo_before(width, tm=ROW_TILE):
    return pl.BlockSpec((HALO, width), lambda i: (jnp.maximum(i * (tm // HALO) - 1, 0), 0))


def _halo_after(width, n_rows, tm=ROW_TILE):
    return pl.BlockSpec((HALO, width), lambda i: (jnp.minimum((i + 1) * (tm // HALO), n_rows // HALO - 1), 0))


def _conv_taps(cg_ref, u_ref, cgh_ref, uh_ref, i):
    a = cg_ref[...] * u_ref[...]
    ah = jnp.where(i > 0, cgh_ref[...] * uh_ref[...], 0.0)
    row = lax.broadcasted_iota(jnp.int32, a.shape, 0)
    a1 = jnp.where(row == 0, ah[HALO - 1:HALO], pltpu.roll(a, 1, 0))
    a2 = jnp.where(row == 0, ah[HALO - 2:HALO - 1], jnp.where(row == 1, ah[HALO - 1:HALO], pltpu.roll(a, 2, 0)))
    return a, a1, a2


def _post_conv_loss(bg, cg, u, qm, kv, z, h1, w_out, cw, gf, tgt):
    s, d = h1.shape
    c = CONV_WIDTH
    nb = c + MEM_WIDTH
    tm = ROW_TILE

    def body(bg_ref, cg_ref, u_ref, cgh_ref, uh_ref, qm_ref, kv_ref, z_ref, h_ref, w_ref, cw_ref, gf_ref, t_ref,
             y_ref, yt_ref, mo_ref, dh_ref, dhb_ref, loss_ref, dgf_ref):
        i = pl.program_id(0)
        a, a1, a2 = _conv_taps(cg_ref, u_ref, cgh_ref, uh_ref, i)
        conv = cw_ref[0:1, :] * a2 + cw_ref[1:2, :] * a1 + cw_ref[2:3, :] * a
        mix = bg_ref[...] * conv
        _mem_attn_into(qm_ref[...], kv_ref, mo_ref)
        sz, _ = _silu_parts(z_ref[...])
        y_ref[:, :c] = (mix * sz[:, :c]).astype(BF16)
        y_ref[:, c:] = (mo_ref[...] * sz[:, c:]).astype(BF16)
        y = y_ref[...]
        yt_ref[...] = y.T
        h2 = h_ref[...] + _dot(y, w_ref[...])
        r = lax.rsqrt(jnp.mean(h2 * h2, axis=-1, keepdims=True) + EPS)
        nh = h2 * r
        gfv = gf_ref[...]
        diff = nh * gfv - t_ref[...]
        dout = diff * (1.0 / d)
        dn = dout * gfv
        dh2 = r * dn - h2 * ((r * r * r) * jnp.mean(dn * h2, axis=-1, keepdims=True))
        dh_ref[...] = dh2
        dhb_ref[...] = dh2.astype(BF16)

        @pl.when(i == 0)
        def _():
            loss_ref[...] = jnp.zeros_like(loss_ref)
            dgf_ref[...] = jnp.zeros_like(dgf_ref)

        loss_ref[...] += 0.5 * jnp.sum(jnp.mean(diff * diff, axis=-1, keepdims=True))
        dgf_ref[...] += jnp.sum(dout * nh, axis=0, keepdims=True)

    return _pallas_call(
        body, name="post_conv_loss", grid=(s // tm,),
        out_shape=[_sds((s, nb), BF16), _sds((nb, s), BF16), _sds((s, MEM_WIDTH), F32), _sds((s, d), F32),
                   _sds((s, d), BF16), _plain((8, LANES), F32), _plain((1, d), F32)],
        in_specs=[_rows(c)] * 3 + [_halo_before(c)] * 2 + [_rows(MEM_WIDTH), _whole(kv.shape), _rows(nb), _rows(d),
                  _whole(w_out.shape), _whole(cw.shape), _whole((1, d)), _rows(d)],
        out_specs=[_rows(nb), pl.BlockSpec((nb, tm), lambda i: (0, i)), _rows(MEM_WIDTH), _rows(d), _rows(d),
                   _whole((8, LANES)), _whole((1, d))],
        compiler_params=_params(1, 48),
    )(bg, cg, u, cg, u, qm, kv, z, h1, w_out, cw, gf, tgt)


def _bwd_post_conv(dhb, w_out, bg, cg, u, z, qm, kv, mo, cw):
    s = dhb.shape[0]
    c = CONV_WIDTH
    nb = c + MEM_WIDTH

    def body(dh_ref, w_ref, bg_ref, cg_ref, u_ref, cgh_ref, uh_ref, z_ref, qm_ref, kv_ref, mo_ref, cw_ref,
             dz_ref, dbg_ref, dc_ref, dqm_ref, dkv_ref):
        i = pl.program_id(0)

        @pl.when(i == 0)
        def _():
            dkv_ref[...] = jnp.zeros_like(dkv_ref)

        dy = _dot_nt(dh_ref[...], w_ref[...])
        sz, dsz = _silu_parts(z_ref[...])
        a, a1, a2 = _conv_taps(cg_ref, u_ref, cgh_ref, uh_ref, i)
        conv = cw_ref[0:1, :] * a2 + cw_ref[1:2, :] * a1 + cw_ref[2:3, :] * a
        bgv = bg_ref[...]
        dz_ref[:, :c] = (dy[:, :c] * (bgv * conv) * dsz[:, :c]).astype(BF16)
        dz_ref[:, c:] = (dy[:, c:] * mo_ref[...] * dsz[:, c:]).astype(BF16)
        dbr = dy * sz
        dmix = dbr[:, :c]
        dbg_ref[...] = (dmix * conv).astype(BF16)
        dc_ref[...] = dmix * bgv
        _mem_attn_bwd(qm_ref[...], kv_ref, dbr[:, c:], dqm_ref, dkv_ref)

    return _pallas_call(
        body, name="bwd_post_conv", grid=(s // ROW_TILE,),
        out_shape=[_sds((s, nb), BF16), _sds((s, c), BF16), _sds((s, c), F32), _sds((s, MEM_WIDTH), BF16),
                   _plain(kv.shape, F32)],
        in_specs=[_rows(D_MODEL), _whole(w_out.shape)] + [_rows(c)] * 3 + [_halo_before(c)] * 2
                 + [_rows(nb), _rows(MEM_WIDTH), _whole(kv.shape), _rows(MEM_WIDTH), _whole(cw.shape)],
        out_specs=[_rows(nb), _rows(c), _rows(c), _rows(MEM_WIDTH), _whole(kv.shape)],
        compiler_params=_params(1, 48),
    )(dhb, w_out, bg, cg, u, cg, u, z, qm, kv, mo, cw)


def _bwd_conv(dconv, cg, u, cw):
    s, c = dconv.shape
    tm = ROW_TILE
    last = s // tm - 1

    def body(dc_ref, dcn_ref, cg_ref, u_ref, cgh_ref, uh_ref, cw_ref, dcg_ref, du_ref, dcw_ref):
        i = pl.program_id(0)

        @pl.when(i == 0)
        def _():
            dcw_ref[...] = jnp.zeros_like(dcw_ref)

        dc = dc_ref[...]
        dcn = jnp.where(i < last, dcn_ref[...], 0.0)
        row = lax.broadcasted_iota(jnp.int32, dc.shape, 0)
        d1 = jnp.where(row == tm - 1, dcn[0:1], pltpu.roll(dc, tm - 1, 0))
        d2 = jnp.where(row == tm - 1, dcn[1:2], jnp.where(row == tm - 2, dcn[0:1], pltpu.roll(dc, tm - 2, 0)))
        da = cw_ref[2:3, :] * dc + cw_ref[1:2, :] * d1 + cw_ref[0:1, :] * d2
        a, a1, a2 = _conv_taps(cg_ref, u_ref, cgh_ref, uh_ref, i)
        dcg_ref[...] = (da * u_ref[...]).astype(BF16)
        du_ref[...] = (da * cg_ref[...]).astype(BF16)
        dcw_ref[0:1, :] += jnp.sum(dc * a2, axis=0, keepdims=True)
        dcw_ref[1:2, :] += jnp.sum(dc * a1, axis=0, keepdims=True)
        dcw_ref[2:3, :] += jnp.sum(dc * a, axis=0, keepdims=True)

    return _pallas_call(
        body, name="bwd_conv", grid=(s // tm,),
        out_shape=[_sds((s, c), BF16), _sds((s, c), BF16), _plain((8, c), F32)],
        in_specs=[_rows(c), _halo_after(c, s), _rows(c), _rows(c), _halo_before(c), _halo_before(c), _whole(cw.shape)],
        out_specs=[_rows(c), _rows(c), _whole((8, c))], compiler_params=_params(1, 40),
    )(dconv, dconv, cg, u, cg, u, cw)


def _assemble(p_refs, pieces, widths, dp, scr):
    off = 0
    for p_ref, (_, d), wd in zip(p_refs, pieces, widths):
        if d == 1:
            dp[:, off:off + wd] = p_ref[...]
        else:
            dp[:, off:off + wd] = _from_view(scr, p_ref, d).astype(BF16)
        off += wd


def _dgrad_norm(pieces, wg, h, g, dres, token, onward, name):
    s, d_model = h.shape
    n = N_DEV * wg.shape[2]
    tm = ROW_TILE
    widths = [p.shape[1] // d for p, d in pieces]
    assert sum(widths) == n
    n_p = len(pieces)

    def body(_, *refs):
        p_refs = refs[:n_p]
        w_ref, h_ref, g_ref, dr_ref, dh_ref, dg_ref = refs[n_p:n_p + 6]
        dp, scr, wj = refs[-3:]
        _join_once(w_ref, wj)

        @pl.when(pl.program_id(0) == 0)
        def _():
            dg_ref[...] = jnp.zeros_like(dg_ref)

        _assemble(p_refs, pieces, widths, dp, scr)
        dhn = _dot_nt(dp[...], wj[...])
        hb = h_ref[...]
        r = lax.rsqrt(jnp.mean(hb * hb, axis=-1, keepdims=True) + EPS)
        dg_ref[...] += jnp.sum(dhn * (hb * r), axis=0, keepdims=True)
        dn = dhn * g_ref[...]
        dh = dr_ref[...] + r * dn - hb * ((r * r * r) * jnp.mean(dn * hb, axis=-1, keepdims=True))
        dh_ref[...] = dh
        if onward:
            dhb_ref, dpt_ref = refs[n_p + 6:n_p + 8]
            dhb_ref[...] = dh.astype(BF16)
            dpt_ref[...] = dp[...].T

    p_specs = [_view_rows(wd, d) for (_, d), wd in zip(pieces, widths)]
    out_shape = [_plain((s, d_model), F32), _plain((1, d_model), F32)]
    out_specs = [_rows(d_model), _whole((1, d_model))]
    if onward:
        out_shape += [_sds((s, d_model), BF16), _sds((n, s), BF16)]
        out_specs += [_rows(d_model), pl.BlockSpec((n, tm), lambda i: (0, i))]
    return _pallas_call(
        body, name=name, grid=(s // tm,), out_shape=out_shape,
        in_specs=[ANY] + p_specs + [_resident(wg.shape), _rows(d_model), _whole((1, d_model)), _rows(d_model)],
        out_specs=out_specs,
        scratch_shapes=[pltpu.VMEM((tm, n), BF16), _view_scratch(GROUP_WIDTH), pltpu.VMEM((d_model, n), BF16)],
        compiler_params=_params(1, 60),
    )(token, *[p for p, _ in pieces], wg, h, g, dres)


def _assemble_dproj_t(pieces, n, name):
    tm = ROW_TILE
    widths = [p.shape[1] // d for p, d in pieces]
    assert sum(widths) == n
    s = pieces[0][0].shape[0] * pieces[0][1]
    n_p = len(pieces)

    def body(*refs):
        p_refs, (dpt_ref, dp, scr) = refs[:n_p], refs[n_p:]
        _assemble(p_refs, pieces, widths, dp, scr)
        dpt_ref[...] = dp[...].T

    return _pallas_call(
        body, name=name, grid=(s // tm,), out_shape=_sds((n, s), BF16),
        in_specs=[_view_rows(wd, d) for (_, d), wd in zip(pieces, widths)],
        out_specs=pl.BlockSpec((n, tm), lambda i: (0, i)),
        scratch_shapes=[pltpu.VMEM((tm, n), BF16), _view_scratch(GROUP_WIDTH)],
        compiler_params=_params(1, 40),
    )(*[p for p, _ in pieces])


def _wgrad_shards_t(dp_t, h, c, name):
    n, s = dp_t.shape
    d_model = h.shape[1]
    per_step = 2

    def body(a_ref, b_ref, o_ref):
        o_ref[...] = _dot(a_ref[...], b_ref[...]).astype(BF16).reshape(per_step, c, d_model)

    return _pallas_call(
        body, name=name, grid=(N_DEV // per_step,), out_shape=_sds((N_DEV, c, d_model), BF16),
        in_specs=[pl.BlockSpec((per_step * c, s), lambda j: (j, 0)), _resident(h.shape)],
        out_specs=pl.BlockSpec((per_step, c, d_model), lambda j: (j, 0, 0)), compiler_params=_params(1, 40),
    )(dp_t, h)


def _wgrad_cols(a_t, b, c, name):
    m, s = a_t.shape
    assert c % LANES == 0

    def body(a_ref, b_ref, o_ref):
        wide = _dot(a_ref[...], b_ref[...]).astype(BF16)
        for j in range(WGRAD_SHARDS):
            o_ref[j] = wide[:, j * c:(j + 1) * c]

    return _pallas_call(
        body, name=name, grid=(N_DEV // WGRAD_SHARDS,), out_shape=_sds((N_DEV, m, c), BF16),
        in_specs=[_whole(a_t.shape), pl.BlockSpec((s, WGRAD_SHARDS * c), lambda j: (0, j))],
        out_specs=pl.BlockSpec((WGRAD_SHARDS, m, c), lambda j: (j, 0, 0)), compiler_params=_params(1, 40),
    )(a_t, b)


def _wgrad_rows(a_t, b, name):
    m, s = a_t.shape
    n = b.shape[1]
    mr = m // N_DEV

    def body(a_ref, b_ref, o_ref):
        o_ref[...] = _dot(a_ref[...], b_ref[...]).astype(BF16).reshape(WGRAD_SHARDS, mr, n)

    return _pallas_call(
        body, name=name, grid=(N_DEV // WGRAD_SHARDS,), out_shape=_sds((N_DEV, mr, n), BF16),
        in_specs=[pl.BlockSpec((WGRAD_SHARDS * mr, s), lambda j: (j, 0)), _whole(b.shape)],
        out_specs=pl.BlockSpec((WGRAD_SHARDS, mr, n), lambda j: (j, 0, 0)), compiler_params=_params(1, 40),
    )(a_t, b)


def _memkv_bwd(dkv, w, mem, g):
    n_layers = w.shape[0]
    rows = D_MODEL // N_DEV

    def body(dkv_ref, w_ref, mem_ref, g_ref, dw_ref, dg_ref):
        mb = mem_ref[...]
        r = lax.rsqrt(jnp.mean(mb * mb, axis=-1, keepdims=True) + EPS)
        nm = mb * r
        mn = (nm * g_ref[...]).astype(BF16)
        dkvb = dkv_ref[...].astype(BF16)
        dw_ref[...] = _dot_tn(mn, dkvb).astype(BF16).reshape(N_DEV, rows, 2 * MEM_WIDTH)
        dmn = _dot_nt(dkvb, w_ref[...])
        dg_ref[...] = jnp.sum(dmn * nm, axis=0, keepdims=True)

    lay = lambda *shape: pl.BlockSpec((None,) + shape, lambda l: (l, 0, 0))
    return _pallas_call(
        body, name="memkv_bwd", grid=(n_layers,),
        out_shape=[_sds((N_DEV, n_layers * rows, 2 * MEM_WIDTH), BF16), _plain((n_layers, 1, D_MODEL), F32)],
        in_specs=[lay(N_MEM, 2 * MEM_WIDTH), lay(D_MODEL, 2 * MEM_WIDTH), _whole(mem.shape), lay(1, D_MODEL)],
        out_specs=[pl.BlockSpec((N_DEV, rows, 2 * MEM_WIDTH), lambda l: (0, l, 0)), lay(1, D_MODEL)],
        compiler_params=_params(1, 32),
    )(dkv, w, mem, g.reshape(n_layers, 1, D_MODEL))


def _bwd_post_attn(dhb, wg_out, z, os_, ls_, qm, kv, mo, head_ones):
    s = dhb.shape[0]
    gw = GROUP_WIDTH
    nb = gw + MEM_WIDTH
    tm = ROW_TILE

    def body(dh_ref, w_ref, z_ref, o0, o1, o2, l0, l1, l2, qm_ref, kv_ref, mo_ref, bd_ref,
             dz_ref, do0, do1, do2, dl0, dl1, dl2, dqm_ref, dkv_ref, s0, s1):
        @pl.when(pl.program_id(0) == 0)
        def _():
            dkv_ref[...] = jnp.zeros_like(dkv_ref)

        dy = _dot_nt(dh_ref[...], _joined_columns(w_ref))
        ov, lv = [], []
        for o_ref, l_ref, d in zip((o0, o1, o2), (l0, l1, l2), DILATIONS):
            ov.append(_from_view(s0, o_ref, d))
            lv.append(_from_view(s1, l_ref, d))
        ws, mix = _mix_groups(ov, lv)
        sz, dsz = _silu_parts(z_ref[...])
        dz_ref[:, :gw] = (dy[:, :gw] * mix * dsz[:, :gw]).astype(BF16)
        dz_ref[:, gw:] = (dy[:, gw:] * mo_ref[...] * dsz[:, gw:]).astype(BF16)
        dbr = dy * sz
        dmix = dbr[:, :gw]
        t = dmix * mix
        th = t.astype(BF16)
        tl = (t - th.astype(F32)).astype(BF16)
        rs = _dot(th, bd_ref[...]) + _dot(tl, bd_ref[...])
        for wg_, do_ref, dl_ref, d in zip(ws, (do0, do1, do2), (dl0, dl1, dl2), DILATIONS):
            _to_view(s0, wg_ * dmix, do_ref, d)
            _to_view(s1, wg_ * rs, dl_ref, d)
        _mem_attn_bwd(qm_ref[...], kv_ref, dbr[:, gw:], dqm_ref, dkv_ref)

    vspecs = [_view_rows(gw, d) for d in DILATIONS]
    return _pallas_call(
        body, name="bwd_post_attn", grid=(s // tm,),
        out_shape=[_sds((s, nb), BF16)] + [_sds((s // d, d * gw), BF16) for d in DILATIONS]
                  + [_sds((s // d, d * gw), F32) for d in DILATIONS] + [_sds((s, MEM_WIDTH), BF16), _plain(kv.shape, F32)],
        in_specs=[_rows(D_MODEL), _whole(wg_out.shape), _rows(nb)] + vspecs * 2
                 + [_rows(MEM_WIDTH), _whole(kv.shape), _rows(MEM_WIDTH), _whole(head_ones.shape)],
        out_specs=[_rows(nb)] + vspecs * 2 + [_rows(MEM_WIDTH), _whole(kv.shape)],
        scratch_shapes=[_view_scratch(gw), _view_scratch(gw)],
        compiler_params=_params(1, 48),
    )(dhb, wg_out, z, *os_, *ls_, qm, kv, mo, head_ones)


def _attn_bwd(q, k, v, lse, do, dl, tabs, d, token):
    ln, dw = q.shape
    w = dw // d
    nb = ln // BLOCK
    reps = w // LANES
    two, before = _pair_specs(d, nb, w)
    two_t, _ = _pair_specs(d, nb, LANES)

    def attend(q_ref, l_ref, do_ref, dl_ref, dqs, acck, accv, rows, col0, kk, vv, acc_rows):
        valid = _band_mask(kk.shape[0])
        low = _low_head_lanes()
        pairs, qcols = _head_tiles(w, col0)
        cols = [slice(col0 + h * HEAD_DIM, col0 + h * HEAD_DIM + 1) for h in range(HEADS_PER_GROUP)]
        qhs = [h for qc in qcols for h in _split_pair(q_ref[rows, qc], low)]
        dobs = [h for qc in qcols for h in _split_pair(do_ref[rows, qc], low)]
        k2s = [kk[:, pr] for pr in pairs for _ in range(2)]
        v2s = [vv[:, pr] for pr in pairs for _ in range(2)]
        scs = [jnp.where(valid, _dot_nt(qh, k2), NEG) for qh, k2 in zip(qhs, k2s)]
        dps = [_dot_nt(dob, v2) for dob, v2 in zip(dobs, v2s)]
        ps = [jnp.exp(sc - l_ref[rows, col]) for sc, col in zip(scs, cols)]
        dss = [(p * (dp - dl_ref[rows, col])).astype(BF16) for p, dp, col in zip(ps, dps, cols)]
        pbs = [p.astype(BF16) for p in ps]
        for i, qc in enumerate(qcols):
            a, b = 2 * i, 2 * i + 1
            dqs[rows, qc] = jnp.where(low, _dot(dss[a], k2s[a]), _dot(dss[b], k2s[b])) * SCALE
            acck[acc_rows, qc] += _dot_tn(dss[a], qhs[a]) + _dot_tn(dss[b], qhs[b])
            accv[acc_rows, qc] += _dot_tn(pbs[a], dobs[a]) + _dot_tn(pbs[b], dobs[b])

    def body_streams(_, q_ref, kc_ref, vc_ref, l_ref, do_ref, dl_ref, c_ref, sa_ref, sb_ref,
                     dq_ref, dk_ref, dv_ref, acck, accv, dqs):
        acck[...] = jnp.zeros_like(acck)
        accv[...] = jnp.zeros_like(accv)
        for sb in range(2):
            cols = slice(sb * w, (sb + 1) * w)
            attend(q_ref, l_ref, do_ref, dl_ref, dqs, acck, accv, TOP, sb * w, kc_ref[:, cols], vc_ref[:, cols], TOP)
        tabs2 = [jnp.concatenate([jnp.tile(r[:, sb * LANES:(sb + 1) * LANES], (1, reps)) for sb in range(2)], axis=1)
                 for r in (c_ref, sa_ref, sb_ref)]
        dq_ref[...] = _rope_bwd(dqs[...], *tabs2).astype(BF16)
        dk_ref[...] = _rope_bwd(acck[...], *tabs2).astype(BF16)
        dv_ref[...] = accv[...].astype(BF16)

    def body_blocks(_, q_ref, kp_ref, kc_ref, vp_ref, vc_ref, l_ref, do_ref, dl_ref, cq, saq, sbq, ck, sak, sbk,
                    dq_ref, dk_ref, dv_ref, acck, accv, dqs):
        i = pl.program_id(0) % (nb // 2)

        @pl.when(i == 0)
        def _():
            acck[...] = jnp.zeros_like(acck)
            accv[...] = jnp.zeros_like(accv)

        refs = (q_ref, l_ref, do_ref, dl_ref, dqs, acck, accv)
        pl.when(i == 0)(lambda: attend(*refs, TOP, 0, kc_ref[TOP, :], vc_ref[TOP, :], TOP))
        pl.when(i != 0)(lambda: attend(
            *refs, TOP, 0, jnp.concatenate([kp_ref[...], kc_ref[TOP, :]], axis=0),
            jnp.concatenate([vp_ref[...], vc_ref[TOP, :]], axis=0),
            pl.ds(pl.multiple_of((2 * i - 1) * BLOCK, BLOCK), 2 * BLOCK)))
        attend(*refs, BOTTOM, 0, kc_ref[...], vc_ref[...], pl.ds(pl.multiple_of(2 * i * BLOCK, BLOCK), 2 * BLOCK))
        tq = [jnp.tile(r[...], (1, reps)) for r in (cq, saq, sbq)]
        dq_ref[...] = _rope_bwd(dqs[...], *tq).astype(BF16)

        @pl.when(i == nb // 2 - 1)
        def _():
            for r0 in range(0, nb * BLOCK, 2 * BLOCK):
                rows = slice(r0, r0 + 2 * BLOCK)
                tk = [jnp.tile(r[rows, :], (1, reps)) for r in (ck, sak, sbk)]
                dk_ref[rows, :] = _rope_bwd(acck[rows, :], *tk).astype(BF16)
                dv_ref[rows, :] = accv[rows, :].astype(BF16)

    if nb == 1:
        body = body_streams
        in_specs = [ANY] + [two] * 6 + [two_t] * 3
        args = (token, q, k, v, lse, do, dl, *tabs)
        out_specs = [two, two, two]
        acc_shape = (BLOCK, 2 * w)
    else:
        body = body_blocks
        stream = pl.BlockSpec((nb * BLOCK, w), lambda n: (0, n // (nb // 2)))
        stream_t = pl.BlockSpec((nb * BLOCK, LANES), lambda n: (0, n // (nb // 2)))
        in_specs = [ANY, two, before, two, before, two, two, two, two] + [two_t] * 3 + [stream_t] * 3
        args = (token, q, k, k, v, v, lse, do, dl, *tabs, *tabs)
        out_specs = [two, stream, stream]
        acc_shape = (nb * BLOCK, w)
    return _pallas_call(
        body, name=f"attn_bwd_d{d}", grid=(d * nb // 2,), out_shape=[_sds((ln, dw), BF16)] * 3,
        in_specs=in_specs, out_specs=out_specs,
        scratch_shapes=[pltpu.VMEM(acc_shape, F32), pltpu.VMEM(acc_shape, F32), pltpu.VMEM(two.block_shape, F32)],
        compiler_params=_params(1, 48),
    )(*args)


def _position():
    return lax.axis_index("x"), lax.axis_index("y"), lax.axis_index("c")


def _all_gather(shards, after, name):
    n_a = len(shards)

    def body(*refs):
        x_refs, out_refs = refs[:n_a], refs[n_a + 1:2 * n_a + 1]
        send_sems, recv_sems, local_sems = refs[2 * n_a + 1:]
        x, y, c = _position()
        me, sibling = (x, y, c), (x, y, 1 - c)
        chips = [(1 - x, y), (x, 1 - y), (1 - x, 1 - y)]

        def rows(a, px, py, pc):
            return out_refs[a].at[4 * px + 2 * py + pc]

        def copy(a, k, block, to, own=False):
            return pltpu.make_async_remote_copy(
                src_ref=x_refs[a] if own else rows(a, *block), dst_ref=rows(a, *block),
                send_sem=send_sems.at[a, k], recv_sem=recv_sems.at[a, k], device_id=to, device_id_type=MESH)

        mine = [pltpu.make_async_copy(x_refs[a], rows(a, *me), local_sems.at[a]) for a in range(n_a)]
        for cp in mine:
            cp.start()
        first = []
        for j, chip in enumerate(chips):
            first += [copy(a, 1 + j, me, (*chip, c), own=True) for a in range(n_a)]
        first += [copy(a, 0, me, sibling, own=True) for a in range(n_a)]
        for cp in first:
            cp.start()
        passed = []
        for j, chip in enumerate(chips):
            for a in range(n_a):
                copy(a, 1 + j, (*chip, c), me).wait_recv()
                fwd = copy(a, 4 + j, (*chip, c), sibling)
                fwd.start()
                passed.append(fwd)
        for a in range(n_a):
            copy(a, 0, sibling, me).wait_recv()
        for j, chip in enumerate(chips):
            for a in range(n_a):
                copy(a, 4 + j, (*chip, 1 - c), me).wait_recv()
        for cp in first + passed:
            cp.wait_send()
        for cp in mine:
            cp.wait()

    return _pallas_call(
        body, name=name, out_shape=[_sds((N_DEV,) + t.shape, t.dtype) for t in shards],
        in_specs=[ANY] * (n_a + 1), out_specs=[ANY] * n_a,
        scratch_shapes=[pltpu.SemaphoreType.DMA((n_a, 7)), pltpu.SemaphoreType.DMA((n_a, 7)),
                        pltpu.SemaphoreType.DMA((n_a,))],
    )(*shards, after)


def _all_gather_relay(xs, name):
    def body(x_ref, out_ref, send_sems, recv_sems, local_sem):
        x, y, c = _position()
        me, sibling = (x, y, c), (x, y, 1 - c)
        xn, yn, diag = (1 - x, y, c), (x, 1 - y, c), (1 - x, 1 - y, c)
        src_nb = (x + c * (1 - 2 * x), y + (1 - c) * (1 - 2 * y), c)
        dst_nb = (x + (1 - c) * (1 - 2 * x), y + c * (1 - 2 * y), c)

        def rows(dev):
            return out_ref.at[4 * dev[0] + 2 * dev[1] + dev[2]]

        def copy(k, block, to, own=False):
            return pltpu.make_async_remote_copy(
                src_ref=x_ref if own else rows(block), dst_ref=rows(block),
                send_sem=send_sems.at[k], recv_sem=recv_sems.at[k], device_id=to, device_id_type=MESH)

        mine = pltpu.make_async_copy(x_ref, rows(me), local_sem)
        mine.start()
        first = [copy(1, me, xn, own=True), copy(2, me, yn, own=True), copy(0, me, sibling, own=True)]
        for cp in first:
            cp.start()
        copy(1, xn, me).wait_recv()
        copy(2, yn, me).wait_recv()
        relay = copy(3, src_nb, dst_nb)
        relay.start()
        passed = [copy(4, xn, sibling), copy(5, yn, sibling)]
        for cp in passed:
            cp.start()
        copy(3, diag, me).wait_recv()
        last = copy(6, diag, sibling)
        last.start()
        copy(0, sibling, me).wait_recv()
        for k, blk in ((4, (1 - x, y, 1 - c)), (5, (x, 1 - y, 1 - c)), (6, (1 - x, 1 - y, 1 - c))):
            copy(k, blk, me).wait_recv()
        for cp in first + [relay] + passed + [last]:
            cp.wait_send()
        mine.wait()

    return _pallas_call(
        body, name=name, out_shape=_sds((N_DEV,) + xs.shape, xs.dtype),
        in_specs=[ANY], out_specs=ANY,
        scratch_shapes=[pltpu.SemaphoreType.DMA((7,)), pltpu.SemaphoreType.DMA((7,)), pltpu.SemaphoreType.DMA],
    )(xs)


HBM_SPEC = pl.BlockSpec(memory_space=pltpu.HBM)
SEM_SPEC = pl.BlockSpec(memory_space=pltpu.SEMAPHORE)
EFFECT = pltpu.SideEffectType.DATAFLOW_SIDE_EFFECTING
def _plan_gather_own(src_refs, land_refs):
    x, y, c = _position()
    me = 4 * x + 2 * y + c
    peers = [(x, y, 1 - c), (1 - x, y, c), (x, 1 - y, c), (1 - x, 1 - y, c)]
    return [(src_refs[a], land_refs[a].at[me], (a, k), peer) for k, peer in enumerate(peers) for a in range(len(src_refs))]


def _plan_gather_pass(src_refs, land_refs):
    x, y, c = _position()
    chips = [(1 - x, y), (x, 1 - y), (1 - x, 1 - y)]
    return [(land_refs[a].at[4 * px + 2 * py + c], land_refs[a].at[4 * px + 2 * py + c], (a, j), (x, y, 1 - c))
            for j, (px, py) in enumerate(chips) for a in range(len(land_refs))]


def _plan_to_sibling(src_refs, land_refs):
    x, y, c = _position()
    return [(src_refs[a].at[2 * k + (1 - c)], land_refs[a].at[k], (a, k), (x, y, 1 - c))
            for k in range(4) for a in range(len(src_refs))]


def _plan_to_chips(src_refs, land_refs):
    x, y, c = _position()
    chips = [(1 - x, y), (x, 1 - y), (1 - x, 1 - y)]
    return [(src_refs[a].at[2 * px + py], land_refs[a].at[j], (a, j), (px, py, c))
            for j, (px, py) in enumerate(chips) for a in range(len(src_refs))]


def _split_start(srcs, lands, plan, n_sem, after, name):
    n_s, n_a = len(srcs), len(lands)
    n_b = n_s + n_a

    def body(*refs):
        src_refs, land_refs = refs[:n_s], refs[n_s:n_b]
        send_sems, recv_sems, token = refs[n_b + 1], refs[n_b + 2], refs[-1]
        for src, dst, (a, k), dev in plan(src_refs, land_refs):
            i = a * n_sem + k
            pltpu.make_async_remote_copy(src_ref=src, dst_ref=dst, send_sem=send_sems.at[i], recv_sem=recv_sems.at[i],
                                         device_id=dev, device_id_type=MESH).start()
        token[...] = jnp.zeros_like(token)

    bufs = list(srcs) + list(lands)
    res = pl.pallas_call(
        body, name=name,
        out_shape=(pltpu.SemaphoreType.DMA((n_a * n_sem,)), pltpu.SemaphoreType.DMA((n_a * n_sem,)),
                   *[pltpu.HBM(t.shape, t.dtype) for t in bufs], _plain((8, LANES), F32)),
        in_specs=[HBM_SPEC] * n_b + [ANY],
        out_specs=(SEM_SPEC, SEM_SPEC, *[HBM_SPEC] * n_b, pl.BlockSpec(memory_space=pltpu.VMEM)),
        input_output_aliases={i: 2 + i for i in range(n_b)},
        compiler_params=pltpu.CompilerParams(has_side_effects=EFFECT),
    )(*[pltpu.with_memory_space_constraint(t, pltpu.HBM) for t in bufs], after)
    return (res[0], res[1], res[2:2 + n_s], res[2 + n_s:2 + n_b]), res[-1]


def _split_wait(started, plan, after, name):
    send_sems, recv_sems, srcs, lands = started
    n_s, n_a = len(srcs), len(lands)
    n_b = n_s + n_a
    n_sem = send_sems.shape[0] // n_a

    def body(*refs):
        src_refs, land_refs = refs[:n_s], refs[n_s:n_b]
        s_sems, r_sems = refs[n_b], refs[n_b + 1]
        for src, dst, (a, k), dev in plan(src_refs, land_refs):
            i = a * n_sem + k
            cp = pltpu.make_async_remote_copy(src_ref=src, dst_ref=dst, send_sem=s_sems.at[i], recv_sem=r_sems.at[i],
                                              device_id=dev, device_id_type=MESH)
            cp.wait_send()
            cp.wait_recv()

    bufs = list(srcs) + list(lands)
    res = pl.pallas_call(
        body, name=name, out_shape=tuple(pltpu.HBM(t.shape, t.dtype) for t in bufs),
        in_specs=[HBM_SPEC] * n_b + [SEM_SPEC, SEM_SPEC, ANY],
        out_specs=tuple([HBM_SPEC] * n_b),
        input_output_aliases={i: i for i in range(n_b)},
        compiler_params=pltpu.CompilerParams(has_side_effects=EFFECT),
    )(*bufs, send_sems, recv_sems, after)
    return res[:n_s], res[n_s:]


SUBLANES = 8


def _row_tile(r):
    return max(t for t in range(SUBLANES, ROW_TILE + 1, SUBLANES) if r % t == 0)


def _rs_add_sibling(gp, recv, ck_arr, name):
    _, r, l = gp.shape
    tr = r if r <= 4 * ROW_TILE else _row_tile(r)
    block = lambda k, ck: (k + ck[1] + 1) % 4

    def body(ck_ref, g_ref, r_ref, pf_ref, pb_ref):
        sm = g_ref[...].astype(F32) + r_ref[...].astype(F32)
        pf_ref[...] = sm
        pb_ref[...] = sm.astype(BF16)

    spec = pl.BlockSpec((None, tr, l), lambda i, k, ck: (block(k, ck), i, 0))
    return _pallas_call(
        body, name=name,
        grid_spec=pltpu.PrefetchScalarGridSpec(
            num_scalar_prefetch=1, grid=(r // tr, 4),
            in_specs=[pl.BlockSpec((None, tr, l), lambda i, k, ck: (2 * block(k, ck) + ck[0], i, 0)), spec],
            out_specs=[pl.BlockSpec((tr, l), lambda i, k, ck: (i, 0)), spec]),
        out_shape=[_sds((r, l), F32), _sds((4, r, l), BF16)], compiler_params=_params(2, 32),
    )(ck_arr, gp, recv)


def _adam_update(w, gv, m, v):
    nm = ADAM_B1 * m + (1.0 - ADAM_B1) * gv
    nv = ADAM_B2 * v + (1.0 - ADAM_B2) * (gv * gv)
    m_hat = nm / (1.0 - ADAM_B1 ** ADAM_STEP)
    v_hat = nv / (1.0 - ADAM_B2 ** ADAM_STEP)
    return -ADAM_LR * (m_hat / (jnp.sqrt(v_hat) + ADAM_EPS) + ADAM_WD * w), nm, nv


def _rs_finish_adamw(pf, recv, w, m, v, name):
    r, l = pf.shape
    tr = _row_tile(r)

    def body(p_ref, r_ref, w_ref, m_ref, v_ref, g_ref, d_ref, nm_ref, nv_ref):
        gv = ((p_ref[...] + r_ref[0].astype(F32)) + r_ref[1].astype(F32)) + r_ref[2].astype(F32)
        g_ref[...] = gv
        d_ref[...], nm_ref[...], nv_ref[...] = _adam_update(w_ref[...], gv, m_ref[...], v_ref[...])

    spec = pl.BlockSpec((tr, l), lambda i: (i, 0))
    return _pallas_call(
        body, name=name, grid=(r // tr,),
        in_specs=[spec, pl.BlockSpec((3, tr, l), lambda i: (0, i, 0)), spec, spec, spec], out_specs=[spec] * 4,
        out_shape=[_plain((r, l), F32)] * 4, compiler_params=_params(1, 32),
    )(pf, recv, w, m, v)


def _sum_devices(g):
    def body(g_ref, o_ref):
        acc = g_ref[0]
        for j in range(1, N_DEV):
            acc = acc + g_ref[j]
        o_ref[...] = acc

    return _pallas_call(body, name="sum_devices", out_shape=_plain(g.shape[1:], F32))(g)


def _adamw(w, g, m, v, name):
    shape = w.shape
    w2, g2, m2, v2 = [t.reshape((-1, shape[-1])) for t in (w, g, m, v)]

    def body(w_ref, g_ref, m_ref, v_ref, d_ref, nm_ref, nv_ref):
        d_ref[...], nm_ref[...], nv_ref[...] = _adam_update(w_ref[...], g_ref[...], m_ref[...], v_ref[...])

    outs = _pallas_call(body, name=name, out_shape=[_plain(w2.shape, F32)] * 3)(w2, g2, m2, v2)
    return tuple(t.reshape(shape) for t in outs)


def _after(t, token):
    return t + token[0:1, 0:1].astype(t.dtype)


def _finish(name, pf, recv, w, m, v):
    if name in ("attn_w_in", "conv_w_in"):
        res = _rs_finish_adamw(pf, recv, w.T, m.T, v.T, "rs_finish_adamw_" + name)
        return tuple(t.T for t in res)
    return _rs_finish_adamw(pf, recv, w, m, v, "rs_finish_adamw_" + name)


def kernel(x, mem, positions, norm_g, mem_norm_g, w_mem_kv, attn_w_in, attn_w_out, conv_w_in, conv_w, conv_w_out, final_g, loss_target, m_norm_g, m_mem_norm_g, m_w_mem_kv, m_attn_w_in, m_attn_w_out, m_conv_w_in, m_conv_w, m_conv_w_out, m_final_g, v_norm_g, v_mem_norm_g, v_w_mem_kv, v_attn_w_in, v_attn_w_out, v_conv_w_in, v_conv_w, v_conv_w_out, v_final_g):
    px, py, pc = _position()
    me = 4 * px + 2 * py + pc
    ck_arr = jnp.stack([pc, 2 * px + py]).astype(jnp.int32)
    x, mem, pos, tgt = x[0], mem[0], positions[0], loss_target[0]

    wg_in0 = _all_gather_relay(attn_w_in[0].astype(BF16), "gather_w_in0")
    late = [attn_w_out[0].astype(BF16), conv_w_in[0].astype(BF16), conv_w_out[0].astype(BF16),
            w_mem_kv.astype(BF16).reshape(-1, w_mem_kv.shape[2]), jnp.pad(conv_w[0], ((0, 5), (0, 0)))]
    lands = [lax.dynamic_update_slice(lax.empty((N_DEV,) + t.shape, t.dtype), t[None], (me, 0, 0)) for t in late]
    late_weights, late_token = _split_start(late, lands, _plan_gather_own, 4, wg_in0, "gather_late_start")

    tabs = _rope_tables(pos)
    g0, g1 = _after(norm_g[0:1], late_token), norm_g[1:2]

    hn0, qs, ks, vs, tabs_v, qm0, z0 = _inproj_attn(x, g0, wg_in0, tabs)
    os_, ls_ = [], []
    for j, d in enumerate(DILATIONS):
        if j == 2:
            _, lands = _split_wait(late_weights, _plan_gather_own, ls_[1], "gather_late_wait")
            late_weights, late_token = _split_start([], lands, _plan_gather_pass, 3, ls_[1], "gather_late_pass_start")
        o, l = _attn_fwd(qs[j], ks[j], vs[j], d, late_token)
        os_.append(o)
        ls_.append(l)

    _, gathered = _split_wait(late_weights, _plan_gather_pass, ls_[2], "gather_late_pass_wait")
    wg_out0, wg_in1, wg_out1, wg_kv, cw_all = gathered
    w_out1 = wg_out1.reshape(-1, wg_out1.shape[2])
    n_kv = w_mem_kv.shape[1]
    w_kv = wg_kv.reshape(N_DEV, 2, n_kv, -1).transpose(1, 0, 2, 3).reshape(2, N_DEV * n_kv, -1)
    cw = cw_all[:, 0:3].transpose(1, 0, 2).reshape(3, -1)
    kv = _memkv_fwd(mem, mem_norm_g, w_kv)
    y0, y0_t, mo0, h1 = _post_attn(os_, ls_, qm0, kv[0], z0, x, wg_out0)

    hn1, bg, cg, u, qm1, z1 = _inproj_conv(h1, g1, wg_in1)
    y1, y1_t, mo1, dh2, dh2b, loss_acc, d_final_g = _post_conv_loss(
        bg, cg, u, qm1, kv[1], z1, h1, w_out1, cw, final_g.reshape(1, -1), tgt)

    d_w_out1 = _wgrad_rows(y1_t, dh2b, "wgrad_out1")
    dz1, dbg, dconv, dqm1, dkv1 = _bwd_post_conv(dh2b, w_out1, bg, cg, u, z1, qm1, kv[1], mo1, cw)
    dcg, du, dcw = _bwd_conv(dconv, cg, u, cw)
    dh1, dg1, dh1b, dproj1_t = _dgrad_norm([(dbg, 1), (dcg, 1), (du, 1), (dqm1, 1), (dz1, 1)], wg_in1, h1, g1, dh2,
                                           dh2, True, "dgrad_norm_conv")
    d_w_in1 = _wgrad_shards_t(dproj1_t, hn1, wg_in1.shape[2], "wgrad_in1")

    d_w_out0 = _wgrad_cols(y0_t, dh1b, wg_out0.shape[2], "wgrad_out0")

    names1 = ["conv_w_in", "conv_w_out", "attn_w_out"]
    grads1 = [d_w_in1, d_w_out1, d_w_out0]
    started, token = _split_start(grads1, [lax.empty((4,) + g.shape[1:], g.dtype) for g in grads1],
                                  _plan_to_sibling, 4, dg1, "rs1_sibling_start")

    gw = GROUP_WIDTH
    ones = (jnp.arange(gw)[:, None] // HEAD_DIM == jnp.arange(gw)[None, :] // HEAD_DIM).astype(BF16)
    ones = _after(ones, token)
    res = _bwd_post_attn(dh1b, wg_out0, z0, os_, ls_, qm0, kv[0], mo0, ones)
    dz0, dos, dls, dqm0, dkv0 = res[0], res[1:4], res[4:7], res[7], res[8]

    grads1, from_sibling = _split_wait(started, _plan_to_sibling, dz0, "rs1_sibling_wait")
    parts1 = [_rs_add_sibling(g, r, ck_arr, "rs_add_sibling_" + n) for g, r, n in zip(grads1, from_sibling, names1)]
    pbs1 = [pb for _, pb in parts1]
    started, token = _split_start(pbs1, [lax.empty((3,) + p.shape[1:], p.dtype) for p in pbs1],
                                  _plan_to_chips, 3, dg1, "rs1_chips_start")

    dqs, dks, dvs = [], [], []
    for j, d in enumerate(DILATIONS):
        dq, dk, dv = _attn_bwd(qs[j], ks[j], vs[j], ls_[j], dos[j], dls[j], tabs_v[j], d, token)
        dqs.append((dq, d))
        dks.append((dk, d))
        dvs.append((dv, d))
    d_w_kv, d_mem_g = _memkv_bwd(jnp.stack([dkv0, dkv1]), w_kv, mem, mem_norm_g)
    pieces0 = dqs + dks + dvs + [(dqm0, 1), (dz0, 1)]
    dproj0_t = _assemble_dproj_t(pieces0, N_DEV * wg_in0.shape[2], "assemble_dproj_attn")
    d_w_in0 = _wgrad_shards_t(dproj0_t, hn0, wg_in0.shape[2], "wgrad_in0")

    names0 = ["attn_w_in", "w_mem_kv"]
    grads0 = [d_w_in0, d_w_kv]
    started0, token0 = _split_start(grads0, [lax.empty((4,) + g.shape[1:], g.dtype) for g in grads0],
                                    _plan_to_sibling, 4, dg1, "rs0_sibling_start")
    _, from_chips1 = _split_wait(started, _plan_to_chips, token0, "rs1_chips_wait")
    shard = dict(attn_w_in=(attn_w_in[0], m_attn_w_in[0], v_attn_w_in[0]),
                 attn_w_out=(attn_w_out[0], m_attn_w_out[0], v_attn_w_out[0]),
                 conv_w_in=(conv_w_in[0], m_conv_w_in[0], v_conv_w_in[0]),
                 conv_w_out=(conv_w_out[0], m_conv_w_out[0], v_conv_w_out[0]),
                 w_mem_kv=tuple(t.reshape(-1, t.shape[2]) for t in (w_mem_kv, m_w_mem_kv, v_w_mem_kv)))
    big = {}
    for n, (pf, _), r in zip(names1, parts1, from_chips1):
        big[n] = _finish(n, pf, r, *shard[n])

    grads0, from_sibling = _split_wait(started0, _plan_to_sibling, big["conv_w_out"][1], "rs0_sibling_wait")
    parts0 = [_rs_add_sibling(g, r, ck_arr, "rs_add_sibling_" + n) for g, r, n in zip(grads0, from_sibling, names0)]
    pbs0 = [pb for _, pb in parts0]
    started0, token0 = _split_start(pbs0, [lax.empty((3,) + p.shape[1:], p.dtype) for p in pbs0],
                                    _plan_to_chips, 3, dg1, "rs0_chips_start")
    dx, dg0 = _dgrad_norm(pieces0, wg_in0, x, g0, dh1, token0, False, "dgrad_norm_attn")

    small_part = jnp.concatenate([dg0, dg1, d_mem_g.reshape(2, -1), d_final_g, dcw[0:3]], axis=0)
    small_part = jnp.concatenate([small_part, jnp.broadcast_to(loss_acc[0, 0], small_part.shape)], axis=0)
    small = _sum_devices(_all_gather([small_part], dg0, "gather_small_grads")[0])
    loss = small[8, 0]
    g_conv_w = lax.dynamic_slice(small[5:8], (0, me * LANES), (3, LANES))[None]
    small_g = dict(norm_g=small[0:2], mem_norm_g=small[2:4], conv_w=g_conv_w, final_g=small[4])
    small_w = dict(norm_g=(norm_g, m_norm_g, v_norm_g), mem_norm_g=(mem_norm_g, m_mem_norm_g, v_mem_norm_g),
                   conv_w=(conv_w, m_conv_w, v_conv_w), final_g=(final_g, m_final_g, v_final_g))
    for n, (w, m, v) in small_w.items():
        big[n] = (small_g[n],) + _adamw(w, small_g[n], m, v, "adamw_" + n)

    _, from_chips0 = _split_wait(started0, _plan_to_chips, big["final_g"][1], "rs0_chips_wait")
    for n, (pf, _), r in zip(names0, parts0, from_chips0):
        big[n] = _finish(n, pf, r, *shard[n])
    for n in ("attn_w_in", "attn_w_out", "conv_w_in", "conv_w_out"):
        big[n] = tuple(t[None] for t in big[n])
    big["w_mem_kv"] = tuple(t.reshape(w_mem_kv.shape) for t in big["w_mem_kv"])

    order = ["norm_g", "mem_norm_g", "w_mem_kv", "attn_w_in", "attn_w_out", "conv_w_in", "conv_w", "conv_w_out", "final_g"]
    return (loss, dx[None], *[big[n][0] for n in order], *[big[n][1] for n in order],
            *[big[n][2] for n in order], *[big[n][3] for n in order])
```

```python
import jax
import jax.numpy as jnp
from jax import lax
from jax.experimental import pallas as pl
from jax.experimental.pallas import tpu as pltpu

F32 = jnp.float32
BF16 = jnp.bfloat16

N_DEV = 8
D_MODEL = 1024
HEAD_DIM = 64
ROT_DIM = HEAD_DIM // 4
ROPE_THETA = 500000.0
DILATIONS = (1, 4, 16)
HEADS_PER_GROUP = 8
GROUP_WIDTH = HEADS_PER_GROUP * HEAD_DIM
BLOCK = 128
N_MEM = 256
MEM_HEADS = 4
MEM_WIDTH = MEM_HEADS * HEAD_DIM
CONV_WIDTH = D_MODEL
EPS = 1e-6
SCALE = HEAD_DIM ** -0.5
NEG = -1e30

ADAM_LR = 0.001
ADAM_B1 = 0.9
ADAM_B2 = 0.999
ADAM_EPS = 1e-08
ADAM_WD = 0.01
ADAM_STEP = 10

ROW_TILE = 256
WGRAD_SHARDS = 4
LANES = 128
MESH = pl.DeviceIdType.MESH
ANY = pl.BlockSpec(memory_space=pl.ANY)


def _pallas_call(body, **kw):
    call = pl.pallas_call(body, **kw)

    def run(*args):
        pinned = [pltpu.with_memory_space_constraint(a, pltpu.HBM) if jnp.issubdtype(a.dtype, jnp.floating) else a
                  for a in args]
        return call(*pinned)

    return run


def _dot(a, b):
    return lax.dot_general(a, b, (((1,), (0,)), ((), ())), preferred_element_type=F32)


def _dot_nt(a, b):
    return lax.dot_general(a, b, (((1,), (1,)), ((), ())), preferred_element_type=F32)


def _dot_tn(a, b):
    return lax.dot_general(a, b, (((0,), (0,)), ((), ())), preferred_element_type=F32)


def _params(n_grid, vmem_mb=48):
    return pltpu.CompilerParams(dimension_semantics=("arbitrary",) * n_grid, vmem_limit_bytes=vmem_mb << 20)


def _rows(width, tm=ROW_TILE):
    return pl.BlockSpec((tm, width), lambda i: (i, 0))


def _view_rows(width, d, tm=ROW_TILE):
    return pl.BlockSpec((tm // d, d * width), lambda i: (i, 0))


def _whole(shape):
    return pl.BlockSpec(shape, lambda *_: (0,) * len(shape))


def _resident(shape):
    return pl.BlockSpec(shape, lambda *_: (0,) * len(shape), pipeline_mode=pl.Buffered(1))


def _sds(shape, dtype):
    return pltpu.HBM(shape, dtype)


def _plain(shape, dtype):
    return jax.ShapeDtypeStruct(shape, dtype)


def _silu_parts(z):
    sg = jax.nn.sigmoid(z)
    return z * sg, sg * (1.0 + z * (1.0 - sg))


def _to_view(scr, val, out_ref, d):
    tm, w = val.shape
    if d == 1:
        out_ref[...] = val.astype(out_ref.dtype)
        return
    for cb in range(w // LANES):
        scr[cb] = val[:, cb * LANES:(cb + 1) * LANES]
    for r in range(d):
        for cb in range(w // LANES):
            lo = r * w + cb * LANES
            out_ref[:, lo:lo + LANES] = scr[cb, pl.ds(r, tm // d, stride=d), :].astype(out_ref.dtype)


def _from_view(scr, in_ref, d):
    if d == 1:
        return in_ref[...].astype(F32)
    nc, tm, _ = scr.shape
    w = nc * LANES
    for r in range(d):
        for cb in range(nc):
            lo = r * w + cb * LANES
            scr[cb, pl.ds(r, tm // d, stride=d), :] = in_ref[:, lo:lo + LANES].astype(F32)
    return jnp.concatenate([scr[cb] for cb in range(nc)], axis=1)


def _view_scratch(width, tm=ROW_TILE):
    return pltpu.VMEM((width // LANES, tm, LANES), F32)


def _rope_tables(pos):
    half = ROT_DIM // 2
    inv_freq = ROPE_THETA ** (-jnp.arange(half, dtype=F32) * (2.0 / ROT_DIM))
    ang = pos.astype(F32)[:, None] * inv_freq
    cos, sin = jnp.cos(ang), jnp.sin(ang)
    s = pos.shape[0]
    z8 = jnp.zeros((s, half), F32)
    rest = HEAD_DIM - ROT_DIM
    cosf = jnp.concatenate([cos, cos, jnp.ones((s, rest), F32)], axis=1)
    sa = jnp.concatenate([-sin, z8, jnp.zeros((s, rest), F32)], axis=1)
    sb = jnp.concatenate([z8, sin, jnp.zeros((s, rest), F32)], axis=1)
    return tuple(jnp.tile(t, (1, LANES // HEAD_DIM)) for t in (cosf, sa, sb))


def _rope_fwd(t, cv, sav, sbv):
    w = t.shape[1]
    return t * cv + pltpu.roll(t, w - ROT_DIM // 2, 1) * sav + pltpu.roll(t, ROT_DIM // 2, 1) * sbv


def _rope_bwd(g, cv, sav, sbv):
    w = g.shape[1]
    return g * cv + pltpu.roll(g * sav, ROT_DIM // 2, 1) + pltpu.roll(g * sbv, w - ROT_DIM // 2, 1)


def _joined_columns(wg_ref):
    assert wg_ref.shape[2] % LANES == 0
    return jnp.concatenate([wg_ref[j] for j in range(N_DEV)], axis=1)


def _join_once(wg_ref, w_scr):
    c = wg_ref.shape[2]

    @pl.when(pl.program_id(0) == 0)
    def _():
        for j in range(N_DEV):
            w_scr[:, j * c:(j + 1) * c] = wg_ref[j]


def _inproj_attn(x, g, wg, tabs, token):
    s, d_model = x.shape
    gw = GROUP_WIDTH
    n = N_DEV * wg.shape[2]
    nz = n - 9 * gw - MEM_WIDTH
    reps = gw // LANES
    tm = ROW_TILE

    def body(_, x_ref, g_ref, w_ref, c_ref, sa_ref, sb_ref, hn_ref, *rest):
        outs, (wj, scr, tscr) = rest[:-3], rest[-3:]
        q_refs, k_refs, v_refs, t_refs, qm_ref, z_ref = outs[0:3], outs[3:6], outs[6:9], outs[9:18], outs[18], outs[19]
        _join_once(w_ref, wj)
        xb = x_ref[...]
        r = lax.rsqrt(jnp.mean(xb * xb, axis=-1, keepdims=True) + EPS)
        hn = ((xb * r) * g_ref[...]).astype(BF16)
        hn_ref[...] = hn
        proj = lambda lo, hi: _dot(hn, wj[:, lo:hi])
        tab = (c_ref[...], sa_ref[...], sb_ref[...])
        cv, sav, sbv = [jnp.tile(t, (1, reps)) for t in tab]
        for j, d in enumerate(DILATIONS):
            tq = _rope_fwd(proj(j * gw, (j + 1) * gw), cv, sav, sbv)
            _to_view(scr, tq * SCALE, q_refs[j], d)
            tk = _rope_fwd(proj((3 + j) * gw, (4 + j) * gw), cv, sav, sbv)
            _to_view(scr, tk, k_refs[j], d)
            _to_view(scr, proj((6 + j) * gw, (7 + j) * gw), v_refs[j], d)
            for i in range(3):
                _to_view(tscr, tab[i], t_refs[3 * j + i], d)
        qm_ref[...] = proj(9 * gw, 9 * gw + MEM_WIDTH).astype(BF16)
        z_ref[...] = proj(9 * gw + MEM_WIDTH, n)

    views = [_sds((s // d, d * gw), BF16) for d in DILATIONS]
    tviews = [_sds((s // d, d * LANES), F32) for d in DILATIONS for _ in range(3)]
    out_shape = [_sds((s, d_model), BF16)] + views * 3 + tviews + [_sds((s, MEM_WIDTH), BF16), _sds((s, nz), F32)]
    vspecs = [_view_rows(gw, d, tm) for d in DILATIONS]
    tspecs = [_view_rows(LANES, d, tm) for d in DILATIONS for _ in range(3)]
    out_specs = [_rows(d_model, tm)] + vspecs * 3 + tspecs + [_rows(MEM_WIDTH, tm), _rows(nz, tm)]
    res = _pallas_call(
        body, name="inproj_attn", grid=(s // tm,), out_shape=out_shape,
        in_specs=[ANY, _rows(d_model, tm), _whole((1, d_model)), _resident(wg.shape)] + [_rows(LANES, tm)] * 3,
        out_specs=out_specs,
        scratch_shapes=[pltpu.VMEM((d_model, n), BF16), _view_scratch(gw, tm), _view_scratch(LANES, tm)],
        compiler_params=_params(1, 60),
    )(token, x, g, wg, *tabs)
    tabs_v = [res[10 + 3 * j:13 + 3 * j] for j in range(3)]
    return res[0], res[1:4], res[4:7], res[7:10], tabs_v, res[19], res[20]


def _band_mask(n_keys):
    qi = lax.broadcasted_iota(jnp.int32, (BLOCK, n_keys), 0)
    kj = lax.broadcasted_iota(jnp.int32, (BLOCK, n_keys), 1)
    if n_keys == BLOCK:
        return kj <= qi
    return jnp.logical_or(jnp.logical_and(kj < BLOCK, kj >= qi), jnp.logical_and(kj >= BLOCK, (kj - BLOCK) <= qi))


def _low_head_lanes():
    return lax.broadcasted_iota(jnp.int32, (1, LANES), 1) < HEAD_DIM


def _split_pair(t, low):
    zero = jnp.zeros_like(t)
    return jnp.where(low, t, zero), jnp.where(low, zero, t)


def _pair_specs(d, nb, w):
    if nb == 1:
        return pl.BlockSpec((BLOCK, 2 * w), lambda n: (0, n)), None
    half = nb // 2
    two = pl.BlockSpec((2 * BLOCK, w), lambda n: (n % half, n // half))
    before = pl.BlockSpec((BLOCK, w), lambda n: (jnp.maximum(2 * (n % half) - 1, 0), n // half))
    return two, before


def _head_tiles(w, col0):
    return ([slice(p * LANES, (p + 1) * LANES) for p in range(w // LANES)],
            [slice(col0 + p * LANES, col0 + (p + 1) * LANES) for p in range(w // LANES)])


def _attend_fwd(q_ref, o_ref, lse_ref, rows, col0, kk, vv):
    w = kk.shape[1]
    valid = _band_mask(kk.shape[0])
    low = _low_head_lanes()
    pairs, qcols = _head_tiles(w, col0)
    qs_ = [h for qc in qcols for h in _split_pair(q_ref[rows, qc], low)]
    k2s = [kk[:, pr] for pr in pairs for _ in range(2)]
    scs = [jnp.where(valid, _dot_nt(qh, k2), NEG) for qh, k2 in zip(qs_, k2s)]
    ms = [jnp.max(sc, axis=-1, keepdims=True) for sc in scs]
    ps = [jnp.exp(sc - m) for sc, m in zip(scs, ms)]
    ls = [jnp.sum(p, axis=-1, keepdims=True) for p in ps]
    pns = [(p * (1.0 / l)).astype(BF16) for p, l in zip(ps, ls)]
    for i, (pr, qc) in enumerate(zip(pairs, qcols)):
        v2 = vv[:, pr]
        a, b = 2 * i, 2 * i + 1
        o_ref[rows, qc] = jnp.where(low, _dot(pns[a], v2), _dot(pns[b], v2))
        lse_ref[rows, qc] = jnp.where(low, ms[a] + jnp.log(ls[a]), ms[b] + jnp.log(ls[b]))


TOP, BOTTOM = slice(0, BLOCK), slice(BLOCK, 2 * BLOCK)


def _attn_fwd(q, k, v, d, token):
    ln, dw = q.shape
    w = dw // d
    nb = ln // BLOCK
    two, before = _pair_specs(d, nb, w)

    def body_streams(_, q_ref, kc_ref, vc_ref, o_ref, lse_ref):
        for sb in range(2):
            cols = slice(sb * w, (sb + 1) * w)
            _attend_fwd(q_ref, o_ref, lse_ref, TOP, sb * w, kc_ref[:, cols], vc_ref[:, cols])

    def body_blocks(_, q_ref, kp_ref, kc_ref, vp_ref, vc_ref, o_ref, lse_ref):
        first = pl.program_id(0) % (nb // 2) == 0
        pl.when(first)(lambda: _attend_fwd(q_ref, o_ref, lse_ref, TOP, 0, kc_ref[TOP, :], vc_ref[TOP, :]))
        pl.when(jnp.logical_not(first))(lambda: _attend_fwd(
            q_ref, o_ref, lse_ref, TOP, 0, jnp.concatenate([kp_ref[...], kc_ref[TOP, :]], axis=0),
            jnp.concatenate([vp_ref[...], vc_ref[TOP, :]], axis=0)))
        _attend_fwd(q_ref, o_ref, lse_ref, BOTTOM, 0, kc_ref[...], vc_ref[...])

    if nb == 1:
        body, in_specs, args = body_streams, [ANY, two, two, two], (token, q, k, v)
    else:
        body, in_specs, args = body_blocks, [ANY, two, before, two, before, two], (token, q, k, k, v, v)
    return _pallas_call(
        body, name=f"attn_fwd_d{d}", grid=(d * nb // 2,), out_shape=[_sds((ln, dw), F32)] * 2,
        in_specs=in_specs, out_specs=[two, two], compiler_params=_params(1, 32),
    )(*args)


def _memkv_fwd(mem, g, w):
    n_layers = g.shape[0]
    rows = D_MODEL // N_DEV

    def body(mem_ref, g_ref, w_ref, kv_ref):
        mb = mem_ref[...]
        r = lax.rsqrt(jnp.mean(mb * mb, axis=-1, keepdims=True) + EPS)
        mn = ((mb * r) * g_ref[...]).astype(BF16)
        kv_ref[...] = _dot(mn, w_ref[...].reshape(D_MODEL, 2 * MEM_WIDTH)).astype(BF16)

    return _pallas_call(
        body, name="memkv_fwd", grid=(n_layers,),
        out_shape=_plain((n_layers, N_MEM, 2 * MEM_WIDTH), BF16),
        in_specs=[_whole(mem.shape), pl.BlockSpec((None, 1, D_MODEL), lambda l: (l, 0, 0)),
                  pl.BlockSpec((N_DEV, rows, 2 * MEM_WIDTH), lambda l: (0, l, 0))],
        out_specs=pl.BlockSpec((None, N_MEM, 2 * MEM_WIDTH), lambda l: (l, 0, 0)),
        compiler_params=_params(1, 32),
    )(mem, g.reshape(n_layers, 1, D_MODEL), w)


def _mix_groups(os_, ls_):
    mx = jnp.maximum(jnp.maximum(ls_[0], ls_[1]), ls_[2])
    es = [jnp.exp(t - mx) for t in ls_]
    inv = 1.0 / (es[0] + es[1] + es[2])
    ws = [e * inv for e in es]
    mix = ws[0] * os_[0] + ws[1] * os_[1] + ws[2] * os_[2]
    return ws, mix


MEM_PAIRS = [slice(p * LANES, (p + 1) * LANES) for p in range(MEM_WIDTH // LANES)]


def _mem_probs(qhs, k2s):
    scs = [_dot_nt(qh, k2) * SCALE for qh, k2 in zip(qhs, k2s)]
    es = [jnp.exp(sc - jnp.max(sc, axis=-1, keepdims=True)) for sc in scs]
    return [e * (1.0 / jnp.sum(e, axis=-1, keepdims=True)) for e in es]


def _mem_attn_into(qm, kv_ref, mo_ref):
    low = _low_head_lanes()
    qhs = [h for pr in MEM_PAIRS for h in _split_pair(qm[:, pr], low)]
    k2s = [kv_ref[:, pr] for pr in MEM_PAIRS for _ in range(2)]
    ps = [p.astype(BF16) for p in _mem_probs(qhs, k2s)]
    for i, pr in enumerate(MEM_PAIRS):
        v2 = kv_ref[:, MEM_WIDTH + i * LANES:MEM_WIDTH + (i + 1) * LANES]
        mo_ref[:, pr] = jnp.where(low, _dot(ps[2 * i], v2), _dot(ps[2 * i + 1], v2))


def _mem_attn_bwd(qm, kv_ref, dmem, dqm_ref, dkv_ref):
    low = _low_head_lanes()
    dmb = dmem.astype(BF16)
    vps = [slice(MEM_WIDTH + i * LANES, MEM_WIDTH + (i + 1) * LANES) for i in range(len(MEM_PAIRS))]
    qhs = [h for pr in MEM_PAIRS for h in _split_pair(qm[:, pr], low)]
    dhs = [h for pr in MEM_PAIRS for h in _split_pair(dmb[:, pr], low)]
    k2s = [kv_ref[:, pr] for pr in MEM_PAIRS for _ in range(2)]
    v2s = [kv_ref[:, vp] for vp in vps for _ in range(2)]
    ps = _mem_probs(qhs, k2s)
    dps = [_dot_nt(dh, v2) for dh, v2 in zip(dhs, v2s)]
    dss = [(p * (dp - jnp.sum(dp * p, axis=-1, keepdims=True)) * SCALE).astype(BF16) for p, dp in zip(ps, dps)]
    pbs = [p.astype(BF16) for p in ps]
    for i, (pr, vp) in enumerate(zip(MEM_PAIRS, vps)):
        a, b = 2 * i, 2 * i + 1
        dqm_ref[:, pr] = jnp.where(low, _dot(dss[a], k2s[a]), _dot(dss[b], k2s[b])).astype(BF16)
        dkv_ref[:, pr] += _dot_tn(dss[a], qhs[a]) + _dot_tn(dss[b], qhs[b])
        dkv_ref[:, vp] += _dot_tn(pbs[a], dhs[a]) + _dot_tn(pbs[b], dhs[b])


def _post_attn(os_, ls_, qm, kv, z, x, wg_out):
    s, d_model = x.shape
    gw = GROUP_WIDTH
    nb = gw + MEM_WIDTH
    tm = ROW_TILE

    def body(o0, o1, o2, l0, l1, l2, qm_ref, kv_ref, z_ref, x_ref, w_ref, y_ref, yt_ref, mo_ref, h_ref, s0, s1):
        ov, lv = [], []
        for o_ref, l_ref, d in zip((o0, o1, o2), (l0, l1, l2), DILATIONS):
            ov.append(_from_view(s0, o_ref, d))
            lv.append(_from_view(s1, l_ref, d))
        _, mix = _mix_groups(ov, lv)
        _mem_attn_into(qm_ref[...], kv_ref, mo_ref)
        sz, _ = _silu_parts(z_ref[...])
        y_ref[:, :gw] = (mix * sz[:, :gw]).astype(BF16)
        y_ref[:, gw:] = (mo_ref[...] * sz[:, gw:]).astype(BF16)
        y = y_ref[...]
        yt_ref[...] = y.T
        h_ref[...] = x_ref[...] + _dot(y, _joined_columns(w_ref))

    vspecs = [_view_rows(gw, d) for d in DILATIONS]
    return _pallas_call(
        body, name="post_attn", grid=(s // tm,),
        out_shape=[_sds((s, nb), BF16), _sds((nb, s), BF16), _sds((s, MEM_WIDTH), F32), _sds((s, d_model), F32)],
        in_specs=vspecs * 2 + [_rows(MEM_WIDTH), _whole(kv.shape), _rows(nb), _rows(d_model), _whole(wg_out.shape)],
        out_specs=[_rows(nb), pl.BlockSpec((nb, tm), lambda i: (0, i)), _rows(MEM_WIDTH), _rows(d_model)],
        scratch_shapes=[_view_scratch(gw), _view_scratch(gw)],
        compiler_params=_params(1, 40),
    )(*os_, *ls_, qm, kv, z, x, wg_out)


def _inproj_conv(x, g, wg):
    s, d_model = x.shape
    c = CONV_WIDTH
    n = N_DEV * wg.shape[2]
    nz = n - 3 * c - MEM_WIDTH
    tm = ROW_TILE

    def body(x_ref, g_ref, w_ref, hn_ref, bg_ref, cg_ref, u_ref, qm_ref, z_ref, wj):
        _join_once(w_ref, wj)
        xb = x_ref[...]
        r = lax.rsqrt(jnp.mean(xb * xb, axis=-1, keepdims=True) + EPS)
        hn = ((xb * r) * g_ref[...]).astype(BF16)
        hn_ref[...] = hn
        bg_ref[...] = _dot(hn, wj[:, 0:c])
        cg_ref[...] = _dot(hn, wj[:, c:2 * c])
        u_ref[...] = _dot(hn, wj[:, 2 * c:3 * c])
        qm_ref[...] = _dot(hn, wj[:, 3 * c:3 * c + MEM_WIDTH]).astype(BF16)
        z_ref[...] = _dot(hn, wj[:, 3 * c + MEM_WIDTH:])

    return _pallas_call(
        body, name="inproj_conv", grid=(s // tm,),
        out_shape=[_sds((s, d_model), BF16)] + [_sds((s, c), F32)] * 3 + [_sds((s, MEM_WIDTH), BF16), _sds((s, nz), F32)],
        in_specs=[_rows(d_model), _whole((1, d_model)), _resident(wg.shape)],
        out_specs=[_rows(d_model)] + [_rows(c)] * 3 + [_rows(MEM_WIDTH), _rows(nz)],
        scratch_shapes=[pltpu.VMEM((d_model, n), BF16)],
        compiler_params=_params(1, 60),
    )(x, g, wg)


HALO = 8


def _halo_before(width, tm=ROW_TILE):
    return pl.BlockSpec((HALO, width), lambda i: (jnp.maximum(i * (tm // HALO) - 1, 0), 0))


def _halo_after(width, n_rows, tm=ROW_TILE):
    return pl.BlockSpec((HALO, width), lambda i: (jnp.minimum((i + 1) * (tm // HALO), n_rows // HALO - 1), 0))


def _conv_taps(cg_ref, u_ref, cgh_ref, uh_ref, i):
    a = cg_ref[...] * u_ref[...]
    ah = jnp.where(i > 0, cgh_ref[...] * uh_ref[...], 0.0)
    row = lax.broadcasted_iota(jnp.int32, a.shape, 0)
    a1 = jnp.where(row == 0, ah[HALO - 1:HALO], pltpu.roll(a, 1, 0))
    a2 = jnp.where(row == 0, ah[HALO - 2:HALO - 1], jnp.where(row == 1, ah[HALO - 1:HALO], pltpu.roll(a, 2, 0)))
    return a, a1, a2


def _post_conv_loss(bg, cg, u, qm, kv, z, h1, w_out, cw, gf, tgt):
    s, d = h1.shape
    c = CONV_WIDTH
    nb = c + MEM_WIDTH
    tm = ROW_TILE

    def body(bg_ref, cg_ref, u_ref, cgh_ref, uh_ref, qm_ref, kv_ref, z_ref, h_ref, w_ref, cw_ref, gf_ref, t_ref,
             y_ref, yt_ref, mo_ref, dh_ref, dhb_ref, loss_ref, dgf_ref):
        i = pl.program_id(0)
        a, a1, a2 = _conv_taps(cg_ref, u_ref, cgh_ref, uh_ref, i)
        conv = cw_ref[0:1, :] * a2 + cw_ref[1:2, :] * a1 + cw_ref[2:3, :] * a
        mix = bg_ref[...] * conv
        _mem_attn_into(qm_ref[...], kv_ref, mo_ref)
        sz, _ = _silu_parts(z_ref[...])
        y_ref[:, :c] = (mix * sz[:, :c]).astype(BF16)
        y_ref[:, c:] = (mo_ref[...] * sz[:, c:]).astype(BF16)
        y = y_ref[...]
        yt_ref[...] = y.T
        h2 = h_ref[...] + _dot(y, w_ref[...])
        r = lax.rsqrt(jnp.mean(h2 * h2, axis=-1, keepdims=True) + EPS)
        nh = h2 * r
        gfv = gf_ref[...]
        diff = nh * gfv - t_ref[...]
        dout = diff * (1.0 / d)
        dn = dout * gfv
        dh2 = r * dn - h2 * ((r * r * r) * jnp.mean(dn * h2, axis=-1, keepdims=True))
        dh_ref[...] = dh2
        dhb_ref[...] = dh2.astype(BF16)

        @pl.when(i == 0)
        def _():
            loss_ref[...] = jnp.zeros_like(loss_ref)
            dgf_ref[...] = jnp.zeros_like(dgf_ref)

        loss_ref[...] += 0.5 * jnp.sum(jnp.mean(diff * diff, axis=-1, keepdims=True))
        dgf_ref[...] += jnp.sum(dout * nh, axis=0, keepdims=True)

    return _pallas_call(
        body, name="post_conv_loss", grid=(s // tm,),
        out_shape=[_sds((s, nb), BF16), _sds((nb, s), BF16), _sds((s, MEM_WIDTH), F32), _sds((s, d), F32),
                   _sds((s, d), BF16), _plain((8, LANES), F32), _plain((1, d), F32)],
        in_specs=[_rows(c)] * 3 + [_halo_before(c)] * 2 + [_rows(MEM_WIDTH), _whole(kv.shape), _rows(nb), _rows(d),
                  _whole(w_out.shape), _whole(cw.shape), _whole((1, d)), _rows(d)],
        out_specs=[_rows(nb), pl.BlockSpec((nb, tm), lambda i: (0, i)), _rows(MEM_WIDTH), _rows(d), _rows(d),
                   _whole((8, LANES)), _whole((1, d))],
        compiler_params=_params(1, 48),
    )(bg, cg, u, cg, u, qm, kv, z, h1, w_out, cw, gf, tgt)


def _bwd_post_conv(dhb, w_out, bg, cg, u, z, qm, kv, mo, cw):
    s = dhb.shape[0]
    c = CONV_WIDTH
    nb = c + MEM_WIDTH

    def body(dh_ref, w_ref, bg_ref, cg_ref, u_ref, cgh_ref, uh_ref, z_ref, qm_ref, kv_ref, mo_ref, cw_ref,
             dz_ref, dbg_ref, dc_ref, dqm_ref, dkv_ref):
        i = pl.program_id(0)

        @pl.when(i == 0)
        def _():
            dkv_ref[...] = jnp.zeros_like(dkv_ref)

        dy = _dot_nt(dh_ref[...], w_ref[...])
        sz, dsz = _silu_parts(z_ref[...])
        a, a1, a2 = _conv_taps(cg_ref, u_ref, cgh_ref, uh_ref, i)
        conv = cw_ref[0:1, :] * a2 + cw_ref[1:2, :] * a1 + cw_ref[2:3, :] * a
        bgv = bg_ref[...]
        dz_ref[:, :c] = (dy[:, :c] * (bgv * conv) * dsz[:, :c]).astype(BF16)
        dz_ref[:, c:] = (dy[:, c:] * mo_ref[...] * dsz[:, c:]).astype(BF16)
        dbr = dy * sz
        dmix = dbr[:, :c]
        dbg_ref[...] = (dmix * conv).astype(BF16)
        dc_ref[...] = dmix * bgv
        _mem_attn_bwd(qm_ref[...], kv_ref, dbr[:, c:], dqm_ref, dkv_ref)

    return _pallas_call(
        body, name="bwd_post_conv", grid=(s // ROW_TILE,),
        out_shape=[_sds((s, nb), BF16), _sds((s, c), BF16), _sds((s, c), F32), _sds((s, MEM_WIDTH), BF16),
                   _plain(kv.shape, F32)],
        in_specs=[_rows(D_MODEL), _whole(w_out.shape)] + [_rows(c)] * 3 + [_halo_before(c)] * 2
                 + [_rows(nb), _rows(MEM_WIDTH), _whole(kv.shape), _rows(MEM_WIDTH), _whole(cw.shape)],
        out_specs=[_rows(nb), _rows(c), _rows(c), _rows(MEM_WIDTH), _whole(kv.shape)],
        compiler_params=_params(1, 48),
    )(dhb, w_out, bg, cg, u, cg, u, z, qm, kv, mo, cw)


def _bwd_conv(dconv, cg, u, cw):
    s, c = dconv.shape
    tm = ROW_TILE
    last = s // tm - 1

    def body(dc_ref, dcn_ref, cg_ref, u_ref, cgh_ref, uh_ref, cw_ref, dcg_ref, du_ref, dcw_ref):
        i = pl.program_id(0)

        @pl.when(i == 0)
        def _():
            dcw_ref[...] = jnp.zeros_like(dcw_ref)

        dc = dc_ref[...]
        dcn = jnp.where(i < last, dcn_ref[...], 0.0)
        row = lax.broadcasted_iota(jnp.int32, dc.shape, 0)
        d1 = jnp.where(row == tm - 1, dcn[0:1], pltpu.roll(dc, tm - 1, 0))
        d2 = jnp.where(row == tm - 1, dcn[1:2], jnp.where(row == tm - 2, dcn[0:1], pltpu.roll(dc, tm - 2, 0)))
        da = cw_ref[2:3, :] * dc + cw_ref[1:2, :] * d1 + cw_ref[0:1, :] * d2
        a, a1, a2 = _conv_taps(cg_ref, u_ref, cgh_ref, uh_ref, i)
        dcg_ref[...] = (da * u_ref[...]).astype(BF16)
        du_ref[...] = (da * cg_ref[...]).astype(BF16)
        dcw_ref[0:1, :] += jnp.sum(dc * a2, axis=0, keepdims=True)
        dcw_ref[1:2, :] += jnp.sum(dc * a1, axis=0, keepdims=True)
        dcw_ref[2:3, :] += jnp.sum(dc * a, axis=0, keepdims=True)

    return _pallas_call(
        body, name="bwd_conv", grid=(s // tm,),
        out_shape=[_sds((s, c), BF16), _sds((s, c), BF16), _plain((8, c), F32)],
        in_specs=[_rows(c), _halo_after(c, s), _rows(c), _rows(c), _halo_before(c), _halo_before(c), _whole(cw.shape)],
        out_specs=[_rows(c), _rows(c), _whole((8, c))], compiler_params=_params(1, 40),
    )(dconv, dconv, cg, u, cg, u, cw)


def _assemble(p_refs, pieces, widths, dp, scr):
    off = 0
    for p_ref, (_, d), wd in zip(p_refs, pieces, widths):
        if d == 1:
            dp[:, off:off + wd] = p_ref[...]
        else:
            dp[:, off:off + wd] = _from_view(scr, p_ref, d).astype(BF16)
        off += wd


def _dgrad_norm(pieces, wg, h, g, dres, token, onward, name):
    s, d_model = h.shape
    n = N_DEV * wg.shape[2]
    tm = ROW_TILE
    widths = [p.shape[1] // d for p, d in pieces]
    assert sum(widths) == n
    n_p = len(pieces)

    def body(_, *refs):
        p_refs = refs[:n_p]
        w_ref, h_ref, g_ref, dr_ref, dh_ref, dg_ref = refs[n_p:n_p + 6]
        dp, scr, wj = refs[-3:]
        _join_once(w_ref, wj)

        @pl.when(pl.program_id(0) == 0)
        def _():
            dg_ref[...] = jnp.zeros_like(dg_ref)

        _assemble(p_refs, pieces, widths, dp, scr)
        dhn = _dot_nt(dp[...], wj[...])
        hb = h_ref[...]
        r = lax.rsqrt(jnp.mean(hb * hb, axis=-1, keepdims=True) + EPS)
        dg_ref[...] += jnp.sum(dhn * (hb * r), axis=0, keepdims=True)
        dn = dhn * g_ref[...]
        dh = dr_ref[...] + r * dn - hb * ((r * r * r) * jnp.mean(dn * hb, axis=-1, keepdims=True))
        dh_ref[...] = dh
        if onward:
            dhb_ref, dpt_ref = refs[n_p + 6:n_p + 8]
            dhb_ref[...] = dh.astype(BF16)
            dpt_ref[...] = dp[...].T

    p_specs = [_view_rows(wd, d) for (_, d), wd in zip(pieces, widths)]
    out_shape = [_plain((s, d_model), F32), _plain((1, d_model), F32)]
    out_specs = [_rows(d_model), _whole((1, d_model))]
    if onward:
        out_shape += [_sds((s, d_model), BF16), _sds((n, s), BF16)]
        out_specs += [_rows(d_model), pl.BlockSpec((n, tm), lambda i: (0, i))]
    return _pallas_call(
        body, name=name, grid=(s // tm,), out_shape=out_shape,
        in_specs=[ANY] + p_specs + [_resident(wg.shape), _rows(d_model), _whole((1, d_model)), _rows(d_model)],
        out_specs=out_specs,
        scratch_shapes=[pltpu.VMEM((tm, n), BF16), _view_scratch(GROUP_WIDTH), pltpu.VMEM((d_model, n), BF16)],
        compiler_params=_params(1, 60),
    )(token, *[p for p, _ in pieces], wg, h, g, dres)


def _assemble_dproj_t(pieces, n, name):
    tm = ROW_TILE
    widths = [p.shape[1] // d for p, d in pieces]
    assert sum(widths) == n
    s = pieces[0][0].shape[0] * pieces[0][1]
    n_p = len(pieces)

    def body(*refs):
        p_refs, (dpt_ref, dp, scr) = refs[:n_p], refs[n_p:]
        _assemble(p_refs, pieces, widths, dp, scr)
        dpt_ref[...] = dp[...].T

    return _pallas_call(
        body, name=name, grid=(s // tm,), out_shape=_sds((n, s), BF16),
        in_specs=[_view_rows(wd, d) for (_, d), wd in zip(pieces, widths)],
        out_specs=pl.BlockSpec((n, tm), lambda i: (0, i)),
        scratch_shapes=[pltpu.VMEM((tm, n), BF16), _view_scratch(GROUP_WIDTH)],
        compiler_params=_params(1, 40),
    )(*[p for p, _ in pieces])


def _wgrad_shards_t(dp_t, h, c, name):
    n, s = dp_t.shape
    d_model = h.shape[1]
    per_step = 2

    def body(a_ref, b_ref, o_ref):
        o_ref[...] = _dot(a_ref[...], b_ref[...]).astype(BF16).reshape(per_step, c, d_model)

    return _pallas_call(
        body, name=name, grid=(N_DEV // per_step,), out_shape=_sds((N_DEV, c, d_model), BF16),
        in_specs=[pl.BlockSpec((per_step * c, s), lambda j: (j, 0)), _resident(h.shape)],
        out_specs=pl.BlockSpec((per_step, c, d_model), lambda j: (j, 0, 0)), compiler_params=_params(1, 40),
    )(dp_t, h)


def _wgrad_cols(a_t, b, c, name):
    m, s = a_t.shape
    assert c % LANES == 0

    def body(a_ref, b_ref, o_ref):
        wide = _dot(a_ref[...], b_ref[...]).astype(BF16)
        for j in range(WGRAD_SHARDS):
            o_ref[j] = wide[:, j * c:(j + 1) * c]

    return _pallas_call(
        body, name=name, grid=(N_DEV // WGRAD_SHARDS,), out_shape=_sds((N_DEV, m, c), BF16),
        in_specs=[_whole(a_t.shape), pl.BlockSpec((s, WGRAD_SHARDS * c), lambda j: (0, j))],
        out_specs=pl.BlockSpec((WGRAD_SHARDS, m, c), lambda j: (j, 0, 0)), compiler_params=_params(1, 40),
    )(a_t, b)


def _wgrad_rows(a_t, b, name):
    m, s = a_t.shape
    n = b.shape[1]
    mr = m // N_DEV

    def body(a_ref, b_ref, o_ref):
        o_ref[...] = _dot(a_ref[...], b_ref[...]).astype(BF16).reshape(WGRAD_SHARDS, mr, n)

    return _pallas_call(
        body, name=name, grid=(N_DEV // WGRAD_SHARDS,), out_shape=_sds((N_DEV, mr, n), BF16),
        in_specs=[pl.BlockSpec((WGRAD_SHARDS * mr, s), lambda j: (j, 0)), _whole(b.shape)],
        out_specs=pl.BlockSpec((WGRAD_SHARDS, mr, n), lambda j: (j, 0, 0)), compiler_params=_params(1, 40),
    )(a_t, b)


def _memkv_bwd(dkv, w, mem, g):
    n_layers = g.shape[0]
    rows = D_MODEL // N_DEV

    def body(dkv_ref, w_ref, mem_ref, g_ref, dw_ref, dg_ref):
        mb = mem_ref[...]
        r = lax.rsqrt(jnp.mean(mb * mb, axis=-1, keepdims=True) + EPS)
        nm = mb * r
        mn = (nm * g_ref[...]).astype(BF16)
        dkvb = dkv_ref[...].astype(BF16)
        dw_ref[...] = _dot_tn(mn, dkvb).astype(BF16).reshape(N_DEV, rows, 2 * MEM_WIDTH)
        dmn = _dot_nt(dkvb, w_ref[...].reshape(D_MODEL, 2 * MEM_WIDTH))
        dg_ref[...] = jnp.sum(dmn * nm, axis=0, keepdims=True)

    lay = lambda *shape: pl.BlockSpec((None,) + shape, lambda l: (l, 0, 0))
    major = pl.BlockSpec((N_DEV, rows, 2 * MEM_WIDTH), lambda l: (0, l, 0))
    return _pallas_call(
        body, name="memkv_bwd", grid=(n_layers,),
        out_shape=[_sds((N_DEV, n_layers * rows, 2 * MEM_WIDTH), BF16), _plain((n_layers, 1, D_MODEL), F32)],
        in_specs=[lay(N_MEM, 2 * MEM_WIDTH), major, _whole(mem.shape), lay(1, D_MODEL)],
        out_specs=[major, lay(1, D_MODEL)],
        compiler_params=_params(1, 32),
    )(dkv, w, mem, g.reshape(n_layers, 1, D_MODEL))


def _bwd_post_attn(dhb, wg_out, z, os_, ls_, qm, kv, mo, head_ones, token):
    s = dhb.shape[0]
    gw = GROUP_WIDTH
    nb = gw + MEM_WIDTH
    tm = ROW_TILE

    def body(_, dh_ref, w_ref, z_ref, o0, o1, o2, l0, l1, l2, qm_ref, kv_ref, mo_ref, bd_ref,
             dz_ref, do0, do1, do2, dl0, dl1, dl2, dqm_ref, dkv_ref, s0, s1):
        @pl.when(pl.program_id(0) == 0)
        def _():
            dkv_ref[...] = jnp.zeros_like(dkv_ref)

        dy = _dot_nt(dh_ref[...], _joined_columns(w_ref))
        ov, lv = [], []
        for o_ref, l_ref, d in zip((o0, o1, o2), (l0, l1, l2), DILATIONS):
            ov.append(_from_view(s0, o_ref, d))
            lv.append(_from_view(s1, l_ref, d))
        ws, mix = _mix_groups(ov, lv)
        sz, dsz = _silu_parts(z_ref[...])
        dz_ref[:, :gw] = (dy[:, :gw] * mix * dsz[:, :gw]).astype(BF16)
        dz_ref[:, gw:] = (dy[:, gw:] * mo_ref[...] * dsz[:, gw:]).astype(BF16)
        dbr = dy * sz
        dmix = dbr[:, :gw]
        t = dmix * mix
        th = t.astype(BF16)
        tl = (t - th.astype(F32)).astype(BF16)
        rs = _dot(th, bd_ref[...]) + _dot(tl, bd_ref[...])
        for wg_, do_ref, dl_ref, d in zip(ws, (do0, do1, do2), (dl0, dl1, dl2), DILATIONS):
            _to_view(s0, wg_ * dmix, do_ref, d)
            _to_view(s1, wg_ * rs, dl_ref, d)
        _mem_attn_bwd(qm_ref[...], kv_ref, dbr[:, gw:], dqm_ref, dkv_ref)

    vspecs = [_view_rows(gw, d) for d in DILATIONS]
    return _pallas_call(
        body, name="bwd_post_attn", grid=(s // tm,),
        out_shape=[_sds((s, nb), BF16)] + [_sds((s // d, d * gw), BF16) for d in DILATIONS]
                  + [_sds((s // d, d * gw), F32) for d in DILATIONS] + [_sds((s, MEM_WIDTH), BF16), _plain(kv.shape, F32)],
        in_specs=[ANY, _rows(D_MODEL), _whole(wg_out.shape), _rows(nb)] + vspecs * 2
                 + [_rows(MEM_WIDTH), _whole(kv.shape), _rows(MEM_WIDTH), _whole(head_ones.shape)],
        out_specs=[_rows(nb)] + vspecs * 2 + [_rows(MEM_WIDTH), _whole(kv.shape)],
        scratch_shapes=[_view_scratch(gw), _view_scratch(gw)],
        compiler_params=_params(1, 48),
    )(token, dhb, wg_out, z, *os_, *ls_, qm, kv, mo, head_ones)


def _attn_bwd(q, k, v, lse, do, dl, tabs, d, token):
    ln, dw = q.shape
    w = dw // d
    nb = ln // BLOCK
    reps = w // LANES
    two, before = _pair_specs(d, nb, w)
    two_t, _ = _pair_specs(d, nb, LANES)

    def attend(q_ref, l_ref, do_ref, dl_ref, dqs, acck, accv, rows, col0, kk, vv, acc_rows):
        valid = _band_mask(kk.shape[0])
        low = _low_head_lanes()
        pairs, qcols = _head_tiles(w, col0)
        cols = [slice(col0 + h * HEAD_DIM, col0 + h * HEAD_DIM + 1) for h in range(HEADS_PER_GROUP)]
        qhs = [h for qc in qcols for h in _split_pair(q_ref[rows, qc], low)]
        dobs = [h for qc in qcols for h in _split_pair(do_ref[rows, qc], low)]
        k2s = [kk[:, pr] for pr in pairs for _ in range(2)]
        v2s = [vv[:, pr] for pr in pairs for _ in range(2)]
        scs = [jnp.where(valid, _dot_nt(qh, k2), NEG) for qh, k2 in zip(qhs, k2s)]
        dps = [_dot_nt(dob, v2) for dob, v2 in zip(dobs, v2s)]
        ps = [jnp.exp(sc - l_ref[rows, col]) for sc, col in zip(scs, cols)]
        dss = [(p * (dp - dl_ref[rows, col])).astype(BF16) for p, dp, col in zip(ps, dps, cols)]
        pbs = [p.astype(BF16) for p in ps]
        for i, qc in enumerate(qcols):
            a, b = 2 * i, 2 * i + 1
            dqs[rows, qc] = jnp.where(low, _dot(dss[a], k2s[a]), _dot(dss[b], k2s[b])) * SCALE
            acck[acc_rows, qc] += _dot_tn(dss[a], qhs[a]) + _dot_tn(dss[b], qhs[b])
            accv[acc_rows, qc] += _dot_tn(pbs[a], dobs[a]) + _dot_tn(pbs[b], dobs[b])

    def body_streams(_, q_ref, kc_ref, vc_ref, l_ref, do_ref, dl_ref, c_ref, sa_ref, sb_ref,
                     dq_ref, dk_ref, dv_ref, acck, accv, dqs):
        acck[...] = jnp.zeros_like(acck)
        accv[...] = jnp.zeros_like(accv)
        for sb in range(2):
            cols = slice(sb * w, (sb + 1) * w)
            attend(q_ref, l_ref, do_ref, dl_ref, dqs, acck, accv, TOP, sb * w, kc_ref[:, cols], vc_ref[:, cols], TOP)
        tabs2 = [jnp.concatenate([jnp.tile(r[:, sb * LANES:(sb + 1) * LANES], (1, reps)) for sb in range(2)], axis=1)
                 for r in (c_ref, sa_ref, sb_ref)]
        dq_ref[...] = _rope_bwd(dqs[...], *tabs2).astype(BF16)
        dk_ref[...] = _rope_bwd(acck[...], *tabs2).astype(BF16)
        dv_ref[...] = accv[...].astype(BF16)

    def body_blocks(_, q_ref, kp_ref, kc_ref, vp_ref, vc_ref, l_ref, do_ref, dl_ref, cq, saq, sbq, ck, sak, sbk,
                    dq_ref, dk_ref, dv_ref, acck, accv, dqs):
        i = pl.program_id(0) % (nb // 2)

        @pl.when(i == 0)
        def _():
            acck[...] = jnp.zeros_like(acck)
            accv[...] = jnp.zeros_like(accv)

        refs = (q_ref, l_ref, do_ref, dl_ref, dqs, acck, accv)
        pl.when(i == 0)(lambda: attend(*refs, TOP, 0, kc_ref[TOP, :], vc_ref[TOP, :], TOP))
        pl.when(i != 0)(lambda: attend(
            *refs, TOP, 0, jnp.concatenate([kp_ref[...], kc_ref[TOP, :]], axis=0),
            jnp.concatenate([vp_ref[...], vc_ref[TOP, :]], axis=0),
            pl.ds(pl.multiple_of((2 * i - 1) * BLOCK, BLOCK), 2 * BLOCK)))
        attend(*refs, BOTTOM, 0, kc_ref[...], vc_ref[...], pl.ds(pl.multiple_of(2 * i * BLOCK, BLOCK), 2 * BLOCK))
        tq = [jnp.tile(r[...], (1, reps)) for r in (cq, saq, sbq)]
        dq_ref[...] = _rope_bwd(dqs[...], *tq).astype(BF16)

        @pl.when(i == nb // 2 - 1)
        def _():
            for r0 in range(0, nb * BLOCK, 2 * BLOCK):
                rows = slice(r0, r0 + 2 * BLOCK)
                tk = [jnp.tile(r[rows, :], (1, reps)) for r in (ck, sak, sbk)]
                dk_ref[rows, :] = _rope_bwd(acck[rows, :], *tk).astype(BF16)
                dv_ref[rows, :] = accv[rows, :].astype(BF16)

    if nb == 1:
        body = body_streams
        in_specs = [ANY] + [two] * 6 + [two_t] * 3
        args = (token, q, k, v, lse, do, dl, *tabs)
        out_specs = [two, two, two]
        acc_shape = (BLOCK, 2 * w)
    else:
        body = body_blocks
        stream = pl.BlockSpec((nb * BLOCK, w), lambda n: (0, n // (nb // 2)))
        stream_t = pl.BlockSpec((nb * BLOCK, LANES), lambda n: (0, n // (nb // 2)))
        in_specs = [ANY, two, before, two, before, two, two, two, two] + [two_t] * 3 + [stream_t] * 3
        args = (token, q, k, k, v, v, lse, do, dl, *tabs, *tabs)
        out_specs = [two, stream, stream]
        acc_shape = (nb * BLOCK, w)
    return _pallas_call(
        body, name=f"attn_bwd_d{d}", grid=(d * nb // 2,), out_shape=[_sds((ln, dw), BF16)] * 3,
        in_specs=in_specs, out_specs=out_specs,
        scratch_shapes=[pltpu.VMEM(acc_shape, F32), pltpu.VMEM(acc_shape, F32), pltpu.VMEM(two.block_shape, F32)],
        compiler_params=_params(1, 48),
    )(*args)


def _position():
    return lax.axis_index("x"), lax.axis_index("y"), lax.axis_index("c")


def _all_gather(shards, after, name):
    n_a = len(shards)

    def body(*refs):
        x_refs, out_refs = refs[:n_a], refs[n_a + 1:2 * n_a + 1]
        send_sems, recv_sems, local_sems = refs[2 * n_a + 1:]
        x, y, c = _position()
        me, sibling = (x, y, c), (x, y, 1 - c)
        chips = [(1 - x, y), (x, 1 - y), (1 - x, 1 - y)]

        def rows(a, px, py, pc):
            return out_refs[a].at[4 * px + 2 * py + pc]

        def copy(a, k, block, to, own=False):
            return pltpu.make_async_remote_copy(
                src_ref=x_refs[a] if own else rows(a, *block), dst_ref=rows(a, *block),
                send_sem=send_sems.at[a, k], recv_sem=recv_sems.at[a, k], device_id=to, device_id_type=MESH)

        mine = [pltpu.make_async_copy(x_refs[a], rows(a, *me), local_sems.at[a]) for a in range(n_a)]
        for cp in mine:
            cp.start()
        first = []
        for j, chip in enumerate(chips):
            first += [copy(a, 1 + j, me, (*chip, c), own=True) for a in range(n_a)]
        first += [copy(a, 0, me, sibling, own=True) for a in range(n_a)]
        for cp in first:
            cp.start()
        passed = []
        for j, chip in enumerate(chips):
            for a in range(n_a):
                copy(a, 1 + j, (*chip, c), me).wait_recv()
                fwd = copy(a, 4 + j, (*chip, c), sibling)
                fwd.start()
                passed.append(fwd)
        for a in range(n_a):
            copy(a, 0, sibling, me).wait_recv()
        for j, chip in enumerate(chips):
            for a in range(n_a):
                copy(a, 4 + j, (*chip, 1 - c), me).wait_recv()
        for cp in first + passed:
            cp.wait_send()
        for cp in mine:
            cp.wait()

    return _pallas_call(
        body, name=name, out_shape=[_sds((N_DEV,) + t.shape, t.dtype) for t in shards],
        in_specs=[ANY] * (n_a + 1), out_specs=[ANY] * n_a,
        scratch_shapes=[pltpu.SemaphoreType.DMA((n_a, 7)), pltpu.SemaphoreType.DMA((n_a, 7)),
                        pltpu.SemaphoreType.DMA((n_a,))],
    )(*shards, after)


def _all_gather_relay(xs, name):
    def body(x_ref, out_ref, send_sems, recv_sems, local_sem):
        x, y, c = _position()
        me, sibling = (x, y, c), (x, y, 1 - c)
        xn, yn, diag = (1 - x, y, c), (x, 1 - y, c), (1 - x, 1 - y, c)
        src_nb = (x + c * (1 - 2 * x), y + (1 - c) * (1 - 2 * y), c)
        dst_nb = (x + (1 - c) * (1 - 2 * x), y + c * (1 - 2 * y), c)

        def rows(dev):
            return out_ref.at[4 * dev[0] + 2 * dev[1] + dev[2]]

        def copy(k, block, to, own=False):
            return pltpu.make_async_remote_copy(
                src_ref=x_ref if own else rows(block), dst_ref=rows(block),
                send_sem=send_sems.at[k], recv_sem=recv_sems.at[k], device_id=to, device_id_type=MESH)

        mine = pltpu.make_async_copy(x_ref, rows(me), local_sem)
        mine.start()
        first = [copy(1, me, xn, own=True), copy(2, me, yn, own=True), copy(0, me, sibling, own=True)]
        for cp in first:
            cp.start()
        copy(1, xn, me).wait_recv()
        copy(2, yn, me).wait_recv()
        relay = copy(3, src_nb, dst_nb)
        relay.start()
        passed = [copy(4, xn, sibling), copy(5, yn, sibling)]
        for cp in passed:
            cp.start()
        copy(3, diag, me).wait_recv()
        last = copy(6, diag, sibling)
        last.start()
        copy(0, sibling, me).wait_recv()
        for k, blk in ((4, (1 - x, y, 1 - c)), (5, (x, 1 - y, 1 - c)), (6, (1 - x, 1 - y, 1 - c))):
            copy(k, blk, me).wait_recv()
        for cp in first + [relay] + passed + [last]:
            cp.wait_send()
        mine.wait()

    return _pallas_call(
        body, name=name, out_shape=_sds((N_DEV,) + xs.shape, xs.dtype),
        in_specs=[ANY], out_specs=ANY,
        scratch_shapes=[pltpu.SemaphoreType.DMA((7,)), pltpu.SemaphoreType.DMA((7,)), pltpu.SemaphoreType.DMA],
    )(xs)


HBM_SPEC = pl.BlockSpec(memory_space=pltpu.HBM)
SEM_SPEC = pl.BlockSpec(memory_space=pltpu.SEMAPHORE)
EFFECT = pltpu.SideEffectType.DATAFLOW_SIDE_EFFECTING
def _plan_gather_own(src_refs, land_refs):
    x, y, c = _position()
    me = 4 * x + 2 * y + c
    peers = [(x, y, 1 - c), (1 - x, y, c), (x, 1 - y, c), (1 - x, 1 - y, c)]
    return [(src_refs[a], land_refs[a].at[me], (a, k), peer) for k, peer in enumerate(peers) for a in range(len(src_refs))]


def _plan_gather_pass(src_refs, land_refs):
    x, y, c = _position()
    chips = [(1 - x, y), (x, 1 - y), (1 - x, 1 - y)]
    return [(land_refs[a].at[4 * px + 2 * py + c], land_refs[a].at[4 * px + 2 * py + c], (a, j), (x, y, 1 - c))
            for j, (px, py) in enumerate(chips) for a in range(len(land_refs))]


def _plan_to_sibling(src_refs, land_refs):
    x, y, c = _position()
    return [(src_refs[a].at[2 * k + (1 - c)], land_refs[a].at[k], (a, k), (x, y, 1 - c))
            for k in range(4) for a in range(len(src_refs))]


def _plan_to_chips(src_refs, land_refs):
    x, y, c = _position()
    chips = [(1 - x, y), (x, 1 - y), (1 - x, 1 - y)]
    return [(src_refs[a].at[2 * px + py], land_refs[a].at[j], (a, j), (px, py, c))
            for j, (px, py) in enumerate(chips) for a in range(len(src_refs))]


def _split_start(srcs, lands, plan, n_sem, after, name):
    n_s, n_a = len(srcs), len(lands)
    n_b = n_s + n_a

    def body(*refs):
        src_refs, land_refs = refs[:n_s], refs[n_s:n_b]
        send_sems, recv_sems, token = refs[n_b + 1], refs[n_b + 2], refs[-1]
        for src, dst, (a, k), dev in plan(src_refs, land_refs):
            i = a * n_sem + k
            pltpu.make_async_remote_copy(src_ref=src, dst_ref=dst, send_sem=send_sems.at[i], recv_sem=recv_sems.at[i],
                                         device_id=dev, device_id_type=MESH).start()
        token[...] = jnp.zeros_like(token)

    bufs = list(srcs) + list(lands)
    res = pl.pallas_call(
        body, name=name,
        out_shape=(pltpu.SemaphoreType.DMA((n_a * n_sem,)), pltpu.SemaphoreType.DMA((n_a * n_sem,)),
                   *[pltpu.HBM(t.shape, t.dtype) for t in bufs], _plain((8, LANES), F32)),
        in_specs=[HBM_SPEC] * n_b + [ANY],
        out_specs=(SEM_SPEC, SEM_SPEC, *[HBM_SPEC] * n_b, pl.BlockSpec(memory_space=pltpu.VMEM)),
        input_output_aliases={i: 2 + i for i in range(n_b)},
        compiler_params=pltpu.CompilerParams(has_side_effects=EFFECT),
    )(*[pltpu.with_memory_space_constraint(t, pltpu.HBM) for t in bufs], after)
    return (res[0], res[1], res[2:2 + n_s], res[2 + n_s:2 + n_b]), res[-1]


def _split_wait(started, plan, after, name):
    send_sems, recv_sems, srcs, lands = started
    n_s, n_a = len(srcs), len(lands)
    n_b = n_s + n_a
    n_sem = send_sems.shape[0] // n_a

    def body(*refs):
        src_refs, land_refs = refs[:n_s], refs[n_s:n_b]
        s_sems, r_sems = refs[n_b], refs[n_b + 1]
        for src, dst, (a, k), dev in plan(src_refs, land_refs):
            i = a * n_sem + k
            cp = pltpu.make_async_remote_copy(src_ref=src, dst_ref=dst, send_sem=s_sems.at[i], recv_sem=r_sems.at[i],
                                              device_id=dev, device_id_type=MESH)
            cp.wait_send()
            cp.wait_recv()

    bufs = list(srcs) + list(lands)
    res = pl.pallas_call(
        body, name=name, out_shape=tuple(pltpu.HBM(t.shape, t.dtype) for t in bufs),
        in_specs=[HBM_SPEC] * n_b + [SEM_SPEC, SEM_SPEC, ANY],
        out_specs=tuple([HBM_SPEC] * n_b),
        input_output_aliases={i: i for i in range(n_b)},
        compiler_params=pltpu.CompilerParams(has_side_effects=EFFECT),
    )(*bufs, send_sems, recv_sems, after)
    return res[:n_s], res[n_s:]


SUBLANES = 8


def _row_tile(r):
    return max(t for t in range(SUBLANES, ROW_TILE + 1, SUBLANES) if r % t == 0)


def _rs_add_sibling(gp, recv, ck_arr, name):
    _, r, l = gp.shape
    tr = r if r <= 4 * ROW_TILE else _row_tile(r)
    block = lambda k, ck: (k + ck[1] + 1) % 4

    def body(ck_ref, g_ref, r_ref, pf_ref, pb_ref):
        sm = g_ref[...].astype(F32) + r_ref[...].astype(F32)
        pf_ref[...] = sm
        pb_ref[...] = sm.astype(BF16)

    spec = pl.BlockSpec((None, tr, l), lambda i, k, ck: (block(k, ck), i, 0))
    return _pallas_call(
        body, name=name,
        grid_spec=pltpu.PrefetchScalarGridSpec(
            num_scalar_prefetch=1, grid=(r // tr, 4),
            in_specs=[pl.BlockSpec((None, tr, l), lambda i, k, ck: (2 * block(k, ck) + ck[0], i, 0)), spec],
            out_specs=[pl.BlockSpec((tr, l), lambda i, k, ck: (i, 0)), spec]),
        out_shape=[_sds((r, l), F32), _sds((4, r, l), BF16)], compiler_params=_params(2, 32),
    )(ck_arr, gp, recv)


def _adam_update(w, gv, m, v):
    nm = ADAM_B1 * m + (1.0 - ADAM_B1) * gv
    nv = ADAM_B2 * v + (1.0 - ADAM_B2) * (gv * gv)
    m_hat = nm / (1.0 - ADAM_B1 ** ADAM_STEP)
    v_hat = nv / (1.0 - ADAM_B2 ** ADAM_STEP)
    return -ADAM_LR * (m_hat / (jnp.sqrt(v_hat) + ADAM_EPS) + ADAM_WD * w), nm, nv


def _rs_finish_adamw(pf, recv, w, m, v, name):
    r, l = pf.shape
    tr = _row_tile(r)

    def body(p_ref, r_ref, w_ref, m_ref, v_ref, g_ref, d_ref, nm_ref, nv_ref):
        gv = ((p_ref[...] + r_ref[0].astype(F32)) + r_ref[1].astype(F32)) + r_ref[2].astype(F32)
        g_ref[...] = gv
        d_ref[...], nm_ref[...], nv_ref[...] = _adam_update(w_ref[...], gv, m_ref[...], v_ref[...])

    spec = pl.BlockSpec((tr, l), lambda i: (i, 0))
    return _pallas_call(
        body, name=name, grid=(r // tr,),
        in_specs=[spec, pl.BlockSpec((3, tr, l), lambda i: (0, i, 0)), spec, spec, spec], out_specs=[spec] * 4,
        out_shape=[_plain((r, l), F32)] * 4, compiler_params=_params(1, 32),
    )(pf, recv, w, m, v)


SMALL_ROWS = dict(norm_g=(0, 2), mem_norm_g=(2, 4), final_g=(4, 5), conv_w=(5, 8))
LOSS_ROWS = (8, 16)


def _sum_adamw_small(g, ck_arr, states):
    names = list(SMALL_ROWS)
    n_dev, n_rows, _ = g.shape

    def body(ck_ref, g_ref, gc_ref, *refs):
        ins, loss_ref, outs = refs[:3 * len(names)], refs[3 * len(names)], refs[3 * len(names) + 1:]

        def total(ref, lo, hi):
            acc = ref[0, lo:hi, :]
            for j in range(1, n_dev):
                acc = acc + ref[j, lo:hi, :]
            return acc

        loss_ref[...] = total(gc_ref, *LOSS_ROWS)
        for i, n in enumerate(names):
            gv = total(gc_ref if n == "conv_w" else g_ref, *SMALL_ROWS[n])
            w_ref, m_ref, v_ref = ins[3 * i:3 * i + 3]
            g_out, d_out, nm_out, nv_out = outs[4 * i:4 * i + 4]
            g_out[...] = gv
            d_out[...], nm_out[...], nv_out[...] = _adam_update(w_ref[...], gv, m_ref[...], v_ref[...])

    flat = [t for n in names for t in states[n]]
    mine = pl.BlockSpec((n_dev, n_rows, LANES), lambda i, ck: (0, 0, 2 * ck[1] + ck[0]))
    res = _pallas_call(
        body, name="sum_adamw_small",
        grid_spec=pltpu.PrefetchScalarGridSpec(
            num_scalar_prefetch=1, grid=(1,),
            in_specs=[_whole(g.shape), mine] + [_whole(t.shape) for t in flat],
            out_specs=[_whole((SUBLANES, LANES))] + [_whole(states[n][0].shape) for n in names for _ in range(4)]),
        out_shape=[_plain((SUBLANES, LANES), F32)] + [_plain(states[n][0].shape, F32) for n in names for _ in range(4)],
        compiler_params=_params(1, 32),
    )(ck_arr, g, g, *flat)
    return res[0], {n: tuple(res[1 + 4 * i:5 + 4 * i]) for i, n in enumerate(names)}


def _finish(name, pf, recv, w, m, v):
    if name in ("attn_w_in", "conv_w_in"):
        res = _rs_finish_adamw(pf, recv, w.T, m.T, v.T, "rs_finish_adamw_" + name)
        return tuple(t.T for t in res)
    return _rs_finish_adamw(pf, recv, w, m, v, "rs_finish_adamw_" + name)


def kernel(x, mem, positions, norm_g, mem_norm_g, w_mem_kv, attn_w_in, attn_w_out, conv_w_in, conv_w, conv_w_out, final_g, loss_target, m_norm_g, m_mem_norm_g, m_w_mem_kv, m_attn_w_in, m_attn_w_out, m_conv_w_in, m_conv_w, m_conv_w_out, m_final_g, v_norm_g, v_mem_norm_g, v_w_mem_kv, v_attn_w_in, v_attn_w_out, v_conv_w_in, v_conv_w, v_conv_w_out, v_final_g):
    px, py, pc = _position()
    me = 4 * px + 2 * py + pc
    ck_arr = jnp.stack([pc, 2 * px + py]).astype(jnp.int32)
    x, mem, pos, tgt = x[0], mem[0], positions[0], loss_target[0]

    wg_in0 = _all_gather_relay(attn_w_in[0].astype(BF16), "gather_w_in0")
    late = [attn_w_out[0].astype(BF16), conv_w_in[0].astype(BF16), conv_w_out[0].astype(BF16),
            w_mem_kv.astype(BF16).reshape(-1, w_mem_kv.shape[2]), jnp.pad(conv_w[0], ((0, 5), (0, 0)))]
    lands = [lax.dynamic_update_slice(lax.empty((N_DEV,) + t.shape, t.dtype), t[None], (me, 0, 0)) for t in late]
    late_weights, late_token = _split_start(late, lands, _plan_gather_own, 4, wg_in0, "gather_late_start")

    tabs = _rope_tables(pos)
    g0, g1 = norm_g[0:1], norm_g[1:2]

    hn0, qs, ks, vs, tabs_v, qm0, z0 = _inproj_attn(x, g0, wg_in0, tabs, late_token)
    os_, ls_ = [], []
    for j, d in enumerate(DILATIONS):
        if j == 2:
            _, lands = _split_wait(late_weights, _plan_gather_own, ls_[1], "gather_late_wait")
            late_weights, late_token = _split_start([], lands, _plan_gather_pass, 3, ls_[1], "gather_late_pass_start")
        o, l = _attn_fwd(qs[j], ks[j], vs[j], d, late_token)
        os_.append(o)
        ls_.append(l)

    _, gathered = _split_wait(late_weights, _plan_gather_pass, ls_[2], "gather_late_pass_wait")
    wg_out0, wg_in1, wg_out1, wg_kv, cw_all = gathered
    w_out1 = wg_out1.reshape(-1, wg_out1.shape[2])
    cw = cw_all[:, 0:3].transpose(1, 0, 2).reshape(3, -1)
    kv = _memkv_fwd(mem, mem_norm_g, wg_kv)
    y0, y0_t, mo0, h1 = _post_attn(os_, ls_, qm0, kv[0], z0, x, wg_out0)

    hn1, bg, cg, u, qm1, z1 = _inproj_conv(h1, g1, wg_in1)
    y1, y1_t, mo1, dh2, dh2b, loss_acc, d_final_g = _post_conv_loss(
        bg, cg, u, qm1, kv[1], z1, h1, w_out1, cw, final_g.reshape(1, -1), tgt)

    d_w_out1 = _wgrad_rows(y1_t, dh2b, "wgrad_out1")
    dz1, dbg, dconv, dqm1, dkv1 = _bwd_post_conv(dh2b, w_out1, bg, cg, u, z1, qm1, kv[1], mo1, cw)
    dcg, du, dcw = _bwd_conv(dconv, cg, u, cw)
    dh1, dg1, dh1b, dproj1_t = _dgrad_norm([(dbg, 1), (dcg, 1), (du, 1), (dqm1, 1), (dz1, 1)], wg_in1, h1, g1, dh2,
                                           dh2, True, "dgrad_norm_conv")
    d_w_in1 = _wgrad_shards_t(dproj1_t, hn1, wg_in1.shape[2], "wgrad_in1")

    d_w_out0 = _wgrad_cols(y0_t, dh1b, wg_out0.shape[2], "wgrad_out0")

    names1 = ["conv_w_in", "conv_w_out", "attn_w_out"]
    grads1 = [d_w_in1, d_w_out1, d_w_out0]
    started, token = _split_start(grads1, [lax.empty((4,) + g.shape[1:], g.dtype) for g in grads1],
                                  _plan_to_sibling, 4, dg1, "rs1_sibling_start")

    gw = GROUP_WIDTH
    ones = (jnp.arange(gw)[:, None] // HEAD_DIM == jnp.arange(gw)[None, :] // HEAD_DIM).astype(BF16)
    res = _bwd_post_attn(dh1b, wg_out0, z0, os_, ls_, qm0, kv[0], mo0, ones, token)
    dz0, dos, dls, dqm0, dkv0 = res[0], res[1:4], res[4:7], res[7], res[8]

    grads1, from_sibling = _split_wait(started, _plan_to_sibling, dz0, "rs1_sibling_wait")
    parts1 = [_rs_add_sibling(g, r, ck_arr, "rs_add_sibling_" + n) for g, r, n in zip(grads1, from_sibling, names1)]
    pbs1 = [pb for _, pb in parts1]
    started, token = _split_start(pbs1, [lax.empty((3,) + p.shape[1:], p.dtype) for p in pbs1],
                                  _plan_to_chips, 3, dg1, "rs1_chips_start")

    dqs, dks, dvs = [], [], []
    for j, d in enumerate(DILATIONS):
        dq, dk, dv = _attn_bwd(qs[j], ks[j], vs[j], ls_[j], dos[j], dls[j], tabs_v[j], d, token)
        dqs.append((dq, d))
        dks.append((dk, d))
        dvs.append((dv, d))
    d_w_kv, d_mem_g = _memkv_bwd(jnp.stack([dkv0, dkv1]), wg_kv, mem, mem_norm_g)
    pieces0 = dqs + dks + dvs + [(dqm0, 1), (dz0, 1)]
    dproj0_t = _assemble_dproj_t(pieces0, N_DEV * wg_in0.shape[2], "assemble_dproj_attn")
    d_w_in0 = _wgrad_shards_t(dproj0_t, hn0, wg_in0.shape[2], "wgrad_in0")

    names0 = ["attn_w_in", "w_mem_kv"]
    grads0 = [d_w_in0, d_w_kv]
    started0, token0 = _split_start(grads0, [lax.empty((4,) + g.shape[1:], g.dtype) for g in grads0],
                                    _plan_to_sibling, 4, dg1, "rs0_sibling_start")
    _, from_chips1 = _split_wait(started, _plan_to_chips, token0, "rs1_chips_wait")
    shard = dict(attn_w_in=(attn_w_in[0], m_attn_w_in[0], v_attn_w_in[0]),
                 attn_w_out=(attn_w_out[0], m_attn_w_out[0], v_attn_w_out[0]),
                 conv_w_in=(conv_w_in[0], m_conv_w_in[0], v_conv_w_in[0]),
                 conv_w_out=(conv_w_out[0], m_conv_w_out[0], v_conv_w_out[0]),
                 w_mem_kv=tuple(t.reshape(-1, t.shape[2]) for t in (w_mem_kv, m_w_mem_kv, v_w_mem_kv)))
    big = {}
    for n, (pf, _), r in zip(names1, parts1, from_chips1):
        big[n] = _finish(n, pf, r, *shard[n])

    grads0, from_sibling = _split_wait(started0, _plan_to_sibling, big["conv_w_out"][1], "rs0_sibling_wait")
    parts0 = [_rs_add_sibling(g, r, ck_arr, "rs_add_sibling_" + n) for g, r, n in zip(grads0, from_sibling, names0)]
    pbs0 = [pb for _, pb in parts0]
    started0, token0 = _split_start(pbs0, [lax.empty((3,) + p.shape[1:], p.dtype) for p in pbs0],
                                    _plan_to_chips, 3, dg1, "rs0_chips_start")
    dx, dg0 = _dgrad_norm(pieces0, wg_in0, x, g0, dh1, token0, False, "dgrad_norm_attn")

    small_part = jnp.concatenate([dg0, dg1, d_mem_g.reshape(2, -1), d_final_g, dcw[0:3]], axis=0)
    small_part = jnp.concatenate([small_part, jnp.broadcast_to(loss_acc[0, 0], small_part.shape)], axis=0)
    small_w = dict(norm_g=(norm_g, m_norm_g, v_norm_g), mem_norm_g=(mem_norm_g, m_mem_norm_g, v_mem_norm_g),
                   conv_w=(conv_w, m_conv_w, v_conv_w), final_g=(final_g, m_final_g, v_final_g))
    loss_tile, small_res = _sum_adamw_small(
        _all_gather([small_part], dg0, "gather_small_grads")[0], ck_arr,
        {n: tuple(t.reshape(-1, t.shape[-1]) for t in wmv) for n, wmv in small_w.items()})
    loss = loss_tile[0, 0]
    for n, wmv in small_w.items():
        big[n] = tuple(t.reshape(wmv[0].shape) for t in small_res[n])

    _, from_chips0 = _split_wait(started0, _plan_to_chips, big["final_g"][1], "rs0_chips_wait")
    for n, (pf, _), r in zip(names0, parts0, from_chips0):
        big[n] = _finish(n, pf, r, *shard[n])
    for n in ("attn_w_in", "attn_w_out", "conv_w_in", "conv_w_out"):
        big[n] = tuple(t[None] for t in big[n])
    big["w_mem_kv"] = tuple(t.reshape(w_mem_kv.shape) for t in big["w_mem_kv"])

    order = ["norm_g", "mem_norm_g", "w_mem_kv", "attn_w_in", "attn_w_out", "conv_w_in", "conv_w", "conv_w_out", "final_g"]
    return (loss, dx[None], *[big[n][0] for n in order], *[big[n][1] for n in order],
            *[big[n][2] for n in order], *[big[n][3] for n in order])
```

```python
import jax
import jax.numpy as jnp
from jax import lax
from jax.experimental import pallas as pl
from jax.experimental.pallas import tpu as pltpu

F32 = jnp.float32
BF16 = jnp.bfloat16

N_DEV = 8
D_MODEL = 1024
HEAD_DIM = 64
ROT_DIM = HEAD_DIM // 4
ROPE_THETA = 500000.0
DILATIONS = (1, 4, 16)
HEADS_PER_GROUP = 8
GROUP_WIDTH = HEADS_PER_GROUP * HEAD_DIM
BLOCK = 128
N_MEM = 256
MEM_HEADS = 4
MEM_WIDTH = MEM_HEADS * HEAD_DIM
CONV_WIDTH = D_MODEL
EPS = 1e-6
SCALE = HEAD_DIM ** -0.5
NEG = -1e30

ADAM_LR = 0.001
ADAM_B1 = 0.9
ADAM_B2 = 0.999
ADAM_EPS = 1e-08
ADAM_WD = 0.01
ADAM_STEP = 10

ROW_TILE = 256
WGRAD_SHARDS = 4
LANES = 128
MESH = pl.DeviceIdType.MESH
ANY = pl.BlockSpec(memory_space=pl.ANY)


def _pallas_call(body, **kw):
    call = pl.pallas_call(body, **kw)

    def run(*args):
        pinned = [pltpu.with_memory_space_constraint(a, pltpu.HBM) if jnp.issubdtype(a.dtype, jnp.floating) else a
                  for a in args]
        return call(*pinned)

    return run


def _dot(a, b):
    return lax.dot_general(a, b, (((1,), (0,)), ((), ())), preferred_element_type=F32)


def _dot_nt(a, b):
    return lax.dot_general(a, b, (((1,), (1,)), ((), ())), preferred_element_type=F32)


def _dot_tn(a, b):
    return lax.dot_general(a, b, (((0,), (0,)), ((), ())), preferred_element_type=F32)


def _params(n_grid, vmem_mb=48):
    return pltpu.CompilerParams(dimension_semantics=("arbitrary",) * n_grid, vmem_limit_bytes=vmem_mb << 20)


def _rows(width, tm=ROW_TILE):
    return pl.BlockSpec((tm, width), lambda i: (i, 0))


def _view_rows(width, d, tm=ROW_TILE):
    return pl.BlockSpec((tm // d, d * width), lambda i: (i, 0))


def _whole(shape):
    return pl.BlockSpec(shape, lambda *_: (0,) * len(shape))


def _resident(shape):
    return pl.BlockSpec(shape, lambda *_: (0,) * len(shape), pipeline_mode=pl.Buffered(1))


def _sds(shape, dtype):
    return pltpu.HBM(shape, dtype)


def _plain(shape, dtype):
    return jax.ShapeDtypeStruct(shape, dtype)


def _silu_parts(z):
    sg = jax.nn.sigmoid(z)
    return z * sg, sg * (1.0 + z * (1.0 - sg))


def _to_view(scr, val, out_ref, d):
    tm, w = val.shape
    if d == 1:
        out_ref[...] = val.astype(out_ref.dtype)
        return
    for cb in range(w // LANES):
        scr[cb] = val[:, cb * LANES:(cb + 1) * LANES]
    for r in range(d):
        for cb in range(w // LANES):
            lo = r * w + cb * LANES
            out_ref[:, lo:lo + LANES] = scr[cb, pl.ds(r, tm // d, stride=d), :].astype(out_ref.dtype)


def _from_view(scr, in_ref, d):
    if d == 1:
        return in_ref[...].astype(F32)
    nc, tm, _ = scr.shape
    w = nc * LANES
    for r in range(d):
        for cb in range(nc):
            lo = r * w + cb * LANES
            scr[cb, pl.ds(r, tm // d, stride=d), :] = in_ref[:, lo:lo + LANES].astype(F32)
    return jnp.concatenate([scr[cb] for cb in range(nc)], axis=1)


def _view_scratch(width, tm=ROW_TILE):
    return pltpu.VMEM((width // LANES, tm, LANES), F32)


def _rope_tables(pos):
    half = ROT_DIM // 2
    inv_freq = ROPE_THETA ** (-jnp.arange(half, dtype=F32) * (2.0 / ROT_DIM))
    ang = pos.astype(F32)[:, None] * inv_freq
    cos, sin = jnp.cos(ang), jnp.sin(ang)
    s = pos.shape[0]
    z8 = jnp.zeros((s, half), F32)
    rest = HEAD_DIM - ROT_DIM
    cosf = jnp.concatenate([cos, cos, jnp.ones((s, rest), F32)], axis=1)
    sa = jnp.concatenate([-sin, z8, jnp.zeros((s, rest), F32)], axis=1)
    sb = jnp.concatenate([z8, sin, jnp.zeros((s, rest), F32)], axis=1)
    return tuple(jnp.tile(t, (1, LANES // HEAD_DIM)) for t in (cosf, sa, sb))


def _rope_fwd(t, cv, sav, sbv):
    w = t.shape[1]
    return t * cv + pltpu.roll(t, w - ROT_DIM // 2, 1) * sav + pltpu.roll(t, ROT_DIM // 2, 1) * sbv


def _rope_bwd(g, cv, sav, sbv):
    w = g.shape[1]
    return g * cv + pltpu.roll(g * sav, ROT_DIM // 2, 1) + pltpu.roll(g * sbv, w - ROT_DIM // 2, 1)


def _joined_columns(wg_ref):
    assert wg_ref.shape[2] % LANES == 0
    return jnp.concatenate([wg_ref[j] for j in range(N_DEV)], axis=1)


def _join_once(wg_ref, w_scr):
    c = wg_ref.shape[2]

    @pl.when(pl.program_id(0) == 0)
    def _():
        for j in range(N_DEV):
            w_scr[:, j * c:(j + 1) * c] = wg_ref[j]


def _inproj_attn(x, g, wg, tabs, token):
    s, d_model = x.shape
    gw = GROUP_WIDTH
    n = N_DEV * wg.shape[2]
    nz = n - 9 * gw - MEM_WIDTH
    reps = gw // LANES
    tm = ROW_TILE

    def body(_, x_ref, g_ref, w_ref, c_ref, sa_ref, sb_ref, hn_ref, *rest):
        outs, (wj, scr, tscr) = rest[:-3], rest[-3:]
        q_refs, k_refs, v_refs, t_refs, qm_ref, z_ref = outs[0:3], outs[3:6], outs[6:9], outs[9:18], outs[18], outs[19]
        _join_once(w_ref, wj)
        xb = x_ref[...]
        r = lax.rsqrt(jnp.mean(xb * xb, axis=-1, keepdims=True) + EPS)
        hn = ((xb * r) * g_ref[...]).astype(BF16)
        hn_ref[...] = hn
        proj = lambda lo, hi: _dot(hn, wj[:, lo:hi])
        tab = (c_ref[...], sa_ref[...], sb_ref[...])
        cv, sav, sbv = [jnp.tile(t, (1, reps)) for t in tab]
        for j, d in enumerate(DILATIONS):
            tq = _rope_fwd(proj(j * gw, (j + 1) * gw), cv, sav, sbv)
            _to_view(scr, tq * SCALE, q_refs[j], d)
            tk = _rope_fwd(proj((3 + j) * gw, (4 + j) * gw), cv, sav, sbv)
            _to_view(scr, tk, k_refs[j], d)
            _to_view(scr, proj((6 + j) * gw, (7 + j) * gw), v_refs[j], d)
            for i in range(3):
                _to_view(tscr, tab[i], t_refs[3 * j + i], d)
        qm_ref[...] = proj(9 * gw, 9 * gw + MEM_WIDTH).astype(BF16)
        z_ref[...] = proj(9 * gw + MEM_WIDTH, n)

    views = [_sds((s // d, d * gw), BF16) for d in DILATIONS]
    tviews = [_sds((s // d, d * LANES), F32) for d in DILATIONS for _ in range(3)]
    out_shape = [_sds((s, d_model), BF16)] + views * 3 + tviews + [_sds((s, MEM_WIDTH), BF16), _sds((s, nz), F32)]
    vspecs = [_view_rows(gw, d, tm) for d in DILATIONS]
    tspecs = [_view_rows(LANES, d, tm) for d in DILATIONS for _ in range(3)]
    out_specs = [_rows(d_model, tm)] + vspecs * 3 + tspecs + [_rows(MEM_WIDTH, tm), _rows(nz, tm)]
    res = _pallas_call(
        body, name="inproj_attn", grid=(s // tm,), out_shape=out_shape,
        in_specs=[ANY, _rows(d_model, tm), _whole((1, d_model)), _resident(wg.shape)] + [_rows(LANES, tm)] * 3,
        out_specs=out_specs,
        scratch_shapes=[pltpu.VMEM((d_model, n), BF16), _view_scratch(gw, tm), _view_scratch(LANES, tm)],
        compiler_params=_params(1, 60),
    )(token, x, g, wg, *tabs)
    tabs_v = [res[10 + 3 * j:13 + 3 * j] for j in range(3)]
    return res[0], res[1:4], res[4:7], res[7:10], tabs_v, res[19], res[20]


def _band_mask(n_keys):
    qi = lax.broadcasted_iota(jnp.int32, (BLOCK, n_keys), 0)
    kj = lax.broadcasted_iota(jnp.int32, (BLOCK, n_keys), 1)
    if n_keys == BLOCK:
        return kj <= qi
    return jnp.logical_or(jnp.logical_and(kj < BLOCK, kj >= qi), jnp.logical_and(kj >= BLOCK, (kj - BLOCK) <= qi))


def _low_head_lanes():
    return lax.broadcasted_iota(jnp.int32, (1, LANES), 1) < HEAD_DIM


def _split_pair(t, low):
    zero = jnp.zeros_like(t)
    return jnp.where(low, t, zero), jnp.where(low, zero, t)


def _pair_specs(d, nb, w):
    if nb == 1:
        return pl.BlockSpec((BLOCK, 2 * w), lambda n: (0, n)), None
    half = nb // 2
    two = pl.BlockSpec((2 * BLOCK, w), lambda n: (n % half, n // half))
    before = pl.BlockSpec((BLOCK, w), lambda n: (jnp.maximum(2 * (n % half) - 1, 0), n // half))
    return two, before


def _head_tiles(w, col0):
    return ([slice(p * LANES, (p + 1) * LANES) for p in range(w // LANES)],
            [slice(col0 + p * LANES, col0 + (p + 1) * LANES) for p in range(w // LANES)])


def _attend_fwd(q_ref, o_ref, lse_ref, rows, col0, kk, vv):
    w = kk.shape[1]
    valid = _band_mask(kk.shape[0])
    low = _low_head_lanes()
    pairs, qcols = _head_tiles(w, col0)
    qs_ = [h for qc in qcols for h in _split_pair(q_ref[rows, qc], low)]
    k2s = [kk[:, pr] for pr in pairs for _ in range(2)]
    scs = [jnp.where(valid, _dot_nt(qh, k2), NEG) for qh, k2 in zip(qs_, k2s)]
    ms = [jnp.max(sc, axis=-1, keepdims=True) for sc in scs]
    ps = [jnp.exp(sc - m) for sc, m in zip(scs, ms)]
    ls = [jnp.sum(p, axis=-1, keepdims=True) for p in ps]
    pns = [(p * (1.0 / l)).astype(BF16) for p, l in zip(ps, ls)]
    for i, (pr, qc) in enumerate(zip(pairs, qcols)):
        v2 = vv[:, pr]
        a, b = 2 * i, 2 * i + 1
        o_ref[rows, qc] = jnp.where(low, _dot(pns[a], v2), _dot(pns[b], v2))
        lse_ref[rows, qc] = jnp.where(low, ms[a] + jnp.log(ls[a]), ms[b] + jnp.log(ls[b]))


TOP, BOTTOM = slice(0, BLOCK), slice(BLOCK, 2 * BLOCK)


def _attn_fwd(q, k, v, d, token):
    ln, dw = q.shape
    w = dw // d
    nb = ln // BLOCK
    two, before = _pair_specs(d, nb, w)

    def body_streams(_, q_ref, kc_ref, vc_ref, o_ref, lse_ref):
        for sb in range(2):
            cols = slice(sb * w, (sb + 1) * w)
            _attend_fwd(q_ref, o_ref, lse_ref, TOP, sb * w, kc_ref[:, cols], vc_ref[:, cols])

    def body_blocks(_, q_ref, kp_ref, kc_ref, vp_ref, vc_ref, o_ref, lse_ref):
        first = pl.program_id(0) % (nb // 2) == 0
        pl.when(first)(lambda: _attend_fwd(q_ref, o_ref, lse_ref, TOP, 0, kc_ref[TOP, :], vc_ref[TOP, :]))
        pl.when(jnp.logical_not(first))(lambda: _attend_fwd(
            q_ref, o_ref, lse_ref, TOP, 0, jnp.concatenate([kp_ref[...], kc_ref[TOP, :]], axis=0),
            jnp.concatenate([vp_ref[...], vc_ref[TOP, :]], axis=0)))
        _attend_fwd(q_ref, o_ref, lse_ref, BOTTOM, 0, kc_ref[...], vc_ref[...])

    if nb == 1:
        body, in_specs, args = body_streams, [ANY, two, two, two], (token, q, k, v)
    else:
        body, in_specs, args = body_blocks, [ANY, two, before, two, before, two], (token, q, k, k, v, v)
    return _pallas_call(
        body, name=f"attn_fwd_d{d}", grid=(d * nb // 2,), out_shape=[_sds((ln, dw), F32)] * 2,
        in_specs=in_specs, out_specs=[two, two], compiler_params=_params(1, 32),
    )(*args)


def _memkv_fwd(mem, g, w):
    n_layers = g.shape[0]
    rows = D_MODEL // N_DEV

    def body(mem_ref, g_ref, w_ref, kv_ref):
        mb = mem_ref[...]
        r = lax.rsqrt(jnp.mean(mb * mb, axis=-1, keepdims=True) + EPS)
        mn = ((mb * r) * g_ref[...]).astype(BF16)
        kv_ref[...] = _dot(mn, w_ref[...].reshape(D_MODEL, 2 * MEM_WIDTH)).astype(BF16)

    return _pallas_call(
        body, name="memkv_fwd", grid=(n_layers,),
        out_shape=_plain((n_layers, N_MEM, 2 * MEM_WIDTH), BF16),
        in_specs=[_whole(mem.shape), pl.BlockSpec((None, 1, D_MODEL), lambda l: (l, 0, 0)),
                  pl.BlockSpec((N_DEV, rows, 2 * MEM_WIDTH), lambda l: (0, l, 0))],
        out_specs=pl.BlockSpec((None, N_MEM, 2 * MEM_WIDTH), lambda l: (l, 0, 0)),
        compiler_params=_params(1, 32),
    )(mem, g.reshape(n_layers, 1, D_MODEL), w)


def _mix_groups(os_, ls_):
    mx = jnp.maximum(jnp.maximum(ls_[0], ls_[1]), ls_[2])
    es = [jnp.exp(t - mx) for t in ls_]
    inv = 1.0 / (es[0] + es[1] + es[2])
    ws = [e * inv for e in es]
    mix = ws[0] * os_[0] + ws[1] * os_[1] + ws[2] * os_[2]
    return ws, mix


MEM_PAIRS = [slice(p * LANES, (p + 1) * LANES) for p in range(MEM_WIDTH // LANES)]


def _mem_probs(qhs, k2s):
    scs = [_dot_nt(qh, k2) * SCALE for qh, k2 in zip(qhs, k2s)]
    es = [jnp.exp(sc - jnp.max(sc, axis=-1, keepdims=True)) for sc in scs]
    return [e * (1.0 / jnp.sum(e, axis=-1, keepdims=True)) for e in es]


def _mem_attn_into(qm, kv_ref, mo_ref):
    low = _low_head_lanes()
    qhs = [h for pr in MEM_PAIRS for h in _split_pair(qm[:, pr], low)]
    k2s = [kv_ref[:, pr] for pr in MEM_PAIRS for _ in range(2)]
    ps = [p.astype(BF16) for p in _mem_probs(qhs, k2s)]
    for i, pr in enumerate(MEM_PAIRS):
        v2 = kv_ref[:, MEM_WIDTH + i * LANES:MEM_WIDTH + (i + 1) * LANES]
        mo_ref[:, pr] = jnp.where(low, _dot(ps[2 * i], v2), _dot(ps[2 * i + 1], v2))


def _mem_attn_bwd(qm, kv_ref, dmem, dqm_ref, dkv_ref):
    low = _low_head_lanes()
    dmb = dmem.astype(BF16)
    vps = [slice(MEM_WIDTH + i * LANES, MEM_WIDTH + (i + 1) * LANES) for i in range(len(MEM_PAIRS))]
    qhs = [h for pr in MEM_PAIRS for h in _split_pair(qm[:, pr], low)]
    dhs = [h for pr in MEM_PAIRS for h in _split_pair(dmb[:, pr], low)]
    k2s = [kv_ref[:, pr] for pr in MEM_PAIRS for _ in range(2)]
    v2s = [kv_ref[:, vp] for vp in vps for _ in range(2)]
    ps = _mem_probs(qhs, k2s)
    dps = [_dot_nt(dh, v2) for dh, v2 in zip(dhs, v2s)]
    dss = [(p * (dp - jnp.sum(dp * p, axis=-1, keepdims=True)) * SCALE).astype(BF16) for p, dp in zip(ps, dps)]
    pbs = [p.astype(BF16) for p in ps]
    for i, (pr, vp) in enumerate(zip(MEM_PAIRS, vps)):
        a, b = 2 * i, 2 * i + 1
        dqm_ref[:, pr] = jnp.where(low, _dot(dss[a], k2s[a]), _dot(dss[b], k2s[b])).astype(BF16)
        dkv_ref[:, pr] += _dot_tn(dss[a], qhs[a]) + _dot_tn(dss[b], qhs[b])
        dkv_ref[:, vp] += _dot_tn(pbs[a], dhs[a]) + _dot_tn(pbs[b], dhs[b])


def _post_attn(os_, ls_, qm, kv, z, x, wg_out):
    s, d_model = x.shape
    gw = GROUP_WIDTH
    nb = gw + MEM_WIDTH
    tm = ROW_TILE

    def body(o0, o1, o2, l0, l1, l2, qm_ref, kv_ref, z_ref, x_ref, w_ref, y_ref, yt_ref, mo_ref, h_ref, s0, s1):
        ov, lv = [], []
        for o_ref, l_ref, d in zip((o0, o1, o2), (l0, l1, l2), DILATIONS):
            ov.append(_from_view(s0, o_ref, d))
            lv.append(_from_view(s1, l_ref, d))
        _, mix = _mix_groups(ov, lv)
        _mem_attn_into(qm_ref[...], kv_ref, mo_ref)
        sz, _ = _silu_parts(z_ref[...])
        y_ref[:, :gw] = (mix * sz[:, :gw]).astype(BF16)
        y_ref[:, gw:] = (mo_ref[...] * sz[:, gw:]).astype(BF16)
        y = y_ref[...]
        yt_ref[...] = y.T
        h_ref[...] = x_ref[...] + _dot(y, _joined_columns(w_ref))

    vspecs = [_view_rows(gw, d) for d in DILATIONS]
    return _pallas_call(
        body, name="post_attn", grid=(s // tm,),
        out_shape=[_sds((s, nb), BF16), _sds((nb, s), BF16), _sds((s, MEM_WIDTH), F32), _sds((s, d_model), F32)],
        in_specs=vspecs * 2 + [_rows(MEM_WIDTH), _whole(kv.shape), _rows(nb), _rows(d_model), _whole(wg_out.shape)],
        out_specs=[_rows(nb), pl.BlockSpec((nb, tm), lambda i: (0, i)), _rows(MEM_WIDTH), _rows(d_model)],
        scratch_shapes=[_view_scratch(gw), _view_scratch(gw)],
        compiler_params=_params(1, 40),
    )(*os_, *ls_, qm, kv, z, x, wg_out)


def _inproj_conv(x, g, wg):
    s, d_model = x.shape
    c = CONV_WIDTH
    n = N_DEV * wg.shape[2]
    nz = n - 3 * c - MEM_WIDTH
    tm = ROW_TILE

    def body(x_ref, g_ref, w_ref, hn_ref, bg_ref, cg_ref, u_ref, qm_ref, z_ref, wj):
        _join_once(w_ref, wj)
        xb = x_ref[...]
        r = lax.rsqrt(jnp.mean(xb * xb, axis=-1, keepdims=True) + EPS)
        hn = ((xb * r) * g_ref[...]).astype(BF16)
        hn_ref[...] = hn
        bg_ref[...] = _dot(hn, wj[:, 0:c])
        cg_ref[...] = _dot(hn, wj[:, c:2 * c])
        u_ref[...] = _dot(hn, wj[:, 2 * c:3 * c])
        qm_ref[...] = _dot(hn, wj[:, 3 * c:3 * c + MEM_WIDTH]).astype(BF16)
        z_ref[...] = _dot(hn, wj[:, 3 * c + MEM_WIDTH:])

    return _pallas_call(
        body, name="inproj_conv", grid=(s // tm,),
        out_shape=[_sds((s, d_model), BF16)] + [_sds((s, c), F32)] * 3 + [_sds((s, MEM_WIDTH), BF16), _sds((s, nz), F32)],
        in_specs=[_rows(d_model), _whole((1, d_model)), _resident(wg.shape)],
        out_specs=[_rows(d_model)] + [_rows(c)] * 3 + [_rows(MEM_WIDTH), _rows(nz)],
        scratch_shapes=[pltpu.VMEM((d_model, n), BF16)],
        compiler_params=_params(1, 60),
    )(x, g, wg)


HALO = 8


def _halo_before(width, tm=ROW_TILE):
    return pl.BlockSpec((HALO, width), lambda i: (jnp.maximum(i * (tm // HALO) - 1, 0), 0))


def _halo_after(width, n_rows, tm=ROW_TILE):
    return pl.BlockSpec((HALO, width), lambda i: (jnp.minimum((i + 1) * (tm // HALO), n_rows // HALO - 1), 0))


def _conv_taps(cg_ref, u_ref, cgh_ref, uh_ref, i):
    a = cg_ref[...] * u_ref[...]
    ah = jnp.where(i > 0, cgh_ref[...] * uh_ref[...], 0.0)
    row = lax.broadcasted_iota(jnp.int32, a.shape, 0)
    a1 = jnp.where(row == 0, ah[HALO - 1:HALO], pltpu.roll(a, 1, 0))
    a2 = jnp.where(row == 0, ah[HALO - 2:HALO - 1], jnp.where(row == 1, ah[HALO - 1:HALO], pltpu.roll(a, 2, 0)))
    return a, a1, a2


def _post_conv_loss(bg, cg, u, qm, kv, z, h1, w_out, cw, gf, tgt):
    s, d = h1.shape
    c = CONV_WIDTH
    nb = c + MEM_WIDTH
    tm = ROW_TILE

    def body(bg_ref, cg_ref, u_ref, cgh_ref, uh_ref, qm_ref, kv_ref, z_ref, h_ref, w_ref, cw_ref, gf_ref, t_ref,
             y_ref, yt_ref, mo_ref, dh_ref, dhb_ref, loss_ref, dgf_ref):
        i = pl.program_id(0)
        a, a1, a2 = _conv_taps(cg_ref, u_ref, cgh_ref, uh_ref, i)
        conv = cw_ref[0:1, :] * a2 + cw_ref[1:2, :] * a1 + cw_ref[2:3, :] * a
        mix = bg_ref[...] * conv
        _mem_attn_into(qm_ref[...], kv_ref, mo_ref)
        sz, _ = _silu_parts(z_ref[...])
        y_ref[:, :c] = (mix * sz[:, :c]).astype(BF16)
        y_ref[:, c:] = (mo_ref[...] * sz[:, c:]).astype(BF16)
        y = y_ref[...]
        yt_ref[...] = y.T
        h2 = h_ref[...] + _dot(y, w_ref[...])
        r = lax.rsqrt(jnp.mean(h2 * h2, axis=-1, keepdims=True) + EPS)
        nh = h2 * r
        gfv = gf_ref[...]
        diff = nh * gfv - t_ref[...]
        dout = diff * (1.0 / d)
        dn = dout * gfv
        dh2 = r * dn - h2 * ((r * r * r) * jnp.mean(dn * h2, axis=-1, keepdims=True))
        dh_ref[...] = dh2
        dhb_ref[...] = dh2.astype(BF16)

        @pl.when(i == 0)
        def _():
            loss_ref[...] = jnp.zeros_like(loss_ref)
            dgf_ref[...] = jnp.zeros_like(dgf_ref)

        loss_ref[...] += 0.5 * jnp.sum(jnp.mean(diff * diff, axis=-1, keepdims=True))
        dgf_ref[...] += jnp.sum(dout * nh, axis=0, keepdims=True)

    return _pallas_call(
        body, name="post_conv_loss", grid=(s // tm,),
        out_shape=[_sds((s, nb), BF16), _sds((nb, s), BF16), _sds((s, MEM_WIDTH), F32), _sds((s, d), F32),
                   _sds((s, d), BF16), _plain((8, LANES), F32), _plain((1, d), F32)],
        in_specs=[_rows(c)] * 3 + [_halo_before(c)] * 2 + [_rows(MEM_WIDTH), _whole(kv.shape), _rows(nb), _rows(d),
                  _whole(w_out.shape), _whole(cw.shape), _whole((1, d)), _rows(d)],
        out_specs=[_rows(nb), pl.BlockSpec((nb, tm), lambda i: (0, i)), _rows(MEM_WIDTH), _rows(d), _rows(d),
                   _whole((8, LANES)), _whole((1, d))],
        compiler_params=_params(1, 48),
    )(bg, cg, u, cg, u, qm, kv, z, h1, w_out, cw, gf, tgt)


def _bwd_post_conv(dhb, w_out, bg, cg, u, z, qm, kv, mo, cw):
    s = dhb.shape[0]
    c = CONV_WIDTH
    nb = c + MEM_WIDTH

    def body(dh_ref, w_ref, bg_ref, cg_ref, u_ref, cgh_ref, uh_ref, z_ref, qm_ref, kv_ref, mo_ref, cw_ref,
             dz_ref, dbg_ref, dc_ref, dqm_ref, dkv_ref):
        i = pl.program_id(0)

        @pl.when(i == 0)
        def _():
            dkv_ref[...] = jnp.zeros_like(dkv_ref)

        dy = _dot_nt(dh_ref[...], w_ref[...])
        sz, dsz = _silu_parts(z_ref[...])
        a, a1, a2 = _conv_taps(cg_ref, u_ref, cgh_ref, uh_ref, i)
        conv = cw_ref[0:1, :] * a2 + cw_ref[1:2, :] * a1 + cw_ref[2:3, :] * a
        bgv = bg_ref[...]
        dz_ref[:, :c] = (dy[:, :c] * (bgv * conv) * dsz[:, :c]).astype(BF16)
        dz_ref[:, c:] = (dy[:, c:] * mo_ref[...] * dsz[:, c:]).astype(BF16)
        dbr = dy * sz
        dmix = dbr[:, :c]
        dbg_ref[...] = (dmix * conv).astype(BF16)
        dc_ref[...] = dmix * bgv
        _mem_attn_bwd(qm_ref[...], kv_ref, dbr[:, c:], dqm_ref, dkv_ref)

    return _pallas_call(
        body, name="bwd_post_conv", grid=(s // ROW_TILE,),
        out_shape=[_sds((s, nb), BF16), _sds((s, c), BF16), _sds((s, c), F32), _sds((s, MEM_WIDTH), BF16),
                   _plain(kv.shape, F32)],
        in_specs=[_rows(D_MODEL), _whole(w_out.shape)] + [_rows(c)] * 3 + [_halo_before(c)] * 2
                 + [_rows(nb), _rows(MEM_WIDTH), _whole(kv.shape), _rows(MEM_WIDTH), _whole(cw.shape)],
        out_specs=[_rows(nb), _rows(c), _rows(c), _rows(MEM_WIDTH), _whole(kv.shape)],
        compiler_params=_params(1, 48),
    )(dhb, w_out, bg, cg, u, cg, u, z, qm, kv, mo, cw)


def _bwd_conv(dconv, cg, u, cw):
    s, c = dconv.shape
    tm = ROW_TILE
    last = s // tm - 1

    def body(dc_ref, dcn_ref, cg_ref, u_ref, cgh_ref, uh_ref, cw_ref, dcg_ref, du_ref, dcw_ref):
        i = pl.program_id(0)

        @pl.when(i == 0)
        def _():
            dcw_ref[...] = jnp.zeros_like(dcw_ref)

        dc = dc_ref[...]
        dcn = jnp.where(i < last, dcn_ref[...], 0.0)
        row = lax.broadcasted_iota(jnp.int32, dc.shape, 0)
        d1 = jnp.where(row == tm - 1, dcn[0:1], pltpu.roll(dc, tm - 1, 0))
        d2 = jnp.where(row == tm - 1, dcn[1:2], jnp.where(row == tm - 2, dcn[0:1], pltpu.roll(dc, tm - 2, 0)))
        da = cw_ref[2:3, :] * dc + cw_ref[1:2, :] * d1 + cw_ref[0:1, :] * d2
        a, a1, a2 = _conv_taps(cg_ref, u_ref, cgh_ref, uh_ref, i)
        dcg_ref[...] = (da * u_ref[...]).astype(BF16)
        du_ref[...] = (da * cg_ref[...]).astype(BF16)
        dcw_ref[0:1, :] += jnp.sum(dc * a2, axis=0, keepdims=True)
        dcw_ref[1:2, :] += jnp.sum(dc * a1, axis=0, keepdims=True)
        dcw_ref[2:3, :] += jnp.sum(dc * a, axis=0, keepdims=True)

    return _pallas_call(
        body, name="bwd_conv", grid=(s // tm,),
        out_shape=[_sds((s, c), BF16), _sds((s, c), BF16), _plain((8, c), F32)],
        in_specs=[_rows(c), _halo_after(c, s), _rows(c), _rows(c), _halo_before(c), _halo_before(c), _whole(cw.shape)],
        out_specs=[_rows(c), _rows(c), _whole((8, c))], compiler_params=_params(1, 40),
    )(dconv, dconv, cg, u, cg, u, cw)


def _assemble(p_refs, pieces, widths, dp, scr):
    off = 0
    for p_ref, (_, d), wd in zip(p_refs, pieces, widths):
        if d == 1:
            dp[:, off:off + wd] = p_ref[...]
        else:
            dp[:, off:off + wd] = _from_view(scr, p_ref, d).astype(BF16)
        off += wd


def _dgrad_norm(pieces, wg, h, g, dres, token, onward, name):
    s, d_model = h.shape
    n = N_DEV * wg.shape[2]
    tm = ROW_TILE
    widths = [p.shape[1] // d for p, d in pieces]
    assert sum(widths) == n
    n_p = len(pieces)

    def body(_, *refs):
        p_refs = refs[:n_p]
        w_ref, h_ref, g_ref, dr_ref, dh_ref, dg_ref = refs[n_p:n_p + 6]
        dp, scr, wj = refs[-3:]
        _join_once(w_ref, wj)

        @pl.when(pl.program_id(0) == 0)
        def _():
            dg_ref[...] = jnp.zeros_like(dg_ref)

        _assemble(p_refs, pieces, widths, dp, scr)
        dhn = _dot_nt(dp[...], wj[...])
        hb = h_ref[...]
        r = lax.rsqrt(jnp.mean(hb * hb, axis=-1, keepdims=True) + EPS)
        dg_ref[...] += jnp.sum(dhn * (hb * r), axis=0, keepdims=True)
        dn = dhn * g_ref[...]
        dh = dr_ref[...] + r * dn - hb * ((r * r * r) * jnp.mean(dn * hb, axis=-1, keepdims=True))
        dh_ref[...] = dh
        if onward:
            dhb_ref, dpt_ref = refs[n_p + 6:n_p + 8]
            dhb_ref[...] = dh.astype(BF16)
            dpt_ref[...] = dp[...].T

    p_specs = [_view_rows(wd, d) for (_, d), wd in zip(pieces, widths)]
    out_shape = [_plain((s, d_model), F32), _plain((1, d_model), F32)]
    out_specs = [_rows(d_model), _whole((1, d_model))]
    if onward:
        out_shape += [_sds((s, d_model), BF16), _sds((n, s), BF16)]
        out_specs += [_rows(d_model), pl.BlockSpec((n, tm), lambda i: (0, i))]
    return _pallas_call(
        body, name=name, grid=(s // tm,), out_shape=out_shape,
        in_specs=[ANY] + p_specs + [_resident(wg.shape), _rows(d_model), _whole((1, d_model)), _rows(d_model)],
        out_specs=out_specs,
        scratch_shapes=[pltpu.VMEM((tm, n), BF16), _view_scratch(GROUP_WIDTH), pltpu.VMEM((d_model, n), BF16)],
        compiler_params=_params(1, 60),
    )(token, *[p for p, _ in pieces], wg, h, g, dres)


def _assemble_dproj_t(pieces, n, name):
    tm = ROW_TILE
    widths = [p.shape[1] // d for p, d in pieces]
    assert sum(widths) == n
    s = pieces[0][0].shape[0] * pieces[0][1]
    n_p = len(pieces)

    def body(*refs):
        p_refs, (dpt_ref, dp, scr) = refs[:n_p], refs[n_p:]
        _assemble(p_refs, pieces, widths, dp, scr)
        dpt_ref[...] = dp[...].T

    return _pallas_call(
        body, name=name, grid=(s // tm,), out_shape=_sds((n, s), BF16),
        in_specs=[_view_rows(wd, d) for (_, d), wd in zip(pieces, widths)],
        out_specs=pl.BlockSpec((n, tm), lambda i: (0, i)),
        scratch_shapes=[pltpu.VMEM((tm, n), BF16), _view_scratch(GROUP_WIDTH)],
        compiler_params=_params(1, 40),
    )(*[p for p, _ in pieces])


def _wgrad_shards_t(dp_t, h, c, name):
    n, s = dp_t.shape
    d_model = h.shape[1]
    per_step = 2

    def body(a_ref, b_ref, o_ref):
        o_ref[...] = _dot(a_ref[...], b_ref[...]).astype(BF16).reshape(per_step, c, d_model)

    return _pallas_call(
        body, name=name, grid=(N_DEV // per_step,), out_shape=_sds((N_DEV, c, d_model), BF16),
        in_specs=[pl.BlockSpec((per_step * c, s), lambda j: (j, 0)), _resident(h.shape)],
        out_specs=pl.BlockSpec((per_step, c, d_model), lambda j: (j, 0, 0)), compiler_params=_params(1, 40),
    )(dp_t, h)


def _wgrad_cols(a_t, b, c, name):
    m, s = a_t.shape
    assert c % LANES == 0

    def body(a_ref, b_ref, o_ref):
        wide = _dot(a_ref[...], b_ref[...]).astype(BF16)
        for j in range(WGRAD_SHARDS):
            o_ref[j] = wide[:, j * c:(j + 1) * c]

    return _pallas_call(
        body, name=name, grid=(N_DEV // WGRAD_SHARDS,), out_shape=_sds((N_DEV, m, c), BF16),
        in_specs=[_whole(a_t.shape), pl.BlockSpec((s, WGRAD_SHARDS * c), lambda j: (0, j))],
        out_specs=pl.BlockSpec((WGRAD_SHARDS, m, c), lambda j: (j, 0, 0)), compiler_params=_params(1, 40),
    )(a_t, b)


def _wgrad_rows(a_t, b, name):
    m, s = a_t.shape
    n = b.shape[1]
    mr = m // N_DEV

    def body(a_ref, b_ref, o_ref):
        o_ref[...] = _dot(a_ref[...], b_ref[...]).astype(BF16).reshape(WGRAD_SHARDS, mr, n)

    return _pallas_call(
        body, name=name, grid=(N_DEV // WGRAD_SHARDS,), out_shape=_sds((N_DEV, mr, n), BF16),
        in_specs=[pl.BlockSpec((WGRAD_SHARDS * mr, s), lambda j: (j, 0)), _whole(b.shape)],
        out_specs=pl.BlockSpec((WGRAD_SHARDS, mr, n), lambda j: (j, 0, 0)), compiler_params=_params(1, 40),
    )(a_t, b)


def _memkv_bwd(dkv, w, mem, g):
    n_layers = g.shape[0]
    rows = D_MODEL // N_DEV

    def body(dkv_ref, w_ref, mem_ref, g_ref, dw_ref, dg_ref):
        mb = mem_ref[...]
        r = lax.rsqrt(jnp.mean(mb * mb, axis=-1, keepdims=True) + EPS)
        nm = mb * r
        mn = (nm * g_ref[...]).astype(BF16)
        dkvb = dkv_ref[...].astype(BF16)
        dw_ref[...] = _dot_tn(mn, dkvb).astype(BF16).reshape(N_DEV, rows, 2 * MEM_WIDTH)
        dmn = _dot_nt(dkvb, w_ref[...].reshape(D_MODEL, 2 * MEM_WIDTH))
        dg_ref[...] = jnp.sum(dmn * nm, axis=0, keepdims=True)

    lay = lambda *shape: pl.BlockSpec((None,) + shape, lambda l: (l, 0, 0))
    major = pl.BlockSpec((N_DEV, rows, 2 * MEM_WIDTH), lambda l: (0, l, 0))
    return _pallas_call(
        body, name="memkv_bwd", grid=(n_layers,),
        out_shape=[_sds((N_DEV, n_layers * rows, 2 * MEM_WIDTH), BF16), _plain((n_layers, 1, D_MODEL), F32)],
        in_specs=[lay(N_MEM, 2 * MEM_WIDTH), major, _whole(mem.shape), lay(1, D_MODEL)],
        out_specs=[major, lay(1, D_MODEL)],
        compiler_params=_params(1, 32),
    )(dkv, w, mem, g.reshape(n_layers, 1, D_MODEL))


def _bwd_post_attn(dhb, wg_out, z, os_, ls_, qm, kv, mo, head_ones, token):
    s = dhb.shape[0]
    gw = GROUP_WIDTH
    nb = gw + MEM_WIDTH
    tm = ROW_TILE

    def body(_, dh_ref, w_ref, z_ref, o0, o1, o2, l0, l1, l2, qm_ref, kv_ref, mo_ref, bd_ref,
             dz_ref, do0, do1, do2, dl0, dl1, dl2, dqm_ref, dkv_ref, s0, s1):
        @pl.when(pl.program_id(0) == 0)
        def _():
            dkv_ref[...] = jnp.zeros_like(dkv_ref)

        dy = _dot_nt(dh_ref[...], _joined_columns(w_ref))
        ov, lv = [], []
        for o_ref, l_ref, d in zip((o0, o1, o2), (l0, l1, l2), DILATIONS):
            ov.append(_from_view(s0, o_ref, d))
            lv.append(_from_view(s1, l_ref, d))
        ws, mix = _mix_groups(ov, lv)
        sz, dsz = _silu_parts(z_ref[...])
        dz_ref[:, :gw] = (dy[:, :gw] * mix * dsz[:, :gw]).astype(BF16)
        dz_ref[:, gw:] = (dy[:, gw:] * mo_ref[...] * dsz[:, gw:]).astype(BF16)
        dbr = dy * sz
        dmix = dbr[:, :gw]
        t = dmix * mix
        th = t.astype(BF16)
        tl = (t - th.astype(F32)).astype(BF16)
        rs = _dot(th, bd_ref[...]) + _dot(tl, bd_ref[...])
        for wg_, do_ref, dl_ref, d in zip(ws, (do0, do1, do2), (dl0, dl1, dl2), DILATIONS):
            _to_view(s0, wg_ * dmix, do_ref, d)
            _to_view(s1, wg_ * rs, dl_ref, d)
        _mem_attn_bwd(qm_ref[...], kv_ref, dbr[:, gw:], dqm_ref, dkv_ref)

    vspecs = [_view_rows(gw, d) for d in DILATIONS]
    return _pallas_call(
        body, name="bwd_post_attn", grid=(s // tm,),
        out_shape=[_sds((s, nb), BF16)] + [_sds((s // d, d * gw), BF16) for d in DILATIONS]
                  + [_sds((s // d, d * gw), F32) for d in DILATIONS] + [_sds((s, MEM_WIDTH), BF16), _plain(kv.shape, F32)],
        in_specs=[ANY, _rows(D_MODEL), _whole(wg_out.shape), _rows(nb)] + vspecs * 2
                 + [_rows(MEM_WIDTH), _whole(kv.shape), _rows(MEM_WIDTH), _whole(head_ones.shape)],
        out_specs=[_rows(nb)] + vspecs * 2 + [_rows(MEM_WIDTH), _whole(kv.shape)],
        scratch_shapes=[_view_scratch(gw), _view_scratch(gw)],
        compiler_params=_params(1, 48),
    )(token, dhb, wg_out, z, *os_, *ls_, qm, kv, mo, head_ones)


def _attn_bwd(q, k, v, lse, do, dl, tabs, d, token):
    ln, dw = q.shape
    w = dw // d
    nb = ln // BLOCK
    reps = w // LANES
    two, before = _pair_specs(d, nb, w)
    two_t, _ = _pair_specs(d, nb, LANES)

    def attend(q_ref, l_ref, do_ref, dl_ref, dqs, acck, accv, rows, col0, kk, vv, acc_rows):
        valid = _band_mask(kk.shape[0])
        low = _low_head_lanes()
        pairs, qcols = _head_tiles(w, col0)
        cols = [slice(col0 + h * HEAD_DIM, col0 + h * HEAD_DIM + 1) for h in range(HEADS_PER_GROUP)]
        qhs = [h for qc in qcols for h in _split_pair(q_ref[rows, qc], low)]
        dobs = [h for qc in qcols for h in _split_pair(do_ref[rows, qc], low)]
        k2s = [kk[:, pr] for pr in pairs for _ in range(2)]
        v2s = [vv[:, pr] for pr in pairs for _ in range(2)]
        scs = [jnp.where(valid, _dot_nt(qh, k2), NEG) for qh, k2 in zip(qhs, k2s)]
        dps = [_dot_nt(dob, v2) for dob, v2 in zip(dobs, v2s)]
        ps = [jnp.exp(sc - l_ref[rows, col]) for sc, col in zip(scs, cols)]
        dss = [(p * (dp - dl_ref[rows, col])).astype(BF16) for p, dp, col in zip(ps, dps, cols)]
        pbs = [p.astype(BF16) for p in ps]
        for i, qc in enumerate(qcols):
            a, b = 2 * i, 2 * i + 1
            dqs[rows, qc] = jnp.where(low, _dot(dss[a], k2s[a]), _dot(dss[b], k2s[b])) * SCALE
            acck[acc_rows, qc] += _dot_tn(dss[a], qhs[a]) + _dot_tn(dss[b], qhs[b])
            accv[acc_rows, qc] += _dot_tn(pbs[a], dobs[a]) + _dot_tn(pbs[b], dobs[b])

    def body_streams(_, q_ref, kc_ref, vc_ref, l_ref, do_ref, dl_ref, c_ref, sa_ref, sb_ref,
                     dq_ref, dk_ref, dv_ref, acck, accv, dqs):
        acck[...] = jnp.zeros_like(acck)
        accv[...] = jnp.zeros_like(accv)
        for sb in range(2):
            cols = slice(sb * w, (sb + 1) * w)
            attend(q_ref, l_ref, do_ref, dl_ref, dqs, acck, accv, TOP, sb * w, kc_ref[:, cols], vc_ref[:, cols], TOP)
        tabs2 = [jnp.concatenate([jnp.tile(r[:, sb * LANES:(sb + 1) * LANES], (1, reps)) for sb in range(2)], axis=1)
                 for r in (c_ref, sa_ref, sb_ref)]
        dq_ref[...] = _rope_bwd(dqs[...], *tabs2).astype(BF16)
        dk_ref[...] = _rope_bwd(acck[...], *tabs2).astype(BF16)
        dv_ref[...] = accv[...].astype(BF16)

    def body_blocks(_, q_ref, kp_ref, kc_ref, vp_ref, vc_ref, l_ref, do_ref, dl_ref, cq, saq, sbq, ck, sak, sbk,
                    dq_ref, dk_ref, dv_ref, acck, accv, dqs):
        i = pl.program_id(0) % (nb // 2)

        @pl.when(i == 0)
        def _():
            acck[...] = jnp.zeros_like(acck)
            accv[...] = jnp.zeros_like(accv)

        refs = (q_ref, l_ref, do_ref, dl_ref, dqs, acck, accv)
        pl.when(i == 0)(lambda: attend(*refs, TOP, 0, kc_ref[TOP, :], vc_ref[TOP, :], TOP))
        pl.when(i != 0)(lambda: attend(
            *refs, TOP, 0, jnp.concatenate([kp_ref[...], kc_ref[TOP, :]], axis=0),
            jnp.concatenate([vp_ref[...], vc_ref[TOP, :]], axis=0),
            pl.ds(pl.multiple_of((2 * i - 1) * BLOCK, BLOCK), 2 * BLOCK)))
        attend(*refs, BOTTOM, 0, kc_ref[...], vc_ref[...], pl.ds(pl.multiple_of(2 * i * BLOCK, BLOCK), 2 * BLOCK))
        tq = [jnp.tile(r[...], (1, reps)) for r in (cq, saq, sbq)]
        dq_ref[...] = _rope_bwd(dqs[...], *tq).astype(BF16)

        @pl.when(i == nb // 2 - 1)
        def _():
            for r0 in range(0, nb * BLOCK, 2 * BLOCK):
                rows = slice(r0, r0 + 2 * BLOCK)
                tk = [jnp.tile(r[rows, :], (1, reps)) for r in (ck, sak, sbk)]
                dk_ref[rows, :] = _rope_bwd(acck[rows, :], *tk).astype(BF16)
                dv_ref[rows, :] = accv[rows, :].astype(BF16)

    if nb == 1:
        body = body_streams
        in_specs = [ANY] + [two] * 6 + [two_t] * 3
        args = (token, q, k, v, lse, do, dl, *tabs)
        out_specs = [two, two, two]
        acc_shape = (BLOCK, 2 * w)
    else:
        body = body_blocks
        stream = pl.BlockSpec((nb * BLOCK, w), lambda n: (0, n // (nb // 2)))
        stream_t = pl.BlockSpec((nb * BLOCK, LANES), lambda n: (0, n // (nb // 2)))
        in_specs = [ANY, two, before, two, before, two, two, two, two] + [two_t] * 3 + [stream_t] * 3
        args = (token, q, k, k, v, v, lse, do, dl, *tabs, *tabs)
        out_specs = [two, stream, stream]
        acc_shape = (nb * BLOCK, w)
    return _pallas_call(
        body, name=f"attn_bwd_d{d}", grid=(d * nb // 2,), out_shape=[_sds((ln, dw), BF16)] * 3,
        in_specs=in_specs, out_specs=out_specs,
        scratch_shapes=[pltpu.VMEM(acc_shape, F32), pltpu.VMEM(acc_shape, F32), pltpu.VMEM(two.block_shape, F32)],
        compiler_params=_params(1, 48),
    )(*args)


def _position():
    return lax.axis_index("x"), lax.axis_index("y"), lax.axis_index("c")


def _all_gather(shards, after, name):
    n_a = len(shards)

    def body(*refs):
        x_refs, out_refs = refs[:n_a], refs[n_a + 1:2 * n_a + 1]
        send_sems, recv_sems, local_sems = refs[2 * n_a + 1:]
        x, y, c = _position()
        me, sibling = (x, y, c), (x, y, 1 - c)
        chips = [(1 - x, y), (x, 1 - y), (1 - x, 1 - y)]

        def rows(a, px, py, pc):
            return out_refs[a].at[4 * px + 2 * py + pc]

        def copy(a, k, block, to, own=False):
            return pltpu.make_async_remote_copy(
                src_ref=x_refs[a] if own else rows(a, *block), dst_ref=rows(a, *block),
                send_sem=send_sems.at[a, k], recv_sem=recv_sems.at[a, k], device_id=to, device_id_type=MESH)

        mine = [pltpu.make_async_copy(x_refs[a], rows(a, *me), local_sems.at[a]) for a in range(n_a)]
        for cp in mine:
            cp.start()
        first = []
        for j, chip in enumerate(chips):
            first += [copy(a, 1 + j, me, (*chip, c), own=True) for a in range(n_a)]
        first += [copy(a, 0, me, sibling, own=True) for a in range(n_a)]
        for cp in first:
            cp.start()
        passed = []
        for j, chip in enumerate(chips):
            for a in range(n_a):
                copy(a, 1 + j, (*chip, c), me).wait_recv()
                fwd = copy(a, 4 + j, (*chip, c), sibling)
                fwd.start()
                passed.append(fwd)
        for a in range(n_a):
            copy(a, 0, sibling, me).wait_recv()
        for j, chip in enumerate(chips):
            for a in range(n_a):
                copy(a, 4 + j, (*chip, 1 - c), me).wait_recv()
        for cp in first + passed:
            cp.wait_send()
        for cp in mine:
            cp.wait()

    return _pallas_call(
        body, name=name, out_shape=[_sds((N_DEV,) + t.shape, t.dtype) for t in shards],
        in_specs=[ANY] * (n_a + 1), out_specs=[ANY] * n_a,
        scratch_shapes=[pltpu.SemaphoreType.DMA((n_a, 7)), pltpu.SemaphoreType.DMA((n_a, 7)),
                        pltpu.SemaphoreType.DMA((n_a,))],
    )(*shards, after)


def _all_gather_relay(xs, name):
    def body(x_ref, out_ref, send_sems, recv_sems, local_sem):
        x, y, c = _position()
        me, sibling = (x, y, c), (x, y, 1 - c)
        xn, yn, diag = (1 - x, y, c), (x, 1 - y, c), (1 - x, 1 - y, c)
        src_nb = (x + c * (1 - 2 * x), y + (1 - c) * (1 - 2 * y), c)
        dst_nb = (x + (1 - c) * (1 - 2 * x), y + c * (1 - 2 * y), c)

        def rows(dev):
            return out_ref.at[4 * dev[0] + 2 * dev[1] + dev[2]]

        def copy(k, block, to, own=False):
            return pltpu.make_async_remote_copy(
                src_ref=x_ref if own else rows(block), dst_ref=rows(block),
                send_sem=send_sems.at[k], recv_sem=recv_sems.at[k], device_id=to, device_id_type=MESH)

        mine = pltpu.make_async_copy(x_ref, rows(me), local_sem)
        mine.start()
        first = [copy(1, me, xn, own=True), copy(2, me, yn, own=True), copy(0, me, sibling, own=True)]
        for cp in first:
            cp.start()
        copy(1, xn, me).wait_recv()
        copy(2, yn, me).wait_recv()
        relay = copy(3, src_nb, dst_nb)
        relay.start()
        passed = [copy(4, xn, sibling), copy(5, yn, sibling)]
        for cp in passed:
            cp.start()
        copy(3, diag, me).wait_recv()
        last = copy(6, diag, sibling)
        last.start()
        copy(0, sibling, me).wait_recv()
        for k, blk in ((4, (1 - x, y, 1 - c)), (5, (x, 1 - y, 1 - c)), (6, (1 - x, 1 - y, 1 - c))):
            copy(k, blk, me).wait_recv()
        for cp in first + [relay] + passed + [last]:
            cp.wait_send()
        mine.wait()

    return _pallas_call(
        body, name=name, out_shape=_sds((N_DEV,) + xs.shape, xs.dtype),
        in_specs=[ANY], out_specs=ANY,
        scratch_shapes=[pltpu.SemaphoreType.DMA((7,)), pltpu.SemaphoreType.DMA((7,)), pltpu.SemaphoreType.DMA],
    )(xs)


HBM_SPEC = pl.BlockSpec(memory_space=pltpu.HBM)
SEM_SPEC = pl.BlockSpec(memory_space=pltpu.SEMAPHORE)
EFFECT = pltpu.SideEffectType.DATAFLOW_SIDE_EFFECTING
def _plan_gather_own(src_refs, land_refs):
    x, y, c = _position()
    me = 4 * x + 2 * y + c
    peers = [(x, y, 1 - c), (1 - x, y, c), (x, 1 - y, c), (1 - x, 1 - y, c)]
    return [(src_refs[a], land_refs[a].at[me], (a, k), peer) for k, peer in enumerate(peers) for a in range(len(src_refs))]


def _plan_gather_pass(src_refs, land_refs):
    x, y, c = _position()
    chips = [(1 - x, y), (x, 1 - y), (1 - x, 1 - y)]
    return [(land_refs[a].at[4 * px + 2 * py + c], land_refs[a].at[4 * px + 2 * py + c], (a, j), (x, y, 1 - c))
            for j, (px, py) in enumerate(chips) for a in range(len(land_refs))]


def _plan_to_sibling(src_refs, land_refs):
    x, y, c = _position()
    return [(src_refs[a].at[2 * k + (1 - c)], land_refs[a].at[k], (a, k), (x, y, 1 - c))
            for k in range(4) for a in range(len(src_refs))]


def _plan_to_chips(src_refs, land_refs):
    x, y, c = _position()
    chips = [(1 - x, y), (x, 1 - y), (1 - x, 1 - y)]
    return [(src_refs[a].at[2 * px + py], land_refs[a].at[j], (a, j), (px, py, c))
            for j, (px, py) in enumerate(chips) for a in range(len(src_refs))]


def _split_start(srcs, lands, plan, n_sem, after, name):
    n_s, n_a = len(srcs), len(lands)
    n_b = n_s + n_a

    def body(*refs):
        src_refs, land_refs = refs[:n_s], refs[n_s:n_b]
        send_sems, recv_sems, token = refs[n_b + 1], refs[n_b + 2], refs[-1]
        for src, dst, (a, k), dev in plan(src_refs, land_refs):
            i = a * n_sem + k
            pltpu.make_async_remote_copy(src_ref=src, dst_ref=dst, send_sem=send_sems.at[i], recv_sem=recv_sems.at[i],
                                         device_id=dev, device_id_type=MESH).start()
        token[...] = jnp.zeros_like(token)

    bufs = list(srcs) + list(lands)
    res = pl.pallas_call(
        body, name=name,
        out_shape=(pltpu.SemaphoreType.DMA((n_a * n_sem,)), pltpu.SemaphoreType.DMA((n_a * n_sem,)),
                   *[pltpu.HBM(t.shape, t.dtype) for t in bufs], _plain((8, LANES), F32)),
        in_specs=[HBM_SPEC] * n_b + [ANY],
        out_specs=(SEM_SPEC, SEM_SPEC, *[HBM_SPEC] * n_b, pl.BlockSpec(memory_space=pltpu.VMEM)),
        input_output_aliases={i: 2 + i for i in range(n_b)},
        compiler_params=pltpu.CompilerParams(has_side_effects=EFFECT),
    )(*[pltpu.with_memory_space_constraint(t, pltpu.HBM) for t in bufs], after)
    return (res[0], res[1], res[2:2 + n_s], res[2 + n_s:2 + n_b]), res[-1]


def _split_wait(started, plan, after, name):
    send_sems, recv_sems, srcs, lands = started
    n_s, n_a = len(srcs), len(lands)
    n_b = n_s + n_a
    n_sem = send_sems.shape[0] // n_a

    def body(*refs):
        src_refs, land_refs = refs[:n_s], refs[n_s:n_b]
        s_sems, r_sems = refs[n_b], refs[n_b + 1]
        for src, dst, (a, k), dev in plan(src_refs, land_refs):
            i = a * n_sem + k
            cp = pltpu.make_async_remote_copy(src_ref=src, dst_ref=dst, send_sem=s_sems.at[i], recv_sem=r_sems.at[i],
                                              device_id=dev, device_id_type=MESH)
            cp.wait_send()
            cp.wait_recv()

    bufs = list(srcs) + list(lands)
    res = pl.pallas_call(
        body, name=name, out_shape=tuple(pltpu.HBM(t.shape, t.dtype) for t in bufs),
        in_specs=[HBM_SPEC] * n_b + [SEM_SPEC, SEM_SPEC, ANY],
        out_specs=tuple([HBM_SPEC] * n_b),
        input_output_aliases={i: i for i in range(n_b)},
        compiler_params=pltpu.CompilerParams(has_side_effects=EFFECT),
    )(*bufs, send_sems, recv_sems, after)
    return res[:n_s], res[n_s:]


SUBLANES = 8


def _row_tile(r):
    return max(t for t in range(SUBLANES, ROW_TILE + 1, SUBLANES) if r % t == 0)


def _rs_add_sibling(gp, recv, ck_arr, name):
    _, r, l = gp.shape
    tr = r if r <= 4 * ROW_TILE else _row_tile(r)
    block = lambda k, ck: (k + ck[1] + 1) % 4

    def body(ck_ref, g_ref, r_ref, pf_ref, pb_ref):
        sm = g_ref[...].astype(F32) + r_ref[...].astype(F32)
        pf_ref[...] = sm
        pb_ref[...] = sm.astype(BF16)

    spec = pl.BlockSpec((None, tr, l), lambda i, k, ck: (block(k, ck), i, 0))
    return _pallas_call(
        body, name=name,
        grid_spec=pltpu.PrefetchScalarGridSpec(
            num_scalar_prefetch=1, grid=(r // tr, 4),
            in_specs=[pl.BlockSpec((None, tr, l), lambda i, k, ck: (2 * block(k, ck) + ck[0], i, 0)), spec],
            out_specs=[pl.BlockSpec((tr, l), lambda i, k, ck: (i, 0)), spec]),
        out_shape=[_sds((r, l), F32), _sds((4, r, l), BF16)], compiler_params=_params(2, 32),
    )(ck_arr, gp, recv)


def _adam_update(w, gv, m, v):
    nm = ADAM_B1 * m + (1.0 - ADAM_B1) * gv
    nv = ADAM_B2 * v + (1.0 - ADAM_B2) * (gv * gv)
    m_hat = nm / (1.0 - ADAM_B1 ** ADAM_STEP)
    v_hat = nv / (1.0 - ADAM_B2 ** ADAM_STEP)
    return -ADAM_LR * (m_hat / (jnp.sqrt(v_hat) + ADAM_EPS) + ADAM_WD * w), nm, nv


def _rs_finish_adamw(pf, recv, w, m, v, name):
    r, l = pf.shape
    tr = _row_tile(r)

    def body(p_ref, r_ref, w_ref, m_ref, v_ref, g_ref, d_ref, nm_ref, nv_ref):
        gv = ((p_ref[...] + r_ref[0].astype(F32)) + r_ref[1].astype(F32)) + r_ref[2].astype(F32)
        g_ref[...] = gv
        d_ref[...], nm_ref[...], nv_ref[...] = _adam_update(w_ref[...], gv, m_ref[...], v_ref[...])

    spec = pl.BlockSpec((tr, l), lambda i: (i, 0))
    return _pallas_call(
        body, name=name, grid=(r // tr,),
        in_specs=[spec, pl.BlockSpec((3, tr, l), lambda i: (0, i, 0)), spec, spec, spec], out_specs=[spec] * 4,
        out_shape=[_plain((r, l), F32)] * 4, compiler_params=_params(1, 32),
    )(pf, recv, w, m, v)


SMALL_ROWS = dict(norm_g=(0, 2), mem_norm_g=(2, 4), final_g=(4, 5), conv_w=(5, 8))
LOSS_ROWS = (8, 16)


def _sum_adamw_small(g, ck_arr, states):
    names = list(SMALL_ROWS)
    n_dev, n_rows, _ = g.shape

    def body(ck_ref, g_ref, gc_ref, *refs):
        ins, loss_ref, outs = refs[:3 * len(names)], refs[3 * len(names)], refs[3 * len(names) + 1:]

        def total(ref, lo, hi):
            acc = ref[0, lo:hi, :]
            for j in range(1, n_dev):
                acc = acc + ref[j, lo:hi, :]
            return acc

        loss_ref[...] = total(gc_ref, *LOSS_ROWS)
        for i, n in enumerate(names):
            gv = total(gc_ref if n == "conv_w" else g_ref, *SMALL_ROWS[n])
            w_ref, m_ref, v_ref = ins[3 * i:3 * i + 3]
            g_out, d_out, nm_out, nv_out = outs[4 * i:4 * i + 4]
            g_out[...] = gv
            d_out[...], nm_out[...], nv_out[...] = _adam_update(w_ref[...], gv, m_ref[...], v_ref[...])

    flat = [t for n in names for t in states[n]]
    mine = pl.BlockSpec((n_dev, n_rows, LANES), lambda i, ck: (0, 0, 2 * ck[1] + ck[0]))
    res = _pallas_call(
        body, name="sum_adamw_small",
        grid_spec=pltpu.PrefetchScalarGridSpec(
            num_scalar_prefetch=1, grid=(1,),
            in_specs=[_whole(g.shape), mine] + [_whole(t.shape) for t in flat],
            out_specs=[_whole((SUBLANES, LANES))] + [_whole(states[n][0].shape) for n in names for _ in range(4)]),
        out_shape=[_plain((SUBLANES, LANES), F32)] + [_plain(states[n][0].shape, F32) for n in names for _ in range(4)],
        compiler_params=_params(1, 32),
    )(ck_arr, g, g, *flat)
    return res[0], {n: tuple(res[1 + 4 * i:5 + 4 * i]) for i, n in enumerate(names)}


def _finish(name, pf, recv, w, m, v):
    if name in ("attn_w_in", "conv_w_in"):
        res = _rs_finish_adamw(pf, recv, w.T, m.T, v.T, "rs_finish_adamw_" + name)
        return tuple(t.T for t in res)
    return _rs_finish_adamw(pf, recv, w, m, v, "rs_finish_adamw_" + name)


def kernel(x, mem, positions, norm_g, mem_norm_g, w_mem_kv, attn_w_in, attn_w_out, conv_w_in, conv_w, conv_w_out, final_g, loss_target, m_norm_g, m_mem_norm_g, m_w_mem_kv, m_attn_w_in, m_attn_w_out, m_conv_w_in, m_conv_w, m_conv_w_out, m_final_g, v_norm_g, v_mem_norm_g, v_w_mem_kv, v_attn_w_in, v_attn_w_out, v_conv_w_in, v_conv_w, v_conv_w_out, v_final_g):
    px, py, pc = _position()
    me = 4 * px + 2 * py + pc
    ck_arr = jnp.stack([pc, 2 * px + py]).astype(jnp.int32)
    x, mem, pos, tgt = x[0], mem[0], positions[0], loss_target[0]

    wg_in0 = _all_gather_relay(attn_w_in[0].astype(BF16), "gather_w_in0")
    late = [attn_w_out[0].astype(BF16), conv_w_in[0].astype(BF16), conv_w_out[0].astype(BF16),
            w_mem_kv.astype(BF16).reshape(-1, w_mem_kv.shape[2]), jnp.pad(conv_w[0], ((0, 5), (0, 0)))]
    lands = [lax.dynamic_update_slice(lax.empty((N_DEV,) + t.shape, t.dtype), t[None], (me, 0, 0)) for t in late]
    late_weights, late_token = _split_start(late, lands, _plan_gather_own, 4, wg_in0, "gather_late_start")

    tabs = _rope_tables(pos)
    g0, g1 = norm_g[0:1], norm_g[1:2]

    hn0, qs, ks, vs, tabs_v, qm0, z0 = _inproj_attn(x, g0, wg_in0, tabs, late_token)
    os_, ls_ = [], []
    for j, d in enumerate(DILATIONS):
        if j == 2:
            _, lands = _split_wait(late_weights, _plan_gather_own, ls_[1], "gather_late_wait")
            late_weights, late_token = _split_start([], lands, _plan_gather_pass, 3, ls_[1], "gather_late_pass_start")
        o, l = _attn_fwd(qs[j], ks[j], vs[j], d, late_token)
        os_.append(o)
        ls_.append(l)

    _, gathered = _split_wait(late_weights, _plan_gather_pass, ls_[2], "gather_late_pass_wait")
    wg_out0, wg_in1, wg_out1, wg_kv, cw_all = gathered
    w_out1 = wg_out1.reshape(-1, wg_out1.shape[2])
    cw = cw_all[:, 0:3].transpose(1, 0, 2).reshape(3, -1)
    kv = _memkv_fwd(mem, mem_norm_g, wg_kv)
    y0, y0_t, mo0, h1 = _post_attn(os_, ls_, qm0, kv[0], z0, x, wg_out0)

    hn1, bg, cg, u, qm1, z1 = _inproj_conv(h1, g1, wg_in1)
    y1, y1_t, mo1, dh2, dh2b, loss_acc, d_final_g = _post_conv_loss(
        bg, cg, u, qm1, kv[1], z1, h1, w_out1, cw, final_g.reshape(1, -1), tgt)

    d_w_out1 = _wgrad_rows(y1_t, dh2b, "wgrad_out1")
    dz1, dbg, dconv, dqm1, dkv1 = _bwd_post_conv(dh2b, w_out1, bg, cg, u, z1, qm1, kv[1], mo1, cw)
    dcg, du, dcw = _bwd_conv(dconv, cg, u, cw)
    dh1, dg1, dh1b, dproj1_t = _dgrad_norm([(dbg, 1), (dcg, 1), (du, 1), (dqm1, 1), (dz1, 1)], wg_in1, h1, g1, dh2,
                                           dh2, True, "dgrad_norm_conv")
    d_w_in1 = _wgrad_shards_t(dproj1_t, hn1, wg_in1.shape[2], "wgrad_in1")

    d_w_out0 = _wgrad_cols(y0_t, dh1b, wg_out0.shape[2], "wgrad_out0")

    gw = GROUP_WIDTH
    ones = (jnp.arange(gw)[:, None] // HEAD_DIM == jnp.arange(gw)[None, :] // HEAD_DIM).astype(BF16)
    res = _bwd_post_attn(dh1b, wg_out0, z0, os_, ls_, qm0, kv[0], mo0, ones, dg1)
    dz0, dos, dls, dqm0, dkv0 = res[0], res[1:4], res[4:7], res[7], res[8]
    d_w_kv, d_mem_g = _memkv_bwd(jnp.stack([dkv0, dkv1]), wg_kv, mem, mem_norm_g)

    names1 = ["conv_w_in", "conv_w_out", "attn_w_out", "w_mem_kv"]
    grads1 = [d_w_in1, d_w_out1, d_w_out0, d_w_kv]
    started, token = _split_start(grads1, [lax.empty((4,) + g.shape[1:], g.dtype) for g in grads1],
                                  _plan_to_sibling, 4, d_mem_g, "rs1_sibling_start")

    dqs, dks, dvs = [], [], []
    for j, d in enumerate(DILATIONS):
        if j == 1:
            grads1, from_sibling = _split_wait(started, _plan_to_sibling, dqs[0][0], "rs1_sibling_wait")
            parts1 = [_rs_add_sibling(g, r, ck_arr, "rs_add_sibling_" + n)
                      for g, r, n in zip(grads1, from_sibling, names1)]
            pbs1 = [pb for _, pb in parts1]
            started, token = _split_start(pbs1, [lax.empty((3,) + p.shape[1:], p.dtype) for p in pbs1],
                                          _plan_to_chips, 3, dg1, "rs1_chips_start")
        dq, dk, dv = _attn_bwd(qs[j], ks[j], vs[j], ls_[j], dos[j], dls[j], tabs_v[j], d, token)
        dqs.append((dq, d))
        dks.append((dk, d))
        dvs.append((dv, d))
    pieces0 = dqs + dks + dvs + [(dqm0, 1), (dz0, 1)]
    dproj0_t = _assemble_dproj_t(pieces0, N_DEV * wg_in0.shape[2], "assemble_dproj_attn")
    d_w_in0 = _wgrad_shards_t(dproj0_t, hn0, wg_in0.shape[2], "wgrad_in0")

    names0 = ["attn_w_in"]
    grads0 = [d_w_in0]
    started0, token0 = _split_start(grads0, [lax.empty((4,) + g.shape[1:], g.dtype) for g in grads0],
                                    _plan_to_sibling, 4, dg1, "rs0_sibling_start")
    _, from_chips1 = _split_wait(started, _plan_to_chips, token0, "rs1_chips_wait")
    shard = dict(attn_w_in=(attn_w_in[0], m_attn_w_in[0], v_attn_w_in[0]),
                 attn_w_out=(attn_w_out[0], m_attn_w_out[0], v_attn_w_out[0]),
                 conv_w_in=(conv_w_in[0], m_conv_w_in[0], v_conv_w_in[0]),
                 conv_w_out=(conv_w_out[0], m_conv_w_out[0], v_conv_w_out[0]),
                 w_mem_kv=tuple(t.reshape(-1, t.shape[2]) for t in (w_mem_kv, m_w_mem_kv, v_w_mem_kv)))
    big = {}
    for n, (pf, _), r in zip(names1, parts1, from_chips1):
        big[n] = _finish(n, pf, r, *shard[n])

    grads0, from_sibling = _split_wait(started0, _plan_to_sibling, big["conv_w_out"][1], "rs0_sibling_wait")
    parts0 = [_rs_add_sibling(g, r, ck_arr, "rs_add_sibling_" + n) for g, r, n in zip(grads0, from_sibling, names0)]
    pbs0 = [pb for _, pb in parts0]
    started0, token0 = _split_start(pbs0, [lax.empty((3,) + p.shape[1:], p.dtype) for p in pbs0],
                                    _plan_to_chips, 3, dg1, "rs0_chips_start")
    dx, dg0 = _dgrad_norm(pieces0, wg_in0, x, g0, dh1, token0, False, "dgrad_norm_attn")

    small_part = jnp.concatenate([dg0, dg1, d_mem_g.reshape(2, -1), d_final_g, dcw[0:3]], axis=0)
    small_part = jnp.concatenate([small_part, jnp.broadcast_to(loss_acc[0, 0], small_part.shape)], axis=0)
    small_w = dict(norm_g=(norm_g, m_norm_g, v_norm_g), mem_norm_g=(mem_norm_g, m_mem_norm_g, v_mem_norm_g),
                   conv_w=(conv_w, m_conv_w, v_conv_w), final_g=(final_g, m_final_g, v_final_g))
    loss_tile, small_res = _sum_adamw_small(
        _all_gather([small_part], dg0, "gather_small_grads")[0], ck_arr,
        {n: tuple(t.reshape(-1, t.shape[-1]) for t in wmv) for n, wmv in small_w.items()})
    loss = loss_tile[0, 0]
    for n, wmv in small_w.items():
        big[n] = tuple(t.reshape(wmv[0].shape) for t in small_res[n])

    _, from_chips0 = _split_wait(started0, _plan_to_chips, big["final_g"][1], "rs0_chips_wait")
    for n, (pf, _), r in zip(names0, parts0, from_chips0):
        big[n] = _finish(n, pf, r, *shard[n])
    for n in ("attn_w_in", "attn_w_out", "conv_w_in", "conv_w_out"):
        big[n] = tuple(t[None] for t in big[n])
    big["w_mem_kv"] = tuple(t.reshape(w_mem_kv.shape) for t in big["w_mem_kv"])

    order = ["norm_g", "mem_norm_g", "w_mem_kv", "attn_w_in", "attn_w_out", "conv_w_in", "conv_w", "conv_w_out", "final_g"]
    return (loss, dx[None], *[big[n][0] for n in order], *[big[n][1] for n in order],
            *[big[n][2] for n in order], *[big[n][3] for n in order])
```

```python
import jax
import jax.numpy as jnp
from jax import lax
from jax.experimental import pallas as pl
from jax.experimental.pallas import tpu as pltpu

F32 = jnp.float32
BF16 = jnp.bfloat16

N_DEV = 8
D_MODEL = 1024
HEAD_DIM = 64
ROT_DIM = HEAD_DIM // 4
ROPE_THETA = 500000.0
DILATIONS = (1, 4, 16)
HEADS_PER_GROUP = 8
GROUP_WIDTH = HEADS_PER_GROUP * HEAD_DIM
BLOCK = 128
N_MEM = 256
MEM_HEADS = 4
MEM_WIDTH = MEM_HEADS * HEAD_DIM
CONV_WIDTH = D_MODEL
EPS = 1e-6
SCALE = HEAD_DIM ** -0.5
NEG = -1e30

ADAM_LR = 0.001
ADAM_B1 = 0.9
ADAM_B2 = 0.999
ADAM_EPS = 1e-08
ADAM_WD = 0.01
ADAM_STEP = 10

ROW_TILE = 256
WGRAD_SHARDS = 4
LANES = 128
MESH = pl.DeviceIdType.MESH
ANY = pl.BlockSpec(memory_space=pl.ANY)


def _pallas_call(body, **kw):
    call = pl.pallas_call(body, **kw)

    def run(*args):
        pinned = [pltpu.with_memory_space_constraint(a, pltpu.HBM) if jnp.issubdtype(a.dtype, jnp.floating) else a
                  for a in args]
        return call(*pinned)

    return run


def _dot(a, b):
    return lax.dot_general(a, b, (((1,), (0,)), ((), ())), preferred_element_type=F32)


def _dot_nt(a, b):
    return lax.dot_general(a, b, (((1,), (1,)), ((), ())), preferred_element_type=F32)


def _dot_tn(a, b):
    return lax.dot_general(a, b, (((0,), (0,)), ((), ())), preferred_element_type=F32)


def _params(n_grid, vmem_mb=48):
    return pltpu.CompilerParams(dimension_semantics=("arbitrary",) * n_grid, vmem_limit_bytes=vmem_mb << 20)


def _rows(width, tm=ROW_TILE):
    return pl.BlockSpec((tm, width), lambda i: (i, 0))


def _view_rows(width, d, tm=ROW_TILE):
    return pl.BlockSpec((tm // d, d * width), lambda i: (i, 0))


def _whole(shape):
    return pl.BlockSpec(shape, lambda *_: (0,) * len(shape))


def _resident(shape):
    return pl.BlockSpec(shape, lambda *_: (0,) * len(shape), pipeline_mode=pl.Buffered(1))


def _sds(shape, dtype):
    return pltpu.HBM(shape, dtype)


def _plain(shape, dtype):
    return jax.ShapeDtypeStruct(shape, dtype)


def _silu_parts(z):
    sg = jax.nn.sigmoid(z)
    return z * sg, sg * (1.0 + z * (1.0 - sg))


def _to_view(scr, val, out_ref, d):
    tm, w = val.shape
    if d == 1:
        out_ref[...] = val.astype(out_ref.dtype)
        return
    for cb in range(w // LANES):
        scr[cb] = val[:, cb * LANES:(cb + 1) * LANES]
    for r in range(d):
        for cb in range(w // LANES):
            lo = r * w + cb * LANES
            out_ref[:, lo:lo + LANES] = scr[cb, pl.ds(r, tm // d, stride=d), :].astype(out_ref.dtype)


def _from_view(scr, in_ref, d):
    if d == 1:
        return in_ref[...].astype(F32)
    nc, tm, _ = scr.shape
    w = nc * LANES
    for r in range(d):
        for cb in range(nc):
            lo = r * w + cb * LANES
            scr[cb, pl.ds(r, tm // d, stride=d), :] = in_ref[:, lo:lo + LANES].astype(F32)
    return jnp.concatenate([scr[cb] for cb in range(nc)], axis=1)


def _view_scratch(width, tm=ROW_TILE):
    return pltpu.VMEM((width // LANES, tm, LANES), F32)


def _rope_tables(pos):
    half = ROT_DIM // 2
    inv_freq = ROPE_THETA ** (-jnp.arange(half, dtype=F32) * (2.0 / ROT_DIM))
    ang = pos.astype(F32)[:, None] * inv_freq
    cos, sin = jnp.cos(ang), jnp.sin(ang)
    s = pos.shape[0]
    z8 = jnp.zeros((s, half), F32)
    rest = HEAD_DIM - ROT_DIM
    cosf = jnp.concatenate([cos, cos, jnp.ones((s, rest), F32)], axis=1)
    sa = jnp.concatenate([-sin, z8, jnp.zeros((s, rest), F32)], axis=1)
    sb = jnp.concatenate([z8, sin, jnp.zeros((s, rest), F32)], axis=1)
    return tuple(jnp.tile(t, (1, LANES // HEAD_DIM)) for t in (cosf, sa, sb))


def _rope_fwd(t, cv, sav, sbv):
    w = t.shape[1]
    return t * cv + pltpu.roll(t, w - ROT_DIM // 2, 1) * sav + pltpu.roll(t, ROT_DIM // 2, 1) * sbv


def _rope_bwd(g, cv, sav, sbv):
    w = g.shape[1]
    return g * cv + pltpu.roll(g * sav, ROT_DIM // 2, 1) + pltpu.roll(g * sbv, w - ROT_DIM // 2, 1)


def _joined_columns(wg_ref):
    assert wg_ref.shape[2] % LANES == 0
    return jnp.concatenate([wg_ref[j] for j in range(N_DEV)], axis=1)


def _join_once(wg_ref, w_scr):
    c = wg_ref.shape[2]

    @pl.when(pl.program_id(0) == 0)
    def _():
        for j in range(N_DEV):
            w_scr[:, j * c:(j + 1) * c] = wg_ref[j]


def _inproj_attn(x, g, wg, tabs, token):
    s, d_model = x.shape
    gw = GROUP_WIDTH
    n = N_DEV * wg.shape[2]
    nz = n - 9 * gw - MEM_WIDTH
    reps = gw // LANES
    tm = ROW_TILE

    def body(_, x_ref, g_ref, w_ref, c_ref, sa_ref, sb_ref, hn_ref, *rest):
        outs, (wj, scr, tscr) = rest[:-3], rest[-3:]
        q_refs, k_refs, v_refs, t_refs, qm_ref, z_ref = outs[0:3], outs[3:6], outs[6:9], outs[9:18], outs[18], outs[19]
        _join_once(w_ref, wj)
        xb = x_ref[...]
        r = lax.rsqrt(jnp.mean(xb * xb, axis=-1, keepdims=True) + EPS)
        hn = ((xb * r) * g_ref[...]).astype(BF16)
        hn_ref[...] = hn
        proj = lambda lo, hi: _dot(hn, wj[:, lo:hi])
        tab = (c_ref[...], sa_ref[...], sb_ref[...])
        cv, sav, sbv = [jnp.tile(t, (1, reps)) for t in tab]
        for j, d in enumerate(DILATIONS):
            tq = _rope_fwd(proj(j * gw, (j + 1) * gw), cv, sav, sbv)
            _to_view(scr, tq * SCALE, q_refs[j], d)
            tk = _rope_fwd(proj((3 + j) * gw, (4 + j) * gw), cv, sav, sbv)
            _to_view(scr, tk, k_refs[j], d)
            _to_view(scr, proj((6 + j) * gw, (7 + j) * gw), v_refs[j], d)
            for i in range(3):
                _to_view(tscr, tab[i], t_refs[3 * j + i], d)
        qm_ref[...] = proj(9 * gw, 9 * gw + MEM_WIDTH).astype(BF16)
        z_ref[...] = proj(9 * gw + MEM_WIDTH, n)

    views = [_sds((s // d, d * gw), BF16) for d in DILATIONS]
    tviews = [_sds((s // d, d * LANES), F32) for d in DILATIONS for _ in range(3)]
    out_shape = [_sds((s, d_model), BF16)] + views * 3 + tviews + [_sds((s, MEM_WIDTH), BF16), _sds((s, nz), F32)]
    vspecs = [_view_rows(gw, d, tm) for d in DILATIONS]
    tspecs = [_view_rows(LANES, d, tm) for d in DILATIONS for _ in range(3)]
    out_specs = [_rows(d_model, tm)] + vspecs * 3 + tspecs + [_rows(MEM_WIDTH, tm), _rows(nz, tm)]
    res = _pallas_call(
        body, name="inproj_attn", grid=(s // tm,), out_shape=out_shape,
        in_specs=[ANY, _rows(d_model, tm), _whole((1, d_model)), _resident(wg.shape)] + [_rows(LANES, tm)] * 3,
        out_specs=out_specs,
        scratch_shapes=[pltpu.VMEM((d_model, n), BF16), _view_scratch(gw, tm), _view_scratch(LANES, tm)],
        compiler_params=_params(1, 60),
    )(token, x, g, wg, *tabs)
    tabs_v = [res[10 + 3 * j:13 + 3 * j] for j in range(3)]
    return res[0], res[1:4], res[4:7], res[7:10], tabs_v, res[19], res[20]


def _band_mask(n_keys):
    qi = lax.broadcasted_iota(jnp.int32, (BLOCK, n_keys), 0)
    kj = lax.broadcasted_iota(jnp.int32, (BLOCK, n_keys), 1)
    if n_keys == BLOCK:
        return kj <= qi
    return jnp.logical_or(jnp.logical_and(kj < BLOCK, kj >= qi), jnp.logical_and(kj >= BLOCK, (kj - BLOCK) <= qi))


def _low_head_lanes():
    return lax.broadcasted_iota(jnp.int32, (1, LANES), 1) < HEAD_DIM


def _split_pair(t, low):
    zero = jnp.zeros_like(t)
    return jnp.where(low, t, zero), jnp.where(low, zero, t)


def _pair_specs(d, nb, w):
    if nb == 1:
        return pl.BlockSpec((BLOCK, 2 * w), lambda n: (0, n)), None
    half = nb // 2
    two = pl.BlockSpec((2 * BLOCK, w), lambda n: (n % half, n // half))
    before = pl.BlockSpec((BLOCK, w), lambda n: (jnp.maximum(2 * (n % half) - 1, 0), n // half))
    return two, before


def _head_tiles(w, col0):
    return ([slice(p * LANES, (p + 1) * LANES) for p in range(w // LANES)],
            [slice(col0 + p * LANES, col0 + (p + 1) * LANES) for p in range(w // LANES)])


def _attend_fwd(q_ref, o_ref, lse_ref, rows, col0, kk, vv):
    w = kk.shape[1]
    valid = _band_mask(kk.shape[0])
    low = _low_head_lanes()
    pairs, qcols = _head_tiles(w, col0)
    qs_ = [h for qc in qcols for h in _split_pair(q_ref[rows, qc], low)]
    k2s = [kk[:, pr] for pr in pairs for _ in range(2)]
    scs = [jnp.where(valid, _dot_nt(qh, k2), NEG) for qh, k2 in zip(qs_, k2s)]
    ms = [jnp.max(sc, axis=-1, keepdims=True) for sc in scs]
    ps = [jnp.exp(sc - m) for sc, m in zip(scs, ms)]
    ls = [jnp.sum(p, axis=-1, keepdims=True) for p in ps]
    pns = [(p * (1.0 / l)).astype(BF16) for p, l in zip(ps, ls)]
    for i, (pr, qc) in enumerate(zip(pairs, qcols)):
        v2 = vv[:, pr]
        a, b = 2 * i, 2 * i + 1
        o_ref[rows, qc] = jnp.where(low, _dot(pns[a], v2), _dot(pns[b], v2))
        lse_ref[rows, qc] = jnp.where(low, ms[a] + jnp.log(ls[a]), ms[b] + jnp.log(ls[b]))


TOP, BOTTOM = slice(0, BLOCK), slice(BLOCK, 2 * BLOCK)


def _attn_fwd(q, k, v, d, token):
    ln, dw = q.shape
    w = dw // d
    nb = ln // BLOCK
    two, before = _pair_specs(d, nb, w)

    def body_streams(_, q_ref, kc_ref, vc_ref, o_ref, lse_ref):
        for sb in range(2):
            cols = slice(sb * w, (sb + 1) * w)
            _attend_fwd(q_ref, o_ref, lse_ref, TOP, sb * w, kc_ref[:, cols], vc_ref[:, cols])

    def body_blocks(_, q_ref, kp_ref, kc_ref, vp_ref, vc_ref, o_ref, lse_ref):
        first = pl.program_id(0) % (nb // 2) == 0
        pl.when(first)(lambda: _attend_fwd(q_ref, o_ref, lse_ref, TOP, 0, kc_ref[TOP, :], vc_ref[TOP, :]))
        pl.when(jnp.logical_not(first))(lambda: _attend_fwd(
            q_ref, o_ref, lse_ref, TOP, 0, jnp.concatenate([kp_ref[...], kc_ref[TOP, :]], axis=0),
            jnp.concatenate([vp_ref[...], vc_ref[TOP, :]], axis=0)))
        _attend_fwd(q_ref, o_ref, lse_ref, BOTTOM, 0, kc_ref[...], vc_ref[...])

    if nb == 1:
        body, in_specs, args = body_streams, [ANY, two, two, two], (token, q, k, v)
    else:
        body, in_specs, args = body_blocks, [ANY, two, before, two, before, two], (token, q, k, k, v, v)
    return _pallas_call(
        body, name=f"attn_fwd_d{d}", grid=(d * nb // 2,), out_shape=[_sds((ln, dw), F32)] * 2,
        in_specs=in_specs, out_specs=[two, two], compiler_params=_params(1, 32),
    )(*args)


def _memkv_fwd(mem, g, w):
    n_layers = g.shape[0]
    rows = D_MODEL // N_DEV

    def body(mem_ref, g_ref, w_ref, kv_ref):
        mb = mem_ref[...]
        r = lax.rsqrt(jnp.mean(mb * mb, axis=-1, keepdims=True) + EPS)
        mn = ((mb * r) * g_ref[...]).astype(BF16)
        kv_ref[...] = _dot(mn, w_ref[...].reshape(D_MODEL, 2 * MEM_WIDTH)).astype(BF16)

    return _pallas_call(
        body, name="memkv_fwd", grid=(n_layers,),
        out_shape=_plain((n_layers, N_MEM, 2 * MEM_WIDTH), BF16),
        in_specs=[_whole(mem.shape), pl.BlockSpec((None, 1, D_MODEL), lambda l: (l, 0, 0)),
                  pl.BlockSpec((N_DEV, rows, 2 * MEM_WIDTH), lambda l: (0, l, 0))],
        out_specs=pl.BlockSpec((None, N_MEM, 2 * MEM_WIDTH), lambda l: (l, 0, 0)),
        compiler_params=_params(1, 32),
    )(mem, g.reshape(n_layers, 1, D_MODEL), w)


def _mix_groups(os_, ls_):
    mx = jnp.maximum(jnp.maximum(ls_[0], ls_[1]), ls_[2])
    es = [jnp.exp(t - mx) for t in ls_]
    inv = 1.0 / (es[0] + es[1] + es[2])
    ws = [e * inv for e in es]
    mix = ws[0] * os_[0] + ws[1] * os_[1] + ws[2] * os_[2]
    return ws, mix


MEM_PAIRS = [slice(p * LANES, (p + 1) * LANES) for p in range(MEM_WIDTH // LANES)]


def _mem_probs(qhs, k2s):
    scs = [_dot_nt(qh, k2) * SCALE for qh, k2 in zip(qhs, k2s)]
    es = [jnp.exp(sc - jnp.max(sc, axis=-1, keepdims=True)) for sc in scs]
    return [e * (1.0 / jnp.sum(e, axis=-1, keepdims=True)) for e in es]


def _mem_attn_into(qm, kv_ref, mo_ref):
    low = _low_head_lanes()
    qhs = [h for pr in MEM_PAIRS for h in _split_pair(qm[:, pr], low)]
    k2s = [kv_ref[:, pr] for pr in MEM_PAIRS for _ in range(2)]
    ps = [p.astype(BF16) for p in _mem_probs(qhs, k2s)]
    for i, pr in enumerate(MEM_PAIRS):
        v2 = kv_ref[:, MEM_WIDTH + i * LANES:MEM_WIDTH + (i + 1) * LANES]
        mo_ref[:, pr] = jnp.where(low, _dot(ps[2 * i], v2), _dot(ps[2 * i + 1], v2))


def _mem_attn_bwd(qm, kv_ref, dmem, dqm_ref, dkv_ref):
    low = _low_head_lanes()
    dmb = dmem.astype(BF16)
    vps = [slice(MEM_WIDTH + i * LANES, MEM_WIDTH + (i + 1) * LANES) for i in range(len(MEM_PAIRS))]
    qhs = [h for pr in MEM_PAIRS for h in _split_pair(qm[:, pr], low)]
    dhs = [h for pr in MEM_PAIRS for h in _split_pair(dmb[:, pr], low)]
    k2s = [kv_ref[:, pr] for pr in MEM_PAIRS for _ in range(2)]
    v2s = [kv_ref[:, vp] for vp in vps for _ in range(2)]
    ps = _mem_probs(qhs, k2s)
    dps = [_dot_nt(dh, v2) for dh, v2 in zip(dhs, v2s)]
    dss = [(p * (dp - jnp.sum(dp * p, axis=-1, keepdims=True)) * SCALE).astype(BF16) for p, dp in zip(ps, dps)]
    pbs = [p.astype(BF16) for p in ps]
    for i, (pr, vp) in enumerate(zip(MEM_PAIRS, vps)):
        a, b = 2 * i, 2 * i + 1
        dqm_ref[:, pr] = jnp.where(low, _dot(dss[a], k2s[a]), _dot(dss[b], k2s[b])).astype(BF16)
        dkv_ref[:, pr] += _dot_tn(dss[a], qhs[a]) + _dot_tn(dss[b], qhs[b])
        dkv_ref[:, vp] += _dot_tn(pbs[a], dhs[a]) + _dot_tn(pbs[b], dhs[b])


def _post_attn(os_, ls_, qm, kv, z, x, wg_out):
    s, d_model = x.shape
    gw = GROUP_WIDTH
    nb = gw + MEM_WIDTH
    tm = ROW_TILE

    def body(o0, o1, o2, l0, l1, l2, qm_ref, kv_ref, z_ref, x_ref, w_ref, y_ref, yt_ref, mo_ref, h_ref, s0, s1):
        ov, lv = [], []
        for o_ref, l_ref, d in zip((o0, o1, o2), (l0, l1, l2), DILATIONS):
            ov.append(_from_view(s0, o_ref, d))
            lv.append(_from_view(s1, l_ref, d))
        _, mix = _mix_groups(ov, lv)
        _mem_attn_into(qm_ref[...], kv_ref, mo_ref)
        sz, _ = _silu_parts(z_ref[...])
        y_ref[:, :gw] = (mix * sz[:, :gw]).astype(BF16)
        y_ref[:, gw:] = (mo_ref[...] * sz[:, gw:]).astype(BF16)
        y = y_ref[...]
        yt_ref[...] = y.T
        h_ref[...] = x_ref[...] + _dot(y, _joined_columns(w_ref))

    vspecs = [_view_rows(gw, d) for d in DILATIONS]
    return _pallas_call(
        body, name="post_attn", grid=(s // tm,),
        out_shape=[_sds((s, nb), BF16), _sds((nb, s), BF16), _sds((s, MEM_WIDTH), F32), _sds((s, d_model), F32)],
        in_specs=vspecs * 2 + [_rows(MEM_WIDTH), _whole(kv.shape), _rows(nb), _rows(d_model), _whole(wg_out.shape)],
        out_specs=[_rows(nb), pl.BlockSpec((nb, tm), lambda i: (0, i)), _rows(MEM_WIDTH), _rows(d_model)],
        scratch_shapes=[_view_scratch(gw), _view_scratch(gw)],
        compiler_params=_params(1, 40),
    )(*os_, *ls_, qm, kv, z, x, wg_out)


def _inproj_conv(x, g, wg):
    s, d_model = x.shape
    c = CONV_WIDTH
    n = N_DEV * wg.shape[2]
    nz = n - 3 * c - MEM_WIDTH
    tm = ROW_TILE

    def body(x_ref, g_ref, w_ref, hn_ref, bg_ref, cg_ref, u_ref, qm_ref, z_ref, wj):
        _join_once(w_ref, wj)
        xb = x_ref[...]
        r = lax.rsqrt(jnp.mean(xb * xb, axis=-1, keepdims=True) + EPS)
        hn = ((xb * r) * g_ref[...]).astype(BF16)
        hn_ref[...] = hn
        bg_ref[...] = _dot(hn, wj[:, 0:c])
        cg_ref[...] = _dot(hn, wj[:, c:2 * c])
        u_ref[...] = _dot(hn, wj[:, 2 * c:3 * c])
        qm_ref[...] = _dot(hn, wj[:, 3 * c:3 * c + MEM_WIDTH]).astype(BF16)
        z_ref[...] = _dot(hn, wj[:, 3 * c + MEM_WIDTH:])

    return _pallas_call(
        body, name="inproj_conv", grid=(s // tm,),
        out_shape=[_sds((s, d_model), BF16)] + [_sds((s, c), F32)] * 3 + [_sds((s, MEM_WIDTH), BF16), _sds((s, nz), F32)],
        in_specs=[_rows(d_model), _whole((1, d_model)), _resident(wg.shape)],
        out_specs=[_rows(d_model)] + [_rows(c)] * 3 + [_rows(MEM_WIDTH), _rows(nz)],
        scratch_shapes=[pltpu.VMEM((d_model, n), BF16)],
        compiler_params=_params(1, 60),
    )(x, g, wg)


HALO = 8


def _halo_before(width, tm=ROW_TILE):
    return pl.BlockSpec((HALO, width), lambda i: (jnp.maximum(i * (tm // HALO) - 1, 0), 0))


def _halo_after(width, n_rows, tm=ROW_TILE):
    return pl.BlockSpec((HALO, width), lambda i: (jnp.minimum((i + 1) * (tm // HALO), n_rows // HALO - 1), 0))


def _conv_taps(cg_ref, u_ref, cgh_ref, uh_ref, i):
    a = cg_ref[...] * u_ref[...]
    ah = jnp.where(i > 0, cgh_ref[...] * uh_ref[...], 0.0)
    row = lax.broadcasted_iota(jnp.int32, a.shape, 0)
    a1 = jnp.where(row == 0, ah[HALO - 1:HALO], pltpu.roll(a, 1, 0))
    a2 = jnp.where(row == 0, ah[HALO - 2:HALO - 1], jnp.where(row == 1, ah[HALO - 1:HALO], pltpu.roll(a, 2, 0)))
    return a, a1, a2


def _post_conv_loss(bg, cg, u, qm, kv, z, h1, w_out, cw, gf, tgt):
    s, d = h1.shape
    c = CONV_WIDTH
    nb = c + MEM_WIDTH
    tm = ROW_TILE

    def body(bg_ref, cg_ref, u_ref, cgh_ref, uh_ref, qm_ref, kv_ref, z_ref, h_ref, w_ref, cw_ref, gf_ref, t_ref,
             y_ref, yt_ref, mo_ref, dh_ref, dhb_ref, loss_ref, dgf_ref):
        i = pl.program_id(0)
        a, a1, a2 = _conv_taps(cg_ref, u_ref, cgh_ref, uh_ref, i)
        conv = cw_ref[0:1, :] * a2 + cw_ref[1:2, :] * a1 + cw_ref[2:3, :] * a
        mix = bg_ref[...] * conv
        _mem_attn_into(qm_ref[...], kv_ref, mo_ref)
        sz, _ = _silu_parts(z_ref[...])
        y_ref[:, :c] = (mix * sz[:, :c]).astype(BF16)
        y_ref[:, c:] = (mo_ref[...] * sz[:, c:]).astype(BF16)
        y = y_ref[...]
        yt_ref[...] = y.T
        h2 = h_ref[...] + _dot(y, w_ref[...])
        r = lax.rsqrt(jnp.mean(h2 * h2, axis=-1, keepdims=True) + EPS)
        nh = h2 * r
        gfv = gf_ref[...]
        diff = nh * gfv - t_ref[...]
        dout = diff * (1.0 / d)
        dn = dout * gfv
        dh2 = r * dn - h2 * ((r * r * r) * jnp.mean(dn * h2, axis=-1, keepdims=True))
        dh_ref[...] = dh2
        dhb_ref[...] = dh2.astype(BF16)

        @pl.when(i == 0)
        def _():
            loss_ref[...] = jnp.zeros_like(loss_ref)
            dgf_ref[...] = jnp.zeros_like(dgf_ref)

        loss_ref[...] += 0.5 * jnp.sum(jnp.mean(diff * diff, axis=-1, keepdims=True))
        dgf_ref[...] += jnp.sum(dout * nh, axis=0, keepdims=True)

    return _pallas_call(
        body, name="post_conv_loss", grid=(s // tm,),
        out_shape=[_sds((s, nb), BF16), _sds((nb, s), BF16), _sds((s, MEM_WIDTH), F32), _sds((s, d), F32),
                   _sds((s, d), BF16), _plain((8, LANES), F32), _plain((1, d), F32)],
        in_specs=[_rows(c)] * 3 + [_halo_before(c)] * 2 + [_rows(MEM_WIDTH), _whole(kv.shape), _rows(nb), _rows(d),
                  _whole(w_out.shape), _whole(cw.shape), _whole((1, d)), _rows(d)],
        out_specs=[_rows(nb), pl.BlockSpec((nb, tm), lambda i: (0, i)), _rows(MEM_WIDTH), _rows(d), _rows(d),
                   _whole((8, LANES)), _whole((1, d))],
        compiler_params=_params(1, 48),
    )(bg, cg, u, cg, u, qm, kv, z, h1, w_out, cw, gf, tgt)


def _bwd_post_conv(dhb, w_out, bg, cg, u, z, qm, kv, mo, cw):
    s = dhb.shape[0]
    c = CONV_WIDTH
    nb = c + MEM_WIDTH

    def body(dh_ref, w_ref, bg_ref, cg_ref, u_ref, cgh_ref, uh_ref, z_ref, qm_ref, kv_ref, mo_ref, cw_ref,
             dz_ref, dbg_ref, dc_ref, dqm_ref, dkv_ref):
        i = pl.program_id(0)

        @pl.when(i == 0)
        def _():
            dkv_ref[...] = jnp.zeros_like(dkv_ref)

        dy = _dot_nt(dh_ref[...], w_ref[...])
        sz, dsz = _silu_parts(z_ref[...])
        a, a1, a2 = _conv_taps(cg_ref, u_ref, cgh_ref, uh_ref, i)
        conv = cw_ref[0:1, :] * a2 + cw_ref[1:2, :] * a1 + cw_ref[2:3, :] * a
        bgv = bg_ref[...]
        dz_ref[:, :c] = (dy[:, :c] * (bgv * conv) * dsz[:, :c]).astype(BF16)
        dz_ref[:, c:] = (dy[:, c:] * mo_ref[...] * dsz[:, c:]).astype(BF16)
        dbr = dy * sz
        dmix = dbr[:, :c]
        dbg_ref[...] = (dmix * conv).astype(BF16)
        dc_ref[...] = dmix * bgv
        _mem_attn_bwd(qm_ref[...], kv_ref, dbr[:, c:], dqm_ref, dkv_ref)

    return _pallas_call(
        body, name="bwd_post_conv", grid=(s // ROW_TILE,),
        out_shape=[_sds((s, nb), BF16), _sds((s, c), BF16), _sds((s, c), F32), _sds((s, MEM_WIDTH), BF16),
                   _plain(kv.shape, F32)],
        in_specs=[_rows(D_MODEL), _whole(w_out.shape)] + [_rows(c)] * 3 + [_halo_before(c)] * 2
                 + [_rows(nb), _rows(MEM_WIDTH), _whole(kv.shape), _rows(MEM_WIDTH), _whole(cw.shape)],
        out_specs=[_rows(nb), _rows(c), _rows(c), _rows(MEM_WIDTH), _whole(kv.shape)],
        compiler_params=_params(1, 48),
    )(dhb, w_out, bg, cg, u, cg, u, z, qm, kv, mo, cw)


def _bwd_conv(dconv, cg, u, cw):
    s, c = dconv.shape
    tm = ROW_TILE
    last = s // tm - 1

    def body(dc_ref, dcn_ref, cg_ref, u_ref, cgh_ref, uh_ref, cw_ref, dcg_ref, du_ref, dcw_ref):
        i = pl.program_id(0)

        @pl.when(i == 0)
        def _():
            dcw_ref[...] = jnp.zeros_like(dcw_ref)

        dc = dc_ref[...]
        dcn = jnp.where(i < last, dcn_ref[...], 0.0)
        row = lax.broadcasted_iota(jnp.int32, dc.shape, 0)
        d1 = jnp.where(row == tm - 1, dcn[0:1], pltpu.roll(dc, tm - 1, 0))
        d2 = jnp.where(row == tm - 1, dcn[1:2], jnp.where(row == tm - 2, dcn[0:1], pltpu.roll(dc, tm - 2, 0)))
        da = cw_ref[2:3, :] * dc + cw_ref[1:2, :] * d1 + cw_ref[0:1, :] * d2
        a, a1, a2 = _conv_taps(cg_ref, u_ref, cgh_ref, uh_ref, i)
        dcg_ref[...] = (da * u_ref[...]).astype(BF16)
        du_ref[...] = (da * cg_ref[...]).astype(BF16)
        dcw_ref[0:1, :] += jnp.sum(dc * a2, axis=0, keepdims=True)
        dcw_ref[1:2, :] += jnp.sum(dc * a1, axis=0, keepdims=True)
        dcw_ref[2:3, :] += jnp.sum(dc * a, axis=0, keepdims=True)

    return _pallas_call(
        body, name="bwd_conv", grid=(s // tm,),
        out_shape=[_sds((s, c), BF16), _sds((s, c), BF16), _plain((8, c), F32)],
        in_specs=[_rows(c), _halo_after(c, s), _rows(c), _rows(c), _halo_before(c), _halo_before(c), _whole(cw.shape)],
        out_specs=[_rows(c), _rows(c), _whole((8, c))], compiler_params=_params(1, 40),
    )(dconv, dconv, cg, u, cg, u, cw)


def _assemble(p_refs, pieces, widths, dp, scr):
    off = 0
    for p_ref, (_, d), wd in zip(p_refs, pieces, widths):
        if d == 1:
            dp[:, off:off + wd] = p_ref[...]
        else:
            dp[:, off:off + wd] = _from_view(scr, p_ref, d).astype(BF16)
        off += wd


def _dgrad_norm(pieces, wg, h, g, dres, token, onward, name):
    s, d_model = h.shape
    n = N_DEV * wg.shape[2]
    tm = ROW_TILE
    widths = [p.shape[1] // d for p, d in pieces]
    assert sum(widths) == n
    n_p = len(pieces)

    def body(_, *refs):
        p_refs = refs[:n_p]
        w_ref, h_ref, g_ref, dr_ref, dh_ref, dg_ref = refs[n_p:n_p + 6]
        dp, scr, wj = refs[-3:]
        _join_once(w_ref, wj)

        @pl.when(pl.program_id(0) == 0)
        def _():
            dg_ref[...] = jnp.zeros_like(dg_ref)

        _assemble(p_refs, pieces, widths, dp, scr)
        dhn = _dot_nt(dp[...], wj[...])
        hb = h_ref[...]
        r = lax.rsqrt(jnp.mean(hb * hb, axis=-1, keepdims=True) + EPS)
        dg_ref[...] += jnp.sum(dhn * (hb * r), axis=0, keepdims=True)
        dn = dhn * g_ref[...]
        dh = dr_ref[...] + r * dn - hb * ((r * r * r) * jnp.mean(dn * hb, axis=-1, keepdims=True))
        dh_ref[...] = dh
        if onward:
            dhb_ref, dpt_ref = refs[n_p + 6:n_p + 8]
            dhb_ref[...] = dh.astype(BF16)
            dpt_ref[...] = dp[...].T

    p_specs = [_view_rows(wd, d) for (_, d), wd in zip(pieces, widths)]
    out_shape = [_plain((s, d_model), F32), _plain((1, d_model), F32)]
    out_specs = [_rows(d_model), _whole((1, d_model))]
    if onward:
        out_shape += [_sds((s, d_model), BF16), _sds((n, s), BF16)]
        out_specs += [_rows(d_model), pl.BlockSpec((n, tm), lambda i: (0, i))]
    return _pallas_call(
        body, name=name, grid=(s // tm,), out_shape=out_shape,
        in_specs=[ANY] + p_specs + [_resident(wg.shape), _rows(d_model), _whole((1, d_model)), _rows(d_model)],
        out_specs=out_specs,
        scratch_shapes=[pltpu.VMEM((tm, n), BF16), _view_scratch(GROUP_WIDTH), pltpu.VMEM((d_model, n), BF16)],
        compiler_params=_params(1, 60),
    )(token, *[p for p, _ in pieces], wg, h, g, dres)


def _assemble_dproj_t(pieces, n, name):
    tm = ROW_TILE
    widths = [p.shape[1] // d for p, d in pieces]
    assert sum(widths) == n
    s = pieces[0][0].shape[0] * pieces[0][1]
    n_p = len(pieces)

    def body(*refs):
        p_refs, (dpt_ref, dp, scr) = refs[:n_p], refs[n_p:]
        _assemble(p_refs, pieces, widths, dp, scr)
        dpt_ref[...] = dp[...].T

    return _pallas_call(
        body, name=name, grid=(s // tm,), out_shape=_sds((n, s), BF16),
        in_specs=[_view_rows(wd, d) for (_, d), wd in zip(pieces, widths)],
        out_specs=pl.BlockSpec((n, tm), lambda i: (0, i)),
        scratch_shapes=[pltpu.VMEM((tm, n), BF16), _view_scratch(GROUP_WIDTH)],
        compiler_params=_params(1, 40),
    )(*[p for p, _ in pieces])


def _wgrad_shards_t(dp_t, h, c, name):
    n, s = dp_t.shape
    d_model = h.shape[1]
    per_step = 2

    def body(a_ref, b_ref, o_ref):
        o_ref[...] = _dot(a_ref[...], b_ref[...]).astype(BF16).reshape(per_step, c, d_model)

    return _pallas_call(
        body, name=name, grid=(N_DEV // per_step,), out_shape=_sds((N_DEV, c, d_model), BF16),
        in_specs=[pl.BlockSpec((per_step * c, s), lambda j: (j, 0)), _resident(h.shape)],
        out_specs=pl.BlockSpec((per_step, c, d_model), lambda j: (j, 0, 0)), compiler_params=_params(1, 40),
    )(dp_t, h)


def _wgrad_cols(a_t, b, c, name):
    m, s = a_t.shape
    assert c % LANES == 0

    def body(a_ref, b_ref, o_ref):
        wide = _dot(a_ref[...], b_ref[...]).astype(BF16)
        for j in range(WGRAD_SHARDS):
            o_ref[j] = wide[:, j * c:(j + 1) * c]

    return _pallas_call(
        body, name=name, grid=(N_DEV // WGRAD_SHARDS,), out_shape=_sds((N_DEV, m, c), BF16),
        in_specs=[_whole(a_t.shape), pl.BlockSpec((s, WGRAD_SHARDS * c), lambda j: (0, j))],
        out_specs=pl.BlockSpec((WGRAD_SHARDS, m, c), lambda j: (j, 0, 0)), compiler_params=_params(1, 40),
    )(a_t, b)


def _wgrad_rows(a_t, b, name):
    m, s = a_t.shape
    n = b.shape[1]
    mr = m // N_DEV

    def body(a_ref, b_ref, o_ref):
        o_ref[...] = _dot(a_ref[...], b_ref[...]).astype(BF16).reshape(WGRAD_SHARDS, mr, n)

    return _pallas_call(
        body, name=name, grid=(N_DEV // WGRAD_SHARDS,), out_shape=_sds((N_DEV, mr, n), BF16),
        in_specs=[pl.BlockSpec((WGRAD_SHARDS * mr, s), lambda j: (j, 0)), _whole(b.shape)],
        out_specs=pl.BlockSpec((WGRAD_SHARDS, mr, n), lambda j: (j, 0, 0)), compiler_params=_params(1, 40),
    )(a_t, b)


def _memkv_bwd(dkv, w, mem, g):
    n_layers = g.shape[0]
    rows = D_MODEL // N_DEV

    def body(dkv_ref, w_ref, mem_ref, g_ref, dw_ref, dg_ref):
        mb = mem_ref[...]
        r = lax.rsqrt(jnp.mean(mb * mb, axis=-1, keepdims=True) + EPS)
        nm = mb * r
        mn = (nm * g_ref[...]).astype(BF16)
        dkvb = dkv_ref[...].astype(BF16)
        dw_ref[...] = _dot_tn(mn, dkvb).astype(BF16).reshape(N_DEV, rows, 2 * MEM_WIDTH)
        dmn = _dot_nt(dkvb, w_ref[...].reshape(D_MODEL, 2 * MEM_WIDTH))
        dg_ref[...] = jnp.sum(dmn * nm, axis=0, keepdims=True)

    lay = lambda *shape: pl.BlockSpec((None,) + shape, lambda l: (l, 0, 0))
    major = pl.BlockSpec((N_DEV, rows, 2 * MEM_WIDTH), lambda l: (0, l, 0))
    return _pallas_call(
        body, name="memkv_bwd", grid=(n_layers,),
        out_shape=[_sds((N_DEV, n_layers * rows, 2 * MEM_WIDTH), BF16), _plain((n_layers, 1, D_MODEL), F32)],
        in_specs=[lay(N_MEM, 2 * MEM_WIDTH), major, _whole(mem.shape), lay(1, D_MODEL)],
        out_specs=[major, lay(1, D_MODEL)],
        compiler_params=_params(1, 32),
    )(dkv, w, mem, g.reshape(n_layers, 1, D_MODEL))


def _bwd_post_attn(dhb, wg_out, z, os_, ls_, qm, kv, mo, head_ones, token):
    s = dhb.shape[0]
    gw = GROUP_WIDTH
    nb = gw + MEM_WIDTH
    tm = ROW_TILE

    def body(_, dh_ref, w_ref, z_ref, o0, o1, o2, l0, l1, l2, qm_ref, kv_ref, mo_ref, bd_ref,
             dz_ref, do0, do1, do2, dl0, dl1, dl2, dqm_ref, dkv_ref, s0, s1):
        @pl.when(pl.program_id(0) == 0)
        def _():
            dkv_ref[...] = jnp.zeros_like(dkv_ref)

        dy = _dot_nt(dh_ref[...], _joined_columns(w_ref))
        ov, lv = [], []
        for o_ref, l_ref, d in zip((o0, o1, o2), (l0, l1, l2), DILATIONS):
            ov.append(_from_view(s0, o_ref, d))
            lv.append(_from_view(s1, l_ref, d))
        ws, mix = _mix_groups(ov, lv)
        sz, dsz = _silu_parts(z_ref[...])
        dz_ref[:, :gw] = (dy[:, :gw] * mix * dsz[:, :gw]).astype(BF16)
        dz_ref[:, gw:] = (dy[:, gw:] * mo_ref[...] * dsz[:, gw:]).astype(BF16)
        dbr = dy * sz
        dmix = dbr[:, :gw]
        t = dmix * mix
        th = t.astype(BF16)
        tl = (t - th.astype(F32)).astype(BF16)
        rs = _dot(th, bd_ref[...]) + _dot(tl, bd_ref[...])
        for wg_, do_ref, dl_ref, d in zip(ws, (do0, do1, do2), (dl0, dl1, dl2), DILATIONS):
            _to_view(s0, wg_ * dmix, do_ref, d)
            _to_view(s1, wg_ * rs, dl_ref, d)
        _mem_attn_bwd(qm_ref[...], kv_ref, dbr[:, gw:], dqm_ref, dkv_ref)

    vspecs = [_view_rows(gw, d) for d in DILATIONS]
    return _pallas_call(
        body, name="bwd_post_attn", grid=(s // tm,),
        out_shape=[_sds((s, nb), BF16)] + [_sds((s // d, d * gw), BF16) for d in DILATIONS]
                  + [_sds((s // d, d * gw), F32) for d in DILATIONS] + [_sds((s, MEM_WIDTH), BF16), _plain(kv.shape, F32)],
        in_specs=[ANY, _rows(D_MODEL), _whole(wg_out.shape), _rows(nb)] + vspecs * 2
                 + [_rows(MEM_WIDTH), _whole(kv.shape), _rows(MEM_WIDTH), _whole(head_ones.shape)],
        out_specs=[_rows(nb)] + vspecs * 2 + [_rows(MEM_WIDTH), _whole(kv.shape)],
        scratch_shapes=[_view_scratch(gw), _view_scratch(gw)],
        compiler_params=_params(1, 48),
    )(token, dhb, wg_out, z, *os_, *ls_, qm, kv, mo, head_ones)


def _attn_bwd(q, k, v, lse, do, dl, tabs, d, token):
    ln, dw = q.shape
    w = dw // d
    nb = ln // BLOCK
    reps = w // LANES
    two, before = _pair_specs(d, nb, w)
    two_t, _ = _pair_specs(d, nb, LANES)

    def attend(q_ref, l_ref, do_ref, dl_ref, dqs, acck, accv, rows, col0, kk, vv, acc_rows):
        valid = _band_mask(kk.shape[0])
        low = _low_head_lanes()
        pairs, qcols = _head_tiles(w, col0)
        cols = [slice(col0 + h * HEAD_DIM, col0 + h * HEAD_DIM + 1) for h in range(HEADS_PER_GROUP)]
        qhs = [h for qc in qcols for h in _split_pair(q_ref[rows, qc], low)]
        dobs = [h for qc in qcols for h in _split_pair(do_ref[rows, qc], low)]
        k2s = [kk[:, pr] for pr in pairs for _ in range(2)]
        v2s = [vv[:, pr] for pr in pairs for _ in range(2)]
        scs = [jnp.where(valid, _dot_nt(qh, k2), NEG) for qh, k2 in zip(qhs, k2s)]
        dps = [_dot_nt(dob, v2) for dob, v2 in zip(dobs, v2s)]
        ps = [jnp.exp(sc - l_ref[rows, col]) for sc, col in zip(scs, cols)]
        dss = [(p * (dp - dl_ref[rows, col])).astype(BF16) for p, dp, col in zip(ps, dps, cols)]
        pbs = [p.astype(BF16) for p in ps]
        for i, qc in enumerate(qcols):
            a, b = 2 * i, 2 * i + 1
            dqs[rows, qc] = jnp.where(low, _dot(dss[a], k2s[a]), _dot(dss[b], k2s[b])) * SCALE
            acck[acc_rows, qc] += _dot_tn(dss[a], qhs[a]) + _dot_tn(dss[b], qhs[b])
            accv[acc_rows, qc] += _dot_tn(pbs[a], dobs[a]) + _dot_tn(pbs[b], dobs[b])

    def body_streams(_, q_ref, kc_ref, vc_ref, l_ref, do_ref, dl_ref, c_ref, sa_ref, sb_ref,
                     dq_ref, dk_ref, dv_ref, acck, accv, dqs):
        acck[...] = jnp.zeros_like(acck)
        accv[...] = jnp.zeros_like(accv)
        for sb in range(2):
            cols = slice(sb * w, (sb + 1) * w)
            attend(q_ref, l_ref, do_ref, dl_ref, dqs, acck, accv, TOP, sb * w, kc_ref[:, cols], vc_ref[:, cols], TOP)
        tabs2 = [jnp.concatenate([jnp.tile(r[:, sb * LANES:(sb + 1) * LANES], (1, reps)) for sb in range(2)], axis=1)
                 for r in (c_ref, sa_ref, sb_ref)]
        dq_ref[...] = _rope_bwd(dqs[...], *tabs2).astype(BF16)
        dk_ref[...] = _rope_bwd(acck[...], *tabs2).astype(BF16)
        dv_ref[...] = accv[...].astype(BF16)

    def body_blocks(_, q_ref, kp_ref, kc_ref, vp_ref, vc_ref, l_ref, do_ref, dl_ref, cq, saq, sbq, ck, sak, sbk,
                    dq_ref, dk_ref, dv_ref, acck, accv, dqs):
        i = pl.program_id(0) % (nb // 2)

        @pl.when(i == 0)
        def _():
            acck[...] = jnp.zeros_like(acck)
            accv[...] = jnp.zeros_like(accv)

        refs = (q_ref, l_ref, do_ref, dl_ref, dqs, acck, accv)
        pl.when(i == 0)(lambda: attend(*refs, TOP, 0, kc_ref[TOP, :], vc_ref[TOP, :], TOP))
        pl.when(i != 0)(lambda: attend(
            *refs, TOP, 0, jnp.concatenate([kp_ref[...], kc_ref[TOP, :]], axis=0),
            jnp.concatenate([vp_ref[...], vc_ref[TOP, :]], axis=0),
            pl.ds(pl.multiple_of((2 * i - 1) * BLOCK, BLOCK), 2 * BLOCK)))
        attend(*refs, BOTTOM, 0, kc_ref[...], vc_ref[...], pl.ds(pl.multiple_of(2 * i * BLOCK, BLOCK), 2 * BLOCK))
        tq = [jnp.tile(r[...], (1, reps)) for r in (cq, saq, sbq)]
        dq_ref[...] = _rope_bwd(dqs[...], *tq).astype(BF16)

        @pl.when(i == nb // 2 - 1)
        def _():
            for r0 in range(0, nb * BLOCK, 2 * BLOCK):
                rows = slice(r0, r0 + 2 * BLOCK)
                tk = [jnp.tile(r[rows, :], (1, reps)) for r in (ck, sak, sbk)]
                dk_ref[rows, :] = _rope_bwd(acck[rows, :], *tk).astype(BF16)
                dv_ref[rows, :] = accv[rows, :].astype(BF16)

    if nb == 1:
        body = body_streams
        in_specs = [ANY] + [two] * 6 + [two_t] * 3
        args = (token, q, k, v, lse, do, dl, *tabs)
        out_specs = [two, two, two]
        acc_shape = (BLOCK, 2 * w)
    else:
        body = body_blocks
        stream = pl.BlockSpec((nb * BLOCK, w), lambda n: (0, n // (nb // 2)))
        stream_t = pl.BlockSpec((nb * BLOCK, LANES), lambda n: (0, n // (nb // 2)))
        in_specs = [ANY, two, before, two, before, two, two, two, two] + [two_t] * 3 + [stream_t] * 3
        args = (token, q, k, k, v, v, lse, do, dl, *tabs, *tabs)
        out_specs = [two, stream, stream]
        acc_shape = (nb * BLOCK, w)
    return _pallas_call(
        body, name=f"attn_bwd_d{d}", grid=(d * nb // 2,), out_shape=[_sds((ln, dw), BF16)] * 3,
        in_specs=in_specs, out_specs=out_specs,
        scratch_shapes=[pltpu.VMEM(acc_shape, F32), pltpu.VMEM(acc_shape, F32), pltpu.VMEM(two.block_shape, F32)],
        compiler_params=_params(1, 48),
    )(*args)


def _position():
    return lax.axis_index("x"), lax.axis_index("y"), lax.axis_index("c")


def _all_gather(shards, after, name):
    n_a = len(shards)

    def body(*refs):
        x_refs, out_refs = refs[:n_a], refs[n_a + 1:2 * n_a + 1]
        send_sems, recv_sems, local_sems = refs[2 * n_a + 1:]
        x, y, c = _position()
        me, sibling = (x, y, c), (x, y, 1 - c)
        chips = [(1 - x, y), (x, 1 - y), (1 - x, 1 - y)]

        def rows(a, px, py, pc):
            return out_refs[a].at[4 * px + 2 * py + pc]

        def copy(a, k, block, to, own=False):
            return pltpu.make_async_remote_copy(
                src_ref=x_refs[a] if own else rows(a, *block), dst_ref=rows(a, *block),
                send_sem=send_sems.at[a, k], recv_sem=recv_sems.at[a, k], device_id=to, device_id_type=MESH)

        mine = [pltpu.make_async_copy(x_refs[a], rows(a, *me), local_sems.at[a]) for a in range(n_a)]
        for cp in mine:
            cp.start()
        first = []
        for j, chip in enumerate(chips):
            first += [copy(a, 1 + j, me, (*chip, c), own=True) for a in range(n_a)]
        first += [copy(a, 0, me, sibling, own=True) for a in range(n_a)]
        for cp in first:
            cp.start()
        passed = []
        for j, chip in enumerate(chips):
            for a in range(n_a):
                copy(a, 1 + j, (*chip, c), me).wait_recv()
                fwd = copy(a, 4 + j, (*chip, c), sibling)
                fwd.start()
                passed.append(fwd)
        for a in range(n_a):
            copy(a, 0, sibling, me).wait_recv()
        for j, chip in enumerate(chips):
            for a in range(n_a):
                copy(a, 4 + j, (*chip, 1 - c), me).wait_recv()
        for cp in first + passed:
            cp.wait_send()
        for cp in mine:
            cp.wait()

    return _pallas_call(
        body, name=name, out_shape=[_sds((N_DEV,) + t.shape, t.dtype) for t in shards],
        in_specs=[ANY] * (n_a + 1), out_specs=[ANY] * n_a,
        scratch_shapes=[pltpu.SemaphoreType.DMA((n_a, 7)), pltpu.SemaphoreType.DMA((n_a, 7)),
                        pltpu.SemaphoreType.DMA((n_a,))],
    )(*shards, after)


def _all_gather_relay(xs, name):
    def body(x_ref, out_ref, send_sems, recv_sems, local_sem):
        x, y, c = _position()
        me, sibling = (x, y, c), (x, y, 1 - c)
        xn, yn, diag = (1 - x, y, c), (x, 1 - y, c), (1 - x, 1 - y, c)
        src_nb = (x + c * (1 - 2 * x), y + (1 - c) * (1 - 2 * y), c)
        dst_nb = (x + (1 - c) * (1 - 2 * x), y + c * (1 - 2 * y), c)

        def rows(dev):
            return out_ref.at[4 * dev[0] + 2 * dev[1] + dev[2]]

        def copy(k, block, to, own=False):
            return pltpu.make_async_remote_copy(
                src_ref=x_ref if own else rows(block), dst_ref=rows(block),
                send_sem=send_sems.at[k], recv_sem=recv_sems.at[k], device_id=to, device_id_type=MESH)

        mine = pltpu.make_async_copy(x_ref, rows(me), local_sem)
        mine.start()
        first = [copy(1, me, xn, own=True), copy(2, me, yn, own=True), copy(0, me, sibling, own=True)]
        for cp in first:
            cp.start()
        copy(1, xn, me).wait_recv()
        copy(2, yn, me).wait_recv()
        relay = copy(3, src_nb, dst_nb)
        relay.start()
        passed = [copy(4, xn, sibling), copy(5, yn, sibling)]
        for cp in passed:
            cp.start()
        copy(3, diag, me).wait_recv()
        last = copy(6, diag, sibling)
        last.start()
        copy(0, sibling, me).wait_recv()
        for k, blk in ((4, (1 - x, y, 1 - c)), (5, (x, 1 - y, 1 - c)), (6, (1 - x, 1 - y, 1 - c))):
            copy(k, blk, me).wait_recv()
        for cp in first + [relay] + passed + [last]:
            cp.wait_send()
        mine.wait()

    return _pallas_call(
        body, name=name, out_shape=_sds((N_DEV,) + xs.shape, xs.dtype),
        in_specs=[ANY], out_specs=ANY,
        scratch_shapes=[pltpu.SemaphoreType.DMA((7,)), pltpu.SemaphoreType.DMA((7,)), pltpu.SemaphoreType.DMA],
    )(xs)


HBM_SPEC = pl.BlockSpec(memory_space=pltpu.HBM)
SEM_SPEC = pl.BlockSpec(memory_space=pltpu.SEMAPHORE)
EFFECT = pltpu.SideEffectType.DATAFLOW_SIDE_EFFECTING
def _plan_gather_own(src_refs, land_refs):
    x, y, c = _position()
    me = 4 * x + 2 * y + c
    peers = [(x, y, 1 - c), (1 - x, y, c), (x, 1 - y, c), (1 - x, 1 - y, c)]
    return [(src_refs[a], land_refs[a].at[me], (a, k), peer) for k, peer in enumerate(peers) for a in range(len(src_refs))]


def _plan_gather_pass(src_refs, land_refs):
    x, y, c = _position()
    chips = [(1 - x, y), (x, 1 - y), (1 - x, 1 - y)]
    return [(land_refs[a].at[4 * px + 2 * py + c], land_refs[a].at[4 * px + 2 * py + c], (a, j), (x, y, 1 - c))
            for j, (px, py) in enumerate(chips) for a in range(len(land_refs))]


def _plan_to_sibling(src_refs, land_refs):
    x, y, c = _position()
    return [(src_refs[a].at[2 * k + (1 - c)], land_refs[a].at[k], (a, k), (x, y, 1 - c))
            for k in range(4) for a in range(len(src_refs))]


def _plan_to_chips(src_refs, land_refs):
    x, y, c = _position()
    chips = [(1 - x, y), (x, 1 - y), (1 - x, 1 - y)]
    return [(src_refs[a].at[2 * px + py], land_refs[a].at[j], (a, j), (px, py, c))
            for j, (px, py) in enumerate(chips) for a in range(len(src_refs))]


def _split_start(srcs, lands, plan, n_sem, after, name):
    n_s, n_a = len(srcs), len(lands)
    n_b = n_s + n_a

    def body(*refs):
        src_refs, land_refs = refs[:n_s], refs[n_s:n_b]
        send_sems, recv_sems, token = refs[n_b + 1], refs[n_b + 2], refs[-1]
        for src, dst, (a, k), dev in plan(src_refs, land_refs):
            i = a * n_sem + k
            pltpu.make_async_remote_copy(src_ref=src, dst_ref=dst, send_sem=send_sems.at[i], recv_sem=recv_sems.at[i],
                                         device_id=dev, device_id_type=MESH).start()
        token[...] = jnp.zeros_like(token)

    bufs = list(srcs) + list(lands)
    res = pl.pallas_call(
        body, name=name,
        out_shape=(pltpu.SemaphoreType.DMA((n_a * n_sem,)), pltpu.SemaphoreType.DMA((n_a * n_sem,)),
                   *[pltpu.HBM(t.shape, t.dtype) for t in bufs], _plain((8, LANES), F32)),
        in_specs=[HBM_SPEC] * n_b + [ANY],
        out_specs=(SEM_SPEC, SEM_SPEC, *[HBM_SPEC] * n_b, pl.BlockSpec(memory_space=pltpu.VMEM)),
        input_output_aliases={i: 2 + i for i in range(n_b)},
        compiler_params=pltpu.CompilerParams(has_side_effects=EFFECT),
    )(*[pltpu.with_memory_space_constraint(t, pltpu.HBM) for t in bufs], after)
    return (res[0], res[1], res[2:2 + n_s], res[2 + n_s:2 + n_b]), res[-1]


def _split_wait(started, plan, after, name):
    send_sems, recv_sems, srcs, lands = started
    n_s, n_a = len(srcs), len(lands)
    n_b = n_s + n_a
    n_sem = send_sems.shape[0] // n_a

    def body(*refs):
        src_refs, land_refs = refs[:n_s], refs[n_s:n_b]
        s_sems, r_sems = refs[n_b], refs[n_b + 1]
        for src, dst, (a, k), dev in plan(src_refs, land_refs):
            i = a * n_sem + k
            cp = pltpu.make_async_remote_copy(src_ref=src, dst_ref=dst, send_sem=s_sems.at[i], recv_sem=r_sems.at[i],
                                              device_id=dev, device_id_type=MESH)
            cp.wait_send()
            cp.wait_recv()

    bufs = list(srcs) + list(lands)
    res = pl.pallas_call(
        body, name=name, out_shape=tuple(pltpu.HBM(t.shape, t.dtype) for t in bufs),
        in_specs=[HBM_SPEC] * n_b + [SEM_SPEC, SEM_SPEC, ANY],
        out_specs=tuple([HBM_SPEC] * n_b),
        input_output_aliases={i: i for i in range(n_b)},
        compiler_params=pltpu.CompilerParams(has_side_effects=EFFECT),
    )(*bufs, send_sems, recv_sems, after)
    return res[:n_s], res[n_s:]


SUBLANES = 8


def _row_tile(r):
    return max(t for t in range(SUBLANES, ROW_TILE + 1, SUBLANES) if r % t == 0)


def _rs_add_sibling(gp, recv, ck_arr, name):
    _, r, l = gp.shape
    tr = r if r <= 4 * ROW_TILE else _row_tile(r)
    block = lambda k, ck: (k + ck[1] + 1) % 4

    def body(ck_ref, g_ref, r_ref, pf_ref, pb_ref):
        sm = g_ref[...].astype(F32) + r_ref[...].astype(F32)
        pf_ref[...] = sm
        pb_ref[...] = sm.astype(BF16)

    spec = pl.BlockSpec((None, tr, l), lambda i, k, ck: (block(k, ck), i, 0))
    return _pallas_call(
        body, name=name,
        grid_spec=pltpu.PrefetchScalarGridSpec(
            num_scalar_prefetch=1, grid=(r // tr, 4),
            in_specs=[pl.BlockSpec((None, tr, l), lambda i, k, ck: (2 * block(k, ck) + ck[0], i, 0)), spec],
            out_specs=[pl.BlockSpec((tr, l), lambda i, k, ck: (i, 0)), spec]),
        out_shape=[_sds((r, l), F32), _sds((4, r, l), BF16)], compiler_params=_params(2, 32),
    )(ck_arr, gp, recv)


def _adam_update(w, gv, m, v):
    nm = ADAM_B1 * m + (1.0 - ADAM_B1) * gv
    nv = ADAM_B2 * v + (1.0 - ADAM_B2) * (gv * gv)
    m_hat = nm / (1.0 - ADAM_B1 ** ADAM_STEP)
    v_hat = nv / (1.0 - ADAM_B2 ** ADAM_STEP)
    return -ADAM_LR * (m_hat / (jnp.sqrt(v_hat) + ADAM_EPS) + ADAM_WD * w), nm, nv


def _rs_finish_adamw(pf, recv, w, m, v, name):
    r, l = pf.shape
    tr = _row_tile(r)

    def body(p_ref, r_ref, w_ref, m_ref, v_ref, g_ref, d_ref, nm_ref, nv_ref):
        gv = ((p_ref[...] + r_ref[0].astype(F32)) + r_ref[1].astype(F32)) + r_ref[2].astype(F32)
        g_ref[...] = gv
        d_ref[...], nm_ref[...], nv_ref[...] = _adam_update(w_ref[...], gv, m_ref[...], v_ref[...])

    spec = pl.BlockSpec((tr, l), lambda i: (i, 0))
    return _pallas_call(
        body, name=name, grid=(r // tr,),
        in_specs=[spec, pl.BlockSpec((3, tr, l), lambda i: (0, i, 0)), spec, spec, spec], out_specs=[spec] * 4,
        out_shape=[_plain((r, l), F32)] * 4, compiler_params=_params(1, 32),
    )(pf, recv, w, m, v)


SMALL_ROWS = dict(norm_g=(0, 2), mem_norm_g=(2, 4), final_g=(4, 5), conv_w=(5, 8))
LOSS_ROWS = (8, 16)


def _sum_adamw_small(g, ck_arr, states):
    names = list(SMALL_ROWS)
    n_dev, n_rows, _ = g.shape

    def body(ck_ref, g_ref, gc_ref, *refs):
        ins, loss_ref, outs = refs[:3 * len(names)], refs[3 * len(names)], refs[3 * len(names) + 1:]

        def total(ref, lo, hi):
            acc = ref[0, lo:hi, :]
            for j in range(1, n_dev):
                acc = acc + ref[j, lo:hi, :]
            return acc

        loss_ref[...] = total(gc_ref, *LOSS_ROWS)
        for i, n in enumerate(names):
            gv = total(gc_ref if n == "conv_w" else g_ref, *SMALL_ROWS[n])
            w_ref, m_ref, v_ref = ins[3 * i:3 * i + 3]
            g_out, d_out, nm_out, nv_out = outs[4 * i:4 * i + 4]
            g_out[...] = gv
            d_out[...], nm_out[...], nv_out[...] = _adam_update(w_ref[...], gv, m_ref[...], v_ref[...])

    flat = [t for n in names for t in states[n]]
    mine = pl.BlockSpec((n_dev, n_rows, LANES), lambda i, ck: (0, 0, 2 * ck[1] + ck[0]))
    res = _pallas_call(
        body, name="sum_adamw_small",
        grid_spec=pltpu.PrefetchScalarGridSpec(
            num_scalar_prefetch=1, grid=(1,),
            in_specs=[_whole(g.shape), mine] + [_whole(t.shape) for t in flat],
            out_specs=[_whole((SUBLANES, LANES))] + [_whole(states[n][0].shape) for n in names for _ in range(4)]),
        out_shape=[_plain((SUBLANES, LANES), F32)] + [_plain(states[n][0].shape, F32) for n in names for _ in range(4)],
        compiler_params=_params(1, 32),
    )(ck_arr, g, g, *flat)
    return res[0], {n: tuple(res[1 + 4 * i:5 + 4 * i]) for i, n in enumerate(names)}


def _finish(name, pf, recv, w, m, v):
    if name in ("attn_w_in", "conv_w_in"):
        res = _rs_finish_adamw(pf, recv, w.T, m.T, v.T, "rs_finish_adamw_" + name)
        return tuple(t.T for t in res)
    return _rs_finish_adamw(pf, recv, w, m, v, "rs_finish_adamw_" + name)


def kernel(x, mem, positions, norm_g, mem_norm_g, w_mem_kv, attn_w_in, attn_w_out, conv_w_in, conv_w, conv_w_out, final_g, loss_target, m_norm_g, m_mem_norm_g, m_w_mem_kv, m_attn_w_in, m_attn_w_out, m_conv_w_in, m_conv_w, m_conv_w_out, m_final_g, v_norm_g, v_mem_norm_g, v_w_mem_kv, v_attn_w_in, v_attn_w_out, v_conv_w_in, v_conv_w, v_conv_w_out, v_final_g):
    px, py, pc = _position()
    me = 4 * px + 2 * py + pc
    ck_arr = jnp.stack([pc, 2 * px + py]).astype(jnp.int32)
    x, mem, pos, tgt = x[0], mem[0], positions[0], loss_target[0]

    wg_in0 = _all_gather_relay(attn_w_in[0].astype(BF16), "gather_w_in0")
    def gather_start(shards, after, name):
        lands = [lax.dynamic_update_slice(lax.empty((N_DEV,) + t.shape, t.dtype), t[None], (me, 0, 0)) for t in shards]
        return _split_start(shards, lands, _plan_gather_own, 4, after, name + "_start")

    def gather_pass(weights, after, name):
        _, lands = _split_wait(weights, _plan_gather_own, after, name + "_wait")
        return _split_start([], lands, _plan_gather_pass, 3, after, name + "_pass_start")

    rest0, token = gather_start([attn_w_out[0].astype(BF16), w_mem_kv.astype(BF16).reshape(-1, w_mem_kv.shape[2]),
                                 jnp.pad(conv_w[0], ((0, 5), (0, 0)))], wg_in0, "gather_rest")
    conv_ws, token = gather_start([conv_w_in[0].astype(BF16), conv_w_out[0].astype(BF16)], token, "gather_conv")

    tabs = _rope_tables(pos)
    g0, g1 = norm_g[0:1], norm_g[1:2]

    hn0, qs, ks, vs, tabs_v, qm0, z0 = _inproj_attn(x, g0, wg_in0, tabs, token)
    os_, ls_ = [], []
    for j, d in enumerate(DILATIONS):
        if j == 2:
            rest0, token = gather_pass(rest0, ls_[1], "gather_rest")
        o, l = _attn_fwd(qs[j], ks[j], vs[j], d, token)
        os_.append(o)
        ls_.append(l)

    conv_ws, token = gather_pass(conv_ws, ls_[2], "gather_conv")
    _, (wg_out0, wg_kv, cw_all) = _split_wait(rest0, _plan_gather_pass, token, "gather_rest_pass_wait")
    cw = cw_all[:, 0:3].transpose(1, 0, 2).reshape(3, -1)
    kv = _memkv_fwd(mem, mem_norm_g, wg_kv)
    y0, y0_t, mo0, h1 = _post_attn(os_, ls_, qm0, kv[0], z0, x, wg_out0)
    _, (wg_in1, wg_out1) = _split_wait(conv_ws, _plan_gather_pass, h1, "gather_conv_pass_wait")
    w_out1 = wg_out1.reshape(-1, wg_out1.shape[2])

    hn1, bg, cg, u, qm1, z1 = _inproj_conv(h1, g1, wg_in1)
    y1, y1_t, mo1, dh2, dh2b, loss_acc, d_final_g = _post_conv_loss(
        bg, cg, u, qm1, kv[1], z1, h1, w_out1, cw, final_g.reshape(1, -1), tgt)

    d_w_out1 = _wgrad_rows(y1_t, dh2b, "wgrad_out1")
    dz1, dbg, dconv, dqm1, dkv1 = _bwd_post_conv(dh2b, w_out1, bg, cg, u, z1, qm1, kv[1], mo1, cw)
    dcg, du, dcw = _bwd_conv(dconv, cg, u, cw)
    dh1, dg1, dh1b, dproj1_t = _dgrad_norm([(dbg, 1), (dcg, 1), (du, 1), (dqm1, 1), (dz1, 1)], wg_in1, h1, g1, dh2,
                                           dh2, True, "dgrad_norm_conv")
    d_w_in1 = _wgrad_shards_t(dproj1_t, hn1, wg_in1.shape[2], "wgrad_in1")

    d_w_out0 = _wgrad_cols(y0_t, dh1b, wg_out0.shape[2], "wgrad_out0")

    gw = GROUP_WIDTH
    ones = (jnp.arange(gw)[:, None] // HEAD_DIM == jnp.arange(gw)[None, :] // HEAD_DIM).astype(BF16)
    res = _bwd_post_attn(dh1b, wg_out0, z0, os_, ls_, qm0, kv[0], mo0, ones, dg1)
    dz0, dos, dls, dqm0, dkv0 = res[0], res[1:4], res[4:7], res[7], res[8]
    d_w_kv, d_mem_g = _memkv_bwd(jnp.stack([dkv0, dkv1]), wg_kv, mem, mem_norm_g)

    names1 = ["conv_w_in", "conv_w_out", "attn_w_out", "w_mem_kv"]
    grads1 = [d_w_in1, d_w_out1, d_w_out0, d_w_kv]
    started, token = _split_start(grads1, [lax.empty((4,) + g.shape[1:], g.dtype) for g in grads1],
                                  _plan_to_sibling, 4, d_mem_g, "rs1_sibling_start")

    dqs, dks, dvs = [], [], []
    for j, d in enumerate(DILATIONS):
        if j == 1:
            grads1, from_sibling = _split_wait(started, _plan_to_sibling, dqs[0][0], "rs1_sibling_wait")
            parts1 = [_rs_add_sibling(g, r, ck_arr, "rs_add_sibling_" + n)
                      for g, r, n in zip(grads1, from_sibling, names1)]
            pbs1 = [pb for _, pb in parts1]
            started, token = _split_start(pbs1, [lax.empty((3,) + p.shape[1:], p.dtype) for p in pbs1],
                                          _plan_to_chips, 3, dg1, "rs1_chips_start")
        dq, dk, dv = _attn_bwd(qs[j], ks[j], vs[j], ls_[j], dos[j], dls[j], tabs_v[j], d, token)
        dqs.append((dq, d))
        dks.append((dk, d))
        dvs.append((dv, d))
    pieces0 = dqs + dks + dvs + [(dqm0, 1), (dz0, 1)]
    dproj0_t = _assemble_dproj_t(pieces0, N_DEV * wg_in0.shape[2], "assemble_dproj_attn")
    d_w_in0 = _wgrad_shards_t(dproj0_t, hn0, wg_in0.shape[2], "wgrad_in0")

    names0 = ["attn_w_in"]
    grads0 = [d_w_in0]
    started0, token0 = _split_start(grads0, [lax.empty((4,) + g.shape[1:], g.dtype) for g in grads0],
                                    _plan_to_sibling, 4, dg1, "rs0_sibling_start")
    _, from_chips1 = _split_wait(started, _plan_to_chips, token0, "rs1_chips_wait")
    shard = dict(attn_w_in=(attn_w_in[0], m_attn_w_in[0], v_attn_w_in[0]),
                 attn_w_out=(attn_w_out[0], m_attn_w_out[0], v_attn_w_out[0]),
                 conv_w_in=(conv_w_in[0], m_conv_w_in[0], v_conv_w_in[0]),
                 conv_w_out=(conv_w_out[0], m_conv_w_out[0], v_conv_w_out[0]),
                 w_mem_kv=tuple(t.reshape(-1, t.shape[2]) for t in (w_mem_kv, m_w_mem_kv, v_w_mem_kv)))
    big = {}
    for n, (pf, _), r in zip(names1, parts1, from_chips1):
        big[n] = _finish(n, pf, r, *shard[n])

    grads0, from_sibling = _split_wait(started0, _plan_to_sibling, big["conv_w_out"][1], "rs0_sibling_wait")
    parts0 = [_rs_add_sibling(g, r, ck_arr, "rs_add_sibling_" + n) for g, r, n in zip(grads0, from_sibling, names0)]
    pbs0 = [pb for _, pb in parts0]
    started0, token0 = _split_start(pbs0, [lax.empty((3,) + p.shape[1:], p.dtype) for p in pbs0],
                                    _plan_to_chips, 3, dg1, "rs0_chips_start")
    dx, dg0 = _dgrad_norm(pieces0, wg_in0, x, g0, dh1, token0, False, "dgrad_norm_attn")

    small_part = jnp.concatenate([dg0, dg1, d_mem_g.reshape(2, -1), d_final_g, dcw[0:3]], axis=0)
    small_part = jnp.concatenate([small_part, jnp.broadcast_to(loss_acc[0, 0], small_part.shape)], axis=0)
    small_w = dict(norm_g=(norm_g, m_norm_g, v_norm_g), mem_norm_g=(mem_norm_g, m_mem_norm_g, v_mem_norm_g),
                   conv_w=(conv_w, m_conv_w, v_conv_w), final_g=(final_g, m_final_g, v_final_g))
    loss_tile, small_res = _sum_adamw_small(
        _all_gather([small_part], dg0, "gather_small_grads")[0], ck_arr,
        {n: tuple(t.reshape(-1, t.shape[-1]) for t in wmv) for n, wmv in small_w.items()})
    loss = loss_tile[0, 0]
    for n, wmv in small_w.items():
        big[n] = tuple(t.reshape(wmv[0].shape) for t in small_res[n])

    _, from_chips0 = _split_wait(started0, _plan_to_chips, big["final_g"][1], "rs0_chips_wait")
    for n, (pf, _), r in zip(names0, parts0, from_chips0):
        big[n] = _finish(n, pf, r, *shard[n])
    for n in ("attn_w_in", "attn_w_out", "conv_w_in", "conv_w_out"):
        big[n] = tuple(t[None] for t in big[n])
    big["w_mem_kv"] = tuple(t.reshape(w_mem_kv.shape) for t in big["w_mem_kv"])

    order = ["norm_g", "mem_norm_g", "w_mem_kv", "attn_w_in", "attn_w_out", "conv_w_in", "conv_w", "conv_w_out", "final_g"]
    return (loss, dx[None], *[big[n][0] for n in order], *[big[n][1] for n in order],
            *[big[n][2] for n in order], *[big[n][3] for n in order])
```

```python
import jax
import jax.numpy as jnp
from jax import lax
from jax.experimental import pallas as pl
from jax.experimental.pallas import tpu as pltpu

F32 = jnp.float32
BF16 = jnp.bfloat16

N_DEV = 8
D_MODEL = 1024
HEAD_DIM = 64
ROT_DIM = HEAD_DIM // 4
ROPE_THETA = 500000.0
DILATIONS = (1, 4, 16)
HEADS_PER_GROUP = 8
GROUP_WIDTH = HEADS_PER_GROUP * HEAD_DIM
BLOCK = 128
N_MEM = 256
MEM_HEADS = 4
MEM_WIDTH = MEM_HEADS * HEAD_DIM
CONV_WIDTH = D_MODEL
EPS = 1e-6
SCALE = HEAD_DIM ** -0.5
NEG = -1e30

ADAM_LR = 0.001
ADAM_B1 = 0.9
ADAM_B2 = 0.999
ADAM_EPS = 1e-08
ADAM_WD = 0.01
ADAM_STEP = 10

ROW_TILE = 256
WGRAD_SHARDS = 4
LANES = 128
MESH = pl.DeviceIdType.MESH
ANY = pl.BlockSpec(memory_space=pl.ANY)


def _pallas_call(body, **kw):
    call = pl.pallas_call(body, **kw)

    def run(*args):
        pinned = [pltpu.with_memory_space_constraint(a, pltpu.HBM) if jnp.issubdtype(a.dtype, jnp.floating) else a
                  for a in args]
        return call(*pinned)

    return run


def _dot(a, b):
    return lax.dot_general(a, b, (((1,), (0,)), ((), ())), preferred_element_type=F32)


def _dot_nt(a, b):
    return lax.dot_general(a, b, (((1,), (1,)), ((), ())), preferred_element_type=F32)


def _dot_tn(a, b):
    return lax.dot_general(a, b, (((0,), (0,)), ((), ())), preferred_element_type=F32)


def _params(n_grid, vmem_mb=48):
    return pltpu.CompilerParams(dimension_semantics=("arbitrary",) * n_grid, vmem_limit_bytes=vmem_mb << 20)


def _rows(width, tm=ROW_TILE):
    return pl.BlockSpec((tm, width), lambda i: (i, 0))


def _view_rows(width, d, tm=ROW_TILE):
    return pl.BlockSpec((tm // d, d * width), lambda i: (i, 0))


def _whole(shape):
    return pl.BlockSpec(shape, lambda *_: (0,) * len(shape))


def _resident(shape):
    return pl.BlockSpec(shape, lambda *_: (0,) * len(shape), pipeline_mode=pl.Buffered(1))


def _sds(shape, dtype):
    return pltpu.HBM(shape, dtype)


def _plain(shape, dtype):
    return jax.ShapeDtypeStruct(shape, dtype)


def _silu_parts(z):
    sg = jax.nn.sigmoid(z)
    return z * sg, sg * (1.0 + z * (1.0 - sg))


def _to_view(scr, val, out_ref, d):
    tm, w = val.shape
    if d == 1:
        out_ref[...] = val.astype(out_ref.dtype)
        return
    for cb in range(w // LANES):
        scr[cb] = val[:, cb * LANES:(cb + 1) * LANES]
    for r in range(d):
        for cb in range(w // LANES):
            lo = r * w + cb * LANES
            out_ref[:, lo:lo + LANES] = scr[cb, pl.ds(r, tm // d, stride=d), :].astype(out_ref.dtype)


def _from_view(scr, in_ref, d):
    if d == 1:
        return in_ref[...].astype(F32)
    nc, tm, _ = scr.shape
    w = nc * LANES
    for r in range(d):
        for cb in range(nc):
            lo = r * w + cb * LANES
            scr[cb, pl.ds(r, tm // d, stride=d), :] = in_ref[:, lo:lo + LANES].astype(F32)
    return jnp.concatenate([scr[cb] for cb in range(nc)], axis=1)


def _view_scratch(width, tm=ROW_TILE):
    return pltpu.VMEM((width // LANES, tm, LANES), F32)


def _rope_tables(pos):
    half = ROT_DIM // 2
    inv_freq = ROPE_THETA ** (-jnp.arange(half, dtype=F32) * (2.0 / ROT_DIM))
    ang = pos.astype(F32)[:, None] * inv_freq
    cos, sin = jnp.cos(ang), jnp.sin(ang)
    s = pos.shape[0]
    z8 = jnp.zeros((s, half), F32)
    rest = HEAD_DIM - ROT_DIM
    cosf = jnp.concatenate([cos, cos, jnp.ones((s, rest), F32)], axis=1)
    sa = jnp.concatenate([-sin, z8, jnp.zeros((s, rest), F32)], axis=1)
    sb = jnp.concatenate([z8, sin, jnp.zeros((s, rest), F32)], axis=1)
    return tuple(jnp.tile(t, (1, LANES // HEAD_DIM)) for t in (cosf, sa, sb))


def _rope_fwd(t, cv, sav, sbv):
    w = t.shape[1]
    return t * cv + pltpu.roll(t, w - ROT_DIM // 2, 1) * sav + pltpu.roll(t, ROT_DIM // 2, 1) * sbv


def _rope_bwd(g, cv, sav, sbv):
    w = g.shape[1]
    return g * cv + pltpu.roll(g * sav, ROT_DIM // 2, 1) + pltpu.roll(g * sbv, w - ROT_DIM // 2, 1)


def _joined_columns(wg_ref):
    assert wg_ref.shape[2] % LANES == 0
    return jnp.concatenate([wg_ref[j] for j in range(N_DEV)], axis=1)


def _join_once(wg_ref, w_scr):
    c = wg_ref.shape[2]

    @pl.when(pl.program_id(0) == 0)
    def _():
        for j in range(N_DEV):
            w_scr[:, j * c:(j + 1) * c] = wg_ref[j]


def _inproj_attn(x, g, wg, tabs, token):
    s, d_model = x.shape
    gw = GROUP_WIDTH
    n = N_DEV * wg.shape[2]
    nz = n - 9 * gw - MEM_WIDTH
    reps = gw // LANES
    tm = ROW_TILE

    def body(_, x_ref, g_ref, w_ref, c_ref, sa_ref, sb_ref, hn_ref, *rest):
        outs, (wj, scr, tscr) = rest[:-3], rest[-3:]
        q_refs, k_refs, v_refs, t_refs, qm_ref, z_ref = outs[0:3], outs[3:6], outs[6:9], outs[9:18], outs[18], outs[19]
        _join_once(w_ref, wj)
        xb = x_ref[...]
        r = lax.rsqrt(jnp.mean(xb * xb, axis=-1, keepdims=True) + EPS)
        hn = ((xb * r) * g_ref[...]).astype(BF16)
        hn_ref[...] = hn
        proj = lambda lo, hi: _dot(hn, wj[:, lo:hi])
        tab = (c_ref[...], sa_ref[...], sb_ref[...])
        cv, sav, sbv = [jnp.tile(t, (1, reps)) for t in tab]
        for j, d in enumerate(DILATIONS):
            tq = _rope_fwd(proj(j * gw, (j + 1) * gw), cv, sav, sbv)
            _to_view(scr, tq * SCALE, q_refs[j], d)
            tk = _rope_fwd(proj((3 + j) * gw, (4 + j) * gw), cv, sav, sbv)
            _to_view(scr, tk, k_refs[j], d)
            _to_view(scr, proj((6 + j) * gw, (7 + j) * gw), v_refs[j], d)
            for i in range(3):
                _to_view(tscr, tab[i], t_refs[3 * j + i], d)
        qm_ref[...] = proj(9 * gw, 9 * gw + MEM_WIDTH).astype(BF16)
        z_ref[...] = proj(9 * gw + MEM_WIDTH, n)

    views = [_sds((s // d, d * gw), BF16) for d in DILATIONS]
    tviews = [_sds((s // d, d * LANES), F32) for d in DILATIONS for _ in range(3)]
    out_shape = [_sds((s, d_model), BF16)] + views * 3 + tviews + [_sds((s, MEM_WIDTH), BF16), _sds((s, nz), F32)]
    vspecs = [_view_rows(gw, d, tm) for d in DILATIONS]
    tspecs = [_view_rows(LANES, d, tm) for d in DILATIONS for _ in range(3)]
    out_specs = [_rows(d_model, tm)] + vspecs * 3 + tspecs + [_rows(MEM_WIDTH, tm), _rows(nz, tm)]
    res = _pallas_call(
        body, name="inproj_attn", grid=(s // tm,), out_shape=out_shape,
        in_specs=[ANY, _rows(d_model, tm), _whole((1, d_model)), _resident(wg.shape)] + [_rows(LANES, tm)] * 3,
        out_specs=out_specs,
        scratch_shapes=[pltpu.VMEM((d_model, n), BF16), _view_scratch(gw, tm), _view_scratch(LANES, tm)],
        compiler_params=_params(1, 60),
    )(token, x, g, wg, *tabs)
    tabs_v = [res[10 + 3 * j:13 + 3 * j] for j in range(3)]
    return res[0], res[1:4], res[4:7], res[7:10], tabs_v, res[19], res[20]


def _band_mask(n_keys):
    qi = lax.broadcasted_iota(jnp.int32, (BLOCK, n_keys), 0)
    kj = lax.broadcasted_iota(jnp.int32, (BLOCK, n_keys), 1)
    if n_keys == BLOCK:
        return kj <= qi
    return jnp.logical_or(jnp.logical_and(kj < BLOCK, kj >= qi), jnp.logical_and(kj >= BLOCK, (kj - BLOCK) <= qi))


def _low_head_lanes():
    return lax.broadcasted_iota(jnp.int32, (1, LANES), 1) < HEAD_DIM


def _split_pair(t, low):
    zero = jnp.zeros_like(t)
    return jnp.where(low, t, zero), jnp.where(low, zero, t)


def _pair_specs(d, nb, w):
    if nb == 1:
        return pl.BlockSpec((BLOCK, 2 * w), lambda n: (0, n)), None
    half = nb // 2
    two = pl.BlockSpec((2 * BLOCK, w), lambda n: (n % half, n // half))
    before = pl.BlockSpec((BLOCK, w), lambda n: (jnp.maximum(2 * (n % half) - 1, 0), n // half))
    return two, before


def _head_tiles(w, col0):
    return ([slice(p * LANES, (p + 1) * LANES) for p in range(w // LANES)],
            [slice(col0 + p * LANES, col0 + (p + 1) * LANES) for p in range(w // LANES)])


def _attend_fwd(q_ref, o_ref, lse_ref, rows, col0, kk, vv):
    w = kk.shape[1]
    valid = _band_mask(kk.shape[0])
    low = _low_head_lanes()
    pairs, qcols = _head_tiles(w, col0)
    qs_ = [h for qc in qcols for h in _split_pair(q_ref[rows, qc], low)]
    k2s = [kk[:, pr] for pr in pairs for _ in range(2)]
    scs = [jnp.where(valid, _dot_nt(qh, k2), NEG) for qh, k2 in zip(qs_, k2s)]
    ms = [jnp.max(sc, axis=-1, keepdims=True) for sc in scs]
    ps = [jnp.exp(sc - m) for sc, m in zip(scs, ms)]
    ls = [jnp.sum(p, axis=-1, keepdims=True) for p in ps]
    pns = [(p * (1.0 / l)).astype(BF16) for p, l in zip(ps, ls)]
    for i, (pr, qc) in enumerate(zip(pairs, qcols)):
        v2 = vv[:, pr]
        a, b = 2 * i, 2 * i + 1
        o_ref[rows, qc] = jnp.where(low, _dot(pns[a], v2), _dot(pns[b], v2))
        lse_ref[rows, qc] = jnp.where(low, ms[a] + jnp.log(ls[a]), ms[b] + jnp.log(ls[b]))


TOP, BOTTOM = slice(0, BLOCK), slice(BLOCK, 2 * BLOCK)


def _attn_fwd(q, k, v, d, token):
    ln, dw = q.shape
    w = dw // d
    nb = ln // BLOCK
    two, before = _pair_specs(d, nb, w)

    def body_streams(_, q_ref, kc_ref, vc_ref, o_ref, lse_ref):
        for sb in range(2):
            cols = slice(sb * w, (sb + 1) * w)
            _attend_fwd(q_ref, o_ref, lse_ref, TOP, sb * w, kc_ref[:, cols], vc_ref[:, cols])

    def body_blocks(_, q_ref, kp_ref, kc_ref, vp_ref, vc_ref, o_ref, lse_ref):
        first = pl.program_id(0) % (nb // 2) == 0
        pl.when(first)(lambda: _attend_fwd(q_ref, o_ref, lse_ref, TOP, 0, kc_ref[TOP, :], vc_ref[TOP, :]))
        pl.when(jnp.logical_not(first))(lambda: _attend_fwd(
            q_ref, o_ref, lse_ref, TOP, 0, jnp.concatenate([kp_ref[...], kc_ref[TOP, :]], axis=0),
            jnp.concatenate([vp_ref[...], vc_ref[TOP, :]], axis=0)))
        _attend_fwd(q_ref, o_ref, lse_ref, BOTTOM, 0, kc_ref[...], vc_ref[...])

    if nb == 1:
        body, in_specs, args = body_streams, [ANY, two, two, two], (token, q, k, v)
    else:
        body, in_specs, args = body_blocks, [ANY, two, before, two, before, two], (token, q, k, k, v, v)
    return _pallas_call(
        body, name=f"attn_fwd_d{d}", grid=(d * nb // 2,), out_shape=[_sds((ln, dw), F32)] * 2,
        in_specs=in_specs, out_specs=[two, two], compiler_params=_params(1, 32),
    )(*args)


def _memkv_fwd(mem, g, w):
    n_layers = g.shape[0]
    rows = D_MODEL // N_DEV

    def body(mem_ref, g_ref, w_ref, kv_ref):
        mb = mem_ref[...]
        r = lax.rsqrt(jnp.mean(mb * mb, axis=-1, keepdims=True) + EPS)
        mn = ((mb * r) * g_ref[...]).astype(BF16)
        kv_ref[...] = _dot(mn, w_ref[...].reshape(D_MODEL, 2 * MEM_WIDTH)).astype(BF16)

    return _pallas_call(
        body, name="memkv_fwd", grid=(n_layers,),
        out_shape=_plain((n_layers, N_MEM, 2 * MEM_WIDTH), BF16),
        in_specs=[_whole(mem.shape), pl.BlockSpec((None, 1, D_MODEL), lambda l: (l, 0, 0)),
                  pl.BlockSpec((N_DEV, rows, 2 * MEM_WIDTH), lambda l: (0, l, 0))],
        out_specs=pl.BlockSpec((None, N_MEM, 2 * MEM_WIDTH), lambda l: (l, 0, 0)),
        compiler_params=_params(1, 32),
    )(mem, g.reshape(n_layers, 1, D_MODEL), w)


def _mix_groups(os_, ls_):
    mx = jnp.maximum(jnp.maximum(ls_[0], ls_[1]), ls_[2])
    es = [jnp.exp(t - mx) for t in ls_]
    inv = 1.0 / (es[0] + es[1] + es[2])
    ws = [e * inv for e in es]
    mix = ws[0] * os_[0] + ws[1] * os_[1] + ws[2] * os_[2]
    return ws, mix


MEM_PAIRS = [slice(p * LANES, (p + 1) * LANES) for p in range(MEM_WIDTH // LANES)]


def _mem_probs(qhs, k2s):
    scs = [_dot_nt(qh, k2) * SCALE for qh, k2 in zip(qhs, k2s)]
    es = [jnp.exp(sc - jnp.max(sc, axis=-1, keepdims=True)) for sc in scs]
    return [e * (1.0 / jnp.sum(e, axis=-1, keepdims=True)) for e in es]


def _mem_attn_into(qm, kv_ref, mo_ref):
    low = _low_head_lanes()
    qhs = [h for pr in MEM_PAIRS for h in _split_pair(qm[:, pr], low)]
    k2s = [kv_ref[:, pr] for pr in MEM_PAIRS for _ in range(2)]
    ps = [p.astype(BF16) for p in _mem_probs(qhs, k2s)]
    for i, pr in enumerate(MEM_PAIRS):
        v2 = kv_ref[:, MEM_WIDTH + i * LANES:MEM_WIDTH + (i + 1) * LANES]
        mo_ref[:, pr] = jnp.where(low, _dot(ps[2 * i], v2), _dot(ps[2 * i + 1], v2))


def _mem_attn_bwd(qm, kv_ref, dmem, dqm_ref, dkv_ref):
    low = _low_head_lanes()
    dmb = dmem.astype(BF16)
    vps = [slice(MEM_WIDTH + i * LANES, MEM_WIDTH + (i + 1) * LANES) for i in range(len(MEM_PAIRS))]
    qhs = [h for pr in MEM_PAIRS for h in _split_pair(qm[:, pr], low)]
    dhs = [h for pr in MEM_PAIRS for h in _split_pair(dmb[:, pr], low)]
    k2s = [kv_ref[:, pr] for pr in MEM_PAIRS for _ in range(2)]
    v2s = [kv_ref[:, vp] for vp in vps for _ in range(2)]
    ps = _mem_probs(qhs, k2s)
    dps = [_dot_nt(dh, v2) for dh, v2 in zip(dhs, v2s)]
    dss = [(p * (dp - jnp.sum(dp * p, axis=-1, keepdims=True)) * SCALE).astype(BF16) for p, dp in zip(ps, dps)]
    pbs = [p.astype(BF16) for p in ps]
    for i, (pr, vp) in enumerate(zip(MEM_PAIRS, vps)):
        a, b = 2 * i, 2 * i + 1
        dqm_ref[:, pr] = jnp.where(low, _dot(dss[a], k2s[a]), _dot(dss[b], k2s[b])).astype(BF16)
        dkv_ref[:, pr] += _dot_tn(dss[a], qhs[a]) + _dot_tn(dss[b], qhs[b])
        dkv_ref[:, vp] += _dot_tn(pbs[a], dhs[a]) + _dot_tn(pbs[b], dhs[b])


def _post_attn(os_, ls_, qm, kv, z, x, wg_out):
    s, d_model = x.shape
    gw = GROUP_WIDTH
    nb = gw + MEM_WIDTH
    tm = ROW_TILE

    def body(o0, o1, o2, l0, l1, l2, qm_ref, kv_ref, z_ref, x_ref, w_ref, y_ref, yt_ref, mo_ref, h_ref, s0, s1):
        ov, lv = [], []
        for o_ref, l_ref, d in zip((o0, o1, o2), (l0, l1, l2), DILATIONS):
            ov.append(_from_view(s0, o_ref, d))
            lv.append(_from_view(s1, l_ref, d))
        _, mix = _mix_groups(ov, lv)
        _mem_attn_into(qm_ref[...], kv_ref, mo_ref)
        sz, _ = _silu_parts(z_ref[...])
        y_ref[:, :gw] = (mix * sz[:, :gw]).astype(BF16)
        y_ref[:, gw:] = (mo_ref[...] * sz[:, gw:]).astype(BF16)
        y = y_ref[...]
        yt_ref[...] = y.T
        h_ref[...] = x_ref[...] + _dot(y, _joined_columns(w_ref))

    vspecs = [_view_rows(gw, d) for d in DILATIONS]
    return _pallas_call(
        body, name="post_attn", grid=(s // tm,),
        out_shape=[_sds((s, nb), BF16), _sds((nb, s), BF16), _sds((s, MEM_WIDTH), F32), _sds((s, d_model), F32)],
        in_specs=vspecs * 2 + [_rows(MEM_WIDTH), _whole(kv.shape), _rows(nb), _rows(d_model), _whole(wg_out.shape)],
        out_specs=[_rows(nb), pl.BlockSpec((nb, tm), lambda i: (0, i)), _rows(MEM_WIDTH), _rows(d_model)],
        scratch_shapes=[_view_scratch(gw), _view_scratch(gw)],
        compiler_params=_params(1, 40),
    )(*os_, *ls_, qm, kv, z, x, wg_out)


def _inproj_conv(x, g, wg):
    s, d_model = x.shape
    c = CONV_WIDTH
    n = N_DEV * wg.shape[2]
    nz = n - 3 * c - MEM_WIDTH
    tm = ROW_TILE

    def body(x_ref, g_ref, w_ref, hn_ref, bg_ref, cg_ref, u_ref, qm_ref, z_ref, wj):
        _join_once(w_ref, wj)
        xb = x_ref[...]
        r = lax.rsqrt(jnp.mean(xb * xb, axis=-1, keepdims=True) + EPS)
        hn = ((xb * r) * g_ref[...]).astype(BF16)
        hn_ref[...] = hn
        bg_ref[...] = _dot(hn, wj[:, 0:c])
        cg_ref[...] = _dot(hn, wj[:, c:2 * c])
        u_ref[...] = _dot(hn, wj[:, 2 * c:3 * c])
        qm_ref[...] = _dot(hn, wj[:, 3 * c:3 * c + MEM_WIDTH]).astype(BF16)
        z_ref[...] = _dot(hn, wj[:, 3 * c + MEM_WIDTH:])

    return _pallas_call(
        body, name="inproj_conv", grid=(s // tm,),
        out_shape=[_sds((s, d_model), BF16)] + [_sds((s, c), F32)] * 3 + [_sds((s, MEM_WIDTH), BF16), _sds((s, nz), F32)],
        in_specs=[_rows(d_model), _whole((1, d_model)), _resident(wg.shape)],
        out_specs=[_rows(d_model)] + [_rows(c)] * 3 + [_rows(MEM_WIDTH), _rows(nz)],
        scratch_shapes=[pltpu.VMEM((d_model, n), BF16)],
        compiler_params=_params(1, 60),
    )(x, g, wg)


HALO = 8


def _halo_before(width, tm=ROW_TILE):
    return pl.BlockSpec((HALO, width), lambda i: (jnp.maximum(i * (tm // HALO) - 1, 0), 0))


def _halo_after(width, n_rows, tm=ROW_TILE):
    return pl.BlockSpec((HALO, width), lambda i: (jnp.minimum((i + 1) * (tm // HALO), n_rows // HALO - 1), 0))


def _conv_taps(cg_ref, u_ref, cgh_ref, uh_ref, i):
    a = cg_ref[...] * u_ref[...]
    ah = jnp.where(i > 0, cgh_ref[...] * uh_ref[...], 0.0)
    row = lax.broadcasted_iota(jnp.int32, a.shape, 0)
    a1 = jnp.where(row == 0, ah[HALO - 1:HALO], pltpu.roll(a, 1, 0))
    a2 = jnp.where(row == 0, ah[HALO - 2:HALO - 1], jnp.where(row == 1, ah[HALO - 1:HALO], pltpu.roll(a, 2, 0)))
    return a, a1, a2


def _post_conv_loss(bg, cg, u, qm, kv, z, h1, w_out, cw, gf, tgt):
    s, d = h1.shape
    c = CONV_WIDTH
    nb = c + MEM_WIDTH
    tm = ROW_TILE

    def body(bg_ref, cg_ref, u_ref, cgh_ref, uh_ref, qm_ref, kv_ref, z_ref, h_ref, w_ref, cw_ref, gf_ref, t_ref,
             y_ref, yt_ref, mo_ref, dh_ref, dhb_ref, loss_ref, dgf_ref):
        i = pl.program_id(0)
        a, a1, a2 = _conv_taps(cg_ref, u_ref, cgh_ref, uh_ref, i)
        conv = cw_ref[0:1, :] * a2 + cw_ref[1:2, :] * a1 + cw_ref[2:3, :] * a
        mix = bg_ref[...] * conv
        _mem_attn_into(qm_ref[...], kv_ref, mo_ref)
        sz, _ = _silu_parts(z_ref[...])
        y_ref[:, :c] = (mix * sz[:, :c]).astype(BF16)
        y_ref[:, c:] = (mo_ref[...] * sz[:, c:]).astype(BF16)
        y = y_ref[...]
        yt_ref[...] = y.T
        h2 = h_ref[...] + _dot(y, w_ref[...])
        r = lax.rsqrt(jnp.mean(h2 * h2, axis=-1, keepdims=True) + EPS)
        nh = h2 * r
        gfv = gf_ref[...]
        diff = nh * gfv - t_ref[...]
        dout = diff * (1.0 / d)
        dn = dout * gfv
        dh2 = r * dn - h2 * ((r * r * r) * jnp.mean(dn * h2, axis=-1, keepdims=True))
        dh_ref[...] = dh2
        dhb_ref[...] = dh2.astype(BF16)

        @pl.when(i == 0)
        def _():
            loss_ref[...] = jnp.zeros_like(loss_ref)
            dgf_ref[...] = jnp.zeros_like(dgf_ref)

        loss_ref[...] += 0.5 * jnp.sum(jnp.mean(diff * diff, axis=-1, keepdims=True))
        dgf_ref[...] += jnp.sum(dout * nh, axis=0, keepdims=True)

    return _pallas_call(
        body, name="post_conv_loss", grid=(s // tm,),
        out_shape=[_sds((s, nb), BF16), _sds((nb, s), BF16), _sds((s, MEM_WIDTH), F32), _sds((s, d), F32),
                   _sds((s, d), BF16), _plain((8, LANES), F32), _plain((1, d), F32)],
        in_specs=[_rows(c)] * 3 + [_halo_before(c)] * 2 + [_rows(MEM_WIDTH), _whole(kv.shape), _rows(nb), _rows(d),
                  _whole(w_out.shape), _whole(cw.shape), _whole((1, d)), _rows(d)],
        out_specs=[_rows(nb), pl.BlockSpec((nb, tm), lambda i: (0, i)), _rows(MEM_WIDTH), _rows(d), _rows(d),
                   _whole((8, LANES)), _whole((1, d))],
        compiler_params=_params(1, 48),
    )(bg, cg, u, cg, u, qm, kv, z, h1, w_out, cw, gf, tgt)


def _bwd_post_conv(dhb, w_out, bg, cg, u, z, qm, kv, mo, cw):
    s = dhb.shape[0]
    c = CONV_WIDTH
    nb = c + MEM_WIDTH

    def body(dh_ref, w_ref, bg_ref, cg_ref, u_ref, cgh_ref, uh_ref, z_ref, qm_ref, kv_ref, mo_ref, cw_ref,
             dz_ref, dbg_ref, dc_ref, dqm_ref, dkv_ref):
        i = pl.program_id(0)

        @pl.when(i == 0)
        def _():
            dkv_ref[...] = jnp.zeros_like(dkv_ref)

        dy = _dot_nt(dh_ref[...], w_ref[...])
        sz, dsz = _silu_parts(z_ref[...])
        a, a1, a2 = _conv_taps(cg_ref, u_ref, cgh_ref, uh_ref, i)
        conv = cw_ref[0:1, :] * a2 + cw_ref[1:2, :] * a1 + cw_ref[2:3, :] * a
        bgv = bg_ref[...]
        dz_ref[:, :c] = (dy[:, :c] * (bgv * conv) * dsz[:, :c]).astype(BF16)
        dz_ref[:, c:] = (dy[:, c:] * mo_ref[...] * dsz[:, c:]).astype(BF16)
        dbr = dy * sz
        dmix = dbr[:, :c]
        dbg_ref[...] = (dmix * conv).astype(BF16)
        dc_ref[...] = dmix * bgv
        _mem_attn_bwd(qm_ref[...], kv_ref, dbr[:, c:], dqm_ref, dkv_ref)

    return _pallas_call(
        body, name="bwd_post_conv", grid=(s // ROW_TILE,),
        out_shape=[_sds((s, nb), BF16), _sds((s, c), BF16), _sds((s, c), F32), _sds((s, MEM_WIDTH), BF16),
                   _plain(kv.shape, F32)],
        in_specs=[_rows(D_MODEL), _whole(w_out.shape)] + [_rows(c)] * 3 + [_halo_before(c)] * 2
                 + [_rows(nb), _rows(MEM_WIDTH), _whole(kv.shape), _rows(MEM_WIDTH), _whole(cw.shape)],
        out_specs=[_rows(nb), _rows(c), _rows(c), _rows(MEM_WIDTH), _whole(kv.shape)],
        compiler_params=_params(1, 48),
    )(dhb, w_out, bg, cg, u, cg, u, z, qm, kv, mo, cw)


def _bwd_conv(dconv, cg, u, cw):
    s, c = dconv.shape
    tm = ROW_TILE
    last = s // tm - 1

    def body(dc_ref, dcn_ref, cg_ref, u_ref, cgh_ref, uh_ref, cw_ref, dcg_ref, du_ref, dcw_ref):
        i = pl.program_id(0)

        @pl.when(i == 0)
        def _():
            dcw_ref[...] = jnp.zeros_like(dcw_ref)

        dc = dc_ref[...]
        dcn = jnp.where(i < last, dcn_ref[...], 0.0)
        row = lax.broadcasted_iota(jnp.int32, dc.shape, 0)
        d1 = jnp.where(row == tm - 1, dcn[0:1], pltpu.roll(dc, tm - 1, 0))
        d2 = jnp.where(row == tm - 1, dcn[1:2], jnp.where(row == tm - 2, dcn[0:1], pltpu.roll(dc, tm - 2, 0)))
        da = cw_ref[2:3, :] * dc + cw_ref[1:2, :] * d1 + cw_ref[0:1, :] * d2
        a, a1, a2 = _conv_taps(cg_ref, u_ref, cgh_ref, uh_ref, i)
        dcg_ref[...] = (da * u_ref[...]).astype(BF16)
        du_ref[...] = (da * cg_ref[...]).astype(BF16)
        dcw_ref[0:1, :] += jnp.sum(dc * a2, axis=0, keepdims=True)
        dcw_ref[1:2, :] += jnp.sum(dc * a1, axis=0, keepdims=True)
        dcw_ref[2:3, :] += jnp.sum(dc * a, axis=0, keepdims=True)

    return _pallas_call(
        body, name="bwd_conv", grid=(s // tm,),
        out_shape=[_sds((s, c), BF16), _sds((s, c), BF16), _plain((8, c), F32)],
        in_specs=[_rows(c), _halo_after(c, s), _rows(c), _rows(c), _halo_before(c), _halo_before(c), _whole(cw.shape)],
        out_specs=[_rows(c), _rows(c), _whole((8, c))], compiler_params=_params(1, 40),
    )(dconv, dconv, cg, u, cg, u, cw)


def _assemble(p_refs, pieces, widths, dp, scr):
    off = 0
    for p_ref, (_, d), wd in zip(p_refs, pieces, widths):
        if d == 1:
            dp[:, off:off + wd] = p_ref[...]
        else:
            dp[:, off:off + wd] = _from_view(scr, p_ref, d).astype(BF16)
        off += wd


def _dgrad_norm(pieces, wg, h, g, dres, token, onward, name):
    s, d_model = h.shape
    n = N_DEV * wg.shape[2]
    tm = ROW_TILE
    widths = [p.shape[1] // d for p, d in pieces]
    assert sum(widths) == n
    n_p = len(pieces)

    def body(_, *refs):
        p_refs = refs[:n_p]
        w_ref, h_ref, g_ref, dr_ref, dh_ref, dg_ref = refs[n_p:n_p + 6]
        dp, scr, wj = refs[-3:]
        _join_once(w_ref, wj)

        @pl.when(pl.program_id(0) == 0)
        def _():
            dg_ref[...] = jnp.zeros_like(dg_ref)

        _assemble(p_refs, pieces, widths, dp, scr)
        dhn = _dot_nt(dp[...], wj[...])
        hb = h_ref[...]
        r = lax.rsqrt(jnp.mean(hb * hb, axis=-1, keepdims=True) + EPS)
        dg_ref[...] += jnp.sum(dhn * (hb * r), axis=0, keepdims=True)
        dn = dhn * g_ref[...]
        dh = dr_ref[...] + r * dn - hb * ((r * r * r) * jnp.mean(dn * hb, axis=-1, keepdims=True))
        dh_ref[...] = dh
        if onward:
            dhb_ref, dpt_ref = refs[n_p + 6:n_p + 8]
            dhb_ref[...] = dh.astype(BF16)
            dpt_ref[...] = dp[...].T

    p_specs = [_view_rows(wd, d) for (_, d), wd in zip(pieces, widths)]
    out_shape = [_plain((s, d_model), F32), _plain((1, d_model), F32)]
    out_specs = [_rows(d_model), _whole((1, d_model))]
    if onward:
        out_shape += [_sds((s, d_model), BF16), _sds((n, s), BF16)]
        out_specs += [_rows(d_model), pl.BlockSpec((n, tm), lambda i: (0, i))]
    return _pallas_call(
        body, name=name, grid=(s // tm,), out_shape=out_shape,
        in_specs=[ANY] + p_specs + [_resident(wg.shape), _rows(d_model), _whole((1, d_model)), _rows(d_model)],
        out_specs=out_specs,
        scratch_shapes=[pltpu.VMEM((tm, n), BF16), _view_scratch(GROUP_WIDTH), pltpu.VMEM((d_model, n), BF16)],
        compiler_params=_params(1, 60),
    )(token, *[p for p, _ in pieces], wg, h, g, dres)


def _assemble_dproj_t(pieces, n, name):
    tm = ROW_TILE
    widths = [p.shape[1] // d for p, d in pieces]
    assert sum(widths) == n
    s = pieces[0][0].shape[0] * pieces[0][1]
    n_p = len(pieces)

    def body(*refs):
        p_refs, (dpt_ref, dp, scr) = refs[:n_p], refs[n_p:]
        _assemble(p_refs, pieces, widths, dp, scr)
        dpt_ref[...] = dp[...].T

    return _pallas_call(
        body, name=name, grid=(s // tm,), out_shape=_sds((n, s), BF16),
        in_specs=[_view_rows(wd, d) for (_, d), wd in zip(pieces, widths)],
        out_specs=pl.BlockSpec((n, tm), lambda i: (0, i)),
        scratch_shapes=[pltpu.VMEM((tm, n), BF16), _view_scratch(GROUP_WIDTH)],
        compiler_params=_params(1, 40),
    )(*[p for p, _ in pieces])


def _wgrad_shards_t(dp_t, h, c, name):
    n, s = dp_t.shape
    d_model = h.shape[1]
    per_step = 2

    def body(a_ref, b_ref, o_ref):
        o_ref[...] = _dot(a_ref[...], b_ref[...]).astype(BF16).reshape(per_step, c, d_model)

    return _pallas_call(
        body, name=name, grid=(N_DEV // per_step,), out_shape=_sds((N_DEV, c, d_model), BF16),
        in_specs=[pl.BlockSpec((per_step * c, s), lambda j: (j, 0)), _resident(h.shape)],
        out_specs=pl.BlockSpec((per_step, c, d_model), lambda j: (j, 0, 0)), compiler_params=_params(1, 40),
    )(dp_t, h)


def _wgrad_cols(a_t, b, c, name):
    m, s = a_t.shape
    assert c % LANES == 0

    def body(a_ref, b_ref, o_ref):
        wide = _dot(a_ref[...], b_ref[...]).astype(BF16)
        for j in range(WGRAD_SHARDS):
            o_ref[j] = wide[:, j * c:(j + 1) * c]

    return _pallas_call(
        body, name=name, grid=(N_DEV // WGRAD_SHARDS,), out_shape=_sds((N_DEV, m, c), BF16),
        in_specs=[_whole(a_t.shape), pl.BlockSpec((s, WGRAD_SHARDS * c), lambda j: (0, j))],
        out_specs=pl.BlockSpec((WGRAD_SHARDS, m, c), lambda j: (j, 0, 0)), compiler_params=_params(1, 40),
    )(a_t, b)


def _wgrad_rows(a_t, b, name):
    m, s = a_t.shape
    n = b.shape[1]
    mr = m // N_DEV

    def body(a_ref, b_ref, o_ref):
        o_ref[...] = _dot(a_ref[...], b_ref[...]).astype(BF16).reshape(WGRAD_SHARDS, mr, n)

    return _pallas_call(
        body, name=name, grid=(N_DEV // WGRAD_SHARDS,), out_shape=_sds((N_DEV, mr, n), BF16),
        in_specs=[pl.BlockSpec((WGRAD_SHARDS * mr, s), lambda j: (j, 0)), _whole(b.shape)],
        out_specs=pl.BlockSpec((WGRAD_SHARDS, mr, n), lambda j: (j, 0, 0)), compiler_params=_params(1, 40),
    )(a_t, b)


def _memkv_bwd(dkv, w, mem, g):
    n_layers = g.shape[0]
    rows = D_MODEL // N_DEV

    def body(dkv_ref, w_ref, mem_ref, g_ref, dw_ref, dg_ref):
        mb = mem_ref[...]
        r = lax.rsqrt(jnp.mean(mb * mb, axis=-1, keepdims=True) + EPS)
        nm = mb * r
        mn = (nm * g_ref[...]).astype(BF16)
        dkvb = dkv_ref[...].astype(BF16)
        dw_ref[...] = _dot_tn(mn, dkvb).astype(BF16).reshape(N_DEV, rows, 2 * MEM_WIDTH)
        dmn = _dot_nt(dkvb, w_ref[...].reshape(D_MODEL, 2 * MEM_WIDTH))
        dg_ref[...] = jnp.sum(dmn * nm, axis=0, keepdims=True)

    lay = lambda *shape: pl.BlockSpec((None,) + shape, lambda l: (l, 0, 0))
    major = pl.BlockSpec((N_DEV, rows, 2 * MEM_WIDTH), lambda l: (0, l, 0))
    return _pallas_call(
        body, name="memkv_bwd", grid=(n_layers,),
        out_shape=[_sds((N_DEV, n_layers * rows, 2 * MEM_WIDTH), BF16), _plain((n_layers, 1, D_MODEL), F32)],
        in_specs=[lay(N_MEM, 2 * MEM_WIDTH), major, _whole(mem.shape), lay(1, D_MODEL)],
        out_specs=[major, lay(1, D_MODEL)],
        compiler_params=_params(1, 32),
    )(dkv, w, mem, g.reshape(n_layers, 1, D_MODEL))


def _bwd_post_attn(dhb, wg_out, z, os_, ls_, qm, kv, mo, head_ones, token):
    s = dhb.shape[0]
    gw = GROUP_WIDTH
    nb = gw + MEM_WIDTH
    tm = ROW_TILE

    def body(_, dh_ref, w_ref, z_ref, o0, o1, o2, l0, l1, l2, qm_ref, kv_ref, mo_ref, bd_ref,
             dz_ref, do0, do1, do2, dl0, dl1, dl2, dqm_ref, dkv_ref, s0, s1):
        @pl.when(pl.program_id(0) == 0)
        def _():
            dkv_ref[...] = jnp.zeros_like(dkv_ref)

        dy = _dot_nt(dh_ref[...], _joined_columns(w_ref))
        ov, lv = [], []
        for o_ref, l_ref, d in zip((o0, o1, o2), (l0, l1, l2), DILATIONS):
            ov.append(_from_view(s0, o_ref, d))
            lv.append(_from_view(s1, l_ref, d))
        ws, mix = _mix_groups(ov, lv)
        sz, dsz = _silu_parts(z_ref[...])
        dz_ref[:, :gw] = (dy[:, :gw] * mix * dsz[:, :gw]).astype(BF16)
        dz_ref[:, gw:] = (dy[:, gw:] * mo_ref[...] * dsz[:, gw:]).astype(BF16)
        dbr = dy * sz
        dmix = dbr[:, :gw]
        t = dmix * mix
        th = t.astype(BF16)
        tl = (t - th.astype(F32)).astype(BF16)
        rs = _dot(th, bd_ref[...]) + _dot(tl, bd_ref[...])
        for wg_, do_ref, dl_ref, d in zip(ws, (do0, do1, do2), (dl0, dl1, dl2), DILATIONS):
            _to_view(s0, wg_ * dmix, do_ref, d)
            _to_view(s1, wg_ * rs, dl_ref, d)
        _mem_attn_bwd(qm_ref[...], kv_ref, dbr[:, gw:], dqm_ref, dkv_ref)

    vspecs = [_view_rows(gw, d) for d in DILATIONS]
    return _pallas_call(
        body, name="bwd_post_attn", grid=(s // tm,),
        out_shape=[_sds((s, nb), BF16)] + [_sds((s // d, d * gw), BF16) for d in DILATIONS]
                  + [_sds((s // d, d * gw), F32) for d in DILATIONS] + [_sds((s, MEM_WIDTH), BF16), _plain(kv.shape, F32)],
        in_specs=[ANY, _rows(D_MODEL), _whole(wg_out.shape), _rows(nb)] + vspecs * 2
                 + [_rows(MEM_WIDTH), _whole(kv.shape), _rows(MEM_WIDTH), _whole(head_ones.shape)],
        out_specs=[_rows(nb)] + vspecs * 2 + [_rows(MEM_WIDTH), _whole(kv.shape)],
        scratch_shapes=[_view_scratch(gw), _view_scratch(gw)],
        compiler_params=_params(1, 48),
    )(token, dhb, wg_out, z, *os_, *ls_, qm, kv, mo, head_ones)


def _attn_bwd(q, k, v, lse, do, dl, tabs, d, token):
    ln, dw = q.shape
    w = dw // d
    nb = ln // BLOCK
    reps = w // LANES
    two, before = _pair_specs(d, nb, w)
    two_t, _ = _pair_specs(d, nb, LANES)

    def attend(q_ref, l_ref, do_ref, dl_ref, dqs, acck, accv, rows, col0, kk, vv, acc_rows):
        valid = _band_mask(kk.shape[0])
        low = _low_head_lanes()
        pairs, qcols = _head_tiles(w, col0)
        cols = [slice(col0 + h * HEAD_DIM, col0 + h * HEAD_DIM + 1) for h in range(HEADS_PER_GROUP)]
        qhs = [h for qc in qcols for h in _split_pair(q_ref[rows, qc], low)]
        dobs = [h for qc in qcols for h in _split_pair(do_ref[rows, qc], low)]
        k2s = [kk[:, pr] for pr in pairs for _ in range(2)]
        v2s = [vv[:, pr] for pr in pairs for _ in range(2)]
        scs = [jnp.where(valid, _dot_nt(qh, k2), NEG) for qh, k2 in zip(qhs, k2s)]
        dps = [_dot_nt(dob, v2) for dob, v2 in zip(dobs, v2s)]
        ps = [jnp.exp(sc - l_ref[rows, col]) for sc, col in zip(scs, cols)]
        dss = [(p * (dp - dl_ref[rows, col])).astype(BF16) for p, dp, col in zip(ps, dps, cols)]
        pbs = [p.astype(BF16) for p in ps]
        for i, qc in enumerate(qcols):
            a, b = 2 * i, 2 * i + 1
            dqs[rows, qc] = jnp.where(low, _dot(dss[a], k2s[a]), _dot(dss[b], k2s[b])) * SCALE
            acck[acc_rows, qc] += _dot_tn(dss[a], qhs[a]) + _dot_tn(dss[b], qhs[b])
            accv[acc_rows, qc] += _dot_tn(pbs[a], dobs[a]) + _dot_tn(pbs[b], dobs[b])

    def body_streams(_, q_ref, kc_ref, vc_ref, l_ref, do_ref, dl_ref, c_ref, sa_ref, sb_ref,
                     dq_ref, dk_ref, dv_ref, acck, accv, dqs):
        acck[...] = jnp.zeros_like(acck)
        accv[...] = jnp.zeros_like(accv)
        for sb in range(2):
            cols = slice(sb * w, (sb + 1) * w)
            attend(q_ref, l_ref, do_ref, dl_ref, dqs, acck, accv, TOP, sb * w, kc_ref[:, cols], vc_ref[:, cols], TOP)
        tabs2 = [jnp.concatenate([jnp.tile(r[:, sb * LANES:(sb + 1) * LANES], (1, reps)) for sb in range(2)], axis=1)
                 for r in (c_ref, sa_ref, sb_ref)]
        dq_ref[...] = _rope_bwd(dqs[...], *tabs2).astype(BF16)
        dk_ref[...] = _rope_bwd(acck[...], *tabs2).astype(BF16)
        dv_ref[...] = accv[...].astype(BF16)

    def body_blocks(_, q_ref, kp_ref, kc_ref, vp_ref, vc_ref, l_ref, do_ref, dl_ref, cq, saq, sbq, ck, sak, sbk,
                    dq_ref, dk_ref, dv_ref, acck, accv, dqs):
        i = pl.program_id(0) % (nb // 2)

        @pl.when(i == 0)
        def _():
            acck[...] = jnp.zeros_like(acck)
            accv[...] = jnp.zeros_like(accv)

        refs = (q_ref, l_ref, do_ref, dl_ref, dqs, acck, accv)
        pl.when(i == 0)(lambda: attend(*refs, TOP, 0, kc_ref[TOP, :], vc_ref[TOP, :], TOP))
        pl.when(i != 0)(lambda: attend(
            *refs, TOP, 0, jnp.concatenate([kp_ref[...], kc_ref[TOP, :]], axis=0),
            jnp.concatenate([vp_ref[...], vc_ref[TOP, :]], axis=0),
            pl.ds(pl.multiple_of((2 * i - 1) * BLOCK, BLOCK), 2 * BLOCK)))
        attend(*refs, BOTTOM, 0, kc_ref[...], vc_ref[...], pl.ds(pl.multiple_of(2 * i * BLOCK, BLOCK), 2 * BLOCK))
        tq = [jnp.tile(r[...], (1, reps)) for r in (cq, saq, sbq)]
        dq_ref[...] = _rope_bwd(dqs[...], *tq).astype(BF16)

        @pl.when(i == nb // 2 - 1)
        def _():
            for r0 in range(0, nb * BLOCK, 2 * BLOCK):
                rows = slice(r0, r0 + 2 * BLOCK)
                tk = [jnp.tile(r[rows, :], (1, reps)) for r in (ck, sak, sbk)]
                dk_ref[rows, :] = _rope_bwd(acck[rows, :], *tk).astype(BF16)
                dv_ref[rows, :] = accv[rows, :].astype(BF16)

    if nb == 1:
        body = body_streams
        in_specs = [ANY] + [two] * 6 + [two_t] * 3
        args = (token, q, k, v, lse, do, dl, *tabs)
        out_specs = [two, two, two]
        acc_shape = (BLOCK, 2 * w)
    else:
        body = body_blocks
        stream = pl.BlockSpec((nb * BLOCK, w), lambda n: (0, n // (nb // 2)))
        stream_t = pl.BlockSpec((nb * BLOCK, LANES), lambda n: (0, n // (nb // 2)))
        in_specs = [ANY, two, before, two, before, two, two, two, two] + [two_t] * 3 + [stream_t] * 3
        args = (token, q, k, k, v, v, lse, do, dl, *tabs, *tabs)
        out_specs = [two, stream, stream]
        acc_shape = (nb * BLOCK, w)
    return _pallas_call(
        body, name=f"attn_bwd_d{d}", grid=(d * nb // 2,), out_shape=[_sds((ln, dw), BF16)] * 3,
        in_specs=in_specs, out_specs=out_specs,
        scratch_shapes=[pltpu.VMEM(acc_shape, F32), pltpu.VMEM(acc_shape, F32), pltpu.VMEM(two.block_shape, F32)],
        compiler_params=_params(1, 48),
    )(*args)


def _position():
    return lax.axis_index("x"), lax.axis_index("y"), lax.axis_index("c")


def _all_gather(shards, afters, name):
    n_a, n_in = len(shards), len(shards) + len(afters)

    def body(*refs):
        x_refs, out_refs = refs[:n_a], refs[n_in:n_in + n_a]
        send_sems, recv_sems, local_sems = refs[n_in + n_a:]
        x, y, c = _position()
        me, sibling = (x, y, c), (x, y, 1 - c)
        chips = [(1 - x, y), (x, 1 - y), (1 - x, 1 - y)]

        def rows(a, px, py, pc):
            return out_refs[a].at[4 * px + 2 * py + pc]

        def copy(a, k, block, to, own=False):
            return pltpu.make_async_remote_copy(
                src_ref=x_refs[a] if own else rows(a, *block), dst_ref=rows(a, *block),
                send_sem=send_sems.at[a, k], recv_sem=recv_sems.at[a, k], device_id=to, device_id_type=MESH)

        mine = [pltpu.make_async_copy(x_refs[a], rows(a, *me), local_sems.at[a]) for a in range(n_a)]
        for cp in mine:
            cp.start()
        first = []
        for j, chip in enumerate(chips):
            first += [copy(a, 1 + j, me, (*chip, c), own=True) for a in range(n_a)]
        first += [copy(a, 0, me, sibling, own=True) for a in range(n_a)]
        for cp in first:
            cp.start()
        passed = []
        for j, chip in enumerate(chips):
            for a in range(n_a):
                copy(a, 1 + j, (*chip, c), me).wait_recv()
                fwd = copy(a, 4 + j, (*chip, c), sibling)
                fwd.start()
                passed.append(fwd)
        for a in range(n_a):
            copy(a, 0, sibling, me).wait_recv()
        for j, chip in enumerate(chips):
            for a in range(n_a):
                copy(a, 4 + j, (*chip, 1 - c), me).wait_recv()
        for cp in first + passed:
            cp.wait_send()
        for cp in mine:
            cp.wait()

    return _pallas_call(
        body, name=name, out_shape=[_sds((N_DEV,) + t.shape, t.dtype) for t in shards],
        in_specs=[ANY] * n_in, out_specs=[ANY] * n_a,
        scratch_shapes=[pltpu.SemaphoreType.DMA((n_a, 7)), pltpu.SemaphoreType.DMA((n_a, 7)),
                        pltpu.SemaphoreType.DMA((n_a,))],
    )(*shards, *afters)


def _all_gather_relay(xs, name):
    def body(x_ref, out_ref, send_sems, recv_sems, local_sem):
        x, y, c = _position()
        me, sibling = (x, y, c), (x, y, 1 - c)
        xn, yn, diag = (1 - x, y, c), (x, 1 - y, c), (1 - x, 1 - y, c)
        src_nb = (x + c * (1 - 2 * x), y + (1 - c) * (1 - 2 * y), c)
        dst_nb = (x + (1 - c) * (1 - 2 * x), y + c * (1 - 2 * y), c)

        def rows(dev):
            return out_ref.at[4 * dev[0] + 2 * dev[1] + dev[2]]

        def copy(k, block, to, own=False):
            return pltpu.make_async_remote_copy(
                src_ref=x_ref if own else rows(block), dst_ref=rows(block),
                send_sem=send_sems.at[k], recv_sem=recv_sems.at[k], device_id=to, device_id_type=MESH)

        mine = pltpu.make_async_copy(x_ref, rows(me), local_sem)
        mine.start()
        first = [copy(1, me, xn, own=True), copy(2, me, yn, own=True), copy(0, me, sibling, own=True)]
        for cp in first:
            cp.start()
        copy(1, xn, me).wait_recv()
        copy(2, yn, me).wait_recv()
        relay = copy(3, src_nb, dst_nb)
        relay.start()
        passed = [copy(4, xn, sibling), copy(5, yn, sibling)]
        for cp in passed:
            cp.start()
        copy(3, diag, me).wait_recv()
        last = copy(6, diag, sibling)
        last.start()
        copy(0, sibling, me).wait_recv()
        for k, blk in ((4, (1 - x, y, 1 - c)), (5, (x, 1 - y, 1 - c)), (6, (1 - x, 1 - y, 1 - c))):
            copy(k, blk, me).wait_recv()
        for cp in first + [relay] + passed + [last]:
            cp.wait_send()
        mine.wait()

    return _pallas_call(
        body, name=name, out_shape=_sds((N_DEV,) + xs.shape, xs.dtype),
        in_specs=[ANY], out_specs=ANY,
        scratch_shapes=[pltpu.SemaphoreType.DMA((7,)), pltpu.SemaphoreType.DMA((7,)), pltpu.SemaphoreType.DMA],
    )(xs)


HBM_SPEC = pl.BlockSpec(memory_space=pltpu.HBM)
SEM_SPEC = pl.BlockSpec(memory_space=pltpu.SEMAPHORE)
EFFECT = pltpu.SideEffectType.DATAFLOW_SIDE_EFFECTING
def _plan_gather_own(src_refs, land_refs):
    x, y, c = _position()
    me = 4 * x + 2 * y + c
    peers = [(x, y, 1 - c), (1 - x, y, c), (x, 1 - y, c), (1 - x, 1 - y, c)]
    return [(src_refs[a], land_refs[a].at[me], (a, k), peer) for k, peer in enumerate(peers) for a in range(len(src_refs))]


def _plan_gather_pass(src_refs, land_refs):
    x, y, c = _position()
    chips = [(1 - x, y), (x, 1 - y), (1 - x, 1 - y)]
    return [(land_refs[a].at[4 * px + 2 * py + c], land_refs[a].at[4 * px + 2 * py + c], (a, j), (x, y, 1 - c))
            for j, (px, py) in enumerate(chips) for a in range(len(land_refs))]


def _plan_to_sibling(src_refs, land_refs):
    x, y, c = _position()
    return [(src_refs[a].at[2 * k + (1 - c)], land_refs[a].at[k], (a, k), (x, y, 1 - c))
            for k in range(4) for a in range(len(src_refs))]


def _plan_to_chips(src_refs, land_refs):
    x, y, c = _position()
    chips = [(1 - x, y), (x, 1 - y), (1 - x, 1 - y)]
    return [(src_refs[a].at[2 * px + py], land_refs[a].at[j], (a, j), (px, py, c))
            for j, (px, py) in enumerate(chips) for a in range(len(src_refs))]


def _split_start(srcs, lands, plan, n_sem, after, name):
    n_s, n_a = len(srcs), len(lands)
    n_b = n_s + n_a

    def body(*refs):
        src_refs, land_refs = refs[:n_s], refs[n_s:n_b]
        send_sems, recv_sems, token = refs[n_b + 1], refs[n_b + 2], refs[-1]
        for src, dst, (a, k), dev in plan(src_refs, land_refs):
            i = a * n_sem + k
            pltpu.make_async_remote_copy(src_ref=src, dst_ref=dst, send_sem=send_sems.at[i], recv_sem=recv_sems.at[i],
                                         device_id=dev, device_id_type=MESH).start()
        token[...] = jnp.zeros_like(token)

    bufs = list(srcs) + list(lands)
    res = pl.pallas_call(
        body, name=name,
        out_shape=(pltpu.SemaphoreType.DMA((n_a * n_sem,)), pltpu.SemaphoreType.DMA((n_a * n_sem,)),
                   *[pltpu.HBM(t.shape, t.dtype) for t in bufs], _plain((8, LANES), F32)),
        in_specs=[HBM_SPEC] * n_b + [ANY],
        out_specs=(SEM_SPEC, SEM_SPEC, *[HBM_SPEC] * n_b, pl.BlockSpec(memory_space=pltpu.VMEM)),
        input_output_aliases={i: 2 + i for i in range(n_b)},
        compiler_params=pltpu.CompilerParams(has_side_effects=EFFECT),
    )(*[pltpu.with_memory_space_constraint(t, pltpu.HBM) for t in bufs], after)
    return (res[0], res[1], res[2:2 + n_s], res[2 + n_s:2 + n_b]), res[-1]


def _split_wait(started, plan, after, name):
    send_sems, recv_sems, srcs, lands = started
    n_s, n_a = len(srcs), len(lands)
    n_b = n_s + n_a
    n_sem = send_sems.shape[0] // n_a

    def body(*refs):
        src_refs, land_refs = refs[:n_s], refs[n_s:n_b]
        s_sems, r_sems = refs[n_b], refs[n_b + 1]
        for src, dst, (a, k), dev in plan(src_refs, land_refs):
            i = a * n_sem + k
            cp = pltpu.make_async_remote_copy(src_ref=src, dst_ref=dst, send_sem=s_sems.at[i], recv_sem=r_sems.at[i],
                                              device_id=dev, device_id_type=MESH)
            cp.wait_send()
            cp.wait_recv()

    bufs = list(srcs) + list(lands)
    res = pl.pallas_call(
        body, name=name, out_shape=tuple(pltpu.HBM(t.shape, t.dtype) for t in bufs),
        in_specs=[HBM_SPEC] * n_b + [SEM_SPEC, SEM_SPEC, ANY],
        out_specs=tuple([HBM_SPEC] * n_b),
        input_output_aliases={i: i for i in range(n_b)},
        compiler_params=pltpu.CompilerParams(has_side_effects=EFFECT),
    )(*bufs, send_sems, recv_sems, after)
    return res[:n_s], res[n_s:]


SUBLANES = 8


def _row_tile(r):
    return max(t for t in range(SUBLANES, ROW_TILE + 1, SUBLANES) if r % t == 0)


def _rs_add_sibling(gp, recv, ck_arr, name):
    _, r, l = gp.shape
    tr = r if r <= 4 * ROW_TILE else _row_tile(r)
    block = lambda k, ck: (k + ck[1] + 1) % 4

    def body(ck_ref, g_ref, r_ref, pf_ref, pb_ref):
        sm = g_ref[...].astype(F32) + r_ref[...].astype(F32)
        pf_ref[...] = sm
        pb_ref[...] = sm.astype(BF16)

    spec = pl.BlockSpec((None, tr, l), lambda i, k, ck: (block(k, ck), i, 0))
    return _pallas_call(
        body, name=name,
        grid_spec=pltpu.PrefetchScalarGridSpec(
            num_scalar_prefetch=1, grid=(r // tr, 4),
            in_specs=[pl.BlockSpec((None, tr, l), lambda i, k, ck: (2 * block(k, ck) + ck[0], i, 0)), spec],
            out_specs=[pl.BlockSpec((tr, l), lambda i, k, ck: (i, 0)), spec]),
        out_shape=[_sds((r, l), F32), _sds((4, r, l), BF16)], compiler_params=_params(2, 32),
    )(ck_arr, gp, recv)


def _adam_update(w, gv, m, v):
    nm = ADAM_B1 * m + (1.0 - ADAM_B1) * gv
    nv = ADAM_B2 * v + (1.0 - ADAM_B2) * (gv * gv)
    m_hat = nm / (1.0 - ADAM_B1 ** ADAM_STEP)
    v_hat = nv / (1.0 - ADAM_B2 ** ADAM_STEP)
    return -ADAM_LR * (m_hat / (jnp.sqrt(v_hat) + ADAM_EPS) + ADAM_WD * w), nm, nv


def _rs_finish_adamw(pf, recv, w, m, v, after, name):
    r, l = pf.shape
    tr = _row_tile(r)

    def body(_, p_ref, r_ref, w_ref, m_ref, v_ref, g_ref, d_ref, nm_ref, nv_ref):
        gv = ((p_ref[...] + r_ref[0].astype(F32)) + r_ref[1].astype(F32)) + r_ref[2].astype(F32)
        g_ref[...] = gv
        d_ref[...], nm_ref[...], nv_ref[...] = _adam_update(w_ref[...], gv, m_ref[...], v_ref[...])

    spec = pl.BlockSpec((tr, l), lambda i: (i, 0))
    return _pallas_call(
        body, name=name, grid=(r // tr,),
        in_specs=[ANY, spec, pl.BlockSpec((3, tr, l), lambda i: (0, i, 0)), spec, spec, spec], out_specs=[spec] * 4,
        out_shape=[_plain((r, l), F32)] * 4, compiler_params=_params(1, 32),
    )(after, pf, recv, w, m, v)


SMALL_ROWS = dict(norm_g=(0, 2), mem_norm_g=(2, 4), final_g=(4, 5), conv_w=(5, 8))
LOSS_ROWS = (8, 16)


def _sum_adamw_small(g, ck_arr, states):
    names = list(SMALL_ROWS)
    n_dev, n_rows, _ = g.shape

    def body(ck_ref, g_ref, gc_ref, *refs):
        ins, loss_ref, outs = refs[:3 * len(names)], refs[3 * len(names)], refs[3 * len(names) + 1:]

        def total(ref, lo, hi):
            acc = ref[0, lo:hi, :]
            for j in range(1, n_dev):
                acc = acc + ref[j, lo:hi, :]
            return acc

        loss_ref[...] = total(gc_ref, *LOSS_ROWS)
        for i, n in enumerate(names):
            gv = total(gc_ref if n == "conv_w" else g_ref, *SMALL_ROWS[n])
            w_ref, m_ref, v_ref = ins[3 * i:3 * i + 3]
            g_out, d_out, nm_out, nv_out = outs[4 * i:4 * i + 4]
            g_out[...] = gv
            d_out[...], nm_out[...], nv_out[...] = _adam_update(w_ref[...], gv, m_ref[...], v_ref[...])

    flat = [t for n in names for t in states[n]]
    mine = pl.BlockSpec((n_dev, n_rows, LANES), lambda i, ck: (0, 0, 2 * ck[1] + ck[0]))
    res = _pallas_call(
        body, name="sum_adamw_small",
        grid_spec=pltpu.PrefetchScalarGridSpec(
            num_scalar_prefetch=1, grid=(1,),
            in_specs=[_whole(g.shape), mine] + [_whole(t.shape) for t in flat],
            out_specs=[_whole((SUBLANES, LANES))] + [_whole(states[n][0].shape) for n in names for _ in range(4)]),
        out_shape=[_plain((SUBLANES, LANES), F32)] + [_plain(states[n][0].shape, F32) for n in names for _ in range(4)],
        compiler_params=_params(1, 32),
    )(ck_arr, g, g, *flat)
    return res[0], {n: tuple(res[1 + 4 * i:5 + 4 * i]) for i, n in enumerate(names)}


def _finish(name, pf, recv, w, m, v, after):
    if name in ("attn_w_in", "conv_w_in"):
        res = _rs_finish_adamw(pf, recv, w.T, m.T, v.T, after, "rs_finish_adamw_" + name)
        return tuple(t.T for t in res)
    return _rs_finish_adamw(pf, recv, w, m, v, after, "rs_finish_adamw_" + name)


def kernel(x, mem, positions, norm_g, mem_norm_g, w_mem_kv, attn_w_in, attn_w_out, conv_w_in, conv_w, conv_w_out, final_g, loss_target, m_norm_g, m_mem_norm_g, m_w_mem_kv, m_attn_w_in, m_attn_w_out, m_conv_w_in, m_conv_w, m_conv_w_out, m_final_g, v_norm_g, v_mem_norm_g, v_w_mem_kv, v_attn_w_in, v_attn_w_out, v_conv_w_in, v_conv_w, v_conv_w_out, v_final_g):
    px, py, pc = _position()
    me = 4 * px + 2 * py + pc
    ck_arr = jnp.stack([pc, 2 * px + py]).astype(jnp.int32)
    x, mem, pos, tgt = x[0], mem[0], positions[0], loss_target[0]

    wg_in0 = _all_gather_relay(attn_w_in[0].astype(BF16), "gather_w_in0")
    def gather_start(shards, after, name):
        lands = [lax.dynamic_update_slice(lax.empty((N_DEV,) + t.shape, t.dtype), t[None], (me, 0, 0)) for t in shards]
        return _split_start(shards, lands, _plan_gather_own, 4, after, name + "_start")

    def gather_pass(weights, after, name):
        _, lands = _split_wait(weights, _plan_gather_own, after, name + "_wait")
        return _split_start([], lands, _plan_gather_pass, 3, after, name + "_pass_start")

    rest0, token = gather_start([attn_w_out[0].astype(BF16), w_mem_kv.astype(BF16).reshape(-1, w_mem_kv.shape[2]),
                                 jnp.pad(conv_w[0], ((0, 5), (0, 0)))], wg_in0, "gather_rest")
    conv_ws, token = gather_start([conv_w_in[0].astype(BF16), conv_w_out[0].astype(BF16)], token, "gather_conv")

    tabs = _rope_tables(pos)
    g0, g1 = norm_g[0:1], norm_g[1:2]

    hn0, qs, ks, vs, tabs_v, qm0, z0 = _inproj_attn(x, g0, wg_in0, tabs, token)
    os_, ls_ = [], []
    for j, d in enumerate(DILATIONS):
        if j == 2:
            rest0, token = gather_pass(rest0, ls_[1], "gather_rest")
        o, l = _attn_fwd(qs[j], ks[j], vs[j], d, token)
        os_.append(o)
        ls_.append(l)

    conv_ws, token = gather_pass(conv_ws, ls_[2], "gather_conv")
    _, (wg_out0, wg_kv, cw_all) = _split_wait(rest0, _plan_gather_pass, token, "gather_rest_pass_wait")
    cw = cw_all[:, 0:3].transpose(1, 0, 2).reshape(3, -1)
    kv = _memkv_fwd(mem, mem_norm_g, wg_kv)
    y0, y0_t, mo0, h1 = _post_attn(os_, ls_, qm0, kv[0], z0, x, wg_out0)
    _, (wg_in1, wg_out1) = _split_wait(conv_ws, _plan_gather_pass, h1, "gather_conv_pass_wait")
    w_out1 = wg_out1.reshape(-1, wg_out1.shape[2])

    hn1, bg, cg, u, qm1, z1 = _inproj_conv(h1, g1, wg_in1)
    y1, y1_t, mo1, dh2, dh2b, loss_acc, d_final_g = _post_conv_loss(
        bg, cg, u, qm1, kv[1], z1, h1, w_out1, cw, final_g.reshape(1, -1), tgt)

    d_w_out1 = _wgrad_rows(y1_t, dh2b, "wgrad_out1")
    dz1, dbg, dconv, dqm1, dkv1 = _bwd_post_conv(dh2b, w_out1, bg, cg, u, z1, qm1, kv[1], mo1, cw)
    dcg, du, dcw = _bwd_conv(dconv, cg, u, cw)
    dh1, dg1, dh1b, dproj1_t = _dgrad_norm([(dbg, 1), (dcg, 1), (du, 1), (dqm1, 1), (dz1, 1)], wg_in1, h1, g1, dh2,
                                           dh2, True, "dgrad_norm_conv")
    d_w_in1 = _wgrad_shards_t(dproj1_t, hn1, wg_in1.shape[2], "wgrad_in1")

    d_w_out0 = _wgrad_cols(y0_t, dh1b, wg_out0.shape[2], "wgrad_out0")

    gw = GROUP_WIDTH
    ones = (jnp.arange(gw)[:, None] // HEAD_DIM == jnp.arange(gw)[None, :] // HEAD_DIM).astype(BF16)
    res = _bwd_post_attn(dh1b, wg_out0, z0, os_, ls_, qm0, kv[0], mo0, ones, dg1)
    dz0, dos, dls, dqm0, dkv0 = res[0], res[1:4], res[4:7], res[7], res[8]
    d_w_kv, d_mem_g = _memkv_bwd(jnp.stack([dkv0, dkv1]), wg_kv, mem, mem_norm_g)

    names1 = ["conv_w_in", "conv_w_out", "attn_w_out", "w_mem_kv"]
    grads1 = [d_w_in1, d_w_out1, d_w_out0, d_w_kv]
    started, token = _split_start(grads1, [lax.empty((4,) + g.shape[1:], g.dtype) for g in grads1],
                                  _plan_to_sibling, 4, d_mem_g, "rs1_sibling_start")

    dqs, dks, dvs = [], [], []
    for j, d in enumerate(DILATIONS):
        if j == 1:
            grads1, from_sibling = _split_wait(started, _plan_to_sibling, dqs[0][0], "rs1_sibling_wait")
            parts1 = [_rs_add_sibling(g, r, ck_arr, "rs_add_sibling_" + n)
                      for g, r, n in zip(grads1, from_sibling, names1)]
            pbs1 = [pb for _, pb in parts1]
            started, token = _split_start(pbs1, [lax.empty((3,) + p.shape[1:], p.dtype) for p in pbs1],
                                          _plan_to_chips, 3, dg1, "rs1_chips_start")
        dq, dk, dv = _attn_bwd(qs[j], ks[j], vs[j], ls_[j], dos[j], dls[j], tabs_v[j], d, token)
        dqs.append((dq, d))
        dks.append((dk, d))
        dvs.append((dv, d))
    pieces0 = dqs + dks + dvs + [(dqm0, 1), (dz0, 1)]
    dproj0_t = _assemble_dproj_t(pieces0, N_DEV * wg_in0.shape[2], "assemble_dproj_attn")
    d_w_in0 = _wgrad_shards_t(dproj0_t, hn0, wg_in0.shape[2], "wgrad_in0")

    names0 = ["attn_w_in"]
    grads0 = [d_w_in0]
    started0, token0 = _split_start(grads0, [lax.empty((4,) + g.shape[1:], g.dtype) for g in grads0],
                                    _plan_to_sibling, 4, dg1, "rs0_sibling_start")
    _, from_chips1 = _split_wait(started, _plan_to_chips, token0, "rs1_chips_wait")
    shard = dict(attn_w_in=(attn_w_in[0], m_attn_w_in[0], v_attn_w_in[0]),
                 attn_w_out=(attn_w_out[0], m_attn_w_out[0], v_attn_w_out[0]),
                 conv_w_in=(conv_w_in[0], m_conv_w_in[0], v_conv_w_in[0]),
                 conv_w_out=(conv_w_out[0], m_conv_w_out[0], v_conv_w_out[0]),
                 w_mem_kv=tuple(t.reshape(-1, t.shape[2]) for t in (w_mem_kv, m_w_mem_kv, v_w_mem_kv)))
    finish1 = {n: (pf, r) for n, (pf, _), r in zip(names1, parts1, from_chips1)}
    big = {"conv_w_in": _finish("conv_w_in", *finish1["conv_w_in"], *shard["conv_w_in"], token0)}

    grads0, from_sibling = _split_wait(started0, _plan_to_sibling, big["conv_w_in"][1].T, "rs0_sibling_wait")
    parts0 = [_rs_add_sibling(g, r, ck_arr, "rs_add_sibling_" + n) for g, r, n in zip(grads0, from_sibling, names0)]
    pbs0 = [pb for _, pb in parts0]
    started0, token0 = _split_start(pbs0, [lax.empty((3,) + p.shape[1:], p.dtype) for p in pbs0],
                                    _plan_to_chips, 3, dg1, "rs0_chips_start")
    dx, dg0 = _dgrad_norm(pieces0, wg_in0, x, g0, dh1, token0, False, "dgrad_norm_attn")
    for n in names1[1:]:
        big[n] = _finish(n, *finish1[n], *shard[n], token0)

    small_part = jnp.concatenate([dg0, dg1, d_mem_g.reshape(2, -1), d_final_g, dcw[0:3]], axis=0)
    small_part = jnp.concatenate([small_part, jnp.broadcast_to(loss_acc[0, 0], small_part.shape)], axis=0)
    small_w = dict(norm_g=(norm_g, m_norm_g, v_norm_g), mem_norm_g=(mem_norm_g, m_mem_norm_g, v_mem_norm_g),
                   conv_w=(conv_w, m_conv_w, v_conv_w), final_g=(final_g, m_final_g, v_final_g))
    loss_tile, small_res = _sum_adamw_small(
        _all_gather([small_part], [big[n][1] for n in names1[1:]], "gather_small_grads")[0], ck_arr,
        {n: tuple(t.reshape(-1, t.shape[-1]) for t in wmv) for n, wmv in small_w.items()})
    loss = loss_tile[0, 0]
    for n, wmv in small_w.items():
        big[n] = tuple(t.reshape(wmv[0].shape) for t in small_res[n])

    _, from_chips0 = _split_wait(started0, _plan_to_chips, big["final_g"][1], "rs0_chips_wait")
    for n, (pf, _), r in zip(names0, parts0, from_chips0):
        big[n] = _finish(n, pf, r, *shard[n], big["final_g"][1])
    for n in ("attn_w_in", "attn_w_out", "conv_w_in", "conv_w_out"):
        big[n] = tuple(t[None] for t in big[n])
    big["w_mem_kv"] = tuple(t.reshape(w_mem_kv.shape) for t in big["w_mem_kv"])

    order = ["norm_g", "mem_norm_g", "w_mem_kv", "attn_w_in", "attn_w_out", "conv_w_in", "conv_w", "conv_w_out", "final_g"]
    return (loss, dx[None], *[big[n][0] for n in order], *[big[n][1] for n in order],
            *[big[n][2] for n in order], *[big[n][3] for n in order])
```

```python
import jax
import jax.numpy as jnp
from jax import lax
from jax.experimental import pallas as pl
from jax.experimental.pallas import tpu as pltpu

F32 = jnp.float32
BF16 = jnp.bfloat16

N_DEV = 8
D_MODEL = 1024
HEAD_DIM = 64
ROT_DIM = HEAD_DIM // 4
ROPE_THETA = 500000.0
DILATIONS = (1, 4, 16)
HEADS_PER_GROUP = 8
GROUP_WIDTH = HEADS_PER_GROUP * HEAD_DIM
BLOCK = 128
N_MEM = 256
MEM_HEADS = 4
MEM_WIDTH = MEM_HEADS * HEAD_DIM
CONV_WIDTH = D_MODEL
EPS = 1e-6
SCALE = HEAD_DIM ** -0.5
NEG = -1e30

ADAM_LR = 0.001
ADAM_B1 = 0.9
ADAM_B2 = 0.999
ADAM_EPS = 1e-08
ADAM_WD = 0.01
ADAM_STEP = 10

ROW_TILE = 256
WGRAD_SHARDS = 4
LANES = 128
MESH = pl.DeviceIdType.MESH
ANY = pl.BlockSpec(memory_space=pl.ANY)


def _pallas_call(body, **kw):
    call = pl.pallas_call(body, **kw)

    def run(*args):
        pinned = [pltpu.with_memory_space_constraint(a, pltpu.HBM) if jnp.issubdtype(a.dtype, jnp.floating) else a
                  for a in args]
        return call(*pinned)

    return run


def _dot(a, b):
    return lax.dot_general(a, b, (((1,), (0,)), ((), ())), preferred_element_type=F32)


def _dot_nt(a, b):
    return lax.dot_general(a, b, (((1,), (1,)), ((), ())), preferred_element_type=F32)


def _dot_tn(a, b):
    return lax.dot_general(a, b, (((0,), (0,)), ((), ())), preferred_element_type=F32)


def _params(n_grid, vmem_mb=48):
    return pltpu.CompilerParams(dimension_semantics=("arbitrary",) * n_grid, vmem_limit_bytes=vmem_mb << 20)


def _rows(width, tm=ROW_TILE):
    return pl.BlockSpec((tm, width), lambda i: (i, 0))


def _view_rows(width, d, tm=ROW_TILE):
    return pl.BlockSpec((tm // d, d * width), lambda i: (i, 0))


def _whole(shape):
    return pl.BlockSpec(shape, lambda *_: (0,) * len(shape))


def _resident(shape):
    return pl.BlockSpec(shape, lambda *_: (0,) * len(shape), pipeline_mode=pl.Buffered(1))


def _sds(shape, dtype):
    return pltpu.HBM(shape, dtype)


def _plain(shape, dtype):
    return jax.ShapeDtypeStruct(shape, dtype)


def _silu_parts(z):
    sg = jax.nn.sigmoid(z)
    return z * sg, sg * (1.0 + z * (1.0 - sg))


def _to_view(scr, val, out_ref, d):
    tm, w = val.shape
    if d == 1:
        out_ref[...] = val.astype(out_ref.dtype)
        return
    for cb in range(w // LANES):
        scr[cb] = val[:, cb * LANES:(cb + 1) * LANES]
    for r in range(d):
        for cb in range(w // LANES):
            lo = r * w + cb * LANES
            out_ref[:, lo:lo + LANES] = scr[cb, pl.ds(r, tm // d, stride=d), :].astype(out_ref.dtype)


def _from_view(scr, in_ref, d):
    if d == 1:
        return in_ref[...].astype(F32)
    nc, tm, _ = scr.shape
    w = nc * LANES
    for r in range(d):
        for cb in range(nc):
            lo = r * w + cb * LANES
            scr[cb, pl.ds(r, tm // d, stride=d), :] = in_ref[:, lo:lo + LANES].astype(F32)
    return jnp.concatenate([scr[cb] for cb in range(nc)], axis=1)


def _view_scratch(width, tm=ROW_TILE):
    return pltpu.VMEM((width // LANES, tm, LANES), F32)


def _rope_tables(pos):
    half = ROT_DIM // 2
    inv_freq = ROPE_THETA ** (-jnp.arange(half, dtype=F32) * (2.0 / ROT_DIM))
    ang = pos.astype(F32)[:, None] * inv_freq
    cos, sin = jnp.cos(ang), jnp.sin(ang)
    s = pos.shape[0]
    z8 = jnp.zeros((s, half), F32)
    rest = HEAD_DIM - ROT_DIM
    cosf = jnp.concatenate([cos, cos, jnp.ones((s, rest), F32)], axis=1)
    sa = jnp.concatenate([-sin, z8, jnp.zeros((s, rest), F32)], axis=1)
    sb = jnp.concatenate([z8, sin, jnp.zeros((s, rest), F32)], axis=1)
    return tuple(jnp.tile(t, (1, LANES // HEAD_DIM)) for t in (cosf, sa, sb))


def _rope_fwd(t, cv, sav, sbv):
    w = t.shape[1]
    return t * cv + pltpu.roll(t, w - ROT_DIM // 2, 1) * sav + pltpu.roll(t, ROT_DIM // 2, 1) * sbv


def _rope_bwd(g, cv, sav, sbv):
    w = g.shape[1]
    return g * cv + pltpu.roll(g * sav, ROT_DIM // 2, 1) + pltpu.roll(g * sbv, w - ROT_DIM // 2, 1)


def _joined_columns(wg_ref):
    assert wg_ref.shape[2] % LANES == 0
    return jnp.concatenate([wg_ref[j] for j in range(N_DEV)], axis=1)


def _join_once(wg_ref, w_scr):
    c = wg_ref.shape[2]

    @pl.when(pl.program_id(0) == 0)
    def _():
        for j in range(N_DEV):
            w_scr[:, j * c:(j + 1) * c] = wg_ref[j]


def _inproj_attn(x, g, wg, tabs, token):
    s, d_model = x.shape
    gw = GROUP_WIDTH
    n = N_DEV * wg.shape[2]
    nz = n - 9 * gw - MEM_WIDTH
    reps = gw // LANES
    tm = ROW_TILE

    def body(_, x_ref, g_ref, w_ref, c_ref, sa_ref, sb_ref, hn_ref, *rest):
        outs, (wj, scr, tscr) = rest[:-3], rest[-3:]
        q_refs, k_refs, v_refs, t_refs, qm_ref, z_ref = outs[0:3], outs[3:6], outs[6:9], outs[9:18], outs[18], outs[19]
        _join_once(w_ref, wj)
        xb = x_ref[...]
        r = lax.rsqrt(jnp.mean(xb * xb, axis=-1, keepdims=True) + EPS)
        hn = ((xb * r) * g_ref[...]).astype(BF16)
        hn_ref[...] = hn
        proj = lambda lo, hi: _dot(hn, wj[:, lo:hi])
        tab = (c_ref[...], sa_ref[...], sb_ref[...])
        cv, sav, sbv = [jnp.tile(t, (1, reps)) for t in tab]
        for j, d in enumerate(DILATIONS):
            tq = _rope_fwd(proj(j * gw, (j + 1) * gw), cv, sav, sbv)
            _to_view(scr, tq * SCALE, q_refs[j], d)
            tk = _rope_fwd(proj((3 + j) * gw, (4 + j) * gw), cv, sav, sbv)
            _to_view(scr, tk, k_refs[j], d)
            _to_view(scr, proj((6 + j) * gw, (7 + j) * gw), v_refs[j], d)
            for i in range(3):
                _to_view(tscr, tab[i], t_refs[3 * j + i], d)
        qm_ref[...] = proj(9 * gw, 9 * gw + MEM_WIDTH).astype(BF16)
        z_ref[...] = proj(9 * gw + MEM_WIDTH, n)

    views = [_sds((s // d, d * gw), BF16) for d in DILATIONS]
    tviews = [_sds((s // d, d * LANES), F32) for d in DILATIONS for _ in range(3)]
    out_shape = [_sds((s, d_model), BF16)] + views * 3 + tviews + [_sds((s, MEM_WIDTH), BF16), _sds((s, nz), F32)]
    vspecs = [_view_rows(gw, d, tm) for d in DILATIONS]
    tspecs = [_view_rows(LANES, d, tm) for d in DILATIONS for _ in range(3)]
    out_specs = [_rows(d_model, tm)] + vspecs * 3 + tspecs + [_rows(MEM_WIDTH, tm), _rows(nz, tm)]
    res = _pallas_call(
        body, name="inproj_attn", grid=(s // tm,), out_shape=out_shape,
        in_specs=[ANY, _rows(d_model, tm), _whole((1, d_model)), _resident(wg.shape)] + [_rows(LANES, tm)] * 3,
        out_specs=out_specs,
        scratch_shapes=[pltpu.VMEM((d_model, n), BF16), _view_scratch(gw, tm), _view_scratch(LANES, tm)],
        compiler_params=_params(1, 60),
    )(token, x, g, wg, *tabs)
    tabs_v = [res[10 + 3 * j:13 + 3 * j] for j in range(3)]
    return res[0], res[1:4], res[4:7], res[7:10], tabs_v, res[19], res[20]


def _band_mask(n_keys):
    qi = lax.broadcasted_iota(jnp.int32, (BLOCK, n_keys), 0)
    kj = lax.broadcasted_iota(jnp.int32, (BLOCK, n_keys), 1)
    if n_keys == BLOCK:
        return kj <= qi
    return jnp.logical_or(jnp.logical_and(kj < BLOCK, kj >= qi), jnp.logical_and(kj >= BLOCK, (kj - BLOCK) <= qi))


def _low_head_lanes():
    return lax.broadcasted_iota(jnp.int32, (1, LANES), 1) < HEAD_DIM


def _split_pair(t, low):
    zero = jnp.zeros_like(t)
    return jnp.where(low, t, zero), jnp.where(low, zero, t)


def _pair_specs(d, nb, w):
    if nb == 1:
        return pl.BlockSpec((BLOCK, 2 * w), lambda n: (0, n)), None
    half = nb // 2
    two = pl.BlockSpec((2 * BLOCK, w), lambda n: (n % half, n // half))
    before = pl.BlockSpec((BLOCK, w), lambda n: (jnp.maximum(2 * (n % half) - 1, 0), n // half))
    return two, before


def _head_tiles(w, col0):
    return ([slice(p * LANES, (p + 1) * LANES) for p in range(w // LANES)],
            [slice(col0 + p * LANES, col0 + (p + 1) * LANES) for p in range(w // LANES)])


def _attend_fwd(q_ref, o_ref, lse_ref, rows, col0, kk, vv):
    w = kk.shape[1]
    valid = _band_mask(kk.shape[0])
    low = _low_head_lanes()
    pairs, qcols = _head_tiles(w, col0)
    qs_ = [h for qc in qcols for h in _split_pair(q_ref[rows, qc], low)]
    k2s = [kk[:, pr] for pr in pairs for _ in range(2)]
    scs = [jnp.where(valid, _dot_nt(qh, k2), NEG) for qh, k2 in zip(qs_, k2s)]
    ms = [jnp.max(sc, axis=-1, keepdims=True) for sc in scs]
    ps = [jnp.exp(sc - m) for sc, m in zip(scs, ms)]
    ls = [jnp.sum(p, axis=-1, keepdims=True) for p in ps]
    pns = [(p * (1.0 / l)).astype(BF16) for p, l in zip(ps, ls)]
    for i, (pr, qc) in enumerate(zip(pairs, qcols)):
        v2 = vv[:, pr]
        a, b = 2 * i, 2 * i + 1
        o_ref[rows, qc] = jnp.where(low, _dot(pns[a], v2), _dot(pns[b], v2))
        lse_ref[rows, qc] = jnp.where(low, ms[a] + jnp.log(ls[a]), ms[b] + jnp.log(ls[b]))


TOP, BOTTOM = slice(0, BLOCK), slice(BLOCK, 2 * BLOCK)


def _attn_fwd(q, k, v, d, token):
    ln, dw = q.shape
    w = dw // d
    nb = ln // BLOCK
    two, before = _pair_specs(d, nb, w)

    def body_streams(_, q_ref, kc_ref, vc_ref, o_ref, lse_ref):
        for sb in range(2):
            cols = slice(sb * w, (sb + 1) * w)
            _attend_fwd(q_ref, o_ref, lse_ref, TOP, sb * w, kc_ref[:, cols], vc_ref[:, cols])

    def body_blocks(_, q_ref, kp_ref, kc_ref, vp_ref, vc_ref, o_ref, lse_ref):
        first = pl.program_id(0) % (nb // 2) == 0
        pl.when(first)(lambda: _attend_fwd(q_ref, o_ref, lse_ref, TOP, 0, kc_ref[TOP, :], vc_ref[TOP, :]))
        pl.when(jnp.logical_not(first))(lambda: _attend_fwd(
            q_ref, o_ref, lse_ref, TOP, 0, jnp.concatenate([kp_ref[...], kc_ref[TOP, :]], axis=0),
            jnp.concatenate([vp_ref[...], vc_ref[TOP, :]], axis=0)))
        _attend_fwd(q_ref, o_ref, lse_ref, BOTTOM, 0, kc_ref[...], vc_ref[...])

    if nb == 1:
        body, in_specs, args = body_streams, [ANY, two, two, two], (token, q, k, v)
    else:
        body, in_specs, args = body_blocks, [ANY, two, before, two, before, two], (token, q, k, k, v, v)
    return _pallas_call(
        body, name=f"attn_fwd_d{d}", grid=(d * nb // 2,), out_shape=[_sds((ln, dw), F32)] * 2,
        in_specs=in_specs, out_specs=[two, two], compiler_params=_params(1, 32),
    )(*args)


def _memkv_fwd(mem, g, w):
    n_layers = g.shape[0]
    rows = D_MODEL // N_DEV

    def body(mem_ref, g_ref, w_ref, kv_ref):
        mb = mem_ref[...]
        r = lax.rsqrt(jnp.mean(mb * mb, axis=-1, keepdims=True) + EPS)
        mn = ((mb * r) * g_ref[...]).astype(BF16)
        kv_ref[...] = _dot(mn, w_ref[...].reshape(D_MODEL, 2 * MEM_WIDTH)).astype(BF16)

    return _pallas_call(
        body, name="memkv_fwd", grid=(n_layers,),
        out_shape=_plain((n_layers, N_MEM, 2 * MEM_WIDTH), BF16),
        in_specs=[_whole(mem.shape), pl.BlockSpec((None, 1, D_MODEL), lambda l: (l, 0, 0)),
                  pl.BlockSpec((N_DEV, rows, 2 * MEM_WIDTH), lambda l: (0, l, 0))],
        out_specs=pl.BlockSpec((None, N_MEM, 2 * MEM_WIDTH), lambda l: (l, 0, 0)),
        compiler_params=_params(1, 32),
    )(mem, g.reshape(n_layers, 1, D_MODEL), w)


def _mix_groups(os_, ls_):
    mx = jnp.maximum(jnp.maximum(ls_[0], ls_[1]), ls_[2])
    es = [jnp.exp(t - mx) for t in ls_]
    inv = 1.0 / (es[0] + es[1] + es[2])
    ws = [e * inv for e in es]
    mix = ws[0] * os_[0] + ws[1] * os_[1] + ws[2] * os_[2]
    return ws, mix


MEM_PAIRS = [slice(p * LANES, (p + 1) * LANES) for p in range(MEM_WIDTH // LANES)]


def _mem_probs(qhs, k2s):
    scs = [_dot_nt(qh, k2) * SCALE for qh, k2 in zip(qhs, k2s)]
    es = [jnp.exp(sc - jnp.max(sc, axis=-1, keepdims=True)) for sc in scs]
    return [e * (1.0 / jnp.sum(e, axis=-1, keepdims=True)) for e in es]


def _mem_attn_into(qm, kv_ref, mo_ref):
    low = _low_head_lanes()
    qhs = [h for pr in MEM_PAIRS for h in _split_pair(qm[:, pr], low)]
    k2s = [kv_ref[:, pr] for pr in MEM_PAIRS for _ in range(2)]
    ps = [p.astype(BF16) for p in _mem_probs(qhs, k2s)]
    for i, pr in enumerate(MEM_PAIRS):
        v2 = kv_ref[:, MEM_WIDTH + i * LANES:MEM_WIDTH + (i + 1) * LANES]
        mo_ref[:, pr] = jnp.where(low, _dot(ps[2 * i], v2), _dot(ps[2 * i + 1], v2))


def _mem_attn_bwd(qm, kv_ref, dmem, dqm_ref, dkv_ref):
    low = _low_head_lanes()
    dmb = dmem.astype(BF16)
    vps = [slice(MEM_WIDTH + i * LANES, MEM_WIDTH + (i + 1) * LANES) for i in range(len(MEM_PAIRS))]
    qhs = [h for pr in MEM_PAIRS for h in _split_pair(qm[:, pr], low)]
    dhs = [h for pr in MEM_PAIRS for h in _split_pair(dmb[:, pr], low)]
    k2s = [kv_ref[:, pr] for pr in MEM_PAIRS for _ in range(2)]
    v2s = [kv_ref[:, vp] for vp in vps for _ in range(2)]
    ps = _mem_probs(qhs, k2s)
    dps = [_dot_nt(dh, v2) for dh, v2 in zip(dhs, v2s)]
    dss = [(p * (dp - jnp.sum(dp * p, axis=-1, keepdims=True)) * SCALE).astype(BF16) for p, dp in zip(ps, dps)]
    pbs = [p.astype(BF16) for p in ps]
    for i, (pr, vp) in enumerate(zip(MEM_PAIRS, vps)):
        a, b = 2 * i, 2 * i + 1
        dqm_ref[:, pr] = jnp.where(low, _dot(dss[a], k2s[a]), _dot(dss[b], k2s[b])).astype(BF16)
        dkv_ref[:, pr] += _dot_tn(dss[a], qhs[a]) + _dot_tn(dss[b], qhs[b])
        dkv_ref[:, vp] += _dot_tn(pbs[a], dhs[a]) + _dot_tn(pbs[b], dhs[b])


def _post_attn(os_, ls_, qm, kv, z, x, wg_out):
    s, d_model = x.shape
    gw = GROUP_WIDTH
    nb = gw + MEM_WIDTH
    tm = ROW_TILE

    def body(o0, o1, o2, l0, l1, l2, qm_ref, kv_ref, z_ref, x_ref, w_ref, y_ref, yt_ref, mo_ref, h_ref, s0, s1):
        ov, lv = [], []
        for o_ref, l_ref, d in zip((o0, o1, o2), (l0, l1, l2), DILATIONS):
            ov.append(_from_view(s0, o_ref, d))
            lv.append(_from_view(s1, l_ref, d))
        _, mix = _mix_groups(ov, lv)
        _mem_attn_into(qm_ref[...], kv_ref, mo_ref)
        sz, _ = _silu_parts(z_ref[...])
        y_ref[:, :gw] = (mix * sz[:, :gw]).astype(BF16)
        y_ref[:, gw:] = (mo_ref[...] * sz[:, gw:]).astype(BF16)
        y = y_ref[...]
        yt_ref[...] = y.T
        h_ref[...] = x_ref[...] + _dot(y, _joined_columns(w_ref))

    vspecs = [_view_rows(gw, d) for d in DILATIONS]
    return _pallas_call(
        body, name="post_attn", grid=(s // tm,),
        out_shape=[_sds((s, nb), BF16), _sds((nb, s), BF16), _sds((s, MEM_WIDTH), F32), _sds((s, d_model), F32)],
        in_specs=vspecs * 2 + [_rows(MEM_WIDTH), _whole(kv.shape), _rows(nb), _rows(d_model), _whole(wg_out.shape)],
        out_specs=[_rows(nb), pl.BlockSpec((nb, tm), lambda i: (0, i)), _rows(MEM_WIDTH), _rows(d_model)],
        scratch_shapes=[_view_scratch(gw), _view_scratch(gw)],
        compiler_params=_params(1, 40),
    )(*os_, *ls_, qm, kv, z, x, wg_out)


def _inproj_conv(x, g, wg):
    s, d_model = x.shape
    c = CONV_WIDTH
    n = N_DEV * wg.shape[2]
    nz = n - 3 * c - MEM_WIDTH
    tm = ROW_TILE

    def body(x_ref, g_ref, w_ref, hn_ref, bg_ref, cg_ref, u_ref, qm_ref, z_ref, wj):
        _join_once(w_ref, wj)
        xb = x_ref[...]
        r = lax.rsqrt(jnp.mean(xb * xb, axis=-1, keepdims=True) + EPS)
        hn = ((xb * r) * g_ref[...]).astype(BF16)
        hn_ref[...] = hn
        bg_ref[...] = _dot(hn, wj[:, 0:c])
        cg_ref[...] = _dot(hn, wj[:, c:2 * c])
        u_ref[...] = _dot(hn, wj[:, 2 * c:3 * c])
        qm_ref[...] = _dot(hn, wj[:, 3 * c:3 * c + MEM_WIDTH]).astype(BF16)
        z_ref[...] = _dot(hn, wj[:, 3 * c + MEM_WIDTH:])

    return _pallas_call(
        body, name="inproj_conv", grid=(s // tm,),
        out_shape=[_sds((s, d_model), BF16)] + [_sds((s, c), F32)] * 3 + [_sds((s, MEM_WIDTH), BF16), _sds((s, nz), F32)],
        in_specs=[_rows(d_model), _whole((1, d_model)), _resident(wg.shape)],
        out_specs=[_rows(d_model)] + [_rows(c)] * 3 + [_rows(MEM_WIDTH), _rows(nz)],
        scratch_shapes=[pltpu.VMEM((d_model, n), BF16)],
        compiler_params=_params(1, 60),
    )(x, g, wg)


HALO = 8


def _halo_before(width, tm=ROW_TILE):
    return pl.BlockSpec((HALO, width), lambda i: (jnp.maximum(i * (tm // HALO) - 1, 0), 0))


def _halo_after(width, n_rows, tm=ROW_TILE):
    return pl.BlockSpec((HALO, width), lambda i: (jnp.minimum((i + 1) * (tm // HALO), n_rows // HALO - 1), 0))


def _conv_taps(cg_ref, u_ref, cgh_ref, uh_ref, i):
    a = cg_ref[...] * u_ref[...]
    ah = jnp.where(i > 0, cgh_ref[...] * uh_ref[...], 0.0)
    row = lax.broadcasted_iota(jnp.int32, a.shape, 0)
    a1 = jnp.where(row == 0, ah[HALO - 1:HALO], pltpu.roll(a, 1, 0))
    a2 = jnp.where(row == 0, ah[HALO - 2:HALO - 1], jnp.where(row == 1, ah[HALO - 1:HALO], pltpu.roll(a, 2, 0)))
    return a, a1, a2


def _post_conv_loss(bg, cg, u, qm, kv, z, h1, w_out, cw, gf, tgt):
    s, d = h1.shape
    c = CONV_WIDTH
    nb = c + MEM_WIDTH
    tm = ROW_TILE

    def body(bg_ref, cg_ref, u_ref, cgh_ref, uh_ref, qm_ref, kv_ref, z_ref, h_ref, w_ref, cw_ref, gf_ref, t_ref,
             y_ref, yt_ref, mo_ref, dh_ref, dhb_ref, loss_ref, dgf_ref):
        i = pl.program_id(0)
        a, a1, a2 = _conv_taps(cg_ref, u_ref, cgh_ref, uh_ref, i)
        conv = cw_ref[0:1, :] * a2 + cw_ref[1:2, :] * a1 + cw_ref[2:3, :] * a
        mix = bg_ref[...] * conv
        _mem_attn_into(qm_ref[...], kv_ref, mo_ref)
        sz, _ = _silu_parts(z_ref[...])
        y_ref[:, :c] = (mix * sz[:, :c]).astype(BF16)
        y_ref[:, c:] = (mo_ref[...] * sz[:, c:]).astype(BF16)
        y = y_ref[...]
        yt_ref[...] = y.T
        h2 = h_ref[...] + _dot(y, w_ref[...])
        r = lax.rsqrt(jnp.mean(h2 * h2, axis=-1, keepdims=True) + EPS)
        nh = h2 * r
        gfv = gf_ref[...]
        diff = nh * gfv - t_ref[...]
        dout = diff * (1.0 / d)
        dn = dout * gfv
        dh2 = r * dn - h2 * ((r * r * r) * jnp.mean(dn * h2, axis=-1, keepdims=True))
        dh_ref[...] = dh2
        dhb_ref[...] = dh2.astype(BF16)

        @pl.when(i == 0)
        def _():
            loss_ref[...] = jnp.zeros_like(loss_ref)
            dgf_ref[...] = jnp.zeros_like(dgf_ref)

        loss_ref[...] += 0.5 * jnp.sum(jnp.mean(diff * diff, axis=-1, keepdims=True))
        dgf_ref[...] += jnp.sum(dout * nh, axis=0, keepdims=True)

    return _pallas_call(
        body, name="post_conv_loss", grid=(s // tm,),
        out_shape=[_sds((s, nb), BF16), _sds((nb, s), BF16), _sds((s, MEM_WIDTH), F32), _sds((s, d), F32),
                   _sds((s, d), BF16), _plain((8, LANES), F32), _plain((1, d), F32)],
        in_specs=[_rows(c)] * 3 + [_halo_before(c)] * 2 + [_rows(MEM_WIDTH), _whole(kv.shape), _rows(nb), _rows(d),
                  _whole(w_out.shape), _whole(cw.shape), _whole((1, d)), _rows(d)],
        out_specs=[_rows(nb), pl.BlockSpec((nb, tm), lambda i: (0, i)), _rows(MEM_WIDTH), _rows(d), _rows(d),
                   _whole((8, LANES)), _whole((1, d))],
        compiler_params=_params(1, 48),
    )(bg, cg, u, cg, u, qm, kv, z, h1, w_out, cw, gf, tgt)


def _bwd_post_conv(dhb, w_out, bg, cg, u, z, qm, kv, mo, cw):
    s = dhb.shape[0]
    c = CONV_WIDTH
    nb = c + MEM_WIDTH

    def body(dh_ref, w_ref, bg_ref, cg_ref, u_ref, cgh_ref, uh_ref, z_ref, qm_ref, kv_ref, mo_ref, cw_ref,
             dz_ref, dbg_ref, dc_ref, dqm_ref, dkv_ref):
        i = pl.program_id(0)

        @pl.when(i == 0)
        def _():
            dkv_ref[...] = jnp.zeros_like(dkv_ref)

        dy = _dot_nt(dh_ref[...], w_ref[...])
        sz, dsz = _silu_parts(z_ref[...])
        a, a1, a2 = _conv_taps(cg_ref, u_ref, cgh_ref, uh_ref, i)
        conv = cw_ref[0:1, :] * a2 + cw_ref[1:2, :] * a1 + cw_ref[2:3, :] * a
        bgv = bg_ref[...]
        dz_ref[:, :c] = (dy[:, :c] * (bgv * conv) * dsz[:, :c]).astype(BF16)
        dz_ref[:, c:] = (dy[:, c:] * mo_ref[...] * dsz[:, c:]).astype(BF16)
        dbr = dy * sz
        dmix = dbr[:, :c]
        dbg_ref[...] = (dmix * conv).astype(BF16)
        dc_ref[...] = dmix * bgv
        _mem_attn_bwd(qm_ref[...], kv_ref, dbr[:, c:], dqm_ref, dkv_ref)

    return _pallas_call(
        body, name="bwd_post_conv", grid=(s // ROW_TILE,),
        out_shape=[_sds((s, nb), BF16), _sds((s, c), BF16), _sds((s, c), F32), _sds((s, MEM_WIDTH), BF16),
                   _plain(kv.shape, F32)],
        in_specs=[_rows(D_MODEL), _whole(w_out.shape)] + [_rows(c)] * 3 + [_halo_before(c)] * 2
                 + [_rows(nb), _rows(MEM_WIDTH), _whole(kv.shape), _rows(MEM_WIDTH), _whole(cw.shape)],
        out_specs=[_rows(nb), _rows(c), _rows(c), _rows(MEM_WIDTH), _whole(kv.shape)],
        compiler_params=_params(1, 48),
    )(dhb, w_out, bg, cg, u, cg, u, z, qm, kv, mo, cw)


def _bwd_conv(dconv, cg, u, cw):
    s, c = dconv.shape
    tm = ROW_TILE
    last = s // tm - 1

    def body(dc_ref, dcn_ref, cg_ref, u_ref, cgh_ref, uh_ref, cw_ref, dcg_ref, du_ref, dcw_ref):
        i = pl.program_id(0)

        @pl.when(i == 0)
        def _():
            dcw_ref[...] = jnp.zeros_like(dcw_ref)

        dc = dc_ref[...]
        dcn = jnp.where(i < last, dcn_ref[...], 0.0)
        row = lax.broadcasted_iota(jnp.int32, dc.shape, 0)
        d1 = jnp.where(row == tm - 1, dcn[0:1], pltpu.roll(dc, tm - 1, 0))
        d2 = jnp.where(row == tm - 1, dcn[1:2], jnp.where(row == tm - 2, dcn[0:1], pltpu.roll(dc, tm - 2, 0)))
        da = cw_ref[2:3, :] * dc + cw_ref[1:2, :] * d1 + cw_ref[0:1, :] * d2
        a, a1, a2 = _conv_taps(cg_ref, u_ref, cgh_ref, uh_ref, i)
        dcg_ref[...] = (da * u_ref[...]).astype(BF16)
        du_ref[...] = (da * cg_ref[...]).astype(BF16)
        dcw_ref[0:1, :] += jnp.sum(dc * a2, axis=0, keepdims=True)
        dcw_ref[1:2, :] += jnp.sum(dc * a1, axis=0, keepdims=True)
        dcw_ref[2:3, :] += jnp.sum(dc * a, axis=0, keepdims=True)

    return _pallas_call(
        body, name="bwd_conv", grid=(s // tm,),
        out_shape=[_sds((s, c), BF16), _sds((s, c), BF16), _plain((8, c), F32)],
        in_specs=[_rows(c), _halo_after(c, s), _rows(c), _rows(c), _halo_before(c), _halo_before(c), _whole(cw.shape)],
        out_specs=[_rows(c), _rows(c), _whole((8, c))], compiler_params=_params(1, 40),
    )(dconv, dconv, cg, u, cg, u, cw)


def _assemble(p_refs, pieces, widths, dp, scr):
    off = 0
    for p_ref, (_, d), wd in zip(p_refs, pieces, widths):
        if d == 1:
            dp[:, off:off + wd] = p_ref[...]
        else:
            dp[:, off:off + wd] = _from_view(scr, p_ref, d).astype(BF16)
        off += wd


def _dgrad_norm(pieces, wg, h, g, dres, token, onward, name):
    s, d_model = h.shape
    n = N_DEV * wg.shape[2]
    tm = ROW_TILE
    widths = [p.shape[1] // d for p, d in pieces]
    assert sum(widths) == n
    n_p = len(pieces)

    def body(_, *refs):
        p_refs = refs[:n_p]
        w_ref, h_ref, g_ref, dr_ref, dh_ref, dg_ref = refs[n_p:n_p + 6]
        dp, scr, wj = refs[-3:]
        _join_once(w_ref, wj)

        @pl.when(pl.program_id(0) == 0)
        def _():
            dg_ref[...] = jnp.zeros_like(dg_ref)

        _assemble(p_refs, pieces, widths, dp, scr)
        dhn = _dot_nt(dp[...], wj[...])
        hb = h_ref[...]
        r = lax.rsqrt(jnp.mean(hb * hb, axis=-1, keepdims=True) + EPS)
        dg_ref[...] += jnp.sum(dhn * (hb * r), axis=0, keepdims=True)
        dn = dhn * g_ref[...]
        dh = dr_ref[...] + r * dn - hb * ((r * r * r) * jnp.mean(dn * hb, axis=-1, keepdims=True))
        dh_ref[...] = dh
        if onward:
            dhb_ref, dpt_ref = refs[n_p + 6:n_p + 8]
            dhb_ref[...] = dh.astype(BF16)
            dpt_ref[...] = dp[...].T

    p_specs = [_view_rows(wd, d) for (_, d), wd in zip(pieces, widths)]
    out_shape = [_plain((s, d_model), F32), _plain((1, d_model), F32)]
    out_specs = [_rows(d_model), _whole((1, d_model))]
    if onward:
        out_shape += [_sds((s, d_model), BF16), _sds((n, s), BF16)]
        out_specs += [_rows(d_model), pl.BlockSpec((n, tm), lambda i: (0, i))]
    return _pallas_call(
        body, name=name, grid=(s // tm,), out_shape=out_shape,
        in_specs=[ANY] + p_specs + [_resident(wg.shape), _rows(d_model), _whole((1, d_model)), _rows(d_model)],
        out_specs=out_specs,
        scratch_shapes=[pltpu.VMEM((tm, n), BF16), _view_scratch(GROUP_WIDTH), pltpu.VMEM((d_model, n), BF16)],
        compiler_params=_params(1, 60),
    )(token, *[p for p, _ in pieces], wg, h, g, dres)


def _assemble_dproj_t(pieces, n, name):
    tm = ROW_TILE
    widths = [p.shape[1] // d for p, d in pieces]
    assert sum(widths) == n
    s = pieces[0][0].shape[0] * pieces[0][1]
    n_p = len(pieces)

    def body(*refs):
        p_refs, (dpt_ref, dp, scr) = refs[:n_p], refs[n_p:]
        _assemble(p_refs, pieces, widths, dp, scr)
        dpt_ref[...] = dp[...].T

    return _pallas_call(
        body, name=name, grid=(s // tm,), out_shape=_sds((n, s), BF16),
        in_specs=[_view_rows(wd, d) for (_, d), wd in zip(pieces, widths)],
        out_specs=pl.BlockSpec((n, tm), lambda i: (0, i)),
        scratch_shapes=[pltpu.VMEM((tm, n), BF16), _view_scratch(GROUP_WIDTH)],
        compiler_params=_params(1, 40),
    )(*[p for p, _ in pieces])


def _wgrad_shards_t(dp_t, h, c, name):
    n, s = dp_t.shape
    d_model = h.shape[1]
    per_step = 2

    def body(a_ref, b_ref, o_ref):
        o_ref[...] = _dot(a_ref[...], b_ref[...]).astype(BF16).reshape(per_step, c, d_model)

    return _pallas_call(
        body, name=name, grid=(N_DEV // per_step,), out_shape=_sds((N_DEV, c, d_model), BF16),
        in_specs=[pl.BlockSpec((per_step * c, s), lambda j: (j, 0)), _resident(h.shape)],
        out_specs=pl.BlockSpec((per_step, c, d_model), lambda j: (j, 0, 0)), compiler_params=_params(1, 40),
    )(dp_t, h)


def _wgrad_cols(a_t, b, c, name):
    m, s = a_t.shape
    assert c % LANES == 0

    def body(a_ref, b_ref, o_ref):
        wide = _dot(a_ref[...], b_ref[...]).astype(BF16)
        for j in range(WGRAD_SHARDS):
            o_ref[j] = wide[:, j * c:(j + 1) * c]

    return _pallas_call(
        body, name=name, grid=(N_DEV // WGRAD_SHARDS,), out_shape=_sds((N_DEV, m, c), BF16),
        in_specs=[_whole(a_t.shape), pl.BlockSpec((s, WGRAD_SHARDS * c), lambda j: (0, j))],
        out_specs=pl.BlockSpec((WGRAD_SHARDS, m, c), lambda j: (j, 0, 0)), compiler_params=_params(1, 40),
    )(a_t, b)


def _wgrad_rows(a_t, b, name):
    m, s = a_t.shape
    n = b.shape[1]
    mr = m // N_DEV

    def body(a_ref, b_ref, o_ref):
        o_ref[...] = _dot(a_ref[...], b_ref[...]).astype(BF16).reshape(WGRAD_SHARDS, mr, n)

    return _pallas_call(
        body, name=name, grid=(N_DEV // WGRAD_SHARDS,), out_shape=_sds((N_DEV, mr, n), BF16),
        in_specs=[pl.BlockSpec((WGRAD_SHARDS * mr, s), lambda j: (j, 0)), _whole(b.shape)],
        out_specs=pl.BlockSpec((WGRAD_SHARDS, mr, n), lambda j: (j, 0, 0)), compiler_params=_params(1, 40),
    )(a_t, b)


def _memkv_bwd(dkv, w, mem, g):
    n_layers = g.shape[0]
    rows = D_MODEL // N_DEV

    def body(dkv_ref, w_ref, mem_ref, g_ref, dw_ref, dg_ref):
        mb = mem_ref[...]
        r = lax.rsqrt(jnp.mean(mb * mb, axis=-1, keepdims=True) + EPS)
        nm = mb * r
        mn = (nm * g_ref[...]).astype(BF16)
        dkvb = dkv_ref[...].astype(BF16)
        dw_ref[...] = _dot_tn(mn, dkvb).astype(BF16).reshape(N_DEV, rows, 2 * MEM_WIDTH)
        dmn = _dot_nt(dkvb, w_ref[...].reshape(D_MODEL, 2 * MEM_WIDTH))
        dg_ref[...] = jnp.sum(dmn * nm, axis=0, keepdims=True)

    lay = lambda *shape: pl.BlockSpec((None,) + shape, lambda l: (l, 0, 0))
    major = pl.BlockSpec((N_DEV, rows, 2 * MEM_WIDTH), lambda l: (0, l, 0))
    return _pallas_call(
        body, name="memkv_bwd", grid=(n_layers,),
        out_shape=[_sds((N_DEV, n_layers * rows, 2 * MEM_WIDTH), BF16), _plain((n_layers, 1, D_MODEL), F32)],
        in_specs=[lay(N_MEM, 2 * MEM_WIDTH), major, _whole(mem.shape), lay(1, D_MODEL)],
        out_specs=[major, lay(1, D_MODEL)],
        compiler_params=_params(1, 32),
    )(dkv, w, mem, g.reshape(n_layers, 1, D_MODEL))


def _bwd_post_attn(dhb, wg_out, z, os_, ls_, qm, kv, mo, head_ones, token):
    s = dhb.shape[0]
    gw = GROUP_WIDTH
    nb = gw + MEM_WIDTH
    tm = ROW_TILE

    def body(_, dh_ref, w_ref, z_ref, o0, o1, o2, l0, l1, l2, qm_ref, kv_ref, mo_ref, bd_ref,
             dz_ref, do0, do1, do2, dl0, dl1, dl2, dqm_ref, dkv_ref, s0, s1):
        @pl.when(pl.program_id(0) == 0)
        def _():
            dkv_ref[...] = jnp.zeros_like(dkv_ref)

        dy = _dot_nt(dh_ref[...], _joined_columns(w_ref))
        ov, lv = [], []
        for o_ref, l_ref, d in zip((o0, o1, o2), (l0, l1, l2), DILATIONS):
            ov.append(_from_view(s0, o_ref, d))
            lv.append(_from_view(s1, l_ref, d))
        ws, mix = _mix_groups(ov, lv)
        sz, dsz = _silu_parts(z_ref[...])
        dz_ref[:, :gw] = (dy[:, :gw] * mix * dsz[:, :gw]).astype(BF16)
        dz_ref[:, gw:] = (dy[:, gw:] * mo_ref[...] * dsz[:, gw:]).astype(BF16)
        dbr = dy * sz
        dmix = dbr[:, :gw]
        t = dmix * mix
        th = t.astype(BF16)
        tl = (t - th.astype(F32)).astype(BF16)
        rs = _dot(th, bd_ref[...]) + _dot(tl, bd_ref[...])
        for wg_, do_ref, dl_ref, d in zip(ws, (do0, do1, do2), (dl0, dl1, dl2), DILATIONS):
            _to_view(s0, wg_ * dmix, do_ref, d)
            _to_view(s1, wg_ * rs, dl_ref, d)
        _mem_attn_bwd(qm_ref[...], kv_ref, dbr[:, gw:], dqm_ref, dkv_ref)

    vspecs = [_view_rows(gw, d) for d in DILATIONS]
    return _pallas_call(
        body, name="bwd_post_attn", grid=(s // tm,),
        out_shape=[_sds((s, nb), BF16)] + [_sds((s // d, d * gw), BF16) for d in DILATIONS]
                  + [_sds((s // d, d * gw), F32) for d in DILATIONS] + [_sds((s, MEM_WIDTH), BF16), _plain(kv.shape, F32)],
        in_specs=[ANY, _rows(D_MODEL), _whole(wg_out.shape), _rows(nb)] + vspecs * 2
                 + [_rows(MEM_WIDTH), _whole(kv.shape), _rows(MEM_WIDTH), _whole(head_ones.shape)],
        out_specs=[_rows(nb)] + vspecs * 2 + [_rows(MEM_WIDTH), _whole(kv.shape)],
        scratch_shapes=[_view_scratch(gw), _view_scratch(gw)],
        compiler_params=_params(1, 48),
    )(token, dhb, wg_out, z, *os_, *ls_, qm, kv, mo, head_ones)


def _attn_bwd(q, k, v, lse, do, dl, tabs, d, token):
    ln, dw = q.shape
    w = dw // d
    nb = ln // BLOCK
    reps = w // LANES
    two, before = _pair_specs(d, nb, w)
    two_t, _ = _pair_specs(d, nb, LANES)

    def attend(q_ref, l_ref, do_ref, dl_ref, dqs, acck, accv, rows, col0, kk, vv, acc_rows):
        valid = _band_mask(kk.shape[0])
        low = _low_head_lanes()
        pairs, qcols = _head_tiles(w, col0)
        cols = [slice(col0 + h * HEAD_DIM, col0 + h * HEAD_DIM + 1) for h in range(HEADS_PER_GROUP)]
        qhs = [h for qc in qcols for h in _split_pair(q_ref[rows, qc], low)]
        dobs = [h for qc in qcols for h in _split_pair(do_ref[rows, qc], low)]
        k2s = [kk[:, pr] for pr in pairs for _ in range(2)]
        v2s = [vv[:, pr] for pr in pairs for _ in range(2)]
        scs = [jnp.where(valid, _dot_nt(qh, k2), NEG) for qh, k2 in zip(qhs, k2s)]
        dps = [_dot_nt(dob, v2) for dob, v2 in zip(dobs, v2s)]
        ps = [jnp.exp(sc - l_ref[rows, col]) for sc, col in zip(scs, cols)]
        dss = [(p * (dp - dl_ref[rows, col])).astype(BF16) for p, dp, col in zip(ps, dps, cols)]
        pbs = [p.astype(BF16) for p in ps]
        for i, qc in enumerate(qcols):
            a, b = 2 * i, 2 * i + 1
            dqs[rows, qc] = jnp.where(low, _dot(dss[a], k2s[a]), _dot(dss[b], k2s[b])) * SCALE
            acck[acc_rows, qc] += _dot_tn(dss[a], qhs[a]) + _dot_tn(dss[b], qhs[b])
            accv[acc_rows, qc] += _dot_tn(pbs[a], dobs[a]) + _dot_tn(pbs[b], dobs[b])

    def body_streams(_, q_ref, kc_ref, vc_ref, l_ref, do_ref, dl_ref, c_ref, sa_ref, sb_ref,
                     dq_ref, dk_ref, dv_ref, acck, accv, dqs):
        acck[...] = jnp.zeros_like(acck)
        accv[...] = jnp.zeros_like(accv)
        for sb in range(2):
            cols = slice(sb * w, (sb + 1) * w)
            attend(q_ref, l_ref, do_ref, dl_ref, dqs, acck, accv, TOP, sb * w, kc_ref[:, cols], vc_ref[:, cols], TOP)
        tabs2 = [jnp.concatenate([jnp.tile(r[:, sb * LANES:(sb + 1) * LANES], (1, reps)) for sb in range(2)], axis=1)
                 for r in (c_ref, sa_ref, sb_ref)]
        dq_ref[...] = _rope_bwd(dqs[...], *tabs2).astype(BF16)
        dk_ref[...] = _rope_bwd(acck[...], *tabs2).astype(BF16)
        dv_ref[...] = accv[...].astype(BF16)

    def body_blocks(_, q_ref, kp_ref, kc_ref, vp_ref, vc_ref, l_ref, do_ref, dl_ref, cq, saq, sbq, ck, sak, sbk,
                    dq_ref, dk_ref, dv_ref, acck, accv, dqs):
        i = pl.program_id(0) % (nb // 2)

        @pl.when(i == 0)
        def _():
            acck[...] = jnp.zeros_like(acck)
            accv[...] = jnp.zeros_like(accv)

        refs = (q_ref, l_ref, do_ref, dl_ref, dqs, acck, accv)
        pl.when(i == 0)(lambda: attend(*refs, TOP, 0, kc_ref[TOP, :], vc_ref[TOP, :], TOP))
        pl.when(i != 0)(lambda: attend(
            *refs, TOP, 0, jnp.concatenate([kp_ref[...], kc_ref[TOP, :]], axis=0),
            jnp.concatenate([vp_ref[...], vc_ref[TOP, :]], axis=0),
            pl.ds(pl.multiple_of((2 * i - 1) * BLOCK, BLOCK), 2 * BLOCK)))
        attend(*refs, BOTTOM, 0, kc_ref[...], vc_ref[...], pl.ds(pl.multiple_of(2 * i * BLOCK, BLOCK), 2 * BLOCK))
        tq = [jnp.tile(r[...], (1, reps)) for r in (cq, saq, sbq)]
        dq_ref[...] = _rope_bwd(dqs[...], *tq).astype(BF16)

        @pl.when(i == nb // 2 - 1)
        def _():
            for r0 in range(0, nb * BLOCK, 2 * BLOCK):
                rows = slice(r0, r0 + 2 * BLOCK)
                tk = [jnp.tile(r[rows, :], (1, reps)) for r in (ck, sak, sbk)]
                dk_ref[rows, :] = _rope_bwd(acck[rows, :], *tk).astype(BF16)
                dv_ref[rows, :] = accv[rows, :].astype(BF16)

    if nb == 1:
        body = body_streams
        in_specs = [ANY] + [two] * 6 + [two_t] * 3
        args = (token, q, k, v, lse, do, dl, *tabs)
        out_specs = [two, two, two]
        acc_shape = (BLOCK, 2 * w)
    else:
        body = body_blocks
        stream = pl.BlockSpec((nb * BLOCK, w), lambda n: (0, n // (nb // 2)))
        stream_t = pl.BlockSpec((nb * BLOCK, LANES), lambda n: (0, n // (nb // 2)))
        in_specs = [ANY, two, before, two, before, two, two, two, two] + [two_t] * 3 + [stream_t] * 3
        args = (token, q, k, k, v, v, lse, do, dl, *tabs, *tabs)
        out_specs = [two, stream, stream]
        acc_shape = (nb * BLOCK, w)
    return _pallas_call(
        body, name=f"attn_bwd_d{d}", grid=(d * nb // 2,), out_shape=[_sds((ln, dw), BF16)] * 3,
        in_specs=in_specs, out_specs=out_specs,
        scratch_shapes=[pltpu.VMEM(acc_shape, F32), pltpu.VMEM(acc_shape, F32), pltpu.VMEM(two.block_shape, F32)],
        compiler_params=_params(1, 48),
    )(*args)


def _position():
    return lax.axis_index("x"), lax.axis_index("y"), lax.axis_index("c")


def _all_gather(shards, afters, name):
    n_a, n_in = len(shards), len(shards) + len(afters)

    def body(*refs):
        x_refs, out_refs = refs[:n_a], refs[n_in:n_in + n_a]
        send_sems, recv_sems, local_sems = refs[n_in + n_a:]
        x, y, c = _position()
        me, sibling = (x, y, c), (x, y, 1 - c)
        chips = [(1 - x, y), (x, 1 - y), (1 - x, 1 - y)]

        def rows(a, px, py, pc):
            return out_refs[a].at[4 * px + 2 * py + pc]

        def copy(a, k, block, to, own=False):
            return pltpu.make_async_remote_copy(
                src_ref=x_refs[a] if own else rows(a, *block), dst_ref=rows(a, *block),
                send_sem=send_sems.at[a, k], recv_sem=recv_sems.at[a, k], device_id=to, device_id_type=MESH)

        mine = [pltpu.make_async_copy(x_refs[a], rows(a, *me), local_sems.at[a]) for a in range(n_a)]
        for cp in mine:
            cp.start()
        first = []
        for j, chip in enumerate(chips):
            first += [copy(a, 1 + j, me, (*chip, c), own=True) for a in range(n_a)]
        first += [copy(a, 0, me, sibling, own=True) for a in range(n_a)]
        for cp in first:
            cp.start()
        passed = []
        for j, chip in enumerate(chips):
            for a in range(n_a):
                copy(a, 1 + j, (*chip, c), me).wait_recv()
                fwd = copy(a, 4 + j, (*chip, c), sibling)
                fwd.start()
                passed.append(fwd)
        for a in range(n_a):
            copy(a, 0, sibling, me).wait_recv()
        for j, chip in enumerate(chips):
            for a in range(n_a):
                copy(a, 4 + j, (*chip, 1 - c), me).wait_recv()
        for cp in first + passed:
            cp.wait_send()
        for cp in mine:
            cp.wait()

    return _pallas_call(
        body, name=name, out_shape=[_sds((N_DEV,) + t.shape, t.dtype) for t in shards],
        in_specs=[ANY] * n_in, out_specs=[ANY] * n_a,
        scratch_shapes=[pltpu.SemaphoreType.DMA((n_a, 7)), pltpu.SemaphoreType.DMA((n_a, 7)),
                        pltpu.SemaphoreType.DMA((n_a,))],
    )(*shards, *afters)


HBM_SPEC = pl.BlockSpec(memory_space=pltpu.HBM)
SEM_SPEC = pl.BlockSpec(memory_space=pltpu.SEMAPHORE)
EFFECT = pltpu.SideEffectType.DATAFLOW_SIDE_EFFECTING


def _relay_copies(x_ref, out_ref, send_sems, recv_sems):
    x, y, c = _position()
    me, sibling = (x, y, c), (x, y, 1 - c)
    xn, yn, diag = (1 - x, y, c), (x, 1 - y, c), (1 - x, 1 - y, c)
    src_nb = (x + c * (1 - 2 * x), y + (1 - c) * (1 - 2 * y), c)
    dst_nb = (x + (1 - c) * (1 - 2 * x), y + c * (1 - 2 * y), c)

    def rows(dev):
        return out_ref.at[4 * dev[0] + 2 * dev[1] + dev[2]]

    def copy(k, block, to, own=False):
        return pltpu.make_async_remote_copy(
            src_ref=x_ref if own else rows(block), dst_ref=rows(block),
            send_sem=send_sems.at[k], recv_sem=recv_sems.at[k], device_id=to, device_id_type=MESH)

    return copy, rows, (x, y, c), (me, sibling, xn, yn, diag, src_nb, dst_nb)


def _relay_first_leg(x_ref, out_ref, send_sems, recv_sems, local_sem):
    copy, rows, _, (me, sibling, xn, yn, _, _, _) = _relay_copies(x_ref, out_ref, send_sems, recv_sems)
    mine = pltpu.make_async_copy(x_ref, rows(me), local_sem.at[0])
    return mine, [copy(1, me, xn, own=True), copy(2, me, yn, own=True), copy(0, me, sibling, own=True)]


def _gather_relay_start(xs, name):
    def body(x_ref, _, out_ref, send_sems, recv_sems, local_sem):
        mine, first = _relay_first_leg(x_ref, out_ref, send_sems, recv_sems, local_sem)
        mine.start()
        for cp in first:
            cp.start()

    return pl.pallas_call(
        body, name=name,
        out_shape=(pltpu.HBM(xs.shape, xs.dtype), pltpu.HBM((N_DEV,) + xs.shape, xs.dtype),
                   pltpu.SemaphoreType.DMA((7,)), pltpu.SemaphoreType.DMA((7,)), pltpu.SemaphoreType.DMA((1,))),
        in_specs=[HBM_SPEC], out_specs=(HBM_SPEC, HBM_SPEC, SEM_SPEC, SEM_SPEC, SEM_SPEC),
        input_output_aliases={0: 0},
        compiler_params=pltpu.CompilerParams(has_side_effects=EFFECT),
    )(pltpu.with_memory_space_constraint(xs, pltpu.HBM))


def _gather_relay_landed(started, afters, name):
    xs, out, send_sems, recv_sems, local_sem = started

    def body(x_ref, out_ref, s_sems, r_sems, l_sem, *_):
        mine, first = _relay_first_leg(x_ref, out_ref, s_sems, r_sems, l_sem)
        for cp in first:
            cp.wait_recv()
            cp.wait_send()
        mine.wait()

    return pl.pallas_call(
        body, name=name, out_shape=pltpu.HBM(out.shape, out.dtype),
        in_specs=[HBM_SPEC, HBM_SPEC, SEM_SPEC, SEM_SPEC, SEM_SPEC] + [ANY] * len(afters), out_specs=HBM_SPEC,
        input_output_aliases={1: 0},
        compiler_params=pltpu.CompilerParams(has_side_effects=EFFECT),
    )(xs, out, send_sems, recv_sems, local_sem, *afters)


def _gather_relay_rest(out, name):
    def body(out_ref, _, send_sems, recv_sems):
        copy, _, (x, y, c), (me, sibling, xn, yn, diag, src_nb, dst_nb) = _relay_copies(
            None, out_ref, send_sems, recv_sems)
        sent = [copy(3, src_nb, dst_nb), copy(4, xn, sibling), copy(5, yn, sibling)]
        for cp in sent:
            cp.start()
        copy(3, diag, me).wait_recv()
        last = copy(6, diag, sibling)
        last.start()
        for k, blk in ((4, (1 - x, y, 1 - c)), (5, (x, 1 - y, 1 - c)), (6, (1 - x, 1 - y, 1 - c))):
            copy(k, blk, me).wait_recv()
        for cp in sent + [last]:
            cp.wait_send()

    return pl.pallas_call(
        body, name=name, out_shape=pltpu.HBM(out.shape, out.dtype),
        in_specs=[HBM_SPEC], out_specs=HBM_SPEC, input_output_aliases={0: 0},
        scratch_shapes=[pltpu.SemaphoreType.DMA((7,)), pltpu.SemaphoreType.DMA((7,))],
        compiler_params=pltpu.CompilerParams(has_side_effects=EFFECT),
    )(out)


def _plan_gather_own(src_refs, land_refs):
    x, y, c = _position()
    me = 4 * x + 2 * y + c
    peers = [(x, y, 1 - c), (1 - x, y, c), (x, 1 - y, c), (1 - x, 1 - y, c)]
    return [(src_refs[a], land_refs[a].at[me], (a, k), peer) for k, peer in enumerate(peers) for a in range(len(src_refs))]


def _plan_gather_pass(src_refs, land_refs):
    x, y, c = _position()
    chips = [(1 - x, y), (x, 1 - y), (1 - x, 1 - y)]
    return [(land_refs[a].at[4 * px + 2 * py + c], land_refs[a].at[4 * px + 2 * py + c], (a, j), (x, y, 1 - c))
            for j, (px, py) in enumerate(chips) for a in range(len(land_refs))]


def _plan_to_sibling(src_refs, land_refs):
    x, y, c = _position()
    return [(src_refs[a].at[2 * k + (1 - c)], land_refs[a].at[k], (a, k), (x, y, 1 - c))
            for k in range(4) for a in range(len(src_refs))]


def _plan_to_chips(src_refs, land_refs):
    x, y, c = _position()
    chips = [(1 - x, y), (x, 1 - y), (1 - x, 1 - y)]
    return [(src_refs[a].at[2 * px + py], land_refs[a].at[j], (a, j), (px, py, c))
            for j, (px, py) in enumerate(chips) for a in range(len(src_refs))]


def _split_start(srcs, lands, plan, n_sem, after, name):
    n_s, n_a = len(srcs), len(lands)
    n_b = n_s + n_a

    def body(*refs):
        src_refs, land_refs = refs[:n_s], refs[n_s:n_b]
        send_sems, recv_sems, token = refs[n_b + 1], refs[n_b + 2], refs[-1]
        for src, dst, (a, k), dev in plan(src_refs, land_refs):
            i = a * n_sem + k
            pltpu.make_async_remote_copy(src_ref=src, dst_ref=dst, send_sem=send_sems.at[i], recv_sem=recv_sems.at[i],
                                         device_id=dev, device_id_type=MESH).start()
        token[...] = jnp.zeros_like(token)

    bufs = list(srcs) + list(lands)
    res = pl.pallas_call(
        body, name=name,
        out_shape=(pltpu.SemaphoreType.DMA((n_a * n_sem,)), pltpu.SemaphoreType.DMA((n_a * n_sem,)),
                   *[pltpu.HBM(t.shape, t.dtype) for t in bufs], _plain((8, LANES), F32)),
        in_specs=[HBM_SPEC] * n_b + [ANY],
        out_specs=(SEM_SPEC, SEM_SPEC, *[HBM_SPEC] * n_b, pl.BlockSpec(memory_space=pltpu.VMEM)),
        input_output_aliases={i: 2 + i for i in range(n_b)},
        compiler_params=pltpu.CompilerParams(has_side_effects=EFFECT),
    )(*[pltpu.with_memory_space_constraint(t, pltpu.HBM) for t in bufs], after)
    return (res[0], res[1], res[2:2 + n_s], res[2 + n_s:2 + n_b]), res[-1]


def _split_wait(started, plan, after, name):
    send_sems, recv_sems, srcs, lands = started
    n_s, n_a = len(srcs), len(lands)
    n_b = n_s + n_a
    n_sem = send_sems.shape[0] // n_a

    def body(*refs):
        src_refs, land_refs = refs[:n_s], refs[n_s:n_b]
        s_sems, r_sems = refs[n_b], refs[n_b + 1]
        for src, dst, (a, k), dev in plan(src_refs, land_refs):
            i = a * n_sem + k
            cp = pltpu.make_async_remote_copy(src_ref=src, dst_ref=dst, send_sem=s_sems.at[i], recv_sem=r_sems.at[i],
                                              device_id=dev, device_id_type=MESH)
            cp.wait_send()
            cp.wait_recv()

    bufs = list(srcs) + list(lands)
    res = pl.pallas_call(
        body, name=name, out_shape=tuple(pltpu.HBM(t.shape, t.dtype) for t in bufs),
        in_specs=[HBM_SPEC] * n_b + [SEM_SPEC, SEM_SPEC, ANY],
        out_specs=tuple([HBM_SPEC] * n_b),
        input_output_aliases={i: i for i in range(n_b)},
        compiler_params=pltpu.CompilerParams(has_side_effects=EFFECT),
    )(*bufs, send_sems, recv_sems, after)
    return res[:n_s], res[n_s:]


SUBLANES = 8


def _row_tile(r):
    return max(t for t in range(SUBLANES, ROW_TILE + 1, SUBLANES) if r % t == 0)


def _rs_add_sibling(gp, recv, ck_arr, name):
    _, r, l = gp.shape
    tr = r if r <= 4 * ROW_TILE else _row_tile(r)
    block = lambda k, ck: (k + ck[1] + 1) % 4

    def body(ck_ref, g_ref, r_ref, pf_ref, pb_ref):
        sm = g_ref[...].astype(F32) + r_ref[...].astype(F32)
        pf_ref[...] = sm
        pb_ref[...] = sm.astype(BF16)

    spec = pl.BlockSpec((None, tr, l), lambda i, k, ck: (block(k, ck), i, 0))
    return _pallas_call(
        body, name=name,
        grid_spec=pltpu.PrefetchScalarGridSpec(
            num_scalar_prefetch=1, grid=(r // tr, 4),
            in_specs=[pl.BlockSpec((None, tr, l), lambda i, k, ck: (2 * block(k, ck) + ck[0], i, 0)), spec],
            out_specs=[pl.BlockSpec((tr, l), lambda i, k, ck: (i, 0)), spec]),
        out_shape=[_sds((r, l), F32), _sds((4, r, l), BF16)], compiler_params=_params(2, 32),
    )(ck_arr, gp, recv)


def _adam_update(w, gv, m, v):
    nm = ADAM_B1 * m + (1.0 - ADAM_B1) * gv
    nv = ADAM_B2 * v + (1.0 - ADAM_B2) * (gv * gv)
    m_hat = nm / (1.0 - ADAM_B1 ** ADAM_STEP)
    v_hat = nv / (1.0 - ADAM_B2 ** ADAM_STEP)
    return -ADAM_LR * (m_hat / (jnp.sqrt(v_hat) + ADAM_EPS) + ADAM_WD * w), nm, nv


def _rs_finish_adamw(pf, recv, w, m, v, after, name):
    r, l = pf.shape
    tr = _row_tile(r)

    def body(_, p_ref, r_ref, w_ref, m_ref, v_ref, g_ref, d_ref, nm_ref, nv_ref):
        gv = ((p_ref[...] + r_ref[0].astype(F32)) + r_ref[1].astype(F32)) + r_ref[2].astype(F32)
        g_ref[...] = gv
        d_ref[...], nm_ref[...], nv_ref[...] = _adam_update(w_ref[...], gv, m_ref[...], v_ref[...])

    spec = pl.BlockSpec((tr, l), lambda i: (i, 0))
    return _pallas_call(
        body, name=name, grid=(r // tr,),
        in_specs=[ANY, spec, pl.BlockSpec((3, tr, l), lambda i: (0, i, 0)), spec, spec, spec], out_specs=[spec] * 4,
        out_shape=[_plain((r, l), F32)] * 4, compiler_params=_params(1, 32),
    )(after, pf, recv, w, m, v)


SMALL_ROWS = dict(norm_g=(0, 2), mem_norm_g=(2, 4), final_g=(4, 5), conv_w=(5, 8))
LOSS_ROWS = (8, 16)


def _sum_adamw_small(g, ck_arr, states):
    names = list(SMALL_ROWS)
    n_dev, n_rows, _ = g.shape

    def body(ck_ref, g_ref, gc_ref, *refs):
        ins, loss_ref, outs = refs[:3 * len(names)], refs[3 * len(names)], refs[3 * len(names) + 1:]

        def total(ref, lo, hi):
            acc = ref[0, lo:hi, :]
            for j in range(1, n_dev):
                acc = acc + ref[j, lo:hi, :]
            return acc

        loss_ref[...] = total(gc_ref, *LOSS_ROWS)
        for i, n in enumerate(names):
            gv = total(gc_ref if n == "conv_w" else g_ref, *SMALL_ROWS[n])
            w_ref, m_ref, v_ref = ins[3 * i:3 * i + 3]
            g_out, d_out, nm_out, nv_out = outs[4 * i:4 * i + 4]
            g_out[...] = gv
            d_out[...], nm_out[...], nv_out[...] = _adam_update(w_ref[...], gv, m_ref[...], v_ref[...])

    flat = [t for n in names for t in states[n]]
    mine = pl.BlockSpec((n_dev, n_rows, LANES), lambda i, ck: (0, 0, 2 * ck[1] + ck[0]))
    res = _pallas_call(
        body, name="sum_adamw_small",
        grid_spec=pltpu.PrefetchScalarGridSpec(
            num_scalar_prefetch=1, grid=(1,),
            in_specs=[_whole(g.shape), mine] + [_whole(t.shape) for t in flat],
            out_specs=[_whole((SUBLANES, LANES))] + [_whole(states[n][0].shape) for n in names for _ in range(4)]),
        out_shape=[_plain((SUBLANES, LANES), F32)] + [_plain(states[n][0].shape, F32) for n in names for _ in range(4)],
        compiler_params=_params(1, 32),
    )(ck_arr, g, g, *flat)
    return res[0], {n: tuple(res[1 + 4 * i:5 + 4 * i]) for i, n in enumerate(names)}


def _finish(name, pf, recv, w, m, v, after):
    if name in ("attn_w_in", "conv_w_in"):
        res = _rs_finish_adamw(pf, recv, w.T, m.T, v.T, after, "rs_finish_adamw_" + name)
        return tuple(t.T for t in res)
    return _rs_finish_adamw(pf, recv, w, m, v, after, "rs_finish_adamw_" + name)


def kernel(x, mem, positions, norm_g, mem_norm_g, w_mem_kv, attn_w_in, attn_w_out, conv_w_in, conv_w, conv_w_out, final_g, loss_target, m_norm_g, m_mem_norm_g, m_w_mem_kv, m_attn_w_in, m_attn_w_out, m_conv_w_in, m_conv_w, m_conv_w_out, m_final_g, v_norm_g, v_mem_norm_g, v_w_mem_kv, v_attn_w_in, v_attn_w_out, v_conv_w_in, v_conv_w, v_conv_w_out, v_final_g):
    px, py, pc = _position()
    me = 4 * px + 2 * py + pc
    ck_arr = jnp.stack([pc, 2 * px + py]).astype(jnp.int32)
    x, mem, pos, tgt = x[0], mem[0], positions[0], loss_target[0]

    def landing(shards):
        return [lax.dynamic_update_slice(lax.empty((N_DEV,) + t.shape, t.dtype), t[None], (me, 0, 0)) for t in shards]

    def gather_pass(weights, after, name):
        _, lands = _split_wait(weights, _plan_gather_own, after, name + "_wait")
        return _split_start([], lands, _plan_gather_pass, 3, after, name + "_pass_start")

    w_in0_started = _gather_relay_start(attn_w_in[0].astype(BF16), "gather_w_in0_start")
    rest0 = [attn_w_out[0].astype(BF16), w_mem_kv.astype(BF16).reshape(-1, w_mem_kv.shape[2]),
             jnp.pad(conv_w[0], ((0, 5), (0, 0)))]
    conv_ws = [conv_w_in[0].astype(BF16), conv_w_out[0].astype(BF16)]
    rest0_lands, conv_lands = landing(rest0), landing(conv_ws)
    tabs = _rope_tables(pos)
    g0, g1 = norm_g[0:1], norm_g[1:2]
    wg_in0 = _gather_relay_landed(w_in0_started, [*rest0, *conv_ws, *rest0_lands, *conv_lands, *tabs, g0],
                                  "gather_w_in0_landed")
    wg_in0 = _gather_relay_rest(wg_in0, "gather_w_in0_rest")

    rest0, token = _split_start(rest0, rest0_lands, _plan_gather_own, 4, wg_in0, "gather_rest_start")
    conv_ws, token = _split_start(conv_ws, conv_lands, _plan_gather_own, 4, token, "gather_conv_start")

    hn0, qs, ks, vs, tabs_v, qm0, z0 = _inproj_attn(x, g0, wg_in0, tabs, token)
    os_, ls_ = [], []
    for j, d in enumerate(DILATIONS):
        if j == 2:
            rest0, token = gather_pass(rest0, ls_[1], "gather_rest")
        o, l = _attn_fwd(qs[j], ks[j], vs[j], d, token)
        os_.append(o)
        ls_.append(l)

    conv_ws, token = gather_pass(conv_ws, ls_[2], "gather_conv")
    _, (wg_out0, wg_kv, cw_all) = _split_wait(rest0, _plan_gather_pass, token, "gather_rest_pass_wait")
    cw = cw_all[:, 0:3].transpose(1, 0, 2).reshape(3, -1)
    kv = _memkv_fwd(mem, mem_norm_g, wg_kv)
    y0, y0_t, mo0, h1 = _post_attn(os_, ls_, qm0, kv[0], z0, x, wg_out0)
    _, (wg_in1, wg_out1) = _split_wait(conv_ws, _plan_gather_pass, h1, "gather_conv_pass_wait")
    w_out1 = wg_out1.reshape(-1, wg_out1.shape[2])

    hn1, bg, cg, u, qm1, z1 = _inproj_conv(h1, g1, wg_in1)
    y1, y1_t, mo1, dh2, dh2b, loss_acc, d_final_g = _post_conv_loss(
        bg, cg, u, qm1, kv[1], z1, h1, w_out1, cw, final_g.reshape(1, -1), tgt)

    d_w_out1 = _wgrad_rows(y1_t, dh2b, "wgrad_out1")
    dz1, dbg, dconv, dqm1, dkv1 = _bwd_post_conv(dh2b, w_out1, bg, cg, u, z1, qm1, kv[1], mo1, cw)
    dcg, du, dcw = _bwd_conv(dconv, cg, u, cw)
    dh1, dg1, dh1b, dproj1_t = _dgrad_norm([(dbg, 1), (dcg, 1), (du, 1), (dqm1, 1), (dz1, 1)], wg_in1, h1, g1, dh2,
                                           dh2, True, "dgrad_norm_conv")
    d_w_in1 = _wgrad_shards_t(dproj1_t, hn1, wg_in1.shape[2], "wgrad_in1")

    d_w_out0 = _wgrad_cols(y0_t, dh1b, wg_out0.shape[2], "wgrad_out0")

    gw = GROUP_WIDTH
    ones = (jnp.arange(gw)[:, None] // HEAD_DIM == jnp.arange(gw)[None, :] // HEAD_DIM).astype(BF16)
    res = _bwd_post_attn(dh1b, wg_out0, z0, os_, ls_, qm0, kv[0], mo0, ones, dg1)
    dz0, dos, dls, dqm0, dkv0 = res[0], res[1:4], res[4:7], res[7], res[8]
    d_w_kv, d_mem_g = _memkv_bwd(jnp.stack([dkv0, dkv1]), wg_kv, mem, mem_norm_g)

    names1 = ["conv_w_in", "conv_w_out", "attn_w_out", "w_mem_kv"]
    grads1 = [d_w_in1, d_w_out1, d_w_out0, d_w_kv]
    started, token = _split_start(grads1, [lax.empty((4,) + g.shape[1:], g.dtype) for g in grads1],
                                  _plan_to_sibling, 4, d_mem_g, "rs1_sibling_start")

    dqs, dks, dvs = [], [], []
    for j, d in enumerate(DILATIONS):
        if j == 1:
            grads1, from_sibling = _split_wait(started, _plan_to_sibling, dqs[0][0], "rs1_sibling_wait")
            parts1 = [_rs_add_sibling(g, r, ck_arr, "rs_add_sibling_" + n)
                      for g, r, n in zip(grads1, from_sibling, names1)]
            pbs1 = [pb for _, pb in parts1]
            started, token = _split_start(pbs1, [lax.empty((3,) + p.shape[1:], p.dtype) for p in pbs1],
                                          _plan_to_chips, 3, dg1, "rs1_chips_start")
        dq, dk, dv = _attn_bwd(qs[j], ks[j], vs[j], ls_[j], dos[j], dls[j], tabs_v[j], d, token)
        dqs.append((dq, d))
        dks.append((dk, d))
        dvs.append((dv, d))
    pieces0 = dqs + dks + dvs + [(dqm0, 1), (dz0, 1)]
    dproj0_t = _assemble_dproj_t(pieces0, N_DEV * wg_in0.shape[2], "assemble_dproj_attn")
    d_w_in0 = _wgrad_shards_t(dproj0_t, hn0, wg_in0.shape[2], "wgrad_in0")

    names0 = ["attn_w_in"]
    grads0 = [d_w_in0]
    started0, token0 = _split_start(grads0, [lax.empty((4,) + g.shape[1:], g.dtype) for g in grads0],
                                    _plan_to_sibling, 4, dg1, "rs0_sibling_start")
    _, from_chips1 = _split_wait(started, _plan_to_chips, token0, "rs1_chips_wait")
    shard = dict(attn_w_in=(attn_w_in[0], m_attn_w_in[0], v_attn_w_in[0]),
                 attn_w_out=(attn_w_out[0], m_attn_w_out[0], v_attn_w_out[0]),
                 conv_w_in=(conv_w_in[0], m_conv_w_in[0], v_conv_w_in[0]),
                 conv_w_out=(conv_w_out[0], m_conv_w_out[0], v_conv_w_out[0]),
                 w_mem_kv=tuple(t.reshape(-1, t.shape[2]) for t in (w_mem_kv, m_w_mem_kv, v_w_mem_kv)))
    finish1 = {n: (pf, r) for n, (pf, _), r in zip(names1, parts1, from_chips1)}
    big = {"conv_w_in": _finish("conv_w_in", *finish1["conv_w_in"], *shard["conv_w_in"], token0)}

    grads0, from_sibling = _split_wait(started0, _plan_to_sibling, big["conv_w_in"][1].T, "rs0_sibling_wait")
    parts0 = [_rs_add_sibling(g, r, ck_arr, "rs_add_sibling_" + n) for g, r, n in zip(grads0, from_sibling, names0)]
    pbs0 = [pb for _, pb in parts0]
    started0, token0 = _split_start(pbs0, [lax.empty((3,) + p.shape[1:], p.dtype) for p in pbs0],
                                    _plan_to_chips, 3, dg1, "rs0_chips_start")
    dx, dg0 = _dgrad_norm(pieces0, wg_in0, x, g0, dh1, token0, False, "dgrad_norm_attn")
    for n in names1[1:]:
        big[n] = _finish(n, *finish1[n], *shard[n], token0)

    small_part = jnp.concatenate([dg0, dg1, d_mem_g.reshape(2, -1), d_final_g, dcw[0:3]], axis=0)
    small_part = jnp.concatenate([small_part, jnp.broadcast_to(loss_acc[0, 0], small_part.shape)], axis=0)
    small_w = dict(norm_g=(norm_g, m_norm_g, v_norm_g), mem_norm_g=(mem_norm_g, m_mem_norm_g, v_mem_norm_g),
                   conv_w=(conv_w, m_conv_w, v_conv_w), final_g=(final_g, m_final_g, v_final_g))
    loss_tile, small_res = _sum_adamw_small(
        _all_gather([small_part], [big[n][1] for n in names1[1:]], "gather_small_grads")[0], ck_arr,
        {n: tuple(t.reshape(-1, t.shape[-1]) for t in wmv) for n, wmv in small_w.items()})
    loss = loss_tile[0, 0]
    for n, wmv in small_w.items():
        big[n] = tuple(t.reshape(wmv[0].shape) for t in small_res[n])

    _, from_chips0 = _split_wait(started0, _plan_to_chips, big["final_g"][1], "rs0_chips_wait")
    for n, (pf, _), r in zip(names0, parts0, from_chips0):
        big[n] = _finish(n, pf, r, *shard[n], big["final_g"][1])
    for n in ("attn_w_in", "attn_w_out", "conv_w_in", "conv_w_out"):
        big[n] = tuple(t[None] for t in big[n])
    big["w_mem_kv"] = tuple(t.reshape(w_mem_kv.shape) for t in big["w_mem_kv"])

    order = ["norm_g", "mem_norm_g", "w_mem_kv", "attn_w_in", "attn_w_out", "conv_w_in", "conv_w", "conv_w_out", "final_g"]
    return (loss, dx[None], *[big[n][0] for n in order], *[big[n][1] for n in order],
            *[big[n][2] for n in order], *[big[n][3] for n in order])
```

```python
import jax
import jax.numpy as jnp
from jax import lax
from jax.experimental import pallas as pl
from jax.experimental.pallas import tpu as pltpu

F32 = jnp.float32
BF16 = jnp.bfloat16

N_DEV = 8
D_MODEL = 1024
HEAD_DIM = 64
ROT_DIM = HEAD_DIM // 4
ROPE_THETA = 500000.0
DILATIONS = (1, 4, 16)
HEADS_PER_GROUP = 8
GROUP_WIDTH = HEADS_PER_GROUP * HEAD_DIM
BLOCK = 128
N_MEM = 256
MEM_HEADS = 4
MEM_WIDTH = MEM_HEADS * HEAD_DIM
CONV_WIDTH = D_MODEL
EPS = 1e-6
SCALE = HEAD_DIM ** -0.5
NEG = -1e30

ADAM_LR = 0.001
ADAM_B1 = 0.9
ADAM_B2 = 0.999
ADAM_EPS = 1e-08
ADAM_WD = 0.01
ADAM_STEP = 10

ROW_TILE = 256
WGRAD_SHARDS = 4
LANES = 128
MESH = pl.DeviceIdType.MESH
ANY = pl.BlockSpec(memory_space=pl.ANY)


def _pallas_call(body, **kw):
    call = pl.pallas_call(body, **kw)

    def run(*args):
        pinned = [pltpu.with_memory_space_constraint(a, pltpu.HBM) if jnp.issubdtype(a.dtype, jnp.floating) else a
                  for a in args]
        return call(*pinned)

    return run


def _dot(a, b):
    return lax.dot_general(a, b, (((1,), (0,)), ((), ())), preferred_element_type=F32)


def _dot_nt(a, b):
    return lax.dot_general(a, b, (((1,), (1,)), ((), ())), preferred_element_type=F32)


def _dot_tn(a, b):
    return lax.dot_general(a, b, (((0,), (0,)), ((), ())), preferred_element_type=F32)


def _params(n_grid, vmem_mb=48):
    return pltpu.CompilerParams(dimension_semantics=("arbitrary",) * n_grid, vmem_limit_bytes=vmem_mb << 20)


def _rows(width, tm=ROW_TILE):
    return pl.BlockSpec((tm, width), lambda i: (i, 0))


def _view_rows(width, d, tm=ROW_TILE):
    return pl.BlockSpec((tm // d, d * width), lambda i: (i, 0))


def _whole(shape):
    return pl.BlockSpec(shape, lambda *_: (0,) * len(shape))


def _resident(shape):
    return pl.BlockSpec(shape, lambda *_: (0,) * len(shape), pipeline_mode=pl.Buffered(1))


def _sds(shape, dtype):
    return pltpu.HBM(shape, dtype)


def _plain(shape, dtype):
    return jax.ShapeDtypeStruct(shape, dtype)


def _silu_parts(z):
    sg = jax.nn.sigmoid(z)
    return z * sg, sg * (1.0 + z * (1.0 - sg))


def _to_view(scr, val, out_ref, d):
    tm, w = val.shape
    if d == 1:
        out_ref[...] = val.astype(out_ref.dtype)
        return
    for cb in range(w // LANES):
        scr[cb] = val[:, cb * LANES:(cb + 1) * LANES]
    for r in range(d):
        for cb in range(w // LANES):
            lo = r * w + cb * LANES
            out_ref[:, lo:lo + LANES] = scr[cb, pl.ds(r, tm // d, stride=d), :].astype(out_ref.dtype)


def _from_view(scr, in_ref, d):
    if d == 1:
        return in_ref[...].astype(F32)
    nc, tm, _ = scr.shape
    w = nc * LANES
    for r in range(d):
        for cb in range(nc):
            lo = r * w + cb * LANES
            scr[cb, pl.ds(r, tm // d, stride=d), :] = in_ref[:, lo:lo + LANES].astype(F32)
    return jnp.concatenate([scr[cb] for cb in range(nc)], axis=1)


def _view_scratch(width, tm=ROW_TILE):
    return pltpu.VMEM((width // LANES, tm, LANES), F32)


def _rope_tables(pos):
    half = ROT_DIM // 2
    inv_freq = ROPE_THETA ** (-jnp.arange(half, dtype=F32) * (2.0 / ROT_DIM))
    ang = pos.astype(F32)[:, None] * inv_freq
    cos, sin = jnp.cos(ang), jnp.sin(ang)
    s = pos.shape[0]
    z8 = jnp.zeros((s, half), F32)
    rest = HEAD_DIM - ROT_DIM
    cosf = jnp.concatenate([cos, cos, jnp.ones((s, rest), F32)], axis=1)
    sa = jnp.concatenate([-sin, z8, jnp.zeros((s, rest), F32)], axis=1)
    sb = jnp.concatenate([z8, sin, jnp.zeros((s, rest), F32)], axis=1)
    return tuple(jnp.tile(t, (1, LANES // HEAD_DIM)) for t in (cosf, sa, sb))


def _rope_fwd(t, cv, sav, sbv):
    w = t.shape[1]
    return t * cv + pltpu.roll(t, w - ROT_DIM // 2, 1) * sav + pltpu.roll(t, ROT_DIM // 2, 1) * sbv


def _rope_bwd(g, cv, sav, sbv):
    w = g.shape[1]
    return g * cv + pltpu.roll(g * sav, ROT_DIM // 2, 1) + pltpu.roll(g * sbv, w - ROT_DIM // 2, 1)


def _joined_columns(wg_ref):
    assert wg_ref.shape[2] % LANES == 0
    return jnp.concatenate([wg_ref[j] for j in range(N_DEV)], axis=1)


def _join_once(wg_ref, w_scr):
    c = wg_ref.shape[2]

    @pl.when(pl.program_id(0) == 0)
    def _():
        for j in range(N_DEV):
            w_scr[:, j * c:(j + 1) * c] = wg_ref[j]


def _inproj_attn(x, g, wg, tabs, token):
    s, d_model = x.shape
    gw = GROUP_WIDTH
    n = N_DEV * wg.shape[2]
    nz = n - 9 * gw - MEM_WIDTH
    reps = gw // LANES
    tm = ROW_TILE

    def body(_, x_ref, g_ref, w_ref, c_ref, sa_ref, sb_ref, hn_ref, *rest):
        outs, (wj, scr, tscr) = rest[:-3], rest[-3:]
        q_refs, k_refs, v_refs, t_refs, qm_ref, z_ref = outs[0:3], outs[3:6], outs[6:9], outs[9:18], outs[18], outs[19]
        _join_once(w_ref, wj)
        xb = x_ref[...]
        r = lax.rsqrt(jnp.mean(xb * xb, axis=-1, keepdims=True) + EPS)
        hn = ((xb * r) * g_ref[...]).astype(BF16)
        hn_ref[...] = hn
        proj = lambda lo, hi: _dot(hn, wj[:, lo:hi])
        tab = (c_ref[...], sa_ref[...], sb_ref[...])
        cv, sav, sbv = [jnp.tile(t, (1, reps)) for t in tab]
        for j, d in enumerate(DILATIONS):
            tq = _rope_fwd(proj(j * gw, (j + 1) * gw), cv, sav, sbv)
            _to_view(scr, tq * SCALE, q_refs[j], d)
            tk = _rope_fwd(proj((3 + j) * gw, (4 + j) * gw), cv, sav, sbv)
            _to_view(scr, tk, k_refs[j], d)
            _to_view(scr, proj((6 + j) * gw, (7 + j) * gw), v_refs[j], d)
            for i in range(3):
                _to_view(tscr, tab[i], t_refs[3 * j + i], d)
        qm_ref[...] = proj(9 * gw, 9 * gw + MEM_WIDTH).astype(BF16)
        z_ref[...] = proj(9 * gw + MEM_WIDTH, n)

    views = [_sds((s // d, d * gw), BF16) for d in DILATIONS]
    tviews = [_sds((s // d, d * LANES), F32) for d in DILATIONS for _ in range(3)]
    out_shape = [_sds((s, d_model), BF16)] + views * 3 + tviews + [_sds((s, MEM_WIDTH), BF16), _sds((s, nz), F32)]
    vspecs = [_view_rows(gw, d, tm) for d in DILATIONS]
    tspecs = [_view_rows(LANES, d, tm) for d in DILATIONS for _ in range(3)]
    out_specs = [_rows(d_model, tm)] + vspecs * 3 + tspecs + [_rows(MEM_WIDTH, tm), _rows(nz, tm)]
    res = _pallas_call(
        body, name="inproj_attn", grid=(s // tm,), out_shape=out_shape,
        in_specs=[ANY, _rows(d_model, tm), _whole((1, d_model)), _resident(wg.shape)] + [_rows(LANES, tm)] * 3,
        out_specs=out_specs,
        scratch_shapes=[pltpu.VMEM((d_model, n), BF16), _view_scratch(gw, tm), _view_scratch(LANES, tm)],
        compiler_params=_params(1, 60),
    )(token, x, g, wg, *tabs)
    tabs_v = [res[10 + 3 * j:13 + 3 * j] for j in range(3)]
    return res[0], res[1:4], res[4:7], res[7:10], tabs_v, res[19], res[20]


def _band_mask(n_keys):
    qi = lax.broadcasted_iota(jnp.int32, (BLOCK, n_keys), 0)
    kj = lax.broadcasted_iota(jnp.int32, (BLOCK, n_keys), 1)
    if n_keys == BLOCK:
        return kj <= qi
    return jnp.logical_or(jnp.logical_and(kj < BLOCK, kj >= qi), jnp.logical_and(kj >= BLOCK, (kj - BLOCK) <= qi))


def _low_head_lanes():
    return lax.broadcasted_iota(jnp.int32, (1, LANES), 1) < HEAD_DIM


def _split_pair(t, low):
    zero = jnp.zeros_like(t)
    return jnp.where(low, t, zero), jnp.where(low, zero, t)


def _pair_specs(d, nb, w):
    if nb == 1:
        return pl.BlockSpec((BLOCK, 2 * w), lambda n: (0, n)), None
    half = nb // 2
    two = pl.BlockSpec((2 * BLOCK, w), lambda n: (n % half, n // half))
    before = pl.BlockSpec((BLOCK, w), lambda n: (jnp.maximum(2 * (n % half) - 1, 0), n // half))
    return two, before


def _head_tiles(w, col0):
    return ([slice(p * LANES, (p + 1) * LANES) for p in range(w // LANES)],
            [slice(col0 + p * LANES, col0 + (p + 1) * LANES) for p in range(w // LANES)])


def _attend_fwd(q_ref, o_ref, lse_ref, rows, col0, kk, vv):
    w = kk.shape[1]
    valid = _band_mask(kk.shape[0])
    low = _low_head_lanes()
    pairs, qcols = _head_tiles(w, col0)
    qs_ = [h for qc in qcols for h in _split_pair(q_ref[rows, qc], low)]
    k2s = [kk[:, pr] for pr in pairs for _ in range(2)]
    scs = [jnp.where(valid, _dot_nt(qh, k2), NEG) for qh, k2 in zip(qs_, k2s)]
    ms = [jnp.max(sc, axis=-1, keepdims=True) for sc in scs]
    ps = [jnp.exp(sc - m) for sc, m in zip(scs, ms)]
    ls = [jnp.sum(p, axis=-1, keepdims=True) for p in ps]
    pns = [(p * (1.0 / l)).astype(BF16) for p, l in zip(ps, ls)]
    for i, (pr, qc) in enumerate(zip(pairs, qcols)):
        v2 = vv[:, pr]
        a, b = 2 * i, 2 * i + 1
        o_ref[rows, qc] = jnp.where(low, _dot(pns[a], v2), _dot(pns[b], v2))
        lse_ref[rows, qc] = jnp.where(low, ms[a] + jnp.log(ls[a]), ms[b] + jnp.log(ls[b]))


TOP, BOTTOM = slice(0, BLOCK), slice(BLOCK, 2 * BLOCK)


def _attn_fwd(q, k, v, d, token):
    ln, dw = q.shape
    w = dw // d
    nb = ln // BLOCK
    two, before = _pair_specs(d, nb, w)

    def body_streams(_, q_ref, kc_ref, vc_ref, o_ref, lse_ref):
        for sb in range(2):
            cols = slice(sb * w, (sb + 1) * w)
            _attend_fwd(q_ref, o_ref, lse_ref, TOP, sb * w, kc_ref[:, cols], vc_ref[:, cols])

    def body_blocks(_, q_ref, kp_ref, kc_ref, vp_ref, vc_ref, o_ref, lse_ref):
        first = pl.program_id(0) % (nb // 2) == 0
        pl.when(first)(lambda: _attend_fwd(q_ref, o_ref, lse_ref, TOP, 0, kc_ref[TOP, :], vc_ref[TOP, :]))
        pl.when(jnp.logical_not(first))(lambda: _attend_fwd(
            q_ref, o_ref, lse_ref, TOP, 0, jnp.concatenate([kp_ref[...], kc_ref[TOP, :]], axis=0),
            jnp.concatenate([vp_ref[...], vc_ref[TOP, :]], axis=0)))
        _attend_fwd(q_ref, o_ref, lse_ref, BOTTOM, 0, kc_ref[...], vc_ref[...])

    if nb == 1:
        body, in_specs, args = body_streams, [ANY, two, two, two], (token, q, k, v)
    else:
        body, in_specs, args = body_blocks, [ANY, two, before, two, before, two], (token, q, k, k, v, v)
    return _pallas_call(
        body, name=f"attn_fwd_d{d}", grid=(d * nb // 2,), out_shape=[_sds((ln, dw), F32)] * 2,
        in_specs=in_specs, out_specs=[two, two], compiler_params=_params(1, 32),
    )(*args)


def _memkv_fwd(mem, g, w):
    n_layers = g.shape[0]
    rows = D_MODEL // N_DEV

    def body(mem_ref, g_ref, w_ref, *kv_refs):
        mb = mem_ref[...]
        r = lax.rsqrt(jnp.mean(mb * mb, axis=-1, keepdims=True) + EPS)
        mn = ((mb * r) * g_ref[...]).astype(BF16)
        kv = _dot(mn, w_ref[...].reshape(D_MODEL, 2 * MEM_WIDTH)).astype(BF16)
        for layer, kv_ref in enumerate(kv_refs):
            @pl.when(pl.program_id(0) == layer)
            def _(kv_ref=kv_ref):
                kv_ref[...] = kv

    return _pallas_call(
        body, name="memkv_fwd", grid=(n_layers,),
        out_shape=[_sds((N_MEM, 2 * MEM_WIDTH), BF16)] * n_layers,
        in_specs=[_whole(mem.shape), pl.BlockSpec((None, 1, D_MODEL), lambda l: (l, 0, 0)),
                  pl.BlockSpec((N_DEV, rows, 2 * MEM_WIDTH), lambda l: (0, l, 0))],
        out_specs=[_whole((N_MEM, 2 * MEM_WIDTH))] * n_layers,
        compiler_params=_params(1, 32),
    )(mem, g.reshape(n_layers, 1, D_MODEL), w)


def _mix_groups(os_, ls_):
    mx = jnp.maximum(jnp.maximum(ls_[0], ls_[1]), ls_[2])
    es = [jnp.exp(t - mx) for t in ls_]
    inv = 1.0 / (es[0] + es[1] + es[2])
    ws = [e * inv for e in es]
    mix = ws[0] * os_[0] + ws[1] * os_[1] + ws[2] * os_[2]
    return ws, mix


MEM_PAIRS = [slice(p * LANES, (p + 1) * LANES) for p in range(MEM_WIDTH // LANES)]


def _mem_probs(qhs, k2s):
    scs = [_dot_nt(qh, k2) * SCALE for qh, k2 in zip(qhs, k2s)]
    es = [jnp.exp(sc - jnp.max(sc, axis=-1, keepdims=True)) for sc in scs]
    return [e * (1.0 / jnp.sum(e, axis=-1, keepdims=True)) for e in es]


def _mem_attn_into(qm, kv_ref, mo_ref):
    low = _low_head_lanes()
    qhs = [h for pr in MEM_PAIRS for h in _split_pair(qm[:, pr], low)]
    k2s = [kv_ref[:, pr] for pr in MEM_PAIRS for _ in range(2)]
    ps = [p.astype(BF16) for p in _mem_probs(qhs, k2s)]
    for i, pr in enumerate(MEM_PAIRS):
        v2 = kv_ref[:, MEM_WIDTH + i * LANES:MEM_WIDTH + (i + 1) * LANES]
        mo_ref[:, pr] = jnp.where(low, _dot(ps[2 * i], v2), _dot(ps[2 * i + 1], v2))


def _mem_attn_bwd(qm, kv_ref, dmem, dqm_ref, dkv_ref):
    low = _low_head_lanes()
    dmb = dmem.astype(BF16)
    vps = [slice(MEM_WIDTH + i * LANES, MEM_WIDTH + (i + 1) * LANES) for i in range(len(MEM_PAIRS))]
    qhs = [h for pr in MEM_PAIRS for h in _split_pair(qm[:, pr], low)]
    dhs = [h for pr in MEM_PAIRS for h in _split_pair(dmb[:, pr], low)]
    k2s = [kv_ref[:, pr] for pr in MEM_PAIRS for _ in range(2)]
    v2s = [kv_ref[:, vp] for vp in vps for _ in range(2)]
    ps = _mem_probs(qhs, k2s)
    dps = [_dot_nt(dh, v2) for dh, v2 in zip(dhs, v2s)]
    dss = [(p * (dp - jnp.sum(dp * p, axis=-1, keepdims=True)) * SCALE).astype(BF16) for p, dp in zip(ps, dps)]
    pbs = [p.astype(BF16) for p in ps]
    for i, (pr, vp) in enumerate(zip(MEM_PAIRS, vps)):
        a, b = 2 * i, 2 * i + 1
        dqm_ref[:, pr] = jnp.where(low, _dot(dss[a], k2s[a]), _dot(dss[b], k2s[b])).astype(BF16)
        dkv_ref[:, pr] += _dot_tn(dss[a], qhs[a]) + _dot_tn(dss[b], qhs[b])
        dkv_ref[:, vp] += _dot_tn(pbs[a], dhs[a]) + _dot_tn(pbs[b], dhs[b])


def _post_attn(os_, ls_, qm, kv, z, x, wg_out):
    s, d_model = x.shape
    gw = GROUP_WIDTH
    nb = gw + MEM_WIDTH
    tm = ROW_TILE

    def body(o0, o1, o2, l0, l1, l2, qm_ref, kv_ref, z_ref, x_ref, w_ref, y_ref, yt_ref, mo_ref, h_ref, s0, s1):
        ov, lv = [], []
        for o_ref, l_ref, d in zip((o0, o1, o2), (l0, l1, l2), DILATIONS):
            ov.append(_from_view(s0, o_ref, d))
            lv.append(_from_view(s1, l_ref, d))
        _, mix = _mix_groups(ov, lv)
        _mem_attn_into(qm_ref[...], kv_ref, mo_ref)
        sz, _ = _silu_parts(z_ref[...])
        y_ref[:, :gw] = (mix * sz[:, :gw]).astype(BF16)
        y_ref[:, gw:] = (mo_ref[...] * sz[:, gw:]).astype(BF16)
        y = y_ref[...]
        yt_ref[...] = y.T
        h_ref[...] = x_ref[...] + _dot(y, _joined_columns(w_ref))

    vspecs = [_view_rows(gw, d) for d in DILATIONS]
    return _pallas_call(
        body, name="post_attn", grid=(s // tm,),
        out_shape=[_sds((s, nb), BF16), _sds((nb, s), BF16), _sds((s, MEM_WIDTH), F32), _sds((s, d_model), F32)],
        in_specs=vspecs * 2 + [_rows(MEM_WIDTH), _whole(kv.shape), _rows(nb), _rows(d_model), _whole(wg_out.shape)],
        out_specs=[_rows(nb), pl.BlockSpec((nb, tm), lambda i: (0, i)), _rows(MEM_WIDTH), _rows(d_model)],
        scratch_shapes=[_view_scratch(gw), _view_scratch(gw)],
        compiler_params=_params(1, 40),
    )(*os_, *ls_, qm, kv, z, x, wg_out)


def _inproj_conv(x, g, wg):
    s, d_model = x.shape
    c = CONV_WIDTH
    n = N_DEV * wg.shape[2]
    nz = n - 3 * c - MEM_WIDTH
    tm = ROW_TILE

    def body(x_ref, g_ref, w_ref, hn_ref, bg_ref, cg_ref, u_ref, qm_ref, z_ref, wj):
        _join_once(w_ref, wj)
        xb = x_ref[...]
        r = lax.rsqrt(jnp.mean(xb * xb, axis=-1, keepdims=True) + EPS)
        hn = ((xb * r) * g_ref[...]).astype(BF16)
        hn_ref[...] = hn
        bg_ref[...] = _dot(hn, wj[:, 0:c])
        cg_ref[...] = _dot(hn, wj[:, c:2 * c])
        u_ref[...] = _dot(hn, wj[:, 2 * c:3 * c])
        qm_ref[...] = _dot(hn, wj[:, 3 * c:3 * c + MEM_WIDTH]).astype(BF16)
        z_ref[...] = _dot(hn, wj[:, 3 * c + MEM_WIDTH:])

    return _pallas_call(
        body, name="inproj_conv", grid=(s // tm,),
        out_shape=[_sds((s, d_model), BF16)] + [_sds((s, c), F32)] * 3 + [_sds((s, MEM_WIDTH), BF16), _sds((s, nz), F32)],
        in_specs=[_rows(d_model), _whole((1, d_model)), _resident(wg.shape)],
        out_specs=[_rows(d_model)] + [_rows(c)] * 3 + [_rows(MEM_WIDTH), _rows(nz)],
        scratch_shapes=[pltpu.VMEM((d_model, n), BF16)],
        compiler_params=_params(1, 60),
    )(x, g, wg)


HALO = 8


def _halo_before(width, tm=ROW_TILE):
    return pl.BlockSpec((HALO, width), lambda i: (jnp.maximum(i * (tm // HALO) - 1, 0), 0))


def _halo_after(width, n_rows, tm=ROW_TILE):
    return pl.BlockSpec((HALO, width), lambda i: (jnp.minimum((i + 1) * (tm // HALO), n_rows // HALO - 1), 0))


def _conv_taps(cg_ref, u_ref, cgh_ref, uh_ref, i):
    a = cg_ref[...] * u_ref[...]
    ah = jnp.where(i > 0, cgh_ref[...] * uh_ref[...], 0.0)
    row = lax.broadcasted_iota(jnp.int32, a.shape, 0)
    a1 = jnp.where(row == 0, ah[HALO - 1:HALO], pltpu.roll(a, 1, 0))
    a2 = jnp.where(row == 0, ah[HALO - 2:HALO - 1], jnp.where(row == 1, ah[HALO - 1:HALO], pltpu.roll(a, 2, 0)))
    return a, a1, a2


def _post_conv_loss(bg, cg, u, qm, kv, z, h1, w_out, cw, gf, tgt):
    s, d = h1.shape
    c = CONV_WIDTH
    nb = c + MEM_WIDTH
    tm = ROW_TILE

    def body(bg_ref, cg_ref, u_ref, cgh_ref, uh_ref, qm_ref, kv_ref, z_ref, h_ref, w_ref, cw_ref, gf_ref, t_ref,
             y_ref, yt_ref, mo_ref, dh_ref, dhb_ref, loss_ref, dgf_ref):
        i = pl.program_id(0)
        a, a1, a2 = _conv_taps(cg_ref, u_ref, cgh_ref, uh_ref, i)
        conv = cw_ref[0:1, :] * a2 + cw_ref[1:2, :] * a1 + cw_ref[2:3, :] * a
        mix = bg_ref[...] * conv
        _mem_attn_into(qm_ref[...], kv_ref, mo_ref)
        sz, _ = _silu_parts(z_ref[...])
        y_ref[:, :c] = (mix * sz[:, :c]).astype(BF16)
        y_ref[:, c:] = (mo_ref[...] * sz[:, c:]).astype(BF16)
        y = y_ref[...]
        yt_ref[...] = y.T
        h2 = h_ref[...] + _dot(y, w_ref[...])
        r = lax.rsqrt(jnp.mean(h2 * h2, axis=-1, keepdims=True) + EPS)
        nh = h2 * r
        gfv = gf_ref[...]
        diff = nh * gfv - t_ref[...]
        dout = diff * (1.0 / d)
        dn = dout * gfv
        dh2 = r * dn - h2 * ((r * r * r) * jnp.mean(dn * h2, axis=-1, keepdims=True))
        dh_ref[...] = dh2
        dhb_ref[...] = dh2.astype(BF16)

        @pl.when(i == 0)
        def _():
            loss_ref[...] = jnp.zeros_like(loss_ref)
            dgf_ref[...] = jnp.zeros_like(dgf_ref)

        loss_ref[...] += 0.5 * jnp.sum(jnp.mean(diff * diff, axis=-1, keepdims=True))
        dgf_ref[...] += jnp.sum(dout * nh, axis=0, keepdims=True)

    return _pallas_call(
        body, name="post_conv_loss", grid=(s // tm,),
        out_shape=[_sds((s, nb), BF16), _sds((nb, s), BF16), _sds((s, MEM_WIDTH), F32), _sds((s, d), F32),
                   _sds((s, d), BF16), _plain((8, LANES), F32), _plain((1, d), F32)],
        in_specs=[_rows(c)] * 3 + [_halo_before(c)] * 2 + [_rows(MEM_WIDTH), _whole(kv.shape), _rows(nb), _rows(d),
                  _whole(w_out.shape), _whole(cw.shape), _whole((1, d)), _rows(d)],
        out_specs=[_rows(nb), pl.BlockSpec((nb, tm), lambda i: (0, i)), _rows(MEM_WIDTH), _rows(d), _rows(d),
                   _whole((8, LANES)), _whole((1, d))],
        compiler_params=_params(1, 48),
    )(bg, cg, u, cg, u, qm, kv, z, h1, w_out, cw, gf, tgt)


def _bwd_post_conv(dhb, w_out, bg, cg, u, z, qm, kv, mo, cw):
    s = dhb.shape[0]
    c = CONV_WIDTH
    nb = c + MEM_WIDTH

    def body(dh_ref, w_ref, bg_ref, cg_ref, u_ref, cgh_ref, uh_ref, z_ref, qm_ref, kv_ref, mo_ref, cw_ref,
             dz_ref, dbg_ref, dc_ref, dqm_ref, dkv_ref):
        i = pl.program_id(0)

        @pl.when(i == 0)
        def _():
            dkv_ref[...] = jnp.zeros_like(dkv_ref)

        dy = _dot_nt(dh_ref[...], w_ref[...])
        sz, dsz = _silu_parts(z_ref[...])
        a, a1, a2 = _conv_taps(cg_ref, u_ref, cgh_ref, uh_ref, i)
        conv = cw_ref[0:1, :] * a2 + cw_ref[1:2, :] * a1 + cw_ref[2:3, :] * a
        bgv = bg_ref[...]
        dz_ref[:, :c] = (dy[:, :c] * (bgv * conv) * dsz[:, :c]).astype(BF16)
        dz_ref[:, c:] = (dy[:, c:] * mo_ref[...] * dsz[:, c:]).astype(BF16)
        dbr = dy * sz
        dmix = dbr[:, :c]
        dbg_ref[...] = (dmix * conv).astype(BF16)
        dc_ref[...] = dmix * bgv
        _mem_attn_bwd(qm_ref[...], kv_ref, dbr[:, c:], dqm_ref, dkv_ref)

    return _pallas_call(
        body, name="bwd_post_conv", grid=(s // ROW_TILE,),
        out_shape=[_sds((s, nb), BF16), _sds((s, c), BF16), _sds((s, c), F32), _sds((s, MEM_WIDTH), BF16),
                   _plain(kv.shape, F32)],
        in_specs=[_rows(D_MODEL), _whole(w_out.shape)] + [_rows(c)] * 3 + [_halo_before(c)] * 2
                 + [_rows(nb), _rows(MEM_WIDTH), _whole(kv.shape), _rows(MEM_WIDTH), _whole(cw.shape)],
        out_specs=[_rows(nb), _rows(c), _rows(c), _rows(MEM_WIDTH), _whole(kv.shape)],
        compiler_params=_params(1, 48),
    )(dhb, w_out, bg, cg, u, cg, u, z, qm, kv, mo, cw)


def _bwd_conv(dconv, cg, u, cw):
    s, c = dconv.shape
    tm = ROW_TILE
    last = s // tm - 1

    def body(dc_ref, dcn_ref, cg_ref, u_ref, cgh_ref, uh_ref, cw_ref, dcg_ref, du_ref, dcw_ref):
        i = pl.program_id(0)

        @pl.when(i == 0)
        def _():
            dcw_ref[...] = jnp.zeros_like(dcw_ref)

        dc = dc_ref[...]
        dcn = jnp.where(i < last, dcn_ref[...], 0.0)
        row = lax.broadcasted_iota(jnp.int32, dc.shape, 0)
        d1 = jnp.where(row == tm - 1, dcn[0:1], pltpu.roll(dc, tm - 1, 0))
        d2 = jnp.where(row == tm - 1, dcn[1:2], jnp.where(row == tm - 2, dcn[0:1], pltpu.roll(dc, tm - 2, 0)))
        da = cw_ref[2:3, :] * dc + cw_ref[1:2, :] * d1 + cw_ref[0:1, :] * d2
        a, a1, a2 = _conv_taps(cg_ref, u_ref, cgh_ref, uh_ref, i)
        dcg_ref[...] = (da * u_ref[...]).astype(BF16)
        du_ref[...] = (da * cg_ref[...]).astype(BF16)
        dcw_ref[0:1, :] += jnp.sum(dc * a2, axis=0, keepdims=True)
        dcw_ref[1:2, :] += jnp.sum(dc * a1, axis=0, keepdims=True)
        dcw_ref[2:3, :] += jnp.sum(dc * a, axis=0, keepdims=True)

    return _pallas_call(
        body, name="bwd_conv", grid=(s // tm,),
        out_shape=[_sds((s, c), BF16), _sds((s, c), BF16), _plain((8, c), F32)],
        in_specs=[_rows(c), _halo_after(c, s), _rows(c), _rows(c), _halo_before(c), _halo_before(c), _whole(cw.shape)],
        out_specs=[_rows(c), _rows(c), _whole((8, c))], compiler_params=_params(1, 40),
    )(dconv, dconv, cg, u, cg, u, cw)


def _assemble(p_refs, pieces, widths, dp, scr):
    off = 0
    for p_ref, (_, d), wd in zip(p_refs, pieces, widths):
        if d == 1:
            dp[:, off:off + wd] = p_ref[...]
        else:
            dp[:, off:off + wd] = _from_view(scr, p_ref, d).astype(BF16)
        off += wd


def _dgrad_norm(pieces, wg, h, g, dres, token, onward, name):
    s, d_model = h.shape
    n = N_DEV * wg.shape[2]
    tm = ROW_TILE
    widths = [p.shape[1] // d for p, d in pieces]
    assert sum(widths) == n
    n_p = len(pieces)

    def body(_, *refs):
        p_refs = refs[:n_p]
        w_ref, h_ref, g_ref, dr_ref, dh_ref, dg_ref = refs[n_p:n_p + 6]
        dp, scr, wj = refs[-3:]
        _join_once(w_ref, wj)

        @pl.when(pl.program_id(0) == 0)
        def _():
            dg_ref[...] = jnp.zeros_like(dg_ref)

        _assemble(p_refs, pieces, widths, dp, scr)
        dhn = _dot_nt(dp[...], wj[...])
        hb = h_ref[...]
        r = lax.rsqrt(jnp.mean(hb * hb, axis=-1, keepdims=True) + EPS)
        dg_ref[...] += jnp.sum(dhn * (hb * r), axis=0, keepdims=True)
        dn = dhn * g_ref[...]
        dh = dr_ref[...] + r * dn - hb * ((r * r * r) * jnp.mean(dn * hb, axis=-1, keepdims=True))
        dh_ref[...] = dh
        if onward:
            dhb_ref, dpt_ref = refs[n_p + 6:n_p + 8]
            dhb_ref[...] = dh.astype(BF16)
            dpt_ref[...] = dp[...].T

    p_specs = [_view_rows(wd, d) for (_, d), wd in zip(pieces, widths)]
    out_shape = [_plain((s, d_model), F32), _plain((1, d_model), F32)]
    out_specs = [_rows(d_model), _whole((1, d_model))]
    if onward:
        out_shape += [_sds((s, d_model), BF16), _sds((n, s), BF16)]
        out_specs += [_rows(d_model), pl.BlockSpec((n, tm), lambda i: (0, i))]
    return _pallas_call(
        body, name=name, grid=(s // tm,), out_shape=out_shape,
        in_specs=[ANY] + p_specs + [_resident(wg.shape), _rows(d_model), _whole((1, d_model)), _rows(d_model)],
        out_specs=out_specs,
        scratch_shapes=[pltpu.VMEM((tm, n), BF16), _view_scratch(GROUP_WIDTH), pltpu.VMEM((d_model, n), BF16)],
        compiler_params=_params(1, 60),
    )(token, *[p for p, _ in pieces], wg, h, g, dres)


def _assemble_dproj_t(pieces, n, name):
    tm = ROW_TILE
    widths = [p.shape[1] // d for p, d in pieces]
    assert sum(widths) == n
    s = pieces[0][0].shape[0] * pieces[0][1]
    n_p = len(pieces)

    def body(*refs):
        p_refs, (dpt_ref, dp, scr) = refs[:n_p], refs[n_p:]
        _assemble(p_refs, pieces, widths, dp, scr)
        dpt_ref[...] = dp[...].T

    return _pallas_call(
        body, name=name, grid=(s // tm,), out_shape=_sds((n, s), BF16),
        in_specs=[_view_rows(wd, d) for (_, d), wd in zip(pieces, widths)],
        out_specs=pl.BlockSpec((n, tm), lambda i: (0, i)),
        scratch_shapes=[pltpu.VMEM((tm, n), BF16), _view_scratch(GROUP_WIDTH)],
        compiler_params=_params(1, 40),
    )(*[p for p, _ in pieces])


def _wgrad_shards_t(dp_t, h, c, name):
    n, s = dp_t.shape
    d_model = h.shape[1]
    per_step = 2

    def body(a_ref, b_ref, o_ref):
        o_ref[...] = _dot(a_ref[...], b_ref[...]).astype(BF16).reshape(per_step, c, d_model)

    return _pallas_call(
        body, name=name, grid=(N_DEV // per_step,), out_shape=_sds((N_DEV, c, d_model), BF16),
        in_specs=[pl.BlockSpec((per_step * c, s), lambda j: (j, 0)), _resident(h.shape)],
        out_specs=pl.BlockSpec((per_step, c, d_model), lambda j: (j, 0, 0)), compiler_params=_params(1, 40),
    )(dp_t, h)


def _wgrad_cols(a_t, b, c, name):
    m, s = a_t.shape
    assert c % LANES == 0

    def body(a_ref, b_ref, o_ref):
        wide = _dot(a_ref[...], b_ref[...]).astype(BF16)
        for j in range(WGRAD_SHARDS):
            o_ref[j] = wide[:, j * c:(j + 1) * c]

    return _pallas_call(
        body, name=name, grid=(N_DEV // WGRAD_SHARDS,), out_shape=_sds((N_DEV, m, c), BF16),
        in_specs=[_whole(a_t.shape), pl.BlockSpec((s, WGRAD_SHARDS * c), lambda j: (0, j))],
        out_specs=pl.BlockSpec((WGRAD_SHARDS, m, c), lambda j: (j, 0, 0)), compiler_params=_params(1, 40),
    )(a_t, b)


def _wgrad_rows(a_t, b, name):
    m, s = a_t.shape
    n = b.shape[1]
    mr = m // N_DEV

    def body(a_ref, b_ref, o_ref):
        o_ref[...] = _dot(a_ref[...], b_ref[...]).astype(BF16).reshape(WGRAD_SHARDS, mr, n)

    return _pallas_call(
        body, name=name, grid=(N_DEV // WGRAD_SHARDS,), out_shape=_sds((N_DEV, mr, n), BF16),
        in_specs=[pl.BlockSpec((WGRAD_SHARDS * mr, s), lambda j: (j, 0)), _whole(b.shape)],
        out_specs=pl.BlockSpec((WGRAD_SHARDS, mr, n), lambda j: (j, 0, 0)), compiler_params=_params(1, 40),
    )(a_t, b)


def _memkv_bwd(dkvs, w, mem, g):
    n_layers = g.shape[0]
    rows = D_MODEL // N_DEV

    def body(dkv0_ref, dkv1_ref, w_ref, mem_ref, g_ref, dw_ref, dg_ref):
        mb = mem_ref[...]
        r = lax.rsqrt(jnp.mean(mb * mb, axis=-1, keepdims=True) + EPS)
        nm = mb * r
        mn = (nm * g_ref[...]).astype(BF16)
        dkvb = jnp.where(pl.program_id(0) == 0, dkv0_ref[...], dkv1_ref[...]).astype(BF16)
        dw_ref[...] = _dot_tn(mn, dkvb).astype(BF16).reshape(N_DEV, rows, 2 * MEM_WIDTH)
        dmn = _dot_nt(dkvb, w_ref[...].reshape(D_MODEL, 2 * MEM_WIDTH))
        dg_ref[...] = jnp.sum(dmn * nm, axis=0, keepdims=True)

    lay = lambda *shape: pl.BlockSpec((None,) + shape, lambda l: (l, 0, 0))
    major = pl.BlockSpec((N_DEV, rows, 2 * MEM_WIDTH), lambda l: (0, l, 0))
    return _pallas_call(
        body, name="memkv_bwd", grid=(n_layers,),
        out_shape=[_sds((N_DEV, n_layers * rows, 2 * MEM_WIDTH), BF16), _plain((n_layers, 1, D_MODEL), F32)],
        in_specs=[_whole(dkvs[0].shape), _whole(dkvs[1].shape), major, _whole(mem.shape), lay(1, D_MODEL)],
        out_specs=[major, lay(1, D_MODEL)],
        compiler_params=_params(1, 32),
    )(*dkvs, w, mem, g.reshape(n_layers, 1, D_MODEL))


def _bwd_post_attn(dhb, wg_out, z, os_, ls_, qm, kv, mo, head_ones, token):
    s = dhb.shape[0]
    gw = GROUP_WIDTH
    nb = gw + MEM_WIDTH
    tm = ROW_TILE

    def body(_, dh_ref, w_ref, z_ref, o0, o1, o2, l0, l1, l2, qm_ref, kv_ref, mo_ref, bd_ref,
             dz_ref, do0, do1, do2, dl0, dl1, dl2, dqm_ref, dkv_ref, s0, s1):
        @pl.when(pl.program_id(0) == 0)
        def _():
            dkv_ref[...] = jnp.zeros_like(dkv_ref)

        dy = _dot_nt(dh_ref[...], _joined_columns(w_ref))
        ov, lv = [], []
        for o_ref, l_ref, d in zip((o0, o1, o2), (l0, l1, l2), DILATIONS):
            ov.append(_from_view(s0, o_ref, d))
            lv.append(_from_view(s1, l_ref, d))
        ws, mix = _mix_groups(ov, lv)
        sz, dsz = _silu_parts(z_ref[...])
        dz_ref[:, :gw] = (dy[:, :gw] * mix * dsz[:, :gw]).astype(BF16)
        dz_ref[:, gw:] = (dy[:, gw:] * mo_ref[...] * dsz[:, gw:]).astype(BF16)
        dbr = dy * sz
        dmix = dbr[:, :gw]
        t = dmix * mix
        th = t.astype(BF16)
        tl = (t - th.astype(F32)).astype(BF16)
        rs = _dot(th, bd_ref[...]) + _dot(tl, bd_ref[...])
        for wg_, do_ref, dl_ref, d in zip(ws, (do0, do1, do2), (dl0, dl1, dl2), DILATIONS):
            _to_view(s0, wg_ * dmix, do_ref, d)
            _to_view(s1, wg_ * rs, dl_ref, d)
        _mem_attn_bwd(qm_ref[...], kv_ref, dbr[:, gw:], dqm_ref, dkv_ref)

    vspecs = [_view_rows(gw, d) for d in DILATIONS]
    return _pallas_call(
        body, name="bwd_post_attn", grid=(s // tm,),
        out_shape=[_sds((s, nb), BF16)] + [_sds((s // d, d * gw), BF16) for d in DILATIONS]
                  + [_sds((s // d, d * gw), F32) for d in DILATIONS] + [_sds((s, MEM_WIDTH), BF16), _plain(kv.shape, F32)],
        in_specs=[ANY, _rows(D_MODEL), _whole(wg_out.shape), _rows(nb)] + vspecs * 2
                 + [_rows(MEM_WIDTH), _whole(kv.shape), _rows(MEM_WIDTH), _whole(head_ones.shape)],
        out_specs=[_rows(nb)] + vspecs * 2 + [_rows(MEM_WIDTH), _whole(kv.shape)],
        scratch_shapes=[_view_scratch(gw), _view_scratch(gw)],
        compiler_params=_params(1, 48),
    )(token, dhb, wg_out, z, *os_, *ls_, qm, kv, mo, head_ones)


def _attn_bwd(q, k, v, lse, do, dl, tabs, d, token):
    ln, dw = q.shape
    w = dw // d
    nb = ln // BLOCK
    reps = w // LANES
    two, before = _pair_specs(d, nb, w)
    two_t, _ = _pair_specs(d, nb, LANES)

    def attend(q_ref, l_ref, do_ref, dl_ref, dqs, acck, accv, rows, col0, kk, vv, acc_rows):
        valid = _band_mask(kk.shape[0])
        low = _low_head_lanes()
        pairs, qcols = _head_tiles(w, col0)
        cols = [slice(col0 + h * HEAD_DIM, col0 + h * HEAD_DIM + 1) for h in range(HEADS_PER_GROUP)]
        qhs = [h for qc in qcols for h in _split_pair(q_ref[rows, qc], low)]
        dobs = [h for qc in qcols for h in _split_pair(do_ref[rows, qc], low)]
        k2s = [kk[:, pr] for pr in pairs for _ in range(2)]
        v2s = [vv[:, pr] for pr in pairs for _ in range(2)]
        scs = [jnp.where(valid, _dot_nt(qh, k2), NEG) for qh, k2 in zip(qhs, k2s)]
        dps = [_dot_nt(dob, v2) for dob, v2 in zip(dobs, v2s)]
        ps = [jnp.exp(sc - l_ref[rows, col]) for sc, col in zip(scs, cols)]
        dss = [(p * (dp - dl_ref[rows, col])).astype(BF16) for p, dp, col in zip(ps, dps, cols)]
        pbs = [p.astype(BF16) for p in ps]
        for i, qc in enumerate(qcols):
            a, b = 2 * i, 2 * i + 1
            dqs[rows, qc] = jnp.where(low, _dot(dss[a], k2s[a]), _dot(dss[b], k2s[b])) * SCALE
            acck[acc_rows, qc] += _dot_tn(dss[a], qhs[a]) + _dot_tn(dss[b], qhs[b])
            accv[acc_rows, qc] += _dot_tn(pbs[a], dobs[a]) + _dot_tn(pbs[b], dobs[b])

    def body_streams(_, q_ref, kc_ref, vc_ref, l_ref, do_ref, dl_ref, c_ref, sa_ref, sb_ref,
                     dq_ref, dk_ref, dv_ref, acck, accv, dqs):
        acck[...] = jnp.zeros_like(acck)
        accv[...] = jnp.zeros_like(accv)
        for sb in range(2):
            cols = slice(sb * w, (sb + 1) * w)
            attend(q_ref, l_ref, do_ref, dl_ref, dqs, acck, accv, TOP, sb * w, kc_ref[:, cols], vc_ref[:, cols], TOP)
        tabs2 = [jnp.concatenate([jnp.tile(r[:, sb * LANES:(sb + 1) * LANES], (1, reps)) for sb in range(2)], axis=1)
                 for r in (c_ref, sa_ref, sb_ref)]
        dq_ref[...] = _rope_bwd(dqs[...], *tabs2).astype(BF16)
        dk_ref[...] = _rope_bwd(acck[...], *tabs2).astype(BF16)
        dv_ref[...] = accv[...].astype(BF16)

    def body_blocks(_, q_ref, kp_ref, kc_ref, vp_ref, vc_ref, l_ref, do_ref, dl_ref, cq, saq, sbq, ck, sak, sbk,
                    dq_ref, dk_ref, dv_ref, acck, accv, dqs):
        i = pl.program_id(0) % (nb // 2)

        @pl.when(i == 0)
        def _():
            acck[...] = jnp.zeros_like(acck)
            accv[...] = jnp.zeros_like(accv)

        refs = (q_ref, l_ref, do_ref, dl_ref, dqs, acck, accv)
        pl.when(i == 0)(lambda: attend(*refs, TOP, 0, kc_ref[TOP, :], vc_ref[TOP, :], TOP))
        pl.when(i != 0)(lambda: attend(
            *refs, TOP, 0, jnp.concatenate([kp_ref[...], kc_ref[TOP, :]], axis=0),
            jnp.concatenate([vp_ref[...], vc_ref[TOP, :]], axis=0),
            pl.ds(pl.multiple_of((2 * i - 1) * BLOCK, BLOCK), 2 * BLOCK)))
        attend(*refs, BOTTOM, 0, kc_ref[...], vc_ref[...], pl.ds(pl.multiple_of(2 * i * BLOCK, BLOCK), 2 * BLOCK))
        tq = [jnp.tile(r[...], (1, reps)) for r in (cq, saq, sbq)]
        dq_ref[...] = _rope_bwd(dqs[...], *tq).astype(BF16)

        @pl.when(i == nb // 2 - 1)
        def _():
            for r0 in range(0, nb * BLOCK, 2 * BLOCK):
                rows = slice(r0, r0 + 2 * BLOCK)
                tk = [jnp.tile(r[rows, :], (1, reps)) for r in (ck, sak, sbk)]
                dk_ref[rows, :] = _rope_bwd(acck[rows, :], *tk).astype(BF16)
                dv_ref[rows, :] = accv[rows, :].astype(BF16)

    if nb == 1:
        body = body_streams
        in_specs = [ANY] + [two] * 6 + [two_t] * 3
        args = (token, q, k, v, lse, do, dl, *tabs)
        out_specs = [two, two, two]
        acc_shape = (BLOCK, 2 * w)
    else:
        body = body_blocks
        stream = pl.BlockSpec((nb * BLOCK, w), lambda n: (0, n // (nb // 2)))
        stream_t = pl.BlockSpec((nb * BLOCK, LANES), lambda n: (0, n // (nb // 2)))
        in_specs = [ANY, two, before, two, before, two, two, two, two] + [two_t] * 3 + [stream_t] * 3
        args = (token, q, k, k, v, v, lse, do, dl, *tabs, *tabs)
        out_specs = [two, stream, stream]
        acc_shape = (nb * BLOCK, w)
    return _pallas_call(
        body, name=f"attn_bwd_d{d}", grid=(d * nb // 2,), out_shape=[_sds((ln, dw), BF16)] * 3,
        in_specs=in_specs, out_specs=out_specs,
        scratch_shapes=[pltpu.VMEM(acc_shape, F32), pltpu.VMEM(acc_shape, F32), pltpu.VMEM(two.block_shape, F32)],
        compiler_params=_params(1, 48),
    )(*args)


def _position():
    return lax.axis_index("x"), lax.axis_index("y"), lax.axis_index("c")


def _all_gather(shards, afters, name):
    n_a, n_in = len(shards), len(shards) + len(afters)

    def body(*refs):
        x_refs, out_refs = refs[:n_a], refs[n_in:n_in + n_a]
        send_sems, recv_sems, local_sems = refs[n_in + n_a:]
        x, y, c = _position()
        me, sibling = (x, y, c), (x, y, 1 - c)
        chips = [(1 - x, y), (x, 1 - y), (1 - x, 1 - y)]

        def rows(a, px, py, pc):
            return out_refs[a].at[4 * px + 2 * py + pc]

        def copy(a, k, block, to, own=False):
            return pltpu.make_async_remote_copy(
                src_ref=x_refs[a] if own else rows(a, *block), dst_ref=rows(a, *block),
                send_sem=send_sems.at[a, k], recv_sem=recv_sems.at[a, k], device_id=to, device_id_type=MESH)

        mine = [pltpu.make_async_copy(x_refs[a], rows(a, *me), local_sems.at[a]) for a in range(n_a)]
        for cp in mine:
            cp.start()
        first = []
        for j, chip in enumerate(chips):
            first += [copy(a, 1 + j, me, (*chip, c), own=True) for a in range(n_a)]
        first += [copy(a, 0, me, sibling, own=True) for a in range(n_a)]
        for cp in first:
            cp.start()
        passed = []
        for j, chip in enumerate(chips):
            for a in range(n_a):
                copy(a, 1 + j, (*chip, c), me).wait_recv()
                fwd = copy(a, 4 + j, (*chip, c), sibling)
                fwd.start()
                passed.append(fwd)
        for a in range(n_a):
            copy(a, 0, sibling, me).wait_recv()
        for j, chip in enumerate(chips):
            for a in range(n_a):
                copy(a, 4 + j, (*chip, 1 - c), me).wait_recv()
        for cp in first + passed:
            cp.wait_send()
        for cp in mine:
            cp.wait()

    return _pallas_call(
        body, name=name, out_shape=[_sds((N_DEV,) + t.shape, t.dtype) for t in shards],
        in_specs=[ANY] * n_in, out_specs=[ANY] * n_a,
        scratch_shapes=[pltpu.SemaphoreType.DMA((n_a, 7)), pltpu.SemaphoreType.DMA((n_a, 7)),
                        pltpu.SemaphoreType.DMA((n_a,))],
    )(*shards, *afters)


def _all_gather_relay(xs, name):
    def body(x_ref, out_ref, send_sems, recv_sems, local_sem):
        x, y, c = _position()
        me, sibling = (x, y, c), (x, y, 1 - c)
        xn, yn, diag = (1 - x, y, c), (x, 1 - y, c), (1 - x, 1 - y, c)
        src_nb = (x + c * (1 - 2 * x), y + (1 - c) * (1 - 2 * y), c)
        dst_nb = (x + (1 - c) * (1 - 2 * x), y + c * (1 - 2 * y), c)

        def rows(dev):
            return out_ref.at[4 * dev[0] + 2 * dev[1] + dev[2]]

        def copy(k, block, to, own=False):
            return pltpu.make_async_remote_copy(
                src_ref=x_ref if own else rows(block), dst_ref=rows(block),
                send_sem=send_sems.at[k], recv_sem=recv_sems.at[k], device_id=to, device_id_type=MESH)

        mine = pltpu.make_async_copy(x_ref, rows(me), local_sem)
        mine.start()
        first = [copy(1, me, xn, own=True), copy(2, me, yn, own=True), copy(0, me, sibling, own=True)]
        for cp in first:
            cp.start()
        copy(1, xn, me).wait_recv()
        copy(2, yn, me).wait_recv()
        relay = copy(3, src_nb, dst_nb)
        relay.start()
        passed = [copy(4, xn, sibling), copy(5, yn, sibling)]
        for cp in passed:
            cp.start()
        copy(3, diag, me).wait_recv()
        last = copy(6, diag, sibling)
        last.start()
        copy(0, sibling, me).wait_recv()
        for k, blk in ((4, (1 - x, y, 1 - c)), (5, (x, 1 - y, 1 - c)), (6, (1 - x, 1 - y, 1 - c))):
            copy(k, blk, me).wait_recv()
        for cp in first + [relay] + passed + [last]:
            cp.wait_send()
        mine.wait()

    return _pallas_call(
        body, name=name, out_shape=_sds((N_DEV,) + xs.shape, xs.dtype),
        in_specs=[ANY], out_specs=ANY,
        scratch_shapes=[pltpu.SemaphoreType.DMA((7,)), pltpu.SemaphoreType.DMA((7,)), pltpu.SemaphoreType.DMA],
    )(xs)


HBM_SPEC = pl.BlockSpec(memory_space=pltpu.HBM)
SEM_SPEC = pl.BlockSpec(memory_space=pltpu.SEMAPHORE)
EFFECT = pltpu.SideEffectType.DATAFLOW_SIDE_EFFECTING
def _plan_gather_own(src_refs, land_refs):
    x, y, c = _position()
    me = 4 * x + 2 * y + c
    peers = [(x, y, 1 - c), (1 - x, y, c), (x, 1 - y, c), (1 - x, 1 - y, c)]
    return [(land_refs[a].at[me], land_refs[a].at[me], (a, k), peer)
            for k, peer in enumerate(peers) for a in range(len(land_refs))]


def _plan_gather_pass(src_refs, land_refs):
    x, y, c = _position()
    chips = [(1 - x, y), (x, 1 - y), (1 - x, 1 - y)]
    return [(land_refs[a].at[4 * px + 2 * py + c], land_refs[a].at[4 * px + 2 * py + c], (a, j), (x, y, 1 - c))
            for j, (px, py) in enumerate(chips) for a in range(len(land_refs))]


def _plan_to_sibling(src_refs, land_refs):
    x, y, c = _position()
    return [(src_refs[a].at[2 * k + (1 - c)], land_refs[a].at[k], (a, k), (x, y, 1 - c))
            for k in range(4) for a in range(len(src_refs))]


def _plan_to_chips(src_refs, land_refs):
    x, y, c = _position()
    chips = [(1 - x, y), (x, 1 - y), (1 - x, 1 - y)]
    return [(src_refs[a].at[2 * px + py], land_refs[a].at[j], (a, j), (px, py, c))
            for j, (px, py) in enumerate(chips) for a in range(len(src_refs))]


def _split_start(srcs, lands, plan, n_sem, after, name):
    n_s, n_a = len(srcs), len(lands)
    n_b = n_s + n_a

    def body(*refs):
        src_refs, land_refs = refs[:n_s], refs[n_s:n_b]
        send_sems, recv_sems, token = refs[n_b + 1], refs[n_b + 2], refs[-1]
        for src, dst, (a, k), dev in plan(src_refs, land_refs):
            i = a * n_sem + k
            pltpu.make_async_remote_copy(src_ref=src, dst_ref=dst, send_sem=send_sems.at[i], recv_sem=recv_sems.at[i],
                                         device_id=dev, device_id_type=MESH).start()
        token[...] = jnp.zeros_like(token)

    bufs = list(srcs) + list(lands)
    res = pl.pallas_call(
        body, name=name,
        out_shape=(pltpu.SemaphoreType.DMA((n_a * n_sem,)), pltpu.SemaphoreType.DMA((n_a * n_sem,)),
                   *[pltpu.HBM(t.shape, t.dtype) for t in bufs], _plain((8, LANES), F32)),
        in_specs=[HBM_SPEC] * n_b + [ANY],
        out_specs=(SEM_SPEC, SEM_SPEC, *[HBM_SPEC] * n_b, pl.BlockSpec(memory_space=pltpu.VMEM)),
        input_output_aliases={i: 2 + i for i in range(n_b)},
        compiler_params=pltpu.CompilerParams(has_side_effects=EFFECT),
    )(*[pltpu.with_memory_space_constraint(t, pltpu.HBM) for t in bufs], after)
    return (res[0], res[1], res[2:2 + n_s], res[2 + n_s:2 + n_b]), res[-1]


def _split_wait(started, plan, after, name, first=0, n_sem=None):
    send_sems, recv_sems, srcs, lands = started
    n_s, n_a = len(srcs), len(lands)
    n_b = n_s + n_a
    n_sem = n_sem or send_sems.shape[0] // n_a

    def body(*refs):
        src_refs, land_refs = refs[:n_s], refs[n_s:n_b]
        s_sems, r_sems = refs[n_b], refs[n_b + 1]
        for src, dst, (a, k), dev in plan(src_refs, land_refs):
            i = (first + a) * n_sem + k
            cp = pltpu.make_async_remote_copy(src_ref=src, dst_ref=dst, send_sem=s_sems.at[i], recv_sem=r_sems.at[i],
                                              device_id=dev, device_id_type=MESH)
            cp.wait_send()
            cp.wait_recv()

    bufs = list(srcs) + list(lands)
    res = pl.pallas_call(
        body, name=name, out_shape=tuple(pltpu.HBM(t.shape, t.dtype) for t in bufs),
        in_specs=[HBM_SPEC] * n_b + [SEM_SPEC, SEM_SPEC, ANY],
        out_specs=tuple([HBM_SPEC] * n_b),
        input_output_aliases={i: i for i in range(n_b)},
        compiler_params=pltpu.CompilerParams(has_side_effects=EFFECT),
    )(*bufs, send_sems, recv_sems, after)
    return res[:n_s], res[n_s:]


SUBLANES = 8


def _row_tile(r):
    return max(t for t in range(SUBLANES, ROW_TILE + 1, SUBLANES) if r % t == 0)


def _rs_add_sibling(gp, recv, ck_arr, name):
    _, r, l = gp.shape
    tr = r if r <= 4 * ROW_TILE else _row_tile(r)
    block = lambda k, ck: (k + ck[1] + 1) % 4

    def body(ck_ref, g_ref, r_ref, pf_ref, pb_ref):
        sm = g_ref[...].astype(F32) + r_ref[...].astype(F32)
        pf_ref[...] = sm
        pb_ref[...] = sm.astype(BF16)

    spec = pl.BlockSpec((None, tr, l), lambda i, k, ck: (block(k, ck), i, 0))
    return _pallas_call(
        body, name=name,
        grid_spec=pltpu.PrefetchScalarGridSpec(
            num_scalar_prefetch=1, grid=(r // tr, 4),
            in_specs=[pl.BlockSpec((None, tr, l), lambda i, k, ck: (2 * block(k, ck) + ck[0], i, 0)), spec],
            out_specs=[pl.BlockSpec((tr, l), lambda i, k, ck: (i, 0)), spec]),
        out_shape=[_sds((r, l), F32), _sds((4, r, l), BF16)], compiler_params=_params(2, 32),
    )(ck_arr, gp, recv)


def _adam_update(w, gv, m, v):
    nm = ADAM_B1 * m + (1.0 - ADAM_B1) * gv
    nv = ADAM_B2 * v + (1.0 - ADAM_B2) * (gv * gv)
    m_hat = nm / (1.0 - ADAM_B1 ** ADAM_STEP)
    v_hat = nv / (1.0 - ADAM_B2 ** ADAM_STEP)
    return -ADAM_LR * (m_hat / (jnp.sqrt(v_hat) + ADAM_EPS) + ADAM_WD * w), nm, nv


def _rs_finish_adamw(pf, recv, w, m, v, after, name):
    r, l = pf.shape
    tr = _row_tile(r)

    def body(_, p_ref, r_ref, w_ref, m_ref, v_ref, g_ref, d_ref, nm_ref, nv_ref):
        gv = ((p_ref[...] + r_ref[0].astype(F32)) + r_ref[1].astype(F32)) + r_ref[2].astype(F32)
        g_ref[...] = gv
        d_ref[...], nm_ref[...], nv_ref[...] = _adam_update(w_ref[...], gv, m_ref[...], v_ref[...])

    spec = pl.BlockSpec((tr, l), lambda i: (i, 0))
    return _pallas_call(
        body, name=name, grid=(r // tr,),
        in_specs=[ANY, spec, pl.BlockSpec((3, tr, l), lambda i: (0, i, 0)), spec, spec, spec], out_specs=[spec] * 4,
        out_shape=[_plain((r, l), F32)] * 4, compiler_params=_params(1, 32),
    )(after, pf, recv, w, m, v)


SMALL_ROWS = dict(norm_g=(0, 2), mem_norm_g=(2, 4), final_g=(4, 5), conv_w=(5, 8))
LOSS_ROWS = (8, 16)


def _sum_adamw_small(g, ck_arr, states):
    names = list(SMALL_ROWS)
    n_dev, n_rows, _ = g.shape

    def body(ck_ref, g_ref, gc_ref, *refs):
        ins, loss_ref, outs = refs[:3 * len(names)], refs[3 * len(names)], refs[3 * len(names) + 1:]

        def total(ref, lo, hi):
            acc = ref[0, lo:hi, :]
            for j in range(1, n_dev):
                acc = acc + ref[j, lo:hi, :]
            return acc

        loss_ref[...] = total(gc_ref, *LOSS_ROWS)
        for i, n in enumerate(names):
            gv = total(gc_ref if n == "conv_w" else g_ref, *SMALL_ROWS[n])
            w_ref, m_ref, v_ref = ins[3 * i:3 * i + 3]
            g_out, d_out, nm_out, nv_out = outs[4 * i:4 * i + 4]
            g_out[...] = gv
            d_out[...], nm_out[...], nv_out[...] = _adam_update(w_ref[...], gv, m_ref[...], v_ref[...])

    flat = [t for n in names for t in states[n]]
    mine = pl.BlockSpec((n_dev, n_rows, LANES), lambda i, ck: (0, 0, 2 * ck[1] + ck[0]))
    res = _pallas_call(
        body, name="sum_adamw_small",
        grid_spec=pltpu.PrefetchScalarGridSpec(
            num_scalar_prefetch=1, grid=(1,),
            in_specs=[_whole(g.shape), mine] + [_whole(t.shape) for t in flat],
            out_specs=[_whole((SUBLANES, LANES))] + [_whole(states[n][0].shape) for n in names for _ in range(4)]),
        out_shape=[_plain((SUBLANES, LANES), F32)] + [_plain(states[n][0].shape, F32) for n in names for _ in range(4)],
        compiler_params=_params(1, 32),
    )(ck_arr, g, g, *flat)
    return res[0], {n: tuple(res[1 + 4 * i:5 + 4 * i]) for i, n in enumerate(names)}


def _finish(name, pf, recv, w, m, v, after):
    if name in ("attn_w_in", "conv_w_in"):
        res = _rs_finish_adamw(pf, recv, w.T, m.T, v.T, after, "rs_finish_adamw_" + name)
        return tuple(t.T for t in res)
    return _rs_finish_adamw(pf, recv, w, m, v, after, "rs_finish_adamw_" + name)


def kernel(x, mem, positions, norm_g, mem_norm_g, w_mem_kv, attn_w_in, attn_w_out, conv_w_in, conv_w, conv_w_out, final_g, loss_target, m_norm_g, m_mem_norm_g, m_w_mem_kv, m_attn_w_in, m_attn_w_out, m_conv_w_in, m_conv_w, m_conv_w_out, m_final_g, v_norm_g, v_mem_norm_g, v_w_mem_kv, v_attn_w_in, v_attn_w_out, v_conv_w_in, v_conv_w, v_conv_w_out, v_final_g):
    px, py, pc = _position()
    me = 4 * px + 2 * py + pc
    ck_arr = jnp.stack([pc, 2 * px + py]).astype(jnp.int32)
    x, mem, pos, tgt = x[0], mem[0], positions[0], loss_target[0]

    wg_in0 = _all_gather_relay(attn_w_in[0].astype(BF16), "gather_w_in0")
    def gather_pass(weights, after, name, first):
        _, lands = _split_wait(weights, _plan_gather_own, after, name + "_wait", first, 4)
        return _split_start([], lands, _plan_gather_pass, 3, after, name + "_pass_start")

    late = [attn_w_out[0].astype(BF16), w_mem_kv.astype(BF16).reshape(-1, w_mem_kv.shape[2]),
            jnp.pad(conv_w[0], ((0, 5), (0, 0))), conv_w_in[0].astype(BF16), conv_w_out[0].astype(BF16)]
    lands = [lax.dynamic_update_slice(lax.empty((N_DEV,) + t.shape, t.dtype), t[None], (me, 0, 0)) for t in late]
    (send_sems, recv_sems, _, lands), token = _split_start([], lands, _plan_gather_own, 4, wg_in0, "gather_late_start")
    rest0, conv_ws = (send_sems, recv_sems, [], lands[:3]), (send_sems, recv_sems, [], lands[3:])

    tabs = _rope_tables(pos)
    g0, g1 = norm_g[0:1], norm_g[1:2]

    hn0, qs, ks, vs, tabs_v, qm0, z0 = _inproj_attn(x, g0, wg_in0, tabs, token)
    os_, ls_ = [], []
    for j, d in enumerate(DILATIONS):
        if j == 2:
            rest0, token = gather_pass(rest0, ls_[1], "gather_rest", 0)
        o, l = _attn_fwd(qs[j], ks[j], vs[j], d, token)
        os_.append(o)
        ls_.append(l)

    conv_ws, token = gather_pass(conv_ws, ls_[2], "gather_conv", 3)
    _, (wg_out0, wg_kv, cw_all) = _split_wait(rest0, _plan_gather_pass, token, "gather_rest_pass_wait")
    cw = cw_all[:, 0:3].transpose(1, 0, 2).reshape(3, -1)
    kv = _memkv_fwd(mem, mem_norm_g, wg_kv)
    y0, y0_t, mo0, h1 = _post_attn(os_, ls_, qm0, kv[0], z0, x, wg_out0)
    _, (wg_in1, wg_out1) = _split_wait(conv_ws, _plan_gather_pass, h1, "gather_conv_pass_wait")
    w_out1 = wg_out1.reshape(-1, wg_out1.shape[2])

    hn1, bg, cg, u, qm1, z1 = _inproj_conv(h1, g1, wg_in1)
    y1, y1_t, mo1, dh2, dh2b, loss_acc, d_final_g = _post_conv_loss(
        bg, cg, u, qm1, kv[1], z1, h1, w_out1, cw, final_g.reshape(1, -1), tgt)

    d_w_out1 = _wgrad_rows(y1_t, dh2b, "wgrad_out1")
    dz1, dbg, dconv, dqm1, dkv1 = _bwd_post_conv(dh2b, w_out1, bg, cg, u, z1, qm1, kv[1], mo1, cw)
    dcg, du, dcw = _bwd_conv(dconv, cg, u, cw)
    dh1, dg1, dh1b, dproj1_t = _dgrad_norm([(dbg, 1), (dcg, 1), (du, 1), (dqm1, 1), (dz1, 1)], wg_in1, h1, g1, dh2,
                                           dh2, True, "dgrad_norm_conv")
    d_w_in1 = _wgrad_shards_t(dproj1_t, hn1, wg_in1.shape[2], "wgrad_in1")

    d_w_out0 = _wgrad_cols(y0_t, dh1b, wg_out0.shape[2], "wgrad_out0")

    gw = GROUP_WIDTH
    ones = (jnp.arange(gw)[:, None] // HEAD_DIM == jnp.arange(gw)[None, :] // HEAD_DIM).astype(BF16)
    res = _bwd_post_attn(dh1b, wg_out0, z0, os_, ls_, qm0, kv[0], mo0, ones, dg1)
    dz0, dos, dls, dqm0, dkv0 = res[0], res[1:4], res[4:7], res[7], res[8]
    d_w_kv, d_mem_g = _memkv_bwd([dkv0, dkv1], wg_kv, mem, mem_norm_g)

    names1 = ["conv_w_in", "conv_w_out", "attn_w_out", "w_mem_kv"]
    grads1 = [d_w_in1, d_w_out1, d_w_out0, d_w_kv]
    started, token = _split_start(grads1, [lax.empty((4,) + g.shape[1:], g.dtype) for g in grads1],
                                  _plan_to_sibling, 4, d_mem_g, "rs1_sibling_start")

    dqs, dks, dvs = [], [], []
    for j, d in enumerate(DILATIONS):
        if j == 1:
            grads1, from_sibling = _split_wait(started, _plan_to_sibling, dqs[0][0], "rs1_sibling_wait")
            parts1 = [_rs_add_sibling(g, r, ck_arr, "rs_add_sibling_" + n)
                      for g, r, n in zip(grads1, from_sibling, names1)]
            pbs1 = [pb for _, pb in parts1]
            started, token = _split_start(pbs1, [lax.empty((3,) + p.shape[1:], p.dtype) for p in pbs1],
                                          _plan_to_chips, 3, dg1, "rs1_chips_start")
        dq, dk, dv = _attn_bwd(qs[j], ks[j], vs[j], ls_[j], dos[j], dls[j], tabs_v[j], d, token)
        dqs.append((dq, d))
        dks.append((dk, d))
        dvs.append((dv, d))
    pieces0 = dqs + dks + dvs + [(dqm0, 1), (dz0, 1)]
    dproj0_t = _assemble_dproj_t(pieces0, N_DEV * wg_in0.shape[2], "assemble_dproj_attn")
    d_w_in0 = _wgrad_shards_t(dproj0_t, hn0, wg_in0.shape[2], "wgrad_in0")

    names0 = ["attn_w_in"]
    grads0 = [d_w_in0]
    started0, token0 = _split_start(grads0, [lax.empty((4,) + g.shape[1:], g.dtype) for g in grads0],
                                    _plan_to_sibling, 4, dg1, "rs0_sibling_start")
    _, from_chips1 = _split_wait(started, _plan_to_chips, token0, "rs1_chips_wait")
    shard = dict(attn_w_in=(attn_w_in[0], m_attn_w_in[0], v_attn_w_in[0]),
                 attn_w_out=(attn_w_out[0], m_attn_w_out[0], v_attn_w_out[0]),
                 conv_w_in=(conv_w_in[0], m_conv_w_in[0], v_conv_w_in[0]),
                 conv_w_out=(conv_w_out[0], m_conv_w_out[0], v_conv_w_out[0]),
                 w_mem_kv=tuple(t.reshape(-1, t.shape[2]) for t in (w_mem_kv, m_w_mem_kv, v_w_mem_kv)))
    finish1 = {n: (pf, r) for n, (pf, _), r in zip(names1, parts1, from_chips1)}
    big = {"conv_w_in": _finish("conv_w_in", *finish1["conv_w_in"], *shard["conv_w_in"], token0)}

    grads0, from_sibling = _split_wait(started0, _plan_to_sibling, big["conv_w_in"][1].T, "rs0_sibling_wait")
    parts0 = [_rs_add_sibling(g, r, ck_arr, "rs_add_sibling_" + n) for g, r, n in zip(grads0, from_sibling, names0)]
    pbs0 = [pb for _, pb in parts0]
    started0, token0 = _split_start(pbs0, [lax.empty((3,) + p.shape[1:], p.dtype) for p in pbs0],
                                    _plan_to_chips, 3, dg1, "rs0_chips_start")
    dx, dg0 = _dgrad_norm(pieces0, wg_in0, x, g0, dh1, token0, False, "dgrad_norm_attn")
    for n in names1[1:]:
        big[n] = _finish(n, *finish1[n], *shard[n], token0)

    small_part = jnp.concatenate([dg0, dg1, d_mem_g.reshape(2, -1), d_final_g, dcw[0:3]], axis=0)
    small_part = jnp.concatenate([small_part, jnp.broadcast_to(loss_acc[0, 0], small_part.shape)], axis=0)
    small_w = dict(norm_g=(norm_g, m_norm_g, v_norm_g), mem_norm_g=(mem_norm_g, m_mem_norm_g, v_mem_norm_g),
                   conv_w=(conv_w, m_conv_w, v_conv_w), final_g=(final_g, m_final_g, v_final_g))
    loss_tile, small_res = _sum_adamw_small(
        _all_gather([small_part], [big[n][1] for n in names1[1:]], "gather_small_grads")[0], ck_arr,
        {n: tuple(t.reshape(-1, t.shape[-1]) for t in wmv) for n, wmv in small_w.items()})
    loss = loss_tile[0, 0]
    for n, wmv in small_w.items():
        big[n] = tuple(t.reshape(wmv[0].shape) for t in small_res[n])

    _, from_chips0 = _split_wait(started0, _plan_to_chips, big["final_g"][1], "rs0_chips_wait")
    for n, (pf, _), r in zip(names0, parts0, from_chips0):
        big[n] = _finish(n, pf, r, *shard[n], big["final_g"][1])
    for n in ("attn_w_in", "attn_w_out", "conv_w_in", "conv_w_out"):
        big[n] = tuple(t[None] for t in big[n])
    big["w_mem_kv"] = tuple(t.reshape(w_mem_kv.shape) for t in big["w_mem_kv"])

    order = ["norm_g", "mem_norm_g", "w_mem_kv", "attn_w_in", "attn_w_out", "conv_w_in", "conv_w", "conv_w_out", "final_g"]
    return (loss, dx[None], *[big[n][0] for n in order], *[big[n][1] for n in order],
            *[big[n][2] for n in order], *[big[n][3] for n in order])
```

```python
import jax
import jax.numpy as jnp
from jax import lax
from jax.experimental import pallas as pl
from jax.experimental.pallas import tpu as pltpu

F32 = jnp.float32
BF16 = jnp.bfloat16

N_DEV = 8
D_MODEL = 1024
HEAD_DIM = 64
ROT_DIM = HEAD_DIM // 4
ROPE_THETA = 500000.0
DILATIONS = (1, 4, 16)
HEADS_PER_GROUP = 8
GROUP_WIDTH = HEADS_PER_GROUP * HEAD_DIM
BLOCK = 128
N_MEM = 256
MEM_HEADS = 4
MEM_WIDTH = MEM_HEADS * HEAD_DIM
CONV_WIDTH = D_MODEL
EPS = 1e-6
SCALE = HEAD_DIM ** -0.5
NEG = -1e30

ADAM_LR = 0.001
ADAM_B1 = 0.9
ADAM_B2 = 0.999
ADAM_EPS = 1e-08
ADAM_WD = 0.01
ADAM_STEP = 10

ROW_TILE = 256
WGRAD_SHARDS = 4
LANES = 128
MESH = pl.DeviceIdType.MESH
ANY = pl.BlockSpec(memory_space=pl.ANY)


def _pallas_call(body, **kw):
    call = pl.pallas_call(body, **kw)

    def run(*args):
        pinned = [pltpu.with_memory_space_constraint(a, pltpu.HBM) if jnp.issubdtype(a.dtype, jnp.floating) else a
                  for a in args]
        return call(*pinned)

    return run


def _dot(a, b):
    return lax.dot_general(a, b, (((1,), (0,)), ((), ())), preferred_element_type=F32)


def _dot_nt(a, b):
    return lax.dot_general(a, b, (((1,), (1,)), ((), ())), preferred_element_type=F32)


def _dot_tn(a, b):
    return lax.dot_general(a, b, (((0,), (0,)), ((), ())), preferred_element_type=F32)


def _params(n_grid, vmem_mb=48):
    return pltpu.CompilerParams(dimension_semantics=("arbitrary",) * n_grid, vmem_limit_bytes=vmem_mb << 20)


def _rows(width, tm=ROW_TILE):
    return pl.BlockSpec((tm, width), lambda i: (i, 0))


def _view_rows(width, d, tm=ROW_TILE):
    return pl.BlockSpec((tm // d, d * width), lambda i: (i, 0))


def _whole(shape):
    return pl.BlockSpec(shape, lambda *_: (0,) * len(shape))


def _resident(shape):
    return pl.BlockSpec(shape, lambda *_: (0,) * len(shape), pipeline_mode=pl.Buffered(1))


def _sds(shape, dtype):
    return pltpu.HBM(shape, dtype)


def _plain(shape, dtype):
    return jax.ShapeDtypeStruct(shape, dtype)


def _silu_parts(z):
    sg = jax.nn.sigmoid(z)
    return z * sg, sg * (1.0 + z * (1.0 - sg))


def _to_view(scr, val, out_ref, d):
    tm, w = val.shape
    if d == 1:
        out_ref[...] = val.astype(out_ref.dtype)
        return
    for cb in range(w // LANES):
        scr[cb] = val[:, cb * LANES:(cb + 1) * LANES]
    for r in range(d):
        for cb in range(w // LANES):
            lo = r * w + cb * LANES
            out_ref[:, lo:lo + LANES] = scr[cb, pl.ds(r, tm // d, stride=d), :].astype(out_ref.dtype)


def _from_view(scr, in_ref, d):
    if d == 1:
        return in_ref[...].astype(F32)
    nc, tm, _ = scr.shape
    w = nc * LANES
    for r in range(d):
        for cb in range(nc):
            lo = r * w + cb * LANES
            scr[cb, pl.ds(r, tm // d, stride=d), :] = in_ref[:, lo:lo + LANES].astype(F32)
    return jnp.concatenate([scr[cb] for cb in range(nc)], axis=1)


def _view_scratch(width, tm=ROW_TILE):
    return pltpu.VMEM((width // LANES, tm, LANES), F32)


def _rope_tables(pos):
    half = ROT_DIM // 2
    inv_freq = ROPE_THETA ** (-jnp.arange(half, dtype=F32) * (2.0 / ROT_DIM))
    ang = pos.astype(F32)[:, None] * inv_freq
    cos, sin = jnp.cos(ang), jnp.sin(ang)
    s = pos.shape[0]
    z8 = jnp.zeros((s, half), F32)
    rest = HEAD_DIM - ROT_DIM
    cosf = jnp.concatenate([cos, cos, jnp.ones((s, rest), F32)], axis=1)
    sa = jnp.concatenate([-sin, z8, jnp.zeros((s, rest), F32)], axis=1)
    sb = jnp.concatenate([z8, sin, jnp.zeros((s, rest), F32)], axis=1)
    return tuple(jnp.tile(t, (1, LANES // HEAD_DIM)) for t in (cosf, sa, sb))


def _rope_fwd(t, cv, sav, sbv):
    w = t.shape[1]
    return t * cv + pltpu.roll(t, w - ROT_DIM // 2, 1) * sav + pltpu.roll(t, ROT_DIM // 2, 1) * sbv


def _rope_bwd(g, cv, sav, sbv):
    w = g.shape[1]
    return g * cv + pltpu.roll(g * sav, ROT_DIM // 2, 1) + pltpu.roll(g * sbv, w - ROT_DIM // 2, 1)


def _joined_columns(wg_ref):
    assert wg_ref.shape[2] % LANES == 0
    return jnp.concatenate([wg_ref[j] for j in range(N_DEV)], axis=1)


def _join_once(wg_ref, w_scr):
    c = wg_ref.shape[2]

    @pl.when(pl.program_id(0) == 0)
    def _():
        for j in range(N_DEV):
            w_scr[:, j * c:(j + 1) * c] = wg_ref[j]


def _inproj_attn(x, g, wg, tabs, token):
    s, d_model = x.shape
    gw = GROUP_WIDTH
    n = N_DEV * wg.shape[2]
    nz = n - 9 * gw - MEM_WIDTH
    reps = gw // LANES
    tm = ROW_TILE

    def body(_, x_ref, g_ref, w_ref, c_ref, sa_ref, sb_ref, hn_ref, *rest):
        outs, (wj, scr, tscr) = rest[:-3], rest[-3:]
        q_refs, k_refs, v_refs, t_refs, qm_ref, z_ref = outs[0:3], outs[3:6], outs[6:9], outs[9:18], outs[18], outs[19]
        _join_once(w_ref, wj)
        xb = x_ref[...]
        r = lax.rsqrt(jnp.mean(xb * xb, axis=-1, keepdims=True) + EPS)
        hn = ((xb * r) * g_ref[...]).astype(BF16)
        hn_ref[...] = hn
        proj = lambda lo, hi: _dot(hn, wj[:, lo:hi])
        tab = (c_ref[...], sa_ref[...], sb_ref[...])
        cv, sav, sbv = [jnp.tile(t, (1, reps)) for t in tab]
        for j, d in enumerate(DILATIONS):
            tq = _rope_fwd(proj(j * gw, (j + 1) * gw), cv, sav, sbv)
            _to_view(scr, tq * SCALE, q_refs[j], d)
            tk = _rope_fwd(proj((3 + j) * gw, (4 + j) * gw), cv, sav, sbv)
            _to_view(scr, tk, k_refs[j], d)
            _to_view(scr, proj((6 + j) * gw, (7 + j) * gw), v_refs[j], d)
            for i in range(3):
                _to_view(tscr, tab[i], t_refs[3 * j + i], d)
        qm_ref[...] = proj(9 * gw, 9 * gw + MEM_WIDTH).astype(BF16)
        z_ref[...] = proj(9 * gw + MEM_WIDTH, n)

    views = [_sds((s // d, d * gw), BF16) for d in DILATIONS]
    tviews = [_sds((s // d, d * LANES), F32) for d in DILATIONS for _ in range(3)]
    out_shape = [_sds((s, d_model), BF16)] + views * 3 + tviews + [_sds((s, MEM_WIDTH), BF16), _sds((s, nz), F32)]
    vspecs = [_view_rows(gw, d, tm) for d in DILATIONS]
    tspecs = [_view_rows(LANES, d, tm) for d in DILATIONS for _ in range(3)]
    out_specs = [_rows(d_model, tm)] + vspecs * 3 + tspecs + [_rows(MEM_WIDTH, tm), _rows(nz, tm)]
    res = _pallas_call(
        body, name="inproj_attn", grid=(s // tm,), out_shape=out_shape,
        in_specs=[ANY, _rows(d_model, tm), _whole((1, d_model)), _resident(wg.shape)] + [_rows(LANES, tm)] * 3,
        out_specs=out_specs,
        scratch_shapes=[pltpu.VMEM((d_model, n), BF16), _view_scratch(gw, tm), _view_scratch(LANES, tm)],
        compiler_params=_params(1, 60),
    )(token, x, g, wg, *tabs)
    tabs_v = [res[10 + 3 * j:13 + 3 * j] for j in range(3)]
    return res[0], res[1:4], res[4:7], res[7:10], tabs_v, res[19], res[20]


def _band_mask(n_keys):
    qi = lax.broadcasted_iota(jnp.int32, (BLOCK, n_keys), 0)
    kj = lax.broadcasted_iota(jnp.int32, (BLOCK, n_keys), 1)
    if n_keys == BLOCK:
        return kj <= qi
    return jnp.logical_or(jnp.logical_and(kj < BLOCK, kj >= qi), jnp.logical_and(kj >= BLOCK, (kj - BLOCK) <= qi))


def _low_head_lanes():
    return lax.broadcasted_iota(jnp.int32, (1, LANES), 1) < HEAD_DIM


def _split_pair(t, low):
    zero = jnp.zeros_like(t)
    return jnp.where(low, t, zero), jnp.where(low, zero, t)


def _pair_specs(d, nb, w):
    if nb == 1:
        return pl.BlockSpec((BLOCK, 2 * w), lambda n: (0, n)), None
    half = nb // 2
    two = pl.BlockSpec((2 * BLOCK, w), lambda n: (n % half, n // half))
    before = pl.BlockSpec((BLOCK, w), lambda n: (jnp.maximum(2 * (n % half) - 1, 0), n // half))
    return two, before


def _head_tiles(w, col0):
    return ([slice(p * LANES, (p + 1) * LANES) for p in range(w // LANES)],
            [slice(col0 + p * LANES, col0 + (p + 1) * LANES) for p in range(w // LANES)])


def _attend_fwd(q_ref, o_ref, lse_ref, rows, col0, kk, vv):
    w = kk.shape[1]
    valid = _band_mask(kk.shape[0])
    low = _low_head_lanes()
    pairs, qcols = _head_tiles(w, col0)
    qs_ = [h for qc in qcols for h in _split_pair(q_ref[rows, qc], low)]
    k2s = [kk[:, pr] for pr in pairs for _ in range(2)]
    scs = [jnp.where(valid, _dot_nt(qh, k2), NEG) for qh, k2 in zip(qs_, k2s)]
    ms = [jnp.max(sc, axis=-1, keepdims=True) for sc in scs]
    ps = [jnp.exp(sc - m) for sc, m in zip(scs, ms)]
    ls = [jnp.sum(p, axis=-1, keepdims=True) for p in ps]
    pns = [(p * (1.0 / l)).astype(BF16) for p, l in zip(ps, ls)]
    for i, (pr, qc) in enumerate(zip(pairs, qcols)):
        v2 = vv[:, pr]
        a, b = 2 * i, 2 * i + 1
        o_ref[rows, qc] = jnp.where(low, _dot(pns[a], v2), _dot(pns[b], v2))
        lse_ref[rows, qc] = jnp.where(low, ms[a] + jnp.log(ls[a]), ms[b] + jnp.log(ls[b]))


TOP, BOTTOM = slice(0, BLOCK), slice(BLOCK, 2 * BLOCK)


def _attn_fwd(q, k, v, d, token):
    ln, dw = q.shape
    w = dw // d
    nb = ln // BLOCK
    two, before = _pair_specs(d, nb, w)

    def body_streams(_, q_ref, kc_ref, vc_ref, o_ref, lse_ref):
        for sb in range(2):
            cols = slice(sb * w, (sb + 1) * w)
            _attend_fwd(q_ref, o_ref, lse_ref, TOP, sb * w, kc_ref[:, cols], vc_ref[:, cols])

    def body_blocks(_, q_ref, kp_ref, kc_ref, vp_ref, vc_ref, o_ref, lse_ref):
        first = pl.program_id(0) % (nb // 2) == 0
        pl.when(first)(lambda: _attend_fwd(q_ref, o_ref, lse_ref, TOP, 0, kc_ref[TOP, :], vc_ref[TOP, :]))
        pl.when(jnp.logical_not(first))(lambda: _attend_fwd(
            q_ref, o_ref, lse_ref, TOP, 0, jnp.concatenate([kp_ref[...], kc_ref[TOP, :]], axis=0),
            jnp.concatenate([vp_ref[...], vc_ref[TOP, :]], axis=0)))
        _attend_fwd(q_ref, o_ref, lse_ref, BOTTOM, 0, kc_ref[...], vc_ref[...])

    if nb == 1:
        body, in_specs, args = body_streams, [ANY, two, two, two], (token, q, k, v)
    else:
        body, in_specs, args = body_blocks, [ANY, two, before, two, before, two], (token, q, k, k, v, v)
    return _pallas_call(
        body, name=f"attn_fwd_d{d}", grid=(d * nb // 2,), out_shape=[_sds((ln, dw), F32)] * 2,
        in_specs=in_specs, out_specs=[two, two], compiler_params=_params(1, 32),
    )(*args)


def _memkv_fwd(mem, g, w):
    n_layers = g.shape[0]
    rows = D_MODEL // N_DEV

    def body(mem_ref, g_ref, w_ref, *kv_refs):
        mb = mem_ref[...]
        r = lax.rsqrt(jnp.mean(mb * mb, axis=-1, keepdims=True) + EPS)
        mn = ((mb * r) * g_ref[...]).astype(BF16)
        kv = _dot(mn, w_ref[...].reshape(D_MODEL, 2 * MEM_WIDTH)).astype(BF16)
        for layer, kv_ref in enumerate(kv_refs):
            @pl.when(pl.program_id(0) == layer)
            def _(kv_ref=kv_ref):
                kv_ref[...] = kv

    return _pallas_call(
        body, name="memkv_fwd", grid=(n_layers,),
        out_shape=[_sds((N_MEM, 2 * MEM_WIDTH), BF16)] * n_layers,
        in_specs=[_whole(mem.shape), pl.BlockSpec((None, 1, D_MODEL), lambda l: (l, 0, 0)),
                  pl.BlockSpec((N_DEV, rows, 2 * MEM_WIDTH), lambda l: (0, l, 0))],
        out_specs=[_whole((N_MEM, 2 * MEM_WIDTH))] * n_layers,
        compiler_params=_params(1, 32),
    )(mem, g.reshape(n_layers, 1, D_MODEL), w)


def _mix_groups(os_, ls_):
    mx = jnp.maximum(jnp.maximum(ls_[0], ls_[1]), ls_[2])
    es = [jnp.exp(t - mx) for t in ls_]
    inv = 1.0 / (es[0] + es[1] + es[2])
    ws = [e * inv for e in es]
    mix = ws[0] * os_[0] + ws[1] * os_[1] + ws[2] * os_[2]
    return ws, mix


MEM_PAIRS = [slice(p * LANES, (p + 1) * LANES) for p in range(MEM_WIDTH // LANES)]


def _mem_probs(qhs, k2s):
    scs = [_dot_nt(qh, k2) * SCALE for qh, k2 in zip(qhs, k2s)]
    es = [jnp.exp(sc - jnp.max(sc, axis=-1, keepdims=True)) for sc in scs]
    return [e * (1.0 / jnp.sum(e, axis=-1, keepdims=True)) for e in es]


def _mem_attn_into(qm, kv_ref, mo_ref):
    low = _low_head_lanes()
    qhs = [h for pr in MEM_PAIRS for h in _split_pair(qm[:, pr], low)]
    k2s = [kv_ref[:, pr] for pr in MEM_PAIRS for _ in range(2)]
    ps = [p.astype(BF16) for p in _mem_probs(qhs, k2s)]
    for i, pr in enumerate(MEM_PAIRS):
        v2 = kv_ref[:, MEM_WIDTH + i * LANES:MEM_WIDTH + (i + 1) * LANES]
        mo_ref[:, pr] = jnp.where(low, _dot(ps[2 * i], v2), _dot(ps[2 * i + 1], v2))


def _mem_attn_bwd(qm, kv_ref, dmem, dqm_ref, dkv_ref):
    low = _low_head_lanes()
    dmb = dmem.astype(BF16)
    vps = [slice(MEM_WIDTH + i * LANES, MEM_WIDTH + (i + 1) * LANES) for i in range(len(MEM_PAIRS))]
    qhs = [h for pr in MEM_PAIRS for h in _split_pair(qm[:, pr], low)]
    dhs = [h for pr in MEM_PAIRS for h in _split_pair(dmb[:, pr], low)]
    k2s = [kv_ref[:, pr] for pr in MEM_PAIRS for _ in range(2)]
    v2s = [kv_ref[:, vp] for vp in vps for _ in range(2)]
    ps = _mem_probs(qhs, k2s)
    dps = [_dot_nt(dh, v2) for dh, v2 in zip(dhs, v2s)]
    dss = [(p * (dp - jnp.sum(dp * p, axis=-1, keepdims=True)) * SCALE).astype(BF16) for p, dp in zip(ps, dps)]
    pbs = [p.astype(BF16) for p in ps]
    for i, (pr, vp) in enumerate(zip(MEM_PAIRS, vps)):
        a, b = 2 * i, 2 * i + 1
        dqm_ref[:, pr] = jnp.where(low, _dot(dss[a], k2s[a]), _dot(dss[b], k2s[b])).astype(BF16)
        dkv_ref[:, pr] += _dot_tn(dss[a], qhs[a]) + _dot_tn(dss[b], qhs[b])
        dkv_ref[:, vp] += _dot_tn(pbs[a], dhs[a]) + _dot_tn(pbs[b], dhs[b])


def _post_attn(os_, ls_, qm, kv, z, x, wg_out):
    s, d_model = x.shape
    gw = GROUP_WIDTH
    nb = gw + MEM_WIDTH
    tm = ROW_TILE

    def body(o0, o1, o2, l0, l1, l2, qm_ref, kv_ref, z_ref, x_ref, w_ref, y_ref, yt_ref, mo_ref, h_ref, s0, s1):
        ov, lv = [], []
        for o_ref, l_ref, d in zip((o0, o1, o2), (l0, l1, l2), DILATIONS):
            ov.append(_from_view(s0, o_ref, d))
            lv.append(_from_view(s1, l_ref, d))
        _, mix = _mix_groups(ov, lv)
        _mem_attn_into(qm_ref[...], kv_ref, mo_ref)
        sz, _ = _silu_parts(z_ref[...])
        y_ref[:, :gw] = (mix * sz[:, :gw]).astype(BF16)
        y_ref[:, gw:] = (mo_ref[...] * sz[:, gw:]).astype(BF16)
        y = y_ref[...]
        yt_ref[...] = y.T
        h_ref[...] = x_ref[...] + _dot(y, _joined_columns(w_ref))

    vspecs = [_view_rows(gw, d) for d in DILATIONS]
    return _pallas_call(
        body, name="post_attn", grid=(s // tm,),
        out_shape=[_sds((s, nb), BF16), _sds((nb, s), BF16), _sds((s, MEM_WIDTH), F32), _sds((s, d_model), F32)],
        in_specs=vspecs * 2 + [_rows(MEM_WIDTH), _whole(kv.shape), _rows(nb), _rows(d_model), _whole(wg_out.shape)],
        out_specs=[_rows(nb), pl.BlockSpec((nb, tm), lambda i: (0, i)), _rows(MEM_WIDTH), _rows(d_model)],
        scratch_shapes=[_view_scratch(gw), _view_scratch(gw)],
        compiler_params=_params(1, 40),
    )(*os_, *ls_, qm, kv, z, x, wg_out)


def _inproj_conv(x, g, wg):
    s, d_model = x.shape
    c = CONV_WIDTH
    n = N_DEV * wg.shape[2]
    nz = n - 3 * c - MEM_WIDTH
    tm = ROW_TILE

    def body(x_ref, g_ref, w_ref, hn_ref, bg_ref, cg_ref, u_ref, qm_ref, z_ref, wj):
        _join_once(w_ref, wj)
        xb = x_ref[...]
        r = lax.rsqrt(jnp.mean(xb * xb, axis=-1, keepdims=True) + EPS)
        hn = ((xb * r) * g_ref[...]).astype(BF16)
        hn_ref[...] = hn
        bg_ref[...] = _dot(hn, wj[:, 0:c])
        cg_ref[...] = _dot(hn, wj[:, c:2 * c])
        u_ref[...] = _dot(hn, wj[:, 2 * c:3 * c])
        qm_ref[...] = _dot(hn, wj[:, 3 * c:3 * c + MEM_WIDTH]).astype(BF16)
        z_ref[...] = _dot(hn, wj[:, 3 * c + MEM_WIDTH:])

    return _pallas_call(
        body, name="inproj_conv", grid=(s // tm,),
        out_shape=[_sds((s, d_model), BF16)] + [_sds((s, c), F32)] * 3 + [_sds((s, MEM_WIDTH), BF16), _sds((s, nz), F32)],
        in_specs=[_rows(d_model), _whole((1, d_model)), _resident(wg.shape)],
        out_specs=[_rows(d_model)] + [_rows(c)] * 3 + [_rows(MEM_WIDTH), _rows(nz)],
        scratch_shapes=[pltpu.VMEM((d_model, n), BF16)],
        compiler_params=_params(1, 60),
    )(x, g, wg)


HALO = 8


def _halo_before(width, tm=ROW_TILE):
    return pl.BlockSpec((HALO, width), lambda i: (jnp.maximum(i * (tm // HALO) - 1, 0), 0))


def _halo_after(width, n_rows, tm=ROW_TILE):
    return pl.BlockSpec((HALO, width), lambda i: (jnp.minimum((i + 1) * (tm // HALO), n_rows // HALO - 1), 0))


def _conv_taps(cg_ref, u_ref, cgh_ref, uh_ref, i):
    a = cg_ref[...] * u_ref[...]
    ah = jnp.where(i > 0, cgh_ref[...] * uh_ref[...], 0.0)
    row = lax.broadcasted_iota(jnp.int32, a.shape, 0)
    a1 = jnp.where(row == 0, ah[HALO - 1:HALO], pltpu.roll(a, 1, 0))
    a2 = jnp.where(row == 0, ah[HALO - 2:HALO - 1], jnp.where(row == 1, ah[HALO - 1:HALO], pltpu.roll(a, 2, 0)))
    return a, a1, a2


def _post_conv_loss(bg, cg, u, qm, kv, z, h1, w_out, cw, gf, tgt):
    s, d = h1.shape
    c = CONV_WIDTH
    nb = c + MEM_WIDTH
    tm = ROW_TILE

    def body(bg_ref, cg_ref, u_ref, cgh_ref, uh_ref, qm_ref, kv_ref, z_ref, h_ref, w_ref, cw_ref, gf_ref, t_ref,
             y_ref, yt_ref, mo_ref, dh_ref, dhb_ref, loss_ref, dgf_ref):
        i = pl.program_id(0)
        a, a1, a2 = _conv_taps(cg_ref, u_ref, cgh_ref, uh_ref, i)
        conv = cw_ref[0:1, :] * a2 + cw_ref[1:2, :] * a1 + cw_ref[2:3, :] * a
        mix = bg_ref[...] * conv
        _mem_attn_into(qm_ref[...], kv_ref, mo_ref)
        sz, _ = _silu_parts(z_ref[...])
        y_ref[:, :c] = (mix * sz[:, :c]).astype(BF16)
        y_ref[:, c:] = (mo_ref[...] * sz[:, c:]).astype(BF16)
        y = y_ref[...]
        yt_ref[...] = y.T
        h2 = h_ref[...] + _dot(y, w_ref[...])
        r = lax.rsqrt(jnp.mean(h2 * h2, axis=-1, keepdims=True) + EPS)
        nh = h2 * r
        gfv = gf_ref[...]
        diff = nh * gfv - t_ref[...]
        dout = diff * (1.0 / d)
        dn = dout * gfv
        dh2 = r * dn - h2 * ((r * r * r) * jnp.mean(dn * h2, axis=-1, keepdims=True))
        dh_ref[...] = dh2
        dhb_ref[...] = dh2.astype(BF16)

        @pl.when(i == 0)
        def _():
            loss_ref[...] = jnp.zeros_like(loss_ref)
            dgf_ref[...] = jnp.zeros_like(dgf_ref)

        loss_ref[...] += 0.5 * jnp.sum(jnp.mean(diff * diff, axis=-1, keepdims=True))
        dgf_ref[...] += jnp.sum(dout * nh, axis=0, keepdims=True)

    return _pallas_call(
        body, name="post_conv_loss", grid=(s // tm,),
        out_shape=[_sds((s, nb), BF16), _sds((nb, s), BF16), _sds((s, MEM_WIDTH), F32), _sds((s, d), F32),
                   _sds((s, d), BF16), _plain((8, LANES), F32), _plain((1, d), F32)],
        in_specs=[_rows(c)] * 3 + [_halo_before(c)] * 2 + [_rows(MEM_WIDTH), _whole(kv.shape), _rows(nb), _rows(d),
                  _whole(w_out.shape), _whole(cw.shape), _whole((1, d)), _rows(d)],
        out_specs=[_rows(nb), pl.BlockSpec((nb, tm), lambda i: (0, i)), _rows(MEM_WIDTH), _rows(d), _rows(d),
                   _whole((8, LANES)), _whole((1, d))],
        compiler_params=_params(1, 48),
    )(bg, cg, u, cg, u, qm, kv, z, h1, w_out, cw, gf, tgt)


def _bwd_post_conv(dhb, w_out, bg, cg, u, z, qm, kv, mo, cw):
    s = dhb.shape[0]
    c = CONV_WIDTH
    nb = c + MEM_WIDTH

    def body(dh_ref, w_ref, bg_ref, cg_ref, u_ref, cgh_ref, uh_ref, z_ref, qm_ref, kv_ref, mo_ref, cw_ref,
             dz_ref, dbg_ref, dc_ref, dqm_ref, dkv_ref):
        i = pl.program_id(0)

        @pl.when(i == 0)
        def _():
            dkv_ref[...] = jnp.zeros_like(dkv_ref)

        dy = _dot_nt(dh_ref[...], w_ref[...])
        sz, dsz = _silu_parts(z_ref[...])
        a, a1, a2 = _conv_taps(cg_ref, u_ref, cgh_ref, uh_ref, i)
        conv = cw_ref[0:1, :] * a2 + cw_ref[1:2, :] * a1 + cw_ref[2:3, :] * a
        bgv = bg_ref[...]
        dz_ref[:, :c] = (dy[:, :c] * (bgv * conv) * dsz[:, :c]).astype(BF16)
        dz_ref[:, c:] = (dy[:, c:] * mo_ref[...] * dsz[:, c:]).astype(BF16)
        dbr = dy * sz
        dmix = dbr[:, :c]
        dbg_ref[...] = (dmix * conv).astype(BF16)
        dc_ref[...] = dmix * bgv
        _mem_attn_bwd(qm_ref[...], kv_ref, dbr[:, c:], dqm_ref, dkv_ref)

    return _pallas_call(
        body, name="bwd_post_conv", grid=(s // ROW_TILE,),
        out_shape=[_sds((s, nb), BF16), _sds((s, c), BF16), _sds((s, c), F32), _sds((s, MEM_WIDTH), BF16),
                   _plain(kv.shape, F32)],
        in_specs=[_rows(D_MODEL), _whole(w_out.shape)] + [_rows(c)] * 3 + [_halo_before(c)] * 2
                 + [_rows(nb), _rows(MEM_WIDTH), _whole(kv.shape), _rows(MEM_WIDTH), _whole(cw.shape)],
        out_specs=[_rows(nb), _rows(c), _rows(c), _rows(MEM_WIDTH), _whole(kv.shape)],
        compiler_params=_params(1, 48),
    )(dhb, w_out, bg, cg, u, cg, u, z, qm, kv, mo, cw)


def _bwd_conv(dconv, cg, u, cw):
    s, c = dconv.shape
    tm = ROW_TILE
    last = s // tm - 1

    def body(dc_ref, dcn_ref, cg_ref, u_ref, cgh_ref, uh_ref, cw_ref, dcg_ref, du_ref, dcw_ref):
        i = pl.program_id(0)

        @pl.when(i == 0)
        def _():
            dcw_ref[...] = jnp.zeros_like(dcw_ref)

        dc = dc_ref[...]
        dcn = jnp.where(i < last, dcn_ref[...], 0.0)
        row = lax.broadcasted_iota(jnp.int32, dc.shape, 0)
        d1 = jnp.where(row == tm - 1, dcn[0:1], pltpu.roll(dc, tm - 1, 0))
        d2 = jnp.where(row == tm - 1, dcn[1:2], jnp.where(row == tm - 2, dcn[0:1], pltpu.roll(dc, tm - 2, 0)))
        da = cw_ref[2:3, :] * dc + cw_ref[1:2, :] * d1 + cw_ref[0:1, :] * d2
        a, a1, a2 = _conv_taps(cg_ref, u_ref, cgh_ref, uh_ref, i)
        dcg_ref[...] = (da * u_ref[...]).astype(BF16)
        du_ref[...] = (da * cg_ref[...]).astype(BF16)
        dcw_ref[0:1, :] += jnp.sum(dc * a2, axis=0, keepdims=True)
        dcw_ref[1:2, :] += jnp.sum(dc * a1, axis=0, keepdims=True)
        dcw_ref[2:3, :] += jnp.sum(dc * a, axis=0, keepdims=True)

    return _pallas_call(
        body, name="bwd_conv", grid=(s // tm,),
        out_shape=[_sds((s, c), BF16), _sds((s, c), BF16), _plain((8, c), F32)],
        in_specs=[_rows(c), _halo_after(c, s), _rows(c), _rows(c), _halo_before(c), _halo_before(c), _whole(cw.shape)],
        out_specs=[_rows(c), _rows(c), _whole((8, c))], compiler_params=_params(1, 40),
    )(dconv, dconv, cg, u, cg, u, cw)


def _assemble(p_refs, pieces, widths, dp, scr):
    off = 0
    for p_ref, (_, d), wd in zip(p_refs, pieces, widths):
        if d == 1:
            dp[:, off:off + wd] = p_ref[...]
        else:
            dp[:, off:off + wd] = _from_view(scr, p_ref, d).astype(BF16)
        off += wd


def _dgrad_norm(pieces, wg, h, g, dres, token, onward, name):
    s, d_model = h.shape
    n = N_DEV * wg.shape[2]
    tm = ROW_TILE
    widths = [p.shape[1] // d for p, d in pieces]
    assert sum(widths) == n
    n_p = len(pieces)

    def body(_, *refs):
        p_refs = refs[:n_p]
        w_ref, h_ref, g_ref, dr_ref, dh_ref, dg_ref = refs[n_p:n_p + 6]
        dp, scr, wj = refs[-3:]
        _join_once(w_ref, wj)

        @pl.when(pl.program_id(0) == 0)
        def _():
            dg_ref[...] = jnp.zeros_like(dg_ref)

        _assemble(p_refs, pieces, widths, dp, scr)
        dhn = _dot_nt(dp[...], wj[...])
        hb = h_ref[...]
        r = lax.rsqrt(jnp.mean(hb * hb, axis=-1, keepdims=True) + EPS)
        dg_ref[...] += jnp.sum(dhn * (hb * r), axis=0, keepdims=True)
        dn = dhn * g_ref[...]
        dh = dr_ref[...] + r * dn - hb * ((r * r * r) * jnp.mean(dn * hb, axis=-1, keepdims=True))
        dh_ref[...] = dh
        if onward:
            dhb_ref, dpt_ref = refs[n_p + 6:n_p + 8]
            dhb_ref[...] = dh.astype(BF16)
            dpt_ref[...] = dp[...].T

    p_specs = [_view_rows(wd, d) for (_, d), wd in zip(pieces, widths)]
    out_shape = [_plain((s, d_model), F32), _plain((1, d_model), F32)]
    out_specs = [_rows(d_model), _whole((1, d_model))]
    if onward:
        out_shape += [_sds((s, d_model), BF16), _sds((n, s), BF16)]
        out_specs += [_rows(d_model), pl.BlockSpec((n, tm), lambda i: (0, i))]
    return _pallas_call(
        body, name=name, grid=(s // tm,), out_shape=out_shape,
        in_specs=[ANY] + p_specs + [_resident(wg.shape), _rows(d_model), _whole((1, d_model)), _rows(d_model)],
        out_specs=out_specs,
        scratch_shapes=[pltpu.VMEM((tm, n), BF16), _view_scratch(GROUP_WIDTH), pltpu.VMEM((d_model, n), BF16)],
        compiler_params=_params(1, 60),
    )(token, *[p for p, _ in pieces], wg, h, g, dres)


def _assemble_dproj_t(pieces, n, name):
    tm = ROW_TILE
    widths = [p.shape[1] // d for p, d in pieces]
    assert sum(widths) == n
    s = pieces[0][0].shape[0] * pieces[0][1]
    n_p = len(pieces)

    def body(*refs):
        p_refs, (dpt_ref, dp, scr) = refs[:n_p], refs[n_p:]
        _assemble(p_refs, pieces, widths, dp, scr)
        dpt_ref[...] = dp[...].T

    return _pallas_call(
        body, name=name, grid=(s // tm,), out_shape=_sds((n, s), BF16),
        in_specs=[_view_rows(wd, d) for (_, d), wd in zip(pieces, widths)],
        out_specs=pl.BlockSpec((n, tm), lambda i: (0, i)),
        scratch_shapes=[pltpu.VMEM((tm, n), BF16), _view_scratch(GROUP_WIDTH)],
        compiler_params=_params(1, 40),
    )(*[p for p, _ in pieces])


def _wgrad_shards_t(dp_t, h, c, name):
    n, s = dp_t.shape
    d_model = h.shape[1]
    per_step = 2

    def body(a_ref, b_ref, o_ref):
        o_ref[...] = _dot(a_ref[...], b_ref[...]).astype(BF16).reshape(per_step, c, d_model)

    return _pallas_call(
        body, name=name, grid=(N_DEV // per_step,), out_shape=_sds((N_DEV, c, d_model), BF16),
        in_specs=[pl.BlockSpec((per_step * c, s), lambda j: (j, 0)), _resident(h.shape)],
        out_specs=pl.BlockSpec((per_step, c, d_model), lambda j: (j, 0, 0)), compiler_params=_params(1, 40),
    )(dp_t, h)


def _wgrad_cols(a_t, b, c, name):
    m, s = a_t.shape
    assert c % LANES == 0

    def body(a_ref, b_ref, o_ref):
        wide = _dot(a_ref[...], b_ref[...]).astype(BF16)
        for j in range(WGRAD_SHARDS):
            o_ref[j] = wide[:, j * c:(j + 1) * c]

    return _pallas_call(
        body, name=name, grid=(N_DEV // WGRAD_SHARDS,), out_shape=_sds((N_DEV, m, c), BF16),
        in_specs=[_whole(a_t.shape), pl.BlockSpec((s, WGRAD_SHARDS * c), lambda j: (0, j))],
        out_specs=pl.BlockSpec((WGRAD_SHARDS, m, c), lambda j: (j, 0, 0)), compiler_params=_params(1, 40),
    )(a_t, b)


def _wgrad_rows(a_t, b, name):
    m, s = a_t.shape
    n = b.shape[1]
    mr = m // N_DEV

    def body(a_ref, b_ref, o_ref):
        o_ref[...] = _dot(a_ref[...], b_ref[...]).astype(BF16).reshape(WGRAD_SHARDS, mr, n)

    return _pallas_call(
        body, name=name, grid=(N_DEV // WGRAD_SHARDS,), out_shape=_sds((N_DEV, mr, n), BF16),
        in_specs=[pl.BlockSpec((WGRAD_SHARDS * mr, s), lambda j: (j, 0)), _whole(b.shape)],
        out_specs=pl.BlockSpec((WGRAD_SHARDS, mr, n), lambda j: (j, 0, 0)), compiler_params=_params(1, 40),
    )(a_t, b)


def _memkv_bwd(dkvs, w, mem, g):
    n_layers = g.shape[0]
    rows = D_MODEL // N_DEV

    def body(dkv0_ref, dkv1_ref, w_ref, mem_ref, g_ref, dw_ref, dg_ref):
        mb = mem_ref[...]
        r = lax.rsqrt(jnp.mean(mb * mb, axis=-1, keepdims=True) + EPS)
        nm = mb * r
        mn = (nm * g_ref[...]).astype(BF16)
        dkvb = jnp.where(pl.program_id(0) == 0, dkv0_ref[...], dkv1_ref[...]).astype(BF16)
        dw_ref[...] = _dot_tn(mn, dkvb).astype(BF16).reshape(N_DEV, rows, 2 * MEM_WIDTH)
        dmn = _dot_nt(dkvb, w_ref[...].reshape(D_MODEL, 2 * MEM_WIDTH))
        dg_ref[...] = jnp.sum(dmn * nm, axis=0, keepdims=True)

    lay = lambda *shape: pl.BlockSpec((None,) + shape, lambda l: (l, 0, 0))
    major = pl.BlockSpec((N_DEV, rows, 2 * MEM_WIDTH), lambda l: (0, l, 0))
    return _pallas_call(
        body, name="memkv_bwd", grid=(n_layers,),
        out_shape=[_sds((N_DEV, n_layers * rows, 2 * MEM_WIDTH), BF16), _plain((n_layers, 1, D_MODEL), F32)],
        in_specs=[_whole(dkvs[0].shape), _whole(dkvs[1].shape), major, _whole(mem.shape), lay(1, D_MODEL)],
        out_specs=[major, lay(1, D_MODEL)],
        compiler_params=_params(1, 32),
    )(*dkvs, w, mem, g.reshape(n_layers, 1, D_MODEL))


def _bwd_post_attn(dhb, wg_out, z, os_, ls_, qm, kv, mo, head_ones, token):
    s = dhb.shape[0]
    gw = GROUP_WIDTH
    nb = gw + MEM_WIDTH
    tm = ROW_TILE

    def body(_, dh_ref, w_ref, z_ref, o0, o1, o2, l0, l1, l2, qm_ref, kv_ref, mo_ref, bd_ref,
             dz_ref, do0, do1, do2, dl0, dl1, dl2, dqm_ref, dkv_ref, s0, s1):
        @pl.when(pl.program_id(0) == 0)
        def _():
            dkv_ref[...] = jnp.zeros_like(dkv_ref)

        dy = _dot_nt(dh_ref[...], _joined_columns(w_ref))
        ov, lv = [], []
        for o_ref, l_ref, d in zip((o0, o1, o2), (l0, l1, l2), DILATIONS):
            ov.append(_from_view(s0, o_ref, d))
            lv.append(_from_view(s1, l_ref, d))
        ws, mix = _mix_groups(ov, lv)
        sz, dsz = _silu_parts(z_ref[...])
        dz_ref[:, :gw] = (dy[:, :gw] * mix * dsz[:, :gw]).astype(BF16)
        dz_ref[:, gw:] = (dy[:, gw:] * mo_ref[...] * dsz[:, gw:]).astype(BF16)
        dbr = dy * sz
        dmix = dbr[:, :gw]
        t = dmix * mix
        th = t.astype(BF16)
        tl = (t - th.astype(F32)).astype(BF16)
        rs = _dot(th, bd_ref[...]) + _dot(tl, bd_ref[...])
        for wg_, do_ref, dl_ref, d in zip(ws, (do0, do1, do2), (dl0, dl1, dl2), DILATIONS):
            _to_view(s0, wg_ * dmix, do_ref, d)
            _to_view(s1, wg_ * rs, dl_ref, d)
        _mem_attn_bwd(qm_ref[...], kv_ref, dbr[:, gw:], dqm_ref, dkv_ref)

    vspecs = [_view_rows(gw, d) for d in DILATIONS]
    return _pallas_call(
        body, name="bwd_post_attn", grid=(s // tm,),
        out_shape=[_sds((s, nb), BF16)] + [_sds((s // d, d * gw), BF16) for d in DILATIONS]
                  + [_sds((s // d, d * gw), F32) for d in DILATIONS] + [_sds((s, MEM_WIDTH), BF16), _plain(kv.shape, F32)],
        in_specs=[ANY, _rows(D_MODEL), _whole(wg_out.shape), _rows(nb)] + vspecs * 2
                 + [_rows(MEM_WIDTH), _whole(kv.shape), _rows(MEM_WIDTH), _whole(head_ones.shape)],
        out_specs=[_rows(nb)] + vspecs * 2 + [_rows(MEM_WIDTH), _whole(kv.shape)],
        scratch_shapes=[_view_scratch(gw), _view_scratch(gw)],
        compiler_params=_params(1, 48),
    )(token, dhb, wg_out, z, *os_, *ls_, qm, kv, mo, head_ones)


def _attn_bwd(q, k, v, lse, do, dl, tabs, d, token):
    ln, dw = q.shape
    w = dw // d
    nb = ln // BLOCK
    reps = w // LANES
    two, before = _pair_specs(d, nb, w)
    two_t, _ = _pair_specs(d, nb, LANES)

    def attend(q_ref, l_ref, do_ref, dl_ref, dqs, acck, accv, rows, col0, kk, vv, acc_rows):
        valid = _band_mask(kk.shape[0])
        low = _low_head_lanes()
        pairs, qcols = _head_tiles(w, col0)
        cols = [slice(col0 + h * HEAD_DIM, col0 + h * HEAD_DIM + 1) for h in range(HEADS_PER_GROUP)]
        qhs = [h for qc in qcols for h in _split_pair(q_ref[rows, qc], low)]
        dobs = [h for qc in qcols for h in _split_pair(do_ref[rows, qc], low)]
        k2s = [kk[:, pr] for pr in pairs for _ in range(2)]
        v2s = [vv[:, pr] for pr in pairs for _ in range(2)]
        scs = [jnp.where(valid, _dot_nt(qh, k2), NEG) for qh, k2 in zip(qhs, k2s)]
        dps = [_dot_nt(dob, v2) for dob, v2 in zip(dobs, v2s)]
        ps = [jnp.exp(sc - l_ref[rows, col]) for sc, col in zip(scs, cols)]
        dss = [(p * (dp - dl_ref[rows, col])).astype(BF16) for p, dp, col in zip(ps, dps, cols)]
        pbs = [p.astype(BF16) for p in ps]
        for i, qc in enumerate(qcols):
            a, b = 2 * i, 2 * i + 1
            dqs[rows, qc] = jnp.where(low, _dot(dss[a], k2s[a]), _dot(dss[b], k2s[b])) * SCALE
            acck[acc_rows, qc] += _dot_tn(dss[a], qhs[a]) + _dot_tn(dss[b], qhs[b])
            accv[acc_rows, qc] += _dot_tn(pbs[a], dobs[a]) + _dot_tn(pbs[b], dobs[b])

    def body_streams(_, q_ref, kc_ref, vc_ref, l_ref, do_ref, dl_ref, c_ref, sa_ref, sb_ref,
                     dq_ref, dk_ref, dv_ref, acck, accv, dqs):
        acck[...] = jnp.zeros_like(acck)
        accv[...] = jnp.zeros_like(accv)
        for sb in range(2):
            cols = slice(sb * w, (sb + 1) * w)
            attend(q_ref, l_ref, do_ref, dl_ref, dqs, acck, accv, TOP, sb * w, kc_ref[:, cols], vc_ref[:, cols], TOP)
        tabs2 = [jnp.concatenate([jnp.tile(r[:, sb * LANES:(sb + 1) * LANES], (1, reps)) for sb in range(2)], axis=1)
                 for r in (c_ref, sa_ref, sb_ref)]
        dq_ref[...] = _rope_bwd(dqs[...], *tabs2).astype(BF16)
        dk_ref[...] = _rope_bwd(acck[...], *tabs2).astype(BF16)
        dv_ref[...] = accv[...].astype(BF16)

    def body_blocks(_, q_ref, kp_ref, kc_ref, vp_ref, vc_ref, l_ref, do_ref, dl_ref, cq, saq, sbq, ck, sak, sbk,
                    dq_ref, dk_ref, dv_ref, acck, accv, dqs):
        i = pl.program_id(0) % (nb // 2)

        @pl.when(i == 0)
        def _():
            acck[...] = jnp.zeros_like(acck)
            accv[...] = jnp.zeros_like(accv)

        refs = (q_ref, l_ref, do_ref, dl_ref, dqs, acck, accv)
        pl.when(i == 0)(lambda: attend(*refs, TOP, 0, kc_ref[TOP, :], vc_ref[TOP, :], TOP))
        pl.when(i != 0)(lambda: attend(
            *refs, TOP, 0, jnp.concatenate([kp_ref[...], kc_ref[TOP, :]], axis=0),
            jnp.concatenate([vp_ref[...], vc_ref[TOP, :]], axis=0),
            pl.ds(pl.multiple_of((2 * i - 1) * BLOCK, BLOCK), 2 * BLOCK)))
        attend(*refs, BOTTOM, 0, kc_ref[...], vc_ref[...], pl.ds(pl.multiple_of(2 * i * BLOCK, BLOCK), 2 * BLOCK))
        tq = [jnp.tile(r[...], (1, reps)) for r in (cq, saq, sbq)]
        dq_ref[...] = _rope_bwd(dqs[...], *tq).astype(BF16)

        @pl.when(i == nb // 2 - 1)
        def _():
            for r0 in range(0, nb * BLOCK, 2 * BLOCK):
                rows = slice(r0, r0 + 2 * BLOCK)
                tk = [jnp.tile(r[rows, :], (1, reps)) for r in (ck, sak, sbk)]
                dk_ref[rows, :] = _rope_bwd(acck[rows, :], *tk).astype(BF16)
                dv_ref[rows, :] = accv[rows, :].astype(BF16)

    if nb == 1:
        body = body_streams
        in_specs = [ANY] + [two] * 6 + [two_t] * 3
        args = (token, q, k, v, lse, do, dl, *tabs)
        out_specs = [two, two, two]
        acc_shape = (BLOCK, 2 * w)
    else:
        body = body_blocks
        stream = pl.BlockSpec((nb * BLOCK, w), lambda n: (0, n // (nb // 2)))
        stream_t = pl.BlockSpec((nb * BLOCK, LANES), lambda n: (0, n // (nb // 2)))
        in_specs = [ANY, two, before, two, before, two, two, two, two] + [two_t] * 3 + [stream_t] * 3
        args = (token, q, k, k, v, v, lse, do, dl, *tabs, *tabs)
        out_specs = [two, stream, stream]
        acc_shape = (nb * BLOCK, w)
    return _pallas_call(
        body, name=f"attn_bwd_d{d}", grid=(d * nb // 2,), out_shape=[_sds((ln, dw), BF16)] * 3,
        in_specs=in_specs, out_specs=out_specs,
        scratch_shapes=[pltpu.VMEM(acc_shape, F32), pltpu.VMEM(acc_shape, F32), pltpu.VMEM(two.block_shape, F32)],
        compiler_params=_params(1, 48),
    )(*args)


def _position():
    return lax.axis_index("x"), lax.axis_index("y"), lax.axis_index("c")


def _all_gather_small(xs, afters, name):
    n_in = 1 + len(afters)

    def body(*refs):
        x_ref, out_ref = refs[0], refs[n_in]
        send_sems, recv_sems, local_sem = refs[n_in + 1:]
        x, y, c = _position()
        my_rows = out_ref.at[4 * x + 2 * y + c]
        mine = pltpu.make_async_copy(x_ref, my_rows, local_sem)
        mine.start()
        copies = [pltpu.make_async_remote_copy(
            src_ref=x_ref, dst_ref=my_rows, send_sem=send_sems.at[k], recv_sem=recv_sems.at[k],
            device_id=(1 - x if k & 4 else x, 1 - y if k & 2 else y, 1 - c if k & 1 else c), device_id_type=MESH)
            for k in range(1, N_DEV)]
        for cp in copies:
            cp.start()
        for cp in copies:
            cp.wait_recv()
        for cp in copies:
            cp.wait_send()
        mine.wait()

    return _pallas_call(
        body, name=name, out_shape=_sds((N_DEV,) + xs.shape, xs.dtype),
        in_specs=[ANY] * n_in, out_specs=ANY,
        scratch_shapes=[pltpu.SemaphoreType.DMA((N_DEV,)), pltpu.SemaphoreType.DMA((N_DEV,)), pltpu.SemaphoreType.DMA],
    )(xs, *afters)


def _all_gather_relay(xs, name):
    def body(x_ref, out_ref, send_sems, recv_sems, local_sem):
        x, y, c = _position()
        me, sibling = (x, y, c), (x, y, 1 - c)
        xn, yn, diag = (1 - x, y, c), (x, 1 - y, c), (1 - x, 1 - y, c)
        src_nb = (x + c * (1 - 2 * x), y + (1 - c) * (1 - 2 * y), c)
        dst_nb = (x + (1 - c) * (1 - 2 * x), y + c * (1 - 2 * y), c)

        def rows(dev):
            return out_ref.at[4 * dev[0] + 2 * dev[1] + dev[2]]

        def copy(k, block, to, own=False):
            return pltpu.make_async_remote_copy(
                src_ref=x_ref if own else rows(block), dst_ref=rows(block),
                send_sem=send_sems.at[k], recv_sem=recv_sems.at[k], device_id=to, device_id_type=MESH)

        mine = pltpu.make_async_copy(x_ref, rows(me), local_sem)
        mine.start()
        first = [copy(1, me, xn, own=True), copy(2, me, yn, own=True), copy(0, me, sibling, own=True)]
        for cp in first:
            cp.start()
        copy(1, xn, me).wait_recv()
        copy(2, yn, me).wait_recv()
        relay = copy(3, src_nb, dst_nb)
        relay.start()
        passed = [copy(4, xn, sibling), copy(5, yn, sibling)]
        for cp in passed:
            cp.start()
        copy(3, diag, me).wait_recv()
        last = copy(6, diag, sibling)
        last.start()
        copy(0, sibling, me).wait_recv()
        for k, blk in ((4, (1 - x, y, 1 - c)), (5, (x, 1 - y, 1 - c)), (6, (1 - x, 1 - y, 1 - c))):
            copy(k, blk, me).wait_recv()
        for cp in first + [relay] + passed + [last]:
            cp.wait_send()
        mine.wait()

    return _pallas_call(
        body, name=name, out_shape=_sds((N_DEV,) + xs.shape, xs.dtype),
        in_specs=[ANY], out_specs=ANY,
        scratch_shapes=[pltpu.SemaphoreType.DMA((7,)), pltpu.SemaphoreType.DMA((7,)), pltpu.SemaphoreType.DMA],
    )(xs)


HBM_SPEC = pl.BlockSpec(memory_space=pltpu.HBM)
SEM_SPEC = pl.BlockSpec(memory_space=pltpu.SEMAPHORE)
EFFECT = pltpu.SideEffectType.DATAFLOW_SIDE_EFFECTING
def _plan_gather_own(src_refs, land_refs):
    x, y, c = _position()
    me = 4 * x + 2 * y + c
    peers = [(x, y, 1 - c), (1 - x, y, c), (x, 1 - y, c), (1 - x, 1 - y, c)]
    return [(land_refs[a].at[me], land_refs[a].at[me], (a, k), peer)
            for k, peer in enumerate(peers) for a in range(len(land_refs))]


def _plan_gather_pass(src_refs, land_refs):
    x, y, c = _position()
    chips = [(1 - x, y), (x, 1 - y), (1 - x, 1 - y)]
    return [(land_refs[a].at[4 * px + 2 * py + c], land_refs[a].at[4 * px + 2 * py + c], (a, j), (x, y, 1 - c))
            for j, (px, py) in enumerate(chips) for a in range(len(land_refs))]


def _plan_to_sibling(src_refs, land_refs):
    x, y, c = _position()
    return [(src_refs[a].at[2 * k + (1 - c)], land_refs[a].at[k], (a, k), (x, y, 1 - c))
            for k in range(4) for a in range(len(src_refs))]


def _plan_to_chips(src_refs, land_refs):
    x, y, c = _position()
    chips = [(1 - x, y), (x, 1 - y), (1 - x, 1 - y)]
    return [(src_refs[a].at[2 * px + py], land_refs[a].at[j], (a, j), (px, py, c))
            for j, (px, py) in enumerate(chips) for a in range(len(src_refs))]


def _split_start(srcs, lands, plan, n_sem, after, name):
    n_s, n_a = len(srcs), len(lands)
    n_b = n_s + n_a

    def body(*refs):
        src_refs, land_refs = refs[:n_s], refs[n_s:n_b]
        send_sems, recv_sems, token = refs[n_b + 1], refs[n_b + 2], refs[-1]
        for src, dst, (a, k), dev in plan(src_refs, land_refs):
            i = a * n_sem + k
            pltpu.make_async_remote_copy(src_ref=src, dst_ref=dst, send_sem=send_sems.at[i], recv_sem=recv_sems.at[i],
                                         device_id=dev, device_id_type=MESH).start()
        token[...] = jnp.zeros_like(token)

    bufs = list(srcs) + list(lands)
    res = pl.pallas_call(
        body, name=name,
        out_shape=(pltpu.SemaphoreType.DMA((n_a * n_sem,)), pltpu.SemaphoreType.DMA((n_a * n_sem,)),
                   *[pltpu.HBM(t.shape, t.dtype) for t in bufs], _plain((8, LANES), F32)),
        in_specs=[HBM_SPEC] * n_b + [ANY],
        out_specs=(SEM_SPEC, SEM_SPEC, *[HBM_SPEC] * n_b, pl.BlockSpec(memory_space=pltpu.VMEM)),
        input_output_aliases={i: 2 + i for i in range(n_b)},
        compiler_params=pltpu.CompilerParams(has_side_effects=EFFECT),
    )(*[pltpu.with_memory_space_constraint(t, pltpu.HBM) for t in bufs], after)
    return (res[0], res[1], res[2:2 + n_s], res[2 + n_s:2 + n_b]), res[-1]


def _split_wait(started, plan, after, name, first=0, n_sem=None):
    send_sems, recv_sems, srcs, lands = started
    n_s, n_a = len(srcs), len(lands)
    n_b = n_s + n_a
    n_sem = n_sem or send_sems.shape[0] // n_a

    def body(*refs):
        src_refs, land_refs = refs[:n_s], refs[n_s:n_b]
        s_sems, r_sems = refs[n_b], refs[n_b + 1]
        for src, dst, (a, k), dev in plan(src_refs, land_refs):
            i = (first + a) * n_sem + k
            cp = pltpu.make_async_remote_copy(src_ref=src, dst_ref=dst, send_sem=s_sems.at[i], recv_sem=r_sems.at[i],
                                              device_id=dev, device_id_type=MESH)
            cp.wait_send()
            cp.wait_recv()

    bufs = list(srcs) + list(lands)
    res = pl.pallas_call(
        body, name=name, out_shape=tuple(pltpu.HBM(t.shape, t.dtype) for t in bufs),
        in_specs=[HBM_SPEC] * n_b + [SEM_SPEC, SEM_SPEC, ANY],
        out_specs=tuple([HBM_SPEC] * n_b),
        input_output_aliases={i: i for i in range(n_b)},
        compiler_params=pltpu.CompilerParams(has_side_effects=EFFECT),
    )(*bufs, send_sems, recv_sems, after)
    return res[:n_s], res[n_s:]


SUBLANES = 8


def _row_tile(r):
    return max(t for t in range(SUBLANES, ROW_TILE + 1, SUBLANES) if r % t == 0)


def _rs_add_sibling(gp, recv, ck_arr, name):
    _, r, l = gp.shape
    tr = r if r <= 4 * ROW_TILE else _row_tile(r)
    block = lambda k, ck: (k + ck[1] + 1) % 4

    def body(ck_ref, g_ref, r_ref, pf_ref, pb_ref):
        sm = g_ref[...].astype(F32) + r_ref[...].astype(F32)
        pf_ref[...] = sm
        pb_ref[...] = sm.astype(BF16)

    spec = pl.BlockSpec((None, tr, l), lambda i, k, ck: (block(k, ck), i, 0))
    return _pallas_call(
        body, name=name,
        grid_spec=pltpu.PrefetchScalarGridSpec(
            num_scalar_prefetch=1, grid=(r // tr, 4),
            in_specs=[pl.BlockSpec((None, tr, l), lambda i, k, ck: (2 * block(k, ck) + ck[0], i, 0)), spec],
            out_specs=[pl.BlockSpec((tr, l), lambda i, k, ck: (i, 0)), spec]),
        out_shape=[_sds((r, l), F32), _sds((4, r, l), BF16)], compiler_params=_params(2, 32),
    )(ck_arr, gp, recv)


def _adam_update(w, gv, m, v):
    nm = ADAM_B1 * m + (1.0 - ADAM_B1) * gv
    nv = ADAM_B2 * v + (1.0 - ADAM_B2) * (gv * gv)
    m_hat = nm / (1.0 - ADAM_B1 ** ADAM_STEP)
    v_hat = nv / (1.0 - ADAM_B2 ** ADAM_STEP)
    return -ADAM_LR * (m_hat / (jnp.sqrt(v_hat) + ADAM_EPS) + ADAM_WD * w), nm, nv


def _rs_finish_adamw(pf, recv, w, m, v, after, name):
    r, l = pf.shape
    tr = _row_tile(r)

    def body(_, p_ref, r_ref, w_ref, m_ref, v_ref, g_ref, d_ref, nm_ref, nv_ref):
        gv = ((p_ref[...] + r_ref[0].astype(F32)) + r_ref[1].astype(F32)) + r_ref[2].astype(F32)
        g_ref[...] = gv
        d_ref[...], nm_ref[...], nv_ref[...] = _adam_update(w_ref[...], gv, m_ref[...], v_ref[...])

    spec = pl.BlockSpec((tr, l), lambda i: (i, 0))
    return _pallas_call(
        body, name=name, grid=(r // tr,),
        in_specs=[ANY, spec, pl.BlockSpec((3, tr, l), lambda i: (0, i, 0)), spec, spec, spec], out_specs=[spec] * 4,
        out_shape=[_plain((r, l), F32)] * 4, compiler_params=_params(1, 32),
    )(after, pf, recv, w, m, v)


SMALL_ROWS = dict(norm_g=(0, 2), mem_norm_g=(2, 4), final_g=(4, 5), conv_w=(5, 8))
LOSS_ROWS = (8, 16)


def _sum_adamw_small(g, ck_arr, states):
    names = list(SMALL_ROWS)
    n_dev, n_rows, _ = g.shape

    def body(ck_ref, g_ref, gc_ref, *refs):
        ins, loss_ref, outs = refs[:3 * len(names)], refs[3 * len(names)], refs[3 * len(names) + 1:]

        def total(ref, lo, hi):
            acc = ref[0, lo:hi, :]
            for j in range(1, n_dev):
                acc = acc + ref[j, lo:hi, :]
            return acc

        loss_ref[...] = total(gc_ref, *LOSS_ROWS)
        for i, n in enumerate(names):
            gv = total(gc_ref if n == "conv_w" else g_ref, *SMALL_ROWS[n])
            w_ref, m_ref, v_ref = ins[3 * i:3 * i + 3]
            g_out, d_out, nm_out, nv_out = outs[4 * i:4 * i + 4]
            g_out[...] = gv
            d_out[...], nm_out[...], nv_out[...] = _adam_update(w_ref[...], gv, m_ref[...], v_ref[...])

    flat = [t for n in names for t in states[n]]
    mine = pl.BlockSpec((n_dev, n_rows, LANES), lambda i, ck: (0, 0, 2 * ck[1] + ck[0]))
    res = _pallas_call(
        body, name="sum_adamw_small",
        grid_spec=pltpu.PrefetchScalarGridSpec(
            num_scalar_prefetch=1, grid=(1,),
            in_specs=[_whole(g.shape), mine] + [_whole(t.shape) for t in flat],
            out_specs=[_whole((SUBLANES, LANES))] + [_whole(states[n][0].shape) for n in names for _ in range(4)]),
        out_shape=[_plain((SUBLANES, LANES), F32)] + [_plain(states[n][0].shape, F32) for n in names for _ in range(4)],
        compiler_params=_params(1, 32),
    )(ck_arr, g, g, *flat)
    return res[0], {n: tuple(res[1 + 4 * i:5 + 4 * i]) for i, n in enumerate(names)}


def _finish(name, pf, recv, w, m, v, after):
    if name in ("attn_w_in", "conv_w_in"):
        res = _rs_finish_adamw(pf, recv, w.T, m.T, v.T, after, "rs_finish_adamw_" + name)
        return tuple(t.T for t in res)
    return _rs_finish_adamw(pf, recv, w, m, v, after, "rs_finish_adamw_" + name)


def kernel(x, mem, positions, norm_g, mem_norm_g, w_mem_kv, attn_w_in, attn_w_out, conv_w_in, conv_w, conv_w_out, final_g, loss_target, m_norm_g, m_mem_norm_g, m_w_mem_kv, m_attn_w_in, m_attn_w_out, m_conv_w_in, m_conv_w, m_conv_w_out, m_final_g, v_norm_g, v_mem_norm_g, v_w_mem_kv, v_attn_w_in, v_attn_w_out, v_conv_w_in, v_conv_w, v_conv_w_out, v_final_g):
    px, py, pc = _position()
    me = 4 * px + 2 * py + pc
    ck_arr = jnp.stack([pc, 2 * px + py]).astype(jnp.int32)
    x, mem, pos, tgt = x[0], mem[0], positions[0], loss_target[0]

    wg_in0 = _all_gather_relay(attn_w_in[0].astype(BF16), "gather_w_in0")
    def gather_pass(weights, after, name, first):
        _, lands = _split_wait(weights, _plan_gather_own, after, name + "_wait", first, 4)
        return _split_start([], lands, _plan_gather_pass, 3, after, name + "_pass_start")

    late = [attn_w_out[0].astype(BF16), w_mem_kv.astype(BF16).reshape(-1, w_mem_kv.shape[2]),
            jnp.pad(conv_w[0], ((0, 5), (0, 0))), conv_w_in[0].astype(BF16), conv_w_out[0].astype(BF16)]
    lands = [lax.dynamic_update_slice(lax.empty((N_DEV,) + t.shape, t.dtype), t[None], (me, 0, 0)) for t in late]
    (send_sems, recv_sems, _, lands), token = _split_start([], lands, _plan_gather_own, 4, wg_in0, "gather_late_start")
    rest0, conv_ws = (send_sems, recv_sems, [], lands[:3]), (send_sems, recv_sems, [], lands[3:])

    tabs = _rope_tables(pos)
    g0, g1 = norm_g[0:1], norm_g[1:2]

    hn0, qs, ks, vs, tabs_v, qm0, z0 = _inproj_attn(x, g0, wg_in0, tabs, token)
    os_, ls_ = [], []
    for j, d in enumerate(DILATIONS):
        if j == 2:
            rest0, token = gather_pass(rest0, ls_[1], "gather_rest", 0)
        o, l = _attn_fwd(qs[j], ks[j], vs[j], d, token)
        os_.append(o)
        ls_.append(l)

    conv_ws, token = gather_pass(conv_ws, ls_[2], "gather_conv", 3)
    _, (wg_out0, wg_kv, cw_all) = _split_wait(rest0, _plan_gather_pass, token, "gather_rest_pass_wait")
    cw = cw_all[:, 0:3].transpose(1, 0, 2).reshape(3, -1)
    kv = _memkv_fwd(mem, mem_norm_g, wg_kv)
    y0, y0_t, mo0, h1 = _post_attn(os_, ls_, qm0, kv[0], z0, x, wg_out0)
    _, (wg_in1, wg_out1) = _split_wait(conv_ws, _plan_gather_pass, h1, "gather_conv_pass_wait")
    w_out1 = wg_out1.reshape(-1, wg_out1.shape[2])

    hn1, bg, cg, u, qm1, z1 = _inproj_conv(h1, g1, wg_in1)
    y1, y1_t, mo1, dh2, dh2b, loss_acc, d_final_g = _post_conv_loss(
        bg, cg, u, qm1, kv[1], z1, h1, w_out1, cw, final_g.reshape(1, -1), tgt)

    d_w_out1 = _wgrad_rows(y1_t, dh2b, "wgrad_out1")
    dz1, dbg, dconv, dqm1, dkv1 = _bwd_post_conv(dh2b, w_out1, bg, cg, u, z1, qm1, kv[1], mo1, cw)
    dcg, du, dcw = _bwd_conv(dconv, cg, u, cw)
    dh1, dg1, dh1b, dproj1_t = _dgrad_norm([(dbg, 1), (dcg, 1), (du, 1), (dqm1, 1), (dz1, 1)], wg_in1, h1, g1, dh2,
                                           dh2, True, "dgrad_norm_conv")
    d_w_in1 = _wgrad_shards_t(dproj1_t, hn1, wg_in1.shape[2], "wgrad_in1")

    d_w_out0 = _wgrad_cols(y0_t, dh1b, wg_out0.shape[2], "wgrad_out0")

    gw = GROUP_WIDTH
    ones = (jnp.arange(gw)[:, None] // HEAD_DIM == jnp.arange(gw)[None, :] // HEAD_DIM).astype(BF16)
    res = _bwd_post_attn(dh1b, wg_out0, z0, os_, ls_, qm0, kv[0], mo0, ones, dg1)
    dz0, dos, dls, dqm0, dkv0 = res[0], res[1:4], res[4:7], res[7], res[8]
    d_w_kv, d_mem_g = _memkv_bwd([dkv0, dkv1], wg_kv, mem, mem_norm_g)

    names1 = ["conv_w_in", "conv_w_out", "attn_w_out", "w_mem_kv"]
    grads1 = [d_w_in1, d_w_out1, d_w_out0, d_w_kv]
    started, token = _split_start(grads1, [lax.empty((4,) + g.shape[1:], g.dtype) for g in grads1],
                                  _plan_to_sibling, 4, d_mem_g, "rs1_sibling_start")

    dqs, dks, dvs = [], [], []
    for j, d in enumerate(DILATIONS):
        if j == 1:
            grads1, from_sibling = _split_wait(started, _plan_to_sibling, dqs[0][0], "rs1_sibling_wait")
            parts1 = [_rs_add_sibling(g, r, ck_arr, "rs_add_sibling_" + n)
                      for g, r, n in zip(grads1, from_sibling, names1)]
            pbs1 = [pb for _, pb in parts1]
            started, token = _split_start(pbs1, [lax.empty((3,) + p.shape[1:], p.dtype) for p in pbs1],
                                          _plan_to_chips, 3, dg1, "rs1_chips_start")
        dq, dk, dv = _attn_bwd(qs[j], ks[j], vs[j], ls_[j], dos[j], dls[j], tabs_v[j], d, token)
        dqs.append((dq, d))
        dks.append((dk, d))
        dvs.append((dv, d))
    pieces0 = dqs + dks + dvs + [(dqm0, 1), (dz0, 1)]
    dproj0_t = _assemble_dproj_t(pieces0, N_DEV * wg_in0.shape[2], "assemble_dproj_attn")
    d_w_in0 = _wgrad_shards_t(dproj0_t, hn0, wg_in0.shape[2], "wgrad_in0")

    names0 = ["attn_w_in"]
    grads0 = [d_w_in0]
    started0, token0 = _split_start(grads0, [lax.empty((4,) + g.shape[1:], g.dtype) for g in grads0],
                                    _plan_to_sibling, 4, dg1, "rs0_sibling_start")
    _, from_chips1 = _split_wait(started, _plan_to_chips, token0, "rs1_chips_wait")
    shard = dict(attn_w_in=(attn_w_in[0], m_attn_w_in[0], v_attn_w_in[0]),
                 attn_w_out=(attn_w_out[0], m_attn_w_out[0], v_attn_w_out[0]),
                 conv_w_in=(conv_w_in[0], m_conv_w_in[0], v_conv_w_in[0]),
                 conv_w_out=(conv_w_out[0], m_conv_w_out[0], v_conv_w_out[0]),
                 w_mem_kv=tuple(t.reshape(-1, t.shape[2]) for t in (w_mem_kv, m_w_mem_kv, v_w_mem_kv)))
    finish1 = {n: (pf, r) for n, (pf, _), r in zip(names1, parts1, from_chips1)}
    big = {"conv_w_in": _finish("conv_w_in", *finish1["conv_w_in"], *shard["conv_w_in"], token0)}

    grads0, from_sibling = _split_wait(started0, _plan_to_sibling, big["conv_w_in"][1].T, "rs0_sibling_wait")
    parts0 = [_rs_add_sibling(g, r, ck_arr, "rs_add_sibling_" + n) for g, r, n in zip(grads0, from_sibling, names0)]
    pbs0 = [pb for _, pb in parts0]
    started0, token0 = _split_start(pbs0, [lax.empty((3,) + p.shape[1:], p.dtype) for p in pbs0],
                                    _plan_to_chips, 3, dg1, "rs0_chips_start")
    dx, dg0 = _dgrad_norm(pieces0, wg_in0, x, g0, dh1, token0, False, "dgrad_norm_attn")
    for n in names1[1:]:
        big[n] = _finish(n, *finish1[n], *shard[n], token0)

    small_part = jnp.concatenate([dg0, dg1, d_mem_g.reshape(2, -1), d_final_g, dcw[0:3]], axis=0)
    small_part = jnp.concatenate([small_part, jnp.broadcast_to(loss_acc[0, 0], small_part.shape)], axis=0)
    small_w = dict(norm_g=(norm_g, m_norm_g, v_norm_g), mem_norm_g=(mem_norm_g, m_mem_norm_g, v_mem_norm_g),
                   conv_w=(conv_w, m_conv_w, v_conv_w), final_g=(final_g, m_final_g, v_final_g))
    loss_tile, small_res = _sum_adamw_small(
        _all_gather_small(small_part, [big[n][1] for n in names1[1:]], "gather_small_grads"), ck_arr,
        {n: tuple(t.reshape(-1, t.shape[-1]) for t in wmv) for n, wmv in small_w.items()})
    loss = loss_tile[0, 0]
    for n, wmv in small_w.items():
        big[n] = tuple(t.reshape(wmv[0].shape) for t in small_res[n])

    _, from_chips0 = _split_wait(started0, _plan_to_chips, big["final_g"][1], "rs0_chips_wait")
    for n, (pf, _), r in zip(names0, parts0, from_chips0):
        big[n] = _finish(n, pf, r, *shard[n], big["final_g"][1])
    for n in ("attn_w_in", "attn_w_out", "conv_w_in", "conv_w_out"):
        big[n] = tuple(t[None] for t in big[n])
    big["w_mem_kv"] = tuple(t.reshape(w_mem_kv.shape) for t in big["w_mem_kv"])

    order = ["norm_g", "mem_norm_g", "w_mem_kv", "attn_w_in", "attn_w_out", "conv_w_in", "conv_w", "conv_w_out", "final_g"]
    return (loss, dx[None], *[big[n][0] for n in order], *[big[n][1] for n in order],
            *[big[n][2] for n in order], *[big[n][3] for n in order])
```

```python
import jax
import jax.numpy as jnp
from jax import lax
from jax.experimental import pallas as pl
from jax.experimental.pallas import tpu as pltpu

F32 = jnp.float32
BF16 = jnp.bfloat16

N_DEV = 8
D_MODEL = 1024
HEAD_DIM = 64
ROT_DIM = HEAD_DIM // 4
ROPE_THETA = 500000.0
DILATIONS = (1, 4, 16)
HEADS_PER_GROUP = 8
GROUP_WIDTH = HEADS_PER_GROUP * HEAD_DIM
BLOCK = 128
N_MEM = 256
MEM_HEADS = 4
MEM_WIDTH = MEM_HEADS * HEAD_DIM
CONV_WIDTH = D_MODEL
EPS = 1e-6
SCALE = HEAD_DIM ** -0.5
NEG = -1e30

ADAM_LR = 0.001
ADAM_B1 = 0.9
ADAM_B2 = 0.999
ADAM_EPS = 1e-08
ADAM_WD = 0.01
ADAM_STEP = 10

ROW_TILE = 256
WGRAD_SHARDS = 4
LANES = 128
MESH = pl.DeviceIdType.MESH
ANY = pl.BlockSpec(memory_space=pl.ANY)


def _pallas_call(body, **kw):
    call = pl.pallas_call(body, **kw)

    def run(*args):
        pinned = [pltpu.with_memory_space_constraint(a, pltpu.HBM) if jnp.issubdtype(a.dtype, jnp.floating) else a
                  for a in args]
        return call(*pinned)

    return run


def _dot(a, b):
    return lax.dot_general(a, b, (((1,), (0,)), ((), ())), preferred_element_type=F32)


def _dot_nt(a, b):
    return lax.dot_general(a, b, (((1,), (1,)), ((), ())), preferred_element_type=F32)


def _dot_tn(a, b):
    return lax.dot_general(a, b, (((0,), (0,)), ((), ())), preferred_element_type=F32)


def _params(n_grid, vmem_mb=48):
    return pltpu.CompilerParams(dimension_semantics=("arbitrary",) * n_grid, vmem_limit_bytes=vmem_mb << 20)


def _rows(width, tm=ROW_TILE):
    return pl.BlockSpec((tm, width), lambda i: (i, 0))


def _view_rows(width, d, tm=ROW_TILE):
    return pl.BlockSpec((tm // d, d * width), lambda i: (i, 0))


def _whole(shape):
    return pl.BlockSpec(shape, lambda *_: (0,) * len(shape))


def _resident(shape):
    return pl.BlockSpec(shape, lambda *_: (0,) * len(shape), pipeline_mode=pl.Buffered(1))


def _sds(shape, dtype):
    return pltpu.HBM(shape, dtype)


def _plain(shape, dtype):
    return jax.ShapeDtypeStruct(shape, dtype)


def _silu_parts(z):
    sg = jax.nn.sigmoid(z)
    return z * sg, sg * (1.0 + z * (1.0 - sg))


def _to_view(scr, val, out_ref, d):
    tm, w = val.shape
    if d == 1:
        out_ref[...] = val.astype(out_ref.dtype)
        return
    for cb in range(w // LANES):
        scr[cb] = val[:, cb * LANES:(cb + 1) * LANES]
    for r in range(d):
        for cb in range(w // LANES):
            lo = r * w + cb * LANES
            out_ref[:, lo:lo + LANES] = scr[cb, pl.ds(r, tm // d, stride=d), :].astype(out_ref.dtype)


def _from_view(scr, in_ref, d):
    if d == 1:
        return in_ref[...].astype(F32)
    nc, tm, _ = scr.shape
    w = nc * LANES
    for r in range(d):
        for cb in range(nc):
            lo = r * w + cb * LANES
            scr[cb, pl.ds(r, tm // d, stride=d), :] = in_ref[:, lo:lo + LANES].astype(F32)
    return jnp.concatenate([scr[cb] for cb in range(nc)], axis=1)


def _view_scratch(width, tm=ROW_TILE):
    return pltpu.VMEM((width // LANES, tm, LANES), F32)


def _rope_tables(pos):
    half = ROT_DIM // 2
    inv_freq = ROPE_THETA ** (-jnp.arange(half, dtype=F32) * (2.0 / ROT_DIM))
    ang = pos.astype(F32)[:, None] * inv_freq
    cos, sin = jnp.cos(ang), jnp.sin(ang)
    s = pos.shape[0]
    z8 = jnp.zeros((s, half), F32)
    rest = HEAD_DIM - ROT_DIM
    cosf = jnp.concatenate([cos, cos, jnp.ones((s, rest), F32)], axis=1)
    sa = jnp.concatenate([-sin, z8, jnp.zeros((s, rest), F32)], axis=1)
    sb = jnp.concatenate([z8, sin, jnp.zeros((s, rest), F32)], axis=1)
    return tuple(jnp.tile(t, (1, LANES // HEAD_DIM)) for t in (cosf, sa, sb))


def _rope_fwd(t, cv, sav, sbv):
    w = t.shape[1]
    return t * cv + pltpu.roll(t, w - ROT_DIM // 2, 1) * sav + pltpu.roll(t, ROT_DIM // 2, 1) * sbv


def _rope_bwd(g, cv, sav, sbv):
    w = g.shape[1]
    return g * cv + pltpu.roll(g * sav, ROT_DIM // 2, 1) + pltpu.roll(g * sbv, w - ROT_DIM // 2, 1)


def _joined_columns(wg_ref):
    assert wg_ref.shape[2] % LANES == 0
    return jnp.concatenate([wg_ref[j] for j in range(N_DEV)], axis=1)


def _join_once(wg_ref, w_scr):
    c = wg_ref.shape[2]

    @pl.when(pl.program_id(0) == 0)
    def _():
        for j in range(N_DEV):
            w_scr[:, j * c:(j + 1) * c] = wg_ref[j]


def _inproj_attn(x, g, wg, tabs, token):
    s, d_model = x.shape
    gw = GROUP_WIDTH
    n = N_DEV * wg.shape[2]
    nz = n - 9 * gw - MEM_WIDTH
    reps = gw // LANES
    tm = ROW_TILE

    def body(_, x_ref, g_ref, w_ref, c_ref, sa_ref, sb_ref, hn_ref, *rest):
        outs, (wj, scr, tscr) = rest[:-3], rest[-3:]
        q_refs, k_refs, v_refs, t_refs, qm_ref, z_ref = outs[0:3], outs[3:6], outs[6:9], outs[9:18], outs[18], outs[19]
        _join_once(w_ref, wj)
        xb = x_ref[...]
        r = lax.rsqrt(jnp.mean(xb * xb, axis=-1, keepdims=True) + EPS)
        hn = ((xb * r) * g_ref[...]).astype(BF16)
        hn_ref[...] = hn
        proj = lambda lo, hi: _dot(hn, wj[:, lo:hi])
        tab = (c_ref[...], sa_ref[...], sb_ref[...])
        cv, sav, sbv = [jnp.tile(t, (1, reps)) for t in tab]
        for j, d in enumerate(DILATIONS):
            tq = _rope_fwd(proj(j * gw, (j + 1) * gw), cv, sav, sbv)
            _to_view(scr, tq * SCALE, q_refs[j], d)
            tk = _rope_fwd(proj((3 + j) * gw, (4 + j) * gw), cv, sav, sbv)
            _to_view(scr, tk, k_refs[j], d)
            _to_view(scr, proj((6 + j) * gw, (7 + j) * gw), v_refs[j], d)
            for i in range(3):
                _to_view(tscr, tab[i], t_refs[3 * j + i], d)
        qm_ref[...] = proj(9 * gw, 9 * gw + MEM_WIDTH).astype(BF16)
        z_ref[...] = proj(9 * gw + MEM_WIDTH, n)

    views = [_sds((s // d, d * gw), BF16) for d in DILATIONS]
    tviews = [_sds((s // d, d * LANES), F32) for d in DILATIONS for _ in range(3)]
    out_shape = [_sds((s, d_model), BF16)] + views * 3 + tviews + [_sds((s, MEM_WIDTH), BF16), _sds((s, nz), F32)]
    vspecs = [_view_rows(gw, d, tm) for d in DILATIONS]
    tspecs = [_view_rows(LANES, d, tm) for d in DILATIONS for _ in range(3)]
    out_specs = [_rows(d_model, tm)] + vspecs * 3 + tspecs + [_rows(MEM_WIDTH, tm), _rows(nz, tm)]
    res = _pallas_call(
        body, name="inproj_attn", grid=(s // tm,), out_shape=out_shape,
        in_specs=[ANY, _rows(d_model, tm), _whole((1, d_model)), _resident(wg.shape)] + [_rows(LANES, tm)] * 3,
        out_specs=out_specs,
        scratch_shapes=[pltpu.VMEM((d_model, n), BF16), _view_scratch(gw, tm), _view_scratch(LANES, tm)],
        compiler_params=_params(1, 60),
    )(token, x, g, wg, *tabs)
    tabs_v = [res[10 + 3 * j:13 + 3 * j] for j in range(3)]
    return res[0], res[1:4], res[4:7], res[7:10], tabs_v, res[19], res[20]


def _band_mask(n_keys):
    qi = lax.broadcasted_iota(jnp.int32, (BLOCK, n_keys), 0)
    kj = lax.broadcasted_iota(jnp.int32, (BLOCK, n_keys), 1)
    if n_keys == BLOCK:
        return kj <= qi
    return jnp.logical_or(jnp.logical_and(kj < BLOCK, kj >= qi), jnp.logical_and(kj >= BLOCK, (kj - BLOCK) <= qi))


def _low_head_lanes():
    return lax.broadcasted_iota(jnp.int32, (1, LANES), 1) < HEAD_DIM


def _split_pair(t, low):
    zero = jnp.zeros_like(t)
    return jnp.where(low, t, zero), jnp.where(low, zero, t)


def _pair_specs(d, nb, w):
    if nb == 1:
        return pl.BlockSpec((BLOCK, 2 * w), lambda n: (0, n)), None
    half = nb // 2
    two = pl.BlockSpec((2 * BLOCK, w), lambda n: (n % half, n // half))
    before = pl.BlockSpec((BLOCK, w), lambda n: (jnp.maximum(2 * (n % half) - 1, 0), n // half))
    return two, before


def _head_tiles(w, col0):
    return ([slice(p * LANES, (p + 1) * LANES) for p in range(w // LANES)],
            [slice(col0 + p * LANES, col0 + (p + 1) * LANES) for p in range(w // LANES)])


def _attend_fwd(q_ref, o_ref, lse_ref, rows, col0, kk, vv):
    w = kk.shape[1]
    valid = _band_mask(kk.shape[0])
    low = _low_head_lanes()
    pairs, qcols = _head_tiles(w, col0)
    qs_ = [h for qc in qcols for h in _split_pair(q_ref[rows, qc], low)]
    k2s = [kk[:, pr] for pr in pairs for _ in range(2)]
    scs = [jnp.where(valid, _dot_nt(qh, k2), NEG) for qh, k2 in zip(qs_, k2s)]
    ms = [jnp.max(sc, axis=-1, keepdims=True) for sc in scs]
    ps = [jnp.exp(sc - m) for sc, m in zip(scs, ms)]
    ls = [jnp.sum(p, axis=-1, keepdims=True) for p in ps]
    pns = [(p * (1.0 / l)).astype(BF16) for p, l in zip(ps, ls)]
    for i, (pr, qc) in enumerate(zip(pairs, qcols)):
        v2 = vv[:, pr]
        a, b = 2 * i, 2 * i + 1
        o_ref[rows, qc] = jnp.where(low, _dot(pns[a], v2), _dot(pns[b], v2))
        lse_ref[rows, qc] = jnp.where(low, ms[a] + jnp.log(ls[a]), ms[b] + jnp.log(ls[b]))


TOP, BOTTOM = slice(0, BLOCK), slice(BLOCK, 2 * BLOCK)


def _attn_fwd(q, k, v, d, token):
    ln, dw = q.shape
    w = dw // d
    nb = ln // BLOCK
    two, before = _pair_specs(d, nb, w)

    def body_streams(_, q_ref, kc_ref, vc_ref, o_ref, lse_ref):
        for sb in range(2):
            cols = slice(sb * w, (sb + 1) * w)
            _attend_fwd(q_ref, o_ref, lse_ref, TOP, sb * w, kc_ref[:, cols], vc_ref[:, cols])

    def body_blocks(_, q_ref, kp_ref, kc_ref, vp_ref, vc_ref, o_ref, lse_ref):
        first = pl.program_id(0) % (nb // 2) == 0
        pl.when(first)(lambda: _attend_fwd(q_ref, o_ref, lse_ref, TOP, 0, kc_ref[TOP, :], vc_ref[TOP, :]))
        pl.when(jnp.logical_not(first))(lambda: _attend_fwd(
            q_ref, o_ref, lse_ref, TOP, 0, jnp.concatenate([kp_ref[...], kc_ref[TOP, :]], axis=0),
            jnp.concatenate([vp_ref[...], vc_ref[TOP, :]], axis=0)))
        _attend_fwd(q_ref, o_ref, lse_ref, BOTTOM, 0, kc_ref[...], vc_ref[...])

    if nb == 1:
        body, in_specs, args = body_streams, [ANY, two, two, two], (token, q, k, v)
    else:
        body, in_specs, args = body_blocks, [ANY, two, before, two, before, two], (token, q, k, k, v, v)
    return _pallas_call(
        body, name=f"attn_fwd_d{d}", grid=(d * nb // 2,), out_shape=[_sds((ln, dw), F32)] * 2,
        in_specs=in_specs, out_specs=[two, two], compiler_params=_params(1, 32),
    )(*args)


def _memkv_fwd(mem, g, w):
    n_layers = g.shape[0]
    rows = D_MODEL // N_DEV

    def body(mem_ref, g_ref, w_ref, *kv_refs):
        mb = mem_ref[...]
        r = lax.rsqrt(jnp.mean(mb * mb, axis=-1, keepdims=True) + EPS)
        mn = ((mb * r) * g_ref[...]).astype(BF16)
        kv = _dot(mn, w_ref[...].reshape(D_MODEL, 2 * MEM_WIDTH)).astype(BF16)
        for layer, kv_ref in enumerate(kv_refs):
            @pl.when(pl.program_id(0) == layer)
            def _(kv_ref=kv_ref):
                kv_ref[...] = kv

    return _pallas_call(
        body, name="memkv_fwd", grid=(n_layers,),
        out_shape=[_sds((N_MEM, 2 * MEM_WIDTH), BF16)] * n_layers,
        in_specs=[_whole(mem.shape), pl.BlockSpec((None, 1, D_MODEL), lambda l: (l, 0, 0)),
                  pl.BlockSpec((N_DEV, rows, 2 * MEM_WIDTH), lambda l: (0, l, 0))],
        out_specs=[_whole((N_MEM, 2 * MEM_WIDTH))] * n_layers,
        compiler_params=_params(1, 32),
    )(mem, g.reshape(n_layers, 1, D_MODEL), w)


def _mix_groups(os_, ls_):
    mx = jnp.maximum(jnp.maximum(ls_[0], ls_[1]), ls_[2])
    es = [jnp.exp(t - mx) for t in ls_]
    inv = 1.0 / (es[0] + es[1] + es[2])
    ws = [e * inv for e in es]
    mix = ws[0] * os_[0] + ws[1] * os_[1] + ws[2] * os_[2]
    return ws, mix


MEM_PAIRS = [slice(p * LANES, (p + 1) * LANES) for p in range(MEM_WIDTH // LANES)]


def _mem_probs(qhs, k2s):
    scs = [_dot_nt(qh, k2) * SCALE for qh, k2 in zip(qhs, k2s)]
    es = [jnp.exp(sc - jnp.max(sc, axis=-1, keepdims=True)) for sc in scs]
    return [e * (1.0 / jnp.sum(e, axis=-1, keepdims=True)) for e in es]


def _mem_attn_into(qm, kv_ref, mo_ref):
    low = _low_head_lanes()
    qhs = [h for pr in MEM_PAIRS for h in _split_pair(qm[:, pr], low)]
    k2s = [kv_ref[:, pr] for pr in MEM_PAIRS for _ in range(2)]
    ps = [p.astype(BF16) for p in _mem_probs(qhs, k2s)]
    for i, pr in enumerate(MEM_PAIRS):
        v2 = kv_ref[:, MEM_WIDTH + i * LANES:MEM_WIDTH + (i + 1) * LANES]
        mo_ref[:, pr] = jnp.where(low, _dot(ps[2 * i], v2), _dot(ps[2 * i + 1], v2))


def _mem_attn_bwd(qm, kv_ref, dmem, dqm_ref, dkv_ref):
    low = _low_head_lanes()
    dmb = dmem.astype(BF16)
    vps = [slice(MEM_WIDTH + i * LANES, MEM_WIDTH + (i + 1) * LANES) for i in range(len(MEM_PAIRS))]
    qhs = [h for pr in MEM_PAIRS for h in _split_pair(qm[:, pr], low)]
    dhs = [h for pr in MEM_PAIRS for h in _split_pair(dmb[:, pr], low)]
    k2s = [kv_ref[:, pr] for pr in MEM_PAIRS for _ in range(2)]
    v2s = [kv_ref[:, vp] for vp in vps for _ in range(2)]
    ps = _mem_probs(qhs, k2s)
    dps = [_dot_nt(dh, v2) for dh, v2 in zip(dhs, v2s)]
    dss = [(p * (dp - jnp.sum(dp * p, axis=-1, keepdims=True)) * SCALE).astype(BF16) for p, dp in zip(ps, dps)]
    pbs = [p.astype(BF16) for p in ps]
    for i, (pr, vp) in enumerate(zip(MEM_PAIRS, vps)):
        a, b = 2 * i, 2 * i + 1
        dqm_ref[:, pr] = jnp.where(low, _dot(dss[a], k2s[a]), _dot(dss[b], k2s[b])).astype(BF16)
        dkv_ref[:, pr] += _dot_tn(dss[a], qhs[a]) + _dot_tn(dss[b], qhs[b])
        dkv_ref[:, vp] += _dot_tn(pbs[a], dhs[a]) + _dot_tn(pbs[b], dhs[b])


def _post_attn(os_, ls_, qm, kv, z, x, wg_out):
    s, d_model = x.shape
    gw = GROUP_WIDTH
    nb = gw + MEM_WIDTH
    tm = ROW_TILE

    def body(o0, o1, o2, l0, l1, l2, qm_ref, kv_ref, z_ref, x_ref, w_ref, y_ref, yt_ref, mo_ref, h_ref, s0, s1):
        ov, lv = [], []
        for o_ref, l_ref, d in zip((o0, o1, o2), (l0, l1, l2), DILATIONS):
            ov.append(_from_view(s0, o_ref, d))
            lv.append(_from_view(s1, l_ref, d))
        _, mix = _mix_groups(ov, lv)
        _mem_attn_into(qm_ref[...], kv_ref, mo_ref)
        sz, _ = _silu_parts(z_ref[...])
        y_ref[:, :gw] = (mix * sz[:, :gw]).astype(BF16)
        y_ref[:, gw:] = (mo_ref[...] * sz[:, gw:]).astype(BF16)
        y = y_ref[...]
        yt_ref[...] = y.T
        h_ref[...] = x_ref[...] + _dot(y, _joined_columns(w_ref))

    vspecs = [_view_rows(gw, d) for d in DILATIONS]
    return _pallas_call(
        body, name="post_attn", grid=(s // tm,),
        out_shape=[_sds((s, nb), BF16), _sds((nb, s), BF16), _sds((s, MEM_WIDTH), F32), _sds((s, d_model), F32)],
        in_specs=vspecs * 2 + [_rows(MEM_WIDTH), _whole(kv.shape), _rows(nb), _rows(d_model), _whole(wg_out.shape)],
        out_specs=[_rows(nb), pl.BlockSpec((nb, tm), lambda i: (0, i)), _rows(MEM_WIDTH), _rows(d_model)],
        scratch_shapes=[_view_scratch(gw), _view_scratch(gw)],
        compiler_params=_params(1, 40),
    )(*os_, *ls_, qm, kv, z, x, wg_out)


def _inproj_conv(x, g, wg):
    s, d_model = x.shape
    c = CONV_WIDTH
    n = N_DEV * wg.shape[2]
    nz = n - 3 * c - MEM_WIDTH
    tm = ROW_TILE

    def body(x_ref, g_ref, w_ref, hn_ref, bg_ref, cg_ref, u_ref, qm_ref, z_ref, wj):
        _join_once(w_ref, wj)
        xb = x_ref[...]
        r = lax.rsqrt(jnp.mean(xb * xb, axis=-1, keepdims=True) + EPS)
        hn = ((xb * r) * g_ref[...]).astype(BF16)
        hn_ref[...] = hn
        bg_ref[...] = _dot(hn, wj[:, 0:c])
        cg_ref[...] = _dot(hn, wj[:, c:2 * c])
        u_ref[...] = _dot(hn, wj[:, 2 * c:3 * c])
        qm_ref[...] = _dot(hn, wj[:, 3 * c:3 * c + MEM_WIDTH]).astype(BF16)
        z_ref[...] = _dot(hn, wj[:, 3 * c + MEM_WIDTH:])

    return _pallas_call(
        body, name="inproj_conv", grid=(s // tm,),
        out_shape=[_sds((s, d_model), BF16)] + [_sds((s, c), F32)] * 3 + [_sds((s, MEM_WIDTH), BF16), _sds((s, nz), F32)],
        in_specs=[_rows(d_model), _whole((1, d_model)), _resident(wg.shape)],
        out_specs=[_rows(d_model)] + [_rows(c)] * 3 + [_rows(MEM_WIDTH), _rows(nz)],
        scratch_shapes=[pltpu.VMEM((d_model, n), BF16)],
        compiler_params=_params(1, 60),
    )(x, g, wg)


HALO = 8


def _halo_before(width, tm=ROW_TILE):
    return pl.BlockSpec((HALO, width), lambda i: (jnp.maximum(i * (tm // HALO) - 1, 0), 0))


def _halo_after(width, n_rows, tm=ROW_TILE):
    return pl.BlockSpec((HALO, width), lambda i: (jnp.minimum((i + 1) * (tm // HALO), n_rows // HALO - 1), 0))


def _conv_taps(cg_ref, u_ref, cgh_ref, uh_ref, i):
    a = cg_ref[...] * u_ref[...]
    ah = jnp.where(i > 0, cgh_ref[...] * uh_ref[...], 0.0)
    row = lax.broadcasted_iota(jnp.int32, a.shape, 0)
    a1 = jnp.where(row == 0, ah[HALO - 1:HALO], pltpu.roll(a, 1, 0))
    a2 = jnp.where(row == 0, ah[HALO - 2:HALO - 1], jnp.where(row == 1, ah[HALO - 1:HALO], pltpu.roll(a, 2, 0)))
    return a, a1, a2


def _post_conv_loss(bg, cg, u, qm, kv, z, h1, w_out, cw, gf, tgt):
    s, d = h1.shape
    c = CONV_WIDTH
    nb = c + MEM_WIDTH
    tm = ROW_TILE

    def body(bg_ref, cg_ref, u_ref, cgh_ref, uh_ref, qm_ref, kv_ref, z_ref, h_ref, w_ref, cw_ref, gf_ref, t_ref,
             y_ref, yt_ref, mo_ref, dh_ref, dhb_ref, loss_ref, dgf_ref):
        i = pl.program_id(0)
        a, a1, a2 = _conv_taps(cg_ref, u_ref, cgh_ref, uh_ref, i)
        conv = cw_ref[0:1, :] * a2 + cw_ref[1:2, :] * a1 + cw_ref[2:3, :] * a
        mix = bg_ref[...] * conv
        _mem_attn_into(qm_ref[...], kv_ref, mo_ref)
        sz, _ = _silu_parts(z_ref[...])
        y_ref[:, :c] = (mix * sz[:, :c]).astype(BF16)
        y_ref[:, c:] = (mo_ref[...] * sz[:, c:]).astype(BF16)
        y = y_ref[...]
        yt_ref[...] = y.T
        h2 = h_ref[...] + _dot(y, w_ref[...])
        r = lax.rsqrt(jnp.mean(h2 * h2, axis=-1, keepdims=True) + EPS)
        nh = h2 * r
        gfv = gf_ref[...]
        diff = nh * gfv - t_ref[...]
        dout = diff * (1.0 / d)
        dn = dout * gfv
        dh2 = r * dn - h2 * ((r * r * r) * jnp.mean(dn * h2, axis=-1, keepdims=True))
        dh_ref[...] = dh2
        dhb_ref[...] = dh2.astype(BF16)

        @pl.when(i == 0)
        def _():
            loss_ref[...] = jnp.zeros_like(loss_ref)
            dgf_ref[...] = jnp.zeros_like(dgf_ref)

        loss_ref[...] += 0.5 * jnp.sum(jnp.mean(diff * diff, axis=-1, keepdims=True))
        dgf_ref[...] += jnp.sum(dout * nh, axis=0, keepdims=True)

    return _pallas_call(
        body, name="post_conv_loss", grid=(s // tm,),
        out_shape=[_sds((s, nb), BF16), _sds((nb, s), BF16), _sds((s, MEM_WIDTH), F32), _sds((s, d), F32),
                   _sds((s, d), BF16), _plain((8, LANES), F32), _plain((1, d), F32)],
        in_specs=[_rows(c)] * 3 + [_halo_before(c)] * 2 + [_rows(MEM_WIDTH), _whole(kv.shape), _rows(nb), _rows(d),
                  _whole(w_out.shape), _whole(cw.shape), _whole((1, d)), _rows(d)],
        out_specs=[_rows(nb), pl.BlockSpec((nb, tm), lambda i: (0, i)), _rows(MEM_WIDTH), _rows(d), _rows(d),
                   _whole((8, LANES)), _whole((1, d))],
        compiler_params=_params(1, 48),
    )(bg, cg, u, cg, u, qm, kv, z, h1, w_out, cw, gf, tgt)


def _bwd_post_conv(dhb, w_out, bg, cg, u, z, qm, kv, mo, cw):
    s = dhb.shape[0]
    c = CONV_WIDTH
    nb = c + MEM_WIDTH

    def body(dh_ref, w_ref, bg_ref, cg_ref, u_ref, cgh_ref, uh_ref, z_ref, qm_ref, kv_ref, mo_ref, cw_ref,
             dz_ref, dbg_ref, dc_ref, dqm_ref, dkv_ref):
        i = pl.program_id(0)

        @pl.when(i == 0)
        def _():
            dkv_ref[...] = jnp.zeros_like(dkv_ref)

        dy = _dot_nt(dh_ref[...], w_ref[...])
        sz, dsz = _silu_parts(z_ref[...])
        a, a1, a2 = _conv_taps(cg_ref, u_ref, cgh_ref, uh_ref, i)
        conv = cw_ref[0:1, :] * a2 + cw_ref[1:2, :] * a1 + cw_ref[2:3, :] * a
        bgv = bg_ref[...]
        dz_ref[:, :c] = (dy[:, :c] * (bgv * conv) * dsz[:, :c]).astype(BF16)
        dz_ref[:, c:] = (dy[:, c:] * mo_ref[...] * dsz[:, c:]).astype(BF16)
        dbr = dy * sz
        dmix = dbr[:, :c]
        dbg_ref[...] = (dmix * conv).astype(BF16)
        dc_ref[...] = dmix * bgv
        _mem_attn_bwd(qm_ref[...], kv_ref, dbr[:, c:], dqm_ref, dkv_ref)

    return _pallas_call(
        body, name="bwd_post_conv", grid=(s // ROW_TILE,),
        out_shape=[_sds((s, nb), BF16), _sds((s, c), BF16), _sds((s, c), F32), _sds((s, MEM_WIDTH), BF16),
                   _plain(kv.shape, F32)],
        in_specs=[_rows(D_MODEL), _whole(w_out.shape)] + [_rows(c)] * 3 + [_halo_before(c)] * 2
                 + [_rows(nb), _rows(MEM_WIDTH), _whole(kv.shape), _rows(MEM_WIDTH), _whole(cw.shape)],
        out_specs=[_rows(nb), _rows(c), _rows(c), _rows(MEM_WIDTH), _whole(kv.shape)],
        compiler_params=_params(1, 48),
    )(dhb, w_out, bg, cg, u, cg, u, z, qm, kv, mo, cw)


def _bwd_conv(dconv, cg, u, cw):
    s, c = dconv.shape
    tm = ROW_TILE
    last = s // tm - 1

    def body(dc_ref, dcn_ref, cg_ref, u_ref, cgh_ref, uh_ref, cw_ref, dcg_ref, du_ref, dcw_ref):
        i = pl.program_id(0)

        @pl.when(i == 0)
        def _():
            dcw_ref[...] = jnp.zeros_like(dcw_ref)

        dc = dc_ref[...]
        dcn = jnp.where(i < last, dcn_ref[...], 0.0)
        row = lax.broadcasted_iota(jnp.int32, dc.shape, 0)
        d1 = jnp.where(row == tm - 1, dcn[0:1], pltpu.roll(dc, tm - 1, 0))
        d2 = jnp.where(row == tm - 1, dcn[1:2], jnp.where(row == tm - 2, dcn[0:1], pltpu.roll(dc, tm - 2, 0)))
        da = cw_ref[2:3, :] * dc + cw_ref[1:2, :] * d1 + cw_ref[0:1, :] * d2
        a, a1, a2 = _conv_taps(cg_ref, u_ref, cgh_ref, uh_ref, i)
        dcg_ref[...] = (da * u_ref[...]).astype(BF16)
        du_ref[...] = (da * cg_ref[...]).astype(BF16)
        dcw_ref[0:1, :] += jnp.sum(dc * a2, axis=0, keepdims=True)
        dcw_ref[1:2, :] += jnp.sum(dc * a1, axis=0, keepdims=True)
        dcw_ref[2:3, :] += jnp.sum(dc * a, axis=0, keepdims=True)

    return _pallas_call(
        body, name="bwd_conv", grid=(s // tm,),
        out_shape=[_sds((s, c), BF16), _sds((s, c), BF16), _plain((8, c), F32)],
        in_specs=[_rows(c), _halo_after(c, s), _rows(c), _rows(c), _halo_before(c), _halo_before(c), _whole(cw.shape)],
        out_specs=[_rows(c), _rows(c), _whole((8, c))], compiler_params=_params(1, 40),
    )(dconv, dconv, cg, u, cg, u, cw)


def _assemble(p_refs, pieces, widths, dp, scr):
    off = 0
    for p_ref, (_, d), wd in zip(p_refs, pieces, widths):
        if d == 1:
            dp[:, off:off + wd] = p_ref[...]
        else:
            dp[:, off:off + wd] = _from_view(scr, p_ref, d).astype(BF16)
        off += wd


def _dgrad_norm(pieces, wg, h, g, dres, token, onward, name):
    s, d_model = h.shape
    n = N_DEV * wg.shape[2]
    tm = ROW_TILE
    widths = [p.shape[1] // d for p, d in pieces]
    assert sum(widths) == n
    n_p = len(pieces)

    def body(_, *refs):
        p_refs = refs[:n_p]
        w_ref, h_ref, g_ref, dr_ref, dh_ref, dg_ref = refs[n_p:n_p + 6]
        dp, scr, wj = refs[-3:]
        _join_once(w_ref, wj)

        @pl.when(pl.program_id(0) == 0)
        def _():
            dg_ref[...] = jnp.zeros_like(dg_ref)

        _assemble(p_refs, pieces, widths, dp, scr)
        dhn = _dot_nt(dp[...], wj[...])
        hb = h_ref[...]
        r = lax.rsqrt(jnp.mean(hb * hb, axis=-1, keepdims=True) + EPS)
        dg_ref[...] += jnp.sum(dhn * (hb * r), axis=0, keepdims=True)
        dn = dhn * g_ref[...]
        dh = dr_ref[...] + r * dn - hb * ((r * r * r) * jnp.mean(dn * hb, axis=-1, keepdims=True))
        dh_ref[...] = dh
        if onward:
            dhb_ref, dpt_ref = refs[n_p + 6:n_p + 8]
            dhb_ref[...] = dh.astype(BF16)
            dpt_ref[...] = dp[...].T

    p_specs = [_view_rows(wd, d) for (_, d), wd in zip(pieces, widths)]
    out_shape = [_plain((s, d_model), F32), _plain((1, d_model), F32)]
    out_specs = [_rows(d_model), _whole((1, d_model))]
    if onward:
        out_shape += [_sds((s, d_model), BF16), _sds((n, s), BF16)]
        out_specs += [_rows(d_model), pl.BlockSpec((n, tm), lambda i: (0, i))]
    return _pallas_call(
        body, name=name, grid=(s // tm,), out_shape=out_shape,
        in_specs=[ANY] + p_specs + [_resident(wg.shape), _rows(d_model), _whole((1, d_model)), _rows(d_model)],
        out_specs=out_specs,
        scratch_shapes=[pltpu.VMEM((tm, n), BF16), _view_scratch(GROUP_WIDTH), pltpu.VMEM((d_model, n), BF16)],
        compiler_params=_params(1, 60),
    )(token, *[p for p, _ in pieces], wg, h, g, dres)


def _assemble_dproj_t(pieces, n, name):
    tm = ROW_TILE
    widths = [p.shape[1] // d for p, d in pieces]
    assert sum(widths) == n
    s = pieces[0][0].shape[0] * pieces[0][1]
    n_p = len(pieces)

    def body(*refs):
        p_refs, (dpt_ref, dp, scr) = refs[:n_p], refs[n_p:]
        _assemble(p_refs, pieces, widths, dp, scr)
        dpt_ref[...] = dp[...].T

    return _pallas_call(
        body, name=name, grid=(s // tm,), out_shape=_sds((n, s), BF16),
        in_specs=[_view_rows(wd, d) for (_, d), wd in zip(pieces, widths)],
        out_specs=pl.BlockSpec((n, tm), lambda i: (0, i)),
        scratch_shapes=[pltpu.VMEM((tm, n), BF16), _view_scratch(GROUP_WIDTH)],
        compiler_params=_params(1, 40),
    )(*[p for p, _ in pieces])


def _wgrad_shards_t(dp_t, h, c, name):
    n, s = dp_t.shape
    d_model = h.shape[1]
    per_step = 2

    def body(a_ref, b_ref, o_ref):
        o_ref[...] = _dot(a_ref[...], b_ref[...]).astype(BF16).reshape(per_step, c, d_model)

    return _pallas_call(
        body, name=name, grid=(N_DEV // per_step,), out_shape=_sds((N_DEV, c, d_model), BF16),
        in_specs=[pl.BlockSpec((per_step * c, s), lambda j: (j, 0)), _resident(h.shape)],
        out_specs=pl.BlockSpec((per_step, c, d_model), lambda j: (j, 0, 0)), compiler_params=_params(1, 40),
    )(dp_t, h)


def _wgrad_cols(a_t, b, c, name):
    m, s = a_t.shape
    assert c % LANES == 0

    def body(a_ref, b_ref, o_ref):
        wide = _dot(a_ref[...], b_ref[...]).astype(BF16)
        for j in range(WGRAD_SHARDS):
            o_ref[j] = wide[:, j * c:(j + 1) * c]

    return _pallas_call(
        body, name=name, grid=(N_DEV // WGRAD_SHARDS,), out_shape=_sds((N_DEV, m, c), BF16),
        in_specs=[_whole(a_t.shape), pl.BlockSpec((s, WGRAD_SHARDS * c), lambda j: (0, j))],
        out_specs=pl.BlockSpec((WGRAD_SHARDS, m, c), lambda j: (j, 0, 0)), compiler_params=_params(1, 40),
    )(a_t, b)


def _wgrad_rows(a_t, b, name):
    m, s = a_t.shape
    n = b.shape[1]
    mr = m // N_DEV

    def body(a_ref, b_ref, o_ref):
        o_ref[...] = _dot(a_ref[...], b_ref[...]).astype(BF16).reshape(WGRAD_SHARDS, mr, n)

    return _pallas_call(
        body, name=name, grid=(N_DEV // WGRAD_SHARDS,), out_shape=_sds((N_DEV, mr, n), BF16),
        in_specs=[pl.BlockSpec((WGRAD_SHARDS * mr, s), lambda j: (j, 0)), _whole(b.shape)],
        out_specs=pl.BlockSpec((WGRAD_SHARDS, mr, n), lambda j: (j, 0, 0)), compiler_params=_params(1, 40),
    )(a_t, b)


def _memkv_bwd(dkvs, w, mem, g):
    n_layers = g.shape[0]
    rows = D_MODEL // N_DEV

    def body(dkv0_ref, dkv1_ref, w_ref, mem_ref, g_ref, dw_ref, dg_ref):
        mb = mem_ref[...]
        r = lax.rsqrt(jnp.mean(mb * mb, axis=-1, keepdims=True) + EPS)
        nm = mb * r
        mn = (nm * g_ref[...]).astype(BF16)
        dkvb = jnp.where(pl.program_id(0) == 0, dkv0_ref[...], dkv1_ref[...]).astype(BF16)
        dw_ref[...] = _dot_tn(mn, dkvb).astype(BF16).reshape(N_DEV, rows, 2 * MEM_WIDTH)
        dmn = _dot_nt(dkvb, w_ref[...].reshape(D_MODEL, 2 * MEM_WIDTH))
        dg_ref[...] = jnp.sum(dmn * nm, axis=0, keepdims=True)

    lay = lambda *shape: pl.BlockSpec((None,) + shape, lambda l: (l, 0, 0))
    major = pl.BlockSpec((N_DEV, rows, 2 * MEM_WIDTH), lambda l: (0, l, 0))
    return _pallas_call(
        body, name="memkv_bwd", grid=(n_layers,),
        out_shape=[_sds((N_DEV, n_layers * rows, 2 * MEM_WIDTH), BF16), _plain((n_layers, 1, D_MODEL), F32)],
        in_specs=[_whole(dkvs[0].shape), _whole(dkvs[1].shape), major, _whole(mem.shape), lay(1, D_MODEL)],
        out_specs=[major, lay(1, D_MODEL)],
        compiler_params=_params(1, 32),
    )(*dkvs, w, mem, g.reshape(n_layers, 1, D_MODEL))


def _bwd_post_attn(dhb, wg_out, z, os_, ls_, qm, kv, mo, head_ones, token):
    s = dhb.shape[0]
    gw = GROUP_WIDTH
    nb = gw + MEM_WIDTH
    tm = ROW_TILE

    def body(_, dh_ref, w_ref, z_ref, o0, o1, o2, l0, l1, l2, qm_ref, kv_ref, mo_ref, bd_ref,
             dz_ref, do0, do1, do2, dl0, dl1, dl2, dqm_ref, dkv_ref, s0, s1):
        @pl.when(pl.program_id(0) == 0)
        def _():
            dkv_ref[...] = jnp.zeros_like(dkv_ref)

        dy = _dot_nt(dh_ref[...], _joined_columns(w_ref))
        ov, lv = [], []
        for o_ref, l_ref, d in zip((o0, o1, o2), (l0, l1, l2), DILATIONS):
            ov.append(_from_view(s0, o_ref, d))
            lv.append(_from_view(s1, l_ref, d))
        ws, mix = _mix_groups(ov, lv)
        sz, dsz = _silu_parts(z_ref[...])
        dz_ref[:, :gw] = (dy[:, :gw] * mix * dsz[:, :gw]).astype(BF16)
        dz_ref[:, gw:] = (dy[:, gw:] * mo_ref[...] * dsz[:, gw:]).astype(BF16)
        dbr = dy * sz
        dmix = dbr[:, :gw]
        t = dmix * mix
        th = t.astype(BF16)
        tl = (t - th.astype(F32)).astype(BF16)
        rs = _dot(th, bd_ref[...]) + _dot(tl, bd_ref[...])
        for wg_, do_ref, dl_ref, d in zip(ws, (do0, do1, do2), (dl0, dl1, dl2), DILATIONS):
            _to_view(s0, wg_ * dmix, do_ref, d)
            _to_view(s1, wg_ * rs, dl_ref, d)
        _mem_attn_bwd(qm_ref[...], kv_ref, dbr[:, gw:], dqm_ref, dkv_ref)

    vspecs = [_view_rows(gw, d) for d in DILATIONS]
    return _pallas_call(
        body, name="bwd_post_attn", grid=(s // tm,),
        out_shape=[_sds((s, nb), BF16)] + [_sds((s // d, d * gw), BF16) for d in DILATIONS]
                  + [_sds((s // d, d * gw), F32) for d in DILATIONS] + [_sds((s, MEM_WIDTH), BF16), _plain(kv.shape, F32)],
        in_specs=[ANY, _rows(D_MODEL), _whole(wg_out.shape), _rows(nb)] + vspecs * 2
                 + [_rows(MEM_WIDTH), _whole(kv.shape), _rows(MEM_WIDTH), _whole(head_ones.shape)],
        out_specs=[_rows(nb)] + vspecs * 2 + [_rows(MEM_WIDTH), _whole(kv.shape)],
        scratch_shapes=[_view_scratch(gw), _view_scratch(gw)],
        compiler_params=_params(1, 48),
    )(token, dhb, wg_out, z, *os_, *ls_, qm, kv, mo, head_ones)


def _attn_bwd(q, k, v, lse, do, dl, tabs, d, token):
    ln, dw = q.shape
    w = dw // d
    nb = ln // BLOCK
    reps = w // LANES
    two, before = _pair_specs(d, nb, w)
    two_t, _ = _pair_specs(d, nb, LANES)

    def attend(q_ref, l_ref, do_ref, dl_ref, dqs, acck, accv, rows, col0, kk, vv, acc_rows):
        valid = _band_mask(kk.shape[0])
        low = _low_head_lanes()
        pairs, qcols = _head_tiles(w, col0)
        cols = [slice(col0 + h * HEAD_DIM, col0 + h * HEAD_DIM + 1) for h in range(HEADS_PER_GROUP)]
        qhs = [h for qc in qcols for h in _split_pair(q_ref[rows, qc], low)]
        dobs = [h for qc in qcols for h in _split_pair(do_ref[rows, qc], low)]
        k2s = [kk[:, pr] for pr in pairs for _ in range(2)]
        v2s = [vv[:, pr] for pr in pairs for _ in range(2)]
        scs = [jnp.where(valid, _dot_nt(qh, k2), NEG) for qh, k2 in zip(qhs, k2s)]
        dps = [_dot_nt(dob, v2) for dob, v2 in zip(dobs, v2s)]
        ps = [jnp.exp(sc - l_ref[rows, col]) for sc, col in zip(scs, cols)]
        dss = [(p * (dp - dl_ref[rows, col])).astype(BF16) for p, dp, col in zip(ps, dps, cols)]
        pbs = [p.astype(BF16) for p in ps]
        for i, qc in enumerate(qcols):
            a, b = 2 * i, 2 * i + 1
            dqs[rows, qc] = jnp.where(low, _dot(dss[a], k2s[a]), _dot(dss[b], k2s[b])) * SCALE
            acck[acc_rows, qc] += _dot_tn(dss[a], qhs[a]) + _dot_tn(dss[b], qhs[b])
            accv[acc_rows, qc] += _dot_tn(pbs[a], dobs[a]) + _dot_tn(pbs[b], dobs[b])

    def body_streams(_, q_ref, kc_ref, vc_ref, l_ref, do_ref, dl_ref, c_ref, sa_ref, sb_ref,
                     dq_ref, dk_ref, dv_ref, acck, accv, dqs):
        acck[...] = jnp.zeros_like(acck)
        accv[...] = jnp.zeros_like(accv)
        for sb in range(2):
            cols = slice(sb * w, (sb + 1) * w)
            attend(q_ref, l_ref, do_ref, dl_ref, dqs, acck, accv, TOP, sb * w, kc_ref[:, cols], vc_ref[:, cols], TOP)
        tabs2 = [jnp.concatenate([jnp.tile(r[:, sb * LANES:(sb + 1) * LANES], (1, reps)) for sb in range(2)], axis=1)
                 for r in (c_ref, sa_ref, sb_ref)]
        dq_ref[...] = _rope_bwd(dqs[...], *tabs2).astype(BF16)
        dk_ref[...] = _rope_bwd(acck[...], *tabs2).astype(BF16)
        dv_ref[...] = accv[...].astype(BF16)

    def body_blocks(_, q_ref, kp_ref, kc_ref, vp_ref, vc_ref, l_ref, do_ref, dl_ref, cq, saq, sbq, ck, sak, sbk,
                    dq_ref, dk_ref, dv_ref, acck, accv, dqs):
        i = pl.program_id(0) % (nb // 2)

        @pl.when(i == 0)
        def _():
            acck[...] = jnp.zeros_like(acck)
            accv[...] = jnp.zeros_like(accv)

        refs = (q_ref, l_ref, do_ref, dl_ref, dqs, acck, accv)
        pl.when(i == 0)(lambda: attend(*refs, TOP, 0, kc_ref[TOP, :], vc_ref[TOP, :], TOP))
        pl.when(i != 0)(lambda: attend(
            *refs, TOP, 0, jnp.concatenate([kp_ref[...], kc_ref[TOP, :]], axis=0),
            jnp.concatenate([vp_ref[...], vc_ref[TOP, :]], axis=0),
            pl.ds(pl.multiple_of((2 * i - 1) * BLOCK, BLOCK), 2 * BLOCK)))
        attend(*refs, BOTTOM, 0, kc_ref[...], vc_ref[...], pl.ds(pl.multiple_of(2 * i * BLOCK, BLOCK), 2 * BLOCK))
        tq = [jnp.tile(r[...], (1, reps)) for r in (cq, saq, sbq)]
        dq_ref[...] = _rope_bwd(dqs[...], *tq).astype(BF16)

        @pl.when(i == nb // 2 - 1)
        def _():
            for r0 in range(0, nb * BLOCK, 2 * BLOCK):
                rows = slice(r0, r0 + 2 * BLOCK)
                tk = [jnp.tile(r[rows, :], (1, reps)) for r in (ck, sak, sbk)]
                dk_ref[rows, :] = _rope_bwd(acck[rows, :], *tk).astype(BF16)
                dv_ref[rows, :] = accv[rows, :].astype(BF16)

    if nb == 1:
        body = body_streams
        in_specs = [ANY] + [two] * 6 + [two_t] * 3
        args = (token, q, k, v, lse, do, dl, *tabs)
        out_specs = [two, two, two]
        acc_shape = (BLOCK, 2 * w)
    else:
        body = body_blocks
        stream = pl.BlockSpec((nb * BLOCK, w), lambda n: (0, n // (nb // 2)))
        stream_t = pl.BlockSpec((nb * BLOCK, LANES), lambda n: (0, n // (nb // 2)))
        in_specs = [ANY, two, before, two, before, two, two, two, two] + [two_t] * 3 + [stream_t] * 3
        args = (token, q, k, k, v, v, lse, do, dl, *tabs, *tabs)
        out_specs = [two, stream, stream]
        acc_shape = (nb * BLOCK, w)
    return _pallas_call(
        body, name=f"attn_bwd_d{d}", grid=(d * nb // 2,), out_shape=[_sds((ln, dw), BF16)] * 3,
        in_specs=in_specs, out_specs=out_specs,
        scratch_shapes=[pltpu.VMEM(acc_shape, F32), pltpu.VMEM(acc_shape, F32), pltpu.VMEM(two.block_shape, F32)],
        compiler_params=_params(1, 48),
    )(*args)


def _position():
    return lax.axis_index("x"), lax.axis_index("y"), lax.axis_index("c")


def _all_gather_small(xs, afters, name):
    n_in = 1 + len(afters)

    def body(*refs):
        x_ref, out_ref = refs[0], refs[n_in]
        send_sems, recv_sems, local_sem = refs[n_in + 1:]
        x, y, c = _position()
        my_rows = out_ref.at[4 * x + 2 * y + c]
        mine = pltpu.make_async_copy(x_ref, my_rows, local_sem)
        mine.start()
        copies = [pltpu.make_async_remote_copy(
            src_ref=x_ref, dst_ref=my_rows, send_sem=send_sems.at[k], recv_sem=recv_sems.at[k],
            device_id=(1 - x if k & 4 else x, 1 - y if k & 2 else y, 1 - c if k & 1 else c), device_id_type=MESH)
            for k in range(1, N_DEV)]
        for cp in copies:
            cp.start()
        for cp in copies:
            cp.wait_recv()
        for cp in copies:
            cp.wait_send()
        mine.wait()

    return _pallas_call(
        body, name=name, out_shape=_sds((N_DEV,) + xs.shape, xs.dtype),
        in_specs=[ANY] * n_in, out_specs=ANY,
        scratch_shapes=[pltpu.SemaphoreType.DMA((N_DEV,)), pltpu.SemaphoreType.DMA((N_DEV,)), pltpu.SemaphoreType.DMA],
    )(xs, *afters)


def _all_gather_relay(xs, name):
    def body(x_ref, out_ref, send_sems, recv_sems, local_sem):
        x, y, c = _position()
        me, sibling = (x, y, c), (x, y, 1 - c)
        xn, yn, diag = (1 - x, y, c), (x, 1 - y, c), (1 - x, 1 - y, c)
        src_nb = (x + c * (1 - 2 * x), y + (1 - c) * (1 - 2 * y), c)
        dst_nb = (x + (1 - c) * (1 - 2 * x), y + c * (1 - 2 * y), c)

        def rows(dev):
            return out_ref.at[4 * dev[0] + 2 * dev[1] + dev[2]]

        def copy(k, block, to, own=False):
            return pltpu.make_async_remote_copy(
                src_ref=x_ref if own else rows(block), dst_ref=rows(block),
                send_sem=send_sems.at[k], recv_sem=recv_sems.at[k], device_id=to, device_id_type=MESH)

        mine = pltpu.make_async_copy(x_ref, rows(me), local_sem)
        mine.start()
        first = [copy(1, me, xn, own=True), copy(2, me, yn, own=True), copy(0, me, sibling, own=True)]
        for cp in first:
            cp.start()
        copy(1, xn, me).wait_recv()
        copy(2, yn, me).wait_recv()
        relay = copy(3, src_nb, dst_nb)
        relay.start()
        passed = [copy(4, xn, sibling), copy(5, yn, sibling)]
        for cp in passed:
            cp.start()
        copy(3, diag, me).wait_recv()
        last = copy(6, diag, sibling)
        last.start()
        copy(0, sibling, me).wait_recv()
        for k, blk in ((4, (1 - x, y, 1 - c)), (5, (x, 1 - y, 1 - c)), (6, (1 - x, 1 - y, 1 - c))):
            copy(k, blk, me).wait_recv()
        for cp in first + [relay] + passed + [last]:
            cp.wait_send()
        mine.wait()

    return _pallas_call(
        body, name=name, out_shape=_sds((N_DEV,) + xs.shape, xs.dtype),
        in_specs=[ANY], out_specs=ANY,
        scratch_shapes=[pltpu.SemaphoreType.DMA((7,)), pltpu.SemaphoreType.DMA((7,)), pltpu.SemaphoreType.DMA],
    )(xs)


HBM_SPEC = pl.BlockSpec(memory_space=pltpu.HBM)
SEM_SPEC = pl.BlockSpec(memory_space=pltpu.SEMAPHORE)
EFFECT = pltpu.SideEffectType.DATAFLOW_SIDE_EFFECTING
def _plan_gather_own(src_refs, land_refs):
    x, y, c = _position()
    me = 4 * x + 2 * y + c
    peers = [(x, y, 1 - c), (1 - x, y, c), (x, 1 - y, c), (1 - x, 1 - y, c)]
    return [(land_refs[a].at[me], land_refs[a].at[me], (a, k), peer)
            for k, peer in enumerate(peers) for a in range(len(land_refs))]


def _plan_gather_pass(src_refs, land_refs):
    x, y, c = _position()
    chips = [(1 - x, y), (x, 1 - y), (1 - x, 1 - y)]
    return [(land_refs[a].at[4 * px + 2 * py + c], land_refs[a].at[4 * px + 2 * py + c], (a, j), (x, y, 1 - c))
            for j, (px, py) in enumerate(chips) for a in range(len(land_refs))]


def _plan_to_sibling(src_refs, land_refs):
    x, y, c = _position()
    return [(src_refs[a].at[2 * k + (1 - c)], land_refs[a].at[k], (a, k), (x, y, 1 - c))
            for k in range(4) for a in range(len(src_refs))]


def _plan_to_chips(src_refs, land_refs):
    x, y, c = _position()
    chips = [(1 - x, y), (x, 1 - y), (1 - x, 1 - y)]
    return [(src_refs[a].at[2 * px + py], land_refs[a].at[j], (a, j), (px, py, c))
            for j, (px, py) in enumerate(chips) for a in range(len(src_refs))]


def _split_start(srcs, lands, plan, n_sem, after, name):
    n_s, n_a = len(srcs), len(lands)
    n_b = n_s + n_a

    def body(*refs):
        src_refs, land_refs = refs[:n_s], refs[n_s:n_b]
        send_sems, recv_sems, token = refs[n_b + 1], refs[n_b + 2], refs[-1]
        for src, dst, (a, k), dev in plan(src_refs, land_refs):
            i = a * n_sem + k
            pltpu.make_async_remote_copy(src_ref=src, dst_ref=dst, send_sem=send_sems.at[i], recv_sem=recv_sems.at[i],
                                         device_id=dev, device_id_type=MESH).start()
        token[...] = jnp.zeros_like(token)

    bufs = list(srcs) + list(lands)
    res = pl.pallas_call(
        body, name=name,
        out_shape=(pltpu.SemaphoreType.DMA((n_a * n_sem,)), pltpu.SemaphoreType.DMA((n_a * n_sem,)),
                   *[pltpu.HBM(t.shape, t.dtype) for t in bufs], _plain((8, LANES), F32)),
        in_specs=[HBM_SPEC] * n_b + [ANY],
        out_specs=(SEM_SPEC, SEM_SPEC, *[HBM_SPEC] * n_b, pl.BlockSpec(memory_space=pltpu.VMEM)),
        input_output_aliases={i: 2 + i for i in range(n_b)},
        compiler_params=pltpu.CompilerParams(has_side_effects=EFFECT),
    )(*[pltpu.with_memory_space_constraint(t, pltpu.HBM) for t in bufs], after)
    return (res[0], res[1], res[2:2 + n_s], res[2 + n_s:2 + n_b]), res[-1]


def _split_wait(started, plan, after, name, first=0, n_sem=None):
    send_sems, recv_sems, srcs, lands = started
    n_s, n_a = len(srcs), len(lands)
    n_b = n_s + n_a
    n_sem = n_sem or send_sems.shape[0] // n_a

    def body(*refs):
        src_refs, land_refs = refs[:n_s], refs[n_s:n_b]
        s_sems, r_sems = refs[n_b], refs[n_b + 1]
        for src, dst, (a, k), dev in plan(src_refs, land_refs):
            i = (first + a) * n_sem + k
            cp = pltpu.make_async_remote_copy(src_ref=src, dst_ref=dst, send_sem=s_sems.at[i], recv_sem=r_sems.at[i],
                                              device_id=dev, device_id_type=MESH)
            cp.wait_send()
            cp.wait_recv()

    bufs = list(srcs) + list(lands)
    res = pl.pallas_call(
        body, name=name, out_shape=tuple(pltpu.HBM(t.shape, t.dtype) for t in bufs),
        in_specs=[HBM_SPEC] * n_b + [SEM_SPEC, SEM_SPEC, ANY],
        out_specs=tuple([HBM_SPEC] * n_b),
        input_output_aliases={i: i for i in range(n_b)},
        compiler_params=pltpu.CompilerParams(has_side_effects=EFFECT),
    )(*bufs, send_sems, recv_sems, after)
    return res[:n_s], res[n_s:]


SUBLANES = 8
ONE_STEP_BYTES = 2 << 20


def _row_tile(r):
    return max(t for t in range(SUBLANES, ROW_TILE + 1, SUBLANES) if r % t == 0)


def _rs_add_sibling(gp, recv, ck_arr, name):
    _, r, l = gp.shape
    out_shape = [_sds((r, l), F32), _sds((4, r, l), BF16)]
    if 4 * r * l * gp.dtype.itemsize <= ONE_STEP_BYTES:

        def body_whole(ck_ref, g_ref, r_ref, pf_ref, pb_ref):
            pb_ref[...] = (g_ref[...].astype(F32) + r_ref[...].astype(F32)).astype(BF16)
            own = ck_ref[1]
            pf_ref[...] = g_ref[own].astype(F32) + r_ref[own].astype(F32)

        whole = pl.BlockSpec((4, r, l), lambda i, ck: (0, 0, 0))
        return _pallas_call(
            body_whole, name=name,
            grid_spec=pltpu.PrefetchScalarGridSpec(
                num_scalar_prefetch=1, grid=(1,),
                in_specs=[pl.BlockSpec((4, None, r, l), lambda i, ck: (0, ck[0], 0, 0)), whole],
                out_specs=[pl.BlockSpec((r, l), lambda i, ck: (0, 0)), whole]),
            out_shape=out_shape, compiler_params=_params(1, 32),
        )(ck_arr, gp.reshape(4, 2, r, l), recv)

    tr = r if r <= 4 * ROW_TILE else _row_tile(r)
    block = lambda k, ck: (k + ck[1] + 1) % 4

    def body(ck_ref, g_ref, r_ref, pf_ref, pb_ref):
        sm = g_ref[...].astype(F32) + r_ref[...].astype(F32)
        pf_ref[...] = sm
        pb_ref[...] = sm.astype(BF16)

    spec = pl.BlockSpec((None, tr, l), lambda i, k, ck: (block(k, ck), i, 0))
    return _pallas_call(
        body, name=name,
        grid_spec=pltpu.PrefetchScalarGridSpec(
            num_scalar_prefetch=1, grid=(r // tr, 4),
            in_specs=[pl.BlockSpec((None, tr, l), lambda i, k, ck: (2 * block(k, ck) + ck[0], i, 0)), spec],
            out_specs=[pl.BlockSpec((tr, l), lambda i, k, ck: (i, 0)), spec]),
        out_shape=out_shape, compiler_params=_params(2, 32),
    )(ck_arr, gp, recv)


def _adam_update(w, gv, m, v):
    nm = ADAM_B1 * m + (1.0 - ADAM_B1) * gv
    nv = ADAM_B2 * v + (1.0 - ADAM_B2) * (gv * gv)
    m_hat = nm / (1.0 - ADAM_B1 ** ADAM_STEP)
    v_hat = nv / (1.0 - ADAM_B2 ** ADAM_STEP)
    return -ADAM_LR * (m_hat / (jnp.sqrt(v_hat) + ADAM_EPS) + ADAM_WD * w), nm, nv


def _rs_finish_adamw(pf, recv, w, m, v, after, name):
    r, l = pf.shape
    tr = _row_tile(r)

    def body(_, p_ref, r_ref, w_ref, m_ref, v_ref, g_ref, d_ref, nm_ref, nv_ref):
        gv = ((p_ref[...] + r_ref[0].astype(F32)) + r_ref[1].astype(F32)) + r_ref[2].astype(F32)
        g_ref[...] = gv
        d_ref[...], nm_ref[...], nv_ref[...] = _adam_update(w_ref[...], gv, m_ref[...], v_ref[...])

    spec = pl.BlockSpec((tr, l), lambda i: (i, 0))
    return _pallas_call(
        body, name=name, grid=(r // tr,),
        in_specs=[ANY, spec, pl.BlockSpec((3, tr, l), lambda i: (0, i, 0)), spec, spec, spec], out_specs=[spec] * 4,
        out_shape=[_plain((r, l), F32)] * 4, compiler_params=_params(1, 32),
    )(after, pf, recv, w, m, v)


SMALL_ROWS = dict(norm_g=(0, 2), mem_norm_g=(2, 4), final_g=(4, 5), conv_w=(5, 8))
LOSS_ROWS = (8, 16)


def _sum_adamw_small(g, ck_arr, states):
    names = list(SMALL_ROWS)
    n_dev, n_rows, _ = g.shape

    def body(ck_ref, g_ref, gc_ref, *refs):
        ins, loss_ref, outs = refs[:3 * len(names)], refs[3 * len(names)], refs[3 * len(names) + 1:]

        def total(ref, lo, hi):
            acc = ref[0, lo:hi, :]
            for j in range(1, n_dev):
                acc = acc + ref[j, lo:hi, :]
            return acc

        loss_ref[...] = total(gc_ref, *LOSS_ROWS)
        for i, n in enumerate(names):
            gv = total(gc_ref if n == "conv_w" else g_ref, *SMALL_ROWS[n])
            w_ref, m_ref, v_ref = ins[3 * i:3 * i + 3]
            g_out, d_out, nm_out, nv_out = outs[4 * i:4 * i + 4]
            g_out[...] = gv
            d_out[...], nm_out[...], nv_out[...] = _adam_update(w_ref[...], gv, m_ref[...], v_ref[...])

    flat = [t for n in names for t in states[n]]
    mine = pl.BlockSpec((n_dev, n_rows, LANES), lambda i, ck: (0, 0, 2 * ck[1] + ck[0]))
    res = _pallas_call(
        body, name="sum_adamw_small",
        grid_spec=pltpu.PrefetchScalarGridSpec(
            num_scalar_prefetch=1, grid=(1,),
            in_specs=[_whole(g.shape), mine] + [_whole(t.shape) for t in flat],
            out_specs=[_whole((SUBLANES, LANES))] + [_whole(states[n][0].shape) for n in names for _ in range(4)]),
        out_shape=[_plain((SUBLANES, LANES), F32)] + [_plain(states[n][0].shape, F32) for n in names for _ in range(4)],
        compiler_params=_params(1, 32),
    )(ck_arr, g, g, *flat)
    return res[0], {n: tuple(res[1 + 4 * i:5 + 4 * i]) for i, n in enumerate(names)}


def _finish(name, pf, recv, w, m, v, after):
    if name in ("attn_w_in", "conv_w_in"):
        res = _rs_finish_adamw(pf, recv, w.T, m.T, v.T, after, "rs_finish_adamw_" + name)
        return tuple(t.T for t in res)
    return _rs_finish_adamw(pf, recv, w, m, v, after, "rs_finish_adamw_" + name)


def kernel(x, mem, positions, norm_g, mem_norm_g, w_mem_kv, attn_w_in, attn_w_out, conv_w_in, conv_w, conv_w_out, final_g, loss_target, m_norm_g, m_mem_norm_g, m_w_mem_kv, m_attn_w_in, m_attn_w_out, m_conv_w_in, m_conv_w, m_conv_w_out, m_final_g, v_norm_g, v_mem_norm_g, v_w_mem_kv, v_attn_w_in, v_attn_w_out, v_conv_w_in, v_conv_w, v_conv_w_out, v_final_g):
    px, py, pc = _position()
    me = 4 * px + 2 * py + pc
    ck_arr = jnp.stack([pc, 2 * px + py]).astype(jnp.int32)
    x, mem, pos, tgt = x[0], mem[0], positions[0], loss_target[0]

    wg_in0 = _all_gather_relay(attn_w_in[0].astype(BF16), "gather_w_in0")
    def gather_pass(weights, after, name, first):
        _, lands = _split_wait(weights, _plan_gather_own, after, name + "_wait", first, 4)
        return _split_start([], lands, _plan_gather_pass, 3, after, name + "_pass_start")

    late = [attn_w_out[0].astype(BF16), w_mem_kv.astype(BF16).reshape(-1, w_mem_kv.shape[2]),
            jnp.pad(conv_w[0], ((0, 5), (0, 0))), conv_w_in[0].astype(BF16), conv_w_out[0].astype(BF16)]
    lands = [lax.dynamic_update_slice(lax.empty((N_DEV,) + t.shape, t.dtype), t[None], (me, 0, 0)) for t in late]
    (send_sems, recv_sems, _, lands), token = _split_start([], lands, _plan_gather_own, 4, wg_in0, "gather_late_start")
    rest0, conv_ws = (send_sems, recv_sems, [], lands[:3]), (send_sems, recv_sems, [], lands[3:])

    tabs = _rope_tables(pos)
    g0, g1 = norm_g[0:1], norm_g[1:2]

    hn0, qs, ks, vs, tabs_v, qm0, z0 = _inproj_attn(x, g0, wg_in0, tabs, token)
    os_, ls_ = [], []
    for j, d in enumerate(DILATIONS):
        if j == 2:
            rest0, token = gather_pass(rest0, ls_[1], "gather_rest", 0)
        o, l = _attn_fwd(qs[j], ks[j], vs[j], d, token)
        os_.append(o)
        ls_.append(l)

    conv_ws, token = gather_pass(conv_ws, ls_[2], "gather_conv", 3)
    _, (wg_out0, wg_kv, cw_all) = _split_wait(rest0, _plan_gather_pass, token, "gather_rest_pass_wait")
    cw = cw_all[:, 0:3].transpose(1, 0, 2).reshape(3, -1)
    kv = _memkv_fwd(mem, mem_norm_g, wg_kv)
    y0, y0_t, mo0, h1 = _post_attn(os_, ls_, qm0, kv[0], z0, x, wg_out0)
    _, (wg_in1, wg_out1) = _split_wait(conv_ws, _plan_gather_pass, h1, "gather_conv_pass_wait")
    w_out1 = wg_out1.reshape(-1, wg_out1.shape[2])

    hn1, bg, cg, u, qm1, z1 = _inproj_conv(h1, g1, wg_in1)
    y1, y1_t, mo1, dh2, dh2b, loss_acc, d_final_g = _post_conv_loss(
        bg, cg, u, qm1, kv[1], z1, h1, w_out1, cw, final_g.reshape(1, -1), tgt)

    d_w_out1 = _wgrad_rows(y1_t, dh2b, "wgrad_out1")
    dz1, dbg, dconv, dqm1, dkv1 = _bwd_post_conv(dh2b, w_out1, bg, cg, u, z1, qm1, kv[1], mo1, cw)
    dcg, du, dcw = _bwd_conv(dconv, cg, u, cw)
    dh1, dg1, dh1b, dproj1_t = _dgrad_norm([(dbg, 1), (dcg, 1), (du, 1), (dqm1, 1), (dz1, 1)], wg_in1, h1, g1, dh2,
                                           dh2, True, "dgrad_norm_conv")
    d_w_in1 = _wgrad_shards_t(dproj1_t, hn1, wg_in1.shape[2], "wgrad_in1")

    d_w_out0 = _wgrad_cols(y0_t, dh1b, wg_out0.shape[2], "wgrad_out0")

    gw = GROUP_WIDTH
    ones = (jnp.arange(gw)[:, None] // HEAD_DIM == jnp.arange(gw)[None, :] // HEAD_DIM).astype(BF16)
    res = _bwd_post_attn(dh1b, wg_out0, z0, os_, ls_, qm0, kv[0], mo0, ones, dg1)
    dz0, dos, dls, dqm0, dkv0 = res[0], res[1:4], res[4:7], res[7], res[8]
    d_w_kv, d_mem_g = _memkv_bwd([dkv0, dkv1], wg_kv, mem, mem_norm_g)

    names1 = ["conv_w_in", "conv_w_out", "attn_w_out", "w_mem_kv"]
    grads1 = [d_w_in1, d_w_out1, d_w_out0, d_w_kv]
    started, token = _split_start(grads1, [lax.empty((4,) + g.shape[1:], g.dtype) for g in grads1],
                                  _plan_to_sibling, 4, d_mem_g, "rs1_sibling_start")

    dqs, dks, dvs = [], [], []
    for j, d in enumerate(DILATIONS):
        if j == 1:
            grads1, from_sibling = _split_wait(started, _plan_to_sibling, dqs[0][0], "rs1_sibling_wait")
            parts1 = [_rs_add_sibling(g, r, ck_arr, "rs_add_sibling_" + n)
                      for g, r, n in zip(grads1, from_sibling, names1)]
            pbs1 = [pb for _, pb in parts1]
            started, token = _split_start(pbs1, [lax.empty((3,) + p.shape[1:], p.dtype) for p in pbs1],
                                          _plan_to_chips, 3, dg1, "rs1_chips_start")
        dq, dk, dv = _attn_bwd(qs[j], ks[j], vs[j], ls_[j], dos[j], dls[j], tabs_v[j], d, token)
        dqs.append((dq, d))
        dks.append((dk, d))
        dvs.append((dv, d))
    pieces0 = dqs + dks + dvs + [(dqm0, 1), (dz0, 1)]
    dproj0_t = _assemble_dproj_t(pieces0, N_DEV * wg_in0.shape[2], "assemble_dproj_attn")
    d_w_in0 = _wgrad_shards_t(dproj0_t, hn0, wg_in0.shape[2], "wgrad_in0")

    names0 = ["attn_w_in"]
    grads0 = [d_w_in0]
    started0, token0 = _split_start(grads0, [lax.empty((4,) + g.shape[1:], g.dtype) for g in grads0],
                                    _plan_to_sibling, 4, dg1, "rs0_sibling_start")
    _, from_chips1 = _split_wait(started, _plan_to_chips, token0, "rs1_chips_wait")
    shard = dict(attn_w_in=(attn_w_in[0], m_attn_w_in[0], v_attn_w_in[0]),
                 attn_w_out=(attn_w_out[0], m_attn_w_out[0], v_attn_w_out[0]),
                 conv_w_in=(conv_w_in[0], m_conv_w_in[0], v_conv_w_in[0]),
                 conv_w_out=(conv_w_out[0], m_conv_w_out[0], v_conv_w_out[0]),
                 w_mem_kv=tuple(t.reshape(-1, t.shape[2]) for t in (w_mem_kv, m_w_mem_kv, v_w_mem_kv)))
    finish1 = {n: (pf, r) for n, (pf, _), r in zip(names1, parts1, from_chips1)}
    big = {"conv_w_in": _finish("conv_w_in", *finish1["conv_w_in"], *shard["conv_w_in"], token0)}

    grads0, from_sibling = _split_wait(started0, _plan_to_sibling, big["conv_w_in"][1].T, "rs0_sibling_wait")
    parts0 = [_rs_add_sibling(g, r, ck_arr, "rs_add_sibling_" + n) for g, r, n in zip(grads0, from_sibling, names0)]
    pbs0 = [pb for _, pb in parts0]
    started0, token0 = _split_start(pbs0, [lax.empty((3,) + p.shape[1:], p.dtype) for p in pbs0],
                                    _plan_to_chips, 3, dg1, "rs0_chips_start")
    dx, dg0 = _dgrad_norm(pieces0, wg_in0, x, g0, dh1, token0, False, "dgrad_norm_attn")
    for n in names1[1:]:
        big[n] = _finish(n, *finish1[n], *shard[n], token0)

    small_part = jnp.concatenate([dg0, dg1, d_mem_g.reshape(2, -1), d_final_g, dcw[0:3]], axis=0)
    small_part = jnp.concatenate([small_part, jnp.broadcast_to(loss_acc[0, 0], small_part.shape)], axis=0)
    small_w = dict(norm_g=(norm_g, m_norm_g, v_norm_g), mem_norm_g=(mem_norm_g, m_mem_norm_g, v_mem_norm_g),
                   conv_w=(conv_w, m_conv_w, v_conv_w), final_g=(final_g, m_final_g, v_final_g))
    loss_tile, small_res = _sum_adamw_small(
        _all_gather_small(small_part, [big[n][1] for n in names1[1:]], "gather_small_grads"), ck_arr,
        {n: tuple(t.reshape(-1, t.shape[-1]) for t in wmv) for n, wmv in small_w.items()})
    loss = loss_tile[0, 0]
    for n, wmv in small_w.items():
        big[n] = tuple(t.reshape(wmv[0].shape) for t in small_res[n])

    _, from_chips0 = _split_wait(started0, _plan_to_chips, big["final_g"][1], "rs0_chips_wait")
    for n, (pf, _), r in zip(names0, parts0, from_chips0):
        big[n] = _finish(n, pf, r, *shard[n], big["final_g"][1])
    for n in ("attn_w_in", "attn_w_out", "conv_w_in", "conv_w_out"):
        big[n] = tuple(t[None] for t in big[n])
    big["w_mem_kv"] = tuple(t.reshape(w_mem_kv.shape) for t in big["w_mem_kv"])

    order = ["norm_g", "mem_norm_g", "w_mem_kv", "attn_w_in", "attn_w_out", "conv_w_in", "conv_w", "conv_w_out", "final_g"]
    return (loss, dx[None], *[big[n][0] for n in order], *[big[n][1] for n in order],
            *[big[n][2] for n in order], *[big[n][3] for n in order])
```

```python
import jax
import jax.numpy as jnp
from jax import lax
from jax.experimental import pallas as pl
from jax.experimental.pallas import tpu as pltpu

F32 = jnp.float32
BF16 = jnp.bfloat16

N_DEV = 8
D_MODEL = 1024
HEAD_DIM = 64
ROT_DIM = HEAD_DIM // 4
ROPE_THETA = 500000.0
DILATIONS = (1, 4, 16)
HEADS_PER_GROUP = 8
GROUP_WIDTH = HEADS_PER_GROUP * HEAD_DIM
BLOCK = 128
N_MEM = 256
MEM_HEADS = 4
MEM_WIDTH = MEM_HEADS * HEAD_DIM
CONV_WIDTH = D_MODEL
EPS = 1e-6
SCALE = HEAD_DIM ** -0.5
NEG = -1e30

ADAM_LR = 0.001
ADAM_B1 = 0.9
ADAM_B2 = 0.999
ADAM_EPS = 1e-08
ADAM_WD = 0.01
ADAM_STEP = 10

ROW_TILE = 256
WGRAD_SHARDS = 4
LANES = 128
MESH = pl.DeviceIdType.MESH
ANY = pl.BlockSpec(memory_space=pl.ANY)


def _pallas_call(body, **kw):
    call = pl.pallas_call(body, **kw)

    def run(*args):
        pinned = [pltpu.with_memory_space_constraint(a, pltpu.HBM) if jnp.issubdtype(a.dtype, jnp.floating) else a
                  for a in args]
        return call(*pinned)

    return run


def _dot(a, b):
    return lax.dot_general(a, b, (((1,), (0,)), ((), ())), preferred_element_type=F32)


def _dot_nt(a, b):
    return lax.dot_general(a, b, (((1,), (1,)), ((), ())), preferred_element_type=F32)


def _dot_tn(a, b):
    return lax.dot_general(a, b, (((0,), (0,)), ((), ())), preferred_element_type=F32)


def _params(n_grid, vmem_mb=48):
    return pltpu.CompilerParams(dimension_semantics=("arbitrary",) * n_grid, vmem_limit_bytes=vmem_mb << 20)


def _rows(width, tm=ROW_TILE):
    return pl.BlockSpec((tm, width), lambda i: (i, 0))


def _view_rows(width, d, tm=ROW_TILE):
    return pl.BlockSpec((tm // d, d * width), lambda i: (i, 0))


def _whole(shape):
    return pl.BlockSpec(shape, lambda *_: (0,) * len(shape))


def _resident(shape):
    return pl.BlockSpec(shape, lambda *_: (0,) * len(shape), pipeline_mode=pl.Buffered(1))


def _sds(shape, dtype):
    return pltpu.HBM(shape, dtype)


def _plain(shape, dtype):
    return jax.ShapeDtypeStruct(shape, dtype)


def _silu_parts(z):
    sg = jax.nn.sigmoid(z)
    return z * sg, sg * (1.0 + z * (1.0 - sg))


def _to_view(scr, val, out_ref, d):
    tm, w = val.shape
    if d == 1:
        out_ref[...] = val.astype(out_ref.dtype)
        return
    for cb in range(w // LANES):
        scr[cb] = val[:, cb * LANES:(cb + 1) * LANES]
    for r in range(d):
        for cb in range(w // LANES):
            lo = r * w + cb * LANES
            out_ref[:, lo:lo + LANES] = scr[cb, pl.ds(r, tm // d, stride=d), :].astype(out_ref.dtype)


def _from_view(scr, in_ref, d):
    if d == 1:
        return in_ref[...].astype(F32)
    nc, tm, _ = scr.shape
    w = nc * LANES
    for r in range(d):
        for cb in range(nc):
            lo = r * w + cb * LANES
            scr[cb, pl.ds(r, tm // d, stride=d), :] = in_ref[:, lo:lo + LANES].astype(F32)
    return jnp.concatenate([scr[cb] for cb in range(nc)], axis=1)


def _view_scratch(width, tm=ROW_TILE):
    return pltpu.VMEM((width // LANES, tm, LANES), F32)


def _rope_tables(pos):
    half = ROT_DIM // 2
    inv_freq = ROPE_THETA ** (-jnp.arange(half, dtype=F32) * (2.0 / ROT_DIM))
    ang = pos.astype(F32)[:, None] * inv_freq
    cos, sin = jnp.cos(ang), jnp.sin(ang)
    s = pos.shape[0]
    z8 = jnp.zeros((s, half), F32)
    rest = HEAD_DIM - ROT_DIM
    cosf = jnp.concatenate([cos, cos, jnp.ones((s, rest), F32)], axis=1)
    sa = jnp.concatenate([-sin, z8, jnp.zeros((s, rest), F32)], axis=1)
    sb = jnp.concatenate([z8, sin, jnp.zeros((s, rest), F32)], axis=1)
    return tuple(jnp.tile(t, (1, LANES // HEAD_DIM)) for t in (cosf, sa, sb))


def _rope_fwd(t, cv, sav, sbv):
    w = t.shape[1]
    return t * cv + pltpu.roll(t, w - ROT_DIM // 2, 1) * sav + pltpu.roll(t, ROT_DIM // 2, 1) * sbv


def _rope_bwd(g, cv, sav, sbv):
    w = g.shape[1]
    return g * cv + pltpu.roll(g * sav, ROT_DIM // 2, 1) + pltpu.roll(g * sbv, w - ROT_DIM // 2, 1)


def _joined_columns(wg_ref):
    assert wg_ref.shape[2] % LANES == 0
    return jnp.concatenate([wg_ref[j] for j in range(N_DEV)], axis=1)


def _join_once(wg_ref, w_scr):
    c = wg_ref.shape[2]

    @pl.when(pl.program_id(0) == 0)
    def _():
        for j in range(N_DEV):
            w_scr[:, j * c:(j + 1) * c] = wg_ref[j]


def _inproj_attn(x, g, wg, tabs, token):
    s, d_model = x.shape
    gw = GROUP_WIDTH
    n = N_DEV * wg.shape[2]
    nz = n - 9 * gw - MEM_WIDTH
    reps = gw // LANES
    tm = ROW_TILE

    def body(_, x_ref, g_ref, w_ref, c_ref, sa_ref, sb_ref, hn_ref, *rest):
        outs, (wj, scr, tscr) = rest[:-3], rest[-3:]
        q_refs, k_refs, v_refs, t_refs, qm_ref, z_ref = outs[0:3], outs[3:6], outs[6:9], outs[9:18], outs[18], outs[19]
        _join_once(w_ref, wj)
        xb = x_ref[...]
        r = lax.rsqrt(jnp.mean(xb * xb, axis=-1, keepdims=True) + EPS)
        hn = ((xb * r) * g_ref[...]).astype(BF16)
        hn_ref[...] = hn
        proj = lambda lo, hi: _dot(hn, wj[:, lo:hi])
        tab = (c_ref[...], sa_ref[...], sb_ref[...])
        cv, sav, sbv = [jnp.tile(t, (1, reps)) for t in tab]
        for j, d in enumerate(DILATIONS):
            tq = _rope_fwd(proj(j * gw, (j + 1) * gw), cv, sav, sbv)
            _to_view(scr, tq * SCALE, q_refs[j], d)
            tk = _rope_fwd(proj((3 + j) * gw, (4 + j) * gw), cv, sav, sbv)
            _to_view(scr, tk, k_refs[j], d)
            _to_view(scr, proj((6 + j) * gw, (7 + j) * gw), v_refs[j], d)
            for i in range(3):
                _to_view(tscr, tab[i], t_refs[3 * j + i], d)
        qm_ref[...] = proj(9 * gw, 9 * gw + MEM_WIDTH).astype(BF16)
        z_ref[...] = proj(9 * gw + MEM_WIDTH, n)

    views = [_sds((s // d, d * gw), BF16) for d in DILATIONS]
    tviews = [_sds((s // d, d * LANES), F32) for d in DILATIONS for _ in range(3)]
    out_shape = [_sds((s, d_model), BF16)] + views * 3 + tviews + [_sds((s, MEM_WIDTH), BF16), _sds((s, nz), F32)]
    vspecs = [_view_rows(gw, d, tm) for d in DILATIONS]
    tspecs = [_view_rows(LANES, d, tm) for d in DILATIONS for _ in range(3)]
    out_specs = [_rows(d_model, tm)] + vspecs * 3 + tspecs + [_rows(MEM_WIDTH, tm), _rows(nz, tm)]
    res = _pallas_call(
        body, name="inproj_attn", grid=(s // tm,), out_shape=out_shape,
        in_specs=[ANY, _rows(d_model, tm), _whole((1, d_model)), _resident(wg.shape)] + [_rows(LANES, tm)] * 3,
        out_specs=out_specs,
        scratch_shapes=[pltpu.VMEM((d_model, n), BF16), _view_scratch(gw, tm), _view_scratch(LANES, tm)],
        compiler_params=_params(1, 60),
    )(token, x, g, wg, *tabs)
    tabs_v = [res[10 + 3 * j:13 + 3 * j] for j in range(3)]
    return res[0], res[1:4], res[4:7], res[7:10], tabs_v, res[19], res[20]


def _band_mask(n_keys):
    qi = lax.broadcasted_iota(jnp.int32, (BLOCK, n_keys), 0)
    kj = lax.broadcasted_iota(jnp.int32, (BLOCK, n_keys), 1)
    if n_keys == BLOCK:
        return kj <= qi
    return jnp.logical_or(jnp.logical_and(kj < BLOCK, kj >= qi), jnp.logical_and(kj >= BLOCK, (kj - BLOCK) <= qi))


def _low_head_lanes():
    return lax.broadcasted_iota(jnp.int32, (1, LANES), 1) < HEAD_DIM


def _split_pair(t, low):
    zero = jnp.zeros_like(t)
    return jnp.where(low, t, zero), jnp.where(low, zero, t)


def _pair_specs(d, nb, w):
    if nb == 1:
        return pl.BlockSpec((BLOCK, 2 * w), lambda n: (0, n)), None
    half = nb // 2
    two = pl.BlockSpec((2 * BLOCK, w), lambda n: (n % half, n // half))
    before = pl.BlockSpec((BLOCK, w), lambda n: (jnp.maximum(2 * (n % half) - 1, 0), n // half))
    return two, before


def _head_tiles(w, col0):
    return ([slice(p * LANES, (p + 1) * LANES) for p in range(w // LANES)],
            [slice(col0 + p * LANES, col0 + (p + 1) * LANES) for p in range(w // LANES)])


def _attend_fwd(q_ref, o_ref, lse_ref, rows, col0, kk, vv):
    w = kk.shape[1]
    valid = _band_mask(kk.shape[0])
    low = _low_head_lanes()
    pairs, qcols = _head_tiles(w, col0)
    qs_ = [h for qc in qcols for h in _split_pair(q_ref[rows, qc], low)]
    k2s = [kk[:, pr] for pr in pairs for _ in range(2)]
    scs = [jnp.where(valid, _dot_nt(qh, k2), NEG) for qh, k2 in zip(qs_, k2s)]
    ms = [jnp.max(sc, axis=-1, keepdims=True) for sc in scs]
    ps = [jnp.exp(sc - m) for sc, m in zip(scs, ms)]
    ls = [jnp.sum(p, axis=-1, keepdims=True) for p in ps]
    pns = [(p * (1.0 / l)).astype(BF16) for p, l in zip(ps, ls)]
    for i, (pr, qc) in enumerate(zip(pairs, qcols)):
        v2 = vv[:, pr]
        a, b = 2 * i, 2 * i + 1
        o_ref[rows, qc] = jnp.where(low, _dot(pns[a], v2), _dot(pns[b], v2))
        lse_ref[rows, qc] = jnp.where(low, ms[a] + jnp.log(ls[a]), ms[b] + jnp.log(ls[b]))


TOP, BOTTOM = slice(0, BLOCK), slice(BLOCK, 2 * BLOCK)


def _attn_fwd(q, k, v, d, token):
    ln, dw = q.shape
    w = dw // d
    nb = ln // BLOCK
    two, before = _pair_specs(d, nb, w)

    def body_streams(_, q_ref, kc_ref, vc_ref, o_ref, lse_ref):
        for sb in range(2):
            cols = slice(sb * w, (sb + 1) * w)
            _attend_fwd(q_ref, o_ref, lse_ref, TOP, sb * w, kc_ref[:, cols], vc_ref[:, cols])

    def body_blocks(_, q_ref, kp_ref, kc_ref, vp_ref, vc_ref, o_ref, lse_ref):
        first = pl.program_id(0) % (nb // 2) == 0
        pl.when(first)(lambda: _attend_fwd(q_ref, o_ref, lse_ref, TOP, 0, kc_ref[TOP, :], vc_ref[TOP, :]))
        pl.when(jnp.logical_not(first))(lambda: _attend_fwd(
            q_ref, o_ref, lse_ref, TOP, 0, jnp.concatenate([kp_ref[...], kc_ref[TOP, :]], axis=0),
            jnp.concatenate([vp_ref[...], vc_ref[TOP, :]], axis=0)))
        _attend_fwd(q_ref, o_ref, lse_ref, BOTTOM, 0, kc_ref[...], vc_ref[...])

    if nb == 1:
        body, in_specs, args = body_streams, [ANY, two, two, two], (token, q, k, v)
    else:
        body, in_specs, args = body_blocks, [ANY, two, before, two, before, two], (token, q, k, k, v, v)
    return _pallas_call(
        body, name=f"attn_fwd_d{d}", grid=(d * nb // 2,), out_shape=[_sds((ln, dw), F32)] * 2,
        in_specs=in_specs, out_specs=[two, two], compiler_params=_params(1, 32),
    )(*args)


def _memkv_fwd(mem, g, w):
    n_layers = g.shape[0]
    rows = D_MODEL // N_DEV

    def body(mem_ref, g_ref, w_ref, *kv_refs):
        mb = mem_ref[...]
        r = lax.rsqrt(jnp.mean(mb * mb, axis=-1, keepdims=True) + EPS)
        mn = ((mb * r) * g_ref[...]).astype(BF16)
        kv = _dot(mn, w_ref[...].reshape(D_MODEL, 2 * MEM_WIDTH)).astype(BF16)
        for layer, kv_ref in enumerate(kv_refs):
            @pl.when(pl.program_id(0) == layer)
            def _(kv_ref=kv_ref):
                kv_ref[...] = kv

    return _pallas_call(
        body, name="memkv_fwd", grid=(n_layers,),
        out_shape=[_sds((N_MEM, 2 * MEM_WIDTH), BF16)] * n_layers,
        in_specs=[_whole(mem.shape), pl.BlockSpec((None, 1, D_MODEL), lambda l: (l, 0, 0)),
                  pl.BlockSpec((N_DEV, rows, 2 * MEM_WIDTH), lambda l: (0, l, 0))],
        out_specs=[_whole((N_MEM, 2 * MEM_WIDTH))] * n_layers,
        compiler_params=_params(1, 32),
    )(mem, g.reshape(n_layers, 1, D_MODEL), w)


def _mix_groups(os_, ls_):
    mx = jnp.maximum(jnp.maximum(ls_[0], ls_[1]), ls_[2])
    es = [jnp.exp(t - mx) for t in ls_]
    inv = 1.0 / (es[0] + es[1] + es[2])
    ws = [e * inv for e in es]
    mix = ws[0] * os_[0] + ws[1] * os_[1] + ws[2] * os_[2]
    return ws, mix


MEM_PAIRS = [slice(p * LANES, (p + 1) * LANES) for p in range(MEM_WIDTH // LANES)]


def _mem_probs(qhs, k2s):
    scs = [_dot_nt(qh, k2) * SCALE for qh, k2 in zip(qhs, k2s)]
    es = [jnp.exp(sc - jnp.max(sc, axis=-1, keepdims=True)) for sc in scs]
    return [e * (1.0 / jnp.sum(e, axis=-1, keepdims=True)) for e in es]


def _mem_attn_into(qm, kv_ref, mo_ref):
    low = _low_head_lanes()
    qhs = [h for pr in MEM_PAIRS for h in _split_pair(qm[:, pr], low)]
    k2s = [kv_ref[:, pr] for pr in MEM_PAIRS for _ in range(2)]
    ps = [p.astype(BF16) for p in _mem_probs(qhs, k2s)]
    for i, pr in enumerate(MEM_PAIRS):
        v2 = kv_ref[:, MEM_WIDTH + i * LANES:MEM_WIDTH + (i + 1) * LANES]
        mo_ref[:, pr] = jnp.where(low, _dot(ps[2 * i], v2), _dot(ps[2 * i + 1], v2))


def _mem_attn_bwd(qm, kv_ref, dmem, dqm_ref, dkv_ref):
    low = _low_head_lanes()
    dmb = dmem.astype(BF16)
    vps = [slice(MEM_WIDTH + i * LANES, MEM_WIDTH + (i + 1) * LANES) for i in range(len(MEM_PAIRS))]
    qhs = [h for pr in MEM_PAIRS for h in _split_pair(qm[:, pr], low)]
    dhs = [h for pr in MEM_PAIRS for h in _split_pair(dmb[:, pr], low)]
    k2s = [kv_ref[:, pr] for pr in MEM_PAIRS for _ in range(2)]
    v2s = [kv_ref[:, vp] for vp in vps for _ in range(2)]
    ps = _mem_probs(qhs, k2s)
    dps = [_dot_nt(dh, v2) for dh, v2 in zip(dhs, v2s)]
    dss = [(p * (dp - jnp.sum(dp * p, axis=-1, keepdims=True)) * SCALE).astype(BF16) for p, dp in zip(ps, dps)]
    pbs = [p.astype(BF16) for p in ps]
    for i, (pr, vp) in enumerate(zip(MEM_PAIRS, vps)):
        a, b = 2 * i, 2 * i + 1
        dqm_ref[:, pr] = jnp.where(low, _dot(dss[a], k2s[a]), _dot(dss[b], k2s[b])).astype(BF16)
        dkv_ref[:, pr] += _dot_tn(dss[a], qhs[a]) + _dot_tn(dss[b], qhs[b])
        dkv_ref[:, vp] += _dot_tn(pbs[a], dhs[a]) + _dot_tn(pbs[b], dhs[b])


def _post_attn(os_, ls_, qm, kv, z, x, wg_out):
    s, d_model = x.shape
    gw = GROUP_WIDTH
    nb = gw + MEM_WIDTH
    tm = ROW_TILE

    def body(o0, o1, o2, l0, l1, l2, qm_ref, kv_ref, z_ref, x_ref, w_ref, y_ref, yt_ref, mo_ref, h_ref, s0, s1):
        ov, lv = [], []
        for o_ref, l_ref, d in zip((o0, o1, o2), (l0, l1, l2), DILATIONS):
            ov.append(_from_view(s0, o_ref, d))
            lv.append(_from_view(s1, l_ref, d))
        _, mix = _mix_groups(ov, lv)
        _mem_attn_into(qm_ref[...], kv_ref, mo_ref)
        sz, _ = _silu_parts(z_ref[...])
        y_ref[:, :gw] = (mix * sz[:, :gw]).astype(BF16)
        y_ref[:, gw:] = (mo_ref[...] * sz[:, gw:]).astype(BF16)
        y = y_ref[...]
        yt_ref[...] = y.T
        h_ref[...] = x_ref[...] + _dot(y, _joined_columns(w_ref))

    vspecs = [_view_rows(gw, d) for d in DILATIONS]
    return _pallas_call(
        body, name="post_attn", grid=(s // tm,),
        out_shape=[_sds((s, nb), BF16), _sds((nb, s), BF16), _sds((s, MEM_WIDTH), F32), _sds((s, d_model), F32)],
        in_specs=vspecs * 2 + [_rows(MEM_WIDTH), _whole(kv.shape), _rows(nb), _rows(d_model), _whole(wg_out.shape)],
        out_specs=[_rows(nb), pl.BlockSpec((nb, tm), lambda i: (0, i)), _rows(MEM_WIDTH), _rows(d_model)],
        scratch_shapes=[_view_scratch(gw), _view_scratch(gw)],
        compiler_params=_params(1, 40),
    )(*os_, *ls_, qm, kv, z, x, wg_out)


def _inproj_conv(x, g, wg):
    s, d_model = x.shape
    c = CONV_WIDTH
    n = N_DEV * wg.shape[2]
    nz = n - 3 * c - MEM_WIDTH
    tm = ROW_TILE

    def body(x_ref, g_ref, w_ref, hn_ref, bg_ref, cg_ref, u_ref, qm_ref, z_ref, wj):
        _join_once(w_ref, wj)
        xb = x_ref[...]
        r = lax.rsqrt(jnp.mean(xb * xb, axis=-1, keepdims=True) + EPS)
        hn = ((xb * r) * g_ref[...]).astype(BF16)
        hn_ref[...] = hn
        bg_ref[...] = _dot(hn, wj[:, 0:c])
        cg_ref[...] = _dot(hn, wj[:, c:2 * c])
        u_ref[...] = _dot(hn, wj[:, 2 * c:3 * c])
        qm_ref[...] = _dot(hn, wj[:, 3 * c:3 * c + MEM_WIDTH]).astype(BF16)
        z_ref[...] = _dot(hn, wj[:, 3 * c + MEM_WIDTH:])

    return _pallas_call(
        body, name="inproj_conv", grid=(s // tm,),
        out_shape=[_sds((s, d_model), BF16)] + [_sds((s, c), F32)] * 3 + [_sds((s, MEM_WIDTH), BF16), _sds((s, nz), F32)],
        in_specs=[_rows(d_model), _whole((1, d_model)), _resident(wg.shape)],
        out_specs=[_rows(d_model)] + [_rows(c)] * 3 + [_rows(MEM_WIDTH), _rows(nz)],
        scratch_shapes=[pltpu.VMEM((d_model, n), BF16)],
        compiler_params=_params(1, 60),
    )(x, g, wg)


HALO = 8


def _halo_before(width, tm=ROW_TILE):
    return pl.BlockSpec((HALO, width), lambda i: (jnp.maximum(i * (tm // HALO) - 1, 0), 0))


def _halo_after(width, n_rows, tm=ROW_TILE):
    return pl.BlockSpec((HALO, width), lambda i: (jnp.minimum((i + 1) * (tm // HALO), n_rows // HALO - 1), 0))


def _conv_taps(cg_ref, u_ref, cgh_ref, uh_ref, i):
    a = cg_ref[...] * u_ref[...]
    ah = jnp.where(i > 0, cgh_ref[...] * uh_ref[...], 0.0)
    row = lax.broadcasted_iota(jnp.int32, a.shape, 0)
    a1 = jnp.where(row == 0, ah[HALO - 1:HALO], pltpu.roll(a, 1, 0))
    a2 = jnp.where(row == 0, ah[HALO - 2:HALO - 1], jnp.where(row == 1, ah[HALO - 1:HALO], pltpu.roll(a, 2, 0)))
    return a, a1, a2


def _post_conv_loss(bg, cg, u, qm, kv, z, h1, w_out, cw, gf, tgt):
    s, d = h1.shape
    c = CONV_WIDTH
    nb = c + MEM_WIDTH
    tm = ROW_TILE

    def body(bg_ref, cg_ref, u_ref, cgh_ref, uh_ref, qm_ref, kv_ref, z_ref, h_ref, w_ref, cw_ref, gf_ref, t_ref,
             y_ref, yt_ref, mo_ref, dh_ref, dhb_ref, loss_ref, dgf_ref):
        i = pl.program_id(0)
        a, a1, a2 = _conv_taps(cg_ref, u_ref, cgh_ref, uh_ref, i)
        conv = cw_ref[0:1, :] * a2 + cw_ref[1:2, :] * a1 + cw_ref[2:3, :] * a
        mix = bg_ref[...] * conv
        _mem_attn_into(qm_ref[...], kv_ref, mo_ref)
        sz, _ = _silu_parts(z_ref[...])
        y_ref[:, :c] = (mix * sz[:, :c]).astype(BF16)
        y_ref[:, c:] = (mo_ref[...] * sz[:, c:]).astype(BF16)
        y = y_ref[...]
        yt_ref[...] = y.T
        h2 = h_ref[...] + _dot(y, w_ref[...])
        r = lax.rsqrt(jnp.mean(h2 * h2, axis=-1, keepdims=True) + EPS)
        nh = h2 * r
        gfv = gf_ref[...]
        diff = nh * gfv - t_ref[...]
        dout = diff * (1.0 / d)
        dn = dout * gfv
        dh2 = r * dn - h2 * ((r * r * r) * jnp.mean(dn * h2, axis=-1, keepdims=True))
        dh_ref[...] = dh2
        dhb_ref[...] = dh2.astype(BF16)

        @pl.when(i == 0)
        def _():
            loss_ref[...] = jnp.zeros_like(loss_ref)
            dgf_ref[...] = jnp.zeros_like(dgf_ref)

        loss_ref[...] += 0.5 * jnp.sum(jnp.mean(diff * diff, axis=-1, keepdims=True))
        dgf_ref[...] += jnp.sum(dout * nh, axis=0, keepdims=True)

    return _pallas_call(
        body, name="post_conv_loss", grid=(s // tm,),
        out_shape=[_sds((s, nb), BF16), _sds((nb, s), BF16), _sds((s, MEM_WIDTH), F32), _sds((s, d), F32),
                   _sds((s, d), BF16), _plain((8, LANES), F32), _plain((1, d), F32)],
        in_specs=[_rows(c)] * 3 + [_halo_before(c)] * 2 + [_rows(MEM_WIDTH), _whole(kv.shape), _rows(nb), _rows(d),
                  _whole(w_out.shape), _whole(cw.shape), _whole((1, d)), _rows(d)],
        out_specs=[_rows(nb), pl.BlockSpec((nb, tm), lambda i: (0, i)), _rows(MEM_WIDTH), _rows(d), _rows(d),
                   _whole((8, LANES)), _whole((1, d))],
        compiler_params=_params(1, 48),
    )(bg, cg, u, cg, u, qm, kv, z, h1, w_out, cw, gf, tgt)


def _bwd_post_conv(dhb, w_out, bg, cg, u, z, qm, kv, mo, cw):
    s = dhb.shape[0]
    c = CONV_WIDTH
    nb = c + MEM_WIDTH

    def body(dh_ref, w_ref, bg_ref, cg_ref, u_ref, cgh_ref, uh_ref, z_ref, qm_ref, kv_ref, mo_ref, cw_ref,
             dz_ref, dbg_ref, dc_ref, dqm_ref, dkv_ref):
        i = pl.program_id(0)

        @pl.when(i == 0)
        def _():
            dkv_ref[...] = jnp.zeros_like(dkv_ref)

        dy = _dot_nt(dh_ref[...], w_ref[...])
        sz, dsz = _silu_parts(z_ref[...])
        a, a1, a2 = _conv_taps(cg_ref, u_ref, cgh_ref, uh_ref, i)
        conv = cw_ref[0:1, :] * a2 + cw_ref[1:2, :] * a1 + cw_ref[2:3, :] * a
        bgv = bg_ref[...]
        dz_ref[:, :c] = (dy[:, :c] * (bgv * conv) * dsz[:, :c]).astype(BF16)
        dz_ref[:, c:] = (dy[:, c:] * mo_ref[...] * dsz[:, c:]).astype(BF16)
        dbr = dy * sz
        dmix = dbr[:, :c]
        dbg_ref[...] = (dmix * conv).astype(BF16)
        dc_ref[...] = dmix * bgv
        _mem_attn_bwd(qm_ref[...], kv_ref, dbr[:, c:], dqm_ref, dkv_ref)

    return _pallas_call(
        body, name="bwd_post_conv", grid=(s // ROW_TILE,),
        out_shape=[_sds((s, nb), BF16), _sds((s, c), BF16), _sds((s, c), F32), _sds((s, MEM_WIDTH), BF16),
                   _plain(kv.shape, F32)],
        in_specs=[_rows(D_MODEL), _whole(w_out.shape)] + [_rows(c)] * 3 + [_halo_before(c)] * 2
                 + [_rows(nb), _rows(MEM_WIDTH), _whole(kv.shape), _rows(MEM_WIDTH), _whole(cw.shape)],
        out_specs=[_rows(nb), _rows(c), _rows(c), _rows(MEM_WIDTH), _whole(kv.shape)],
        compiler_params=_params(1, 48),
    )(dhb, w_out, bg, cg, u, cg, u, z, qm, kv, mo, cw)


def _bwd_conv(dconv, cg, u, cw):
    s, c = dconv.shape
    tm = ROW_TILE
    last = s // tm - 1

    def body(dc_ref, dcn_ref, cg_ref, u_ref, cgh_ref, uh_ref, cw_ref, dcg_ref, du_ref, dcw_ref):
        i = pl.program_id(0)

        @pl.when(i == 0)
        def _():
            dcw_ref[...] = jnp.zeros_like(dcw_ref)

        dc = dc_ref[...]
        dcn = jnp.where(i < last, dcn_ref[...], 0.0)
        row = lax.broadcasted_iota(jnp.int32, dc.shape, 0)
        d1 = jnp.where(row == tm - 1, dcn[0:1], pltpu.roll(dc, tm - 1, 0))
        d2 = jnp.where(row == tm - 1, dcn[1:2], jnp.where(row == tm - 2, dcn[0:1], pltpu.roll(dc, tm - 2, 0)))
        da = cw_ref[2:3, :] * dc + cw_ref[1:2, :] * d1 + cw_ref[0:1, :] * d2
        a, a1, a2 = _conv_taps(cg_ref, u_ref, cgh_ref, uh_ref, i)
        dcg_ref[...] = (da * u_ref[...]).astype(BF16)
        du_ref[...] = (da * cg_ref[...]).astype(BF16)
        dcw_ref[0:1, :] += jnp.sum(dc * a2, axis=0, keepdims=True)
        dcw_ref[1:2, :] += jnp.sum(dc * a1, axis=0, keepdims=True)
        dcw_ref[2:3, :] += jnp.sum(dc * a, axis=0, keepdims=True)

    return _pallas_call(
        body, name="bwd_conv", grid=(s // tm,),
        out_shape=[_sds((s, c), BF16), _sds((s, c), BF16), _plain((8, c), F32)],
        in_specs=[_rows(c), _halo_after(c, s), _rows(c), _rows(c), _halo_before(c), _halo_before(c), _whole(cw.shape)],
        out_specs=[_rows(c), _rows(c), _whole((8, c))], compiler_params=_params(1, 40),
    )(dconv, dconv, cg, u, cg, u, cw)


def _assemble(p_refs, pieces, widths, dp, scr):
    off = 0
    for p_ref, (_, d), wd in zip(p_refs, pieces, widths):
        if d == 1:
            dp[:, off:off + wd] = p_ref[...]
        else:
            dp[:, off:off + wd] = _from_view(scr, p_ref, d).astype(BF16)
        off += wd


def _dgrad_norm(pieces, wg, h, g, dres, token, onward, name):
    s, d_model = h.shape
    n = N_DEV * wg.shape[2]
    tm = ROW_TILE
    widths = [p.shape[1] // d for p, d in pieces]
    assert sum(widths) == n
    n_p = len(pieces)

    def body(_, *refs):
        p_refs = refs[:n_p]
        w_ref, h_ref, g_ref, dr_ref, dh_ref, dg_ref = refs[n_p:n_p + 6]
        dp, scr, wj = refs[-3:]
        _join_once(w_ref, wj)

        @pl.when(pl.program_id(0) == 0)
        def _():
            dg_ref[...] = jnp.zeros_like(dg_ref)

        _assemble(p_refs, pieces, widths, dp, scr)
        dhn = _dot_nt(dp[...], wj[...])
        hb = h_ref[...]
        r = lax.rsqrt(jnp.mean(hb * hb, axis=-1, keepdims=True) + EPS)
        dg_ref[...] += jnp.sum(dhn * (hb * r), axis=0, keepdims=True)
        dn = dhn * g_ref[...]
        dh = dr_ref[...] + r * dn - hb * ((r * r * r) * jnp.mean(dn * hb, axis=-1, keepdims=True))
        dh_ref[...] = dh
        if onward:
            dhb_ref, dpt_ref = refs[n_p + 6:n_p + 8]
            dhb_ref[...] = dh.astype(BF16)
            dpt_ref[...] = dp[...].T

    p_specs = [_view_rows(wd, d) for (_, d), wd in zip(pieces, widths)]
    out_shape = [_plain((s, d_model), F32), _plain((1, d_model), F32)]
    out_specs = [_rows(d_model), _whole((1, d_model))]
    if onward:
        out_shape += [_sds((s, d_model), BF16), _sds((n, s), BF16)]
        out_specs += [_rows(d_model), pl.BlockSpec((n, tm), lambda i: (0, i))]
    return _pallas_call(
        body, name=name, grid=(s // tm,), out_shape=out_shape,
        in_specs=[ANY] + p_specs + [_resident(wg.shape), _rows(d_model), _whole((1, d_model)), _rows(d_model)],
        out_specs=out_specs,
        scratch_shapes=[pltpu.VMEM((tm, n), BF16), _view_scratch(GROUP_WIDTH), pltpu.VMEM((d_model, n), BF16)],
        compiler_params=_params(1, 60),
    )(token, *[p for p, _ in pieces], wg, h, g, dres)


def _assemble_dproj_t(pieces, n, name):
    tm = ROW_TILE
    widths = [p.shape[1] // d for p, d in pieces]
    assert sum(widths) == n
    s = pieces[0][0].shape[0] * pieces[0][1]
    n_p = len(pieces)

    def body(*refs):
        p_refs, (dpt_ref, dp, scr) = refs[:n_p], refs[n_p:]
        _assemble(p_refs, pieces, widths, dp, scr)
        dpt_ref[...] = dp[...].T

    return _pallas_call(
        body, name=name, grid=(s // tm,), out_shape=_sds((n, s), BF16),
        in_specs=[_view_rows(wd, d) for (_, d), wd in zip(pieces, widths)],
        out_specs=pl.BlockSpec((n, tm), lambda i: (0, i)),
        scratch_shapes=[pltpu.VMEM((tm, n), BF16), _view_scratch(GROUP_WIDTH)],
        compiler_params=_params(1, 40),
    )(*[p for p, _ in pieces])


def _wgrad_shards_t(dp_t, h, c, name):
    n, s = dp_t.shape
    d_model = h.shape[1]
    per_step = 2

    def body(a_ref, b_ref, o_ref):
        o_ref[...] = _dot(a_ref[...], b_ref[...]).astype(BF16).reshape(per_step, c, d_model)

    return _pallas_call(
        body, name=name, grid=(N_DEV // per_step,), out_shape=_sds((N_DEV, c, d_model), BF16),
        in_specs=[pl.BlockSpec((per_step * c, s), lambda j: (j, 0)), _resident(h.shape)],
        out_specs=pl.BlockSpec((per_step, c, d_model), lambda j: (j, 0, 0)), compiler_params=_params(1, 40),
    )(dp_t, h)


def _wgrad_cols(a_t, b, c, name):
    m, s = a_t.shape
    assert c % LANES == 0

    def body(a_ref, b_ref, o_ref):
        wide = _dot(a_ref[...], b_ref[...]).astype(BF16)
        for j in range(WGRAD_SHARDS):
            o_ref[j] = wide[:, j * c:(j + 1) * c]

    return _pallas_call(
        body, name=name, grid=(N_DEV // WGRAD_SHARDS,), out_shape=_sds((N_DEV, m, c), BF16),
        in_specs=[_whole(a_t.shape), pl.BlockSpec((s, WGRAD_SHARDS * c), lambda j: (0, j))],
        out_specs=pl.BlockSpec((WGRAD_SHARDS, m, c), lambda j: (j, 0, 0)), compiler_params=_params(1, 40),
    )(a_t, b)


def _wgrad_rows(a_t, b, name):
    m, s = a_t.shape
    n = b.shape[1]
    mr = m // N_DEV

    def body(a_ref, b_ref, o_ref):
        o_ref[...] = _dot(a_ref[...], b_ref[...]).astype(BF16).reshape(WGRAD_SHARDS, mr, n)

    return _pallas_call(
        body, name=name, grid=(N_DEV // WGRAD_SHARDS,), out_shape=_sds((N_DEV, mr, n), BF16),
        in_specs=[pl.BlockSpec((WGRAD_SHARDS * mr, s), lambda j: (j, 0)), _whole(b.shape)],
        out_specs=pl.BlockSpec((WGRAD_SHARDS, mr, n), lambda j: (j, 0, 0)), compiler_params=_params(1, 40),
    )(a_t, b)


def _memkv_bwd(dkvs, w, mem, g):
    n_layers = g.shape[0]
    rows = D_MODEL // N_DEV

    def body(dkv0_ref, dkv1_ref, w_ref, mem_ref, g_ref, dw_ref, dg_ref):
        mb = mem_ref[...]
        r = lax.rsqrt(jnp.mean(mb * mb, axis=-1, keepdims=True) + EPS)
        nm = mb * r
        mn = (nm * g_ref[...]).astype(BF16)
        dkvb = jnp.where(pl.program_id(0) == 0, dkv0_ref[...], dkv1_ref[...]).astype(BF16)
        dw_ref[...] = _dot_tn(mn, dkvb).astype(BF16).reshape(N_DEV, rows, 2 * MEM_WIDTH)
        dmn = _dot_nt(dkvb, w_ref[...].reshape(D_MODEL, 2 * MEM_WIDTH))
        dg_ref[...] = jnp.sum(dmn * nm, axis=0, keepdims=True)

    lay = lambda *shape: pl.BlockSpec((None,) + shape, lambda l: (l, 0, 0))
    major = pl.BlockSpec((N_DEV, rows, 2 * MEM_WIDTH), lambda l: (0, l, 0))
    return _pallas_call(
        body, name="memkv_bwd", grid=(n_layers,),
        out_shape=[_sds((N_DEV, n_layers * rows, 2 * MEM_WIDTH), BF16), _plain((n_layers, 1, D_MODEL), F32)],
        in_specs=[_whole(dkvs[0].shape), _whole(dkvs[1].shape), major, _whole(mem.shape), lay(1, D_MODEL)],
        out_specs=[major, lay(1, D_MODEL)],
        compiler_params=_params(1, 32),
    )(*dkvs, w, mem, g.reshape(n_layers, 1, D_MODEL))


def _bwd_post_attn(dhb, wg_out, z, os_, ls_, qm, kv, mo, head_ones, token):
    s = dhb.shape[0]
    gw = GROUP_WIDTH
    nb = gw + MEM_WIDTH
    tm = ROW_TILE

    def body(_, dh_ref, w_ref, z_ref, o0, o1, o2, l0, l1, l2, qm_ref, kv_ref, mo_ref, bd_ref,
             dz_ref, do0, do1, do2, dl0, dl1, dl2, dqm_ref, dkv_ref, s0, s1):
        @pl.when(pl.program_id(0) == 0)
        def _():
            dkv_ref[...] = jnp.zeros_like(dkv_ref)

        dy = _dot_nt(dh_ref[...], _joined_columns(w_ref))
        ov, lv = [], []
        for o_ref, l_ref, d in zip((o0, o1, o2), (l0, l1, l2), DILATIONS):
            ov.append(_from_view(s0, o_ref, d))
            lv.append(_from_view(s1, l_ref, d))
        ws, mix = _mix_groups(ov, lv)
        sz, dsz = _silu_parts(z_ref[...])
        dz_ref[:, :gw] = (dy[:, :gw] * mix * dsz[:, :gw]).astype(BF16)
        dz_ref[:, gw:] = (dy[:, gw:] * mo_ref[...] * dsz[:, gw:]).astype(BF16)
        dbr = dy * sz
        dmix = dbr[:, :gw]
        t = dmix * mix
        th = t.astype(BF16)
        tl = (t - th.astype(F32)).astype(BF16)
        rs = _dot(th, bd_ref[...]) + _dot(tl, bd_ref[...])
        for wg_, do_ref, dl_ref, d in zip(ws, (do0, do1, do2), (dl0, dl1, dl2), DILATIONS):
            _to_view(s0, wg_ * dmix, do_ref, d)
            _to_view(s1, wg_ * rs, dl_ref, d)
        _mem_attn_bwd(qm_ref[...], kv_ref, dbr[:, gw:], dqm_ref, dkv_ref)

    vspecs = [_view_rows(gw, d) for d in DILATIONS]
    return _pallas_call(
        body, name="bwd_post_attn", grid=(s // tm,),
        out_shape=[_sds((s, nb), BF16)] + [_sds((s // d, d * gw), BF16) for d in DILATIONS]
                  + [_sds((s // d, d * gw), F32) for d in DILATIONS] + [_sds((s, MEM_WIDTH), BF16), _plain(kv.shape, F32)],
        in_specs=[ANY, _rows(D_MODEL), _whole(wg_out.shape), _rows(nb)] + vspecs * 2
                 + [_rows(MEM_WIDTH), _whole(kv.shape), _rows(MEM_WIDTH), _whole(head_ones.shape)],
        out_specs=[_rows(nb)] + vspecs * 2 + [_rows(MEM_WIDTH), _whole(kv.shape)],
        scratch_shapes=[_view_scratch(gw), _view_scratch(gw)],
        compiler_params=_params(1, 48),
    )(token, dhb, wg_out, z, *os_, *ls_, qm, kv, mo, head_ones)


def _attn_bwd(q, k, v, lse, do, dl, tabs, d, token):
    ln, dw = q.shape
    w = dw // d
    nb = ln // BLOCK
    reps = w // LANES
    two, before = _pair_specs(d, nb, w)
    two_t, _ = _pair_specs(d, nb, LANES)

    def attend(q_ref, l_ref, do_ref, dl_ref, dqs, acck, accv, rows, col0, kk, vv, acc_rows):
        valid = _band_mask(kk.shape[0])
        low = _low_head_lanes()
        pairs, qcols = _head_tiles(w, col0)
        cols = [slice(col0 + h * HEAD_DIM, col0 + h * HEAD_DIM + 1) for h in range(HEADS_PER_GROUP)]
        qhs = [h for qc in qcols for h in _split_pair(q_ref[rows, qc], low)]
        dobs = [h for qc in qcols for h in _split_pair(do_ref[rows, qc], low)]
        k2s = [kk[:, pr] for pr in pairs for _ in range(2)]
        v2s = [vv[:, pr] for pr in pairs for _ in range(2)]
        scs = [jnp.where(valid, _dot_nt(qh, k2), NEG) for qh, k2 in zip(qhs, k2s)]
        dps = [_dot_nt(dob, v2) for dob, v2 in zip(dobs, v2s)]
        ps = [jnp.exp(sc - l_ref[rows, col]) for sc, col in zip(scs, cols)]
        dss = [(p * (dp - dl_ref[rows, col])).astype(BF16) for p, dp, col in zip(ps, dps, cols)]
        pbs = [p.astype(BF16) for p in ps]
        for i, qc in enumerate(qcols):
            a, b = 2 * i, 2 * i + 1
            dqs[rows, qc] = jnp.where(low, _dot(dss[a], k2s[a]), _dot(dss[b], k2s[b])) * SCALE
            acck[acc_rows, qc] += _dot_tn(dss[a], qhs[a]) + _dot_tn(dss[b], qhs[b])
            accv[acc_rows, qc] += _dot_tn(pbs[a], dobs[a]) + _dot_tn(pbs[b], dobs[b])

    def body_streams(_, q_ref, kc_ref, vc_ref, l_ref, do_ref, dl_ref, c_ref, sa_ref, sb_ref,
                     dq_ref, dk_ref, dv_ref, acck, accv, dqs):
        acck[...] = jnp.zeros_like(acck)
        accv[...] = jnp.zeros_like(accv)
        for sb in range(2):
            cols = slice(sb * w, (sb + 1) * w)
            attend(q_ref, l_ref, do_ref, dl_ref, dqs, acck, accv, TOP, sb * w, kc_ref[:, cols], vc_ref[:, cols], TOP)
        tabs2 = [jnp.concatenate([jnp.tile(r[:, sb * LANES:(sb + 1) * LANES], (1, reps)) for sb in range(2)], axis=1)
                 for r in (c_ref, sa_ref, sb_ref)]
        dq_ref[...] = _rope_bwd(dqs[...], *tabs2).astype(BF16)
        dk_ref[...] = _rope_bwd(acck[...], *tabs2).astype(BF16)
        dv_ref[...] = accv[...].astype(BF16)

    def body_blocks(_, q_ref, kp_ref, kc_ref, vp_ref, vc_ref, l_ref, do_ref, dl_ref, cq, saq, sbq, ck, sak, sbk,
                    dq_ref, dk_ref, dv_ref, acck, accv, dqs):
        i = pl.program_id(0) % (nb // 2)

        @pl.when(i == 0)
        def _():
            acck[...] = jnp.zeros_like(acck)
            accv[...] = jnp.zeros_like(accv)

        refs = (q_ref, l_ref, do_ref, dl_ref, dqs, acck, accv)
        pl.when(i == 0)(lambda: attend(*refs, TOP, 0, kc_ref[TOP, :], vc_ref[TOP, :], TOP))
        pl.when(i != 0)(lambda: attend(
            *refs, TOP, 0, jnp.concatenate([kp_ref[...], kc_ref[TOP, :]], axis=0),
            jnp.concatenate([vp_ref[...], vc_ref[TOP, :]], axis=0),
            pl.ds(pl.multiple_of((2 * i - 1) * BLOCK, BLOCK), 2 * BLOCK)))
        attend(*refs, BOTTOM, 0, kc_ref[...], vc_ref[...], pl.ds(pl.multiple_of(2 * i * BLOCK, BLOCK), 2 * BLOCK))
        tq = [jnp.tile(r[...], (1, reps)) for r in (cq, saq, sbq)]
        dq_ref[...] = _rope_bwd(dqs[...], *tq).astype(BF16)

        @pl.when(i == nb // 2 - 1)
        def _():
            for r0 in range(0, nb * BLOCK, 2 * BLOCK):
                rows = slice(r0, r0 + 2 * BLOCK)
                tk = [jnp.tile(r[rows, :], (1, reps)) for r in (ck, sak, sbk)]
                dk_ref[rows, :] = _rope_bwd(acck[rows, :], *tk).astype(BF16)
                dv_ref[rows, :] = accv[rows, :].astype(BF16)

    if nb == 1:
        body = body_streams
        in_specs = [ANY] + [two] * 6 + [two_t] * 3
        args = (token, q, k, v, lse, do, dl, *tabs)
        out_specs = [two, two, two]
        acc_shape = (BLOCK, 2 * w)
    else:
        body = body_blocks
        stream = pl.BlockSpec((nb * BLOCK, w), lambda n: (0, n // (nb // 2)))
        stream_t = pl.BlockSpec((nb * BLOCK, LANES), lambda n: (0, n // (nb // 2)))
        in_specs = [ANY, two, before, two, before, two, two, two, two] + [two_t] * 3 + [stream_t] * 3
        args = (token, q, k, k, v, v, lse, do, dl, *tabs, *tabs)
        out_specs = [two, stream, stream]
        acc_shape = (nb * BLOCK, w)
    return _pallas_call(
        body, name=f"attn_bwd_d{d}", grid=(d * nb // 2,), out_shape=[_sds((ln, dw), BF16)] * 3,
        in_specs=in_specs, out_specs=out_specs,
        scratch_shapes=[pltpu.VMEM(acc_shape, F32), pltpu.VMEM(acc_shape, F32), pltpu.VMEM(two.block_shape, F32)],
        compiler_params=_params(1, 48),
    )(*args)


def _position():
    return lax.axis_index("x"), lax.axis_index("y"), lax.axis_index("c")


def _all_gather_small(xs, afters, name):
    n_in = 1 + len(afters)

    def body(*refs):
        x_ref, out_ref = refs[0], refs[n_in]
        send_sems, recv_sems, local_sem = refs[n_in + 1:]
        x, y, c = _position()
        my_rows = out_ref.at[4 * x + 2 * y + c]
        mine = pltpu.make_async_copy(x_ref, my_rows, local_sem)
        mine.start()
        copies = [pltpu.make_async_remote_copy(
            src_ref=x_ref, dst_ref=my_rows, send_sem=send_sems.at[k], recv_sem=recv_sems.at[k],
            device_id=(1 - x if k & 4 else x, 1 - y if k & 2 else y, 1 - c if k & 1 else c), device_id_type=MESH)
            for k in range(1, N_DEV)]
        for cp in copies:
            cp.start()
        for cp in copies:
            cp.wait_recv()
        for cp in copies:
            cp.wait_send()
        mine.wait()

    return _pallas_call(
        body, name=name, out_shape=_sds((N_DEV,) + xs.shape, xs.dtype),
        in_specs=[ANY] * n_in, out_specs=ANY,
        scratch_shapes=[pltpu.SemaphoreType.DMA((N_DEV,)), pltpu.SemaphoreType.DMA((N_DEV,)), pltpu.SemaphoreType.DMA],
    )(xs, *afters)


def _all_gather_relay(xs, name):
    def body(x_ref, out_ref, send_sems, recv_sems, local_sem):
        x, y, c = _position()
        me, sibling = (x, y, c), (x, y, 1 - c)
        xn, yn, diag = (1 - x, y, c), (x, 1 - y, c), (1 - x, 1 - y, c)
        src_nb = (x + c * (1 - 2 * x), y + (1 - c) * (1 - 2 * y), c)
        dst_nb = (x + (1 - c) * (1 - 2 * x), y + c * (1 - 2 * y), c)

        def rows(dev):
            return out_ref.at[4 * dev[0] + 2 * dev[1] + dev[2]]

        def copy(k, block, to, own=False):
            return pltpu.make_async_remote_copy(
                src_ref=x_ref if own else rows(block), dst_ref=rows(block),
                send_sem=send_sems.at[k], recv_sem=recv_sems.at[k], device_id=to, device_id_type=MESH)

        mine = pltpu.make_async_copy(x_ref, rows(me), local_sem)
        mine.start()
        first = [copy(1, me, xn, own=True), copy(2, me, yn, own=True), copy(0, me, sibling, own=True)]
        for cp in first:
            cp.start()
        copy(1, xn, me).wait_recv()
        copy(2, yn, me).wait_recv()
        relay = copy(3, src_nb, dst_nb)
        relay.start()
        passed = [copy(4, xn, sibling), copy(5, yn, sibling)]
        for cp in passed:
            cp.start()
        copy(3, diag, me).wait_recv()
        last = copy(6, diag, sibling)
        last.start()
        copy(0, sibling, me).wait_recv()
        for k, blk in ((4, (1 - x, y, 1 - c)), (5, (x, 1 - y, 1 - c)), (6, (1 - x, 1 - y, 1 - c))):
            copy(k, blk, me).wait_recv()
        for cp in first + [relay] + passed + [last]:
            cp.wait_send()
        mine.wait()

    return _pallas_call(
        body, name=name, out_shape=_sds((N_DEV,) + xs.shape, xs.dtype),
        in_specs=[ANY], out_specs=ANY,
        scratch_shapes=[pltpu.SemaphoreType.DMA((7,)), pltpu.SemaphoreType.DMA((7,)), pltpu.SemaphoreType.DMA],
    )(xs)


HBM_SPEC = pl.BlockSpec(memory_space=pltpu.HBM)
SEM_SPEC = pl.BlockSpec(memory_space=pltpu.SEMAPHORE)
EFFECT = pltpu.SideEffectType.DATAFLOW_SIDE_EFFECTING
def _plan_gather_own(src_refs, land_refs):
    x, y, c = _position()
    me = 4 * x + 2 * y + c
    peers = [(x, y, 1 - c), (1 - x, y, c), (x, 1 - y, c), (1 - x, 1 - y, c)]
    return [(land_refs[a].at[me], land_refs[a].at[me], (a, k), peer)
            for k, peer in enumerate(peers) for a in range(len(land_refs))]


def _plan_gather_pass(src_refs, land_refs):
    x, y, c = _position()
    chips = [(1 - x, y), (x, 1 - y), (1 - x, 1 - y)]
    return [(land_refs[a].at[4 * px + 2 * py + c], land_refs[a].at[4 * px + 2 * py + c], (a, j), (x, y, 1 - c))
            for j, (px, py) in enumerate(chips) for a in range(len(land_refs))]


def _plan_to_sibling(src_refs, land_refs):
    x, y, c = _position()
    return [(src_refs[a].at[2 * k + (1 - c)], land_refs[a].at[k], (a, k), (x, y, 1 - c))
            for k in range(4) for a in range(len(src_refs))]


def _plan_to_chips(src_refs, land_refs):
    x, y, c = _position()
    chips = [(1 - x, y), (x, 1 - y), (1 - x, 1 - y)]
    return [(src_refs[a].at[2 * px + py], land_refs[a].at[j], (a, j), (px, py, c))
            for j, (px, py) in enumerate(chips) for a in range(len(src_refs))]


def _split_start(srcs, lands, plan, n_sem, after, name):
    n_s, n_a = len(srcs), len(lands)
    n_b = n_s + n_a

    def body(*refs):
        src_refs, land_refs = refs[:n_s], refs[n_s:n_b]
        send_sems, recv_sems, token = refs[n_b + 1], refs[n_b + 2], refs[-1]
        for src, dst, (a, k), dev in plan(src_refs, land_refs):
            i = a * n_sem + k
            pltpu.make_async_remote_copy(src_ref=src, dst_ref=dst, send_sem=send_sems.at[i], recv_sem=recv_sems.at[i],
                                         device_id=dev, device_id_type=MESH).start()
        token[...] = jnp.zeros_like(token)

    bufs = list(srcs) + list(lands)
    res = pl.pallas_call(
        body, name=name,
        out_shape=(pltpu.SemaphoreType.DMA((n_a * n_sem,)), pltpu.SemaphoreType.DMA((n_a * n_sem,)),
                   *[pltpu.HBM(t.shape, t.dtype) for t in bufs], _plain((8, LANES), F32)),
        in_specs=[HBM_SPEC] * n_b + [ANY],
        out_specs=(SEM_SPEC, SEM_SPEC, *[HBM_SPEC] * n_b, pl.BlockSpec(memory_space=pltpu.VMEM)),
        input_output_aliases={i: 2 + i for i in range(n_b)},
        compiler_params=pltpu.CompilerParams(has_side_effects=EFFECT),
    )(*[pltpu.with_memory_space_constraint(t, pltpu.HBM) for t in bufs], after)
    return (res[0], res[1], res[2:2 + n_s], res[2 + n_s:2 + n_b]), res[-1]


def _split_wait(started, plan, after, name, first=0, n_sem=None):
    send_sems, recv_sems, srcs, lands = started
    n_s, n_a = len(srcs), len(lands)
    n_b = n_s + n_a
    n_sem = n_sem or send_sems.shape[0] // n_a

    def body(*refs):
        src_refs, land_refs = refs[:n_s], refs[n_s:n_b]
        s_sems, r_sems = refs[n_b], refs[n_b + 1]
        for src, dst, (a, k), dev in plan(src_refs, land_refs):
            i = (first + a) * n_sem + k
            cp = pltpu.make_async_remote_copy(src_ref=src, dst_ref=dst, send_sem=s_sems.at[i], recv_sem=r_sems.at[i],
                                              device_id=dev, device_id_type=MESH)
            cp.wait_send()
            cp.wait_recv()

    bufs = list(srcs) + list(lands)
    res = pl.pallas_call(
        body, name=name, out_shape=tuple(pltpu.HBM(t.shape, t.dtype) for t in bufs),
        in_specs=[HBM_SPEC] * n_b + [SEM_SPEC, SEM_SPEC, ANY],
        out_specs=tuple([HBM_SPEC] * n_b),
        input_output_aliases={i: i for i in range(n_b)},
        compiler_params=pltpu.CompilerParams(has_side_effects=EFFECT),
    )(*bufs, send_sems, recv_sems, after)
    return res[:n_s], res[n_s:]


SUBLANES = 8


def _row_tile(r):
    return max(t for t in range(SUBLANES, ROW_TILE + 1, SUBLANES) if r % t == 0)


def _rs_add_sibling(gps, recvs, ck_arr, name):
    n = len(gps)
    block = lambda k, ck: (k + ck[1] + 1) % 4

    def body(ck_ref, *refs):
        for g_ref, r_ref, pf_ref, pb_ref in zip(refs[:n], refs[n:2 * n], refs[2 * n::2], refs[2 * n + 1::2]):
            sm = g_ref[...].astype(F32) + r_ref[...].astype(F32)
            pf_ref[...] = sm
            pb_ref[...] = sm.astype(BF16)

    mine = lambda t: pl.BlockSpec((None,) + t.shape[1:], lambda k, ck: (2 * block(k, ck) + ck[0], 0, 0))
    one = lambda t: pl.BlockSpec((None,) + t.shape[1:], lambda k, ck: (block(k, ck), 0, 0))
    res = _pallas_call(
        body, name=name,
        grid_spec=pltpu.PrefetchScalarGridSpec(
            num_scalar_prefetch=1, grid=(4,),
            in_specs=[mine(t) for t in gps] + [one(t) for t in recvs],
            out_specs=[s for t in recvs for s in (pl.BlockSpec(t.shape[1:], lambda k, ck: (0, 0)), one(t))]),
        out_shape=[s for t in recvs for s in (_sds(t.shape[1:], F32), _sds(t.shape, BF16))],
        compiler_params=_params(1, 48),
    )(ck_arr, *gps, *recvs)
    return list(zip(res[0::2], res[1::2]))


def _adam_update(w, gv, m, v):
    nm = ADAM_B1 * m + (1.0 - ADAM_B1) * gv
    nv = ADAM_B2 * v + (1.0 - ADAM_B2) * (gv * gv)
    m_hat = nm / (1.0 - ADAM_B1 ** ADAM_STEP)
    v_hat = nv / (1.0 - ADAM_B2 ** ADAM_STEP)
    return -ADAM_LR * (m_hat / (jnp.sqrt(v_hat) + ADAM_EPS) + ADAM_WD * w), nm, nv


def _rs_finish_adamw(pf, recv, w, m, v, after, name):
    r, l = pf.shape
    tr = _row_tile(r)

    def body(_, p_ref, r_ref, w_ref, m_ref, v_ref, g_ref, d_ref, nm_ref, nv_ref):
        gv = ((p_ref[...] + r_ref[0].astype(F32)) + r_ref[1].astype(F32)) + r_ref[2].astype(F32)
        g_ref[...] = gv
        d_ref[...], nm_ref[...], nv_ref[...] = _adam_update(w_ref[...], gv, m_ref[...], v_ref[...])

    spec = pl.BlockSpec((tr, l), lambda i: (i, 0))
    return _pallas_call(
        body, name=name, grid=(r // tr,),
        in_specs=[ANY, spec, pl.BlockSpec((3, tr, l), lambda i: (0, i, 0)), spec, spec, spec], out_specs=[spec] * 4,
        out_shape=[_plain((r, l), F32)] * 4, compiler_params=_params(1, 32),
    )(after, pf, recv, w, m, v)


SMALL_ROWS = dict(norm_g=(0, 2), mem_norm_g=(2, 4), final_g=(4, 5), conv_w=(5, 8))
LOSS_ROWS = (8, 16)


def _sum_adamw_small(g, ck_arr, states):
    names = list(SMALL_ROWS)
    n_dev, n_rows, _ = g.shape

    def body(ck_ref, g_ref, gc_ref, *refs):
        ins, loss_ref, outs = refs[:3 * len(names)], refs[3 * len(names)], refs[3 * len(names) + 1:]

        def total(ref, lo, hi):
            acc = ref[0, lo:hi, :]
            for j in range(1, n_dev):
                acc = acc + ref[j, lo:hi, :]
            return acc

        loss_ref[...] = total(gc_ref, *LOSS_ROWS)
        for i, n in enumerate(names):
            gv = total(gc_ref if n == "conv_w" else g_ref, *SMALL_ROWS[n])
            w_ref, m_ref, v_ref = ins[3 * i:3 * i + 3]
            g_out, d_out, nm_out, nv_out = outs[4 * i:4 * i + 4]
            g_out[...] = gv
            d_out[...], nm_out[...], nv_out[...] = _adam_update(w_ref[...], gv, m_ref[...], v_ref[...])

    flat = [t for n in names for t in states[n]]
    mine = pl.BlockSpec((n_dev, n_rows, LANES), lambda i, ck: (0, 0, 2 * ck[1] + ck[0]))
    res = _pallas_call(
        body, name="sum_adamw_small",
        grid_spec=pltpu.PrefetchScalarGridSpec(
            num_scalar_prefetch=1, grid=(1,),
            in_specs=[_whole(g.shape), mine] + [_whole(t.shape) for t in flat],
            out_specs=[_whole((SUBLANES, LANES))] + [_whole(states[n][0].shape) for n in names for _ in range(4)]),
        out_shape=[_plain((SUBLANES, LANES), F32)] + [_plain(states[n][0].shape, F32) for n in names for _ in range(4)],
        compiler_params=_params(1, 32),
    )(ck_arr, g, g, *flat)
    return res[0], {n: tuple(res[1 + 4 * i:5 + 4 * i]) for i, n in enumerate(names)}


def _finish(name, pf, recv, w, m, v, after):
    if name in ("attn_w_in", "conv_w_in"):
        res = _rs_finish_adamw(pf, recv, w.T, m.T, v.T, after, "rs_finish_adamw_" + name)
        return tuple(t.T for t in res)
    return _rs_finish_adamw(pf, recv, w, m, v, after, "rs_finish_adamw_" + name)


def kernel(x, mem, positions, norm_g, mem_norm_g, w_mem_kv, attn_w_in, attn_w_out, conv_w_in, conv_w, conv_w_out, final_g, loss_target, m_norm_g, m_mem_norm_g, m_w_mem_kv, m_attn_w_in, m_attn_w_out, m_conv_w_in, m_conv_w, m_conv_w_out, m_final_g, v_norm_g, v_mem_norm_g, v_w_mem_kv, v_attn_w_in, v_attn_w_out, v_conv_w_in, v_conv_w, v_conv_w_out, v_final_g):
    px, py, pc = _position()
    me = 4 * px + 2 * py + pc
    ck_arr = jnp.stack([pc, 2 * px + py]).astype(jnp.int32)
    x, mem, pos, tgt = x[0], mem[0], positions[0], loss_target[0]

    wg_in0 = _all_gather_relay(attn_w_in[0].astype(BF16), "gather_w_in0")
    def gather_pass(weights, after, name, first):
        _, lands = _split_wait(weights, _plan_gather_own, after, name + "_wait", first, 4)
        return _split_start([], lands, _plan_gather_pass, 3, after, name + "_pass_start")

    late = [attn_w_out[0].astype(BF16), w_mem_kv.astype(BF16).reshape(-1, w_mem_kv.shape[2]),
            jnp.pad(conv_w[0], ((0, 5), (0, 0))), conv_w_in[0].astype(BF16), conv_w_out[0].astype(BF16)]
    lands = [lax.dynamic_update_slice(lax.empty((N_DEV,) + t.shape, t.dtype), t[None], (me, 0, 0)) for t in late]
    (send_sems, recv_sems, _, lands), token = _split_start([], lands, _plan_gather_own, 4, wg_in0, "gather_late_start")
    rest0, conv_ws = (send_sems, recv_sems, [], lands[:3]), (send_sems, recv_sems, [], lands[3:])

    tabs = _rope_tables(pos)
    g0, g1 = norm_g[0:1], norm_g[1:2]

    hn0, qs, ks, vs, tabs_v, qm0, z0 = _inproj_attn(x, g0, wg_in0, tabs, token)
    os_, ls_ = [], []
    for j, d in enumerate(DILATIONS):
        if j == 2:
            rest0, token = gather_pass(rest0, ls_[1], "gather_rest", 0)
        o, l = _attn_fwd(qs[j], ks[j], vs[j], d, token)
        os_.append(o)
        ls_.append(l)

    conv_ws, token = gather_pass(conv_ws, ls_[2], "gather_conv", 3)
    _, (wg_out0, wg_kv, cw_all) = _split_wait(rest0, _plan_gather_pass, token, "gather_rest_pass_wait")
    cw = cw_all[:, 0:3].transpose(1, 0, 2).reshape(3, -1)
    kv = _memkv_fwd(mem, mem_norm_g, wg_kv)
    y0, y0_t, mo0, h1 = _post_attn(os_, ls_, qm0, kv[0], z0, x, wg_out0)
    _, (wg_in1, wg_out1) = _split_wait(conv_ws, _plan_gather_pass, h1, "gather_conv_pass_wait")
    w_out1 = wg_out1.reshape(-1, wg_out1.shape[2])

    hn1, bg, cg, u, qm1, z1 = _inproj_conv(h1, g1, wg_in1)
    y1, y1_t, mo1, dh2, dh2b, loss_acc, d_final_g = _post_conv_loss(
        bg, cg, u, qm1, kv[1], z1, h1, w_out1, cw, final_g.reshape(1, -1), tgt)

    d_w_out1 = _wgrad_rows(y1_t, dh2b, "wgrad_out1")
    dz1, dbg, dconv, dqm1, dkv1 = _bwd_post_conv(dh2b, w_out1, bg, cg, u, z1, qm1, kv[1], mo1, cw)
    dcg, du, dcw = _bwd_conv(dconv, cg, u, cw)
    dh1, dg1, dh1b, dproj1_t = _dgrad_norm([(dbg, 1), (dcg, 1), (du, 1), (dqm1, 1), (dz1, 1)], wg_in1, h1, g1, dh2,
                                           dh2, True, "dgrad_norm_conv")
    d_w_in1 = _wgrad_shards_t(dproj1_t, hn1, wg_in1.shape[2], "wgrad_in1")

    d_w_out0 = _wgrad_cols(y0_t, dh1b, wg_out0.shape[2], "wgrad_out0")

    gw = GROUP_WIDTH
    ones = (jnp.arange(gw)[:, None] // HEAD_DIM == jnp.arange(gw)[None, :] // HEAD_DIM).astype(BF16)
    res = _bwd_post_attn(dh1b, wg_out0, z0, os_, ls_, qm0, kv[0], mo0, ones, dg1)
    dz0, dos, dls, dqm0, dkv0 = res[0], res[1:4], res[4:7], res[7], res[8]
    d_w_kv, d_mem_g = _memkv_bwd([dkv0, dkv1], wg_kv, mem, mem_norm_g)

    names1 = ["conv_w_in", "conv_w_out", "attn_w_out", "w_mem_kv"]
    grads1 = [d_w_in1, d_w_out1, d_w_out0, d_w_kv]
    started, token = _split_start(grads1, [lax.empty((4,) + g.shape[1:], g.dtype) for g in grads1],
                                  _plan_to_sibling, 4, d_mem_g, "rs1_sibling_start")

    dqs, dks, dvs = [], [], []
    for j, d in enumerate(DILATIONS):
        if j == 1:
            grads1, from_sibling = _split_wait(started, _plan_to_sibling, dqs[0][0], "rs1_sibling_wait")
            parts1 = _rs_add_sibling(grads1, from_sibling, ck_arr, "rs1_add_sibling")
            pbs1 = [pb for _, pb in parts1]
            started, token = _split_start(pbs1, [lax.empty((3,) + p.shape[1:], p.dtype) for p in pbs1],
                                          _plan_to_chips, 3, dg1, "rs1_chips_start")
        dq, dk, dv = _attn_bwd(qs[j], ks[j], vs[j], ls_[j], dos[j], dls[j], tabs_v[j], d, token)
        dqs.append((dq, d))
        dks.append((dk, d))
        dvs.append((dv, d))
    pieces0 = dqs + dks + dvs + [(dqm0, 1), (dz0, 1)]
    dproj0_t = _assemble_dproj_t(pieces0, N_DEV * wg_in0.shape[2], "assemble_dproj_attn")
    d_w_in0 = _wgrad_shards_t(dproj0_t, hn0, wg_in0.shape[2], "wgrad_in0")

    names0 = ["attn_w_in"]
    grads0 = [d_w_in0]
    started0, token0 = _split_start(grads0, [lax.empty((4,) + g.shape[1:], g.dtype) for g in grads0],
                                    _plan_to_sibling, 4, dg1, "rs0_sibling_start")
    _, from_chips1 = _split_wait(started, _plan_to_chips, token0, "rs1_chips_wait")
    shard = dict(attn_w_in=(attn_w_in[0], m_attn_w_in[0], v_attn_w_in[0]),
                 attn_w_out=(attn_w_out[0], m_attn_w_out[0], v_attn_w_out[0]),
                 conv_w_in=(conv_w_in[0], m_conv_w_in[0], v_conv_w_in[0]),
                 conv_w_out=(conv_w_out[0], m_conv_w_out[0], v_conv_w_out[0]),
                 w_mem_kv=tuple(t.reshape(-1, t.shape[2]) for t in (w_mem_kv, m_w_mem_kv, v_w_mem_kv)))
    finish1 = {n: (pf, r) for n, (pf, _), r in zip(names1, parts1, from_chips1)}
    big = {"conv_w_in": _finish("conv_w_in", *finish1["conv_w_in"], *shard["conv_w_in"], token0)}

    grads0, from_sibling = _split_wait(started0, _plan_to_sibling, big["conv_w_in"][1].T, "rs0_sibling_wait")
    parts0 = _rs_add_sibling(grads0, from_sibling, ck_arr, "rs0_add_sibling")
    pbs0 = [pb for _, pb in parts0]
    started0, token0 = _split_start(pbs0, [lax.empty((3,) + p.shape[1:], p.dtype) for p in pbs0],
                                    _plan_to_chips, 3, dg1, "rs0_chips_start")
    dx, dg0 = _dgrad_norm(pieces0, wg_in0, x, g0, dh1, token0, False, "dgrad_norm_attn")
    for n in names1[1:]:
        big[n] = _finish(n, *finish1[n], *shard[n], token0)

    small_part = jnp.concatenate([dg0, dg1, d_mem_g.reshape(2, -1), d_final_g, dcw[0:3]], axis=0)
    small_part = jnp.concatenate([small_part, jnp.broadcast_to(loss_acc[0, 0], small_part.shape)], axis=0)
    small_w = dict(norm_g=(norm_g, m_norm_g, v_norm_g), mem_norm_g=(mem_norm_g, m_mem_norm_g, v_mem_norm_g),
                   conv_w=(conv_w, m_conv_w, v_conv_w), final_g=(final_g, m_final_g, v_final_g))
    loss_tile, small_res = _sum_adamw_small(
        _all_gather_small(small_part, [big[n][1] for n in names1[1:]], "gather_small_grads"), ck_arr,
        {n: tuple(t.reshape(-1, t.shape[-1]) for t in wmv) for n, wmv in small_w.items()})
    loss = loss_tile[0, 0]
    for n, wmv in small_w.items():
        big[n] = tuple(t.reshape(wmv[0].shape) for t in small_res[n])

    _, from_chips0 = _split_wait(started0, _plan_to_chips, big["final_g"][1], "rs0_chips_wait")
    for n, (pf, _), r in zip(names0, parts0, from_chips0):
        big[n] = _finish(n, pf, r, *shard[n], big["final_g"][1])
    for n in ("attn_w_in", "attn_w_out", "conv_w_in", "conv_w_out"):
        big[n] = tuple(t[None] for t in big[n])
    big["w_mem_kv"] = tuple(t.reshape(w_mem_kv.shape) for t in big["w_mem_kv"])

    order = ["norm_g", "mem_norm_g", "w_mem_kv", "attn_w_in", "attn_w_out", "conv_w_in", "conv_w", "conv_w_out", "final_g"]
    return (loss, dx[None], *[big[n][0] for n in order], *[big[n][1] for n in order],
            *[big[n][2] for n in order], *[big[n][3] for n in order])
```

```python
import jax
import jax.numpy as jnp
from jax import lax
from jax.experimental import pallas as pl
from jax.experimental.pallas import tpu as pltpu

F32 = jnp.float32
BF16 = jnp.bfloat16

N_DEV = 8
D_MODEL = 1024
HEAD_DIM = 64
ROT_DIM = HEAD_DIM // 4
ROPE_THETA = 500000.0
DILATIONS = (1, 4, 16)
HEADS_PER_GROUP = 8
GROUP_WIDTH = HEADS_PER_GROUP * HEAD_DIM
BLOCK = 128
N_MEM = 256
MEM_HEADS = 4
MEM_WIDTH = MEM_HEADS * HEAD_DIM
CONV_WIDTH = D_MODEL
EPS = 1e-6
SCALE = HEAD_DIM ** -0.5
NEG = -1e30

ADAM_LR = 0.001
ADAM_B1 = 0.9
ADAM_B2 = 0.999
ADAM_EPS = 1e-08
ADAM_WD = 0.01
ADAM_STEP = 10

ROW_TILE = 256
WGRAD_SHARDS = 4
LANES = 128
MESH = pl.DeviceIdType.MESH
ANY = pl.BlockSpec(memory_space=pl.ANY)


def _pallas_call(body, **kw):
    call = pl.pallas_call(body, **kw)

    def run(*args):
        pinned = [pltpu.with_memory_space_constraint(a, pltpu.HBM) if jnp.issubdtype(a.dtype, jnp.floating) else a
                  for a in args]
        return call(*pinned)

    return run


def _dot(a, b):
    return lax.dot_general(a, b, (((1,), (0,)), ((), ())), preferred_element_type=F32)


def _dot_nt(a, b):
    return lax.dot_general(a, b, (((1,), (1,)), ((), ())), preferred_element_type=F32)


def _dot_tn(a, b):
    return lax.dot_general(a, b, (((0,), (0,)), ((), ())), preferred_element_type=F32)


def _params(n_grid, vmem_mb=48):
    return pltpu.CompilerParams(dimension_semantics=("arbitrary",) * n_grid, vmem_limit_bytes=vmem_mb << 20)


def _rows(width, tm=ROW_TILE):
    return pl.BlockSpec((tm, width), lambda i: (i, 0))


def _view_rows(width, d, tm=ROW_TILE):
    return pl.BlockSpec((tm // d, d * width), lambda i: (i, 0))


def _whole(shape):
    return pl.BlockSpec(shape, lambda *_: (0,) * len(shape))


def _resident(shape):
    return pl.BlockSpec(shape, lambda *_: (0,) * len(shape), pipeline_mode=pl.Buffered(1))


def _sds(shape, dtype):
    return pltpu.HBM(shape, dtype)


def _plain(shape, dtype):
    return jax.ShapeDtypeStruct(shape, dtype)


def _silu_parts(z):
    sg = jax.nn.sigmoid(z)
    return z * sg, sg * (1.0 + z * (1.0 - sg))


def _to_view(scr, val, out_ref, d):
    tm, w = val.shape
    if d == 1:
        out_ref[...] = val.astype(out_ref.dtype)
        return
    for cb in range(w // LANES):
        scr[cb] = val[:, cb * LANES:(cb + 1) * LANES]
    for r in range(d):
        for cb in range(w // LANES):
            lo = r * w + cb * LANES
            out_ref[:, lo:lo + LANES] = scr[cb, pl.ds(r, tm // d, stride=d), :].astype(out_ref.dtype)


def _from_view(scr, in_ref, d):
    if d == 1:
        return in_ref[...].astype(F32)
    nc, tm, _ = scr.shape
    w = nc * LANES
    for r in range(d):
        for cb in range(nc):
            lo = r * w + cb * LANES
            scr[cb, pl.ds(r, tm // d, stride=d), :] = in_ref[:, lo:lo + LANES].astype(F32)
    return jnp.concatenate([scr[cb] for cb in range(nc)], axis=1)


def _view_scratch(width, tm=ROW_TILE):
    return pltpu.VMEM((width // LANES, tm, LANES), F32)


def _rope_tables(pos):
    half = ROT_DIM // 2
    inv_freq = ROPE_THETA ** (-jnp.arange(half, dtype=F32) * (2.0 / ROT_DIM))
    ang = pos.astype(F32)[:, None] * inv_freq
    cos, sin = jnp.cos(ang), jnp.sin(ang)
    s = pos.shape[0]
    z8 = jnp.zeros((s, half), F32)
    rest = HEAD_DIM - ROT_DIM
    cosf = jnp.concatenate([cos, cos, jnp.ones((s, rest), F32)], axis=1)
    sa = jnp.concatenate([-sin, z8, jnp.zeros((s, rest), F32)], axis=1)
    sb = jnp.concatenate([z8, sin, jnp.zeros((s, rest), F32)], axis=1)
    return tuple(jnp.tile(t, (1, LANES // HEAD_DIM)) for t in (cosf, sa, sb))


def _rope_fwd(t, cv, sav, sbv):
    w = t.shape[1]
    return t * cv + pltpu.roll(t, w - ROT_DIM // 2, 1) * sav + pltpu.roll(t, ROT_DIM // 2, 1) * sbv


def _rope_bwd(g, cv, sav, sbv):
    w = g.shape[1]
    return g * cv + pltpu.roll(g * sav, ROT_DIM // 2, 1) + pltpu.roll(g * sbv, w - ROT_DIM // 2, 1)


def _joined_columns(wg_ref):
    assert wg_ref.shape[2] % LANES == 0
    return jnp.concatenate([wg_ref[j] for j in range(N_DEV)], axis=1)


def _join_once(wg_ref, w_scr):
    c = wg_ref.shape[2]

    @pl.when(pl.program_id(0) == 0)
    def _():
        for j in range(N_DEV):
            w_scr[:, j * c:(j + 1) * c] = wg_ref[j]


def _inproj_attn(x, g, wg, tabs, token):
    s, d_model = x.shape
    gw = GROUP_WIDTH
    n = N_DEV * wg.shape[2]
    nz = n - 9 * gw - MEM_WIDTH
    reps = gw // LANES
    tm = ROW_TILE

    def body(_, x_ref, g_ref, w_ref, c_ref, sa_ref, sb_ref, hn_ref, *rest):
        outs, (wj, scr, tscr) = rest[:-3], rest[-3:]
        q_refs, k_refs, v_refs, t_refs, qm_ref, z_ref = outs[0:3], outs[3:6], outs[6:9], outs[9:18], outs[18], outs[19]
        _join_once(w_ref, wj)
        xb = x_ref[...]
        r = lax.rsqrt(jnp.mean(xb * xb, axis=-1, keepdims=True) + EPS)
        hn = ((xb * r) * g_ref[...]).astype(BF16)
        hn_ref[...] = hn
        proj = lambda lo, hi: _dot(hn, wj[:, lo:hi])
        tab = (c_ref[...], sa_ref[...], sb_ref[...])
        cv, sav, sbv = [jnp.tile(t, (1, reps)) for t in tab]
        for j, d in enumerate(DILATIONS):
            tq = _rope_fwd(proj(j * gw, (j + 1) * gw), cv, sav, sbv)
            _to_view(scr, tq * SCALE, q_refs[j], d)
            tk = _rope_fwd(proj((3 + j) * gw, (4 + j) * gw), cv, sav, sbv)
            _to_view(scr, tk, k_refs[j], d)
            _to_view(scr, proj((6 + j) * gw, (7 + j) * gw), v_refs[j], d)
            for i in range(3):
                _to_view(tscr, tab[i], t_refs[3 * j + i], d)
        qm_ref[...] = proj(9 * gw, 9 * gw + MEM_WIDTH).astype(BF16)
        z_ref[...] = proj(9 * gw + MEM_WIDTH, n)

    views = [_sds((s // d, d * gw), BF16) for d in DILATIONS]
    tviews = [_sds((s // d, d * LANES), F32) for d in DILATIONS for _ in range(3)]
    out_shape = [_sds((s, d_model), BF16)] + views * 3 + tviews + [_sds((s, MEM_WIDTH), BF16), _sds((s, nz), F32)]
    vspecs = [_view_rows(gw, d, tm) for d in DILATIONS]
    tspecs = [_view_rows(LANES, d, tm) for d in DILATIONS for _ in range(3)]
    out_specs = [_rows(d_model, tm)] + vspecs * 3 + tspecs + [_rows(MEM_WIDTH, tm), _rows(nz, tm)]
    res = _pallas_call(
        body, name="inproj_attn", grid=(s // tm,), out_shape=out_shape,
        in_specs=[ANY, _rows(d_model, tm), _whole((1, d_model)), _resident(wg.shape)] + [_rows(LANES, tm)] * 3,
        out_specs=out_specs,
        scratch_shapes=[pltpu.VMEM((d_model, n), BF16), _view_scratch(gw, tm), _view_scratch(LANES, tm)],
        compiler_params=_params(1, 60),
    )(token, x, g, wg, *tabs)
    tabs_v = [res[10 + 3 * j:13 + 3 * j] for j in range(3)]
    return res[0], res[1:4], res[4:7], res[7:10], tabs_v, res[19], res[20]


def _band_mask(n_keys):
    qi = lax.broadcasted_iota(jnp.int32, (BLOCK, n_keys), 0)
    kj = lax.broadcasted_iota(jnp.int32, (BLOCK, n_keys), 1)
    if n_keys == BLOCK:
        return kj <= qi
    return jnp.logical_or(jnp.logical_and(kj < BLOCK, kj >= qi), jnp.logical_and(kj >= BLOCK, (kj - BLOCK) <= qi))


def _low_head_lanes():
    return lax.broadcasted_iota(jnp.int32, (1, LANES), 1) < HEAD_DIM


def _split_pair(t, low):
    zero = jnp.zeros_like(t)
    return jnp.where(low, t, zero), jnp.where(low, zero, t)


def _pair_specs(d, nb, w):
    if nb == 1:
        return pl.BlockSpec((BLOCK, 2 * w), lambda n: (0, n)), None
    half = nb // 2
    two = pl.BlockSpec((2 * BLOCK, w), lambda n: (n % half, n // half))
    before = pl.BlockSpec((BLOCK, w), lambda n: (jnp.maximum(2 * (n % half) - 1, 0), n // half))
    return two, before


def _head_tiles(w, col0):
    return ([slice(p * LANES, (p + 1) * LANES) for p in range(w // LANES)],
            [slice(col0 + p * LANES, col0 + (p + 1) * LANES) for p in range(w // LANES)])


def _attend_fwd(q_ref, o_ref, lse_ref, rows, col0, kk, vv):
    w = kk.shape[1]
    valid = _band_mask(kk.shape[0])
    low = _low_head_lanes()
    pairs, qcols = _head_tiles(w, col0)
    qs_ = [h for qc in qcols for h in _split_pair(q_ref[rows, qc], low)]
    k2s = [kk[:, pr] for pr in pairs for _ in range(2)]
    scs = [jnp.where(valid, _dot_nt(qh, k2), NEG) for qh, k2 in zip(qs_, k2s)]
    ms = [jnp.max(sc, axis=-1, keepdims=True) for sc in scs]
    ps = [jnp.exp(sc - m) for sc, m in zip(scs, ms)]
    ls = [jnp.sum(p, axis=-1, keepdims=True) for p in ps]
    pns = [(p * (1.0 / l)).astype(BF16) for p, l in zip(ps, ls)]
    for i, (pr, qc) in enumerate(zip(pairs, qcols)):
        v2 = vv[:, pr]
        a, b = 2 * i, 2 * i + 1
        o_ref[rows, qc] = jnp.where(low, _dot(pns[a], v2), _dot(pns[b], v2))
        lse_ref[rows, qc] = jnp.where(low, ms[a] + jnp.log(ls[a]), ms[b] + jnp.log(ls[b]))


TOP, BOTTOM = slice(0, BLOCK), slice(BLOCK, 2 * BLOCK)


def _attn_fwd(q, k, v, d, token):
    ln, dw = q.shape
    w = dw // d
    nb = ln // BLOCK
    two, before = _pair_specs(d, nb, w)

    def body_streams(_, q_ref, kc_ref, vc_ref, o_ref, lse_ref):
        for sb in range(2):
            cols = slice(sb * w, (sb + 1) * w)
            _attend_fwd(q_ref, o_ref, lse_ref, TOP, sb * w, kc_ref[:, cols], vc_ref[:, cols])

    def body_blocks(_, q_ref, kp_ref, kc_ref, vp_ref, vc_ref, o_ref, lse_ref):
        first = pl.program_id(0) % (nb // 2) == 0
        pl.when(first)(lambda: _attend_fwd(q_ref, o_ref, lse_ref, TOP, 0, kc_ref[TOP, :], vc_ref[TOP, :]))
        pl.when(jnp.logical_not(first))(lambda: _attend_fwd(
            q_ref, o_ref, lse_ref, TOP, 0, jnp.concatenate([kp_ref[...], kc_ref[TOP, :]], axis=0),
            jnp.concatenate([vp_ref[...], vc_ref[TOP, :]], axis=0)))
        _attend_fwd(q_ref, o_ref, lse_ref, BOTTOM, 0, kc_ref[...], vc_ref[...])

    if nb == 1:
        body, in_specs, args = body_streams, [ANY, two, two, two], (token, q, k, v)
    else:
        body, in_specs, args = body_blocks, [ANY, two, before, two, before, two], (token, q, k, k, v, v)
    return _pallas_call(
        body, name=f"attn_fwd_d{d}", grid=(d * nb // 2,), out_shape=[_sds((ln, dw), F32)] * 2,
        in_specs=in_specs, out_specs=[two, two], compiler_params=_params(1, 32),
    )(*args)


def _memkv_fwd(mem, g, w):
    n_layers = g.shape[0]
    rows = D_MODEL // N_DEV

    def body(mem_ref, g_ref, w_ref, *kv_refs):
        mb = mem_ref[...]
        r = lax.rsqrt(jnp.mean(mb * mb, axis=-1, keepdims=True) + EPS)
        mn = ((mb * r) * g_ref[...]).astype(BF16)
        kv = _dot(mn, w_ref[...].reshape(D_MODEL, 2 * MEM_WIDTH)).astype(BF16)
        for layer, kv_ref in enumerate(kv_refs):
            @pl.when(pl.program_id(0) == layer)
            def _(kv_ref=kv_ref):
                kv_ref[...] = kv

    return _pallas_call(
        body, name="memkv_fwd", grid=(n_layers,),
        out_shape=[_sds((N_MEM, 2 * MEM_WIDTH), BF16)] * n_layers,
        in_specs=[_whole(mem.shape), pl.BlockSpec((None, 1, D_MODEL), lambda l: (l, 0, 0)),
                  pl.BlockSpec((N_DEV, rows, 2 * MEM_WIDTH), lambda l: (0, l, 0))],
        out_specs=[_whole((N_MEM, 2 * MEM_WIDTH))] * n_layers,
        compiler_params=_params(1, 32),
    )(mem, g.reshape(n_layers, 1, D_MODEL), w)


def _mix_groups(os_, ls_):
    mx = jnp.maximum(jnp.maximum(ls_[0], ls_[1]), ls_[2])
    es = [jnp.exp(t - mx) for t in ls_]
    inv = 1.0 / (es[0] + es[1] + es[2])
    ws = [e * inv for e in es]
    mix = ws[0] * os_[0] + ws[1] * os_[1] + ws[2] * os_[2]
    return ws, mix


MEM_PAIRS = [slice(p * LANES, (p + 1) * LANES) for p in range(MEM_WIDTH // LANES)]


def _mem_probs(qhs, k2s):
    scs = [_dot_nt(qh, k2) * SCALE for qh, k2 in zip(qhs, k2s)]
    es = [jnp.exp(sc - jnp.max(sc, axis=-1, keepdims=True)) for sc in scs]
    return [e * (1.0 / jnp.sum(e, axis=-1, keepdims=True)) for e in es]


def _mem_attn_into(qm, kv_ref, mo_ref):
    low = _low_head_lanes()
    qhs = [h for pr in MEM_PAIRS for h in _split_pair(qm[:, pr], low)]
    k2s = [kv_ref[:, pr] for pr in MEM_PAIRS for _ in range(2)]
    ps = [p.astype(BF16) for p in _mem_probs(qhs, k2s)]
    for i, pr in enumerate(MEM_PAIRS):
        v2 = kv_ref[:, MEM_WIDTH + i * LANES:MEM_WIDTH + (i + 1) * LANES]
        mo_ref[:, pr] = jnp.where(low, _dot(ps[2 * i], v2), _dot(ps[2 * i + 1], v2))


def _mem_attn_bwd(qm, kv_ref, dmem, dqm_ref, dkv_ref):
    low = _low_head_lanes()
    dmb = dmem.astype(BF16)
    vps = [slice(MEM_WIDTH + i * LANES, MEM_WIDTH + (i + 1) * LANES) for i in range(len(MEM_PAIRS))]
    qhs = [h for pr in MEM_PAIRS for h in _split_pair(qm[:, pr], low)]
    dhs = [h for pr in MEM_PAIRS for h in _split_pair(dmb[:, pr], low)]
    k2s = [kv_ref[:, pr] for pr in MEM_PAIRS for _ in range(2)]
    v2s = [kv_ref[:, vp] for vp in vps for _ in range(2)]
    ps = _mem_probs(qhs, k2s)
    dps = [_dot_nt(dh, v2) for dh, v2 in zip(dhs, v2s)]
    dss = [(p * (dp - jnp.sum(dp * p, axis=-1, keepdims=True)) * SCALE).astype(BF16) for p, dp in zip(ps, dps)]
    pbs = [p.astype(BF16) for p in ps]
    for i, (pr, vp) in enumerate(zip(MEM_PAIRS, vps)):
        a, b = 2 * i, 2 * i + 1
        dqm_ref[:, pr] = jnp.where(low, _dot(dss[a], k2s[a]), _dot(dss[b], k2s[b])).astype(BF16)
        dkv_ref[:, pr] += _dot_tn(dss[a], qhs[a]) + _dot_tn(dss[b], qhs[b])
        dkv_ref[:, vp] += _dot_tn(pbs[a], dhs[a]) + _dot_tn(pbs[b], dhs[b])


def _post_attn(os_, ls_, qm, kv, z, x, wg_out):
    s, d_model = x.shape
    gw = GROUP_WIDTH
    nb = gw + MEM_WIDTH
    tm = ROW_TILE

    def body(o0, o1, o2, l0, l1, l2, qm_ref, kv_ref, z_ref, x_ref, w_ref, y_ref, yt_ref, mo_ref, h_ref, s0, s1):
        ov, lv = [], []
        for o_ref, l_ref, d in zip((o0, o1, o2), (l0, l1, l2), DILATIONS):
            ov.append(_from_view(s0, o_ref, d))
            lv.append(_from_view(s1, l_ref, d))
        _, mix = _mix_groups(ov, lv)
        _mem_attn_into(qm_ref[...], kv_ref, mo_ref)
        sz, _ = _silu_parts(z_ref[...])
        y_ref[:, :gw] = (mix * sz[:, :gw]).astype(BF16)
        y_ref[:, gw:] = (mo_ref[...] * sz[:, gw:]).astype(BF16)
        y = y_ref[...]
        yt_ref[...] = y.T
        h_ref[...] = x_ref[...] + _dot(y, _joined_columns(w_ref))

    vspecs = [_view_rows(gw, d) for d in DILATIONS]
    return _pallas_call(
        body, name="post_attn", grid=(s // tm,),
        out_shape=[_sds((s, nb), BF16), _sds((nb, s), BF16), _sds((s, MEM_WIDTH), F32), _sds((s, d_model), F32)],
        in_specs=vspecs * 2 + [_rows(MEM_WIDTH), _whole(kv.shape), _rows(nb), _rows(d_model), _whole(wg_out.shape)],
        out_specs=[_rows(nb), pl.BlockSpec((nb, tm), lambda i: (0, i)), _rows(MEM_WIDTH), _rows(d_model)],
        scratch_shapes=[_view_scratch(gw), _view_scratch(gw)],
        compiler_params=_params(1, 40),
    )(*os_, *ls_, qm, kv, z, x, wg_out)


def _inproj_conv(x, g, wg):
    s, d_model = x.shape
    c = CONV_WIDTH
    n = N_DEV * wg.shape[2]
    nz = n - 3 * c - MEM_WIDTH
    tm = ROW_TILE

    def body(x_ref, g_ref, w_ref, hn_ref, bg_ref, cg_ref, u_ref, qm_ref, z_ref, wj):
        _join_once(w_ref, wj)
        xb = x_ref[...]
        r = lax.rsqrt(jnp.mean(xb * xb, axis=-1, keepdims=True) + EPS)
        hn = ((xb * r) * g_ref[...]).astype(BF16)
        hn_ref[...] = hn
        bg_ref[...] = _dot(hn, wj[:, 0:c])
        cg_ref[...] = _dot(hn, wj[:, c:2 * c])
        u_ref[...] = _dot(hn, wj[:, 2 * c:3 * c])
        qm_ref[...] = _dot(hn, wj[:, 3 * c:3 * c + MEM_WIDTH]).astype(BF16)
        z_ref[...] = _dot(hn, wj[:, 3 * c + MEM_WIDTH:])

    return _pallas_call(
        body, name="inproj_conv", grid=(s // tm,),
        out_shape=[_sds((s, d_model), BF16)] + [_sds((s, c), F32)] * 3 + [_sds((s, MEM_WIDTH), BF16), _sds((s, nz), F32)],
        in_specs=[_rows(d_model), _whole((1, d_model)), _resident(wg.shape)],
        out_specs=[_rows(d_model)] + [_rows(c)] * 3 + [_rows(MEM_WIDTH), _rows(nz)],
        scratch_shapes=[pltpu.VMEM((d_model, n), BF16)],
        compiler_params=_params(1, 60),
    )(x, g, wg)


HALO = 8


def _halo_before(width, tm=ROW_TILE):
    return pl.BlockSpec((HALO, width), lambda i: (jnp.maximum(i * (tm // HALO) - 1, 0), 0))


def _halo_after(width, n_rows, tm=ROW_TILE):
    return pl.BlockSpec((HALO, width), lambda i: (jnp.minimum((i + 1) * (tm // HALO), n_rows // HALO - 1), 0))


def _conv_taps(cg_ref, u_ref, cgh_ref, uh_ref, i):
    a = cg_ref[...] * u_ref[...]
    ah = jnp.where(i > 0, cgh_ref[...] * uh_ref[...], 0.0)
    row = lax.broadcasted_iota(jnp.int32, a.shape, 0)
    a1 = jnp.where(row == 0, ah[HALO - 1:HALO], pltpu.roll(a, 1, 0))
    a2 = jnp.where(row == 0, ah[HALO - 2:HALO - 1], jnp.where(row == 1, ah[HALO - 1:HALO], pltpu.roll(a, 2, 0)))
    return a, a1, a2


def _post_conv_loss(bg, cg, u, qm, kv, z, h1, w_out, cw, gf, tgt):
    s, d = h1.shape
    c = CONV_WIDTH
    nb = c + MEM_WIDTH
    tm = ROW_TILE

    def body(bg_ref, cg_ref, u_ref, cgh_ref, uh_ref, qm_ref, kv_ref, z_ref, h_ref, w_ref, cw_ref, gf_ref, t_ref,
             y_ref, yt_ref, mo_ref, dh_ref, dhb_ref, loss_ref, dgf_ref):
        i = pl.program_id(0)
        a, a1, a2 = _conv_taps(cg_ref, u_ref, cgh_ref, uh_ref, i)
        conv = cw_ref[0:1, :] * a2 + cw_ref[1:2, :] * a1 + cw_ref[2:3, :] * a
        mix = bg_ref[...] * conv
        _mem_attn_into(qm_ref[...], kv_ref, mo_ref)
        sz, _ = _silu_parts(z_ref[...])
        y_ref[:, :c] = (mix * sz[:, :c]).astype(BF16)
        y_ref[:, c:] = (mo_ref[...] * sz[:, c:]).astype(BF16)
        y = y_ref[...]
        yt_ref[...] = y.T
        h2 = h_ref[...] + _dot(y, w_ref[...])
        r = lax.rsqrt(jnp.mean(h2 * h2, axis=-1, keepdims=True) + EPS)
        nh = h2 * r
        gfv = gf_ref[...]
        diff = nh * gfv - t_ref[...]
        dout = diff * (1.0 / d)
        dn = dout * gfv
        dh2 = r * dn - h2 * ((r * r * r) * jnp.mean(dn * h2, axis=-1, keepdims=True))
        dh_ref[...] = dh2
        dhb_ref[...] = dh2.astype(BF16)

        @pl.when(i == 0)
        def _():
            loss_ref[...] = jnp.zeros_like(loss_ref)
            dgf_ref[...] = jnp.zeros_like(dgf_ref)

        loss_ref[...] += 0.5 * jnp.sum(jnp.mean(diff * diff, axis=-1, keepdims=True))
        dgf_ref[...] += jnp.sum(dout * nh, axis=0, keepdims=True)

    return _pallas_call(
        body, name="post_conv_loss", grid=(s // tm,),
        out_shape=[_sds((s, nb), BF16), _sds((nb, s), BF16), _sds((s, MEM_WIDTH), F32), _sds((s, d), F32),
                   _sds((s, d), BF16), _plain((8, LANES), F32), _plain((1, d), F32)],
        in_specs=[_rows(c)] * 3 + [_halo_before(c)] * 2 + [_rows(MEM_WIDTH), _whole(kv.shape), _rows(nb), _rows(d),
                  _whole(w_out.shape), _whole(cw.shape), _whole((1, d)), _rows(d)],
        out_specs=[_rows(nb), pl.BlockSpec((nb, tm), lambda i: (0, i)), _rows(MEM_WIDTH), _rows(d), _rows(d),
                   _whole((8, LANES)), _whole((1, d))],
        compiler_params=_params(1, 48),
    )(bg, cg, u, cg, u, qm, kv, z, h1, w_out, cw, gf, tgt)


def _bwd_post_conv(dhb, w_out, bg, cg, u, z, qm, kv, mo, cw):
    s = dhb.shape[0]
    c = CONV_WIDTH
    nb = c + MEM_WIDTH

    def body(dh_ref, w_ref, bg_ref, cg_ref, u_ref, cgh_ref, uh_ref, z_ref, qm_ref, kv_ref, mo_ref, cw_ref,
             dz_ref, dbg_ref, dc_ref, dqm_ref, dkv_ref):
        i = pl.program_id(0)

        @pl.when(i == 0)
        def _():
            dkv_ref[...] = jnp.zeros_like(dkv_ref)

        dy = _dot_nt(dh_ref[...], w_ref[...])
        sz, dsz = _silu_parts(z_ref[...])
        a, a1, a2 = _conv_taps(cg_ref, u_ref, cgh_ref, uh_ref, i)
        conv = cw_ref[0:1, :] * a2 + cw_ref[1:2, :] * a1 + cw_ref[2:3, :] * a
        bgv = bg_ref[...]
        dz_ref[:, :c] = (dy[:, :c] * (bgv * conv) * dsz[:, :c]).astype(BF16)
        dz_ref[:, c:] = (dy[:, c:] * mo_ref[...] * dsz[:, c:]).astype(BF16)
        dbr = dy * sz
        dmix = dbr[:, :c]
        dbg_ref[...] = (dmix * conv).astype(BF16)
        dc_ref[...] = dmix * bgv
        _mem_attn_bwd(qm_ref[...], kv_ref, dbr[:, c:], dqm_ref, dkv_ref)

    return _pallas_call(
        body, name="bwd_post_conv", grid=(s // ROW_TILE,),
        out_shape=[_sds((s, nb), BF16), _sds((s, c), BF16), _sds((s, c), F32), _sds((s, MEM_WIDTH), BF16),
                   _plain(kv.shape, F32)],
        in_specs=[_rows(D_MODEL), _whole(w_out.shape)] + [_rows(c)] * 3 + [_halo_before(c)] * 2
                 + [_rows(nb), _rows(MEM_WIDTH), _whole(kv.shape), _rows(MEM_WIDTH), _whole(cw.shape)],
        out_specs=[_rows(nb), _rows(c), _rows(c), _rows(MEM_WIDTH), _whole(kv.shape)],
        compiler_params=_params(1, 48),
    )(dhb, w_out, bg, cg, u, cg, u, z, qm, kv, mo, cw)


def _bwd_conv(dconv, cg, u, cw):
    s, c = dconv.shape
    tm = ROW_TILE
    last = s // tm - 1

    def body(dc_ref, dcn_ref, cg_ref, u_ref, cgh_ref, uh_ref, cw_ref, dcg_ref, du_ref, dcw_ref):
        i = pl.program_id(0)

        @pl.when(i == 0)
        def _():
            dcw_ref[...] = jnp.zeros_like(dcw_ref)

        dc = dc_ref[...]
        dcn = jnp.where(i < last, dcn_ref[...], 0.0)
        row = lax.broadcasted_iota(jnp.int32, dc.shape, 0)
        d1 = jnp.where(row == tm - 1, dcn[0:1], pltpu.roll(dc, tm - 1, 0))
        d2 = jnp.where(row == tm - 1, dcn[1:2], jnp.where(row == tm - 2, dcn[0:1], pltpu.roll(dc, tm - 2, 0)))
        da = cw_ref[2:3, :] * dc + cw_ref[1:2, :] * d1 + cw_ref[0:1, :] * d2
        a, a1, a2 = _conv_taps(cg_ref, u_ref, cgh_ref, uh_ref, i)
        dcg_ref[...] = (da * u_ref[...]).astype(BF16)
        du_ref[...] = (da * cg_ref[...]).astype(BF16)
        dcw_ref[0:1, :] += jnp.sum(dc * a2, axis=0, keepdims=True)
        dcw_ref[1:2, :] += jnp.sum(dc * a1, axis=0, keepdims=True)
        dcw_ref[2:3, :] += jnp.sum(dc * a, axis=0, keepdims=True)

    return _pallas_call(
        body, name="bwd_conv", grid=(s // tm,),
        out_shape=[_sds((s, c), BF16), _sds((s, c), BF16), _plain((8, c), F32)],
        in_specs=[_rows(c), _halo_after(c, s), _rows(c), _rows(c), _halo_before(c), _halo_before(c), _whole(cw.shape)],
        out_specs=[_rows(c), _rows(c), _whole((8, c))], compiler_params=_params(1, 40),
    )(dconv, dconv, cg, u, cg, u, cw)


def _assemble(p_refs, pieces, widths, dp, scr):
    off = 0
    for p_ref, (_, d), wd in zip(p_refs, pieces, widths):
        if d == 1:
            dp[:, off:off + wd] = p_ref[...]
        else:
            dp[:, off:off + wd] = _from_view(scr, p_ref, d).astype(BF16)
        off += wd


def _dgrad_norm(pieces, wg, h, g, dres, token, onward, name):
    s, d_model = h.shape
    n = N_DEV * wg.shape[2]
    tm = ROW_TILE
    widths = [p.shape[1] // d for p, d in pieces]
    assert sum(widths) == n
    n_p = len(pieces)

    def body(_, *refs):
        p_refs = refs[:n_p]
        w_ref, h_ref, g_ref, dr_ref, dh_ref, dg_ref = refs[n_p:n_p + 6]
        dp, scr, wj = refs[-3:]
        _join_once(w_ref, wj)

        @pl.when(pl.program_id(0) == 0)
        def _():
            dg_ref[...] = jnp.zeros_like(dg_ref)

        _assemble(p_refs, pieces, widths, dp, scr)
        dhn = _dot_nt(dp[...], wj[...])
        hb = h_ref[...]
        r = lax.rsqrt(jnp.mean(hb * hb, axis=-1, keepdims=True) + EPS)
        dg_ref[...] += jnp.sum(dhn * (hb * r), axis=0, keepdims=True)
        dn = dhn * g_ref[...]
        dh = dr_ref[...] + r * dn - hb * ((r * r * r) * jnp.mean(dn * hb, axis=-1, keepdims=True))
        dh_ref[...] = dh
        if onward:
            dhb_ref, dpt_ref = refs[n_p + 6:n_p + 8]
            dhb_ref[...] = dh.astype(BF16)
            dpt_ref[...] = dp[...].T

    p_specs = [_view_rows(wd, d) for (_, d), wd in zip(pieces, widths)]
    out_shape = [_plain((s, d_model), F32), _plain((1, d_model), F32)]
    out_specs = [_rows(d_model), _whole((1, d_model))]
    if onward:
        out_shape += [_sds((s, d_model), BF16), _sds((n, s), BF16)]
        out_specs += [_rows(d_model), pl.BlockSpec((n, tm), lambda i: (0, i))]
    return _pallas_call(
        body, name=name, grid=(s // tm,), out_shape=out_shape,
        in_specs=[ANY] + p_specs + [_resident(wg.shape), _rows(d_model), _whole((1, d_model)), _rows(d_model)],
        out_specs=out_specs,
        scratch_shapes=[pltpu.VMEM((tm, n), BF16), _view_scratch(GROUP_WIDTH), pltpu.VMEM((d_model, n), BF16)],
        compiler_params=_params(1, 60),
    )(token, *[p for p, _ in pieces], wg, h, g, dres)


def _assemble_dproj_t(pieces, n, name):
    tm = ROW_TILE
    widths = [p.shape[1] // d for p, d in pieces]
    assert sum(widths) == n
    s = pieces[0][0].shape[0] * pieces[0][1]
    n_p = len(pieces)

    def body(*refs):
        p_refs, (dpt_ref, dp, scr) = refs[:n_p], refs[n_p:]
        _assemble(p_refs, pieces, widths, dp, scr)
        dpt_ref[...] = dp[...].T

    return _pallas_call(
        body, name=name, grid=(s // tm,), out_shape=_sds((n, s), BF16),
        in_specs=[_view_rows(wd, d) for (_, d), wd in zip(pieces, widths)],
        out_specs=pl.BlockSpec((n, tm), lambda i: (0, i)),
        scratch_shapes=[pltpu.VMEM((tm, n), BF16), _view_scratch(GROUP_WIDTH)],
        compiler_params=_params(1, 40),
    )(*[p for p, _ in pieces])


def _wgrad_shards_t(dp_t, h, c, name):
    n, s = dp_t.shape
    d_model = h.shape[1]
    per_step = 2

    def body(a_ref, b_ref, o_ref):
        o_ref[...] = _dot(a_ref[...], b_ref[...]).astype(BF16).reshape(per_step, c, d_model)

    return _pallas_call(
        body, name=name, grid=(N_DEV // per_step,), out_shape=_sds((N_DEV, c, d_model), BF16),
        in_specs=[pl.BlockSpec((per_step * c, s), lambda j: (j, 0)), _resident(h.shape)],
        out_specs=pl.BlockSpec((per_step, c, d_model), lambda j: (j, 0, 0)), compiler_params=_params(1, 40),
    )(dp_t, h)


def _wgrad_cols(a_t, b, c, name):
    m, s = a_t.shape
    assert c % LANES == 0

    def body(a_ref, b_ref, o_ref):
        wide = _dot(a_ref[...], b_ref[...]).astype(BF16)
        for j in range(WGRAD_SHARDS):
            o_ref[j] = wide[:, j * c:(j + 1) * c]

    return _pallas_call(
        body, name=name, grid=(N_DEV // WGRAD_SHARDS,), out_shape=_sds((N_DEV, m, c), BF16),
        in_specs=[_whole(a_t.shape), pl.BlockSpec((s, WGRAD_SHARDS * c), lambda j: (0, j))],
        out_specs=pl.BlockSpec((WGRAD_SHARDS, m, c), lambda j: (j, 0, 0)), compiler_params=_params(1, 40),
    )(a_t, b)


def _wgrad_rows(a_t, b, name):
    m, s = a_t.shape
    n = b.shape[1]
    mr = m // N_DEV

    def body(a_ref, b_ref, o_ref):
        o_ref[...] = _dot(a_ref[...], b_ref[...]).astype(BF16).reshape(WGRAD_SHARDS, mr, n)

    return _pallas_call(
        body, name=name, grid=(N_DEV // WGRAD_SHARDS,), out_shape=_sds((N_DEV, mr, n), BF16),
        in_specs=[pl.BlockSpec((WGRAD_SHARDS * mr, s), lambda j: (j, 0)), _whole(b.shape)],
        out_specs=pl.BlockSpec((WGRAD_SHARDS, mr, n), lambda j: (j, 0, 0)), compiler_params=_params(1, 40),
    )(a_t, b)


def _memkv_bwd(dkvs, w, mem, g):
    n_layers = g.shape[0]
    rows = D_MODEL // N_DEV

    def body(dkv0_ref, dkv1_ref, w_ref, mem_ref, g_ref, dw_ref, dg_ref):
        mb = mem_ref[...]
        r = lax.rsqrt(jnp.mean(mb * mb, axis=-1, keepdims=True) + EPS)
        nm = mb * r
        mn = (nm * g_ref[...]).astype(BF16)
        dkvb = jnp.where(pl.program_id(0) == 0, dkv0_ref[...], dkv1_ref[...]).astype(BF16)
        dw_ref[...] = _dot_tn(mn, dkvb).astype(BF16).reshape(N_DEV, rows, 2 * MEM_WIDTH)
        dmn = _dot_nt(dkvb, w_ref[...].reshape(D_MODEL, 2 * MEM_WIDTH))
        dg_ref[...] = jnp.sum(dmn * nm, axis=0, keepdims=True)

    lay = lambda *shape: pl.BlockSpec((None,) + shape, lambda l: (l, 0, 0))
    major = pl.BlockSpec((N_DEV, rows, 2 * MEM_WIDTH), lambda l: (0, l, 0))
    return _pallas_call(
        body, name="memkv_bwd", grid=(n_layers,),
        out_shape=[_sds((N_DEV, n_layers * rows, 2 * MEM_WIDTH), BF16), _plain((n_layers, 1, D_MODEL), F32)],
        in_specs=[_whole(dkvs[0].shape), _whole(dkvs[1].shape), major, _whole(mem.shape), lay(1, D_MODEL)],
        out_specs=[major, lay(1, D_MODEL)],
        compiler_params=_params(1, 32),
    )(*dkvs, w, mem, g.reshape(n_layers, 1, D_MODEL))


def _bwd_post_attn(dhb, wg_out, z, os_, ls_, qm, kv, mo, head_ones, token):
    s = dhb.shape[0]
    gw = GROUP_WIDTH
    nb = gw + MEM_WIDTH
    tm = ROW_TILE

    def body(_, dh_ref, w_ref, z_ref, o0, o1, o2, l0, l1, l2, qm_ref, kv_ref, mo_ref, bd_ref,
             dz_ref, do0, do1, do2, dl0, dl1, dl2, dqm_ref, dkv_ref, s0, s1):
        @pl.when(pl.program_id(0) == 0)
        def _():
            dkv_ref[...] = jnp.zeros_like(dkv_ref)

        dy = _dot_nt(dh_ref[...], _joined_columns(w_ref))
        ov, lv = [], []
        for o_ref, l_ref, d in zip((o0, o1, o2), (l0, l1, l2), DILATIONS):
            ov.append(_from_view(s0, o_ref, d))
            lv.append(_from_view(s1, l_ref, d))
        ws, mix = _mix_groups(ov, lv)
        sz, dsz = _silu_parts(z_ref[...])
        dz_ref[:, :gw] = (dy[:, :gw] * mix * dsz[:, :gw]).astype(BF16)
        dz_ref[:, gw:] = (dy[:, gw:] * mo_ref[...] * dsz[:, gw:]).astype(BF16)
        dbr = dy * sz
        dmix = dbr[:, :gw]
        t = dmix * mix
        th = t.astype(BF16)
        tl = (t - th.astype(F32)).astype(BF16)
        rs = _dot(th, bd_ref[...]) + _dot(tl, bd_ref[...])
        for wg_, do_ref, dl_ref, d in zip(ws, (do0, do1, do2), (dl0, dl1, dl2), DILATIONS):
            _to_view(s0, wg_ * dmix, do_ref, d)
            _to_view(s1, wg_ * rs, dl_ref, d)
        _mem_attn_bwd(qm_ref[...], kv_ref, dbr[:, gw:], dqm_ref, dkv_ref)

    vspecs = [_view_rows(gw, d) for d in DILATIONS]
    return _pallas_call(
        body, name="bwd_post_attn", grid=(s // tm,),
        out_shape=[_sds((s, nb), BF16)] + [_sds((s // d, d * gw), BF16) for d in DILATIONS]
                  + [_sds((s // d, d * gw), F32) for d in DILATIONS] + [_sds((s, MEM_WIDTH), BF16), _plain(kv.shape, F32)],
        in_specs=[ANY, _rows(D_MODEL), _whole(wg_out.shape), _rows(nb)] + vspecs * 2
                 + [_rows(MEM_WIDTH), _whole(kv.shape), _rows(MEM_WIDTH), _whole(head_ones.shape)],
        out_specs=[_rows(nb)] + vspecs * 2 + [_rows(MEM_WIDTH), _whole(kv.shape)],
        scratch_shapes=[_view_scratch(gw), _view_scratch(gw)],
        compiler_params=_params(1, 48),
    )(token, dhb, wg_out, z, *os_, *ls_, qm, kv, mo, head_ones)


def _attn_bwd(q, k, v, lse, do, dl, tabs, d, token):
    ln, dw = q.shape
    w = dw // d
    nb = ln // BLOCK
    reps = w // LANES
    two, before = _pair_specs(d, nb, w)
    two_t, _ = _pair_specs(d, nb, LANES)

    def attend(q_ref, l_ref, do_ref, dl_ref, dqs, acck, accv, rows, col0, kk, vv, acc_rows):
        valid = _band_mask(kk.shape[0])
        low = _low_head_lanes()
        pairs, qcols = _head_tiles(w, col0)
        cols = [slice(col0 + h * HEAD_DIM, col0 + h * HEAD_DIM + 1) for h in range(HEADS_PER_GROUP)]
        qhs = [h for qc in qcols for h in _split_pair(q_ref[rows, qc], low)]
        dobs = [h for qc in qcols for h in _split_pair(do_ref[rows, qc], low)]
        k2s = [kk[:, pr] for pr in pairs for _ in range(2)]
        v2s = [vv[:, pr] for pr in pairs for _ in range(2)]
        scs = [jnp.where(valid, _dot_nt(qh, k2), NEG) for qh, k2 in zip(qhs, k2s)]
        dps = [_dot_nt(dob, v2) for dob, v2 in zip(dobs, v2s)]
        ps = [jnp.exp(sc - l_ref[rows, col]) for sc, col in zip(scs, cols)]
        dss = [(p * (dp - dl_ref[rows, col])).astype(BF16) for p, dp, col in zip(ps, dps, cols)]
        pbs = [p.astype(BF16) for p in ps]
        for i, qc in enumerate(qcols):
            a, b = 2 * i, 2 * i + 1
            dqs[rows, qc] = jnp.where(low, _dot(dss[a], k2s[a]), _dot(dss[b], k2s[b])) * SCALE
            acck[acc_rows, qc] += _dot_tn(dss[a], qhs[a]) + _dot_tn(dss[b], qhs[b])
            accv[acc_rows, qc] += _dot_tn(pbs[a], dobs[a]) + _dot_tn(pbs[b], dobs[b])

    def body_streams(_, q_ref, kc_ref, vc_ref, l_ref, do_ref, dl_ref, c_ref, sa_ref, sb_ref,
                     dq_ref, dk_ref, dv_ref, acck, accv, dqs):
        acck[...] = jnp.zeros_like(acck)
        accv[...] = jnp.zeros_like(accv)
        for sb in range(2):
            cols = slice(sb * w, (sb + 1) * w)
            attend(q_ref, l_ref, do_ref, dl_ref, dqs, acck, accv, TOP, sb * w, kc_ref[:, cols], vc_ref[:, cols], TOP)
        tabs2 = [jnp.concatenate([jnp.tile(r[:, sb * LANES:(sb + 1) * LANES], (1, reps)) for sb in range(2)], axis=1)
                 for r in (c_ref, sa_ref, sb_ref)]
        dq_ref[...] = _rope_bwd(dqs[...], *tabs2).astype(BF16)
        dk_ref[...] = _rope_bwd(acck[...], *tabs2).astype(BF16)
        dv_ref[...] = accv[...].astype(BF16)

    def body_blocks(_, q_ref, kp_ref, kc_ref, vp_ref, vc_ref, l_ref, do_ref, dl_ref, cq, saq, sbq, ck, sak, sbk,
                    dq_ref, dk_ref, dv_ref, acck, accv, dqs):
        i = pl.program_id(0) % (nb // 2)

        @pl.when(i == 0)
        def _():
            acck[...] = jnp.zeros_like(acck)
            accv[...] = jnp.zeros_like(accv)

        refs = (q_ref, l_ref, do_ref, dl_ref, dqs, acck, accv)
        pl.when(i == 0)(lambda: attend(*refs, TOP, 0, kc_ref[TOP, :], vc_ref[TOP, :], TOP))
        pl.when(i != 0)(lambda: attend(
            *refs, TOP, 0, jnp.concatenate([kp_ref[...], kc_ref[TOP, :]], axis=0),
            jnp.concatenate([vp_ref[...], vc_ref[TOP, :]], axis=0),
            pl.ds(pl.multiple_of((2 * i - 1) * BLOCK, BLOCK), 2 * BLOCK)))
        attend(*refs, BOTTOM, 0, kc_ref[...], vc_ref[...], pl.ds(pl.multiple_of(2 * i * BLOCK, BLOCK), 2 * BLOCK))
        tq = [jnp.tile(r[...], (1, reps)) for r in (cq, saq, sbq)]
        dq_ref[...] = _rope_bwd(dqs[...], *tq).astype(BF16)

        @pl.when(i == nb // 2 - 1)
        def _():
            for r0 in range(0, nb * BLOCK, 2 * BLOCK):
                rows = slice(r0, r0 + 2 * BLOCK)
                tk = [jnp.tile(r[rows, :], (1, reps)) for r in (ck, sak, sbk)]
                dk_ref[rows, :] = _rope_bwd(acck[rows, :], *tk).astype(BF16)
                dv_ref[rows, :] = accv[rows, :].astype(BF16)

    if nb == 1:
        body = body_streams
        in_specs = [ANY] + [two] * 6 + [two_t] * 3
        args = (token, q, k, v, lse, do, dl, *tabs)
        out_specs = [two, two, two]
        acc_shape = (BLOCK, 2 * w)
    else:
        body = body_blocks
        stream = pl.BlockSpec((nb * BLOCK, w), lambda n: (0, n // (nb // 2)))
        stream_t = pl.BlockSpec((nb * BLOCK, LANES), lambda n: (0, n // (nb // 2)))
        in_specs = [ANY, two, before, two, before, two, two, two, two] + [two_t] * 3 + [stream_t] * 3
        args = (token, q, k, k, v, v, lse, do, dl, *tabs, *tabs)
        out_specs = [two, stream, stream]
        acc_shape = (nb * BLOCK, w)
    return _pallas_call(
        body, name=f"attn_bwd_d{d}", grid=(d * nb // 2,), out_shape=[_sds((ln, dw), BF16)] * 3,
        in_specs=in_specs, out_specs=out_specs,
        scratch_shapes=[pltpu.VMEM(acc_shape, F32), pltpu.VMEM(acc_shape, F32), pltpu.VMEM(two.block_shape, F32)],
        compiler_params=_params(1, 48),
    )(*args)


def _position():
    return lax.axis_index("x"), lax.axis_index("y"), lax.axis_index("c")


def _all_gather_small(xs, afters, name):
    n_in = 1 + len(afters)

    def body(*refs):
        x_ref, out_ref = refs[0], refs[n_in]
        send_sems, recv_sems, local_sem = refs[n_in + 1:]
        x, y, c = _position()
        my_rows = out_ref.at[4 * x + 2 * y + c]
        mine = pltpu.make_async_copy(x_ref, my_rows, local_sem)
        mine.start()
        copies = [pltpu.make_async_remote_copy(
            src_ref=x_ref, dst_ref=my_rows, send_sem=send_sems.at[k], recv_sem=recv_sems.at[k],
            device_id=(1 - x if k & 4 else x, 1 - y if k & 2 else y, 1 - c if k & 1 else c), device_id_type=MESH)
            for k in range(1, N_DEV)]
        for cp in copies:
            cp.start()
        for cp in copies:
            cp.wait_recv()
        for cp in copies:
            cp.wait_send()
        mine.wait()

    return _pallas_call(
        body, name=name, out_shape=_sds((N_DEV,) + xs.shape, xs.dtype),
        in_specs=[ANY] * n_in, out_specs=ANY,
        scratch_shapes=[pltpu.SemaphoreType.DMA((N_DEV,)), pltpu.SemaphoreType.DMA((N_DEV,)), pltpu.SemaphoreType.DMA],
    )(xs, *afters)


GATHER_CHUNKS = 4


def _all_gather_relay(xs, name):
    def body(x_ref, out_ref, send_sems, recv_sems, local_sem):
        x, y, c = _position()
        me, sibling = (x, y, c), (x, y, 1 - c)
        xn, yn, diag = (1 - x, y, c), (x, 1 - y, c), (1 - x, 1 - y, c)
        src_nb = (x + c * (1 - 2 * x), y + (1 - c) * (1 - 2 * y), c)
        dst_nb = (x + (1 - c) * (1 - 2 * x), y + c * (1 - 2 * y), c)

        def rows(dev, i):
            return out_ref.at[4 * dev[0] + 2 * dev[1] + dev[2], pl.ds(i * step, step)]

        def copy(k, i, block, to, own=False):
            return pltpu.make_async_remote_copy(
                src_ref=x_ref.at[pl.ds(i * step, step)] if own else rows(block, i), dst_ref=rows(block, i),
                send_sem=send_sems.at[k, i], recv_sem=recv_sems.at[k, i], device_id=to, device_id_type=MESH)

        mine = pltpu.make_async_copy(x_ref, out_ref.at[4 * x + 2 * y + c], local_sem)
        mine.start()
        sent = []
        for i in chunks:
            sent += [copy(1, i, me, xn, own=True), copy(2, i, me, yn, own=True), copy(0, i, me, sibling, own=True)]
        for cp in sent:
            cp.start()
        for i in chunks:
            copy(1, i, xn, me).wait_recv()
            copy(2, i, yn, me).wait_recv()
            onward = [copy(3, i, src_nb, dst_nb), copy(4, i, xn, sibling), copy(5, i, yn, sibling)]
            for cp in onward:
                cp.start()
            sent += onward
        for i in chunks:
            copy(3, i, diag, me).wait_recv()
            last = copy(6, i, diag, sibling)
            last.start()
            sent.append(last)
        for i in chunks:
            copy(0, i, sibling, me).wait_recv()
            for k, blk in ((4, (1 - x, y, 1 - c)), (5, (x, 1 - y, 1 - c)), (6, (1 - x, 1 - y, 1 - c))):
                copy(k, i, blk, me).wait_recv()
        for cp in sent:
            cp.wait_send()
        mine.wait()

    step = xs.shape[0] // GATHER_CHUNKS
    chunks = range(GATHER_CHUNKS)
    return _pallas_call(
        body, name=name, out_shape=_sds((N_DEV,) + xs.shape, xs.dtype),
        in_specs=[ANY], out_specs=ANY,
        scratch_shapes=[pltpu.SemaphoreType.DMA((7, GATHER_CHUNKS)), pltpu.SemaphoreType.DMA((7, GATHER_CHUNKS)),
                        pltpu.SemaphoreType.DMA],
    )(xs)


HBM_SPEC = pl.BlockSpec(memory_space=pltpu.HBM)
SEM_SPEC = pl.BlockSpec(memory_space=pltpu.SEMAPHORE)
EFFECT = pltpu.SideEffectType.DATAFLOW_SIDE_EFFECTING
def _plan_gather_own(src_refs, land_refs):
    x, y, c = _position()
    me = 4 * x + 2 * y + c
    peers = [(x, y, 1 - c), (1 - x, y, c), (x, 1 - y, c), (1 - x, 1 - y, c)]
    return [(land_refs[a].at[me], land_refs[a].at[me], (a, k), peer)
            for k, peer in enumerate(peers) for a in range(len(land_refs))]


def _plan_gather_pass(src_refs, land_refs):
    x, y, c = _position()
    chips = [(1 - x, y), (x, 1 - y), (1 - x, 1 - y)]
    return [(land_refs[a].at[4 * px + 2 * py + c], land_refs[a].at[4 * px + 2 * py + c], (a, j), (x, y, 1 - c))
            for j, (px, py) in enumerate(chips) for a in range(len(land_refs))]


def _plan_to_sibling(src_refs, land_refs):
    x, y, c = _position()
    return [(src_refs[a].at[2 * k + (1 - c)], land_refs[a].at[k], (a, k), (x, y, 1 - c))
            for k in range(4) for a in range(len(src_refs))]


def _plan_to_chips(src_refs, land_refs):
    x, y, c = _position()
    chips = [(1 - x, y), (x, 1 - y), (1 - x, 1 - y)]
    return [(src_refs[a].at[2 * px + py], land_refs[a].at[j], (a, j), (px, py, c))
            for j, (px, py) in enumerate(chips) for a in range(len(src_refs))]


def _split_start(srcs, lands, plan, n_sem, after, name):
    n_s, n_a = len(srcs), len(lands)
    n_b = n_s + n_a

    def body(*refs):
        src_refs, land_refs = refs[:n_s], refs[n_s:n_b]
        send_sems, recv_sems, token = refs[n_b + 1], refs[n_b + 2], refs[-1]
        for src, dst, (a, k), dev in plan(src_refs, land_refs):
            i = a * n_sem + k
            pltpu.make_async_remote_copy(src_ref=src, dst_ref=dst, send_sem=send_sems.at[i], recv_sem=recv_sems.at[i],
                                         device_id=dev, device_id_type=MESH).start()
        token[...] = jnp.zeros_like(token)

    bufs = list(srcs) + list(lands)
    res = pl.pallas_call(
        body, name=name,
        out_shape=(pltpu.SemaphoreType.DMA((n_a * n_sem,)), pltpu.SemaphoreType.DMA((n_a * n_sem,)),
                   *[pltpu.HBM(t.shape, t.dtype) for t in bufs], _plain((8, LANES), F32)),
        in_specs=[HBM_SPEC] * n_b + [ANY],
        out_specs=(SEM_SPEC, SEM_SPEC, *[HBM_SPEC] * n_b, pl.BlockSpec(memory_space=pltpu.VMEM)),
        input_output_aliases={i: 2 + i for i in range(n_b)},
        compiler_params=pltpu.CompilerParams(has_side_effects=EFFECT),
    )(*[pltpu.with_memory_space_constraint(t, pltpu.HBM) for t in bufs], after)
    return (res[0], res[1], res[2:2 + n_s], res[2 + n_s:2 + n_b]), res[-1]


def _split_wait(started, plan, after, name, first=0, n_sem=None):
    send_sems, recv_sems, srcs, lands = started
    n_s, n_a = len(srcs), len(lands)
    n_b = n_s + n_a
    n_sem = n_sem or send_sems.shape[0] // n_a

    def body(*refs):
        src_refs, land_refs = refs[:n_s], refs[n_s:n_b]
        s_sems, r_sems = refs[n_b], refs[n_b + 1]
        for src, dst, (a, k), dev in plan(src_refs, land_refs):
            i = (first + a) * n_sem + k
            cp = pltpu.make_async_remote_copy(src_ref=src, dst_ref=dst, send_sem=s_sems.at[i], recv_sem=r_sems.at[i],
                                              device_id=dev, device_id_type=MESH)
            cp.wait_send()
            cp.wait_recv()

    bufs = list(srcs) + list(lands)
    res = pl.pallas_call(
        body, name=name, out_shape=tuple(pltpu.HBM(t.shape, t.dtype) for t in bufs),
        in_specs=[HBM_SPEC] * n_b + [SEM_SPEC, SEM_SPEC, ANY],
        out_specs=tuple([HBM_SPEC] * n_b),
        input_output_aliases={i: i for i in range(n_b)},
        compiler_params=pltpu.CompilerParams(has_side_effects=EFFECT),
    )(*bufs, send_sems, recv_sems, after)
    return res[:n_s], res[n_s:]


SUBLANES = 8


def _row_tile(r):
    return max(t for t in range(SUBLANES, ROW_TILE + 1, SUBLANES) if r % t == 0)


def _rs_add_sibling(gps, recvs, ck_arr, name):
    n = len(gps)
    block = lambda k, ck: (k + ck[1] + 1) % 4

    def body(ck_ref, *refs):
        for g_ref, r_ref, pf_ref, pb_ref in zip(refs[:n], refs[n:2 * n], refs[2 * n::2], refs[2 * n + 1::2]):
            sm = g_ref[...].astype(F32) + r_ref[...].astype(F32)
            pf_ref[...] = sm
            pb_ref[...] = sm.astype(BF16)

    mine = lambda t: pl.BlockSpec((None,) + t.shape[1:], lambda k, ck: (2 * block(k, ck) + ck[0], 0, 0))
    one = lambda t: pl.BlockSpec((None,) + t.shape[1:], lambda k, ck: (block(k, ck), 0, 0))
    res = _pallas_call(
        body, name=name,
        grid_spec=pltpu.PrefetchScalarGridSpec(
            num_scalar_prefetch=1, grid=(4,),
            in_specs=[mine(t) for t in gps] + [one(t) for t in recvs],
            out_specs=[s for t in recvs for s in (pl.BlockSpec(t.shape[1:], lambda k, ck: (0, 0)), one(t))]),
        out_shape=[s for t in recvs for s in (_sds(t.shape[1:], F32), _sds(t.shape, BF16))],
        compiler_params=_params(1, 48),
    )(ck_arr, *gps, *recvs)
    return list(zip(res[0::2], res[1::2]))


def _adam_update(w, gv, m, v):
    nm = ADAM_B1 * m + (1.0 - ADAM_B1) * gv
    nv = ADAM_B2 * v + (1.0 - ADAM_B2) * (gv * gv)
    m_hat = nm / (1.0 - ADAM_B1 ** ADAM_STEP)
    v_hat = nv / (1.0 - ADAM_B2 ** ADAM_STEP)
    return -ADAM_LR * (m_hat / (jnp.sqrt(v_hat) + ADAM_EPS) + ADAM_WD * w), nm, nv


def _rs_finish_adamw(pf, recv, w, m, v, after, name):
    r, l = pf.shape
    tr = _row_tile(r)

    def body(_, p_ref, r_ref, w_ref, m_ref, v_ref, g_ref, d_ref, nm_ref, nv_ref):
        gv = ((p_ref[...] + r_ref[0].astype(F32)) + r_ref[1].astype(F32)) + r_ref[2].astype(F32)
        g_ref[...] = gv
        d_ref[...], nm_ref[...], nv_ref[...] = _adam_update(w_ref[...], gv, m_ref[...], v_ref[...])

    spec = pl.BlockSpec((tr, l), lambda i: (i, 0))
    return _pallas_call(
        body, name=name, grid=(r // tr,),
        in_specs=[ANY, spec, pl.BlockSpec((3, tr, l), lambda i: (0, i, 0)), spec, spec, spec], out_specs=[spec] * 4,
        out_shape=[_plain((r, l), F32)] * 4, compiler_params=_params(1, 32),
    )(after, pf, recv, w, m, v)


SMALL_ROWS = dict(norm_g=(0, 2), mem_norm_g=(2, 4), final_g=(4, 5), conv_w=(5, 8))
LOSS_ROWS = (8, 16)


def _sum_adamw_small(g, ck_arr, states):
    names = list(SMALL_ROWS)
    n_dev, n_rows, _ = g.shape

    def body(ck_ref, g_ref, gc_ref, *refs):
        ins, loss_ref, outs = refs[:3 * len(names)], refs[3 * len(names)], refs[3 * len(names) + 1:]

        def total(ref, lo, hi):
            acc = ref[0, lo:hi, :]
            for j in range(1, n_dev):
                acc = acc + ref[j, lo:hi, :]
            return acc

        loss_ref[...] = total(gc_ref, *LOSS_ROWS)
        for i, n in enumerate(names):
            gv = total(gc_ref if n == "conv_w" else g_ref, *SMALL_ROWS[n])
            w_ref, m_ref, v_ref = ins[3 * i:3 * i + 3]
            g_out, d_out, nm_out, nv_out = outs[4 * i:4 * i + 4]
            g_out[...] = gv
            d_out[...], nm_out[...], nv_out[...] = _adam_update(w_ref[...], gv, m_ref[...], v_ref[...])

    flat = [t for n in names for t in states[n]]
    mine = pl.BlockSpec((n_dev, n_rows, LANES), lambda i, ck: (0, 0, 2 * ck[1] + ck[0]))
    res = _pallas_call(
        body, name="sum_adamw_small",
        grid_spec=pltpu.PrefetchScalarGridSpec(
            num_scalar_prefetch=1, grid=(1,),
            in_specs=[_whole(g.shape), mine] + [_whole(t.shape) for t in flat],
            out_specs=[_whole((SUBLANES, LANES))] + [_whole(states[n][0].shape) for n in names for _ in range(4)]),
        out_shape=[_plain((SUBLANES, LANES), F32)] + [_plain(states[n][0].shape, F32) for n in names for _ in range(4)],
        compiler_params=_params(1, 32),
    )(ck_arr, g, g, *flat)
    return res[0], {n: tuple(res[1 + 4 * i:5 + 4 * i]) for i, n in enumerate(names)}


def _finish(name, pf, recv, w, m, v, after):
    if name in ("attn_w_in", "conv_w_in"):
        res = _rs_finish_adamw(pf, recv, w.T, m.T, v.T, after, "rs_finish_adamw_" + name)
        return tuple(t.T for t in res)
    return _rs_finish_adamw(pf, recv, w, m, v, after, "rs_finish_adamw_" + name)


def kernel(x, mem, positions, norm_g, mem_norm_g, w_mem_kv, attn_w_in, attn_w_out, conv_w_in, conv_w, conv_w_out, final_g, loss_target, m_norm_g, m_mem_norm_g, m_w_mem_kv, m_attn_w_in, m_attn_w_out, m_conv_w_in, m_conv_w, m_conv_w_out, m_final_g, v_norm_g, v_mem_norm_g, v_w_mem_kv, v_attn_w_in, v_attn_w_out, v_conv_w_in, v_conv_w, v_conv_w_out, v_final_g):
    px, py, pc = _position()
    me = 4 * px + 2 * py + pc
    ck_arr = jnp.stack([pc, 2 * px + py]).astype(jnp.int32)
    x, mem, pos, tgt = x[0], mem[0], positions[0], loss_target[0]

    wg_in0 = _all_gather_relay(attn_w_in[0].astype(BF16), "gather_w_in0")
    def gather_pass(weights, after, name, first):
        _, lands = _split_wait(weights, _plan_gather_own, after, name + "_wait", first, 4)
        return _split_start([], lands, _plan_gather_pass, 3, after, name + "_pass_start")

    late = [attn_w_out[0].astype(BF16), w_mem_kv.astype(BF16).reshape(-1, w_mem_kv.shape[2]),
            jnp.pad(conv_w[0], ((0, 5), (0, 0))), conv_w_in[0].astype(BF16), conv_w_out[0].astype(BF16)]
    lands = [lax.dynamic_update_slice(lax.empty((N_DEV,) + t.shape, t.dtype), t[None], (me, 0, 0)) for t in late]
    (send_sems, recv_sems, _, lands), token = _split_start([], lands, _plan_gather_own, 4, wg_in0, "gather_late_start")
    rest0, conv_ws = (send_sems, recv_sems, [], lands[:3]), (send_sems, recv_sems, [], lands[3:])

    tabs = _rope_tables(pos)
    g0, g1 = norm_g[0:1], norm_g[1:2]

    hn0, qs, ks, vs, tabs_v, qm0, z0 = _inproj_attn(x, g0, wg_in0, tabs, token)
    os_, ls_ = [], []
    for j, d in enumerate(DILATIONS):
        if j == 2:
            rest0, token = gather_pass(rest0, ls_[1], "gather_rest", 0)
        o, l = _attn_fwd(qs[j], ks[j], vs[j], d, token)
        os_.append(o)
        ls_.append(l)

    conv_ws, token = gather_pass(conv_ws, ls_[2], "gather_conv", 3)
    _, (wg_out0, wg_kv, cw_all) = _split_wait(rest0, _plan_gather_pass, token, "gather_rest_pass_wait")
    cw = cw_all[:, 0:3].transpose(1, 0, 2).reshape(3, -1)
    kv = _memkv_fwd(mem, mem_norm_g, wg_kv)
    y0, y0_t, mo0, h1 = _post_attn(os_, ls_, qm0, kv[0], z0, x, wg_out0)
    _, (wg_in1, wg_out1) = _split_wait(conv_ws, _plan_gather_pass, h1, "gather_conv_pass_wait")
    w_out1 = wg_out1.reshape(-1, wg_out1.shape[2])

    hn1, bg, cg, u, qm1, z1 = _inproj_conv(h1, g1, wg_in1)
    y1, y1_t, mo1, dh2, dh2b, loss_acc, d_final_g = _post_conv_loss(
        bg, cg, u, qm1, kv[1], z1, h1, w_out1, cw, final_g.reshape(1, -1), tgt)

    d_w_out1 = _wgrad_rows(y1_t, dh2b, "wgrad_out1")
    dz1, dbg, dconv, dqm1, dkv1 = _bwd_post_conv(dh2b, w_out1, bg, cg, u, z1, qm1, kv[1], mo1, cw)
    dcg, du, dcw = _bwd_conv(dconv, cg, u, cw)
    dh1, dg1, dh1b, dproj1_t = _dgrad_norm([(dbg, 1), (dcg, 1), (du, 1), (dqm1, 1), (dz1, 1)], wg_in1, h1, g1, dh2,
                                           dh2, True, "dgrad_norm_conv")
    d_w_in1 = _wgrad_shards_t(dproj1_t, hn1, wg_in1.shape[2], "wgrad_in1")

    d_w_out0 = _wgrad_cols(y0_t, dh1b, wg_out0.shape[2], "wgrad_out0")

    gw = GROUP_WIDTH
    ones = (jnp.arange(gw)[:, None] // HEAD_DIM == jnp.arange(gw)[None, :] // HEAD_DIM).astype(BF16)
    res = _bwd_post_attn(dh1b, wg_out0, z0, os_, ls_, qm0, kv[0], mo0, ones, dg1)
    dz0, dos, dls, dqm0, dkv0 = res[0], res[1:4], res[4:7], res[7], res[8]
    d_w_kv, d_mem_g = _memkv_bwd([dkv0, dkv1], wg_kv, mem, mem_norm_g)

    names1 = ["conv_w_in", "conv_w_out", "attn_w_out", "w_mem_kv"]
    grads1 = [d_w_in1, d_w_out1, d_w_out0, d_w_kv]
    started, token = _split_start(grads1, [lax.empty((4,) + g.shape[1:], g.dtype) for g in grads1],
                                  _plan_to_sibling, 4, d_mem_g, "rs1_sibling_start")

    dqs, dks, dvs = [], [], []
    for j, d in enumerate(DILATIONS):
        if j == 1:
            grads1, from_sibling = _split_wait(started, _plan_to_sibling, dqs[0][0], "rs1_sibling_wait")
            parts1 = _rs_add_sibling(grads1, from_sibling, ck_arr, "rs1_add_sibling")
            pbs1 = [pb for _, pb in parts1]
            started, token = _split_start(pbs1, [lax.empty((3,) + p.shape[1:], p.dtype) for p in pbs1],
                                          _plan_to_chips, 3, dg1, "rs1_chips_start")
        dq, dk, dv = _attn_bwd(qs[j], ks[j], vs[j], ls_[j], dos[j], dls[j], tabs_v[j], d, token)
        dqs.append((dq, d))
        dks.append((dk, d))
        dvs.append((dv, d))
    pieces0 = dqs + dks + dvs + [(dqm0, 1), (dz0, 1)]
    dproj0_t = _assemble_dproj_t(pieces0, N_DEV * wg_in0.shape[2], "assemble_dproj_attn")
    d_w_in0 = _wgrad_shards_t(dproj0_t, hn0, wg_in0.shape[2], "wgrad_in0")

    names0 = ["attn_w_in"]
    grads0 = [d_w_in0]
    started0, token0 = _split_start(grads0, [lax.empty((4,) + g.shape[1:], g.dtype) for g in grads0],
                                    _plan_to_sibling, 4, dg1, "rs0_sibling_start")
    _, from_chips1 = _split_wait(started, _plan_to_chips, token0, "rs1_chips_wait")
    shard = dict(attn_w_in=(attn_w_in[0], m_attn_w_in[0], v_attn_w_in[0]),
                 attn_w_out=(attn_w_out[0], m_attn_w_out[0], v_attn_w_out[0]),
                 conv_w_in=(conv_w_in[0], m_conv_w_in[0], v_conv_w_in[0]),
                 conv_w_out=(conv_w_out[0], m_conv_w_out[0], v_conv_w_out[0]),
                 w_mem_kv=tuple(t.reshape(-1, t.shape[2]) for t in (w_mem_kv, m_w_mem_kv, v_w_mem_kv)))
    finish1 = {n: (pf, r) for n, (pf, _), r in zip(names1, parts1, from_chips1)}
    big = {"conv_w_in": _finish("conv_w_in", *finish1["conv_w_in"], *shard["conv_w_in"], token0)}

    grads0, from_sibling = _split_wait(started0, _plan_to_sibling, big["conv_w_in"][1].T, "rs0_sibling_wait")
    parts0 = _rs_add_sibling(grads0, from_sibling, ck_arr, "rs0_add_sibling")
    pbs0 = [pb for _, pb in parts0]
    started0, token0 = _split_start(pbs0, [lax.empty((3,) + p.shape[1:], p.dtype) for p in pbs0],
                                    _plan_to_chips, 3, dg1, "rs0_chips_start")
    dx, dg0 = _dgrad_norm(pieces0, wg_in0, x, g0, dh1, token0, False, "dgrad_norm_attn")
    for n in names1[1:]:
        big[n] = _finish(n, *finish1[n], *shard[n], token0)

    small_part = jnp.concatenate([dg0, dg1, d_mem_g.reshape(2, -1), d_final_g, dcw[0:3]], axis=0)
    small_part = jnp.concatenate([small_part, jnp.broadcast_to(loss_acc[0, 0], small_part.shape)], axis=0)
    small_w = dict(norm_g=(norm_g, m_norm_g, v_norm_g), mem_norm_g=(mem_norm_g, m_mem_norm_g, v_mem_norm_g),
                   conv_w=(conv_w, m_conv_w, v_conv_w), final_g=(final_g, m_final_g, v_final_g))
    loss_tile, small_res = _sum_adamw_small(
        _all_gather_small(small_part, [big[n][1] for n in names1[1:]], "gather_small_grads"), ck_arr,
        {n: tuple(t.reshape(-1, t.shape[-1]) for t in wmv) for n, wmv in small_w.items()})
    loss = loss_tile[0, 0]
    for n, wmv in small_w.items():
        big[n] = tuple(t.reshape(wmv[0].shape) for t in small_res[n])

    _, from_chips0 = _split_wait(started0, _plan_to_chips, big["final_g"][1], "rs0_chips_wait")
    for n, (pf, _), r in zip(names0, parts0, from_chips0):
        big[n] = _finish(n, pf, r, *shard[n], big["final_g"][1])
    for n in ("attn_w_in", "attn_w_out", "conv_w_in", "conv_w_out"):
        big[n] = tuple(t[None] for t in big[n])
    big["w_mem_kv"] = tuple(t.reshape(w_mem_kv.shape) for t in big["w_mem_kv"])

    order = ["norm_g", "mem_norm_g", "w_mem_kv", "attn_w_in", "attn_w_out", "conv_w_in", "conv_w", "conv_w_out", "final_g"]
    return (loss, dx[None], *[big[n][0] for n in order], *[big[n][1] for n in order],
            *[big[n][2] for n in order], *[big[n][3] for n in order])
```

```python
import jax
import jax.numpy as jnp
from jax import lax
from jax.experimental import pallas as pl
from jax.experimental.pallas import tpu as pltpu

F32 = jnp.float32
BF16 = jnp.bfloat16

N_DEV = 8
D_MODEL = 1024
HEAD_DIM = 64
ROT_DIM = HEAD_DIM // 4
ROPE_THETA = 500000.0
DILATIONS = (1, 4, 16)
HEADS_PER_GROUP = 8
GROUP_WIDTH = HEADS_PER_GROUP * HEAD_DIM
BLOCK = 128
N_MEM = 256
MEM_HEADS = 4
MEM_WIDTH = MEM_HEADS * HEAD_DIM
CONV_WIDTH = D_MODEL
EPS = 1e-6
SCALE = HEAD_DIM ** -0.5
NEG = -1e30

ADAM_LR = 0.001
ADAM_B1 = 0.9
ADAM_B2 = 0.999
ADAM_EPS = 1e-08
ADAM_WD = 0.01
ADAM_STEP = 10

ROW_TILE = 256
WGRAD_SHARDS = 4
LANES = 128
MESH = pl.DeviceIdType.MESH
ANY = pl.BlockSpec(memory_space=pl.ANY)


def _pallas_call(body, **kw):
    call = pl.pallas_call(body, **kw)

    def run(*args):
        pinned = [pltpu.with_memory_space_constraint(a, pltpu.HBM) if jnp.issubdtype(a.dtype, jnp.floating) else a
                  for a in args]
        return call(*pinned)

    return run


def _dot(a, b):
    return lax.dot_general(a, b, (((1,), (0,)), ((), ())), preferred_element_type=F32)


def _dot_nt(a, b):
    return lax.dot_general(a, b, (((1,), (1,)), ((), ())), preferred_element_type=F32)


def _dot_tn(a, b):
    return lax.dot_general(a, b, (((0,), (0,)), ((), ())), preferred_element_type=F32)


def _params(n_grid, vmem_mb=48):
    return pltpu.CompilerParams(dimension_semantics=("arbitrary",) * n_grid, vmem_limit_bytes=vmem_mb << 20)


def _rows(width, tm=ROW_TILE):
    return pl.BlockSpec((tm, width), lambda i: (i, 0))


def _view_rows(width, d, tm=ROW_TILE):
    return pl.BlockSpec((tm // d, d * width), lambda i: (i, 0))


def _whole(shape):
    return pl.BlockSpec(shape, lambda *_: (0,) * len(shape))


def _resident(shape):
    return pl.BlockSpec(shape, lambda *_: (0,) * len(shape), pipeline_mode=pl.Buffered(1))


def _sds(shape, dtype):
    return pltpu.HBM(shape, dtype)


def _plain(shape, dtype):
    return jax.ShapeDtypeStruct(shape, dtype)


def _silu_parts(z):
    sg = jax.nn.sigmoid(z)
    return z * sg, sg * (1.0 + z * (1.0 - sg))


def _to_view(scr, val, out_ref, d):
    tm, w = val.shape
    if d == 1:
        out_ref[...] = val.astype(out_ref.dtype)
        return
    for cb in range(w // LANES):
        scr[cb] = val[:, cb * LANES:(cb + 1) * LANES]
    for r in range(d):
        for cb in range(w // LANES):
            lo = r * w + cb * LANES
            out_ref[:, lo:lo + LANES] = scr[cb, pl.ds(r, tm // d, stride=d), :].astype(out_ref.dtype)


def _from_view(scr, in_ref, d):
    if d == 1:
        return in_ref[...].astype(F32)
    nc, tm, _ = scr.shape
    w = nc * LANES
    for r in range(d):
        for cb in range(nc):
            lo = r * w + cb * LANES
            scr[cb, pl.ds(r, tm // d, stride=d), :] = in_ref[:, lo:lo + LANES].astype(F32)
    return jnp.concatenate([scr[cb] for cb in range(nc)], axis=1)


def _view_scratch(width, tm=ROW_TILE):
    return pltpu.VMEM((width // LANES, tm, LANES), F32)


def _rope_tables(pos):
    half = ROT_DIM // 2
    inv_freq = ROPE_THETA ** (-jnp.arange(half, dtype=F32) * (2.0 / ROT_DIM))
    ang = pos.astype(F32)[:, None] * inv_freq
    cos, sin = jnp.cos(ang), jnp.sin(ang)
    s = pos.shape[0]
    z8 = jnp.zeros((s, half), F32)
    rest = HEAD_DIM - ROT_DIM
    cosf = jnp.concatenate([cos, cos, jnp.ones((s, rest), F32)], axis=1)
    sa = jnp.concatenate([-sin, z8, jnp.zeros((s, rest), F32)], axis=1)
    sb = jnp.concatenate([z8, sin, jnp.zeros((s, rest), F32)], axis=1)
    return tuple(jnp.tile(t, (1, LANES // HEAD_DIM)) for t in (cosf, sa, sb))


def _rope_fwd(t, cv, sav, sbv):
    w = t.shape[1]
    return t * cv + pltpu.roll(t, w - ROT_DIM // 2, 1) * sav + pltpu.roll(t, ROT_DIM // 2, 1) * sbv


def _rope_bwd(g, cv, sav, sbv):
    w = g.shape[1]
    return g * cv + pltpu.roll(g * sav, ROT_DIM // 2, 1) + pltpu.roll(g * sbv, w - ROT_DIM // 2, 1)


def _joined_columns(wg_ref):
    assert wg_ref.shape[2] % LANES == 0
    return jnp.concatenate([wg_ref[j] for j in range(N_DEV)], axis=1)


def _join_once(wg_ref, w_scr):
    c = wg_ref.shape[2]

    @pl.when(pl.program_id(0) == 0)
    def _():
        for j in range(N_DEV):
            w_scr[:, j * c:(j + 1) * c] = wg_ref[j]


def _inproj_attn(x, g, wg, tabs, token):
    s, d_model = x.shape
    gw = GROUP_WIDTH
    n = N_DEV * wg.shape[2]
    nz = n - 9 * gw - MEM_WIDTH
    reps = gw // LANES
    tm = ROW_TILE

    def body(_, x_ref, g_ref, w_ref, c_ref, sa_ref, sb_ref, hn_ref, *rest):
        outs, (wj, scr, tscr) = rest[:-3], rest[-3:]
        q_refs, k_refs, v_refs, t_refs, qm_ref, z_ref = outs[0:3], outs[3:6], outs[6:9], outs[9:18], outs[18], outs[19]
        _join_once(w_ref, wj)
        xb = x_ref[...]
        r = lax.rsqrt(jnp.mean(xb * xb, axis=-1, keepdims=True) + EPS)
        hn = ((xb * r) * g_ref[...]).astype(BF16)
        hn_ref[...] = hn
        proj = lambda lo, hi: _dot(hn, wj[:, lo:hi])
        tab = (c_ref[...], sa_ref[...], sb_ref[...])
        cv, sav, sbv = [jnp.tile(t, (1, reps)) for t in tab]
        for j, d in enumerate(DILATIONS):
            tq = _rope_fwd(proj(j * gw, (j + 1) * gw), cv, sav, sbv)
            _to_view(scr, tq * SCALE, q_refs[j], d)
            tk = _rope_fwd(proj((3 + j) * gw, (4 + j) * gw), cv, sav, sbv)
            _to_view(scr, tk, k_refs[j], d)
            _to_view(scr, proj((6 + j) * gw, (7 + j) * gw), v_refs[j], d)
            for i in range(3):
                _to_view(tscr, tab[i], t_refs[3 * j + i], d)
        qm_ref[...] = proj(9 * gw, 9 * gw + MEM_WIDTH).astype(BF16)
        z_ref[...] = proj(9 * gw + MEM_WIDTH, n)

    views = [_sds((s // d, d * gw), BF16) for d in DILATIONS]
    tviews = [_sds((s // d, d * LANES), F32) for d in DILATIONS for _ in range(3)]
    out_shape = [_sds((s, d_model), BF16)] + views * 3 + tviews + [_sds((s, MEM_WIDTH), BF16), _sds((s, nz), F32)]
    vspecs = [_view_rows(gw, d, tm) for d in DILATIONS]
    tspecs = [_view_rows(LANES, d, tm) for d in DILATIONS for _ in range(3)]
    out_specs = [_rows(d_model, tm)] + vspecs * 3 + tspecs + [_rows(MEM_WIDTH, tm), _rows(nz, tm)]
    res = _pallas_call(
        body, name="inproj_attn", grid=(s // tm,), out_shape=out_shape,
        in_specs=[ANY, _rows(d_model, tm), _whole((1, d_model)), _resident(wg.shape)] + [_rows(LANES, tm)] * 3,
        out_specs=out_specs,
        scratch_shapes=[pltpu.VMEM((d_model, n), BF16), _view_scratch(gw, tm), _view_scratch(LANES, tm)],
        compiler_params=_params(1, 60),
    )(token, x, g, wg, *tabs)
    tabs_v = [res[10 + 3 * j:13 + 3 * j] for j in range(3)]
    return res[0], res[1:4], res[4:7], res[7:10], tabs_v, res[19], res[20]


def _band_mask(n_keys):
    qi = lax.broadcasted_iota(jnp.int32, (BLOCK, n_keys), 0)
    kj = lax.broadcasted_iota(jnp.int32, (BLOCK, n_keys), 1)
    if n_keys == BLOCK:
        return kj <= qi
    return jnp.logical_or(jnp.logical_and(kj < BLOCK, kj >= qi), jnp.logical_and(kj >= BLOCK, (kj - BLOCK) <= qi))


def _low_head_lanes():
    return lax.broadcasted_iota(jnp.int32, (1, LANES), 1) < HEAD_DIM


def _split_pair(t, low):
    zero = jnp.zeros_like(t)
    return jnp.where(low, t, zero), jnp.where(low, zero, t)


def _pair_specs(d, nb, w):
    if nb == 1:
        return pl.BlockSpec((BLOCK, 2 * w), lambda n: (0, n)), None
    half = nb // 2
    two = pl.BlockSpec((2 * BLOCK, w), lambda n: (n % half, n // half))
    before = pl.BlockSpec((BLOCK, w), lambda n: (jnp.maximum(2 * (n % half) - 1, 0), n // half))
    return two, before


def _head_tiles(w, col0):
    return ([slice(p * LANES, (p + 1) * LANES) for p in range(w // LANES)],
            [slice(col0 + p * LANES, col0 + (p + 1) * LANES) for p in range(w // LANES)])


def _attend_fwd(q_ref, o_ref, lse_ref, rows, col0, kk, vv):
    w = kk.shape[1]
    valid = _band_mask(kk.shape[0])
    low = _low_head_lanes()
    pairs, qcols = _head_tiles(w, col0)
    qs_ = [h for qc in qcols for h in _split_pair(q_ref[rows, qc], low)]
    k2s = [kk[:, pr] for pr in pairs for _ in range(2)]
    scs = [jnp.where(valid, _dot_nt(qh, k2), NEG) for qh, k2 in zip(qs_, k2s)]
    ms = [jnp.max(sc, axis=-1, keepdims=True) for sc in scs]
    ps = [jnp.exp(sc - m) for sc, m in zip(scs, ms)]
    ls = [jnp.sum(p, axis=-1, keepdims=True) for p in ps]
    pns = [(p * (1.0 / l)).astype(BF16) for p, l in zip(ps, ls)]
    for i, (pr, qc) in enumerate(zip(pairs, qcols)):
        v2 = vv[:, pr]
        a, b = 2 * i, 2 * i + 1
        o_ref[rows, qc] = jnp.where(low, _dot(pns[a], v2), _dot(pns[b], v2))
        lse_ref[rows, qc] = jnp.where(low, ms[a] + jnp.log(ls[a]), ms[b] + jnp.log(ls[b]))


TOP, BOTTOM = slice(0, BLOCK), slice(BLOCK, 2 * BLOCK)


def _attn_fwd(q, k, v, d, token):
    ln, dw = q.shape
    w = dw // d
    nb = ln // BLOCK
    two, before = _pair_specs(d, nb, w)

    def body_streams(_, q_ref, kc_ref, vc_ref, o_ref, lse_ref):
        for sb in range(2):
            cols = slice(sb * w, (sb + 1) * w)
            _attend_fwd(q_ref, o_ref, lse_ref, TOP, sb * w, kc_ref[:, cols], vc_ref[:, cols])

    def body_blocks(_, q_ref, kp_ref, kc_ref, vp_ref, vc_ref, o_ref, lse_ref):
        first = pl.program_id(0) % (nb // 2) == 0
        pl.when(first)(lambda: _attend_fwd(q_ref, o_ref, lse_ref, TOP, 0, kc_ref[TOP, :], vc_ref[TOP, :]))
        pl.when(jnp.logical_not(first))(lambda: _attend_fwd(
            q_ref, o_ref, lse_ref, TOP, 0, jnp.concatenate([kp_ref[...], kc_ref[TOP, :]], axis=0),
            jnp.concatenate([vp_ref[...], vc_ref[TOP, :]], axis=0)))
        _attend_fwd(q_ref, o_ref, lse_ref, BOTTOM, 0, kc_ref[...], vc_ref[...])

    if nb == 1:
        body, in_specs, args = body_streams, [ANY, two, two, two], (token, q, k, v)
    else:
        body, in_specs, args = body_blocks, [ANY, two, before, two, before, two], (token, q, k, k, v, v)
    return _pallas_call(
        body, name=f"attn_fwd_d{d}", grid=(d * nb // 2,), out_shape=[_sds((ln, dw), F32)] * 2,
        in_specs=in_specs, out_specs=[two, two], compiler_params=_params(1, 32),
    )(*args)


def _memkv_fwd(mem, g, w):
    n_layers = g.shape[0]
    rows = D_MODEL // N_DEV

    def body(mem_ref, g_ref, w_ref, *kv_refs):
        mb = mem_ref[...]
        r = lax.rsqrt(jnp.mean(mb * mb, axis=-1, keepdims=True) + EPS)
        mn = ((mb * r) * g_ref[...]).astype(BF16)
        kv = _dot(mn, w_ref[...].reshape(D_MODEL, 2 * MEM_WIDTH)).astype(BF16)
        for layer, kv_ref in enumerate(kv_refs):
            @pl.when(pl.program_id(0) == layer)
            def _(kv_ref=kv_ref):
                kv_ref[...] = kv

    return _pallas_call(
        body, name="memkv_fwd", grid=(n_layers,),
        out_shape=[_sds((N_MEM, 2 * MEM_WIDTH), BF16)] * n_layers,
        in_specs=[_whole(mem.shape), pl.BlockSpec((None, 1, D_MODEL), lambda l: (l, 0, 0)),
                  pl.BlockSpec((N_DEV, rows, 2 * MEM_WIDTH), lambda l: (0, l, 0))],
        out_specs=[_whole((N_MEM, 2 * MEM_WIDTH))] * n_layers,
        compiler_params=_params(1, 32),
    )(mem, g.reshape(n_layers, 1, D_MODEL), w)


def _mix_groups(os_, ls_):
    mx = jnp.maximum(jnp.maximum(ls_[0], ls_[1]), ls_[2])
    es = [jnp.exp(t - mx) for t in ls_]
    inv = 1.0 / (es[0] + es[1] + es[2])
    ws = [e * inv for e in es]
    mix = ws[0] * os_[0] + ws[1] * os_[1] + ws[2] * os_[2]
    return ws, mix


MEM_PAIRS = [slice(p * LANES, (p + 1) * LANES) for p in range(MEM_WIDTH // LANES)]


def _mem_probs(qhs, k2s):
    scs = [_dot_nt(qh, k2) * SCALE for qh, k2 in zip(qhs, k2s)]
    es = [jnp.exp(sc - jnp.max(sc, axis=-1, keepdims=True)) for sc in scs]
    return [e * (1.0 / jnp.sum(e, axis=-1, keepdims=True)) for e in es]


def _mem_attn_into(qm, kv_ref, mo_ref):
    low = _low_head_lanes()
    qhs = [h for pr in MEM_PAIRS for h in _split_pair(qm[:, pr], low)]
    k2s = [kv_ref[:, pr] for pr in MEM_PAIRS for _ in range(2)]
    ps = [p.astype(BF16) for p in _mem_probs(qhs, k2s)]
    for i, pr in enumerate(MEM_PAIRS):
        v2 = kv_ref[:, MEM_WIDTH + i * LANES:MEM_WIDTH + (i + 1) * LANES]
        mo_ref[:, pr] = jnp.where(low, _dot(ps[2 * i], v2), _dot(ps[2 * i + 1], v2))


def _mem_attn_bwd(qm, kv_ref, dmem, dqm_ref, dkv_ref):
    low = _low_head_lanes()
    dmb = dmem.astype(BF16)
    vps = [slice(MEM_WIDTH + i * LANES, MEM_WIDTH + (i + 1) * LANES) for i in range(len(MEM_PAIRS))]
    qhs = [h for pr in MEM_PAIRS for h in _split_pair(qm[:, pr], low)]
    dhs = [h for pr in MEM_PAIRS for h in _split_pair(dmb[:, pr], low)]
    k2s = [kv_ref[:, pr] for pr in MEM_PAIRS for _ in range(2)]
    v2s = [kv_ref[:, vp] for vp in vps for _ in range(2)]
    ps = _mem_probs(qhs, k2s)
    dps = [_dot_nt(dh, v2) for dh, v2 in zip(dhs, v2s)]
    dss = [(p * (dp - jnp.sum(dp * p, axis=-1, keepdims=True)) * SCALE).astype(BF16) for p, dp in zip(ps, dps)]
    pbs = [p.astype(BF16) for p in ps]
    for i, (pr, vp) in enumerate(zip(MEM_PAIRS, vps)):
        a, b = 2 * i, 2 * i + 1
        dqm_ref[:, pr] = jnp.where(low, _dot(dss[a], k2s[a]), _dot(dss[b], k2s[b])).astype(BF16)
        dkv_ref[:, pr] += _dot_tn(dss[a], qhs[a]) + _dot_tn(dss[b], qhs[b])
        dkv_ref[:, vp] += _dot_tn(pbs[a], dhs[a]) + _dot_tn(pbs[b], dhs[b])


def _post_attn(os_, ls_, qm, kv, z, x, wg_out):
    s, d_model = x.shape
    gw = GROUP_WIDTH
    nb = gw + MEM_WIDTH
    tm = ROW_TILE

    def body(o0, o1, o2, l0, l1, l2, qm_ref, kv_ref, z_ref, x_ref, w_ref, y_ref, yt_ref, mo_ref, h_ref, s0, s1):
        ov, lv = [], []
        for o_ref, l_ref, d in zip((o0, o1, o2), (l0, l1, l2), DILATIONS):
            ov.append(_from_view(s0, o_ref, d))
            lv.append(_from_view(s1, l_ref, d))
        _, mix = _mix_groups(ov, lv)
        _mem_attn_into(qm_ref[...], kv_ref, mo_ref)
        sz, _ = _silu_parts(z_ref[...])
        y_ref[:, :gw] = (mix * sz[:, :gw]).astype(BF16)
        y_ref[:, gw:] = (mo_ref[...] * sz[:, gw:]).astype(BF16)
        y = y_ref[...]
        yt_ref[...] = y.T
        h_ref[...] = x_ref[...] + _dot(y, _joined_columns(w_ref))

    vspecs = [_view_rows(gw, d) for d in DILATIONS]
    return _pallas_call(
        body, name="post_attn", grid=(s // tm,),
        out_shape=[_sds((s, nb), BF16), _sds((nb, s), BF16), _sds((s, MEM_WIDTH), F32), _sds((s, d_model), F32)],
        in_specs=vspecs * 2 + [_rows(MEM_WIDTH), _whole(kv.shape), _rows(nb), _rows(d_model), _whole(wg_out.shape)],
        out_specs=[_rows(nb), pl.BlockSpec((nb, tm), lambda i: (0, i)), _rows(MEM_WIDTH), _rows(d_model)],
        scratch_shapes=[_view_scratch(gw), _view_scratch(gw)],
        compiler_params=_params(1, 40),
    )(*os_, *ls_, qm, kv, z, x, wg_out)


def _inproj_conv(x, g, wg):
    s, d_model = x.shape
    c = CONV_WIDTH
    n = N_DEV * wg.shape[2]
    nz = n - 3 * c - MEM_WIDTH
    tm = ROW_TILE

    def body(x_ref, g_ref, w_ref, hn_ref, bg_ref, cg_ref, u_ref, qm_ref, z_ref, wj):
        _join_once(w_ref, wj)
        xb = x_ref[...]
        r = lax.rsqrt(jnp.mean(xb * xb, axis=-1, keepdims=True) + EPS)
        hn = ((xb * r) * g_ref[...]).astype(BF16)
        hn_ref[...] = hn
        bg_ref[...] = _dot(hn, wj[:, 0:c])
        cg_ref[...] = _dot(hn, wj[:, c:2 * c])
        u_ref[...] = _dot(hn, wj[:, 2 * c:3 * c])
        qm_ref[...] = _dot(hn, wj[:, 3 * c:3 * c + MEM_WIDTH]).astype(BF16)
        z_ref[...] = _dot(hn, wj[:, 3 * c + MEM_WIDTH:])

    return _pallas_call(
        body, name="inproj_conv", grid=(s // tm,),
        out_shape=[_sds((s, d_model), BF16)] + [_sds((s, c), F32)] * 3 + [_sds((s, MEM_WIDTH), BF16), _sds((s, nz), F32)],
        in_specs=[_rows(d_model), _whole((1, d_model)), _resident(wg.shape)],
        out_specs=[_rows(d_model)] + [_rows(c)] * 3 + [_rows(MEM_WIDTH), _rows(nz)],
        scratch_shapes=[pltpu.VMEM((d_model, n), BF16)],
        compiler_params=_params(1, 60),
    )(x, g, wg)


HALO = 8


def _halo_before(width, tm=ROW_TILE):
    return pl.BlockSpec((HALO, width), lambda i: (jnp.maximum(i * (tm // HALO) - 1, 0), 0))


def _halo_after(width, n_rows, tm=ROW_TILE):
    return pl.BlockSpec((HALO, width), lambda i: (jnp.minimum((i + 1) * (tm // HALO), n_rows // HALO - 1), 0))


def _conv_taps(cg_ref, u_ref, cgh_ref, uh_ref, i):
    a = cg_ref[...] * u_ref[...]
    ah = jnp.where(i > 0, cgh_ref[...] * uh_ref[...], 0.0)
    row = lax.broadcasted_iota(jnp.int32, a.shape, 0)
    a1 = jnp.where(row == 0, ah[HALO - 1:HALO], pltpu.roll(a, 1, 0))
    a2 = jnp.where(row == 0, ah[HALO - 2:HALO - 1], jnp.where(row == 1, ah[HALO - 1:HALO], pltpu.roll(a, 2, 0)))
    return a, a1, a2


def _post_conv_loss(bg, cg, u, qm, kv, z, h1, w_out, cw, gf, tgt):
    s, d = h1.shape
    c = CONV_WIDTH
    nb = c + MEM_WIDTH
    tm = ROW_TILE

    def body(bg_ref, cg_ref, u_ref, cgh_ref, uh_ref, qm_ref, kv_ref, z_ref, h_ref, w_ref, cw_ref, gf_ref, t_ref,
             y_ref, yt_ref, mo_ref, dh_ref, dhb_ref, loss_ref, dgf_ref):
        i = pl.program_id(0)
        a, a1, a2 = _conv_taps(cg_ref, u_ref, cgh_ref, uh_ref, i)
        conv = cw_ref[0:1, :] * a2 + cw_ref[1:2, :] * a1 + cw_ref[2:3, :] * a
        mix = bg_ref[...] * conv
        _mem_attn_into(qm_ref[...], kv_ref, mo_ref)
        sz, _ = _silu_parts(z_ref[...])
        y_ref[:, :c] = (mix * sz[:, :c]).astype(BF16)
        y_ref[:, c:] = (mo_ref[...] * sz[:, c:]).astype(BF16)
        y = y_ref[...]
        yt_ref[...] = y.T
        h2 = h_ref[...] + _dot(y, w_ref[...])
        r = lax.rsqrt(jnp.mean(h2 * h2, axis=-1, keepdims=True) + EPS)
        nh = h2 * r
        gfv = gf_ref[...]
        diff = nh * gfv - t_ref[...]
        dout = diff * (1.0 / d)
        dn = dout * gfv
        dh2 = r * dn - h2 * ((r * r * r) * jnp.mean(dn * h2, axis=-1, keepdims=True))
        dh_ref[...] = dh2
        dhb_ref[...] = dh2.astype(BF16)

        @pl.when(i == 0)
        def _():
            loss_ref[...] = jnp.zeros_like(loss_ref)
            dgf_ref[...] = jnp.zeros_like(dgf_ref)

        loss_ref[...] += 0.5 * jnp.sum(jnp.mean(diff * diff, axis=-1, keepdims=True))
        dgf_ref[...] += jnp.sum(dout * nh, axis=0, keepdims=True)

    return _pallas_call(
        body, name="post_conv_loss", grid=(s // tm,),
        out_shape=[_sds((s, nb), BF16), _sds((nb, s), BF16), _sds((s, MEM_WIDTH), F32), _sds((s, d), F32),
                   _sds((s, d), BF16), _plain((8, LANES), F32), _plain((1, d), F32)],
        in_specs=[_rows(c)] * 3 + [_halo_before(c)] * 2 + [_rows(MEM_WIDTH), _whole(kv.shape), _rows(nb), _rows(d),
                  _whole(w_out.shape), _whole(cw.shape), _whole((1, d)), _rows(d)],
        out_specs=[_rows(nb), pl.BlockSpec((nb, tm), lambda i: (0, i)), _rows(MEM_WIDTH), _rows(d), _rows(d),
                   _whole((8, LANES)), _whole((1, d))],
        compiler_params=_params(1, 48),
    )(bg, cg, u, cg, u, qm, kv, z, h1, w_out, cw, gf, tgt)


def _bwd_post_conv(dhb, w_out, bg, cg, u, z, qm, kv, mo, cw):
    s = dhb.shape[0]
    c = CONV_WIDTH
    nb = c + MEM_WIDTH

    def body(dh_ref, w_ref, bg_ref, cg_ref, u_ref, cgh_ref, uh_ref, z_ref, qm_ref, kv_ref, mo_ref, cw_ref,
             dz_ref, dbg_ref, dc_ref, dqm_ref, dkv_ref):
        i = pl.program_id(0)

        @pl.when(i == 0)
        def _():
            dkv_ref[...] = jnp.zeros_like(dkv_ref)

        dy = _dot_nt(dh_ref[...], w_ref[...])
        sz, dsz = _silu_parts(z_ref[...])
        a, a1, a2 = _conv_taps(cg_ref, u_ref, cgh_ref, uh_ref, i)
        conv = cw_ref[0:1, :] * a2 + cw_ref[1:2, :] * a1 + cw_ref[2:3, :] * a
        bgv = bg_ref[...]
        dz_ref[:, :c] = (dy[:, :c] * (bgv * conv) * dsz[:, :c]).astype(BF16)
        dz_ref[:, c:] = (dy[:, c:] * mo_ref[...] * dsz[:, c:]).astype(BF16)
        dbr = dy * sz
        dmix = dbr[:, :c]
        dbg_ref[...] = (dmix * conv).astype(BF16)
        dc_ref[...] = dmix * bgv
        _mem_attn_bwd(qm_ref[...], kv_ref, dbr[:, c:], dqm_ref, dkv_ref)

    return _pallas_call(
        body, name="bwd_post_conv", grid=(s // ROW_TILE,),
        out_shape=[_sds((s, nb), BF16), _sds((s, c), BF16), _sds((s, c), F32), _sds((s, MEM_WIDTH), BF16),
                   _plain(kv.shape, F32)],
        in_specs=[_rows(D_MODEL), _whole(w_out.shape)] + [_rows(c)] * 3 + [_halo_before(c)] * 2
                 + [_rows(nb), _rows(MEM_WIDTH), _whole(kv.shape), _rows(MEM_WIDTH), _whole(cw.shape)],
        out_specs=[_rows(nb), _rows(c), _rows(c), _rows(MEM_WIDTH), _whole(kv.shape)],
        compiler_params=_params(1, 48),
    )(dhb, w_out, bg, cg, u, cg, u, z, qm, kv, mo, cw)


def _bwd_conv(dconv, cg, u, cw):
    s, c = dconv.shape
    tm = ROW_TILE
    last = s // tm - 1

    def body(dc_ref, dcn_ref, cg_ref, u_ref, cgh_ref, uh_ref, cw_ref, dcg_ref, du_ref, dcw_ref):
        i = pl.program_id(0)

        @pl.when(i == 0)
        def _():
            dcw_ref[...] = jnp.zeros_like(dcw_ref)

        dc = dc_ref[...]
        dcn = jnp.where(i < last, dcn_ref[...], 0.0)
        row = lax.broadcasted_iota(jnp.int32, dc.shape, 0)
        d1 = jnp.where(row == tm - 1, dcn[0:1], pltpu.roll(dc, tm - 1, 0))
        d2 = jnp.where(row == tm - 1, dcn[1:2], jnp.where(row == tm - 2, dcn[0:1], pltpu.roll(dc, tm - 2, 0)))
        da = cw_ref[2:3, :] * dc + cw_ref[1:2, :] * d1 + cw_ref[0:1, :] * d2
        a, a1, a2 = _conv_taps(cg_ref, u_ref, cgh_ref, uh_ref, i)
        dcg_ref[...] = (da * u_ref[...]).astype(BF16)
        du_ref[...] = (da * cg_ref[...]).astype(BF16)
        dcw_ref[0:1, :] += jnp.sum(dc * a2, axis=0, keepdims=True)
        dcw_ref[1:2, :] += jnp.sum(dc * a1, axis=0, keepdims=True)
        dcw_ref[2:3, :] += jnp.sum(dc * a, axis=0, keepdims=True)

    return _pallas_call(
        body, name="bwd_conv", grid=(s // tm,),
        out_shape=[_sds((s, c), BF16), _sds((s, c), BF16), _plain((8, c), F32)],
        in_specs=[_rows(c), _halo_after(c, s), _rows(c), _rows(c), _halo_before(c), _halo_before(c), _whole(cw.shape)],
        out_specs=[_rows(c), _rows(c), _whole((8, c))], compiler_params=_params(1, 40),
    )(dconv, dconv, cg, u, cg, u, cw)


def _assemble(p_refs, pieces, widths, dp, scr):
    off = 0
    for p_ref, (_, d), wd in zip(p_refs, pieces, widths):
        if d == 1:
            dp[:, off:off + wd] = p_ref[...]
        else:
            dp[:, off:off + wd] = _from_view(scr, p_ref, d).astype(BF16)
        off += wd


def _dgrad_norm(pieces, wg, h, g, dres, token, onward, name):
    s, d_model = h.shape
    n = N_DEV * wg.shape[2]
    tm = ROW_TILE
    widths = [p.shape[1] // d for p, d in pieces]
    assert sum(widths) == n
    n_p = len(pieces)

    def body(_, *refs):
        p_refs = refs[:n_p]
        w_ref, h_ref, g_ref, dr_ref, dh_ref, dg_ref = refs[n_p:n_p + 6]
        dp, scr, wj = refs[-3:]
        _join_once(w_ref, wj)

        @pl.when(pl.program_id(0) == 0)
        def _():
            dg_ref[...] = jnp.zeros_like(dg_ref)

        _assemble(p_refs, pieces, widths, dp, scr)
        dhn = _dot_nt(dp[...], wj[...])
        hb = h_ref[...]
        r = lax.rsqrt(jnp.mean(hb * hb, axis=-1, keepdims=True) + EPS)
        dg_ref[...] += jnp.sum(dhn * (hb * r), axis=0, keepdims=True)
        dn = dhn * g_ref[...]
        dh = dr_ref[...] + r * dn - hb * ((r * r * r) * jnp.mean(dn * hb, axis=-1, keepdims=True))
        dh_ref[...] = dh
        if onward:
            dhb_ref, dpt_ref = refs[n_p + 6:n_p + 8]
            dhb_ref[...] = dh.astype(BF16)
            dpt_ref[...] = dp[...].T

    p_specs = [_view_rows(wd, d) for (_, d), wd in zip(pieces, widths)]
    out_shape = [_plain((s, d_model), F32), _plain((1, d_model), F32)]
    out_specs = [_rows(d_model), _whole((1, d_model))]
    if onward:
        out_shape += [_sds((s, d_model), BF16), _sds((n, s), BF16)]
        out_specs += [_rows(d_model), pl.BlockSpec((n, tm), lambda i: (0, i))]
    return _pallas_call(
        body, name=name, grid=(s // tm,), out_shape=out_shape,
        in_specs=[ANY] + p_specs + [_resident(wg.shape), _rows(d_model), _whole((1, d_model)), _rows(d_model)],
        out_specs=out_specs,
        scratch_shapes=[pltpu.VMEM((tm, n), BF16), _view_scratch(GROUP_WIDTH), pltpu.VMEM((d_model, n), BF16)],
        compiler_params=_params(1, 60),
    )(token, *[p for p, _ in pieces], wg, h, g, dres)


def _assemble_dproj_t(pieces, n, name):
    tm = ROW_TILE
    widths = [p.shape[1] // d for p, d in pieces]
    assert sum(widths) == n
    s = pieces[0][0].shape[0] * pieces[0][1]
    n_p = len(pieces)

    def body(*refs):
        p_refs, (dpt_ref, dp, scr) = refs[:n_p], refs[n_p:]
        _assemble(p_refs, pieces, widths, dp, scr)
        dpt_ref[...] = dp[...].T

    return _pallas_call(
        body, name=name, grid=(s // tm,), out_shape=_sds((n, s), BF16),
        in_specs=[_view_rows(wd, d) for (_, d), wd in zip(pieces, widths)],
        out_specs=pl.BlockSpec((n, tm), lambda i: (0, i)),
        scratch_shapes=[pltpu.VMEM((tm, n), BF16), _view_scratch(GROUP_WIDTH)],
        compiler_params=_params(1, 40),
    )(*[p for p, _ in pieces])


def _wgrad_shards_t(dp_t, h, c, name):
    n, s = dp_t.shape
    d_model = h.shape[1]
    per_step = 2

    def body(a_ref, b_ref, o_ref):
        o_ref[...] = _dot(a_ref[...], b_ref[...]).astype(BF16).reshape(per_step, c, d_model)

    return _pallas_call(
        body, name=name, grid=(N_DEV // per_step,), out_shape=_sds((N_DEV, c, d_model), BF16),
        in_specs=[pl.BlockSpec((per_step * c, s), lambda j: (j, 0)), _resident(h.shape)],
        out_specs=pl.BlockSpec((per_step, c, d_model), lambda j: (j, 0, 0)), compiler_params=_params(1, 40),
    )(dp_t, h)


def _wgrad_cols(a_t, b, c, name):
    m, s = a_t.shape
    assert c % LANES == 0

    def body(a_ref, b_ref, o_ref):
        wide = _dot(a_ref[...], b_ref[...]).astype(BF16)
        for j in range(WGRAD_SHARDS):
            o_ref[j] = wide[:, j * c:(j + 1) * c]

    return _pallas_call(
        body, name=name, grid=(N_DEV // WGRAD_SHARDS,), out_shape=_sds((N_DEV, m, c), BF16),
        in_specs=[_whole(a_t.shape), pl.BlockSpec((s, WGRAD_SHARDS * c), lambda j: (0, j))],
        out_specs=pl.BlockSpec((WGRAD_SHARDS, m, c), lambda j: (j, 0, 0)), compiler_params=_params(1, 40),
    )(a_t, b)


def _wgrad_rows(a_t, b, name):
    m, s = a_t.shape
    n = b.shape[1]
    mr = m // N_DEV

    def body(a_ref, b_ref, o_ref):
        o_ref[...] = _dot(a_ref[...], b_ref[...]).astype(BF16).reshape(WGRAD_SHARDS, mr, n)

    return _pallas_call(
        body, name=name, grid=(N_DEV // WGRAD_SHARDS,), out_shape=_sds((N_DEV, mr, n), BF16),
        in_specs=[pl.BlockSpec((WGRAD_SHARDS * mr, s), lambda j: (j, 0)), _whole(b.shape)],
        out_specs=pl.BlockSpec((WGRAD_SHARDS, mr, n), lambda j: (j, 0, 0)), compiler_params=_params(1, 40),
    )(a_t, b)


def _memkv_bwd(dkvs, w, mem, g):
    n_layers = g.shape[0]
    rows = D_MODEL // N_DEV

    def body(dkv0_ref, dkv1_ref, w_ref, mem_ref, g_ref, dw_ref, dg_ref):
        mb = mem_ref[...]
        r = lax.rsqrt(jnp.mean(mb * mb, axis=-1, keepdims=True) + EPS)
        nm = mb * r
        mn = (nm * g_ref[...]).astype(BF16)
        dkvb = jnp.where(pl.program_id(0) == 0, dkv0_ref[...], dkv1_ref[...]).astype(BF16)
        dw_ref[...] = _dot_tn(mn, dkvb).astype(BF16).reshape(N_DEV, rows, 2 * MEM_WIDTH)
        dmn = _dot_nt(dkvb, w_ref[...].reshape(D_MODEL, 2 * MEM_WIDTH))
        dg_ref[...] = jnp.sum(dmn * nm, axis=0, keepdims=True)

    lay = lambda *shape: pl.BlockSpec((None,) + shape, lambda l: (l, 0, 0))
    major = pl.BlockSpec((N_DEV, rows, 2 * MEM_WIDTH), lambda l: (0, l, 0))
    return _pallas_call(
        body, name="memkv_bwd", grid=(n_layers,),
        out_shape=[_sds((N_DEV, n_layers * rows, 2 * MEM_WIDTH), BF16), _plain((n_layers, 1, D_MODEL), F32)],
        in_specs=[_whole(dkvs[0].shape), _whole(dkvs[1].shape), major, _whole(mem.shape), lay(1, D_MODEL)],
        out_specs=[major, lay(1, D_MODEL)],
        compiler_params=_params(1, 32),
    )(*dkvs, w, mem, g.reshape(n_layers, 1, D_MODEL))


def _bwd_post_attn(dhb, wg_out, z, os_, ls_, qm, kv, mo, head_ones, token):
    s = dhb.shape[0]
    gw = GROUP_WIDTH
    nb = gw + MEM_WIDTH
    tm = ROW_TILE

    def body(_, dh_ref, w_ref, z_ref, o0, o1, o2, l0, l1, l2, qm_ref, kv_ref, mo_ref, bd_ref,
             dz_ref, do0, do1, do2, dl0, dl1, dl2, dqm_ref, dkv_ref, s0, s1):
        @pl.when(pl.program_id(0) == 0)
        def _():
            dkv_ref[...] = jnp.zeros_like(dkv_ref)

        dy = _dot_nt(dh_ref[...], _joined_columns(w_ref))
        ov, lv = [], []
        for o_ref, l_ref, d in zip((o0, o1, o2), (l0, l1, l2), DILATIONS):
            ov.append(_from_view(s0, o_ref, d))
            lv.append(_from_view(s1, l_ref, d))
        ws, mix = _mix_groups(ov, lv)
        sz, dsz = _silu_parts(z_ref[...])
        dz_ref[:, :gw] = (dy[:, :gw] * mix * dsz[:, :gw]).astype(BF16)
        dz_ref[:, gw:] = (dy[:, gw:] * mo_ref[...] * dsz[:, gw:]).astype(BF16)
        dbr = dy * sz
        dmix = dbr[:, :gw]
        t = dmix * mix
        th = t.astype(BF16)
        tl = (t - th.astype(F32)).astype(BF16)
        rs = _dot(th, bd_ref[...]) + _dot(tl, bd_ref[...])
        for wg_, do_ref, dl_ref, d in zip(ws, (do0, do1, do2), (dl0, dl1, dl2), DILATIONS):
            _to_view(s0, wg_ * dmix, do_ref, d)
            _to_view(s1, wg_ * rs, dl_ref, d)
        _mem_attn_bwd(qm_ref[...], kv_ref, dbr[:, gw:], dqm_ref, dkv_ref)

    vspecs = [_view_rows(gw, d) for d in DILATIONS]
    return _pallas_call(
        body, name="bwd_post_attn", grid=(s // tm,),
        out_shape=[_sds((s, nb), BF16)] + [_sds((s // d, d * gw), BF16) for d in DILATIONS]
                  + [_sds((s // d, d * gw), F32) for d in DILATIONS] + [_sds((s, MEM_WIDTH), BF16), _plain(kv.shape, F32)],
        in_specs=[ANY, _rows(D_MODEL), _whole(wg_out.shape), _rows(nb)] + vspecs * 2
                 + [_rows(MEM_WIDTH), _whole(kv.shape), _rows(MEM_WIDTH), _whole(head_ones.shape)],
        out_specs=[_rows(nb)] + vspecs * 2 + [_rows(MEM_WIDTH), _whole(kv.shape)],
        scratch_shapes=[_view_scratch(gw), _view_scratch(gw)],
        compiler_params=_params(1, 48),
    )(token, dhb, wg_out, z, *os_, *ls_, qm, kv, mo, head_ones)


def _attn_bwd(q, k, v, lse, do, dl, tabs, d, token):
    ln, dw = q.shape
    w = dw // d
    nb = ln // BLOCK
    reps = w // LANES
    two, before = _pair_specs(d, nb, w)
    two_t, _ = _pair_specs(d, nb, LANES)

    def attend(q_ref, l_ref, do_ref, dl_ref, dqs, acck, accv, rows, col0, kk, vv, acc_rows):
        valid = _band_mask(kk.shape[0])
        low = _low_head_lanes()
        pairs, qcols = _head_tiles(w, col0)
        cols = [slice(col0 + h * HEAD_DIM, col0 + h * HEAD_DIM + 1) for h in range(HEADS_PER_GROUP)]
        qhs = [h for qc in qcols for h in _split_pair(q_ref[rows, qc], low)]
        dobs = [h for qc in qcols for h in _split_pair(do_ref[rows, qc], low)]
        k2s = [kk[:, pr] for pr in pairs for _ in range(2)]
        v2s = [vv[:, pr] for pr in pairs for _ in range(2)]
        scs = [jnp.where(valid, _dot_nt(qh, k2), NEG) for qh, k2 in zip(qhs, k2s)]
        dps = [_dot_nt(dob, v2) for dob, v2 in zip(dobs, v2s)]
        ps = [jnp.exp(sc - l_ref[rows, col]) for sc, col in zip(scs, cols)]
        dss = [(p * (dp - dl_ref[rows, col])).astype(BF16) for p, dp, col in zip(ps, dps, cols)]
        pbs = [p.astype(BF16) for p in ps]
        for i, qc in enumerate(qcols):
            a, b = 2 * i, 2 * i + 1
            dqs[rows, qc] = jnp.where(low, _dot(dss[a], k2s[a]), _dot(dss[b], k2s[b])) * SCALE
            acck[acc_rows, qc] += _dot_tn(dss[a], qhs[a]) + _dot_tn(dss[b], qhs[b])
            accv[acc_rows, qc] += _dot_tn(pbs[a], dobs[a]) + _dot_tn(pbs[b], dobs[b])

    def body_streams(_, q_ref, kc_ref, vc_ref, l_ref, do_ref, dl_ref, c_ref, sa_ref, sb_ref,
                     dq_ref, dk_ref, dv_ref, acck, accv, dqs):
        acck[...] = jnp.zeros_like(acck)
        accv[...] = jnp.zeros_like(accv)
        for sb in range(2):
            cols = slice(sb * w, (sb + 1) * w)
            attend(q_ref, l_ref, do_ref, dl_ref, dqs, acck, accv, TOP, sb * w, kc_ref[:, cols], vc_ref[:, cols], TOP)
        tabs2 = [jnp.concatenate([jnp.tile(r[:, sb * LANES:(sb + 1) * LANES], (1, reps)) for sb in range(2)], axis=1)
                 for r in (c_ref, sa_ref, sb_ref)]
        dq_ref[...] = _rope_bwd(dqs[...], *tabs2).astype(BF16)
        dk_ref[...] = _rope_bwd(acck[...], *tabs2).astype(BF16)
        dv_ref[...] = accv[...].astype(BF16)

    def body_blocks(_, q_ref, kp_ref, kc_ref, vp_ref, vc_ref, l_ref, do_ref, dl_ref, cq, saq, sbq, ck, sak, sbk,
                    dq_ref, dk_ref, dv_ref, acck, accv, dqs):
        i = pl.program_id(0) % (nb // 2)

        @pl.when(i == 0)
        def _():
            acck[...] = jnp.zeros_like(acck)
            accv[...] = jnp.zeros_like(accv)

        refs = (q_ref, l_ref, do_ref, dl_ref, dqs, acck, accv)
        pl.when(i == 0)(lambda: attend(*refs, TOP, 0, kc_ref[TOP, :], vc_ref[TOP, :], TOP))
        pl.when(i != 0)(lambda: attend(
            *refs, TOP, 0, jnp.concatenate([kp_ref[...], kc_ref[TOP, :]], axis=0),
            jnp.concatenate([vp_ref[...], vc_ref[TOP, :]], axis=0),
            pl.ds(pl.multiple_of((2 * i - 1) * BLOCK, BLOCK), 2 * BLOCK)))
        attend(*refs, BOTTOM, 0, kc_ref[...], vc_ref[...], pl.ds(pl.multiple_of(2 * i * BLOCK, BLOCK), 2 * BLOCK))
        tq = [jnp.tile(r[...], (1, reps)) for r in (cq, saq, sbq)]
        dq_ref[...] = _rope_bwd(dqs[...], *tq).astype(BF16)

        @pl.when(i == nb // 2 - 1)
        def _():
            for r0 in range(0, nb * BLOCK, 2 * BLOCK):
                rows = slice(r0, r0 + 2 * BLOCK)
                tk = [jnp.tile(r[rows, :], (1, reps)) for r in (ck, sak, sbk)]
                dk_ref[rows, :] = _rope_bwd(acck[rows, :], *tk).astype(BF16)
                dv_ref[rows, :] = accv[rows, :].astype(BF16)

    if nb == 1:
        body = body_streams
        in_specs = [ANY] + [two] * 6 + [two_t] * 3
        args = (token, q, k, v, lse, do, dl, *tabs)
        out_specs = [two, two, two]
        acc_shape = (BLOCK, 2 * w)
    else:
        body = body_blocks
        stream = pl.BlockSpec((nb * BLOCK, w), lambda n: (0, n // (nb // 2)))
        stream_t = pl.BlockSpec((nb * BLOCK, LANES), lambda n: (0, n // (nb // 2)))
        in_specs = [ANY, two, before, two, before, two, two, two, two] + [two_t] * 3 + [stream_t] * 3
        args = (token, q, k, k, v, v, lse, do, dl, *tabs, *tabs)
        out_specs = [two, stream, stream]
        acc_shape = (nb * BLOCK, w)
    return _pallas_call(
        body, name=f"attn_bwd_d{d}", grid=(d * nb // 2,), out_shape=[_sds((ln, dw), BF16)] * 3,
        in_specs=in_specs, out_specs=out_specs,
        scratch_shapes=[pltpu.VMEM(acc_shape, F32), pltpu.VMEM(acc_shape, F32), pltpu.VMEM(two.block_shape, F32)],
        compiler_params=_params(1, 48),
    )(*args)


def _position():
    return lax.axis_index("x"), lax.axis_index("y"), lax.axis_index("c")


def _all_gather_small(xs, afters, name):
    n_in = 1 + len(afters)

    def body(*refs):
        x_ref, out_ref = refs[0], refs[n_in]
        send_sems, recv_sems, local_sem = refs[n_in + 1:]
        x, y, c = _position()
        my_rows = out_ref.at[4 * x + 2 * y + c]
        mine = pltpu.make_async_copy(x_ref, my_rows, local_sem)
        mine.start()
        copies = [pltpu.make_async_remote_copy(
            src_ref=x_ref, dst_ref=my_rows, send_sem=send_sems.at[k], recv_sem=recv_sems.at[k],
            device_id=(1 - x if k & 4 else x, 1 - y if k & 2 else y, 1 - c if k & 1 else c), device_id_type=MESH)
            for k in range(1, N_DEV)]
        for cp in copies:
            cp.start()
        for cp in copies:
            cp.wait_recv()
        for cp in copies:
            cp.wait_send()
        mine.wait()

    return _pallas_call(
        body, name=name, out_shape=_sds((N_DEV,) + xs.shape, xs.dtype),
        in_specs=[ANY] * n_in, out_specs=ANY,
        scratch_shapes=[pltpu.SemaphoreType.DMA((N_DEV,)), pltpu.SemaphoreType.DMA((N_DEV,)), pltpu.SemaphoreType.DMA],
    )(xs, *afters)


GATHER_CHUNKS = 8


def _all_gather_relay(xs, name):
    def body(x_ref, out_ref, send_sems, recv_sems, local_sem):
        x, y, c = _position()
        me, sibling = (x, y, c), (x, y, 1 - c)
        xn, yn, diag = (1 - x, y, c), (x, 1 - y, c), (1 - x, 1 - y, c)
        src_nb = (x + c * (1 - 2 * x), y + (1 - c) * (1 - 2 * y), c)
        dst_nb = (x + (1 - c) * (1 - 2 * x), y + c * (1 - 2 * y), c)

        def rows(dev, i):
            return out_ref.at[4 * dev[0] + 2 * dev[1] + dev[2], pl.ds(i * step, step)]

        def copy(k, i, block, to, own=False):
            return pltpu.make_async_remote_copy(
                src_ref=x_ref.at[pl.ds(i * step, step)] if own else rows(block, i), dst_ref=rows(block, i),
                send_sem=send_sems.at[k, i], recv_sem=recv_sems.at[k, i], device_id=to, device_id_type=MESH)

        mine = pltpu.make_async_copy(x_ref, out_ref.at[4 * x + 2 * y + c], local_sem)
        mine.start()
        sent = []
        for i in chunks:
            sent += [copy(1, i, me, xn, own=True), copy(2, i, me, yn, own=True), copy(0, i, me, sibling, own=True)]
        for cp in sent:
            cp.start()
        for i in chunks:
            copy(1, i, xn, me).wait_recv()
            copy(2, i, yn, me).wait_recv()
            onward = [copy(3, i, src_nb, dst_nb), copy(4, i, xn, sibling), copy(5, i, yn, sibling)]
            for cp in onward:
                cp.start()
            sent += onward
        for i in chunks:
            copy(3, i, diag, me).wait_recv()
            last = copy(6, i, diag, sibling)
            last.start()
            sent.append(last)
        for i in chunks:
            copy(0, i, sibling, me).wait_recv()
            for k, blk in ((4, (1 - x, y, 1 - c)), (5, (x, 1 - y, 1 - c)), (6, (1 - x, 1 - y, 1 - c))):
                copy(k, i, blk, me).wait_recv()
        for cp in sent:
            cp.wait_send()
        mine.wait()

    step = xs.shape[0] // GATHER_CHUNKS
    chunks = range(GATHER_CHUNKS)
    return _pallas_call(
        body, name=name, out_shape=_sds((N_DEV,) + xs.shape, xs.dtype),
        in_specs=[ANY], out_specs=ANY,
        scratch_shapes=[pltpu.SemaphoreType.DMA((7, GATHER_CHUNKS)), pltpu.SemaphoreType.DMA((7, GATHER_CHUNKS)),
                        pltpu.SemaphoreType.DMA],
    )(xs)


HBM_SPEC = pl.BlockSpec(memory_space=pltpu.HBM)
SEM_SPEC = pl.BlockSpec(memory_space=pltpu.SEMAPHORE)
EFFECT = pltpu.SideEffectType.DATAFLOW_SIDE_EFFECTING
def _plan_gather_own(src_refs, land_refs):
    x, y, c = _position()
    me = 4 * x + 2 * y + c
    peers = [(x, y, 1 - c), (1 - x, y, c), (x, 1 - y, c), (1 - x, 1 - y, c)]
    return [(land_refs[a].at[me], land_refs[a].at[me], (a, k), peer)
            for k, peer in enumerate(peers) for a in range(len(land_refs))]


def _plan_gather_pass(src_refs, land_refs):
    x, y, c = _position()
    chips = [(1 - x, y), (x, 1 - y), (1 - x, 1 - y)]
    return [(land_refs[a].at[4 * px + 2 * py + c], land_refs[a].at[4 * px + 2 * py + c], (a, j), (x, y, 1 - c))
            for j, (px, py) in enumerate(chips) for a in range(len(land_refs))]


def _plan_to_sibling(src_refs, land_refs):
    x, y, c = _position()
    return [(src_refs[a].at[2 * k + (1 - c)], land_refs[a].at[k], (a, k), (x, y, 1 - c))
            for k in range(4) for a in range(len(src_refs))]


CHIP_CHUNKS = 4
BF16_TILE_ROWS = 16


def _plan_to_chips(src_refs, land_refs):
    x, y, c = _position()
    chips = [(1 - x, y), (x, 1 - y), (1 - x, 1 - y)]
    plan = []
    for j, (px, py) in enumerate(chips):
        for a, src in enumerate(src_refs):
            r = src.shape[1]
            n = max(n for n in range(1, CHIP_CHUNKS + 1) if r % (n * BF16_TILE_ROWS) == 0)
            for i in range(n):
                part = pl.ds(i * (r // n), r // n)
                plan.append((src.at[2 * px + py, part], land_refs[a].at[j, part], (a, j * CHIP_CHUNKS + i), (px, py, c)))
    return plan


def _split_start(srcs, lands, plan, n_sem, after, name):
    n_s, n_a = len(srcs), len(lands)
    n_b = n_s + n_a

    def body(*refs):
        src_refs, land_refs = refs[:n_s], refs[n_s:n_b]
        send_sems, recv_sems, token = refs[n_b + 1], refs[n_b + 2], refs[-1]
        for src, dst, (a, k), dev in plan(src_refs, land_refs):
            i = a * n_sem + k
            pltpu.make_async_remote_copy(src_ref=src, dst_ref=dst, send_sem=send_sems.at[i], recv_sem=recv_sems.at[i],
                                         device_id=dev, device_id_type=MESH).start()
        token[...] = jnp.zeros_like(token)

    bufs = list(srcs) + list(lands)
    res = pl.pallas_call(
        body, name=name,
        out_shape=(pltpu.SemaphoreType.DMA((n_a * n_sem,)), pltpu.SemaphoreType.DMA((n_a * n_sem,)),
                   *[pltpu.HBM(t.shape, t.dtype) for t in bufs], _plain((8, LANES), F32)),
        in_specs=[HBM_SPEC] * n_b + [ANY],
        out_specs=(SEM_SPEC, SEM_SPEC, *[HBM_SPEC] * n_b, pl.BlockSpec(memory_space=pltpu.VMEM)),
        input_output_aliases={i: 2 + i for i in range(n_b)},
        compiler_params=pltpu.CompilerParams(has_side_effects=EFFECT),
    )(*[pltpu.with_memory_space_constraint(t, pltpu.HBM) for t in bufs], after)
    return (res[0], res[1], res[2:2 + n_s], res[2 + n_s:2 + n_b]), res[-1]


def _split_wait(started, plan, after, name, first=0, n_sem=None):
    send_sems, recv_sems, srcs, lands = started
    n_s, n_a = len(srcs), len(lands)
    n_b = n_s + n_a
    n_sem = n_sem or send_sems.shape[0] // n_a

    def body(*refs):
        src_refs, land_refs = refs[:n_s], refs[n_s:n_b]
        s_sems, r_sems = refs[n_b], refs[n_b + 1]
        for src, dst, (a, k), dev in plan(src_refs, land_refs):
            i = (first + a) * n_sem + k
            cp = pltpu.make_async_remote_copy(src_ref=src, dst_ref=dst, send_sem=s_sems.at[i], recv_sem=r_sems.at[i],
                                              device_id=dev, device_id_type=MESH)
            cp.wait_send()
            cp.wait_recv()

    bufs = list(srcs) + list(lands)
    res = pl.pallas_call(
        body, name=name, out_shape=tuple(pltpu.HBM(t.shape, t.dtype) for t in bufs),
        in_specs=[HBM_SPEC] * n_b + [SEM_SPEC, SEM_SPEC, ANY],
        out_specs=tuple([HBM_SPEC] * n_b),
        input_output_aliases={i: i for i in range(n_b)},
        compiler_params=pltpu.CompilerParams(has_side_effects=EFFECT),
    )(*bufs, send_sems, recv_sems, after)
    return res[:n_s], res[n_s:]


SUBLANES = 8


def _row_tile(r):
    return max(t for t in range(SUBLANES, ROW_TILE + 1, SUBLANES) if r % t == 0)


def _rs_add_sibling(gps, recvs, ck_arr, name):
    n = len(gps)
    block = lambda k, ck: (k + ck[1] + 1) % 4

    def body(ck_ref, *refs):
        for g_ref, r_ref, pf_ref, pb_ref in zip(refs[:n], refs[n:2 * n], refs[2 * n::2], refs[2 * n + 1::2]):
            sm = g_ref[...].astype(F32) + r_ref[...].astype(F32)
            pf_ref[...] = sm
            pb_ref[...] = sm.astype(BF16)

    mine = lambda t: pl.BlockSpec((None,) + t.shape[1:], lambda k, ck: (2 * block(k, ck) + ck[0], 0, 0))
    one = lambda t: pl.BlockSpec((None,) + t.shape[1:], lambda k, ck: (block(k, ck), 0, 0))
    res = _pallas_call(
        body, name=name,
        grid_spec=pltpu.PrefetchScalarGridSpec(
            num_scalar_prefetch=1, grid=(4,),
            in_specs=[mine(t) for t in gps] + [one(t) for t in recvs],
            out_specs=[s for t in recvs for s in (pl.BlockSpec(t.shape[1:], lambda k, ck: (0, 0)), one(t))]),
        out_shape=[s for t in recvs for s in (_sds(t.shape[1:], F32), _sds(t.shape, BF16))],
        compiler_params=_params(1, 48),
    )(ck_arr, *gps, *recvs)
    return list(zip(res[0::2], res[1::2]))


def _adam_update(w, gv, m, v):
    nm = ADAM_B1 * m + (1.0 - ADAM_B1) * gv
    nv = ADAM_B2 * v + (1.0 - ADAM_B2) * (gv * gv)
    m_hat = nm / (1.0 - ADAM_B1 ** ADAM_STEP)
    v_hat = nv / (1.0 - ADAM_B2 ** ADAM_STEP)
    return -ADAM_LR * (m_hat / (jnp.sqrt(v_hat) + ADAM_EPS) + ADAM_WD * w), nm, nv


def _rs_finish_adamw(pf, recv, w, m, v, after, name):
    r, l = pf.shape
    tr = _row_tile(r)

    def body(_, p_ref, r_ref, w_ref, m_ref, v_ref, g_ref, d_ref, nm_ref, nv_ref):
        gv = ((p_ref[...] + r_ref[0].astype(F32)) + r_ref[1].astype(F32)) + r_ref[2].astype(F32)
        g_ref[...] = gv
        d_ref[...], nm_ref[...], nv_ref[...] = _adam_update(w_ref[...], gv, m_ref[...], v_ref[...])

    spec = pl.BlockSpec((tr, l), lambda i: (i, 0))
    return _pallas_call(
        body, name=name, grid=(r // tr,),
        in_specs=[ANY, spec, pl.BlockSpec((3, tr, l), lambda i: (0, i, 0)), spec, spec, spec], out_specs=[spec] * 4,
        out_shape=[_plain((r, l), F32)] * 4, compiler_params=_params(1, 32),
    )(after, pf, recv, w, m, v)


SMALL_ROWS = dict(norm_g=(0, 2), mem_norm_g=(2, 4), final_g=(4, 5), conv_w=(5, 8))
LOSS_ROWS = (8, 16)


def _sum_adamw_small(g, ck_arr, states):
    names = list(SMALL_ROWS)
    n_dev, n_rows, _ = g.shape

    def body(ck_ref, g_ref, gc_ref, *refs):
        ins, loss_ref, outs = refs[:3 * len(names)], refs[3 * len(names)], refs[3 * len(names) + 1:]

        def total(ref, lo, hi):
            acc = ref[0, lo:hi, :]
            for j in range(1, n_dev):
                acc = acc + ref[j, lo:hi, :]
            return acc

        loss_ref[...] = total(gc_ref, *LOSS_ROWS)
        for i, n in enumerate(names):
            gv = total(gc_ref if n == "conv_w" else g_ref, *SMALL_ROWS[n])
            w_ref, m_ref, v_ref = ins[3 * i:3 * i + 3]
            g_out, d_out, nm_out, nv_out = outs[4 * i:4 * i + 4]
            g_out[...] = gv
            d_out[...], nm_out[...], nv_out[...] = _adam_update(w_ref[...], gv, m_ref[...], v_ref[...])

    flat = [t for n in names for t in states[n]]
    mine = pl.BlockSpec((n_dev, n_rows, LANES), lambda i, ck: (0, 0, 2 * ck[1] + ck[0]))
    res = _pallas_call(
        body, name="sum_adamw_small",
        grid_spec=pltpu.PrefetchScalarGridSpec(
            num_scalar_prefetch=1, grid=(1,),
            in_specs=[_whole(g.shape), mine] + [_whole(t.shape) for t in flat],
            out_specs=[_whole((SUBLANES, LANES))] + [_whole(states[n][0].shape) for n in names for _ in range(4)]),
        out_shape=[_plain((SUBLANES, LANES), F32)] + [_plain(states[n][0].shape, F32) for n in names for _ in range(4)],
        compiler_params=_params(1, 32),
    )(ck_arr, g, g, *flat)
    return res[0], {n: tuple(res[1 + 4 * i:5 + 4 * i]) for i, n in enumerate(names)}


def _finish(name, pf, recv, w, m, v, after):
    if name in ("attn_w_in", "conv_w_in"):
        res = _rs_finish_adamw(pf, recv, w.T, m.T, v.T, after, "rs_finish_adamw_" + name)
        return tuple(t.T for t in res)
    return _rs_finish_adamw(pf, recv, w, m, v, after, "rs_finish_adamw_" + name)


def kernel(x, mem, positions, norm_g, mem_norm_g, w_mem_kv, attn_w_in, attn_w_out, conv_w_in, conv_w, conv_w_out, final_g, loss_target, m_norm_g, m_mem_norm_g, m_w_mem_kv, m_attn_w_in, m_attn_w_out, m_conv_w_in, m_conv_w, m_conv_w_out, m_final_g, v_norm_g, v_mem_norm_g, v_w_mem_kv, v_attn_w_in, v_attn_w_out, v_conv_w_in, v_conv_w, v_conv_w_out, v_final_g):
    px, py, pc = _position()
    me = 4 * px + 2 * py + pc
    ck_arr = jnp.stack([pc, 2 * px + py]).astype(jnp.int32)
    x, mem, pos, tgt = x[0], mem[0], positions[0], loss_target[0]

    wg_in0 = _all_gather_relay(attn_w_in[0].astype(BF16), "gather_w_in0")
    def gather_pass(weights, after, name, first):
        _, lands = _split_wait(weights, _plan_gather_own, after, name + "_wait", first, 4)
        return _split_start([], lands, _plan_gather_pass, 3, after, name + "_pass_start")

    late = [attn_w_out[0].astype(BF16), w_mem_kv.astype(BF16).reshape(-1, w_mem_kv.shape[2]),
            jnp.pad(conv_w[0], ((0, 5), (0, 0))), conv_w_in[0].astype(BF16), conv_w_out[0].astype(BF16)]
    lands = [lax.dynamic_update_slice(lax.empty((N_DEV,) + t.shape, t.dtype), t[None], (me, 0, 0)) for t in late]
    (send_sems, recv_sems, _, lands), token = _split_start([], lands, _plan_gather_own, 4, wg_in0, "gather_late_start")
    rest0, conv_ws = (send_sems, recv_sems, [], lands[:3]), (send_sems, recv_sems, [], lands[3:])

    tabs = _rope_tables(pos)
    g0, g1 = norm_g[0:1], norm_g[1:2]

    hn0, qs, ks, vs, tabs_v, qm0, z0 = _inproj_attn(x, g0, wg_in0, tabs, token)
    os_, ls_ = [], []
    for j, d in enumerate(DILATIONS):
        if j == 2:
            rest0, token = gather_pass(rest0, ls_[1], "gather_rest", 0)
        o, l = _attn_fwd(qs[j], ks[j], vs[j], d, token)
        os_.append(o)
        ls_.append(l)

    conv_ws, token = gather_pass(conv_ws, ls_[2], "gather_conv", 3)
    _, (wg_out0, wg_kv, cw_all) = _split_wait(rest0, _plan_gather_pass, token, "gather_rest_pass_wait")
    cw = cw_all[:, 0:3].transpose(1, 0, 2).reshape(3, -1)
    kv = _memkv_fwd(mem, mem_norm_g, wg_kv)
    y0, y0_t, mo0, h1 = _post_attn(os_, ls_, qm0, kv[0], z0, x, wg_out0)
    _, (wg_in1, wg_out1) = _split_wait(conv_ws, _plan_gather_pass, h1, "gather_conv_pass_wait")
    w_out1 = wg_out1.reshape(-1, wg_out1.shape[2])

    hn1, bg, cg, u, qm1, z1 = _inproj_conv(h1, g1, wg_in1)
    y1, y1_t, mo1, dh2, dh2b, loss_acc, d_final_g = _post_conv_loss(
        bg, cg, u, qm1, kv[1], z1, h1, w_out1, cw, final_g.reshape(1, -1), tgt)

    d_w_out1 = _wgrad_rows(y1_t, dh2b, "wgrad_out1")
    dz1, dbg, dconv, dqm1, dkv1 = _bwd_post_conv(dh2b, w_out1, bg, cg, u, z1, qm1, kv[1], mo1, cw)
    dcg, du, dcw = _bwd_conv(dconv, cg, u, cw)
    dh1, dg1, dh1b, dproj1_t = _dgrad_norm([(dbg, 1), (dcg, 1), (du, 1), (dqm1, 1), (dz1, 1)], wg_in1, h1, g1, dh2,
                                           dh2, True, "dgrad_norm_conv")
    d_w_in1 = _wgrad_shards_t(dproj1_t, hn1, wg_in1.shape[2], "wgrad_in1")

    d_w_out0 = _wgrad_cols(y0_t, dh1b, wg_out0.shape[2], "wgrad_out0")

    gw = GROUP_WIDTH
    ones = (jnp.arange(gw)[:, None] // HEAD_DIM == jnp.arange(gw)[None, :] // HEAD_DIM).astype(BF16)
    res = _bwd_post_attn(dh1b, wg_out0, z0, os_, ls_, qm0, kv[0], mo0, ones, dg1)
    dz0, dos, dls, dqm0, dkv0 = res[0], res[1:4], res[4:7], res[7], res[8]
    d_w_kv, d_mem_g = _memkv_bwd([dkv0, dkv1], wg_kv, mem, mem_norm_g)

    names1 = ["conv_w_in", "conv_w_out", "attn_w_out", "w_mem_kv"]
    grads1 = [d_w_in1, d_w_out1, d_w_out0, d_w_kv]
    started, token = _split_start(grads1, [lax.empty((4,) + g.shape[1:], g.dtype) for g in grads1],
                                  _plan_to_sibling, 4, d_mem_g, "rs1_sibling_start")

    dqs, dks, dvs = [], [], []
    for j, d in enumerate(DILATIONS):
        if j == 1:
            grads1, from_sibling = _split_wait(started, _plan_to_sibling, dqs[0][0], "rs1_sibling_wait")
            parts1 = _rs_add_sibling(grads1, from_sibling, ck_arr, "rs1_add_sibling")
            pbs1 = [pb for _, pb in parts1]
            started, token = _split_start(pbs1, [lax.empty((3,) + p.shape[1:], p.dtype) for p in pbs1],
                                          _plan_to_chips, 3 * CHIP_CHUNKS, dg1,"rs1_chips_start")
        dq, dk, dv = _attn_bwd(qs[j], ks[j], vs[j], ls_[j], dos[j], dls[j], tabs_v[j], d, token)
        dqs.append((dq, d))
        dks.append((dk, d))
        dvs.append((dv, d))
    pieces0 = dqs + dks + dvs + [(dqm0, 1), (dz0, 1)]
    dproj0_t = _assemble_dproj_t(pieces0, N_DEV * wg_in0.shape[2], "assemble_dproj_attn")
    d_w_in0 = _wgrad_shards_t(dproj0_t, hn0, wg_in0.shape[2], "wgrad_in0")

    names0 = ["attn_w_in"]
    grads0 = [d_w_in0]
    started0, token0 = _split_start(grads0, [lax.empty((4,) + g.shape[1:], g.dtype) for g in grads0],
                                    _plan_to_sibling, 4, dg1, "rs0_sibling_start")
    _, from_chips1 = _split_wait(started, _plan_to_chips, token0, "rs1_chips_wait")
    shard = dict(attn_w_in=(attn_w_in[0], m_attn_w_in[0], v_attn_w_in[0]),
                 attn_w_out=(attn_w_out[0], m_attn_w_out[0], v_attn_w_out[0]),
                 conv_w_in=(conv_w_in[0], m_conv_w_in[0], v_conv_w_in[0]),
                 conv_w_out=(conv_w_out[0], m_conv_w_out[0], v_conv_w_out[0]),
                 w_mem_kv=tuple(t.reshape(-1, t.shape[2]) for t in (w_mem_kv, m_w_mem_kv, v_w_mem_kv)))
    finish1 = {n: (pf, r) for n, (pf, _), r in zip(names1, parts1, from_chips1)}
    big = {"conv_w_in": _finish("conv_w_in", *finish1["conv_w_in"], *shard["conv_w_in"], token0)}

    grads0, from_sibling = _split_wait(started0, _plan_to_sibling, big["conv_w_in"][1].T, "rs0_sibling_wait")
    parts0 = _rs_add_sibling(grads0, from_sibling, ck_arr, "rs0_add_sibling")
    pbs0 = [pb for _, pb in parts0]
    started0, token0 = _split_start(pbs0, [lax.empty((3,) + p.shape[1:], p.dtype) for p in pbs0],
                                    _plan_to_chips, 3 * CHIP_CHUNKS, dg1,"rs0_chips_start")
    dx, dg0 = _dgrad_norm(pieces0, wg_in0, x, g0, dh1, token0, False, "dgrad_norm_attn")
    for n in names1[1:]:
        big[n] = _finish(n, *finish1[n], *shard[n], token0)

    small_part = jnp.concatenate([dg0, dg1, d_mem_g.reshape(2, -1), d_final_g, dcw[0:3]], axis=0)
    small_part = jnp.concatenate([small_part, jnp.broadcast_to(loss_acc[0, 0], small_part.shape)], axis=0)
    small_w = dict(norm_g=(norm_g, m_norm_g, v_norm_g), mem_norm_g=(mem_norm_g, m_mem_norm_g, v_mem_norm_g),
                   conv_w=(conv_w, m_conv_w, v_conv_w), final_g=(final_g, m_final_g, v_final_g))
    loss_tile, small_res = _sum_adamw_small(
        _all_gather_small(small_part, [big[n][1] for n in names1[1:]], "gather_small_grads"), ck_arr,
        {n: tuple(t.reshape(-1, t.shape[-1]) for t in wmv) for n, wmv in small_w.items()})
    loss = loss_tile[0, 0]
    for n, wmv in small_w.items():
        big[n] = tuple(t.reshape(wmv[0].shape) for t in small_res[n])

    _, from_chips0 = _split_wait(started0, _plan_to_chips, big["final_g"][1], "rs0_chips_wait")
    for n, (pf, _), r in zip(names0, parts0, from_chips0):
        big[n] = _finish(n, pf, r, *shard[n], big["final_g"][1])
    for n in ("attn_w_in", "attn_w_out", "conv_w_in", "conv_w_out"):
        big[n] = tuple(t[None] for t in big[n])
    big["w_mem_kv"] = tuple(t.reshape(w_mem_kv.shape) for t in big["w_mem_kv"])

    order = ["norm_g", "mem_norm_g", "w_mem_kv", "attn_w_in", "attn_w_out", "conv_w_in", "conv_w", "conv_w_out", "final_g"]
    return (loss, dx[None], *[big[n][0] for n in order], *[big[n][1] for n in order],
            *[big[n][2] for n in order], *[big[n][3] for n in order])
```

```python
import jax
import jax.numpy as jnp
from jax import lax
from jax.experimental import pallas as pl
from jax.experimental.pallas import tpu as pltpu

F32 = jnp.float32
BF16 = jnp.bfloat16

N_DEV = 8
D_MODEL = 1024
HEAD_DIM = 64
ROT_DIM = HEAD_DIM // 4
ROPE_THETA = 500000.0
DILATIONS = (1, 4, 16)
HEADS_PER_GROUP = 8
GROUP_WIDTH = HEADS_PER_GROUP * HEAD_DIM
BLOCK = 128
N_MEM = 256
MEM_HEADS = 4
MEM_WIDTH = MEM_HEADS * HEAD_DIM
CONV_WIDTH = D_MODEL
EPS = 1e-6
SCALE = HEAD_DIM ** -0.5
NEG = -1e30

ADAM_LR = 0.001
ADAM_B1 = 0.9
ADAM_B2 = 0.999
ADAM_EPS = 1e-08
ADAM_WD = 0.01
ADAM_STEP = 10

ROW_TILE = 256
WGRAD_SHARDS = 4
LANES = 128
MESH = pl.DeviceIdType.MESH
ANY = pl.BlockSpec(memory_space=pl.ANY)


def _pallas_call(body, **kw):
    call = pl.pallas_call(body, **kw)

    def run(*args):
        pinned = [pltpu.with_memory_space_constraint(a, pltpu.HBM) if jnp.issubdtype(a.dtype, jnp.floating) else a
                  for a in args]
        return call(*pinned)

    return run


def _dot(a, b):
    return lax.dot_general(a, b, (((1,), (0,)), ((), ())), preferred_element_type=F32)


def _dot_nt(a, b):
    return lax.dot_general(a, b, (((1,), (1,)), ((), ())), preferred_element_type=F32)


def _dot_tn(a, b):
    return lax.dot_general(a, b, (((0,), (0,)), ((), ())), preferred_element_type=F32)


def _params(n_grid, vmem_mb=48):
    return pltpu.CompilerParams(dimension_semantics=("arbitrary",) * n_grid, vmem_limit_bytes=vmem_mb << 20)


def _rows(width, tm=ROW_TILE):
    return pl.BlockSpec((tm, width), lambda i: (i, 0))


def _view_rows(width, d, tm=ROW_TILE):
    return pl.BlockSpec((tm // d, d * width), lambda i: (i, 0))


def _whole(shape):
    return pl.BlockSpec(shape, lambda *_: (0,) * len(shape))


def _resident(shape):
    return pl.BlockSpec(shape, lambda *_: (0,) * len(shape), pipeline_mode=pl.Buffered(1))


def _sds(shape, dtype):
    return pltpu.HBM(shape, dtype)


def _plain(shape, dtype):
    return jax.ShapeDtypeStruct(shape, dtype)


def _silu_parts(z):
    sg = jax.nn.sigmoid(z)
    return z * sg, sg * (1.0 + z * (1.0 - sg))


def _to_view(scr, val, out_ref, d):
    tm, w = val.shape
    if d == 1:
        out_ref[...] = val.astype(out_ref.dtype)
        return
    for cb in range(w // LANES):
        scr[cb] = val[:, cb * LANES:(cb + 1) * LANES]
    for r in range(d):
        for cb in range(w // LANES):
            lo = r * w + cb * LANES
            out_ref[:, lo:lo + LANES] = scr[cb, pl.ds(r, tm // d, stride=d), :].astype(out_ref.dtype)


def _from_view(scr, in_ref, d):
    if d == 1:
        return in_ref[...].astype(F32)
    nc, tm, _ = scr.shape
    w = nc * LANES
    for r in range(d):
        for cb in range(nc):
            lo = r * w + cb * LANES
            scr[cb, pl.ds(r, tm // d, stride=d), :] = in_ref[:, lo:lo + LANES].astype(F32)
    return jnp.concatenate([scr[cb] for cb in range(nc)], axis=1)


def _view_scratch(width, tm=ROW_TILE):
    return pltpu.VMEM((width // LANES, tm, LANES), F32)


def _rope_tables(pos):
    half = ROT_DIM // 2
    inv_freq = ROPE_THETA ** (-jnp.arange(half, dtype=F32) * (2.0 / ROT_DIM))
    ang = pos.astype(F32)[:, None] * inv_freq
    cos, sin = jnp.cos(ang), jnp.sin(ang)
    s = pos.shape[0]
    z8 = jnp.zeros((s, half), F32)
    rest = HEAD_DIM - ROT_DIM
    cosf = jnp.concatenate([cos, cos, jnp.ones((s, rest), F32)], axis=1)
    sa = jnp.concatenate([-sin, z8, jnp.zeros((s, rest), F32)], axis=1)
    sb = jnp.concatenate([z8, sin, jnp.zeros((s, rest), F32)], axis=1)
    return tuple(jnp.tile(t, (1, LANES // HEAD_DIM)) for t in (cosf, sa, sb))


def _rope_fwd(t, cv, sav, sbv):
    w = t.shape[1]
    return t * cv + pltpu.roll(t, w - ROT_DIM // 2, 1) * sav + pltpu.roll(t, ROT_DIM // 2, 1) * sbv


def _rope_bwd(g, cv, sav, sbv):
    w = g.shape[1]
    return g * cv + pltpu.roll(g * sav, ROT_DIM // 2, 1) + pltpu.roll(g * sbv, w - ROT_DIM // 2, 1)


def _joined_columns(wg_ref):
    assert wg_ref.shape[2] % LANES == 0
    return jnp.concatenate([wg_ref[j] for j in range(N_DEV)], axis=1)


def _join_once(wg_ref, w_scr):
    c = wg_ref.shape[2]

    @pl.when(pl.program_id(0) == 0)
    def _():
        for j in range(N_DEV):
            w_scr[:, j * c:(j + 1) * c] = wg_ref[j]


def _inproj_attn(x, g, wg, tabs, token):
    s, d_model = x.shape
    gw = GROUP_WIDTH
    n = N_DEV * wg.shape[2]
    nz = n - 9 * gw - MEM_WIDTH
    reps = gw // LANES
    tm = ROW_TILE

    def body(_, x_ref, g_ref, w_ref, c_ref, sa_ref, sb_ref, hn_ref, *rest):
        outs, (wj, scr, tscr) = rest[:-3], rest[-3:]
        q_refs, k_refs, v_refs, t_refs, qm_ref, z_ref = outs[0:3], outs[3:6], outs[6:9], outs[9:18], outs[18], outs[19]
        _join_once(w_ref, wj)
        xb = x_ref[...]
        r = lax.rsqrt(jnp.mean(xb * xb, axis=-1, keepdims=True) + EPS)
        hn = ((xb * r) * g_ref[...]).astype(BF16)
        hn_ref[...] = hn
        proj = lambda lo, hi: _dot(hn, wj[:, lo:hi])
        tab = (c_ref[...], sa_ref[...], sb_ref[...])
        cv, sav, sbv = [jnp.tile(t, (1, reps)) for t in tab]
        for j, d in enumerate(DILATIONS):
            tq = _rope_fwd(proj(j * gw, (j + 1) * gw), cv, sav, sbv)
            _to_view(scr, tq * SCALE, q_refs[j], d)
            tk = _rope_fwd(proj((3 + j) * gw, (4 + j) * gw), cv, sav, sbv)
            _to_view(scr, tk, k_refs[j], d)
            _to_view(scr, proj((6 + j) * gw, (7 + j) * gw), v_refs[j], d)
            for i in range(3):
                _to_view(tscr, tab[i], t_refs[3 * j + i], d)
        qm_ref[...] = proj(9 * gw, 9 * gw + MEM_WIDTH).astype(BF16)
        z_ref[...] = proj(9 * gw + MEM_WIDTH, n)

    views = [_sds((s // d, d * gw), BF16) for d in DILATIONS]
    tviews = [_sds((s // d, d * LANES), F32) for d in DILATIONS for _ in range(3)]
    out_shape = [_sds((s, d_model), BF16)] + views * 3 + tviews + [_sds((s, MEM_WIDTH), BF16), _sds((s, nz), F32)]
    vspecs = [_view_rows(gw, d, tm) for d in DILATIONS]
    tspecs = [_view_rows(LANES, d, tm) for d in DILATIONS for _ in range(3)]
    out_specs = [_rows(d_model, tm)] + vspecs * 3 + tspecs + [_rows(MEM_WIDTH, tm), _rows(nz, tm)]
    res = _pallas_call(
        body, name="inproj_attn", grid=(s // tm,), out_shape=out_shape,
        in_specs=[ANY, _rows(d_model, tm), _whole((1, d_model)), _resident(wg.shape)] + [_rows(LANES, tm)] * 3,
        out_specs=out_specs,
        scratch_shapes=[pltpu.VMEM((d_model, n), BF16), _view_scratch(gw, tm), _view_scratch(LANES, tm)],
        compiler_params=_params(1, 60),
    )(token, x, g, wg, *tabs)
    tabs_v = [res[10 + 3 * j:13 + 3 * j] for j in range(3)]
    return res[0], res[1:4], res[4:7], res[7:10], tabs_v, res[19], res[20]


def _band_mask(n_keys):
    qi = lax.broadcasted_iota(jnp.int32, (BLOCK, n_keys), 0)
    kj = lax.broadcasted_iota(jnp.int32, (BLOCK, n_keys), 1)
    if n_keys == BLOCK:
        return kj <= qi
    return jnp.logical_or(jnp.logical_and(kj < BLOCK, kj >= qi), jnp.logical_and(kj >= BLOCK, (kj - BLOCK) <= qi))


def _low_head_lanes():
    return lax.broadcasted_iota(jnp.int32, (1, LANES), 1) < HEAD_DIM


def _split_pair(t, low):
    zero = jnp.zeros_like(t)
    return jnp.where(low, t, zero), jnp.where(low, zero, t)


def _pair_specs(d, nb, w):
    if nb == 1:
        return pl.BlockSpec((BLOCK, 2 * w), lambda n: (0, n)), None
    half = nb // 2
    two = pl.BlockSpec((2 * BLOCK, w), lambda n: (n % half, n // half))
    before = pl.BlockSpec((BLOCK, w), lambda n: (jnp.maximum(2 * (n % half) - 1, 0), n // half))
    return two, before


def _head_tiles(w, col0):
    return ([slice(p * LANES, (p + 1) * LANES) for p in range(w // LANES)],
            [slice(col0 + p * LANES, col0 + (p + 1) * LANES) for p in range(w // LANES)])


def _attend_fwd(q_ref, o_ref, lse_ref, rows, col0, kk, vv):
    w = kk.shape[1]
    valid = _band_mask(kk.shape[0])
    low = _low_head_lanes()
    pairs, qcols = _head_tiles(w, col0)
    qs_ = [h for qc in qcols for h in _split_pair(q_ref[rows, qc], low)]
    k2s = [kk[:, pr] for pr in pairs for _ in range(2)]
    scs = [jnp.where(valid, _dot_nt(qh, k2), NEG) for qh, k2 in zip(qs_, k2s)]
    ms = [jnp.max(sc, axis=-1, keepdims=True) for sc in scs]
    ps = [jnp.exp(sc - m) for sc, m in zip(scs, ms)]
    ls = [jnp.sum(p, axis=-1, keepdims=True) for p in ps]
    pns = [(p * (1.0 / l)).astype(BF16) for p, l in zip(ps, ls)]
    for i, (pr, qc) in enumerate(zip(pairs, qcols)):
        v2 = vv[:, pr]
        a, b = 2 * i, 2 * i + 1
        o_ref[rows, qc] = jnp.where(low, _dot(pns[a], v2), _dot(pns[b], v2))
        lse_ref[rows, qc] = jnp.where(low, ms[a] + jnp.log(ls[a]), ms[b] + jnp.log(ls[b]))


TOP, BOTTOM = slice(0, BLOCK), slice(BLOCK, 2 * BLOCK)


def _attn_fwd(q, k, v, d, token):
    ln, dw = q.shape
    w = dw // d
    nb = ln // BLOCK
    two, before = _pair_specs(d, nb, w)

    def body_streams(_, q_ref, kc_ref, vc_ref, o_ref, lse_ref):
        for sb in range(2):
            cols = slice(sb * w, (sb + 1) * w)
            _attend_fwd(q_ref, o_ref, lse_ref, TOP, sb * w, kc_ref[:, cols], vc_ref[:, cols])

    def body_blocks(_, q_ref, kp_ref, kc_ref, vp_ref, vc_ref, o_ref, lse_ref):
        first = pl.program_id(0) % (nb // 2) == 0
        pl.when(first)(lambda: _attend_fwd(q_ref, o_ref, lse_ref, TOP, 0, kc_ref[TOP, :], vc_ref[TOP, :]))
        pl.when(jnp.logical_not(first))(lambda: _attend_fwd(
            q_ref, o_ref, lse_ref, TOP, 0, jnp.concatenate([kp_ref[...], kc_ref[TOP, :]], axis=0),
            jnp.concatenate([vp_ref[...], vc_ref[TOP, :]], axis=0)))
        _attend_fwd(q_ref, o_ref, lse_ref, BOTTOM, 0, kc_ref[...], vc_ref[...])

    if nb == 1:
        body, in_specs, args = body_streams, [ANY, two, two, two], (token, q, k, v)
    else:
        body, in_specs, args = body_blocks, [ANY, two, before, two, before, two], (token, q, k, k, v, v)
    return _pallas_call(
        body, name=f"attn_fwd_d{d}", grid=(d * nb // 2,), out_shape=[_sds((ln, dw), F32)] * 2,
        in_specs=in_specs, out_specs=[two, two], compiler_params=_params(1, 32),
    )(*args)


def _memkv_fwd(mem, g, w):
    n_layers = g.shape[0]
    rows = D_MODEL // N_DEV

    def body(mem_ref, g_ref, w_ref, *kv_refs):
        mb = mem_ref[...]
        r = lax.rsqrt(jnp.mean(mb * mb, axis=-1, keepdims=True) + EPS)
        mn = ((mb * r) * g_ref[...]).astype(BF16)
        kv = _dot(mn, w_ref[...].reshape(D_MODEL, 2 * MEM_WIDTH)).astype(BF16)
        for layer, kv_ref in enumerate(kv_refs):
            @pl.when(pl.program_id(0) == layer)
            def _(kv_ref=kv_ref):
                kv_ref[...] = kv

    return _pallas_call(
        body, name="memkv_fwd", grid=(n_layers,),
        out_shape=[_sds((N_MEM, 2 * MEM_WIDTH), BF16)] * n_layers,
        in_specs=[_whole(mem.shape), pl.BlockSpec((None, 1, D_MODEL), lambda l: (l, 0, 0)),
                  pl.BlockSpec((N_DEV, rows, 2 * MEM_WIDTH), lambda l: (0, l, 0))],
        out_specs=[_whole((N_MEM, 2 * MEM_WIDTH))] * n_layers,
        compiler_params=_params(1, 32),
    )(mem, g.reshape(n_layers, 1, D_MODEL), w)


def _mix_groups(os_, ls_):
    mx = jnp.maximum(jnp.maximum(ls_[0], ls_[1]), ls_[2])
    es = [jnp.exp(t - mx) for t in ls_]
    inv = 1.0 / (es[0] + es[1] + es[2])
    ws = [e * inv for e in es]
    mix = ws[0] * os_[0] + ws[1] * os_[1] + ws[2] * os_[2]
    return ws, mix


MEM_PAIRS = [slice(p * LANES, (p + 1) * LANES) for p in range(MEM_WIDTH // LANES)]


def _mem_probs(qhs, k2s):
    scs = [_dot_nt(qh, k2) * SCALE for qh, k2 in zip(qhs, k2s)]
    es = [jnp.exp(sc - jnp.max(sc, axis=-1, keepdims=True)) for sc in scs]
    return [e * (1.0 / jnp.sum(e, axis=-1, keepdims=True)) for e in es]


def _mem_attn_into(qm, kv_ref, mo_ref):
    low = _low_head_lanes()
    qhs = [h for pr in MEM_PAIRS for h in _split_pair(qm[:, pr], low)]
    k2s = [kv_ref[:, pr] for pr in MEM_PAIRS for _ in range(2)]
    ps = [p.astype(BF16) for p in _mem_probs(qhs, k2s)]
    for i, pr in enumerate(MEM_PAIRS):
        v2 = kv_ref[:, MEM_WIDTH + i * LANES:MEM_WIDTH + (i + 1) * LANES]
        mo_ref[:, pr] = jnp.where(low, _dot(ps[2 * i], v2), _dot(ps[2 * i + 1], v2))


def _mem_attn_bwd(qm, kv_ref, dmem, dqm_ref, dkv_ref):
    low = _low_head_lanes()
    dmb = dmem.astype(BF16)
    vps = [slice(MEM_WIDTH + i * LANES, MEM_WIDTH + (i + 1) * LANES) for i in range(len(MEM_PAIRS))]
    qhs = [h for pr in MEM_PAIRS for h in _split_pair(qm[:, pr], low)]
    dhs = [h for pr in MEM_PAIRS for h in _split_pair(dmb[:, pr], low)]
    k2s = [kv_ref[:, pr] for pr in MEM_PAIRS for _ in range(2)]
    v2s = [kv_ref[:, vp] for vp in vps for _ in range(2)]
    ps = _mem_probs(qhs, k2s)
    dps = [_dot_nt(dh, v2) for dh, v2 in zip(dhs, v2s)]
    dss = [(p * (dp - jnp.sum(dp * p, axis=-1, keepdims=True)) * SCALE).astype(BF16) for p, dp in zip(ps, dps)]
    pbs = [p.astype(BF16) for p in ps]
    for i, (pr, vp) in enumerate(zip(MEM_PAIRS, vps)):
        a, b = 2 * i, 2 * i + 1
        dqm_ref[:, pr] = jnp.where(low, _dot(dss[a], k2s[a]), _dot(dss[b], k2s[b])).astype(BF16)
        dkv_ref[:, pr] += _dot_tn(dss[a], qhs[a]) + _dot_tn(dss[b], qhs[b])
        dkv_ref[:, vp] += _dot_tn(pbs[a], dhs[a]) + _dot_tn(pbs[b], dhs[b])


def _post_attn(os_, ls_, qm, kv, z, x, wg_out):
    s, d_model = x.shape
    gw = GROUP_WIDTH
    nb = gw + MEM_WIDTH
    tm = ROW_TILE

    def body(o0, o1, o2, l0, l1, l2, qm_ref, kv_ref, z_ref, x_ref, w_ref, y_ref, yt_ref, mo_ref, h_ref, s0, s1):
        ov, lv = [], []
        for o_ref, l_ref, d in zip((o0, o1, o2), (l0, l1, l2), DILATIONS):
            ov.append(_from_view(s0, o_ref, d))
            lv.append(_from_view(s1, l_ref, d))
        _, mix = _mix_groups(ov, lv)
        _mem_attn_into(qm_ref[...], kv_ref, mo_ref)
        sz, _ = _silu_parts(z_ref[...])
        y_ref[:, :gw] = (mix * sz[:, :gw]).astype(BF16)
        y_ref[:, gw:] = (mo_ref[...] * sz[:, gw:]).astype(BF16)
        y = y_ref[...]
        yt_ref[...] = y.T
        h_ref[...] = x_ref[...] + _dot(y, _joined_columns(w_ref))

    vspecs = [_view_rows(gw, d) for d in DILATIONS]
    return _pallas_call(
        body, name="post_attn", grid=(s // tm,),
        out_shape=[_sds((s, nb), BF16), _sds((nb, s), BF16), _sds((s, MEM_WIDTH), F32), _sds((s, d_model), F32)],
        in_specs=vspecs * 2 + [_rows(MEM_WIDTH), _whole(kv.shape), _rows(nb), _rows(d_model), _whole(wg_out.shape)],
        out_specs=[_rows(nb), pl.BlockSpec((nb, tm), lambda i: (0, i)), _rows(MEM_WIDTH), _rows(d_model)],
        scratch_shapes=[_view_scratch(gw), _view_scratch(gw)],
        compiler_params=_params(1, 40),
    )(*os_, *ls_, qm, kv, z, x, wg_out)


def _inproj_conv(x, g, wg):
    s, d_model = x.shape
    c = CONV_WIDTH
    n = N_DEV * wg.shape[2]
    nz = n - 3 * c - MEM_WIDTH
    tm = ROW_TILE

    def body(x_ref, g_ref, w_ref, hn_ref, bg_ref, cg_ref, u_ref, qm_ref, z_ref, wj):
        _join_once(w_ref, wj)
        xb = x_ref[...]
        r = lax.rsqrt(jnp.mean(xb * xb, axis=-1, keepdims=True) + EPS)
        hn = ((xb * r) * g_ref[...]).astype(BF16)
        hn_ref[...] = hn
        bg_ref[...] = _dot(hn, wj[:, 0:c])
        cg_ref[...] = _dot(hn, wj[:, c:2 * c])
        u_ref[...] = _dot(hn, wj[:, 2 * c:3 * c])
        qm_ref[...] = _dot(hn, wj[:, 3 * c:3 * c + MEM_WIDTH]).astype(BF16)
        z_ref[...] = _dot(hn, wj[:, 3 * c + MEM_WIDTH:])

    return _pallas_call(
        body, name="inproj_conv", grid=(s // tm,),
        out_shape=[_sds((s, d_model), BF16)] + [_sds((s, c), F32)] * 3 + [_sds((s, MEM_WIDTH), BF16), _sds((s, nz), F32)],
        in_specs=[_rows(d_model), _whole((1, d_model)), _resident(wg.shape)],
        out_specs=[_rows(d_model)] + [_rows(c)] * 3 + [_rows(MEM_WIDTH), _rows(nz)],
        scratch_shapes=[pltpu.VMEM((d_model, n), BF16)],
        compiler_params=_params(1, 60),
    )(x, g, wg)


HALO = 8


def _halo_before(width, tm=ROW_TILE):
    return pl.BlockSpec((HALO, width), lambda i: (jnp.maximum(i * (tm // HALO) - 1, 0), 0))


def _halo_after(width, n_rows, tm=ROW_TILE):
    return pl.BlockSpec((HALO, width), lambda i: (jnp.minimum((i + 1) * (tm // HALO), n_rows // HALO - 1), 0))


def _conv_taps(cg_ref, u_ref, cgh_ref, uh_ref, i):
    a = cg_ref[...] * u_ref[...]
    ah = jnp.where(i > 0, cgh_ref[...] * uh_ref[...], 0.0)
    row = lax.broadcasted_iota(jnp.int32, a.shape, 0)
    a1 = jnp.where(row == 0, ah[HALO - 1:HALO], pltpu.roll(a, 1, 0))
    a2 = jnp.where(row == 0, ah[HALO - 2:HALO - 1], jnp.where(row == 1, ah[HALO - 1:HALO], pltpu.roll(a, 2, 0)))
    return a, a1, a2


def _post_conv_loss(bg, cg, u, qm, kv, z, h1, w_out, cw, gf, tgt):
    s, d = h1.shape
    c = CONV_WIDTH
    nb = c + MEM_WIDTH
    tm = ROW_TILE

    def body(bg_ref, cg_ref, u_ref, cgh_ref, uh_ref, qm_ref, kv_ref, z_ref, h_ref, w_ref, cw_ref, gf_ref, t_ref,
             y_ref, yt_ref, mo_ref, dh_ref, dhb_ref, loss_ref, dgf_ref):
        i = pl.program_id(0)
        a, a1, a2 = _conv_taps(cg_ref, u_ref, cgh_ref, uh_ref, i)
        conv = cw_ref[0:1, :] * a2 + cw_ref[1:2, :] * a1 + cw_ref[2:3, :] * a
        mix = bg_ref[...] * conv
        _mem_attn_into(qm_ref[...], kv_ref, mo_ref)
        sz, _ = _silu_parts(z_ref[...])
        y_ref[:, :c] = (mix * sz[:, :c]).astype(BF16)
        y_ref[:, c:] = (mo_ref[...] * sz[:, c:]).astype(BF16)
        y = y_ref[...]
        yt_ref[...] = y.T
        h2 = h_ref[...] + _dot(y, w_ref[...])
        r = lax.rsqrt(jnp.mean(h2 * h2, axis=-1, keepdims=True) + EPS)
        nh = h2 * r
        gfv = gf_ref[...]
        diff = nh * gfv - t_ref[...]
        dout = diff * (1.0 / d)
        dn = dout * gfv
        dh2 = r * dn - h2 * ((r * r * r) * jnp.mean(dn * h2, axis=-1, keepdims=True))
        dh_ref[...] = dh2
        dhb_ref[...] = dh2.astype(BF16)

        @pl.when(i == 0)
        def _():
            loss_ref[...] = jnp.zeros_like(loss_ref)
            dgf_ref[...] = jnp.zeros_like(dgf_ref)

        loss_ref[...] += 0.5 * jnp.sum(jnp.mean(diff * diff, axis=-1, keepdims=True))
        dgf_ref[...] += jnp.sum(dout * nh, axis=0, keepdims=True)

    return _pallas_call(
        body, name="post_conv_loss", grid=(s // tm,),
        out_shape=[_sds((s, nb), BF16), _sds((nb, s), BF16), _sds((s, MEM_WIDTH), F32), _sds((s, d), F32),
                   _sds((s, d), BF16), _plain((8, LANES), F32), _plain((1, d), F32)],
        in_specs=[_rows(c)] * 3 + [_halo_before(c)] * 2 + [_rows(MEM_WIDTH), _whole(kv.shape), _rows(nb), _rows(d),
                  _whole(w_out.shape), _whole(cw.shape), _whole((1, d)), _rows(d)],
        out_specs=[_rows(nb), pl.BlockSpec((nb, tm), lambda i: (0, i)), _rows(MEM_WIDTH), _rows(d), _rows(d),
                   _whole((8, LANES)), _whole((1, d))],
        compiler_params=_params(1, 48),
    )(bg, cg, u, cg, u, qm, kv, z, h1, w_out, cw, gf, tgt)


def _bwd_post_conv(dhb, w_out, bg, cg, u, z, qm, kv, mo, cw):
    s = dhb.shape[0]
    c = CONV_WIDTH
    nb = c + MEM_WIDTH

    def body(dh_ref, w_ref, bg_ref, cg_ref, u_ref, cgh_ref, uh_ref, z_ref, qm_ref, kv_ref, mo_ref, cw_ref,
             dz_ref, dbg_ref, dc_ref, dqm_ref, dkv_ref):
        i = pl.program_id(0)

        @pl.when(i == 0)
        def _():
            dkv_ref[...] = jnp.zeros_like(dkv_ref)

        dy = _dot_nt(dh_ref[...], w_ref[...])
        sz, dsz = _silu_parts(z_ref[...])
        a, a1, a2 = _conv_taps(cg_ref, u_ref, cgh_ref, uh_ref, i)
        conv = cw_ref[0:1, :] * a2 + cw_ref[1:2, :] * a1 + cw_ref[2:3, :] * a
        bgv = bg_ref[...]
        dz_ref[:, :c] = (dy[:, :c] * (bgv * conv) * dsz[:, :c]).astype(BF16)
        dz_ref[:, c:] = (dy[:, c:] * mo_ref[...] * dsz[:, c:]).astype(BF16)
        dbr = dy * sz
        dmix = dbr[:, :c]
        dbg_ref[...] = (dmix * conv).astype(BF16)
        dc_ref[...] = dmix * bgv
        _mem_attn_bwd(qm_ref[...], kv_ref, dbr[:, c:], dqm_ref, dkv_ref)

    return _pallas_call(
        body, name="bwd_post_conv", grid=(s // ROW_TILE,),
        out_shape=[_sds((s, nb), BF16), _sds((s, c), BF16), _sds((s, c), F32), _sds((s, MEM_WIDTH), BF16),
                   _sds(kv.shape, F32)],
        in_specs=[_rows(D_MODEL), _whole(w_out.shape)] + [_rows(c)] * 3 + [_halo_before(c)] * 2
                 + [_rows(nb), _rows(MEM_WIDTH), _whole(kv.shape), _rows(MEM_WIDTH), _whole(cw.shape)],
        out_specs=[_rows(nb), _rows(c), _rows(c), _rows(MEM_WIDTH), _whole(kv.shape)],
        compiler_params=_params(1, 48),
    )(dhb, w_out, bg, cg, u, cg, u, z, qm, kv, mo, cw)


def _bwd_conv(dconv, cg, u, cw):
    s, c = dconv.shape
    tm = ROW_TILE
    last = s // tm - 1

    def body(dc_ref, dcn_ref, cg_ref, u_ref, cgh_ref, uh_ref, cw_ref, dcg_ref, du_ref, dcw_ref):
        i = pl.program_id(0)

        @pl.when(i == 0)
        def _():
            dcw_ref[...] = jnp.zeros_like(dcw_ref)

        dc = dc_ref[...]
        dcn = jnp.where(i < last, dcn_ref[...], 0.0)
        row = lax.broadcasted_iota(jnp.int32, dc.shape, 0)
        d1 = jnp.where(row == tm - 1, dcn[0:1], pltpu.roll(dc, tm - 1, 0))
        d2 = jnp.where(row == tm - 1, dcn[1:2], jnp.where(row == tm - 2, dcn[0:1], pltpu.roll(dc, tm - 2, 0)))
        da = cw_ref[2:3, :] * dc + cw_ref[1:2, :] * d1 + cw_ref[0:1, :] * d2
        a, a1, a2 = _conv_taps(cg_ref, u_ref, cgh_ref, uh_ref, i)
        dcg_ref[...] = (da * u_ref[...]).astype(BF16)
        du_ref[...] = (da * cg_ref[...]).astype(BF16)
        dcw_ref[0:1, :] += jnp.sum(dc * a2, axis=0, keepdims=True)
        dcw_ref[1:2, :] += jnp.sum(dc * a1, axis=0, keepdims=True)
        dcw_ref[2:3, :] += jnp.sum(dc * a, axis=0, keepdims=True)

    return _pallas_call(
        body, name="bwd_conv", grid=(s // tm,),
        out_shape=[_sds((s, c), BF16), _sds((s, c), BF16), _plain((8, c), F32)],
        in_specs=[_rows(c), _halo_after(c, s), _rows(c), _rows(c), _halo_before(c), _halo_before(c), _whole(cw.shape)],
        out_specs=[_rows(c), _rows(c), _whole((8, c))], compiler_params=_params(1, 40),
    )(dconv, dconv, cg, u, cg, u, cw)


def _assemble(p_refs, pieces, widths, dp, scr):
    off = 0
    for p_ref, (_, d), wd in zip(p_refs, pieces, widths):
        if d == 1:
            dp[:, off:off + wd] = p_ref[...]
        else:
            dp[:, off:off + wd] = _from_view(scr, p_ref, d).astype(BF16)
        off += wd


def _dgrad_norm(pieces, wg, h, g, dres, token, onward, name):
    s, d_model = h.shape
    n = N_DEV * wg.shape[2]
    tm = ROW_TILE
    widths = [p.shape[1] // d for p, d in pieces]
    assert sum(widths) == n
    n_p = len(pieces)

    def body(_, *refs):
        p_refs = refs[:n_p]
        w_ref, h_ref, g_ref, dr_ref, dh_ref, dg_ref = refs[n_p:n_p + 6]
        dp, scr, wj = refs[-3:]
        _join_once(w_ref, wj)

        @pl.when(pl.program_id(0) == 0)
        def _():
            dg_ref[...] = jnp.zeros_like(dg_ref)

        _assemble(p_refs, pieces, widths, dp, scr)
        dhn = _dot_nt(dp[...], wj[...])
        hb = h_ref[...]
        r = lax.rsqrt(jnp.mean(hb * hb, axis=-1, keepdims=True) + EPS)
        dg_ref[...] += jnp.sum(dhn * (hb * r), axis=0, keepdims=True)
        dn = dhn * g_ref[...]
        dh = dr_ref[...] + r * dn - hb * ((r * r * r) * jnp.mean(dn * hb, axis=-1, keepdims=True))
        dh_ref[...] = dh
        if onward:
            dhb_ref, dpt_ref = refs[n_p + 6:n_p + 8]
            dhb_ref[...] = dh.astype(BF16)
            dpt_ref[...] = dp[...].T

    p_specs = [_view_rows(wd, d) for (_, d), wd in zip(pieces, widths)]
    out_shape = [_plain((s, d_model), F32), _plain((1, d_model), F32)]
    out_specs = [_rows(d_model), _whole((1, d_model))]
    if onward:
        out_shape += [_sds((s, d_model), BF16), _sds((n, s), BF16)]
        out_specs += [_rows(d_model), pl.BlockSpec((n, tm), lambda i: (0, i))]
    return _pallas_call(
        body, name=name, grid=(s // tm,), out_shape=out_shape,
        in_specs=[ANY] + p_specs + [_resident(wg.shape), _rows(d_model), _whole((1, d_model)), _rows(d_model)],
        out_specs=out_specs,
        scratch_shapes=[pltpu.VMEM((tm, n), BF16), _view_scratch(GROUP_WIDTH), pltpu.VMEM((d_model, n), BF16)],
        compiler_params=_params(1, 60),
    )(token, *[p for p, _ in pieces], wg, h, g, dres)


def _assemble_dproj_t(pieces, n, name):
    tm = ROW_TILE
    widths = [p.shape[1] // d for p, d in pieces]
    assert sum(widths) == n
    s = pieces[0][0].shape[0] * pieces[0][1]
    n_p = len(pieces)

    def body(*refs):
        p_refs, (dpt_ref, dp, scr) = refs[:n_p], refs[n_p:]
        _assemble(p_refs, pieces, widths, dp, scr)
        dpt_ref[...] = dp[...].T

    return _pallas_call(
        body, name=name, grid=(s // tm,), out_shape=_sds((n, s), BF16),
        in_specs=[_view_rows(wd, d) for (_, d), wd in zip(pieces, widths)],
        out_specs=pl.BlockSpec((n, tm), lambda i: (0, i)),
        scratch_shapes=[pltpu.VMEM((tm, n), BF16), _view_scratch(GROUP_WIDTH)],
        compiler_params=_params(1, 40),
    )(*[p for p, _ in pieces])


def _wgrad_shards_t(dp_t, h, c, name):
    n, s = dp_t.shape
    d_model = h.shape[1]
    per_step = 2

    def body(a_ref, b_ref, o_ref):
        o_ref[...] = _dot(a_ref[...], b_ref[...]).astype(BF16).reshape(per_step, c, d_model)

    return _pallas_call(
        body, name=name, grid=(N_DEV // per_step,), out_shape=_sds((N_DEV, c, d_model), BF16),
        in_specs=[pl.BlockSpec((per_step * c, s), lambda j: (j, 0)), _resident(h.shape)],
        out_specs=pl.BlockSpec((per_step, c, d_model), lambda j: (j, 0, 0)), compiler_params=_params(1, 40),
    )(dp_t, h)


def _wgrad_cols(a_t, b, c, name):
    m, s = a_t.shape
    assert c % LANES == 0

    def body(a_ref, b_ref, o_ref):
        wide = _dot(a_ref[...], b_ref[...]).astype(BF16)
        for j in range(WGRAD_SHARDS):
            o_ref[j] = wide[:, j * c:(j + 1) * c]

    return _pallas_call(
        body, name=name, grid=(N_DEV // WGRAD_SHARDS,), out_shape=_sds((N_DEV, m, c), BF16),
        in_specs=[_whole(a_t.shape), pl.BlockSpec((s, WGRAD_SHARDS * c), lambda j: (0, j))],
        out_specs=pl.BlockSpec((WGRAD_SHARDS, m, c), lambda j: (j, 0, 0)), compiler_params=_params(1, 40),
    )(a_t, b)


def _wgrad_rows(a_t, b, name):
    m, s = a_t.shape
    n = b.shape[1]
    mr = m // N_DEV

    def body(a_ref, b_ref, o_ref):
        o_ref[...] = _dot(a_ref[...], b_ref[...]).astype(BF16).reshape(WGRAD_SHARDS, mr, n)

    return _pallas_call(
        body, name=name, grid=(N_DEV // WGRAD_SHARDS,), out_shape=_sds((N_DEV, mr, n), BF16),
        in_specs=[pl.BlockSpec((WGRAD_SHARDS * mr, s), lambda j: (j, 0)), _whole(b.shape)],
        out_specs=pl.BlockSpec((WGRAD_SHARDS, mr, n), lambda j: (j, 0, 0)), compiler_params=_params(1, 40),
    )(a_t, b)


def _memkv_bwd(dkvs, w, mem, g):
    n_layers = g.shape[0]
    rows = D_MODEL // N_DEV

    def body(dkv0_ref, dkv1_ref, w_ref, mem_ref, g_ref, dw_ref, dg_ref):
        mb = mem_ref[...]
        r = lax.rsqrt(jnp.mean(mb * mb, axis=-1, keepdims=True) + EPS)
        nm = mb * r
        mn = (nm * g_ref[...]).astype(BF16)
        dkvb = jnp.where(pl.program_id(0) == 0, dkv0_ref[...], dkv1_ref[...]).astype(BF16)
        dw_ref[...] = _dot_tn(mn, dkvb).astype(BF16).reshape(N_DEV, rows, 2 * MEM_WIDTH)
        dmn = _dot_nt(dkvb, w_ref[...].reshape(D_MODEL, 2 * MEM_WIDTH))
        dg_ref[...] = jnp.sum(dmn * nm, axis=0, keepdims=True)

    lay = lambda *shape: pl.BlockSpec((None,) + shape, lambda l: (l, 0, 0))
    major = pl.BlockSpec((N_DEV, rows, 2 * MEM_WIDTH), lambda l: (0, l, 0))
    return _pallas_call(
        body, name="memkv_bwd", grid=(n_layers,),
        out_shape=[_sds((N_DEV, n_layers * rows, 2 * MEM_WIDTH), BF16), _plain((n_layers, 1, D_MODEL), F32)],
        in_specs=[_whole(dkvs[0].shape), _whole(dkvs[1].shape), major, _whole(mem.shape), lay(1, D_MODEL)],
        out_specs=[major, lay(1, D_MODEL)],
        compiler_params=_params(1, 32),
    )(*dkvs, w, mem, g.reshape(n_layers, 1, D_MODEL))


def _bwd_post_attn(dhb, wg_out, z, os_, ls_, qm, kv, mo, head_ones, token):
    s = dhb.shape[0]
    gw = GROUP_WIDTH
    nb = gw + MEM_WIDTH
    tm = ROW_TILE

    def body(_, dh_ref, w_ref, z_ref, o0, o1, o2, l0, l1, l2, qm_ref, kv_ref, mo_ref, bd_ref,
             dz_ref, do0, do1, do2, dl0, dl1, dl2, dqm_ref, dkv_ref, s0, s1):
        @pl.when(pl.program_id(0) == 0)
        def _():
            dkv_ref[...] = jnp.zeros_like(dkv_ref)

        dy = _dot_nt(dh_ref[...], _joined_columns(w_ref))
        ov, lv = [], []
        for o_ref, l_ref, d in zip((o0, o1, o2), (l0, l1, l2), DILATIONS):
            ov.append(_from_view(s0, o_ref, d))
            lv.append(_from_view(s1, l_ref, d))
        ws, mix = _mix_groups(ov, lv)
        sz, dsz = _silu_parts(z_ref[...])
        dz_ref[:, :gw] = (dy[:, :gw] * mix * dsz[:, :gw]).astype(BF16)
        dz_ref[:, gw:] = (dy[:, gw:] * mo_ref[...] * dsz[:, gw:]).astype(BF16)
        dbr = dy * sz
        dmix = dbr[:, :gw]
        t = dmix * mix
        th = t.astype(BF16)
        tl = (t - th.astype(F32)).astype(BF16)
        rs = _dot(th, bd_ref[...]) + _dot(tl, bd_ref[...])
        for wg_, do_ref, dl_ref, d in zip(ws, (do0, do1, do2), (dl0, dl1, dl2), DILATIONS):
            _to_view(s0, wg_ * dmix, do_ref, d)
            _to_view(s1, wg_ * rs, dl_ref, d)
        _mem_attn_bwd(qm_ref[...], kv_ref, dbr[:, gw:], dqm_ref, dkv_ref)

    vspecs = [_view_rows(gw, d) for d in DILATIONS]
    return _pallas_call(
        body, name="bwd_post_attn", grid=(s // tm,),
        out_shape=[_sds((s, nb), BF16)] + [_sds((s // d, d * gw), BF16) for d in DILATIONS]
                  + [_sds((s // d, d * gw), F32) for d in DILATIONS] + [_sds((s, MEM_WIDTH), BF16), _sds(kv.shape, F32)],
        in_specs=[ANY, _rows(D_MODEL), _whole(wg_out.shape), _rows(nb)] + vspecs * 2
                 + [_rows(MEM_WIDTH), _whole(kv.shape), _rows(MEM_WIDTH), _whole(head_ones.shape)],
        out_specs=[_rows(nb)] + vspecs * 2 + [_rows(MEM_WIDTH), _whole(kv.shape)],
        scratch_shapes=[_view_scratch(gw), _view_scratch(gw)],
        compiler_params=_params(1, 48),
    )(token, dhb, wg_out, z, *os_, *ls_, qm, kv, mo, head_ones)


def _attn_bwd(q, k, v, lse, do, dl, tabs, d, token):
    ln, dw = q.shape
    w = dw // d
    nb = ln // BLOCK
    reps = w // LANES
    two, before = _pair_specs(d, nb, w)
    two_t, _ = _pair_specs(d, nb, LANES)

    def attend(q_ref, l_ref, do_ref, dl_ref, dqs, acck, accv, rows, col0, kk, vv, acc_rows):
        valid = _band_mask(kk.shape[0])
        low = _low_head_lanes()
        pairs, qcols = _head_tiles(w, col0)
        cols = [slice(col0 + h * HEAD_DIM, col0 + h * HEAD_DIM + 1) for h in range(HEADS_PER_GROUP)]
        qhs = [h for qc in qcols for h in _split_pair(q_ref[rows, qc], low)]
        dobs = [h for qc in qcols for h in _split_pair(do_ref[rows, qc], low)]
        k2s = [kk[:, pr] for pr in pairs for _ in range(2)]
        v2s = [vv[:, pr] for pr in pairs for _ in range(2)]
        scs = [jnp.where(valid, _dot_nt(qh, k2), NEG) for qh, k2 in zip(qhs, k2s)]
        dps = [_dot_nt(dob, v2) for dob, v2 in zip(dobs, v2s)]
        ps = [jnp.exp(sc - l_ref[rows, col]) for sc, col in zip(scs, cols)]
        dss = [(p * (dp - dl_ref[rows, col])).astype(BF16) for p, dp, col in zip(ps, dps, cols)]
        pbs = [p.astype(BF16) for p in ps]
        for i, qc in enumerate(qcols):
            a, b = 2 * i, 2 * i + 1
            dqs[rows, qc] = jnp.where(low, _dot(dss[a], k2s[a]), _dot(dss[b], k2s[b])) * SCALE
            acck[acc_rows, qc] += _dot_tn(dss[a], qhs[a]) + _dot_tn(dss[b], qhs[b])
            accv[acc_rows, qc] += _dot_tn(pbs[a], dobs[a]) + _dot_tn(pbs[b], dobs[b])

    def body_streams(_, q_ref, kc_ref, vc_ref, l_ref, do_ref, dl_ref, c_ref, sa_ref, sb_ref,
                     dq_ref, dk_ref, dv_ref, acck, accv, dqs):
        acck[...] = jnp.zeros_like(acck)
        accv[...] = jnp.zeros_like(accv)
        for sb in range(2):
            cols = slice(sb * w, (sb + 1) * w)
            attend(q_ref, l_ref, do_ref, dl_ref, dqs, acck, accv, TOP, sb * w, kc_ref[:, cols], vc_ref[:, cols], TOP)
        tabs2 = [jnp.concatenate([jnp.tile(r[:, sb * LANES:(sb + 1) * LANES], (1, reps)) for sb in range(2)], axis=1)
                 for r in (c_ref, sa_ref, sb_ref)]
        dq_ref[...] = _rope_bwd(dqs[...], *tabs2).astype(BF16)
        dk_ref[...] = _rope_bwd(acck[...], *tabs2).astype(BF16)
        dv_ref[...] = accv[...].astype(BF16)

    def body_blocks(_, q_ref, kp_ref, kc_ref, vp_ref, vc_ref, l_ref, do_ref, dl_ref, cq, saq, sbq, ck, sak, sbk,
                    dq_ref, dk_ref, dv_ref, acck, accv, dqs):
        i = pl.program_id(0) % (nb // 2)

        @pl.when(i == 0)
        def _():
            acck[...] = jnp.zeros_like(acck)
            accv[...] = jnp.zeros_like(accv)

        refs = (q_ref, l_ref, do_ref, dl_ref, dqs, acck, accv)
        pl.when(i == 0)(lambda: attend(*refs, TOP, 0, kc_ref[TOP, :], vc_ref[TOP, :], TOP))
        pl.when(i != 0)(lambda: attend(
            *refs, TOP, 0, jnp.concatenate([kp_ref[...], kc_ref[TOP, :]], axis=0),
            jnp.concatenate([vp_ref[...], vc_ref[TOP, :]], axis=0),
            pl.ds(pl.multiple_of((2 * i - 1) * BLOCK, BLOCK), 2 * BLOCK)))
        attend(*refs, BOTTOM, 0, kc_ref[...], vc_ref[...], pl.ds(pl.multiple_of(2 * i * BLOCK, BLOCK), 2 * BLOCK))
        tq = [jnp.tile(r[...], (1, reps)) for r in (cq, saq, sbq)]
        dq_ref[...] = _rope_bwd(dqs[...], *tq).astype(BF16)

        @pl.when(i == nb // 2 - 1)
        def _():
            for r0 in range(0, nb * BLOCK, 2 * BLOCK):
                rows = slice(r0, r0 + 2 * BLOCK)
                tk = [jnp.tile(r[rows, :], (1, reps)) for r in (ck, sak, sbk)]
                dk_ref[rows, :] = _rope_bwd(acck[rows, :], *tk).astype(BF16)
                dv_ref[rows, :] = accv[rows, :].astype(BF16)

    if nb == 1:
        body = body_streams
        in_specs = [ANY] + [two] * 6 + [two_t] * 3
        args = (token, q, k, v, lse, do, dl, *tabs)
        out_specs = [two, two, two]
        acc_shape = (BLOCK, 2 * w)
    else:
        body = body_blocks
        stream = pl.BlockSpec((nb * BLOCK, w), lambda n: (0, n // (nb // 2)))
        stream_t = pl.BlockSpec((nb * BLOCK, LANES), lambda n: (0, n // (nb // 2)))
        in_specs = [ANY, two, before, two, before, two, two, two, two] + [two_t] * 3 + [stream_t] * 3
        args = (token, q, k, k, v, v, lse, do, dl, *tabs, *tabs)
        out_specs = [two, stream, stream]
        acc_shape = (nb * BLOCK, w)
    return _pallas_call(
        body, name=f"attn_bwd_d{d}", grid=(d * nb // 2,), out_shape=[_sds((ln, dw), BF16)] * 3,
        in_specs=in_specs, out_specs=out_specs,
        scratch_shapes=[pltpu.VMEM(acc_shape, F32), pltpu.VMEM(acc_shape, F32), pltpu.VMEM(two.block_shape, F32)],
        compiler_params=_params(1, 48),
    )(*args)


def _position():
    return lax.axis_index("x"), lax.axis_index("y"), lax.axis_index("c")


def _all_gather_small(xs, afters, name):
    n_in = 1 + len(afters)

    def body(*refs):
        x_ref, out_ref = refs[0], refs[n_in]
        send_sems, recv_sems, local_sem = refs[n_in + 1:]
        x, y, c = _position()
        my_rows = out_ref.at[4 * x + 2 * y + c]
        mine = pltpu.make_async_copy(x_ref, my_rows, local_sem)
        mine.start()
        copies = [pltpu.make_async_remote_copy(
            src_ref=x_ref, dst_ref=my_rows, send_sem=send_sems.at[k], recv_sem=recv_sems.at[k],
            device_id=(1 - x if k & 4 else x, 1 - y if k & 2 else y, 1 - c if k & 1 else c), device_id_type=MESH)
            for k in range(1, N_DEV)]
        for cp in copies:
            cp.start()
        for cp in copies:
            cp.wait_recv()
        for cp in copies:
            cp.wait_send()
        mine.wait()

    return _pallas_call(
        body, name=name, out_shape=_sds((N_DEV,) + xs.shape, xs.dtype),
        in_specs=[ANY] * n_in, out_specs=ANY,
        scratch_shapes=[pltpu.SemaphoreType.DMA((N_DEV,)), pltpu.SemaphoreType.DMA((N_DEV,)), pltpu.SemaphoreType.DMA],
    )(xs, *afters)


GATHER_CHUNKS = 4


def _all_gather_relay(xs, name):
    def body(x_ref, out_ref, send_sems, recv_sems, local_sem):
        x, y, c = _position()
        me, sibling = (x, y, c), (x, y, 1 - c)
        xn, yn, diag = (1 - x, y, c), (x, 1 - y, c), (1 - x, 1 - y, c)
        src_nb = (x + c * (1 - 2 * x), y + (1 - c) * (1 - 2 * y), c)
        dst_nb = (x + (1 - c) * (1 - 2 * x), y + c * (1 - 2 * y), c)

        def rows(dev, i):
            return out_ref.at[4 * dev[0] + 2 * dev[1] + dev[2], pl.ds(i * step, step)]

        def copy(k, i, block, to, own=False):
            return pltpu.make_async_remote_copy(
                src_ref=x_ref.at[pl.ds(i * step, step)] if own else rows(block, i), dst_ref=rows(block, i),
                send_sem=send_sems.at[k, i], recv_sem=recv_sems.at[k, i], device_id=to, device_id_type=MESH)

        mine = pltpu.make_async_copy(x_ref, out_ref.at[4 * x + 2 * y + c], local_sem)
        mine.start()
        sent = []
        for i in chunks:
            sent += [copy(1, i, me, xn, own=True), copy(2, i, me, yn, own=True), copy(0, i, me, sibling, own=True)]
        for cp in sent:
            cp.start()
        for i in chunks:
            copy(1, i, xn, me).wait_recv()
            copy(2, i, yn, me).wait_recv()
            onward = [copy(3, i, src_nb, dst_nb), copy(4, i, xn, sibling), copy(5, i, yn, sibling)]
            for cp in onward:
                cp.start()
            sent += onward
        for i in chunks:
            copy(3, i, diag, me).wait_recv()
            last = copy(6, i, diag, sibling)
            last.start()
            sent.append(last)
        for i in chunks:
            copy(0, i, sibling, me).wait_recv()
            for k, blk in ((4, (1 - x, y, 1 - c)), (5, (x, 1 - y, 1 - c)), (6, (1 - x, 1 - y, 1 - c))):
                copy(k, i, blk, me).wait_recv()
        for cp in sent:
            cp.wait_send()
        mine.wait()

    step = xs.shape[0] // GATHER_CHUNKS
    chunks = range(GATHER_CHUNKS)
    return _pallas_call(
        body, name=name, out_shape=_sds((N_DEV,) + xs.shape, xs.dtype),
        in_specs=[ANY], out_specs=ANY,
        scratch_shapes=[pltpu.SemaphoreType.DMA((7, GATHER_CHUNKS)), pltpu.SemaphoreType.DMA((7, GATHER_CHUNKS)),
                        pltpu.SemaphoreType.DMA],
    )(xs)


HBM_SPEC = pl.BlockSpec(memory_space=pltpu.HBM)
SEM_SPEC = pl.BlockSpec(memory_space=pltpu.SEMAPHORE)
EFFECT = pltpu.SideEffectType.DATAFLOW_SIDE_EFFECTING
def _plan_gather_own(src_refs, land_refs):
    x, y, c = _position()
    me = 4 * x + 2 * y + c
    peers = [(x, y, 1 - c), (1 - x, y, c), (x, 1 - y, c), (1 - x, 1 - y, c)]
    return [(land_refs[a].at[me], land_refs[a].at[me], (a, k), peer)
            for k, peer in enumerate(peers) for a in range(len(land_refs))]


def _plan_gather_pass(src_refs, land_refs):
    x, y, c = _position()
    chips = [(1 - x, y), (x, 1 - y), (1 - x, 1 - y)]
    return [(land_refs[a].at[4 * px + 2 * py + c], land_refs[a].at[4 * px + 2 * py + c], (a, j), (x, y, 1 - c))
            for j, (px, py) in enumerate(chips) for a in range(len(land_refs))]


def _plan_to_sibling(src_refs, land_refs):
    x, y, c = _position()
    return [(src_refs[a].at[2 * k + (1 - c)], land_refs[a].at[k], (a, k), (x, y, 1 - c))
            for k in range(4) for a in range(len(src_refs))]


def _plan_to_chips(src_refs, land_refs):
    x, y, c = _position()
    chips = [(1 - x, y), (x, 1 - y), (1 - x, 1 - y)]
    return [(src_refs[a].at[2 * px + py], land_refs[a].at[j], (a, j), (px, py, c))
            for j, (px, py) in enumerate(chips) for a in range(len(src_refs))]


def _split_start(srcs, lands, plan, n_sem, after, name):
    n_s, n_a = len(srcs), len(lands)
    n_b = n_s + n_a

    def body(*refs):
        src_refs, land_refs = refs[:n_s], refs[n_s:n_b]
        send_sems, recv_sems, token = refs[n_b + 1], refs[n_b + 2], refs[-1]
        for src, dst, (a, k), dev in plan(src_refs, land_refs):
            i = a * n_sem + k
            pltpu.make_async_remote_copy(src_ref=src, dst_ref=dst, send_sem=send_sems.at[i], recv_sem=recv_sems.at[i],
                                         device_id=dev, device_id_type=MESH).start()
        token[...] = jnp.zeros_like(token)

    bufs = list(srcs) + list(lands)
    res = pl.pallas_call(
        body, name=name,
        out_shape=(pltpu.SemaphoreType.DMA((n_a * n_sem,)), pltpu.SemaphoreType.DMA((n_a * n_sem,)),
                   *[pltpu.HBM(t.shape, t.dtype) for t in bufs], _plain((8, LANES), F32)),
        in_specs=[HBM_SPEC] * n_b + [ANY],
        out_specs=(SEM_SPEC, SEM_SPEC, *[HBM_SPEC] * n_b, pl.BlockSpec(memory_space=pltpu.VMEM)),
        input_output_aliases={i: 2 + i for i in range(n_b)},
        compiler_params=pltpu.CompilerParams(has_side_effects=EFFECT),
    )(*[pltpu.with_memory_space_constraint(t, pltpu.HBM) for t in bufs], after)
    return (res[0], res[1], res[2:2 + n_s], res[2 + n_s:2 + n_b]), res[-1]


def _split_wait(started, plan, after, name, first=0, n_sem=None):
    send_sems, recv_sems, srcs, lands = started
    n_s, n_a = len(srcs), len(lands)
    n_b = n_s + n_a
    n_sem = n_sem or send_sems.shape[0] // n_a

    def body(*refs):
        src_refs, land_refs = refs[:n_s], refs[n_s:n_b]
        s_sems, r_sems = refs[n_b], refs[n_b + 1]
        for src, dst, (a, k), dev in plan(src_refs, land_refs):
            i = (first + a) * n_sem + k
            cp = pltpu.make_async_remote_copy(src_ref=src, dst_ref=dst, send_sem=s_sems.at[i], recv_sem=r_sems.at[i],
                                              device_id=dev, device_id_type=MESH)
            cp.wait_send()
            cp.wait_recv()

    bufs = list(srcs) + list(lands)
    res = pl.pallas_call(
        body, name=name, out_shape=tuple(pltpu.HBM(t.shape, t.dtype) for t in bufs),
        in_specs=[HBM_SPEC] * n_b + [SEM_SPEC, SEM_SPEC, ANY],
        out_specs=tuple([HBM_SPEC] * n_b),
        input_output_aliases={i: i for i in range(n_b)},
        compiler_params=pltpu.CompilerParams(has_side_effects=EFFECT),
    )(*bufs, send_sems, recv_sems, after)
    return res[:n_s], res[n_s:]


SUBLANES = 8


def _row_tile(r):
    return max(t for t in range(SUBLANES, ROW_TILE + 1, SUBLANES) if r % t == 0)


def _rs_add_sibling(gps, recvs, ck_arr, name):
    n = len(gps)
    block = lambda k, ck: (k + ck[1] + 1) % 4

    def body(ck_ref, *refs):
        for g_ref, r_ref, pf_ref, pb_ref in zip(refs[:n], refs[n:2 * n], refs[2 * n::2], refs[2 * n + 1::2]):
            sm = g_ref[...].astype(F32) + r_ref[...].astype(F32)
            pf_ref[...] = sm
            pb_ref[...] = sm.astype(BF16)

    mine = lambda t: pl.BlockSpec((None,) + t.shape[1:], lambda k, ck: (2 * block(k, ck) + ck[0], 0, 0))
    one = lambda t: pl.BlockSpec((None,) + t.shape[1:], lambda k, ck: (block(k, ck), 0, 0))
    res = _pallas_call(
        body, name=name,
        grid_spec=pltpu.PrefetchScalarGridSpec(
            num_scalar_prefetch=1, grid=(4,),
            in_specs=[mine(t) for t in gps] + [one(t) for t in recvs],
            out_specs=[s for t in recvs for s in (pl.BlockSpec(t.shape[1:], lambda k, ck: (0, 0)), one(t))]),
        out_shape=[s for t in recvs for s in (_sds(t.shape[1:], F32), _sds(t.shape, BF16))],
        compiler_params=_params(1, 48),
    )(ck_arr, *gps, *recvs)
    return list(zip(res[0::2], res[1::2]))


def _adam_update(w, gv, m, v):
    nm = ADAM_B1 * m + (1.0 - ADAM_B1) * gv
    nv = ADAM_B2 * v + (1.0 - ADAM_B2) * (gv * gv)
    m_hat = nm / (1.0 - ADAM_B1 ** ADAM_STEP)
    v_hat = nv / (1.0 - ADAM_B2 ** ADAM_STEP)
    return -ADAM_LR * (m_hat / (jnp.sqrt(v_hat) + ADAM_EPS) + ADAM_WD * w), nm, nv


def _rs_finish_adamw(pf, recv, w, m, v, after, name):
    r, l = pf.shape
    tr = _row_tile(r)

    def body(_, p_ref, r_ref, w_ref, m_ref, v_ref, g_ref, d_ref, nm_ref, nv_ref):
        gv = ((p_ref[...] + r_ref[0].astype(F32)) + r_ref[1].astype(F32)) + r_ref[2].astype(F32)
        g_ref[...] = gv
        d_ref[...], nm_ref[...], nv_ref[...] = _adam_update(w_ref[...], gv, m_ref[...], v_ref[...])

    spec = pl.BlockSpec((tr, l), lambda i: (i, 0))
    return _pallas_call(
        body, name=name, grid=(r // tr,),
        in_specs=[ANY, spec, pl.BlockSpec((3, tr, l), lambda i: (0, i, 0)), spec, spec, spec], out_specs=[spec] * 4,
        out_shape=[_plain((r, l), F32)] * 4, compiler_params=_params(1, 32),
    )(after, pf, recv, w, m, v)


SMALL_ROWS = dict(norm_g=(0, 2), mem_norm_g=(2, 4), final_g=(4, 5), conv_w=(5, 8))
LOSS_ROWS = (8, 16)


def _sum_adamw_small(g, ck_arr, states):
    names = list(SMALL_ROWS)
    n_dev, n_rows, _ = g.shape

    def body(ck_ref, g_ref, gc_ref, *refs):
        ins, loss_ref, outs = refs[:3 * len(names)], refs[3 * len(names)], refs[3 * len(names) + 1:]

        def total(ref, lo, hi):
            acc = ref[0, lo:hi, :]
            for j in range(1, n_dev):
                acc = acc + ref[j, lo:hi, :]
            return acc

        loss_ref[...] = total(gc_ref, *LOSS_ROWS)
        for i, n in enumerate(names):
            gv = total(gc_ref if n == "conv_w" else g_ref, *SMALL_ROWS[n])
            w_ref, m_ref, v_ref = ins[3 * i:3 * i + 3]
            g_out, d_out, nm_out, nv_out = outs[4 * i:4 * i + 4]
            g_out[...] = gv
            d_out[...], nm_out[...], nv_out[...] = _adam_update(w_ref[...], gv, m_ref[...], v_ref[...])

    flat = [t for n in names for t in states[n]]
    mine = pl.BlockSpec((n_dev, n_rows, LANES), lambda i, ck: (0, 0, 2 * ck[1] + ck[0]))
    res = _pallas_call(
        body, name="sum_adamw_small",
        grid_spec=pltpu.PrefetchScalarGridSpec(
            num_scalar_prefetch=1, grid=(1,),
            in_specs=[_whole(g.shape), mine] + [_whole(t.shape) for t in flat],
            out_specs=[_whole((SUBLANES, LANES))] + [_whole(states[n][0].shape) for n in names for _ in range(4)]),
        out_shape=[_plain((SUBLANES, LANES), F32)] + [_plain(states[n][0].shape, F32) for n in names for _ in range(4)],
        compiler_params=_params(1, 32),
    )(ck_arr, g, g, *flat)
    return res[0], {n: tuple(res[1 + 4 * i:5 + 4 * i]) for i, n in enumerate(names)}


def _finish(name, pf, recv, w, m, v, after):
    if name in ("attn_w_in", "conv_w_in"):
        res = _rs_finish_adamw(pf, recv, w.T, m.T, v.T, after, "rs_finish_adamw_" + name)
        return tuple(t.T for t in res)
    return _rs_finish_adamw(pf, recv, w, m, v, after, "rs_finish_adamw_" + name)


def kernel(x, mem, positions, norm_g, mem_norm_g, w_mem_kv, attn_w_in, attn_w_out, conv_w_in, conv_w, conv_w_out, final_g, loss_target, m_norm_g, m_mem_norm_g, m_w_mem_kv, m_attn_w_in, m_attn_w_out, m_conv_w_in, m_conv_w, m_conv_w_out, m_final_g, v_norm_g, v_mem_norm_g, v_w_mem_kv, v_attn_w_in, v_attn_w_out, v_conv_w_in, v_conv_w, v_conv_w_out, v_final_g):
    px, py, pc = _position()
    me = 4 * px + 2 * py + pc
    ck_arr = jnp.stack([pc, 2 * px + py]).astype(jnp.int32)
    x, mem, pos, tgt = x[0], mem[0], positions[0], loss_target[0]

    wg_in0 = _all_gather_relay(attn_w_in[0].astype(BF16), "gather_w_in0")

    def gather_pass(weights, after, name, first):
        _, lands = _split_wait(weights, _plan_gather_own, after, name + "_wait", first, 4)
        return _split_start([], lands, _plan_gather_pass, 3, after, name + "_pass_start")

    late = [attn_w_out[0].astype(BF16), w_mem_kv.astype(BF16).reshape(-1, w_mem_kv.shape[2]),
            jnp.pad(conv_w[0], ((0, 5), (0, 0))), conv_w_in[0].astype(BF16), conv_w_out[0].astype(BF16)]
    lands = [lax.dynamic_update_slice(lax.empty((N_DEV,) + t.shape, t.dtype), t[None], (me, 0, 0)) for t in late]
    (send_sems, recv_sems, _, lands), token = _split_start([], lands, _plan_gather_own, 4, wg_in0, "gather_late_start")
    rest0, conv_ws = (send_sems, recv_sems, [], lands[:3]), (send_sems, recv_sems, [], lands[3:])

    tabs = _rope_tables(pos)
    g0, g1 = norm_g[0:1], norm_g[1:2]

    hn0, qs, ks, vs, tabs_v, qm0, z0 = _inproj_attn(x, g0, wg_in0, tabs, token)
    os_, ls_ = [], []
    for j, d in enumerate(DILATIONS):
        if j == 2:
            rest0, token = gather_pass(rest0, ls_[1], "gather_rest", 0)
        o, l = _attn_fwd(qs[j], ks[j], vs[j], d, token)
        os_.append(o)
        ls_.append(l)

    conv_ws, token = gather_pass(conv_ws, ls_[2], "gather_conv", 3)
    _, (wg_out0, wg_kv, cw_all) = _split_wait(rest0, _plan_gather_pass, token, "gather_rest_pass_wait")
    cw = cw_all[:, 0:3].transpose(1, 0, 2).reshape(3, -1)
    kv = _memkv_fwd(mem, mem_norm_g, wg_kv)
    y0, y0_t, mo0, h1 = _post_attn(os_, ls_, qm0, kv[0], z0, x, wg_out0)
    _, (wg_in1, wg_out1) = _split_wait(conv_ws, _plan_gather_pass, h1, "gather_conv_pass_wait")
    w_out1 = wg_out1.reshape(-1, wg_out1.shape[2])

    hn1, bg, cg, u, qm1, z1 = _inproj_conv(h1, g1, wg_in1)
    y1, y1_t, mo1, dh2, dh2b, loss_acc, d_final_g = _post_conv_loss(
        bg, cg, u, qm1, kv[1], z1, h1, w_out1, cw, final_g.reshape(1, -1), tgt)

    d_w_out1 = _wgrad_rows(y1_t, dh2b, "wgrad_out1")
    dz1, dbg, dconv, dqm1, dkv1 = _bwd_post_conv(dh2b, w_out1, bg, cg, u, z1, qm1, kv[1], mo1, cw)
    dcg, du, dcw = _bwd_conv(dconv, cg, u, cw)
    dh1, dg1, dh1b, dproj1_t = _dgrad_norm([(dbg, 1), (dcg, 1), (du, 1), (dqm1, 1), (dz1, 1)], wg_in1, h1, g1, dh2,
                                           dh2, True, "dgrad_norm_conv")
    d_w_in1 = _wgrad_shards_t(dproj1_t, hn1, wg_in1.shape[2], "wgrad_in1")

    d_w_out0 = _wgrad_cols(y0_t, dh1b, wg_out0.shape[2], "wgrad_out0")

    gw = GROUP_WIDTH
    ones = (jnp.arange(gw)[:, None] // HEAD_DIM == jnp.arange(gw)[None, :] // HEAD_DIM).astype(BF16)
    res = _bwd_post_attn(dh1b, wg_out0, z0, os_, ls_, qm0, kv[0], mo0, ones, dg1)
    dz0, dos, dls, dqm0, dkv0 = res[0], res[1:4], res[4:7], res[7], res[8]
    d_w_kv, d_mem_g = _memkv_bwd([dkv0, dkv1], wg_kv, mem, mem_norm_g)

    names1 = ["conv_w_in", "conv_w_out", "attn_w_out", "w_mem_kv"]
    grads1 = [d_w_in1, d_w_out1, d_w_out0, d_w_kv]
    started, token = _split_start(grads1, [lax.empty((4,) + g.shape[1:], g.dtype) for g in grads1],
                                  _plan_to_sibling, 4, d_mem_g, "rs1_sibling_start")

    dqs, dks, dvs = [], [], []
    for j, d in enumerate(DILATIONS):
        if j == 1:
            grads1, from_sibling = _split_wait(started, _plan_to_sibling, dqs[0][0], "rs1_sibling_wait")
            parts1 = _rs_add_sibling(grads1, from_sibling, ck_arr, "rs1_add_sibling")
            pbs1 = [pb for _, pb in parts1]
            started, token = _split_start(pbs1, [lax.empty((3,) + p.shape[1:], p.dtype) for p in pbs1],
                                          _plan_to_chips, 3, dg1, "rs1_chips_start")
        dq, dk, dv = _attn_bwd(qs[j], ks[j], vs[j], ls_[j], dos[j], dls[j], tabs_v[j], d, token)
        dqs.append((dq, d))
        dks.append((dk, d))
        dvs.append((dv, d))
    pieces0 = dqs + dks + dvs + [(dqm0, 1), (dz0, 1)]
    dproj0_t = _assemble_dproj_t(pieces0, N_DEV * wg_in0.shape[2], "assemble_dproj_attn")
    d_w_in0 = _wgrad_shards_t(dproj0_t, hn0, wg_in0.shape[2], "wgrad_in0")

    names0 = ["attn_w_in"]
    grads0 = [d_w_in0]
    started0, token0 = _split_start(grads0, [lax.empty((4,) + g.shape[1:], g.dtype) for g in grads0],
                                    _plan_to_sibling, 4, dg1, "rs0_sibling_start")
    _, from_chips1 = _split_wait(started, _plan_to_chips, token0, "rs1_chips_wait")
    shard = dict(attn_w_in=(attn_w_in[0], m_attn_w_in[0], v_attn_w_in[0]),
                 attn_w_out=(attn_w_out[0], m_attn_w_out[0], v_attn_w_out[0]),
                 conv_w_in=(conv_w_in[0], m_conv_w_in[0], v_conv_w_in[0]),
                 conv_w_out=(conv_w_out[0], m_conv_w_out[0], v_conv_w_out[0]),
                 w_mem_kv=tuple(t.reshape(-1, t.shape[2]) for t in (w_mem_kv, m_w_mem_kv, v_w_mem_kv)))
    finish1 = {n: (pf, r) for n, (pf, _), r in zip(names1, parts1, from_chips1)}
    big = {"conv_w_in": _finish("conv_w_in", *finish1["conv_w_in"], *shard["conv_w_in"], token0)}

    grads0, from_sibling = _split_wait(started0, _plan_to_sibling, big["conv_w_in"][1].T, "rs0_sibling_wait")
    parts0 = _rs_add_sibling(grads0, from_sibling, ck_arr, "rs0_add_sibling")
    pbs0 = [pb for _, pb in parts0]
    started0, token0 = _split_start(pbs0, [lax.empty((3,) + p.shape[1:], p.dtype) for p in pbs0],
                                    _plan_to_chips, 3, dg1, "rs0_chips_start")
    dx, dg0 = _dgrad_norm(pieces0, wg_in0, x, g0, dh1, token0, False, "dgrad_norm_attn")
    for n in names1[1:]:
        big[n] = _finish(n, *finish1[n], *shard[n], token0)

    small_part = jnp.concatenate([dg0, dg1, d_mem_g.reshape(2, -1), d_final_g, dcw[0:3]], axis=0)
    small_part = jnp.concatenate([small_part, jnp.broadcast_to(loss_acc[0, 0], small_part.shape)], axis=0)
    small_w = dict(norm_g=(norm_g, m_norm_g, v_norm_g), mem_norm_g=(mem_norm_g, m_mem_norm_g, v_mem_norm_g),
                   conv_w=(conv_w, m_conv_w, v_conv_w), final_g=(final_g, m_final_g, v_final_g))
    loss_tile, small_res = _sum_adamw_small(
        _all_gather_small(small_part, [big[n][1] for n in names1[1:]], "gather_small_grads"), ck_arr,
        {n: tuple(t.reshape(-1, t.shape[-1]) for t in wmv) for n, wmv in small_w.items()})
    loss = loss_tile[0, 0]
    for n, wmv in small_w.items():
        big[n] = tuple(t.reshape(wmv[0].shape) for t in small_res[n])

    _, from_chips0 = _split_wait(started0, _plan_to_chips, big["final_g"][1], "rs0_chips_wait")
    for n, (pf, _), r in zip(names0, parts0, from_chips0):
        big[n] = _finish(n, pf, r, *shard[n], big["final_g"][1])
    for n in ("attn_w_in", "attn_w_out", "conv_w_in", "conv_w_out"):
        big[n] = tuple(t[None] for t in big[n])
    big["w_mem_kv"] = tuple(t.reshape(w_mem_kv.shape) for t in big["w_mem_kv"])

    order = ["norm_g", "mem_norm_g", "w_mem_kv", "attn_w_in", "attn_w_out", "conv_w_in", "conv_w", "conv_w_out", "final_g"]
    return (loss, dx[None], *[big[n][0] for n in order], *[big[n][1] for n in order],
            *[big[n][2] for n in order], *[big[n][3] for n in order])
```

```python
import jax
import jax.numpy as jnp
from jax import lax
from jax.experimental import pallas as pl
from jax.experimental.pallas import tpu as pltpu

F32 = jnp.float32
BF16 = jnp.bfloat16

N_DEV = 8
D_MODEL = 1024
HEAD_DIM = 64
ROT_DIM = HEAD_DIM // 4
ROPE_THETA = 500000.0
DILATIONS = (1, 4, 16)
HEADS_PER_GROUP = 8
GROUP_WIDTH = HEADS_PER_GROUP * HEAD_DIM
BLOCK = 128
N_MEM = 256
MEM_HEADS = 4
MEM_WIDTH = MEM_HEADS * HEAD_DIM
CONV_WIDTH = D_MODEL
EPS = 1e-6
SCALE = HEAD_DIM ** -0.5
NEG = -1e30

ADAM_LR = 0.001
ADAM_B1 = 0.9
ADAM_B2 = 0.999
ADAM_EPS = 1e-08
ADAM_WD = 0.01
ADAM_STEP = 10

ROW_TILE = 256
WGRAD_SHARDS = 4
LANES = 128
MESH = pl.DeviceIdType.MESH
ANY = pl.BlockSpec(memory_space=pl.ANY)


def _pallas_call(body, **kw):
    call = pl.pallas_call(body, **kw)

    def run(*args):
        pinned = [pltpu.with_memory_space_constraint(a, pltpu.HBM) if jnp.issubdtype(a.dtype, jnp.floating) else a
                  for a in args]
        return call(*pinned)

    return run


def _dot(a, b):
    return lax.dot_general(a, b, (((1,), (0,)), ((), ())), preferred_element_type=F32)


def _dot_nt(a, b):
    return lax.dot_general(a, b, (((1,), (1,)), ((), ())), preferred_element_type=F32)


def _dot_tn(a, b):
    return lax.dot_general(a, b, (((0,), (0,)), ((), ())), preferred_element_type=F32)


def _params(n_grid, vmem_mb=48):
    return pltpu.CompilerParams(dimension_semantics=("arbitrary",) * n_grid, vmem_limit_bytes=vmem_mb << 20)


def _rows(width, tm=ROW_TILE):
    return pl.BlockSpec((tm, width), lambda i: (i, 0))


def _view_rows(width, d, tm=ROW_TILE):
    return pl.BlockSpec((tm // d, d * width), lambda i: (i, 0))


def _whole(shape):
    return pl.BlockSpec(shape, lambda *_: (0,) * len(shape))


def _resident(shape):
    return pl.BlockSpec(shape, lambda *_: (0,) * len(shape), pipeline_mode=pl.Buffered(1))


def _sds(shape, dtype):
    return pltpu.HBM(shape, dtype)


def _plain(shape, dtype):
    return jax.ShapeDtypeStruct(shape, dtype)


def _silu_parts(z):
    sg = jax.nn.sigmoid(z)
    return z * sg, sg * (1.0 + z * (1.0 - sg))


def _to_view(scr, val, out_ref, d):
    tm, w = val.shape
    if d == 1:
        out_ref[...] = val.astype(out_ref.dtype)
        return
    for cb in range(w // LANES):
        scr[cb] = val[:, cb * LANES:(cb + 1) * LANES]
    for r in range(d):
        for cb in range(w // LANES):
            lo = r * w + cb * LANES
            out_ref[:, lo:lo + LANES] = scr[cb, pl.ds(r, tm // d, stride=d), :].astype(out_ref.dtype)


def _from_view(scr, in_ref, d):
    if d == 1:
        return in_ref[...].astype(F32)
    nc, tm, _ = scr.shape
    w = nc * LANES
    for r in range(d):
        for cb in range(nc):
            lo = r * w + cb * LANES
            scr[cb, pl.ds(r, tm // d, stride=d), :] = in_ref[:, lo:lo + LANES].astype(F32)
    return jnp.concatenate([scr[cb] for cb in range(nc)], axis=1)


def _view_scratch(width, tm=ROW_TILE):
    return pltpu.VMEM((width // LANES, tm, LANES), F32)


def _rope_tables(pos):
    half = ROT_DIM // 2
    inv_freq = ROPE_THETA ** (-jnp.arange(half, dtype=F32) * (2.0 / ROT_DIM))
    ang = pos.astype(F32)[:, None] * inv_freq
    cos, sin = jnp.cos(ang), jnp.sin(ang)
    s = pos.shape[0]
    z8 = jnp.zeros((s, half), F32)
    rest = HEAD_DIM - ROT_DIM
    cosf = jnp.concatenate([cos, cos, jnp.ones((s, rest), F32)], axis=1)
    sa = jnp.concatenate([-sin, z8, jnp.zeros((s, rest), F32)], axis=1)
    sb = jnp.concatenate([z8, sin, jnp.zeros((s, rest), F32)], axis=1)
    return tuple(jnp.tile(t, (1, LANES // HEAD_DIM)) for t in (cosf, sa, sb))


def _rope_fwd(t, cv, sav, sbv):
    w = t.shape[1]
    return t * cv + pltpu.roll(t, w - ROT_DIM // 2, 1) * sav + pltpu.roll(t, ROT_DIM // 2, 1) * sbv


def _rope_bwd(g, cv, sav, sbv):
    w = g.shape[1]
    return g * cv + pltpu.roll(g * sav, ROT_DIM // 2, 1) + pltpu.roll(g * sbv, w - ROT_DIM // 2, 1)


def _joined_columns(wg_ref):
    assert wg_ref.shape[2] % LANES == 0
    return jnp.concatenate([wg_ref[j] for j in range(N_DEV)], axis=1)


def _join_once(wg_ref, w_scr):
    c = wg_ref.shape[2]

    @pl.when(pl.program_id(0) == 0)
    def _():
        for j in range(N_DEV):
            w_scr[:, j * c:(j + 1) * c] = wg_ref[j]


def _inproj_attn(x, g, wg, tabs, token):
    s, d_model = x.shape
    gw = GROUP_WIDTH
    n = N_DEV * wg.shape[2]
    nz = n - 9 * gw - MEM_WIDTH
    reps = gw // LANES
    tm = ROW_TILE

    def body(_, x_ref, g_ref, w_ref, c_ref, sa_ref, sb_ref, hn_ref, *rest):
        outs, (wj, scr, tscr) = rest[:-3], rest[-3:]
        q_refs, k_refs, v_refs, t_refs, qm_ref, z_ref = outs[0:3], outs[3:6], outs[6:9], outs[9:18], outs[18], outs[19]
        _join_once(w_ref, wj)
        xb = x_ref[...]
        r = lax.rsqrt(jnp.mean(xb * xb, axis=-1, keepdims=True) + EPS)
        hn = ((xb * r) * g_ref[...]).astype(BF16)
        hn_ref[...] = hn
        proj = lambda lo, hi: _dot(hn, wj[:, lo:hi])
        tab = (c_ref[...], sa_ref[...], sb_ref[...])
        cv, sav, sbv = [jnp.tile(t, (1, reps)) for t in tab]
        for j, d in enumerate(DILATIONS):
            tq = _rope_fwd(proj(j * gw, (j + 1) * gw), cv, sav, sbv)
            _to_view(scr, tq * SCALE, q_refs[j], d)
            tk = _rope_fwd(proj((3 + j) * gw, (4 + j) * gw), cv, sav, sbv)
            _to_view(scr, tk, k_refs[j], d)
            _to_view(scr, proj((6 + j) * gw, (7 + j) * gw), v_refs[j], d)
            for i in range(3):
                _to_view(tscr, tab[i], t_refs[3 * j + i], d)
        qm_ref[...] = proj(9 * gw, 9 * gw + MEM_WIDTH).astype(BF16)
        z_ref[...] = proj(9 * gw + MEM_WIDTH, n)

    views = [_sds((s // d, d * gw), BF16) for d in DILATIONS]
    tviews = [_sds((s // d, d * LANES), F32) for d in DILATIONS for _ in range(3)]
    out_shape = [_sds((s, d_model), BF16)] + views * 3 + tviews + [_sds((s, MEM_WIDTH), BF16), _sds((s, nz), F32)]
    vspecs = [_view_rows(gw, d, tm) for d in DILATIONS]
    tspecs = [_view_rows(LANES, d, tm) for d in DILATIONS for _ in range(3)]
    out_specs = [_rows(d_model, tm)] + vspecs * 3 + tspecs + [_rows(MEM_WIDTH, tm), _rows(nz, tm)]
    res = _pallas_call(
        body, name="inproj_attn", grid=(s // tm,), out_shape=out_shape,
        in_specs=[ANY, _rows(d_model, tm), _whole((1, d_model)), _resident(wg.shape)] + [_rows(LANES, tm)] * 3,
        out_specs=out_specs,
        scratch_shapes=[pltpu.VMEM((d_model, n), BF16), _view_scratch(gw, tm), _view_scratch(LANES, tm)],
        compiler_params=_params(1, 60),
    )(token, x, g, wg, *tabs)
    tabs_v = [res[10 + 3 * j:13 + 3 * j] for j in range(3)]
    return res[0], res[1:4], res[4:7], res[7:10], tabs_v, res[19], res[20]


def _band_mask(n_keys):
    qi = lax.broadcasted_iota(jnp.int32, (BLOCK, n_keys), 0)
    kj = lax.broadcasted_iota(jnp.int32, (BLOCK, n_keys), 1)
    if n_keys == BLOCK:
        return kj <= qi
    return jnp.logical_or(jnp.logical_and(kj < BLOCK, kj >= qi), jnp.logical_and(kj >= BLOCK, (kj - BLOCK) <= qi))


def _low_head_lanes():
    return lax.broadcasted_iota(jnp.int32, (1, LANES), 1) < HEAD_DIM


def _split_pair(t, low):
    zero = jnp.zeros_like(t)
    return jnp.where(low, t, zero), jnp.where(low, zero, t)


def _pair_specs(d, nb, w):
    if nb == 1:
        return pl.BlockSpec((BLOCK, 2 * w), lambda n: (0, n)), None
    half = nb // 2
    two = pl.BlockSpec((2 * BLOCK, w), lambda n: (n % half, n // half))
    before = pl.BlockSpec((BLOCK, w), lambda n: (jnp.maximum(2 * (n % half) - 1, 0), n // half))
    return two, before


def _head_tiles(w, col0):
    return ([slice(p * LANES, (p + 1) * LANES) for p in range(w // LANES)],
            [slice(col0 + p * LANES, col0 + (p + 1) * LANES) for p in range(w // LANES)])


def _attend_fwd(q_ref, o_ref, lse_ref, rows, col0, kk, vv):
    w = kk.shape[1]
    valid = _band_mask(kk.shape[0])
    low = _low_head_lanes()
    pairs, qcols = _head_tiles(w, col0)
    qs_ = [h for qc in qcols for h in _split_pair(q_ref[rows, qc], low)]
    k2s = [kk[:, pr] for pr in pairs for _ in range(2)]
    scs = [jnp.where(valid, _dot_nt(qh, k2), NEG) for qh, k2 in zip(qs_, k2s)]
    ms = [jnp.max(sc, axis=-1, keepdims=True) for sc in scs]
    ps = [jnp.exp(sc - m) for sc, m in zip(scs, ms)]
    ls = [jnp.sum(p, axis=-1, keepdims=True) for p in ps]
    pns = [(p * (1.0 / l)).astype(BF16) for p, l in zip(ps, ls)]
    for i, (pr, qc) in enumerate(zip(pairs, qcols)):
        v2 = vv[:, pr]
        a, b = 2 * i, 2 * i + 1
        o_ref[rows, qc] = jnp.where(low, _dot(pns[a], v2), _dot(pns[b], v2))
        lse_ref[rows, qc] = jnp.where(low, ms[a] + jnp.log(ls[a]), ms[b] + jnp.log(ls[b]))


TOP, BOTTOM = slice(0, BLOCK), slice(BLOCK, 2 * BLOCK)


def _attn_fwd(q, k, v, d, token):
    ln, dw = q.shape
    w = dw // d
    nb = ln // BLOCK
    two, before = _pair_specs(d, nb, w)

    def body_streams(_, q_ref, kc_ref, vc_ref, o_ref, lse_ref):
        for sb in range(2):
            cols = slice(sb * w, (sb + 1) * w)
            _attend_fwd(q_ref, o_ref, lse_ref, TOP, sb * w, kc_ref[:, cols], vc_ref[:, cols])

    def body_blocks(_, q_ref, kp_ref, kc_ref, vp_ref, vc_ref, o_ref, lse_ref):
        first = pl.program_id(0) % (nb // 2) == 0
        pl.when(first)(lambda: _attend_fwd(q_ref, o_ref, lse_ref, TOP, 0, kc_ref[TOP, :], vc_ref[TOP, :]))
        pl.when(jnp.logical_not(first))(lambda: _attend_fwd(
            q_ref, o_ref, lse_ref, TOP, 0, jnp.concatenate([kp_ref[...], kc_ref[TOP, :]], axis=0),
            jnp.concatenate([vp_ref[...], vc_ref[TOP, :]], axis=0)))
        _attend_fwd(q_ref, o_ref, lse_ref, BOTTOM, 0, kc_ref[...], vc_ref[...])

    if nb == 1:
        body, in_specs, args = body_streams, [ANY, two, two, two], (token, q, k, v)
    else:
        body, in_specs, args = body_blocks, [ANY, two, before, two, before, two], (token, q, k, k, v, v)
    return _pallas_call(
        body, name=f"attn_fwd_d{d}", grid=(d * nb // 2,), out_shape=[_sds((ln, dw), F32)] * 2,
        in_specs=in_specs, out_specs=[two, two], compiler_params=_params(1, 32),
    )(*args)


def _memkv_fwd(mem, g, w):
    n_layers = g.shape[0]
    rows = D_MODEL // N_DEV

    def body(mem_ref, g_ref, w_ref, *kv_refs):
        mb = mem_ref[...]
        r = lax.rsqrt(jnp.mean(mb * mb, axis=-1, keepdims=True) + EPS)
        mn = ((mb * r) * g_ref[...]).astype(BF16)
        kv = _dot(mn, w_ref[...].reshape(D_MODEL, 2 * MEM_WIDTH)).astype(BF16)
        for layer, kv_ref in enumerate(kv_refs):
            @pl.when(pl.program_id(0) == layer)
            def _(kv_ref=kv_ref):
                kv_ref[...] = kv

    return _pallas_call(
        body, name="memkv_fwd", grid=(n_layers,),
        out_shape=[_sds((N_MEM, 2 * MEM_WIDTH), BF16)] * n_layers,
        in_specs=[_whole(mem.shape), pl.BlockSpec((None, 1, D_MODEL), lambda l: (l, 0, 0)),
                  pl.BlockSpec((N_DEV, rows, 2 * MEM_WIDTH), lambda l: (0, l, 0))],
        out_specs=[_whole((N_MEM, 2 * MEM_WIDTH))] * n_layers,
        compiler_params=_params(1, 32),
    )(mem, g.reshape(n_layers, 1, D_MODEL), w)


def _mix_groups(os_, ls_):
    mx = jnp.maximum(jnp.maximum(ls_[0], ls_[1]), ls_[2])
    es = [jnp.exp(t - mx) for t in ls_]
    inv = 1.0 / (es[0] + es[1] + es[2])
    ws = [e * inv for e in es]
    mix = ws[0] * os_[0] + ws[1] * os_[1] + ws[2] * os_[2]
    return ws, mix


MEM_PAIRS = [slice(p * LANES, (p + 1) * LANES) for p in range(MEM_WIDTH // LANES)]


def _mem_probs(qhs, k2s):
    scs = [_dot_nt(qh, k2) * SCALE for qh, k2 in zip(qhs, k2s)]
    es = [jnp.exp(sc - jnp.max(sc, axis=-1, keepdims=True)) for sc in scs]
    return [e * (1.0 / jnp.sum(e, axis=-1, keepdims=True)) for e in es]


def _mem_attn_into(qm, kv_ref, mo_ref):
    low = _low_head_lanes()
    qhs = [h for pr in MEM_PAIRS for h in _split_pair(qm[:, pr], low)]
    k2s = [kv_ref[:, pr] for pr in MEM_PAIRS for _ in range(2)]
    ps = [p.astype(BF16) for p in _mem_probs(qhs, k2s)]
    for i, pr in enumerate(MEM_PAIRS):
        v2 = kv_ref[:, MEM_WIDTH + i * LANES:MEM_WIDTH + (i + 1) * LANES]
        mo_ref[:, pr] = jnp.where(low, _dot(ps[2 * i], v2), _dot(ps[2 * i + 1], v2))


def _mem_attn_bwd(qm, kv_ref, dmem, dqm_ref, dkv_ref):
    low = _low_head_lanes()
    dmb = dmem.astype(BF16)
    vps = [slice(MEM_WIDTH + i * LANES, MEM_WIDTH + (i + 1) * LANES) for i in range(len(MEM_PAIRS))]
    qhs = [h for pr in MEM_PAIRS for h in _split_pair(qm[:, pr], low)]
    dhs = [h for pr in MEM_PAIRS for h in _split_pair(dmb[:, pr], low)]
    k2s = [kv_ref[:, pr] for pr in MEM_PAIRS for _ in range(2)]
    v2s = [kv_ref[:, vp] for vp in vps for _ in range(2)]
    ps = _mem_probs(qhs, k2s)
    dps = [_dot_nt(dh, v2) for dh, v2 in zip(dhs, v2s)]
    dss = [(p * (dp - jnp.sum(dp * p, axis=-1, keepdims=True)) * SCALE).astype(BF16) for p, dp in zip(ps, dps)]
    pbs = [p.astype(BF16) for p in ps]
    for i, (pr, vp) in enumerate(zip(MEM_PAIRS, vps)):
        a, b = 2 * i, 2 * i + 1
        dqm_ref[:, pr] = jnp.where(low, _dot(dss[a], k2s[a]), _dot(dss[b], k2s[b])).astype(BF16)
        dkv_ref[:, pr] += _dot_tn(dss[a], qhs[a]) + _dot_tn(dss[b], qhs[b])
        dkv_ref[:, vp] += _dot_tn(pbs[a], dhs[a]) + _dot_tn(pbs[b], dhs[b])


def _post_attn(os_, ls_, qm, kv, z, x, wg_out):
    s, d_model = x.shape
    gw = GROUP_WIDTH
    nb = gw + MEM_WIDTH
    tm = ROW_TILE

    def body(o0, o1, o2, l0, l1, l2, qm_ref, kv_ref, z_ref, x_ref, w_ref, y_ref, yt_ref, mo_ref, h_ref, s0, s1):
        ov, lv = [], []
        for o_ref, l_ref, d in zip((o0, o1, o2), (l0, l1, l2), DILATIONS):
            ov.append(_from_view(s0, o_ref, d))
            lv.append(_from_view(s1, l_ref, d))
        _, mix = _mix_groups(ov, lv)
        _mem_attn_into(qm_ref[...], kv_ref, mo_ref)
        sz, _ = _silu_parts(z_ref[...])
        y_ref[:, :gw] = (mix * sz[:, :gw]).astype(BF16)
        y_ref[:, gw:] = (mo_ref[...] * sz[:, gw:]).astype(BF16)
        y = y_ref[...]
        yt_ref[...] = y.T
        h_ref[...] = x_ref[...] + _dot(y, _joined_columns(w_ref))

    vspecs = [_view_rows(gw, d) for d in DILATIONS]
    return _pallas_call(
        body, name="post_attn", grid=(s // tm,),
        out_shape=[_sds((s, nb), BF16), _sds((nb, s), BF16), _sds((s, MEM_WIDTH), F32), _sds((s, d_model), F32)],
        in_specs=vspecs * 2 + [_rows(MEM_WIDTH), _whole(kv.shape), _rows(nb), _rows(d_model), _whole(wg_out.shape)],
        out_specs=[_rows(nb), pl.BlockSpec((nb, tm), lambda i: (0, i)), _rows(MEM_WIDTH), _rows(d_model)],
        scratch_shapes=[_view_scratch(gw), _view_scratch(gw)],
        compiler_params=_params(1, 40),
    )(*os_, *ls_, qm, kv, z, x, wg_out)


def _inproj_conv(x, g, wg):
    s, d_model = x.shape
    c = CONV_WIDTH
    n = N_DEV * wg.shape[2]
    nz = n - 3 * c - MEM_WIDTH
    tm = ROW_TILE

    def body(x_ref, g_ref, w_ref, hn_ref, bg_ref, cg_ref, u_ref, qm_ref, z_ref, wj):
        _join_once(w_ref, wj)
        xb = x_ref[...]
        r = lax.rsqrt(jnp.mean(xb * xb, axis=-1, keepdims=True) + EPS)
        hn = ((xb * r) * g_ref[...]).astype(BF16)
        hn_ref[...] = hn
        bg_ref[...] = _dot(hn, wj[:, 0:c])
        cg_ref[...] = _dot(hn, wj[:, c:2 * c])
        u_ref[...] = _dot(hn, wj[:, 2 * c:3 * c])
        qm_ref[...] = _dot(hn, wj[:, 3 * c:3 * c + MEM_WIDTH]).astype(BF16)
        z_ref[...] = _dot(hn, wj[:, 3 * c + MEM_WIDTH:])

    return _pallas_call(
        body, name="inproj_conv", grid=(s // tm,),
        out_shape=[_sds((s, d_model), BF16)] + [_sds((s, c), F32)] * 3 + [_sds((s, MEM_WIDTH), BF16), _sds((s, nz), F32)],
        in_specs=[_rows(d_model), _whole((1, d_model)), _resident(wg.shape)],
        out_specs=[_rows(d_model)] + [_rows(c)] * 3 + [_rows(MEM_WIDTH), _rows(nz)],
        scratch_shapes=[pltpu.VMEM((d_model, n), BF16)],
        compiler_params=_params(1, 60),
    )(x, g, wg)


HALO = 8


def _halo_before(width, tm=ROW_TILE):
    return pl.BlockSpec((HALO, width), lambda i: (jnp.maximum(i * (tm // HALO) - 1, 0), 0))


def _halo_after(width, n_rows, tm=ROW_TILE):
    return pl.BlockSpec((HALO, width), lambda i: (jnp.minimum((i + 1) * (tm // HALO), n_rows // HALO - 1), 0))


def _conv_taps(cg_ref, u_ref, cgh_ref, uh_ref, i):
    a = cg_ref[...] * u_ref[...]
    ah = jnp.where(i > 0, cgh_ref[...] * uh_ref[...], 0.0)
    row = lax.broadcasted_iota(jnp.int32, a.shape, 0)
    a1 = jnp.where(row == 0, ah[HALO - 1:HALO], pltpu.roll(a, 1, 0))
    a2 = jnp.where(row == 0, ah[HALO - 2:HALO - 1], jnp.where(row == 1, ah[HALO - 1:HALO], pltpu.roll(a, 2, 0)))
    return a, a1, a2


def _post_conv_loss(bg, cg, u, qm, kv, z, h1, w_out, cw, gf, tgt):
    s, d = h1.shape
    c = CONV_WIDTH
    nb = c + MEM_WIDTH
    tm = ROW_TILE

    def body(bg_ref, cg_ref, u_ref, cgh_ref, uh_ref, qm_ref, kv_ref, z_ref, h_ref, w_ref, cw_ref, gf_ref, t_ref,
             y_ref, yt_ref, mo_ref, dh_ref, dhb_ref, loss_ref, dgf_ref):
        i = pl.program_id(0)
        a, a1, a2 = _conv_taps(cg_ref, u_ref, cgh_ref, uh_ref, i)
        conv = cw_ref[0:1, :] * a2 + cw_ref[1:2, :] * a1 + cw_ref[2:3, :] * a
        mix = bg_ref[...] * conv
        _mem_attn_into(qm_ref[...], kv_ref, mo_ref)
        sz, _ = _silu_parts(z_ref[...])
        y_ref[:, :c] = (mix * sz[:, :c]).astype(BF16)
        y_ref[:, c:] = (mo_ref[...] * sz[:, c:]).astype(BF16)
        y = y_ref[...]
        yt_ref[...] = y.T
        h2 = h_ref[...] + _dot(y, w_ref[...])
        r = lax.rsqrt(jnp.mean(h2 * h2, axis=-1, keepdims=True) + EPS)
        nh = h2 * r
        gfv = gf_ref[...]
        diff = nh * gfv - t_ref[...]
        dout = diff * (1.0 / d)
        dn = dout * gfv
        dh2 = r * dn - h2 * ((r * r * r) * jnp.mean(dn * h2, axis=-1, keepdims=True))
        dh_ref[...] = dh2
        dhb_ref[...] = dh2.astype(BF16)

        @pl.when(i == 0)
        def _():
            loss_ref[...] = jnp.zeros_like(loss_ref)
            dgf_ref[...] = jnp.zeros_like(dgf_ref)

        loss_ref[...] += 0.5 * jnp.sum(jnp.mean(diff * diff, axis=-1, keepdims=True))
        dgf_ref[...] += jnp.sum(dout * nh, axis=0, keepdims=True)

    return _pallas_call(
        body, name="post_conv_loss", grid=(s // tm,),
        out_shape=[_sds((s, nb), BF16), _sds((nb, s), BF16), _sds((s, MEM_WIDTH), F32), _sds((s, d), F32),
                   _sds((s, d), BF16), _plain((8, LANES), F32), _plain((1, d), F32)],
        in_specs=[_rows(c)] * 3 + [_halo_before(c)] * 2 + [_rows(MEM_WIDTH), _whole(kv.shape), _rows(nb), _rows(d),
                  _whole(w_out.shape), _whole(cw.shape), _whole((1, d)), _rows(d)],
        out_specs=[_rows(nb), pl.BlockSpec((nb, tm), lambda i: (0, i)), _rows(MEM_WIDTH), _rows(d), _rows(d),
                   _whole((8, LANES)), _whole((1, d))],
        compiler_params=_params(1, 48),
    )(bg, cg, u, cg, u, qm, kv, z, h1, w_out, cw, gf, tgt)


def _bwd_post_conv(dhb, w_out, bg, cg, u, z, qm, kv, mo, cw):
    s = dhb.shape[0]
    c = CONV_WIDTH
    nb = c + MEM_WIDTH

    def body(dh_ref, w_ref, bg_ref, cg_ref, u_ref, cgh_ref, uh_ref, z_ref, qm_ref, kv_ref, mo_ref, cw_ref,
             dz_ref, dbg_ref, dc_ref, dqm_ref, dkv_ref):
        i = pl.program_id(0)

        @pl.when(i == 0)
        def _():
            dkv_ref[...] = jnp.zeros_like(dkv_ref)

        dy = _dot_nt(dh_ref[...], w_ref[...])
        sz, dsz = _silu_parts(z_ref[...])
        a, a1, a2 = _conv_taps(cg_ref, u_ref, cgh_ref, uh_ref, i)
        conv = cw_ref[0:1, :] * a2 + cw_ref[1:2, :] * a1 + cw_ref[2:3, :] * a
        bgv = bg_ref[...]
        dz_ref[:, :c] = (dy[:, :c] * (bgv * conv) * dsz[:, :c]).astype(BF16)
        dz_ref[:, c:] = (dy[:, c:] * mo_ref[...] * dsz[:, c:]).astype(BF16)
        dbr = dy * sz
        dmix = dbr[:, :c]
        dbg_ref[...] = (dmix * conv).astype(BF16)
        dc_ref[...] = dmix * bgv
        _mem_attn_bwd(qm_ref[...], kv_ref, dbr[:, c:], dqm_ref, dkv_ref)

    return _pallas_call(
        body, name="bwd_post_conv", grid=(s // ROW_TILE,),
        out_shape=[_sds((s, nb), BF16), _sds((s, c), BF16), _sds((s, c), F32), _sds((s, MEM_WIDTH), BF16),
                   _sds(kv.shape, F32)],
        in_specs=[_rows(D_MODEL), _whole(w_out.shape)] + [_rows(c)] * 3 + [_halo_before(c)] * 2
                 + [_rows(nb), _rows(MEM_WIDTH), _whole(kv.shape), _rows(MEM_WIDTH), _whole(cw.shape)],
        out_specs=[_rows(nb), _rows(c), _rows(c), _rows(MEM_WIDTH), _whole(kv.shape)],
        compiler_params=_params(1, 48),
    )(dhb, w_out, bg, cg, u, cg, u, z, qm, kv, mo, cw)


def _bwd_conv(dconv, cg, u, cw):
    s, c = dconv.shape
    tm = ROW_TILE
    last = s // tm - 1

    def body(dc_ref, dcn_ref, cg_ref, u_ref, cgh_ref, uh_ref, cw_ref, dcg_ref, du_ref, dcw_ref):
        i = pl.program_id(0)

        @pl.when(i == 0)
        def _():
            dcw_ref[...] = jnp.zeros_like(dcw_ref)

        dc = dc_ref[...]
        dcn = jnp.where(i < last, dcn_ref[...], 0.0)
        row = lax.broadcasted_iota(jnp.int32, dc.shape, 0)
        d1 = jnp.where(row == tm - 1, dcn[0:1], pltpu.roll(dc, tm - 1, 0))
        d2 = jnp.where(row == tm - 1, dcn[1:2], jnp.where(row == tm - 2, dcn[0:1], pltpu.roll(dc, tm - 2, 0)))
        da = cw_ref[2:3, :] * dc + cw_ref[1:2, :] * d1 + cw_ref[0:1, :] * d2
        a, a1, a2 = _conv_taps(cg_ref, u_ref, cgh_ref, uh_ref, i)
        dcg_ref[...] = (da * u_ref[...]).astype(BF16)
        du_ref[...] = (da * cg_ref[...]).astype(BF16)
        dcw_ref[0:1, :] += jnp.sum(dc * a2, axis=0, keepdims=True)
        dcw_ref[1:2, :] += jnp.sum(dc * a1, axis=0, keepdims=True)
        dcw_ref[2:3, :] += jnp.sum(dc * a, axis=0, keepdims=True)

    return _pallas_call(
        body, name="bwd_conv", grid=(s // tm,),
        out_shape=[_sds((s, c), BF16), _sds((s, c), BF16), _plain((8, c), F32)],
        in_specs=[_rows(c), _halo_after(c, s), _rows(c), _rows(c), _halo_before(c), _halo_before(c), _whole(cw.shape)],
        out_specs=[_rows(c), _rows(c), _whole((8, c))], compiler_params=_params(1, 40),
    )(dconv, dconv, cg, u, cg, u, cw)


def _assemble(p_refs, pieces, widths, dp, scr):
    off = 0
    for p_ref, (_, d), wd in zip(p_refs, pieces, widths):
        if d == 1:
            dp[:, off:off + wd] = p_ref[...]
        else:
            dp[:, off:off + wd] = _from_view(scr, p_ref, d).astype(BF16)
        off += wd


def _dgrad_norm(pieces, wg, h, g, dres, token, onward, name):
    s, d_model = h.shape
    n = N_DEV * wg.shape[2]
    tm = ROW_TILE
    widths = [p.shape[1] // d for p, d in pieces]
    assert sum(widths) == n
    n_p = len(pieces)

    def body(_, *refs):
        p_refs = refs[:n_p]
        w_ref, h_ref, g_ref, dr_ref, dh_ref, dg_ref = refs[n_p:n_p + 6]
        dp, scr, wj = refs[-3:]
        _join_once(w_ref, wj)

        @pl.when(pl.program_id(0) == 0)
        def _():
            dg_ref[...] = jnp.zeros_like(dg_ref)

        _assemble(p_refs, pieces, widths, dp, scr)
        dhn = _dot_nt(dp[...], wj[...])
        hb = h_ref[...]
        r = lax.rsqrt(jnp.mean(hb * hb, axis=-1, keepdims=True) + EPS)
        dg_ref[...] += jnp.sum(dhn * (hb * r), axis=0, keepdims=True)
        dn = dhn * g_ref[...]
        dh = dr_ref[...] + r * dn - hb * ((r * r * r) * jnp.mean(dn * hb, axis=-1, keepdims=True))
        dh_ref[...] = dh
        if onward:
            dhb_ref, dpt_ref = refs[n_p + 6:n_p + 8]
            dhb_ref[...] = dh.astype(BF16)
            dpt_ref[...] = dp[...].T

    p_specs = [_view_rows(wd, d) for (_, d), wd in zip(pieces, widths)]
    out_shape = [_plain((s, d_model), F32), _plain((1, d_model), F32)]
    out_specs = [_rows(d_model), _whole((1, d_model))]
    if onward:
        out_shape += [_sds((s, d_model), BF16), _sds((n, s), BF16)]
        out_specs += [_rows(d_model), pl.BlockSpec((n, tm), lambda i: (0, i))]
    return _pallas_call(
        body, name=name, grid=(s // tm,), out_shape=out_shape,
        in_specs=[ANY] + p_specs + [_resident(wg.shape), _rows(d_model), _whole((1, d_model)), _rows(d_model)],
        out_specs=out_specs,
        scratch_shapes=[pltpu.VMEM((tm, n), BF16), _view_scratch(GROUP_WIDTH), pltpu.VMEM((d_model, n), BF16)],
        compiler_params=_params(1, 60),
    )(token, *[p for p, _ in pieces], wg, h, g, dres)


def _assemble_dproj_t(pieces, n, name):
    tm = ROW_TILE
    widths = [p.shape[1] // d for p, d in pieces]
    assert sum(widths) == n
    s = pieces[0][0].shape[0] * pieces[0][1]
    n_p = len(pieces)

    def body(*refs):
        p_refs, (dpt_ref, dp, scr) = refs[:n_p], refs[n_p:]
        _assemble(p_refs, pieces, widths, dp, scr)
        dpt_ref[...] = dp[...].T

    return _pallas_call(
        body, name=name, grid=(s // tm,), out_shape=_sds((n, s), BF16),
        in_specs=[_view_rows(wd, d) for (_, d), wd in zip(pieces, widths)],
        out_specs=pl.BlockSpec((n, tm), lambda i: (0, i)),
        scratch_shapes=[pltpu.VMEM((tm, n), BF16), _view_scratch(GROUP_WIDTH)],
        compiler_params=_params(1, 40),
    )(*[p for p, _ in pieces])


def _wgrad_shards_t(dp_t, h, c, name):
    n, s = dp_t.shape
    d_model = h.shape[1]
    per_step = 2

    def body(a_ref, b_ref, o_ref):
        o_ref[...] = _dot(a_ref[...], b_ref[...]).astype(BF16).reshape(per_step, c, d_model)

    return _pallas_call(
        body, name=name, grid=(N_DEV // per_step,), out_shape=_sds((N_DEV, c, d_model), BF16),
        in_specs=[pl.BlockSpec((per_step * c, s), lambda j: (j, 0)), _resident(h.shape)],
        out_specs=pl.BlockSpec((per_step, c, d_model), lambda j: (j, 0, 0)), compiler_params=_params(1, 40),
    )(dp_t, h)


def _wgrad_cols(a_t, b, c, name):
    m, s = a_t.shape
    assert c % LANES == 0

    def body(a_ref, b_ref, o_ref):
        wide = _dot(a_ref[...], b_ref[...]).astype(BF16)
        for j in range(WGRAD_SHARDS):
            o_ref[j] = wide[:, j * c:(j + 1) * c]

    return _pallas_call(
        body, name=name, grid=(N_DEV // WGRAD_SHARDS,), out_shape=_sds((N_DEV, m, c), BF16),
        in_specs=[_whole(a_t.shape), pl.BlockSpec((s, WGRAD_SHARDS * c), lambda j: (0, j))],
        out_specs=pl.BlockSpec((WGRAD_SHARDS, m, c), lambda j: (j, 0, 0)), compiler_params=_params(1, 40),
    )(a_t, b)


def _wgrad_rows(a_t, b, name):
    m, s = a_t.shape
    n = b.shape[1]
    mr = m // N_DEV

    def body(a_ref, b_ref, o_ref):
        o_ref[...] = _dot(a_ref[...], b_ref[...]).astype(BF16).reshape(WGRAD_SHARDS, mr, n)

    return _pallas_call(
        body, name=name, grid=(N_DEV // WGRAD_SHARDS,), out_shape=_sds((N_DEV, mr, n), BF16),
        in_specs=[pl.BlockSpec((WGRAD_SHARDS * mr, s), lambda j: (j, 0)), _whole(b.shape)],
        out_specs=pl.BlockSpec((WGRAD_SHARDS, mr, n), lambda j: (j, 0, 0)), compiler_params=_params(1, 40),
    )(a_t, b)


def _memkv_bwd(dkvs, w, mem, g):
    n_layers = g.shape[0]
    rows = D_MODEL // N_DEV

    def body(dkv0_ref, dkv1_ref, w_ref, mem_ref, g_ref, dw_ref, dg_ref):
        mb = mem_ref[...]
        r = lax.rsqrt(jnp.mean(mb * mb, axis=-1, keepdims=True) + EPS)
        nm = mb * r
        mn = (nm * g_ref[...]).astype(BF16)
        dkvb = jnp.where(pl.program_id(0) == 0, dkv0_ref[...], dkv1_ref[...]).astype(BF16)
        dw_ref[...] = _dot_tn(mn, dkvb).astype(BF16).reshape(N_DEV, rows, 2 * MEM_WIDTH)
        dmn = _dot_nt(dkvb, w_ref[...].reshape(D_MODEL, 2 * MEM_WIDTH))
        dg_ref[...] = jnp.sum(dmn * nm, axis=0, keepdims=True)

    lay = lambda *shape: pl.BlockSpec((None,) + shape, lambda l: (l, 0, 0))
    major = pl.BlockSpec((N_DEV, rows, 2 * MEM_WIDTH), lambda l: (0, l, 0))
    return _pallas_call(
        body, name="memkv_bwd", grid=(n_layers,),
        out_shape=[_sds((N_DEV, n_layers * rows, 2 * MEM_WIDTH), BF16), _plain((n_layers, 1, D_MODEL), F32)],
        in_specs=[_whole(dkvs[0].shape), _whole(dkvs[1].shape), major, _whole(mem.shape), lay(1, D_MODEL)],
        out_specs=[major, lay(1, D_MODEL)],
        compiler_params=_params(1, 32),
    )(*dkvs, w, mem, g.reshape(n_layers, 1, D_MODEL))


def _bwd_post_attn(dhb, wg_out, z, os_, ls_, qm, kv, mo, head_ones, token):
    s = dhb.shape[0]
    gw = GROUP_WIDTH
    nb = gw + MEM_WIDTH
    tm = ROW_TILE

    def body(_, dh_ref, w_ref, z_ref, o0, o1, o2, l0, l1, l2, qm_ref, kv_ref, mo_ref, bd_ref,
             dz_ref, do0, do1, do2, dl0, dl1, dl2, dqm_ref, dkv_ref, s0, s1):
        @pl.when(pl.program_id(0) == 0)
        def _():
            dkv_ref[...] = jnp.zeros_like(dkv_ref)

        dy = _dot_nt(dh_ref[...], _joined_columns(w_ref))
        ov, lv = [], []
        for o_ref, l_ref, d in zip((o0, o1, o2), (l0, l1, l2), DILATIONS):
            ov.append(_from_view(s0, o_ref, d))
            lv.append(_from_view(s1, l_ref, d))
        ws, mix = _mix_groups(ov, lv)
        sz, dsz = _silu_parts(z_ref[...])
        dz_ref[:, :gw] = (dy[:, :gw] * mix * dsz[:, :gw]).astype(BF16)
        dz_ref[:, gw:] = (dy[:, gw:] * mo_ref[...] * dsz[:, gw:]).astype(BF16)
        dbr = dy * sz
        dmix = dbr[:, :gw]
        t = dmix * mix
        th = t.astype(BF16)
        tl = (t - th.astype(F32)).astype(BF16)
        rs = _dot(th, bd_ref[...]) + _dot(tl, bd_ref[...])
        for wg_, do_ref, dl_ref, d in zip(ws, (do0, do1, do2), (dl0, dl1, dl2), DILATIONS):
            _to_view(s0, wg_ * dmix, do_ref, d)
            _to_view(s1, wg_ * rs, dl_ref, d)
        _mem_attn_bwd(qm_ref[...], kv_ref, dbr[:, gw:], dqm_ref, dkv_ref)

    vspecs = [_view_rows(gw, d) for d in DILATIONS]
    return _pallas_call(
        body, name="bwd_post_attn", grid=(s // tm,),
        out_shape=[_sds((s, nb), BF16)] + [_sds((s // d, d * gw), BF16) for d in DILATIONS]
                  + [_sds((s // d, d * gw), F32) for d in DILATIONS] + [_sds((s, MEM_WIDTH), BF16), _sds(kv.shape, F32)],
        in_specs=[ANY, _rows(D_MODEL), _whole(wg_out.shape), _rows(nb)] + vspecs * 2
                 + [_rows(MEM_WIDTH), _whole(kv.shape), _rows(MEM_WIDTH), _whole(head_ones.shape)],
        out_specs=[_rows(nb)] + vspecs * 2 + [_rows(MEM_WIDTH), _whole(kv.shape)],
        scratch_shapes=[_view_scratch(gw), _view_scratch(gw)],
        compiler_params=_params(1, 48),
    )(token, dhb, wg_out, z, *os_, *ls_, qm, kv, mo, head_ones)


def _attn_bwd(q, k, v, lse, do, dl, tabs, d, token):
    ln, dw = q.shape
    w = dw // d
    nb = ln // BLOCK
    reps = w // LANES
    two, before = _pair_specs(d, nb, w)
    two_t, _ = _pair_specs(d, nb, LANES)

    def attend(q_ref, l_ref, do_ref, dl_ref, dqs, acck, accv, rows, col0, kk, vv, acc_rows):
        valid = _band_mask(kk.shape[0])
        low = _low_head_lanes()
        pairs, qcols = _head_tiles(w, col0)
        cols = [slice(col0 + h * HEAD_DIM, col0 + h * HEAD_DIM + 1) for h in range(HEADS_PER_GROUP)]
        qhs = [h for qc in qcols for h in _split_pair(q_ref[rows, qc], low)]
        dobs = [h for qc in qcols for h in _split_pair(do_ref[rows, qc], low)]
        k2s = [kk[:, pr] for pr in pairs for _ in range(2)]
        v2s = [vv[:, pr] for pr in pairs for _ in range(2)]
        scs = [jnp.where(valid, _dot_nt(qh, k2), NEG) for qh, k2 in zip(qhs, k2s)]
        dps = [_dot_nt(dob, v2) for dob, v2 in zip(dobs, v2s)]
        ps = [jnp.exp(sc - l_ref[rows, col]) for sc, col in zip(scs, cols)]
        dss = [(p * (dp - dl_ref[rows, col])).astype(BF16) for p, dp, col in zip(ps, dps, cols)]
        pbs = [p.astype(BF16) for p in ps]
        for i, qc in enumerate(qcols):
            a, b = 2 * i, 2 * i + 1
            dqs[rows, qc] = jnp.where(low, _dot(dss[a], k2s[a]), _dot(dss[b], k2s[b])) * SCALE
            acck[acc_rows, qc] += _dot_tn(dss[a], qhs[a]) + _dot_tn(dss[b], qhs[b])
            accv[acc_rows, qc] += _dot_tn(pbs[a], dobs[a]) + _dot_tn(pbs[b], dobs[b])

    def body_streams(_, q_ref, kc_ref, vc_ref, l_ref, do_ref, dl_ref, c_ref, sa_ref, sb_ref,
                     dq_ref, dk_ref, dv_ref, acck, accv, dqs):
        acck[...] = jnp.zeros_like(acck)
        accv[...] = jnp.zeros_like(accv)
        for sb in range(2):
            cols = slice(sb * w, (sb + 1) * w)
            attend(q_ref, l_ref, do_ref, dl_ref, dqs, acck, accv, TOP, sb * w, kc_ref[:, cols], vc_ref[:, cols], TOP)
        tabs2 = [jnp.concatenate([jnp.tile(r[:, sb * LANES:(sb + 1) * LANES], (1, reps)) for sb in range(2)], axis=1)
                 for r in (c_ref, sa_ref, sb_ref)]
        dq_ref[...] = _rope_bwd(dqs[...], *tabs2).astype(BF16)
        dk_ref[...] = _rope_bwd(acck[...], *tabs2).astype(BF16)
        dv_ref[...] = accv[...].astype(BF16)

    def body_blocks(_, q_ref, kp_ref, kc_ref, vp_ref, vc_ref, l_ref, do_ref, dl_ref, cq, saq, sbq, ck, sak, sbk,
                    dq_ref, dk_ref, dv_ref, acck, accv, dqs):
        i = pl.program_id(0) % (nb // 2)

        @pl.when(i == 0)
        def _():
            acck[...] = jnp.zeros_like(acck)
            accv[...] = jnp.zeros_like(accv)

        refs = (q_ref, l_ref, do_ref, dl_ref, dqs, acck, accv)
        pl.when(i == 0)(lambda: attend(*refs, TOP, 0, kc_ref[TOP, :], vc_ref[TOP, :], TOP))
        pl.when(i != 0)(lambda: attend(
            *refs, TOP, 0, jnp.concatenate([kp_ref[...], kc_ref[TOP, :]], axis=0),
            jnp.concatenate([vp_ref[...], vc_ref[TOP, :]], axis=0),
            pl.ds(pl.multiple_of((2 * i - 1) * BLOCK, BLOCK), 2 * BLOCK)))
        attend(*refs, BOTTOM, 0, kc_ref[...], vc_ref[...], pl.ds(pl.multiple_of(2 * i * BLOCK, BLOCK), 2 * BLOCK))
        tq = [jnp.tile(r[...], (1, reps)) for r in (cq, saq, sbq)]
        dq_ref[...] = _rope_bwd(dqs[...], *tq).astype(BF16)

        @pl.when(i == nb // 2 - 1)
        def _():
            for r0 in range(0, nb * BLOCK, 2 * BLOCK):
                rows = slice(r0, r0 + 2 * BLOCK)
                tk = [jnp.tile(r[rows, :], (1, reps)) for r in (ck, sak, sbk)]
                dk_ref[rows, :] = _rope_bwd(acck[rows, :], *tk).astype(BF16)
                dv_ref[rows, :] = accv[rows, :].astype(BF16)

    if nb == 1:
        body = body_streams
        in_specs = [ANY] + [two] * 6 + [two_t] * 3
        args = (token, q, k, v, lse, do, dl, *tabs)
        out_specs = [two, two, two]
        acc_shape = (BLOCK, 2 * w)
    else:
        body = body_blocks
        stream = pl.BlockSpec((nb * BLOCK, w), lambda n: (0, n // (nb // 2)))
        stream_t = pl.BlockSpec((nb * BLOCK, LANES), lambda n: (0, n // (nb // 2)))
        in_specs = [ANY, two, before, two, before, two, two, two, two] + [two_t] * 3 + [stream_t] * 3
        args = (token, q, k, k, v, v, lse, do, dl, *tabs, *tabs)
        out_specs = [two, stream, stream]
        acc_shape = (nb * BLOCK, w)
    return _pallas_call(
        body, name=f"attn_bwd_d{d}", grid=(d * nb // 2,), out_shape=[_sds((ln, dw), BF16)] * 3,
        in_specs=in_specs, out_specs=out_specs,
        scratch_shapes=[pltpu.VMEM(acc_shape, F32), pltpu.VMEM(acc_shape, F32), pltpu.VMEM(two.block_shape, F32)],
        compiler_params=_params(1, 48),
    )(*args)


def _position():
    return lax.axis_index("x"), lax.axis_index("y"), lax.axis_index("c")


def _all_gather_small(xs, afters, name):
    n_in = 1 + len(afters)

    def body(*refs):
        x_ref, out_ref = refs[0], refs[n_in]
        send_sems, recv_sems, local_sem = refs[n_in + 1:]
        x, y, c = _position()
        my_rows = out_ref.at[4 * x + 2 * y + c]
        mine = pltpu.make_async_copy(x_ref, my_rows, local_sem)
        mine.start()
        copies = [pltpu.make_async_remote_copy(
            src_ref=x_ref, dst_ref=my_rows, send_sem=send_sems.at[k], recv_sem=recv_sems.at[k],
            device_id=(1 - x if k & 4 else x, 1 - y if k & 2 else y, 1 - c if k & 1 else c), device_id_type=MESH)
            for k in range(1, N_DEV)]
        for cp in copies:
            cp.start()
        for cp in copies:
            cp.wait_recv()
        for cp in copies:
            cp.wait_send()
        mine.wait()

    return _pallas_call(
        body, name=name, out_shape=_sds((N_DEV,) + xs.shape, xs.dtype),
        in_specs=[ANY] * n_in, out_specs=ANY,
        scratch_shapes=[pltpu.SemaphoreType.DMA((N_DEV,)), pltpu.SemaphoreType.DMA((N_DEV,)), pltpu.SemaphoreType.DMA],
    )(xs, *afters)


GATHER_CHUNKS = 4


def _all_gather_relay(xs, later, name):
    n = len(later)

    def body(*refs):
        x_ref, f32_refs = refs[0], refs[1:1 + n]
        out_ref, land_refs = refs[1 + n], refs[2 + n:2 + 2 * n]
        send_sems, recv_sems, local_sem, prep_sems = refs[2 + 2 * n:6 + 2 * n]
        staged, rounded = refs[6 + 2 * n:6 + 3 * n], refs[6 + 3 * n:]
        x, y, c = _position()
        me, sibling = (x, y, c), (x, y, 1 - c)
        xn, yn, diag = (1 - x, y, c), (x, 1 - y, c), (1 - x, 1 - y, c)
        src_nb = (x + c * (1 - 2 * x), y + (1 - c) * (1 - 2 * y), c)
        dst_nb = (x + (1 - c) * (1 - 2 * x), y + c * (1 - 2 * y), c)

        def rows(dev, i):
            return out_ref.at[4 * dev[0] + 2 * dev[1] + dev[2], pl.ds(i * step, step)]

        def copy(k, i, block, to, own=False):
            return pltpu.make_async_remote_copy(
                src_ref=x_ref.at[pl.ds(i * step, step)] if own else rows(block, i), dst_ref=rows(block, i),
                send_sem=send_sems.at[k, i], recv_sem=recv_sems.at[k, i], device_id=to, device_id_type=MESH)

        mine = pltpu.make_async_copy(x_ref, out_ref.at[4 * x + 2 * y + c], local_sem)
        mine.start()
        sent = []
        for i in chunks:
            sent += [copy(1, i, me, xn, own=True), copy(2, i, me, yn, own=True), copy(0, i, me, sibling, own=True)]
        for cp in sent:
            cp.start()
        fetches = [pltpu.make_async_copy(f32_refs[a], staged[a], prep_sems.at[0, a]) for a in range(n)]
        for cp in fetches:
            cp.start()
        placed = []
        for a in range(n):
            fetches[a].wait()
            rounded[a][...] = staged[a][...].astype(BF16)
            placed.append(pltpu.make_async_copy(rounded[a], land_refs[a].at[4 * x + 2 * y + c], prep_sems.at[1, a]))
            placed[a].start()
        for i in chunks:
            copy(1, i, xn, me).wait_recv()
            copy(2, i, yn, me).wait_recv()
            onward = [copy(3, i, src_nb, dst_nb), copy(4, i, xn, sibling), copy(5, i, yn, sibling)]
            for cp in onward:
                cp.start()
            sent += onward
        for i in chunks:
            copy(3, i, diag, me).wait_recv()
            last = copy(6, i, diag, sibling)
            last.start()
            sent.append(last)
        for i in chunks:
            copy(0, i, sibling, me).wait_recv()
            for k, blk in ((4, (1 - x, y, 1 - c)), (5, (x, 1 - y, 1 - c)), (6, (1 - x, 1 - y, 1 - c))):
                copy(k, i, blk, me).wait_recv()
        for cp in sent:
            cp.wait_send()
        for cp in [mine] + placed:
            cp.wait()

    step = xs.shape[0] // GATHER_CHUNKS
    chunks = range(GATHER_CHUNKS)
    res = _pallas_call(
        body, name=name,
        out_shape=[_sds((N_DEV,) + xs.shape, xs.dtype)] + [_sds((N_DEV,) + t.shape, BF16) for t in later],
        in_specs=[ANY] * (1 + n), out_specs=[ANY] * (1 + n),
        scratch_shapes=[pltpu.SemaphoreType.DMA((7, GATHER_CHUNKS)), pltpu.SemaphoreType.DMA((7, GATHER_CHUNKS)),
                        pltpu.SemaphoreType.DMA, pltpu.SemaphoreType.DMA((2, n))]
                       + [pltpu.VMEM(t.shape, F32) for t in later] + [pltpu.VMEM(t.shape, BF16) for t in later],
        compiler_params=pltpu.CompilerParams(vmem_limit_bytes=32 << 20),
    )(xs, *later)
    return res[0], res[1:]


HBM_SPEC = pl.BlockSpec(memory_space=pltpu.HBM)
SEM_SPEC = pl.BlockSpec(memory_space=pltpu.SEMAPHORE)
EFFECT = pltpu.SideEffectType.DATAFLOW_SIDE_EFFECTING
def _plan_gather_own(src_refs, land_refs):
    x, y, c = _position()
    me = 4 * x + 2 * y + c
    peers = [(x, y, 1 - c), (1 - x, y, c), (x, 1 - y, c), (1 - x, 1 - y, c)]
    return [(land_refs[a].at[me], land_refs[a].at[me], (a, k), peer)
            for k, peer in enumerate(peers) for a in range(len(land_refs))]


def _plan_gather_pass(src_refs, land_refs):
    x, y, c = _position()
    chips = [(1 - x, y), (x, 1 - y), (1 - x, 1 - y)]
    return [(land_refs[a].at[4 * px + 2 * py + c], land_refs[a].at[4 * px + 2 * py + c], (a, j), (x, y, 1 - c))
            for j, (px, py) in enumerate(chips) for a in range(len(land_refs))]


def _plan_to_sibling(src_refs, land_refs):
    x, y, c = _position()
    return [(src_refs[a].at[2 * k + (1 - c)], land_refs[a].at[k], (a, k), (x, y, 1 - c))
            for k in range(4) for a in range(len(src_refs))]


def _plan_to_chips(src_refs, land_refs):
    x, y, c = _position()
    chips = [(1 - x, y), (x, 1 - y), (1 - x, 1 - y)]
    return [(src_refs[a].at[2 * px + py], land_refs[a].at[j], (a, j), (px, py, c))
            for j, (px, py) in enumerate(chips) for a in range(len(src_refs))]


def _split_start(srcs, lands, plan, n_sem, after, name):
    n_s, n_a = len(srcs), len(lands)
    n_b = n_s + n_a

    def body(*refs):
        src_refs, land_refs = refs[:n_s], refs[n_s:n_b]
        send_sems, recv_sems, token = refs[n_b + 1], refs[n_b + 2], refs[-1]
        for src, dst, (a, k), dev in plan(src_refs, land_refs):
            i = a * n_sem + k
            pltpu.make_async_remote_copy(src_ref=src, dst_ref=dst, send_sem=send_sems.at[i], recv_sem=recv_sems.at[i],
                                         device_id=dev, device_id_type=MESH).start()
        token[...] = jnp.zeros_like(token)

    bufs = list(srcs) + list(lands)
    res = pl.pallas_call(
        body, name=name,
        out_shape=(pltpu.SemaphoreType.DMA((n_a * n_sem,)), pltpu.SemaphoreType.DMA((n_a * n_sem,)),
                   *[pltpu.HBM(t.shape, t.dtype) for t in bufs], _plain((8, LANES), F32)),
        in_specs=[HBM_SPEC] * n_b + [ANY],
        out_specs=(SEM_SPEC, SEM_SPEC, *[HBM_SPEC] * n_b, pl.BlockSpec(memory_space=pltpu.VMEM)),
        input_output_aliases={i: 2 + i for i in range(n_b)},
        compiler_params=pltpu.CompilerParams(has_side_effects=EFFECT),
    )(*[pltpu.with_memory_space_constraint(t, pltpu.HBM) for t in bufs], after)
    return (res[0], res[1], res[2:2 + n_s], res[2 + n_s:2 + n_b]), res[-1]


def _split_wait(started, plan, after, name, first=0, n_sem=None):
    send_sems, recv_sems, srcs, lands = started
    n_s, n_a = len(srcs), len(lands)
    n_b = n_s + n_a
    n_sem = n_sem or send_sems.shape[0] // n_a

    def body(*refs):
        src_refs, land_refs = refs[:n_s], refs[n_s:n_b]
        s_sems, r_sems = refs[n_b], refs[n_b + 1]
        for src, dst, (a, k), dev in plan(src_refs, land_refs):
            i = (first + a) * n_sem + k
            cp = pltpu.make_async_remote_copy(src_ref=src, dst_ref=dst, send_sem=s_sems.at[i], recv_sem=r_sems.at[i],
                                              device_id=dev, device_id_type=MESH)
            cp.wait_send()
            cp.wait_recv()

    bufs = list(srcs) + list(lands)
    res = pl.pallas_call(
        body, name=name, out_shape=tuple(pltpu.HBM(t.shape, t.dtype) for t in bufs),
        in_specs=[HBM_SPEC] * n_b + [SEM_SPEC, SEM_SPEC, ANY],
        out_specs=tuple([HBM_SPEC] * n_b),
        input_output_aliases={i: i for i in range(n_b)},
        compiler_params=pltpu.CompilerParams(has_side_effects=EFFECT),
    )(*bufs, send_sems, recv_sems, after)
    return res[:n_s], res[n_s:]


SUBLANES = 8


def _row_tile(r):
    return max(t for t in range(SUBLANES, ROW_TILE + 1, SUBLANES) if r % t == 0)


def _rs_add_sibling(gps, recvs, ck_arr, name):
    n = len(gps)
    block = lambda k, ck: (k + ck[1] + 1) % 4

    def body(ck_ref, *refs):
        for g_ref, r_ref, pf_ref, pb_ref in zip(refs[:n], refs[n:2 * n], refs[2 * n::2], refs[2 * n + 1::2]):
            sm = g_ref[...].astype(F32) + r_ref[...].astype(F32)
            pf_ref[...] = sm
            pb_ref[...] = sm.astype(BF16)

    mine = lambda t: pl.BlockSpec((None,) + t.shape[1:], lambda k, ck: (2 * block(k, ck) + ck[0], 0, 0))
    one = lambda t: pl.BlockSpec((None,) + t.shape[1:], lambda k, ck: (block(k, ck), 0, 0))
    res = _pallas_call(
        body, name=name,
        grid_spec=pltpu.PrefetchScalarGridSpec(
            num_scalar_prefetch=1, grid=(4,),
            in_specs=[mine(t) for t in gps] + [one(t) for t in recvs],
            out_specs=[s for t in recvs for s in (pl.BlockSpec(t.shape[1:], lambda k, ck: (0, 0)), one(t))]),
        out_shape=[s for t in recvs for s in (_sds(t.shape[1:], F32), _sds(t.shape, BF16))],
        compiler_params=_params(1, 48),
    )(ck_arr, *gps, *recvs)
    return list(zip(res[0::2], res[1::2]))


def _adam_update(w, gv, m, v):
    nm = ADAM_B1 * m + (1.0 - ADAM_B1) * gv
    nv = ADAM_B2 * v + (1.0 - ADAM_B2) * (gv * gv)
    m_hat = nm / (1.0 - ADAM_B1 ** ADAM_STEP)
    v_hat = nv / (1.0 - ADAM_B2 ** ADAM_STEP)
    return -ADAM_LR * (m_hat / (jnp.sqrt(v_hat) + ADAM_EPS) + ADAM_WD * w), nm, nv


def _rs_finish_adamw(pf, recv, w, m, v, after, name):
    r, l = pf.shape
    tr = _row_tile(r)

    def body(_, p_ref, r_ref, w_ref, m_ref, v_ref, g_ref, d_ref, nm_ref, nv_ref):
        gv = ((p_ref[...] + r_ref[0].astype(F32)) + r_ref[1].astype(F32)) + r_ref[2].astype(F32)
        g_ref[...] = gv
        d_ref[...], nm_ref[...], nv_ref[...] = _adam_update(w_ref[...], gv, m_ref[...], v_ref[...])

    spec = pl.BlockSpec((tr, l), lambda i: (i, 0))
    return _pallas_call(
        body, name=name, grid=(r // tr,),
        in_specs=[ANY, spec, pl.BlockSpec((3, tr, l), lambda i: (0, i, 0)), spec, spec, spec], out_specs=[spec] * 4,
        out_shape=[_plain((r, l), F32)] * 4, compiler_params=_params(1, 32),
    )(after, pf, recv, w, m, v)


SMALL_ROWS = dict(norm_g=(0, 2), mem_norm_g=(2, 4), final_g=(4, 5), conv_w=(5, 8))
LOSS_ROWS = (8, 16)


def _sum_adamw_small(g, ck_arr, states):
    names = list(SMALL_ROWS)
    n_dev, n_rows, _ = g.shape

    def body(ck_ref, g_ref, gc_ref, *refs):
        ins, loss_ref, outs = refs[:3 * len(names)], refs[3 * len(names)], refs[3 * len(names) + 1:]

        def total(ref, lo, hi):
            acc = ref[0, lo:hi, :]
            for j in range(1, n_dev):
                acc = acc + ref[j, lo:hi, :]
            return acc

        loss_ref[...] = total(gc_ref, *LOSS_ROWS)
        for i, n in enumerate(names):
            gv = total(gc_ref if n == "conv_w" else g_ref, *SMALL_ROWS[n])
            w_ref, m_ref, v_ref = ins[3 * i:3 * i + 3]
            g_out, d_out, nm_out, nv_out = outs[4 * i:4 * i + 4]
            g_out[...] = gv
            d_out[...], nm_out[...], nv_out[...] = _adam_update(w_ref[...], gv, m_ref[...], v_ref[...])

    flat = [t for n in names for t in states[n]]
    mine = pl.BlockSpec((n_dev, n_rows, LANES), lambda i, ck: (0, 0, 2 * ck[1] + ck[0]))
    res = _pallas_call(
        body, name="sum_adamw_small",
        grid_spec=pltpu.PrefetchScalarGridSpec(
            num_scalar_prefetch=1, grid=(1,),
            in_specs=[_whole(g.shape), mine] + [_whole(t.shape) for t in flat],
            out_specs=[_whole((SUBLANES, LANES))] + [_whole(states[n][0].shape) for n in names for _ in range(4)]),
        out_shape=[_plain((SUBLANES, LANES), F32)] + [_plain(states[n][0].shape, F32) for n in names for _ in range(4)],
        compiler_params=_params(1, 32),
    )(ck_arr, g, g, *flat)
    return res[0], {n: tuple(res[1 + 4 * i:5 + 4 * i]) for i, n in enumerate(names)}


def _finish(name, pf, recv, w, m, v, after):
    if name in ("attn_w_in", "conv_w_in"):
        res = _rs_finish_adamw(pf, recv, w.T, m.T, v.T, after, "rs_finish_adamw_" + name)
        return tuple(t.T for t in res)
    return _rs_finish_adamw(pf, recv, w, m, v, after, "rs_finish_adamw_" + name)


def kernel(x, mem, positions, norm_g, mem_norm_g, w_mem_kv, attn_w_in, attn_w_out, conv_w_in, conv_w, conv_w_out, final_g, loss_target, m_norm_g, m_mem_norm_g, m_w_mem_kv, m_attn_w_in, m_attn_w_out, m_conv_w_in, m_conv_w, m_conv_w_out, m_final_g, v_norm_g, v_mem_norm_g, v_w_mem_kv, v_attn_w_in, v_attn_w_out, v_conv_w_in, v_conv_w, v_conv_w_out, v_final_g):
    px, py, pc = _position()
    me = 4 * px + 2 * py + pc
    ck_arr = jnp.stack([pc, 2 * px + py]).astype(jnp.int32)
    x, mem, pos, tgt = x[0], mem[0], positions[0], loss_target[0]

    wg_in0, (land_out0, land_kv, land_in1, land_out1) = _all_gather_relay(
        attn_w_in[0].astype(BF16),
        [attn_w_out[0], w_mem_kv.reshape(-1, w_mem_kv.shape[2]), conv_w_in[0], conv_w_out[0]], "gather_w_in0")

    def gather_pass(weights, after, name, first):
        _, lands = _split_wait(weights, _plan_gather_own, after, name + "_wait", first, 4)
        return _split_start([], lands, _plan_gather_pass, 3, after, name + "_pass_start")

    taps = jnp.pad(conv_w[0], ((0, 5), (0, 0)))
    land_taps = lax.dynamic_update_slice(lax.empty((N_DEV,) + taps.shape, taps.dtype), taps[None], (me, 0, 0))
    lands = [land_out0, land_kv, land_taps, land_in1, land_out1]
    (send_sems, recv_sems, _, lands), token = _split_start([], lands, _plan_gather_own, 4, wg_in0, "gather_late_start")
    rest0, conv_ws = (send_sems, recv_sems, [], lands[:3]), (send_sems, recv_sems, [], lands[3:])

    tabs = _rope_tables(pos)
    g0, g1 = norm_g[0:1], norm_g[1:2]

    hn0, qs, ks, vs, tabs_v, qm0, z0 = _inproj_attn(x, g0, wg_in0, tabs, token)
    os_, ls_ = [], []
    for j, d in enumerate(DILATIONS):
        if j == 2:
            rest0, token = gather_pass(rest0, ls_[1], "gather_rest", 0)
        o, l = _attn_fwd(qs[j], ks[j], vs[j], d, token)
        os_.append(o)
        ls_.append(l)

    conv_ws, token = gather_pass(conv_ws, ls_[2], "gather_conv", 3)
    _, (wg_out0, wg_kv, cw_all) = _split_wait(rest0, _plan_gather_pass, token, "gather_rest_pass_wait")
    cw = cw_all[:, 0:3].transpose(1, 0, 2).reshape(3, -1)
    kv = _memkv_fwd(mem, mem_norm_g, wg_kv)
    y0, y0_t, mo0, h1 = _post_attn(os_, ls_, qm0, kv[0], z0, x, wg_out0)
    _, (wg_in1, wg_out1) = _split_wait(conv_ws, _plan_gather_pass, h1, "gather_conv_pass_wait")
    w_out1 = wg_out1.reshape(-1, wg_out1.shape[2])

    hn1, bg, cg, u, qm1, z1 = _inproj_conv(h1, g1, wg_in1)
    y1, y1_t, mo1, dh2, dh2b, loss_acc, d_final_g = _post_conv_loss(
        bg, cg, u, qm1, kv[1], z1, h1, w_out1, cw, final_g.reshape(1, -1), tgt)

    d_w_out1 = _wgrad_rows(y1_t, dh2b, "wgrad_out1")
    dz1, dbg, dconv, dqm1, dkv1 = _bwd_post_conv(dh2b, w_out1, bg, cg, u, z1, qm1, kv[1], mo1, cw)
    dcg, du, dcw = _bwd_conv(dconv, cg, u, cw)
    dh1, dg1, dh1b, dproj1_t = _dgrad_norm([(dbg, 1), (dcg, 1), (du, 1), (dqm1, 1), (dz1, 1)], wg_in1, h1, g1, dh2,
                                           dh2, True, "dgrad_norm_conv")
    d_w_in1 = _wgrad_shards_t(dproj1_t, hn1, wg_in1.shape[2], "wgrad_in1")

    d_w_out0 = _wgrad_cols(y0_t, dh1b, wg_out0.shape[2], "wgrad_out0")

    gw = GROUP_WIDTH
    ones = (jnp.arange(gw)[:, None] // HEAD_DIM == jnp.arange(gw)[None, :] // HEAD_DIM).astype(BF16)
    res = _bwd_post_attn(dh1b, wg_out0, z0, os_, ls_, qm0, kv[0], mo0, ones, dg1)
    dz0, dos, dls, dqm0, dkv0 = res[0], res[1:4], res[4:7], res[7], res[8]
    d_w_kv, d_mem_g = _memkv_bwd([dkv0, dkv1], wg_kv, mem, mem_norm_g)

    names1 = ["conv_w_in", "conv_w_out", "attn_w_out", "w_mem_kv"]
    grads1 = [d_w_in1, d_w_out1, d_w_out0, d_w_kv]
    started, token = _split_start(grads1, [lax.empty((4,) + g.shape[1:], g.dtype) for g in grads1],
                                  _plan_to_sibling, 4, d_mem_g, "rs1_sibling_start")

    dqs, dks, dvs = [], [], []
    for j, d in enumerate(DILATIONS):
        if j == 1:
            grads1, from_sibling = _split_wait(started, _plan_to_sibling, dqs[0][0], "rs1_sibling_wait")
            parts1 = _rs_add_sibling(grads1, from_sibling, ck_arr, "rs1_add_sibling")
            pbs1 = [pb for _, pb in parts1]
            started, token = _split_start(pbs1, [lax.empty((3,) + p.shape[1:], p.dtype) for p in pbs1],
                                          _plan_to_chips, 3, dg1, "rs1_chips_start")
        dq, dk, dv = _attn_bwd(qs[j], ks[j], vs[j], ls_[j], dos[j], dls[j], tabs_v[j], d, token)
        dqs.append((dq, d))
        dks.append((dk, d))
        dvs.append((dv, d))
    pieces0 = dqs + dks + dvs + [(dqm0, 1), (dz0, 1)]
    dproj0_t = _assemble_dproj_t(pieces0, N_DEV * wg_in0.shape[2], "assemble_dproj_attn")
    d_w_in0 = _wgrad_shards_t(dproj0_t, hn0, wg_in0.shape[2], "wgrad_in0")

    names0 = ["attn_w_in"]
    grads0 = [d_w_in0]
    started0, token0 = _split_start(grads0, [lax.empty((4,) + g.shape[1:], g.dtype) for g in grads0],
                                    _plan_to_sibling, 4, dg1, "rs0_sibling_start")
    _, from_chips1 = _split_wait(started, _plan_to_chips, token0, "rs1_chips_wait")
    shard = dict(attn_w_in=(attn_w_in[0], m_attn_w_in[0], v_attn_w_in[0]),
                 attn_w_out=(attn_w_out[0], m_attn_w_out[0], v_attn_w_out[0]),
                 conv_w_in=(conv_w_in[0], m_conv_w_in[0], v_conv_w_in[0]),
                 conv_w_out=(conv_w_out[0], m_conv_w_out[0], v_conv_w_out[0]),
                 w_mem_kv=tuple(t.reshape(-1, t.shape[2]) for t in (w_mem_kv, m_w_mem_kv, v_w_mem_kv)))
    finish1 = {n: (pf, r) for n, (pf, _), r in zip(names1, parts1, from_chips1)}
    big = {"conv_w_in": _finish("conv_w_in", *finish1["conv_w_in"], *shard["conv_w_in"], token0)}

    grads0, from_sibling = _split_wait(started0, _plan_to_sibling, big["conv_w_in"][1].T, "rs0_sibling_wait")
    parts0 = _rs_add_sibling(grads0, from_sibling, ck_arr, "rs0_add_sibling")
    pbs0 = [pb for _, pb in parts0]
    started0, token0 = _split_start(pbs0, [lax.empty((3,) + p.shape[1:], p.dtype) for p in pbs0],
                                    _plan_to_chips, 3, dg1, "rs0_chips_start")
    dx, dg0 = _dgrad_norm(pieces0, wg_in0, x, g0, dh1, token0, False, "dgrad_norm_attn")
    for n in names1[1:]:
        big[n] = _finish(n, *finish1[n], *shard[n], token0)

    small_part = jnp.concatenate([dg0, dg1, d_mem_g.reshape(2, -1), d_final_g, dcw[0:3]], axis=0)
    small_part = jnp.concatenate([small_part, jnp.broadcast_to(loss_acc[0, 0], small_part.shape)], axis=0)
    small_w = dict(norm_g=(norm_g, m_norm_g, v_norm_g), mem_norm_g=(mem_norm_g, m_mem_norm_g, v_mem_norm_g),
                   conv_w=(conv_w, m_conv_w, v_conv_w), final_g=(final_g, m_final_g, v_final_g))
    loss_tile, small_res = _sum_adamw_small(
        _all_gather_small(small_part, [big[n][1] for n in names1[1:]], "gather_small_grads"), ck_arr,
        {n: tuple(t.reshape(-1, t.shape[-1]) for t in wmv) for n, wmv in small_w.items()})
    loss = loss_tile[0, 0]
    for n, wmv in small_w.items():
        big[n] = tuple(t.reshape(wmv[0].shape) for t in small_res[n])

    _, from_chips0 = _split_wait(started0, _plan_to_chips, big["final_g"][1], "rs0_chips_wait")
    for n, (pf, _), r in zip(names0, parts0, from_chips0):
        big[n] = _finish(n, pf, r, *shard[n], big["final_g"][1])
    for n in ("attn_w_in", "attn_w_out", "conv_w_in", "conv_w_out"):
        big[n] = tuple(t[None] for t in big[n])
    big["w_mem_kv"] = tuple(t.reshape(w_mem_kv.shape) for t in big["w_mem_kv"])

    order = ["norm_g", "mem_norm_g", "w_mem_kv", "attn_w_in", "attn_w_out", "conv_w_in", "conv_w", "conv_w_out", "final_g"]
    return (loss, dx[None], *[big[n][0] for n in order], *[big[n][1] for n in order],
            *[big[n][2] for n in order], *[big[n][3] for n in order])
```

```python
import jax
import jax.numpy as jnp
from jax import lax
from jax.experimental import pallas as pl
from jax.experimental.pallas import tpu as pltpu

F32 = jnp.float32
BF16 = jnp.bfloat16

N_DEV = 8
D_MODEL = 1024
HEAD_DIM = 64
ROT_DIM = HEAD_DIM // 4
ROPE_THETA = 500000.0
DILATIONS = (1, 4, 16)
HEADS_PER_GROUP = 8
GROUP_WIDTH = HEADS_PER_GROUP * HEAD_DIM
BLOCK = 128
N_MEM = 256
MEM_HEADS = 4
MEM_WIDTH = MEM_HEADS * HEAD_DIM
CONV_WIDTH = D_MODEL
EPS = 1e-6
SCALE = HEAD_DIM ** -0.5
NEG = -1e30

ADAM_LR = 0.001
ADAM_B1 = 0.9
ADAM_B2 = 0.999
ADAM_EPS = 1e-08
ADAM_WD = 0.01
ADAM_STEP = 10

ROW_TILE = 256
WGRAD_SHARDS = 4
LANES = 128
MESH = pl.DeviceIdType.MESH
ANY = pl.BlockSpec(memory_space=pl.ANY)


def _pallas_call(body, **kw):
    call = pl.pallas_call(body, **kw)

    def run(*args):
        pinned = [pltpu.with_memory_space_constraint(a, pltpu.HBM) if jnp.issubdtype(a.dtype, jnp.floating) else a
                  for a in args]
        return call(*pinned)

    return run


def _dot(a, b):
    return lax.dot_general(a, b, (((1,), (0,)), ((), ())), preferred_element_type=F32)


def _dot_nt(a, b):
    return lax.dot_general(a, b, (((1,), (1,)), ((), ())), preferred_element_type=F32)


def _dot_tn(a, b):
    return lax.dot_general(a, b, (((0,), (0,)), ((), ())), preferred_element_type=F32)


def _params(n_grid, vmem_mb=48):
    return pltpu.CompilerParams(dimension_semantics=("arbitrary",) * n_grid, vmem_limit_bytes=vmem_mb << 20)


def _rows(width, tm=ROW_TILE):
    return pl.BlockSpec((tm, width), lambda i: (i, 0))


def _view_rows(width, d, tm=ROW_TILE):
    return pl.BlockSpec((tm // d, d * width), lambda i: (i, 0))


def _whole(shape):
    return pl.BlockSpec(shape, lambda *_: (0,) * len(shape))


def _resident(shape):
    return pl.BlockSpec(shape, lambda *_: (0,) * len(shape), pipeline_mode=pl.Buffered(1))


def _sds(shape, dtype):
    return pltpu.HBM(shape, dtype)


def _plain(shape, dtype):
    return jax.ShapeDtypeStruct(shape, dtype)


def _silu_parts(z):
    sg = jax.nn.sigmoid(z)
    return z * sg, sg * (1.0 + z * (1.0 - sg))


def _to_view(scr, val, out_ref, d):
    tm, w = val.shape
    if d == 1:
        out_ref[...] = val.astype(out_ref.dtype)
        return
    for cb in range(w // LANES):
        scr[cb] = val[:, cb * LANES:(cb + 1) * LANES]
    for r in range(d):
        for cb in range(w // LANES):
            lo = r * w + cb * LANES
            out_ref[:, lo:lo + LANES] = scr[cb, pl.ds(r, tm // d, stride=d), :].astype(out_ref.dtype)


def _from_view(scr, in_ref, d):
    if d == 1:
        return in_ref[...].astype(F32)
    nc, tm, _ = scr.shape
    w = nc * LANES
    for r in range(d):
        for cb in range(nc):
            lo = r * w + cb * LANES
            scr[cb, pl.ds(r, tm // d, stride=d), :] = in_ref[:, lo:lo + LANES].astype(F32)
    return jnp.concatenate([scr[cb] for cb in range(nc)], axis=1)


def _view_scratch(width, tm=ROW_TILE):
    return pltpu.VMEM((width // LANES, tm, LANES), F32)


def _rope_angles(pos):
    half = ROT_DIM // 2
    inv_freq = ROPE_THETA ** (-jnp.arange(half, dtype=F32) * (2.0 / ROT_DIM))
    ang = pos.astype(F32)[:, None] * inv_freq
    return jnp.cos(ang), jnp.sin(ang)


def _rope_expand(cos_ref, sin_ref, c_ref, sa_ref, sb_ref):
    half = ROT_DIM // 2
    tm = ROW_TILE
    lane = lax.broadcasted_iota(jnp.int32, (tm, LANES), 1) % HEAD_DIM
    for r0 in range(0, cos_ref.shape[0], tm):
        c, sa, sb = jnp.ones((tm, LANES), F32), jnp.zeros((tm, LANES), F32), jnp.zeros((tm, LANES), F32)
        for f in range(half):
            cf = jnp.broadcast_to(cos_ref[r0:r0 + tm, f:f + 1], (tm, LANES))
            sf = jnp.broadcast_to(sin_ref[r0:r0 + tm, f:f + 1], (tm, LANES))
            low, high = lane == f, lane == f + half
            c = jnp.where(jnp.logical_or(low, high), cf, c)
            sa = jnp.where(low, -sf, sa)
            sb = jnp.where(high, sf, sb)
        c_ref[r0:r0 + tm, :], sa_ref[r0:r0 + tm, :], sb_ref[r0:r0 + tm, :] = c, sa, sb


def _rope_fwd(t, cv, sav, sbv):
    w = t.shape[1]
    return t * cv + pltpu.roll(t, w - ROT_DIM // 2, 1) * sav + pltpu.roll(t, ROT_DIM // 2, 1) * sbv


def _rope_bwd(g, cv, sav, sbv):
    w = g.shape[1]
    return g * cv + pltpu.roll(g * sav, ROT_DIM // 2, 1) + pltpu.roll(g * sbv, w - ROT_DIM // 2, 1)


def _joined_columns(wg_ref):
    assert wg_ref.shape[2] % LANES == 0
    return jnp.concatenate([wg_ref[j] for j in range(N_DEV)], axis=1)


def _join_once(wg_ref, w_scr):
    c = wg_ref.shape[2]

    @pl.when(pl.program_id(0) == 0)
    def _():
        for j in range(N_DEV):
            w_scr[:, j * c:(j + 1) * c] = wg_ref[j]


def _inproj_attn(x, g, wg, tabs, token):
    s, d_model = x.shape
    gw = GROUP_WIDTH
    n = N_DEV * wg.shape[2]
    nz = n - 9 * gw - MEM_WIDTH
    reps = gw // LANES
    tm = ROW_TILE

    def body(_, x_ref, g_ref, w_ref, c_ref, sa_ref, sb_ref, hn_ref, *rest):
        outs, (wj, scr, tscr) = rest[:-3], rest[-3:]
        q_refs, k_refs, v_refs, t_refs, qm_ref, z_ref = outs[0:3], outs[3:6], outs[6:9], outs[9:18], outs[18], outs[19]
        _join_once(w_ref, wj)
        xb = x_ref[...]
        r = lax.rsqrt(jnp.mean(xb * xb, axis=-1, keepdims=True) + EPS)
        hn = ((xb * r) * g_ref[...]).astype(BF16)
        hn_ref[...] = hn
        proj = lambda lo, hi: _dot(hn, wj[:, lo:hi])
        tab = (c_ref[...], sa_ref[...], sb_ref[...])
        cv, sav, sbv = [jnp.tile(t, (1, reps)) for t in tab]
        for j, d in enumerate(DILATIONS):
            tq = _rope_fwd(proj(j * gw, (j + 1) * gw), cv, sav, sbv)
            _to_view(scr, tq * SCALE, q_refs[j], d)
            tk = _rope_fwd(proj((3 + j) * gw, (4 + j) * gw), cv, sav, sbv)
            _to_view(scr, tk, k_refs[j], d)
            _to_view(scr, proj((6 + j) * gw, (7 + j) * gw), v_refs[j], d)
            for i in range(3):
                _to_view(tscr, tab[i], t_refs[3 * j + i], d)
        qm_ref[...] = proj(9 * gw, 9 * gw + MEM_WIDTH).astype(BF16)
        z_ref[...] = proj(9 * gw + MEM_WIDTH, n)

    views = [_sds((s // d, d * gw), BF16) for d in DILATIONS]
    tviews = [_sds((s // d, d * LANES), F32) for d in DILATIONS for _ in range(3)]
    out_shape = [_sds((s, d_model), BF16)] + views * 3 + tviews + [_sds((s, MEM_WIDTH), BF16), _sds((s, nz), F32)]
    vspecs = [_view_rows(gw, d, tm) for d in DILATIONS]
    tspecs = [_view_rows(LANES, d, tm) for d in DILATIONS for _ in range(3)]
    out_specs = [_rows(d_model, tm)] + vspecs * 3 + tspecs + [_rows(MEM_WIDTH, tm), _rows(nz, tm)]
    res = _pallas_call(
        body, name="inproj_attn", grid=(s // tm,), out_shape=out_shape,
        in_specs=[ANY, _rows(d_model, tm), _whole((1, d_model)), _resident(wg.shape)] + [_rows(LANES, tm)] * 3,
        out_specs=out_specs,
        scratch_shapes=[pltpu.VMEM((d_model, n), BF16), _view_scratch(gw, tm), _view_scratch(LANES, tm)],
        compiler_params=_params(1, 60),
    )(token, x, g, wg, *tabs)
    tabs_v = [res[10 + 3 * j:13 + 3 * j] for j in range(3)]
    return res[0], res[1:4], res[4:7], res[7:10], tabs_v, res[19], res[20]


def _band_mask(n_keys):
    qi = lax.broadcasted_iota(jnp.int32, (BLOCK, n_keys), 0)
    kj = lax.broadcasted_iota(jnp.int32, (BLOCK, n_keys), 1)
    if n_keys == BLOCK:
        return kj <= qi
    return jnp.logical_or(jnp.logical_and(kj < BLOCK, kj >= qi), jnp.logical_and(kj >= BLOCK, (kj - BLOCK) <= qi))


def _low_head_lanes():
    return lax.broadcasted_iota(jnp.int32, (1, LANES), 1) < HEAD_DIM


def _split_pair(t, low):
    zero = jnp.zeros_like(t)
    return jnp.where(low, t, zero), jnp.where(low, zero, t)


def _pair_specs(d, nb, w):
    if nb == 1:
        return pl.BlockSpec((BLOCK, 2 * w), lambda n: (0, n)), None
    half = nb // 2
    two = pl.BlockSpec((2 * BLOCK, w), lambda n: (n % half, n // half))
    before = pl.BlockSpec((BLOCK, w), lambda n: (jnp.maximum(2 * (n % half) - 1, 0), n // half))
    return two, before


def _head_tiles(w, col0):
    return ([slice(p * LANES, (p + 1) * LANES) for p in range(w // LANES)],
            [slice(col0 + p * LANES, col0 + (p + 1) * LANES) for p in range(w // LANES)])


def _attend_fwd(q_ref, o_ref, lse_ref, rows, col0, kk, vv):
    w = kk.shape[1]
    valid = _band_mask(kk.shape[0])
    low = _low_head_lanes()
    pairs, qcols = _head_tiles(w, col0)
    qs_ = [h for qc in qcols for h in _split_pair(q_ref[rows, qc], low)]
    k2s = [kk[:, pr] for pr in pairs for _ in range(2)]
    scs = [jnp.where(valid, _dot_nt(qh, k2), NEG) for qh, k2 in zip(qs_, k2s)]
    ms = [jnp.max(sc, axis=-1, keepdims=True) for sc in scs]
    ps = [jnp.exp(sc - m) for sc, m in zip(scs, ms)]
    ls = [jnp.sum(p, axis=-1, keepdims=True) for p in ps]
    pns = [(p * (1.0 / l)).astype(BF16) for p, l in zip(ps, ls)]
    for i, (pr, qc) in enumerate(zip(pairs, qcols)):
        v2 = vv[:, pr]
        a, b = 2 * i, 2 * i + 1
        o_ref[rows, qc] = jnp.where(low, _dot(pns[a], v2), _dot(pns[b], v2))
        lse_ref[rows, qc] = jnp.where(low, ms[a] + jnp.log(ls[a]), ms[b] + jnp.log(ls[b]))


TOP, BOTTOM = slice(0, BLOCK), slice(BLOCK, 2 * BLOCK)


def _attn_fwd(q, k, v, d, token):
    ln, dw = q.shape
    w = dw // d
    nb = ln // BLOCK
    two, before = _pair_specs(d, nb, w)

    def body_streams(_, q_ref, kc_ref, vc_ref, o_ref, lse_ref):
        for sb in range(2):
            cols = slice(sb * w, (sb + 1) * w)
            _attend_fwd(q_ref, o_ref, lse_ref, TOP, sb * w, kc_ref[:, cols], vc_ref[:, cols])

    def body_blocks(_, q_ref, kp_ref, kc_ref, vp_ref, vc_ref, o_ref, lse_ref):
        first = pl.program_id(0) % (nb // 2) == 0
        pl.when(first)(lambda: _attend_fwd(q_ref, o_ref, lse_ref, TOP, 0, kc_ref[TOP, :], vc_ref[TOP, :]))
        pl.when(jnp.logical_not(first))(lambda: _attend_fwd(
            q_ref, o_ref, lse_ref, TOP, 0, jnp.concatenate([kp_ref[...], kc_ref[TOP, :]], axis=0),
            jnp.concatenate([vp_ref[...], vc_ref[TOP, :]], axis=0)))
        _attend_fwd(q_ref, o_ref, lse_ref, BOTTOM, 0, kc_ref[...], vc_ref[...])

    if nb == 1:
        body, in_specs, args = body_streams, [ANY, two, two, two], (token, q, k, v)
    else:
        body, in_specs, args = body_blocks, [ANY, two, before, two, before, two], (token, q, k, k, v, v)
    return _pallas_call(
        body, name=f"attn_fwd_d{d}", grid=(d * nb // 2,), out_shape=[_sds((ln, dw), F32)] * 2,
        in_specs=in_specs, out_specs=[two, two], compiler_params=_params(1, 32),
    )(*args)


def _memkv_fwd(mem, g, w):
    n_layers = g.shape[0]
    rows = D_MODEL // N_DEV

    def body(mem_ref, g_ref, w_ref, *kv_refs):
        mb = mem_ref[...]
        r = lax.rsqrt(jnp.mean(mb * mb, axis=-1, keepdims=True) + EPS)
        mn = ((mb * r) * g_ref[...]).astype(BF16)
        kv = _dot(mn, w_ref[...].reshape(D_MODEL, 2 * MEM_WIDTH)).astype(BF16)
        for layer, kv_ref in enumerate(kv_refs):
            @pl.when(pl.program_id(0) == layer)
            def _(kv_ref=kv_ref):
                kv_ref[...] = kv

    return _pallas_call(
        body, name="memkv_fwd", grid=(n_layers,),
        out_shape=[_sds((N_MEM, 2 * MEM_WIDTH), BF16)] * n_layers,
        in_specs=[_whole(mem.shape), pl.BlockSpec((None, 1, D_MODEL), lambda l: (l, 0, 0)),
                  pl.BlockSpec((N_DEV, rows, 2 * MEM_WIDTH), lambda l: (0, l, 0))],
        out_specs=[_whole((N_MEM, 2 * MEM_WIDTH))] * n_layers,
        compiler_params=_params(1, 32),
    )(mem, g.reshape(n_layers, 1, D_MODEL), w)


def _mix_groups(os_, ls_):
    mx = jnp.maximum(jnp.maximum(ls_[0], ls_[1]), ls_[2])
    es = [jnp.exp(t - mx) for t in ls_]
    inv = 1.0 / (es[0] + es[1] + es[2])
    ws = [e * inv for e in es]
    mix = ws[0] * os_[0] + ws[1] * os_[1] + ws[2] * os_[2]
    return ws, mix


MEM_PAIRS = [slice(p * LANES, (p + 1) * LANES) for p in range(MEM_WIDTH // LANES)]


def _mem_probs(qhs, k2s):
    scs = [_dot_nt(qh, k2) * SCALE for qh, k2 in zip(qhs, k2s)]
    es = [jnp.exp(sc - jnp.max(sc, axis=-1, keepdims=True)) for sc in scs]
    return [e * (1.0 / jnp.sum(e, axis=-1, keepdims=True)) for e in es]


def _mem_attn_into(qm, kv_ref, mo_ref):
    low = _low_head_lanes()
    qhs = [h for pr in MEM_PAIRS for h in _split_pair(qm[:, pr], low)]
    k2s = [kv_ref[:, pr] for pr in MEM_PAIRS for _ in range(2)]
    ps = [p.astype(BF16) for p in _mem_probs(qhs, k2s)]
    for i, pr in enumerate(MEM_PAIRS):
        v2 = kv_ref[:, MEM_WIDTH + i * LANES:MEM_WIDTH + (i + 1) * LANES]
        mo_ref[:, pr] = jnp.where(low, _dot(ps[2 * i], v2), _dot(ps[2 * i + 1], v2))


def _mem_attn_bwd(qm, kv_ref, dmem, dqm_ref, dkv_ref):
    low = _low_head_lanes()
    dmb = dmem.astype(BF16)
    vps = [slice(MEM_WIDTH + i * LANES, MEM_WIDTH + (i + 1) * LANES) for i in range(len(MEM_PAIRS))]
    qhs = [h for pr in MEM_PAIRS for h in _split_pair(qm[:, pr], low)]
    dhs = [h for pr in MEM_PAIRS for h in _split_pair(dmb[:, pr], low)]
    k2s = [kv_ref[:, pr] for pr in MEM_PAIRS for _ in range(2)]
    v2s = [kv_ref[:, vp] for vp in vps for _ in range(2)]
    ps = _mem_probs(qhs, k2s)
    dps = [_dot_nt(dh, v2) for dh, v2 in zip(dhs, v2s)]
    dss = [(p * (dp - jnp.sum(dp * p, axis=-1, keepdims=True)) * SCALE).astype(BF16) for p, dp in zip(ps, dps)]
    pbs = [p.astype(BF16) for p in ps]
    for i, (pr, vp) in enumerate(zip(MEM_PAIRS, vps)):
        a, b = 2 * i, 2 * i + 1
        dqm_ref[:, pr] = jnp.where(low, _dot(dss[a], k2s[a]), _dot(dss[b], k2s[b])).astype(BF16)
        dkv_ref[:, pr] += _dot_tn(dss[a], qhs[a]) + _dot_tn(dss[b], qhs[b])
        dkv_ref[:, vp] += _dot_tn(pbs[a], dhs[a]) + _dot_tn(pbs[b], dhs[b])


def _post_attn(os_, ls_, qm, kv, z, x, wg_out):
    s, d_model = x.shape
    gw = GROUP_WIDTH
    nb = gw + MEM_WIDTH
    tm = ROW_TILE

    def body(o0, o1, o2, l0, l1, l2, qm_ref, kv_ref, z_ref, x_ref, w_ref, y_ref, yt_ref, mo_ref, h_ref, s0, s1):
        ov, lv = [], []
        for o_ref, l_ref, d in zip((o0, o1, o2), (l0, l1, l2), DILATIONS):
            ov.append(_from_view(s0, o_ref, d))
            lv.append(_from_view(s1, l_ref, d))
        _, mix = _mix_groups(ov, lv)
        _mem_attn_into(qm_ref[...], kv_ref, mo_ref)
        sz, _ = _silu_parts(z_ref[...])
        y_ref[:, :gw] = (mix * sz[:, :gw]).astype(BF16)
        y_ref[:, gw:] = (mo_ref[...] * sz[:, gw:]).astype(BF16)
        y = y_ref[...]
        yt_ref[...] = y.T
        h_ref[...] = x_ref[...] + _dot(y, _joined_columns(w_ref))

    vspecs = [_view_rows(gw, d) for d in DILATIONS]
    return _pallas_call(
        body, name="post_attn", grid=(s // tm,),
        out_shape=[_sds((s, nb), BF16), _sds((nb, s), BF16), _sds((s, MEM_WIDTH), F32), _sds((s, d_model), F32)],
        in_specs=vspecs * 2 + [_rows(MEM_WIDTH), _whole(kv.shape), _rows(nb), _rows(d_model), _whole(wg_out.shape)],
        out_specs=[_rows(nb), pl.BlockSpec((nb, tm), lambda i: (0, i)), _rows(MEM_WIDTH), _rows(d_model)],
        scratch_shapes=[_view_scratch(gw), _view_scratch(gw)],
        compiler_params=_params(1, 40),
    )(*os_, *ls_, qm, kv, z, x, wg_out)


def _inproj_conv(x, g, wg):
    s, d_model = x.shape
    c = CONV_WIDTH
    n = N_DEV * wg.shape[2]
    nz = n - 3 * c - MEM_WIDTH
    tm = ROW_TILE

    def body(x_ref, g_ref, w_ref, hn_ref, bg_ref, cg_ref, u_ref, qm_ref, z_ref, wj):
        _join_once(w_ref, wj)
        xb = x_ref[...]
        r = lax.rsqrt(jnp.mean(xb * xb, axis=-1, keepdims=True) + EPS)
        hn = ((xb * r) * g_ref[...]).astype(BF16)
        hn_ref[...] = hn
        bg_ref[...] = _dot(hn, wj[:, 0:c])
        cg_ref[...] = _dot(hn, wj[:, c:2 * c])
        u_ref[...] = _dot(hn, wj[:, 2 * c:3 * c])
        qm_ref[...] = _dot(hn, wj[:, 3 * c:3 * c + MEM_WIDTH]).astype(BF16)
        z_ref[...] = _dot(hn, wj[:, 3 * c + MEM_WIDTH:])

    return _pallas_call(
        body, name="inproj_conv", grid=(s // tm,),
        out_shape=[_sds((s, d_model), BF16)] + [_sds((s, c), F32)] * 3 + [_sds((s, MEM_WIDTH), BF16), _sds((s, nz), F32)],
        in_specs=[_rows(d_model), _whole((1, d_model)), _resident(wg.shape)],
        out_specs=[_rows(d_model)] + [_rows(c)] * 3 + [_rows(MEM_WIDTH), _rows(nz)],
        scratch_shapes=[pltpu.VMEM((d_model, n), BF16)],
        compiler_params=_params(1, 60),
    )(x, g, wg)


HALO = 8


def _halo_before(width, tm=ROW_TILE):
    return pl.BlockSpec((HALO, width), lambda i: (jnp.maximum(i * (tm // HALO) - 1, 0), 0))


def _halo_after(width, n_rows, tm=ROW_TILE):
    return pl.BlockSpec((HALO, width), lambda i: (jnp.minimum((i + 1) * (tm // HALO), n_rows // HALO - 1), 0))


def _conv_taps(cg_ref, u_ref, cgh_ref, uh_ref, i):
    a = cg_ref[...] * u_ref[...]
    ah = jnp.where(i > 0, cgh_ref[...] * uh_ref[...], 0.0)
    row = lax.broadcasted_iota(jnp.int32, a.shape, 0)
    a1 = jnp.where(row == 0, ah[HALO - 1:HALO], pltpu.roll(a, 1, 0))
    a2 = jnp.where(row == 0, ah[HALO - 2:HALO - 1], jnp.where(row == 1, ah[HALO - 1:HALO], pltpu.roll(a, 2, 0)))
    return a, a1, a2


def _post_conv_loss(bg, cg, u, qm, kv, z, h1, w_out, cw, gf, tgt):
    s, d = h1.shape
    c = CONV_WIDTH
    nb = c + MEM_WIDTH
    tm = ROW_TILE

    def body(bg_ref, cg_ref, u_ref, cgh_ref, uh_ref, qm_ref, kv_ref, z_ref, h_ref, w_ref, cw_ref, gf_ref, t_ref,
             y_ref, yt_ref, mo_ref, dh_ref, dhb_ref, loss_ref, dgf_ref):
        i = pl.program_id(0)
        a, a1, a2 = _conv_taps(cg_ref, u_ref, cgh_ref, uh_ref, i)
        conv = cw_ref[0:1, :] * a2 + cw_ref[1:2, :] * a1 + cw_ref[2:3, :] * a
        mix = bg_ref[...] * conv
        _mem_attn_into(qm_ref[...], kv_ref, mo_ref)
        sz, _ = _silu_parts(z_ref[...])
        y_ref[:, :c] = (mix * sz[:, :c]).astype(BF16)
        y_ref[:, c:] = (mo_ref[...] * sz[:, c:]).astype(BF16)
        y = y_ref[...]
        yt_ref[...] = y.T
        h2 = h_ref[...] + _dot(y, w_ref[...])
        r = lax.rsqrt(jnp.mean(h2 * h2, axis=-1, keepdims=True) + EPS)
        nh = h2 * r
        gfv = gf_ref[...]
        diff = nh * gfv - t_ref[...]
        dout = diff * (1.0 / d)
        dn = dout * gfv
        dh2 = r * dn - h2 * ((r * r * r) * jnp.mean(dn * h2, axis=-1, keepdims=True))
        dh_ref[...] = dh2
        dhb_ref[...] = dh2.astype(BF16)

        @pl.when(i == 0)
        def _():
            loss_ref[...] = jnp.zeros_like(loss_ref)
            dgf_ref[...] = jnp.zeros_like(dgf_ref)

        loss_ref[...] += 0.5 * jnp.sum(jnp.mean(diff * diff, axis=-1, keepdims=True))
        dgf_ref[...] += jnp.sum(dout * nh, axis=0, keepdims=True)

    return _pallas_call(
        body, name="post_conv_loss", grid=(s // tm,),
        out_shape=[_sds((s, nb), BF16), _sds((nb, s), BF16), _sds((s, MEM_WIDTH), F32), _sds((s, d), F32),
                   _sds((s, d), BF16), _plain((8, LANES), F32), _plain((1, d), F32)],
        in_specs=[_rows(c)] * 3 + [_halo_before(c)] * 2 + [_rows(MEM_WIDTH), _whole(kv.shape), _rows(nb), _rows(d),
                  _whole(w_out.shape), _whole(cw.shape), _whole((1, d)), _rows(d)],
        out_specs=[_rows(nb), pl.BlockSpec((nb, tm), lambda i: (0, i)), _rows(MEM_WIDTH), _rows(d), _rows(d),
                   _whole((8, LANES)), _whole((1, d))],
        compiler_params=_params(1, 48),
    )(bg, cg, u, cg, u, qm, kv, z, h1, w_out, cw, gf, tgt)


def _bwd_post_conv(dhb, w_out, bg, cg, u, z, qm, kv, mo, cw):
    s = dhb.shape[0]
    c = CONV_WIDTH
    nb = c + MEM_WIDTH

    def body(dh_ref, w_ref, bg_ref, cg_ref, u_ref, cgh_ref, uh_ref, z_ref, qm_ref, kv_ref, mo_ref, cw_ref,
             dz_ref, dbg_ref, dc_ref, dqm_ref, dkv_ref):
        i = pl.program_id(0)

        @pl.when(i == 0)
        def _():
            dkv_ref[...] = jnp.zeros_like(dkv_ref)

        dy = _dot_nt(dh_ref[...], w_ref[...])
        sz, dsz = _silu_parts(z_ref[...])
        a, a1, a2 = _conv_taps(cg_ref, u_ref, cgh_ref, uh_ref, i)
        conv = cw_ref[0:1, :] * a2 + cw_ref[1:2, :] * a1 + cw_ref[2:3, :] * a
        bgv = bg_ref[...]
        dz_ref[:, :c] = (dy[:, :c] * (bgv * conv) * dsz[:, :c]).astype(BF16)
        dz_ref[:, c:] = (dy[:, c:] * mo_ref[...] * dsz[:, c:]).astype(BF16)
        dbr = dy * sz
        dmix = dbr[:, :c]
        dbg_ref[...] = (dmix * conv).astype(BF16)
        dc_ref[...] = dmix * bgv
        _mem_attn_bwd(qm_ref[...], kv_ref, dbr[:, c:], dqm_ref, dkv_ref)

    return _pallas_call(
        body, name="bwd_post_conv", grid=(s // ROW_TILE,),
        out_shape=[_sds((s, nb), BF16), _sds((s, c), BF16), _sds((s, c), F32), _sds((s, MEM_WIDTH), BF16),
                   _sds(kv.shape, F32)],
        in_specs=[_rows(D_MODEL), _whole(w_out.shape)] + [_rows(c)] * 3 + [_halo_before(c)] * 2
                 + [_rows(nb), _rows(MEM_WIDTH), _whole(kv.shape), _rows(MEM_WIDTH), _whole(cw.shape)],
        out_specs=[_rows(nb), _rows(c), _rows(c), _rows(MEM_WIDTH), _whole(kv.shape)],
        compiler_params=_params(1, 48),
    )(dhb, w_out, bg, cg, u, cg, u, z, qm, kv, mo, cw)


def _bwd_conv(dconv, cg, u, cw):
    s, c = dconv.shape
    tm = ROW_TILE
    last = s // tm - 1

    def body(dc_ref, dcn_ref, cg_ref, u_ref, cgh_ref, uh_ref, cw_ref, dcg_ref, du_ref, dcw_ref):
        i = pl.program_id(0)

        @pl.when(i == 0)
        def _():
            dcw_ref[...] = jnp.zeros_like(dcw_ref)

        dc = dc_ref[...]
        dcn = jnp.where(i < last, dcn_ref[...], 0.0)
        row = lax.broadcasted_iota(jnp.int32, dc.shape, 0)
        d1 = jnp.where(row == tm - 1, dcn[0:1], pltpu.roll(dc, tm - 1, 0))
        d2 = jnp.where(row == tm - 1, dcn[1:2], jnp.where(row == tm - 2, dcn[0:1], pltpu.roll(dc, tm - 2, 0)))
        da = cw_ref[2:3, :] * dc + cw_ref[1:2, :] * d1 + cw_ref[0:1, :] * d2
        a, a1, a2 = _conv_taps(cg_ref, u_ref, cgh_ref, uh_ref, i)
        dcg_ref[...] = (da * u_ref[...]).astype(BF16)
        du_ref[...] = (da * cg_ref[...]).astype(BF16)
        dcw_ref[0:1, :] += jnp.sum(dc * a2, axis=0, keepdims=True)
        dcw_ref[1:2, :] += jnp.sum(dc * a1, axis=0, keepdims=True)
        dcw_ref[2:3, :] += jnp.sum(dc * a, axis=0, keepdims=True)

    return _pallas_call(
        body, name="bwd_conv", grid=(s // tm,),
        out_shape=[_sds((s, c), BF16), _sds((s, c), BF16), _plain((8, c), F32)],
        in_specs=[_rows(c), _halo_after(c, s), _rows(c), _rows(c), _halo_before(c), _halo_before(c), _whole(cw.shape)],
        out_specs=[_rows(c), _rows(c), _whole((8, c))], compiler_params=_params(1, 40),
    )(dconv, dconv, cg, u, cg, u, cw)


def _assemble(p_refs, pieces, widths, dp, scr):
    off = 0
    for p_ref, (_, d), wd in zip(p_refs, pieces, widths):
        if d == 1:
            dp[:, off:off + wd] = p_ref[...]
        else:
            dp[:, off:off + wd] = _from_view(scr, p_ref, d).astype(BF16)
        off += wd


def _dgrad_norm(pieces, wg, h, g, dres, token, onward, name):
    s, d_model = h.shape
    n = N_DEV * wg.shape[2]
    tm = ROW_TILE
    widths = [p.shape[1] // d for p, d in pieces]
    assert sum(widths) == n
    n_p = len(pieces)

    def body(_, *refs):
        p_refs = refs[:n_p]
        w_ref, h_ref, g_ref, dr_ref, dh_ref, dg_ref = refs[n_p:n_p + 6]
        dp, scr, wj = refs[-3:]
        _join_once(w_ref, wj)

        @pl.when(pl.program_id(0) == 0)
        def _():
            dg_ref[...] = jnp.zeros_like(dg_ref)

        _assemble(p_refs, pieces, widths, dp, scr)
        dhn = _dot_nt(dp[...], wj[...])
        hb = h_ref[...]
        r = lax.rsqrt(jnp.mean(hb * hb, axis=-1, keepdims=True) + EPS)
        dg_ref[...] += jnp.sum(dhn * (hb * r), axis=0, keepdims=True)
        dn = dhn * g_ref[...]
        dh = dr_ref[...] + r * dn - hb * ((r * r * r) * jnp.mean(dn * hb, axis=-1, keepdims=True))
        dh_ref[...] = dh
        if onward:
            dhb_ref, dpt_ref = refs[n_p + 6:n_p + 8]
            dhb_ref[...] = dh.astype(BF16)
            dpt_ref[...] = dp[...].T

    p_specs = [_view_rows(wd, d) for (_, d), wd in zip(pieces, widths)]
    out_shape = [_plain((s, d_model), F32), _plain((1, d_model), F32)]
    out_specs = [_rows(d_model), _whole((1, d_model))]
    if onward:
        out_shape += [_sds((s, d_model), BF16), _sds((n, s), BF16)]
        out_specs += [_rows(d_model), pl.BlockSpec((n, tm), lambda i: (0, i))]
    return _pallas_call(
        body, name=name, grid=(s // tm,), out_shape=out_shape,
        in_specs=[ANY] + p_specs + [_resident(wg.shape), _rows(d_model), _whole((1, d_model)), _rows(d_model)],
        out_specs=out_specs,
        scratch_shapes=[pltpu.VMEM((tm, n), BF16), _view_scratch(GROUP_WIDTH), pltpu.VMEM((d_model, n), BF16)],
        compiler_params=_params(1, 60),
    )(token, *[p for p, _ in pieces], wg, h, g, dres)


def _assemble_dproj_t(pieces, n, name):
    tm = ROW_TILE
    widths = [p.shape[1] // d for p, d in pieces]
    assert sum(widths) == n
    s = pieces[0][0].shape[0] * pieces[0][1]
    n_p = len(pieces)

    def body(*refs):
        p_refs, (dpt_ref, dp, scr) = refs[:n_p], refs[n_p:]
        _assemble(p_refs, pieces, widths, dp, scr)
        dpt_ref[...] = dp[...].T

    return _pallas_call(
        body, name=name, grid=(s // tm,), out_shape=_sds((n, s), BF16),
        in_specs=[_view_rows(wd, d) for (_, d), wd in zip(pieces, widths)],
        out_specs=pl.BlockSpec((n, tm), lambda i: (0, i)),
        scratch_shapes=[pltpu.VMEM((tm, n), BF16), _view_scratch(GROUP_WIDTH)],
        compiler_params=_params(1, 40),
    )(*[p for p, _ in pieces])


def _wgrad_shards_t(dp_t, h, c, name):
    n, s = dp_t.shape
    d_model = h.shape[1]
    per_step = 2

    def body(a_ref, b_ref, o_ref):
        o_ref[...] = _dot(a_ref[...], b_ref[...]).astype(BF16).reshape(per_step, c, d_model)

    return _pallas_call(
        body, name=name, grid=(N_DEV // per_step,), out_shape=_sds((N_DEV, c, d_model), BF16),
        in_specs=[pl.BlockSpec((per_step * c, s), lambda j: (j, 0)), _resident(h.shape)],
        out_specs=pl.BlockSpec((per_step, c, d_model), lambda j: (j, 0, 0)), compiler_params=_params(1, 40),
    )(dp_t, h)


def _wgrad_cols(a_t, b, c, name):
    m, s = a_t.shape
    assert c % LANES == 0

    def body(a_ref, b_ref, o_ref):
        wide = _dot(a_ref[...], b_ref[...]).astype(BF16)
        for j in range(WGRAD_SHARDS):
            o_ref[j] = wide[:, j * c:(j + 1) * c]

    return _pallas_call(
        body, name=name, grid=(N_DEV // WGRAD_SHARDS,), out_shape=_sds((N_DEV, m, c), BF16),
        in_specs=[_whole(a_t.shape), pl.BlockSpec((s, WGRAD_SHARDS * c), lambda j: (0, j))],
        out_specs=pl.BlockSpec((WGRAD_SHARDS, m, c), lambda j: (j, 0, 0)), compiler_params=_params(1, 40),
    )(a_t, b)


def _wgrad_rows(a_t, b, name):
    m, s = a_t.shape
    n = b.shape[1]
    mr = m // N_DEV

    def body(a_ref, b_ref, o_ref):
        o_ref[...] = _dot(a_ref[...], b_ref[...]).astype(BF16).reshape(WGRAD_SHARDS, mr, n)

    return _pallas_call(
        body, name=name, grid=(N_DEV // WGRAD_SHARDS,), out_shape=_sds((N_DEV, mr, n), BF16),
        in_specs=[pl.BlockSpec((WGRAD_SHARDS * mr, s), lambda j: (j, 0)), _whole(b.shape)],
        out_specs=pl.BlockSpec((WGRAD_SHARDS, mr, n), lambda j: (j, 0, 0)), compiler_params=_params(1, 40),
    )(a_t, b)


def _memkv_bwd(dkvs, w, mem, g):
    n_layers = g.shape[0]
    rows = D_MODEL // N_DEV

    def body(dkv0_ref, dkv1_ref, w_ref, mem_ref, g_ref, dw_ref, dg_ref):
        mb = mem_ref[...]
        r = lax.rsqrt(jnp.mean(mb * mb, axis=-1, keepdims=True) + EPS)
        nm = mb * r
        mn = (nm * g_ref[...]).astype(BF16)
        dkvb = jnp.where(pl.program_id(0) == 0, dkv0_ref[...], dkv1_ref[...]).astype(BF16)
        dw_ref[...] = _dot_tn(mn, dkvb).astype(BF16).reshape(N_DEV, rows, 2 * MEM_WIDTH)
        dmn = _dot_nt(dkvb, w_ref[...].reshape(D_MODEL, 2 * MEM_WIDTH))
        dg_ref[...] = jnp.sum(dmn * nm, axis=0, keepdims=True)

    lay = lambda *shape: pl.BlockSpec((None,) + shape, lambda l: (l, 0, 0))
    major = pl.BlockSpec((N_DEV, rows, 2 * MEM_WIDTH), lambda l: (0, l, 0))
    return _pallas_call(
        body, name="memkv_bwd", grid=(n_layers,),
        out_shape=[_sds((N_DEV, n_layers * rows, 2 * MEM_WIDTH), BF16), _plain((n_layers, 1, D_MODEL), F32)],
        in_specs=[_whole(dkvs[0].shape), _whole(dkvs[1].shape), major, _whole(mem.shape), lay(1, D_MODEL)],
        out_specs=[major, lay(1, D_MODEL)],
        compiler_params=_params(1, 32),
    )(*dkvs, w, mem, g.reshape(n_layers, 1, D_MODEL))


def _bwd_post_attn(dhb, wg_out, z, os_, ls_, qm, kv, mo, head_ones, token):
    s = dhb.shape[0]
    gw = GROUP_WIDTH
    nb = gw + MEM_WIDTH
    tm = ROW_TILE

    def body(_, dh_ref, w_ref, z_ref, o0, o1, o2, l0, l1, l2, qm_ref, kv_ref, mo_ref, bd_ref,
             dz_ref, do0, do1, do2, dl0, dl1, dl2, dqm_ref, dkv_ref, s0, s1):
        @pl.when(pl.program_id(0) == 0)
        def _():
            dkv_ref[...] = jnp.zeros_like(dkv_ref)

        dy = _dot_nt(dh_ref[...], _joined_columns(w_ref))
        ov, lv = [], []
        for o_ref, l_ref, d in zip((o0, o1, o2), (l0, l1, l2), DILATIONS):
            ov.append(_from_view(s0, o_ref, d))
            lv.append(_from_view(s1, l_ref, d))
        ws, mix = _mix_groups(ov, lv)
        sz, dsz = _silu_parts(z_ref[...])
        dz_ref[:, :gw] = (dy[:, :gw] * mix * dsz[:, :gw]).astype(BF16)
        dz_ref[:, gw:] = (dy[:, gw:] * mo_ref[...] * dsz[:, gw:]).astype(BF16)
        dbr = dy * sz
        dmix = dbr[:, :gw]
        t = dmix * mix
        th = t.astype(BF16)
        tl = (t - th.astype(F32)).astype(BF16)
        rs = _dot(th, bd_ref[...]) + _dot(tl, bd_ref[...])
        for wg_, do_ref, dl_ref, d in zip(ws, (do0, do1, do2), (dl0, dl1, dl2), DILATIONS):
            _to_view(s0, wg_ * dmix, do_ref, d)
            _to_view(s1, wg_ * rs, dl_ref, d)
        _mem_attn_bwd(qm_ref[...], kv_ref, dbr[:, gw:], dqm_ref, dkv_ref)

    vspecs = [_view_rows(gw, d) for d in DILATIONS]
    return _pallas_call(
        body, name="bwd_post_attn", grid=(s // tm,),
        out_shape=[_sds((s, nb), BF16)] + [_sds((s // d, d * gw), BF16) for d in DILATIONS]
                  + [_sds((s // d, d * gw), F32) for d in DILATIONS] + [_sds((s, MEM_WIDTH), BF16), _sds(kv.shape, F32)],
        in_specs=[ANY, _rows(D_MODEL), _whole(wg_out.shape), _rows(nb)] + vspecs * 2
                 + [_rows(MEM_WIDTH), _whole(kv.shape), _rows(MEM_WIDTH), _whole(head_ones.shape)],
        out_specs=[_rows(nb)] + vspecs * 2 + [_rows(MEM_WIDTH), _whole(kv.shape)],
        scratch_shapes=[_view_scratch(gw), _view_scratch(gw)],
        compiler_params=_params(1, 48),
    )(token, dhb, wg_out, z, *os_, *ls_, qm, kv, mo, head_ones)


def _attn_bwd(q, k, v, lse, do, dl, tabs, d, token):
    ln, dw = q.shape
    w = dw // d
    nb = ln // BLOCK
    reps = w // LANES
    two, before = _pair_specs(d, nb, w)
    two_t, _ = _pair_specs(d, nb, LANES)

    def attend(q_ref, l_ref, do_ref, dl_ref, dqs, acck, accv, rows, col0, kk, vv, acc_rows):
        valid = _band_mask(kk.shape[0])
        low = _low_head_lanes()
        pairs, qcols = _head_tiles(w, col0)
        cols = [slice(col0 + h * HEAD_DIM, col0 + h * HEAD_DIM + 1) for h in range(HEADS_PER_GROUP)]
        qhs = [h for qc in qcols for h in _split_pair(q_ref[rows, qc], low)]
        dobs = [h for qc in qcols for h in _split_pair(do_ref[rows, qc], low)]
        k2s = [kk[:, pr] for pr in pairs for _ in range(2)]
        v2s = [vv[:, pr] for pr in pairs for _ in range(2)]
        scs = [jnp.where(valid, _dot_nt(qh, k2), NEG) for qh, k2 in zip(qhs, k2s)]
        dps = [_dot_nt(dob, v2) for dob, v2 in zip(dobs, v2s)]
        ps = [jnp.exp(sc - l_ref[rows, col]) for sc, col in zip(scs, cols)]
        dss = [(p * (dp - dl_ref[rows, col])).astype(BF16) for p, dp, col in zip(ps, dps, cols)]
        pbs = [p.astype(BF16) for p in ps]
        for i, qc in enumerate(qcols):
            a, b = 2 * i, 2 * i + 1
            dqs[rows, qc] = jnp.where(low, _dot(dss[a], k2s[a]), _dot(dss[b], k2s[b])) * SCALE
            acck[acc_rows, qc] += _dot_tn(dss[a], qhs[a]) + _dot_tn(dss[b], qhs[b])
            accv[acc_rows, qc] += _dot_tn(pbs[a], dobs[a]) + _dot_tn(pbs[b], dobs[b])

    def body_streams(_, q_ref, kc_ref, vc_ref, l_ref, do_ref, dl_ref, c_ref, sa_ref, sb_ref,
                     dq_ref, dk_ref, dv_ref, acck, accv, dqs):
        acck[...] = jnp.zeros_like(acck)
        accv[...] = jnp.zeros_like(accv)
        for sb in range(2):
            cols = slice(sb * w, (sb + 1) * w)
            attend(q_ref, l_ref, do_ref, dl_ref, dqs, acck, accv, TOP, sb * w, kc_ref[:, cols], vc_ref[:, cols], TOP)
        tabs2 = [jnp.concatenate([jnp.tile(r[:, sb * LANES:(sb + 1) * LANES], (1, reps)) for sb in range(2)], axis=1)
                 for r in (c_ref, sa_ref, sb_ref)]
        dq_ref[...] = _rope_bwd(dqs[...], *tabs2).astype(BF16)
        dk_ref[...] = _rope_bwd(acck[...], *tabs2).astype(BF16)
        dv_ref[...] = accv[...].astype(BF16)

    def body_blocks(_, q_ref, kp_ref, kc_ref, vp_ref, vc_ref, l_ref, do_ref, dl_ref, cq, saq, sbq, ck, sak, sbk,
                    dq_ref, dk_ref, dv_ref, acck, accv, dqs):
        i = pl.program_id(0) % (nb // 2)

        @pl.when(i == 0)
        def _():
            acck[...] = jnp.zeros_like(acck)
            accv[...] = jnp.zeros_like(accv)

        refs = (q_ref, l_ref, do_ref, dl_ref, dqs, acck, accv)
        pl.when(i == 0)(lambda: attend(*refs, TOP, 0, kc_ref[TOP, :], vc_ref[TOP, :], TOP))
        pl.when(i != 0)(lambda: attend(
            *refs, TOP, 0, jnp.concatenate([kp_ref[...], kc_ref[TOP, :]], axis=0),
            jnp.concatenate([vp_ref[...], vc_ref[TOP, :]], axis=0),
            pl.ds(pl.multiple_of((2 * i - 1) * BLOCK, BLOCK), 2 * BLOCK)))
        attend(*refs, BOTTOM, 0, kc_ref[...], vc_ref[...], pl.ds(pl.multiple_of(2 * i * BLOCK, BLOCK), 2 * BLOCK))
        tq = [jnp.tile(r[...], (1, reps)) for r in (cq, saq, sbq)]
        dq_ref[...] = _rope_bwd(dqs[...], *tq).astype(BF16)

        @pl.when(i == nb // 2 - 1)
        def _():
            for r0 in range(0, nb * BLOCK, 2 * BLOCK):
                rows = slice(r0, r0 + 2 * BLOCK)
                tk = [jnp.tile(r[rows, :], (1, reps)) for r in (ck, sak, sbk)]
                dk_ref[rows, :] = _rope_bwd(acck[rows, :], *tk).astype(BF16)
                dv_ref[rows, :] = accv[rows, :].astype(BF16)

    if nb == 1:
        body = body_streams
        in_specs = [ANY] + [two] * 6 + [two_t] * 3
        args = (token, q, k, v, lse, do, dl, *tabs)
        out_specs = [two, two, two]
        acc_shape = (BLOCK, 2 * w)
    else:
        body = body_blocks
        stream = pl.BlockSpec((nb * BLOCK, w), lambda n: (0, n // (nb // 2)))
        stream_t = pl.BlockSpec((nb * BLOCK, LANES), lambda n: (0, n // (nb // 2)))
        in_specs = [ANY, two, before, two, before, two, two, two, two] + [two_t] * 3 + [stream_t] * 3
        args = (token, q, k, k, v, v, lse, do, dl, *tabs, *tabs)
        out_specs = [two, stream, stream]
        acc_shape = (nb * BLOCK, w)
    return _pallas_call(
        body, name=f"attn_bwd_d{d}", grid=(d * nb // 2,), out_shape=[_sds((ln, dw), BF16)] * 3,
        in_specs=in_specs, out_specs=out_specs,
        scratch_shapes=[pltpu.VMEM(acc_shape, F32), pltpu.VMEM(acc_shape, F32), pltpu.VMEM(two.block_shape, F32)],
        compiler_params=_params(1, 48),
    )(*args)


def _position():
    return lax.axis_index("x"), lax.axis_index("y"), lax.axis_index("c")


def _all_gather_small(xs, afters, name):
    n_in = 1 + len(afters)

    def body(*refs):
        x_ref, out_ref = refs[0], refs[n_in]
        send_sems, recv_sems, local_sem = refs[n_in + 1:]
        x, y, c = _position()
        my_rows = out_ref.at[4 * x + 2 * y + c]
        mine = pltpu.make_async_copy(x_ref, my_rows, local_sem)
        mine.start()
        copies = [pltpu.make_async_remote_copy(
            src_ref=x_ref, dst_ref=my_rows, send_sem=send_sems.at[k], recv_sem=recv_sems.at[k],
            device_id=(1 - x if k & 4 else x, 1 - y if k & 2 else y, 1 - c if k & 1 else c), device_id_type=MESH)
            for k in range(1, N_DEV)]
        for cp in copies:
            cp.start()
        for cp in copies:
            cp.wait_recv()
        for cp in copies:
            cp.wait_send()
        mine.wait()

    return _pallas_call(
        body, name=name, out_shape=_sds((N_DEV,) + xs.shape, xs.dtype),
        in_specs=[ANY] * n_in, out_specs=ANY,
        scratch_shapes=[pltpu.SemaphoreType.DMA((N_DEV,)), pltpu.SemaphoreType.DMA((N_DEV,)), pltpu.SemaphoreType.DMA],
    )(xs, *afters)


GATHER_CHUNKS = 4


def _all_gather_relay(xs, later, rope, name):
    n = len(later)
    n_io = 3 + n

    def body(*refs):
        x_ref, f32_refs, angle_refs = refs[0], refs[1:1 + n], refs[1 + n:n_io]
        out_ref, land_refs, table_refs = refs[n_io], refs[n_io + 1:n_io + 1 + n], refs[n_io + 1 + n:2 * n_io + 1]
        scratch = refs[2 * n_io + 1:]
        send_sems, recv_sems, local_sem, prep_sems, rope_sems = scratch[:5]
        staged, rounded = scratch[5:5 + n], scratch[5 + n:5 + 2 * n]
        angles, tables = scratch[5 + 2 * n:7 + 2 * n], scratch[7 + 2 * n:]
        x, y, c = _position()
        me, sibling = (x, y, c), (x, y, 1 - c)
        xn, yn, diag = (1 - x, y, c), (x, 1 - y, c), (1 - x, 1 - y, c)
        src_nb = (x + c * (1 - 2 * x), y + (1 - c) * (1 - 2 * y), c)
        dst_nb = (x + (1 - c) * (1 - 2 * x), y + c * (1 - 2 * y), c)

        def rows(dev, i):
            return out_ref.at[4 * dev[0] + 2 * dev[1] + dev[2], pl.ds(i * step, step)]

        def copy(k, i, block, to, own=False):
            return pltpu.make_async_remote_copy(
                src_ref=x_ref.at[pl.ds(i * step, step)] if own else rows(block, i), dst_ref=rows(block, i),
                send_sem=send_sems.at[k, i], recv_sem=recv_sems.at[k, i], device_id=to, device_id_type=MESH)

        mine = pltpu.make_async_copy(x_ref, out_ref.at[4 * x + 2 * y + c], local_sem)
        mine.start()
        sent = []
        for i in chunks:
            sent += [copy(1, i, me, xn, own=True), copy(2, i, me, yn, own=True), copy(0, i, me, sibling, own=True)]
        for cp in sent:
            cp.start()
        fetches = [pltpu.make_async_copy(f32_refs[a], staged[a], prep_sems.at[0, a]) for a in range(n)]
        fetches += [pltpu.make_async_copy(angle_refs[a], angles[a], rope_sems.at[a]) for a in range(2)]
        for cp in fetches:
            cp.start()
        placed = []
        for a in range(n):
            fetches[a].wait()
            rounded[a][...] = staged[a][...].astype(BF16)
            placed.append(pltpu.make_async_copy(rounded[a], land_refs[a].at[4 * x + 2 * y + c], prep_sems.at[1, a]))
            placed[a].start()
        for cp in fetches[n:]:
            cp.wait()
        _rope_expand(*angles, *tables)
        for a in range(3):
            placed.append(pltpu.make_async_copy(tables[a], table_refs[a], rope_sems.at[2 + a]))
            placed[-1].start()
        for i in chunks:
            copy(1, i, xn, me).wait_recv()
            copy(2, i, yn, me).wait_recv()
            onward = [copy(3, i, src_nb, dst_nb), copy(4, i, xn, sibling), copy(5, i, yn, sibling)]
            for cp in onward:
                cp.start()
            sent += onward
        for i in chunks:
            copy(3, i, diag, me).wait_recv()
            last = copy(6, i, diag, sibling)
            last.start()
            sent.append(last)
        for i in chunks:
            copy(0, i, sibling, me).wait_recv()
            for k, blk in ((4, (1 - x, y, 1 - c)), (5, (x, 1 - y, 1 - c)), (6, (1 - x, 1 - y, 1 - c))):
                copy(k, i, blk, me).wait_recv()
        for cp in sent:
            cp.wait_send()
        for cp in [mine] + placed:
            cp.wait()

    step = xs.shape[0] // GATHER_CHUNKS
    chunks = range(GATHER_CHUNKS)
    table = (rope[0].shape[0], LANES)
    res = _pallas_call(
        body, name=name,
        out_shape=[_sds((N_DEV,) + xs.shape, xs.dtype)] + [_sds((N_DEV,) + t.shape, BF16) for t in later]
                  + [_sds(table, F32)] * 3,
        in_specs=[ANY] * n_io, out_specs=[ANY] * (n_io + 1),
        scratch_shapes=[pltpu.SemaphoreType.DMA((7, GATHER_CHUNKS)), pltpu.SemaphoreType.DMA((7, GATHER_CHUNKS)),
                        pltpu.SemaphoreType.DMA, pltpu.SemaphoreType.DMA((2, n)), pltpu.SemaphoreType.DMA((5,))]
                       + [pltpu.VMEM(t.shape, F32) for t in later] + [pltpu.VMEM(t.shape, BF16) for t in later]
                       + [pltpu.VMEM(t.shape, F32) for t in rope] + [pltpu.VMEM(table, F32)] * 3,
        compiler_params=pltpu.CompilerParams(vmem_limit_bytes=32 << 20),
    )(xs, *later, *rope)
    return res[0], res[1:1 + n], res[1 + n:]


HBM_SPEC = pl.BlockSpec(memory_space=pltpu.HBM)
SEM_SPEC = pl.BlockSpec(memory_space=pltpu.SEMAPHORE)
EFFECT = pltpu.SideEffectType.DATAFLOW_SIDE_EFFECTING
def _plan_gather_own(src_refs, land_refs):
    x, y, c = _position()
    me = 4 * x + 2 * y + c
    peers = [(x, y, 1 - c), (1 - x, y, c), (x, 1 - y, c), (1 - x, 1 - y, c)]
    return [(land_refs[a].at[me], land_refs[a].at[me], (a, k), peer)
            for k, peer in enumerate(peers) for a in range(len(land_refs))]


def _plan_gather_pass(src_refs, land_refs):
    x, y, c = _position()
    chips = [(1 - x, y), (x, 1 - y), (1 - x, 1 - y)]
    return [(land_refs[a].at[4 * px + 2 * py + c], land_refs[a].at[4 * px + 2 * py + c], (a, j), (x, y, 1 - c))
            for j, (px, py) in enumerate(chips) for a in range(len(land_refs))]


def _plan_to_sibling(src_refs, land_refs):
    x, y, c = _position()
    return [(src_refs[a].at[2 * k + (1 - c)], land_refs[a].at[k], (a, k), (x, y, 1 - c))
            for k in range(4) for a in range(len(src_refs))]


def _plan_to_chips(src_refs, land_refs):
    x, y, c = _position()
    chips = [(1 - x, y), (x, 1 - y), (1 - x, 1 - y)]
    return [(src_refs[a].at[2 * px + py], land_refs[a].at[j], (a, j), (px, py, c))
            for j, (px, py) in enumerate(chips) for a in range(len(src_refs))]


def _split_start(srcs, lands, plan, n_sem, after, name):
    n_s, n_a = len(srcs), len(lands)
    n_b = n_s + n_a

    def body(*refs):
        src_refs, land_refs = refs[:n_s], refs[n_s:n_b]
        send_sems, recv_sems, token = refs[n_b + 1], refs[n_b + 2], refs[-1]
        for src, dst, (a, k), dev in plan(src_refs, land_refs):
            i = a * n_sem + k
            pltpu.make_async_remote_copy(src_ref=src, dst_ref=dst, send_sem=send_sems.at[i], recv_sem=recv_sems.at[i],
                                         device_id=dev, device_id_type=MESH).start()
        token[...] = jnp.zeros_like(token)

    bufs = list(srcs) + list(lands)
    res = pl.pallas_call(
        body, name=name,
        out_shape=(pltpu.SemaphoreType.DMA((n_a * n_sem,)), pltpu.SemaphoreType.DMA((n_a * n_sem,)),
                   *[pltpu.HBM(t.shape, t.dtype) for t in bufs], _plain((8, LANES), F32)),
        in_specs=[HBM_SPEC] * n_b + [ANY],
        out_specs=(SEM_SPEC, SEM_SPEC, *[HBM_SPEC] * n_b, pl.BlockSpec(memory_space=pltpu.VMEM)),
        input_output_aliases={i: 2 + i for i in range(n_b)},
        compiler_params=pltpu.CompilerParams(has_side_effects=EFFECT),
    )(*[pltpu.with_memory_space_constraint(t, pltpu.HBM) for t in bufs], after)
    return (res[0], res[1], res[2:2 + n_s], res[2 + n_s:2 + n_b]), res[-1]


def _split_wait(started, plan, after, name, first=0, n_sem=None):
    send_sems, recv_sems, srcs, lands = started
    n_s, n_a = len(srcs), len(lands)
    n_b = n_s + n_a
    n_sem = n_sem or send_sems.shape[0] // n_a

    def body(*refs):
        src_refs, land_refs = refs[:n_s], refs[n_s:n_b]
        s_sems, r_sems = refs[n_b], refs[n_b + 1]
        for src, dst, (a, k), dev in plan(src_refs, land_refs):
            i = (first + a) * n_sem + k
            cp = pltpu.make_async_remote_copy(src_ref=src, dst_ref=dst, send_sem=s_sems.at[i], recv_sem=r_sems.at[i],
                                              device_id=dev, device_id_type=MESH)
            cp.wait_send()
            cp.wait_recv()

    bufs = list(srcs) + list(lands)
    res = pl.pallas_call(
        body, name=name, out_shape=tuple(pltpu.HBM(t.shape, t.dtype) for t in bufs),
        in_specs=[HBM_SPEC] * n_b + [SEM_SPEC, SEM_SPEC, ANY],
        out_specs=tuple([HBM_SPEC] * n_b),
        input_output_aliases={i: i for i in range(n_b)},
        compiler_params=pltpu.CompilerParams(has_side_effects=EFFECT),
    )(*bufs, send_sems, recv_sems, after)
    return res[:n_s], res[n_s:]


SUBLANES = 8


def _row_tile(r):
    return max(t for t in range(SUBLANES, ROW_TILE + 1, SUBLANES) if r % t == 0)


def _rs_add_sibling(gps, recvs, ck_arr, name):
    n = len(gps)
    block = lambda k, ck: (k + ck[1] + 1) % 4

    def body(ck_ref, *refs):
        for g_ref, r_ref, pf_ref, pb_ref in zip(refs[:n], refs[n:2 * n], refs[2 * n::2], refs[2 * n + 1::2]):
            sm = g_ref[...].astype(F32) + r_ref[...].astype(F32)
            pf_ref[...] = sm
            pb_ref[...] = sm.astype(BF16)

    mine = lambda t: pl.BlockSpec((None,) + t.shape[1:], lambda k, ck: (2 * block(k, ck) + ck[0], 0, 0))
    one = lambda t: pl.BlockSpec((None,) + t.shape[1:], lambda k, ck: (block(k, ck), 0, 0))
    res = _pallas_call(
        body, name=name,
        grid_spec=pltpu.PrefetchScalarGridSpec(
            num_scalar_prefetch=1, grid=(4,),
            in_specs=[mine(t) for t in gps] + [one(t) for t in recvs],
            out_specs=[s for t in recvs for s in (pl.BlockSpec(t.shape[1:], lambda k, ck: (0, 0)), one(t))]),
        out_shape=[s for t in recvs for s in (_sds(t.shape[1:], F32), _sds(t.shape, BF16))],
        compiler_params=_params(1, 48),
    )(ck_arr, *gps, *recvs)
    return list(zip(res[0::2], res[1::2]))


def _adam_update(w, gv, m, v):
    nm = ADAM_B1 * m + (1.0 - ADAM_B1) * gv
    nv = ADAM_B2 * v + (1.0 - ADAM_B2) * (gv * gv)
    m_hat = nm / (1.0 - ADAM_B1 ** ADAM_STEP)
    v_hat = nv / (1.0 - ADAM_B2 ** ADAM_STEP)
    return -ADAM_LR * (m_hat / (jnp.sqrt(v_hat) + ADAM_EPS) + ADAM_WD * w), nm, nv


def _rs_finish_adamw(pf, recv, w, m, v, after, name):
    r, l = pf.shape
    tr = _row_tile(r)

    def body(_, p_ref, r_ref, w_ref, m_ref, v_ref, g_ref, d_ref, nm_ref, nv_ref):
        gv = ((p_ref[...] + r_ref[0].astype(F32)) + r_ref[1].astype(F32)) + r_ref[2].astype(F32)
        g_ref[...] = gv
        d_ref[...], nm_ref[...], nv_ref[...] = _adam_update(w_ref[...], gv, m_ref[...], v_ref[...])

    spec = pl.BlockSpec((tr, l), lambda i: (i, 0))
    return _pallas_call(
        body, name=name, grid=(r // tr,),
        in_specs=[ANY, spec, pl.BlockSpec((3, tr, l), lambda i: (0, i, 0)), spec, spec, spec], out_specs=[spec] * 4,
        out_shape=[_plain((r, l), F32)] * 4, compiler_params=_params(1, 32),
    )(after, pf, recv, w, m, v)


SMALL_ROWS = dict(norm_g=(0, 2), mem_norm_g=(2, 4), final_g=(4, 5), conv_w=(5, 8))
LOSS_ROWS = (8, 16)


def _sum_adamw_small(g, ck_arr, states):
    names = list(SMALL_ROWS)
    n_dev, n_rows, _ = g.shape

    def body(ck_ref, g_ref, gc_ref, *refs):
        ins, loss_ref, outs = refs[:3 * len(names)], refs[3 * len(names)], refs[3 * len(names) + 1:]

        def total(ref, lo, hi):
            acc = ref[0, lo:hi, :]
            for j in range(1, n_dev):
                acc = acc + ref[j, lo:hi, :]
            return acc

        loss_ref[...] = total(gc_ref, *LOSS_ROWS)
        for i, n in enumerate(names):
            gv = total(gc_ref if n == "conv_w" else g_ref, *SMALL_ROWS[n])
            w_ref, m_ref, v_ref = ins[3 * i:3 * i + 3]
            g_out, d_out, nm_out, nv_out = outs[4 * i:4 * i + 4]
            g_out[...] = gv
            d_out[...], nm_out[...], nv_out[...] = _adam_update(w_ref[...], gv, m_ref[...], v_ref[...])

    flat = [t for n in names for t in states[n]]
    mine = pl.BlockSpec((n_dev, n_rows, LANES), lambda i, ck: (0, 0, 2 * ck[1] + ck[0]))
    res = _pallas_call(
        body, name="sum_adamw_small",
        grid_spec=pltpu.PrefetchScalarGridSpec(
            num_scalar_prefetch=1, grid=(1,),
            in_specs=[_whole(g.shape), mine] + [_whole(t.shape) for t in flat],
            out_specs=[_whole((SUBLANES, LANES))] + [_whole(states[n][0].shape) for n in names for _ in range(4)]),
        out_shape=[_plain((SUBLANES, LANES), F32)] + [_plain(states[n][0].shape, F32) for n in names for _ in range(4)],
        compiler_params=_params(1, 32),
    )(ck_arr, g, g, *flat)
    return res[0], {n: tuple(res[1 + 4 * i:5 + 4 * i]) for i, n in enumerate(names)}


def _finish(name, pf, recv, w, m, v, after):
    if name in ("attn_w_in", "conv_w_in"):
        res = _rs_finish_adamw(pf, recv, w.T, m.T, v.T, after, "rs_finish_adamw_" + name)
        return tuple(t.T for t in res)
    return _rs_finish_adamw(pf, recv, w, m, v, after, "rs_finish_adamw_" + name)


def kernel(x, mem, positions, norm_g, mem_norm_g, w_mem_kv, attn_w_in, attn_w_out, conv_w_in, conv_w, conv_w_out, final_g, loss_target, m_norm_g, m_mem_norm_g, m_w_mem_kv, m_attn_w_in, m_attn_w_out, m_conv_w_in, m_conv_w, m_conv_w_out, m_final_g, v_norm_g, v_mem_norm_g, v_w_mem_kv, v_attn_w_in, v_attn_w_out, v_conv_w_in, v_conv_w, v_conv_w_out, v_final_g):
    px, py, pc = _position()
    me = 4 * px + 2 * py + pc
    ck_arr = jnp.stack([pc, 2 * px + py]).astype(jnp.int32)
    x, mem, pos, tgt = x[0], mem[0], positions[0], loss_target[0]

    wg_in0, (land_out0, land_kv, land_in1, land_out1), tabs = _all_gather_relay(
        attn_w_in[0].astype(BF16),
        [attn_w_out[0], w_mem_kv.reshape(-1, w_mem_kv.shape[2]), conv_w_in[0], conv_w_out[0]],
        _rope_angles(pos), "gather_w_in0")

    def gather_pass(weights, after, name, first):
        _, lands = _split_wait(weights, _plan_gather_own, after, name + "_wait", first, 4)
        return _split_start([], lands, _plan_gather_pass, 3, after, name + "_pass_start")

    taps = jnp.pad(conv_w[0], ((0, 5), (0, 0)))
    land_taps = lax.dynamic_update_slice(lax.empty((N_DEV,) + taps.shape, taps.dtype), taps[None], (me, 0, 0))
    lands = [land_out0, land_kv, land_taps, land_in1, land_out1]
    (send_sems, recv_sems, _, lands), token = _split_start([], lands, _plan_gather_own, 4, wg_in0, "gather_late_start")
    rest0, conv_ws = (send_sems, recv_sems, [], lands[:3]), (send_sems, recv_sems, [], lands[3:])

    g0, g1 = norm_g[0:1], norm_g[1:2]

    hn0, qs, ks, vs, tabs_v, qm0, z0 = _inproj_attn(x, g0, wg_in0, tabs, token)
    os_, ls_ = [], []
    for j, d in enumerate(DILATIONS):
        if j == 2:
            rest0, token = gather_pass(rest0, ls_[1], "gather_rest", 0)
        o, l = _attn_fwd(qs[j], ks[j], vs[j], d, token)
        os_.append(o)
        ls_.append(l)

    conv_ws, token = gather_pass(conv_ws, ls_[2], "gather_conv", 3)
    _, (wg_out0, wg_kv, cw_all) = _split_wait(rest0, _plan_gather_pass, token, "gather_rest_pass_wait")
    cw = cw_all[:, 0:3].transpose(1, 0, 2).reshape(3, -1)
    kv = _memkv_fwd(mem, mem_norm_g, wg_kv)
    y0, y0_t, mo0, h1 = _post_attn(os_, ls_, qm0, kv[0], z0, x, wg_out0)
    _, (wg_in1, wg_out1) = _split_wait(conv_ws, _plan_gather_pass, h1, "gather_conv_pass_wait")
    w_out1 = wg_out1.reshape(-1, wg_out1.shape[2])

    hn1, bg, cg, u, qm1, z1 = _inproj_conv(h1, g1, wg_in1)
    y1, y1_t, mo1, dh2, dh2b, loss_acc, d_final_g = _post_conv_loss(
        bg, cg, u, qm1, kv[1], z1, h1, w_out1, cw, final_g.reshape(1, -1), tgt)

    d_w_out1 = _wgrad_rows(y1_t, dh2b, "wgrad_out1")
    dz1, dbg, dconv, dqm1, dkv1 = _bwd_post_conv(dh2b, w_out1, bg, cg, u, z1, qm1, kv[1], mo1, cw)
    dcg, du, dcw = _bwd_conv(dconv, cg, u, cw)
    dh1, dg1, dh1b, dproj1_t = _dgrad_norm([(dbg, 1), (dcg, 1), (du, 1), (dqm1, 1), (dz1, 1)], wg_in1, h1, g1, dh2,
                                           dh2, True, "dgrad_norm_conv")
    d_w_in1 = _wgrad_shards_t(dproj1_t, hn1, wg_in1.shape[2], "wgrad_in1")

    d_w_out0 = _wgrad_cols(y0_t, dh1b, wg_out0.shape[2], "wgrad_out0")

    gw = GROUP_WIDTH
    ones = (jnp.arange(gw)[:, None] // HEAD_DIM == jnp.arange(gw)[None, :] // HEAD_DIM).astype(BF16)
    res = _bwd_post_attn(dh1b, wg_out0, z0, os_, ls_, qm0, kv[0], mo0, ones, dg1)
    dz0, dos, dls, dqm0, dkv0 = res[0], res[1:4], res[4:7], res[7], res[8]
    d_w_kv, d_mem_g = _memkv_bwd([dkv0, dkv1], wg_kv, mem, mem_norm_g)

    names1 = ["conv_w_in", "conv_w_out", "attn_w_out", "w_mem_kv"]
    grads1 = [d_w_in1, d_w_out1, d_w_out0, d_w_kv]
    started, token = _split_start(grads1, [lax.empty((4,) + g.shape[1:], g.dtype) for g in grads1],
                                  _plan_to_sibling, 4, d_mem_g, "rs1_sibling_start")

    dqs, dks, dvs = [], [], []
    for j, d in enumerate(DILATIONS):
        if j == 1:
            grads1, from_sibling = _split_wait(started, _plan_to_sibling, dqs[0][0], "rs1_sibling_wait")
            parts1 = _rs_add_sibling(grads1, from_sibling, ck_arr, "rs1_add_sibling")
            pbs1 = [pb for _, pb in parts1]
            started, token = _split_start(pbs1, [lax.empty((3,) + p.shape[1:], p.dtype) for p in pbs1],
                                          _plan_to_chips, 3, dg1, "rs1_chips_start")
        dq, dk, dv = _attn_bwd(qs[j], ks[j], vs[j], ls_[j], dos[j], dls[j], tabs_v[j], d, token)
        dqs.append((dq, d))
        dks.append((dk, d))
        dvs.append((dv, d))
    pieces0 = dqs + dks + dvs + [(dqm0, 1), (dz0, 1)]
    dproj0_t = _assemble_dproj_t(pieces0, N_DEV * wg_in0.shape[2], "assemble_dproj_attn")
    d_w_in0 = _wgrad_shards_t(dproj0_t, hn0, wg_in0.shape[2], "wgrad_in0")

    names0 = ["attn_w_in"]
    grads0 = [d_w_in0]
    started0, token0 = _split_start(grads0, [lax.empty((4,) + g.shape[1:], g.dtype) for g in grads0],
                                    _plan_to_sibling, 4, dg1, "rs0_sibling_start")
    _, from_chips1 = _split_wait(started, _plan_to_chips, token0, "rs1_chips_wait")
    shard = dict(attn_w_in=(attn_w_in[0], m_attn_w_in[0], v_attn_w_in[0]),
                 attn_w_out=(attn_w_out[0], m_attn_w_out[0], v_attn_w_out[0]),
                 conv_w_in=(conv_w_in[0], m_conv_w_in[0], v_conv_w_in[0]),
                 conv_w_out=(conv_w_out[0], m_conv_w_out[0], v_conv_w_out[0]),
                 w_mem_kv=tuple(t.reshape(-1, t.shape[2]) for t in (w_mem_kv, m_w_mem_kv, v_w_mem_kv)))
    finish1 = {n: (pf, r) for n, (pf, _), r in zip(names1, parts1, from_chips1)}
    big = {"conv_w_in": _finish("conv_w_in", *finish1["conv_w_in"], *shard["conv_w_in"], token0)}

    grads0, from_sibling = _split_wait(started0, _plan_to_sibling, big["conv_w_in"][1].T, "rs0_sibling_wait")
    parts0 = _rs_add_sibling(grads0, from_sibling, ck_arr, "rs0_add_sibling")
    pbs0 = [pb for _, pb in parts0]
    started0, token0 = _split_start(pbs0, [lax.empty((3,) + p.shape[1:], p.dtype) for p in pbs0],
                                    _plan_to_chips, 3, dg1, "rs0_chips_start")
    dx, dg0 = _dgrad_norm(pieces0, wg_in0, x, g0, dh1, token0, False, "dgrad_norm_attn")
    for n in names1[1:]:
        big[n] = _finish(n, *finish1[n], *shard[n], token0)

    small_part = jnp.concatenate([dg0, dg1, d_mem_g.reshape(2, -1), d_final_g, dcw[0:3]], axis=0)
    small_part = jnp.concatenate([small_part, jnp.broadcast_to(loss_acc[0, 0], small_part.shape)], axis=0)
    small_w = dict(norm_g=(norm_g, m_norm_g, v_norm_g), mem_norm_g=(mem_norm_g, m_mem_norm_g, v_mem_norm_g),
                   conv_w=(conv_w, m_conv_w, v_conv_w), final_g=(final_g, m_final_g, v_final_g))
    loss_tile, small_res = _sum_adamw_small(
        _all_gather_small(small_part, [big[n][1] for n in names1[1:]], "gather_small_grads"), ck_arr,
        {n: tuple(t.reshape(-1, t.shape[-1]) for t in wmv) for n, wmv in small_w.items()})
    loss = loss_tile[0, 0]
    for n, wmv in small_w.items():
        big[n] = tuple(t.reshape(wmv[0].shape) for t in small_res[n])

    _, from_chips0 = _split_wait(started0, _plan_to_chips, big["final_g"][1], "rs0_chips_wait")
    for n, (pf, _), r in zip(names0, parts0, from_chips0):
        big[n] = _finish(n, pf, r, *shard[n], big["final_g"][1])
    for n in ("attn_w_in", "attn_w_out", "conv_w_in", "conv_w_out"):
        big[n] = tuple(t[None] for t in big[n])
    big["w_mem_kv"] = tuple(t.reshape(w_mem_kv.shape) for t in big["w_mem_kv"])

    order = ["norm_g", "mem_norm_g", "w_mem_kv", "attn_w_in", "attn_w_out", "conv_w_in", "conv_w", "conv_w_out", "final_g"]
    return (loss, dx[None], *[big[n][0] for n in order], *[big[n][1] for n in order],
            *[big[n][2] for n in order], *[big[n][3] for n in order])
```

```python
import jax
import jax.numpy as jnp
from jax import lax
from jax.experimental import pallas as pl
from jax.experimental.pallas import tpu as pltpu

F32 = jnp.float32
BF16 = jnp.bfloat16

N_DEV = 8
D_MODEL = 1024
HEAD_DIM = 64
ROT_DIM = HEAD_DIM // 4
ROPE_THETA = 500000.0
DILATIONS = (1, 4, 16)
HEADS_PER_GROUP = 8
GROUP_WIDTH = HEADS_PER_GROUP * HEAD_DIM
BLOCK = 128
N_MEM = 256
MEM_HEADS = 4
MEM_WIDTH = MEM_HEADS * HEAD_DIM
CONV_WIDTH = D_MODEL
EPS = 1e-6
SCALE = HEAD_DIM ** -0.5
NEG = -1e30

ADAM_LR = 0.001
ADAM_B1 = 0.9
ADAM_B2 = 0.999
ADAM_EPS = 1e-08
ADAM_WD = 0.01
ADAM_STEP = 10

ROW_TILE = 256
WGRAD_SHARDS = 4
LANES = 128
MESH = pl.DeviceIdType.MESH
ANY = pl.BlockSpec(memory_space=pl.ANY)


def _pallas_call(body, **kw):
    call = pl.pallas_call(body, **kw)

    def run(*args):
        pinned = [pltpu.with_memory_space_constraint(a, pltpu.HBM) if jnp.issubdtype(a.dtype, jnp.floating) else a
                  for a in args]
        return call(*pinned)

    return run


def _dot(a, b):
    return lax.dot_general(a, b, (((1,), (0,)), ((), ())), preferred_element_type=F32)


def _dot_nt(a, b):
    return lax.dot_general(a, b, (((1,), (1,)), ((), ())), preferred_element_type=F32)


def _dot_tn(a, b):
    return lax.dot_general(a, b, (((0,), (0,)), ((), ())), preferred_element_type=F32)


def _params(n_grid, vmem_mb=48):
    return pltpu.CompilerParams(dimension_semantics=("arbitrary",) * n_grid, vmem_limit_bytes=vmem_mb << 20)


def _rows(width, tm=ROW_TILE):
    return pl.BlockSpec((tm, width), lambda i: (i, 0))


def _view_rows(width, d, tm=ROW_TILE):
    return pl.BlockSpec((tm // d, d * width), lambda i: (i, 0))


def _whole(shape):
    return pl.BlockSpec(shape, lambda *_: (0,) * len(shape))


def _resident(shape):
    return pl.BlockSpec(shape, lambda *_: (0,) * len(shape), pipeline_mode=pl.Buffered(1))


def _sds(shape, dtype):
    return pltpu.HBM(shape, dtype)


def _plain(shape, dtype):
    return jax.ShapeDtypeStruct(shape, dtype)


def _silu_parts(z):
    sg = jax.nn.sigmoid(z)
    return z * sg, sg * (1.0 + z * (1.0 - sg))


def _to_view(scr, val, out_ref, d):
    tm, w = val.shape
    if d == 1:
        out_ref[...] = val.astype(out_ref.dtype)
        return
    for cb in range(w // LANES):
        scr[cb] = val[:, cb * LANES:(cb + 1) * LANES]
    for r in range(d):
        for cb in range(w // LANES):
            lo = r * w + cb * LANES
            out_ref[:, lo:lo + LANES] = scr[cb, pl.ds(r, tm // d, stride=d), :].astype(out_ref.dtype)


def _from_view(scr, in_ref, d):
    if d == 1:
        return in_ref[...].astype(F32)
    nc, tm, _ = scr.shape
    w = nc * LANES
    for r in range(d):
        for cb in range(nc):
            lo = r * w + cb * LANES
            scr[cb, pl.ds(r, tm // d, stride=d), :] = in_ref[:, lo:lo + LANES].astype(F32)
    return jnp.concatenate([scr[cb] for cb in range(nc)], axis=1)


def _view_scratch(width, tm=ROW_TILE):
    return pltpu.VMEM((width // LANES, tm, LANES), F32)


def _rope_angles(pos):
    half = ROT_DIM // 2
    inv_freq = ROPE_THETA ** (-jnp.arange(half, dtype=F32) * (2.0 / ROT_DIM))
    ang = inv_freq[:, None] * pos.astype(F32)[None, :]
    return jnp.cos(ang), jnp.sin(ang)


def _rope_expand(cos_ref, sin_ref, c_ref, sa_ref, sb_ref):
    half = ROT_DIM // 2
    tm = ROW_TILE
    f_idx = lax.broadcasted_iota(jnp.int32, (half, LANES), 0)
    lane = lax.broadcasted_iota(jnp.int32, (half, LANES), 1) % HEAD_DIM
    low, high = (lane == f_idx).astype(F32), (lane == f_idx + half).astype(F32)
    outside = lax.broadcasted_iota(jnp.int32, (tm, LANES), 1) % HEAD_DIM >= ROT_DIM
    place = lambda t, p: lax.dot_general(t, p, (((0,), (0,)), ((), ())), precision=lax.Precision.HIGHEST,
                                         preferred_element_type=F32)
    for r0 in range(0, cos_ref.shape[1], tm):
        ct, st = cos_ref[:, r0:r0 + tm], sin_ref[:, r0:r0 + tm]
        c_ref[r0:r0 + tm, :] = jnp.where(outside, 1.0, place(ct, low + high))
        sa_ref[r0:r0 + tm, :] = place(st, -low)
        sb_ref[r0:r0 + tm, :] = place(st, high)


def _rope_fwd(t, cv, sav, sbv):
    w = t.shape[1]
    return t * cv + pltpu.roll(t, w - ROT_DIM // 2, 1) * sav + pltpu.roll(t, ROT_DIM // 2, 1) * sbv


def _rope_bwd(g, cv, sav, sbv):
    w = g.shape[1]
    return g * cv + pltpu.roll(g * sav, ROT_DIM // 2, 1) + pltpu.roll(g * sbv, w - ROT_DIM // 2, 1)


def _joined_columns(wg_ref):
    assert wg_ref.shape[2] % LANES == 0
    return jnp.concatenate([wg_ref[j] for j in range(N_DEV)], axis=1)


def _join_once(wg_ref, w_scr):
    c = wg_ref.shape[2]

    @pl.when(pl.program_id(0) == 0)
    def _():
        for j in range(N_DEV):
            w_scr[:, j * c:(j + 1) * c] = wg_ref[j]


def _inproj_attn(x, g, wg, tabs, token):
    s, d_model = x.shape
    gw = GROUP_WIDTH
    n = N_DEV * wg.shape[2]
    nz = n - 9 * gw - MEM_WIDTH
    reps = gw // LANES
    tm = ROW_TILE

    def body(_, x_ref, g_ref, w_ref, c_ref, sa_ref, sb_ref, hn_ref, *rest):
        outs, (wj, scr, tscr) = rest[:-3], rest[-3:]
        q_refs, k_refs, v_refs, t_refs, qm_ref, z_ref = outs[0:3], outs[3:6], outs[6:9], outs[9:18], outs[18], outs[19]
        _join_once(w_ref, wj)
        xb = x_ref[...]
        r = lax.rsqrt(jnp.mean(xb * xb, axis=-1, keepdims=True) + EPS)
        hn = ((xb * r) * g_ref[...]).astype(BF16)
        hn_ref[...] = hn
        proj = lambda lo, hi: _dot(hn, wj[:, lo:hi])
        tab = (c_ref[...], sa_ref[...], sb_ref[...])
        cv, sav, sbv = [jnp.tile(t, (1, reps)) for t in tab]
        for j, d in enumerate(DILATIONS):
            tq = _rope_fwd(proj(j * gw, (j + 1) * gw), cv, sav, sbv)
            _to_view(scr, tq * SCALE, q_refs[j], d)
            tk = _rope_fwd(proj((3 + j) * gw, (4 + j) * gw), cv, sav, sbv)
            _to_view(scr, tk, k_refs[j], d)
            _to_view(scr, proj((6 + j) * gw, (7 + j) * gw), v_refs[j], d)
            for i in range(3):
                _to_view(tscr, tab[i], t_refs[3 * j + i], d)
        qm_ref[...] = proj(9 * gw, 9 * gw + MEM_WIDTH).astype(BF16)
        z_ref[...] = proj(9 * gw + MEM_WIDTH, n)

    views = [_sds((s // d, d * gw), BF16) for d in DILATIONS]
    tviews = [_sds((s // d, d * LANES), F32) for d in DILATIONS for _ in range(3)]
    out_shape = [_sds((s, d_model), BF16)] + views * 3 + tviews + [_sds((s, MEM_WIDTH), BF16), _sds((s, nz), F32)]
    vspecs = [_view_rows(gw, d, tm) for d in DILATIONS]
    tspecs = [_view_rows(LANES, d, tm) for d in DILATIONS for _ in range(3)]
    out_specs = [_rows(d_model, tm)] + vspecs * 3 + tspecs + [_rows(MEM_WIDTH, tm), _rows(nz, tm)]
    res = _pallas_call(
        body, name="inproj_attn", grid=(s // tm,), out_shape=out_shape,
        in_specs=[ANY, _rows(d_model, tm), _whole((1, d_model)), _resident(wg.shape)] + [_rows(LANES, tm)] * 3,
        out_specs=out_specs,
        scratch_shapes=[pltpu.VMEM((d_model, n), BF16), _view_scratch(gw, tm), _view_scratch(LANES, tm)],
        compiler_params=_params(1, 60),
    )(token, x, g, wg, *tabs)
    tabs_v = [res[10 + 3 * j:13 + 3 * j] for j in range(3)]
    return res[0], res[1:4], res[4:7], res[7:10], tabs_v, res[19], res[20]


def _band_mask(n_keys):
    qi = lax.broadcasted_iota(jnp.int32, (BLOCK, n_keys), 0)
    kj = lax.broadcasted_iota(jnp.int32, (BLOCK, n_keys), 1)
    if n_keys == BLOCK:
        return kj <= qi
    return jnp.logical_or(jnp.logical_and(kj < BLOCK, kj >= qi), jnp.logical_and(kj >= BLOCK, (kj - BLOCK) <= qi))


def _low_head_lanes():
    return lax.broadcasted_iota(jnp.int32, (1, LANES), 1) < HEAD_DIM


def _split_pair(t, low):
    zero = jnp.zeros_like(t)
    return jnp.where(low, t, zero), jnp.where(low, zero, t)


def _pair_specs(d, nb, w):
    if nb == 1:
        return pl.BlockSpec((BLOCK, 2 * w), lambda n: (0, n)), None
    half = nb // 2
    two = pl.BlockSpec((2 * BLOCK, w), lambda n: (n % half, n // half))
    before = pl.BlockSpec((BLOCK, w), lambda n: (jnp.maximum(2 * (n % half) - 1, 0), n // half))
    return two, before


def _head_tiles(w, col0):
    return ([slice(p * LANES, (p + 1) * LANES) for p in range(w // LANES)],
            [slice(col0 + p * LANES, col0 + (p + 1) * LANES) for p in range(w // LANES)])


def _attend_fwd(q_ref, o_ref, lse_ref, rows, col0, kk, vv):
    w = kk.shape[1]
    valid = _band_mask(kk.shape[0])
    low = _low_head_lanes()
    pairs, qcols = _head_tiles(w, col0)
    qs_ = [h for qc in qcols for h in _split_pair(q_ref[rows, qc], low)]
    k2s = [kk[:, pr] for pr in pairs for _ in range(2)]
    scs = [jnp.where(valid, _dot_nt(qh, k2), NEG) for qh, k2 in zip(qs_, k2s)]
    ms = [jnp.max(sc, axis=-1, keepdims=True) for sc in scs]
    ps = [jnp.exp(sc - m) for sc, m in zip(scs, ms)]
    ls = [jnp.sum(p, axis=-1, keepdims=True) for p in ps]
    pns = [(p * (1.0 / l)).astype(BF16) for p, l in zip(ps, ls)]
    for i, (pr, qc) in enumerate(zip(pairs, qcols)):
        v2 = vv[:, pr]
        a, b = 2 * i, 2 * i + 1
        o_ref[rows, qc] = jnp.where(low, _dot(pns[a], v2), _dot(pns[b], v2))
        lse_ref[rows, qc] = jnp.where(low, ms[a] + jnp.log(ls[a]), ms[b] + jnp.log(ls[b]))


TOP, BOTTOM = slice(0, BLOCK), slice(BLOCK, 2 * BLOCK)


def _attn_fwd(q, k, v, d, token):
    ln, dw = q.shape
    w = dw // d
    nb = ln // BLOCK
    two, before = _pair_specs(d, nb, w)

    def body_streams(_, q_ref, kc_ref, vc_ref, o_ref, lse_ref):
        for sb in range(2):
            cols = slice(sb * w, (sb + 1) * w)
            _attend_fwd(q_ref, o_ref, lse_ref, TOP, sb * w, kc_ref[:, cols], vc_ref[:, cols])

    def body_blocks(_, q_ref, kp_ref, kc_ref, vp_ref, vc_ref, o_ref, lse_ref):
        first = pl.program_id(0) % (nb // 2) == 0
        pl.when(first)(lambda: _attend_fwd(q_ref, o_ref, lse_ref, TOP, 0, kc_ref[TOP, :], vc_ref[TOP, :]))
        pl.when(jnp.logical_not(first))(lambda: _attend_fwd(
            q_ref, o_ref, lse_ref, TOP, 0, jnp.concatenate([kp_ref[...], kc_ref[TOP, :]], axis=0),
            jnp.concatenate([vp_ref[...], vc_ref[TOP, :]], axis=0)))
        _attend_fwd(q_ref, o_ref, lse_ref, BOTTOM, 0, kc_ref[...], vc_ref[...])

    if nb == 1:
        body, in_specs, args = body_streams, [ANY, two, two, two], (token, q, k, v)
    else:
        body, in_specs, args = body_blocks, [ANY, two, before, two, before, two], (token, q, k, k, v, v)
    return _pallas_call(
        body, name=f"attn_fwd_d{d}", grid=(d * nb // 2,), out_shape=[_sds((ln, dw), F32)] * 2,
        in_specs=in_specs, out_specs=[two, two], compiler_params=_params(1, 32),
    )(*args)


def _memkv_fwd(mem, g, w):
    n_layers = g.shape[0]
    rows = D_MODEL // N_DEV

    def body(mem_ref, g_ref, w_ref, *kv_refs):
        mb = mem_ref[...]
        r = lax.rsqrt(jnp.mean(mb * mb, axis=-1, keepdims=True) + EPS)
        mn = ((mb * r) * g_ref[...]).astype(BF16)
        kv = _dot(mn, w_ref[...].reshape(D_MODEL, 2 * MEM_WIDTH)).astype(BF16)
        for layer, kv_ref in enumerate(kv_refs):
            @pl.when(pl.program_id(0) == layer)
            def _(kv_ref=kv_ref):
                kv_ref[...] = kv

    return _pallas_call(
        body, name="memkv_fwd", grid=(n_layers,),
        out_shape=[_sds((N_MEM, 2 * MEM_WIDTH), BF16)] * n_layers,
        in_specs=[_whole(mem.shape), pl.BlockSpec((None, 1, D_MODEL), lambda l: (l, 0, 0)),
                  pl.BlockSpec((N_DEV, rows, 2 * MEM_WIDTH), lambda l: (0, l, 0))],
        out_specs=[_whole((N_MEM, 2 * MEM_WIDTH))] * n_layers,
        compiler_params=_params(1, 32),
    )(mem, g.reshape(n_layers, 1, D_MODEL), w)


def _mix_groups(os_, ls_):
    mx = jnp.maximum(jnp.maximum(ls_[0], ls_[1]), ls_[2])
    es = [jnp.exp(t - mx) for t in ls_]
    inv = 1.0 / (es[0] + es[1] + es[2])
    ws = [e * inv for e in es]
    mix = ws[0] * os_[0] + ws[1] * os_[1] + ws[2] * os_[2]
    return ws, mix


MEM_PAIRS = [slice(p * LANES, (p + 1) * LANES) for p in range(MEM_WIDTH // LANES)]


def _mem_probs(qhs, k2s):
    scs = [_dot_nt(qh, k2) * SCALE for qh, k2 in zip(qhs, k2s)]
    es = [jnp.exp(sc - jnp.max(sc, axis=-1, keepdims=True)) for sc in scs]
    return [e * (1.0 / jnp.sum(e, axis=-1, keepdims=True)) for e in es]


def _mem_attn_into(qm, kv_ref, mo_ref):
    low = _low_head_lanes()
    qhs = [h for pr in MEM_PAIRS for h in _split_pair(qm[:, pr], low)]
    k2s = [kv_ref[:, pr] for pr in MEM_PAIRS for _ in range(2)]
    ps = [p.astype(BF16) for p in _mem_probs(qhs, k2s)]
    for i, pr in enumerate(MEM_PAIRS):
        v2 = kv_ref[:, MEM_WIDTH + i * LANES:MEM_WIDTH + (i + 1) * LANES]
        mo_ref[:, pr] = jnp.where(low, _dot(ps[2 * i], v2), _dot(ps[2 * i + 1], v2))


def _mem_attn_bwd(qm, kv_ref, dmem, dqm_ref, dkv_ref):
    low = _low_head_lanes()
    dmb = dmem.astype(BF16)
    vps = [slice(MEM_WIDTH + i * LANES, MEM_WIDTH + (i + 1) * LANES) for i in range(len(MEM_PAIRS))]
    qhs = [h for pr in MEM_PAIRS for h in _split_pair(qm[:, pr], low)]
    dhs = [h for pr in MEM_PAIRS for h in _split_pair(dmb[:, pr], low)]
    k2s = [kv_ref[:, pr] for pr in MEM_PAIRS for _ in range(2)]
    v2s = [kv_ref[:, vp] for vp in vps for _ in range(2)]
    ps = _mem_probs(qhs, k2s)
    dps = [_dot_nt(dh, v2) for dh, v2 in zip(dhs, v2s)]
    dss = [(p * (dp - jnp.sum(dp * p, axis=-1, keepdims=True)) * SCALE).astype(BF16) for p, dp in zip(ps, dps)]
    pbs = [p.astype(BF16) for p in ps]
    for i, (pr, vp) in enumerate(zip(MEM_PAIRS, vps)):
        a, b = 2 * i, 2 * i + 1
        dqm_ref[:, pr] = jnp.where(low, _dot(dss[a], k2s[a]), _dot(dss[b], k2s[b])).astype(BF16)
        dkv_ref[:, pr] += _dot_tn(dss[a], qhs[a]) + _dot_tn(dss[b], qhs[b])
        dkv_ref[:, vp] += _dot_tn(pbs[a], dhs[a]) + _dot_tn(pbs[b], dhs[b])


def _post_attn(os_, ls_, qm, kv, z, x, wg_out):
    s, d_model = x.shape
    gw = GROUP_WIDTH
    nb = gw + MEM_WIDTH
    tm = ROW_TILE

    def body(o0, o1, o2, l0, l1, l2, qm_ref, kv_ref, z_ref, x_ref, w_ref, y_ref, yt_ref, mo_ref, h_ref, s0, s1):
        ov, lv = [], []
        for o_ref, l_ref, d in zip((o0, o1, o2), (l0, l1, l2), DILATIONS):
            ov.append(_from_view(s0, o_ref, d))
            lv.append(_from_view(s1, l_ref, d))
        _, mix = _mix_groups(ov, lv)
        _mem_attn_into(qm_ref[...], kv_ref, mo_ref)
        sz, _ = _silu_parts(z_ref[...])
        y_ref[:, :gw] = (mix * sz[:, :gw]).astype(BF16)
        y_ref[:, gw:] = (mo_ref[...] * sz[:, gw:]).astype(BF16)
        y = y_ref[...]
        yt_ref[...] = y.T
        h_ref[...] = x_ref[...] + _dot(y, _joined_columns(w_ref))

    vspecs = [_view_rows(gw, d) for d in DILATIONS]
    return _pallas_call(
        body, name="post_attn", grid=(s // tm,),
        out_shape=[_sds((s, nb), BF16), _sds((nb, s), BF16), _sds((s, MEM_WIDTH), F32), _sds((s, d_model), F32)],
        in_specs=vspecs * 2 + [_rows(MEM_WIDTH), _whole(kv.shape), _rows(nb), _rows(d_model), _whole(wg_out.shape)],
        out_specs=[_rows(nb), pl.BlockSpec((nb, tm), lambda i: (0, i)), _rows(MEM_WIDTH), _rows(d_model)],
        scratch_shapes=[_view_scratch(gw), _view_scratch(gw)],
        compiler_params=_params(1, 40),
    )(*os_, *ls_, qm, kv, z, x, wg_out)


def _inproj_conv(x, g, wg):
    s, d_model = x.shape
    c = CONV_WIDTH
    n = N_DEV * wg.shape[2]
    nz = n - 3 * c - MEM_WIDTH
    tm = ROW_TILE

    def body(x_ref, g_ref, w_ref, hn_ref, bg_ref, cg_ref, u_ref, qm_ref, z_ref, wj):
        _join_once(w_ref, wj)
        xb = x_ref[...]
        r = lax.rsqrt(jnp.mean(xb * xb, axis=-1, keepdims=True) + EPS)
        hn = ((xb * r) * g_ref[...]).astype(BF16)
        hn_ref[...] = hn
        bg_ref[...] = _dot(hn, wj[:, 0:c])
        cg_ref[...] = _dot(hn, wj[:, c:2 * c])
        u_ref[...] = _dot(hn, wj[:, 2 * c:3 * c])
        qm_ref[...] = _dot(hn, wj[:, 3 * c:3 * c + MEM_WIDTH]).astype(BF16)
        z_ref[...] = _dot(hn, wj[:, 3 * c + MEM_WIDTH:])

    return _pallas_call(
        body, name="inproj_conv", grid=(s // tm,),
        out_shape=[_sds((s, d_model), BF16)] + [_sds((s, c), F32)] * 3 + [_sds((s, MEM_WIDTH), BF16), _sds((s, nz), F32)],
        in_specs=[_rows(d_model), _whole((1, d_model)), _resident(wg.shape)],
        out_specs=[_rows(d_model)] + [_rows(c)] * 3 + [_rows(MEM_WIDTH), _rows(nz)],
        scratch_shapes=[pltpu.VMEM((d_model, n), BF16)],
        compiler_params=_params(1, 60),
    )(x, g, wg)


HALO = 8


def _halo_before(width, tm=ROW_TILE):
    return pl.BlockSpec((HALO, width), lambda i: (jnp.maximum(i * (tm // HALO) - 1, 0), 0))


def _halo_after(width, n_rows, tm=ROW_TILE):
    return pl.BlockSpec((HALO, width), lambda i: (jnp.minimum((i + 1) * (tm // HALO), n_rows // HALO - 1), 0))


def _conv_taps(cg_ref, u_ref, cgh_ref, uh_ref, i):
    a = cg_ref[...] * u_ref[...]
    ah = jnp.where(i > 0, cgh_ref[...] * uh_ref[...], 0.0)
    row = lax.broadcasted_iota(jnp.int32, a.shape, 0)
    a1 = jnp.where(row == 0, ah[HALO - 1:HALO], pltpu.roll(a, 1, 0))
    a2 = jnp.where(row == 0, ah[HALO - 2:HALO - 1], jnp.where(row == 1, ah[HALO - 1:HALO], pltpu.roll(a, 2, 0)))
    return a, a1, a2


def _post_conv_loss(bg, cg, u, qm, kv, z, h1, w_out, cw, gf, tgt):
    s, d = h1.shape
    c = CONV_WIDTH
    nb = c + MEM_WIDTH
    tm = ROW_TILE

    def body(bg_ref, cg_ref, u_ref, cgh_ref, uh_ref, qm_ref, kv_ref, z_ref, h_ref, w_ref, cw_ref, gf_ref, t_ref,
             y_ref, yt_ref, mo_ref, dh_ref, dhb_ref, loss_ref, dgf_ref):
        i = pl.program_id(0)
        a, a1, a2 = _conv_taps(cg_ref, u_ref, cgh_ref, uh_ref, i)
        conv = cw_ref[0:1, :] * a2 + cw_ref[1:2, :] * a1 + cw_ref[2:3, :] * a
        mix = bg_ref[...] * conv
        _mem_attn_into(qm_ref[...], kv_ref, mo_ref)
        sz, _ = _silu_parts(z_ref[...])
        y_ref[:, :c] = (mix * sz[:, :c]).astype(BF16)
        y_ref[:, c:] = (mo_ref[...] * sz[:, c:]).astype(BF16)
        y = y_ref[...]
        yt_ref[...] = y.T
        h2 = h_ref[...] + _dot(y, w_ref[...])
        r = lax.rsqrt(jnp.mean(h2 * h2, axis=-1, keepdims=True) + EPS)
        nh = h2 * r
        gfv = gf_ref[...]
        diff = nh * gfv - t_ref[...]
        dout = diff * (1.0 / d)
        dn = dout * gfv
        dh2 = r * dn - h2 * ((r * r * r) * jnp.mean(dn * h2, axis=-1, keepdims=True))
        dh_ref[...] = dh2
        dhb_ref[...] = dh2.astype(BF16)

        @pl.when(i == 0)
        def _():
            loss_ref[...] = jnp.zeros_like(loss_ref)
            dgf_ref[...] = jnp.zeros_like(dgf_ref)

        loss_ref[...] += 0.5 * jnp.sum(jnp.mean(diff * diff, axis=-1, keepdims=True))
        dgf_ref[...] += jnp.sum(dout * nh, axis=0, keepdims=True)

    return _pallas_call(
        body, name="post_conv_loss", grid=(s // tm,),
        out_shape=[_sds((s, nb), BF16), _sds((nb, s), BF16), _sds((s, MEM_WIDTH), F32), _sds((s, d), F32),
                   _sds((s, d), BF16), _plain((8, LANES), F32), _plain((1, d), F32)],
        in_specs=[_rows(c)] * 3 + [_halo_before(c)] * 2 + [_rows(MEM_WIDTH), _whole(kv.shape), _rows(nb), _rows(d),
                  _whole(w_out.shape), _whole(cw.shape), _whole((1, d)), _rows(d)],
        out_specs=[_rows(nb), pl.BlockSpec((nb, tm), lambda i: (0, i)), _rows(MEM_WIDTH), _rows(d), _rows(d),
                   _whole((8, LANES)), _whole((1, d))],
        compiler_params=_params(1, 48),
    )(bg, cg, u, cg, u, qm, kv, z, h1, w_out, cw, gf, tgt)


def _bwd_post_conv(dhb, w_out, bg, cg, u, z, qm, kv, mo, cw):
    s = dhb.shape[0]
    c = CONV_WIDTH
    nb = c + MEM_WIDTH

    def body(dh_ref, w_ref, bg_ref, cg_ref, u_ref, cgh_ref, uh_ref, z_ref, qm_ref, kv_ref, mo_ref, cw_ref,
             dz_ref, dbg_ref, dc_ref, dqm_ref, dkv_ref):
        i = pl.program_id(0)

        @pl.when(i == 0)
        def _():
            dkv_ref[...] = jnp.zeros_like(dkv_ref)

        dy = _dot_nt(dh_ref[...], w_ref[...])
        sz, dsz = _silu_parts(z_ref[...])
        a, a1, a2 = _conv_taps(cg_ref, u_ref, cgh_ref, uh_ref, i)
        conv = cw_ref[0:1, :] * a2 + cw_ref[1:2, :] * a1 + cw_ref[2:3, :] * a
        bgv = bg_ref[...]
        dz_ref[:, :c] = (dy[:, :c] * (bgv * conv) * dsz[:, :c]).astype(BF16)
        dz_ref[:, c:] = (dy[:, c:] * mo_ref[...] * dsz[:, c:]).astype(BF16)
        dbr = dy * sz
        dmix = dbr[:, :c]
        dbg_ref[...] = (dmix * conv).astype(BF16)
        dc_ref[...] = dmix * bgv
        _mem_attn_bwd(qm_ref[...], kv_ref, dbr[:, c:], dqm_ref, dkv_ref)

    return _pallas_call(
        body, name="bwd_post_conv", grid=(s // ROW_TILE,),
        out_shape=[_sds((s, nb), BF16), _sds((s, c), BF16), _sds((s, c), F32), _sds((s, MEM_WIDTH), BF16),
                   _sds(kv.shape, F32)],
        in_specs=[_rows(D_MODEL), _whole(w_out.shape)] + [_rows(c)] * 3 + [_halo_before(c)] * 2
                 + [_rows(nb), _rows(MEM_WIDTH), _whole(kv.shape), _rows(MEM_WIDTH), _whole(cw.shape)],
        out_specs=[_rows(nb), _rows(c), _rows(c), _rows(MEM_WIDTH), _whole(kv.shape)],
        compiler_params=_params(1, 48),
    )(dhb, w_out, bg, cg, u, cg, u, z, qm, kv, mo, cw)


def _bwd_conv(dconv, cg, u, cw):
    s, c = dconv.shape
    tm = ROW_TILE
    last = s // tm - 1

    def body(dc_ref, dcn_ref, cg_ref, u_ref, cgh_ref, uh_ref, cw_ref, dcg_ref, du_ref, dcw_ref):
        i = pl.program_id(0)

        @pl.when(i == 0)
        def _():
            dcw_ref[...] = jnp.zeros_like(dcw_ref)

        dc = dc_ref[...]
        dcn = jnp.where(i < last, dcn_ref[...], 0.0)
        row = lax.broadcasted_iota(jnp.int32, dc.shape, 0)
        d1 = jnp.where(row == tm - 1, dcn[0:1], pltpu.roll(dc, tm - 1, 0))
        d2 = jnp.where(row == tm - 1, dcn[1:2], jnp.where(row == tm - 2, dcn[0:1], pltpu.roll(dc, tm - 2, 0)))
        da = cw_ref[2:3, :] * dc + cw_ref[1:2, :] * d1 + cw_ref[0:1, :] * d2
        a, a1, a2 = _conv_taps(cg_ref, u_ref, cgh_ref, uh_ref, i)
        dcg_ref[...] = (da * u_ref[...]).astype(BF16)
        du_ref[...] = (da * cg_ref[...]).astype(BF16)
        dcw_ref[0:1, :] += jnp.sum(dc * a2, axis=0, keepdims=True)
        dcw_ref[1:2, :] += jnp.sum(dc * a1, axis=0, keepdims=True)
        dcw_ref[2:3, :] += jnp.sum(dc * a, axis=0, keepdims=True)

    return _pallas_call(
        body, name="bwd_conv", grid=(s // tm,),
        out_shape=[_sds((s, c), BF16), _sds((s, c), BF16), _plain((8, c), F32)],
        in_specs=[_rows(c), _halo_after(c, s), _rows(c), _rows(c), _halo_before(c), _halo_before(c), _whole(cw.shape)],
        out_specs=[_rows(c), _rows(c), _whole((8, c))], compiler_params=_params(1, 40),
    )(dconv, dconv, cg, u, cg, u, cw)


def _assemble(p_refs, pieces, widths, dp, scr):
    off = 0
    for p_ref, (_, d), wd in zip(p_refs, pieces, widths):
        if d == 1:
            dp[:, off:off + wd] = p_ref[...]
        else:
            dp[:, off:off + wd] = _from_view(scr, p_ref, d).astype(BF16)
        off += wd


def _dgrad_norm(pieces, wg, h, g, dres, token, onward, name):
    s, d_model = h.shape
    n = N_DEV * wg.shape[2]
    tm = ROW_TILE
    widths = [p.shape[1] // d for p, d in pieces]
    assert sum(widths) == n
    n_p = len(pieces)

    def body(_, *refs):
        p_refs = refs[:n_p]
        w_ref, h_ref, g_ref, dr_ref, dh_ref, dg_ref = refs[n_p:n_p + 6]
        dp, scr, wj = refs[-3:]
        _join_once(w_ref, wj)

        @pl.when(pl.program_id(0) == 0)
        def _():
            dg_ref[...] = jnp.zeros_like(dg_ref)

        _assemble(p_refs, pieces, widths, dp, scr)
        dhn = _dot_nt(dp[...], wj[...])
        hb = h_ref[...]
        r = lax.rsqrt(jnp.mean(hb * hb, axis=-1, keepdims=True) + EPS)
        dg_ref[...] += jnp.sum(dhn * (hb * r), axis=0, keepdims=True)
        dn = dhn * g_ref[...]
        dh = dr_ref[...] + r * dn - hb * ((r * r * r) * jnp.mean(dn * hb, axis=-1, keepdims=True))
        dh_ref[...] = dh
        if onward:
            dhb_ref, dpt_ref = refs[n_p + 6:n_p + 8]
            dhb_ref[...] = dh.astype(BF16)
            dpt_ref[...] = dp[...].T

    p_specs = [_view_rows(wd, d) for (_, d), wd in zip(pieces, widths)]
    out_shape = [_plain((s, d_model), F32), _plain((1, d_model), F32)]
    out_specs = [_rows(d_model), _whole((1, d_model))]
    if onward:
        out_shape += [_sds((s, d_model), BF16), _sds((n, s), BF16)]
        out_specs += [_rows(d_model), pl.BlockSpec((n, tm), lambda i: (0, i))]
    return _pallas_call(
        body, name=name, grid=(s // tm,), out_shape=out_shape,
        in_specs=[ANY] + p_specs + [_resident(wg.shape), _rows(d_model), _whole((1, d_model)), _rows(d_model)],
        out_specs=out_specs,
        scratch_shapes=[pltpu.VMEM((tm, n), BF16), _view_scratch(GROUP_WIDTH), pltpu.VMEM((d_model, n), BF16)],
        compiler_params=_params(1, 60),
    )(token, *[p for p, _ in pieces], wg, h, g, dres)


def _assemble_dproj_t(pieces, n, name):
    tm = ROW_TILE
    widths = [p.shape[1] // d for p, d in pieces]
    assert sum(widths) == n
    s = pieces[0][0].shape[0] * pieces[0][1]
    n_p = len(pieces)

    def body(*refs):
        p_refs, (dpt_ref, dp, scr) = refs[:n_p], refs[n_p:]
        _assemble(p_refs, pieces, widths, dp, scr)
        dpt_ref[...] = dp[...].T

    return _pallas_call(
        body, name=name, grid=(s // tm,), out_shape=_sds((n, s), BF16),
        in_specs=[_view_rows(wd, d) for (_, d), wd in zip(pieces, widths)],
        out_specs=pl.BlockSpec((n, tm), lambda i: (0, i)),
        scratch_shapes=[pltpu.VMEM((tm, n), BF16), _view_scratch(GROUP_WIDTH)],
        compiler_params=_params(1, 40),
    )(*[p for p, _ in pieces])


def _wgrad_shards_t(dp_t, h, c, name):
    n, s = dp_t.shape
    d_model = h.shape[1]
    per_step = 2

    def body(a_ref, b_ref, o_ref):
        o_ref[...] = _dot(a_ref[...], b_ref[...]).astype(BF16).reshape(per_step, c, d_model)

    return _pallas_call(
        body, name=name, grid=(N_DEV // per_step,), out_shape=_sds((N_DEV, c, d_model), BF16),
        in_specs=[pl.BlockSpec((per_step * c, s), lambda j: (j, 0)), _resident(h.shape)],
        out_specs=pl.BlockSpec((per_step, c, d_model), lambda j: (j, 0, 0)), compiler_params=_params(1, 40),
    )(dp_t, h)


def _wgrad_cols(a_t, b, c, name):
    m, s = a_t.shape
    assert c % LANES == 0

    def body(a_ref, b_ref, o_ref):
        wide = _dot(a_ref[...], b_ref[...]).astype(BF16)
        for j in range(WGRAD_SHARDS):
            o_ref[j] = wide[:, j * c:(j + 1) * c]

    return _pallas_call(
        body, name=name, grid=(N_DEV // WGRAD_SHARDS,), out_shape=_sds((N_DEV, m, c), BF16),
        in_specs=[_whole(a_t.shape), pl.BlockSpec((s, WGRAD_SHARDS * c), lambda j: (0, j))],
        out_specs=pl.BlockSpec((WGRAD_SHARDS, m, c), lambda j: (j, 0, 0)), compiler_params=_params(1, 40),
    )(a_t, b)


def _wgrad_rows(a_t, b, name):
    m, s = a_t.shape
    n = b.shape[1]
    mr = m // N_DEV

    def body(a_ref, b_ref, o_ref):
        o_ref[...] = _dot(a_ref[...], b_ref[...]).astype(BF16).reshape(WGRAD_SHARDS, mr, n)

    return _pallas_call(
        body, name=name, grid=(N_DEV // WGRAD_SHARDS,), out_shape=_sds((N_DEV, mr, n), BF16),
        in_specs=[pl.BlockSpec((WGRAD_SHARDS * mr, s), lambda j: (j, 0)), _whole(b.shape)],
        out_specs=pl.BlockSpec((WGRAD_SHARDS, mr, n), lambda j: (j, 0, 0)), compiler_params=_params(1, 40),
    )(a_t, b)


def _memkv_bwd(dkvs, w, mem, g):
    n_layers = g.shape[0]
    rows = D_MODEL // N_DEV

    def body(dkv0_ref, dkv1_ref, w_ref, mem_ref, g_ref, dw_ref, dg_ref):
        mb = mem_ref[...]
        r = lax.rsqrt(jnp.mean(mb * mb, axis=-1, keepdims=True) + EPS)
        nm = mb * r
        mn = (nm * g_ref[...]).astype(BF16)
        dkvb = jnp.where(pl.program_id(0) == 0, dkv0_ref[...], dkv1_ref[...]).astype(BF16)
        dw_ref[...] = _dot_tn(mn, dkvb).astype(BF16).reshape(N_DEV, rows, 2 * MEM_WIDTH)
        dmn = _dot_nt(dkvb, w_ref[...].reshape(D_MODEL, 2 * MEM_WIDTH))
        dg_ref[...] = jnp.sum(dmn * nm, axis=0, keepdims=True)

    lay = lambda *shape: pl.BlockSpec((None,) + shape, lambda l: (l, 0, 0))
    major = pl.BlockSpec((N_DEV, rows, 2 * MEM_WIDTH), lambda l: (0, l, 0))
    return _pallas_call(
        body, name="memkv_bwd", grid=(n_layers,),
        out_shape=[_sds((N_DEV, n_layers * rows, 2 * MEM_WIDTH), BF16), _plain((n_layers, 1, D_MODEL), F32)],
        in_specs=[_whole(dkvs[0].shape), _whole(dkvs[1].shape), major, _whole(mem.shape), lay(1, D_MODEL)],
        out_specs=[major, lay(1, D_MODEL)],
        compiler_params=_params(1, 32),
    )(*dkvs, w, mem, g.reshape(n_layers, 1, D_MODEL))


def _bwd_post_attn(dhb, wg_out, z, os_, ls_, qm, kv, mo, head_ones, token):
    s = dhb.shape[0]
    gw = GROUP_WIDTH
    nb = gw + MEM_WIDTH
    tm = ROW_TILE

    def body(_, dh_ref, w_ref, z_ref, o0, o1, o2, l0, l1, l2, qm_ref, kv_ref, mo_ref, bd_ref,
             dz_ref, do0, do1, do2, dl0, dl1, dl2, dqm_ref, dkv_ref, s0, s1):
        @pl.when(pl.program_id(0) == 0)
        def _():
            dkv_ref[...] = jnp.zeros_like(dkv_ref)

        dy = _dot_nt(dh_ref[...], _joined_columns(w_ref))
        ov, lv = [], []
        for o_ref, l_ref, d in zip((o0, o1, o2), (l0, l1, l2), DILATIONS):
            ov.append(_from_view(s0, o_ref, d))
            lv.append(_from_view(s1, l_ref, d))
        ws, mix = _mix_groups(ov, lv)
        sz, dsz = _silu_parts(z_ref[...])
        dz_ref[:, :gw] = (dy[:, :gw] * mix * dsz[:, :gw]).astype(BF16)
        dz_ref[:, gw:] = (dy[:, gw:] * mo_ref[...] * dsz[:, gw:]).astype(BF16)
        dbr = dy * sz
        dmix = dbr[:, :gw]
        t = dmix * mix
        th = t.astype(BF16)
        tl = (t - th.astype(F32)).astype(BF16)
        rs = _dot(th, bd_ref[...]) + _dot(tl, bd_ref[...])
        for wg_, do_ref, dl_ref, d in zip(ws, (do0, do1, do2), (dl0, dl1, dl2), DILATIONS):
            _to_view(s0, wg_ * dmix, do_ref, d)
            _to_view(s1, wg_ * rs, dl_ref, d)
        _mem_attn_bwd(qm_ref[...], kv_ref, dbr[:, gw:], dqm_ref, dkv_ref)

    vspecs = [_view_rows(gw, d) for d in DILATIONS]
    return _pallas_call(
        body, name="bwd_post_attn", grid=(s // tm,),
        out_shape=[_sds((s, nb), BF16)] + [_sds((s // d, d * gw), BF16) for d in DILATIONS]
                  + [_sds((s // d, d * gw), F32) for d in DILATIONS] + [_sds((s, MEM_WIDTH), BF16), _sds(kv.shape, F32)],
        in_specs=[ANY, _rows(D_MODEL), _whole(wg_out.shape), _rows(nb)] + vspecs * 2
                 + [_rows(MEM_WIDTH), _whole(kv.shape), _rows(MEM_WIDTH), _whole(head_ones.shape)],
        out_specs=[_rows(nb)] + vspecs * 2 + [_rows(MEM_WIDTH), _whole(kv.shape)],
        scratch_shapes=[_view_scratch(gw), _view_scratch(gw)],
        compiler_params=_params(1, 48),
    )(token, dhb, wg_out, z, *os_, *ls_, qm, kv, mo, head_ones)


def _attn_bwd(q, k, v, lse, do, dl, tabs, d, token):
    ln, dw = q.shape
    w = dw // d
    nb = ln // BLOCK
    reps = w // LANES
    two, before = _pair_specs(d, nb, w)
    two_t, _ = _pair_specs(d, nb, LANES)

    def attend(q_ref, l_ref, do_ref, dl_ref, dqs, acck, accv, rows, col0, kk, vv, acc_rows):
        valid = _band_mask(kk.shape[0])
        low = _low_head_lanes()
        pairs, qcols = _head_tiles(w, col0)
        cols = [slice(col0 + h * HEAD_DIM, col0 + h * HEAD_DIM + 1) for h in range(HEADS_PER_GROUP)]
        qhs = [h for qc in qcols for h in _split_pair(q_ref[rows, qc], low)]
        dobs = [h for qc in qcols for h in _split_pair(do_ref[rows, qc], low)]
        k2s = [kk[:, pr] for pr in pairs for _ in range(2)]
        v2s = [vv[:, pr] for pr in pairs for _ in range(2)]
        scs = [jnp.where(valid, _dot_nt(qh, k2), NEG) for qh, k2 in zip(qhs, k2s)]
        dps = [_dot_nt(dob, v2) for dob, v2 in zip(dobs, v2s)]
        ps = [jnp.exp(sc - l_ref[rows, col]) for sc, col in zip(scs, cols)]
        dss = [(p * (dp - dl_ref[rows, col])).astype(BF16) for p, dp, col in zip(ps, dps, cols)]
        pbs = [p.astype(BF16) for p in ps]
        for i, qc in enumerate(qcols):
            a, b = 2 * i, 2 * i + 1
            dqs[rows, qc] = jnp.where(low, _dot(dss[a], k2s[a]), _dot(dss[b], k2s[b])) * SCALE
            acck[acc_rows, qc] += _dot_tn(dss[a], qhs[a]) + _dot_tn(dss[b], qhs[b])
            accv[acc_rows, qc] += _dot_tn(pbs[a], dobs[a]) + _dot_tn(pbs[b], dobs[b])

    def body_streams(_, q_ref, kc_ref, vc_ref, l_ref, do_ref, dl_ref, c_ref, sa_ref, sb_ref,
                     dq_ref, dk_ref, dv_ref, acck, accv, dqs):
        acck[...] = jnp.zeros_like(acck)
        accv[...] = jnp.zeros_like(accv)
        for sb in range(2):
            cols = slice(sb * w, (sb + 1) * w)
            attend(q_ref, l_ref, do_ref, dl_ref, dqs, acck, accv, TOP, sb * w, kc_ref[:, cols], vc_ref[:, cols], TOP)
        tabs2 = [jnp.concatenate([jnp.tile(r[:, sb * LANES:(sb + 1) * LANES], (1, reps)) for sb in range(2)], axis=1)
                 for r in (c_ref, sa_ref, sb_ref)]
        dq_ref[...] = _rope_bwd(dqs[...], *tabs2).astype(BF16)
        dk_ref[...] = _rope_bwd(acck[...], *tabs2).astype(BF16)
        dv_ref[...] = accv[...].astype(BF16)

    def body_blocks(_, q_ref, kp_ref, kc_ref, vp_ref, vc_ref, l_ref, do_ref, dl_ref, cq, saq, sbq, ck, sak, sbk,
                    dq_ref, dk_ref, dv_ref, acck, accv, dqs):
        i = pl.program_id(0) % (nb // 2)

        @pl.when(i == 0)
        def _():
            acck[...] = jnp.zeros_like(acck)
            accv[...] = jnp.zeros_like(accv)

        refs = (q_ref, l_ref, do_ref, dl_ref, dqs, acck, accv)
        pl.when(i == 0)(lambda: attend(*refs, TOP, 0, kc_ref[TOP, :], vc_ref[TOP, :], TOP))
        pl.when(i != 0)(lambda: attend(
            *refs, TOP, 0, jnp.concatenate([kp_ref[...], kc_ref[TOP, :]], axis=0),
            jnp.concatenate([vp_ref[...], vc_ref[TOP, :]], axis=0),
            pl.ds(pl.multiple_of((2 * i - 1) * BLOCK, BLOCK), 2 * BLOCK)))
        attend(*refs, BOTTOM, 0, kc_ref[...], vc_ref[...], pl.ds(pl.multiple_of(2 * i * BLOCK, BLOCK), 2 * BLOCK))
        tq = [jnp.tile(r[...], (1, reps)) for r in (cq, saq, sbq)]
        dq_ref[...] = _rope_bwd(dqs[...], *tq).astype(BF16)

        @pl.when(i == nb // 2 - 1)
        def _():
            for r0 in range(0, nb * BLOCK, 2 * BLOCK):
                rows = slice(r0, r0 + 2 * BLOCK)
                tk = [jnp.tile(r[rows, :], (1, reps)) for r in (ck, sak, sbk)]
                dk_ref[rows, :] = _rope_bwd(acck[rows, :], *tk).astype(BF16)
                dv_ref[rows, :] = accv[rows, :].astype(BF16)

    if nb == 1:
        body = body_streams
        in_specs = [ANY] + [two] * 6 + [two_t] * 3
        args = (token, q, k, v, lse, do, dl, *tabs)
        out_specs = [two, two, two]
        acc_shape = (BLOCK, 2 * w)
    else:
        body = body_blocks
        stream = pl.BlockSpec((nb * BLOCK, w), lambda n: (0, n // (nb // 2)))
        stream_t = pl.BlockSpec((nb * BLOCK, LANES), lambda n: (0, n // (nb // 2)))
        in_specs = [ANY, two, before, two, before, two, two, two, two] + [two_t] * 3 + [stream_t] * 3
        args = (token, q, k, k, v, v, lse, do, dl, *tabs, *tabs)
        out_specs = [two, stream, stream]
        acc_shape = (nb * BLOCK, w)
    return _pallas_call(
        body, name=f"attn_bwd_d{d}", grid=(d * nb // 2,), out_shape=[_sds((ln, dw), BF16)] * 3,
        in_specs=in_specs, out_specs=out_specs,
        scratch_shapes=[pltpu.VMEM(acc_shape, F32), pltpu.VMEM(acc_shape, F32), pltpu.VMEM(two.block_shape, F32)],
        compiler_params=_params(1, 48),
    )(*args)


def _position():
    return lax.axis_index("x"), lax.axis_index("y"), lax.axis_index("c")


def _all_gather_small(xs, afters, name):
    n_in = 1 + len(afters)

    def body(*refs):
        x_ref, out_ref = refs[0], refs[n_in]
        send_sems, recv_sems, local_sem = refs[n_in + 1:]
        x, y, c = _position()
        my_rows = out_ref.at[4 * x + 2 * y + c]
        mine = pltpu.make_async_copy(x_ref, my_rows, local_sem)
        mine.start()
        copies = [pltpu.make_async_remote_copy(
            src_ref=x_ref, dst_ref=my_rows, send_sem=send_sems.at[k], recv_sem=recv_sems.at[k],
            device_id=(1 - x if k & 4 else x, 1 - y if k & 2 else y, 1 - c if k & 1 else c), device_id_type=MESH)
            for k in range(1, N_DEV)]
        for cp in copies:
            cp.start()
        for cp in copies:
            cp.wait_recv()
        for cp in copies:
            cp.wait_send()
        mine.wait()

    return _pallas_call(
        body, name=name, out_shape=_sds((N_DEV,) + xs.shape, xs.dtype),
        in_specs=[ANY] * n_in, out_specs=ANY,
        scratch_shapes=[pltpu.SemaphoreType.DMA((N_DEV,)), pltpu.SemaphoreType.DMA((N_DEV,)), pltpu.SemaphoreType.DMA],
    )(xs, *afters)


GATHER_CHUNKS = 4


def _all_gather_relay(xs, later, rope, name):
    n = len(later)
    n_io = 3 + n

    def body(*refs):
        x_ref, f32_refs, angle_refs = refs[0], refs[1:1 + n], refs[1 + n:n_io]
        out_ref, land_refs, table_refs = refs[n_io], refs[n_io + 1:n_io + 1 + n], refs[n_io + 1 + n:2 * n_io + 1]
        scratch = refs[2 * n_io + 1:]
        send_sems, recv_sems, local_sem, prep_sems, rope_sems = scratch[:5]
        staged, rounded = scratch[5:5 + n], scratch[5 + n:5 + 2 * n]
        angles, tables = scratch[5 + 2 * n:7 + 2 * n], scratch[7 + 2 * n:]
        x, y, c = _position()
        me, sibling = (x, y, c), (x, y, 1 - c)
        xn, yn, diag = (1 - x, y, c), (x, 1 - y, c), (1 - x, 1 - y, c)
        src_nb = (x + c * (1 - 2 * x), y + (1 - c) * (1 - 2 * y), c)
        dst_nb = (x + (1 - c) * (1 - 2 * x), y + c * (1 - 2 * y), c)

        def rows(dev, i):
            return out_ref.at[4 * dev[0] + 2 * dev[1] + dev[2], pl.ds(i * step, step)]

        def copy(k, i, block, to, own=False):
            return pltpu.make_async_remote_copy(
                src_ref=x_ref.at[pl.ds(i * step, step)] if own else rows(block, i), dst_ref=rows(block, i),
                send_sem=send_sems.at[k, i], recv_sem=recv_sems.at[k, i], device_id=to, device_id_type=MESH)

        mine = pltpu.make_async_copy(x_ref, out_ref.at[4 * x + 2 * y + c], local_sem)
        mine.start()
        sent = []
        for i in chunks:
            sent += [copy(1, i, me, xn, own=True), copy(2, i, me, yn, own=True), copy(0, i, me, sibling, own=True)]
        for cp in sent:
            cp.start()
        fetches = [pltpu.make_async_copy(f32_refs[a], staged[a], prep_sems.at[0, a]) for a in range(n)]
        fetches += [pltpu.make_async_copy(angle_refs[a], angles[a], rope_sems.at[a]) for a in range(2)]
        for cp in fetches:
            cp.start()
        placed = []
        for a in range(n):
            fetches[a].wait()
            rounded[a][...] = staged[a][...].astype(BF16)
            placed.append(pltpu.make_async_copy(rounded[a], land_refs[a].at[4 * x + 2 * y + c], prep_sems.at[1, a]))
            placed[a].start()
        for cp in fetches[n:]:
            cp.wait()
        _rope_expand(*angles, *tables)
        for a in range(3):
            placed.append(pltpu.make_async_copy(tables[a], table_refs[a], rope_sems.at[2 + a]))
            placed[-1].start()
        for i in chunks:
            copy(1, i, xn, me).wait_recv()
            copy(2, i, yn, me).wait_recv()
            onward = [copy(3, i, src_nb, dst_nb), copy(4, i, xn, sibling), copy(5, i, yn, sibling)]
            for cp in onward:
                cp.start()
            sent += onward
        for i in chunks:
            copy(3, i, diag, me).wait_recv()
            last = copy(6, i, diag, sibling)
            last.start()
            sent.append(last)
        for i in chunks:
            copy(0, i, sibling, me).wait_recv()
            for k, blk in ((4, (1 - x, y, 1 - c)), (5, (x, 1 - y, 1 - c)), (6, (1 - x, 1 - y, 1 - c))):
                copy(k, i, blk, me).wait_recv()
        for cp in sent:
            cp.wait_send()
        for cp in [mine] + placed:
            cp.wait()

    step = xs.shape[0] // GATHER_CHUNKS
    chunks = range(GATHER_CHUNKS)
    table = (rope[0].shape[1], LANES)
    res = _pallas_call(
        body, name=name,
        out_shape=[_sds((N_DEV,) + xs.shape, xs.dtype)] + [_sds((N_DEV,) + t.shape, BF16) for t in later]
                  + [_sds(table, F32)] * 3,
        in_specs=[ANY] * n_io, out_specs=[ANY] * (n_io + 1),
        scratch_shapes=[pltpu.SemaphoreType.DMA((7, GATHER_CHUNKS)), pltpu.SemaphoreType.DMA((7, GATHER_CHUNKS)),
                        pltpu.SemaphoreType.DMA, pltpu.SemaphoreType.DMA((2, n)), pltpu.SemaphoreType.DMA((5,))]
                       + [pltpu.VMEM(t.shape, F32) for t in later] + [pltpu.VMEM(t.shape, BF16) for t in later]
                       + [pltpu.VMEM(t.shape, F32) for t in rope] + [pltpu.VMEM(table, F32)] * 3,
        compiler_params=pltpu.CompilerParams(vmem_limit_bytes=32 << 20),
    )(xs, *later, *rope)
    return res[0], res[1:1 + n], res[1 + n:]


HBM_SPEC = pl.BlockSpec(memory_space=pltpu.HBM)
SEM_SPEC = pl.BlockSpec(memory_space=pltpu.SEMAPHORE)
EFFECT = pltpu.SideEffectType.DATAFLOW_SIDE_EFFECTING
def _plan_gather_own(src_refs, land_refs):
    x, y, c = _position()
    me = 4 * x + 2 * y + c
    peers = [(x, y, 1 - c), (1 - x, y, c), (x, 1 - y, c), (1 - x, 1 - y, c)]
    return [(land_refs[a].at[me], land_refs[a].at[me], (a, k), peer)
            for k, peer in enumerate(peers) for a in range(len(land_refs))]


def _plan_gather_pass(src_refs, land_refs):
    x, y, c = _position()
    chips = [(1 - x, y), (x, 1 - y), (1 - x, 1 - y)]
    return [(land_refs[a].at[4 * px + 2 * py + c], land_refs[a].at[4 * px + 2 * py + c], (a, j), (x, y, 1 - c))
            for j, (px, py) in enumerate(chips) for a in range(len(land_refs))]


def _plan_to_sibling(src_refs, land_refs):
    x, y, c = _position()
    return [(src_refs[a].at[2 * k + (1 - c)], land_refs[a].at[k], (a, k), (x, y, 1 - c))
            for k in range(4) for a in range(len(src_refs))]


def _plan_to_chips(src_refs, land_refs):
    x, y, c = _position()
    chips = [(1 - x, y), (x, 1 - y), (1 - x, 1 - y)]
    return [(src_refs[a].at[2 * px + py], land_refs[a].at[j], (a, j), (px, py, c))
            for j, (px, py) in enumerate(chips) for a in range(len(src_refs))]


def _split_start(srcs, lands, plan, n_sem, after, name):
    n_s, n_a = len(srcs), len(lands)
    n_b = n_s + n_a

    def body(*refs):
        src_refs, land_refs = refs[:n_s], refs[n_s:n_b]
        send_sems, recv_sems, token = refs[n_b + 1], refs[n_b + 2], refs[-1]
        for src, dst, (a, k), dev in plan(src_refs, land_refs):
            i = a * n_sem + k
            pltpu.make_async_remote_copy(src_ref=src, dst_ref=dst, send_sem=send_sems.at[i], recv_sem=recv_sems.at[i],
                                         device_id=dev, device_id_type=MESH).start()
        token[...] = jnp.zeros_like(token)

    bufs = list(srcs) + list(lands)
    res = pl.pallas_call(
        body, name=name,
        out_shape=(pltpu.SemaphoreType.DMA((n_a * n_sem,)), pltpu.SemaphoreType.DMA((n_a * n_sem,)),
                   *[pltpu.HBM(t.shape, t.dtype) for t in bufs], _plain((8, LANES), F32)),
        in_specs=[HBM_SPEC] * n_b + [ANY],
        out_specs=(SEM_SPEC, SEM_SPEC, *[HBM_SPEC] * n_b, pl.BlockSpec(memory_space=pltpu.VMEM)),
        input_output_aliases={i: 2 + i for i in range(n_b)},
        compiler_params=pltpu.CompilerParams(has_side_effects=EFFECT),
    )(*[pltpu.with_memory_space_constraint(t, pltpu.HBM) for t in bufs], after)
    return (res[0], res[1], res[2:2 + n_s], res[2 + n_s:2 + n_b]), res[-1]


def _split_wait(started, plan, after, name, first=0, n_sem=None):
    send_sems, recv_sems, srcs, lands = started
    n_s, n_a = len(srcs), len(lands)
    n_b = n_s + n_a
    n_sem = n_sem or send_sems.shape[0] // n_a

    def body(*refs):
        src_refs, land_refs = refs[:n_s], refs[n_s:n_b]
        s_sems, r_sems = refs[n_b], refs[n_b + 1]
        for src, dst, (a, k), dev in plan(src_refs, land_refs):
            i = (first + a) * n_sem + k
            cp = pltpu.make_async_remote_copy(src_ref=src, dst_ref=dst, send_sem=s_sems.at[i], recv_sem=r_sems.at[i],
                                              device_id=dev, device_id_type=MESH)
            cp.wait_send()
            cp.wait_recv()

    bufs = list(srcs) + list(lands)
    res = pl.pallas_call(
        body, name=name, out_shape=tuple(pltpu.HBM(t.shape, t.dtype) for t in bufs),
        in_specs=[HBM_SPEC] * n_b + [SEM_SPEC, SEM_SPEC, ANY],
        out_specs=tuple([HBM_SPEC] * n_b),
        input_output_aliases={i: i for i in range(n_b)},
        compiler_params=pltpu.CompilerParams(has_side_effects=EFFECT),
    )(*bufs, send_sems, recv_sems, after)
    return res[:n_s], res[n_s:]


SUBLANES = 8


def _row_tile(r):
    return max(t for t in range(SUBLANES, ROW_TILE + 1, SUBLANES) if r % t == 0)


def _rs_add_sibling(gps, recvs, ck_arr, name):
    n = len(gps)
    block = lambda k, ck: (k + ck[1] + 1) % 4

    def body(ck_ref, *refs):
        for g_ref, r_ref, pf_ref, pb_ref in zip(refs[:n], refs[n:2 * n], refs[2 * n::2], refs[2 * n + 1::2]):
            sm = g_ref[...].astype(F32) + r_ref[...].astype(F32)
            pf_ref[...] = sm
            pb_ref[...] = sm.astype(BF16)

    mine = lambda t: pl.BlockSpec((None,) + t.shape[1:], lambda k, ck: (2 * block(k, ck) + ck[0], 0, 0))
    one = lambda t: pl.BlockSpec((None,) + t.shape[1:], lambda k, ck: (block(k, ck), 0, 0))
    res = _pallas_call(
        body, name=name,
        grid_spec=pltpu.PrefetchScalarGridSpec(
            num_scalar_prefetch=1, grid=(4,),
            in_specs=[mine(t) for t in gps] + [one(t) for t in recvs],
            out_specs=[s for t in recvs for s in (pl.BlockSpec(t.shape[1:], lambda k, ck: (0, 0)), one(t))]),
        out_shape=[s for t in recvs for s in (_sds(t.shape[1:], F32), _sds(t.shape, BF16))],
        compiler_params=_params(1, 48),
    )(ck_arr, *gps, *recvs)
    return list(zip(res[0::2], res[1::2]))


def _adam_update(w, gv, m, v):
    nm = ADAM_B1 * m + (1.0 - ADAM_B1) * gv
    nv = ADAM_B2 * v + (1.0 - ADAM_B2) * (gv * gv)
    m_hat = nm / (1.0 - ADAM_B1 ** ADAM_STEP)
    v_hat = nv / (1.0 - ADAM_B2 ** ADAM_STEP)
    return -ADAM_LR * (m_hat / (jnp.sqrt(v_hat) + ADAM_EPS) + ADAM_WD * w), nm, nv


def _rs_finish_adamw(pf, recv, w, m, v, after, name):
    r, l = pf.shape
    tr = _row_tile(r)

    def body(_, p_ref, r_ref, w_ref, m_ref, v_ref, g_ref, d_ref, nm_ref, nv_ref):
        gv = ((p_ref[...] + r_ref[0].astype(F32)) + r_ref[1].astype(F32)) + r_ref[2].astype(F32)
        g_ref[...] = gv
        d_ref[...], nm_ref[...], nv_ref[...] = _adam_update(w_ref[...], gv, m_ref[...], v_ref[...])

    spec = pl.BlockSpec((tr, l), lambda i: (i, 0))
    return _pallas_call(
        body, name=name, grid=(r // tr,),
        in_specs=[ANY, spec, pl.BlockSpec((3, tr, l), lambda i: (0, i, 0)), spec, spec, spec], out_specs=[spec] * 4,
        out_shape=[_plain((r, l), F32)] * 4, compiler_params=_params(1, 32),
    )(after, pf, recv, w, m, v)


SMALL_ROWS = dict(norm_g=(0, 2), mem_norm_g=(2, 4), final_g=(4, 5), conv_w=(5, 8))
LOSS_ROWS = (8, 16)


def _sum_adamw_small(g, ck_arr, states):
    names = list(SMALL_ROWS)
    n_dev, n_rows, _ = g.shape

    def body(ck_ref, g_ref, gc_ref, *refs):
        ins, loss_ref, outs = refs[:3 * len(names)], refs[3 * len(names)], refs[3 * len(names) + 1:]

        def total(ref, lo, hi):
            acc = ref[0, lo:hi, :]
            for j in range(1, n_dev):
                acc = acc + ref[j, lo:hi, :]
            return acc

        loss_ref[...] = total(gc_ref, *LOSS_ROWS)
        for i, n in enumerate(names):
            gv = total(gc_ref if n == "conv_w" else g_ref, *SMALL_ROWS[n])
            w_ref, m_ref, v_ref = ins[3 * i:3 * i + 3]
            g_out, d_out, nm_out, nv_out = outs[4 * i:4 * i + 4]
            g_out[...] = gv
            d_out[...], nm_out[...], nv_out[...] = _adam_update(w_ref[...], gv, m_ref[...], v_ref[...])

    flat = [t for n in names for t in states[n]]
    mine = pl.BlockSpec((n_dev, n_rows, LANES), lambda i, ck: (0, 0, 2 * ck[1] + ck[0]))
    res = _pallas_call(
        body, name="sum_adamw_small",
        grid_spec=pltpu.PrefetchScalarGridSpec(
            num_scalar_prefetch=1, grid=(1,),
            in_specs=[_whole(g.shape), mine] + [_whole(t.shape) for t in flat],
            out_specs=[_whole((SUBLANES, LANES))] + [_whole(states[n][0].shape) for n in names for _ in range(4)]),
        out_shape=[_plain((SUBLANES, LANES), F32)] + [_plain(states[n][0].shape, F32) for n in names for _ in range(4)],
        compiler_params=_params(1, 32),
    )(ck_arr, g, g, *flat)
    return res[0], {n: tuple(res[1 + 4 * i:5 + 4 * i]) for i, n in enumerate(names)}


def _finish(name, pf, recv, w, m, v, after):
    if name in ("attn_w_in", "conv_w_in"):
        res = _rs_finish_adamw(pf, recv, w.T, m.T, v.T, after, "rs_finish_adamw_" + name)
        return tuple(t.T for t in res)
    return _rs_finish_adamw(pf, recv, w, m, v, after, "rs_finish_adamw_" + name)


def kernel(x, mem, positions, norm_g, mem_norm_g, w_mem_kv, attn_w_in, attn_w_out, conv_w_in, conv_w, conv_w_out, final_g, loss_target, m_norm_g, m_mem_norm_g, m_w_mem_kv, m_attn_w_in, m_attn_w_out, m_conv_w_in, m_conv_w, m_conv_w_out, m_final_g, v_norm_g, v_mem_norm_g, v_w_mem_kv, v_attn_w_in, v_attn_w_out, v_conv_w_in, v_conv_w, v_conv_w_out, v_final_g):
    px, py, pc = _position()
    me = 4 * px + 2 * py + pc
    ck_arr = jnp.stack([pc, 2 * px + py]).astype(jnp.int32)
    x, mem, pos, tgt = x[0], mem[0], positions[0], loss_target[0]

    wg_in0, (land_out0, land_kv, land_in1, land_out1), tabs = _all_gather_relay(
        attn_w_in[0].astype(BF16),
        [attn_w_out[0], w_mem_kv.reshape(-1, w_mem_kv.shape[2]), conv_w_in[0], conv_w_out[0]],
        _rope_angles(pos), "gather_w_in0")

    def gather_pass(weights, after, name, first):
        _, lands = _split_wait(weights, _plan_gather_own, after, name + "_wait", first, 4)
        return _split_start([], lands, _plan_gather_pass, 3, after, name + "_pass_start")

    taps = jnp.pad(conv_w[0], ((0, 5), (0, 0)))
    land_taps = lax.dynamic_update_slice(lax.empty((N_DEV,) + taps.shape, taps.dtype), taps[None], (me, 0, 0))
    lands = [land_out0, land_kv, land_taps, land_in1, land_out1]
    (send_sems, recv_sems, _, lands), token = _split_start([], lands, _plan_gather_own, 4, wg_in0, "gather_late_start")
    rest0, conv_ws = (send_sems, recv_sems, [], lands[:3]), (send_sems, recv_sems, [], lands[3:])

    g0, g1 = norm_g[0:1], norm_g[1:2]

    hn0, qs, ks, vs, tabs_v, qm0, z0 = _inproj_attn(x, g0, wg_in0, tabs, token)
    os_, ls_ = [], []
    for j, d in enumerate(DILATIONS):
        if j == 2:
            rest0, token = gather_pass(rest0, ls_[1], "gather_rest", 0)
        o, l = _attn_fwd(qs[j], ks[j], vs[j], d, token)
        os_.append(o)
        ls_.append(l)

    conv_ws, token = gather_pass(conv_ws, ls_[2], "gather_conv", 3)
    _, (wg_out0, wg_kv, cw_all) = _split_wait(rest0, _plan_gather_pass, token, "gather_rest_pass_wait")
    cw = cw_all[:, 0:3].transpose(1, 0, 2).reshape(3, -1)
    kv = _memkv_fwd(mem, mem_norm_g, wg_kv)
    y0, y0_t, mo0, h1 = _post_attn(os_, ls_, qm0, kv[0], z0, x, wg_out0)
    _, (wg_in1, wg_out1) = _split_wait(conv_ws, _plan_gather_pass, h1, "gather_conv_pass_wait")
    w_out1 = wg_out1.reshape(-1, wg_out1.shape[2])

    hn1, bg, cg, u, qm1, z1 = _inproj_conv(h1, g1, wg_in1)
    y1, y1_t, mo1, dh2, dh2b, loss_acc, d_final_g = _post_conv_loss(
        bg, cg, u, qm1, kv[1], z1, h1, w_out1, cw, final_g.reshape(1, -1), tgt)

    d_w_out1 = _wgrad_rows(y1_t, dh2b, "wgrad_out1")
    dz1, dbg, dconv, dqm1, dkv1 = _bwd_post_conv(dh2b, w_out1, bg, cg, u, z1, qm1, kv[1], mo1, cw)
    dcg, du, dcw = _bwd_conv(dconv, cg, u, cw)
    dh1, dg1, dh1b, dproj1_t = _dgrad_norm([(dbg, 1), (dcg, 1), (du, 1), (dqm1, 1), (dz1, 1)], wg_in1, h1, g1, dh2,
                                           dh2, True, "dgrad_norm_conv")
    d_w_in1 = _wgrad_shards_t(dproj1_t, hn1, wg_in1.shape[2], "wgrad_in1")

    d_w_out0 = _wgrad_cols(y0_t, dh1b, wg_out0.shape[2], "wgrad_out0")

    gw = GROUP_WIDTH
    ones = (jnp.arange(gw)[:, None] // HEAD_DIM == jnp.arange(gw)[None, :] // HEAD_DIM).astype(BF16)
    res = _bwd_post_attn(dh1b, wg_out0, z0, os_, ls_, qm0, kv[0], mo0, ones, dg1)
    dz0, dos, dls, dqm0, dkv0 = res[0], res[1:4], res[4:7], res[7], res[8]
    d_w_kv, d_mem_g = _memkv_bwd([dkv0, dkv1], wg_kv, mem, mem_norm_g)

    names1 = ["conv_w_in", "conv_w_out", "attn_w_out", "w_mem_kv"]
    grads1 = [d_w_in1, d_w_out1, d_w_out0, d_w_kv]
    started, token = _split_start(grads1, [lax.empty((4,) + g.shape[1:], g.dtype) for g in grads1],
                                  _plan_to_sibling, 4, d_mem_g, "rs1_sibling_start")

    dqs, dks, dvs = [], [], []
    for j, d in enumerate(DILATIONS):
        if j == 1:
            grads1, from_sibling = _split_wait(started, _plan_to_sibling, dqs[0][0], "rs1_sibling_wait")
            parts1 = _rs_add_sibling(grads1, from_sibling, ck_arr, "rs1_add_sibling")
            pbs1 = [pb for _, pb in parts1]
            started, token = _split_start(pbs1, [lax.empty((3,) + p.shape[1:], p.dtype) for p in pbs1],
                                          _plan_to_chips, 3, dg1, "rs1_chips_start")
        dq, dk, dv = _attn_bwd(qs[j], ks[j], vs[j], ls_[j], dos[j], dls[j], tabs_v[j], d, token)
        dqs.append((dq, d))
        dks.append((dk, d))
        dvs.append((dv, d))
    pieces0 = dqs + dks + dvs + [(dqm0, 1), (dz0, 1)]
    dproj0_t = _assemble_dproj_t(pieces0, N_DEV * wg_in0.shape[2], "assemble_dproj_attn")
    d_w_in0 = _wgrad_shards_t(dproj0_t, hn0, wg_in0.shape[2], "wgrad_in0")

    names0 = ["attn_w_in"]
    grads0 = [d_w_in0]
    started0, token0 = _split_start(grads0, [lax.empty((4,) + g.shape[1:], g.dtype) for g in grads0],
                                    _plan_to_sibling, 4, dg1, "rs0_sibling_start")
    _, from_chips1 = _split_wait(started, _plan_to_chips, token0, "rs1_chips_wait")
    shard = dict(attn_w_in=(attn_w_in[0], m_attn_w_in[0], v_attn_w_in[0]),
                 attn_w_out=(attn_w_out[0], m_attn_w_out[0], v_attn_w_out[0]),
                 conv_w_in=(conv_w_in[0], m_conv_w_in[0], v_conv_w_in[0]),
                 conv_w_out=(conv_w_out[0], m_conv_w_out[0], v_conv_w_out[0]),
                 w_mem_kv=tuple(t.reshape(-1, t.shape[2]) for t in (w_mem_kv, m_w_mem_kv, v_w_mem_kv)))
    finish1 = {n: (pf, r) for n, (pf, _), r in zip(names1, parts1, from_chips1)}
    big = {"conv_w_in": _finish("conv_w_in", *finish1["conv_w_in"], *shard["conv_w_in"], token0)}

    grads0, from_sibling = _split_wait(started0, _plan_to_sibling, big["conv_w_in"][1].T, "rs0_sibling_wait")
    parts0 = _rs_add_sibling(grads0, from_sibling, ck_arr, "rs0_add_sibling")
    pbs0 = [pb for _, pb in parts0]
    started0, token0 = _split_start(pbs0, [lax.empty((3,) + p.shape[1:], p.dtype) for p in pbs0],
                                    _plan_to_chips, 3, dg1, "rs0_chips_start")
    dx, dg0 = _dgrad_norm(pieces0, wg_in0, x, g0, dh1, token0, False, "dgrad_norm_attn")
    for n in names1[1:]:
        big[n] = _finish(n, *finish1[n], *shard[n], token0)

    small_part = jnp.concatenate([dg0, dg1, d_mem_g.reshape(2, -1), d_final_g, dcw[0:3]], axis=0)
    small_part = jnp.concatenate([small_part, jnp.broadcast_to(loss_acc[0, 0], small_part.shape)], axis=0)
    small_w = dict(norm_g=(norm_g, m_norm_g, v_norm_g), mem_norm_g=(mem_norm_g, m_mem_norm_g, v_mem_norm_g),
                   conv_w=(conv_w, m_conv_w, v_conv_w), final_g=(final_g, m_final_g, v_final_g))
    loss_tile, small_res = _sum_adamw_small(
        _all_gather_small(small_part, [big[n][1] for n in names1[1:]], "gather_small_grads"), ck_arr,
        {n: tuple(t.reshape(-1, t.shape[-1]) for t in wmv) for n, wmv in small_w.items()})
    loss = loss_tile[0, 0]
    for n, wmv in small_w.items():
        big[n] = tuple(t.reshape(wmv[0].shape) for t in small_res[n])

    _, from_chips0 = _split_wait(started0, _plan_to_chips, big["final_g"][1], "rs0_chips_wait")
    for n, (pf, _), r in zip(names0, parts0, from_chips0):
        big[n] = _finish(n, pf, r, *shard[n], big["final_g"][1])
    for n in ("attn_w_in", "attn_w_out", "conv_w_in", "conv_w_out"):
        big[n] = tuple(t[None] for t in big[n])
    big["w_mem_kv"] = tuple(t.reshape(w_mem_kv.shape) for t in big["w_mem_kv"])

    order = ["norm_g", "mem_norm_g", "w_mem_kv", "attn_w_in", "attn_w_out", "conv_w_in", "conv_w", "conv_w_out", "final_g"]
    return (loss, dx[None], *[big[n][0] for n in order], *[big[n][1] for n in order],
            *[big[n][2] for n in order], *[big[n][3] for n in order])
```
